```python
import jax, jax.numpy as jnp
from jax import lax
import numpy as np

D_MODEL = 1024
BATCH = 16
SEQ = 2048
DEPTH = 1

D_FF = 2816
N_HEADS = 8
QK_NOPE_DIM = 64
QK_ROPE_DIM = 32
QK_HEAD_DIM = QK_NOPE_DIM + QK_ROPE_DIM
V_HEAD_DIM = 64
Q_LORA_RANK = 384
KV_LORA_RANK = 256
CONV_DIM = 1024
CONV_WIDTH = 3
ROPE_THETA = 10000.0
Q_BLOCK = 128
NORM_EPS = 1e-6
ATTN_OUT_DIM = N_HEADS * V_HEAD_DIM
N_BRANCHES = 2
IN_SIZES = (Q_LORA_RANK, KV_LORA_RANK, QK_ROPE_DIM, CONV_DIM, CONV_DIM, CONV_DIM, D_MODEL, D_MODEL)
IN_DIM = sum(IN_SIZES)
IN_SPLITS = tuple(int(s) for s in np.cumsum(IN_SIZES)[:-1])

kernel_name = "hybrid_mla_shortconv_macaron_block"


def rmsnorm(x, gain):
    x32 = x.astype(jnp.float32)
    y = x32 * lax.rsqrt(jnp.mean(x32 * x32, axis=-1, keepdims=True) + NORM_EPS)
    return (y * gain.astype(jnp.float32)).astype(x.dtype)


def swiglu(h, w_gate, w_up, w_down):
    return (jax.nn.silu(h @ w_gate) * (h @ w_up)) @ w_down


def rope(t, positions):
    half = QK_ROPE_DIM // 2
    inv_freq = 1.0 / (ROPE_THETA ** (jnp.arange(half, dtype=jnp.float32) / half))
    ang = positions.astype(jnp.float32)[..., None] * inv_freq
    ang = ang.reshape(ang.shape[:2] + (1,) * (t.ndim - 3) + (half,))
    cos, sin = jnp.cos(ang).astype(t.dtype), jnp.sin(ang).astype(t.dtype)
    t1, t2 = t[..., :half], t[..., half:]
    return jnp.concatenate([t1 * cos - t2 * sin, t1 * sin + t2 * cos], axis=-1)


def causal_block_attention(q, k, v):
    b, s, h, dq = q.shape
    nb = s // Q_BLOCK
    scale = QK_HEAD_DIM ** -0.5
    q_blocks = q.reshape(b, nb, Q_BLOCK, h, dq).transpose(1, 0, 2, 3, 4)
    key_pos = jnp.arange(s)

    def one_block(args):
        qb, blk = args
        scores = jnp.einsum('bqhd,bkhd->bhqk', qb, k).astype(jnp.float32) * scale
        q_pos = blk * Q_BLOCK + jnp.arange(Q_BLOCK)
        mask = key_pos[None, :] <= q_pos[:, None]
        scores = jnp.where(mask[None, None], scores, -1e30)
        p = jax.nn.softmax(scores, axis=-1).astype(v.dtype)
        return jnp.einsum('bhqk,bkhd->bqhd', p, v)

    out = lax.map(one_block, (q_blocks, jnp.arange(nb)))
    return out.transpose(1, 0, 2, 3, 4).reshape(b, s, h * V_HEAD_DIM)


def _fwd_setup_inputs(seed: int = 0) -> dict:
    key = jax.random.key(seed)
    ks = jax.random.split(key, 32)

    def w(k, shape, fan_in):
        return jax.random.normal(k, shape, jnp.float32) * fan_in ** -0.5

    def gain(k, n):
        return 1.0 + 0.02 * jax.random.normal(k, (n,), jnp.float32)

    positions = jnp.broadcast_to(jnp.arange(SEQ, dtype=jnp.int32)[None, :], (BATCH, SEQ))
    return {
        "x": jax.random.normal(ks[0], (BATCH, SEQ, D_MODEL), jnp.float32),
        "positions": positions,
        "ffn1_norm": gain(ks[1], D_MODEL),
        "ffn1_w_gate": w(ks[2], (D_MODEL, D_FF), D_MODEL),
        "ffn1_w_up": w(ks[3], (D_MODEL, D_FF), D_MODEL),
        "ffn1_w_down": w(ks[4], (D_FF, D_MODEL), D_FF),
        "mix_norm": gain(ks[5], D_MODEL),
        "w_in": w(ks[6], (D_MODEL, IN_DIM), D_MODEL),
        "gate_bias": 0.01 * jax.random.normal(ks[7], (N_BRANCHES * D_MODEL,), jnp.float32),
        "q_a_norm": gain(ks[8], Q_LORA_RANK),
        "w_uq": w(ks[9], (Q_LORA_RANK, N_HEADS * QK_HEAD_DIM), Q_LORA_RANK),
        "kv_a_norm": gain(ks[10], KV_LORA_RANK),
        "w_uk": w(ks[11], (KV_LORA_RANK, N_HEADS * QK_NOPE_DIM), KV_LORA_RANK),
        "w_uv": w(ks[12], (KV_LORA_RANK, N_HEADS * V_HEAD_DIM), KV_LORA_RANK),
        "q_head_norm": gain(ks[13], QK_HEAD_DIM),
        "k_head_norm": gain(ks[14], QK_HEAD_DIM),
        "w_proj_attn": w(ks[15], (ATTN_OUT_DIM, D_MODEL), ATTN_OUT_DIM),
        "conv_w": w(ks[16], (CONV_WIDTH, CONV_DIM), CONV_WIDTH),
        "w_proj_conv": w(ks[17], (CONV_DIM, D_MODEL), CONV_DIM),
        "w_out": w(ks[18], (D_MODEL, D_MODEL), D_MODEL),
        "ffn2_norm": gain(ks[19], D_MODEL),
        "ffn2_w_gate": w(ks[20], (D_MODEL, D_FF), D_MODEL),
        "ffn2_w_up": w(ks[21], (D_MODEL, D_FF), D_MODEL),
        "ffn2_w_down": w(ks[22], (D_FF, D_MODEL), D_FF),
    }


def _fwd_reference(x, positions, ffn1_norm, ffn1_w_gate, ffn1_w_up, ffn1_w_down,
              mix_norm, w_in, gate_bias, q_a_norm, w_uq, kv_a_norm, w_uk, w_uv,
              q_head_norm, k_head_norm, w_proj_attn, conv_w, w_proj_conv, w_out,
              ffn2_norm, ffn2_w_gate, ffn2_w_up, ffn2_w_down):
    b, s, _ = x.shape
    for _layer in range(DEPTH):
        x = x + 0.5 * swiglu(rmsnorm(x, ffn1_norm), ffn1_w_gate, ffn1_w_up, ffn1_w_down)

        h = rmsnorm(x, mix_norm)
        proj = h @ w_in
        q_lat, kv_lat, k_rope_raw, xc, gB, gC, gate_logits = jnp.split(proj, IN_SPLITS, axis=-1)[:7] + []  if False else jnp.split(proj, IN_SPLITS, axis=-1)[:7]
        gate_logits = proj[..., IN_SPLITS[-1]:] if False else jnp.concatenate([gate_logits, proj[..., IN_SPLITS[-1]:]], axis=-1)

        q = (rmsnorm(q_lat, q_a_norm) @ w_uq).reshape(b, s, N_HEADS, QK_HEAD_DIM)
        c_kv = rmsnorm(kv_lat, kv_a_norm)
        k_nope = (c_kv @ w_uk).reshape(b, s, N_HEADS, QK_NOPE_DIM)
        v = (c_kv @ w_uv).reshape(b, s, N_HEADS, V_HEAD_DIM)
        k_rope = jnp.broadcast_to(k_rope_raw[:, :, None, :], (b, s, N_HEADS, QK_ROPE_DIM))
        k = jnp.concatenate([k_nope, k_rope], axis=-1)
        q = rmsnorm(q, q_head_norm)
        k = rmsnorm(k, k_head_norm)
        q = jnp.concatenate([q[..., :QK_NOPE_DIM], rope(q[..., QK_NOPE_DIM:], positions)], axis=-1)
        k = jnp.concatenate([k[..., :QK_NOPE_DIM], rope(k[..., QK_NOPE_DIM:], positions)], axis=-1)
        y_a = causal_block_attention(q, k, v) @ w_proj_attn

        u = gC * xc
        up = jnp.pad(u, ((0, 0), (CONV_WIDTH - 1, 0), (0, 0)))
        z = conv_w[0] * up[:, :s] + conv_w[1] * up[:, 1:s + 1] + conv_w[2] * up[:, 2:s + 2]
        y_b = (gB * z) @ w_proj_conv

        gates = jax.nn.sigmoid(gate_logits + gate_bias)
        merged = gates[..., :D_MODEL] * y_a + gates[..., D_MODEL:] * y_b
        x = x + merged @ w_out

        x = x + 0.5 * swiglu(rmsnorm(x, ffn2_norm), ffn2_w_gate, ffn2_w_up, ffn2_w_down)
    return x


import jax as _jax
import jax.numpy as _jnp

TWIN_FORMAT = 'train_step'
FWD_PARAMS = ['x', 'positions', 'ffn1_norm', 'ffn1_w_gate', 'ffn1_w_up', 'ffn1_w_down', 'mix_norm', 'w_in', 'gate_bias', 'q_a_norm', 'w_uq', 'kv_a_norm', 'w_uk', 'w_uv', 'q_head_norm', 'k_head_norm', 'w_proj_attn', 'conv_w', 'w_proj_conv', 'w_out', 'ffn2_norm', 'ffn2_w_gate', 'ffn2_w_up', 'ffn2_w_down']
TWIN_WEIGHTS = ['ffn1_norm', 'ffn1_w_gate', 'ffn1_w_up', 'ffn1_w_down', 'mix_norm', 'w_in', 'gate_bias', 'q_a_norm', 'w_uq', 'kv_a_norm', 'w_uk', 'w_uv', 'q_head_norm', 'k_head_norm', 'w_proj_attn', 'conv_w', 'w_proj_conv', 'w_out', 'ffn2_norm', 'ffn2_w_gate', 'ffn2_w_up', 'ffn2_w_down']
TWIN_DIFF_INPUT = 'x'
TWIN_INPUTS = ['x', 'positions', 'ffn1_norm', 'ffn1_w_gate', 'ffn1_w_up', 'ffn1_w_down', 'mix_norm', 'w_in', 'gate_bias', 'q_a_norm', 'w_uq', 'kv_a_norm', 'w_uk', 'w_uv', 'q_head_norm', 'k_head_norm', 'w_proj_attn', 'conv_w', 'w_proj_conv', 'w_out', 'ffn2_norm', 'ffn2_w_gate', 'ffn2_w_up', 'ffn2_w_down', 'loss_target', 'm_ffn1_norm', 'm_ffn1_w_gate', 'm_ffn1_w_up', 'm_ffn1_w_down', 'm_mix_norm', 'm_w_in', 'm_gate_bias', 'm_q_a_norm', 'm_w_uq', 'm_kv_a_norm', 'm_w_uk', 'm_w_uv', 'm_q_head_norm', 'm_k_head_norm', 'm_w_proj_attn', 'm_conv_w', 'm_w_proj_conv', 'm_w_out', 'm_ffn2_norm', 'm_ffn2_w_gate', 'm_ffn2_w_up', 'm_ffn2_w_down', 'v_ffn1_norm', 'v_ffn1_w_gate', 'v_ffn1_w_up', 'v_ffn1_w_down', 'v_mix_norm', 'v_w_in', 'v_gate_bias', 'v_q_a_norm', 'v_w_uq', 'v_kv_a_norm', 'v_w_uk', 'v_w_uv', 'v_q_head_norm', 'v_k_head_norm', 'v_w_proj_attn', 'v_conv_w', 'v_w_proj_conv', 'v_w_out', 'v_ffn2_norm', 'v_ffn2_w_gate', 'v_ffn2_w_up', 'v_ffn2_w_down']
TWIN_OUTPUTS = ['loss', 'grad_x', 'grad_ffn1_norm', 'grad_ffn1_w_gate', 'grad_ffn1_w_up', 'grad_ffn1_w_down', 'grad_mix_norm', 'grad_w_in', 'grad_gate_bias', 'grad_q_a_norm', 'grad_w_uq', 'grad_kv_a_norm', 'grad_w_uk', 'grad_w_uv', 'grad_q_head_norm', 'grad_k_head_norm', 'grad_w_proj_attn', 'grad_conv_w', 'grad_w_proj_conv', 'grad_w_out', 'grad_ffn2_norm', 'grad_ffn2_w_gate', 'grad_ffn2_w_up', 'grad_ffn2_w_down', 'delta_ffn1_norm', 'delta_ffn1_w_gate', 'delta_ffn1_w_up', 'delta_ffn1_w_down', 'delta_mix_norm', 'delta_w_in', 'delta_gate_bias', 'delta_q_a_norm', 'delta_w_uq', 'delta_kv_a_norm', 'delta_w_uk', 'delta_w_uv', 'delta_q_head_norm', 'delta_k_head_norm', 'delta_w_proj_attn', 'delta_conv_w', 'delta_w_proj_conv', 'delta_w_out', 'delta_ffn2_norm', 'delta_ffn2_w_gate', 'delta_ffn2_w_up', 'delta_ffn2_w_down', 'new_m_ffn1_norm', 'new_m_ffn1_w_gate', 'new_m_ffn1_w_up', 'new_m_ffn1_w_down', 'new_m_mix_norm', 'new_m_w_in', 'new_m_gate_bias', 'new_m_q_a_norm', 'new_m_w_uq', 'new_m_kv_a_norm', 'new_m_w_uk', 'new_m_w_uv', 'new_m_q_head_norm', 'new_m_k_head_norm', 'new_m_w_proj_attn', 'new_m_conv_w', 'new_m_w_proj_conv', 'new_m_w_out', 'new_m_ffn2_norm', 'new_m_ffn2_w_gate', 'new_m_ffn2_w_up', 'new_m_ffn2_w_down', 'new_v_ffn1_norm', 'new_v_ffn1_w_gate', 'new_v_ffn1_w_up', 'new_v_ffn1_w_down', 'new_v_mix_norm', 'new_v_w_in', 'new_v_gate_bias', 'new_v_q_a_norm', 'new_v_w_uq', 'new_v_kv_a_norm', 'new_v_w_uk', 'new_v_w_uv', 'new_v_q_head_norm', 'new_v_k_head_norm', 'new_v_w_proj_attn', 'new_v_conv_w', 'new_v_w_proj_conv', 'new_v_w_out', 'new_v_ffn2_norm', 'new_v_ffn2_w_gate', 'new_v_ffn2_w_up', 'new_v_ffn2_w_down']
TWIN_LEAF_KINDS = {'loss': 'loss', 'grad_x': 'grad_x', 'grad_ffn1_norm': 'grad_w', 'grad_ffn1_w_gate': 'grad_w', 'grad_ffn1_w_up': 'grad_w', 'grad_ffn1_w_down': 'grad_w', 'grad_mix_norm': 'grad_w', 'grad_w_in': 'grad_w', 'grad_gate_bias': 'grad_w', 'grad_q_a_norm': 'grad_w', 'grad_w_uq': 'grad_w', 'grad_kv_a_norm': 'grad_w', 'grad_w_uk': 'grad_w', 'grad_w_uv': 'grad_w', 'grad_q_head_norm': 'grad_w', 'grad_k_head_norm': 'grad_w', 'grad_w_proj_attn': 'grad_w', 'grad_conv_w': 'grad_w', 'grad_w_proj_conv': 'grad_w', 'grad_w_out': 'grad_w', 'grad_ffn2_norm': 'grad_w', 'grad_ffn2_w_gate': 'grad_w', 'grad_ffn2_w_up': 'grad_w', 'grad_ffn2_w_down': 'grad_w', 'delta_ffn1_norm': 'delta_w', 'delta_ffn1_w_gate': 'delta_w', 'delta_ffn1_w_up': 'delta_w', 'delta_ffn1_w_down': 'delta_w', 'delta_mix_norm': 'delta_w', 'delta_w_in': 'delta_w', 'delta_gate_bias': 'delta_w', 'delta_q_a_norm': 'delta_w', 'delta_w_uq': 'delta_w', 'delta_kv_a_norm': 'delta_w', 'delta_w_uk': 'delta_w', 'delta_w_uv': 'delta_w', 'delta_q_head_norm': 'delta_w', 'delta_k_head_norm': 'delta_w', 'delta_w_proj_attn': 'delta_w', 'delta_conv_w': 'delta_w', 'delta_w_proj_conv': 'delta_w', 'delta_w_out': 'delta_w', 'delta_ffn2_norm': 'delta_w', 'delta_ffn2_w_gate': 'delta_w', 'delta_ffn2_w_up': 'delta_w', 'delta_ffn2_w_down': 'delta_w', 'new_m_ffn1_norm': 'new_m', 'new_m_ffn1_w_gate': 'new_m', 'new_m_ffn1_w_up': 'new_m', 'new_m_ffn1_w_down': 'new_m', 'new_m_mix_norm': 'new_m', 'new_m_w_in': 'new_m', 'new_m_gate_bias': 'new_m', 'new_m_q_a_norm': 'new_m', 'new_m_w_uq': 'new_m', 'new_m_kv_a_norm': 'new_m', 'new_m_w_uk': 'new_m', 'new_m_w_uv': 'new_m', 'new_m_q_head_norm': 'new_m', 'new_m_k_head_norm': 'new_m', 'new_m_w_proj_attn': 'new_m', 'new_m_conv_w': 'new_m', 'new_m_w_proj_conv': 'new_m', 'new_m_w_out': 'new_m', 'new_m_ffn2_norm': 'new_m', 'new_m_ffn2_w_gate': 'new_m', 'new_m_ffn2_w_up': 'new_m', 'new_m_ffn2_w_down': 'new_m', 'new_v_ffn1_norm': 'new_v', 'new_v_ffn1_w_gate': 'new_v', 'new_v_ffn1_w_up': 'new_v', 'new_v_ffn1_w_down': 'new_v', 'new_v_mix_norm': 'new_v', 'new_v_w_in': 'new_v', 'new_v_gate_bias': 'new_v', 'new_v_q_a_norm': 'new_v', 'new_v_w_uq': 'new_v', 'new_v_kv_a_norm': 'new_v', 'new_v_w_uk': 'new_v', 'new_v_w_uv': 'new_v', 'new_v_q_head_norm': 'new_v', 'new_v_k_head_norm': 'new_v', 'new_v_w_proj_attn': 'new_v', 'new_v_conv_w': 'new_v', 'new_v_w_proj_conv': 'new_v', 'new_v_w_out': 'new_v', 'new_v_ffn2_norm': 'new_v', 'new_v_ffn2_w_gate': 'new_v', 'new_v_ffn2_w_up': 'new_v', 'new_v_ffn2_w_down': 'new_v'}


def _forward(args):
    return _fwd_reference(*[args[k] for k in FWD_PARAMS])


def _output_shape():
    out = _jax.eval_shape(lambda: _forward(_fwd_setup_inputs(0)))
    return out.shape, out.dtype

N_MICROBATCH = 1
ADAM_LR = 0.001
ADAM_B1 = 0.9
ADAM_B2 = 0.999
ADAM_EPS = 1e-08
ADAM_WD = 0.01
ADAM_STEP = 10
PER_EXAMPLE_BATCH_AXIS = {'x': 0, 'positions': 0, 'loss_target': 0}
SHARED_INPUTS = []
_WEIGHT_DTYPES = {'ffn1_norm': _jnp.float32, 'ffn1_w_gate': _jnp.float32, 'ffn1_w_up': _jnp.float32, 'ffn1_w_down': _jnp.float32, 'mix_norm': _jnp.float32, 'w_in': _jnp.float32, 'gate_bias': _jnp.float32, 'q_a_norm': _jnp.float32, 'w_uq': _jnp.float32, 'kv_a_norm': _jnp.float32, 'w_uk': _jnp.float32, 'w_uv': _jnp.float32, 'q_head_norm': _jnp.float32, 'k_head_norm': _jnp.float32, 'w_proj_attn': _jnp.float32, 'conv_w': _jnp.float32, 'w_proj_conv': _jnp.float32, 'w_out': _jnp.float32, 'ffn2_norm': _jnp.float32, 'ffn2_w_gate': _jnp.float32, 'ffn2_w_up': _jnp.float32, 'ffn2_w_down': _jnp.float32}
MOMENT_SCALE = {'ffn1_norm': 6.049037e+00, 'ffn1_w_gate': 9.717092e-02, 'ffn1_w_up': 1.009899e-01, 'ffn1_w_down': 1.667782e-01, 'mix_norm': 2.802040e+01, 'w_in': 2.310169e-01, 'gate_bias': 2.250124e+00, 'q_a_norm': 5.473481e-02, 'w_uq': 3.949289e-02, 'kv_a_norm': 3.356603e-01, 'w_uk': 4.007073e-02, 'w_uv': 6.284345e-02, 'q_head_norm': 4.947185e-01, 'k_head_norm': 4.942565e-01, 'w_proj_attn': 4.087066e-02, 'conv_w': 5.210036e+00, 'w_proj_conv': 3.403244e-01, 'w_out': 2.988438e-01, 'ffn2_norm': 6.177766e+00, 'ffn2_w_gate': 5.852691e-02, 'ffn2_w_up': 7.007204e-02, 'ffn2_w_down': 1.135141e-01}


def _to_microbatches(a, axis):
    t = _jnp.moveaxis(a, axis, 0)
    t = t.reshape((N_MICROBATCH, t.shape[0] // N_MICROBATCH) + t.shape[1:])
    return _jnp.moveaxis(t, 1, axis + 1)


def setup_inputs(seed: int = 0) -> dict:
    inp = _fwd_setup_inputs(seed)
    key = _jax.random.fold_in(_jax.random.key(seed), 7919)
    shape, _ = _output_shape()
    out = dict(inp)
    out["loss_target"] = _jax.random.normal(_jax.random.fold_in(key, 0), shape, _jnp.float32)
    for i, name in enumerate(TWIN_WEIGHTS):
        w = inp[name].astype(_jnp.float32)
        if MOMENT_SCALE is None:
            s = _jnp.sqrt(_jnp.mean(_jnp.square(w)) + 1e-30)
        else:
            s = MOMENT_SCALE[name]
        km, kv = _jax.random.split(_jax.random.fold_in(key, i + 1))
        out[name] = w
        out["m_" + name] = s * _jax.random.normal(km, w.shape, _jnp.float32)
        out["v_" + name] = (s * s) * _jax.random.uniform(kv, w.shape, _jnp.float32, 0.5, 1.5)
    if N_MICROBATCH > 1:
        for name, axis in PER_EXAMPLE_BATCH_AXIS.items():
            out[name] = _to_microbatches(out[name], axis)
    return {'x': out['x'], 'positions': out['positions'], 'ffn1_norm': out['ffn1_norm'], 'ffn1_w_gate': out['ffn1_w_gate'], 'ffn1_w_up': out['ffn1_w_up'], 'ffn1_w_down': out['ffn1_w_down'], 'mix_norm': out['mix_norm'], 'w_in': out['w_in'], 'gate_bias': out['gate_bias'], 'q_a_norm': out['q_a_norm'], 'w_uq': out['w_uq'], 'kv_a_norm': out['kv_a_norm'], 'w_uk': out['w_uk'], 'w_uv': out['w_uv'], 'q_head_norm': out['q_head_norm'], 'k_head_norm': out['k_head_norm'], 'w_proj_attn': out['w_proj_attn'], 'conv_w': out['conv_w'], 'w_proj_conv': out['w_proj_conv'], 'w_out': out['w_out'], 'ffn2_norm': out['ffn2_norm'], 'ffn2_w_gate': out['ffn2_w_gate'], 'ffn2_w_up': out['ffn2_w_up'], 'ffn2_w_down': out['ffn2_w_down'], 'loss_target': out['loss_target'], 'm_ffn1_norm': out['m_ffn1_norm'], 'm_ffn1_w_gate': out['m_ffn1_w_gate'], 'm_ffn1_w_up': out['m_ffn1_w_up'], 'm_ffn1_w_down': out['m_ffn1_w_down'], 'm_mix_norm': out['m_mix_norm'], 'm_w_in': out['m_w_in'], 'm_gate_bias': out['m_gate_bias'], 'm_q_a_norm': out['m_q_a_norm'], 'm_w_uq': out['m_w_uq'], 'm_kv_a_norm': out['m_kv_a_norm'], 'm_w_uk': out['m_w_uk'], 'm_w_uv': out['m_w_uv'], 'm_q_head_norm': out['m_q_head_norm'], 'm_k_head_norm': out['m_k_head_norm'], 'm_w_proj_attn': out['m_w_proj_attn'], 'm_conv_w': out['m_conv_w'], 'm_w_proj_conv': out['m_w_proj_conv'], 'm_w_out': out['m_w_out'], 'm_ffn2_norm': out['m_ffn2_norm'], 'm_ffn2_w_gate': out['m_ffn2_w_gate'], 'm_ffn2_w_up': out['m_ffn2_w_up'], 'm_ffn2_w_down': out['m_ffn2_w_down'], 'v_ffn1_norm': out['v_ffn1_norm'], 'v_ffn1_w_gate': out['v_ffn1_w_gate'], 'v_ffn1_w_up': out['v_ffn1_w_up'], 'v_ffn1_w_down': out['v_ffn1_w_down'], 'v_mix_norm': out['v_mix_norm'], 'v_w_in': out['v_w_in'], 'v_gate_bias': out['v_gate_bias'], 'v_q_a_norm': out['v_q_a_norm'], 'v_w_uq': out['v_w_uq'], 'v_kv_a_norm': out['v_kv_a_norm'], 'v_w_uk': out['v_w_uk'], 'v_w_uv': out['v_w_uv'], 'v_q_head_norm': out['v_q_head_norm'], 'v_k_head_norm': out['v_k_head_norm'], 'v_w_proj_attn': out['v_w_proj_attn'], 'v_conv_w': out['v_conv_w'], 'v_w_proj_conv': out['v_w_proj_conv'], 'v_w_out': out['v_w_out'], 'v_ffn2_norm': out['v_ffn2_norm'], 'v_ffn2_w_gate': out['v_ffn2_w_gate'], 'v_ffn2_w_up': out['v_ffn2_w_up'], 'v_ffn2_w_down': out['v_ffn2_w_down']}


def _loss(weights, diff, rest, loss_target):
    with _jax.named_scope("forward"):
        args = {**rest, TWIN_DIFF_INPUT: diff, **{k: w.astype(_WEIGHT_DTYPES[k]) for k, w in weights.items()}}
        y = _forward(args)
    with _jax.named_scope("loss_head"):
        err = _jnp.square(y.astype(_jnp.float32) - loss_target)
        return 0.5 * _jnp.sum(_jnp.mean(err, axis=-1)) if err.ndim else 0.5 * err


def _adamw(w, g, m, v):
    m = ADAM_B1 * m + (1.0 - ADAM_B1) * g
    v = ADAM_B2 * v + (1.0 - ADAM_B2) * _jnp.square(g)
    m_hat = m / (1.0 - ADAM_B1 ** ADAM_STEP)
    v_hat = v / (1.0 - ADAM_B2 ** ADAM_STEP)
    delta = -ADAM_LR * (m_hat / (_jnp.sqrt(v_hat) + ADAM_EPS) + ADAM_WD * w)
    return delta, m, v


def reference(x, positions, ffn1_norm, ffn1_w_gate, ffn1_w_up, ffn1_w_down, mix_norm, w_in, gate_bias, q_a_norm, w_uq, kv_a_norm, w_uk, w_uv, q_head_norm, k_head_norm, w_proj_attn, conv_w, w_proj_conv, w_out, ffn2_norm, ffn2_w_gate, ffn2_w_up, ffn2_w_down, loss_target, m_ffn1_norm, m_ffn1_w_gate, m_ffn1_w_up, m_ffn1_w_down, m_mix_norm, m_w_in, m_gate_bias, m_q_a_norm, m_w_uq, m_kv_a_norm, m_w_uk, m_w_uv, m_q_head_norm, m_k_head_norm, m_w_proj_attn, m_conv_w, m_w_proj_conv, m_w_out, m_ffn2_norm, m_ffn2_w_gate, m_ffn2_w_up, m_ffn2_w_down, v_ffn1_norm, v_ffn1_w_gate, v_ffn1_w_up, v_ffn1_w_down, v_mix_norm, v_w_in, v_gate_bias, v_q_a_norm, v_w_uq, v_kv_a_norm, v_w_uk, v_w_uv, v_q_head_norm, v_k_head_norm, v_w_proj_attn, v_conv_w, v_w_proj_conv, v_w_out, v_ffn2_norm, v_ffn2_w_gate, v_ffn2_w_up, v_ffn2_w_down):
    given = dict(x=x, positions=positions, ffn1_norm=ffn1_norm, ffn1_w_gate=ffn1_w_gate, ffn1_w_up=ffn1_w_up, ffn1_w_down=ffn1_w_down, mix_norm=mix_norm, w_in=w_in, gate_bias=gate_bias, q_a_norm=q_a_norm, w_uq=w_uq, kv_a_norm=kv_a_norm, w_uk=w_uk, w_uv=w_uv, q_head_norm=q_head_norm, k_head_norm=k_head_norm, w_proj_attn=w_proj_attn, conv_w=conv_w, w_proj_conv=w_proj_conv, w_out=w_out, ffn2_norm=ffn2_norm, ffn2_w_gate=ffn2_w_gate, ffn2_w_up=ffn2_w_up, ffn2_w_down=ffn2_w_down, loss_target=loss_target, m_ffn1_norm=m_ffn1_norm, m_ffn1_w_gate=m_ffn1_w_gate, m_ffn1_w_up=m_ffn1_w_up, m_ffn1_w_down=m_ffn1_w_down, m_mix_norm=m_mix_norm, m_w_in=m_w_in, m_gate_bias=m_gate_bias, m_q_a_norm=m_q_a_norm, m_w_uq=m_w_uq, m_kv_a_norm=m_kv_a_norm, m_w_uk=m_w_uk, m_w_uv=m_w_uv, m_q_head_norm=m_q_head_norm, m_k_head_norm=m_k_head_norm, m_w_proj_attn=m_w_proj_attn, m_conv_w=m_conv_w, m_w_proj_conv=m_w_proj_conv, m_w_out=m_w_out, m_ffn2_norm=m_ffn2_norm, m_ffn2_w_gate=m_ffn2_w_gate, m_ffn2_w_up=m_ffn2_w_up, m_ffn2_w_down=m_ffn2_w_down, v_ffn1_norm=v_ffn1_norm, v_ffn1_w_gate=v_ffn1_w_gate, v_ffn1_w_up=v_ffn1_w_up, v_ffn1_w_down=v_ffn1_w_down, v_mix_norm=v_mix_norm, v_w_in=v_w_in, v_gate_bias=v_gate_bias, v_q_a_norm=v_q_a_norm, v_w_uq=v_w_uq, v_kv_a_norm=v_kv_a_norm, v_w_uk=v_w_uk, v_w_uv=v_w_uv, v_q_head_norm=v_q_head_norm, v_k_head_norm=v_k_head_norm, v_w_proj_attn=v_w_proj_attn, v_conv_w=v_conv_w, v_w_proj_conv=v_w_proj_conv, v_w_out=v_w_out, v_ffn2_norm=v_ffn2_norm, v_ffn2_w_gate=v_ffn2_w_gate, v_ffn2_w_up=v_ffn2_w_up, v_ffn2_w_down=v_ffn2_w_down)
    weights = {n: given[n] for n in TWIN_WEIGHTS}
    shared = {n: given[n] for n in SHARED_INPUTS}
    per_example = {n: given[n] for n in ['x', 'positions']}
    grad_fn = _jax.value_and_grad(_loss, argnums=(0, 1))

    def one_microbatch(ex, loss_target):
        ex = dict(ex)
        diff = ex.pop(TWIN_DIFF_INPUT)
        return grad_fn(weights, diff, {**shared, **ex}, loss_target)

    if N_MICROBATCH == 1:
        loss, (grad_w, grad_x) = one_microbatch(per_example, given["loss_target"])
    else:
        def body(carry, xs):
            loss_sum, grad_sum = carry
            l_k, (gw_k, gx_k) = one_microbatch(xs[0], xs[1])
            with _jax.named_scope("update"):
                return (loss_sum + l_k, _jax.tree.map(_jnp.add, grad_sum, gw_k)), gx_k

        init = (_jnp.zeros((), _jnp.float32), _jax.tree.map(_jnp.zeros_like, weights))
        (loss, grad_w), grad_x = _jax.lax.scan(body, init, (per_example, given["loss_target"]))
    with _jax.named_scope("update"):
        delta_w, new_m, new_v = {}, {}, {}
        for n in TWIN_WEIGHTS:
            delta_w[n], new_m[n], new_v[n] = _adamw(weights[n], grad_w[n], given["m_" + n], given["v_" + n])
    return (loss, grad_x, *[grad_w[n] for n in TWIN_WEIGHTS], *[delta_w[n] for n in TWIN_WEIGHTS],
            *[new_m[n] for n in TWIN_WEIGHTS], *[new_v[n] for n in TWIN_WEIGHTS])
```

```python
import functools

import jax
import jax.numpy as jnp
from jax import lax
from jax.experimental import pallas as pl
from jax.experimental.pallas import tpu as pltpu

F32 = jnp.float32
BF16 = jnp.bfloat16
MESH = pl.DeviceIdType.MESH
ANY = pl.BlockSpec(memory_space=pl.ANY)

N_DEV = 8
D = 1024
DFF = 2816
N_HEADS = 8
HEAD_PAD = 128
QK_DIM = 96
NOPE = 64
ROPE_HALF = 16
Q_LORA = 384
KV_LORA = 256
LAT_PAD = 768
CONV_COLS = 3072
GATE_COLS = 2048
IN_DIM = 5792
IN_SHARD = IN_DIM // N_DEV
IN_SHARD_PAD = 736
FF_SHARD = DFF // N_DEV
ROPE_THETA = 10000.0
NORM_EPS = 1e-6
ATTN_SCALE = QK_DIM ** -0.5
NEG = -1e30

ADAM_LR, ADAM_B1, ADAM_B2, ADAM_EPS, ADAM_WD, ADAM_STEP = 0.001, 0.9, 0.999, 1e-08, 0.01, 10

PACK = (("ffn1_gT", 352), ("ffn1_uT", 352), ("ffn1_d", 352), ("ffn2_gT", 352), ("ffn2_uT", 352), ("ffn2_d", 352),
        ("w_inT", IN_SHARD_PAD), ("w_uq", 48), ("w_uk", 32), ("w_uv", 32), ("w_pa", 64), ("w_pc", 128), ("w_out", 128))
PACK_ROWS = sum(r for _, r in PACK)
PACK_OFF = {}
_o = 0
for _n, _r in PACK:
    PACK_OFF[_n] = (_o, _r)
    _o += _r
PACK_TILE = 656

VMEM_LIMIT = 56 * 1024 * 1024


def _params(*sem):
    return pltpu.CompilerParams(dimension_semantics=sem if sem else None, vmem_limit_bytes=VMEM_LIMIT)


def _dot_nn(a, b):
    return lax.dot_general(a, b, (((1,), (0,)), ((), ())), preferred_element_type=F32)


def _dot_nt(a, b):
    return lax.dot_general(a, b, (((1,), (1,)), ((), ())), preferred_element_type=F32)


def _dot_tn(a, b):
    return lax.dot_general(a, b, (((0,), (0,)), ((), ())), preferred_element_type=F32)


def _sigmoid(x):
    return 1.0 / (1.0 + jnp.exp(-x))


def _rms_stats(x):
    r = lax.rsqrt(jnp.mean(x * x, axis=-1, keepdims=True) + NORM_EPS)
    return x * r, r


def _rms_bwd(dy, xhat, r, g):
    dg = jnp.sum(dy * xhat, axis=0, keepdims=True)
    dxh = dy * g
    dx = r * (dxh - xhat * jnp.mean(dxh * xhat, axis=-1, keepdims=True))
    return dx, dg


def _mm(a, b, *, mode, out_dtype, tm, tn, tk, name, add=None):
    if mode == "nn":
        (m, k), (_, n) = a.shape, b.shape
    elif mode == "nt":
        (m, k), (n, _) = a.shape, b.shape
    else:
        (k, m), (_, n) = a.shape, b.shape
    assert m % tm == 0 and n % tn == 0 and k % tk == 0, (name, m, n, k, tm, tn, tk)
    nk = k // tk
    dot = {"nn": _dot_nn, "nt": _dot_nt, "tn": _dot_tn}[mode]
    a_spec = pl.BlockSpec((tk, tm), lambda i, j, kk: (kk, i)) if mode == "tn" else pl.BlockSpec((tm, tk), lambda i, j, kk: (i, kk))
    b_spec = pl.BlockSpec((tn, tk), lambda i, j, kk: (j, kk)) if mode == "nt" else pl.BlockSpec((tk, tn), lambda i, j, kk: (kk, j))
    o_spec = pl.BlockSpec((tm, tn), lambda i, j, kk: (i, j))
    has_add = add is not None

    def body(*refs):
        if has_add:
            a_ref, b_ref, c_ref, o_ref, acc_ref = refs
        else:
            a_ref, b_ref, o_ref, acc_ref = refs
        kk = pl.program_id(2)

        @pl.when(kk == 0)
        def _():
            acc_ref[...] = c_ref[...] if has_add else jnp.zeros_like(acc_ref)

        acc_ref[...] += dot(a_ref[...], b_ref[...])

        @pl.when(kk == nk - 1)
        def _():
            o_ref[...] = acc_ref[...].astype(out_dtype)

    operands = (a, b, add) if has_add else (a, b)
    in_specs = [a_spec, b_spec] + ([o_spec] if has_add else [])
    return pl.pallas_call(
        body, name=name, grid=(m // tm, n // tn, nk), in_specs=in_specs, out_specs=o_spec,
        out_shape=jax.ShapeDtypeStruct((m, n), out_dtype), scratch_shapes=[pltpu.VMEM((tm, tn), F32)],
        compiler_params=_params("parallel", "parallel", "arbitrary"),
    )(*operands)


def _rms_fwd(x, g, *, tm, name):
    t, d = x.shape

    def body(x_ref, g_ref, h_ref):
        xhat, _ = _rms_stats(x_ref[...])
        h_ref[...] = (xhat * g_ref[...]).astype(BF16)

    return pl.pallas_call(
        body, name=name, grid=(t // tm,),
        in_specs=[pl.BlockSpec((tm, d), lambda i: (i, 0)), pl.BlockSpec((1, d), lambda i: (0, 0))],
        out_specs=pl.BlockSpec((tm, d), lambda i: (i, 0)), out_shape=jax.ShapeDtypeStruct((t, d), BF16),
        compiler_params=_params("parallel"),
    )(x, g)


def _rms_bwd_res(dh, x, g, dres, *, tm, name):
    t, d = x.shape

    def body(dh_ref, x_ref, g_ref, dres_ref, dx_ref, dg_ref):
        xhat, r = _rms_stats(x_ref[...])
        dx, dg = _rms_bwd(dh_ref[...], xhat, r, g_ref[...])
        dx_ref[...] = dres_ref[...] + dx

        @pl.when(pl.program_id(0) == 0)
        def _():
            dg_ref[...] = jnp.zeros_like(dg_ref)

        dg_ref[...] += dg

    row = pl.BlockSpec((tm, d), lambda i: (i, 0))
    vec = pl.BlockSpec((1, d), lambda i: (0, 0))
    return pl.pallas_call(
        body, name=name, grid=(t // tm,), in_specs=[row, row, vec, row], out_specs=[row, vec],
        out_shape=[jax.ShapeDtypeStruct((t, d), F32), jax.ShapeDtypeStruct((1, d), F32)],
        compiler_params=_params("arbitrary"),
    )(dh, x, g, dres)


def _ffn_fwd(x, g, wgT, wuT, wd, *, tm, hc, name):
    t, d = x.shape
    nj = DFF // hc

    def body(x_ref, g_ref, wg_ref, wu_ref, wd_ref, xo_ref, h_ref, a_ref, b_ref, acc_ref):
        j = pl.program_id(1)

        @pl.when(j == 0)
        def _():
            xhat, _ = _rms_stats(x_ref[...])
            h_ref[...] = (xhat * g_ref[...]).astype(BF16)
            acc_ref[...] = jnp.zeros_like(acc_ref)

        h = h_ref[...]
        a = _dot_nt(h, wg_ref[...])
        b = _dot_nt(h, wu_ref[...])
        a_ref[...] = a.astype(BF16)
        b_ref[...] = b.astype(BF16)
        s = (a * _sigmoid(a) * b).astype(BF16)
        acc_ref[...] += _dot_nn(s, wd_ref[...])

        @pl.when(j == nj - 1)
        def _():
            xo_ref[...] = x_ref[...] + 0.5 * acc_ref[...]

    row = pl.BlockSpec((tm, d), lambda i, j: (i, 0))
    vec = pl.BlockSpec((1, d), lambda i, j: (0, 0))
    wsp = pl.BlockSpec((hc, d), lambda i, j: (j, 0))
    hid = pl.BlockSpec((tm, hc), lambda i, j: (i, j))
    return pl.pallas_call(
        body, name=name, grid=(t // tm, nj), in_specs=[row, vec, wsp, wsp, wsp], out_specs=[row, row, hid, hid],
        out_shape=[jax.ShapeDtypeStruct((t, d), F32), jax.ShapeDtypeStruct((t, d), BF16),
                   jax.ShapeDtypeStruct((t, DFF), BF16), jax.ShapeDtypeStruct((t, DFF), BF16)],
        scratch_shapes=[pltpu.VMEM((tm, d), F32)],
        compiler_params=_params("parallel", "arbitrary"),
    )(x, g, wgT, wuT, wd)


def _ffn_bwd(dout, x, g, a, b, wgT, wuT, wd, *, tm, hc, name):
    t, d = x.shape
    nj = DFF // hc

    def body(dout_ref, x_ref, g_ref, a_ref, b_ref, wg_ref, wu_ref, wd_ref,
             dx_ref, dy_ref, da_ref, db_ref, s_ref, dg_ref, acc_ref):
        i, j = pl.program_id(0), pl.program_id(1)

        @pl.when(j == 0)
        def _():
            dy_ref[...] = (0.5 * dout_ref[...]).astype(BF16)
            acc_ref[...] = jnp.zeros_like(acc_ref)

        @pl.when((i == 0) & (j == 0))
        def _():
            dg_ref[...] = jnp.zeros_like(dg_ref)

        ds = _dot_nt(dy_ref[...], wd_ref[...])
        av = a_ref[...].astype(F32)
        bv = b_ref[...].astype(F32)
        sg = _sigmoid(av)
        sl = av * sg
        s_ref[...] = (sl * bv).astype(BF16)
        da = (ds * bv * (sg * (1.0 + av * (1.0 - sg)))).astype(BF16)
        db = (ds * sl).astype(BF16)
        da_ref[...] = da
        db_ref[...] = db
        acc_ref[...] += _dot_nn(da, wg_ref[...]) + _dot_nn(db, wu_ref[...])

        @pl.when(j == nj - 1)
        def _():
            xhat, r = _rms_stats(x_ref[...])
            dx, dg = _rms_bwd(acc_ref[...], xhat, r, g_ref[...])
            dx_ref[...] = dout_ref[...] + dx
            dg_ref[...] += dg

    row = pl.BlockSpec((tm, d), lambda i, j: (i, 0))
    vec = pl.BlockSpec((1, d), lambda i, j: (0, 0))
    wsp = pl.BlockSpec((hc, d), lambda i, j: (j, 0))
    hid = pl.BlockSpec((tm, hc), lambda i, j: (i, j))
    hid_shape = jax.ShapeDtypeStruct((t, DFF), BF16)
    return pl.pallas_call(
        body, name=name, grid=(t // tm, nj), in_specs=[row, row, vec, hid, hid, wsp, wsp, wsp],
        out_specs=[row, row, hid, hid, hid, vec],
        out_shape=[jax.ShapeDtypeStruct((t, d), F32), jax.ShapeDtypeStruct((t, d), BF16), hid_shape, hid_shape, hid_shape,
                   jax.ShapeDtypeStruct((1, d), F32)],
        scratch_shapes=[pltpu.VMEM((tm, d), F32)],
        compiler_params=_params("arbitrary", "arbitrary"),
    )(dout, x, g, a, b, wgT, wuT, wd)


def _rope_fwd(x, c, s1, s2):
    return x * c + pltpu.roll(x, HEAD_PAD - ROPE_HALF, 1) * s1 + pltpu.roll(x, ROPE_HALF, 1) * s2


def _rope_bwd(dy, c, s1, s2):
    return dy * c + pltpu.roll(dy * s1, ROPE_HALF, 1) + pltpu.roll(dy * s2, HEAD_PAD - ROPE_HALF, 1)


def _head_stats(x):
    r = lax.rsqrt(jnp.sum(x * x, axis=-1, keepdims=True) * (1.0 / QK_DIM) + NORM_EPS)
    return x * r, r


def _mla_prep_fwd(lat, gq, gkv, ghq, ghk, wq, wk, wv, rc, rs1, rs2, *, tm, name):
    t = lat.shape[0]

    def body(lat_ref, gq_ref, gkv_ref, ghq_ref, ghk_ref, wq_ref, wk_ref, wv_ref, c_ref, s1_ref, s2_ref,
             q_ref, k_ref, v_ref, qn_ref, ckv_ref):
        lat_v = lat_ref[...]
        qhat, _ = _rms_stats(lat_v[:, :Q_LORA].astype(F32))
        qn = (qhat * gq_ref[...]).astype(BF16)
        khat, _ = _rms_stats(lat_v[:, Q_LORA:Q_LORA + KV_LORA].astype(F32))
        ckv = (khat * gkv_ref[...]).astype(BF16)
        ckv_ext = jnp.concatenate([ckv, lat_v[:, Q_LORA + KV_LORA:]], axis=1)
        qn_ref[...] = qn
        ckv_ref[...] = ckv_ext
        q_pre = _dot_nn(qn, wq_ref[...])
        k_pre = _dot_nn(ckv_ext, wk_ref[...])
        v_ref[...] = _dot_nn(ckv, wv_ref[...]).astype(BF16)
        c, s1, s2 = c_ref[...], s1_ref[...], s2_ref[...]
        for h in range(N_HEADS):
            hs = slice(h * HEAD_PAD, (h + 1) * HEAD_PAD)
            xq, _ = _head_stats(q_pre[:, hs])
            q_ref[:, hs] = _rope_fwd(xq * ghq_ref[...], c, s1, s2).astype(BF16)
            xk, _ = _head_stats(k_pre[:, hs])
            k_ref[:, hs] = _rope_fwd(xk * ghk_ref[...], c, s1, s2).astype(BF16)

    def row(w):
        return pl.BlockSpec((tm, w), lambda i: (i, 0))

    def full(r, w):
        return pl.BlockSpec((r, w), lambda i: (0, 0))

    wide = jax.ShapeDtypeStruct((t, D), BF16)
    lat3 = jax.ShapeDtypeStruct((t, Q_LORA), BF16)
    return pl.pallas_call(
        body, name=name, grid=(t // tm,),
        in_specs=[row(LAT_PAD), full(1, Q_LORA), full(1, KV_LORA), full(1, HEAD_PAD), full(1, HEAD_PAD),
                  full(Q_LORA, D), full(Q_LORA, D), full(KV_LORA, D), row(HEAD_PAD), row(HEAD_PAD), row(HEAD_PAD)],
        out_specs=[row(D), row(D), row(D), row(Q_LORA), row(Q_LORA)],
        out_shape=[wide, wide, wide, lat3, lat3],
        compiler_params=_params("parallel"),
    )(lat, gq, gkv, ghq, ghk, wq, wk, wv, rc, rs1, rs2)


def _mla_prep_bwd(dq, dk, dv, lat, qn, ckv_ext, gq, gkv, ghq, ghk, wq, wk, wv, rc, rs1, rs2, *, tm, name):
    t = lat.shape[0]

    def body(dq_ref, dk_ref, dv_ref, lat_ref, qn_ref, ckv_ref, gq_ref, gkv_ref, ghq_ref, ghk_ref, wq_ref, wk_ref, wv_ref,
             c_ref, s1_ref, s2_ref, dlat_ref, dqp_ref, dkp_ref, dgq_ref, dgkv_ref, dghq_ref, dghk_ref):
        @pl.when(pl.program_id(0) == 0)
        def _():
            dgq_ref[...] = jnp.zeros_like(dgq_ref)
            dgkv_ref[...] = jnp.zeros_like(dgkv_ref)
            dghq_ref[...] = jnp.zeros_like(dghq_ref)
            dghk_ref[...] = jnp.zeros_like(dghk_ref)

        c, s1, s2 = c_ref[...], s1_ref[...], s2_ref[...]
        q_pre = _dot_nn(qn_ref[...], wq_ref[...])
        k_pre = _dot_nn(ckv_ref[...], wk_ref[...])

        def heads(pre, dy_ref, gh_ref, dgh_ref, out_ref):
            dgh = jnp.zeros((1, HEAD_PAD), F32)
            for h in range(N_HEADS):
                hs = slice(h * HEAD_PAD, (h + 1) * HEAD_PAD)
                d = _rope_bwd(dy_ref[:, hs], c, s1, s2)
                xhat, r = _head_stats(pre[:, hs])
                dgh = dgh + jnp.sum(d * xhat, axis=0, keepdims=True)
                dxh = d * gh_ref[...]
                dx = r * (dxh - xhat * (jnp.sum(dxh * xhat, axis=-1, keepdims=True) * (1.0 / QK_DIM)))
                out_ref[:, hs] = dx.astype(BF16)
            dgh_ref[...] += dgh

        heads(q_pre, dq_ref, ghq_ref, dghq_ref, dqp_ref)
        heads(k_pre, dk_ref, ghk_ref, dghk_ref, dkp_ref)
        dqn = _dot_nt(dqp_ref[...], wq_ref[...])
        dce = _dot_nt(dkp_ref[...], wk_ref[...])
        dckv = dce[:, :KV_LORA] + _dot_nt(dv_ref[...], wv_ref[...])
        lat_v = lat_ref[...]
        qhat, rq = _rms_stats(lat_v[:, :Q_LORA].astype(F32))
        dql, dgq = _rms_bwd(dqn, qhat, rq, gq_ref[...])
        khat, rk = _rms_stats(lat_v[:, Q_LORA:Q_LORA + KV_LORA].astype(F32))
        dkl, dgkv = _rms_bwd(dckv, khat, rk, gkv_ref[...])
        dgq_ref[...] += dgq
        dgkv_ref[...] += dgkv
        dlat_ref[...] = jnp.concatenate([dql, dkl, dce[:, KV_LORA:]], axis=1).astype(BF16)

    def row(w):
        return pl.BlockSpec((tm, w), lambda i: (i, 0))

    def full(r, w):
        return pl.BlockSpec((r, w), lambda i: (0, 0))

    return pl.pallas_call(
        body, name=name, grid=(t // tm,),
        in_specs=[row(D), row(D), row(D), row(LAT_PAD), row(Q_LORA), row(Q_LORA), full(1, Q_LORA), full(1, KV_LORA),
                  full(1, HEAD_PAD), full(1, HEAD_PAD), full(Q_LORA, D), full(Q_LORA, D), full(KV_LORA, D),
                  row(HEAD_PAD), row(HEAD_PAD), row(HEAD_PAD)],
        out_specs=[row(LAT_PAD), row(D), row(D), full(1, Q_LORA), full(1, KV_LORA), full(1, HEAD_PAD), full(1, HEAD_PAD)],
        out_shape=[jax.ShapeDtypeStruct((t, LAT_PAD), BF16), jax.ShapeDtypeStruct((t, D), BF16), jax.ShapeDtypeStruct((t, D), BF16),
                   jax.ShapeDtypeStruct((1, Q_LORA), F32), jax.ShapeDtypeStruct((1, KV_LORA), F32),
                   jax.ShapeDtypeStruct((1, HEAD_PAD), F32), jax.ShapeDtypeStruct((1, HEAD_PAD), F32)],
        compiler_params=_params("arbitrary"),
    )(dq, dk, dv, lat, qn, ckv_ext, gq, gkv, ghq, ghk, wq, wk, wv, rc, rs1, rs2)


def _causal_keep(tq):
    r = lax.broadcasted_iota(jnp.int32, (tq, tq), 0)
    c = lax.broadcasted_iota(jnp.int32, (tq, tq), 1)
    return c <= r


def _flash_fwd(q, k, v, *, n_seq, seq, tq, name):
    nq = seq // tq

    def body(q_ref, k_ref, v_ref, o_ref, lse_ref):
        qi = pl.program_id(2)
        qv = q_ref[...]

        def step(j, carry, masked):
            m, l, acc = carry
            kj = k_ref[pl.ds(pl.multiple_of(j * tq, tq), tq), :]
            vj = v_ref[pl.ds(pl.multiple_of(j * tq, tq), tq), :]
            s = _dot_nt(qv, kj) * ATTN_SCALE
            if masked:
                s = jnp.where(_causal_keep(tq), s, NEG)
            m_new = jnp.maximum(m, jnp.max(s, axis=-1, keepdims=True))
            alpha = jnp.exp(m - m_new)
            p = jnp.exp(s - m_new)
            l = alpha * l + jnp.sum(p, axis=-1, keepdims=True)
            acc = alpha * acc + _dot_nn(p.astype(BF16), vj)
            return m_new, l, acc

        init = (jnp.full((tq, 1), NEG, F32), jnp.zeros((tq, 1), F32), jnp.zeros((tq, HEAD_PAD), F32))
        carry = lax.fori_loop(0, qi, lambda j, cr: step(j, cr, False), init)
        m, l, acc = step(qi, carry, True)
        o_ref[...] = (acc / l).astype(BF16)
        lse_ref[...] = jnp.broadcast_to(m + jnp.log(l), (tq, HEAD_PAD))

    qspec = pl.BlockSpec((tq, HEAD_PAD), lambda b, h, i: (b * nq + i, h))
    kspec = pl.BlockSpec((seq, HEAD_PAD), lambda b, h, i: (b, h))
    t = n_seq * seq
    return pl.pallas_call(
        body, name=name, grid=(n_seq, N_HEADS, nq), in_specs=[qspec, kspec, kspec], out_specs=[qspec, qspec],
        out_shape=[jax.ShapeDtypeStruct((t, D), BF16), jax.ShapeDtypeStruct((t, D), F32)],
        compiler_params=_params("parallel", "parallel", "arbitrary"),
    )(q, k, v)


def _flash_bwd(q, k, v, o, lse, do, *, n_seq, seq, tq, name):
    nq = seq // tq

    def body(q_ref, k_ref, v_ref, o_ref, lse_ref, do_ref, dq_ref, dk_ref, dv_ref, dk_acc, dv_acc):
        j = pl.program_id(2)

        @pl.when(j == 0)
        def _():
            dq_ref[...] = jnp.zeros_like(dq_ref)

        dk_acc[...] = jnp.zeros_like(dk_acc)
        dv_acc[...] = jnp.zeros_like(dv_acc)
        kv = k_ref[...]
        vv = v_ref[...]

        def step(i, masked):
            rows = pl.ds(pl.multiple_of(i * tq, tq), tq)
            qi = q_ref[rows, :]
            doi = do_ref[rows, :]
            delta = jnp.sum(doi.astype(F32) * o_ref[rows, :].astype(F32), axis=-1, keepdims=True)
            s = _dot_nt(qi, kv) * ATTN_SCALE
            p = jnp.exp(s - lse_ref[rows, :][:, :1])
            if masked:
                p = jnp.where(_causal_keep(tq), p, 0.0)
            dv_acc[...] += _dot_tn(p.astype(BF16), doi)
            dp = _dot_nt(doi, vv)
            ds = (p * (dp - delta) * ATTN_SCALE).astype(BF16)
            dk_acc[...] += _dot_tn(ds, qi)
            dq_ref[rows, :] += _dot_nn(ds, kv)

        step(j, True)

        def loop_body(i, carry):
            step(i, False)
            return carry

        lax.fori_loop(j + 1, nq, loop_body, 0)
        dk_ref[...] = dk_acc[...]
        dv_ref[...] = dv_acc[...].astype(BF16)

    full = pl.BlockSpec((seq, HEAD_PAD), lambda b, h, j: (b, h))
    tile = pl.BlockSpec((tq, HEAD_PAD), lambda b, h, j: (b * nq + j, h))
    t = n_seq * seq
    return pl.pallas_call(
        body, name=name, grid=(n_seq, N_HEADS, nq), in_specs=[full, tile, tile, full, full, full],
        out_specs=[full, tile, tile],
        out_shape=[jax.ShapeDtypeStruct((t, D), F32), jax.ShapeDtypeStruct((t, D), F32), jax.ShapeDtypeStruct((t, D), BF16)],
        scratch_shapes=[pltpu.VMEM((tq, HEAD_PAD), F32), pltpu.VMEM((tq, HEAD_PAD), F32)],
        compiler_params=_params("parallel", "parallel", "arbitrary"),
    )(q, k, v, o, lse, do)


CONV_CB = 256


def _shift_down(u, k, row):
    return jnp.where(row >= k, pltpu.roll(u, k, 0), 0.0)


def _shift_up(u, k, row, n):
    return jnp.where(row < n - k, pltpu.roll(u, n - k, 0), 0.0)


def _conv_fwd(conv3, cw, *, n_seq, seq, name):
    def body(c_ref, w_ref, p_ref):
        blk = c_ref[...].astype(F32)
        xc, gb, gc = blk[:, :CONV_CB], blk[:, CONV_CB:2 * CONV_CB], blk[:, 2 * CONV_CB:]
        row = lax.broadcasted_iota(jnp.int32, (seq, CONV_CB), 0)
        u = gc * xc
        z = w_ref[0:1, :] * _shift_down(u, 2, row) + w_ref[1:2, :] * _shift_down(u, 1, row) + w_ref[2:3, :] * u
        p_ref[...] = (gb * z).astype(BF16)

    return pl.pallas_call(
        body, name=name, grid=(n_seq, D // CONV_CB),
        in_specs=[pl.BlockSpec((seq, 3 * CONV_CB), lambda b, j: (b, j)), pl.BlockSpec((3, CONV_CB), lambda b, j: (0, j))],
        out_specs=pl.BlockSpec((seq, CONV_CB), lambda b, j: (b, j)),
        out_shape=jax.ShapeDtypeStruct((n_seq * seq, D), BF16),
        compiler_params=_params("parallel", "parallel"),
    )(conv3, cw)


def _conv_bwd(dp, conv3, cw, *, n_seq, seq, name):
    def body(dp_ref, c_ref, w_ref, dc_ref, dw_ref):
        @pl.when(pl.program_id(1) == 0)
        def _():
            dw_ref[...] = jnp.zeros_like(dw_ref)

        blk = c_ref[...].astype(F32)
        xc, gb, gc = blk[:, :CONV_CB], blk[:, CONV_CB:2 * CONV_CB], blk[:, 2 * CONV_CB:]
        row = lax.broadcasted_iota(jnp.int32, (seq, CONV_CB), 0)
        w0, w1, w2 = w_ref[0:1, :], w_ref[1:2, :], w_ref[2:3, :]
        u = gc * xc
        u1 = _shift_down(u, 1, row)
        u2 = _shift_down(u, 2, row)
        z = w0 * u2 + w1 * u1 + w2 * u
        dpv = dp_ref[...].astype(F32)
        dz = dpv * gb
        du = w2 * dz + w1 * _shift_up(dz, 1, row, seq) + w0 * _shift_up(dz, 2, row, seq)
        dc_ref[...] = jnp.concatenate([du * gc, dpv * z, du * xc], axis=1).astype(BF16)
        dw_ref[0:1, :] += jnp.sum(dz * u2, axis=0, keepdims=True)
        dw_ref[1:2, :] += jnp.sum(dz * u1, axis=0, keepdims=True)
        dw_ref[2:3, :] += jnp.sum(dz * u, axis=0, keepdims=True)

    return pl.pallas_call(
        body, name=name, grid=(D // CONV_CB, n_seq),
        in_specs=[pl.BlockSpec((seq, CONV_CB), lambda j, b: (b, j)), pl.BlockSpec((seq, 3 * CONV_CB), lambda j, b: (b, j)),
                  pl.BlockSpec((3, CONV_CB), lambda j, b: (0, j))],
        out_specs=[pl.BlockSpec((seq, 3 * CONV_CB), lambda j, b: (b, j)), pl.BlockSpec((3, CONV_CB), lambda j, b: (0, j))],
        out_shape=[jax.ShapeDtypeStruct((n_seq * seq, CONV_COLS), BF16), jax.ShapeDtypeStruct((3, D), F32)],
        compiler_params=_params("parallel", "arbitrary"),
    )(dp, conv3, cw)


def _merge_fwd(o, p, gl, bias, x1, wpa, wpc, wout, *, tm, name):
    t = x1.shape[0]

    def body(o_ref, p_ref, gl_ref, b_ref, x_ref, wpa_ref, wpc_ref, wout_ref, x2_ref, mg_ref, ya_ref, yb_ref):
        ya = _dot_nn(o_ref[...], wpa_ref[...])
        yb = _dot_nn(p_ref[...], wpc_ref[...])
        gates = _sigmoid(gl_ref[...].astype(F32) + b_ref[...])
        merged = (gates[:, :D] * ya + gates[:, D:] * yb).astype(BF16)
        ya_ref[...] = ya.astype(BF16)
        yb_ref[...] = yb.astype(BF16)
        mg_ref[...] = merged
        x2_ref[...] = x_ref[...] + _dot_nn(merged, wout_ref[...])

    row = pl.BlockSpec((tm, D), lambda i: (i, 0))
    row2 = pl.BlockSpec((tm, GATE_COLS), lambda i: (i, 0))
    wsp = pl.BlockSpec((D, D), lambda i: (0, 0))
    wide = jax.ShapeDtypeStruct((t, D), BF16)
    return pl.pallas_call(
        body, name=name, grid=(t // tm,),
        in_specs=[row, row, row2, pl.BlockSpec((1, GATE_COLS), lambda i: (0, 0)), row, wsp, wsp, wsp],
        out_specs=[row, row, row, row], out_shape=[jax.ShapeDtypeStruct((t, D), F32), wide, wide, wide],
        compiler_params=_params("parallel"),
    )(o, p, gl, bias, x1, wpa, wpc, wout)


def _merge_bwd(dx2, ya, yb, gl, bias, wpa, wpc, wout, *, tm, name):
    t = dx2.shape[0]

    def body(dx_ref, ya_ref, yb_ref, gl_ref, b_ref, wpa_ref, wpc_ref, wout_ref,
             dxb_ref, dya_ref, dyb_ref, dgl_ref, do_ref, dp_ref, db_ref):
        @pl.when(pl.program_id(0) == 0)
        def _():
            db_ref[...] = jnp.zeros_like(db_ref)

        dxb = dx_ref[...].astype(BF16)
        dxb_ref[...] = dxb
        dm = _dot_nt(dxb, wout_ref[...])
        gates = _sigmoid(gl_ref[...].astype(F32) + b_ref[...])
        ga, gb = gates[:, :D], gates[:, D:]
        dya = (dm * ga).astype(BF16)
        dyb = (dm * gb).astype(BF16)
        dya_ref[...] = dya
        dyb_ref[...] = dyb
        dgl = jnp.concatenate([dm * ya_ref[...].astype(F32) * ga * (1.0 - ga),
                               dm * yb_ref[...].astype(F32) * gb * (1.0 - gb)], axis=1)
        dgl_ref[...] = dgl.astype(BF16)
        db_ref[...] += jnp.sum(dgl, axis=0, keepdims=True)
        do_ref[...] = _dot_nt(dya, wpa_ref[...]).astype(BF16)
        dp_ref[...] = _dot_nt(dyb, wpc_ref[...]).astype(BF16)

    row = pl.BlockSpec((tm, D), lambda i: (i, 0))
    row2 = pl.BlockSpec((tm, GATE_COLS), lambda i: (i, 0))
    vec2 = pl.BlockSpec((1, GATE_COLS), lambda i: (0, 0))
    wsp = pl.BlockSpec((D, D), lambda i: (0, 0))
    wide = jax.ShapeDtypeStruct((t, D), BF16)
    return pl.pallas_call(
        body, name=name, grid=(t // tm,), in_specs=[row, row, row, row2, vec2, wsp, wsp, wsp],
        out_specs=[row, row, row, row2, row, row, vec2],
        out_shape=[wide, wide, wide, jax.ShapeDtypeStruct((t, GATE_COLS), BF16), wide, wide,
                   jax.ShapeDtypeStruct((1, GATE_COLS), F32)],
        compiler_params=_params("arbitrary"),
    )(dx2, ya, yb, gl, bias, wpa, wpc, wout)


def _loss_head(y, target, *, tm, name):
    t, d = y.shape

    def body(y_ref, t_ref, dy_ref, loss_ref):
        @pl.when(pl.program_id(0) == 0)
        def _():
            loss_ref[...] = jnp.zeros_like(loss_ref)

        err = y_ref[...] - t_ref[...]
        dy_ref[...] = err * (1.0 / d)
        loss_ref[...] += jnp.sum(jnp.sum(err * err, axis=-1, keepdims=True), axis=0, keepdims=True) * (0.5 / d)

    row = pl.BlockSpec((tm, d), lambda i: (i, 0))
    return pl.pallas_call(
        body, name=name, grid=(t // tm,), in_specs=[row, row], out_specs=[row, pl.BlockSpec((1, 128), lambda i: (0, 0))],
        out_shape=[jax.ShapeDtypeStruct((t, d), F32), jax.ShapeDtypeStruct((1, 128), F32)],
        compiler_params=_params("arbitrary"),
    )(y, target)


def _adamw(w, g, m, v, *, name):
    rows, cols = w.shape
    tr = max([c for c in range(8, 513, 8) if rows % c == 0], default=rows)
    c1 = 1.0 / (1.0 - ADAM_B1 ** ADAM_STEP)
    c2 = 1.0 / (1.0 - ADAM_B2 ** ADAM_STEP)

    def body(w_ref, g_ref, m_ref, v_ref, d_ref, nm_ref, nv_ref):
        gv = g_ref[...]
        nm = ADAM_B1 * m_ref[...] + (1.0 - ADAM_B1) * gv
        nv = ADAM_B2 * v_ref[...] + (1.0 - ADAM_B2) * (gv * gv)
        nm_ref[...] = nm
        nv_ref[...] = nv
        d_ref[...] = -ADAM_LR * ((nm * c1) / (jnp.sqrt(nv * c2) + ADAM_EPS) + ADAM_WD * w_ref[...])

    spec = pl.BlockSpec((tr, cols), lambda i: (i, 0))
    shp = jax.ShapeDtypeStruct((rows, cols), F32)
    return pl.pallas_call(
        body, name=name, grid=(rows // tr,), in_specs=[spec] * 4, out_specs=[spec] * 3, out_shape=[shp] * 3,
        compiler_params=_params("parallel"),
    )(w, g, m, v)


def _place():
    return lax.axis_index("x"), lax.axis_index("y"), lax.axis_index("c")


def _other_chips(x, y):
    return [(1 - x, y), (x, 1 - y), (1 - x, 1 - y)]


def _gather_chips(blk, *, name):
    r, c = blk.shape

    def body(x_ref, g_ref, send_sems, recv_sems, local_sem):
        x, y, cc = _place()
        me = 4 * x + 2 * y + cc
        mine = pltpu.make_async_copy(x_ref, g_ref.at[me], local_sem)
        mine.start()
        sends = []
        for k, (px, py) in enumerate(_other_chips(x, y)):
            cp = pltpu.make_async_remote_copy(src_ref=x_ref, dst_ref=g_ref.at[me], send_sem=send_sems.at[k], recv_sem=recv_sems.at[k],
                                              device_id=(px, py, cc), device_id_type=MESH)
            cp.start()
            sends.append(cp)
        for k, (px, py) in enumerate(_other_chips(x, y)):
            pltpu.make_async_remote_copy(src_ref=x_ref, dst_ref=g_ref.at[4 * px + 2 * py + cc], send_sem=send_sems.at[k],
                                         recv_sem=recv_sems.at[k], device_id=(px, py, cc), device_id_type=MESH).wait_recv()
        for cp in sends:
            cp.wait_send()
        mine.wait()

    return pl.pallas_call(
        body, name=name, in_specs=[ANY], out_specs=ANY, out_shape=jax.ShapeDtypeStruct((N_DEV, r, c), blk.dtype),
        scratch_shapes=[pltpu.SemaphoreType.DMA((3,)), pltpu.SemaphoreType.DMA((3,)), pltpu.SemaphoreType.DMA(())],
    )(blk)


def _gather_sibling(g, *, name):
    def body(g_in, g_ref, send_sems, recv_sems):
        x, y, cc = _place()
        sends = []
        for q in range(4):
            cp = pltpu.make_async_remote_copy(src_ref=g_in.at[2 * q + cc], dst_ref=g_ref.at[2 * q + cc], send_sem=send_sems.at[q],
                                              recv_sem=recv_sems.at[q], device_id=(x, y, 1 - cc), device_id_type=MESH)
            cp.start()
            sends.append(cp)
        for q in range(4):
            pltpu.make_async_remote_copy(src_ref=g_in.at[2 * q + cc], dst_ref=g_ref.at[2 * q + 1 - cc], send_sem=send_sems.at[q],
                                         recv_sem=recv_sems.at[q], device_id=(x, y, 1 - cc), device_id_type=MESH).wait_recv()
        for cp in sends:
            cp.wait_send()

    return pl.pallas_call(
        body, name=name, in_specs=[ANY], out_specs=ANY, out_shape=jax.ShapeDtypeStruct(g.shape, g.dtype),
        input_output_aliases={0: 0},
        scratch_shapes=[pltpu.SemaphoreType.DMA((4,)), pltpu.SemaphoreType.DMA((4,))],
    )(g)


def _scatter_sibling(p, *, name):
    _, r, c = p.shape

    def body(p_ref, q_ref, send_sems, recv_sems):
        x, y, cc = _place()
        sends = []
        for q in range(4):
            cp = pltpu.make_async_remote_copy(src_ref=p_ref.at[2 * q + 1 - cc], dst_ref=q_ref.at[q], send_sem=send_sems.at[q],
                                              recv_sem=recv_sems.at[q], device_id=(x, y, 1 - cc), device_id_type=MESH)
            cp.start()
            sends.append(cp)
        for q in range(4):
            pltpu.make_async_remote_copy(src_ref=p_ref.at[q], dst_ref=q_ref.at[q], send_sem=send_sems.at[q],
                                         recv_sem=recv_sems.at[q], device_id=(x, y, 1 - cc), device_id_type=MESH).wait_recv()
        for cp in sends:
            cp.wait_send()

    return pl.pallas_call(
        body, name=name, in_specs=[ANY], out_specs=ANY, out_shape=jax.ShapeDtypeStruct((4, r, c), p.dtype),
        scratch_shapes=[pltpu.SemaphoreType.DMA((4,)), pltpu.SemaphoreType.DMA((4,))],
    )(p)


def _scatter_chips(s1, *, name):
    _, r, c = s1.shape

    def body(s_ref, r_ref, send_sems, recv_sems):
        x, y, cc = _place()
        sends = []
        for k, (px, py) in enumerate(_other_chips(x, y)):
            cp = pltpu.make_async_remote_copy(src_ref=s_ref.at[2 * px + py], dst_ref=r_ref.at[k], send_sem=send_sems.at[k],
                                              recv_sem=recv_sems.at[k], device_id=(px, py, cc), device_id_type=MESH)
            cp.start()
            sends.append(cp)
        for k, (px, py) in enumerate(_other_chips(x, y)):
            pltpu.make_async_remote_copy(src_ref=s_ref.at[k], dst_ref=r_ref.at[k], send_sem=send_sems.at[k],
                                         recv_sem=recv_sems.at[k], device_id=(px, py, cc), device_id_type=MESH).wait_recv()
        for cp in sends:
            cp.wait_send()

    return pl.pallas_call(
        body, name=name, in_specs=[ANY], out_specs=ANY, out_shape=jax.ShapeDtypeStruct((3, r, c), s1.dtype),
        scratch_shapes=[pltpu.SemaphoreType.DMA((3,)), pltpu.SemaphoreType.DMA((3,))],
    )(s1)


def _sum_sibling(p, q, core, *, name):
    _, r, c = p.shape

    def body(core_ref, p_ref, q_ref, o_ref):
        o_ref[...] = (p_ref[...].astype(F32) + q_ref[...].astype(F32)).astype(BF16)

    grid_spec = pltpu.PrefetchScalarGridSpec(
        num_scalar_prefetch=1, grid=(4, r // PACK_TILE),
        in_specs=[pl.BlockSpec((1, PACK_TILE, c), lambda ch, i, core_ref: (2 * ch + core_ref[0], i, 0)),
                  pl.BlockSpec((1, PACK_TILE, c), lambda ch, i, core_ref: (ch, i, 0))],
        out_specs=pl.BlockSpec((1, PACK_TILE, c), lambda ch, i, core_ref: (ch, i, 0)))
    return pl.pallas_call(
        body, name=name, grid_spec=grid_spec, out_shape=jax.ShapeDtypeStruct((4, r, c), BF16),
        compiler_params=_params("parallel", "parallel"),
    )(core, p, q)


def _sum_chips(s1, r2, chip, *, name):
    _, r, c = s1.shape

    def body(chip_ref, s_ref, r_ref, o_ref):
        acc = s_ref[0].astype(F32)
        for k in range(3):
            acc = acc + r_ref[k].astype(F32)
        o_ref[...] = acc

    grid_spec = pltpu.PrefetchScalarGridSpec(
        num_scalar_prefetch=1, grid=(r // PACK_TILE,),
        in_specs=[pl.BlockSpec((1, PACK_TILE, c), lambda i, chip_ref: (chip_ref[0], i, 0)),
                  pl.BlockSpec((3, PACK_TILE, c), lambda i, chip_ref: (0, i, 0))],
        out_specs=pl.BlockSpec((PACK_TILE, c), lambda i, chip_ref: (i, 0)))
    return pl.pallas_call(
        body, name=name, grid_spec=grid_spec, out_shape=jax.ShapeDtypeStruct((r, c), F32),
        compiler_params=_params("parallel"),
    )(chip, s1, r2)


def _small_exchange(v, *, reduce, name):
    r, c = v.shape

    def body(x_ref, o_ref, *rest):
        if reduce:
            buf_ref, send_sems, recv_sems = rest
        else:
            buf_ref = o_ref
            send_sems, recv_sems = rest
        x, y, cc = _place()
        me = 4 * x + 2 * y + cc

        def peer(k):
            return ((1 - x) if k & 4 else x, (1 - y) if k & 2 else y, (1 - cc) if k & 1 else cc)

        buf_ref[me] = x_ref[...]
        sends = []
        for k in range(1, N_DEV):
            cp = pltpu.make_async_remote_copy(src_ref=x_ref, dst_ref=buf_ref.at[me], send_sem=send_sems.at[k - 1],
                                              recv_sem=recv_sems.at[k - 1], device_id=peer(k), device_id_type=MESH)
            cp.start()
            sends.append(cp)
        for k in range(1, N_DEV):
            px, py, pc = peer(k)
            pltpu.make_async_remote_copy(src_ref=x_ref, dst_ref=buf_ref.at[4 * px + 2 * py + pc], send_sem=send_sems.at[k - 1],
                                         recv_sem=recv_sems.at[k - 1], device_id=peer(k), device_id_type=MESH).wait_recv()
        for cp in sends:
            cp.wait_send()
        if reduce:
            acc = buf_ref[0]
            for s in range(1, N_DEV):
                acc = acc + buf_ref[s]
            o_ref[...] = acc

    vm = pl.BlockSpec(memory_space=pltpu.VMEM)
    sems = [pltpu.SemaphoreType.DMA((N_DEV - 1,)), pltpu.SemaphoreType.DMA((N_DEV - 1,))]
    if reduce:
        out_shape, scratch = jax.ShapeDtypeStruct((r, c), F32), [pltpu.VMEM((N_DEV, r, c), F32)] + sems
    else:
        out_shape, scratch = jax.ShapeDtypeStruct((N_DEV, r, c), F32), sems
    return pl.pallas_call(body, name=name, in_specs=[vm], out_specs=vm, out_shape=out_shape, scratch_shapes=scratch)(v)


def _rows(a):
    return a.reshape(-1, D)


def _pad_cols(a, to):
    return jnp.pad(a, ((0, 0), (0, to - a.shape[1])))


def _pack_weights(w):
    parts = {
        "ffn1_gT": w["ffn1_w_gate"].T, "ffn1_uT": w["ffn1_w_up"].T, "ffn1_d": w["ffn1_w_down"],
        "ffn2_gT": w["ffn2_w_gate"].T, "ffn2_uT": w["ffn2_w_up"].T, "ffn2_d": w["ffn2_w_down"],
        "w_inT": jnp.pad(w["w_in"].T, ((0, IN_SHARD_PAD - IN_SHARD), (0, 0))),
        "w_uq": _rows(_pad_cols(w["w_uq"], HEAD_PAD)), "w_uk": _rows(_pad_cols(w["w_uk"], HEAD_PAD)),
        "w_uv": _rows(_pad_cols(w["w_uv"], HEAD_PAD)), "w_pa": _rows(w["w_proj_attn"]),
        "w_pc": w["w_proj_conv"], "w_out": w["w_out"],
    }
    return jnp.concatenate([parts[n].astype(BF16) for n, _ in PACK], axis=0)


def _cols_from_shards(g, name, rows):
    off, r = PACK_OFF[name]
    return g[:, off:off + r].reshape(N_DEV, rows, HEAD_PAD).transpose(1, 0, 2).reshape(rows, N_DEV * HEAD_PAD)


def _rows_from_shards(g, name, keep=None):
    off, r = PACK_OFF[name]
    keep = r if keep is None else keep
    return g[:, off:off + keep].reshape(N_DEV * keep, D)


def _rope_placement():
    i = lax.broadcasted_iota(jnp.int32, (HEAD_PAD, D), 0)
    j = lax.broadcasted_iota(jnp.int32, (HEAD_PAD, D), 1)
    return ((i < 2 * ROPE_HALF) & (j % HEAD_PAD == NOPE + i)).astype(BF16)


def _unpack_weights(g):
    w_inT = _rows_from_shards(g, "w_inT", IN_SHARD)
    lat_rows = Q_LORA + KV_LORA + 2 * ROPE_HALF
    conv = w_inT[lat_rows:lat_rows + CONV_COLS].reshape(3, D // CONV_CB, CONV_CB, D).transpose(1, 0, 2, 3).reshape(CONV_COLS, D)
    wpa = _cols_from_shards(g, "w_pa", 512).reshape(N_HEADS, NOPE, D)
    return {
        "ffn1_gT": _rows_from_shards(g, "ffn1_gT"), "ffn1_uT": _rows_from_shards(g, "ffn1_uT"), "ffn1_d": _rows_from_shards(g, "ffn1_d"),
        "ffn2_gT": _rows_from_shards(g, "ffn2_gT"), "ffn2_uT": _rows_from_shards(g, "ffn2_uT"), "ffn2_d": _rows_from_shards(g, "ffn2_d"),
        "latT": jnp.pad(w_inT[:lat_rows], ((0, LAT_PAD - lat_rows), (0, 0))),
        "convT": conv, "gateT": w_inT[lat_rows + CONV_COLS:],
        "wq": _cols_from_shards(g, "w_uq", Q_LORA),
        "wk": jnp.concatenate([_cols_from_shards(g, "w_uk", KV_LORA), _rope_placement()], axis=0),
        "wv": _cols_from_shards(g, "w_uv", KV_LORA),
        "wpa": jnp.pad(wpa, ((0, 0), (0, HEAD_PAD - NOPE), (0, 0))).reshape(D, D),
        "wpc": _rows_from_shards(g, "w_pc"), "wout": _rows_from_shards(g, "w_out"),
    }


def _shards_from_cols(a):
    rows = a.shape[0]
    return a.reshape(rows, N_DEV, HEAD_PAD).transpose(1, 0, 2).reshape(N_DEV, rows * HEAD_PAD // D, D)


def _pack_grads(gw):
    lat_rows = Q_LORA + KV_LORA + 2 * ROPE_HALF
    conv = gw["convT"].reshape(D // CONV_CB, 3, CONV_CB, D).transpose(1, 0, 2, 3).reshape(CONV_COLS, D)
    w_inT = jnp.concatenate([gw["latT"][:lat_rows], conv, gw["gateT"]], axis=0).reshape(N_DEV, IN_SHARD, D)
    wpa = gw["wpa"].reshape(N_HEADS, HEAD_PAD, D)[:, :NOPE].reshape(N_HEADS * NOPE, D)
    parts = {n: gw[n].reshape(N_DEV, FF_SHARD, D) for n in ("ffn1_gT", "ffn1_uT", "ffn1_d", "ffn2_gT", "ffn2_uT", "ffn2_d")}
    parts.update({
        "w_inT": jnp.pad(w_inT, ((0, 0), (0, IN_SHARD_PAD - IN_SHARD), (0, 0))),
        "w_uq": _shards_from_cols(gw["wq"]), "w_uk": _shards_from_cols(gw["wk"][:KV_LORA]),
        "w_uv": _shards_from_cols(gw["wv"][:KV_LORA]), "w_pa": _shards_from_cols(wpa),
        "w_pc": gw["wpc"].reshape(N_DEV, D // N_DEV, D), "w_out": gw["wout"].reshape(N_DEV, D // N_DEV, D),
    })
    return jnp.concatenate([parts[n] for n, _ in PACK], axis=1)


def _unpack_grads(mine):
    def seg(name, keep=None):
        off, r = PACK_OFF[name]
        return mine[off:off + (r if keep is None else keep)]

    return {
        "ffn1_w_gate": seg("ffn1_gT").T, "ffn1_w_up": seg("ffn1_uT").T, "ffn1_w_down": seg("ffn1_d"),
        "ffn2_w_gate": seg("ffn2_gT").T, "ffn2_w_up": seg("ffn2_uT").T, "ffn2_w_down": seg("ffn2_d"),
        "w_in": seg("w_inT", IN_SHARD).T,
        "w_uq": seg("w_uq").reshape(Q_LORA, HEAD_PAD)[:, :QK_DIM],
        "w_uk": seg("w_uk").reshape(KV_LORA, HEAD_PAD)[:, :NOPE],
        "w_uv": seg("w_uv").reshape(KV_LORA, HEAD_PAD)[:, :NOPE],
        "w_proj_attn": seg("w_pa").reshape(512, HEAD_PAD),
        "w_proj_conv": seg("w_pc"), "w_out": seg("w_out"),
    }


def _rope_tables(positions):
    inv_freq = 1.0 / (ROPE_THETA ** (jnp.arange(ROPE_HALF, dtype=F32) / ROPE_HALF))
    ang = positions.reshape(-1).astype(F32)[:, None] * inv_freq
    cos, sin = jnp.cos(ang), jnp.sin(ang)
    t = ang.shape[0]
    zero = jnp.zeros((t, ROPE_HALF), F32)
    head = jnp.ones((t, NOPE), F32)
    tail = jnp.zeros((t, HEAD_PAD - QK_DIM), F32)
    nohead = jnp.zeros((t, NOPE), F32)
    rc = jnp.concatenate([head, cos, cos, tail], axis=1)
    rs1 = jnp.concatenate([nohead, -sin, zero, tail], axis=1)
    rs2 = jnp.concatenate([nohead, zero, sin, tail], axis=1)
    return rc, rs1, rs2


def _local_step(x, positions, target, W, conv_w, small):
    n_seq, seq, d = x.shape
    t = n_seq * seq
    x0 = x.reshape(t, d)
    tgt = target.reshape(t, d)
    rc, rs1, rs2 = _rope_tables(positions)
    ghq = _pad_cols(small["q_head_norm"], HEAD_PAD)
    ghk = _pad_cols(small["k_head_norm"], HEAD_PAD)
    TM, HC, TQ = 1024, 256, 512

    x1, h1, a1, b1 = _ffn_fwd(x0, small["ffn1_norm"], W["ffn1_gT"], W["ffn1_uT"], W["ffn1_d"], tm=TM, hc=HC, name="ffn1_fwd")
    hm = _rms_fwd(x1, small["mix_norm"], tm=TM, name="mix_norm_fwd")
    lat = _mm(hm, W["latT"], mode="nt", out_dtype=BF16, tm=TM, tn=LAT_PAD, tk=D, name="proj_lat")
    conv3 = _mm(hm, W["convT"], mode="nt", out_dtype=BF16, tm=TM, tn=CONV_COLS // 2, tk=D, name="proj_conv")
    gl = _mm(hm, W["gateT"], mode="nt", out_dtype=BF16, tm=TM, tn=GATE_COLS // 2, tk=D, name="proj_gate")
    q, k, v, qn, ckv = _mla_prep_fwd(lat, small["q_a_norm"], small["kv_a_norm"], ghq, ghk, W["wq"], W["wk"], W["wv"], rc, rs1, rs2,
                                     tm=512, name="mla_prep_fwd")
    o, lse = _flash_fwd(q, k, v, n_seq=n_seq, seq=seq, tq=TQ, name="attn_fwd")
    p = _conv_fwd(conv3, conv_w, n_seq=n_seq, seq=seq, name="conv_fwd")
    x2, merged, ya, yb = _merge_fwd(o, p, gl, small["gate_bias"], x1, W["wpa"], W["wpc"], W["wout"], tm=512, name="merge_fwd")
    y, h2, a2, b2 = _ffn_fwd(x2, small["ffn2_norm"], W["ffn2_gT"], W["ffn2_uT"], W["ffn2_d"], tm=TM, hc=HC, name="ffn2_fwd")
    dy, loss_row = _loss_head(y, tgt, tm=TM, name="loss_head")

    gw, gs = {}, {}

    def wgrad(a, b, name, tm=None):
        m = a.shape[1]
        return _mm(a, b, mode="tn", out_dtype=BF16, tm=tm or m, tn=b.shape[1], tk=512, name=name)

    dx2, dyb2, da2, db2, s2, gs["ffn2_norm"] = _ffn_bwd(dy, x2, small["ffn2_norm"], a2, b2, W["ffn2_gT"], W["ffn2_uT"], W["ffn2_d"],
                                                        tm=TM, hc=HC, name="ffn2_bwd")
    gw["ffn2_gT"] = wgrad(da2, h2, "ffn2_dwg", tm=DFF // 2)
    gw["ffn2_uT"] = wgrad(db2, h2, "ffn2_dwu", tm=DFF // 2)
    gw["ffn2_d"] = wgrad(s2, dyb2, "ffn2_dwd", tm=DFF // 2)

    dx2b, dya, dyb, dgl, do, dp, gs["gate_bias"] = _merge_bwd(dx2, ya, yb, gl, small["gate_bias"], W["wpa"], W["wpc"], W["wout"],
                                                               tm=512, name="merge_bwd")
    gw["wout"] = wgrad(merged, dx2b, "dw_out")
    gw["wpa"] = wgrad(o, dya, "dw_pa")
    gw["wpc"] = wgrad(p, dyb, "dw_pc")
    dconv3, dconv_w = _conv_bwd(dp, conv3, conv_w, n_seq=n_seq, seq=seq, name="conv_bwd")
    dq, dk, dv = _flash_bwd(q, k, v, o, lse, do, n_seq=n_seq, seq=seq, tq=TQ, name="attn_bwd")
    dlat, dqp, dkp, gs["q_a_norm"], gs["kv_a_norm"], dghq, dghk = _mla_prep_bwd(
        dq, dk, dv, lat, qn, ckv, small["q_a_norm"], small["kv_a_norm"], ghq, ghk, W["wq"], W["wk"], W["wv"], rc, rs1, rs2,
        tm=512, name="mla_prep_bwd")
    gs["q_head_norm"], gs["k_head_norm"] = dghq[:, :QK_DIM], dghk[:, :QK_DIM]
    gw["wq"] = wgrad(qn, dqp, "dw_uq")
    gw["wk"] = wgrad(ckv, dkp, "dw_uk")
    gw["wv"] = wgrad(ckv, dv, "dw_uv")
    gw["latT"] = wgrad(dlat, hm, "dw_lat")
    gw["convT"] = wgrad(dconv3, hm, "dw_conv", tm=CONV_COLS // 2)
    gw["gateT"] = wgrad(dgl, hm, "dw_gate")
    dh = _mm(dlat, W["latT"], mode="nn", out_dtype=F32, tm=TM, tn=D, tk=LAT_PAD, name="dh_lat")
    dh = _mm(dconv3, W["convT"], mode="nn", out_dtype=F32, tm=TM, tn=D, tk=1024, name="dh_conv", add=dh)
    dh = _mm(dgl, W["gateT"], mode="nn", out_dtype=F32, tm=TM, tn=D, tk=1024, name="dh_gate", add=dh)
    dx1, gs["mix_norm"] = _rms_bwd_res(dh, x1, small["mix_norm"], dx2, tm=TM, name="mix_norm_bwd")

    dx0, dyb1, da1, db1, s1, gs["ffn1_norm"] = _ffn_bwd(dx1, x0, small["ffn1_norm"], a1, b1, W["ffn1_gT"], W["ffn1_uT"], W["ffn1_d"],
                                                        tm=TM, hc=HC, name="ffn1_bwd")
    gw["ffn1_gT"] = wgrad(da1, h1, "ffn1_dwg", tm=DFF // 2)
    gw["ffn1_uT"] = wgrad(db1, h1, "ffn1_dwu", tm=DFF // 2)
    gw["ffn1_d"] = wgrad(s1, dyb1, "ffn1_dwd", tm=DFF // 2)
    return loss_row, dx0.reshape(n_seq, seq, d), gw, dconv_w, gs


SMALL_NAMES = ("ffn1_norm", "mix_norm", "gate_bias", "q_a_norm", "kv_a_norm", "q_head_norm", "k_head_norm", "ffn2_norm")
SMALL_SLOTS = {"ffn1_norm": 1024, "mix_norm": 1024, "gate_bias": 2048, "q_a_norm": 384, "kv_a_norm": 256, "q_head_norm": 128,
               "k_head_norm": 128, "ffn2_norm": 1024, "conv_w": 3072, "loss": 128}
WEIGHT_NAMES = ("ffn1_norm", "ffn1_w_gate", "ffn1_w_up", "ffn1_w_down", "mix_norm", "w_in", "gate_bias", "q_a_norm", "w_uq",
                "kv_a_norm", "w_uk", "w_uv", "q_head_norm", "k_head_norm", "w_proj_attn", "conv_w", "w_proj_conv", "w_out",
                "ffn2_norm", "ffn2_w_gate", "ffn2_w_up", "ffn2_w_down")


def _step(x, positions, loss_target, w, m, v):
    xi, yi, ci = _place()
    core = ci.astype(jnp.int32).reshape(1)
    chip = (2 * xi + yi).astype(jnp.int32).reshape(1)
    me = 4 * xi + 2 * yi + ci

    gathered = _gather_sibling(_gather_chips(_pack_weights(w), name="gather_chips"), name="gather_sibling")
    W = _unpack_weights(gathered)
    cw_all = _small_exchange(jnp.pad(w["conv_w"], ((0, 5), (0, 0))), reduce=False, name="gather_conv_w")
    conv_w = cw_all[:, :3].transpose(1, 0, 2).reshape(3, D)
    small = {n: w[n].reshape(1, -1) for n in SMALL_NAMES}

    loss_row, grad_x, gw, dconv_w, gs = _local_step(x, positions, loss_target, W, conv_w, small)

    packed = _pack_grads(gw)
    from_sibling = _scatter_sibling(packed, name="scatter_sibling")
    s1 = _sum_sibling(packed, from_sibling, core, name="sum_sibling")
    from_chips = _scatter_chips(s1, name="scatter_chips")
    grads = _unpack_grads(_sum_chips(s1, from_chips, chip, name="sum_chips"))

    pieces = [_pad_cols(gs[n], SMALL_SLOTS[n]) for n in SMALL_NAMES] + [dconv_w.reshape(1, 3 * D), loss_row]
    total = _small_exchange(jnp.concatenate(pieces, axis=1).reshape(-1, 128), reduce=True, name="reduce_small").reshape(-1)
    off = 0
    for n in SMALL_NAMES:
        grads[n] = total[off:off + w[n].shape[0]]
        off += SMALL_SLOTS[n]
    conv_full = total[off:off + 3 * D].reshape(3, D)
    grads["conv_w"] = lax.dynamic_slice(conv_full, (0, me * HEAD_PAD), (3, HEAD_PAD))
    loss = total[off + 3 * D]

    deltas, new_m, new_v = {}, {}, {}
    for n in WEIGHT_NAMES:
        shape = w[n].shape
        if len(shape) == 1:
            view = (-1, 128) if shape[0] % 128 == 0 else (1, shape[0])
        else:
            view = shape
        dlt, nm, nv = _adamw(w[n].reshape(view), grads[n].reshape(view), m[n].reshape(view), v[n].reshape(view), name="adamw_" + n)
        deltas[n], new_m[n], new_v[n] = dlt.reshape(shape), nm.reshape(shape), nv.reshape(shape)
    return (loss, grad_x, *[grads[n] for n in WEIGHT_NAMES], *[deltas[n] for n in WEIGHT_NAMES],
            *[new_m[n] for n in WEIGHT_NAMES], *[new_v[n] for n in WEIGHT_NAMES])


def kernel(x, positions, ffn1_norm, ffn1_w_gate, ffn1_w_up, ffn1_w_down, mix_norm, w_in, gate_bias, q_a_norm, w_uq, kv_a_norm, w_uk, w_uv, q_head_norm, k_head_norm, w_proj_attn, conv_w, w_proj_conv, w_out, ffn2_norm, ffn2_w_gate, ffn2_w_up, ffn2_w_down, loss_target, m_ffn1_norm, m_ffn1_w_gate, m_ffn1_w_up, m_ffn1_w_down, m_mix_norm, m_w_in, m_gate_bias, m_q_a_norm, m_w_uq, m_kv_a_norm, m_w_uk, m_w_uv, m_q_head_norm, m_k_head_norm, m_w_proj_attn, m_conv_w, m_w_proj_conv, m_w_out, m_ffn2_norm, m_ffn2_w_gate, m_ffn2_w_up, m_ffn2_w_down, v_ffn1_norm, v_ffn1_w_gate, v_ffn1_w_up, v_ffn1_w_down, v_mix_norm, v_w_in, v_gate_bias, v_q_a_norm, v_w_uq, v_kv_a_norm, v_w_uk, v_w_uv, v_q_head_norm, v_k_head_norm, v_w_proj_attn, v_conv_w, v_w_proj_conv, v_w_out, v_ffn2_norm, v_ffn2_w_gate, v_ffn2_w_up, v_ffn2_w_down):
    given = dict(locals())
    w = {n: given[n] for n in WEIGHT_NAMES}
    m = {n: given["m_" + n] for n in WEIGHT_NAMES}
    v = {n: given["v_" + n] for n in WEIGHT_NAMES}
    return _step(x, positions, loss_target, w, m, v)
```

```python
import functools

import jax
import jax.numpy as jnp
from jax import lax
from jax.experimental import pallas as pl
from jax.experimental.pallas import tpu as pltpu

F32 = jnp.float32
BF16 = jnp.bfloat16
MESH = pl.DeviceIdType.MESH
ANY = pl.BlockSpec(memory_space=pl.ANY)

N_DEV = 8
D = 1024
DFF = 2816
N_HEADS = 8
HEAD_PAD = 128
QK_DIM = 96
NOPE = 64
ROPE_HALF = 16
Q_LORA = 384
KV_LORA = 256
LAT_PAD = 768
CONV_COLS = 3072
GATE_COLS = 2048
IN_DIM = 5792
IN_SHARD = IN_DIM // N_DEV
IN_SHARD_PAD = 736
FF_SHARD = DFF // N_DEV
ROPE_THETA = 10000.0
NORM_EPS = 1e-6
ATTN_SCALE = QK_DIM ** -0.5
NEG = -1e30

ADAM_LR, ADAM_B1, ADAM_B2, ADAM_EPS, ADAM_WD, ADAM_STEP = 0.001, 0.9, 0.999, 1e-08, 0.01, 10

PACK = (("w_inT", IN_SHARD_PAD), ("w_uq", 48), ("w_uk", 32), ("w_uv", 32), ("w_pa", 64), ("w_pc", 128), ("w_out", 128))
PACK_ROWS = sum(r for _, r in PACK)
PACK_OFF = {}
_o = 0
for _n, _r in PACK:
    PACK_OFF[_n] = (_o, _r)
    _o += _r

VMEM_LIMIT = 56 * 1024 * 1024


def _params(*sem):
    return pltpu.CompilerParams(dimension_semantics=sem if sem else None, vmem_limit_bytes=VMEM_LIMIT)


class _Plan:
    def __init__(self, start, wait, n_remote, n_local):
        self.start, self.wait, self.n_remote, self.n_local = start, wait, n_remote, n_local

    def sems(self):
        return [pltpu.SemaphoreType.DMA((self.n_remote,)), pltpu.SemaphoreType.DMA((self.n_remote,)),
                pltpu.SemaphoreType.DMA((max(self.n_local, 1),))]


def _call(body, *, name, grid, in_specs, out_specs, out_shape, scratch_shapes, operands, sem, hosted=None):
    if hosted is None:
        outs = pl.pallas_call(body, name=name, grid=grid, in_specs=in_specs, out_specs=out_specs, out_shape=out_shape,
                              scratch_shapes=scratch_shapes, compiler_params=_params(*sem))(*operands)
        return outs, None
    plan, srcs, h_shapes = hosted
    n_in, n_out, n_scr, nh_in, nh_out = len(in_specs), len(out_specs), len(scratch_shapes), len(srcs), len(h_shapes)

    def full_body(*refs):
        ins, refs = refs[:n_in], refs[n_in:]
        h_in, refs = refs[:nh_in], refs[nh_in:]
        outs, refs = refs[:n_out], refs[n_out:]
        h_out, refs = refs[:nh_out], refs[nh_out:]
        scr, sems = refs[:n_scr], refs[n_scr:]
        ids = [pl.program_id(ax) for ax in range(len(grid))]
        first = functools.reduce(jnp.logical_and, [i == 0 for i in ids])
        last = functools.reduce(jnp.logical_and, [i == g - 1 for i, g in zip(ids, grid)])

        @pl.when(first)
        def _():
            plan.start(h_in, h_out, *sems)

        body(*ins, *outs, *scr)

        @pl.when(last)
        def _():
            plan.wait(h_in, h_out, *sems)

    res = pl.pallas_call(
        full_body, name=name, grid=grid, in_specs=list(in_specs) + [ANY] * nh_in, out_specs=list(out_specs) + [ANY] * nh_out,
        out_shape=list(out_shape) + list(h_shapes), scratch_shapes=list(scratch_shapes) + plan.sems(),
        compiler_params=_params(*(["arbitrary"] * len(grid))),
    )(*operands, *srcs)
    return res[:n_out], res[n_out:]


def _dot_nn(a, b):
    return lax.dot_general(a, b, (((1,), (0,)), ((), ())), preferred_element_type=F32)


def _dot_nt(a, b):
    return lax.dot_general(a, b, (((1,), (1,)), ((), ())), preferred_element_type=F32)


def _dot_tn(a, b):
    return lax.dot_general(a, b, (((0,), (0,)), ((), ())), preferred_element_type=F32)


def _sigmoid(x):
    return 1.0 / (1.0 + jnp.exp(-x))


def _rms_stats(x):
    r = lax.rsqrt(jnp.mean(x * x, axis=-1, keepdims=True) + NORM_EPS)
    return x * r, r


def _rms_bwd(dy, xhat, r, g):
    dg = jnp.sum(dy * xhat, axis=0, keepdims=True)
    dxh = dy * g
    dx = r * (dxh - xhat * jnp.mean(dxh * xhat, axis=-1, keepdims=True))
    return dx, dg


def _mm(a, b, *, mode, out_dtype, tm, tn, tk, name, add=None):
    if mode == "nn":
        (m, k), (_, n) = a.shape, b.shape
    elif mode == "nt":
        (m, k), (n, _) = a.shape, b.shape
    else:
        (k, m), (_, n) = a.shape, b.shape
    assert m % tm == 0 and n % tn == 0 and k % tk == 0, (name, m, n, k, tm, tn, tk)
    nk = k // tk
    dot = {"nn": _dot_nn, "nt": _dot_nt, "tn": _dot_tn}[mode]
    a_spec = pl.BlockSpec((tk, tm), lambda i, j, kk: (kk, i)) if mode == "tn" else pl.BlockSpec((tm, tk), lambda i, j, kk: (i, kk))
    b_spec = pl.BlockSpec((tn, tk), lambda i, j, kk: (j, kk)) if mode == "nt" else pl.BlockSpec((tk, tn), lambda i, j, kk: (kk, j))
    o_spec = pl.BlockSpec((tm, tn), lambda i, j, kk: (i, j))
    has_add = add is not None

    def body(*refs):
        if has_add:
            a_ref, b_ref, c_ref, o_ref, acc_ref = refs
        else:
            a_ref, b_ref, o_ref, acc_ref = refs
        kk = pl.program_id(2)

        @pl.when(kk == 0)
        def _():
            acc_ref[...] = c_ref[...] if has_add else jnp.zeros_like(acc_ref)

        acc_ref[...] += dot(a_ref[...], b_ref[...])

        @pl.when(kk == nk - 1)
        def _():
            o_ref[...] = acc_ref[...].astype(out_dtype)

    operands = (a, b, add) if has_add else (a, b)
    in_specs = [a_spec, b_spec] + ([o_spec] if has_add else [])
    return pl.pallas_call(
        body, name=name, grid=(m // tm, n // tn, nk), in_specs=in_specs, out_specs=o_spec,
        out_shape=jax.ShapeDtypeStruct((m, n), out_dtype), scratch_shapes=[pltpu.VMEM((tm, tn), F32)],
        compiler_params=_params("parallel", "parallel", "arbitrary"),
    )(*operands)


def _rms_fwd(x, g, *, tm, name):
    t, d = x.shape

    def body(x_ref, g_ref, h_ref):
        xhat, _ = _rms_stats(x_ref[...])
        h_ref[...] = (xhat * g_ref[...]).astype(BF16)

    return pl.pallas_call(
        body, name=name, grid=(t // tm,),
        in_specs=[pl.BlockSpec((tm, d), lambda i: (i, 0)), pl.BlockSpec((1, d), lambda i: (0, 0))],
        out_specs=pl.BlockSpec((tm, d), lambda i: (i, 0)), out_shape=jax.ShapeDtypeStruct((t, d), BF16),
        compiler_params=_params("parallel"),
    )(x, g)


def _rms_bwd_res(dh, x, g, dres, *, tm, name):
    t, d = x.shape

    def body(dh_ref, x_ref, g_ref, dres_ref, dx_ref, dg_ref):
        xhat, r = _rms_stats(x_ref[...])
        dx, dg = _rms_bwd(dh_ref[...], xhat, r, g_ref[...])
        dx_ref[...] = dres_ref[...] + dx

        @pl.when(pl.program_id(0) == 0)
        def _():
            dg_ref[...] = jnp.zeros_like(dg_ref)

        dg_ref[...] += dg

    row = pl.BlockSpec((tm, d), lambda i: (i, 0))
    vec = pl.BlockSpec((1, d), lambda i: (0, 0))
    return pl.pallas_call(
        body, name=name, grid=(t // tm,), in_specs=[row, row, vec, row], out_specs=[row, vec],
        out_shape=[jax.ShapeDtypeStruct((t, d), F32), jax.ShapeDtypeStruct((1, d), F32)],
        compiler_params=_params("arbitrary"),
    )(dh, x, g, dres)


def _ffn_fwd(x, g, wgT, wuT, wd, *, tm, hc, name, hosted=None):
    t, d = x.shape
    nj = DFF // hc

    def body(x_ref, g_ref, wg_ref, wu_ref, wd_ref, xo_ref, h_ref, a_ref, b_ref, acc_ref):
        j = pl.program_id(1)

        @pl.when(j == 0)
        def _():
            xhat, _ = _rms_stats(x_ref[...])
            h_ref[...] = (xhat * g_ref[...]).astype(BF16)
            acc_ref[...] = jnp.zeros_like(acc_ref)

        h = h_ref[...]
        a = _dot_nt(h, wg_ref[...])
        b = _dot_nt(h, wu_ref[...])
        a_ref[...] = a.astype(BF16)
        b_ref[...] = b.astype(BF16)
        s = (a * _sigmoid(a) * b).astype(BF16)
        acc_ref[...] += _dot_nn(s, wd_ref[...])

        @pl.when(j == nj - 1)
        def _():
            xo_ref[...] = x_ref[...] + 0.5 * acc_ref[...]

    row = pl.BlockSpec((tm, d), lambda i, j: (i, 0))
    vec = pl.BlockSpec((1, d), lambda i, j: (0, 0))
    wsp = pl.BlockSpec((hc, d), lambda i, j: (j, 0))
    hid = pl.BlockSpec((tm, hc), lambda i, j: (i, j))
    return _call(
        body, name=name, grid=(t // tm, nj), in_specs=[row, vec, wsp, wsp, wsp], out_specs=[row, row, hid, hid],
        out_shape=[jax.ShapeDtypeStruct((t, d), F32), jax.ShapeDtypeStruct((t, d), BF16),
                   jax.ShapeDtypeStruct((t, DFF), BF16), jax.ShapeDtypeStruct((t, DFF), BF16)],
        scratch_shapes=[pltpu.VMEM((tm, d), F32)], operands=(x, g, wgT, wuT, wd), sem=("parallel", "arbitrary"), hosted=hosted)


def _ffn_bwd(dout, x, g, a, b, wgT, wuT, wd, *, tm, hc, name, hosted=None):
    t, d = x.shape
    nj = DFF // hc

    def body(dout_ref, x_ref, g_ref, a_ref, b_ref, wg_ref, wu_ref, wd_ref,
             dx_ref, dy_ref, da_ref, db_ref, s_ref, dg_ref, acc_ref):
        i, j = pl.program_id(0), pl.program_id(1)

        @pl.when(j == 0)
        def _():
            dy_ref[...] = (0.5 * dout_ref[...]).astype(BF16)
            acc_ref[...] = jnp.zeros_like(acc_ref)

        @pl.when((i == 0) & (j == 0))
        def _():
            dg_ref[...] = jnp.zeros_like(dg_ref)

        ds = _dot_nt(dy_ref[...], wd_ref[...])
        av = a_ref[...].astype(F32)
        bv = b_ref[...].astype(F32)
        sg = _sigmoid(av)
        sl = av * sg
        s_ref[...] = (sl * bv).astype(BF16)
        da = (ds * bv * (sg * (1.0 + av * (1.0 - sg)))).astype(BF16)
        db = (ds * sl).astype(BF16)
        da_ref[...] = da
        db_ref[...] = db
        acc_ref[...] += _dot_nn(da, wg_ref[...]) + _dot_nn(db, wu_ref[...])

        @pl.when(j == nj - 1)
        def _():
            xhat, r = _rms_stats(x_ref[...])
            dx, dg = _rms_bwd(acc_ref[...], xhat, r, g_ref[...])
            dx_ref[...] = dout_ref[...] + dx
            dg_ref[...] += dg

    row = pl.BlockSpec((tm, d), lambda i, j: (i, 0))
    vec = pl.BlockSpec((1, d), lambda i, j: (0, 0))
    wsp = pl.BlockSpec((hc, d), lambda i, j: (j, 0))
    hid = pl.BlockSpec((tm, hc), lambda i, j: (i, j))
    hid_shape = jax.ShapeDtypeStruct((t, DFF), BF16)
    return _call(
        body, name=name, grid=(t // tm, nj), in_specs=[row, row, vec, hid, hid, wsp, wsp, wsp],
        out_specs=[row, row, hid, hid, hid, vec],
        out_shape=[jax.ShapeDtypeStruct((t, d), F32), jax.ShapeDtypeStruct((t, d), BF16), hid_shape, hid_shape, hid_shape,
                   jax.ShapeDtypeStruct((1, d), F32)],
        scratch_shapes=[pltpu.VMEM((tm, d), F32)], operands=(dout, x, g, a, b, wgT, wuT, wd), sem=("arbitrary", "arbitrary"),
        hosted=hosted)


def _rope_fwd(x, c, s1, s2):
    return x * c + pltpu.roll(x, HEAD_PAD - ROPE_HALF, 1) * s1 + pltpu.roll(x, ROPE_HALF, 1) * s2


def _rope_bwd(dy, c, s1, s2):
    return dy * c + pltpu.roll(dy * s1, ROPE_HALF, 1) + pltpu.roll(dy * s2, HEAD_PAD - ROPE_HALF, 1)


def _head_stats(x):
    r = lax.rsqrt(jnp.sum(x * x, axis=-1, keepdims=True) * (1.0 / QK_DIM) + NORM_EPS)
    return x * r, r


def _mla_prep_fwd(lat, gq, gkv, ghq, ghk, wq, wk, wv, rc, rs1, rs2, *, tm, name):
    t = lat.shape[0]

    def body(lat_ref, gq_ref, gkv_ref, ghq_ref, ghk_ref, wq_ref, wk_ref, wv_ref, c_ref, s1_ref, s2_ref,
             q_ref, k_ref, v_ref, qn_ref, ckv_ref):
        lat_v = lat_ref[...]
        qhat, _ = _rms_stats(lat_v[:, :Q_LORA].astype(F32))
        qn = (qhat * gq_ref[...]).astype(BF16)
        khat, _ = _rms_stats(lat_v[:, Q_LORA:Q_LORA + KV_LORA].astype(F32))
        ckv = (khat * gkv_ref[...]).astype(BF16)
        ckv_ext = jnp.concatenate([ckv, lat_v[:, Q_LORA + KV_LORA:]], axis=1)
        qn_ref[...] = qn
        ckv_ref[...] = ckv_ext
        q_pre = _dot_nn(qn, wq_ref[...])
        k_pre = _dot_nn(ckv_ext, wk_ref[...])
        v_ref[...] = _dot_nn(ckv, wv_ref[...]).astype(BF16)
        c, s1, s2 = c_ref[...], s1_ref[...], s2_ref[...]
        for h in range(N_HEADS):
            hs = slice(h * HEAD_PAD, (h + 1) * HEAD_PAD)
            xq, _ = _head_stats(q_pre[:, hs])
            q_ref[:, hs] = _rope_fwd(xq * ghq_ref[...], c, s1, s2).astype(BF16)
            xk, _ = _head_stats(k_pre[:, hs])
            k_ref[:, hs] = _rope_fwd(xk * ghk_ref[...], c, s1, s2).astype(BF16)

    def row(w):
        return pl.BlockSpec((tm, w), lambda i: (i, 0))

    def full(r, w):
        return pl.BlockSpec((r, w), lambda i: (0, 0))

    wide = jax.ShapeDtypeStruct((t, D), BF16)
    lat3 = jax.ShapeDtypeStruct((t, Q_LORA), BF16)
    return pl.pallas_call(
        body, name=name, grid=(t // tm,),
        in_specs=[row(LAT_PAD), full(1, Q_LORA), full(1, KV_LORA), full(1, HEAD_PAD), full(1, HEAD_PAD),
                  full(Q_LORA, D), full(Q_LORA, D), full(KV_LORA, D), row(HEAD_PAD), row(HEAD_PAD), row(HEAD_PAD)],
        out_specs=[row(D), row(D), row(D), row(Q_LORA), row(Q_LORA)],
        out_shape=[wide, wide, wide, lat3, lat3],
        compiler_params=_params("parallel"),
    )(lat, gq, gkv, ghq, ghk, wq, wk, wv, rc, rs1, rs2)


def _mla_prep_bwd(dq, dk, dv, lat, qn, ckv_ext, gq, gkv, ghq, ghk, wq, wk, wv, rc, rs1, rs2, *, tm, name):
    t = lat.shape[0]

    def body(dq_ref, dk_ref, dv_ref, lat_ref, qn_ref, ckv_ref, gq_ref, gkv_ref, ghq_ref, ghk_ref, wq_ref, wk_ref, wv_ref,
             c_ref, s1_ref, s2_ref, dlat_ref, dqp_ref, dkp_ref, dgq_ref, dgkv_ref, dghq_ref, dghk_ref):
        @pl.when(pl.program_id(0) == 0)
        def _():
            dgq_ref[...] = jnp.zeros_like(dgq_ref)
            dgkv_ref[...] = jnp.zeros_like(dgkv_ref)
            dghq_ref[...] = jnp.zeros_like(dghq_ref)
            dghk_ref[...] = jnp.zeros_like(dghk_ref)

        c, s1, s2 = c_ref[...], s1_ref[...], s2_ref[...]
        q_pre = _dot_nn(qn_ref[...], wq_ref[...])
        k_pre = _dot_nn(ckv_ref[...], wk_ref[...])

        def heads(pre, dy_ref, gh_ref, dgh_ref, out_ref):
            dgh = jnp.zeros((1, HEAD_PAD), F32)
            for h in range(N_HEADS):
                hs = slice(h * HEAD_PAD, (h + 1) * HEAD_PAD)
                d = _rope_bwd(dy_ref[:, hs], c, s1, s2)
                xhat, r = _head_stats(pre[:, hs])
                dgh = dgh + jnp.sum(d * xhat, axis=0, keepdims=True)
                dxh = d * gh_ref[...]
                dx = r * (dxh - xhat * (jnp.sum(dxh * xhat, axis=-1, keepdims=True) * (1.0 / QK_DIM)))
                out_ref[:, hs] = dx.astype(BF16)
            dgh_ref[...] += dgh

        heads(q_pre, dq_ref, ghq_ref, dghq_ref, dqp_ref)
        heads(k_pre, dk_ref, ghk_ref, dghk_ref, dkp_ref)
        dqn = _dot_nt(dqp_ref[...], wq_ref[...])
        dce = _dot_nt(dkp_ref[...], wk_ref[...])
        dckv = dce[:, :KV_LORA] + _dot_nt(dv_ref[...], wv_ref[...])
        lat_v = lat_ref[...]
        qhat, rq = _rms_stats(lat_v[:, :Q_LORA].astype(F32))
        dql, dgq = _rms_bwd(dqn, qhat, rq, gq_ref[...])
        khat, rk = _rms_stats(lat_v[:, Q_LORA:Q_LORA + KV_LORA].astype(F32))
        dkl, dgkv = _rms_bwd(dckv, khat, rk, gkv_ref[...])
        dgq_ref[...] += dgq
        dgkv_ref[...] += dgkv
        dlat_ref[...] = jnp.concatenate([dql, dkl, dce[:, KV_LORA:]], axis=1).astype(BF16)

    def row(w):
        return pl.BlockSpec((tm, w), lambda i: (i, 0))

    def full(r, w):
        return pl.BlockSpec((r, w), lambda i: (0, 0))

    return pl.pallas_call(
        body, name=name, grid=(t // tm,),
        in_specs=[row(D), row(D), row(D), row(LAT_PAD), row(Q_LORA), row(Q_LORA), full(1, Q_LORA), full(1, KV_LORA),
                  full(1, HEAD_PAD), full(1, HEAD_PAD), full(Q_LORA, D), full(Q_LORA, D), full(KV_LORA, D),
                  row(HEAD_PAD), row(HEAD_PAD), row(HEAD_PAD)],
        out_specs=[row(LAT_PAD), row(D), row(D), full(1, Q_LORA), full(1, KV_LORA), full(1, HEAD_PAD), full(1, HEAD_PAD)],
        out_shape=[jax.ShapeDtypeStruct((t, LAT_PAD), BF16), jax.ShapeDtypeStruct((t, D), BF16), jax.ShapeDtypeStruct((t, D), BF16),
                   jax.ShapeDtypeStruct((1, Q_LORA), F32), jax.ShapeDtypeStruct((1, KV_LORA), F32),
                   jax.ShapeDtypeStruct((1, HEAD_PAD), F32), jax.ShapeDtypeStruct((1, HEAD_PAD), F32)],
        compiler_params=_params("arbitrary"),
    )(dq, dk, dv, lat, qn, ckv_ext, gq, gkv, ghq, ghk, wq, wk, wv, rc, rs1, rs2)


def _causal_keep(tq):
    r = lax.broadcasted_iota(jnp.int32, (tq, tq), 0)
    c = lax.broadcasted_iota(jnp.int32, (tq, tq), 1)
    return c <= r


def _flash_fwd(q, k, v, *, n_seq, seq, tq, name, hosted=None):
    nq = seq // tq

    def body(q_ref, k_ref, v_ref, o_ref, lse_ref):
        qi = pl.program_id(2)
        qv = q_ref[...]

        def step(j, carry, masked):
            m, l, acc = carry
            kj = k_ref[pl.ds(pl.multiple_of(j * tq, tq), tq), :]
            vj = v_ref[pl.ds(pl.multiple_of(j * tq, tq), tq), :]
            s = _dot_nt(qv, kj) * ATTN_SCALE
            if masked:
                s = jnp.where(_causal_keep(tq), s, NEG)
            m_new = jnp.maximum(m, jnp.max(s, axis=-1, keepdims=True))
            alpha = jnp.exp(m - m_new)
            p = jnp.exp(s - m_new)
            l = alpha * l + jnp.sum(p, axis=-1, keepdims=True)
            acc = alpha * acc + _dot_nn(p.astype(BF16), vj)
            return m_new, l, acc

        init = (jnp.full((tq, 1), NEG, F32), jnp.zeros((tq, 1), F32), jnp.zeros((tq, HEAD_PAD), F32))
        carry = lax.fori_loop(0, qi, lambda j, cr: step(j, cr, False), init)
        m, l, acc = step(qi, carry, True)
        o_ref[...] = (acc / l).astype(BF16)
        lse_ref[...] = jnp.broadcast_to(m + jnp.log(l), (tq, HEAD_PAD))

    qspec = pl.BlockSpec((tq, HEAD_PAD), lambda b, h, i: (b * nq + i, h))
    kspec = pl.BlockSpec((seq, HEAD_PAD), lambda b, h, i: (b, h))
    t = n_seq * seq
    return _call(
        body, name=name, grid=(n_seq, N_HEADS, nq), in_specs=[qspec, kspec, kspec], out_specs=[qspec, qspec],
        out_shape=[jax.ShapeDtypeStruct((t, D), BF16), jax.ShapeDtypeStruct((t, D), F32)], scratch_shapes=[],
        operands=(q, k, v), sem=("parallel", "parallel", "arbitrary"), hosted=hosted)


def _flash_bwd(q, k, v, o, lse, do, *, n_seq, seq, tq, name, hosted=None):
    nq = seq // tq

    def body(q_ref, k_ref, v_ref, o_ref, lse_ref, do_ref, dq_ref, dk_ref, dv_ref, dk_acc, dv_acc):
        j = pl.program_id(2)

        @pl.when(j == 0)
        def _():
            dq_ref[...] = jnp.zeros_like(dq_ref)

        dk_acc[...] = jnp.zeros_like(dk_acc)
        dv_acc[...] = jnp.zeros_like(dv_acc)
        kv = k_ref[...]
        vv = v_ref[...]

        def step(i, masked):
            rows = pl.ds(pl.multiple_of(i * tq, tq), tq)
            qi = q_ref[rows, :]
            doi = do_ref[rows, :]
            delta = jnp.sum(doi.astype(F32) * o_ref[rows, :].astype(F32), axis=-1, keepdims=True)
            s = _dot_nt(qi, kv) * ATTN_SCALE
            p = jnp.exp(s - lse_ref[rows, :][:, :1])
            if masked:
                p = jnp.where(_causal_keep(tq), p, 0.0)
            dv_acc[...] += _dot_tn(p.astype(BF16), doi)
            dp = _dot_nt(doi, vv)
            ds = (p * (dp - delta) * ATTN_SCALE).astype(BF16)
            dk_acc[...] += _dot_tn(ds, qi)
            dq_ref[rows, :] += _dot_nn(ds, kv)

        step(j, True)

        def loop_body(i, carry):
            step(i, False)
            return carry

        lax.fori_loop(j + 1, nq, loop_body, 0)
        dk_ref[...] = dk_acc[...]
        dv_ref[...] = dv_acc[...].astype(BF16)

    full = pl.BlockSpec((seq, HEAD_PAD), lambda b, h, j: (b, h))
    tile = pl.BlockSpec((tq, HEAD_PAD), lambda b, h, j: (b * nq + j, h))
    t = n_seq * seq
    return _call(
        body, name=name, grid=(n_seq, N_HEADS, nq), in_specs=[full, tile, tile, full, full, full],
        out_specs=[full, tile, tile],
        out_shape=[jax.ShapeDtypeStruct((t, D), F32), jax.ShapeDtypeStruct((t, D), F32), jax.ShapeDtypeStruct((t, D), BF16)],
        scratch_shapes=[pltpu.VMEM((tq, HEAD_PAD), F32), pltpu.VMEM((tq, HEAD_PAD), F32)],
        operands=(q, k, v, o, lse, do), sem=("parallel", "parallel", "arbitrary"), hosted=hosted)


CONV_CB = 256


def _shift_down(u, k, row):
    return jnp.where(row >= k, pltpu.roll(u, k, 0), 0.0)


def _shift_up(u, k, row, n):
    return jnp.where(row < n - k, pltpu.roll(u, n - k, 0), 0.0)


def _conv_fwd(conv3, cw, *, n_seq, seq, name):
    def body(c_ref, w_ref, p_ref):
        blk = c_ref[...].astype(F32)
        xc, gb, gc = blk[:, :CONV_CB], blk[:, CONV_CB:2 * CONV_CB], blk[:, 2 * CONV_CB:]
        row = lax.broadcasted_iota(jnp.int32, (seq, CONV_CB), 0)
        u = gc * xc
        z = w_ref[0:1, :] * _shift_down(u, 2, row) + w_ref[1:2, :] * _shift_down(u, 1, row) + w_ref[2:3, :] * u
        p_ref[...] = (gb * z).astype(BF16)

    return pl.pallas_call(
        body, name=name, grid=(n_seq, D // CONV_CB),
        in_specs=[pl.BlockSpec((seq, 3 * CONV_CB), lambda b, j: (b, j)), pl.BlockSpec((3, CONV_CB), lambda b, j: (0, j))],
        out_specs=pl.BlockSpec((seq, CONV_CB), lambda b, j: (b, j)),
        out_shape=jax.ShapeDtypeStruct((n_seq * seq, D), BF16),
        compiler_params=_params("parallel", "parallel"),
    )(conv3, cw)


def _conv_bwd(dp, conv3, cw, *, n_seq, seq, name):
    def body(dp_ref, c_ref, w_ref, dc_ref, dw_ref):
        @pl.when(pl.program_id(1) == 0)
        def _():
            dw_ref[...] = jnp.zeros_like(dw_ref)

        blk = c_ref[...].astype(F32)
        xc, gb, gc = blk[:, :CONV_CB], blk[:, CONV_CB:2 * CONV_CB], blk[:, 2 * CONV_CB:]
        row = lax.broadcasted_iota(jnp.int32, (seq, CONV_CB), 0)
        w0, w1, w2 = w_ref[0:1, :], w_ref[1:2, :], w_ref[2:3, :]
        u = gc * xc
        u1 = _shift_down(u, 1, row)
        u2 = _shift_down(u, 2, row)
        z = w0 * u2 + w1 * u1 + w2 * u
        dpv = dp_ref[...].astype(F32)
        dz = dpv * gb
        du = w2 * dz + w1 * _shift_up(dz, 1, row, seq) + w0 * _shift_up(dz, 2, row, seq)
        dc_ref[...] = jnp.concatenate([du * gc, dpv * z, du * xc], axis=1).astype(BF16)
        dw_ref[0:1, :] += jnp.sum(dz * u2, axis=0, keepdims=True)
        dw_ref[1:2, :] += jnp.sum(dz * u1, axis=0, keepdims=True)
        dw_ref[2:3, :] += jnp.sum(dz * u, axis=0, keepdims=True)

    return pl.pallas_call(
        body, name=name, grid=(D // CONV_CB, n_seq),
        in_specs=[pl.BlockSpec((seq, CONV_CB), lambda j, b: (b, j)), pl.BlockSpec((seq, 3 * CONV_CB), lambda j, b: (b, j)),
                  pl.BlockSpec((3, CONV_CB), lambda j, b: (0, j))],
        out_specs=[pl.BlockSpec((seq, 3 * CONV_CB), lambda j, b: (b, j)), pl.BlockSpec((3, CONV_CB), lambda j, b: (0, j))],
        out_shape=[jax.ShapeDtypeStruct((n_seq * seq, CONV_COLS), BF16), jax.ShapeDtypeStruct((3, D), F32)],
        compiler_params=_params("parallel", "arbitrary"),
    )(dp, conv3, cw)


def _merge_fwd(o, p, gl, bias, x1, wpa, wpc, wout, *, tm, name):
    t = x1.shape[0]

    def body(o_ref, p_ref, gl_ref, b_ref, x_ref, wpa_ref, wpc_ref, wout_ref, x2_ref, mg_ref, ya_ref, yb_ref):
        ya = _dot_nn(o_ref[...], wpa_ref[...])
        yb = _dot_nn(p_ref[...], wpc_ref[...])
        gates = _sigmoid(gl_ref[...].astype(F32) + b_ref[...])
        merged = (gates[:, :D] * ya + gates[:, D:] * yb).astype(BF16)
        ya_ref[...] = ya.astype(BF16)
        yb_ref[...] = yb.astype(BF16)
        mg_ref[...] = merged
        x2_ref[...] = x_ref[...] + _dot_nn(merged, wout_ref[...])

    row = pl.BlockSpec((tm, D), lambda i: (i, 0))
    row2 = pl.BlockSpec((tm, GATE_COLS), lambda i: (i, 0))
    wsp = pl.BlockSpec((D, D), lambda i: (0, 0))
    wide = jax.ShapeDtypeStruct((t, D), BF16)
    return pl.pallas_call(
        body, name=name, grid=(t // tm,),
        in_specs=[row, row, row2, pl.BlockSpec((1, GATE_COLS), lambda i: (0, 0)), row, wsp, wsp, wsp],
        out_specs=[row, row, row, row], out_shape=[jax.ShapeDtypeStruct((t, D), F32), wide, wide, wide],
        compiler_params=_params("parallel"),
    )(o, p, gl, bias, x1, wpa, wpc, wout)


def _merge_bwd(dx2, ya, yb, gl, bias, wpa, wpc, wout, *, tm, name):
    t = dx2.shape[0]

    def body(dx_ref, ya_ref, yb_ref, gl_ref, b_ref, wpa_ref, wpc_ref, wout_ref,
             dxb_ref, dya_ref, dyb_ref, dgl_ref, do_ref, dp_ref, db_ref):
        @pl.when(pl.program_id(0) == 0)
        def _():
            db_ref[...] = jnp.zeros_like(db_ref)

        dxb = dx_ref[...].astype(BF16)
        dxb_ref[...] = dxb
        dm = _dot_nt(dxb, wout_ref[...])
        gates = _sigmoid(gl_ref[...].astype(F32) + b_ref[...])
        ga, gb = gates[:, :D], gates[:, D:]
        dya = (dm * ga).astype(BF16)
        dyb = (dm * gb).astype(BF16)
        dya_ref[...] = dya
        dyb_ref[...] = dyb
        dgl = jnp.concatenate([dm * ya_ref[...].astype(F32) * ga * (1.0 - ga),
                               dm * yb_ref[...].astype(F32) * gb * (1.0 - gb)], axis=1)
        dgl_ref[...] = dgl.astype(BF16)
        db_ref[...] += jnp.sum(dgl, axis=0, keepdims=True)
        do_ref[...] = _dot_nt(dya, wpa_ref[...]).astype(BF16)
        dp_ref[...] = _dot_nt(dyb, wpc_ref[...]).astype(BF16)

    row = pl.BlockSpec((tm, D), lambda i: (i, 0))
    row2 = pl.BlockSpec((tm, GATE_COLS), lambda i: (i, 0))
    vec2 = pl.BlockSpec((1, GATE_COLS), lambda i: (0, 0))
    wsp = pl.BlockSpec((D, D), lambda i: (0, 0))
    wide = jax.ShapeDtypeStruct((t, D), BF16)
    return pl.pallas_call(
        body, name=name, grid=(t // tm,), in_specs=[row, row, row, row2, vec2, wsp, wsp, wsp],
        out_specs=[row, row, row, row2, row, row, vec2],
        out_shape=[wide, wide, wide, jax.ShapeDtypeStruct((t, GATE_COLS), BF16), wide, wide,
                   jax.ShapeDtypeStruct((1, GATE_COLS), F32)],
        compiler_params=_params("arbitrary"),
    )(dx2, ya, yb, gl, bias, wpa, wpc, wout)


def _loss_head(y, target, *, tm, name):
    t, d = y.shape

    def body(y_ref, t_ref, dy_ref, loss_ref):
        @pl.when(pl.program_id(0) == 0)
        def _():
            loss_ref[...] = jnp.zeros_like(loss_ref)

        err = y_ref[...] - t_ref[...]
        dy_ref[...] = err * (1.0 / d)
        loss_ref[...] += jnp.sum(jnp.sum(err * err, axis=-1, keepdims=True), axis=0, keepdims=True) * (0.5 / d)

    row = pl.BlockSpec((tm, d), lambda i: (i, 0))
    return pl.pallas_call(
        body, name=name, grid=(t // tm,), in_specs=[row, row], out_specs=[row, pl.BlockSpec((1, 128), lambda i: (0, 0))],
        out_shape=[jax.ShapeDtypeStruct((t, d), F32), jax.ShapeDtypeStruct((1, 128), F32)],
        compiler_params=_params("arbitrary"),
    )(y, target)


def _adamw(w, g, m, v, *, name):
    rows, cols = w.shape
    tr = max([c for c in range(8, 513, 8) if rows % c == 0], default=rows)
    c1 = 1.0 / (1.0 - ADAM_B1 ** ADAM_STEP)
    c2 = 1.0 / (1.0 - ADAM_B2 ** ADAM_STEP)

    def body(w_ref, g_ref, m_ref, v_ref, d_ref, nm_ref, nv_ref):
        gv = g_ref[...]
        nm = ADAM_B1 * m_ref[...] + (1.0 - ADAM_B1) * gv
        nv = ADAM_B2 * v_ref[...] + (1.0 - ADAM_B2) * (gv * gv)
        nm_ref[...] = nm
        nv_ref[...] = nv
        d_ref[...] = -ADAM_LR * ((nm * c1) / (jnp.sqrt(nv * c2) + ADAM_EPS) + ADAM_WD * w_ref[...])

    spec = pl.BlockSpec((tr, cols), lambda i: (i, 0))
    shp = jax.ShapeDtypeStruct((rows, cols), F32)
    return pl.pallas_call(
        body, name=name, grid=(rows // tr,), in_specs=[spec] * 4, out_specs=[spec] * 3, out_shape=[shp] * 3,
        compiler_params=_params("parallel"),
    )(w, g, m, v)


def _place():
    return lax.axis_index("x"), lax.axis_index("y"), lax.axis_index("c")


def _other_chips(x, y):
    return [(1 - x, y), (x, 1 - y), (1 - x, 1 - y)]


def _remote(src, dst, send, recv, dev):
    return pltpu.make_async_remote_copy(src_ref=src, dst_ref=dst, send_sem=send, recv_sem=recv, device_id=dev, device_id_type=MESH)


def _gather_chips_plan(n):
    def start(srcs, dsts, send, recv, local):
        x, y, cc = _place()
        me = 4 * x + 2 * y + cc
        for a in range(n):
            pltpu.make_async_copy(srcs[a], dsts[a].at[me], local.at[a]).start()
            for k, (px, py) in enumerate(_other_chips(x, y)):
                _remote(srcs[a], dsts[a].at[me], send.at[3 * a + k], recv.at[3 * a + k], (px, py, cc)).start()

    def wait(srcs, dsts, send, recv, local):
        x, y, cc = _place()
        me = 4 * x + 2 * y + cc
        for a in range(n):
            for k, (px, py) in enumerate(_other_chips(x, y)):
                _remote(srcs[a], dsts[a].at[4 * px + 2 * py + cc], send.at[3 * a + k], recv.at[3 * a + k], (px, py, cc)).wait_recv()
        for a in range(n):
            for k, (px, py) in enumerate(_other_chips(x, y)):
                _remote(srcs[a], dsts[a].at[me], send.at[3 * a + k], recv.at[3 * a + k], (px, py, cc)).wait_send()
            pltpu.make_async_copy(srcs[a], dsts[a].at[me], local.at[a]).wait()

    return _Plan(start, wait, 3 * n, n)


def _scatter_chips_plan(n):
    def start(srcs, dsts, send, recv, local):
        x, y, cc = _place()
        for a in range(n):
            for k, (px, py) in enumerate(_other_chips(x, y)):
                _remote(srcs[a].at[2 * px + py], dsts[a].at[k], send.at[3 * a + k], recv.at[3 * a + k], (px, py, cc)).start()

    def wait(srcs, dsts, send, recv, local):
        x, y, cc = _place()
        for a in range(n):
            for k, (px, py) in enumerate(_other_chips(x, y)):
                _remote(srcs[a].at[k], dsts[a].at[k], send.at[3 * a + k], recv.at[3 * a + k], (px, py, cc)).wait_recv()
        for a in range(n):
            for k, (px, py) in enumerate(_other_chips(x, y)):
                _remote(srcs[a].at[k], dsts[a].at[k], send.at[3 * a + k], recv.at[3 * a + k], (px, py, cc)).wait_send()

    return _Plan(start, wait, 3 * n, 0)


def _gather_shapes(blocks):
    return [jax.ShapeDtypeStruct((N_DEV,) + b.shape, b.dtype) for b in blocks]


def _scatter_shapes(parts):
    return [jax.ShapeDtypeStruct((3,) + p.shape[1:], p.dtype) for p in parts]


def _run_plan(plan, srcs, out_shapes, *, name):
    n_in, n_out = len(srcs), len(out_shapes)

    def body(*refs):
        h_in, h_out, sems = refs[:n_in], refs[n_in:n_in + n_out], refs[n_in + n_out:]
        plan.start(h_in, h_out, *sems)
        plan.wait(h_in, h_out, *sems)

    return pl.pallas_call(body, name=name, in_specs=[ANY] * n_in, out_specs=[ANY] * n_out, out_shape=list(out_shapes),
                          scratch_shapes=plan.sems())(*srcs)


def _gather_sibling(gs, *, name):
    n = len(gs)

    def body(*refs):
        g_in, g_out, (send, recv) = refs[:n], refs[n:2 * n], refs[2 * n:]
        x, y, cc = _place()
        sib = (x, y, 1 - cc)
        for a in range(n):
            for q in range(4):
                _remote(g_in[a].at[2 * q + cc], g_out[a].at[2 * q + cc], send.at[4 * a + q], recv.at[4 * a + q], sib).start()
        for a in range(n):
            for q in range(4):
                _remote(g_in[a].at[2 * q + cc], g_out[a].at[2 * q + 1 - cc], send.at[4 * a + q], recv.at[4 * a + q], sib).wait_recv()
        for a in range(n):
            for q in range(4):
                _remote(g_in[a].at[2 * q + cc], g_out[a].at[2 * q + cc], send.at[4 * a + q], recv.at[4 * a + q], sib).wait_send()

    return pl.pallas_call(
        body, name=name, in_specs=[ANY] * n, out_specs=[ANY] * n, out_shape=[jax.ShapeDtypeStruct(g.shape, g.dtype) for g in gs],
        input_output_aliases={a: a for a in range(n)},
        scratch_shapes=[pltpu.SemaphoreType.DMA((4 * n,)), pltpu.SemaphoreType.DMA((4 * n,))],
    )(*gs)


def _scatter_sibling(ps, *, name):
    n = len(ps)

    def body(*refs):
        p_in, q_out, (send, recv) = refs[:n], refs[n:2 * n], refs[2 * n:]
        x, y, cc = _place()
        sib = (x, y, 1 - cc)
        for a in range(n):
            for q in range(4):
                _remote(p_in[a].at[2 * q + 1 - cc], q_out[a].at[q], send.at[4 * a + q], recv.at[4 * a + q], sib).start()
        for a in range(n):
            for q in range(4):
                _remote(p_in[a].at[q], q_out[a].at[q], send.at[4 * a + q], recv.at[4 * a + q], sib).wait_recv()
        for a in range(n):
            for q in range(4):
                _remote(p_in[a].at[q], q_out[a].at[q], send.at[4 * a + q], recv.at[4 * a + q], sib).wait_send()

    return pl.pallas_call(
        body, name=name, in_specs=[ANY] * n, out_specs=[ANY] * n,
        out_shape=[jax.ShapeDtypeStruct((4,) + p.shape[1:], p.dtype) for p in ps],
        scratch_shapes=[pltpu.SemaphoreType.DMA((4 * n,)), pltpu.SemaphoreType.DMA((4 * n,))],
    )(*ps)


def _sum_sibling(p, q, core, *, name):
    _, r, c = p.shape

    def body(core_ref, p_ref, q_ref, o_ref):
        o_ref[...] = (p_ref[...].astype(F32) + q_ref[...].astype(F32)).astype(BF16)

    grid_spec = pltpu.PrefetchScalarGridSpec(
        num_scalar_prefetch=1, grid=(4,),
        in_specs=[pl.BlockSpec((1, r, c), lambda ch, core_ref: (2 * ch + core_ref[0], 0, 0)),
                  pl.BlockSpec((1, r, c), lambda ch, core_ref: (ch, 0, 0))],
        out_specs=pl.BlockSpec((1, r, c), lambda ch, core_ref: (ch, 0, 0)))
    return pl.pallas_call(
        body, name=name, grid_spec=grid_spec, out_shape=jax.ShapeDtypeStruct((4, r, c), BF16),
        compiler_params=_params("parallel"),
    )(core, p, q)


def _sum_chips(s1, r2, chip, *, name):
    _, r, c = s1.shape

    def body(chip_ref, s_ref, r_ref, o_ref):
        acc = s_ref[0].astype(F32)
        for k in range(3):
            acc = acc + r_ref[k].astype(F32)
        o_ref[...] = acc

    grid_spec = pltpu.PrefetchScalarGridSpec(
        num_scalar_prefetch=1, grid=(1,),
        in_specs=[pl.BlockSpec((1, r, c), lambda i, chip_ref: (chip_ref[0], 0, 0)),
                  pl.BlockSpec((3, r, c), lambda i, chip_ref: (0, 0, 0))],
        out_specs=pl.BlockSpec((r, c), lambda i, chip_ref: (0, 0)))
    return pl.pallas_call(
        body, name=name, grid_spec=grid_spec, out_shape=jax.ShapeDtypeStruct((r, c), F32),
        compiler_params=_params("arbitrary"),
    )(chip, s1, r2)


def _sum_adamw(s1, r2, chip, w, m, v, *, name):
    _, r, c = s1.shape
    c1 = 1.0 / (1.0 - ADAM_B1 ** ADAM_STEP)
    c2 = 1.0 / (1.0 - ADAM_B2 ** ADAM_STEP)

    def body(chip_ref, s_ref, r_ref, w_ref, m_ref, v_ref, g_ref, d_ref, nm_ref, nv_ref):
        gv = s_ref[0].astype(F32)
        for k in range(3):
            gv = gv + r_ref[k].astype(F32)
        g_ref[...] = gv
        nm = ADAM_B1 * m_ref[...] + (1.0 - ADAM_B1) * gv
        nv = ADAM_B2 * v_ref[...] + (1.0 - ADAM_B2) * (gv * gv)
        nm_ref[...] = nm
        nv_ref[...] = nv
        d_ref[...] = -ADAM_LR * ((nm * c1) / (jnp.sqrt(nv * c2) + ADAM_EPS) + ADAM_WD * w_ref[...])

    flat = pl.BlockSpec((r, c), lambda i, chip_ref: (0, 0))
    grid_spec = pltpu.PrefetchScalarGridSpec(
        num_scalar_prefetch=1, grid=(1,),
        in_specs=[pl.BlockSpec((1, r, c), lambda i, chip_ref: (chip_ref[0], 0, 0)),
                  pl.BlockSpec((3, r, c), lambda i, chip_ref: (0, 0, 0)), flat, flat, flat],
        out_specs=[flat] * 4)
    return pl.pallas_call(
        body, name=name, grid_spec=grid_spec, out_shape=[jax.ShapeDtypeStruct((r, c), F32)] * 4,
        compiler_params=_params("arbitrary"),
    )(chip, s1, r2, w, m, v)


def _small_exchange(v, *, reduce, name):
    r, c = v.shape

    def body(x_ref, o_ref, *rest):
        if reduce:
            buf_ref, send_sems, recv_sems = rest
        else:
            buf_ref = o_ref
            send_sems, recv_sems = rest
        x, y, cc = _place()
        me = 4 * x + 2 * y + cc

        def peer(k):
            return ((1 - x) if k & 4 else x, (1 - y) if k & 2 else y, (1 - cc) if k & 1 else cc)

        buf_ref[me] = x_ref[...]
        sends = []
        for k in range(1, N_DEV):
            cp = pltpu.make_async_remote_copy(src_ref=x_ref, dst_ref=buf_ref.at[me], send_sem=send_sems.at[k - 1],
                                              recv_sem=recv_sems.at[k - 1], device_id=peer(k), device_id_type=MESH)
            cp.start()
            sends.append(cp)
        for k in range(1, N_DEV):
            px, py, pc = peer(k)
            pltpu.make_async_remote_copy(src_ref=x_ref, dst_ref=buf_ref.at[4 * px + 2 * py + pc], send_sem=send_sems.at[k - 1],
                                         recv_sem=recv_sems.at[k - 1], device_id=peer(k), device_id_type=MESH).wait_recv()
        for cp in sends:
            cp.wait_send()
        if reduce:
            acc = buf_ref[0]
            for s in range(1, N_DEV):
                acc = acc + buf_ref[s]
            o_ref[...] = acc

    vm = pl.BlockSpec(memory_space=pltpu.VMEM)
    sems = [pltpu.SemaphoreType.DMA((N_DEV - 1,)), pltpu.SemaphoreType.DMA((N_DEV - 1,))]
    if reduce:
        out_shape, scratch = jax.ShapeDtypeStruct((r, c), F32), [pltpu.VMEM((N_DEV, r, c), F32)] + sems
    else:
        out_shape, scratch = jax.ShapeDtypeStruct((N_DEV, r, c), F32), sems
    return pl.pallas_call(body, name=name, in_specs=[vm], out_specs=vm, out_shape=out_shape, scratch_shapes=scratch)(v)


def _rows(a):
    return a.reshape(-1, D)


def _pad_cols(a, to):
    return jnp.pad(a, ((0, 0), (0, to - a.shape[1])))


def _pack_weights(w):
    parts = {
        "w_inT": jnp.pad(w["w_in"].T, ((0, IN_SHARD_PAD - IN_SHARD), (0, 0))),
        "w_uq": _rows(_pad_cols(w["w_uq"], HEAD_PAD)), "w_uk": _rows(_pad_cols(w["w_uk"], HEAD_PAD)),
        "w_uv": _rows(_pad_cols(w["w_uv"], HEAD_PAD)), "w_pa": _rows(w["w_proj_attn"]),
        "w_pc": w["w_proj_conv"], "w_out": w["w_out"],
    }
    return jnp.concatenate([parts[n].astype(BF16) for n, _ in PACK], axis=0)


def _cols_from_shards(g, name, rows):
    off, r = PACK_OFF[name]
    return g[:, off:off + r].reshape(N_DEV, rows, HEAD_PAD).transpose(1, 0, 2).reshape(rows, N_DEV * HEAD_PAD)


def _rows_from_shards(g, name, keep=None):
    off, r = PACK_OFF[name]
    keep = r if keep is None else keep
    return g[:, off:off + keep].reshape(N_DEV * keep, D)


def _rope_placement():
    i = lax.broadcasted_iota(jnp.int32, (HEAD_PAD, D), 0)
    j = lax.broadcasted_iota(jnp.int32, (HEAD_PAD, D), 1)
    return ((i < 2 * ROPE_HALF) & (j % HEAD_PAD == NOPE + i)).astype(BF16)


def _unpack_weights(g):
    w_inT = _rows_from_shards(g, "w_inT", IN_SHARD)
    lat_rows = Q_LORA + KV_LORA + 2 * ROPE_HALF
    conv = w_inT[lat_rows:lat_rows + CONV_COLS].reshape(3, D // CONV_CB, CONV_CB, D).transpose(1, 0, 2, 3).reshape(CONV_COLS, D)
    wpa = _cols_from_shards(g, "w_pa", 512).reshape(N_HEADS, NOPE, D)
    return {
        "latT": jnp.pad(w_inT[:lat_rows], ((0, LAT_PAD - lat_rows), (0, 0))),
        "convT": conv, "gateT": w_inT[lat_rows + CONV_COLS:],
        "wq": _cols_from_shards(g, "w_uq", Q_LORA),
        "wk": jnp.concatenate([_cols_from_shards(g, "w_uk", KV_LORA), _rope_placement()], axis=0),
        "wv": _cols_from_shards(g, "w_uv", KV_LORA),
        "wpa": jnp.pad(wpa, ((0, 0), (0, HEAD_PAD - NOPE), (0, 0))).reshape(D, D),
        "wpc": _rows_from_shards(g, "w_pc"), "wout": _rows_from_shards(g, "w_out"),
    }


def _shards_from_cols(a):
    rows = a.shape[0]
    return a.reshape(rows, N_DEV, HEAD_PAD).transpose(1, 0, 2).reshape(N_DEV, rows * HEAD_PAD // D, D)


def _pack_grads(gw):
    lat_rows = Q_LORA + KV_LORA + 2 * ROPE_HALF
    conv = gw["convT"].reshape(D // CONV_CB, 3, CONV_CB, D).transpose(1, 0, 2, 3).reshape(CONV_COLS, D)
    w_inT = jnp.concatenate([gw["latT"][:lat_rows], conv, gw["gateT"]], axis=0).reshape(N_DEV, IN_SHARD, D)
    wpa = gw["wpa"].reshape(N_HEADS, HEAD_PAD, D)[:, :NOPE].reshape(N_HEADS * NOPE, D)
    parts = {}
    parts.update({
        "w_inT": jnp.pad(w_inT, ((0, 0), (0, IN_SHARD_PAD - IN_SHARD), (0, 0))),
        "w_uq": _shards_from_cols(gw["wq"]), "w_uk": _shards_from_cols(gw["wk"][:KV_LORA]),
        "w_uv": _shards_from_cols(gw["wv"][:KV_LORA]), "w_pa": _shards_from_cols(wpa),
        "w_pc": gw["wpc"].reshape(N_DEV, D // N_DEV, D), "w_out": gw["wout"].reshape(N_DEV, D // N_DEV, D),
    })
    return jnp.concatenate([parts[n] for n, _ in PACK], axis=1)


def _unpack_grads(mine):
    def seg(name, keep=None):
        off, r = PACK_OFF[name]
        return mine[off:off + (r if keep is None else keep)]

    return {
        "w_in": seg("w_inT", IN_SHARD).T,
        "w_uq": seg("w_uq").reshape(Q_LORA, HEAD_PAD)[:, :QK_DIM],
        "w_uk": seg("w_uk").reshape(KV_LORA, HEAD_PAD)[:, :NOPE],
        "w_uv": seg("w_uv").reshape(KV_LORA, HEAD_PAD)[:, :NOPE],
        "w_proj_attn": seg("w_pa").reshape(512, HEAD_PAD),
        "w_proj_conv": seg("w_pc"), "w_out": seg("w_out"),
    }


def _rope_tables(positions):
    inv_freq = 1.0 / (ROPE_THETA ** (jnp.arange(ROPE_HALF, dtype=F32) / ROPE_HALF))
    ang = positions.reshape(-1).astype(F32)[:, None] * inv_freq
    cos, sin = jnp.cos(ang), jnp.sin(ang)
    t = ang.shape[0]
    zero = jnp.zeros((t, ROPE_HALF), F32)
    head = jnp.ones((t, NOPE), F32)
    tail = jnp.zeros((t, HEAD_PAD - QK_DIM), F32)
    nohead = jnp.zeros((t, NOPE), F32)
    rc = jnp.concatenate([head, cos, cos, tail], axis=1)
    rs1 = jnp.concatenate([nohead, -sin, zero, tail], axis=1)
    rs2 = jnp.concatenate([nohead, zero, sin, tail], axis=1)
    return rc, rs1, rs2


def _local_step(x, positions, target, conv_w, small, ex):
    n_seq, seq, d = x.shape
    t = n_seq * seq
    x0 = x.reshape(t, d)
    tgt = target.reshape(t, d)
    rc, rs1, rs2 = _rope_tables(positions)
    ghq = _pad_cols(small["q_head_norm"], HEAD_PAD)
    ghk = _pad_cols(small["k_head_norm"], HEAD_PAD)
    TM, HC, TQ = 1024, 256, 512

    f1g, f1u, f1d = ex.ffn1_weights()
    (x1, h1, a1, b1), got = _ffn_fwd(x0, small["ffn1_norm"], f1g, f1u, f1d, tm=TM, hc=HC, name="ffn1_fwd", hosted=ex.host_mix_weights())
    W = ex.mix_weights(got)
    hm = _rms_fwd(x1, small["mix_norm"], tm=TM, name="mix_norm_fwd")
    lat = _mm(hm, W["latT"], mode="nt", out_dtype=BF16, tm=TM, tn=LAT_PAD, tk=D, name="proj_lat")
    conv3 = _mm(hm, W["convT"], mode="nt", out_dtype=BF16, tm=TM, tn=CONV_COLS // 2, tk=D, name="proj_conv")
    gl = _mm(hm, W["gateT"], mode="nt", out_dtype=BF16, tm=TM, tn=GATE_COLS // 2, tk=D, name="proj_gate")
    q, k, v, qn, ckv = _mla_prep_fwd(lat, small["q_a_norm"], small["kv_a_norm"], ghq, ghk, W["wq"], W["wk"], W["wv"], rc, rs1, rs2,
                                     tm=512, name="mla_prep_fwd")
    (o, lse), got = _flash_fwd(q, k, v, n_seq=n_seq, seq=seq, tq=TQ, name="attn_fwd", hosted=ex.host_ffn2_weights())
    f2g, f2u, f2d = ex.ffn2_weights(got)
    p = _conv_fwd(conv3, conv_w, n_seq=n_seq, seq=seq, name="conv_fwd")
    x2, merged, ya, yb = _merge_fwd(o, p, gl, small["gate_bias"], x1, W["wpa"], W["wpc"], W["wout"], tm=512, name="merge_fwd")
    (y, h2, a2, b2), _ = _ffn_fwd(x2, small["ffn2_norm"], f2g, f2u, f2d, tm=TM, hc=HC, name="ffn2_fwd")
    dy, loss_row = _loss_head(y, tgt, tm=TM, name="loss_head")

    gw, gs = {}, {}

    def wgrad(a, b, name, tm=None):
        m = a.shape[1]
        return _mm(a, b, mode="tn", out_dtype=BF16, tm=tm or m, tn=b.shape[1], tk=512, name=name)

    (dx2, dyb2, da2, db2, s2, gs["ffn2_norm"]), _ = _ffn_bwd(dy, x2, small["ffn2_norm"], a2, b2, f2g, f2u, f2d,
                                                             tm=TM, hc=HC, name="ffn2_bwd")
    ffn2_grads = [wgrad(da2, h2, "ffn2_dwg", tm=DFF // 2), wgrad(db2, h2, "ffn2_dwu", tm=DFF // 2),
                  wgrad(s2, dyb2, "ffn2_dwd", tm=DFF // 2)]

    dx2b, dya, dyb, dgl, do, dp, gs["gate_bias"] = _merge_bwd(dx2, ya, yb, gl, small["gate_bias"], W["wpa"], W["wpc"], W["wout"],
                                                               tm=512, name="merge_bwd")
    gw["wout"] = wgrad(merged, dx2b, "dw_out")
    gw["wpa"] = wgrad(o, dya, "dw_pa")
    gw["wpc"] = wgrad(p, dyb, "dw_pc")
    dconv3, dconv_w = _conv_bwd(dp, conv3, conv_w, n_seq=n_seq, seq=seq, name="conv_bwd")
    (dq, dk, dv), got = _flash_bwd(q, k, v, o, lse, do, n_seq=n_seq, seq=seq, tq=TQ, name="attn_bwd",
                                   hosted=ex.host_ffn2_grads(ffn2_grads))
    ex.ffn2_grads_done(got)
    dlat, dqp, dkp, gs["q_a_norm"], gs["kv_a_norm"], dghq, dghk = _mla_prep_bwd(
        dq, dk, dv, lat, qn, ckv, small["q_a_norm"], small["kv_a_norm"], ghq, ghk, W["wq"], W["wk"], W["wv"], rc, rs1, rs2,
        tm=512, name="mla_prep_bwd")
    gs["q_head_norm"], gs["k_head_norm"] = dghq[:, :QK_DIM], dghk[:, :QK_DIM]
    gw["wq"] = wgrad(qn, dqp, "dw_uq")
    gw["wk"] = wgrad(ckv, dkp, "dw_uk")
    gw["wv"] = wgrad(ckv, dv, "dw_uv")
    gw["latT"] = wgrad(dlat, hm, "dw_lat")
    gw["convT"] = wgrad(dconv3, hm, "dw_conv", tm=CONV_COLS // 2)
    gw["gateT"] = wgrad(dgl, hm, "dw_gate")
    dh = _mm(dlat, W["latT"], mode="nn", out_dtype=F32, tm=TM, tn=D, tk=LAT_PAD, name="dh_lat")
    dh = _mm(dconv3, W["convT"], mode="nn", out_dtype=F32, tm=TM, tn=D, tk=1024, name="dh_conv", add=dh)
    dh = _mm(dgl, W["gateT"], mode="nn", out_dtype=F32, tm=TM, tn=D, tk=1024, name="dh_gate", add=dh)
    dx1, gs["mix_norm"] = _rms_bwd_res(dh, x1, small["mix_norm"], dx2, tm=TM, name="mix_norm_bwd")

    (dx0, dyb1, da1, db1, s1, gs["ffn1_norm"]), got = _ffn_bwd(dx1, x0, small["ffn1_norm"], a1, b1, f1g, f1u, f1d,
                                                               tm=TM, hc=HC, name="ffn1_bwd", hosted=ex.host_mix_grads(gw))
    ex.mix_grads_done(got)
    ex.ffn1_grads([wgrad(da1, h1, "ffn1_dwg", tm=DFF // 2), wgrad(db1, h1, "ffn1_dwu", tm=DFF // 2),
                   wgrad(s1, dyb1, "ffn1_dwd", tm=DFF // 2)])
    return loss_row, dx0.reshape(n_seq, seq, d), dconv_w, gs


class _MeshExchange:
    def __init__(self, w, core, chip):
        self.w, self.core, self.chip = w, core, chip
        self.partial, self.received = {}, {}

    def _ffn_blocks(self, pre):
        w = self.w
        return [w[pre + "_w_gate"].T.astype(BF16), w[pre + "_w_up"].T.astype(BF16), w[pre + "_w_down"].astype(BF16)]

    def ffn1_weights(self):
        blocks = self._ffn_blocks("ffn1")
        g = _run_plan(_gather_chips_plan(3), blocks, _gather_shapes(blocks), name="gather_ffn1_chips")
        return [a.reshape(DFF, D) for a in _gather_sibling(list(g), name="gather_ffn1_sibling")]

    def host_mix_weights(self):
        blocks = [_pack_weights(self.w)]
        return _gather_chips_plan(1), blocks, _gather_shapes(blocks)

    def mix_weights(self, got):
        (g,) = _gather_sibling(list(got), name="gather_mix_sibling")
        return _unpack_weights(g)

    def host_ffn2_weights(self):
        blocks = self._ffn_blocks("ffn2")
        return _gather_chips_plan(3), blocks, _gather_shapes(blocks)

    def ffn2_weights(self, got):
        return [a.reshape(DFF, D) for a in _gather_sibling(list(got), name="gather_ffn2_sibling")]

    def _to_sibling(self, grads, tag):
        parts = [g.reshape(N_DEV, -1, D) for g in grads]
        qs = _scatter_sibling(parts, name="scatter_%s_sibling" % tag)
        self.partial[tag] = [_sum_sibling(p, q, self.core, name="sum_%s_sibling_%d" % (tag, i))
                             for i, (p, q) in enumerate(zip(parts, qs))]
        return self.partial[tag]

    def host_ffn2_grads(self, grads):
        s1 = self._to_sibling(grads, "ffn2")
        return _scatter_chips_plan(3), s1, _scatter_shapes(s1)

    def ffn2_grads_done(self, got):
        self.received["ffn2"] = list(got)

    def host_mix_grads(self, gw):
        s1 = self._to_sibling([_pack_grads(gw)], "mix")
        return _scatter_chips_plan(1), s1, _scatter_shapes(s1)

    def mix_grads_done(self, got):
        self.received["mix"] = list(got)

    def ffn1_grads(self, grads):
        s1 = self._to_sibling(grads, "ffn1")
        self.received["ffn1"] = list(_run_plan(_scatter_chips_plan(3), s1, _scatter_shapes(s1), name="scatter_ffn1_chips"))


SMALL_NAMES = ("ffn1_norm", "mix_norm", "gate_bias", "q_a_norm", "kv_a_norm", "q_head_norm", "k_head_norm", "ffn2_norm")
SMALL_SLOTS = {"ffn1_norm": 1024, "mix_norm": 1024, "gate_bias": 2048, "q_a_norm": 384, "kv_a_norm": 256, "q_head_norm": 128,
               "k_head_norm": 128, "ffn2_norm": 1024, "conv_w": 3072, "loss": 128}
COLUMN_MAJOR = ("w_in", "w_uq", "w_uk", "w_uv")
WEIGHT_NAMES = ("ffn1_norm", "ffn1_w_gate", "ffn1_w_up", "ffn1_w_down", "mix_norm", "w_in", "gate_bias", "q_a_norm", "w_uq",
                "kv_a_norm", "w_uk", "w_uv", "q_head_norm", "k_head_norm", "w_proj_attn", "conv_w", "w_proj_conv", "w_out",
                "ffn2_norm", "ffn2_w_gate", "ffn2_w_up", "ffn2_w_down")


def _step(x, positions, loss_target, w, m, v):
    xi, yi, ci = _place()
    core = ci.astype(jnp.int32).reshape(1)
    chip = (2 * xi + yi).astype(jnp.int32).reshape(1)
    me = 4 * xi + 2 * yi + ci

    cw_all = _small_exchange(jnp.pad(w["conv_w"], ((0, 5), (0, 0))), reduce=False, name="gather_conv_w")
    conv_w = cw_all[:, :3].transpose(1, 0, 2).reshape(3, D)
    small = {n: w[n].reshape(1, -1) for n in SMALL_NAMES}
    ex = _MeshExchange(w, core, chip)

    loss_row, grad_x, dconv_w, gs = _local_step(x, positions, loss_target, conv_w, small, ex)

    grads, deltas, new_m, new_v = {}, {}, {}, {}
    for pre in ("ffn1", "ffn2"):
        for i, (n, transposed) in enumerate(((pre + "_w_gate", True), (pre + "_w_up", True), (pre + "_w_down", False))):
            wv, mv, vv = (a[n].T if transposed else a[n] for a in (w, m, v))
            res = _sum_adamw(ex.partial[pre][i], ex.received[pre][i], chip, wv, mv, vv, name="adamw_" + n)
            grads[n], deltas[n], new_m[n], new_v[n] = (r.T if transposed else r for r in res)
    grads.update(_unpack_grads(_sum_chips(ex.partial["mix"][0], ex.received["mix"][0], chip, name="sum_mix_chips")))

    pieces = [_pad_cols(gs[n], SMALL_SLOTS[n]) for n in SMALL_NAMES] + [dconv_w.reshape(1, 3 * D), loss_row]
    total = _small_exchange(jnp.concatenate(pieces, axis=1).reshape(-1, 128), reduce=True, name="reduce_small").reshape(-1)
    off = 0
    for n in SMALL_NAMES:
        grads[n] = total[off:off + w[n].shape[0]]
        off += SMALL_SLOTS[n]
    conv_full = total[off:off + 3 * D].reshape(3, D)
    grads["conv_w"] = lax.dynamic_slice(conv_full, (0, me * HEAD_PAD), (3, HEAD_PAD))
    loss = total[off + 3 * D]

    for n in WEIGHT_NAMES:
        if n in deltas:
            continue
        shape = w[n].shape
        if n in COLUMN_MAJOR:
            ops = [a.T for a in (w[n], grads[n], m[n], v[n])]
            deltas[n], new_m[n], new_v[n] = (r.T for r in _adamw(*ops, name="adamw_" + n))
            continue
        if len(shape) == 1:
            view = (-1, 128) if shape[0] % 128 == 0 else (1, shape[0])
        else:
            view = shape
        dlt, nm, nv = _adamw(w[n].reshape(view), grads[n].reshape(view), m[n].reshape(view), v[n].reshape(view), name="adamw_" + n)
        deltas[n], new_m[n], new_v[n] = dlt.reshape(shape), nm.reshape(shape), nv.reshape(shape)
    return (loss, grad_x, *[grads[n] for n in WEIGHT_NAMES], *[deltas[n] for n in WEIGHT_NAMES],
            *[new_m[n] for n in WEIGHT_NAMES], *[new_v[n] for n in WEIGHT_NAMES])


def kernel(x, positions, ffn1_norm, ffn1_w_gate, ffn1_w_up, ffn1_w_down, mix_norm, w_in, gate_bias, q_a_norm, w_uq, kv_a_norm, w_uk, w_uv, q_head_norm, k_head_norm, w_proj_attn, conv_w, w_proj_conv, w_out, ffn2_norm, ffn2_w_gate, ffn2_w_up, ffn2_w_down, loss_target, m_ffn1_norm, m_ffn1_w_gate, m_ffn1_w_up, m_ffn1_w_down, m_mix_norm, m_w_in, m_gate_bias, m_q_a_norm, m_w_uq, m_kv_a_norm, m_w_uk, m_w_uv, m_q_head_norm, m_k_head_norm, m_w_proj_attn, m_conv_w, m_w_proj_conv, m_w_out, m_ffn2_norm, m_ffn2_w_gate, m_ffn2_w_up, m_ffn2_w_down, v_ffn1_norm, v_ffn1_w_gate, v_ffn1_w_up, v_ffn1_w_down, v_mix_norm, v_w_in, v_gate_bias, v_q_a_norm, v_w_uq, v_kv_a_norm, v_w_uk, v_w_uv, v_q_head_norm, v_k_head_norm, v_w_proj_attn, v_conv_w, v_w_proj_conv, v_w_out, v_ffn2_norm, v_ffn2_w_gate, v_ffn2_w_up, v_ffn2_w_down):
    given = dict(locals())
    w = {n: given[n] for n in WEIGHT_NAMES}
    m = {n: given["m_" + n] for n in WEIGHT_NAMES}
    v = {n: given["v_" + n] for n in WEIGHT_NAMES}
    return _step(x, positions, loss_target, w, m, v)
```

```python
import functools

import jax
import jax.numpy as jnp
from jax import lax
from jax.experimental import pallas as pl
from jax.experimental.pallas import tpu as pltpu

F32 = jnp.float32
BF16 = jnp.bfloat16
MESH = pl.DeviceIdType.MESH
ANY = pl.BlockSpec(memory_space=pl.ANY)

N_DEV = 8
D = 1024
DFF = 2816
N_HEADS = 8
HEAD_PAD = 128
QK_DIM = 96
NOPE = 64
ROPE_HALF = 16
Q_LORA = 384
KV_LORA = 256
LAT_PAD = 768
CONV_COLS = 3072
GATE_COLS = 2048
IN_DIM = 5792
IN_SHARD = IN_DIM // N_DEV
IN_SHARD_PAD = 736
FF_SHARD = DFF // N_DEV
ROPE_THETA = 10000.0
NORM_EPS = 1e-6
ATTN_SCALE = QK_DIM ** -0.5
NEG = -1e30

ADAM_LR, ADAM_B1, ADAM_B2, ADAM_EPS, ADAM_WD, ADAM_STEP = 0.001, 0.9, 0.999, 1e-08, 0.01, 10

PACK = ((("w_inT", IN_SHARD_PAD),), (("w_uq", 48), ("w_uk", 32), ("w_uv", 32), ("w_pa", 64), ("w_pc", 128), ("w_out", 128)))
PACK_OFF = {}
for _i, _group in enumerate(PACK):
    _o = 0
    for _n, _r in _group:
        PACK_OFF[_n] = (_i, _o, _r)
        _o += _r

VMEM_LIMIT = 56 * 1024 * 1024


def _params(*sem):
    return pltpu.CompilerParams(dimension_semantics=sem if sem else None, vmem_limit_bytes=VMEM_LIMIT)


class _Plan:
    def __init__(self, start, wait, n_remote, n_local, in_place=False):
        self.start, self.wait, self.n_remote, self.n_local, self.in_place = start, wait, n_remote, n_local, in_place

    def sems(self):
        return [pltpu.SemaphoreType.DMA((self.n_remote,)), pltpu.SemaphoreType.DMA((self.n_remote,)),
                pltpu.SemaphoreType.DMA((max(self.n_local, 1),))]


def _call(body, *, name, grid, in_specs, out_specs, out_shape, scratch_shapes, operands, sem, hosted=None):
    if hosted is None:
        outs = pl.pallas_call(body, name=name, grid=grid, in_specs=in_specs, out_specs=out_specs, out_shape=out_shape,
                              scratch_shapes=scratch_shapes, compiler_params=_params(*sem))(*operands)
        return outs, None
    plan, srcs, h_shapes = hosted
    n_in, n_out, n_scr, nh_in, nh_out = len(in_specs), len(out_specs), len(scratch_shapes), len(srcs), len(h_shapes)
    aliases = {n_in + a: n_out + a for a in range(nh_in)} if plan.in_place else {}

    def full_body(*refs):
        ins, refs = refs[:n_in], refs[n_in:]
        h_in, refs = refs[:nh_in], refs[nh_in:]
        outs, refs = refs[:n_out], refs[n_out:]
        h_out, refs = refs[:nh_out], refs[nh_out:]
        scr, sems = refs[:n_scr], refs[n_scr:]
        ids = [pl.program_id(ax) for ax in range(len(grid))]
        first = functools.reduce(jnp.logical_and, [i == 0 for i in ids])
        last = functools.reduce(jnp.logical_and, [i == g - 1 for i, g in zip(ids, grid)])

        @pl.when(first)
        def _():
            plan.start(h_in, h_out, *sems)

        body(*ins, *outs, *scr)

        @pl.when(last)
        def _():
            plan.wait(h_in, h_out, *sems)

    res = pl.pallas_call(
        full_body, name=name, grid=grid, in_specs=list(in_specs) + [ANY] * nh_in, out_specs=list(out_specs) + [ANY] * nh_out,
        out_shape=list(out_shape) + list(h_shapes), scratch_shapes=list(scratch_shapes) + plan.sems(),
        input_output_aliases=aliases, compiler_params=_params(*(["arbitrary"] * len(grid))),
    )(*operands, *srcs)
    return res[:n_out], res[n_out:]


def _dot_nn(a, b):
    return lax.dot_general(a, b, (((1,), (0,)), ((), ())), preferred_element_type=F32)


def _dot_nt(a, b):
    return lax.dot_general(a, b, (((1,), (1,)), ((), ())), preferred_element_type=F32)


def _dot_tn(a, b):
    return lax.dot_general(a, b, (((0,), (0,)), ((), ())), preferred_element_type=F32)


def _sigmoid(x):
    return 1.0 / (1.0 + jnp.exp(-x))


def _rms_stats(x):
    r = lax.rsqrt(jnp.mean(x * x, axis=-1, keepdims=True) + NORM_EPS)
    return x * r, r


def _rms_bwd(dy, xhat, r, g):
    dg = jnp.sum(dy * xhat, axis=0, keepdims=True)
    dxh = dy * g
    dx = r * (dxh - xhat * jnp.mean(dxh * xhat, axis=-1, keepdims=True))
    return dx, dg


def _mm(a, b, *, mode, out_dtype, tm, tn, tk, name, add=None, scale=1.0, hosted=None):
    if mode == "nn":
        (m, k), (_, n) = a.shape, b.shape
    elif mode == "nt":
        (m, k), (n, _) = a.shape, b.shape
    else:
        (k, m), (_, n) = a.shape, b.shape
    assert m % tm == 0 and n % tn == 0 and k % tk == 0, (name, m, n, k, tm, tn, tk)
    nk = k // tk
    dot = {"nn": _dot_nn, "nt": _dot_nt, "tn": _dot_tn}[mode]
    a_spec = pl.BlockSpec((tk, tm), lambda i, j, kk: (kk, i)) if mode == "tn" else pl.BlockSpec((tm, tk), lambda i, j, kk: (i, kk))
    b_spec = pl.BlockSpec((tn, tk), lambda i, j, kk: (j, kk)) if mode == "nt" else pl.BlockSpec((tk, tn), lambda i, j, kk: (kk, j))
    o_spec = pl.BlockSpec((tm, tn), lambda i, j, kk: (i, j))
    has_add = add is not None

    def finish(prod, c_ref, o_ref):
        if scale != 1.0:
            prod = prod * scale
        o_ref[...] = ((c_ref[...] + prod) if has_add else prod).astype(out_dtype)

    def body(*refs):
        a_ref, b_ref = refs[:2]
        c_ref = refs[2] if has_add else None
        o_ref = refs[3] if has_add else refs[2]
        if nk == 1:
            finish(dot(a_ref[...], b_ref[...]), c_ref, o_ref)
            return
        acc_ref = refs[-1]
        kk = pl.program_id(2)

        @pl.when(kk == 0)
        def _():
            acc_ref[...] = jnp.zeros_like(acc_ref)

        acc_ref[...] += dot(a_ref[...], b_ref[...])

        @pl.when(kk == nk - 1)
        def _():
            finish(acc_ref[...], c_ref, o_ref)

    operands = (a, b, add) if has_add else (a, b)
    in_specs = [a_spec, b_spec] + ([o_spec] if has_add else [])
    (out,), got = _call(
        body, name=name, grid=(m // tm, n // tn, nk), in_specs=in_specs, out_specs=[o_spec],
        out_shape=[jax.ShapeDtypeStruct((m, n), out_dtype)], scratch_shapes=[pltpu.VMEM((tm, tn), F32)] if nk > 1 else [],
        operands=operands, sem=("parallel", "parallel", "arbitrary"), hosted=hosted)
    return out if hosted is None else (out, got)


def _rms_fwd(x, g, *, tm, name, hosted=None):
    t, d = x.shape

    def body(x_ref, g_ref, h_ref):
        xhat, _ = _rms_stats(x_ref[...])
        h_ref[...] = (xhat * g_ref[...]).astype(BF16)

    (h,), got = _call(
        body, name=name, grid=(t // tm,),
        in_specs=[pl.BlockSpec((tm, d), lambda i: (i, 0)), pl.BlockSpec((1, d), lambda i: (0, 0))],
        out_specs=[pl.BlockSpec((tm, d), lambda i: (i, 0))], out_shape=[jax.ShapeDtypeStruct((t, d), BF16)], scratch_shapes=[],
        operands=(x, g), sem=("parallel",), hosted=hosted)
    return h, got


def _rms_bwd_res(dh, x, g, dres, *, tm, name):
    t, d = x.shape

    def body(dh_ref, x_ref, g_ref, dres_ref, dx_ref, dg_ref):
        xhat, r = _rms_stats(x_ref[...])
        dx, dg = _rms_bwd(dh_ref[...], xhat, r, g_ref[...])
        dx_ref[...] = dres_ref[...] + dx

        @pl.when(pl.program_id(0) == 0)
        def _():
            dg_ref[...] = jnp.zeros_like(dg_ref)

        dg_ref[...] += dg

    row = pl.BlockSpec((tm, d), lambda i: (i, 0))
    vec = pl.BlockSpec((1, d), lambda i: (0, 0))
    return pl.pallas_call(
        body, name=name, grid=(t // tm,), in_specs=[row, row, vec, row], out_specs=[row, vec],
        out_shape=[jax.ShapeDtypeStruct((t, d), F32), jax.ShapeDtypeStruct((1, d), F32)],
        compiler_params=_params("arbitrary"),
    )(dh, x, g, dres)


def _ffn_up(x, g, wgT, wuT, *, tm, hc, name, hosted=None):
    t, d = x.shape

    def body(x_ref, g_ref, wg_ref, wu_ref, h_ref, a_ref, b_ref, s_ref):
        @pl.when(pl.program_id(1) == 0)
        def _():
            xhat, _ = _rms_stats(x_ref[...])
            h_ref[...] = (xhat * g_ref[...]).astype(BF16)

        h = h_ref[...]
        a = _dot_nt(h, wg_ref[...])
        b = _dot_nt(h, wu_ref[...])
        a_ref[...] = a.astype(BF16)
        b_ref[...] = b.astype(BF16)
        s_ref[...] = (a * _sigmoid(a) * b).astype(BF16)

    row = pl.BlockSpec((tm, d), lambda i, j: (i, 0))
    vec = pl.BlockSpec((1, d), lambda i, j: (0, 0))
    wsp = pl.BlockSpec((hc, d), lambda i, j: (j, 0))
    hid = pl.BlockSpec((tm, hc), lambda i, j: (i, j))
    hid_shape = jax.ShapeDtypeStruct((t, DFF), BF16)
    return _call(
        body, name=name, grid=(t // tm, DFF // hc), in_specs=[row, vec, wsp, wsp], out_specs=[row, hid, hid, hid],
        out_shape=[jax.ShapeDtypeStruct((t, d), BF16), hid_shape, hid_shape, hid_shape], scratch_shapes=[],
        operands=(x, g, wgT, wuT), sem=("parallel", "arbitrary"), hosted=hosted)


def _ffn_down_bwd(dout, a, b, wd, *, tm, hc, name, hosted=None):
    t, d = dout.shape

    def body(dout_ref, a_ref, b_ref, wd_ref, dy_ref, da_ref, db_ref):
        @pl.when(pl.program_id(1) == 0)
        def _():
            dy_ref[...] = (0.5 * dout_ref[...]).astype(BF16)

        ds = _dot_nt(dy_ref[...], wd_ref[...])
        av = a_ref[...].astype(F32)
        sg = _sigmoid(av)
        sl = av * sg
        da_ref[...] = (ds * b_ref[...].astype(F32) * (sg + sl * (1.0 - sg))).astype(BF16)
        db_ref[...] = (ds * sl).astype(BF16)

    row = pl.BlockSpec((tm, d), lambda i, j: (i, 0))
    wsp = pl.BlockSpec((hc, d), lambda i, j: (j, 0))
    hid = pl.BlockSpec((tm, hc), lambda i, j: (i, j))
    hid_shape = jax.ShapeDtypeStruct((t, DFF), BF16)
    return _call(
        body, name=name, grid=(t // tm, DFF // hc), in_specs=[row, hid, hid, wsp], out_specs=[row, hid, hid],
        out_shape=[jax.ShapeDtypeStruct((t, d), BF16), hid_shape, hid_shape], scratch_shapes=[],
        operands=(dout, a, b, wd), sem=("parallel", "arbitrary"), hosted=hosted)


def _proj_bwd(dlat, dconv3, dgl, latT, convT, gateT, x, g, dres, *, tm, name, hosted=None):
    t, d = x.shape

    def body(dl_ref, dc_ref, dg_ref, wl_ref, wc_ref, wg_ref, x_ref, g_ref, dres_ref, dx_ref, dgain_ref):
        @pl.when(pl.program_id(0) == 0)
        def _():
            dgain_ref[...] = jnp.zeros_like(dgain_ref)

        dh = _dot_nn(dl_ref[...], wl_ref[...]) + _dot_nn(dc_ref[...], wc_ref[...]) + _dot_nn(dg_ref[...], wg_ref[...])
        xhat, r = _rms_stats(x_ref[...])
        dx, dgain = _rms_bwd(dh, xhat, r, g_ref[...])
        dx_ref[...] = dres_ref[...] + dx
        dgain_ref[...] += dgain

    def rows(w):
        return pl.BlockSpec((tm, w), lambda i: (i, 0))

    def full(r):
        return pl.BlockSpec((r, d), lambda i: (0, 0))

    return _call(
        body, name=name, grid=(t // tm,),
        in_specs=[rows(LAT_PAD), rows(CONV_COLS), rows(GATE_COLS), full(LAT_PAD), full(CONV_COLS), full(GATE_COLS), rows(d), full(1), rows(d)],
        out_specs=[rows(d), full(1)], out_shape=[jax.ShapeDtypeStruct((t, d), F32), jax.ShapeDtypeStruct((1, d), F32)],
        scratch_shapes=[], operands=(dlat, dconv3, dgl, latT, convT, gateT, x, g, dres), sem=("arbitrary",), hosted=hosted)


def _ffn_up_bwd(da, db, wgT, wuT, x, g, dout, *, tm, name, hosted=None):
    t, d = x.shape

    def body(da_ref, db_ref, wg_ref, wu_ref, x_ref, g_ref, dout_ref, dx_ref, dg_ref):
        @pl.when(pl.program_id(0) == 0)
        def _():
            dg_ref[...] = jnp.zeros_like(dg_ref)

        dh = _dot_nn(da_ref[...], wg_ref[...]) + _dot_nn(db_ref[...], wu_ref[...])
        xhat, r = _rms_stats(x_ref[...])
        dx, dg = _rms_bwd(dh, xhat, r, g_ref[...])
        dx_ref[...] = dout_ref[...] + dx
        dg_ref[...] += dg

    row = pl.BlockSpec((tm, d), lambda i: (i, 0))
    vec = pl.BlockSpec((1, d), lambda i: (0, 0))
    hid = pl.BlockSpec((tm, DFF), lambda i: (i, 0))
    wsp = pl.BlockSpec((DFF, d), lambda i: (0, 0))
    return _call(
        body, name=name, grid=(t // tm,), in_specs=[hid, hid, wsp, wsp, row, vec, row], out_specs=[row, vec],
        out_shape=[jax.ShapeDtypeStruct((t, d), F32), jax.ShapeDtypeStruct((1, d), F32)], scratch_shapes=[],
        operands=(da, db, wgT, wuT, x, g, dout), sem=("arbitrary",), hosted=hosted)


def _rope_fwd(x, c, s1, s2):
    return x * c + pltpu.roll(x, HEAD_PAD - ROPE_HALF, 1) * s1 + pltpu.roll(x, ROPE_HALF, 1) * s2


def _rope_bwd(dy, c, s1, s2):
    return dy * c + pltpu.roll(dy * s1, ROPE_HALF, 1) + pltpu.roll(dy * s2, HEAD_PAD - ROPE_HALF, 1)


def _head_stats(x):
    r = lax.rsqrt(jnp.sum(x * x, axis=-1, keepdims=True) * (1.0 / QK_DIM) + NORM_EPS)
    return x * r, r


def _mla_prep_fwd(lat, gq, gkv, ghq, ghk, wq, wk, wv, rc, rs1, rs2, *, tm, name):
    t = lat.shape[0]

    def body(lat_ref, gq_ref, gkv_ref, ghq_ref, ghk_ref, wq_ref, wk_ref, wv_ref, c_ref, s1_ref, s2_ref,
             q_ref, k_ref, v_ref, qn_ref, ckv_ref):
        lat_v = lat_ref[...]
        qhat, _ = _rms_stats(lat_v[:, :Q_LORA].astype(F32))
        qn = (qhat * gq_ref[...]).astype(BF16)
        khat, _ = _rms_stats(lat_v[:, Q_LORA:Q_LORA + KV_LORA].astype(F32))
        ckv = (khat * gkv_ref[...]).astype(BF16)
        ckv_ext = jnp.concatenate([ckv, lat_v[:, Q_LORA + KV_LORA:]], axis=1)
        qn_ref[...] = qn
        ckv_ref[...] = ckv_ext
        q_pre = _dot_nn(qn, wq_ref[...])
        k_pre = _dot_nn(ckv_ext, wk_ref[...])
        v_ref[...] = _dot_nn(ckv, wv_ref[...]).astype(BF16)
        c, s1, s2 = c_ref[...], s1_ref[...], s2_ref[...]
        for h in range(N_HEADS):
            hs = slice(h * HEAD_PAD, (h + 1) * HEAD_PAD)
            xq, _ = _head_stats(q_pre[:, hs])
            q_ref[:, hs] = _rope_fwd(xq * ghq_ref[...], c, s1, s2).astype(BF16)
            xk, _ = _head_stats(k_pre[:, hs])
            k_ref[:, hs] = _rope_fwd(xk * ghk_ref[...], c, s1, s2).astype(BF16)

    def row(w):
        return pl.BlockSpec((tm, w), lambda i: (i, 0))

    def full(r, w):
        return pl.BlockSpec((r, w), lambda i: (0, 0))

    wide = jax.ShapeDtypeStruct((t, D), BF16)
    lat3 = jax.ShapeDtypeStruct((t, Q_LORA), BF16)
    return pl.pallas_call(
        body, name=name, grid=(t // tm,),
        in_specs=[row(LAT_PAD), full(1, Q_LORA), full(1, KV_LORA), full(1, HEAD_PAD), full(1, HEAD_PAD),
                  full(Q_LORA, D), full(Q_LORA, D), full(KV_LORA, D), row(HEAD_PAD), row(HEAD_PAD), row(HEAD_PAD)],
        out_specs=[row(D), row(D), row(D), row(Q_LORA), row(Q_LORA)],
        out_shape=[wide, wide, wide, lat3, lat3],
        compiler_params=_params("parallel"),
    )(lat, gq, gkv, ghq, ghk, wq, wk, wv, rc, rs1, rs2)


def _mla_prep_bwd(dq, dk, dv, lat, qn, ckv_ext, gq, gkv, ghq, ghk, wq, wk, wv, rc, rs1, rs2, *, tm, name):
    t = lat.shape[0]

    def body(dq_ref, dk_ref, dv_ref, lat_ref, qn_ref, ckv_ref, gq_ref, gkv_ref, ghq_ref, ghk_ref, wq_ref, wk_ref, wv_ref,
             c_ref, s1_ref, s2_ref, dlat_ref, dqp_ref, dkp_ref, dgq_ref, dgkv_ref, dghq_ref, dghk_ref):
        @pl.when(pl.program_id(0) == 0)
        def _():
            dgq_ref[...] = jnp.zeros_like(dgq_ref)
            dgkv_ref[...] = jnp.zeros_like(dgkv_ref)
            dghq_ref[...] = jnp.zeros_like(dghq_ref)
            dghk_ref[...] = jnp.zeros_like(dghk_ref)

        c, s1, s2 = c_ref[...], s1_ref[...], s2_ref[...]
        q_pre = _dot_nn(qn_ref[...], wq_ref[...])
        k_pre = _dot_nn(ckv_ref[...], wk_ref[...])

        def heads(pre, dy_ref, gh_ref, dgh_ref, out_ref):
            dgh = jnp.zeros((1, HEAD_PAD), F32)
            for h in range(N_HEADS):
                hs = slice(h * HEAD_PAD, (h + 1) * HEAD_PAD)
                d = _rope_bwd(dy_ref[:, hs], c, s1, s2)
                xhat, r = _head_stats(pre[:, hs])
                dgh = dgh + jnp.sum(d * xhat, axis=0, keepdims=True)
                dxh = d * gh_ref[...]
                dx = r * (dxh - xhat * (jnp.sum(dxh * xhat, axis=-1, keepdims=True) * (1.0 / QK_DIM)))
                out_ref[:, hs] = dx.astype(BF16)
            dgh_ref[...] += dgh

        heads(q_pre, dq_ref, ghq_ref, dghq_ref, dqp_ref)
        heads(k_pre, dk_ref, ghk_ref, dghk_ref, dkp_ref)
        dqn = _dot_nt(dqp_ref[...], wq_ref[...])
        dce = _dot_nt(dkp_ref[...], wk_ref[...])
        dckv = dce[:, :KV_LORA] + _dot_nt(dv_ref[...], wv_ref[...])
        lat_v = lat_ref[...]
        qhat, rq = _rms_stats(lat_v[:, :Q_LORA].astype(F32))
        dql, dgq = _rms_bwd(dqn, qhat, rq, gq_ref[...])
        khat, rk = _rms_stats(lat_v[:, Q_LORA:Q_LORA + KV_LORA].astype(F32))
        dkl, dgkv = _rms_bwd(dckv, khat, rk, gkv_ref[...])
        dgq_ref[...] += dgq
        dgkv_ref[...] += dgkv
        dlat_ref[...] = jnp.concatenate([dql, dkl, dce[:, KV_LORA:]], axis=1).astype(BF16)

    def row(w):
        return pl.BlockSpec((tm, w), lambda i: (i, 0))

    def full(r, w):
        return pl.BlockSpec((r, w), lambda i: (0, 0))

    return pl.pallas_call(
        body, name=name, grid=(t // tm,),
        in_specs=[row(D), row(D), row(D), row(LAT_PAD), row(Q_LORA), row(Q_LORA), full(1, Q_LORA), full(1, KV_LORA),
                  full(1, HEAD_PAD), full(1, HEAD_PAD), full(Q_LORA, D), full(Q_LORA, D), full(KV_LORA, D),
                  row(HEAD_PAD), row(HEAD_PAD), row(HEAD_PAD)],
        out_specs=[row(LAT_PAD), row(D), row(D), full(1, Q_LORA), full(1, KV_LORA), full(1, HEAD_PAD), full(1, HEAD_PAD)],
        out_shape=[jax.ShapeDtypeStruct((t, LAT_PAD), BF16), jax.ShapeDtypeStruct((t, D), BF16), jax.ShapeDtypeStruct((t, D), BF16),
                   jax.ShapeDtypeStruct((1, Q_LORA), F32), jax.ShapeDtypeStruct((1, KV_LORA), F32),
                   jax.ShapeDtypeStruct((1, HEAD_PAD), F32), jax.ShapeDtypeStruct((1, HEAD_PAD), F32)],
        compiler_params=_params("arbitrary"),
    )(dq, dk, dv, lat, qn, ckv_ext, gq, gkv, ghq, ghk, wq, wk, wv, rc, rs1, rs2)


def _causal_keep(tq):
    r = lax.broadcasted_iota(jnp.int32, (tq, tq), 0)
    c = lax.broadcasted_iota(jnp.int32, (tq, tq), 1)
    return c <= r


def _flash_fwd(q, k, v, *, n_seq, seq, tq, name, hosted=None):
    nq = seq // tq

    def body(q_ref, k_ref, v_ref, o_ref, lse_ref):
        qi = pl.program_id(2)
        qv = q_ref[...]

        def step(j, carry, masked):
            m, l, acc = carry
            kj = k_ref[pl.ds(pl.multiple_of(j * tq, tq), tq), :]
            vj = v_ref[pl.ds(pl.multiple_of(j * tq, tq), tq), :]
            s = _dot_nt(qv, kj) * ATTN_SCALE
            if masked:
                s = jnp.where(_causal_keep(tq), s, NEG)
            m_new = jnp.maximum(m, jnp.max(s, axis=-1, keepdims=True))
            alpha = jnp.exp(m - m_new)
            p = jnp.exp(s - m_new)
            l = alpha * l + jnp.sum(p, axis=-1, keepdims=True)
            acc = alpha * acc + _dot_nn(p.astype(BF16), vj)
            return m_new, l, acc

        init = (jnp.full((tq, 1), NEG, F32), jnp.zeros((tq, 1), F32), jnp.zeros((tq, HEAD_PAD), F32))
        carry = lax.fori_loop(0, qi, lambda j, cr: step(j, cr, False), init)
        m, l, acc = step(qi, carry, True)
        o_ref[...] = (acc / l).astype(BF16)
        lse_ref[...] = jnp.broadcast_to(m + jnp.log(l), (tq, HEAD_PAD))

    qspec = pl.BlockSpec((tq, HEAD_PAD), lambda b, h, i: (b * nq + i, h))
    kspec = pl.BlockSpec((seq, HEAD_PAD), lambda b, h, i: (b, h))
    t = n_seq * seq
    return _call(
        body, name=name, grid=(n_seq, N_HEADS, nq), in_specs=[qspec, kspec, kspec], out_specs=[qspec, qspec],
        out_shape=[jax.ShapeDtypeStruct((t, D), BF16), jax.ShapeDtypeStruct((t, D), F32)], scratch_shapes=[],
        operands=(q, k, v), sem=("parallel", "parallel", "arbitrary"), hosted=hosted)


def _flash_bwd(q, k, v, o, lse, do, *, n_seq, seq, tq, name, hosted=None):
    nq = seq // tq

    def body(q_ref, k_ref, v_ref, o_ref, lse_ref, do_ref, dq_ref, dk_ref, dv_ref, dk_acc, dv_acc):
        j = pl.program_id(2)

        @pl.when(j == 0)
        def _():
            dq_ref[...] = jnp.zeros_like(dq_ref)

        dk_acc[...] = jnp.zeros_like(dk_acc)
        dv_acc[...] = jnp.zeros_like(dv_acc)
        kv = k_ref[...]
        vv = v_ref[...]

        def step(i, masked):
            rows = pl.ds(pl.multiple_of(i * tq, tq), tq)
            qi = q_ref[rows, :]
            doi = do_ref[rows, :]
            delta = jnp.sum(doi.astype(F32) * o_ref[rows, :].astype(F32), axis=-1, keepdims=True)
            s = _dot_nt(qi, kv) * ATTN_SCALE
            p = jnp.exp(s - lse_ref[rows, :][:, :1])
            if masked:
                p = jnp.where(_causal_keep(tq), p, 0.0)
            dv_acc[...] += _dot_tn(p.astype(BF16), doi)
            dp = _dot_nt(doi, vv)
            ds = (p * (dp - delta) * ATTN_SCALE).astype(BF16)
            dk_acc[...] += _dot_tn(ds, qi)
            dq_ref[rows, :] += _dot_nn(ds, kv)

        step(j, True)

        def loop_body(i, carry):
            step(i, False)
            return carry

        lax.fori_loop(j + 1, nq, loop_body, 0)
        dk_ref[...] = dk_acc[...]
        dv_ref[...] = dv_acc[...].astype(BF16)

    full = pl.BlockSpec((seq, HEAD_PAD), lambda b, h, j: (b, h))
    tile = pl.BlockSpec((tq, HEAD_PAD), lambda b, h, j: (b * nq + j, h))
    t = n_seq * seq
    return _call(
        body, name=name, grid=(n_seq, N_HEADS, nq), in_specs=[full, tile, tile, full, full, full],
        out_specs=[full, tile, tile],
        out_shape=[jax.ShapeDtypeStruct((t, D), F32), jax.ShapeDtypeStruct((t, D), F32), jax.ShapeDtypeStruct((t, D), BF16)],
        scratch_shapes=[pltpu.VMEM((tq, HEAD_PAD), F32), pltpu.VMEM((tq, HEAD_PAD), F32)],
        operands=(q, k, v, o, lse, do), sem=("parallel", "parallel", "arbitrary"), hosted=hosted)


CONV_CB = 256


def _shift_down(u, k, row):
    return jnp.where(row >= k, pltpu.roll(u, k, 0), 0.0)


def _shift_up(u, k, row, n):
    return jnp.where(row < n - k, pltpu.roll(u, n - k, 0), 0.0)


def _conv_fwd(conv3, cw, *, n_seq, seq, name):
    def body(c_ref, w_ref, p_ref):
        blk = c_ref[...].astype(F32)
        xc, gb, gc = blk[:, :CONV_CB], blk[:, CONV_CB:2 * CONV_CB], blk[:, 2 * CONV_CB:]
        row = lax.broadcasted_iota(jnp.int32, (seq, CONV_CB), 0)
        u = gc * xc
        z = w_ref[0:1, :] * _shift_down(u, 2, row) + w_ref[1:2, :] * _shift_down(u, 1, row) + w_ref[2:3, :] * u
        p_ref[...] = (gb * z).astype(BF16)

    return pl.pallas_call(
        body, name=name, grid=(n_seq, D // CONV_CB),
        in_specs=[pl.BlockSpec((seq, 3 * CONV_CB), lambda b, j: (b, j)), pl.BlockSpec((3, CONV_CB), lambda b, j: (0, j))],
        out_specs=pl.BlockSpec((seq, CONV_CB), lambda b, j: (b, j)),
        out_shape=jax.ShapeDtypeStruct((n_seq * seq, D), BF16),
        compiler_params=_params("parallel", "parallel"),
    )(conv3, cw)


def _conv_bwd(dp, conv3, cw, *, n_seq, seq, name):
    def body(dp_ref, c_ref, w_ref, dc_ref, dw_ref):
        @pl.when(pl.program_id(1) == 0)
        def _():
            dw_ref[...] = jnp.zeros_like(dw_ref)

        blk = c_ref[...].astype(F32)
        xc, gb, gc = blk[:, :CONV_CB], blk[:, CONV_CB:2 * CONV_CB], blk[:, 2 * CONV_CB:]
        row = lax.broadcasted_iota(jnp.int32, (seq, CONV_CB), 0)
        w0, w1, w2 = w_ref[0:1, :], w_ref[1:2, :], w_ref[2:3, :]
        u = gc * xc
        u1 = _shift_down(u, 1, row)
        u2 = _shift_down(u, 2, row)
        z = w0 * u2 + w1 * u1 + w2 * u
        dpv = dp_ref[...].astype(F32)
        dz = dpv * gb
        du = w2 * dz + w1 * _shift_up(dz, 1, row, seq) + w0 * _shift_up(dz, 2, row, seq)
        dc_ref[...] = jnp.concatenate([du * gc, dpv * z, du * xc], axis=1).astype(BF16)
        dw_ref[0:1, :] += jnp.sum(dz * u2, axis=0, keepdims=True)
        dw_ref[1:2, :] += jnp.sum(dz * u1, axis=0, keepdims=True)
        dw_ref[2:3, :] += jnp.sum(dz * u, axis=0, keepdims=True)

    return pl.pallas_call(
        body, name=name, grid=(D // CONV_CB, n_seq),
        in_specs=[pl.BlockSpec((seq, CONV_CB), lambda j, b: (b, j)), pl.BlockSpec((seq, 3 * CONV_CB), lambda j, b: (b, j)),
                  pl.BlockSpec((3, CONV_CB), lambda j, b: (0, j))],
        out_specs=[pl.BlockSpec((seq, 3 * CONV_CB), lambda j, b: (b, j)), pl.BlockSpec((3, CONV_CB), lambda j, b: (0, j))],
        out_shape=[jax.ShapeDtypeStruct((n_seq * seq, CONV_COLS), BF16), jax.ShapeDtypeStruct((3, D), F32)],
        compiler_params=_params("parallel", "arbitrary"),
    )(dp, conv3, cw)


def _merge_fwd(o, p, gl, bias, x1, wpa, wpc, wout, *, tm, name, hosted=None):
    t = x1.shape[0]

    def body(o_ref, p_ref, gl_ref, b_ref, x_ref, wpa_ref, wpc_ref, wout_ref, x2_ref, mg_ref, ya_ref, yb_ref):
        ya = _dot_nn(o_ref[...], wpa_ref[...])
        yb = _dot_nn(p_ref[...], wpc_ref[...])
        gates = _sigmoid(gl_ref[...].astype(F32) + b_ref[...])
        merged = (gates[:, :D] * ya + gates[:, D:] * yb).astype(BF16)
        ya_ref[...] = ya.astype(BF16)
        yb_ref[...] = yb.astype(BF16)
        mg_ref[...] = merged
        x2_ref[...] = x_ref[...] + _dot_nn(merged, wout_ref[...])

    row = pl.BlockSpec((tm, D), lambda i: (i, 0))
    row2 = pl.BlockSpec((tm, GATE_COLS), lambda i: (i, 0))
    wsp = pl.BlockSpec((D, D), lambda i: (0, 0))
    wide = jax.ShapeDtypeStruct((t, D), BF16)
    return _call(
        body, name=name, grid=(t // tm,),
        in_specs=[row, row, row2, pl.BlockSpec((1, GATE_COLS), lambda i: (0, 0)), row, wsp, wsp, wsp],
        out_specs=[row, row, row, row], out_shape=[jax.ShapeDtypeStruct((t, D), F32), wide, wide, wide], scratch_shapes=[],
        operands=(o, p, gl, bias, x1, wpa, wpc, wout), sem=("parallel",), hosted=hosted)


def _merge_bwd(dx2, ya, yb, gl, bias, wpa, wpc, wout, *, tm, name, hosted=None):
    t = dx2.shape[0]

    def body(dx_ref, ya_ref, yb_ref, gl_ref, b_ref, wpa_ref, wpc_ref, wout_ref,
             dxb_ref, dya_ref, dyb_ref, dgl_ref, do_ref, dp_ref, db_ref):
        @pl.when(pl.program_id(0) == 0)
        def _():
            db_ref[...] = jnp.zeros_like(db_ref)

        dxb = dx_ref[...].astype(BF16)
        dxb_ref[...] = dxb
        dm = _dot_nt(dxb, wout_ref[...])
        gates = _sigmoid(gl_ref[...].astype(F32) + b_ref[...])
        ga, gb = gates[:, :D], gates[:, D:]
        dya = (dm * ga).astype(BF16)
        dyb = (dm * gb).astype(BF16)
        dya_ref[...] = dya
        dyb_ref[...] = dyb
        dgl = jnp.concatenate([dm * ya_ref[...].astype(F32) * ga * (1.0 - ga),
                               dm * yb_ref[...].astype(F32) * gb * (1.0 - gb)], axis=1)
        dgl_ref[...] = dgl.astype(BF16)
        db_ref[...] += jnp.sum(dgl, axis=0, keepdims=True)
        do_ref[...] = _dot_nt(dya, wpa_ref[...]).astype(BF16)
        dp_ref[...] = _dot_nt(dyb, wpc_ref[...]).astype(BF16)

    row = pl.BlockSpec((tm, D), lambda i: (i, 0))
    row2 = pl.BlockSpec((tm, GATE_COLS), lambda i: (i, 0))
    vec2 = pl.BlockSpec((1, GATE_COLS), lambda i: (0, 0))
    wsp = pl.BlockSpec((D, D), lambda i: (0, 0))
    wide = jax.ShapeDtypeStruct((t, D), BF16)
    return _call(
        body, name=name, grid=(t // tm,), in_specs=[row, row, row, row2, vec2, wsp, wsp, wsp],
        out_specs=[row, row, row, row2, row, row, vec2],
        out_shape=[wide, wide, wide, jax.ShapeDtypeStruct((t, GATE_COLS), BF16), wide, wide,
                   jax.ShapeDtypeStruct((1, GATE_COLS), F32)],
        scratch_shapes=[], operands=(dx2, ya, yb, gl, bias, wpa, wpc, wout), sem=("arbitrary",), hosted=hosted)


def _loss_head(y, target, *, tm, name):
    t, d = y.shape

    def body(y_ref, t_ref, dy_ref, loss_ref):
        @pl.when(pl.program_id(0) == 0)
        def _():
            loss_ref[...] = jnp.zeros_like(loss_ref)

        err = y_ref[...] - t_ref[...]
        dy_ref[...] = err * (1.0 / d)
        loss_ref[...] += jnp.sum(jnp.sum(err * err, axis=-1, keepdims=True), axis=0, keepdims=True) * (0.5 / d)

    row = pl.BlockSpec((tm, d), lambda i: (i, 0))
    return pl.pallas_call(
        body, name=name, grid=(t // tm,), in_specs=[row, row], out_specs=[row, pl.BlockSpec((1, 128), lambda i: (0, 0))],
        out_shape=[jax.ShapeDtypeStruct((t, d), F32), jax.ShapeDtypeStruct((1, 128), F32)],
        compiler_params=_params("arbitrary"),
    )(y, target)


def _adamw(w, g, m, v, *, name):
    rows, cols = w.shape
    tr = max([c for c in range(8, 513, 8) if rows % c == 0], default=rows)
    c1 = 1.0 / (1.0 - ADAM_B1 ** ADAM_STEP)
    c2 = 1.0 / (1.0 - ADAM_B2 ** ADAM_STEP)

    def body(w_ref, g_ref, m_ref, v_ref, d_ref, nm_ref, nv_ref):
        gv = g_ref[...]
        nm = ADAM_B1 * m_ref[...] + (1.0 - ADAM_B1) * gv
        nv = ADAM_B2 * v_ref[...] + (1.0 - ADAM_B2) * (gv * gv)
        nm_ref[...] = nm
        nv_ref[...] = nv
        d_ref[...] = -ADAM_LR * ((nm * c1) / (jnp.sqrt(nv * c2) + ADAM_EPS) + ADAM_WD * w_ref[...])

    spec = pl.BlockSpec((tr, cols), lambda i: (i, 0))
    shp = jax.ShapeDtypeStruct((rows, cols), F32)
    return pl.pallas_call(
        body, name=name, grid=(rows // tr,), in_specs=[spec] * 4, out_specs=[spec] * 3, out_shape=[shp] * 3,
        compiler_params=_params("parallel"),
    )(w, g, m, v)


def _place():
    return lax.axis_index("x"), lax.axis_index("y"), lax.axis_index("c")


def _other_chips(x, y):
    return [(1 - x, y), (x, 1 - y), (1 - x, 1 - y)]


def _remote(src, dst, send, recv, dev):
    return pltpu.make_async_remote_copy(src_ref=src, dst_ref=dst, send_sem=send, recv_sem=recv, device_id=dev, device_id_type=MESH)


def _gather_chips_plan(n):
    def start(srcs, dsts, send, recv, local):
        x, y, cc = _place()
        me = 4 * x + 2 * y + cc
        for a in range(n):
            pltpu.make_async_copy(srcs[a], dsts[a].at[me], local.at[a]).start()
            for k, (px, py) in enumerate(_other_chips(x, y)):
                _remote(srcs[a], dsts[a].at[me], send.at[3 * a + k], recv.at[3 * a + k], (px, py, cc)).start()

    def wait(srcs, dsts, send, recv, local):
        x, y, cc = _place()
        me = 4 * x + 2 * y + cc
        for a in range(n):
            for k, (px, py) in enumerate(_other_chips(x, y)):
                _remote(srcs[a], dsts[a].at[4 * px + 2 * py + cc], send.at[3 * a + k], recv.at[3 * a + k], (px, py, cc)).wait_recv()
        for a in range(n):
            for k, (px, py) in enumerate(_other_chips(x, y)):
                _remote(srcs[a], dsts[a].at[me], send.at[3 * a + k], recv.at[3 * a + k], (px, py, cc)).wait_send()
            pltpu.make_async_copy(srcs[a], dsts[a].at[me], local.at[a]).wait()

    return _Plan(start, wait, 3 * n, n)


def _scatter_chips_plan(n):
    def start(srcs, dsts, send, recv, local):
        x, y, cc = _place()
        for a in range(n):
            for k, (px, py) in enumerate(_other_chips(x, y)):
                _remote(srcs[a].at[2 * px + py], dsts[a].at[k], send.at[3 * a + k], recv.at[3 * a + k], (px, py, cc)).start()

    def wait(srcs, dsts, send, recv, local):
        x, y, cc = _place()
        for a in range(n):
            for k, (px, py) in enumerate(_other_chips(x, y)):
                _remote(srcs[a].at[k], dsts[a].at[k], send.at[3 * a + k], recv.at[3 * a + k], (px, py, cc)).wait_recv()
        for a in range(n):
            for k, (px, py) in enumerate(_other_chips(x, y)):
                _remote(srcs[a].at[k], dsts[a].at[k], send.at[3 * a + k], recv.at[3 * a + k], (px, py, cc)).wait_send()

    return _Plan(start, wait, 3 * n, 0)


def _gather_shapes(blocks):
    return [jax.ShapeDtypeStruct((N_DEV,) + b.shape, b.dtype) for b in blocks]


def _scatter_shapes(parts):
    return [jax.ShapeDtypeStruct((3,) + p.shape[1:], p.dtype) for p in parts]


def _gather_sibling_plan(n):
    def start(srcs, dsts, send, recv, local):
        x, y, cc = _place()
        for a in range(n):
            for q in range(4):
                _remote(srcs[a].at[2 * q + cc], dsts[a].at[2 * q + cc], send.at[4 * a + q], recv.at[4 * a + q], (x, y, 1 - cc)).start()

    def wait(srcs, dsts, send, recv, local):
        x, y, cc = _place()
        for a in range(n):
            for q in range(4):
                _remote(srcs[a].at[2 * q + cc], dsts[a].at[2 * q + 1 - cc], send.at[4 * a + q], recv.at[4 * a + q],
                        (x, y, 1 - cc)).wait_recv()
        for a in range(n):
            for q in range(4):
                _remote(srcs[a].at[2 * q + cc], dsts[a].at[2 * q + cc], send.at[4 * a + q], recv.at[4 * a + q],
                        (x, y, 1 - cc)).wait_send()

    return _Plan(start, wait, 4 * n, 0, in_place=True)


def _scatter_sibling_plan(n):
    def start(srcs, dsts, send, recv, local):
        x, y, cc = _place()
        for a in range(n):
            for q in range(4):
                _remote(srcs[a].at[2 * q + 1 - cc], dsts[a].at[q], send.at[4 * a + q], recv.at[4 * a + q], (x, y, 1 - cc)).start()

    def wait(srcs, dsts, send, recv, local):
        x, y, cc = _place()
        for a in range(n):
            for q in range(4):
                _remote(srcs[a].at[q], dsts[a].at[q], send.at[4 * a + q], recv.at[4 * a + q], (x, y, 1 - cc)).wait_recv()
        for a in range(n):
            for q in range(4):
                _remote(srcs[a].at[q], dsts[a].at[q], send.at[4 * a + q], recv.at[4 * a + q], (x, y, 1 - cc)).wait_send()

    return _Plan(start, wait, 4 * n, 0)


def _same_shapes(arrs):
    return [jax.ShapeDtypeStruct(a.shape, a.dtype) for a in arrs]


def _halved_shapes(parts):
    return [jax.ShapeDtypeStruct((4,) + p.shape[1:], p.dtype) for p in parts]


def _run_plan(plan, srcs, out_shapes, *, name):
    n_in, n_out = len(srcs), len(out_shapes)

    def body(*refs):
        h_in, h_out, sems = refs[:n_in], refs[n_in:n_in + n_out], refs[n_in + n_out:]
        plan.start(h_in, h_out, *sems)
        plan.wait(h_in, h_out, *sems)

    return pl.pallas_call(body, name=name, in_specs=[ANY] * n_in, out_specs=[ANY] * n_out, out_shape=list(out_shapes),
                          input_output_aliases={a: a for a in range(n_in)} if plan.in_place else {},
                          scratch_shapes=plan.sems())(*srcs)


def _sum_sibling(p, q, core, *, name):
    _, r, c = p.shape

    def body(core_ref, p_ref, q_ref, o_ref):
        o_ref[...] = (p_ref[...].astype(F32) + q_ref[...].astype(F32)).astype(BF16)

    grid_spec = pltpu.PrefetchScalarGridSpec(
        num_scalar_prefetch=1, grid=(4,),
        in_specs=[pl.BlockSpec((1, r, c), lambda ch, core_ref: (2 * ch + core_ref[0], 0, 0)),
                  pl.BlockSpec((1, r, c), lambda ch, core_ref: (ch, 0, 0))],
        out_specs=pl.BlockSpec((1, r, c), lambda ch, core_ref: (ch, 0, 0)))
    return pl.pallas_call(
        body, name=name, grid_spec=grid_spec, out_shape=jax.ShapeDtypeStruct((4, r, c), BF16),
        compiler_params=_params("parallel"),
    )(core, p, q)


def _sum_chips(s1, r2, chip, *, name):
    _, r, c = s1.shape

    def body(chip_ref, s_ref, r_ref, o_ref):
        acc = s_ref[0].astype(F32)
        for k in range(3):
            acc = acc + r_ref[k].astype(F32)
        o_ref[...] = acc

    grid_spec = pltpu.PrefetchScalarGridSpec(
        num_scalar_prefetch=1, grid=(1,),
        in_specs=[pl.BlockSpec((1, r, c), lambda i, chip_ref: (chip_ref[0], 0, 0)),
                  pl.BlockSpec((3, r, c), lambda i, chip_ref: (0, 0, 0))],
        out_specs=pl.BlockSpec((r, c), lambda i, chip_ref: (0, 0)))
    return pl.pallas_call(
        body, name=name, grid_spec=grid_spec, out_shape=jax.ShapeDtypeStruct((r, c), F32),
        compiler_params=_params("arbitrary"),
    )(chip, s1, r2)


def _sum_adamw(s1, r2, chip, w, m, v, *, name):
    _, r, c = s1.shape
    c1 = 1.0 / (1.0 - ADAM_B1 ** ADAM_STEP)
    c2 = 1.0 / (1.0 - ADAM_B2 ** ADAM_STEP)

    def body(chip_ref, s_ref, r_ref, w_ref, m_ref, v_ref, g_ref, d_ref, nm_ref, nv_ref):
        gv = s_ref[0].astype(F32)
        for k in range(3):
            gv = gv + r_ref[k].astype(F32)
        g_ref[...] = gv
        nm = ADAM_B1 * m_ref[...] + (1.0 - ADAM_B1) * gv
        nv = ADAM_B2 * v_ref[...] + (1.0 - ADAM_B2) * (gv * gv)
        nm_ref[...] = nm
        nv_ref[...] = nv
        d_ref[...] = -ADAM_LR * ((nm * c1) / (jnp.sqrt(nv * c2) + ADAM_EPS) + ADAM_WD * w_ref[...])

    flat = pl.BlockSpec((r, c), lambda i, chip_ref: (0, 0))
    grid_spec = pltpu.PrefetchScalarGridSpec(
        num_scalar_prefetch=1, grid=(1,),
        in_specs=[pl.BlockSpec((1, r, c), lambda i, chip_ref: (chip_ref[0], 0, 0)),
                  pl.BlockSpec((3, r, c), lambda i, chip_ref: (0, 0, 0)), flat, flat, flat],
        out_specs=[flat] * 4)
    return pl.pallas_call(
        body, name=name, grid_spec=grid_spec, out_shape=[jax.ShapeDtypeStruct((r, c), F32)] * 4,
        compiler_params=_params("arbitrary"),
    )(chip, s1, r2, w, m, v)


def _small_exchange(v, *, reduce, name):
    r, c = v.shape

    def body(x_ref, o_ref, *rest):
        if reduce:
            buf_ref, send_sems, recv_sems = rest
        else:
            buf_ref = o_ref
            send_sems, recv_sems = rest
        x, y, cc = _place()
        me = 4 * x + 2 * y + cc

        def peer(k):
            return ((1 - x) if k & 4 else x, (1 - y) if k & 2 else y, (1 - cc) if k & 1 else cc)

        buf_ref[me] = x_ref[...]
        sends = []
        for k in range(1, N_DEV):
            cp = pltpu.make_async_remote_copy(src_ref=x_ref, dst_ref=buf_ref.at[me], send_sem=send_sems.at[k - 1],
                                              recv_sem=recv_sems.at[k - 1], device_id=peer(k), device_id_type=MESH)
            cp.start()
            sends.append(cp)
        for k in range(1, N_DEV):
            px, py, pc = peer(k)
            pltpu.make_async_remote_copy(src_ref=x_ref, dst_ref=buf_ref.at[4 * px + 2 * py + pc], send_sem=send_sems.at[k - 1],
                                         recv_sem=recv_sems.at[k - 1], device_id=peer(k), device_id_type=MESH).wait_recv()
        for cp in sends:
            cp.wait_send()
        if reduce:
            acc = buf_ref[0]
            for s in range(1, N_DEV):
                acc = acc + buf_ref[s]
            o_ref[...] = acc

    vm = pl.BlockSpec(memory_space=pltpu.VMEM)
    sems = [pltpu.SemaphoreType.DMA((N_DEV - 1,)), pltpu.SemaphoreType.DMA((N_DEV - 1,))]
    if reduce:
        out_shape, scratch = jax.ShapeDtypeStruct((r, c), F32), [pltpu.VMEM((N_DEV, r, c), F32)] + sems
    else:
        out_shape, scratch = jax.ShapeDtypeStruct((N_DEV, r, c), F32), sems
    return pl.pallas_call(body, name=name, in_specs=[vm], out_specs=vm, out_shape=out_shape, scratch_shapes=scratch)(v)


def _rows(a):
    return a.reshape(-1, D)


def _pad_cols(a, to):
    return jnp.pad(a, ((0, 0), (0, to - a.shape[1])))


def _pack_weights(w):
    parts = {
        "w_inT": jnp.pad(w["w_in"].T, ((0, IN_SHARD_PAD - IN_SHARD), (0, 0))),
        "w_uq": _rows(_pad_cols(w["w_uq"], HEAD_PAD)), "w_uk": _rows(_pad_cols(w["w_uk"], HEAD_PAD)),
        "w_uv": _rows(_pad_cols(w["w_uv"], HEAD_PAD)), "w_pa": _rows(w["w_proj_attn"]),
        "w_pc": w["w_proj_conv"], "w_out": w["w_out"],
    }
    return [jnp.concatenate([parts[n].astype(BF16) for n, _ in group], axis=0) for group in PACK]


def _cols_from_shards(gs, name, rows):
    idx, off, r = PACK_OFF[name]
    return gs[idx][:, off:off + r].reshape(N_DEV, rows, HEAD_PAD).transpose(1, 0, 2).reshape(rows, N_DEV * HEAD_PAD)


def _rows_from_shards(gs, name, keep=None):
    idx, off, r = PACK_OFF[name]
    keep = r if keep is None else keep
    return gs[idx][:, off:off + keep].reshape(N_DEV * keep, D)


def _rope_placement():
    i = lax.broadcasted_iota(jnp.int32, (HEAD_PAD, D), 0)
    j = lax.broadcasted_iota(jnp.int32, (HEAD_PAD, D), 1)
    return ((i < 2 * ROPE_HALF) & (j % HEAD_PAD == NOPE + i)).astype(BF16)


def _unpack_weights(g):
    w_inT = _rows_from_shards(g, "w_inT", IN_SHARD)
    lat_rows = Q_LORA + KV_LORA + 2 * ROPE_HALF
    conv = w_inT[lat_rows:lat_rows + CONV_COLS].reshape(3, D // CONV_CB, CONV_CB, D).transpose(1, 0, 2, 3).reshape(CONV_COLS, D)
    wpa = _cols_from_shards(g, "w_pa", 512).reshape(N_HEADS, NOPE, D)
    return {
        "latT": jnp.pad(w_inT[:lat_rows], ((0, LAT_PAD - lat_rows), (0, 0))),
        "convT": conv, "gateT": w_inT[lat_rows + CONV_COLS:],
        "wq": _cols_from_shards(g, "w_uq", Q_LORA),
        "wk": jnp.concatenate([_cols_from_shards(g, "w_uk", KV_LORA), _rope_placement()], axis=0),
        "wv": _cols_from_shards(g, "w_uv", KV_LORA),
        "wpa": jnp.pad(wpa, ((0, 0), (0, HEAD_PAD - NOPE), (0, 0))).reshape(D, D),
        "wpc": _rows_from_shards(g, "w_pc"), "wout": _rows_from_shards(g, "w_out"),
    }


def _shards_from_cols(a):
    rows = a.shape[0]
    return a.reshape(rows, N_DEV, HEAD_PAD).transpose(1, 0, 2).reshape(N_DEV, rows * HEAD_PAD // D, D)


def _pack_grads(gw):
    lat_rows = Q_LORA + KV_LORA + 2 * ROPE_HALF
    conv = gw["convT"].reshape(D // CONV_CB, 3, CONV_CB, D).transpose(1, 0, 2, 3).reshape(CONV_COLS, D)
    w_inT = jnp.concatenate([gw["latT"][:lat_rows], conv, gw["gateT"]], axis=0).reshape(N_DEV, IN_SHARD, D)
    wpa = gw["wpa"].reshape(N_HEADS, HEAD_PAD, D)[:, :NOPE].reshape(N_HEADS * NOPE, D)
    parts = {}
    parts.update({
        "w_inT": jnp.pad(w_inT, ((0, 0), (0, IN_SHARD_PAD - IN_SHARD), (0, 0))),
        "w_uq": _shards_from_cols(gw["wq"]), "w_uk": _shards_from_cols(gw["wk"][:KV_LORA]),
        "w_uv": _shards_from_cols(gw["wv"][:KV_LORA]), "w_pa": _shards_from_cols(wpa),
        "w_pc": gw["wpc"].reshape(N_DEV, D // N_DEV, D), "w_out": gw["wout"].reshape(N_DEV, D // N_DEV, D),
    })
    return [jnp.concatenate([parts[n] for n, _ in group], axis=1) for group in PACK]


def _unpack_grads(mines):
    def seg(name, keep=None):
        idx, off, r = PACK_OFF[name]
        return mines[idx][off:off + (r if keep is None else keep)]

    return {
        "w_in": seg("w_inT", IN_SHARD).T,
        "w_uq": seg("w_uq").reshape(Q_LORA, HEAD_PAD)[:, :QK_DIM],
        "w_uk": seg("w_uk").reshape(KV_LORA, HEAD_PAD)[:, :NOPE],
        "w_uv": seg("w_uv").reshape(KV_LORA, HEAD_PAD)[:, :NOPE],
        "w_proj_attn": seg("w_pa").reshape(512, HEAD_PAD),
        "w_proj_conv": seg("w_pc"), "w_out": seg("w_out"),
    }


def _rope_tables(positions):
    inv_freq = 1.0 / (ROPE_THETA ** (jnp.arange(ROPE_HALF, dtype=F32) / ROPE_HALF))
    ang = positions.reshape(-1).astype(F32)[:, None] * inv_freq
    cos, sin = jnp.cos(ang), jnp.sin(ang)
    t = ang.shape[0]
    zero = jnp.zeros((t, ROPE_HALF), F32)
    head = jnp.ones((t, NOPE), F32)
    tail = jnp.zeros((t, HEAD_PAD - QK_DIM), F32)
    nohead = jnp.zeros((t, NOPE), F32)
    rc = jnp.concatenate([head, cos, cos, tail], axis=1)
    rs1 = jnp.concatenate([nohead, -sin, zero, tail], axis=1)
    rs2 = jnp.concatenate([nohead, zero, sin, tail], axis=1)
    return rc, rs1, rs2


def _local_step(x, positions, target, conv_w, small, ex):
    n_seq, seq, d = x.shape
    t = n_seq * seq
    x0 = x.reshape(t, d)
    tgt = target.reshape(t, d)
    rc, rs1, rs2 = _rope_tables(positions)
    ghq = _pad_cols(small["q_head_norm"], HEAD_PAD)
    ghk = _pad_cols(small["k_head_norm"], HEAD_PAD)
    TM, HC, TQ = 1024, 256, 512

    def mm(*args, hosted=None, **kw):
        res = _mm(*args, hosted=hosted, **kw)
        return res if hosted is not None else (res, None)

    def wgrad(a, b, name, tm=None, hosted=None):
        return mm(a, b, mode="tn", out_dtype=BF16, tm=tm or a.shape[1], tn=b.shape[1], tk=512, name=name, hosted=hosted)

    def ffn_fwd(xin, gain, wg, wu, wd, tag, host_up, host_down):
        (h, a, b, s), got_up = _ffn_up(xin, gain, wg, wu, tm=TM, hc=HC, name=tag + "_up", hosted=host_up)
        xout, got_down = mm(s, wd, mode="nn", out_dtype=F32, tm=512, tn=D, tk=DFF, name=tag + "_down", add=xin, scale=0.5,
                            hosted=host_down)
        return xout, (h, a, b, s), got_up, got_down

    f1g, f1u, f1d = ex.gather_now("ffn1")
    x1, (h1, a1, b1, s1), got_in, got_misc = ffn_fwd(x0, small["ffn1_norm"], f1g, f1u, f1d, "ffn1",
                                                     ex.gather_chips("mix_in"), ex.gather_chips("mix_misc"))
    hm, got = _rms_fwd(x1, small["mix_norm"], tm=TM, name="mix_norm_fwd", hosted=ex.gather_sibling(got_in, got_misc))
    W = ex.mix_weights(got)
    lat = _mm(hm, W["latT"], mode="nt", out_dtype=BF16, tm=TM, tn=LAT_PAD, tk=D, name="proj_lat")
    conv3 = _mm(hm, W["convT"], mode="nt", out_dtype=BF16, tm=TM, tn=CONV_COLS // 2, tk=D, name="proj_conv")
    gl = _mm(hm, W["gateT"], mode="nt", out_dtype=BF16, tm=TM, tn=GATE_COLS // 2, tk=D, name="proj_gate")
    q, k, v, qn, ckv = _mla_prep_fwd(lat, small["q_a_norm"], small["kv_a_norm"], ghq, ghk, W["wq"], W["wk"], W["wv"], rc, rs1, rs2,
                                     tm=512, name="mla_prep_fwd")
    (o, lse), got = _flash_fwd(q, k, v, n_seq=n_seq, seq=seq, tq=TQ, name="attn_fwd", hosted=ex.gather_chips("ffn2"))
    p = _conv_fwd(conv3, conv_w, n_seq=n_seq, seq=seq, name="conv_fwd")
    (x2, merged, ya, yb), got = _merge_fwd(o, p, gl, small["gate_bias"], x1, W["wpa"], W["wpc"], W["wout"], tm=512, name="merge_fwd",
                                           hosted=ex.gather_sibling(got))
    f2g, f2u, f2d = ex.ffn_weights(got)
    y, (h2, a2, b2, s2), _, _ = ffn_fwd(x2, small["ffn2_norm"], f2g, f2u, f2d, "ffn2", None, None)
    dy, loss_row = _loss_head(y, tgt, tm=TM, name="loss_head")

    gw, gs = {}, {}
    (dyb2, da2, db2), _ = _ffn_down_bwd(dy, a2, b2, f2d, tm=TM, hc=HC, name="ffn2_down_bwd")
    (dx2, gs["ffn2_norm"]), _ = _ffn_up_bwd(da2, db2, f2g, f2u, x2, small["ffn2_norm"], dy, tm=512, name="ffn2_up_bwd")
    ffn2_grads = [wgrad(da2, h2, "ffn2_dwg", tm=DFF // 2)[0], wgrad(db2, h2, "ffn2_dwu", tm=DFF // 2)[0],
                  wgrad(s2, dyb2, "ffn2_dwd", tm=DFF // 2)[0]]

    (dx2b, dya, dyb, dgl, do, dp, gs["gate_bias"]), got = _merge_bwd(
        dx2, ya, yb, gl, small["gate_bias"], W["wpa"], W["wpc"], W["wout"], tm=512, name="merge_bwd",
        hosted=ex.scatter_sibling("ffn2", ffn2_grads))
    ex.scatter_sibling_done("ffn2", got)
    gw["wout"] = wgrad(merged, dx2b, "dw_out")[0]
    gw["wpa"] = wgrad(o, dya, "dw_pa")[0]
    gw["wpc"] = wgrad(p, dyb, "dw_pc")[0]
    dconv3, dconv_w = _conv_bwd(dp, conv3, conv_w, n_seq=n_seq, seq=seq, name="conv_bwd")
    (dq, dk, dv), got = _flash_bwd(q, k, v, o, lse, do, n_seq=n_seq, seq=seq, tq=TQ, name="attn_bwd",
                                   hosted=ex.scatter_chips("ffn2"))
    ex.scatter_chips_done("ffn2", got)
    dlat, dqp, dkp, gs["q_a_norm"], gs["kv_a_norm"], dghq, dghk = _mla_prep_bwd(
        dq, dk, dv, lat, qn, ckv, small["q_a_norm"], small["kv_a_norm"], ghq, ghk, W["wq"], W["wk"], W["wv"], rc, rs1, rs2,
        tm=512, name="mla_prep_bwd")
    gs["q_head_norm"], gs["k_head_norm"] = dghq[:, :QK_DIM], dghk[:, :QK_DIM]
    gw["wq"] = wgrad(qn, dqp, "dw_uq")[0]
    gw["wk"] = wgrad(ckv, dkp, "dw_uk")[0]
    gw["wv"] = wgrad(ckv, dv, "dw_uv")[0]
    gw["convT"] = wgrad(dconv3, hm, "dw_conv", tm=CONV_COLS // 2)[0]
    gw["gateT"] = wgrad(dgl, hm, "dw_gate")[0]
    gw["latT"] = wgrad(dlat, hm, "dw_lat")[0]
    ex.scatter_sibling_now("mix", gw)
    (dx1, gs["mix_norm"]), got = _proj_bwd(dlat, dconv3, dgl, W["latT"], W["convT"], W["gateT"], x1, small["mix_norm"], dx2,
                                           tm=512, name="proj_bwd", hosted=ex.scatter_chips("mix_in"))
    ex.scatter_chips_done("mix_in", got)

    (dyb1, da1, db1), got = _ffn_down_bwd(dx1, a1, b1, f1d, tm=TM, hc=HC, name="ffn1_down_bwd", hosted=ex.scatter_chips("mix_misc"))
    ex.scatter_chips_done("mix_misc", got)
    ex.scatter_sibling_now("ffn1_d", [wgrad(s1, dyb1, "ffn1_dwd", tm=DFF // 2)[0]])
    dwg, got = wgrad(da1, h1, "ffn1_dwg", tm=DFF // 2, hosted=ex.scatter_chips("ffn1_d"))
    ex.scatter_chips_done("ffn1_d", got)
    ex.scatter_sibling_now("ffn1_g", [dwg])
    dwu, got = wgrad(db1, h1, "ffn1_dwu", tm=DFF // 2, hosted=ex.scatter_chips("ffn1_g"))
    ex.scatter_chips_done("ffn1_g", got)
    ex.scatter_sibling_now("ffn1_u", [dwu])
    (dx0, gs["ffn1_norm"]), got = _ffn_up_bwd(da1, db1, f1g, f1u, x0, small["ffn1_norm"], dx1, tm=512, name="ffn1_up_bwd",
                                              hosted=ex.scatter_chips("ffn1_u"))
    ex.scatter_chips_done("ffn1_u", got)
    return loss_row, dx0.reshape(n_seq, seq, d), dconv_w, gs


class _MeshExchange:
    def __init__(self, w, core, chip):
        self.w, self.core, self.chip = w, core, chip
        self.partial, self.received, self._packed = {}, {}, None

    def _blocks(self, group):
        w = self.w
        if group.startswith("ffn"):
            return [w[group + "_w_gate"].T.astype(BF16), w[group + "_w_up"].T.astype(BF16), w[group + "_w_down"].astype(BF16)]
        if self._packed is None:
            self._packed = _pack_weights(w)
        return [self._packed[0 if group == "mix_in" else 1]]

    def gather_chips(self, group):
        blocks = self._blocks(group)
        return _gather_chips_plan(len(blocks)), blocks, _gather_shapes(blocks)

    def gather_sibling(self, *gots):
        half = [g for got in gots for g in got]
        return _gather_sibling_plan(len(half)), half, _same_shapes(half)

    def gather_now(self, group):
        plan, blocks, shapes = self.gather_chips(group)
        half = list(_run_plan(plan, blocks, shapes, name="gather_%s_chips" % group))
        return self.ffn_weights(_run_plan(_gather_sibling_plan(len(half)), half, _same_shapes(half), name="gather_%s_sibling" % group))

    def ffn_weights(self, got):
        return [a.reshape(DFF, D) for a in got]

    def mix_weights(self, got):
        return _unpack_weights(list(got))

    def _parts(self, group, grads):
        if group == "mix":
            return _pack_grads(grads), ["mix_in", "mix_misc"]
        parts = [g.reshape(N_DEV, -1, D) for g in grads]
        return parts, ([group] if len(parts) == 1 else None)

    def scatter_sibling(self, group, grads):
        self._sent, self._names = self._parts(group, grads)
        return _scatter_sibling_plan(len(self._sent)), self._sent, _halved_shapes(self._sent)

    def scatter_sibling_done(self, group, got):
        sums = [_sum_sibling(p, q, self.core, name="sum_%s_sibling_%d" % (group, i)) for i, (p, q) in enumerate(zip(self._sent, got))]
        if self._names is None:
            self.partial[group] = sums
        else:
            for n, s in zip(self._names, sums):
                self.partial[n] = [s]

    def scatter_sibling_now(self, group, grads):
        plan, parts, shapes = self.scatter_sibling(group, grads)
        self.scatter_sibling_done(group, _run_plan(plan, parts, shapes, name="scatter_%s_sibling" % group))

    def scatter_chips(self, group):
        s1 = self.partial[group]
        return _scatter_chips_plan(len(s1)), s1, _scatter_shapes(s1)

    def scatter_chips_done(self, group, got):
        self.received[group] = list(got)


SMALL_NAMES = ("ffn1_norm", "mix_norm", "gate_bias", "q_a_norm", "kv_a_norm", "q_head_norm", "k_head_norm", "ffn2_norm")
SMALL_SLOTS = {"ffn1_norm": 1024, "mix_norm": 1024, "gate_bias": 2048, "q_a_norm": 384, "kv_a_norm": 256, "q_head_norm": 128,
               "k_head_norm": 128, "ffn2_norm": 1024, "conv_w": 3072, "loss": 128}
COLUMN_MAJOR = ("w_in", "w_uq", "w_uk", "w_uv")
WEIGHT_NAMES = ("ffn1_norm", "ffn1_w_gate", "ffn1_w_up", "ffn1_w_down", "mix_norm", "w_in", "gate_bias", "q_a_norm", "w_uq",
                "kv_a_norm", "w_uk", "w_uv", "q_head_norm", "k_head_norm", "w_proj_attn", "conv_w", "w_proj_conv", "w_out",
                "ffn2_norm", "ffn2_w_gate", "ffn2_w_up", "ffn2_w_down")


def _step(x, positions, loss_target, w, m, v):
    xi, yi, ci = _place()
    core = ci.astype(jnp.int32).reshape(1)
    chip = (2 * xi + yi).astype(jnp.int32).reshape(1)
    me = 4 * xi + 2 * yi + ci

    cw_all = _small_exchange(jnp.pad(w["conv_w"], ((0, 5), (0, 0))), reduce=False, name="gather_conv_w")
    conv_w = cw_all[:, :3].transpose(1, 0, 2).reshape(3, D)
    small = {n: w[n].reshape(1, -1) for n in SMALL_NAMES}
    ex = _MeshExchange(w, core, chip)

    loss_row, grad_x, dconv_w, gs = _local_step(x, positions, loss_target, conv_w, small, ex)

    grads, deltas, new_m, new_v = {}, {}, {}, {}
    where = {"ffn1_w_gate": ("ffn1_g", 0), "ffn1_w_up": ("ffn1_u", 0), "ffn1_w_down": ("ffn1_d", 0),
             "ffn2_w_gate": ("ffn2", 0), "ffn2_w_up": ("ffn2", 1), "ffn2_w_down": ("ffn2", 2)}
    for n, (group, i) in where.items():
        transposed = not n.endswith("down")
        wv, mv, vv = (a[n].T if transposed else a[n] for a in (w, m, v))
        res = _sum_adamw(ex.partial[group][i], ex.received[group][i], chip, wv, mv, vv, name="adamw_" + n)
        grads[n], deltas[n], new_m[n], new_v[n] = (r.T if transposed else r for r in res)
    grads.update(_unpack_grads([_sum_chips(ex.partial[g][0], ex.received[g][0], chip, name="sum_%s_chips" % g)
                                for g in ("mix_in", "mix_misc")]))

    pieces = [_pad_cols(gs[n], SMALL_SLOTS[n]) for n in SMALL_NAMES] + [dconv_w.reshape(1, 3 * D), loss_row]
    total = _small_exchange(jnp.concatenate(pieces, axis=1).reshape(-1, 128), reduce=True, name="reduce_small").reshape(-1)
    off = 0
    for n in SMALL_NAMES:
        grads[n] = total[off:off + w[n].shape[0]]
        off += SMALL_SLOTS[n]
    conv_full = total[off:off + 3 * D].reshape(3, D)
    grads["conv_w"] = lax.dynamic_slice(conv_full, (0, me * HEAD_PAD), (3, HEAD_PAD))
    loss = total[off + 3 * D]

    for n in WEIGHT_NAMES:
        if n in deltas:
            continue
        shape = w[n].shape
        if n in COLUMN_MAJOR:
            ops = [a.T for a in (w[n], grads[n], m[n], v[n])]
            deltas[n], new_m[n], new_v[n] = (r.T for r in _adamw(*ops, name="adamw_" + n))
            continue
        if len(shape) == 1:
            view = (-1, 128) if shape[0] % 128 == 0 else (1, shape[0])
        else:
            view = shape
        dlt, nm, nv = _adamw(w[n].reshape(view), grads[n].reshape(view), m[n].reshape(view), v[n].reshape(view), name="adamw_" + n)
        deltas[n], new_m[n], new_v[n] = dlt.reshape(shape), nm.reshape(shape), nv.reshape(shape)
    return (loss, grad_x, *[grads[n] for n in WEIGHT_NAMES], *[deltas[n] for n in WEIGHT_NAMES],
            *[new_m[n] for n in WEIGHT_NAMES], *[new_v[n] for n in WEIGHT_NAMES])


def kernel(x, positions, ffn1_norm, ffn1_w_gate, ffn1_w_up, ffn1_w_down, mix_norm, w_in, gate_bias, q_a_norm, w_uq, kv_a_norm, w_uk, w_uv, q_head_norm, k_head_norm, w_proj_attn, conv_w, w_proj_conv, w_out, ffn2_norm, ffn2_w_gate, ffn2_w_up, ffn2_w_down, loss_target, m_ffn1_norm, m_ffn1_w_gate, m_ffn1_w_up, m_ffn1_w_down, m_mix_norm, m_w_in, m_gate_bias, m_q_a_norm, m_w_uq, m_kv_a_norm, m_w_uk, m_w_uv, m_q_head_norm, m_k_head_norm, m_w_proj_attn, m_conv_w, m_w_proj_conv, m_w_out, m_ffn2_norm, m_ffn2_w_gate, m_ffn2_w_up, m_ffn2_w_down, v_ffn1_norm, v_ffn1_w_gate, v_ffn1_w_up, v_ffn1_w_down, v_mix_norm, v_w_in, v_gate_bias, v_q_a_norm, v_w_uq, v_kv_a_norm, v_w_uk, v_w_uv, v_q_head_norm, v_k_head_norm, v_w_proj_attn, v_conv_w, v_w_proj_conv, v_w_out, v_ffn2_norm, v_ffn2_w_gate, v_ffn2_w_up, v_ffn2_w_down):
    given = dict(locals())
    w = {n: given[n] for n in WEIGHT_NAMES}
    m = {n: given["m_" + n] for n in WEIGHT_NAMES}
    v = {n: given["v_" + n] for n in WEIGHT_NAMES}
    return _step(x, positions, loss_target, w, m, v)
```

```python
import functools

import jax
import jax.numpy as jnp
from jax import lax
from jax.experimental import pallas as pl
from jax.experimental.pallas import tpu as pltpu

F32 = jnp.float32
BF16 = jnp.bfloat16
MESH = pl.DeviceIdType.MESH
ANY = pl.BlockSpec(memory_space=pl.ANY)

N_DEV = 8
D = 1024
DFF = 2816
N_HEADS = 8
HEAD_PAD = 128
QK_DIM = 96
NOPE = 64
ROPE_HALF = 16
Q_LORA = 384
KV_LORA = 256
LAT_PAD = 768
CONV_COLS = 3072
GATE_COLS = 2048
IN_DIM = 5792
IN_SHARD = IN_DIM // N_DEV
IN_SHARD_PAD = 736
FF_SHARD = DFF // N_DEV
ROPE_THETA = 10000.0
NORM_EPS = 1e-6
ATTN_SCALE = QK_DIM ** -0.5
NEG = -1e30

ADAM_LR, ADAM_B1, ADAM_B2, ADAM_EPS, ADAM_WD, ADAM_STEP = 0.001, 0.9, 0.999, 1e-08, 0.01, 10

PACK = ((("w_inT", IN_SHARD_PAD),), (("w_uq", 48), ("w_uk", 32), ("w_uv", 32), ("w_pa", 64), ("w_pc", 128), ("w_out", 128)))
PACK_OFF = {}
for _i, _group in enumerate(PACK):
    _o = 0
    for _n, _r in _group:
        PACK_OFF[_n] = (_i, _o, _r)
        _o += _r

VMEM_LIMIT = 56 * 1024 * 1024


def _params(*sem):
    return pltpu.CompilerParams(dimension_semantics=sem if sem else None, vmem_limit_bytes=VMEM_LIMIT)


class _Plan:
    def __init__(self, start, wait, n_remote, n_local, in_place=False):
        self.start, self.wait, self.n_remote, self.n_local, self.in_place = start, wait, n_remote, n_local, in_place

    def sems(self):
        return [pltpu.SemaphoreType.DMA((self.n_remote,)), pltpu.SemaphoreType.DMA((self.n_remote,)),
                pltpu.SemaphoreType.DMA((max(self.n_local, 1),))]


def _call(body, *, name, grid, in_specs, out_specs, out_shape, scratch_shapes, operands, sem, hosted=None):
    if hosted is None:
        outs = pl.pallas_call(body, name=name, grid=grid, in_specs=in_specs, out_specs=out_specs, out_shape=out_shape,
                              scratch_shapes=scratch_shapes, compiler_params=_params(*sem))(*operands)
        return outs, None
    plan, srcs, h_shapes = hosted
    n_in, n_out, n_scr, nh_in, nh_out = len(in_specs), len(out_specs), len(scratch_shapes), len(srcs), len(h_shapes)
    aliases = {n_in + a: n_out + a for a in range(nh_in)} if plan.in_place else {}

    def full_body(*refs):
        ins, refs = refs[:n_in], refs[n_in:]
        h_in, refs = refs[:nh_in], refs[nh_in:]
        outs, refs = refs[:n_out], refs[n_out:]
        h_out, refs = refs[:nh_out], refs[nh_out:]
        scr, sems = refs[:n_scr], refs[n_scr:]
        ids = [pl.program_id(ax) for ax in range(len(grid))]
        first = functools.reduce(jnp.logical_and, [i == 0 for i in ids])
        last = functools.reduce(jnp.logical_and, [i == g - 1 for i, g in zip(ids, grid)])

        @pl.when(first)
        def _():
            plan.start(h_in, h_out, *sems)

        body(*ins, *outs, *scr)

        @pl.when(last)
        def _():
            plan.wait(h_in, h_out, *sems)

    res = pl.pallas_call(
        full_body, name=name, grid=grid, in_specs=list(in_specs) + [ANY] * nh_in, out_specs=list(out_specs) + [ANY] * nh_out,
        out_shape=list(out_shape) + list(h_shapes), scratch_shapes=list(scratch_shapes) + plan.sems(),
        input_output_aliases=aliases, compiler_params=_params(*(["arbitrary"] * len(grid))),
    )(*operands, *srcs)
    return res[:n_out], res[n_out:]


def _dot_nn(a, b):
    return lax.dot_general(a, b, (((1,), (0,)), ((), ())), preferred_element_type=F32)


def _dot_nt(a, b):
    return lax.dot_general(a, b, (((1,), (1,)), ((), ())), preferred_element_type=F32)


def _dot_tn(a, b):
    return lax.dot_general(a, b, (((0,), (0,)), ((), ())), preferred_element_type=F32)


def _sigmoid(x):
    return 1.0 / (1.0 + jnp.exp(-x))


def _rms_stats(x):
    r = lax.rsqrt(jnp.mean(x * x, axis=-1, keepdims=True) + NORM_EPS)
    return x * r, r


ROWS_WIDE = 16
ROWS_NARROW = 32
MM_ROWS = 256


def _row_chunks(n_rows, rows, fn, unrolled=False):
    if unrolled:
        for c in range(n_rows // rows):
            fn(slice(c * rows, (c + 1) * rows))
        return

    def step(c, carry):
        fn(pl.ds(pl.multiple_of(c * rows, rows), rows))
        return carry

    lax.fori_loop(0, n_rows // rows, step, 0)


def _rms_bwd(dy, xhat, r, g):
    dg = jnp.sum(dy * xhat, axis=0, keepdims=True)
    dxh = dy * g
    dx = r * (dxh - xhat * jnp.mean(dxh * xhat, axis=-1, keepdims=True))
    return dx, dg


def _mm(a, b, *, mode, out_dtype, tm, tn, tk, name, add=None, scale=1.0, hosted=None):
    if mode == "nn":
        (m, k), (_, n) = a.shape, b.shape
    elif mode == "nt":
        (m, k), (n, _) = a.shape, b.shape
    else:
        (k, m), (_, n) = a.shape, b.shape
    assert m % tm == 0 and n % tn == 0 and k % tk == 0, (name, m, n, k, tm, tn, tk)
    nk = k // tk
    dot = {"nn": _dot_nn, "nt": _dot_nt, "tn": _dot_tn}[mode]
    a_spec = pl.BlockSpec((tk, tm), lambda i, j, kk: (kk, i)) if mode == "tn" else pl.BlockSpec((tm, tk), lambda i, j, kk: (i, kk))
    b_spec = pl.BlockSpec((tn, tk), lambda i, j, kk: (j, kk)) if mode == "nt" else pl.BlockSpec((tk, tn), lambda i, j, kk: (kk, j))
    o_spec = pl.BlockSpec((tm, tn), lambda i, j, kk: (i, j))
    has_add = add is not None

    def finish(prod, c_ref, o_ref):
        if scale != 1.0:
            prod = prod * scale
        o_ref[...] = ((c_ref[...] + prod) if has_add else prod).astype(out_dtype)

    def body(*refs):
        a_ref, b_ref = refs[:2]
        c_ref = refs[2] if has_add else None
        o_ref = refs[3] if has_add else refs[2]
        if nk == 1:
            finish(dot(a_ref[...], b_ref[...]), c_ref, o_ref)
            return
        acc_ref = refs[-1]
        kk = pl.program_id(2)

        @pl.when(kk == 0)
        def _():
            acc_ref[...] = jnp.zeros_like(acc_ref)

        acc_ref[...] += dot(a_ref[...], b_ref[...])

        @pl.when(kk == nk - 1)
        def _():
            finish(acc_ref[...], c_ref, o_ref)

    operands = (a, b, add) if has_add else (a, b)
    in_specs = [a_spec, b_spec] + ([o_spec] if has_add else [])
    (out,), got = _call(
        body, name=name, grid=(m // tm, n // tn, nk), in_specs=in_specs, out_specs=[o_spec],
        out_shape=[jax.ShapeDtypeStruct((m, n), out_dtype)], scratch_shapes=[pltpu.VMEM((tm, tn), F32)] if nk > 1 else [],
        operands=operands, sem=("parallel", "parallel", "arbitrary"), hosted=hosted)
    return out if hosted is None else (out, got)


def _rms_fwd(x, g, *, tm, name, hosted=None):
    t, d = x.shape

    def body(x_ref, g_ref, h_ref):
        xhat, _ = _rms_stats(x_ref[...])
        h_ref[...] = (xhat * g_ref[...]).astype(BF16)

    (h,), got = _call(
        body, name=name, grid=(t // tm,),
        in_specs=[pl.BlockSpec((tm, d), lambda i: (i, 0)), pl.BlockSpec((1, d), lambda i: (0, 0))],
        out_specs=[pl.BlockSpec((tm, d), lambda i: (i, 0))], out_shape=[jax.ShapeDtypeStruct((t, d), BF16)], scratch_shapes=[],
        operands=(x, g), sem=("parallel",), hosted=hosted)
    return h, got


def _rms_bwd_res(dh, x, g, dres, *, tm, name):
    t, d = x.shape

    def body(dh_ref, x_ref, g_ref, dres_ref, dx_ref, dg_ref):
        xhat, r = _rms_stats(x_ref[...])
        dx, dg = _rms_bwd(dh_ref[...], xhat, r, g_ref[...])
        dx_ref[...] = dres_ref[...] + dx

        @pl.when(pl.program_id(0) == 0)
        def _():
            dg_ref[...] = jnp.zeros_like(dg_ref)

        dg_ref[...] += dg

    row = pl.BlockSpec((tm, d), lambda i: (i, 0))
    vec = pl.BlockSpec((1, d), lambda i: (0, 0))
    return pl.pallas_call(
        body, name=name, grid=(t // tm,), in_specs=[row, row, vec, row], out_specs=[row, vec],
        out_shape=[jax.ShapeDtypeStruct((t, d), F32), jax.ShapeDtypeStruct((1, d), F32)],
        compiler_params=_params("arbitrary"),
    )(dh, x, g, dres)


def _ffn_up(x, g, wgT, wuT, *, tm, hc, name, hosted=None):
    t, d = x.shape

    def body(x_ref, g_ref, wg_ref, wu_ref, h_ref, a_ref, b_ref, s_ref, a_scr, b_scr):
        @pl.when(pl.program_id(1) == 0)
        def _():
            xhat, _ = _rms_stats(x_ref[...])
            h_ref[...] = (xhat * g_ref[...]).astype(BF16)

        def act_rows(rows):
            a, b = a_scr[rows, :], b_scr[rows, :]
            a_ref[rows, :] = a.astype(BF16)
            b_ref[rows, :] = b.astype(BF16)
            s_ref[rows, :] = (a * _sigmoid(a) * b).astype(BF16)

        for blk in range(tm // MM_ROWS):
            rs = slice(blk * MM_ROWS, (blk + 1) * MM_ROWS)
            a_scr[rs, :] = _dot_nt(h_ref[rs, :], wg_ref[...])
            b_scr[rs, :] = _dot_nt(h_ref[rs, :], wu_ref[...])
            for c in range(MM_ROWS // ROWS_NARROW):
                act_rows(slice(blk * MM_ROWS + c * ROWS_NARROW, blk * MM_ROWS + (c + 1) * ROWS_NARROW))

    row = pl.BlockSpec((tm, d), lambda i, j: (i, 0))
    vec = pl.BlockSpec((1, d), lambda i, j: (0, 0))
    wsp = pl.BlockSpec((hc, d), lambda i, j: (j, 0))
    hid = pl.BlockSpec((tm, hc), lambda i, j: (i, j))
    hid_shape = jax.ShapeDtypeStruct((t, DFF), BF16)
    return _call(
        body, name=name, grid=(t // tm, DFF // hc), in_specs=[row, vec, wsp, wsp], out_specs=[row, hid, hid, hid],
        out_shape=[jax.ShapeDtypeStruct((t, d), BF16), hid_shape, hid_shape, hid_shape],
        scratch_shapes=[pltpu.VMEM((tm, hc), F32), pltpu.VMEM((tm, hc), F32)],
        operands=(x, g, wgT, wuT), sem=("parallel", "arbitrary"), hosted=hosted)


def _ffn_down_bwd(dout, a, b, wd, *, tm, hc, name, hosted=None):
    t, d = dout.shape

    def body(dout_ref, a_ref, b_ref, wd_ref, dy_ref, da_ref, db_ref, ds_scr):
        @pl.when(pl.program_id(1) == 0)
        def _():
            def half_rows(rows):
                dy_ref[rows, :] = (0.5 * dout_ref[rows, :]).astype(BF16)

            _row_chunks(tm, ROWS_WIDE, half_rows)

        def grad_rows(rows):
            ds = ds_scr[rows, :]
            av = a_ref[rows, :].astype(F32)
            sg = _sigmoid(av)
            sl = av * sg
            da_ref[rows, :] = (ds * b_ref[rows, :].astype(F32) * (sg + sl * (1.0 - sg))).astype(BF16)
            db_ref[rows, :] = (ds * sl).astype(BF16)

        for blk in range(tm // MM_ROWS):
            rs = slice(blk * MM_ROWS, (blk + 1) * MM_ROWS)
            ds_scr[rs, :] = _dot_nt(dy_ref[rs, :], wd_ref[...])
            for c in range(MM_ROWS // ROWS_WIDE):
                grad_rows(slice(blk * MM_ROWS + c * ROWS_WIDE, blk * MM_ROWS + (c + 1) * ROWS_WIDE))

    row = pl.BlockSpec((tm, d), lambda i, j: (i, 0))
    wsp = pl.BlockSpec((hc, d), lambda i, j: (j, 0))
    hid = pl.BlockSpec((tm, hc), lambda i, j: (i, j))
    hid_shape = jax.ShapeDtypeStruct((t, DFF), BF16)
    return _call(
        body, name=name, grid=(t // tm, DFF // hc), in_specs=[row, hid, hid, wsp], out_specs=[row, hid, hid],
        out_shape=[jax.ShapeDtypeStruct((t, d), BF16), hid_shape, hid_shape], scratch_shapes=[pltpu.VMEM((tm, hc), F32)],
        operands=(dout, a, b, wd), sem=("parallel", "arbitrary"), hosted=hosted)


def _proj_bwd(dlat, dconv3, dgl, latT, convT, gateT, x, g, dres, *, tm, name, hosted=None):
    t, d = x.shape

    def body(dl_ref, dc_ref, dg_ref, wl_ref, wc_ref, wg_ref, x_ref, g_ref, dres_ref, dx_ref, dgain_ref):
        @pl.when(pl.program_id(0) == 0)
        def _():
            dgain_ref[...] = jnp.zeros_like(dgain_ref)

        dh = _dot_nn(dl_ref[...], wl_ref[...]) + _dot_nn(dc_ref[...], wc_ref[...]) + _dot_nn(dg_ref[...], wg_ref[...])
        xhat, r = _rms_stats(x_ref[...])
        dx, dgain = _rms_bwd(dh, xhat, r, g_ref[...])
        dx_ref[...] = dres_ref[...] + dx
        dgain_ref[...] += dgain

    def rows(w):
        return pl.BlockSpec((tm, w), lambda i: (i, 0))

    def full(r):
        return pl.BlockSpec((r, d), lambda i: (0, 0))

    return _call(
        body, name=name, grid=(t // tm,),
        in_specs=[rows(LAT_PAD), rows(CONV_COLS), rows(GATE_COLS), full(LAT_PAD), full(CONV_COLS), full(GATE_COLS), rows(d), full(1), rows(d)],
        out_specs=[rows(d), full(1)], out_shape=[jax.ShapeDtypeStruct((t, d), F32), jax.ShapeDtypeStruct((1, d), F32)],
        scratch_shapes=[], operands=(dlat, dconv3, dgl, latT, convT, gateT, x, g, dres), sem=("arbitrary",), hosted=hosted)


def _ffn_up_bwd(da, db, wgT, wuT, x, g, dout, *, tm, name, hosted=None):
    t, d = x.shape

    def body(da_ref, db_ref, wg_ref, wu_ref, x_ref, g_ref, dout_ref, dx_ref, dg_ref):
        @pl.when(pl.program_id(0) == 0)
        def _():
            dg_ref[...] = jnp.zeros_like(dg_ref)

        dh = _dot_nn(da_ref[...], wg_ref[...]) + _dot_nn(db_ref[...], wu_ref[...])
        xhat, r = _rms_stats(x_ref[...])
        dx, dg = _rms_bwd(dh, xhat, r, g_ref[...])
        dx_ref[...] = dout_ref[...] + dx
        dg_ref[...] += dg

    row = pl.BlockSpec((tm, d), lambda i: (i, 0))
    vec = pl.BlockSpec((1, d), lambda i: (0, 0))
    hid = pl.BlockSpec((tm, DFF), lambda i: (i, 0))
    wsp = pl.BlockSpec((DFF, d), lambda i: (0, 0))
    return _call(
        body, name=name, grid=(t // tm,), in_specs=[hid, hid, wsp, wsp, row, vec, row], out_specs=[row, vec],
        out_shape=[jax.ShapeDtypeStruct((t, d), F32), jax.ShapeDtypeStruct((1, d), F32)], scratch_shapes=[],
        operands=(da, db, wgT, wuT, x, g, dout), sem=("arbitrary",), hosted=hosted)


def _rope_fwd(x, c, s1, s2):
    return x * c + pltpu.roll(x, HEAD_PAD - ROPE_HALF, 1) * s1 + pltpu.roll(x, ROPE_HALF, 1) * s2


def _rope_bwd(dy, c, s1, s2):
    return dy * c + pltpu.roll(dy * s1, ROPE_HALF, 1) + pltpu.roll(dy * s2, HEAD_PAD - ROPE_HALF, 1)


def _head_stats(x):
    r = lax.rsqrt(jnp.sum(x * x, axis=-1, keepdims=True) * (1.0 / QK_DIM) + NORM_EPS)
    return x * r, r


def _mla_prep_fwd(lat, gq, gkv, ghq, ghk, wq, wk, wv, rc, rs1, rs2, *, tm, name):
    t = lat.shape[0]

    def body(lat_ref, gq_ref, gkv_ref, ghq_ref, ghk_ref, wq_ref, wk_ref, wv_ref, c_ref, s1_ref, s2_ref,
             q_ref, k_ref, v_ref, qn_ref, ckv_ref):
        lat_v = lat_ref[...]
        qhat, _ = _rms_stats(lat_v[:, :Q_LORA].astype(F32))
        qn = (qhat * gq_ref[...]).astype(BF16)
        khat, _ = _rms_stats(lat_v[:, Q_LORA:Q_LORA + KV_LORA].astype(F32))
        ckv = (khat * gkv_ref[...]).astype(BF16)
        ckv_ext = jnp.concatenate([ckv, lat_v[:, Q_LORA + KV_LORA:]], axis=1)
        qn_ref[...] = qn
        ckv_ref[...] = ckv_ext
        q_pre = _dot_nn(qn, wq_ref[...])
        k_pre = _dot_nn(ckv_ext, wk_ref[...])
        v_ref[...] = _dot_nn(ckv, wv_ref[...]).astype(BF16)
        c, s1, s2 = c_ref[...], s1_ref[...], s2_ref[...]
        for h in range(N_HEADS):
            hs = slice(h * HEAD_PAD, (h + 1) * HEAD_PAD)
            xq, _ = _head_stats(q_pre[:, hs])
            q_ref[:, hs] = _rope_fwd(xq * ghq_ref[...], c, s1, s2).astype(BF16)
            xk, _ = _head_stats(k_pre[:, hs])
            k_ref[:, hs] = _rope_fwd(xk * ghk_ref[...], c, s1, s2).astype(BF16)

    def row(w):
        return pl.BlockSpec((tm, w), lambda i: (i, 0))

    def full(r, w):
        return pl.BlockSpec((r, w), lambda i: (0, 0))

    wide = jax.ShapeDtypeStruct((t, D), BF16)
    lat3 = jax.ShapeDtypeStruct((t, Q_LORA), BF16)
    return pl.pallas_call(
        body, name=name, grid=(t // tm,),
        in_specs=[row(LAT_PAD), full(1, Q_LORA), full(1, KV_LORA), full(1, HEAD_PAD), full(1, HEAD_PAD),
                  full(Q_LORA, D), full(Q_LORA, D), full(KV_LORA, D), row(HEAD_PAD), row(HEAD_PAD), row(HEAD_PAD)],
        out_specs=[row(D), row(D), row(D), row(Q_LORA), row(Q_LORA)],
        out_shape=[wide, wide, wide, lat3, lat3],
        compiler_params=_params("parallel"),
    )(lat, gq, gkv, ghq, ghk, wq, wk, wv, rc, rs1, rs2)


def _mla_prep_bwd(dq, dk, dv, lat, qn, ckv_ext, gq, gkv, ghq, ghk, wq, wk, wv, rc, rs1, rs2, *, tm, name):
    t = lat.shape[0]

    def body(dq_ref, dk_ref, dv_ref, lat_ref, qn_ref, ckv_ref, gq_ref, gkv_ref, ghq_ref, ghk_ref, wq_ref, wk_ref, wv_ref,
             c_ref, s1_ref, s2_ref, dlat_ref, dqp_ref, dkp_ref, dgq_ref, dgkv_ref, dghq_ref, dghk_ref):
        @pl.when(pl.program_id(0) == 0)
        def _():
            dgq_ref[...] = jnp.zeros_like(dgq_ref)
            dgkv_ref[...] = jnp.zeros_like(dgkv_ref)
            dghq_ref[...] = jnp.zeros_like(dghq_ref)
            dghk_ref[...] = jnp.zeros_like(dghk_ref)

        c, s1, s2 = c_ref[...], s1_ref[...], s2_ref[...]
        q_pre = _dot_nn(qn_ref[...], wq_ref[...])
        k_pre = _dot_nn(ckv_ref[...], wk_ref[...])

        def heads(pre, dy_ref, gh_ref, dgh_ref, out_ref):
            dgh = jnp.zeros((1, HEAD_PAD), F32)
            for h in range(N_HEADS):
                hs = slice(h * HEAD_PAD, (h + 1) * HEAD_PAD)
                d = _rope_bwd(dy_ref[:, hs], c, s1, s2)
                xhat, r = _head_stats(pre[:, hs])
                dgh = dgh + jnp.sum(d * xhat, axis=0, keepdims=True)
                dxh = d * gh_ref[...]
                dx = r * (dxh - xhat * (jnp.sum(dxh * xhat, axis=-1, keepdims=True) * (1.0 / QK_DIM)))
                out_ref[:, hs] = dx.astype(BF16)
            dgh_ref[...] += dgh

        heads(q_pre, dq_ref, ghq_ref, dghq_ref, dqp_ref)
        heads(k_pre, dk_ref, ghk_ref, dghk_ref, dkp_ref)
        dqn = _dot_nt(dqp_ref[...], wq_ref[...])
        dce = _dot_nt(dkp_ref[...], wk_ref[...])
        dckv = dce[:, :KV_LORA] + _dot_nt(dv_ref[...], wv_ref[...])
        lat_v = lat_ref[...]
        qhat, rq = _rms_stats(lat_v[:, :Q_LORA].astype(F32))
        dql, dgq = _rms_bwd(dqn, qhat, rq, gq_ref[...])
        khat, rk = _rms_stats(lat_v[:, Q_LORA:Q_LORA + KV_LORA].astype(F32))
        dkl, dgkv = _rms_bwd(dckv, khat, rk, gkv_ref[...])
        dgq_ref[...] += dgq
        dgkv_ref[...] += dgkv
        dlat_ref[...] = jnp.concatenate([dql, dkl, dce[:, KV_LORA:]], axis=1).astype(BF16)

    def row(w):
        return pl.BlockSpec((tm, w), lambda i: (i, 0))

    def full(r, w):
        return pl.BlockSpec((r, w), lambda i: (0, 0))

    return pl.pallas_call(
        body, name=name, grid=(t // tm,),
        in_specs=[row(D), row(D), row(D), row(LAT_PAD), row(Q_LORA), row(Q_LORA), full(1, Q_LORA), full(1, KV_LORA),
                  full(1, HEAD_PAD), full(1, HEAD_PAD), full(Q_LORA, D), full(Q_LORA, D), full(KV_LORA, D),
                  row(HEAD_PAD), row(HEAD_PAD), row(HEAD_PAD)],
        out_specs=[row(LAT_PAD), row(D), row(D), full(1, Q_LORA), full(1, KV_LORA), full(1, HEAD_PAD), full(1, HEAD_PAD)],
        out_shape=[jax.ShapeDtypeStruct((t, LAT_PAD), BF16), jax.ShapeDtypeStruct((t, D), BF16), jax.ShapeDtypeStruct((t, D), BF16),
                   jax.ShapeDtypeStruct((1, Q_LORA), F32), jax.ShapeDtypeStruct((1, KV_LORA), F32),
                   jax.ShapeDtypeStruct((1, HEAD_PAD), F32), jax.ShapeDtypeStruct((1, HEAD_PAD), F32)],
        compiler_params=_params("arbitrary"),
    )(dq, dk, dv, lat, qn, ckv_ext, gq, gkv, ghq, ghk, wq, wk, wv, rc, rs1, rs2)


def _causal_keep(tq):
    r = lax.broadcasted_iota(jnp.int32, (tq, tq), 0)
    c = lax.broadcasted_iota(jnp.int32, (tq, tq), 1)
    return c <= r


def _flash_fwd(q, k, v, *, n_seq, seq, tq, name, hosted=None):
    nq = seq // tq

    def body(q_ref, k_ref, v_ref, o_ref, lse_ref):
        qi = pl.program_id(2)
        qv = q_ref[...]

        def step(j, carry, masked):
            m, l, acc = carry
            kj = k_ref[pl.ds(pl.multiple_of(j * tq, tq), tq), :]
            vj = v_ref[pl.ds(pl.multiple_of(j * tq, tq), tq), :]
            s = _dot_nt(qv, kj) * ATTN_SCALE
            if masked:
                s = jnp.where(_causal_keep(tq), s, NEG)
            m_new = jnp.maximum(m, jnp.max(s, axis=-1, keepdims=True))
            alpha = jnp.exp(m - m_new)
            p = jnp.exp(s - m_new)
            l = alpha * l + jnp.sum(p, axis=-1, keepdims=True)
            acc = alpha * acc + _dot_nn(p.astype(BF16), vj)
            return m_new, l, acc

        init = (jnp.full((tq, 1), NEG, F32), jnp.zeros((tq, 1), F32), jnp.zeros((tq, HEAD_PAD), F32))
        carry = lax.fori_loop(0, qi, lambda j, cr: step(j, cr, False), init)
        m, l, acc = step(qi, carry, True)
        o_ref[...] = (acc / l).astype(BF16)
        lse_ref[...] = jnp.broadcast_to(m + jnp.log(l), (tq, HEAD_PAD))

    qspec = pl.BlockSpec((tq, HEAD_PAD), lambda b, h, i: (b * nq + i, h))
    kspec = pl.BlockSpec((seq, HEAD_PAD), lambda b, h, i: (b, h))
    t = n_seq * seq
    return _call(
        body, name=name, grid=(n_seq, N_HEADS, nq), in_specs=[qspec, kspec, kspec], out_specs=[qspec, qspec],
        out_shape=[jax.ShapeDtypeStruct((t, D), BF16), jax.ShapeDtypeStruct((t, D), F32)], scratch_shapes=[],
        operands=(q, k, v), sem=("parallel", "parallel", "arbitrary"), hosted=hosted)


def _flash_bwd(q, k, v, o, lse, do, *, n_seq, seq, tq, name, hosted=None):
    nq = seq // tq

    def body(q_ref, k_ref, v_ref, o_ref, lse_ref, do_ref, dq_ref, dk_ref, dv_ref, dk_acc, dv_acc):
        j = pl.program_id(2)

        @pl.when(j == 0)
        def _():
            dq_ref[...] = jnp.zeros_like(dq_ref)

        dk_acc[...] = jnp.zeros_like(dk_acc)
        dv_acc[...] = jnp.zeros_like(dv_acc)
        kv = k_ref[...]
        vv = v_ref[...]

        def step(i, masked):
            rows = pl.ds(pl.multiple_of(i * tq, tq), tq)
            qi = q_ref[rows, :]
            doi = do_ref[rows, :]
            delta = jnp.sum(doi.astype(F32) * o_ref[rows, :].astype(F32), axis=-1, keepdims=True)
            s = _dot_nt(qi, kv) * ATTN_SCALE
            p = jnp.exp(s - lse_ref[rows, :][:, :1])
            if masked:
                p = jnp.where(_causal_keep(tq), p, 0.0)
            dv_acc[...] += _dot_tn(p.astype(BF16), doi)
            dp = _dot_nt(doi, vv)
            ds = (p * (dp - delta) * ATTN_SCALE).astype(BF16)
            dk_acc[...] += _dot_tn(ds, qi)
            dq_ref[rows, :] += _dot_nn(ds, kv)

        step(j, True)

        def loop_body(i, carry):
            step(i, False)
            return carry

        lax.fori_loop(j + 1, nq, loop_body, 0)
        dk_ref[...] = dk_acc[...]
        dv_ref[...] = dv_acc[...].astype(BF16)

    full = pl.BlockSpec((seq, HEAD_PAD), lambda b, h, j: (b, h))
    tile = pl.BlockSpec((tq, HEAD_PAD), lambda b, h, j: (b * nq + j, h))
    t = n_seq * seq
    return _call(
        body, name=name, grid=(n_seq, N_HEADS, nq), in_specs=[full, tile, tile, full, full, full],
        out_specs=[full, tile, tile],
        out_shape=[jax.ShapeDtypeStruct((t, D), F32), jax.ShapeDtypeStruct((t, D), F32), jax.ShapeDtypeStruct((t, D), BF16)],
        scratch_shapes=[pltpu.VMEM((tq, HEAD_PAD), F32), pltpu.VMEM((tq, HEAD_PAD), F32)],
        operands=(q, k, v, o, lse, do), sem=("parallel", "parallel", "arbitrary"), hosted=hosted)


CONV_CB = 256


def _shift_down(u, k, row):
    return jnp.where(row >= k, pltpu.roll(u, k, 0), 0.0)


def _shift_up(u, k, row, n):
    return jnp.where(row < n - k, pltpu.roll(u, n - k, 0), 0.0)


def _conv_fwd(conv3, cw, *, n_seq, seq, name):
    def body(c_ref, w_ref, p_ref):
        blk = c_ref[...].astype(F32)
        xc, gb, gc = blk[:, :CONV_CB], blk[:, CONV_CB:2 * CONV_CB], blk[:, 2 * CONV_CB:]
        row = lax.broadcasted_iota(jnp.int32, (seq, CONV_CB), 0)
        u = gc * xc
        z = w_ref[0:1, :] * _shift_down(u, 2, row) + w_ref[1:2, :] * _shift_down(u, 1, row) + w_ref[2:3, :] * u
        p_ref[...] = (gb * z).astype(BF16)

    return pl.pallas_call(
        body, name=name, grid=(n_seq, D // CONV_CB),
        in_specs=[pl.BlockSpec((seq, 3 * CONV_CB), lambda b, j: (b, j)), pl.BlockSpec((3, CONV_CB), lambda b, j: (0, j))],
        out_specs=pl.BlockSpec((seq, CONV_CB), lambda b, j: (b, j)),
        out_shape=jax.ShapeDtypeStruct((n_seq * seq, D), BF16),
        compiler_params=_params("parallel", "parallel"),
    )(conv3, cw)


def _conv_bwd(dp, conv3, cw, *, n_seq, seq, name):
    def body(dp_ref, c_ref, w_ref, dc_ref, dw_ref):
        @pl.when(pl.program_id(1) == 0)
        def _():
            dw_ref[...] = jnp.zeros_like(dw_ref)

        blk = c_ref[...].astype(F32)
        xc, gb, gc = blk[:, :CONV_CB], blk[:, CONV_CB:2 * CONV_CB], blk[:, 2 * CONV_CB:]
        row = lax.broadcasted_iota(jnp.int32, (seq, CONV_CB), 0)
        w0, w1, w2 = w_ref[0:1, :], w_ref[1:2, :], w_ref[2:3, :]
        u = gc * xc
        u1 = _shift_down(u, 1, row)
        u2 = _shift_down(u, 2, row)
        z = w0 * u2 + w1 * u1 + w2 * u
        dpv = dp_ref[...].astype(F32)
        dz = dpv * gb
        du = w2 * dz + w1 * _shift_up(dz, 1, row, seq) + w0 * _shift_up(dz, 2, row, seq)
        dc_ref[...] = jnp.concatenate([du * gc, dpv * z, du * xc], axis=1).astype(BF16)
        dw_ref[0:1, :] += jnp.sum(dz * u2, axis=0, keepdims=True)
        dw_ref[1:2, :] += jnp.sum(dz * u1, axis=0, keepdims=True)
        dw_ref[2:3, :] += jnp.sum(dz * u, axis=0, keepdims=True)

    return pl.pallas_call(
        body, name=name, grid=(D // CONV_CB, n_seq),
        in_specs=[pl.BlockSpec((seq, CONV_CB), lambda j, b: (b, j)), pl.BlockSpec((seq, 3 * CONV_CB), lambda j, b: (b, j)),
                  pl.BlockSpec((3, CONV_CB), lambda j, b: (0, j))],
        out_specs=[pl.BlockSpec((seq, 3 * CONV_CB), lambda j, b: (b, j)), pl.BlockSpec((3, CONV_CB), lambda j, b: (0, j))],
        out_shape=[jax.ShapeDtypeStruct((n_seq * seq, CONV_COLS), BF16), jax.ShapeDtypeStruct((3, D), F32)],
        compiler_params=_params("parallel", "arbitrary"),
    )(dp, conv3, cw)


def _merge_fwd(o, p, gl, bias, x1, wpa, wpc, wout, *, tm, name, hosted=None):
    t = x1.shape[0]

    def body(o_ref, p_ref, gl_ref, b_ref, x_ref, wpa_ref, wpc_ref, wout_ref, x2_ref, mg_ref, ya_ref, yb_ref):
        ya = _dot_nn(o_ref[...], wpa_ref[...])
        yb = _dot_nn(p_ref[...], wpc_ref[...])
        gates = _sigmoid(gl_ref[...].astype(F32) + b_ref[...])
        merged = (gates[:, :D] * ya + gates[:, D:] * yb).astype(BF16)
        ya_ref[...] = ya.astype(BF16)
        yb_ref[...] = yb.astype(BF16)
        mg_ref[...] = merged
        x2_ref[...] = x_ref[...] + _dot_nn(merged, wout_ref[...])

    row = pl.BlockSpec((tm, D), lambda i: (i, 0))
    row2 = pl.BlockSpec((tm, GATE_COLS), lambda i: (i, 0))
    wsp = pl.BlockSpec((D, D), lambda i: (0, 0))
    wide = jax.ShapeDtypeStruct((t, D), BF16)
    return _call(
        body, name=name, grid=(t // tm,),
        in_specs=[row, row, row2, pl.BlockSpec((1, GATE_COLS), lambda i: (0, 0)), row, wsp, wsp, wsp],
        out_specs=[row, row, row, row], out_shape=[jax.ShapeDtypeStruct((t, D), F32), wide, wide, wide], scratch_shapes=[],
        operands=(o, p, gl, bias, x1, wpa, wpc, wout), sem=("parallel",), hosted=hosted)


def _merge_bwd(dx2, ya, yb, gl, bias, wpa, wpc, wout, *, tm, name, hosted=None):
    t = dx2.shape[0]

    def body(dx_ref, ya_ref, yb_ref, gl_ref, b_ref, wpa_ref, wpc_ref, wout_ref,
             dxb_ref, dya_ref, dyb_ref, dgl_ref, do_ref, dp_ref, db_ref):
        @pl.when(pl.program_id(0) == 0)
        def _():
            db_ref[...] = jnp.zeros_like(db_ref)

        dxb = dx_ref[...].astype(BF16)
        dxb_ref[...] = dxb
        dm = _dot_nt(dxb, wout_ref[...])
        gates = _sigmoid(gl_ref[...].astype(F32) + b_ref[...])
        ga, gb = gates[:, :D], gates[:, D:]
        dya = (dm * ga).astype(BF16)
        dyb = (dm * gb).astype(BF16)
        dya_ref[...] = dya
        dyb_ref[...] = dyb
        dgl = jnp.concatenate([dm * ya_ref[...].astype(F32) * ga * (1.0 - ga),
                               dm * yb_ref[...].astype(F32) * gb * (1.0 - gb)], axis=1)
        dgl_ref[...] = dgl.astype(BF16)
        db_ref[...] += jnp.sum(dgl, axis=0, keepdims=True)
        do_ref[...] = _dot_nt(dya, wpa_ref[...]).astype(BF16)
        dp_ref[...] = _dot_nt(dyb, wpc_ref[...]).astype(BF16)

    row = pl.BlockSpec((tm, D), lambda i: (i, 0))
    row2 = pl.BlockSpec((tm, GATE_COLS), lambda i: (i, 0))
    vec2 = pl.BlockSpec((1, GATE_COLS), lambda i: (0, 0))
    wsp = pl.BlockSpec((D, D), lambda i: (0, 0))
    wide = jax.ShapeDtypeStruct((t, D), BF16)
    return _call(
        body, name=name, grid=(t // tm,), in_specs=[row, row, row, row2, vec2, wsp, wsp, wsp],
        out_specs=[row, row, row, row2, row, row, vec2],
        out_shape=[wide, wide, wide, jax.ShapeDtypeStruct((t, GATE_COLS), BF16), wide, wide,
                   jax.ShapeDtypeStruct((1, GATE_COLS), F32)],
        scratch_shapes=[], operands=(dx2, ya, yb, gl, bias, wpa, wpc, wout), sem=("arbitrary",), hosted=hosted)


def _loss_head(y, target, *, tm, name):
    t, d = y.shape

    def body(y_ref, t_ref, dy_ref, loss_ref):
        @pl.when(pl.program_id(0) == 0)
        def _():
            loss_ref[...] = jnp.zeros_like(loss_ref)

        err = y_ref[...] - t_ref[...]
        dy_ref[...] = err * (1.0 / d)
        loss_ref[...] += jnp.sum(jnp.sum(err * err, axis=-1, keepdims=True), axis=0, keepdims=True) * (0.5 / d)

    row = pl.BlockSpec((tm, d), lambda i: (i, 0))
    return pl.pallas_call(
        body, name=name, grid=(t // tm,), in_specs=[row, row], out_specs=[row, pl.BlockSpec((1, 128), lambda i: (0, 0))],
        out_shape=[jax.ShapeDtypeStruct((t, d), F32), jax.ShapeDtypeStruct((1, 128), F32)],
        compiler_params=_params("arbitrary"),
    )(y, target)


def _adamw(w, g, m, v, *, name):
    rows, cols = w.shape
    tr = max([c for c in range(8, 513, 8) if rows % c == 0], default=rows)
    c1 = 1.0 / (1.0 - ADAM_B1 ** ADAM_STEP)
    c2 = 1.0 / (1.0 - ADAM_B2 ** ADAM_STEP)

    def body(w_ref, g_ref, m_ref, v_ref, d_ref, nm_ref, nv_ref):
        gv = g_ref[...]
        nm = ADAM_B1 * m_ref[...] + (1.0 - ADAM_B1) * gv
        nv = ADAM_B2 * v_ref[...] + (1.0 - ADAM_B2) * (gv * gv)
        nm_ref[...] = nm
        nv_ref[...] = nv
        d_ref[...] = -ADAM_LR * ((nm * c1) / (jnp.sqrt(nv * c2) + ADAM_EPS) + ADAM_WD * w_ref[...])

    spec = pl.BlockSpec((tr, cols), lambda i: (i, 0))
    shp = jax.ShapeDtypeStruct((rows, cols), F32)
    return pl.pallas_call(
        body, name=name, grid=(rows // tr,), in_specs=[spec] * 4, out_specs=[spec] * 3, out_shape=[shp] * 3,
        compiler_params=_params("parallel"),
    )(w, g, m, v)


def _place():
    return lax.axis_index("x"), lax.axis_index("y"), lax.axis_index("c")


def _other_chips(x, y):
    return [(1 - x, y), (x, 1 - y), (1 - x, 1 - y)]


def _remote(src, dst, send, recv, dev):
    return pltpu.make_async_remote_copy(src_ref=src, dst_ref=dst, send_sem=send, recv_sem=recv, device_id=dev, device_id_type=MESH)


def _gather_chips_plan(n):
    def start(srcs, dsts, send, recv, local):
        x, y, cc = _place()
        me = 4 * x + 2 * y + cc
        for a in range(n):
            pltpu.make_async_copy(srcs[a], dsts[a].at[me], local.at[a]).start()
            for k, (px, py) in enumerate(_other_chips(x, y)):
                _remote(srcs[a], dsts[a].at[me], send.at[3 * a + k], recv.at[3 * a + k], (px, py, cc)).start()

    def wait(srcs, dsts, send, recv, local):
        x, y, cc = _place()
        me = 4 * x + 2 * y + cc
        for a in range(n):
            for k, (px, py) in enumerate(_other_chips(x, y)):
                _remote(srcs[a], dsts[a].at[4 * px + 2 * py + cc], send.at[3 * a + k], recv.at[3 * a + k], (px, py, cc)).wait_recv()
        for a in range(n):
            for k, (px, py) in enumerate(_other_chips(x, y)):
                _remote(srcs[a], dsts[a].at[me], send.at[3 * a + k], recv.at[3 * a + k], (px, py, cc)).wait_send()
            pltpu.make_async_copy(srcs[a], dsts[a].at[me], local.at[a]).wait()

    return _Plan(start, wait, 3 * n, n)


def _scatter_chips_plan(n):
    def start(srcs, dsts, send, recv, local):
        x, y, cc = _place()
        for a in range(n):
            for k, (px, py) in enumerate(_other_chips(x, y)):
                _remote(srcs[a].at[2 * px + py], dsts[a].at[k], send.at[3 * a + k], recv.at[3 * a + k], (px, py, cc)).start()

    def wait(srcs, dsts, send, recv, local):
        x, y, cc = _place()
        for a in range(n):
            for k, (px, py) in enumerate(_other_chips(x, y)):
                _remote(srcs[a].at[k], dsts[a].at[k], send.at[3 * a + k], recv.at[3 * a + k], (px, py, cc)).wait_recv()
        for a in range(n):
            for k, (px, py) in enumerate(_other_chips(x, y)):
                _remote(srcs[a].at[k], dsts[a].at[k], send.at[3 * a + k], recv.at[3 * a + k], (px, py, cc)).wait_send()

    return _Plan(start, wait, 3 * n, 0)


def _gather_shapes(blocks):
    return [jax.ShapeDtypeStruct((N_DEV,) + b.shape, b.dtype) for b in blocks]


def _scatter_shapes(parts):
    return [jax.ShapeDtypeStruct((3,) + p.shape[1:], p.dtype) for p in parts]


def _gather_sibling_plan(n):
    def start(srcs, dsts, send, recv, local):
        x, y, cc = _place()
        for a in range(n):
            for q in range(4):
                _remote(srcs[a].at[2 * q + cc], dsts[a].at[2 * q + cc], send.at[4 * a + q], recv.at[4 * a + q], (x, y, 1 - cc)).start()

    def wait(srcs, dsts, send, recv, local):
        x, y, cc = _place()
        for a in range(n):
            for q in range(4):
                _remote(srcs[a].at[2 * q + cc], dsts[a].at[2 * q + 1 - cc], send.at[4 * a + q], recv.at[4 * a + q],
                        (x, y, 1 - cc)).wait_recv()
        for a in range(n):
            for q in range(4):
                _remote(srcs[a].at[2 * q + cc], dsts[a].at[2 * q + cc], send.at[4 * a + q], recv.at[4 * a + q],
                        (x, y, 1 - cc)).wait_send()

    return _Plan(start, wait, 4 * n, 0, in_place=True)


def _scatter_sibling_plan(n):
    def start(srcs, dsts, send, recv, local):
        x, y, cc = _place()
        for a in range(n):
            for q in range(4):
                _remote(srcs[a].at[2 * q + 1 - cc], dsts[a].at[q], send.at[4 * a + q], recv.at[4 * a + q], (x, y, 1 - cc)).start()

    def wait(srcs, dsts, send, recv, local):
        x, y, cc = _place()
        for a in range(n):
            for q in range(4):
                _remote(srcs[a].at[q], dsts[a].at[q], send.at[4 * a + q], recv.at[4 * a + q], (x, y, 1 - cc)).wait_recv()
        for a in range(n):
            for q in range(4):
                _remote(srcs[a].at[q], dsts[a].at[q], send.at[4 * a + q], recv.at[4 * a + q], (x, y, 1 - cc)).wait_send()

    return _Plan(start, wait, 4 * n, 0)


def _same_shapes(arrs):
    return [jax.ShapeDtypeStruct(a.shape, a.dtype) for a in arrs]


def _halved_shapes(parts):
    return [jax.ShapeDtypeStruct((4,) + p.shape[1:], p.dtype) for p in parts]


def _run_plan(plan, srcs, out_shapes, *, name):
    n_in, n_out = len(srcs), len(out_shapes)

    def body(*refs):
        h_in, h_out, sems = refs[:n_in], refs[n_in:n_in + n_out], refs[n_in + n_out:]
        plan.start(h_in, h_out, *sems)
        plan.wait(h_in, h_out, *sems)

    return pl.pallas_call(body, name=name, in_specs=[ANY] * n_in, out_specs=[ANY] * n_out, out_shape=list(out_shapes),
                          input_output_aliases={a: a for a in range(n_in)} if plan.in_place else {},
                          scratch_shapes=plan.sems())(*srcs)


def _sum_sibling(p, q, core, *, name):
    _, r, c = p.shape

    def body(core_ref, p_ref, q_ref, o_ref):
        o_ref[...] = (p_ref[...].astype(F32) + q_ref[...].astype(F32)).astype(BF16)

    grid_spec = pltpu.PrefetchScalarGridSpec(
        num_scalar_prefetch=1, grid=(4,),
        in_specs=[pl.BlockSpec((1, r, c), lambda ch, core_ref: (2 * ch + core_ref[0], 0, 0)),
                  pl.BlockSpec((1, r, c), lambda ch, core_ref: (ch, 0, 0))],
        out_specs=pl.BlockSpec((1, r, c), lambda ch, core_ref: (ch, 0, 0)))
    return pl.pallas_call(
        body, name=name, grid_spec=grid_spec, out_shape=jax.ShapeDtypeStruct((4, r, c), BF16),
        compiler_params=_params("parallel"),
    )(core, p, q)


def _sum_chips(s1, r2, chip, *, name):
    _, r, c = s1.shape

    def body(chip_ref, s_ref, r_ref, o_ref):
        acc = s_ref[0].astype(F32)
        for k in range(3):
            acc = acc + r_ref[k].astype(F32)
        o_ref[...] = acc

    grid_spec = pltpu.PrefetchScalarGridSpec(
        num_scalar_prefetch=1, grid=(1,),
        in_specs=[pl.BlockSpec((1, r, c), lambda i, chip_ref: (chip_ref[0], 0, 0)),
                  pl.BlockSpec((3, r, c), lambda i, chip_ref: (0, 0, 0))],
        out_specs=pl.BlockSpec((r, c), lambda i, chip_ref: (0, 0)))
    return pl.pallas_call(
        body, name=name, grid_spec=grid_spec, out_shape=jax.ShapeDtypeStruct((r, c), F32),
        compiler_params=_params("arbitrary"),
    )(chip, s1, r2)


def _sum_adamw(s1, r2, chip, w, m, v, *, name):
    _, r, c = s1.shape
    c1 = 1.0 / (1.0 - ADAM_B1 ** ADAM_STEP)
    c2 = 1.0 / (1.0 - ADAM_B2 ** ADAM_STEP)

    def body(chip_ref, s_ref, r_ref, w_ref, m_ref, v_ref, g_ref, d_ref, nm_ref, nv_ref):
        gv = s_ref[0].astype(F32)
        for k in range(3):
            gv = gv + r_ref[k].astype(F32)
        g_ref[...] = gv
        nm = ADAM_B1 * m_ref[...] + (1.0 - ADAM_B1) * gv
        nv = ADAM_B2 * v_ref[...] + (1.0 - ADAM_B2) * (gv * gv)
        nm_ref[...] = nm
        nv_ref[...] = nv
        d_ref[...] = -ADAM_LR * ((nm * c1) / (jnp.sqrt(nv * c2) + ADAM_EPS) + ADAM_WD * w_ref[...])

    flat = pl.BlockSpec((r, c), lambda i, chip_ref: (0, 0))
    grid_spec = pltpu.PrefetchScalarGridSpec(
        num_scalar_prefetch=1, grid=(1,),
        in_specs=[pl.BlockSpec((1, r, c), lambda i, chip_ref: (chip_ref[0], 0, 0)),
                  pl.BlockSpec((3, r, c), lambda i, chip_ref: (0, 0, 0)), flat, flat, flat],
        out_specs=[flat] * 4)
    return pl.pallas_call(
        body, name=name, grid_spec=grid_spec, out_shape=[jax.ShapeDtypeStruct((r, c), F32)] * 4,
        compiler_params=_params("arbitrary"),
    )(chip, s1, r2, w, m, v)


def _small_exchange(v, *, reduce, name):
    r, c = v.shape

    def body(x_ref, o_ref, *rest):
        if reduce:
            buf_ref, send_sems, recv_sems = rest
        else:
            buf_ref = o_ref
            send_sems, recv_sems = rest
        x, y, cc = _place()
        me = 4 * x + 2 * y + cc

        def peer(k):
            return ((1 - x) if k & 4 else x, (1 - y) if k & 2 else y, (1 - cc) if k & 1 else cc)

        buf_ref[me] = x_ref[...]
        sends = []
        for k in range(1, N_DEV):
            cp = pltpu.make_async_remote_copy(src_ref=x_ref, dst_ref=buf_ref.at[me], send_sem=send_sems.at[k - 1],
                                              recv_sem=recv_sems.at[k - 1], device_id=peer(k), device_id_type=MESH)
            cp.start()
            sends.append(cp)
        for k in range(1, N_DEV):
            px, py, pc = peer(k)
            pltpu.make_async_remote_copy(src_ref=x_ref, dst_ref=buf_ref.at[4 * px + 2 * py + pc], send_sem=send_sems.at[k - 1],
                                         recv_sem=recv_sems.at[k - 1], device_id=peer(k), device_id_type=MESH).wait_recv()
        for cp in sends:
            cp.wait_send()
        if reduce:
            acc = buf_ref[0]
            for s in range(1, N_DEV):
                acc = acc + buf_ref[s]
            o_ref[...] = acc

    vm = pl.BlockSpec(memory_space=pltpu.VMEM)
    sems = [pltpu.SemaphoreType.DMA((N_DEV - 1,)), pltpu.SemaphoreType.DMA((N_DEV - 1,))]
    if reduce:
        out_shape, scratch = jax.ShapeDtypeStruct((r, c), F32), [pltpu.VMEM((N_DEV, r, c), F32)] + sems
    else:
        out_shape, scratch = jax.ShapeDtypeStruct((N_DEV, r, c), F32), sems
    return pl.pallas_call(body, name=name, in_specs=[vm], out_specs=vm, out_shape=out_shape, scratch_shapes=scratch)(v)


def _rows(a):
    return a.reshape(-1, D)


def _pad_cols(a, to):
    return jnp.pad(a, ((0, 0), (0, to - a.shape[1])))


def _pack_weights(w):
    parts = {
        "w_inT": jnp.pad(w["w_in"].T, ((0, IN_SHARD_PAD - IN_SHARD), (0, 0))),
        "w_uq": _rows(_pad_cols(w["w_uq"], HEAD_PAD)), "w_uk": _rows(_pad_cols(w["w_uk"], HEAD_PAD)),
        "w_uv": _rows(_pad_cols(w["w_uv"], HEAD_PAD)), "w_pa": _rows(w["w_proj_attn"]),
        "w_pc": w["w_proj_conv"], "w_out": w["w_out"],
    }
    return [jnp.concatenate([parts[n].astype(BF16) for n, _ in group], axis=0) for group in PACK]


def _cols_from_shards(gs, name, rows):
    idx, off, r = PACK_OFF[name]
    return gs[idx][:, off:off + r].reshape(N_DEV, rows, HEAD_PAD).transpose(1, 0, 2).reshape(rows, N_DEV * HEAD_PAD)


def _rows_from_shards(gs, name, keep=None):
    idx, off, r = PACK_OFF[name]
    keep = r if keep is None else keep
    return gs[idx][:, off:off + keep].reshape(N_DEV * keep, D)


def _rope_placement():
    i = lax.broadcasted_iota(jnp.int32, (HEAD_PAD, D), 0)
    j = lax.broadcasted_iota(jnp.int32, (HEAD_PAD, D), 1)
    return ((i < 2 * ROPE_HALF) & (j % HEAD_PAD == NOPE + i)).astype(BF16)


def _unpack_weights(g):
    w_inT = _rows_from_shards(g, "w_inT", IN_SHARD)
    lat_rows = Q_LORA + KV_LORA + 2 * ROPE_HALF
    conv = w_inT[lat_rows:lat_rows + CONV_COLS].reshape(3, D // CONV_CB, CONV_CB, D).transpose(1, 0, 2, 3).reshape(CONV_COLS, D)
    wpa = _cols_from_shards(g, "w_pa", 512).reshape(N_HEADS, NOPE, D)
    return {
        "latT": jnp.pad(w_inT[:lat_rows], ((0, LAT_PAD - lat_rows), (0, 0))),
        "convT": conv, "gateT": w_inT[lat_rows + CONV_COLS:],
        "wq": _cols_from_shards(g, "w_uq", Q_LORA),
        "wk": jnp.concatenate([_cols_from_shards(g, "w_uk", KV_LORA), _rope_placement()], axis=0),
        "wv": _cols_from_shards(g, "w_uv", KV_LORA),
        "wpa": jnp.pad(wpa, ((0, 0), (0, HEAD_PAD - NOPE), (0, 0))).reshape(D, D),
        "wpc": _rows_from_shards(g, "w_pc"), "wout": _rows_from_shards(g, "w_out"),
    }


def _shards_from_cols(a):
    rows = a.shape[0]
    return a.reshape(rows, N_DEV, HEAD_PAD).transpose(1, 0, 2).reshape(N_DEV, rows * HEAD_PAD // D, D)


def _pack_grads(gw):
    lat_rows = Q_LORA + KV_LORA + 2 * ROPE_HALF
    conv = gw["convT"].reshape(D // CONV_CB, 3, CONV_CB, D).transpose(1, 0, 2, 3).reshape(CONV_COLS, D)
    w_inT = jnp.concatenate([gw["latT"][:lat_rows], conv, gw["gateT"]], axis=0).reshape(N_DEV, IN_SHARD, D)
    wpa = gw["wpa"].reshape(N_HEADS, HEAD_PAD, D)[:, :NOPE].reshape(N_HEADS * NOPE, D)
    parts = {}
    parts.update({
        "w_inT": jnp.pad(w_inT, ((0, 0), (0, IN_SHARD_PAD - IN_SHARD), (0, 0))),
        "w_uq": _shards_from_cols(gw["wq"]), "w_uk": _shards_from_cols(gw["wk"][:KV_LORA]),
        "w_uv": _shards_from_cols(gw["wv"][:KV_LORA]), "w_pa": _shards_from_cols(wpa),
        "w_pc": gw["wpc"].reshape(N_DEV, D // N_DEV, D), "w_out": gw["wout"].reshape(N_DEV, D // N_DEV, D),
    })
    return [jnp.concatenate([parts[n] for n, _ in group], axis=1) for group in PACK]


def _unpack_grads(mines):
    def seg(name, keep=None):
        idx, off, r = PACK_OFF[name]
        return mines[idx][off:off + (r if keep is None else keep)]

    return {
        "w_in": seg("w_inT", IN_SHARD).T,
        "w_uq": seg("w_uq").reshape(Q_LORA, HEAD_PAD)[:, :QK_DIM],
        "w_uk": seg("w_uk").reshape(KV_LORA, HEAD_PAD)[:, :NOPE],
        "w_uv": seg("w_uv").reshape(KV_LORA, HEAD_PAD)[:, :NOPE],
        "w_proj_attn": seg("w_pa").reshape(512, HEAD_PAD),
        "w_proj_conv": seg("w_pc"), "w_out": seg("w_out"),
    }


def _rope_tables(positions):
    inv_freq = 1.0 / (ROPE_THETA ** (jnp.arange(ROPE_HALF, dtype=F32) / ROPE_HALF))
    ang = positions.reshape(-1).astype(F32)[:, None] * inv_freq
    cos, sin = jnp.cos(ang), jnp.sin(ang)
    t = ang.shape[0]
    zero = jnp.zeros((t, ROPE_HALF), F32)
    head = jnp.ones((t, NOPE), F32)
    tail = jnp.zeros((t, HEAD_PAD - QK_DIM), F32)
    nohead = jnp.zeros((t, NOPE), F32)
    rc = jnp.concatenate([head, cos, cos, tail], axis=1)
    rs1 = jnp.concatenate([nohead, -sin, zero, tail], axis=1)
    rs2 = jnp.concatenate([nohead, zero, sin, tail], axis=1)
    return rc, rs1, rs2


def _local_step(x, positions, target, conv_w, small, ex):
    n_seq, seq, d = x.shape
    t = n_seq * seq
    x0 = x.reshape(t, d)
    tgt = target.reshape(t, d)
    rc, rs1, rs2 = _rope_tables(positions)
    ghq = _pad_cols(small["q_head_norm"], HEAD_PAD)
    ghk = _pad_cols(small["k_head_norm"], HEAD_PAD)
    TM, HC, TQ = 1024, 256, 512

    def mm(*args, hosted=None, **kw):
        res = _mm(*args, hosted=hosted, **kw)
        return res if hosted is not None else (res, None)

    def wgrad(a, b, name, tm=None, hosted=None):
        return mm(a, b, mode="tn", out_dtype=BF16, tm=tm or a.shape[1], tn=b.shape[1], tk=512, name=name, hosted=hosted)

    def ffn_fwd(xin, gain, wg, wu, wd, tag, host_up, host_down):
        (h, a, b, s), got_up = _ffn_up(xin, gain, wg, wu, tm=TM, hc=HC, name=tag + "_up", hosted=host_up)
        xout, got_down = mm(s, wd, mode="nn", out_dtype=F32, tm=512, tn=D, tk=DFF, name=tag + "_down", add=xin, scale=0.5,
                            hosted=host_down)
        return xout, (h, a, b, s), got_up, got_down

    f1g, f1u, f1d = ex.gather_now("ffn1")
    x1, (h1, a1, b1, s1), got_in, got_misc = ffn_fwd(x0, small["ffn1_norm"], f1g, f1u, f1d, "ffn1",
                                                     ex.gather_chips("mix_in"), ex.gather_chips("mix_misc"))
    hm, got = _rms_fwd(x1, small["mix_norm"], tm=TM, name="mix_norm_fwd", hosted=ex.gather_sibling(got_in, got_misc))
    W = ex.mix_weights(got)
    lat = _mm(hm, W["latT"], mode="nt", out_dtype=BF16, tm=TM, tn=LAT_PAD, tk=D, name="proj_lat")
    conv3 = _mm(hm, W["convT"], mode="nt", out_dtype=BF16, tm=TM, tn=CONV_COLS // 2, tk=D, name="proj_conv")
    gl = _mm(hm, W["gateT"], mode="nt", out_dtype=BF16, tm=TM, tn=GATE_COLS // 2, tk=D, name="proj_gate")
    q, k, v, qn, ckv = _mla_prep_fwd(lat, small["q_a_norm"], small["kv_a_norm"], ghq, ghk, W["wq"], W["wk"], W["wv"], rc, rs1, rs2,
                                     tm=512, name="mla_prep_fwd")
    (o, lse), got = _flash_fwd(q, k, v, n_seq=n_seq, seq=seq, tq=TQ, name="attn_fwd", hosted=ex.gather_chips("ffn2"))
    p = _conv_fwd(conv3, conv_w, n_seq=n_seq, seq=seq, name="conv_fwd")
    (x2, merged, ya, yb), got = _merge_fwd(o, p, gl, small["gate_bias"], x1, W["wpa"], W["wpc"], W["wout"], tm=512, name="merge_fwd",
                                           hosted=ex.gather_sibling(got))
    f2g, f2u, f2d = ex.ffn_weights(got)
    y, (h2, a2, b2, s2), _, _ = ffn_fwd(x2, small["ffn2_norm"], f2g, f2u, f2d, "ffn2", None, None)
    dy, loss_row = _loss_head(y, tgt, tm=TM, name="loss_head")

    gw, gs = {}, {}
    (dyb2, da2, db2), _ = _ffn_down_bwd(dy, a2, b2, f2d, tm=TM, hc=HC, name="ffn2_down_bwd")
    (dx2, gs["ffn2_norm"]), _ = _ffn_up_bwd(da2, db2, f2g, f2u, x2, small["ffn2_norm"], dy, tm=512, name="ffn2_up_bwd")
    ffn2_grads = [wgrad(da2, h2, "ffn2_dwg", tm=DFF // 2)[0], wgrad(db2, h2, "ffn2_dwu", tm=DFF // 2)[0],
                  wgrad(s2, dyb2, "ffn2_dwd", tm=DFF // 2)[0]]

    (dx2b, dya, dyb, dgl, do, dp, gs["gate_bias"]), got = _merge_bwd(
        dx2, ya, yb, gl, small["gate_bias"], W["wpa"], W["wpc"], W["wout"], tm=512, name="merge_bwd",
        hosted=ex.scatter_sibling("ffn2", ffn2_grads))
    ex.scatter_sibling_done("ffn2", got)
    gw["wout"] = wgrad(merged, dx2b, "dw_out")[0]
    gw["wpa"] = wgrad(o, dya, "dw_pa")[0]
    gw["wpc"] = wgrad(p, dyb, "dw_pc")[0]
    dconv3, dconv_w = _conv_bwd(dp, conv3, conv_w, n_seq=n_seq, seq=seq, name="conv_bwd")
    (dq, dk, dv), got = _flash_bwd(q, k, v, o, lse, do, n_seq=n_seq, seq=seq, tq=TQ, name="attn_bwd",
                                   hosted=ex.scatter_chips("ffn2"))
    ex.scatter_chips_done("ffn2", got)
    dlat, dqp, dkp, gs["q_a_norm"], gs["kv_a_norm"], dghq, dghk = _mla_prep_bwd(
        dq, dk, dv, lat, qn, ckv, small["q_a_norm"], small["kv_a_norm"], ghq, ghk, W["wq"], W["wk"], W["wv"], rc, rs1, rs2,
        tm=512, name="mla_prep_bwd")
    gs["q_head_norm"], gs["k_head_norm"] = dghq[:, :QK_DIM], dghk[:, :QK_DIM]
    gw["wq"] = wgrad(qn, dqp, "dw_uq")[0]
    gw["wk"] = wgrad(ckv, dkp, "dw_uk")[0]
    gw["wv"] = wgrad(ckv, dv, "dw_uv")[0]
    gw["convT"] = wgrad(dconv3, hm, "dw_conv", tm=CONV_COLS // 2)[0]
    gw["gateT"] = wgrad(dgl, hm, "dw_gate")[0]
    gw["latT"] = wgrad(dlat, hm, "dw_lat")[0]
    ex.scatter_sibling_now("mix", gw)
    (dx1, gs["mix_norm"]), got = _proj_bwd(dlat, dconv3, dgl, W["latT"], W["convT"], W["gateT"], x1, small["mix_norm"], dx2,
                                           tm=512, name="proj_bwd", hosted=ex.scatter_chips("mix_in"))
    ex.scatter_chips_done("mix_in", got)

    (dyb1, da1, db1), got = _ffn_down_bwd(dx1, a1, b1, f1d, tm=TM, hc=HC, name="ffn1_down_bwd", hosted=ex.scatter_chips("mix_misc"))
    ex.scatter_chips_done("mix_misc", got)
    ex.scatter_sibling_now("ffn1_d", [wgrad(s1, dyb1, "ffn1_dwd", tm=DFF // 2)[0]])
    dwg, got = wgrad(da1, h1, "ffn1_dwg", tm=DFF // 2, hosted=ex.scatter_chips("ffn1_d"))
    ex.scatter_chips_done("ffn1_d", got)
    ex.scatter_sibling_now("ffn1_g", [dwg])
    dwu, got = wgrad(db1, h1, "ffn1_dwu", tm=DFF // 2, hosted=ex.scatter_chips("ffn1_g"))
    ex.scatter_chips_done("ffn1_g", got)
    ex.scatter_sibling_now("ffn1_u", [dwu])
    (dx0, gs["ffn1_norm"]), got = _ffn_up_bwd(da1, db1, f1g, f1u, x0, small["ffn1_norm"], dx1, tm=512, name="ffn1_up_bwd",
                                              hosted=ex.scatter_chips("ffn1_u"))
    ex.scatter_chips_done("ffn1_u", got)
    return loss_row, dx0.reshape(n_seq, seq, d), dconv_w, gs


class _MeshExchange:
    def __init__(self, w, core, chip):
        self.w, self.core, self.chip = w, core, chip
        self.partial, self.received, self._packed = {}, {}, None

    def _blocks(self, group):
        w = self.w
        if group.startswith("ffn"):
            return [w[group + "_w_gate"].T.astype(BF16), w[group + "_w_up"].T.astype(BF16), w[group + "_w_down"].astype(BF16)]
        if self._packed is None:
            self._packed = _pack_weights(w)
        return [self._packed[0 if group == "mix_in" else 1]]

    def gather_chips(self, group):
        blocks = self._blocks(group)
        return _gather_chips_plan(len(blocks)), blocks, _gather_shapes(blocks)

    def gather_sibling(self, *gots):
        half = [g for got in gots for g in got]
        return _gather_sibling_plan(len(half)), half, _same_shapes(half)

    def gather_now(self, group):
        plan, blocks, shapes = self.gather_chips(group)
        half = list(_run_plan(plan, blocks, shapes, name="gather_%s_chips" % group))
        return self.ffn_weights(_run_plan(_gather_sibling_plan(len(half)), half, _same_shapes(half), name="gather_%s_sibling" % group))

    def ffn_weights(self, got):
        return [a.reshape(DFF, D) for a in got]

    def mix_weights(self, got):
        return _unpack_weights(list(got))

    def _parts(self, group, grads):
        if group == "mix":
            return _pack_grads(grads), ["mix_in", "mix_misc"]
        parts = [g.reshape(N_DEV, -1, D) for g in grads]
        return parts, ([group] if len(parts) == 1 else None)

    def scatter_sibling(self, group, grads):
        self._sent, self._names = self._parts(group, grads)
        return _scatter_sibling_plan(len(self._sent)), self._sent, _halved_shapes(self._sent)

    def scatter_sibling_done(self, group, got):
        sums = [_sum_sibling(p, q, self.core, name="sum_%s_sibling_%d" % (group, i)) for i, (p, q) in enumerate(zip(self._sent, got))]
        if self._names is None:
            self.partial[group] = sums
        else:
            for n, s in zip(self._names, sums):
                self.partial[n] = [s]

    def scatter_sibling_now(self, group, grads):
        plan, parts, shapes = self.scatter_sibling(group, grads)
        self.scatter_sibling_done(group, _run_plan(plan, parts, shapes, name="scatter_%s_sibling" % group))

    def scatter_chips(self, group):
        s1 = self.partial[group]
        return _scatter_chips_plan(len(s1)), s1, _scatter_shapes(s1)

    def scatter_chips_done(self, group, got):
        self.received[group] = list(got)


SMALL_NAMES = ("ffn1_norm", "mix_norm", "gate_bias", "q_a_norm", "kv_a_norm", "q_head_norm", "k_head_norm", "ffn2_norm")
SMALL_SLOTS = {"ffn1_norm": 1024, "mix_norm": 1024, "gate_bias": 2048, "q_a_norm": 384, "kv_a_norm": 256, "q_head_norm": 128,
               "k_head_norm": 128, "ffn2_norm": 1024, "conv_w": 3072, "loss": 128}
COLUMN_MAJOR = ("w_in", "w_uq", "w_uk", "w_uv")
WEIGHT_NAMES = ("ffn1_norm", "ffn1_w_gate", "ffn1_w_up", "ffn1_w_down", "mix_norm", "w_in", "gate_bias", "q_a_norm", "w_uq",
                "kv_a_norm", "w_uk", "w_uv", "q_head_norm", "k_head_norm", "w_proj_attn", "conv_w", "w_proj_conv", "w_out",
                "ffn2_norm", "ffn2_w_gate", "ffn2_w_up", "ffn2_w_down")


def _step(x, positions, loss_target, w, m, v):
    xi, yi, ci = _place()
    core = ci.astype(jnp.int32).reshape(1)
    chip = (2 * xi + yi).astype(jnp.int32).reshape(1)
    me = 4 * xi + 2 * yi + ci

    cw_all = _small_exchange(jnp.pad(w["conv_w"], ((0, 5), (0, 0))), reduce=False, name="gather_conv_w")
    conv_w = cw_all[:, :3].transpose(1, 0, 2).reshape(3, D)
    small = {n: w[n].reshape(1, -1) for n in SMALL_NAMES}
    ex = _MeshExchange(w, core, chip)

    loss_row, grad_x, dconv_w, gs = _local_step(x, positions, loss_target, conv_w, small, ex)

    grads, deltas, new_m, new_v = {}, {}, {}, {}
    where = {"ffn1_w_gate": ("ffn1_g", 0), "ffn1_w_up": ("ffn1_u", 0), "ffn1_w_down": ("ffn1_d", 0),
             "ffn2_w_gate": ("ffn2", 0), "ffn2_w_up": ("ffn2", 1), "ffn2_w_down": ("ffn2", 2)}
    for n, (group, i) in where.items():
        transposed = not n.endswith("down")
        wv, mv, vv = (a[n].T if transposed else a[n] for a in (w, m, v))
        res = _sum_adamw(ex.partial[group][i], ex.received[group][i], chip, wv, mv, vv, name="adamw_" + n)
        grads[n], deltas[n], new_m[n], new_v[n] = (r.T if transposed else r for r in res)
    grads.update(_unpack_grads([_sum_chips(ex.partial[g][0], ex.received[g][0], chip, name="sum_%s_chips" % g)
                                for g in ("mix_in", "mix_misc")]))

    pieces = [_pad_cols(gs[n], SMALL_SLOTS[n]) for n in SMALL_NAMES] + [dconv_w.reshape(1, 3 * D), loss_row]
    total = _small_exchange(jnp.concatenate(pieces, axis=1).reshape(-1, 128), reduce=True, name="reduce_small").reshape(-1)
    off = 0
    for n in SMALL_NAMES:
        grads[n] = total[off:off + w[n].shape[0]]
        off += SMALL_SLOTS[n]
    conv_full = total[off:off + 3 * D].reshape(3, D)
    grads["conv_w"] = lax.dynamic_slice(conv_full, (0, me * HEAD_PAD), (3, HEAD_PAD))
    loss = total[off + 3 * D]

    for n in WEIGHT_NAMES:
        if n in deltas:
            continue
        shape = w[n].shape
        if n in COLUMN_MAJOR:
            ops = [a.T for a in (w[n], grads[n], m[n], v[n])]
            deltas[n], new_m[n], new_v[n] = (r.T for r in _adamw(*ops, name="adamw_" + n))
            continue
        if len(shape) == 1:
            view = (-1, 128) if shape[0] % 128 == 0 else (1, shape[0])
        else:
            view = shape
        dlt, nm, nv = _adamw(w[n].reshape(view), grads[n].reshape(view), m[n].reshape(view), v[n].reshape(view), name="adamw_" + n)
        deltas[n], new_m[n], new_v[n] = dlt.reshape(shape), nm.reshape(shape), nv.reshape(shape)
    return (loss, grad_x, *[grads[n] for n in WEIGHT_NAMES], *[deltas[n] for n in WEIGHT_NAMES],
            *[new_m[n] for n in WEIGHT_NAMES], *[new_v[n] for n in WEIGHT_NAMES])


def kernel(x, positions, ffn1_norm, ffn1_w_gate, ffn1_w_up, ffn1_w_down, mix_norm, w_in, gate_bias, q_a_norm, w_uq, kv_a_norm, w_uk, w_uv, q_head_norm, k_head_norm, w_proj_attn, conv_w, w_proj_conv, w_out, ffn2_norm, ffn2_w_gate, ffn2_w_up, ffn2_w_down, loss_target, m_ffn1_norm, m_ffn1_w_gate, m_ffn1_w_up, m_ffn1_w_down, m_mix_norm, m_w_in, m_gate_bias, m_q_a_norm, m_w_uq, m_kv_a_norm, m_w_uk, m_w_uv, m_q_head_norm, m_k_head_norm, m_w_proj_attn, m_conv_w, m_w_proj_conv, m_w_out, m_ffn2_norm, m_ffn2_w_gate, m_ffn2_w_up, m_ffn2_w_down, v_ffn1_norm, v_ffn1_w_gate, v_ffn1_w_up, v_ffn1_w_down, v_mix_norm, v_w_in, v_gate_bias, v_q_a_norm, v_w_uq, v_kv_a_norm, v_w_uk, v_w_uv, v_q_head_norm, v_k_head_norm, v_w_proj_attn, v_conv_w, v_w_proj_conv, v_w_out, v_ffn2_norm, v_ffn2_w_gate, v_ffn2_w_up, v_ffn2_w_down):
    given = dict(locals())
    w = {n: given[n] for n in WEIGHT_NAMES}
    m = {n: given["m_" + n] for n in WEIGHT_NAMES}
    v = {n: given["v_" + n] for n in WEIGHT_NAMES}
    return _step(x, positions, loss_target, w, m, v)
```

```python
import functools

import jax
import jax.numpy as jnp
from jax import lax
from jax.experimental import pallas as pl
from jax.experimental.pallas import tpu as pltpu

F32 = jnp.float32
BF16 = jnp.bfloat16
MESH = pl.DeviceIdType.MESH
ANY = pl.BlockSpec(memory_space=pl.ANY)

N_DEV = 8
D = 1024
DFF = 2816
N_HEADS = 8
HEAD_PAD = 128
QK_DIM = 96
NOPE = 64
ROPE_HALF = 16
Q_LORA = 384
KV_LORA = 256
LAT_PAD = 768
CONV_COLS = 3072
GATE_COLS = 2048
IN_DIM = 5792
IN_SHARD = IN_DIM // N_DEV
IN_SHARD_PAD = 736
FF_SHARD = DFF // N_DEV
ROPE_THETA = 10000.0
NORM_EPS = 1e-6
ATTN_SCALE = QK_DIM ** -0.5
NEG = -1e30

ADAM_LR, ADAM_B1, ADAM_B2, ADAM_EPS, ADAM_WD, ADAM_STEP = 0.001, 0.9, 0.999, 1e-08, 0.01, 10

PACK = ((("w_inT", IN_SHARD_PAD),), (("w_uq", 48), ("w_uk", 32), ("w_uv", 32), ("w_pa", 64), ("w_pc", 128), ("w_out", 128)))
PACK_OFF = {}
for _i, _group in enumerate(PACK):
    _o = 0
    for _n, _r in _group:
        PACK_OFF[_n] = (_i, _o, _r)
        _o += _r

VMEM_LIMIT = 56 * 1024 * 1024


def _params(*sem):
    return pltpu.CompilerParams(dimension_semantics=sem if sem else None, vmem_limit_bytes=VMEM_LIMIT)


class _Plan:
    def __init__(self, start, wait, n_remote, n_local, in_place=False):
        self.start, self.wait, self.n_remote, self.n_local, self.in_place = start, wait, n_remote, n_local, in_place

    def sems(self):
        return [pltpu.SemaphoreType.DMA((self.n_remote,)), pltpu.SemaphoreType.DMA((self.n_remote,)),
                pltpu.SemaphoreType.DMA((max(self.n_local, 1),))]


def _call(body, *, name, grid, in_specs, out_specs, out_shape, scratch_shapes, operands, sem, hosted=None):
    if hosted is None:
        outs = pl.pallas_call(body, name=name, grid=grid, in_specs=in_specs, out_specs=out_specs, out_shape=out_shape,
                              scratch_shapes=scratch_shapes, compiler_params=_params(*sem))(*operands)
        return outs, None
    plan, srcs, h_shapes = hosted
    n_in, n_out, n_scr, nh_in, nh_out = len(in_specs), len(out_specs), len(scratch_shapes), len(srcs), len(h_shapes)
    aliases = {n_in + a: n_out + a for a in range(nh_in)} if plan.in_place else {}

    def full_body(*refs):
        ins, refs = refs[:n_in], refs[n_in:]
        h_in, refs = refs[:nh_in], refs[nh_in:]
        outs, refs = refs[:n_out], refs[n_out:]
        h_out, refs = refs[:nh_out], refs[nh_out:]
        scr, sems = refs[:n_scr], refs[n_scr:]
        ids = [pl.program_id(ax) for ax in range(len(grid))]
        first = functools.reduce(jnp.logical_and, [i == 0 for i in ids])
        last = functools.reduce(jnp.logical_and, [i == g - 1 for i, g in zip(ids, grid)])

        @pl.when(first)
        def _():
            plan.start(h_in, h_out, *sems)

        body(*ins, *outs, *scr)

        @pl.when(last)
        def _():
            plan.wait(h_in, h_out, *sems)

    res = pl.pallas_call(
        full_body, name=name, grid=grid, in_specs=list(in_specs) + [ANY] * nh_in, out_specs=list(out_specs) + [ANY] * nh_out,
        out_shape=list(out_shape) + list(h_shapes), scratch_shapes=list(scratch_shapes) + plan.sems(),
        input_output_aliases=aliases, compiler_params=_params(*(["arbitrary"] * len(grid))),
    )(*operands, *srcs)
    return res[:n_out], res[n_out:]


def _dot_nn(a, b):
    return lax.dot_general(a, b, (((1,), (0,)), ((), ())), preferred_element_type=F32)


def _dot_nt(a, b):
    return lax.dot_general(a, b, (((1,), (1,)), ((), ())), preferred_element_type=F32)


def _dot_tn(a, b):
    return lax.dot_general(a, b, (((0,), (0,)), ((), ())), preferred_element_type=F32)


def _sigmoid(x):
    return 1.0 / (1.0 + jnp.exp(-x))


def _rms_stats(x):
    r = lax.rsqrt(jnp.mean(x * x, axis=-1, keepdims=True) + NORM_EPS)
    return x * r, r


ROWS_WIDE = 16
ROWS_NARROW = 32
MM_ROWS = 256


def _row_chunks(n_rows, rows, fn, unrolled=False):
    if unrolled:
        for c in range(n_rows // rows):
            fn(slice(c * rows, (c + 1) * rows))
        return

    def step(c, carry):
        fn(pl.ds(pl.multiple_of(c * rows, rows), rows))
        return carry

    lax.fori_loop(0, n_rows // rows, step, 0)


def _rms_bwd(dy, xhat, r, g):
    dg = jnp.sum(dy * xhat, axis=0, keepdims=True)
    dxh = dy * g
    dx = r * (dxh - xhat * jnp.mean(dxh * xhat, axis=-1, keepdims=True))
    return dx, dg


def _mm(a, b, *, mode, out_dtype, tm, tn, tk, name, add=None, scale=1.0, hosted=None):
    if mode == "nn":
        (m, k), (_, n) = a.shape, b.shape
    elif mode == "nt":
        (m, k), (n, _) = a.shape, b.shape
    else:
        (k, m), (_, n) = a.shape, b.shape
    assert m % tm == 0 and n % tn == 0 and k % tk == 0, (name, m, n, k, tm, tn, tk)
    nk = k // tk
    dot = {"nn": _dot_nn, "nt": _dot_nt, "tn": _dot_tn}[mode]
    a_spec = pl.BlockSpec((tk, tm), lambda i, j, kk: (kk, i)) if mode == "tn" else pl.BlockSpec((tm, tk), lambda i, j, kk: (i, kk))
    b_spec = pl.BlockSpec((tn, tk), lambda i, j, kk: (j, kk)) if mode == "nt" else pl.BlockSpec((tk, tn), lambda i, j, kk: (kk, j))
    o_spec = pl.BlockSpec((tm, tn), lambda i, j, kk: (i, j))
    has_add = add is not None

    def finish(prod, c_ref, o_ref):
        if scale != 1.0:
            prod = prod * scale
        o_ref[...] = ((c_ref[...] + prod) if has_add else prod).astype(out_dtype)

    def body(*refs):
        a_ref, b_ref = refs[:2]
        c_ref = refs[2] if has_add else None
        o_ref = refs[3] if has_add else refs[2]
        if nk == 1:
            finish(dot(a_ref[...], b_ref[...]), c_ref, o_ref)
            return
        acc_ref = refs[-1]
        kk = pl.program_id(2)

        @pl.when(kk == 0)
        def _():
            acc_ref[...] = jnp.zeros_like(acc_ref)

        acc_ref[...] += dot(a_ref[...], b_ref[...])

        @pl.when(kk == nk - 1)
        def _():
            finish(acc_ref[...], c_ref, o_ref)

    operands = (a, b, add) if has_add else (a, b)
    in_specs = [a_spec, b_spec] + ([o_spec] if has_add else [])
    (out,), got = _call(
        body, name=name, grid=(m // tm, n // tn, nk), in_specs=in_specs, out_specs=[o_spec],
        out_shape=[jax.ShapeDtypeStruct((m, n), out_dtype)], scratch_shapes=[pltpu.VMEM((tm, tn), F32)] if nk > 1 else [],
        operands=operands, sem=("parallel", "parallel", "arbitrary"), hosted=hosted)
    return out if hosted is None else (out, got)


def _rms_fwd(x, g, *, tm, name, hosted=None):
    t, d = x.shape

    def body(x_ref, g_ref, h_ref):
        xhat, _ = _rms_stats(x_ref[...])
        h_ref[...] = (xhat * g_ref[...]).astype(BF16)

    (h,), got = _call(
        body, name=name, grid=(t // tm,),
        in_specs=[pl.BlockSpec((tm, d), lambda i: (i, 0)), pl.BlockSpec((1, d), lambda i: (0, 0))],
        out_specs=[pl.BlockSpec((tm, d), lambda i: (i, 0))], out_shape=[jax.ShapeDtypeStruct((t, d), BF16)], scratch_shapes=[],
        operands=(x, g), sem=("parallel",), hosted=hosted)
    return h, got


def _rms_bwd_res(dh, x, g, dres, *, tm, name):
    t, d = x.shape

    def body(dh_ref, x_ref, g_ref, dres_ref, dx_ref, dg_ref):
        xhat, r = _rms_stats(x_ref[...])
        dx, dg = _rms_bwd(dh_ref[...], xhat, r, g_ref[...])
        dx_ref[...] = dres_ref[...] + dx

        @pl.when(pl.program_id(0) == 0)
        def _():
            dg_ref[...] = jnp.zeros_like(dg_ref)

        dg_ref[...] += dg

    row = pl.BlockSpec((tm, d), lambda i: (i, 0))
    vec = pl.BlockSpec((1, d), lambda i: (0, 0))
    return pl.pallas_call(
        body, name=name, grid=(t // tm,), in_specs=[row, row, vec, row], out_specs=[row, vec],
        out_shape=[jax.ShapeDtypeStruct((t, d), F32), jax.ShapeDtypeStruct((1, d), F32)],
        compiler_params=_params("arbitrary"),
    )(dh, x, g, dres)


def _ffn_fwd(x, g, wgT, wuT, wd, *, tm, hc, name, hosted=None):
    t, d = x.shape
    nj = DFF // hc

    def body(x_ref, g_ref, wg_ref, wu_ref, wd_ref, xo_ref, h_ref, a_ref, b_ref, acc_ref):
        j = pl.program_id(1)

        @pl.when(j == 0)
        def _():
            xhat, _ = _rms_stats(x_ref[...])
            h_ref[...] = (xhat * g_ref[...]).astype(BF16)
            acc_ref[...] = jnp.zeros_like(acc_ref)

        h = h_ref[...]
        a = _dot_nt(h, wg_ref[...])
        b = _dot_nt(h, wu_ref[...])
        a_ref[...] = a.astype(BF16)
        b_ref[...] = b.astype(BF16)
        s = (a * _sigmoid(a) * b).astype(BF16)
        acc_ref[...] += _dot_nn(s, wd_ref[...])

        @pl.when(j == nj - 1)
        def _():
            xo_ref[...] = x_ref[...] + 0.5 * acc_ref[...]

    row = pl.BlockSpec((tm, d), lambda i, j: (i, 0))
    vec = pl.BlockSpec((1, d), lambda i, j: (0, 0))
    wsp = pl.BlockSpec((hc, d), lambda i, j: (j, 0))
    hid = pl.BlockSpec((tm, hc), lambda i, j: (i, j))
    return _call(
        body, name=name, grid=(t // tm, nj), in_specs=[row, vec, wsp, wsp, wsp], out_specs=[row, row, hid, hid],
        out_shape=[jax.ShapeDtypeStruct((t, d), F32), jax.ShapeDtypeStruct((t, d), BF16),
                   jax.ShapeDtypeStruct((t, DFF), BF16), jax.ShapeDtypeStruct((t, DFF), BF16)],
        scratch_shapes=[pltpu.VMEM((tm, d), F32)], operands=(x, g, wgT, wuT, wd), sem=("parallel", "arbitrary"), hosted=hosted)


def _ffn_grads(dout, h, a, b, wd, *, tm, hc, name, hosted=None):
    t, d = dout.shape
    ni, nj = t // tm, DFF // hc

    def body(dout_ref, h_ref, a_ref, b_ref, wd_ref, da_ref, db_ref, dwg_ref, dwu_ref, dwd_ref,
             dy_all, h_all, ds_scr, s_scr, acc_g, acc_u, acc_d):
        j, i = pl.program_id(0), pl.program_id(1)
        rows_i = pl.ds(pl.multiple_of(i * tm, tm), tm)

        @pl.when(j == 0)
        def _():
            dy_all[rows_i, :] = (0.5 * dout_ref[...]).astype(BF16)
            h_all[rows_i, :] = h_ref[...]

        @pl.when(i == 0)
        def _():
            acc_g[...] = jnp.zeros_like(acc_g)
            acc_u[...] = jnp.zeros_like(acc_u)
            acc_d[...] = jnp.zeros_like(acc_d)

        def grad_rows(rows):
            ds = ds_scr[rows, :]
            av = a_ref[rows, :].astype(F32)
            bv = b_ref[rows, :].astype(F32)
            sg = _sigmoid(av)
            sl = av * sg
            s_scr[rows, :] = (sl * bv).astype(BF16)
            da_ref[rows, :] = (ds * bv * (sg + sl * (1.0 - sg))).astype(BF16)
            db_ref[rows, :] = (ds * sl).astype(BF16)

        for blk in range(tm // MM_ROWS):
            rs = slice(blk * MM_ROWS, (blk + 1) * MM_ROWS)
            ds_scr[rs, :] = _dot_nt(dy_all[pl.ds(pl.multiple_of(i * tm + blk * MM_ROWS, MM_ROWS), MM_ROWS), :], wd_ref[...])
            for c in range(MM_ROWS // ROWS_WIDE):
                grad_rows(slice(blk * MM_ROWS + c * ROWS_WIDE, blk * MM_ROWS + (c + 1) * ROWS_WIDE))

        dy_i = dy_all[rows_i, :]
        h_i = h_all[rows_i, :]
        acc_d[...] += _dot_tn(s_scr[...], dy_i)
        acc_g[...] += _dot_tn(da_ref[...], h_i)
        acc_u[...] += _dot_tn(db_ref[...], h_i)

        @pl.when(i == ni - 1)
        def _():
            dwg_ref[...] = acc_g[...].astype(BF16)
            dwu_ref[...] = acc_u[...].astype(BF16)
            dwd_ref[...] = acc_d[...].astype(BF16)

    first = pl.BlockSpec((tm, d), lambda j, i: (jnp.where(j == 0, i, 0), 0))
    hid = pl.BlockSpec((tm, hc), lambda j, i: (i, j))
    wsp = pl.BlockSpec((hc, d), lambda j, i: (j, 0))
    hid_shape = jax.ShapeDtypeStruct((t, DFF), BF16)
    w_shape = jax.ShapeDtypeStruct((DFF, d), BF16)
    return _call(
        body, name=name, grid=(nj, ni), in_specs=[first, first, hid, hid, wsp], out_specs=[hid, hid, wsp, wsp, wsp],
        out_shape=[hid_shape, hid_shape, w_shape, w_shape, w_shape],
        scratch_shapes=[pltpu.VMEM((t, d), BF16), pltpu.VMEM((t, d), BF16), pltpu.VMEM((tm, hc), F32), pltpu.VMEM((tm, hc), BF16),
                        pltpu.VMEM((hc, d), F32), pltpu.VMEM((hc, d), F32), pltpu.VMEM((hc, d), F32)],
        operands=(dout, h, a, b, wd), sem=("arbitrary", "arbitrary"), hosted=hosted)


def _proj_bwd(dlat, dconv3, dgl, latT, convT, gateT, x, g, dres, *, tm, name, hosted=None):
    t, d = x.shape

    def body(dl_ref, dc_ref, dg_ref, wl_ref, wc_ref, wg_ref, x_ref, g_ref, dres_ref, dx_ref, dgain_ref):
        @pl.when(pl.program_id(0) == 0)
        def _():
            dgain_ref[...] = jnp.zeros_like(dgain_ref)

        dh = _dot_nn(dl_ref[...], wl_ref[...]) + _dot_nn(dc_ref[...], wc_ref[...]) + _dot_nn(dg_ref[...], wg_ref[...])
        xhat, r = _rms_stats(x_ref[...])
        dx, dgain = _rms_bwd(dh, xhat, r, g_ref[...])
        dx_ref[...] = dres_ref[...] + dx
        dgain_ref[...] += dgain

    def rows(w):
        return pl.BlockSpec((tm, w), lambda i: (i, 0))

    def full(r):
        return pl.BlockSpec((r, d), lambda i: (0, 0))

    return _call(
        body, name=name, grid=(t // tm,),
        in_specs=[rows(LAT_PAD), rows(CONV_COLS), rows(GATE_COLS), full(LAT_PAD), full(CONV_COLS), full(GATE_COLS), rows(d), full(1), rows(d)],
        out_specs=[rows(d), full(1)], out_shape=[jax.ShapeDtypeStruct((t, d), F32), jax.ShapeDtypeStruct((1, d), F32)],
        scratch_shapes=[], operands=(dlat, dconv3, dgl, latT, convT, gateT, x, g, dres), sem=("arbitrary",), hosted=hosted)


def _ffn_up_bwd(da, db, wgT, wuT, x, g, dout, *, tm, name, hosted=None):
    t, d = x.shape

    def body(da_ref, db_ref, wg_ref, wu_ref, x_ref, g_ref, dout_ref, dx_ref, dg_ref):
        @pl.when(pl.program_id(0) == 0)
        def _():
            dg_ref[...] = jnp.zeros_like(dg_ref)

        dh = _dot_nn(da_ref[...], wg_ref[...]) + _dot_nn(db_ref[...], wu_ref[...])
        xhat, r = _rms_stats(x_ref[...])
        dx, dg = _rms_bwd(dh, xhat, r, g_ref[...])
        dx_ref[...] = dout_ref[...] + dx
        dg_ref[...] += dg

    row = pl.BlockSpec((tm, d), lambda i: (i, 0))
    vec = pl.BlockSpec((1, d), lambda i: (0, 0))
    hid = pl.BlockSpec((tm, DFF), lambda i: (i, 0))
    wsp = pl.BlockSpec((DFF, d), lambda i: (0, 0))
    return _call(
        body, name=name, grid=(t // tm,), in_specs=[hid, hid, wsp, wsp, row, vec, row], out_specs=[row, vec],
        out_shape=[jax.ShapeDtypeStruct((t, d), F32), jax.ShapeDtypeStruct((1, d), F32)], scratch_shapes=[],
        operands=(da, db, wgT, wuT, x, g, dout), sem=("arbitrary",), hosted=hosted)


def _rope_fwd(x, c, s1, s2):
    return x * c + pltpu.roll(x, HEAD_PAD - ROPE_HALF, 1) * s1 + pltpu.roll(x, ROPE_HALF, 1) * s2


def _rope_bwd(dy, c, s1, s2):
    return dy * c + pltpu.roll(dy * s1, ROPE_HALF, 1) + pltpu.roll(dy * s2, HEAD_PAD - ROPE_HALF, 1)


def _head_stats(x):
    r = lax.rsqrt(jnp.sum(x * x, axis=-1, keepdims=True) * (1.0 / QK_DIM) + NORM_EPS)
    return x * r, r


def _mla_prep_fwd(lat, gq, gkv, ghq, ghk, wq, wk, wv, rc, rs1, rs2, *, tm, name):
    t = lat.shape[0]

    def body(lat_ref, gq_ref, gkv_ref, ghq_ref, ghk_ref, wq_ref, wk_ref, wv_ref, c_ref, s1_ref, s2_ref,
             q_ref, k_ref, v_ref, qn_ref, ckv_ref):
        lat_v = lat_ref[...]
        qhat, _ = _rms_stats(lat_v[:, :Q_LORA].astype(F32))
        qn = (qhat * gq_ref[...]).astype(BF16)
        khat, _ = _rms_stats(lat_v[:, Q_LORA:Q_LORA + KV_LORA].astype(F32))
        ckv = (khat * gkv_ref[...]).astype(BF16)
        ckv_ext = jnp.concatenate([ckv, lat_v[:, Q_LORA + KV_LORA:]], axis=1)
        qn_ref[...] = qn
        ckv_ref[...] = ckv_ext
        q_pre = _dot_nn(qn, wq_ref[...])
        k_pre = _dot_nn(ckv_ext, wk_ref[...])
        v_ref[...] = _dot_nn(ckv, wv_ref[...]).astype(BF16)
        c, s1, s2 = c_ref[...], s1_ref[...], s2_ref[...]
        for h in range(N_HEADS):
            hs = slice(h * HEAD_PAD, (h + 1) * HEAD_PAD)
            xq, _ = _head_stats(q_pre[:, hs])
            q_ref[:, hs] = _rope_fwd(xq * ghq_ref[...], c, s1, s2).astype(BF16)
            xk, _ = _head_stats(k_pre[:, hs])
            k_ref[:, hs] = _rope_fwd(xk * ghk_ref[...], c, s1, s2).astype(BF16)

    def row(w):
        return pl.BlockSpec((tm, w), lambda i: (i, 0))

    def full(r, w):
        return pl.BlockSpec((r, w), lambda i: (0, 0))

    wide = jax.ShapeDtypeStruct((t, D), BF16)
    lat3 = jax.ShapeDtypeStruct((t, Q_LORA), BF16)
    return pl.pallas_call(
        body, name=name, grid=(t // tm,),
        in_specs=[row(LAT_PAD), full(1, Q_LORA), full(1, KV_LORA), full(1, HEAD_PAD), full(1, HEAD_PAD),
                  full(Q_LORA, D), full(Q_LORA, D), full(KV_LORA, D), row(HEAD_PAD), row(HEAD_PAD), row(HEAD_PAD)],
        out_specs=[row(D), row(D), row(D), row(Q_LORA), row(Q_LORA)],
        out_shape=[wide, wide, wide, lat3, lat3],
        compiler_params=_params("parallel"),
    )(lat, gq, gkv, ghq, ghk, wq, wk, wv, rc, rs1, rs2)


def _mla_prep_bwd(dq, dk, dv, lat, qn, ckv_ext, gq, gkv, ghq, ghk, wq, wk, wv, rc, rs1, rs2, *, tm, name):
    t = lat.shape[0]

    def body(dq_ref, dk_ref, dv_ref, lat_ref, qn_ref, ckv_ref, gq_ref, gkv_ref, ghq_ref, ghk_ref, wq_ref, wk_ref, wv_ref,
             c_ref, s1_ref, s2_ref, dlat_ref, dqp_ref, dkp_ref, dgq_ref, dgkv_ref, dghq_ref, dghk_ref):
        @pl.when(pl.program_id(0) == 0)
        def _():
            dgq_ref[...] = jnp.zeros_like(dgq_ref)
            dgkv_ref[...] = jnp.zeros_like(dgkv_ref)
            dghq_ref[...] = jnp.zeros_like(dghq_ref)
            dghk_ref[...] = jnp.zeros_like(dghk_ref)

        c, s1, s2 = c_ref[...], s1_ref[...], s2_ref[...]
        q_pre = _dot_nn(qn_ref[...], wq_ref[...])
        k_pre = _dot_nn(ckv_ref[...], wk_ref[...])

        def heads(pre, dy_ref, gh_ref, dgh_ref, out_ref):
            dgh = jnp.zeros((1, HEAD_PAD), F32)
            for h in range(N_HEADS):
                hs = slice(h * HEAD_PAD, (h + 1) * HEAD_PAD)
                d = _rope_bwd(dy_ref[:, hs], c, s1, s2)
                xhat, r = _head_stats(pre[:, hs])
                dgh = dgh + jnp.sum(d * xhat, axis=0, keepdims=True)
                dxh = d * gh_ref[...]
                dx = r * (dxh - xhat * (jnp.sum(dxh * xhat, axis=-1, keepdims=True) * (1.0 / QK_DIM)))
                out_ref[:, hs] = dx.astype(BF16)
            dgh_ref[...] += dgh

        heads(q_pre, dq_ref, ghq_ref, dghq_ref, dqp_ref)
        heads(k_pre, dk_ref, ghk_ref, dghk_ref, dkp_ref)
        dqn = _dot_nt(dqp_ref[...], wq_ref[...])
        dce = _dot_nt(dkp_ref[...], wk_ref[...])
        dckv = dce[:, :KV_LORA] + _dot_nt(dv_ref[...], wv_ref[...])
        lat_v = lat_ref[...]
        qhat, rq = _rms_stats(lat_v[:, :Q_LORA].astype(F32))
        dql, dgq = _rms_bwd(dqn, qhat, rq, gq_ref[...])
        khat, rk = _rms_stats(lat_v[:, Q_LORA:Q_LORA + KV_LORA].astype(F32))
        dkl, dgkv = _rms_bwd(dckv, khat, rk, gkv_ref[...])
        dgq_ref[...] += dgq
        dgkv_ref[...] += dgkv
        dlat_ref[...] = jnp.concatenate([dql, dkl, dce[:, KV_LORA:]], axis=1).astype(BF16)

    def row(w):
        return pl.BlockSpec((tm, w), lambda i: (i, 0))

    def full(r, w):
        return pl.BlockSpec((r, w), lambda i: (0, 0))

    return pl.pallas_call(
        body, name=name, grid=(t // tm,),
        in_specs=[row(D), row(D), row(D), row(LAT_PAD), row(Q_LORA), row(Q_LORA), full(1, Q_LORA), full(1, KV_LORA),
                  full(1, HEAD_PAD), full(1, HEAD_PAD), full(Q_LORA, D), full(Q_LORA, D), full(KV_LORA, D),
                  row(HEAD_PAD), row(HEAD_PAD), row(HEAD_PAD)],
        out_specs=[row(LAT_PAD), row(D), row(D), full(1, Q_LORA), full(1, KV_LORA), full(1, HEAD_PAD), full(1, HEAD_PAD)],
        out_shape=[jax.ShapeDtypeStruct((t, LAT_PAD), BF16), jax.ShapeDtypeStruct((t, D), BF16), jax.ShapeDtypeStruct((t, D), BF16),
                   jax.ShapeDtypeStruct((1, Q_LORA), F32), jax.ShapeDtypeStruct((1, KV_LORA), F32),
                   jax.ShapeDtypeStruct((1, HEAD_PAD), F32), jax.ShapeDtypeStruct((1, HEAD_PAD), F32)],
        compiler_params=_params("arbitrary"),
    )(dq, dk, dv, lat, qn, ckv_ext, gq, gkv, ghq, ghk, wq, wk, wv, rc, rs1, rs2)


def _causal_keep(tq):
    r = lax.broadcasted_iota(jnp.int32, (tq, tq), 0)
    c = lax.broadcasted_iota(jnp.int32, (tq, tq), 1)
    return c <= r


def _flash_fwd(q, k, v, *, n_seq, seq, tq, name, hosted=None):
    nq = seq // tq

    def body(q_ref, k_ref, v_ref, o_ref, lse_ref):
        qi = pl.program_id(2)
        qv = q_ref[...]

        def step(j, carry, masked):
            m, l, acc = carry
            kj = k_ref[pl.ds(pl.multiple_of(j * tq, tq), tq), :]
            vj = v_ref[pl.ds(pl.multiple_of(j * tq, tq), tq), :]
            s = _dot_nt(qv, kj) * ATTN_SCALE
            if masked:
                s = jnp.where(_causal_keep(tq), s, NEG)
            m_new = jnp.maximum(m, jnp.max(s, axis=-1, keepdims=True))
            alpha = jnp.exp(m - m_new)
            p = jnp.exp(s - m_new)
            l = alpha * l + jnp.sum(p, axis=-1, keepdims=True)
            acc = alpha * acc + _dot_nn(p.astype(BF16), vj)
            return m_new, l, acc

        init = (jnp.full((tq, 1), NEG, F32), jnp.zeros((tq, 1), F32), jnp.zeros((tq, HEAD_PAD), F32))
        carry = lax.fori_loop(0, qi, lambda j, cr: step(j, cr, False), init)
        m, l, acc = step(qi, carry, True)
        o_ref[...] = (acc / l).astype(BF16)
        lse_ref[...] = jnp.broadcast_to(m + jnp.log(l), (tq, HEAD_PAD))

    qspec = pl.BlockSpec((tq, HEAD_PAD), lambda b, h, i: (b * nq + i, h))
    kspec = pl.BlockSpec((seq, HEAD_PAD), lambda b, h, i: (b, h))
    t = n_seq * seq
    return _call(
        body, name=name, grid=(n_seq, N_HEADS, nq), in_specs=[qspec, kspec, kspec], out_specs=[qspec, qspec],
        out_shape=[jax.ShapeDtypeStruct((t, D), BF16), jax.ShapeDtypeStruct((t, D), F32)], scratch_shapes=[],
        operands=(q, k, v), sem=("parallel", "parallel", "arbitrary"), hosted=hosted)


def _flash_bwd(q, k, v, o, lse, do, *, n_seq, seq, tq, name, hosted=None):
    nq = seq // tq

    def body(q_ref, k_ref, v_ref, o_ref, lse_ref, do_ref, dq_ref, dk_ref, dv_ref, dk_acc, dv_acc):
        j = pl.program_id(2)

        @pl.when(j == 0)
        def _():
            dq_ref[...] = jnp.zeros_like(dq_ref)

        dk_acc[...] = jnp.zeros_like(dk_acc)
        dv_acc[...] = jnp.zeros_like(dv_acc)
        kv = k_ref[...]
        vv = v_ref[...]

        def step(i, masked):
            rows = pl.ds(pl.multiple_of(i * tq, tq), tq)
            qi = q_ref[rows, :]
            doi = do_ref[rows, :]
            delta = jnp.sum(doi.astype(F32) * o_ref[rows, :].astype(F32), axis=-1, keepdims=True)
            s = _dot_nt(qi, kv) * ATTN_SCALE
            p = jnp.exp(s - lse_ref[rows, :][:, :1])
            if masked:
                p = jnp.where(_causal_keep(tq), p, 0.0)
            dv_acc[...] += _dot_tn(p.astype(BF16), doi)
            dp = _dot_nt(doi, vv)
            ds = (p * (dp - delta) * ATTN_SCALE).astype(BF16)
            dk_acc[...] += _dot_tn(ds, qi)
            dq_ref[rows, :] += _dot_nn(ds, kv)

        step(j, True)

        def loop_body(i, carry):
            step(i, False)
            return carry

        lax.fori_loop(j + 1, nq, loop_body, 0)
        dk_ref[...] = dk_acc[...]
        dv_ref[...] = dv_acc[...].astype(BF16)

    full = pl.BlockSpec((seq, HEAD_PAD), lambda b, h, j: (b, h))
    tile = pl.BlockSpec((tq, HEAD_PAD), lambda b, h, j: (b * nq + j, h))
    t = n_seq * seq
    return _call(
        body, name=name, grid=(n_seq, N_HEADS, nq), in_specs=[full, tile, tile, full, full, full],
        out_specs=[full, tile, tile],
        out_shape=[jax.ShapeDtypeStruct((t, D), F32), jax.ShapeDtypeStruct((t, D), F32), jax.ShapeDtypeStruct((t, D), BF16)],
        scratch_shapes=[pltpu.VMEM((tq, HEAD_PAD), F32), pltpu.VMEM((tq, HEAD_PAD), F32)],
        operands=(q, k, v, o, lse, do), sem=("parallel", "parallel", "arbitrary"), hosted=hosted)


CONV_CB = 256


def _shift_down(u, k, row):
    return jnp.where(row >= k, pltpu.roll(u, k, 0), 0.0)


def _shift_up(u, k, row, n):
    return jnp.where(row < n - k, pltpu.roll(u, n - k, 0), 0.0)


def _conv_fwd(conv3, cw, *, n_seq, seq, name):
    def body(c_ref, w_ref, p_ref):
        blk = c_ref[...].astype(F32)
        xc, gb, gc = blk[:, :CONV_CB], blk[:, CONV_CB:2 * CONV_CB], blk[:, 2 * CONV_CB:]
        row = lax.broadcasted_iota(jnp.int32, (seq, CONV_CB), 0)
        u = gc * xc
        z = w_ref[0:1, :] * _shift_down(u, 2, row) + w_ref[1:2, :] * _shift_down(u, 1, row) + w_ref[2:3, :] * u
        p_ref[...] = (gb * z).astype(BF16)

    return pl.pallas_call(
        body, name=name, grid=(n_seq, D // CONV_CB),
        in_specs=[pl.BlockSpec((seq, 3 * CONV_CB), lambda b, j: (b, j)), pl.BlockSpec((3, CONV_CB), lambda b, j: (0, j))],
        out_specs=pl.BlockSpec((seq, CONV_CB), lambda b, j: (b, j)),
        out_shape=jax.ShapeDtypeStruct((n_seq * seq, D), BF16),
        compiler_params=_params("parallel", "parallel"),
    )(conv3, cw)


def _conv_bwd(dp, conv3, cw, *, n_seq, seq, name):
    def body(dp_ref, c_ref, w_ref, dc_ref, dw_ref):
        @pl.when(pl.program_id(1) == 0)
        def _():
            dw_ref[...] = jnp.zeros_like(dw_ref)

        blk = c_ref[...].astype(F32)
        xc, gb, gc = blk[:, :CONV_CB], blk[:, CONV_CB:2 * CONV_CB], blk[:, 2 * CONV_CB:]
        row = lax.broadcasted_iota(jnp.int32, (seq, CONV_CB), 0)
        w0, w1, w2 = w_ref[0:1, :], w_ref[1:2, :], w_ref[2:3, :]
        u = gc * xc
        u1 = _shift_down(u, 1, row)
        u2 = _shift_down(u, 2, row)
        z = w0 * u2 + w1 * u1 + w2 * u
        dpv = dp_ref[...].astype(F32)
        dz = dpv * gb
        du = w2 * dz + w1 * _shift_up(dz, 1, row, seq) + w0 * _shift_up(dz, 2, row, seq)
        dc_ref[...] = jnp.concatenate([du * gc, dpv * z, du * xc], axis=1).astype(BF16)
        dw_ref[0:1, :] += jnp.sum(dz * u2, axis=0, keepdims=True)
        dw_ref[1:2, :] += jnp.sum(dz * u1, axis=0, keepdims=True)
        dw_ref[2:3, :] += jnp.sum(dz * u, axis=0, keepdims=True)

    return pl.pallas_call(
        body, name=name, grid=(D // CONV_CB, n_seq),
        in_specs=[pl.BlockSpec((seq, CONV_CB), lambda j, b: (b, j)), pl.BlockSpec((seq, 3 * CONV_CB), lambda j, b: (b, j)),
                  pl.BlockSpec((3, CONV_CB), lambda j, b: (0, j))],
        out_specs=[pl.BlockSpec((seq, 3 * CONV_CB), lambda j, b: (b, j)), pl.BlockSpec((3, CONV_CB), lambda j, b: (0, j))],
        out_shape=[jax.ShapeDtypeStruct((n_seq * seq, CONV_COLS), BF16), jax.ShapeDtypeStruct((3, D), F32)],
        compiler_params=_params("parallel", "arbitrary"),
    )(dp, conv3, cw)


def _merge_fwd(o, p, gl, bias, x1, wpa, wpc, wout, *, tm, name, hosted=None):
    t = x1.shape[0]

    def body(o_ref, p_ref, gl_ref, b_ref, x_ref, wpa_ref, wpc_ref, wout_ref, x2_ref, mg_ref, ya_ref, yb_ref):
        ya = _dot_nn(o_ref[...], wpa_ref[...])
        yb = _dot_nn(p_ref[...], wpc_ref[...])
        gates = _sigmoid(gl_ref[...].astype(F32) + b_ref[...])
        merged = (gates[:, :D] * ya + gates[:, D:] * yb).astype(BF16)
        ya_ref[...] = ya.astype(BF16)
        yb_ref[...] = yb.astype(BF16)
        mg_ref[...] = merged
        x2_ref[...] = x_ref[...] + _dot_nn(merged, wout_ref[...])

    row = pl.BlockSpec((tm, D), lambda i: (i, 0))
    row2 = pl.BlockSpec((tm, GATE_COLS), lambda i: (i, 0))
    wsp = pl.BlockSpec((D, D), lambda i: (0, 0))
    wide = jax.ShapeDtypeStruct((t, D), BF16)
    return _call(
        body, name=name, grid=(t // tm,),
        in_specs=[row, row, row2, pl.BlockSpec((1, GATE_COLS), lambda i: (0, 0)), row, wsp, wsp, wsp],
        out_specs=[row, row, row, row], out_shape=[jax.ShapeDtypeStruct((t, D), F32), wide, wide, wide], scratch_shapes=[],
        operands=(o, p, gl, bias, x1, wpa, wpc, wout), sem=("parallel",), hosted=hosted)


def _merge_bwd(dx2, ya, yb, gl, bias, wpa, wpc, wout, *, tm, name, hosted=None):
    t = dx2.shape[0]

    def body(dx_ref, ya_ref, yb_ref, gl_ref, b_ref, wpa_ref, wpc_ref, wout_ref,
             dxb_ref, dya_ref, dyb_ref, dgl_ref, do_ref, dp_ref, db_ref):
        @pl.when(pl.program_id(0) == 0)
        def _():
            db_ref[...] = jnp.zeros_like(db_ref)

        dxb = dx_ref[...].astype(BF16)
        dxb_ref[...] = dxb
        dm = _dot_nt(dxb, wout_ref[...])
        gates = _sigmoid(gl_ref[...].astype(F32) + b_ref[...])
        ga, gb = gates[:, :D], gates[:, D:]
        dya = (dm * ga).astype(BF16)
        dyb = (dm * gb).astype(BF16)
        dya_ref[...] = dya
        dyb_ref[...] = dyb
        dgl = jnp.concatenate([dm * ya_ref[...].astype(F32) * ga * (1.0 - ga),
                               dm * yb_ref[...].astype(F32) * gb * (1.0 - gb)], axis=1)
        dgl_ref[...] = dgl.astype(BF16)
        db_ref[...] += jnp.sum(dgl, axis=0, keepdims=True)
        do_ref[...] = _dot_nt(dya, wpa_ref[...]).astype(BF16)
        dp_ref[...] = _dot_nt(dyb, wpc_ref[...]).astype(BF16)

    row = pl.BlockSpec((tm, D), lambda i: (i, 0))
    row2 = pl.BlockSpec((tm, GATE_COLS), lambda i: (i, 0))
    vec2 = pl.BlockSpec((1, GATE_COLS), lambda i: (0, 0))
    wsp = pl.BlockSpec((D, D), lambda i: (0, 0))
    wide = jax.ShapeDtypeStruct((t, D), BF16)
    return _call(
        body, name=name, grid=(t // tm,), in_specs=[row, row, row, row2, vec2, wsp, wsp, wsp],
        out_specs=[row, row, row, row2, row, row, vec2],
        out_shape=[wide, wide, wide, jax.ShapeDtypeStruct((t, GATE_COLS), BF16), wide, wide,
                   jax.ShapeDtypeStruct((1, GATE_COLS), F32)],
        scratch_shapes=[], operands=(dx2, ya, yb, gl, bias, wpa, wpc, wout), sem=("arbitrary",), hosted=hosted)


def _loss_head(y, target, *, tm, name):
    t, d = y.shape

    def body(y_ref, t_ref, dy_ref, loss_ref):
        @pl.when(pl.program_id(0) == 0)
        def _():
            loss_ref[...] = jnp.zeros_like(loss_ref)

        err = y_ref[...] - t_ref[...]
        dy_ref[...] = err * (1.0 / d)
        loss_ref[...] += jnp.sum(jnp.sum(err * err, axis=-1, keepdims=True), axis=0, keepdims=True) * (0.5 / d)

    row = pl.BlockSpec((tm, d), lambda i: (i, 0))
    return pl.pallas_call(
        body, name=name, grid=(t // tm,), in_specs=[row, row], out_specs=[row, pl.BlockSpec((1, 128), lambda i: (0, 0))],
        out_shape=[jax.ShapeDtypeStruct((t, d), F32), jax.ShapeDtypeStruct((1, 128), F32)],
        compiler_params=_params("arbitrary"),
    )(y, target)


def _adamw(w, g, m, v, *, name):
    rows, cols = w.shape
    tr = max([c for c in range(8, 513, 8) if rows % c == 0], default=rows)
    c1 = 1.0 / (1.0 - ADAM_B1 ** ADAM_STEP)
    c2 = 1.0 / (1.0 - ADAM_B2 ** ADAM_STEP)

    def body(w_ref, g_ref, m_ref, v_ref, d_ref, nm_ref, nv_ref):
        gv = g_ref[...]
        nm = ADAM_B1 * m_ref[...] + (1.0 - ADAM_B1) * gv
        nv = ADAM_B2 * v_ref[...] + (1.0 - ADAM_B2) * (gv * gv)
        nm_ref[...] = nm
        nv_ref[...] = nv
        d_ref[...] = -ADAM_LR * ((nm * c1) / (jnp.sqrt(nv * c2) + ADAM_EPS) + ADAM_WD * w_ref[...])

    spec = pl.BlockSpec((tr, cols), lambda i: (i, 0))
    shp = jax.ShapeDtypeStruct((rows, cols), F32)
    return pl.pallas_call(
        body, name=name, grid=(rows // tr,), in_specs=[spec] * 4, out_specs=[spec] * 3, out_shape=[shp] * 3,
        compiler_params=_params("parallel"),
    )(w, g, m, v)


def _place():
    return lax.axis_index("x"), lax.axis_index("y"), lax.axis_index("c")


def _other_chips(x, y):
    return [(1 - x, y), (x, 1 - y), (1 - x, 1 - y)]


def _remote(src, dst, send, recv, dev):
    return pltpu.make_async_remote_copy(src_ref=src, dst_ref=dst, send_sem=send, recv_sem=recv, device_id=dev, device_id_type=MESH)


def _gather_chips_plan(n):
    def start(srcs, dsts, send, recv, local):
        x, y, cc = _place()
        me = 4 * x + 2 * y + cc
        for a in range(n):
            pltpu.make_async_copy(srcs[a], dsts[a].at[me], local.at[a]).start()
            for k, (px, py) in enumerate(_other_chips(x, y)):
                _remote(srcs[a], dsts[a].at[me], send.at[3 * a + k], recv.at[3 * a + k], (px, py, cc)).start()

    def wait(srcs, dsts, send, recv, local):
        x, y, cc = _place()
        me = 4 * x + 2 * y + cc
        for a in range(n):
            for k, (px, py) in enumerate(_other_chips(x, y)):
                _remote(srcs[a], dsts[a].at[4 * px + 2 * py + cc], send.at[3 * a + k], recv.at[3 * a + k], (px, py, cc)).wait_recv()
        for a in range(n):
            for k, (px, py) in enumerate(_other_chips(x, y)):
                _remote(srcs[a], dsts[a].at[me], send.at[3 * a + k], recv.at[3 * a + k], (px, py, cc)).wait_send()
            pltpu.make_async_copy(srcs[a], dsts[a].at[me], local.at[a]).wait()

    return _Plan(start, wait, 3 * n, n)


def _scatter_chips_plan(n):
    def start(srcs, dsts, send, recv, local):
        x, y, cc = _place()
        for a in range(n):
            for k, (px, py) in enumerate(_other_chips(x, y)):
                _remote(srcs[a].at[2 * px + py], dsts[a].at[k], send.at[3 * a + k], recv.at[3 * a + k], (px, py, cc)).start()

    def wait(srcs, dsts, send, recv, local):
        x, y, cc = _place()
        for a in range(n):
            for k, (px, py) in enumerate(_other_chips(x, y)):
                _remote(srcs[a].at[k], dsts[a].at[k], send.at[3 * a + k], recv.at[3 * a + k], (px, py, cc)).wait_recv()
        for a in range(n):
            for k, (px, py) in enumerate(_other_chips(x, y)):
                _remote(srcs[a].at[k], dsts[a].at[k], send.at[3 * a + k], recv.at[3 * a + k], (px, py, cc)).wait_send()

    return _Plan(start, wait, 3 * n, 0)


def _gather_shapes(blocks):
    return [jax.ShapeDtypeStruct((N_DEV,) + b.shape, b.dtype) for b in blocks]


def _scatter_shapes(parts):
    return [jax.ShapeDtypeStruct((3,) + p.shape[1:], p.dtype) for p in parts]


def _gather_sibling_plan(n):
    def start(srcs, dsts, send, recv, local):
        x, y, cc = _place()
        for a in range(n):
            for q in range(4):
                _remote(srcs[a].at[2 * q + cc], dsts[a].at[2 * q + cc], send.at[4 * a + q], recv.at[4 * a + q], (x, y, 1 - cc)).start()

    def wait(srcs, dsts, send, recv, local):
        x, y, cc = _place()
        for a in range(n):
            for q in range(4):
                _remote(srcs[a].at[2 * q + cc], dsts[a].at[2 * q + 1 - cc], send.at[4 * a + q], recv.at[4 * a + q],
                        (x, y, 1 - cc)).wait_recv()
        for a in range(n):
            for q in range(4):
                _remote(srcs[a].at[2 * q + cc], dsts[a].at[2 * q + cc], send.at[4 * a + q], recv.at[4 * a + q],
                        (x, y, 1 - cc)).wait_send()

    return _Plan(start, wait, 4 * n, 0, in_place=True)


def _scatter_sibling_plan(n):
    def start(srcs, dsts, send, recv, local):
        x, y, cc = _place()
        for a in range(n):
            for q in range(4):
                _remote(srcs[a].at[2 * q + 1 - cc], dsts[a].at[q], send.at[4 * a + q], recv.at[4 * a + q], (x, y, 1 - cc)).start()

    def wait(srcs, dsts, send, recv, local):
        x, y, cc = _place()
        for a in range(n):
            for q in range(4):
                _remote(srcs[a].at[q], dsts[a].at[q], send.at[4 * a + q], recv.at[4 * a + q], (x, y, 1 - cc)).wait_recv()
        for a in range(n):
            for q in range(4):
                _remote(srcs[a].at[q], dsts[a].at[q], send.at[4 * a + q], recv.at[4 * a + q], (x, y, 1 - cc)).wait_send()

    return _Plan(start, wait, 4 * n, 0)


def _same_shapes(arrs):
    return [jax.ShapeDtypeStruct(a.shape, a.dtype) for a in arrs]


def _halved_shapes(parts):
    return [jax.ShapeDtypeStruct((4,) + p.shape[1:], p.dtype) for p in parts]


def _run_plan(plan, srcs, out_shapes, *, name):
    n_in, n_out = len(srcs), len(out_shapes)

    def body(*refs):
        h_in, h_out, sems = refs[:n_in], refs[n_in:n_in + n_out], refs[n_in + n_out:]
        plan.start(h_in, h_out, *sems)
        plan.wait(h_in, h_out, *sems)

    return pl.pallas_call(body, name=name, in_specs=[ANY] * n_in, out_specs=[ANY] * n_out, out_shape=list(out_shapes),
                          input_output_aliases={a: a for a in range(n_in)} if plan.in_place else {},
                          scratch_shapes=plan.sems())(*srcs)


def _sum_sibling(p, q, core, *, name):
    _, r, c = p.shape

    def body(core_ref, p_ref, q_ref, o_ref):
        o_ref[...] = (p_ref[...].astype(F32) + q_ref[...].astype(F32)).astype(BF16)

    grid_spec = pltpu.PrefetchScalarGridSpec(
        num_scalar_prefetch=1, grid=(4,),
        in_specs=[pl.BlockSpec((1, r, c), lambda ch, core_ref: (2 * ch + core_ref[0], 0, 0)),
                  pl.BlockSpec((1, r, c), lambda ch, core_ref: (ch, 0, 0))],
        out_specs=pl.BlockSpec((1, r, c), lambda ch, core_ref: (ch, 0, 0)))
    return pl.pallas_call(
        body, name=name, grid_spec=grid_spec, out_shape=jax.ShapeDtypeStruct((4, r, c), BF16),
        compiler_params=_params("parallel"),
    )(core, p, q)


def _sum_chips(s1, r2, chip, *, name):
    _, r, c = s1.shape

    def body(chip_ref, s_ref, r_ref, o_ref):
        acc = s_ref[0].astype(F32)
        for k in range(3):
            acc = acc + r_ref[k].astype(F32)
        o_ref[...] = acc

    grid_spec = pltpu.PrefetchScalarGridSpec(
        num_scalar_prefetch=1, grid=(1,),
        in_specs=[pl.BlockSpec((1, r, c), lambda i, chip_ref: (chip_ref[0], 0, 0)),
                  pl.BlockSpec((3, r, c), lambda i, chip_ref: (0, 0, 0))],
        out_specs=pl.BlockSpec((r, c), lambda i, chip_ref: (0, 0)))
    return pl.pallas_call(
        body, name=name, grid_spec=grid_spec, out_shape=jax.ShapeDtypeStruct((r, c), F32),
        compiler_params=_params("arbitrary"),
    )(chip, s1, r2)


def _sum_adamw(s1, r2, chip, w, m, v, *, name):
    _, r, c = s1.shape
    c1 = 1.0 / (1.0 - ADAM_B1 ** ADAM_STEP)
    c2 = 1.0 / (1.0 - ADAM_B2 ** ADAM_STEP)

    def body(chip_ref, s_ref, r_ref, w_ref, m_ref, v_ref, g_ref, d_ref, nm_ref, nv_ref):
        gv = s_ref[0].astype(F32)
        for k in range(3):
            gv = gv + r_ref[k].astype(F32)
        g_ref[...] = gv
        nm = ADAM_B1 * m_ref[...] + (1.0 - ADAM_B1) * gv
        nv = ADAM_B2 * v_ref[...] + (1.0 - ADAM_B2) * (gv * gv)
        nm_ref[...] = nm
        nv_ref[...] = nv
        d_ref[...] = -ADAM_LR * ((nm * c1) / (jnp.sqrt(nv * c2) + ADAM_EPS) + ADAM_WD * w_ref[...])

    flat = pl.BlockSpec((r, c), lambda i, chip_ref: (0, 0))
    grid_spec = pltpu.PrefetchScalarGridSpec(
        num_scalar_prefetch=1, grid=(1,),
        in_specs=[pl.BlockSpec((1, r, c), lambda i, chip_ref: (chip_ref[0], 0, 0)),
                  pl.BlockSpec((3, r, c), lambda i, chip_ref: (0, 0, 0)), flat, flat, flat],
        out_specs=[flat] * 4)
    return pl.pallas_call(
        body, name=name, grid_spec=grid_spec, out_shape=[jax.ShapeDtypeStruct((r, c), F32)] * 4,
        compiler_params=_params("arbitrary"),
    )(chip, s1, r2, w, m, v)


def _small_exchange(v, *, reduce, name):
    r, c = v.shape

    def body(x_ref, o_ref, *rest):
        if reduce:
            buf_ref, send_sems, recv_sems = rest
        else:
            buf_ref = o_ref
            send_sems, recv_sems = rest
        x, y, cc = _place()
        me = 4 * x + 2 * y + cc

        def peer(k):
            return ((1 - x) if k & 4 else x, (1 - y) if k & 2 else y, (1 - cc) if k & 1 else cc)

        buf_ref[me] = x_ref[...]
        sends = []
        for k in range(1, N_DEV):
            cp = pltpu.make_async_remote_copy(src_ref=x_ref, dst_ref=buf_ref.at[me], send_sem=send_sems.at[k - 1],
                                              recv_sem=recv_sems.at[k - 1], device_id=peer(k), device_id_type=MESH)
            cp.start()
            sends.append(cp)
        for k in range(1, N_DEV):
            px, py, pc = peer(k)
            pltpu.make_async_remote_copy(src_ref=x_ref, dst_ref=buf_ref.at[4 * px + 2 * py + pc], send_sem=send_sems.at[k - 1],
                                         recv_sem=recv_sems.at[k - 1], device_id=peer(k), device_id_type=MESH).wait_recv()
        for cp in sends:
            cp.wait_send()
        if reduce:
            acc = buf_ref[0]
            for s in range(1, N_DEV):
                acc = acc + buf_ref[s]
            o_ref[...] = acc

    vm = pl.BlockSpec(memory_space=pltpu.VMEM)
    sems = [pltpu.SemaphoreType.DMA((N_DEV - 1,)), pltpu.SemaphoreType.DMA((N_DEV - 1,))]
    if reduce:
        out_shape, scratch = jax.ShapeDtypeStruct((r, c), F32), [pltpu.VMEM((N_DEV, r, c), F32)] + sems
    else:
        out_shape, scratch = jax.ShapeDtypeStruct((N_DEV, r, c), F32), sems
    return pl.pallas_call(body, name=name, in_specs=[vm], out_specs=vm, out_shape=out_shape, scratch_shapes=scratch)(v)


def _rows(a):
    return a.reshape(-1, D)


def _pad_cols(a, to):
    return jnp.pad(a, ((0, 0), (0, to - a.shape[1])))


def _pack_weights(w):
    parts = {
        "w_inT": jnp.pad(w["w_in"].T, ((0, IN_SHARD_PAD - IN_SHARD), (0, 0))),
        "w_uq": _rows(_pad_cols(w["w_uq"], HEAD_PAD)), "w_uk": _rows(_pad_cols(w["w_uk"], HEAD_PAD)),
        "w_uv": _rows(_pad_cols(w["w_uv"], HEAD_PAD)), "w_pa": _rows(w["w_proj_attn"]),
        "w_pc": w["w_proj_conv"], "w_out": w["w_out"],
    }
    return [jnp.concatenate([parts[n].astype(BF16) for n, _ in group], axis=0) for group in PACK]


def _cols_from_shards(gs, name, rows):
    idx, off, r = PACK_OFF[name]
    return gs[idx][:, off:off + r].reshape(N_DEV, rows, HEAD_PAD).transpose(1, 0, 2).reshape(rows, N_DEV * HEAD_PAD)


def _rows_from_shards(gs, name, keep=None):
    idx, off, r = PACK_OFF[name]
    keep = r if keep is None else keep
    return gs[idx][:, off:off + keep].reshape(N_DEV * keep, D)


def _rope_placement():
    i = lax.broadcasted_iota(jnp.int32, (HEAD_PAD, D), 0)
    j = lax.broadcasted_iota(jnp.int32, (HEAD_PAD, D), 1)
    return ((i < 2 * ROPE_HALF) & (j % HEAD_PAD == NOPE + i)).astype(BF16)


def _unpack_weights(g):
    w_inT = _rows_from_shards(g, "w_inT", IN_SHARD)
    lat_rows = Q_LORA + KV_LORA + 2 * ROPE_HALF
    conv = w_inT[lat_rows:lat_rows + CONV_COLS].reshape(3, D // CONV_CB, CONV_CB, D).transpose(1, 0, 2, 3).reshape(CONV_COLS, D)
    wpa = _cols_from_shards(g, "w_pa", 512).reshape(N_HEADS, NOPE, D)
    return {
        "latT": jnp.pad(w_inT[:lat_rows], ((0, LAT_PAD - lat_rows), (0, 0))),
        "convT": conv, "gateT": w_inT[lat_rows + CONV_COLS:],
        "wq": _cols_from_shards(g, "w_uq", Q_LORA),
        "wk": jnp.concatenate([_cols_from_shards(g, "w_uk", KV_LORA), _rope_placement()], axis=0),
        "wv": _cols_from_shards(g, "w_uv", KV_LORA),
        "wpa": jnp.pad(wpa, ((0, 0), (0, HEAD_PAD - NOPE), (0, 0))).reshape(D, D),
        "wpc": _rows_from_shards(g, "w_pc"), "wout": _rows_from_shards(g, "w_out"),
    }


def _shards_from_cols(a):
    rows = a.shape[0]
    return a.reshape(rows, N_DEV, HEAD_PAD).transpose(1, 0, 2).reshape(N_DEV, rows * HEAD_PAD // D, D)


def _pack_grads(gw):
    lat_rows = Q_LORA + KV_LORA + 2 * ROPE_HALF
    conv = gw["convT"].reshape(D // CONV_CB, 3, CONV_CB, D).transpose(1, 0, 2, 3).reshape(CONV_COLS, D)
    w_inT = jnp.concatenate([gw["latT"][:lat_rows], conv, gw["gateT"]], axis=0).reshape(N_DEV, IN_SHARD, D)
    wpa = gw["wpa"].reshape(N_HEADS, HEAD_PAD, D)[:, :NOPE].reshape(N_HEADS * NOPE, D)
    parts = {}
    parts.update({
        "w_inT": jnp.pad(w_inT, ((0, 0), (0, IN_SHARD_PAD - IN_SHARD), (0, 0))),
        "w_uq": _shards_from_cols(gw["wq"]), "w_uk": _shards_from_cols(gw["wk"][:KV_LORA]),
        "w_uv": _shards_from_cols(gw["wv"][:KV_LORA]), "w_pa": _shards_from_cols(wpa),
        "w_pc": gw["wpc"].reshape(N_DEV, D // N_DEV, D), "w_out": gw["wout"].reshape(N_DEV, D // N_DEV, D),
    })
    return [jnp.concatenate([parts[n] for n, _ in group], axis=1) for group in PACK]


def _unpack_grads(mines):
    def seg(name, keep=None):
        idx, off, r = PACK_OFF[name]
        return mines[idx][off:off + (r if keep is None else keep)]

    return {
        "w_in": seg("w_inT", IN_SHARD).T,
        "w_uq": seg("w_uq").reshape(Q_LORA, HEAD_PAD)[:, :QK_DIM],
        "w_uk": seg("w_uk").reshape(KV_LORA, HEAD_PAD)[:, :NOPE],
        "w_uv": seg("w_uv").reshape(KV_LORA, HEAD_PAD)[:, :NOPE],
        "w_proj_attn": seg("w_pa").reshape(512, HEAD_PAD),
        "w_proj_conv": seg("w_pc"), "w_out": seg("w_out"),
    }


def _rope_tables(positions):
    inv_freq = 1.0 / (ROPE_THETA ** (jnp.arange(ROPE_HALF, dtype=F32) / ROPE_HALF))
    ang = positions.reshape(-1).astype(F32)[:, None] * inv_freq
    cos, sin = jnp.cos(ang), jnp.sin(ang)
    t = ang.shape[0]
    zero = jnp.zeros((t, ROPE_HALF), F32)
    head = jnp.ones((t, NOPE), F32)
    tail = jnp.zeros((t, HEAD_PAD - QK_DIM), F32)
    nohead = jnp.zeros((t, NOPE), F32)
    rc = jnp.concatenate([head, cos, cos, tail], axis=1)
    rs1 = jnp.concatenate([nohead, -sin, zero, tail], axis=1)
    rs2 = jnp.concatenate([nohead, zero, sin, tail], axis=1)
    return rc, rs1, rs2


def _local_step(x, positions, target, conv_w, small, ex):
    n_seq, seq, d = x.shape
    t = n_seq * seq
    x0 = x.reshape(t, d)
    tgt = target.reshape(t, d)
    rc, rs1, rs2 = _rope_tables(positions)
    ghq = _pad_cols(small["q_head_norm"], HEAD_PAD)
    ghk = _pad_cols(small["k_head_norm"], HEAD_PAD)
    TM, HC, TQ = 1024, 256, 512

    def mm(*args, hosted=None, **kw):
        res = _mm(*args, hosted=hosted, **kw)
        return res if hosted is not None else (res, None)

    def wgrad(a, b, name, tm=None, hosted=None):
        return mm(a, b, mode="tn", out_dtype=BF16, tm=tm or a.shape[1], tn=b.shape[1], tk=512, name=name, hosted=hosted)

    f1g, f1u, f1d = ex.gather_now("ffn1")
    (x1, h1, a1, b1), got = _ffn_fwd(x0, small["ffn1_norm"], f1g, f1u, f1d, tm=TM, hc=HC, name="ffn1_fwd",
                                     hosted=ex.gather_chips("mix_in", "mix_misc"))
    hm, got = _rms_fwd(x1, small["mix_norm"], tm=TM, name="mix_norm_fwd", hosted=ex.gather_sibling(got))
    W = ex.mix_weights(got)
    lat = _mm(hm, W["latT"], mode="nt", out_dtype=BF16, tm=TM, tn=LAT_PAD, tk=D, name="proj_lat")
    conv3 = _mm(hm, W["convT"], mode="nt", out_dtype=BF16, tm=TM, tn=CONV_COLS // 2, tk=D, name="proj_conv")
    gl = _mm(hm, W["gateT"], mode="nt", out_dtype=BF16, tm=TM, tn=GATE_COLS // 2, tk=D, name="proj_gate")
    q, k, v, qn, ckv = _mla_prep_fwd(lat, small["q_a_norm"], small["kv_a_norm"], ghq, ghk, W["wq"], W["wk"], W["wv"], rc, rs1, rs2,
                                     tm=512, name="mla_prep_fwd")
    (o, lse), got = _flash_fwd(q, k, v, n_seq=n_seq, seq=seq, tq=TQ, name="attn_fwd", hosted=ex.gather_chips("ffn2"))
    p = _conv_fwd(conv3, conv_w, n_seq=n_seq, seq=seq, name="conv_fwd")
    (x2, merged, ya, yb), got = _merge_fwd(o, p, gl, small["gate_bias"], x1, W["wpa"], W["wpc"], W["wout"], tm=512, name="merge_fwd",
                                           hosted=ex.gather_sibling(got))
    f2g, f2u, f2d = ex.ffn_weights(got)
    (y, h2, a2, b2), _ = _ffn_fwd(x2, small["ffn2_norm"], f2g, f2u, f2d, tm=TM, hc=HC, name="ffn2_fwd")
    dy, loss_row = _loss_head(y, tgt, tm=TM, name="loss_head")

    gw, gs = {}, {}
    (da2, db2, *ffn2_grads), _ = _ffn_grads(dy, h2, a2, b2, f2d, tm=TM, hc=HC, name="ffn2_grads")
    (dx2, gs["ffn2_norm"]), _ = _ffn_up_bwd(da2, db2, f2g, f2u, x2, small["ffn2_norm"], dy, tm=512, name="ffn2_up_bwd")

    (dx2b, dya, dyb, dgl, do, dp, gs["gate_bias"]), got = _merge_bwd(
        dx2, ya, yb, gl, small["gate_bias"], W["wpa"], W["wpc"], W["wout"], tm=512, name="merge_bwd",
        hosted=ex.scatter_sibling("ffn2", ffn2_grads))
    ex.scatter_sibling_done("ffn2", got)
    gw["wout"] = wgrad(merged, dx2b, "dw_out")[0]
    gw["wpa"] = wgrad(o, dya, "dw_pa")[0]
    gw["wpc"] = wgrad(p, dyb, "dw_pc")[0]
    dconv3, dconv_w = _conv_bwd(dp, conv3, conv_w, n_seq=n_seq, seq=seq, name="conv_bwd")
    (dq, dk, dv), got = _flash_bwd(q, k, v, o, lse, do, n_seq=n_seq, seq=seq, tq=TQ, name="attn_bwd",
                                   hosted=ex.scatter_chips("ffn2"))
    ex.scatter_chips_done("ffn2", got)
    dlat, dqp, dkp, gs["q_a_norm"], gs["kv_a_norm"], dghq, dghk = _mla_prep_bwd(
        dq, dk, dv, lat, qn, ckv, small["q_a_norm"], small["kv_a_norm"], ghq, ghk, W["wq"], W["wk"], W["wv"], rc, rs1, rs2,
        tm=512, name="mla_prep_bwd")
    gs["q_head_norm"], gs["k_head_norm"] = dghq[:, :QK_DIM], dghk[:, :QK_DIM]
    gw["wq"] = wgrad(qn, dqp, "dw_uq")[0]
    gw["wk"] = wgrad(ckv, dkp, "dw_uk")[0]
    gw["wv"] = wgrad(ckv, dv, "dw_uv")[0]
    gw["convT"] = wgrad(dconv3, hm, "dw_conv", tm=CONV_COLS // 2)[0]
    gw["gateT"] = wgrad(dgl, hm, "dw_gate")[0]
    gw["latT"] = wgrad(dlat, hm, "dw_lat")[0]
    ex.scatter_sibling_now("mix", gw)
    (dx1, gs["mix_norm"]), got = _proj_bwd(dlat, dconv3, dgl, W["latT"], W["convT"], W["gateT"], x1, small["mix_norm"], dx2,
                                           tm=512, name="proj_bwd", hosted=ex.scatter_chips("mix_in"))
    ex.scatter_chips_done("mix_in", got)

    (da1, db1, *ffn1_grads), got = _ffn_grads(dx1, h1, a1, b1, f1d, tm=TM, hc=HC, name="ffn1_grads",
                                              hosted=ex.scatter_chips("mix_misc"))
    ex.scatter_chips_done("mix_misc", got)
    ex.scatter_sibling_now("ffn1", ffn1_grads)
    (dx0, gs["ffn1_norm"]), got = _ffn_up_bwd(da1, db1, f1g, f1u, x0, small["ffn1_norm"], dx1, tm=512, name="ffn1_up_bwd",
                                              hosted=ex.scatter_chips("ffn1"))
    ex.scatter_chips_done("ffn1", got)
    return loss_row, dx0.reshape(n_seq, seq, d), dconv_w, gs


class _MeshExchange:
    def __init__(self, w, core, chip):
        self.w, self.core, self.chip = w, core, chip
        self.partial, self.received, self._packed = {}, {}, None

    def _blocks(self, group):
        w = self.w
        if group.startswith("ffn"):
            return [w[group + "_w_gate"].T.astype(BF16), w[group + "_w_up"].T.astype(BF16), w[group + "_w_down"].astype(BF16)]
        if self._packed is None:
            self._packed = _pack_weights(w)
        return [self._packed[0 if group == "mix_in" else 1]]

    def gather_chips(self, *groups):
        blocks = [b for group in groups for b in self._blocks(group)]
        return _gather_chips_plan(len(blocks)), blocks, _gather_shapes(blocks)

    def gather_sibling(self, got):
        half = list(got)
        return _gather_sibling_plan(len(half)), half, _same_shapes(half)

    def gather_now(self, group):
        plan, blocks, shapes = self.gather_chips(group)
        half = list(_run_plan(plan, blocks, shapes, name="gather_%s_chips" % group))
        return self.ffn_weights(_run_plan(_gather_sibling_plan(len(half)), half, _same_shapes(half), name="gather_%s_sibling" % group))

    def ffn_weights(self, got):
        return [a.reshape(DFF, D) for a in got]

    def mix_weights(self, got):
        return _unpack_weights(list(got))

    def _parts(self, group, grads):
        if group == "mix":
            return _pack_grads(grads), ["mix_in", "mix_misc"]
        parts = [g.reshape(N_DEV, -1, D) for g in grads]
        return parts, ([group] if len(parts) == 1 else None)

    def scatter_sibling(self, group, grads):
        self._sent, self._names = self._parts(group, grads)
        return _scatter_sibling_plan(len(self._sent)), self._sent, _halved_shapes(self._sent)

    def scatter_sibling_done(self, group, got):
        sums = [_sum_sibling(p, q, self.core, name="sum_%s_sibling_%d" % (group, i)) for i, (p, q) in enumerate(zip(self._sent, got))]
        if self._names is None:
            self.partial[group] = sums
        else:
            for n, s in zip(self._names, sums):
                self.partial[n] = [s]

    def scatter_sibling_now(self, group, grads):
        plan, parts, shapes = self.scatter_sibling(group, grads)
        self.scatter_sibling_done(group, _run_plan(plan, parts, shapes, name="scatter_%s_sibling" % group))

    def scatter_chips(self, group):
        s1 = self.partial[group]
        return _scatter_chips_plan(len(s1)), s1, _scatter_shapes(s1)

    def scatter_chips_done(self, group, got):
        self.received[group] = list(got)


SMALL_NAMES = ("ffn1_norm", "mix_norm", "gate_bias", "q_a_norm", "kv_a_norm", "q_head_norm", "k_head_norm", "ffn2_norm")
SMALL_SLOTS = {"ffn1_norm": 1024, "mix_norm": 1024, "gate_bias": 2048, "q_a_norm": 384, "kv_a_norm": 256, "q_head_norm": 128,
               "k_head_norm": 128, "ffn2_norm": 1024, "conv_w": 3072, "loss": 128}
COLUMN_MAJOR = ("w_in", "w_uq", "w_uk", "w_uv")
WEIGHT_NAMES = ("ffn1_norm", "ffn1_w_gate", "ffn1_w_up", "ffn1_w_down", "mix_norm", "w_in", "gate_bias", "q_a_norm", "w_uq",
                "kv_a_norm", "w_uk", "w_uv", "q_head_norm", "k_head_norm", "w_proj_attn", "conv_w", "w_proj_conv", "w_out",
                "ffn2_norm", "ffn2_w_gate", "ffn2_w_up", "ffn2_w_down")


def _step(x, positions, loss_target, w, m, v):
    xi, yi, ci = _place()
    core = ci.astype(jnp.int32).reshape(1)
    chip = (2 * xi + yi).astype(jnp.int32).reshape(1)
    me = 4 * xi + 2 * yi + ci

    cw_all = _small_exchange(jnp.pad(w["conv_w"], ((0, 5), (0, 0))), reduce=False, name="gather_conv_w")
    conv_w = cw_all[:, :3].transpose(1, 0, 2).reshape(3, D)
    small = {n: w[n].reshape(1, -1) for n in SMALL_NAMES}
    ex = _MeshExchange(w, core, chip)

    loss_row, grad_x, dconv_w, gs = _local_step(x, positions, loss_target, conv_w, small, ex)

    grads, deltas, new_m, new_v = {}, {}, {}, {}
    where = {"ffn1_w_gate": ("ffn1", 0), "ffn1_w_up": ("ffn1", 1), "ffn1_w_down": ("ffn1", 2),
             "ffn2_w_gate": ("ffn2", 0), "ffn2_w_up": ("ffn2", 1), "ffn2_w_down": ("ffn2", 2)}
    for n, (group, i) in where.items():
        transposed = not n.endswith("down")
        wv, mv, vv = (a[n].T if transposed else a[n] for a in (w, m, v))
        res = _sum_adamw(ex.partial[group][i], ex.received[group][i], chip, wv, mv, vv, name="adamw_" + n)
        grads[n], deltas[n], new_m[n], new_v[n] = (r.T if transposed else r for r in res)
    grads.update(_unpack_grads([_sum_chips(ex.partial[g][0], ex.received[g][0], chip, name="sum_%s_chips" % g)
                                for g in ("mix_in", "mix_misc")]))

    pieces = [_pad_cols(gs[n], SMALL_SLOTS[n]) for n in SMALL_NAMES] + [dconv_w.reshape(1, 3 * D), loss_row]
    total = _small_exchange(jnp.concatenate(pieces, axis=1).reshape(-1, 128), reduce=True, name="reduce_small").reshape(-1)
    off = 0
    for n in SMALL_NAMES:
        grads[n] = total[off:off + w[n].shape[0]]
        off += SMALL_SLOTS[n]
    conv_full = total[off:off + 3 * D].reshape(3, D)
    grads["conv_w"] = lax.dynamic_slice(conv_full, (0, me * HEAD_PAD), (3, HEAD_PAD))
    loss = total[off + 3 * D]

    for n in WEIGHT_NAMES:
        if n in deltas:
            continue
        shape = w[n].shape
        if n in COLUMN_MAJOR:
            ops = [a.T for a in (w[n], grads[n], m[n], v[n])]
            deltas[n], new_m[n], new_v[n] = (r.T for r in _adamw(*ops, name="adamw_" + n))
            continue
        if len(shape) == 1:
            view = (-1, 128) if shape[0] % 128 == 0 else (1, shape[0])
        else:
            view = shape
        dlt, nm, nv = _adamw(w[n].reshape(view), grads[n].reshape(view), m[n].reshape(view), v[n].reshape(view), name="adamw_" + n)
        deltas[n], new_m[n], new_v[n] = dlt.reshape(shape), nm.reshape(shape), nv.reshape(shape)
    return (loss, grad_x, *[grads[n] for n in WEIGHT_NAMES], *[deltas[n] for n in WEIGHT_NAMES],
            *[new_m[n] for n in WEIGHT_NAMES], *[new_v[n] for n in WEIGHT_NAMES])


def kernel(x, positions, ffn1_norm, ffn1_w_gate, ffn1_w_up, ffn1_w_down, mix_norm, w_in, gate_bias, q_a_norm, w_uq, kv_a_norm, w_uk, w_uv, q_head_norm, k_head_norm, w_proj_attn, conv_w, w_proj_conv, w_out, ffn2_norm, ffn2_w_gate, ffn2_w_up, ffn2_w_down, loss_target, m_ffn1_norm, m_ffn1_w_gate, m_ffn1_w_up, m_ffn1_w_down, m_mix_norm, m_w_in, m_gate_bias, m_q_a_norm, m_w_uq, m_kv_a_norm, m_w_uk, m_w_uv, m_q_head_norm, m_k_head_norm, m_w_proj_attn, m_conv_w, m_w_proj_conv, m_w_out, m_ffn2_norm, m_ffn2_w_gate, m_ffn2_w_up, m_ffn2_w_down, v_ffn1_norm, v_ffn1_w_gate, v_ffn1_w_up, v_ffn1_w_down, v_mix_norm, v_w_in, v_gate_bias, v_q_a_norm, v_w_uq, v_kv_a_norm, v_w_uk, v_w_uv, v_q_head_norm, v_k_head_norm, v_w_proj_attn, v_conv_w, v_w_proj_conv, v_w_out, v_ffn2_norm, v_ffn2_w_gate, v_ffn2_w_up, v_ffn2_w_down):
    given = dict(locals())
    w = {n: given[n] for n in WEIGHT_NAMES}
    m = {n: given["m_" + n] for n in WEIGHT_NAMES}
    v = {n: given["v_" + n] for n in WEIGHT_NAMES}
    return _step(x, positions, loss_target, w, m, v)
```

```python
import functools

import jax
import jax.numpy as jnp
from jax import lax
from jax.experimental import pallas as pl
from jax.experimental.pallas import tpu as pltpu

F32 = jnp.float32
BF16 = jnp.bfloat16
MESH = pl.DeviceIdType.MESH
ANY = pl.BlockSpec(memory_space=pl.ANY)

N_DEV = 8
D = 1024
DFF = 2816
N_HEADS = 8
HEAD_PAD = 128
QK_DIM = 96
NOPE = 64
ROPE_HALF = 16
Q_LORA = 384
KV_LORA = 256
LAT_PAD = 768
CONV_COLS = 3072
GATE_COLS = 2048
IN_DIM = 5792
IN_SHARD = IN_DIM // N_DEV
IN_SHARD_PAD = 736
FF_SHARD = DFF // N_DEV
ROPE_THETA = 10000.0
NORM_EPS = 1e-6
ATTN_SCALE = QK_DIM ** -0.5
NEG = -1e30

ADAM_LR, ADAM_B1, ADAM_B2, ADAM_EPS, ADAM_WD, ADAM_STEP = 0.001, 0.9, 0.999, 1e-08, 0.01, 10

PACK = ((("w_inT", IN_SHARD_PAD),), (("w_uq", 48), ("w_uk", 32), ("w_uv", 32), ("w_pa", 64), ("w_pc", 128), ("w_out", 128)))
PACK_OFF = {}
for _i, _group in enumerate(PACK):
    _o = 0
    for _n, _r in _group:
        PACK_OFF[_n] = (_i, _o, _r)
        _o += _r

VMEM_LIMIT = 56 * 1024 * 1024


def _params(*sem):
    return pltpu.CompilerParams(dimension_semantics=sem if sem else None, vmem_limit_bytes=VMEM_LIMIT)


class _Plan:
    def __init__(self, start, wait, n_remote, n_local, in_place=False):
        self.start, self.wait, self.n_remote, self.n_local, self.in_place = start, wait, n_remote, n_local, in_place

    def sems(self):
        return [pltpu.SemaphoreType.DMA((self.n_remote,)), pltpu.SemaphoreType.DMA((self.n_remote,)),
                pltpu.SemaphoreType.DMA((max(self.n_local, 1),))]


def _call(body, *, name, grid, in_specs, out_specs, out_shape, scratch_shapes, operands, sem, hosted=None):
    if hosted is None:
        outs = pl.pallas_call(body, name=name, grid=grid, in_specs=in_specs, out_specs=out_specs, out_shape=out_shape,
                              scratch_shapes=scratch_shapes, compiler_params=_params(*sem))(*operands)
        return outs, None
    plan, srcs, h_shapes = hosted
    n_in, n_out, n_scr, nh_in, nh_out = len(in_specs), len(out_specs), len(scratch_shapes), len(srcs), len(h_shapes)
    aliases = {n_in + a: n_out + a for a in range(nh_in)} if plan.in_place else {}

    def full_body(*refs):
        ins, refs = refs[:n_in], refs[n_in:]
        h_in, refs = refs[:nh_in], refs[nh_in:]
        outs, refs = refs[:n_out], refs[n_out:]
        h_out, refs = refs[:nh_out], refs[nh_out:]
        scr, sems = refs[:n_scr], refs[n_scr:]
        ids = [pl.program_id(ax) for ax in range(len(grid))]
        first = functools.reduce(jnp.logical_and, [i == 0 for i in ids])
        last = functools.reduce(jnp.logical_and, [i == g - 1 for i, g in zip(ids, grid)])

        @pl.when(first)
        def _():
            plan.start(h_in, h_out, *sems)

        body(*ins, *outs, *scr)

        @pl.when(last)
        def _():
            plan.wait(h_in, h_out, *sems)

    res = pl.pallas_call(
        full_body, name=name, grid=grid, in_specs=list(in_specs) + [ANY] * nh_in, out_specs=list(out_specs) + [ANY] * nh_out,
        out_shape=list(out_shape) + list(h_shapes), scratch_shapes=list(scratch_shapes) + plan.sems(),
        input_output_aliases=aliases, compiler_params=_params(*(["arbitrary"] * len(grid))),
    )(*operands, *srcs)
    return res[:n_out], res[n_out:]


def _dot_nn(a, b):
    return lax.dot_general(a, b, (((1,), (0,)), ((), ())), preferred_element_type=F32)


def _dot_nt(a, b):
    return lax.dot_general(a, b, (((1,), (1,)), ((), ())), preferred_element_type=F32)


def _dot_tn(a, b):
    return lax.dot_general(a, b, (((0,), (0,)), ((), ())), preferred_element_type=F32)


def _sigmoid(x):
    return 1.0 / (1.0 + jnp.exp(-x))


def _rms_stats(x):
    r = lax.rsqrt(jnp.mean(x * x, axis=-1, keepdims=True) + NORM_EPS)
    return x * r, r


ROWS_WIDE = 16
ROWS_NARROW = 32
MM_ROWS = 256


def _row_chunks(n_rows, rows, fn, unrolled=False):
    if unrolled:
        for c in range(n_rows // rows):
            fn(slice(c * rows, (c + 1) * rows))
        return

    def step(c, carry):
        fn(pl.ds(pl.multiple_of(c * rows, rows), rows))
        return carry

    lax.fori_loop(0, n_rows // rows, step, 0)


def _rms_bwd(dy, xhat, r, g):
    dg = jnp.sum(dy * xhat, axis=0, keepdims=True)
    dxh = dy * g
    dx = r * (dxh - xhat * jnp.mean(dxh * xhat, axis=-1, keepdims=True))
    return dx, dg


def _mm(a, b, *, mode, out_dtype, tm, tn, tk, name, add=None, scale=1.0, hosted=None):
    if mode == "nn":
        (m, k), (_, n) = a.shape, b.shape
    elif mode == "nt":
        (m, k), (n, _) = a.shape, b.shape
    else:
        (k, m), (_, n) = a.shape, b.shape
    assert m % tm == 0 and n % tn == 0 and k % tk == 0, (name, m, n, k, tm, tn, tk)
    nk = k // tk
    dot = {"nn": _dot_nn, "nt": _dot_nt, "tn": _dot_tn}[mode]
    a_spec = pl.BlockSpec((tk, tm), lambda i, j, kk: (kk, i)) if mode == "tn" else pl.BlockSpec((tm, tk), lambda i, j, kk: (i, kk))
    b_spec = pl.BlockSpec((tn, tk), lambda i, j, kk: (j, kk)) if mode == "nt" else pl.BlockSpec((tk, tn), lambda i, j, kk: (kk, j))
    o_spec = pl.BlockSpec((tm, tn), lambda i, j, kk: (i, j))
    has_add = add is not None

    def finish(prod, c_ref, o_ref):
        if scale != 1.0:
            prod = prod * scale
        o_ref[...] = ((c_ref[...] + prod) if has_add else prod).astype(out_dtype)

    def body(*refs):
        a_ref, b_ref = refs[:2]
        c_ref = refs[2] if has_add else None
        o_ref = refs[3] if has_add else refs[2]
        if nk == 1:
            finish(dot(a_ref[...], b_ref[...]), c_ref, o_ref)
            return
        acc_ref = refs[-1]
        kk = pl.program_id(2)

        @pl.when(kk == 0)
        def _():
            acc_ref[...] = jnp.zeros_like(acc_ref)

        acc_ref[...] += dot(a_ref[...], b_ref[...])

        @pl.when(kk == nk - 1)
        def _():
            finish(acc_ref[...], c_ref, o_ref)

    operands = (a, b, add) if has_add else (a, b)
    in_specs = [a_spec, b_spec] + ([o_spec] if has_add else [])
    (out,), got = _call(
        body, name=name, grid=(m // tm, n // tn, nk), in_specs=in_specs, out_specs=[o_spec],
        out_shape=[jax.ShapeDtypeStruct((m, n), out_dtype)], scratch_shapes=[pltpu.VMEM((tm, tn), F32)] if nk > 1 else [],
        operands=operands, sem=("parallel", "parallel", "arbitrary"), hosted=hosted)
    return out if hosted is None else (out, got)


def _rms_fwd(x, g, *, tm, name, hosted=None):
    t, d = x.shape

    def body(x_ref, g_ref, h_ref):
        xhat, _ = _rms_stats(x_ref[...])
        h_ref[...] = (xhat * g_ref[...]).astype(BF16)

    (h,), got = _call(
        body, name=name, grid=(t // tm,),
        in_specs=[pl.BlockSpec((tm, d), lambda i: (i, 0)), pl.BlockSpec((1, d), lambda i: (0, 0))],
        out_specs=[pl.BlockSpec((tm, d), lambda i: (i, 0))], out_shape=[jax.ShapeDtypeStruct((t, d), BF16)], scratch_shapes=[],
        operands=(x, g), sem=("parallel",), hosted=hosted)
    return h, got


def _rms_bwd_res(dh, x, g, dres, *, tm, name):
    t, d = x.shape

    def body(dh_ref, x_ref, g_ref, dres_ref, dx_ref, dg_ref):
        xhat, r = _rms_stats(x_ref[...])
        dx, dg = _rms_bwd(dh_ref[...], xhat, r, g_ref[...])
        dx_ref[...] = dres_ref[...] + dx

        @pl.when(pl.program_id(0) == 0)
        def _():
            dg_ref[...] = jnp.zeros_like(dg_ref)

        dg_ref[...] += dg

    row = pl.BlockSpec((tm, d), lambda i: (i, 0))
    vec = pl.BlockSpec((1, d), lambda i: (0, 0))
    return pl.pallas_call(
        body, name=name, grid=(t // tm,), in_specs=[row, row, vec, row], out_specs=[row, vec],
        out_shape=[jax.ShapeDtypeStruct((t, d), F32), jax.ShapeDtypeStruct((1, d), F32)],
        compiler_params=_params("arbitrary"),
    )(dh, x, g, dres)


def _ffn_fwd(x, g, wgT, wuT, wd, *, tm, hc, name, hosted=None):
    t, d = x.shape
    nj = DFF // hc

    def body(x_ref, g_ref, wg_ref, wu_ref, wd_ref, xo_ref, h_ref, a_ref, b_ref, acc_ref):
        j = pl.program_id(1)

        @pl.when(j == 0)
        def _():
            xhat, _ = _rms_stats(x_ref[...])
            h_ref[...] = (xhat * g_ref[...]).astype(BF16)
            acc_ref[...] = jnp.zeros_like(acc_ref)

        h = h_ref[...]
        a = _dot_nt(h, wg_ref[...])
        b = _dot_nt(h, wu_ref[...])
        a_ref[...] = a.astype(BF16)
        b_ref[...] = b.astype(BF16)
        s = (a * _sigmoid(a) * b).astype(BF16)
        acc_ref[...] += _dot_nn(s, wd_ref[...])

        @pl.when(j == nj - 1)
        def _():
            xo_ref[...] = x_ref[...] + 0.5 * acc_ref[...]

    row = pl.BlockSpec((tm, d), lambda i, j: (i, 0))
    vec = pl.BlockSpec((1, d), lambda i, j: (0, 0))
    wsp = pl.BlockSpec((hc, d), lambda i, j: (j, 0))
    hid = pl.BlockSpec((tm, hc), lambda i, j: (i, j))
    return _call(
        body, name=name, grid=(t // tm, nj), in_specs=[row, vec, wsp, wsp, wsp], out_specs=[row, row, hid, hid],
        out_shape=[jax.ShapeDtypeStruct((t, d), F32), jax.ShapeDtypeStruct((t, d), BF16),
                   jax.ShapeDtypeStruct((t, DFF), BF16), jax.ShapeDtypeStruct((t, DFF), BF16)],
        scratch_shapes=[pltpu.VMEM((tm, d), F32)], operands=(x, g, wgT, wuT, wd), sem=("parallel", "arbitrary"), hosted=hosted)


def _ffn_grads(dout, h, a, b, wd, *, tm, hc, name, hosted=None):
    t, d = dout.shape
    ni, nj = t // tm, DFF // hc

    def body(dout_ref, h_ref, a_ref, b_ref, wd_ref, da_ref, db_ref, dwg_ref, dwu_ref, dwd_ref,
             dy_all, h_all, ds_scr, s_scr, acc_g, acc_u, acc_d):
        j, i = pl.program_id(0), pl.program_id(1)
        rows_i = pl.ds(pl.multiple_of(i * tm, tm), tm)

        @pl.when(j == 0)
        def _():
            dy_all[rows_i, :] = (0.5 * dout_ref[...]).astype(BF16)
            h_all[rows_i, :] = h_ref[...]

        @pl.when(i == 0)
        def _():
            acc_g[...] = jnp.zeros_like(acc_g)
            acc_u[...] = jnp.zeros_like(acc_u)
            acc_d[...] = jnp.zeros_like(acc_d)

        def grad_rows(rows):
            ds = ds_scr[rows, :]
            av = a_ref[rows, :].astype(F32)
            bv = b_ref[rows, :].astype(F32)
            sg = _sigmoid(av)
            sl = av * sg
            s_scr[rows, :] = (sl * bv).astype(BF16)
            da_ref[rows, :] = (ds * bv * (sg + sl * (1.0 - sg))).astype(BF16)
            db_ref[rows, :] = (ds * sl).astype(BF16)

        for blk in range(tm // MM_ROWS):
            rs = slice(blk * MM_ROWS, (blk + 1) * MM_ROWS)
            ds_scr[rs, :] = _dot_nt(dy_all[pl.ds(pl.multiple_of(i * tm + blk * MM_ROWS, MM_ROWS), MM_ROWS), :], wd_ref[...])
            for c in range(MM_ROWS // ROWS_WIDE):
                grad_rows(slice(blk * MM_ROWS + c * ROWS_WIDE, blk * MM_ROWS + (c + 1) * ROWS_WIDE))

        dy_i = dy_all[rows_i, :]
        h_i = h_all[rows_i, :]
        acc_d[...] += _dot_tn(s_scr[...], dy_i)
        acc_g[...] += _dot_tn(da_ref[...], h_i)
        acc_u[...] += _dot_tn(db_ref[...], h_i)

        @pl.when(i == ni - 1)
        def _():
            dwg_ref[...] = acc_g[...].astype(BF16)
            dwu_ref[...] = acc_u[...].astype(BF16)
            dwd_ref[...] = acc_d[...].astype(BF16)

    first = pl.BlockSpec((tm, d), lambda j, i: (jnp.where(j == 0, i, 0), 0))
    hid = pl.BlockSpec((tm, hc), lambda j, i: (i, j))
    wsp = pl.BlockSpec((hc, d), lambda j, i: (j, 0))
    hid_shape = jax.ShapeDtypeStruct((t, DFF), BF16)
    w_shape = jax.ShapeDtypeStruct((DFF, d), BF16)
    return _call(
        body, name=name, grid=(nj, ni), in_specs=[first, first, hid, hid, wsp], out_specs=[hid, hid, wsp, wsp, wsp],
        out_shape=[hid_shape, hid_shape, w_shape, w_shape, w_shape],
        scratch_shapes=[pltpu.VMEM((t, d), BF16), pltpu.VMEM((t, d), BF16), pltpu.VMEM((tm, hc), F32), pltpu.VMEM((tm, hc), BF16),
                        pltpu.VMEM((hc, d), F32), pltpu.VMEM((hc, d), F32), pltpu.VMEM((hc, d), F32)],
        operands=(dout, h, a, b, wd), sem=("arbitrary", "arbitrary"), hosted=hosted)


def _proj_bwd(dlat, dconv3, dgl, latT, convT, gateT, x, g, dres, *, tm, name, hosted=None):
    t, d = x.shape

    def body(dl_ref, dc_ref, dg_ref, wl_ref, wc_ref, wg_ref, x_ref, g_ref, dres_ref, dx_ref, dgain_ref):
        @pl.when(pl.program_id(0) == 0)
        def _():
            dgain_ref[...] = jnp.zeros_like(dgain_ref)

        dh = _dot_nn(dl_ref[...], wl_ref[...]) + _dot_nn(dc_ref[...], wc_ref[...]) + _dot_nn(dg_ref[...], wg_ref[...])
        xhat, r = _rms_stats(x_ref[...])
        dx, dgain = _rms_bwd(dh, xhat, r, g_ref[...])
        dx_ref[...] = dres_ref[...] + dx
        dgain_ref[...] += dgain

    def rows(w):
        return pl.BlockSpec((tm, w), lambda i: (i, 0))

    def full(r):
        return pl.BlockSpec((r, d), lambda i: (0, 0))

    return _call(
        body, name=name, grid=(t // tm,),
        in_specs=[rows(LAT_PAD), rows(CONV_COLS), rows(GATE_COLS), full(LAT_PAD), full(CONV_COLS), full(GATE_COLS), rows(d), full(1), rows(d)],
        out_specs=[rows(d), full(1)], out_shape=[jax.ShapeDtypeStruct((t, d), F32), jax.ShapeDtypeStruct((1, d), F32)],
        scratch_shapes=[], operands=(dlat, dconv3, dgl, latT, convT, gateT, x, g, dres), sem=("arbitrary",), hosted=hosted)


def _ffn_up_bwd(da, db, wgT, wuT, x, g, dout, *, tm, name, hosted=None):
    t, d = x.shape

    def body(da_ref, db_ref, wg_ref, wu_ref, x_ref, g_ref, dout_ref, dx_ref, dg_ref):
        @pl.when(pl.program_id(0) == 0)
        def _():
            dg_ref[...] = jnp.zeros_like(dg_ref)

        dh = _dot_nn(da_ref[...], wg_ref[...]) + _dot_nn(db_ref[...], wu_ref[...])
        xhat, r = _rms_stats(x_ref[...])
        dx, dg = _rms_bwd(dh, xhat, r, g_ref[...])
        dx_ref[...] = dout_ref[...] + dx
        dg_ref[...] += dg

    row = pl.BlockSpec((tm, d), lambda i: (i, 0))
    vec = pl.BlockSpec((1, d), lambda i: (0, 0))
    hid = pl.BlockSpec((tm, DFF), lambda i: (i, 0))
    wsp = pl.BlockSpec((DFF, d), lambda i: (0, 0))
    return _call(
        body, name=name, grid=(t // tm,), in_specs=[hid, hid, wsp, wsp, row, vec, row], out_specs=[row, vec],
        out_shape=[jax.ShapeDtypeStruct((t, d), F32), jax.ShapeDtypeStruct((1, d), F32)], scratch_shapes=[],
        operands=(da, db, wgT, wuT, x, g, dout), sem=("arbitrary",), hosted=hosted)


def _rope_fwd(x, c, s1, s2):
    return x * c + pltpu.roll(x, HEAD_PAD - ROPE_HALF, 1) * s1 + pltpu.roll(x, ROPE_HALF, 1) * s2


def _rope_bwd(dy, c, s1, s2):
    return dy * c + pltpu.roll(dy * s1, ROPE_HALF, 1) + pltpu.roll(dy * s2, HEAD_PAD - ROPE_HALF, 1)


def _head_stats(x):
    r = lax.rsqrt(jnp.sum(x * x, axis=-1, keepdims=True) * (1.0 / QK_DIM) + NORM_EPS)
    return x * r, r


def _mla_prep_fwd(lat, gq, gkv, ghq, ghk, wq, wk, wv, rc, rs1, rs2, *, tm, name):
    t = lat.shape[0]

    def body(lat_ref, gq_ref, gkv_ref, ghq_ref, ghk_ref, wq_ref, wk_ref, wv_ref, c_ref, s1_ref, s2_ref,
             q_ref, k_ref, v_ref, qn_ref, ckv_ref):
        lat_v = lat_ref[...]
        qhat, _ = _rms_stats(lat_v[:, :Q_LORA].astype(F32))
        qn = (qhat * gq_ref[...]).astype(BF16)
        khat, _ = _rms_stats(lat_v[:, Q_LORA:Q_LORA + KV_LORA].astype(F32))
        ckv = (khat * gkv_ref[...]).astype(BF16)
        ckv_ext = jnp.concatenate([ckv, lat_v[:, Q_LORA + KV_LORA:]], axis=1)
        qn_ref[...] = qn
        ckv_ref[...] = ckv_ext
        q_pre = _dot_nn(qn, wq_ref[...])
        k_pre = _dot_nn(ckv_ext, wk_ref[...])
        v_ref[...] = _dot_nn(ckv, wv_ref[...]).astype(BF16)
        c, s1, s2 = c_ref[...], s1_ref[...], s2_ref[...]
        for h in range(N_HEADS):
            hs = slice(h * HEAD_PAD, (h + 1) * HEAD_PAD)
            xq, _ = _head_stats(q_pre[:, hs])
            q_ref[:, hs] = _rope_fwd(xq * ghq_ref[...], c, s1, s2).astype(BF16)
            xk, _ = _head_stats(k_pre[:, hs])
            k_ref[:, hs] = _rope_fwd(xk * ghk_ref[...], c, s1, s2).astype(BF16)

    def row(w):
        return pl.BlockSpec((tm, w), lambda i: (i, 0))

    def full(r, w):
        return pl.BlockSpec((r, w), lambda i: (0, 0))

    wide = jax.ShapeDtypeStruct((t, D), BF16)
    lat3 = jax.ShapeDtypeStruct((t, Q_LORA), BF16)
    return pl.pallas_call(
        body, name=name, grid=(t // tm,),
        in_specs=[row(LAT_PAD), full(1, Q_LORA), full(1, KV_LORA), full(1, HEAD_PAD), full(1, HEAD_PAD),
                  full(Q_LORA, D), full(Q_LORA, D), full(KV_LORA, D), row(HEAD_PAD), row(HEAD_PAD), row(HEAD_PAD)],
        out_specs=[row(D), row(D), row(D), row(Q_LORA), row(Q_LORA)],
        out_shape=[wide, wide, wide, lat3, lat3],
        compiler_params=_params("parallel"),
    )(lat, gq, gkv, ghq, ghk, wq, wk, wv, rc, rs1, rs2)


def _mla_prep_bwd(dq, dk, dv, lat, qn, ckv_ext, gq, gkv, ghq, ghk, wq, wk, wv, rc, rs1, rs2, *, tm, name):
    t = lat.shape[0]

    def body(dq_ref, dk_ref, dv_ref, lat_ref, qn_ref, ckv_ref, gq_ref, gkv_ref, ghq_ref, ghk_ref, wq_ref, wk_ref, wv_ref,
             c_ref, s1_ref, s2_ref, dlat_ref, dqp_ref, dkp_ref, dgq_ref, dgkv_ref, dghq_ref, dghk_ref):
        @pl.when(pl.program_id(0) == 0)
        def _():
            dgq_ref[...] = jnp.zeros_like(dgq_ref)
            dgkv_ref[...] = jnp.zeros_like(dgkv_ref)
            dghq_ref[...] = jnp.zeros_like(dghq_ref)
            dghk_ref[...] = jnp.zeros_like(dghk_ref)

        c, s1, s2 = c_ref[...], s1_ref[...], s2_ref[...]
        q_pre = _dot_nn(qn_ref[...], wq_ref[...])
        k_pre = _dot_nn(ckv_ref[...], wk_ref[...])

        def heads(pre, dy_ref, gh_ref, dgh_ref, out_ref):
            dgh = jnp.zeros((1, HEAD_PAD), F32)
            for h in range(N_HEADS):
                hs = slice(h * HEAD_PAD, (h + 1) * HEAD_PAD)
                d = _rope_bwd(dy_ref[:, hs].astype(F32), c, s1, s2)
                xhat, r = _head_stats(pre[:, hs])
                dgh = dgh + jnp.sum(d * xhat, axis=0, keepdims=True)
                dxh = d * gh_ref[...]
                dx = r * (dxh - xhat * (jnp.sum(dxh * xhat, axis=-1, keepdims=True) * (1.0 / QK_DIM)))
                out_ref[:, hs] = dx.astype(BF16)
            dgh_ref[...] += dgh

        heads(q_pre, dq_ref, ghq_ref, dghq_ref, dqp_ref)
        heads(k_pre, dk_ref, ghk_ref, dghk_ref, dkp_ref)
        dqn = _dot_nt(dqp_ref[...], wq_ref[...])
        dce = _dot_nt(dkp_ref[...], wk_ref[...])
        dckv = dce[:, :KV_LORA] + _dot_nt(dv_ref[...], wv_ref[...])
        lat_v = lat_ref[...]
        qhat, rq = _rms_stats(lat_v[:, :Q_LORA].astype(F32))
        dql, dgq = _rms_bwd(dqn, qhat, rq, gq_ref[...])
        khat, rk = _rms_stats(lat_v[:, Q_LORA:Q_LORA + KV_LORA].astype(F32))
        dkl, dgkv = _rms_bwd(dckv, khat, rk, gkv_ref[...])
        dgq_ref[...] += dgq
        dgkv_ref[...] += dgkv
        dlat_ref[...] = jnp.concatenate([dql, dkl, dce[:, KV_LORA:]], axis=1).astype(BF16)

    def row(w):
        return pl.BlockSpec((tm, w), lambda i: (i, 0))

    def full(r, w):
        return pl.BlockSpec((r, w), lambda i: (0, 0))

    return pl.pallas_call(
        body, name=name, grid=(t // tm,),
        in_specs=[row(D), row(D), row(D), row(LAT_PAD), row(Q_LORA), row(Q_LORA), full(1, Q_LORA), full(1, KV_LORA),
                  full(1, HEAD_PAD), full(1, HEAD_PAD), full(Q_LORA, D), full(Q_LORA, D), full(KV_LORA, D),
                  row(HEAD_PAD), row(HEAD_PAD), row(HEAD_PAD)],
        out_specs=[row(LAT_PAD), row(D), row(D), full(1, Q_LORA), full(1, KV_LORA), full(1, HEAD_PAD), full(1, HEAD_PAD)],
        out_shape=[jax.ShapeDtypeStruct((t, LAT_PAD), BF16), jax.ShapeDtypeStruct((t, D), BF16), jax.ShapeDtypeStruct((t, D), BF16),
                   jax.ShapeDtypeStruct((1, Q_LORA), F32), jax.ShapeDtypeStruct((1, KV_LORA), F32),
                   jax.ShapeDtypeStruct((1, HEAD_PAD), F32), jax.ShapeDtypeStruct((1, HEAD_PAD), F32)],
        compiler_params=_params("arbitrary"),
    )(dq, dk, dv, lat, qn, ckv_ext, gq, gkv, ghq, ghk, wq, wk, wv, rc, rs1, rs2)


ATT_ROWS = 16
ATT_HALF = 256


def _lanes(stat, width):
    return jnp.tile(stat, (1, width // HEAD_PAD))


def _keep_causal(r0, width):
    row = r0 + lax.broadcasted_iota(jnp.int32, (ATT_ROWS, width), 0)
    return lax.broadcasted_iota(jnp.int32, (ATT_ROWS, width), 1) <= row


def _flash_fwd(q, k, v, *, n_seq, seq, tq, name, hosted=None):
    nq = seq // tq

    def body(q_ref, k_ref, v_ref, o_ref, lse_ref, s_scr, p_scr, m_scr, l_scr, alpha_scr, acc_scr):
        qi = pl.program_id(2)
        m_scr[...] = jnp.full_like(m_scr, NEG)
        l_scr[...] = jnp.zeros_like(l_scr)
        acc_scr[...] = jnp.zeros_like(acc_scr)

        def tile(j, masked):
            cols = pl.ds(pl.multiple_of(j * tq, tq), tq)
            kj, vj = k_ref[cols, :], v_ref[cols, :]
            for half in range(tq // ATT_HALF):
                hs = slice(half * ATT_HALF, (half + 1) * ATT_HALF)
                s_scr[hs, :] = _dot_nt(q_ref[hs, :], kj)
                for c in range(ATT_HALF // ATT_ROWS):
                    r0 = half * ATT_HALF + c * ATT_ROWS
                    rows = slice(r0, r0 + ATT_ROWS)
                    s = s_scr[rows, :] * ATTN_SCALE
                    if masked:
                        s = jnp.where(_keep_causal(r0, tq), s, NEG)
                    m_prev = m_scr[rows, :]
                    m_new = jnp.maximum(m_prev, jnp.max(s, axis=-1, keepdims=True))
                    alpha = jnp.exp(m_prev - m_new)
                    p = jnp.exp(s - _lanes(m_new, tq))
                    l_scr[rows, :] = alpha * l_scr[rows, :] + jnp.sum(p, axis=-1, keepdims=True)
                    m_scr[rows, :] = m_new
                    alpha_scr[rows, :] = alpha
                    p_scr[rows, :] = p.astype(BF16)
                acc_scr[hs, :] = alpha_scr[hs, :] * acc_scr[hs, :] + _dot_nn(p_scr[hs, :], vj)

        def unmasked(j, carry):
            tile(j, False)
            return carry

        lax.fori_loop(0, qi, unmasked, 0)
        tile(qi, True)
        o_ref[...] = (acc_scr[...] / l_scr[...]).astype(BF16)
        lse_ref[...] = m_scr[...] + jnp.log(l_scr[...])

    qspec = pl.BlockSpec((tq, HEAD_PAD), lambda b, h, i: (b * nq + i, h))
    kspec = pl.BlockSpec((seq, HEAD_PAD), lambda b, h, i: (b, h))
    t = n_seq * seq
    stat = pltpu.VMEM((tq, HEAD_PAD), F32)
    return _call(
        body, name=name, grid=(n_seq, N_HEADS, nq), in_specs=[qspec, kspec, kspec], out_specs=[qspec, qspec],
        out_shape=[jax.ShapeDtypeStruct((t, D), BF16), jax.ShapeDtypeStruct((t, D), F32)],
        scratch_shapes=[pltpu.VMEM((tq, tq), F32), pltpu.VMEM((tq, tq), BF16), stat, stat, stat, stat],
        operands=(q, k, v), sem=("parallel", "parallel", "arbitrary"), hosted=hosted)


def _flash_bwd(q, k, v, o, lse, do, *, n_seq, seq, tq, name, hosted=None):
    nq = seq // tq

    def body(q_ref, k_ref, v_ref, o_ref, lse_ref, do_ref, dq_ref, dk_ref, dv_ref,
             s_scr, dp_scr, p_scr, ds_scr, delta_scr, dq_acc, dk_acc, dv_acc):
        j = pl.program_id(2)

        @pl.when(j == 0)
        def _():
            dq_acc[...] = jnp.zeros_like(dq_acc)
            delta = jnp.sum(do_ref[...].astype(F32) * o_ref[...].astype(F32), axis=-1, keepdims=True)
            delta_scr[...] = jnp.broadcast_to(delta, delta_scr.shape)

        dk_acc[...] = jnp.zeros_like(dk_acc)
        dv_acc[...] = jnp.zeros_like(dv_acc)
        kv, vv = k_ref[...], v_ref[...]

        def tile(i, masked):
            for half in range(tq // ATT_HALF):
                hs = slice(half * ATT_HALF, (half + 1) * ATT_HALF)
                hrows = pl.ds(pl.multiple_of(i * tq + half * ATT_HALF, ATT_HALF), ATT_HALF)
                qh, doh = q_ref[hrows, :], do_ref[hrows, :]
                s_scr[hs, :] = _dot_nt(qh, kv)
                dp_scr[hs, :] = _dot_nt(doh, vv)
                for c in range(ATT_HALF // ATT_ROWS):
                    r0 = half * ATT_HALF + c * ATT_ROWS
                    rows = slice(r0, r0 + ATT_ROWS)
                    grows = pl.ds(pl.multiple_of(i * tq + r0, ATT_ROWS), ATT_ROWS)
                    p = jnp.exp(s_scr[rows, :] * ATTN_SCALE - _lanes(lse_ref[grows, :], tq))
                    if masked:
                        p = jnp.where(_keep_causal(r0, tq), p, 0.0)
                    ds = p * (dp_scr[rows, :] - _lanes(delta_scr[grows, :], tq)) * ATTN_SCALE
                    p_scr[rows, :] = p.astype(BF16)
                    ds_scr[rows, :] = ds.astype(BF16)
                dv_acc[...] += _dot_tn(p_scr[hs, :], doh)
                dk_acc[...] += _dot_tn(ds_scr[hs, :], qh)
                dq_acc[hrows, :] += _dot_nn(ds_scr[hs, :], kv)

        tile(j, True)

        def unmasked(i, carry):
            tile(i, False)
            return carry

        lax.fori_loop(j + 1, nq, unmasked, 0)
        dk_ref[...] = dk_acc[...].astype(BF16)
        dv_ref[...] = dv_acc[...].astype(BF16)

        @pl.when(j == nq - 1)
        def _():
            dq_ref[...] = dq_acc[...].astype(BF16)

    full = pl.BlockSpec((seq, HEAD_PAD), lambda b, h, j: (b, h))
    tile_spec = pl.BlockSpec((tq, HEAD_PAD), lambda b, h, j: (b * nq + j, h))
    t = n_seq * seq
    wide = jax.ShapeDtypeStruct((t, D), BF16)
    return _call(
        body, name=name, grid=(n_seq, N_HEADS, nq), in_specs=[full, tile_spec, tile_spec, full, full, full],
        out_specs=[full, tile_spec, tile_spec], out_shape=[wide, wide, wide],
        scratch_shapes=[pltpu.VMEM((tq, tq), F32), pltpu.VMEM((tq, tq), F32), pltpu.VMEM((tq, tq), BF16), pltpu.VMEM((tq, tq), BF16),
                        pltpu.VMEM((seq, HEAD_PAD), F32), pltpu.VMEM((seq, HEAD_PAD), F32),
                        pltpu.VMEM((tq, HEAD_PAD), F32), pltpu.VMEM((tq, HEAD_PAD), F32)],
        operands=(q, k, v, o, lse, do), sem=("parallel", "parallel", "arbitrary"), hosted=hosted)


CONV_CB = 256


def _shift_down(u, k, row):
    return jnp.where(row >= k, pltpu.roll(u, k, 0), 0.0)


def _shift_up(u, k, row, n):
    return jnp.where(row < n - k, pltpu.roll(u, n - k, 0), 0.0)


def _conv_fwd(conv3, cw, *, n_seq, seq, name):
    def body(c_ref, w_ref, p_ref):
        blk = c_ref[...].astype(F32)
        xc, gb, gc = blk[:, :CONV_CB], blk[:, CONV_CB:2 * CONV_CB], blk[:, 2 * CONV_CB:]
        row = lax.broadcasted_iota(jnp.int32, (seq, CONV_CB), 0)
        u = gc * xc
        z = w_ref[0:1, :] * _shift_down(u, 2, row) + w_ref[1:2, :] * _shift_down(u, 1, row) + w_ref[2:3, :] * u
        p_ref[...] = (gb * z).astype(BF16)

    return pl.pallas_call(
        body, name=name, grid=(n_seq, D // CONV_CB),
        in_specs=[pl.BlockSpec((seq, 3 * CONV_CB), lambda b, j: (b, j)), pl.BlockSpec((3, CONV_CB), lambda b, j: (0, j))],
        out_specs=pl.BlockSpec((seq, CONV_CB), lambda b, j: (b, j)),
        out_shape=jax.ShapeDtypeStruct((n_seq * seq, D), BF16),
        compiler_params=_params("parallel", "parallel"),
    )(conv3, cw)


def _conv_bwd(dp, conv3, cw, *, n_seq, seq, name):
    def body(dp_ref, c_ref, w_ref, dc_ref, dw_ref):
        @pl.when(pl.program_id(1) == 0)
        def _():
            dw_ref[...] = jnp.zeros_like(dw_ref)

        blk = c_ref[...].astype(F32)
        xc, gb, gc = blk[:, :CONV_CB], blk[:, CONV_CB:2 * CONV_CB], blk[:, 2 * CONV_CB:]
        row = lax.broadcasted_iota(jnp.int32, (seq, CONV_CB), 0)
        w0, w1, w2 = w_ref[0:1, :], w_ref[1:2, :], w_ref[2:3, :]
        u = gc * xc
        u1 = _shift_down(u, 1, row)
        u2 = _shift_down(u, 2, row)
        z = w0 * u2 + w1 * u1 + w2 * u
        dpv = dp_ref[...].astype(F32)
        dz = dpv * gb
        du = w2 * dz + w1 * _shift_up(dz, 1, row, seq) + w0 * _shift_up(dz, 2, row, seq)
        dc_ref[...] = jnp.concatenate([du * gc, dpv * z, du * xc], axis=1).astype(BF16)
        dw_ref[0:1, :] += jnp.sum(dz * u2, axis=0, keepdims=True)
        dw_ref[1:2, :] += jnp.sum(dz * u1, axis=0, keepdims=True)
        dw_ref[2:3, :] += jnp.sum(dz * u, axis=0, keepdims=True)

    return pl.pallas_call(
        body, name=name, grid=(D // CONV_CB, n_seq),
        in_specs=[pl.BlockSpec((seq, CONV_CB), lambda j, b: (b, j)), pl.BlockSpec((seq, 3 * CONV_CB), lambda j, b: (b, j)),
                  pl.BlockSpec((3, CONV_CB), lambda j, b: (0, j))],
        out_specs=[pl.BlockSpec((seq, 3 * CONV_CB), lambda j, b: (b, j)), pl.BlockSpec((3, CONV_CB), lambda j, b: (0, j))],
        out_shape=[jax.ShapeDtypeStruct((n_seq * seq, CONV_COLS), BF16), jax.ShapeDtypeStruct((3, D), F32)],
        compiler_params=_params("parallel", "arbitrary"),
    )(dp, conv3, cw)


def _merge_fwd(o, p, gl, bias, x1, wpa, wpc, wout, *, tm, name, hosted=None):
    t = x1.shape[0]

    def body(o_ref, p_ref, gl_ref, b_ref, x_ref, wpa_ref, wpc_ref, wout_ref, x2_ref, mg_ref, ya_ref, yb_ref):
        ya = _dot_nn(o_ref[...], wpa_ref[...])
        yb = _dot_nn(p_ref[...], wpc_ref[...])
        gates = _sigmoid(gl_ref[...].astype(F32) + b_ref[...])
        merged = (gates[:, :D] * ya + gates[:, D:] * yb).astype(BF16)
        ya_ref[...] = ya.astype(BF16)
        yb_ref[...] = yb.astype(BF16)
        mg_ref[...] = merged
        x2_ref[...] = x_ref[...] + _dot_nn(merged, wout_ref[...])

    row = pl.BlockSpec((tm, D), lambda i: (i, 0))
    row2 = pl.BlockSpec((tm, GATE_COLS), lambda i: (i, 0))
    wsp = pl.BlockSpec((D, D), lambda i: (0, 0))
    wide = jax.ShapeDtypeStruct((t, D), BF16)
    return _call(
        body, name=name, grid=(t // tm,),
        in_specs=[row, row, row2, pl.BlockSpec((1, GATE_COLS), lambda i: (0, 0)), row, wsp, wsp, wsp],
        out_specs=[row, row, row, row], out_shape=[jax.ShapeDtypeStruct((t, D), F32), wide, wide, wide], scratch_shapes=[],
        operands=(o, p, gl, bias, x1, wpa, wpc, wout), sem=("parallel",), hosted=hosted)


def _merge_bwd(dx2, ya, yb, gl, bias, wpa, wpc, wout, *, tm, name, hosted=None):
    t = dx2.shape[0]

    def body(dx_ref, ya_ref, yb_ref, gl_ref, b_ref, wpa_ref, wpc_ref, wout_ref,
             dxb_ref, dya_ref, dyb_ref, dgl_ref, do_ref, dp_ref, db_ref):
        @pl.when(pl.program_id(0) == 0)
        def _():
            db_ref[...] = jnp.zeros_like(db_ref)

        dxb = dx_ref[...].astype(BF16)
        dxb_ref[...] = dxb
        dm = _dot_nt(dxb, wout_ref[...])
        gates = _sigmoid(gl_ref[...].astype(F32) + b_ref[...])
        ga, gb = gates[:, :D], gates[:, D:]
        dya = (dm * ga).astype(BF16)
        dyb = (dm * gb).astype(BF16)
        dya_ref[...] = dya
        dyb_ref[...] = dyb
        dgl = jnp.concatenate([dm * ya_ref[...].astype(F32) * ga * (1.0 - ga),
                               dm * yb_ref[...].astype(F32) * gb * (1.0 - gb)], axis=1)
        dgl_ref[...] = dgl.astype(BF16)
        db_ref[...] += jnp.sum(dgl, axis=0, keepdims=True)
        do_ref[...] = _dot_nt(dya, wpa_ref[...]).astype(BF16)
        dp_ref[...] = _dot_nt(dyb, wpc_ref[...]).astype(BF16)

    row = pl.BlockSpec((tm, D), lambda i: (i, 0))
    row2 = pl.BlockSpec((tm, GATE_COLS), lambda i: (i, 0))
    vec2 = pl.BlockSpec((1, GATE_COLS), lambda i: (0, 0))
    wsp = pl.BlockSpec((D, D), lambda i: (0, 0))
    wide = jax.ShapeDtypeStruct((t, D), BF16)
    return _call(
        body, name=name, grid=(t // tm,), in_specs=[row, row, row, row2, vec2, wsp, wsp, wsp],
        out_specs=[row, row, row, row2, row, row, vec2],
        out_shape=[wide, wide, wide, jax.ShapeDtypeStruct((t, GATE_COLS), BF16), wide, wide,
                   jax.ShapeDtypeStruct((1, GATE_COLS), F32)],
        scratch_shapes=[], operands=(dx2, ya, yb, gl, bias, wpa, wpc, wout), sem=("arbitrary",), hosted=hosted)


def _loss_head(y, target, *, tm, name):
    t, d = y.shape

    def body(y_ref, t_ref, dy_ref, loss_ref):
        @pl.when(pl.program_id(0) == 0)
        def _():
            loss_ref[...] = jnp.zeros_like(loss_ref)

        err = y_ref[...] - t_ref[...]
        dy_ref[...] = err * (1.0 / d)
        loss_ref[...] += jnp.sum(jnp.sum(err * err, axis=-1, keepdims=True), axis=0, keepdims=True) * (0.5 / d)

    row = pl.BlockSpec((tm, d), lambda i: (i, 0))
    return pl.pallas_call(
        body, name=name, grid=(t // tm,), in_specs=[row, row], out_specs=[row, pl.BlockSpec((1, 128), lambda i: (0, 0))],
        out_shape=[jax.ShapeDtypeStruct((t, d), F32), jax.ShapeDtypeStruct((1, 128), F32)],
        compiler_params=_params("arbitrary"),
    )(y, target)


def _adamw(w, g, m, v, *, name):
    rows, cols = w.shape
    tr = max([c for c in range(8, 513, 8) if rows % c == 0], default=rows)
    c1 = 1.0 / (1.0 - ADAM_B1 ** ADAM_STEP)
    c2 = 1.0 / (1.0 - ADAM_B2 ** ADAM_STEP)

    def body(w_ref, g_ref, m_ref, v_ref, d_ref, nm_ref, nv_ref):
        gv = g_ref[...]
        nm = ADAM_B1 * m_ref[...] + (1.0 - ADAM_B1) * gv
        nv = ADAM_B2 * v_ref[...] + (1.0 - ADAM_B2) * (gv * gv)
        nm_ref[...] = nm
        nv_ref[...] = nv
        d_ref[...] = -ADAM_LR * ((nm * c1) / (jnp.sqrt(nv * c2) + ADAM_EPS) + ADAM_WD * w_ref[...])

    spec = pl.BlockSpec((tr, cols), lambda i: (i, 0))
    shp = jax.ShapeDtypeStruct((rows, cols), F32)
    return pl.pallas_call(
        body, name=name, grid=(rows // tr,), in_specs=[spec] * 4, out_specs=[spec] * 3, out_shape=[shp] * 3,
        compiler_params=_params("parallel"),
    )(w, g, m, v)


def _place():
    return lax.axis_index("x"), lax.axis_index("y"), lax.axis_index("c")


def _other_chips(x, y):
    return [(1 - x, y), (x, 1 - y), (1 - x, 1 - y)]


def _remote(src, dst, send, recv, dev):
    return pltpu.make_async_remote_copy(src_ref=src, dst_ref=dst, send_sem=send, recv_sem=recv, device_id=dev, device_id_type=MESH)


def _gather_chips_plan(n):
    def start(srcs, dsts, send, recv, local):
        x, y, cc = _place()
        me = 4 * x + 2 * y + cc
        for a in range(n):
            pltpu.make_async_copy(srcs[a], dsts[a].at[me], local.at[a]).start()
            for k, (px, py) in enumerate(_other_chips(x, y)):
                _remote(srcs[a], dsts[a].at[me], send.at[3 * a + k], recv.at[3 * a + k], (px, py, cc)).start()

    def wait(srcs, dsts, send, recv, local):
        x, y, cc = _place()
        me = 4 * x + 2 * y + cc
        for a in range(n):
            for k, (px, py) in enumerate(_other_chips(x, y)):
                _remote(srcs[a], dsts[a].at[4 * px + 2 * py + cc], send.at[3 * a + k], recv.at[3 * a + k], (px, py, cc)).wait_recv()
        for a in range(n):
            for k, (px, py) in enumerate(_other_chips(x, y)):
                _remote(srcs[a], dsts[a].at[me], send.at[3 * a + k], recv.at[3 * a + k], (px, py, cc)).wait_send()
            pltpu.make_async_copy(srcs[a], dsts[a].at[me], local.at[a]).wait()

    return _Plan(start, wait, 3 * n, n)


def _scatter_chips_plan(n):
    def start(srcs, dsts, send, recv, local):
        x, y, cc = _place()
        for a in range(n):
            for k, (px, py) in enumerate(_other_chips(x, y)):
                _remote(srcs[a].at[2 * px + py], dsts[a].at[k], send.at[3 * a + k], recv.at[3 * a + k], (px, py, cc)).start()

    def wait(srcs, dsts, send, recv, local):
        x, y, cc = _place()
        for a in range(n):
            for k, (px, py) in enumerate(_other_chips(x, y)):
                _remote(srcs[a].at[k], dsts[a].at[k], send.at[3 * a + k], recv.at[3 * a + k], (px, py, cc)).wait_recv()
        for a in range(n):
            for k, (px, py) in enumerate(_other_chips(x, y)):
                _remote(srcs[a].at[k], dsts[a].at[k], send.at[3 * a + k], recv.at[3 * a + k], (px, py, cc)).wait_send()

    return _Plan(start, wait, 3 * n, 0)


def _gather_shapes(blocks):
    return [jax.ShapeDtypeStruct((N_DEV,) + b.shape, b.dtype) for b in blocks]


def _scatter_shapes(parts):
    return [jax.ShapeDtypeStruct((3,) + p.shape[1:], p.dtype) for p in parts]


def _gather_sibling_plan(n):
    def start(srcs, dsts, send, recv, local):
        x, y, cc = _place()
        for a in range(n):
            for q in range(4):
                _remote(srcs[a].at[2 * q + cc], dsts[a].at[2 * q + cc], send.at[4 * a + q], recv.at[4 * a + q], (x, y, 1 - cc)).start()

    def wait(srcs, dsts, send, recv, local):
        x, y, cc = _place()
        for a in range(n):
            for q in range(4):
                _remote(srcs[a].at[2 * q + cc], dsts[a].at[2 * q + 1 - cc], send.at[4 * a + q], recv.at[4 * a + q],
                        (x, y, 1 - cc)).wait_recv()
        for a in range(n):
            for q in range(4):
                _remote(srcs[a].at[2 * q + cc], dsts[a].at[2 * q + cc], send.at[4 * a + q], recv.at[4 * a + q],
                        (x, y, 1 - cc)).wait_send()

    return _Plan(start, wait, 4 * n, 0, in_place=True)


def _scatter_sibling_plan(n):
    def start(srcs, dsts, send, recv, local):
        x, y, cc = _place()
        for a in range(n):
            for q in range(4):
                _remote(srcs[a].at[2 * q + 1 - cc], dsts[a].at[q], send.at[4 * a + q], recv.at[4 * a + q], (x, y, 1 - cc)).start()

    def wait(srcs, dsts, send, recv, local):
        x, y, cc = _place()
        for a in range(n):
            for q in range(4):
                _remote(srcs[a].at[q], dsts[a].at[q], send.at[4 * a + q], recv.at[4 * a + q], (x, y, 1 - cc)).wait_recv()
        for a in range(n):
            for q in range(4):
                _remote(srcs[a].at[q], dsts[a].at[q], send.at[4 * a + q], recv.at[4 * a + q], (x, y, 1 - cc)).wait_send()

    return _Plan(start, wait, 4 * n, 0)


def _same_shapes(arrs):
    return [jax.ShapeDtypeStruct(a.shape, a.dtype) for a in arrs]


def _halved_shapes(parts):
    return [jax.ShapeDtypeStruct((4,) + p.shape[1:], p.dtype) for p in parts]


def _run_plan(plan, srcs, out_shapes, *, name):
    n_in, n_out = len(srcs), len(out_shapes)

    def body(*refs):
        h_in, h_out, sems = refs[:n_in], refs[n_in:n_in + n_out], refs[n_in + n_out:]
        plan.start(h_in, h_out, *sems)
        plan.wait(h_in, h_out, *sems)

    return pl.pallas_call(body, name=name, in_specs=[ANY] * n_in, out_specs=[ANY] * n_out, out_shape=list(out_shapes),
                          input_output_aliases={a: a for a in range(n_in)} if plan.in_place else {},
                          scratch_shapes=plan.sems())(*srcs)


def _sum_sibling(p, q, core, *, name):
    _, r, c = p.shape

    def body(core_ref, p_ref, q_ref, o_ref):
        o_ref[...] = (p_ref[...].astype(F32) + q_ref[...].astype(F32)).astype(BF16)

    grid_spec = pltpu.PrefetchScalarGridSpec(
        num_scalar_prefetch=1, grid=(4,),
        in_specs=[pl.BlockSpec((1, r, c), lambda ch, core_ref: (2 * ch + core_ref[0], 0, 0)),
                  pl.BlockSpec((1, r, c), lambda ch, core_ref: (ch, 0, 0))],
        out_specs=pl.BlockSpec((1, r, c), lambda ch, core_ref: (ch, 0, 0)))
    return pl.pallas_call(
        body, name=name, grid_spec=grid_spec, out_shape=jax.ShapeDtypeStruct((4, r, c), BF16),
        compiler_params=_params("parallel"),
    )(core, p, q)


def _sum_chips(s1, r2, chip, *, name):
    _, r, c = s1.shape

    def body(chip_ref, s_ref, r_ref, o_ref):
        acc = s_ref[0].astype(F32)
        for k in range(3):
            acc = acc + r_ref[k].astype(F32)
        o_ref[...] = acc

    grid_spec = pltpu.PrefetchScalarGridSpec(
        num_scalar_prefetch=1, grid=(1,),
        in_specs=[pl.BlockSpec((1, r, c), lambda i, chip_ref: (chip_ref[0], 0, 0)),
                  pl.BlockSpec((3, r, c), lambda i, chip_ref: (0, 0, 0))],
        out_specs=pl.BlockSpec((r, c), lambda i, chip_ref: (0, 0)))
    return pl.pallas_call(
        body, name=name, grid_spec=grid_spec, out_shape=jax.ShapeDtypeStruct((r, c), F32),
        compiler_params=_params("arbitrary"),
    )(chip, s1, r2)


def _sum_adamw(s1, r2, chip, w, m, v, *, name):
    _, r, c = s1.shape
    c1 = 1.0 / (1.0 - ADAM_B1 ** ADAM_STEP)
    c2 = 1.0 / (1.0 - ADAM_B2 ** ADAM_STEP)

    def body(chip_ref, s_ref, r_ref, w_ref, m_ref, v_ref, g_ref, d_ref, nm_ref, nv_ref):
        gv = s_ref[0].astype(F32)
        for k in range(3):
            gv = gv + r_ref[k].astype(F32)
        g_ref[...] = gv
        nm = ADAM_B1 * m_ref[...] + (1.0 - ADAM_B1) * gv
        nv = ADAM_B2 * v_ref[...] + (1.0 - ADAM_B2) * (gv * gv)
        nm_ref[...] = nm
        nv_ref[...] = nv
        d_ref[...] = -ADAM_LR * ((nm * c1) / (jnp.sqrt(nv * c2) + ADAM_EPS) + ADAM_WD * w_ref[...])

    flat = pl.BlockSpec((r, c), lambda i, chip_ref: (0, 0))
    grid_spec = pltpu.PrefetchScalarGridSpec(
        num_scalar_prefetch=1, grid=(1,),
        in_specs=[pl.BlockSpec((1, r, c), lambda i, chip_ref: (chip_ref[0], 0, 0)),
                  pl.BlockSpec((3, r, c), lambda i, chip_ref: (0, 0, 0)), flat, flat, flat],
        out_specs=[flat] * 4)
    return pl.pallas_call(
        body, name=name, grid_spec=grid_spec, out_shape=[jax.ShapeDtypeStruct((r, c), F32)] * 4,
        compiler_params=_params("arbitrary"),
    )(chip, s1, r2, w, m, v)


def _small_exchange(v, *, reduce, name):
    r, c = v.shape

    def body(x_ref, o_ref, *rest):
        if reduce:
            buf_ref, send_sems, recv_sems = rest
        else:
            buf_ref = o_ref
            send_sems, recv_sems = rest
        x, y, cc = _place()
        me = 4 * x + 2 * y + cc

        def peer(k):
            return ((1 - x) if k & 4 else x, (1 - y) if k & 2 else y, (1 - cc) if k & 1 else cc)

        buf_ref[me] = x_ref[...]
        sends = []
        for k in range(1, N_DEV):
            cp = pltpu.make_async_remote_copy(src_ref=x_ref, dst_ref=buf_ref.at[me], send_sem=send_sems.at[k - 1],
                                              recv_sem=recv_sems.at[k - 1], device_id=peer(k), device_id_type=MESH)
            cp.start()
            sends.append(cp)
        for k in range(1, N_DEV):
            px, py, pc = peer(k)
            pltpu.make_async_remote_copy(src_ref=x_ref, dst_ref=buf_ref.at[4 * px + 2 * py + pc], send_sem=send_sems.at[k - 1],
                                         recv_sem=recv_sems.at[k - 1], device_id=peer(k), device_id_type=MESH).wait_recv()
        for cp in sends:
            cp.wait_send()
        if reduce:
            acc = buf_ref[0]
            for s in range(1, N_DEV):
                acc = acc + buf_ref[s]
            o_ref[...] = acc

    vm = pl.BlockSpec(memory_space=pltpu.VMEM)
    sems = [pltpu.SemaphoreType.DMA((N_DEV - 1,)), pltpu.SemaphoreType.DMA((N_DEV - 1,))]
    if reduce:
        out_shape, scratch = jax.ShapeDtypeStruct((r, c), F32), [pltpu.VMEM((N_DEV, r, c), F32)] + sems
    else:
        out_shape, scratch = jax.ShapeDtypeStruct((N_DEV, r, c), F32), sems
    return pl.pallas_call(body, name=name, in_specs=[vm], out_specs=vm, out_shape=out_shape, scratch_shapes=scratch)(v)


def _rows(a):
    return a.reshape(-1, D)


def _pad_cols(a, to):
    return jnp.pad(a, ((0, 0), (0, to - a.shape[1])))


def _pack_weights(w):
    parts = {
        "w_inT": jnp.pad(w["w_in"].T, ((0, IN_SHARD_PAD - IN_SHARD), (0, 0))),
        "w_uq": _rows(_pad_cols(w["w_uq"], HEAD_PAD)), "w_uk": _rows(_pad_cols(w["w_uk"], HEAD_PAD)),
        "w_uv": _rows(_pad_cols(w["w_uv"], HEAD_PAD)), "w_pa": _rows(w["w_proj_attn"]),
        "w_pc": w["w_proj_conv"], "w_out": w["w_out"],
    }
    return [jnp.concatenate([parts[n].astype(BF16) for n, _ in group], axis=0) for group in PACK]


def _cols_from_shards(gs, name, rows):
    idx, off, r = PACK_OFF[name]
    return gs[idx][:, off:off + r].reshape(N_DEV, rows, HEAD_PAD).transpose(1, 0, 2).reshape(rows, N_DEV * HEAD_PAD)


def _rows_from_shards(gs, name, keep=None):
    idx, off, r = PACK_OFF[name]
    keep = r if keep is None else keep
    return gs[idx][:, off:off + keep].reshape(N_DEV * keep, D)


def _rope_placement():
    i = lax.broadcasted_iota(jnp.int32, (HEAD_PAD, D), 0)
    j = lax.broadcasted_iota(jnp.int32, (HEAD_PAD, D), 1)
    return ((i < 2 * ROPE_HALF) & (j % HEAD_PAD == NOPE + i)).astype(BF16)


def _unpack_weights(g):
    w_inT = _rows_from_shards(g, "w_inT", IN_SHARD)
    lat_rows = Q_LORA + KV_LORA + 2 * ROPE_HALF
    conv = w_inT[lat_rows:lat_rows + CONV_COLS].reshape(3, D // CONV_CB, CONV_CB, D).transpose(1, 0, 2, 3).reshape(CONV_COLS, D)
    wpa = _cols_from_shards(g, "w_pa", 512).reshape(N_HEADS, NOPE, D)
    return {
        "latT": jnp.pad(w_inT[:lat_rows], ((0, LAT_PAD - lat_rows), (0, 0))),
        "convT": conv, "gateT": w_inT[lat_rows + CONV_COLS:],
        "wq": _cols_from_shards(g, "w_uq", Q_LORA),
        "wk": jnp.concatenate([_cols_from_shards(g, "w_uk", KV_LORA), _rope_placement()], axis=0),
        "wv": _cols_from_shards(g, "w_uv", KV_LORA),
        "wpa": jnp.pad(wpa, ((0, 0), (0, HEAD_PAD - NOPE), (0, 0))).reshape(D, D),
        "wpc": _rows_from_shards(g, "w_pc"), "wout": _rows_from_shards(g, "w_out"),
    }


def _shards_from_cols(a):
    rows = a.shape[0]
    return a.reshape(rows, N_DEV, HEAD_PAD).transpose(1, 0, 2).reshape(N_DEV, rows * HEAD_PAD // D, D)


def _pack_grads(gw):
    lat_rows = Q_LORA + KV_LORA + 2 * ROPE_HALF
    conv = gw["convT"].reshape(D // CONV_CB, 3, CONV_CB, D).transpose(1, 0, 2, 3).reshape(CONV_COLS, D)
    w_inT = jnp.concatenate([gw["latT"][:lat_rows], conv, gw["gateT"]], axis=0).reshape(N_DEV, IN_SHARD, D)
    wpa = gw["wpa"].reshape(N_HEADS, HEAD_PAD, D)[:, :NOPE].reshape(N_HEADS * NOPE, D)
    parts = {}
    parts.update({
        "w_inT": jnp.pad(w_inT, ((0, 0), (0, IN_SHARD_PAD - IN_SHARD), (0, 0))),
        "w_uq": _shards_from_cols(gw["wq"]), "w_uk": _shards_from_cols(gw["wk"][:KV_LORA]),
        "w_uv": _shards_from_cols(gw["wv"][:KV_LORA]), "w_pa": _shards_from_cols(wpa),
        "w_pc": gw["wpc"].reshape(N_DEV, D // N_DEV, D), "w_out": gw["wout"].reshape(N_DEV, D // N_DEV, D),
    })
    return [jnp.concatenate([parts[n] for n, _ in group], axis=1) for group in PACK]


def _unpack_grads(mines):
    def seg(name, keep=None):
        idx, off, r = PACK_OFF[name]
        return mines[idx][off:off + (r if keep is None else keep)]

    return {
        "w_in": seg("w_inT", IN_SHARD).T,
        "w_uq": seg("w_uq").reshape(Q_LORA, HEAD_PAD)[:, :QK_DIM],
        "w_uk": seg("w_uk").reshape(KV_LORA, HEAD_PAD)[:, :NOPE],
        "w_uv": seg("w_uv").reshape(KV_LORA, HEAD_PAD)[:, :NOPE],
        "w_proj_attn": seg("w_pa").reshape(512, HEAD_PAD),
        "w_proj_conv": seg("w_pc"), "w_out": seg("w_out"),
    }


def _rope_tables(positions):
    inv_freq = 1.0 / (ROPE_THETA ** (jnp.arange(ROPE_HALF, dtype=F32) / ROPE_HALF))
    ang = positions.reshape(-1).astype(F32)[:, None] * inv_freq
    cos, sin = jnp.cos(ang), jnp.sin(ang)
    t = ang.shape[0]
    zero = jnp.zeros((t, ROPE_HALF), F32)
    head = jnp.ones((t, NOPE), F32)
    tail = jnp.zeros((t, HEAD_PAD - QK_DIM), F32)
    nohead = jnp.zeros((t, NOPE), F32)
    rc = jnp.concatenate([head, cos, cos, tail], axis=1)
    rs1 = jnp.concatenate([nohead, -sin, zero, tail], axis=1)
    rs2 = jnp.concatenate([nohead, zero, sin, tail], axis=1)
    return rc, rs1, rs2


def _local_step(x, positions, target, conv_w, small, ex):
    n_seq, seq, d = x.shape
    t = n_seq * seq
    x0 = x.reshape(t, d)
    tgt = target.reshape(t, d)
    rc, rs1, rs2 = _rope_tables(positions)
    ghq = _pad_cols(small["q_head_norm"], HEAD_PAD)
    ghk = _pad_cols(small["k_head_norm"], HEAD_PAD)
    TM, HC, TQ = 1024, 256, 512

    def mm(*args, hosted=None, **kw):
        res = _mm(*args, hosted=hosted, **kw)
        return res if hosted is not None else (res, None)

    def wgrad(a, b, name, tm=None, hosted=None):
        return mm(a, b, mode="tn", out_dtype=BF16, tm=tm or a.shape[1], tn=b.shape[1], tk=512, name=name, hosted=hosted)

    f1g, f1u, f1d = ex.gather_now("ffn1")
    (x1, h1, a1, b1), got = _ffn_fwd(x0, small["ffn1_norm"], f1g, f1u, f1d, tm=TM, hc=HC, name="ffn1_fwd",
                                     hosted=ex.gather_chips("mix_in", "mix_misc"))
    hm, got = _rms_fwd(x1, small["mix_norm"], tm=TM, name="mix_norm_fwd", hosted=ex.gather_sibling(got))
    W = ex.mix_weights(got)
    lat = _mm(hm, W["latT"], mode="nt", out_dtype=BF16, tm=TM, tn=LAT_PAD, tk=D, name="proj_lat")
    conv3 = _mm(hm, W["convT"], mode="nt", out_dtype=BF16, tm=TM, tn=CONV_COLS // 2, tk=D, name="proj_conv")
    gl = _mm(hm, W["gateT"], mode="nt", out_dtype=BF16, tm=TM, tn=GATE_COLS // 2, tk=D, name="proj_gate")
    q, k, v, qn, ckv = _mla_prep_fwd(lat, small["q_a_norm"], small["kv_a_norm"], ghq, ghk, W["wq"], W["wk"], W["wv"], rc, rs1, rs2,
                                     tm=512, name="mla_prep_fwd")
    (o, lse), got = _flash_fwd(q, k, v, n_seq=n_seq, seq=seq, tq=TQ, name="attn_fwd", hosted=ex.gather_chips("ffn2"))
    p = _conv_fwd(conv3, conv_w, n_seq=n_seq, seq=seq, name="conv_fwd")
    (x2, merged, ya, yb), got = _merge_fwd(o, p, gl, small["gate_bias"], x1, W["wpa"], W["wpc"], W["wout"], tm=512, name="merge_fwd",
                                           hosted=ex.gather_sibling(got))
    f2g, f2u, f2d = ex.ffn_weights(got)
    (y, h2, a2, b2), _ = _ffn_fwd(x2, small["ffn2_norm"], f2g, f2u, f2d, tm=TM, hc=HC, name="ffn2_fwd")
    dy, loss_row = _loss_head(y, tgt, tm=TM, name="loss_head")

    gw, gs = {}, {}
    (da2, db2, *ffn2_grads), _ = _ffn_grads(dy, h2, a2, b2, f2d, tm=TM, hc=HC, name="ffn2_grads")
    (dx2, gs["ffn2_norm"]), _ = _ffn_up_bwd(da2, db2, f2g, f2u, x2, small["ffn2_norm"], dy, tm=512, name="ffn2_up_bwd")

    (dx2b, dya, dyb, dgl, do, dp, gs["gate_bias"]), got = _merge_bwd(
        dx2, ya, yb, gl, small["gate_bias"], W["wpa"], W["wpc"], W["wout"], tm=512, name="merge_bwd",
        hosted=ex.scatter_sibling("ffn2", ffn2_grads))
    ex.scatter_sibling_done("ffn2", got)
    gw["wout"] = wgrad(merged, dx2b, "dw_out")[0]
    gw["wpa"] = wgrad(o, dya, "dw_pa")[0]
    gw["wpc"] = wgrad(p, dyb, "dw_pc")[0]
    dconv3, dconv_w = _conv_bwd(dp, conv3, conv_w, n_seq=n_seq, seq=seq, name="conv_bwd")
    (dq, dk, dv), got = _flash_bwd(q, k, v, o, lse, do, n_seq=n_seq, seq=seq, tq=TQ, name="attn_bwd",
                                   hosted=ex.scatter_chips("ffn2"))
    ex.scatter_chips_done("ffn2", got)
    dlat, dqp, dkp, gs["q_a_norm"], gs["kv_a_norm"], dghq, dghk = _mla_prep_bwd(
        dq, dk, dv, lat, qn, ckv, small["q_a_norm"], small["kv_a_norm"], ghq, ghk, W["wq"], W["wk"], W["wv"], rc, rs1, rs2,
        tm=512, name="mla_prep_bwd")
    gs["q_head_norm"], gs["k_head_norm"] = dghq[:, :QK_DIM], dghk[:, :QK_DIM]
    gw["wq"] = wgrad(qn, dqp, "dw_uq")[0]
    gw["wk"] = wgrad(ckv, dkp, "dw_uk")[0]
    gw["wv"] = wgrad(ckv, dv, "dw_uv")[0]
    gw["convT"] = wgrad(dconv3, hm, "dw_conv", tm=CONV_COLS // 2)[0]
    gw["gateT"] = wgrad(dgl, hm, "dw_gate")[0]
    gw["latT"] = wgrad(dlat, hm, "dw_lat")[0]
    ex.scatter_sibling_now("mix", gw)
    (dx1, gs["mix_norm"]), got = _proj_bwd(dlat, dconv3, dgl, W["latT"], W["convT"], W["gateT"], x1, small["mix_norm"], dx2,
                                           tm=512, name="proj_bwd", hosted=ex.scatter_chips("mix_in"))
    ex.scatter_chips_done("mix_in", got)

    (da1, db1, *ffn1_grads), got = _ffn_grads(dx1, h1, a1, b1, f1d, tm=TM, hc=HC, name="ffn1_grads",
                                              hosted=ex.scatter_chips("mix_misc"))
    ex.scatter_chips_done("mix_misc", got)
    ex.scatter_sibling_now("ffn1", ffn1_grads)
    (dx0, gs["ffn1_norm"]), got = _ffn_up_bwd(da1, db1, f1g, f1u, x0, small["ffn1_norm"], dx1, tm=512, name="ffn1_up_bwd",
                                              hosted=ex.scatter_chips("ffn1"))
    ex.scatter_chips_done("ffn1", got)
    return loss_row, dx0.reshape(n_seq, seq, d), dconv_w, gs


class _MeshExchange:
    def __init__(self, w, core, chip):
        self.w, self.core, self.chip = w, core, chip
        self.partial, self.received, self._packed = {}, {}, None

    def _blocks(self, group):
        w = self.w
        if group.startswith("ffn"):
            return [w[group + "_w_gate"].T.astype(BF16), w[group + "_w_up"].T.astype(BF16), w[group + "_w_down"].astype(BF16)]
        if self._packed is None:
            self._packed = _pack_weights(w)
        return [self._packed[0 if group == "mix_in" else 1]]

    def gather_chips(self, *groups):
        blocks = [b for group in groups for b in self._blocks(group)]
        return _gather_chips_plan(len(blocks)), blocks, _gather_shapes(blocks)

    def gather_sibling(self, got):
        half = list(got)
        return _gather_sibling_plan(len(half)), half, _same_shapes(half)

    def gather_now(self, group):
        plan, blocks, shapes = self.gather_chips(group)
        half = list(_run_plan(plan, blocks, shapes, name="gather_%s_chips" % group))
        return self.ffn_weights(_run_plan(_gather_sibling_plan(len(half)), half, _same_shapes(half), name="gather_%s_sibling" % group))

    def ffn_weights(self, got):
        return [a.reshape(DFF, D) for a in got]

    def mix_weights(self, got):
        return _unpack_weights(list(got))

    def _parts(self, group, grads):
        if group == "mix":
            return _pack_grads(grads), ["mix_in", "mix_misc"]
        parts = [g.reshape(N_DEV, -1, D) for g in grads]
        return parts, ([group] if len(parts) == 1 else None)

    def scatter_sibling(self, group, grads):
        self._sent, self._names = self._parts(group, grads)
        return _scatter_sibling_plan(len(self._sent)), self._sent, _halved_shapes(self._sent)

    def scatter_sibling_done(self, group, got):
        sums = [_sum_sibling(p, q, self.core, name="sum_%s_sibling_%d" % (group, i)) for i, (p, q) in enumerate(zip(self._sent, got))]
        if self._names is None:
            self.partial[group] = sums
        else:
            for n, s in zip(self._names, sums):
                self.partial[n] = [s]

    def scatter_sibling_now(self, group, grads):
        plan, parts, shapes = self.scatter_sibling(group, grads)
        self.scatter_sibling_done(group, _run_plan(plan, parts, shapes, name="scatter_%s_sibling" % group))

    def scatter_chips(self, group):
        s1 = self.partial[group]
        return _scatter_chips_plan(len(s1)), s1, _scatter_shapes(s1)

    def scatter_chips_done(self, group, got):
        self.received[group] = list(got)


SMALL_NAMES = ("ffn1_norm", "mix_norm", "gate_bias", "q_a_norm", "kv_a_norm", "q_head_norm", "k_head_norm", "ffn2_norm")
SMALL_SLOTS = {"ffn1_norm": 1024, "mix_norm": 1024, "gate_bias": 2048, "q_a_norm": 384, "kv_a_norm": 256, "q_head_norm": 128,
               "k_head_norm": 128, "ffn2_norm": 1024, "conv_w": 3072, "loss": 128}
COLUMN_MAJOR = ("w_in", "w_uq", "w_uk", "w_uv")
WEIGHT_NAMES = ("ffn1_norm", "ffn1_w_gate", "ffn1_w_up", "ffn1_w_down", "mix_norm", "w_in", "gate_bias", "q_a_norm", "w_uq",
                "kv_a_norm", "w_uk", "w_uv", "q_head_norm", "k_head_norm", "w_proj_attn", "conv_w", "w_proj_conv", "w_out",
                "ffn2_norm", "ffn2_w_gate", "ffn2_w_up", "ffn2_w_down")


def _step(x, positions, loss_target, w, m, v):
    xi, yi, ci = _place()
    core = ci.astype(jnp.int32).reshape(1)
    chip = (2 * xi + yi).astype(jnp.int32).reshape(1)
    me = 4 * xi + 2 * yi + ci

    cw_all = _small_exchange(jnp.pad(w["conv_w"], ((0, 5), (0, 0))), reduce=False, name="gather_conv_w")
    conv_w = cw_all[:, :3].transpose(1, 0, 2).reshape(3, D)
    small = {n: w[n].reshape(1, -1) for n in SMALL_NAMES}
    ex = _MeshExchange(w, core, chip)

    loss_row, grad_x, dconv_w, gs = _local_step(x, positions, loss_target, conv_w, small, ex)

    grads, deltas, new_m, new_v = {}, {}, {}, {}
    where = {"ffn1_w_gate": ("ffn1", 0), "ffn1_w_up": ("ffn1", 1), "ffn1_w_down": ("ffn1", 2),
             "ffn2_w_gate": ("ffn2", 0), "ffn2_w_up": ("ffn2", 1), "ffn2_w_down": ("ffn2", 2)}
    for n, (group, i) in where.items():
        transposed = not n.endswith("down")
        wv, mv, vv = (a[n].T if transposed else a[n] for a in (w, m, v))
        res = _sum_adamw(ex.partial[group][i], ex.received[group][i], chip, wv, mv, vv, name="adamw_" + n)
        grads[n], deltas[n], new_m[n], new_v[n] = (r.T if transposed else r for r in res)
    grads.update(_unpack_grads([_sum_chips(ex.partial[g][0], ex.received[g][0], chip, name="sum_%s_chips" % g)
                                for g in ("mix_in", "mix_misc")]))

    pieces = [_pad_cols(gs[n], SMALL_SLOTS[n]) for n in SMALL_NAMES] + [dconv_w.reshape(1, 3 * D), loss_row]
    total = _small_exchange(jnp.concatenate(pieces, axis=1).reshape(-1, 128), reduce=True, name="reduce_small").reshape(-1)
    off = 0
    for n in SMALL_NAMES:
        grads[n] = total[off:off + w[n].shape[0]]
        off += SMALL_SLOTS[n]
    conv_full = total[off:off + 3 * D].reshape(3, D)
    grads["conv_w"] = lax.dynamic_slice(conv_full, (0, me * HEAD_PAD), (3, HEAD_PAD))
    loss = total[off + 3 * D]

    for n in WEIGHT_NAMES:
        if n in deltas:
            continue
        shape = w[n].shape
        if n in COLUMN_MAJOR:
            ops = [a.T for a in (w[n], grads[n], m[n], v[n])]
            deltas[n], new_m[n], new_v[n] = (r.T for r in _adamw(*ops, name="adamw_" + n))
            continue
        if len(shape) == 1:
            view = (-1, 128) if shape[0] % 128 == 0 else (1, shape[0])
        else:
            view = shape
        dlt, nm, nv = _adamw(w[n].reshape(view), grads[n].reshape(view), m[n].reshape(view), v[n].reshape(view), name="adamw_" + n)
        deltas[n], new_m[n], new_v[n] = dlt.reshape(shape), nm.reshape(shape), nv.reshape(shape)
    return (loss, grad_x, *[grads[n] for n in WEIGHT_NAMES], *[deltas[n] for n in WEIGHT_NAMES],
            *[new_m[n] for n in WEIGHT_NAMES], *[new_v[n] for n in WEIGHT_NAMES])


def kernel(x, positions, ffn1_norm, ffn1_w_gate, ffn1_w_up, ffn1_w_down, mix_norm, w_in, gate_bias, q_a_norm, w_uq, kv_a_norm, w_uk, w_uv, q_head_norm, k_head_norm, w_proj_attn, conv_w, w_proj_conv, w_out, ffn2_norm, ffn2_w_gate, ffn2_w_up, ffn2_w_down, loss_target, m_ffn1_norm, m_ffn1_w_gate, m_ffn1_w_up, m_ffn1_w_down, m_mix_norm, m_w_in, m_gate_bias, m_q_a_norm, m_w_uq, m_kv_a_norm, m_w_uk, m_w_uv, m_q_head_norm, m_k_head_norm, m_w_proj_attn, m_conv_w, m_w_proj_conv, m_w_out, m_ffn2_norm, m_ffn2_w_gate, m_ffn2_w_up, m_ffn2_w_down, v_ffn1_norm, v_ffn1_w_gate, v_ffn1_w_up, v_ffn1_w_down, v_mix_norm, v_w_in, v_gate_bias, v_q_a_norm, v_w_uq, v_kv_a_norm, v_w_uk, v_w_uv, v_q_head_norm, v_k_head_norm, v_w_proj_attn, v_conv_w, v_w_proj_conv, v_w_out, v_ffn2_norm, v_ffn2_w_gate, v_ffn2_w_up, v_ffn2_w_down):
    given = dict(locals())
    w = {n: given[n] for n in WEIGHT_NAMES}
    m = {n: given["m_" + n] for n in WEIGHT_NAMES}
    v = {n: given["v_" + n] for n in WEIGHT_NAMES}
    return _step(x, positions, loss_target, w, m, v)
```

```python
import functools

import jax
import jax.numpy as jnp
from jax import lax
from jax.experimental import pallas as pl
from jax.experimental.pallas import tpu as pltpu

F32 = jnp.float32
BF16 = jnp.bfloat16
MESH = pl.DeviceIdType.MESH
ANY = pl.BlockSpec(memory_space=pl.ANY)

N_DEV = 8
D = 1024
DFF = 2816
N_HEADS = 8
HEAD_PAD = 128
QK_DIM = 96
NOPE = 64
ROPE_HALF = 16
Q_LORA = 384
KV_LORA = 256
LAT_PAD = 768
CONV_COLS = 3072
GATE_COLS = 2048
IN_DIM = 5792
IN_SHARD = IN_DIM // N_DEV
IN_SHARD_PAD = 736
FF_SHARD = DFF // N_DEV
ROPE_THETA = 10000.0
NORM_EPS = 1e-6
ATTN_SCALE = QK_DIM ** -0.5
NEG = -1e30

ADAM_LR, ADAM_B1, ADAM_B2, ADAM_EPS, ADAM_WD, ADAM_STEP = 0.001, 0.9, 0.999, 1e-08, 0.01, 10

PACK = ((("w_inT", IN_SHARD_PAD),), (("w_uq", 48), ("w_uk", 32), ("w_uv", 32), ("w_pa", 64), ("w_pc", 128), ("w_out", 128)))
PACK_OFF = {}
for _i, _group in enumerate(PACK):
    _o = 0
    for _n, _r in _group:
        PACK_OFF[_n] = (_i, _o, _r)
        _o += _r

VMEM_LIMIT = 56 * 1024 * 1024


def _params(*sem):
    return pltpu.CompilerParams(dimension_semantics=sem if sem else None, vmem_limit_bytes=VMEM_LIMIT)


class _Plan:
    def __init__(self, start, wait, n_remote, n_local, in_place=False):
        self.start, self.wait, self.n_remote, self.n_local, self.in_place = start, wait, n_remote, n_local, in_place

    def sems(self):
        return [pltpu.SemaphoreType.DMA((self.n_remote,)), pltpu.SemaphoreType.DMA((self.n_remote,)),
                pltpu.SemaphoreType.DMA((max(self.n_local, 1),))]


def _call(body, *, name, grid, in_specs, out_specs, out_shape, scratch_shapes, operands, sem, hosted=None):
    if hosted is None:
        outs = pl.pallas_call(body, name=name, grid=grid, in_specs=in_specs, out_specs=out_specs, out_shape=out_shape,
                              scratch_shapes=scratch_shapes, compiler_params=_params(*sem))(*operands)
        return outs, None
    plan, srcs, h_shapes = hosted
    n_in, n_out, n_scr, nh_in, nh_out = len(in_specs), len(out_specs), len(scratch_shapes), len(srcs), len(h_shapes)
    aliases = {n_in + a: n_out + a for a in range(nh_in)} if plan.in_place else {}

    def full_body(*refs):
        ins, refs = refs[:n_in], refs[n_in:]
        h_in, refs = refs[:nh_in], refs[nh_in:]
        outs, refs = refs[:n_out], refs[n_out:]
        h_out, refs = refs[:nh_out], refs[nh_out:]
        scr, sems = refs[:n_scr], refs[n_scr:]
        ids = [pl.program_id(ax) for ax in range(len(grid))]
        first = functools.reduce(jnp.logical_and, [i == 0 for i in ids])
        last = functools.reduce(jnp.logical_and, [i == g - 1 for i, g in zip(ids, grid)])

        @pl.when(first)
        def _():
            plan.start(h_in, h_out, *sems)

        body(*ins, *outs, *scr)

        @pl.when(last)
        def _():
            plan.wait(h_in, h_out, *sems)

    res = pl.pallas_call(
        full_body, name=name, grid=grid, in_specs=list(in_specs) + [ANY] * nh_in, out_specs=list(out_specs) + [ANY] * nh_out,
        out_shape=list(out_shape) + list(h_shapes), scratch_shapes=list(scratch_shapes) + plan.sems(),
        input_output_aliases=aliases, compiler_params=_params(*(["arbitrary"] * len(grid))),
    )(*operands, *srcs)
    return res[:n_out], res[n_out:]


def _dot_nn(a, b):
    return lax.dot_general(a, b, (((1,), (0,)), ((), ())), preferred_element_type=F32)


def _dot_nt(a, b):
    return lax.dot_general(a, b, (((1,), (1,)), ((), ())), preferred_element_type=F32)


def _dot_tn(a, b):
    return lax.dot_general(a, b, (((0,), (0,)), ((), ())), preferred_element_type=F32)


def _sigmoid(x):
    return 0.5 * jnp.tanh(0.5 * x) + 0.5


def _rms_stats(x):
    r = lax.rsqrt(jnp.mean(x * x, axis=-1, keepdims=True) + NORM_EPS)
    return x * r, r


ROWS_WIDE = 16
MM_ROWS = 256


def _rms_bwd(dy, xhat, r, g):
    dg = jnp.sum(dy * xhat, axis=0, keepdims=True)
    dxh = dy * g
    dx = r * (dxh - xhat * jnp.mean(dxh * xhat, axis=-1, keepdims=True))
    return dx, dg


def _mm(a, b, *, mode, out_dtype, tm, tn, tk, name, add=None, scale=1.0, hosted=None):
    if mode == "nn":
        (m, k), (_, n) = a.shape, b.shape
    elif mode == "nt":
        (m, k), (n, _) = a.shape, b.shape
    else:
        (k, m), (_, n) = a.shape, b.shape
    assert m % tm == 0 and n % tn == 0 and k % tk == 0, (name, m, n, k, tm, tn, tk)
    nk = k // tk
    dot = {"nn": _dot_nn, "nt": _dot_nt, "tn": _dot_tn}[mode]
    a_spec = pl.BlockSpec((tk, tm), lambda i, j, kk: (kk, i)) if mode == "tn" else pl.BlockSpec((tm, tk), lambda i, j, kk: (i, kk))
    b_spec = pl.BlockSpec((tn, tk), lambda i, j, kk: (j, kk)) if mode == "nt" else pl.BlockSpec((tk, tn), lambda i, j, kk: (kk, j))
    o_spec = pl.BlockSpec((tm, tn), lambda i, j, kk: (i, j))
    has_add = add is not None

    def finish(prod, c_ref, o_ref):
        if scale != 1.0:
            prod = prod * scale
        o_ref[...] = ((c_ref[...] + prod) if has_add else prod).astype(out_dtype)

    def body(*refs):
        a_ref, b_ref = refs[:2]
        c_ref = refs[2] if has_add else None
        o_ref = refs[3] if has_add else refs[2]
        if nk == 1:
            finish(dot(a_ref[...], b_ref[...]), c_ref, o_ref)
            return
        acc_ref = refs[-1]
        kk = pl.program_id(2)

        @pl.when(kk == 0)
        def _():
            acc_ref[...] = jnp.zeros_like(acc_ref)

        acc_ref[...] += dot(a_ref[...], b_ref[...])

        @pl.when(kk == nk - 1)
        def _():
            finish(acc_ref[...], c_ref, o_ref)

    operands = (a, b, add) if has_add else (a, b)
    in_specs = [a_spec, b_spec] + ([o_spec] if has_add else [])
    (out,), got = _call(
        body, name=name, grid=(m // tm, n // tn, nk), in_specs=in_specs, out_specs=[o_spec],
        out_shape=[jax.ShapeDtypeStruct((m, n), out_dtype)], scratch_shapes=[pltpu.VMEM((tm, tn), F32)] if nk > 1 else [],
        operands=operands, sem=("parallel", "parallel", "arbitrary"), hosted=hosted)
    return out if hosted is None else (out, got)


def _rms_fwd(x, g, *, tm, name, hosted=None):
    t, d = x.shape

    def body(x_ref, g_ref, h_ref):
        xhat, _ = _rms_stats(x_ref[...])
        h_ref[...] = (xhat * g_ref[...]).astype(BF16)

    (h,), got = _call(
        body, name=name, grid=(t // tm,),
        in_specs=[pl.BlockSpec((tm, d), lambda i: (i, 0)), pl.BlockSpec((1, d), lambda i: (0, 0))],
        out_specs=[pl.BlockSpec((tm, d), lambda i: (i, 0))], out_shape=[jax.ShapeDtypeStruct((t, d), BF16)], scratch_shapes=[],
        operands=(x, g), sem=("parallel",), hosted=hosted)
    return h, got


def _ffn_fwd(x, g, wgT, wuT, wd, *, tm, hc, name, hosted=None, target=None):
    t, d = x.shape
    nj = DFF // hc
    with_loss = target is not None

    def body(*refs):
        x_ref, g_ref, wg_ref, wu_ref, wd_ref = refs[:5]
        t_ref = refs[5] if with_loss else None
        xo_ref, h_ref, a_ref, b_ref = refs[5 + with_loss:9 + with_loss]
        loss_ref = refs[9 + with_loss] if with_loss else None
        acc_ref = refs[-1]
        i, j = pl.program_id(0), pl.program_id(1)

        @pl.when(j == 0)
        def _():
            xhat, _ = _rms_stats(x_ref[...])
            h_ref[...] = (xhat * g_ref[...]).astype(BF16)
            acc_ref[...] = jnp.zeros_like(acc_ref)

        h = h_ref[...]
        a = _dot_nt(h, wg_ref[...])
        b = _dot_nt(h, wu_ref[...])
        a_ref[...] = a.astype(BF16)
        b_ref[...] = b.astype(BF16)
        s = (a * _sigmoid(a) * b).astype(BF16)
        acc_ref[...] += _dot_nn(s, wd_ref[...])

        if with_loss:
            @pl.when((i == 0) & (j == 0))
            def _():
                loss_ref[...] = jnp.zeros_like(loss_ref)

        @pl.when(j == nj - 1)
        def _():
            y = x_ref[...] + 0.5 * acc_ref[...]
            if with_loss:
                err = y - t_ref[...]
                xo_ref[...] = err * (1.0 / d)
                loss_ref[...] += jnp.sum(jnp.sum(err * err, axis=-1, keepdims=True), axis=0, keepdims=True) * (0.5 / d)
            else:
                xo_ref[...] = y

    row = pl.BlockSpec((tm, d), lambda i, j: (i, 0))
    vec = pl.BlockSpec((1, d), lambda i, j: (0, 0))
    wsp = pl.BlockSpec((hc, d), lambda i, j: (j, 0))
    hid = pl.BlockSpec((tm, hc), lambda i, j: (i, j))
    out_specs = [row, row, hid, hid] + ([pl.BlockSpec((1, 128), lambda i, j: (0, 0))] if with_loss else [])
    out_shape = [jax.ShapeDtypeStruct((t, d), F32), jax.ShapeDtypeStruct((t, d), BF16), jax.ShapeDtypeStruct((t, DFF), BF16),
                 jax.ShapeDtypeStruct((t, DFF), BF16)] + ([jax.ShapeDtypeStruct((1, 128), F32)] if with_loss else [])
    return _call(
        body, name=name, grid=(t // tm, nj), in_specs=[row, vec, wsp, wsp, wsp] + ([row] if with_loss else []),
        out_specs=out_specs, out_shape=out_shape, scratch_shapes=[pltpu.VMEM((tm, d), F32)],
        operands=(x, g, wgT, wuT, wd) + ((target,) if with_loss else ()),
        sem=("arbitrary" if with_loss else "parallel", "arbitrary"), hosted=hosted)


def _ffn_grads(dout, h, a, b, wd, *, tm, hc, name, hosted=None):
    t, d = dout.shape
    ni, nj = t // tm, DFF // hc

    def body(dout_ref, h_ref, a_ref, b_ref, wd_ref, da_ref, db_ref, dwg_ref, dwu_ref, dwd_ref,
             dy_all, h_all, ds_scr, s_scr, acc_g, acc_u, acc_d):
        j, i = pl.program_id(0), pl.program_id(1)
        rows_i = pl.ds(pl.multiple_of(i * tm, tm), tm)

        @pl.when(j == 0)
        def _():
            dy_all[rows_i, :] = (0.5 * dout_ref[...]).astype(BF16)
            h_all[rows_i, :] = h_ref[...]

        @pl.when(i == 0)
        def _():
            acc_g[...] = jnp.zeros_like(acc_g)
            acc_u[...] = jnp.zeros_like(acc_u)
            acc_d[...] = jnp.zeros_like(acc_d)

        def grad_rows(rows):
            ds = ds_scr[rows, :]
            av = a_ref[rows, :].astype(F32)
            bv = b_ref[rows, :].astype(F32)
            sg = _sigmoid(av)
            sl = av * sg
            s_scr[rows, :] = (sl * bv).astype(BF16)
            da_ref[rows, :] = (ds * bv * (sg + sl * (1.0 - sg))).astype(BF16)
            db_ref[rows, :] = (ds * sl).astype(BF16)

        for blk in range(tm // MM_ROWS):
            rs = slice(blk * MM_ROWS, (blk + 1) * MM_ROWS)
            ds_scr[rs, :] = _dot_nt(dy_all[pl.ds(pl.multiple_of(i * tm + blk * MM_ROWS, MM_ROWS), MM_ROWS), :], wd_ref[...])
            for c in range(MM_ROWS // ROWS_WIDE):
                grad_rows(slice(blk * MM_ROWS + c * ROWS_WIDE, blk * MM_ROWS + (c + 1) * ROWS_WIDE))

        dy_i = dy_all[rows_i, :]
        h_i = h_all[rows_i, :]
        acc_d[...] += _dot_tn(s_scr[...], dy_i)
        acc_g[...] += _dot_tn(da_ref[...], h_i)
        acc_u[...] += _dot_tn(db_ref[...], h_i)

        @pl.when(i == ni - 1)
        def _():
            dwg_ref[...] = acc_g[...].astype(BF16)
            dwu_ref[...] = acc_u[...].astype(BF16)
            dwd_ref[...] = acc_d[...].astype(BF16)

    first = pl.BlockSpec((tm, d), lambda j, i: (jnp.where(j == 0, i, 0), 0))
    hid = pl.BlockSpec((tm, hc), lambda j, i: (i, j))
    wsp = pl.BlockSpec((hc, d), lambda j, i: (j, 0))
    hid_shape = jax.ShapeDtypeStruct((t, DFF), BF16)
    w_shape = jax.ShapeDtypeStruct((DFF, d), BF16)
    return _call(
        body, name=name, grid=(nj, ni), in_specs=[first, first, hid, hid, wsp], out_specs=[hid, hid, wsp, wsp, wsp],
        out_shape=[hid_shape, hid_shape, w_shape, w_shape, w_shape],
        scratch_shapes=[pltpu.VMEM((t, d), BF16), pltpu.VMEM((t, d), BF16), pltpu.VMEM((tm, hc), F32), pltpu.VMEM((tm, hc), BF16),
                        pltpu.VMEM((hc, d), F32), pltpu.VMEM((hc, d), F32), pltpu.VMEM((hc, d), F32)],
        operands=(dout, h, a, b, wd), sem=("arbitrary", "arbitrary"), hosted=hosted)


def _proj_bwd(dlat, dconv3, dgl, latT, convT, gateT, x, g, dres, *, tm, name, hosted=None):
    t, d = x.shape

    def body(dl_ref, dc_ref, dg_ref, wl_ref, wc_ref, wg_ref, x_ref, g_ref, dres_ref, dx_ref, dgain_ref):
        @pl.when(pl.program_id(0) == 0)
        def _():
            dgain_ref[...] = jnp.zeros_like(dgain_ref)

        dh = _dot_nn(dl_ref[...], wl_ref[...]) + _dot_nn(dc_ref[...], wc_ref[...]) + _dot_nn(dg_ref[...], wg_ref[...])
        xhat, r = _rms_stats(x_ref[...])
        dx, dgain = _rms_bwd(dh, xhat, r, g_ref[...])
        dx_ref[...] = dres_ref[...] + dx
        dgain_ref[...] += dgain

    def rows(w):
        return pl.BlockSpec((tm, w), lambda i: (i, 0))

    def full(r):
        return pl.BlockSpec((r, d), lambda i: (0, 0))

    return _call(
        body, name=name, grid=(t // tm,),
        in_specs=[rows(LAT_PAD), rows(CONV_COLS), rows(GATE_COLS), full(LAT_PAD), full(CONV_COLS), full(GATE_COLS), rows(d), full(1), rows(d)],
        out_specs=[rows(d), full(1)], out_shape=[jax.ShapeDtypeStruct((t, d), F32), jax.ShapeDtypeStruct((1, d), F32)],
        scratch_shapes=[], operands=(dlat, dconv3, dgl, latT, convT, gateT, x, g, dres), sem=("arbitrary",), hosted=hosted)


def _ffn_up_bwd(da, db, wgT, wuT, x, g, dout, *, tm, name, hosted=None):
    t, d = x.shape

    def body(da_ref, db_ref, wg_ref, wu_ref, x_ref, g_ref, dout_ref, dx_ref, dg_ref):
        @pl.when(pl.program_id(0) == 0)
        def _():
            dg_ref[...] = jnp.zeros_like(dg_ref)

        dh = _dot_nn(da_ref[...], wg_ref[...]) + _dot_nn(db_ref[...], wu_ref[...])
        xhat, r = _rms_stats(x_ref[...])
        dx, dg = _rms_bwd(dh, xhat, r, g_ref[...])
        dx_ref[...] = dout_ref[...] + dx
        dg_ref[...] += dg

    row = pl.BlockSpec((tm, d), lambda i: (i, 0))
    vec = pl.BlockSpec((1, d), lambda i: (0, 0))
    hid = pl.BlockSpec((tm, DFF), lambda i: (i, 0))
    wsp = pl.BlockSpec((DFF, d), lambda i: (0, 0))
    return _call(
        body, name=name, grid=(t // tm,), in_specs=[hid, hid, wsp, wsp, row, vec, row], out_specs=[row, vec],
        out_shape=[jax.ShapeDtypeStruct((t, d), F32), jax.ShapeDtypeStruct((1, d), F32)], scratch_shapes=[],
        operands=(da, db, wgT, wuT, x, g, dout), sem=("arbitrary",), hosted=hosted)


def _rope_fwd(x, c, s1, s2):
    return x * c + pltpu.roll(x, HEAD_PAD - ROPE_HALF, 1) * s1 + pltpu.roll(x, ROPE_HALF, 1) * s2


def _rope_bwd(dy, c, s1, s2):
    return dy * c + pltpu.roll(dy * s1, ROPE_HALF, 1) + pltpu.roll(dy * s2, HEAD_PAD - ROPE_HALF, 1)


def _head_stats(x):
    r = lax.rsqrt(jnp.sum(x * x, axis=-1, keepdims=True) * (1.0 / QK_DIM) + NORM_EPS)
    return x * r, r


def _mla_prep_fwd(lat, gq, gkv, ghq, ghk, wq, wk, wv, rc, rs1, rs2, *, tm, name):
    t = lat.shape[0]

    def body(lat_ref, gq_ref, gkv_ref, ghq_ref, ghk_ref, wq_ref, wk_ref, wv_ref, c_ref, s1_ref, s2_ref,
             q_ref, k_ref, v_ref, qn_ref, ckv_ref):
        lat_v = lat_ref[...]
        qhat, _ = _rms_stats(lat_v[:, :Q_LORA].astype(F32))
        qn = (qhat * gq_ref[...]).astype(BF16)
        khat, _ = _rms_stats(lat_v[:, Q_LORA:Q_LORA + KV_LORA].astype(F32))
        ckv = (khat * gkv_ref[...]).astype(BF16)
        ckv_ext = jnp.concatenate([ckv, lat_v[:, Q_LORA + KV_LORA:]], axis=1)
        qn_ref[...] = qn
        ckv_ref[...] = ckv_ext
        q_pre = _dot_nn(qn, wq_ref[...])
        k_pre = _dot_nn(ckv_ext, wk_ref[...])
        v_ref[...] = _dot_nn(ckv, wv_ref[...]).astype(BF16)
        c, s1, s2 = c_ref[...], s1_ref[...], s2_ref[...]
        for h in range(N_HEADS):
            hs = slice(h * HEAD_PAD, (h + 1) * HEAD_PAD)
            xq, _ = _head_stats(q_pre[:, hs])
            q_ref[:, hs] = _rope_fwd(xq * ghq_ref[...], c, s1, s2).astype(BF16)
            xk, _ = _head_stats(k_pre[:, hs])
            k_ref[:, hs] = _rope_fwd(xk * ghk_ref[...], c, s1, s2).astype(BF16)

    def row(w):
        return pl.BlockSpec((tm, w), lambda i: (i, 0))

    def full(r, w):
        return pl.BlockSpec((r, w), lambda i: (0, 0))

    wide = jax.ShapeDtypeStruct((t, D), BF16)
    lat3 = jax.ShapeDtypeStruct((t, Q_LORA), BF16)
    return pl.pallas_call(
        body, name=name, grid=(t // tm,),
        in_specs=[row(LAT_PAD), full(1, Q_LORA), full(1, KV_LORA), full(1, HEAD_PAD), full(1, HEAD_PAD),
                  full(Q_LORA, D), full(Q_LORA, D), full(KV_LORA, D), row(HEAD_PAD), row(HEAD_PAD), row(HEAD_PAD)],
        out_specs=[row(D), row(D), row(D), row(Q_LORA), row(Q_LORA)],
        out_shape=[wide, wide, wide, lat3, lat3],
        compiler_params=_params("parallel"),
    )(lat, gq, gkv, ghq, ghk, wq, wk, wv, rc, rs1, rs2)


def _mla_prep_bwd(dq, dk, dv, lat, qn, ckv_ext, gq, gkv, ghq, ghk, wq, wk, wv, rc, rs1, rs2, *, tm, name):
    t = lat.shape[0]

    def body(dq_ref, dk_ref, dv_ref, lat_ref, qn_ref, ckv_ref, gq_ref, gkv_ref, ghq_ref, ghk_ref, wq_ref, wk_ref, wv_ref,
             c_ref, s1_ref, s2_ref, dlat_ref, dqp_ref, dkp_ref, dgq_ref, dgkv_ref, dghq_ref, dghk_ref):
        @pl.when(pl.program_id(0) == 0)
        def _():
            dgq_ref[...] = jnp.zeros_like(dgq_ref)
            dgkv_ref[...] = jnp.zeros_like(dgkv_ref)
            dghq_ref[...] = jnp.zeros_like(dghq_ref)
            dghk_ref[...] = jnp.zeros_like(dghk_ref)

        c, s1, s2 = c_ref[...], s1_ref[...], s2_ref[...]
        q_pre = _dot_nn(qn_ref[...], wq_ref[...])
        k_pre = _dot_nn(ckv_ref[...], wk_ref[...])

        def heads(pre, dy_ref, gh_ref, dgh_ref, out_ref):
            dgh = jnp.zeros((1, HEAD_PAD), F32)
            for h in range(N_HEADS):
                hs = slice(h * HEAD_PAD, (h + 1) * HEAD_PAD)
                d = _rope_bwd(dy_ref[:, hs].astype(F32), c, s1, s2)
                xhat, r = _head_stats(pre[:, hs])
                dgh = dgh + jnp.sum(d * xhat, axis=0, keepdims=True)
                dxh = d * gh_ref[...]
                dx = r * (dxh - xhat * (jnp.sum(dxh * xhat, axis=-1, keepdims=True) * (1.0 / QK_DIM)))
                out_ref[:, hs] = dx.astype(BF16)
            dgh_ref[...] += dgh

        heads(q_pre, dq_ref, ghq_ref, dghq_ref, dqp_ref)
        heads(k_pre, dk_ref, ghk_ref, dghk_ref, dkp_ref)
        dqn = _dot_nt(dqp_ref[...], wq_ref[...])
        dce = _dot_nt(dkp_ref[...], wk_ref[...])
        dckv = dce[:, :KV_LORA] + _dot_nt(dv_ref[...], wv_ref[...])
        lat_v = lat_ref[...]
        qhat, rq = _rms_stats(lat_v[:, :Q_LORA].astype(F32))
        dql, dgq = _rms_bwd(dqn, qhat, rq, gq_ref[...])
        khat, rk = _rms_stats(lat_v[:, Q_LORA:Q_LORA + KV_LORA].astype(F32))
        dkl, dgkv = _rms_bwd(dckv, khat, rk, gkv_ref[...])
        dgq_ref[...] += dgq
        dgkv_ref[...] += dgkv
        dlat_ref[...] = jnp.concatenate([dql, dkl, dce[:, KV_LORA:]], axis=1).astype(BF16)

    def row(w):
        return pl.BlockSpec((tm, w), lambda i: (i, 0))

    def full(r, w):
        return pl.BlockSpec((r, w), lambda i: (0, 0))

    return pl.pallas_call(
        body, name=name, grid=(t // tm,),
        in_specs=[row(D), row(D), row(D), row(LAT_PAD), row(Q_LORA), row(Q_LORA), full(1, Q_LORA), full(1, KV_LORA),
                  full(1, HEAD_PAD), full(1, HEAD_PAD), full(Q_LORA, D), full(Q_LORA, D), full(KV_LORA, D),
                  row(HEAD_PAD), row(HEAD_PAD), row(HEAD_PAD)],
        out_specs=[row(LAT_PAD), row(D), row(D), full(1, Q_LORA), full(1, KV_LORA), full(1, HEAD_PAD), full(1, HEAD_PAD)],
        out_shape=[jax.ShapeDtypeStruct((t, LAT_PAD), BF16), jax.ShapeDtypeStruct((t, D), BF16), jax.ShapeDtypeStruct((t, D), BF16),
                   jax.ShapeDtypeStruct((1, Q_LORA), F32), jax.ShapeDtypeStruct((1, KV_LORA), F32),
                   jax.ShapeDtypeStruct((1, HEAD_PAD), F32), jax.ShapeDtypeStruct((1, HEAD_PAD), F32)],
        compiler_params=_params("arbitrary"),
    )(dq, dk, dv, lat, qn, ckv_ext, gq, gkv, ghq, ghk, wq, wk, wv, rc, rs1, rs2)


def _causal_keep(tq):
    r = lax.broadcasted_iota(jnp.int32, (tq, tq), 0)
    c = lax.broadcasted_iota(jnp.int32, (tq, tq), 1)
    return c <= r


def _flash_fwd(q, k, v, *, n_seq, seq, tq, name, hosted=None):
    nq = seq // tq

    def body(q_ref, k_ref, v_ref, o_ref, lse_ref):
        qi = pl.program_id(2)
        qv = q_ref[...]

        def step(j, carry, masked):
            m, l, acc = carry
            kj = k_ref[pl.ds(pl.multiple_of(j * tq, tq), tq), :]
            vj = v_ref[pl.ds(pl.multiple_of(j * tq, tq), tq), :]
            s = _dot_nt(qv, kj) * ATTN_SCALE
            if masked:
                s = jnp.where(_causal_keep(tq), s, NEG)
            m_new = jnp.maximum(m, jnp.max(s, axis=-1, keepdims=True))
            alpha = jnp.exp(m - m_new)
            p = jnp.exp(s - m_new)
            l = alpha * l + jnp.sum(p, axis=-1, keepdims=True)
            acc = alpha * acc + _dot_nn(p.astype(BF16), vj)
            return m_new, l, acc

        init = (jnp.full((tq, 1), NEG, F32), jnp.zeros((tq, 1), F32), jnp.zeros((tq, HEAD_PAD), F32))
        carry = lax.fori_loop(0, qi, lambda j, cr: step(j, cr, False), init)
        m, l, acc = step(qi, carry, True)
        o_ref[...] = (acc / l).astype(BF16)
        lse_ref[...] = jnp.broadcast_to(m + jnp.log(l), (tq, HEAD_PAD))

    qspec = pl.BlockSpec((tq, HEAD_PAD), lambda b, h, i: (b * nq + i, h))
    kspec = pl.BlockSpec((seq, HEAD_PAD), lambda b, h, i: (b, h))
    t = n_seq * seq
    return _call(
        body, name=name, grid=(n_seq, N_HEADS, nq), in_specs=[qspec, kspec, kspec], out_specs=[qspec, qspec],
        out_shape=[jax.ShapeDtypeStruct((t, D), BF16), jax.ShapeDtypeStruct((t, D), F32)], scratch_shapes=[],
        operands=(q, k, v), sem=("parallel", "parallel", "arbitrary"), hosted=hosted)


def _flash_bwd(q, k, v, o, lse, do, *, n_seq, seq, tq, name, hosted=None):
    nq = seq // tq

    def body(q_ref, k_ref, v_ref, o_ref, lse_ref, do_ref, dq_ref, dk_ref, dv_ref, dk_acc, dv_acc):
        j = pl.program_id(2)

        @pl.when(j == 0)
        def _():
            dq_ref[...] = jnp.zeros_like(dq_ref)

        dk_acc[...] = jnp.zeros_like(dk_acc)
        dv_acc[...] = jnp.zeros_like(dv_acc)
        kv = k_ref[...]
        vv = v_ref[...]

        def step(i, masked):
            rows = pl.ds(pl.multiple_of(i * tq, tq), tq)
            qi = q_ref[rows, :]
            doi = do_ref[rows, :]
            delta = jnp.sum(doi.astype(F32) * o_ref[rows, :].astype(F32), axis=-1, keepdims=True)
            s = _dot_nt(qi, kv) * ATTN_SCALE
            p = jnp.exp(s - lse_ref[rows, :][:, :1])
            if masked:
                p = jnp.where(_causal_keep(tq), p, 0.0)
            dv_acc[...] += _dot_tn(p.astype(BF16), doi)
            dp = _dot_nt(doi, vv)
            ds = (p * (dp - delta) * ATTN_SCALE).astype(BF16)
            dk_acc[...] += _dot_tn(ds, qi)
            dq_ref[rows, :] += _dot_nn(ds, kv)

        step(j, True)

        def loop_body(i, carry):
            step(i, False)
            return carry

        lax.fori_loop(j + 1, nq, loop_body, 0)
        dk_ref[...] = dk_acc[...]
        dv_ref[...] = dv_acc[...].astype(BF16)

    full = pl.BlockSpec((seq, HEAD_PAD), lambda b, h, j: (b, h))
    tile = pl.BlockSpec((tq, HEAD_PAD), lambda b, h, j: (b * nq + j, h))
    t = n_seq * seq
    return _call(
        body, name=name, grid=(n_seq, N_HEADS, nq), in_specs=[full, tile, tile, full, full, full],
        out_specs=[full, tile, tile],
        out_shape=[jax.ShapeDtypeStruct((t, D), F32), jax.ShapeDtypeStruct((t, D), F32), jax.ShapeDtypeStruct((t, D), BF16)],
        scratch_shapes=[pltpu.VMEM((tq, HEAD_PAD), F32), pltpu.VMEM((tq, HEAD_PAD), F32)],
        operands=(q, k, v, o, lse, do), sem=("parallel", "parallel", "arbitrary"), hosted=hosted)


CONV_CB = 256


def _shift_down(u, k, row):
    return jnp.where(row >= k, pltpu.roll(u, k, 0), 0.0)


def _shift_up(u, k, row, n):
    return jnp.where(row < n - k, pltpu.roll(u, n - k, 0), 0.0)


def _conv_fwd(conv3, cw, *, n_seq, seq, name):
    def body(c_ref, w_ref, p_ref):
        blk = c_ref[...].astype(F32)
        xc, gb, gc = blk[:, :CONV_CB], blk[:, CONV_CB:2 * CONV_CB], blk[:, 2 * CONV_CB:]
        row = lax.broadcasted_iota(jnp.int32, (seq, CONV_CB), 0)
        u = gc * xc
        z = w_ref[0:1, :] * _shift_down(u, 2, row) + w_ref[1:2, :] * _shift_down(u, 1, row) + w_ref[2:3, :] * u
        p_ref[...] = (gb * z).astype(BF16)

    return pl.pallas_call(
        body, name=name, grid=(n_seq, D // CONV_CB),
        in_specs=[pl.BlockSpec((seq, 3 * CONV_CB), lambda b, j: (b, j)), pl.BlockSpec((3, CONV_CB), lambda b, j: (0, j))],
        out_specs=pl.BlockSpec((seq, CONV_CB), lambda b, j: (b, j)),
        out_shape=jax.ShapeDtypeStruct((n_seq * seq, D), BF16),
        compiler_params=_params("parallel", "parallel"),
    )(conv3, cw)


def _conv_bwd(dp, conv3, cw, *, n_seq, seq, name):
    def body(dp_ref, c_ref, w_ref, dc_ref, dw_ref):
        @pl.when(pl.program_id(1) == 0)
        def _():
            dw_ref[...] = jnp.zeros_like(dw_ref)

        blk = c_ref[...].astype(F32)
        xc, gb, gc = blk[:, :CONV_CB], blk[:, CONV_CB:2 * CONV_CB], blk[:, 2 * CONV_CB:]
        row = lax.broadcasted_iota(jnp.int32, (seq, CONV_CB), 0)
        w0, w1, w2 = w_ref[0:1, :], w_ref[1:2, :], w_ref[2:3, :]
        u = gc * xc
        u1 = _shift_down(u, 1, row)
        u2 = _shift_down(u, 2, row)
        z = w0 * u2 + w1 * u1 + w2 * u
        dpv = dp_ref[...].astype(F32)
        dz = dpv * gb
        du = w2 * dz + w1 * _shift_up(dz, 1, row, seq) + w0 * _shift_up(dz, 2, row, seq)
        dc_ref[...] = jnp.concatenate([du * gc, dpv * z, du * xc], axis=1).astype(BF16)
        dw_ref[0:1, :] += jnp.sum(dz * u2, axis=0, keepdims=True)
        dw_ref[1:2, :] += jnp.sum(dz * u1, axis=0, keepdims=True)
        dw_ref[2:3, :] += jnp.sum(dz * u, axis=0, keepdims=True)

    return pl.pallas_call(
        body, name=name, grid=(D // CONV_CB, n_seq),
        in_specs=[pl.BlockSpec((seq, CONV_CB), lambda j, b: (b, j)), pl.BlockSpec((seq, 3 * CONV_CB), lambda j, b: (b, j)),
                  pl.BlockSpec((3, CONV_CB), lambda j, b: (0, j))],
        out_specs=[pl.BlockSpec((seq, 3 * CONV_CB), lambda j, b: (b, j)), pl.BlockSpec((3, CONV_CB), lambda j, b: (0, j))],
        out_shape=[jax.ShapeDtypeStruct((n_seq * seq, CONV_COLS), BF16), jax.ShapeDtypeStruct((3, D), F32)],
        compiler_params=_params("parallel", "arbitrary"),
    )(dp, conv3, cw)


def _merge_fwd(o, p, gl, bias, x1, wpa, wpc, wout, *, tm, name, hosted=None):
    t = x1.shape[0]

    def body(o_ref, p_ref, gl_ref, b_ref, x_ref, wpa_ref, wpc_ref, wout_ref, x2_ref, mg_ref, ya_ref, yb_ref):
        ya = _dot_nn(o_ref[...], wpa_ref[...])
        yb = _dot_nn(p_ref[...], wpc_ref[...])
        gates = _sigmoid(gl_ref[...].astype(F32) + b_ref[...])
        merged = (gates[:, :D] * ya + gates[:, D:] * yb).astype(BF16)
        ya_ref[...] = ya.astype(BF16)
        yb_ref[...] = yb.astype(BF16)
        mg_ref[...] = merged
        x2_ref[...] = x_ref[...] + _dot_nn(merged, wout_ref[...])

    row = pl.BlockSpec((tm, D), lambda i: (i, 0))
    row2 = pl.BlockSpec((tm, GATE_COLS), lambda i: (i, 0))
    wsp = pl.BlockSpec((D, D), lambda i: (0, 0))
    wide = jax.ShapeDtypeStruct((t, D), BF16)
    return _call(
        body, name=name, grid=(t // tm,),
        in_specs=[row, row, row2, pl.BlockSpec((1, GATE_COLS), lambda i: (0, 0)), row, wsp, wsp, wsp],
        out_specs=[row, row, row, row], out_shape=[jax.ShapeDtypeStruct((t, D), F32), wide, wide, wide], scratch_shapes=[],
        operands=(o, p, gl, bias, x1, wpa, wpc, wout), sem=("parallel",), hosted=hosted)


def _merge_bwd(dx2, ya, yb, gl, bias, wpa, wpc, wout, *, tm, name, hosted=None):
    t = dx2.shape[0]

    def body(dx_ref, ya_ref, yb_ref, gl_ref, b_ref, wpa_ref, wpc_ref, wout_ref,
             dxb_ref, dya_ref, dyb_ref, dgl_ref, do_ref, dp_ref, db_ref):
        @pl.when(pl.program_id(0) == 0)
        def _():
            db_ref[...] = jnp.zeros_like(db_ref)

        dxb = dx_ref[...].astype(BF16)
        dxb_ref[...] = dxb
        dm = _dot_nt(dxb, wout_ref[...])
        gates = _sigmoid(gl_ref[...].astype(F32) + b_ref[...])
        ga, gb = gates[:, :D], gates[:, D:]
        dya = (dm * ga).astype(BF16)
        dyb = (dm * gb).astype(BF16)
        dya_ref[...] = dya
        dyb_ref[...] = dyb
        dgl = jnp.concatenate([dm * ya_ref[...].astype(F32) * ga * (1.0 - ga),
                               dm * yb_ref[...].astype(F32) * gb * (1.0 - gb)], axis=1)
        dgl_ref[...] = dgl.astype(BF16)
        db_ref[...] += jnp.sum(dgl, axis=0, keepdims=True)
        do_ref[...] = _dot_nt(dya, wpa_ref[...]).astype(BF16)
        dp_ref[...] = _dot_nt(dyb, wpc_ref[...]).astype(BF16)

    row = pl.BlockSpec((tm, D), lambda i: (i, 0))
    row2 = pl.BlockSpec((tm, GATE_COLS), lambda i: (i, 0))
    vec2 = pl.BlockSpec((1, GATE_COLS), lambda i: (0, 0))
    wsp = pl.BlockSpec((D, D), lambda i: (0, 0))
    wide = jax.ShapeDtypeStruct((t, D), BF16)
    return _call(
        body, name=name, grid=(t // tm,), in_specs=[row, row, row, row2, vec2, wsp, wsp, wsp],
        out_specs=[row, row, row, row2, row, row, vec2],
        out_shape=[wide, wide, wide, jax.ShapeDtypeStruct((t, GATE_COLS), BF16), wide, wide,
                   jax.ShapeDtypeStruct((1, GATE_COLS), F32)],
        scratch_shapes=[], operands=(dx2, ya, yb, gl, bias, wpa, wpc, wout), sem=("arbitrary",), hosted=hosted)


def _adamw(w, g, m, v, *, name):
    rows, cols = w.shape
    tr = max([c for c in range(8, 513, 8) if rows % c == 0], default=rows)
    c1 = 1.0 / (1.0 - ADAM_B1 ** ADAM_STEP)
    c2 = 1.0 / (1.0 - ADAM_B2 ** ADAM_STEP)

    def body(w_ref, g_ref, m_ref, v_ref, d_ref, nm_ref, nv_ref):
        gv = g_ref[...]
        nm = ADAM_B1 * m_ref[...] + (1.0 - ADAM_B1) * gv
        nv = ADAM_B2 * v_ref[...] + (1.0 - ADAM_B2) * (gv * gv)
        nm_ref[...] = nm
        nv_ref[...] = nv
        d_ref[...] = -ADAM_LR * ((nm * c1) / (jnp.sqrt(nv * c2) + ADAM_EPS) + ADAM_WD * w_ref[...])

    spec = pl.BlockSpec((tr, cols), lambda i: (i, 0))
    shp = jax.ShapeDtypeStruct((rows, cols), F32)
    return pl.pallas_call(
        body, name=name, grid=(rows // tr,), in_specs=[spec] * 4, out_specs=[spec] * 3, out_shape=[shp] * 3,
        compiler_params=_params("parallel"),
    )(w, g, m, v)


def _place():
    return lax.axis_index("x"), lax.axis_index("y"), lax.axis_index("c")


def _other_chips(x, y):
    return [(1 - x, y), (x, 1 - y), (1 - x, 1 - y)]


def _remote(src, dst, send, recv, dev):
    return pltpu.make_async_remote_copy(src_ref=src, dst_ref=dst, send_sem=send, recv_sem=recv, device_id=dev, device_id_type=MESH)


def _gather_chips_plan(n):
    def start(srcs, dsts, send, recv, local):
        x, y, cc = _place()
        me = 4 * x + 2 * y + cc
        for a in range(n):
            pltpu.make_async_copy(srcs[a], dsts[a].at[me], local.at[a]).start()
            for k, (px, py) in enumerate(_other_chips(x, y)):
                _remote(srcs[a], dsts[a].at[me], send.at[3 * a + k], recv.at[3 * a + k], (px, py, cc)).start()

    def wait(srcs, dsts, send, recv, local):
        x, y, cc = _place()
        me = 4 * x + 2 * y + cc
        for a in range(n):
            for k, (px, py) in enumerate(_other_chips(x, y)):
                _remote(srcs[a], dsts[a].at[4 * px + 2 * py + cc], send.at[3 * a + k], recv.at[3 * a + k], (px, py, cc)).wait_recv()
        for a in range(n):
            for k, (px, py) in enumerate(_other_chips(x, y)):
                _remote(srcs[a], dsts[a].at[me], send.at[3 * a + k], recv.at[3 * a + k], (px, py, cc)).wait_send()
            pltpu.make_async_copy(srcs[a], dsts[a].at[me], local.at[a]).wait()

    return _Plan(start, wait, 3 * n, n)


def _scatter_chips_plan(n):
    def start(srcs, dsts, send, recv, local):
        x, y, cc = _place()
        for a in range(n):
            for k, (px, py) in enumerate(_other_chips(x, y)):
                _remote(srcs[a].at[2 * px + py], dsts[a].at[k], send.at[3 * a + k], recv.at[3 * a + k], (px, py, cc)).start()

    def wait(srcs, dsts, send, recv, local):
        x, y, cc = _place()
        for a in range(n):
            for k, (px, py) in enumerate(_other_chips(x, y)):
                _remote(srcs[a].at[k], dsts[a].at[k], send.at[3 * a + k], recv.at[3 * a + k], (px, py, cc)).wait_recv()
        for a in range(n):
            for k, (px, py) in enumerate(_other_chips(x, y)):
                _remote(srcs[a].at[k], dsts[a].at[k], send.at[3 * a + k], recv.at[3 * a + k], (px, py, cc)).wait_send()

    return _Plan(start, wait, 3 * n, 0)


def _gather_shapes(blocks):
    return [jax.ShapeDtypeStruct((N_DEV,) + b.shape, b.dtype) for b in blocks]


def _scatter_shapes(parts):
    return [jax.ShapeDtypeStruct((3,) + p.shape[1:], p.dtype) for p in parts]


def _gather_sibling_plan(n):
    def start(srcs, dsts, send, recv, local):
        x, y, cc = _place()
        for a in range(n):
            for q in range(4):
                _remote(srcs[a].at[2 * q + cc], dsts[a].at[2 * q + cc], send.at[4 * a + q], recv.at[4 * a + q], (x, y, 1 - cc)).start()

    def wait(srcs, dsts, send, recv, local):
        x, y, cc = _place()
        for a in range(n):
            for q in range(4):
                _remote(srcs[a].at[2 * q + cc], dsts[a].at[2 * q + 1 - cc], send.at[4 * a + q], recv.at[4 * a + q],
                        (x, y, 1 - cc)).wait_recv()
        for a in range(n):
            for q in range(4):
                _remote(srcs[a].at[2 * q + cc], dsts[a].at[2 * q + cc], send.at[4 * a + q], recv.at[4 * a + q],
                        (x, y, 1 - cc)).wait_send()

    return _Plan(start, wait, 4 * n, 0, in_place=True)


def _scatter_sibling_plan(n):
    def start(srcs, dsts, send, recv, local):
        x, y, cc = _place()
        for a in range(n):
            for q in range(4):
                _remote(srcs[a].at[2 * q + 1 - cc], dsts[a].at[q], send.at[4 * a + q], recv.at[4 * a + q], (x, y, 1 - cc)).start()

    def wait(srcs, dsts, send, recv, local):
        x, y, cc = _place()
        for a in range(n):
            for q in range(4):
                _remote(srcs[a].at[q], dsts[a].at[q], send.at[4 * a + q], recv.at[4 * a + q], (x, y, 1 - cc)).wait_recv()
        for a in range(n):
            for q in range(4):
                _remote(srcs[a].at[q], dsts[a].at[q], send.at[4 * a + q], recv.at[4 * a + q], (x, y, 1 - cc)).wait_send()

    return _Plan(start, wait, 4 * n, 0)


def _same_shapes(arrs):
    return [jax.ShapeDtypeStruct(a.shape, a.dtype) for a in arrs]


def _halved_shapes(parts):
    return [jax.ShapeDtypeStruct((4,) + p.shape[1:], p.dtype) for p in parts]


def _run_plan(plan, srcs, out_shapes, *, name):
    n_in, n_out = len(srcs), len(out_shapes)

    def body(*refs):
        h_in, h_out, sems = refs[:n_in], refs[n_in:n_in + n_out], refs[n_in + n_out:]
        plan.start(h_in, h_out, *sems)
        plan.wait(h_in, h_out, *sems)

    return pl.pallas_call(body, name=name, in_specs=[ANY] * n_in, out_specs=[ANY] * n_out, out_shape=list(out_shapes),
                          input_output_aliases={a: a for a in range(n_in)} if plan.in_place else {},
                          scratch_shapes=plan.sems())(*srcs)


def _sum_sibling(p, q, core, *, name):
    _, r, c = p.shape

    def body(core_ref, p_ref, q_ref, o_ref):
        o_ref[...] = (p_ref[...].astype(F32) + q_ref[...].astype(F32)).astype(BF16)

    grid_spec = pltpu.PrefetchScalarGridSpec(
        num_scalar_prefetch=1, grid=(4,),
        in_specs=[pl.BlockSpec((1, r, c), lambda ch, core_ref: (2 * ch + core_ref[0], 0, 0)),
                  pl.BlockSpec((1, r, c), lambda ch, core_ref: (ch, 0, 0))],
        out_specs=pl.BlockSpec((1, r, c), lambda ch, core_ref: (ch, 0, 0)))
    return pl.pallas_call(
        body, name=name, grid_spec=grid_spec, out_shape=jax.ShapeDtypeStruct((4, r, c), BF16),
        compiler_params=_params("parallel"),
    )(core, p, q)


def _sum_chips(s1, r2, chip, *, name):
    _, r, c = s1.shape

    def body(chip_ref, s_ref, r_ref, o_ref):
        acc = s_ref[0].astype(F32)
        for k in range(3):
            acc = acc + r_ref[k].astype(F32)
        o_ref[...] = acc

    grid_spec = pltpu.PrefetchScalarGridSpec(
        num_scalar_prefetch=1, grid=(1,),
        in_specs=[pl.BlockSpec((1, r, c), lambda i, chip_ref: (chip_ref[0], 0, 0)),
                  pl.BlockSpec((3, r, c), lambda i, chip_ref: (0, 0, 0))],
        out_specs=pl.BlockSpec((r, c), lambda i, chip_ref: (0, 0)))
    return pl.pallas_call(
        body, name=name, grid_spec=grid_spec, out_shape=jax.ShapeDtypeStruct((r, c), F32),
        compiler_params=_params("arbitrary"),
    )(chip, s1, r2)


def _sum_adamw(s1, r2, chip, w, m, v, *, name):
    _, r, c = s1.shape
    c1 = 1.0 / (1.0 - ADAM_B1 ** ADAM_STEP)
    c2 = 1.0 / (1.0 - ADAM_B2 ** ADAM_STEP)

    def body(chip_ref, s_ref, r_ref, w_ref, m_ref, v_ref, g_ref, d_ref, nm_ref, nv_ref):
        gv = s_ref[0].astype(F32)
        for k in range(3):
            gv = gv + r_ref[k].astype(F32)
        g_ref[...] = gv
        nm = ADAM_B1 * m_ref[...] + (1.0 - ADAM_B1) * gv
        nv = ADAM_B2 * v_ref[...] + (1.0 - ADAM_B2) * (gv * gv)
        nm_ref[...] = nm
        nv_ref[...] = nv
        d_ref[...] = -ADAM_LR * ((nm * c1) / (jnp.sqrt(nv * c2) + ADAM_EPS) + ADAM_WD * w_ref[...])

    flat = pl.BlockSpec((r, c), lambda i, chip_ref: (0, 0))
    grid_spec = pltpu.PrefetchScalarGridSpec(
        num_scalar_prefetch=1, grid=(1,),
        in_specs=[pl.BlockSpec((1, r, c), lambda i, chip_ref: (chip_ref[0], 0, 0)),
                  pl.BlockSpec((3, r, c), lambda i, chip_ref: (0, 0, 0)), flat, flat, flat],
        out_specs=[flat] * 4)
    return pl.pallas_call(
        body, name=name, grid_spec=grid_spec, out_shape=[jax.ShapeDtypeStruct((r, c), F32)] * 4,
        compiler_params=_params("arbitrary"),
    )(chip, s1, r2, w, m, v)


def _small_exchange(v, *, reduce, name):
    r, c = v.shape

    def body(x_ref, o_ref, *rest):
        if reduce:
            buf_ref, send_sems, recv_sems = rest
        else:
            buf_ref = o_ref
            send_sems, recv_sems = rest
        x, y, cc = _place()
        me = 4 * x + 2 * y + cc

        def peer(k):
            return ((1 - x) if k & 4 else x, (1 - y) if k & 2 else y, (1 - cc) if k & 1 else cc)

        buf_ref[me] = x_ref[...]
        sends = []
        for k in range(1, N_DEV):
            cp = pltpu.make_async_remote_copy(src_ref=x_ref, dst_ref=buf_ref.at[me], send_sem=send_sems.at[k - 1],
                                              recv_sem=recv_sems.at[k - 1], device_id=peer(k), device_id_type=MESH)
            cp.start()
            sends.append(cp)
        for k in range(1, N_DEV):
            px, py, pc = peer(k)
            pltpu.make_async_remote_copy(src_ref=x_ref, dst_ref=buf_ref.at[4 * px + 2 * py + pc], send_sem=send_sems.at[k - 1],
                                         recv_sem=recv_sems.at[k - 1], device_id=peer(k), device_id_type=MESH).wait_recv()
        for cp in sends:
            cp.wait_send()
        if reduce:
            acc = buf_ref[0]
            for s in range(1, N_DEV):
                acc = acc + buf_ref[s]
            o_ref[...] = acc

    vm = pl.BlockSpec(memory_space=pltpu.VMEM)
    sems = [pltpu.SemaphoreType.DMA((N_DEV - 1,)), pltpu.SemaphoreType.DMA((N_DEV - 1,))]
    if reduce:
        out_shape, scratch = jax.ShapeDtypeStruct((r, c), F32), [pltpu.VMEM((N_DEV, r, c), F32)] + sems
    else:
        out_shape, scratch = jax.ShapeDtypeStruct((N_DEV, r, c), F32), sems
    return pl.pallas_call(body, name=name, in_specs=[vm], out_specs=vm, out_shape=out_shape, scratch_shapes=scratch)(v)


def _rows(a):
    return a.reshape(-1, D)


def _pad_cols(a, to):
    return jnp.pad(a, ((0, 0), (0, to - a.shape[1])))


def _pack_weights(w):
    parts = {
        "w_inT": jnp.pad(w["w_in"].T, ((0, IN_SHARD_PAD - IN_SHARD), (0, 0))),
        "w_uq": _rows(_pad_cols(w["w_uq"], HEAD_PAD)), "w_uk": _rows(_pad_cols(w["w_uk"], HEAD_PAD)),
        "w_uv": _rows(_pad_cols(w["w_uv"], HEAD_PAD)), "w_pa": _rows(w["w_proj_attn"]),
        "w_pc": w["w_proj_conv"], "w_out": w["w_out"],
    }
    return [jnp.concatenate([parts[n].astype(BF16) for n, _ in group], axis=0) for group in PACK]


def _cols_from_shards(gs, name, rows):
    idx, off, r = PACK_OFF[name]
    return gs[idx][:, off:off + r].reshape(N_DEV, rows, HEAD_PAD).transpose(1, 0, 2).reshape(rows, N_DEV * HEAD_PAD)


def _rows_from_shards(gs, name, keep=None):
    idx, off, r = PACK_OFF[name]
    keep = r if keep is None else keep
    return gs[idx][:, off:off + keep].reshape(N_DEV * keep, D)


def _rope_placement():
    i = lax.broadcasted_iota(jnp.int32, (HEAD_PAD, D), 0)
    j = lax.broadcasted_iota(jnp.int32, (HEAD_PAD, D), 1)
    return ((i < 2 * ROPE_HALF) & (j % HEAD_PAD == NOPE + i)).astype(BF16)


def _unpack_in(g_in):
    w_inT = _rows_from_shards([g_in, None], "w_inT", IN_SHARD)
    lat_rows = Q_LORA + KV_LORA + 2 * ROPE_HALF
    conv = w_inT[lat_rows:lat_rows + CONV_COLS].reshape(3, D // CONV_CB, CONV_CB, D).transpose(1, 0, 2, 3).reshape(CONV_COLS, D)
    return {"latT": jnp.pad(w_inT[:lat_rows], ((0, LAT_PAD - lat_rows), (0, 0))), "convT": conv,
            "gateT": w_inT[lat_rows + CONV_COLS:]}


def _unpack_misc(g_misc):
    g = [None, g_misc]
    wpa = _cols_from_shards(g, "w_pa", 512).reshape(N_HEADS, NOPE, D)
    return {
        "wq": _cols_from_shards(g, "w_uq", Q_LORA),
        "wk": jnp.concatenate([_cols_from_shards(g, "w_uk", KV_LORA), _rope_placement()], axis=0),
        "wv": _cols_from_shards(g, "w_uv", KV_LORA),
        "wpa": jnp.pad(wpa, ((0, 0), (0, HEAD_PAD - NOPE), (0, 0))).reshape(D, D),
        "wpc": _rows_from_shards(g, "w_pc"), "wout": _rows_from_shards(g, "w_out"),
    }


def _shards_from_cols(a):
    rows = a.shape[0]
    return a.reshape(rows, N_DEV, HEAD_PAD).transpose(1, 0, 2).reshape(N_DEV, rows * HEAD_PAD // D, D)


def _pack_grads(gw):
    lat_rows = Q_LORA + KV_LORA + 2 * ROPE_HALF
    conv = gw["convT"].reshape(D // CONV_CB, 3, CONV_CB, D).transpose(1, 0, 2, 3).reshape(CONV_COLS, D)
    w_inT = jnp.concatenate([gw["latT"][:lat_rows], conv, gw["gateT"]], axis=0).reshape(N_DEV, IN_SHARD, D)
    wpa = gw["wpa"].reshape(N_HEADS, HEAD_PAD, D)[:, :NOPE].reshape(N_HEADS * NOPE, D)
    parts = {}
    parts.update({
        "w_inT": jnp.pad(w_inT, ((0, 0), (0, IN_SHARD_PAD - IN_SHARD), (0, 0))),
        "w_uq": _shards_from_cols(gw["wq"]), "w_uk": _shards_from_cols(gw["wk"][:KV_LORA]),
        "w_uv": _shards_from_cols(gw["wv"][:KV_LORA]), "w_pa": _shards_from_cols(wpa),
        "w_pc": gw["wpc"].reshape(N_DEV, D // N_DEV, D), "w_out": gw["wout"].reshape(N_DEV, D // N_DEV, D),
    })
    return [jnp.concatenate([parts[n] for n, _ in group], axis=1) for group in PACK]


def _unpack_grads(mines):
    def seg(name, keep=None):
        idx, off, r = PACK_OFF[name]
        return mines[idx][off:off + (r if keep is None else keep)]

    return {
        "w_in": seg("w_inT", IN_SHARD).T,
        "w_uq": seg("w_uq").reshape(Q_LORA, HEAD_PAD)[:, :QK_DIM],
        "w_uk": seg("w_uk").reshape(KV_LORA, HEAD_PAD)[:, :NOPE],
        "w_uv": seg("w_uv").reshape(KV_LORA, HEAD_PAD)[:, :NOPE],
        "w_proj_attn": seg("w_pa").reshape(512, HEAD_PAD),
        "w_proj_conv": seg("w_pc"), "w_out": seg("w_out"),
    }


def _rope_tables(positions):
    inv_freq = 1.0 / (ROPE_THETA ** (jnp.arange(ROPE_HALF, dtype=F32) / ROPE_HALF))
    ang = positions.reshape(-1).astype(F32)[:, None] * inv_freq
    cos, sin = jnp.cos(ang), jnp.sin(ang)
    t = ang.shape[0]
    zero = jnp.zeros((t, ROPE_HALF), F32)
    head = jnp.ones((t, NOPE), F32)
    tail = jnp.zeros((t, HEAD_PAD - QK_DIM), F32)
    nohead = jnp.zeros((t, NOPE), F32)
    rc = jnp.concatenate([head, cos, cos, tail], axis=1)
    rs1 = jnp.concatenate([nohead, -sin, zero, tail], axis=1)
    rs2 = jnp.concatenate([nohead, zero, sin, tail], axis=1)
    return rc, rs1, rs2


def _local_step(x, positions, target, conv_w, small, ex):
    n_seq, seq, d = x.shape
    t = n_seq * seq
    x0 = x.reshape(t, d)
    tgt = target.reshape(t, d)
    rc, rs1, rs2 = _rope_tables(positions)
    ghq = _pad_cols(small["q_head_norm"], HEAD_PAD)
    ghk = _pad_cols(small["k_head_norm"], HEAD_PAD)
    TM, HC, TQ = 1024, 256, 512

    def mm(*args, hosted=None, **kw):
        res = _mm(*args, hosted=hosted, **kw)
        return res if hosted is not None else (res, None)

    def wgrad(a, b, name, tm=None, hosted=None):
        return mm(a, b, mode="tn", out_dtype=BF16, tm=tm or a.shape[1], tn=b.shape[1], tk=512, name=name, hosted=hosted)

    f1g, f1u, f1d = ex.gather_now("ffn1")
    (x1, h1, a1, b1), got = _ffn_fwd(x0, small["ffn1_norm"], f1g, f1u, f1d, tm=TM, hc=HC, name="ffn1_fwd",
                                     hosted=ex.gather_chips("mix_in"))
    hm, got = _rms_fwd(x1, small["mix_norm"], tm=TM, name="mix_norm_fwd", hosted=ex.gather_sibling(got))
    W = ex.mix_in_weights(got)
    lat = _mm(hm, W["latT"], mode="nt", out_dtype=BF16, tm=TM, tn=LAT_PAD, tk=D, name="proj_lat")
    conv3, got = mm(hm, W["convT"], mode="nt", out_dtype=BF16, tm=TM, tn=CONV_COLS // 2, tk=D, name="proj_conv",
                    hosted=ex.gather_chips("mix_misc"))
    gl, got = mm(hm, W["gateT"], mode="nt", out_dtype=BF16, tm=TM, tn=GATE_COLS // 2, tk=D, name="proj_gate",
                 hosted=ex.gather_sibling(got))
    W.update(ex.mix_misc_weights(got))
    q, k, v, qn, ckv = _mla_prep_fwd(lat, small["q_a_norm"], small["kv_a_norm"], ghq, ghk, W["wq"], W["wk"], W["wv"], rc, rs1, rs2,
                                     tm=512, name="mla_prep_fwd")
    (o, lse), got = _flash_fwd(q, k, v, n_seq=n_seq, seq=seq, tq=TQ, name="attn_fwd", hosted=ex.gather_chips("ffn2"))
    p = _conv_fwd(conv3, conv_w, n_seq=n_seq, seq=seq, name="conv_fwd")
    (x2, merged, ya, yb), got = _merge_fwd(o, p, gl, small["gate_bias"], x1, W["wpa"], W["wpc"], W["wout"], tm=512, name="merge_fwd",
                                           hosted=ex.gather_sibling(got))
    f2g, f2u, f2d = ex.ffn_weights(got)
    (dy, h2, a2, b2, loss_row), _ = _ffn_fwd(x2, small["ffn2_norm"], f2g, f2u, f2d, tm=TM, hc=HC, name="ffn2_fwd", target=tgt)

    gw, gs = {}, {}
    (da2, db2, *ffn2_grads), _ = _ffn_grads(dy, h2, a2, b2, f2d, tm=TM, hc=HC, name="ffn2_grads")
    (dx2, gs["ffn2_norm"]), _ = _ffn_up_bwd(da2, db2, f2g, f2u, x2, small["ffn2_norm"], dy, tm=512, name="ffn2_up_bwd")

    (dx2b, dya, dyb, dgl, do, dp, gs["gate_bias"]), got = _merge_bwd(
        dx2, ya, yb, gl, small["gate_bias"], W["wpa"], W["wpc"], W["wout"], tm=512, name="merge_bwd",
        hosted=ex.scatter_sibling("ffn2", ffn2_grads))
    ex.scatter_sibling_done("ffn2", got)
    gw["wout"] = wgrad(merged, dx2b, "dw_out")[0]
    gw["wpa"] = wgrad(o, dya, "dw_pa")[0]
    gw["wpc"] = wgrad(p, dyb, "dw_pc")[0]
    dconv3, dconv_w = _conv_bwd(dp, conv3, conv_w, n_seq=n_seq, seq=seq, name="conv_bwd")
    (dq, dk, dv), got = _flash_bwd(q, k, v, o, lse, do, n_seq=n_seq, seq=seq, tq=TQ, name="attn_bwd",
                                   hosted=ex.scatter_chips("ffn2"))
    ex.scatter_chips_done("ffn2", got)
    dlat, dqp, dkp, gs["q_a_norm"], gs["kv_a_norm"], dghq, dghk = _mla_prep_bwd(
        dq, dk, dv, lat, qn, ckv, small["q_a_norm"], small["kv_a_norm"], ghq, ghk, W["wq"], W["wk"], W["wv"], rc, rs1, rs2,
        tm=512, name="mla_prep_bwd")
    gs["q_head_norm"], gs["k_head_norm"] = dghq[:, :QK_DIM], dghk[:, :QK_DIM]
    gw["wq"] = wgrad(qn, dqp, "dw_uq")[0]
    gw["wk"] = wgrad(ckv, dkp, "dw_uk")[0]
    gw["wv"] = wgrad(ckv, dv, "dw_uv")[0]
    gw["convT"] = wgrad(dconv3, hm, "dw_conv", tm=CONV_COLS // 2)[0]
    gw["gateT"] = wgrad(dgl, hm, "dw_gate")[0]
    gw["latT"] = wgrad(dlat, hm, "dw_lat")[0]
    ex.scatter_sibling_now("mix", gw)
    (dx1, gs["mix_norm"]), got = _proj_bwd(dlat, dconv3, dgl, W["latT"], W["convT"], W["gateT"], x1, small["mix_norm"], dx2,
                                           tm=512, name="proj_bwd", hosted=ex.scatter_chips("mix_in"))
    ex.scatter_chips_done("mix_in", got)

    (da1, db1, *ffn1_grads), got = _ffn_grads(dx1, h1, a1, b1, f1d, tm=TM, hc=HC, name="ffn1_grads",
                                              hosted=ex.scatter_chips("mix_misc"))
    ex.scatter_chips_done("mix_misc", got)
    ex.scatter_sibling_now("ffn1", ffn1_grads)
    (dx0, gs["ffn1_norm"]), got = _ffn_up_bwd(da1, db1, f1g, f1u, x0, small["ffn1_norm"], dx1, tm=512, name="ffn1_up_bwd",
                                              hosted=ex.scatter_chips("ffn1"))
    ex.scatter_chips_done("ffn1", got)
    return loss_row, dx0.reshape(n_seq, seq, d), dconv_w, gs


class _MeshExchange:
    def __init__(self, w, core, chip):
        self.w, self.core, self.chip = w, core, chip
        self.partial, self.received, self._packed = {}, {}, None

    def _blocks(self, group):
        w = self.w
        if group.startswith("ffn"):
            return [w[group + "_w_gate"].T.astype(BF16), w[group + "_w_up"].T.astype(BF16), w[group + "_w_down"].astype(BF16)]
        if self._packed is None:
            self._packed = _pack_weights(w)
        return [self._packed[0 if group == "mix_in" else 1]]

    def gather_chips(self, *groups):
        blocks = [b for group in groups for b in self._blocks(group)]
        return _gather_chips_plan(len(blocks)), blocks, _gather_shapes(blocks)

    def gather_sibling(self, got):
        half = list(got)
        return _gather_sibling_plan(len(half)), half, _same_shapes(half)

    def gather_now(self, group):
        plan, blocks, shapes = self.gather_chips(group)
        half = list(_run_plan(plan, blocks, shapes, name="gather_%s_chips" % group))
        return self.ffn_weights(_run_plan(_gather_sibling_plan(len(half)), half, _same_shapes(half), name="gather_%s_sibling" % group))

    def ffn_weights(self, got):
        return [a.reshape(DFF, D) for a in got]

    def mix_in_weights(self, got):
        return _unpack_in(got[0])

    def mix_misc_weights(self, got):
        return _unpack_misc(got[0])

    def _parts(self, group, grads):
        if group == "mix":
            return _pack_grads(grads), ["mix_in", "mix_misc"]
        parts = [g.reshape(N_DEV, -1, D) for g in grads]
        return parts, ([group] if len(parts) == 1 else None)

    def scatter_sibling(self, group, grads):
        self._sent, self._names = self._parts(group, grads)
        return _scatter_sibling_plan(len(self._sent)), self._sent, _halved_shapes(self._sent)

    def scatter_sibling_done(self, group, got):
        sums = [_sum_sibling(p, q, self.core, name="sum_%s_sibling_%d" % (group, i)) for i, (p, q) in enumerate(zip(self._sent, got))]
        if self._names is None:
            self.partial[group] = sums
        else:
            for n, s in zip(self._names, sums):
                self.partial[n] = [s]

    def scatter_sibling_now(self, group, grads):
        plan, parts, shapes = self.scatter_sibling(group, grads)
        self.scatter_sibling_done(group, _run_plan(plan, parts, shapes, name="scatter_%s_sibling" % group))

    def scatter_chips(self, group):
        s1 = self.partial[group]
        return _scatter_chips_plan(len(s1)), s1, _scatter_shapes(s1)

    def scatter_chips_done(self, group, got):
        self.received[group] = list(got)


SMALL_NAMES = ("ffn1_norm", "mix_norm", "gate_bias", "q_a_norm", "kv_a_norm", "q_head_norm", "k_head_norm", "ffn2_norm")
SMALL_SLOTS = {"ffn1_norm": 1024, "mix_norm": 1024, "gate_bias": 2048, "q_a_norm": 384, "kv_a_norm": 256, "q_head_norm": 128,
               "k_head_norm": 128, "ffn2_norm": 1024, "conv_w": 3072, "loss": 128}
COLUMN_MAJOR = ("w_in", "w_uq", "w_uk", "w_uv")
WEIGHT_NAMES = ("ffn1_norm", "ffn1_w_gate", "ffn1_w_up", "ffn1_w_down", "mix_norm", "w_in", "gate_bias", "q_a_norm", "w_uq",
                "kv_a_norm", "w_uk", "w_uv", "q_head_norm", "k_head_norm", "w_proj_attn", "conv_w", "w_proj_conv", "w_out",
                "ffn2_norm", "ffn2_w_gate", "ffn2_w_up", "ffn2_w_down")


def _step(x, positions, loss_target, w, m, v):
    xi, yi, ci = _place()
    core = ci.astype(jnp.int32).reshape(1)
    chip = (2 * xi + yi).astype(jnp.int32).reshape(1)
    me = 4 * xi + 2 * yi + ci

    cw_all = _small_exchange(jnp.pad(w["conv_w"], ((0, 5), (0, 0))), reduce=False, name="gather_conv_w")
    conv_w = cw_all[:, :3].transpose(1, 0, 2).reshape(3, D)
    small = {n: w[n].reshape(1, -1) for n in SMALL_NAMES}
    ex = _MeshExchange(w, core, chip)

    loss_row, grad_x, dconv_w, gs = _local_step(x, positions, loss_target, conv_w, small, ex)

    grads, deltas, new_m, new_v = {}, {}, {}, {}
    where = {"ffn1_w_gate": ("ffn1", 0), "ffn1_w_up": ("ffn1", 1), "ffn1_w_down": ("ffn1", 2),
             "ffn2_w_gate": ("ffn2", 0), "ffn2_w_up": ("ffn2", 1), "ffn2_w_down": ("ffn2", 2)}
    for n, (group, i) in where.items():
        transposed = not n.endswith("down")
        wv, mv, vv = (a[n].T if transposed else a[n] for a in (w, m, v))
        res = _sum_adamw(ex.partial[group][i], ex.received[group][i], chip, wv, mv, vv, name="adamw_" + n)
        grads[n], deltas[n], new_m[n], new_v[n] = (r.T if transposed else r for r in res)
    grads.update(_unpack_grads([_sum_chips(ex.partial[g][0], ex.received[g][0], chip, name="sum_%s_chips" % g)
                                for g in ("mix_in", "mix_misc")]))

    pieces = [_pad_cols(gs[n], SMALL_SLOTS[n]) for n in SMALL_NAMES] + [dconv_w.reshape(1, 3 * D), loss_row]
    total = _small_exchange(jnp.concatenate(pieces, axis=1).reshape(-1, 128), reduce=True, name="reduce_small").reshape(-1)
    off = 0
    for n in SMALL_NAMES:
        grads[n] = total[off:off + w[n].shape[0]]
        off += SMALL_SLOTS[n]
    conv_full = total[off:off + 3 * D].reshape(3, D)
    grads["conv_w"] = lax.dynamic_slice(conv_full, (0, me * HEAD_PAD), (3, HEAD_PAD))
    loss = total[off + 3 * D]

    for n in WEIGHT_NAMES:
        if n in deltas:
            continue
        shape = w[n].shape
        if n in COLUMN_MAJOR:
            ops = [a.T for a in (w[n], grads[n], m[n], v[n])]
            deltas[n], new_m[n], new_v[n] = (r.T for r in _adamw(*ops, name="adamw_" + n))
            continue
        if len(shape) == 1:
            view = (-1, 128) if shape[0] % 128 == 0 else (1, shape[0])
        else:
            view = shape
        dlt, nm, nv = _adamw(w[n].reshape(view), grads[n].reshape(view), m[n].reshape(view), v[n].reshape(view), name="adamw_" + n)
        deltas[n], new_m[n], new_v[n] = dlt.reshape(shape), nm.reshape(shape), nv.reshape(shape)
    return (loss, grad_x, *[grads[n] for n in WEIGHT_NAMES], *[deltas[n] for n in WEIGHT_NAMES],
            *[new_m[n] for n in WEIGHT_NAMES], *[new_v[n] for n in WEIGHT_NAMES])


def kernel(x, positions, ffn1_norm, ffn1_w_gate, ffn1_w_up, ffn1_w_down, mix_norm, w_in, gate_bias, q_a_norm, w_uq, kv_a_norm, w_uk, w_uv, q_head_norm, k_head_norm, w_proj_attn, conv_w, w_proj_conv, w_out, ffn2_norm, ffn2_w_gate, ffn2_w_up, ffn2_w_down, loss_target, m_ffn1_norm, m_ffn1_w_gate, m_ffn1_w_up, m_ffn1_w_down, m_mix_norm, m_w_in, m_gate_bias, m_q_a_norm, m_w_uq, m_kv_a_norm, m_w_uk, m_w_uv, m_q_head_norm, m_k_head_norm, m_w_proj_attn, m_conv_w, m_w_proj_conv, m_w_out, m_ffn2_norm, m_ffn2_w_gate, m_ffn2_w_up, m_ffn2_w_down, v_ffn1_norm, v_ffn1_w_gate, v_ffn1_w_up, v_ffn1_w_down, v_mix_norm, v_w_in, v_gate_bias, v_q_a_norm, v_w_uq, v_kv_a_norm, v_w_uk, v_w_uv, v_q_head_norm, v_k_head_norm, v_w_proj_attn, v_conv_w, v_w_proj_conv, v_w_out, v_ffn2_norm, v_ffn2_w_gate, v_ffn2_w_up, v_ffn2_w_down):
    given = dict(locals())
    w = {n: given[n] for n in WEIGHT_NAMES}
    m = {n: given["m_" + n] for n in WEIGHT_NAMES}
    v = {n: given["v_" + n] for n in WEIGHT_NAMES}
    return _step(x, positions, loss_target, w, m, v)
```

```python
import functools

import jax
import jax.numpy as jnp
from jax import lax
from jax.experimental import pallas as pl
from jax.experimental.pallas import tpu as pltpu

F32 = jnp.float32
BF16 = jnp.bfloat16
MESH = pl.DeviceIdType.MESH
ANY = pl.BlockSpec(memory_space=pl.ANY)

N_DEV = 8
D = 1024
DFF = 2816
N_HEADS = 8
HEAD_PAD = 128
QK_DIM = 96
NOPE = 64
ROPE_HALF = 16
Q_LORA = 384
KV_LORA = 256
LAT_PAD = 768
CONV_COLS = 3072
GATE_COLS = 2048
IN_DIM = 5792
IN_SHARD = IN_DIM // N_DEV
IN_SHARD_PAD = 736
FF_SHARD = DFF // N_DEV
ROPE_THETA = 10000.0
NORM_EPS = 1e-6
ATTN_SCALE = QK_DIM ** -0.5
NEG = -1e30

ADAM_LR, ADAM_B1, ADAM_B2, ADAM_EPS, ADAM_WD, ADAM_STEP = 0.001, 0.9, 0.999, 1e-08, 0.01, 10

PACK = ((("w_inT", IN_SHARD_PAD),), (("w_uq", 48), ("w_uk", 32), ("w_uv", 32), ("w_pa", 64), ("w_pc", 128), ("w_out", 128)))
PACK_OFF = {}
for _i, _group in enumerate(PACK):
    _o = 0
    for _n, _r in _group:
        PACK_OFF[_n] = (_i, _o, _r)
        _o += _r

VMEM_LIMIT = 56 * 1024 * 1024


def _params(*sem):
    return pltpu.CompilerParams(dimension_semantics=sem if sem else None, vmem_limit_bytes=VMEM_LIMIT)


class _Plan:
    def __init__(self, start, wait, n_remote, n_local, in_place=False):
        self.start, self.wait, self.n_remote, self.n_local, self.in_place = start, wait, n_remote, n_local, in_place

    def sems(self):
        return [pltpu.SemaphoreType.DMA((self.n_remote,)), pltpu.SemaphoreType.DMA((self.n_remote,)),
                pltpu.SemaphoreType.DMA((max(self.n_local, 1),))]


def _call(body, *, name, grid, in_specs, out_specs, out_shape, scratch_shapes, operands, sem, hosted=None):
    if hosted is None:
        outs = pl.pallas_call(body, name=name, grid=grid, in_specs=in_specs, out_specs=out_specs, out_shape=out_shape,
                              scratch_shapes=scratch_shapes, compiler_params=_params(*sem))(*operands)
        return outs, None
    plan, srcs, h_shapes = hosted
    n_in, n_out, n_scr, nh_in, nh_out = len(in_specs), len(out_specs), len(scratch_shapes), len(srcs), len(h_shapes)
    aliases = {n_in + a: n_out + a for a in range(nh_in)} if plan.in_place else {}

    def full_body(*refs):
        ins, refs = refs[:n_in], refs[n_in:]
        h_in, refs = refs[:nh_in], refs[nh_in:]
        outs, refs = refs[:n_out], refs[n_out:]
        h_out, refs = refs[:nh_out], refs[nh_out:]
        scr, sems = refs[:n_scr], refs[n_scr:]
        ids = [pl.program_id(ax) for ax in range(len(grid))]
        first = functools.reduce(jnp.logical_and, [i == 0 for i in ids])
        last = functools.reduce(jnp.logical_and, [i == g - 1 for i, g in zip(ids, grid)])

        @pl.when(first)
        def _():
            plan.start(h_in, h_out, *sems)

        body(*ins, *outs, *scr)

        @pl.when(last)
        def _():
            plan.wait(h_in, h_out, *sems)

    res = pl.pallas_call(
        full_body, name=name, grid=grid, in_specs=list(in_specs) + [ANY] * nh_in, out_specs=list(out_specs) + [ANY] * nh_out,
        out_shape=list(out_shape) + list(h_shapes), scratch_shapes=list(scratch_shapes) + plan.sems(),
        input_output_aliases=aliases, compiler_params=_params(*(["arbitrary"] * len(grid))),
    )(*operands, *srcs)
    return res[:n_out], res[n_out:]


def _dot_nn(a, b):
    return lax.dot_general(a, b, (((1,), (0,)), ((), ())), preferred_element_type=F32)


def _dot_nt(a, b):
    return lax.dot_general(a, b, (((1,), (1,)), ((), ())), preferred_element_type=F32)


def _dot_tn(a, b):
    return lax.dot_general(a, b, (((0,), (0,)), ((), ())), preferred_element_type=F32)


def _sigmoid(x):
    return 0.5 * jnp.tanh(0.5 * x) + 0.5


def _rms_stats(x):
    r = lax.rsqrt(jnp.mean(x * x, axis=-1, keepdims=True) + NORM_EPS)
    return x * r, r


ROWS_WIDE = 16
MM_ROWS = 256


def _rms_bwd(dy, xhat, r, g):
    dg = jnp.sum(dy * xhat, axis=0, keepdims=True)
    dxh = dy * g
    dx = r * (dxh - xhat * jnp.mean(dxh * xhat, axis=-1, keepdims=True))
    return dx, dg


def _mm(a, b, *, mode, out_dtype, tm, tn, tk, name, add=None, scale=1.0, hosted=None):
    if mode == "nn":
        (m, k), (_, n) = a.shape, b.shape
    elif mode == "nt":
        (m, k), (n, _) = a.shape, b.shape
    else:
        (k, m), (_, n) = a.shape, b.shape
    assert m % tm == 0 and n % tn == 0 and k % tk == 0, (name, m, n, k, tm, tn, tk)
    nk = k // tk
    dot = {"nn": _dot_nn, "nt": _dot_nt, "tn": _dot_tn}[mode]
    a_spec = pl.BlockSpec((tk, tm), lambda i, j, kk: (kk, i)) if mode == "tn" else pl.BlockSpec((tm, tk), lambda i, j, kk: (i, kk))
    b_spec = pl.BlockSpec((tn, tk), lambda i, j, kk: (j, kk)) if mode == "nt" else pl.BlockSpec((tk, tn), lambda i, j, kk: (kk, j))
    o_spec = pl.BlockSpec((tm, tn), lambda i, j, kk: (i, j))
    has_add = add is not None

    def finish(prod, c_ref, o_ref):
        if scale != 1.0:
            prod = prod * scale
        o_ref[...] = ((c_ref[...] + prod) if has_add else prod).astype(out_dtype)

    def body(*refs):
        a_ref, b_ref = refs[:2]
        c_ref = refs[2] if has_add else None
        o_ref = refs[3] if has_add else refs[2]
        if nk == 1:
            finish(dot(a_ref[...], b_ref[...]), c_ref, o_ref)
            return
        acc_ref = refs[-1]
        kk = pl.program_id(2)

        @pl.when(kk == 0)
        def _():
            acc_ref[...] = jnp.zeros_like(acc_ref)

        acc_ref[...] += dot(a_ref[...], b_ref[...])

        @pl.when(kk == nk - 1)
        def _():
            finish(acc_ref[...], c_ref, o_ref)

    operands = (a, b, add) if has_add else (a, b)
    in_specs = [a_spec, b_spec] + ([o_spec] if has_add else [])
    (out,), got = _call(
        body, name=name, grid=(m // tm, n // tn, nk), in_specs=in_specs, out_specs=[o_spec],
        out_shape=[jax.ShapeDtypeStruct((m, n), out_dtype)], scratch_shapes=[pltpu.VMEM((tm, tn), F32)] if nk > 1 else [],
        operands=operands, sem=("parallel", "parallel", "arbitrary"), hosted=hosted)
    return out if hosted is None else (out, got)


def _rms_fwd(x, g, *, tm, name, hosted=None):
    t, d = x.shape

    def body(x_ref, g_ref, h_ref):
        xhat, _ = _rms_stats(x_ref[...])
        h_ref[...] = (xhat * g_ref[...]).astype(BF16)

    (h,), got = _call(
        body, name=name, grid=(t // tm,),
        in_specs=[pl.BlockSpec((tm, d), lambda i: (i, 0)), pl.BlockSpec((1, d), lambda i: (0, 0))],
        out_specs=[pl.BlockSpec((tm, d), lambda i: (i, 0))], out_shape=[jax.ShapeDtypeStruct((t, d), BF16)], scratch_shapes=[],
        operands=(x, g), sem=("parallel",), hosted=hosted)
    return h, got


def _ffn_fwd(x, g, wgT, wuT, wd, *, tm, hc, name, hosted=None, target=None):
    t, d = x.shape
    nj = DFF // hc
    with_loss = target is not None

    def body(*refs):
        x_ref, g_ref, wg_ref, wu_ref, wd_ref = refs[:5]
        t_ref = refs[5] if with_loss else None
        xo_ref, h_ref, a_ref, b_ref = refs[5 + with_loss:9 + with_loss]
        loss_ref = refs[9 + with_loss] if with_loss else None
        acc_ref = refs[-1]
        i, j = pl.program_id(0), pl.program_id(1)

        @pl.when(j == 0)
        def _():
            xhat, _ = _rms_stats(x_ref[...])
            h_ref[...] = (xhat * g_ref[...]).astype(BF16)
            acc_ref[...] = jnp.zeros_like(acc_ref)

        h = h_ref[...]
        a = _dot_nt(h, wg_ref[...])
        b = _dot_nt(h, wu_ref[...])
        a_ref[...] = a.astype(BF16)
        b_ref[...] = b.astype(BF16)
        s = (a * _sigmoid(a) * b).astype(BF16)
        acc_ref[...] += _dot_nn(s, wd_ref[...])

        if with_loss:
            @pl.when((i == 0) & (j == 0))
            def _():
                loss_ref[...] = jnp.zeros_like(loss_ref)

        @pl.when(j == nj - 1)
        def _():
            y = x_ref[...] + 0.5 * acc_ref[...]
            if with_loss:
                err = y - t_ref[...]
                xo_ref[...] = err * (1.0 / d)
                loss_ref[...] += jnp.sum(jnp.sum(err * err, axis=-1, keepdims=True), axis=0, keepdims=True) * (0.5 / d)
            else:
                xo_ref[...] = y

    row = pl.BlockSpec((tm, d), lambda i, j: (i, 0))
    vec = pl.BlockSpec((1, d), lambda i, j: (0, 0))
    wsp = pl.BlockSpec((hc, d), lambda i, j: (j, 0))
    hid = pl.BlockSpec((tm, hc), lambda i, j: (i, j))
    out_specs = [row, row, hid, hid] + ([pl.BlockSpec((1, 128), lambda i, j: (0, 0))] if with_loss else [])
    out_shape = [jax.ShapeDtypeStruct((t, d), F32), jax.ShapeDtypeStruct((t, d), BF16), jax.ShapeDtypeStruct((t, DFF), BF16),
                 jax.ShapeDtypeStruct((t, DFF), BF16)] + ([jax.ShapeDtypeStruct((1, 128), F32)] if with_loss else [])
    return _call(
        body, name=name, grid=(t // tm, nj), in_specs=[row, vec, wsp, wsp, wsp] + ([row] if with_loss else []),
        out_specs=out_specs, out_shape=out_shape, scratch_shapes=[pltpu.VMEM((tm, d), F32)],
        operands=(x, g, wgT, wuT, wd) + ((target,) if with_loss else ()),
        sem=("arbitrary" if with_loss else "parallel", "arbitrary"), hosted=hosted)


def _ffn_grads(dout, h, a, b, wd, *, tm, hc, name, hosted=None):
    t, d = dout.shape
    ni, nj = t // tm, DFF // hc

    def body(dout_ref, h_ref, a_ref, b_ref, wd_ref, da_ref, db_ref, dwg_ref, dwu_ref, dwd_ref,
             dy_all, h_all, ds_scr, s_scr, acc_g, acc_u, acc_d):
        j, i = pl.program_id(0), pl.program_id(1)
        rows_i = pl.ds(pl.multiple_of(i * tm, tm), tm)

        @pl.when(j == 0)
        def _():
            dy_all[rows_i, :] = (0.5 * dout_ref[...]).astype(BF16)
            h_all[rows_i, :] = h_ref[...]

        @pl.when(i == 0)
        def _():
            acc_g[...] = jnp.zeros_like(acc_g)
            acc_u[...] = jnp.zeros_like(acc_u)
            acc_d[...] = jnp.zeros_like(acc_d)

        def grad_rows(rows):
            ds = ds_scr[rows, :]
            av = a_ref[rows, :].astype(F32)
            bv = b_ref[rows, :].astype(F32)
            sg = _sigmoid(av)
            sl = av * sg
            s_scr[rows, :] = (sl * bv).astype(BF16)
            da_ref[rows, :] = (ds * bv * (sg + sl * (1.0 - sg))).astype(BF16)
            db_ref[rows, :] = (ds * sl).astype(BF16)

        for blk in range(tm // MM_ROWS):
            rs = slice(blk * MM_ROWS, (blk + 1) * MM_ROWS)
            ds_scr[rs, :] = _dot_nt(dy_all[pl.ds(pl.multiple_of(i * tm + blk * MM_ROWS, MM_ROWS), MM_ROWS), :], wd_ref[...])
            for c in range(MM_ROWS // ROWS_WIDE):
                grad_rows(slice(blk * MM_ROWS + c * ROWS_WIDE, blk * MM_ROWS + (c + 1) * ROWS_WIDE))

        dy_i = dy_all[rows_i, :]
        h_i = h_all[rows_i, :]
        acc_d[...] += _dot_tn(s_scr[...], dy_i)
        acc_g[...] += _dot_tn(da_ref[...], h_i)
        acc_u[...] += _dot_tn(db_ref[...], h_i)

        @pl.when(i == ni - 1)
        def _():
            dwg_ref[...] = acc_g[...].astype(BF16)
            dwu_ref[...] = acc_u[...].astype(BF16)
            dwd_ref[...] = acc_d[...].astype(BF16)

    first = pl.BlockSpec((tm, d), lambda j, i: (jnp.where(j == 0, i, 0), 0))
    hid = pl.BlockSpec((tm, hc), lambda j, i: (i, j))
    wsp = pl.BlockSpec((hc, d), lambda j, i: (j, 0))
    hid_shape = jax.ShapeDtypeStruct((t, DFF), BF16)
    w_shape = jax.ShapeDtypeStruct((DFF, d), BF16)
    return _call(
        body, name=name, grid=(nj, ni), in_specs=[first, first, hid, hid, wsp], out_specs=[hid, hid, wsp, wsp, wsp],
        out_shape=[hid_shape, hid_shape, w_shape, w_shape, w_shape],
        scratch_shapes=[pltpu.VMEM((t, d), BF16), pltpu.VMEM((t, d), BF16), pltpu.VMEM((tm, hc), F32), pltpu.VMEM((tm, hc), BF16),
                        pltpu.VMEM((hc, d), F32), pltpu.VMEM((hc, d), F32), pltpu.VMEM((hc, d), F32)],
        operands=(dout, h, a, b, wd), sem=("arbitrary", "arbitrary"), hosted=hosted)


def _proj_bwd(dlat, dconv3, dgl, latT, convT, gateT, x, g, dres, *, tm, name, hosted=None):
    t, d = x.shape

    def body(dl_ref, dc_ref, dg_ref, wl_ref, wc_ref, wg_ref, x_ref, g_ref, dres_ref, dx_ref, dgain_ref):
        @pl.when(pl.program_id(0) == 0)
        def _():
            dgain_ref[...] = jnp.zeros_like(dgain_ref)

        dh = _dot_nn(dl_ref[...], wl_ref[...]) + _dot_nn(dc_ref[...], wc_ref[...]) + _dot_nn(dg_ref[...], wg_ref[...])
        xhat, r = _rms_stats(x_ref[...])
        dx, dgain = _rms_bwd(dh, xhat, r, g_ref[...])
        dx_ref[...] = dres_ref[...] + dx
        dgain_ref[...] += dgain

    def rows(w):
        return pl.BlockSpec((tm, w), lambda i: (i, 0))

    def full(r):
        return pl.BlockSpec((r, d), lambda i: (0, 0))

    return _call(
        body, name=name, grid=(t // tm,),
        in_specs=[rows(LAT_PAD), rows(CONV_COLS), rows(GATE_COLS), full(LAT_PAD), full(CONV_COLS), full(GATE_COLS), rows(d), full(1), rows(d)],
        out_specs=[rows(d), full(1)], out_shape=[jax.ShapeDtypeStruct((t, d), F32), jax.ShapeDtypeStruct((1, d), F32)],
        scratch_shapes=[], operands=(dlat, dconv3, dgl, latT, convT, gateT, x, g, dres), sem=("arbitrary",), hosted=hosted)


def _ffn_up_bwd(da, db, wgT, wuT, x, g, dout, *, tm, name, hosted=None):
    t, d = x.shape

    def body(da_ref, db_ref, wg_ref, wu_ref, x_ref, g_ref, dout_ref, dx_ref, dg_ref):
        @pl.when(pl.program_id(0) == 0)
        def _():
            dg_ref[...] = jnp.zeros_like(dg_ref)

        dh = _dot_nn(da_ref[...], wg_ref[...]) + _dot_nn(db_ref[...], wu_ref[...])
        xhat, r = _rms_stats(x_ref[...])
        dx, dg = _rms_bwd(dh, xhat, r, g_ref[...])
        dx_ref[...] = dout_ref[...] + dx
        dg_ref[...] += dg

    row = pl.BlockSpec((tm, d), lambda i: (i, 0))
    vec = pl.BlockSpec((1, d), lambda i: (0, 0))
    hid = pl.BlockSpec((tm, DFF), lambda i: (i, 0))
    wsp = pl.BlockSpec((DFF, d), lambda i: (0, 0))
    return _call(
        body, name=name, grid=(t // tm,), in_specs=[hid, hid, wsp, wsp, row, vec, row], out_specs=[row, vec],
        out_shape=[jax.ShapeDtypeStruct((t, d), F32), jax.ShapeDtypeStruct((1, d), F32)], scratch_shapes=[],
        operands=(da, db, wgT, wuT, x, g, dout), sem=("arbitrary",), hosted=hosted)


def _rope_fwd(x, c, s1, s2):
    return x * c + pltpu.roll(x, HEAD_PAD - ROPE_HALF, 1) * s1 + pltpu.roll(x, ROPE_HALF, 1) * s2


def _rope_bwd(dy, c, s1, s2):
    return dy * c + pltpu.roll(dy * s1, ROPE_HALF, 1) + pltpu.roll(dy * s2, HEAD_PAD - ROPE_HALF, 1)


def _head_stats(x):
    r = lax.rsqrt(jnp.sum(x * x, axis=-1, keepdims=True) * (1.0 / QK_DIM) + NORM_EPS)
    return x * r, r


def _mla_prep_fwd(lat, gq, gkv, ghq, ghk, wq, wk, wv, rc, rs1, rs2, *, tm, name):
    t = lat.shape[0]

    def body(lat_ref, gq_ref, gkv_ref, ghq_ref, ghk_ref, wq_ref, wk_ref, wv_ref, c_ref, s1_ref, s2_ref,
             q_ref, k_ref, v_ref, qn_ref, ckv_ref):
        lat_v = lat_ref[...]
        qhat, _ = _rms_stats(lat_v[:, :Q_LORA].astype(F32))
        qn = (qhat * gq_ref[...]).astype(BF16)
        khat, _ = _rms_stats(lat_v[:, Q_LORA:Q_LORA + KV_LORA].astype(F32))
        ckv = (khat * gkv_ref[...]).astype(BF16)
        ckv_ext = jnp.concatenate([ckv, lat_v[:, Q_LORA + KV_LORA:]], axis=1)
        qn_ref[...] = qn
        ckv_ref[...] = ckv_ext
        q_pre = _dot_nn(qn, wq_ref[...])
        k_pre = _dot_nn(ckv_ext, wk_ref[...])
        v_ref[...] = _dot_nn(ckv, wv_ref[...]).astype(BF16)
        c, s1, s2 = c_ref[...], s1_ref[...], s2_ref[...]
        for h in range(N_HEADS):
            hs = slice(h * HEAD_PAD, (h + 1) * HEAD_PAD)
            xq, _ = _head_stats(q_pre[:, hs])
            q_ref[:, hs] = _rope_fwd(xq * ghq_ref[...], c, s1, s2).astype(BF16)
            xk, _ = _head_stats(k_pre[:, hs])
            k_ref[:, hs] = _rope_fwd(xk * ghk_ref[...], c, s1, s2).astype(BF16)

    def row(w):
        return pl.BlockSpec((tm, w), lambda i: (i, 0))

    def full(r, w):
        return pl.BlockSpec((r, w), lambda i: (0, 0))

    wide = jax.ShapeDtypeStruct((t, D), BF16)
    lat3 = jax.ShapeDtypeStruct((t, Q_LORA), BF16)
    return pl.pallas_call(
        body, name=name, grid=(t // tm,),
        in_specs=[row(LAT_PAD), full(1, Q_LORA), full(1, KV_LORA), full(1, HEAD_PAD), full(1, HEAD_PAD),
                  full(Q_LORA, D), full(Q_LORA, D), full(KV_LORA, D), row(HEAD_PAD), row(HEAD_PAD), row(HEAD_PAD)],
        out_specs=[row(D), row(D), row(D), row(Q_LORA), row(Q_LORA)],
        out_shape=[wide, wide, wide, lat3, lat3],
        compiler_params=_params("parallel"),
    )(lat, gq, gkv, ghq, ghk, wq, wk, wv, rc, rs1, rs2)


def _mla_prep_bwd(dq, dk, dv, lat, qn, ckv_ext, gq, gkv, ghq, ghk, wq, wk, wv, rc, rs1, rs2, *, tm, name):
    t = lat.shape[0]

    def body(dq_ref, dk_ref, dv_ref, lat_ref, qn_ref, ckv_ref, gq_ref, gkv_ref, ghq_ref, ghk_ref, wq_ref, wk_ref, wv_ref,
             c_ref, s1_ref, s2_ref, dlat_ref, dqp_ref, dkp_ref, dgq_ref, dgkv_ref, dghq_ref, dghk_ref):
        @pl.when(pl.program_id(0) == 0)
        def _():
            dgq_ref[...] = jnp.zeros_like(dgq_ref)
            dgkv_ref[...] = jnp.zeros_like(dgkv_ref)
            dghq_ref[...] = jnp.zeros_like(dghq_ref)
            dghk_ref[...] = jnp.zeros_like(dghk_ref)

        c, s1, s2 = c_ref[...], s1_ref[...], s2_ref[...]
        q_pre = _dot_nn(qn_ref[...], wq_ref[...])
        k_pre = _dot_nn(ckv_ref[...], wk_ref[...])

        def heads(pre, dy_ref, gh_ref, dgh_ref, out_ref):
            dgh = jnp.zeros((1, HEAD_PAD), F32)
            for h in range(N_HEADS):
                hs = slice(h * HEAD_PAD, (h + 1) * HEAD_PAD)
                d = _rope_bwd(dy_ref[:, hs].astype(F32), c, s1, s2)
                xhat, r = _head_stats(pre[:, hs])
                dgh = dgh + jnp.sum(d * xhat, axis=0, keepdims=True)
                dxh = d * gh_ref[...]
                dx = r * (dxh - xhat * (jnp.sum(dxh * xhat, axis=-1, keepdims=True) * (1.0 / QK_DIM)))
                out_ref[:, hs] = dx.astype(BF16)
            dgh_ref[...] += dgh

        heads(q_pre, dq_ref, ghq_ref, dghq_ref, dqp_ref)
        heads(k_pre, dk_ref, ghk_ref, dghk_ref, dkp_ref)
        dqn = _dot_nt(dqp_ref[...], wq_ref[...])
        dce = _dot_nt(dkp_ref[...], wk_ref[...])
        dckv = dce[:, :KV_LORA] + _dot_nt(dv_ref[...], wv_ref[...])
        lat_v = lat_ref[...]
        qhat, rq = _rms_stats(lat_v[:, :Q_LORA].astype(F32))
        dql, dgq = _rms_bwd(dqn, qhat, rq, gq_ref[...])
        khat, rk = _rms_stats(lat_v[:, Q_LORA:Q_LORA + KV_LORA].astype(F32))
        dkl, dgkv = _rms_bwd(dckv, khat, rk, gkv_ref[...])
        dgq_ref[...] += dgq
        dgkv_ref[...] += dgkv
        dlat_ref[...] = jnp.concatenate([dql, dkl, dce[:, KV_LORA:]], axis=1).astype(BF16)

    def row(w):
        return pl.BlockSpec((tm, w), lambda i: (i, 0))

    def full(r, w):
        return pl.BlockSpec((r, w), lambda i: (0, 0))

    return pl.pallas_call(
        body, name=name, grid=(t // tm,),
        in_specs=[row(D), row(D), row(D), row(LAT_PAD), row(Q_LORA), row(Q_LORA), full(1, Q_LORA), full(1, KV_LORA),
                  full(1, HEAD_PAD), full(1, HEAD_PAD), full(Q_LORA, D), full(Q_LORA, D), full(KV_LORA, D),
                  row(HEAD_PAD), row(HEAD_PAD), row(HEAD_PAD)],
        out_specs=[row(LAT_PAD), row(D), row(D), full(1, Q_LORA), full(1, KV_LORA), full(1, HEAD_PAD), full(1, HEAD_PAD)],
        out_shape=[jax.ShapeDtypeStruct((t, LAT_PAD), BF16), jax.ShapeDtypeStruct((t, D), BF16), jax.ShapeDtypeStruct((t, D), BF16),
                   jax.ShapeDtypeStruct((1, Q_LORA), F32), jax.ShapeDtypeStruct((1, KV_LORA), F32),
                   jax.ShapeDtypeStruct((1, HEAD_PAD), F32), jax.ShapeDtypeStruct((1, HEAD_PAD), F32)],
        compiler_params=_params("arbitrary"),
    )(dq, dk, dv, lat, qn, ckv_ext, gq, gkv, ghq, ghk, wq, wk, wv, rc, rs1, rs2)


def _causal_keep(tq):
    r = lax.broadcasted_iota(jnp.int32, (tq, tq), 0)
    c = lax.broadcasted_iota(jnp.int32, (tq, tq), 1)
    return c <= r


def _flash_fwd(q, k, v, *, n_seq, seq, tq, name, hosted=None):
    nq = seq // tq

    def body(q_ref, k_ref, v_ref, o_ref, lse_ref):
        qi = pl.program_id(2)
        qv = q_ref[...]

        def step(j, carry, masked):
            m, l, acc = carry
            kj = k_ref[pl.ds(pl.multiple_of(j * tq, tq), tq), :]
            vj = v_ref[pl.ds(pl.multiple_of(j * tq, tq), tq), :]
            s = _dot_nt(qv, kj) * ATTN_SCALE
            if masked:
                s = jnp.where(_causal_keep(tq), s, NEG)
            m_new = jnp.maximum(m, jnp.max(s, axis=-1, keepdims=True))
            alpha = jnp.exp(m - m_new)
            p = jnp.exp(s - m_new)
            l = alpha * l + jnp.sum(p, axis=-1, keepdims=True)
            acc = alpha * acc + _dot_nn(p.astype(BF16), vj)
            return m_new, l, acc

        init = (jnp.full((tq, 1), NEG, F32), jnp.zeros((tq, 1), F32), jnp.zeros((tq, HEAD_PAD), F32))
        carry = lax.fori_loop(0, qi, lambda j, cr: step(j, cr, False), init)
        m, l, acc = step(qi, carry, True)
        o_ref[...] = (acc / l).astype(BF16)
        lse_ref[...] = jnp.broadcast_to(m + jnp.log(l), (tq, HEAD_PAD))

    qspec = pl.BlockSpec((tq, HEAD_PAD), lambda b, h, i: (b * nq + i, h))
    kspec = pl.BlockSpec((seq, HEAD_PAD), lambda b, h, i: (b, h))
    t = n_seq * seq
    return _call(
        body, name=name, grid=(n_seq, N_HEADS, nq), in_specs=[qspec, kspec, kspec], out_specs=[qspec, qspec],
        out_shape=[jax.ShapeDtypeStruct((t, D), BF16), jax.ShapeDtypeStruct((t, D), F32)], scratch_shapes=[],
        operands=(q, k, v), sem=("parallel", "parallel", "arbitrary"), hosted=hosted)


def _flash_bwd(q, k, v, o, lse, do, *, n_seq, seq, tq, name, hosted=None):
    nq = seq // tq

    def body(q_ref, k_ref, v_ref, o_ref, lse_ref, do_ref, dq_ref, dk_ref, dv_ref, dk_acc, dv_acc):
        j = pl.program_id(2)

        @pl.when(j == 0)
        def _():
            dq_ref[...] = jnp.zeros_like(dq_ref)

        dk_acc[...] = jnp.zeros_like(dk_acc)
        dv_acc[...] = jnp.zeros_like(dv_acc)
        kv = k_ref[...]
        vv = v_ref[...]

        def step(i, masked):
            rows = pl.ds(pl.multiple_of(i * tq, tq), tq)
            qi = q_ref[rows, :]
            doi = do_ref[rows, :]
            delta = jnp.sum(doi.astype(F32) * o_ref[rows, :].astype(F32), axis=-1, keepdims=True)
            s = _dot_nt(qi, kv) * ATTN_SCALE
            p = jnp.exp(s - lse_ref[rows, :][:, :1])
            if masked:
                p = jnp.where(_causal_keep(tq), p, 0.0)
            dv_acc[...] += _dot_tn(p.astype(BF16), doi)
            dp = _dot_nt(doi, vv)
            ds = (p * (dp - delta) * ATTN_SCALE).astype(BF16)
            dk_acc[...] += _dot_tn(ds, qi)
            dq_ref[rows, :] += _dot_nn(ds, kv)

        step(j, True)

        def loop_body(i, carry):
            step(i, False)
            return carry

        lax.fori_loop(j + 1, nq, loop_body, 0)
        dk_ref[...] = dk_acc[...]
        dv_ref[...] = dv_acc[...].astype(BF16)

    full = pl.BlockSpec((seq, HEAD_PAD), lambda b, h, j: (b, h))
    tile = pl.BlockSpec((tq, HEAD_PAD), lambda b, h, j: (b * nq + j, h))
    t = n_seq * seq
    return _call(
        body, name=name, grid=(n_seq, N_HEADS, nq), in_specs=[full, tile, tile, full, full, full],
        out_specs=[full, tile, tile],
        out_shape=[jax.ShapeDtypeStruct((t, D), F32), jax.ShapeDtypeStruct((t, D), F32), jax.ShapeDtypeStruct((t, D), BF16)],
        scratch_shapes=[pltpu.VMEM((tq, HEAD_PAD), F32), pltpu.VMEM((tq, HEAD_PAD), F32)],
        operands=(q, k, v, o, lse, do), sem=("parallel", "parallel", "arbitrary"), hosted=hosted)


CONV_CB = 256


def _shift_down(u, k, row):
    return jnp.where(row >= k, pltpu.roll(u, k, 0), 0.0)


def _shift_up(u, k, row, n):
    return jnp.where(row < n - k, pltpu.roll(u, n - k, 0), 0.0)


def _conv_fwd(conv3, cw, *, n_seq, seq, name):
    def body(c_ref, w_ref, p_ref):
        blk = c_ref[...].astype(F32)
        xc, gb, gc = blk[:, :CONV_CB], blk[:, CONV_CB:2 * CONV_CB], blk[:, 2 * CONV_CB:]
        row = lax.broadcasted_iota(jnp.int32, (seq, CONV_CB), 0)
        u = gc * xc
        z = w_ref[0:1, :] * _shift_down(u, 2, row) + w_ref[1:2, :] * _shift_down(u, 1, row) + w_ref[2:3, :] * u
        p_ref[...] = (gb * z).astype(BF16)

    return pl.pallas_call(
        body, name=name, grid=(n_seq, D // CONV_CB),
        in_specs=[pl.BlockSpec((seq, 3 * CONV_CB), lambda b, j: (b, j)), pl.BlockSpec((3, CONV_CB), lambda b, j: (0, j))],
        out_specs=pl.BlockSpec((seq, CONV_CB), lambda b, j: (b, j)),
        out_shape=jax.ShapeDtypeStruct((n_seq * seq, D), BF16),
        compiler_params=_params("parallel", "parallel"),
    )(conv3, cw)


def _conv_bwd(dp, conv3, cw, *, n_seq, seq, name):
    def body(dp_ref, c_ref, w_ref, dc_ref, dw_ref):
        @pl.when(pl.program_id(1) == 0)
        def _():
            dw_ref[...] = jnp.zeros_like(dw_ref)

        blk = c_ref[...].astype(F32)
        xc, gb, gc = blk[:, :CONV_CB], blk[:, CONV_CB:2 * CONV_CB], blk[:, 2 * CONV_CB:]
        row = lax.broadcasted_iota(jnp.int32, (seq, CONV_CB), 0)
        w0, w1, w2 = w_ref[0:1, :], w_ref[1:2, :], w_ref[2:3, :]
        u = gc * xc
        u1 = _shift_down(u, 1, row)
        u2 = _shift_down(u, 2, row)
        z = w0 * u2 + w1 * u1 + w2 * u
        dpv = dp_ref[...].astype(F32)
        dz = dpv * gb
        du = w2 * dz + w1 * _shift_up(dz, 1, row, seq) + w0 * _shift_up(dz, 2, row, seq)
        dc_ref[...] = jnp.concatenate([du * gc, dpv * z, du * xc], axis=1).astype(BF16)
        dw_ref[0:1, :] += jnp.sum(dz * u2, axis=0, keepdims=True)
        dw_ref[1:2, :] += jnp.sum(dz * u1, axis=0, keepdims=True)
        dw_ref[2:3, :] += jnp.sum(dz * u, axis=0, keepdims=True)

    return pl.pallas_call(
        body, name=name, grid=(D // CONV_CB, n_seq),
        in_specs=[pl.BlockSpec((seq, CONV_CB), lambda j, b: (b, j)), pl.BlockSpec((seq, 3 * CONV_CB), lambda j, b: (b, j)),
                  pl.BlockSpec((3, CONV_CB), lambda j, b: (0, j))],
        out_specs=[pl.BlockSpec((seq, 3 * CONV_CB), lambda j, b: (b, j)), pl.BlockSpec((3, CONV_CB), lambda j, b: (0, j))],
        out_shape=[jax.ShapeDtypeStruct((n_seq * seq, CONV_COLS), BF16), jax.ShapeDtypeStruct((3, D), F32)],
        compiler_params=_params("parallel", "arbitrary"),
    )(dp, conv3, cw)


def _merge_fwd(o, p, gl, bias, x1, wpa, wpc, wout, *, tm, name, hosted=None):
    t = x1.shape[0]

    def body(o_ref, p_ref, gl_ref, b_ref, x_ref, wpa_ref, wpc_ref, wout_ref, x2_ref, mg_ref, ya_ref, yb_ref):
        ya = _dot_nn(o_ref[...], wpa_ref[...])
        yb = _dot_nn(p_ref[...], wpc_ref[...])
        gates = _sigmoid(gl_ref[...].astype(F32) + b_ref[...])
        merged = (gates[:, :D] * ya + gates[:, D:] * yb).astype(BF16)
        ya_ref[...] = ya.astype(BF16)
        yb_ref[...] = yb.astype(BF16)
        mg_ref[...] = merged
        x2_ref[...] = x_ref[...] + _dot_nn(merged, wout_ref[...])

    row = pl.BlockSpec((tm, D), lambda i: (i, 0))
    row2 = pl.BlockSpec((tm, GATE_COLS), lambda i: (i, 0))
    wsp = pl.BlockSpec((D, D), lambda i: (0, 0))
    wide = jax.ShapeDtypeStruct((t, D), BF16)
    return _call(
        body, name=name, grid=(t // tm,),
        in_specs=[row, row, row2, pl.BlockSpec((1, GATE_COLS), lambda i: (0, 0)), row, wsp, wsp, wsp],
        out_specs=[row, row, row, row], out_shape=[jax.ShapeDtypeStruct((t, D), F32), wide, wide, wide], scratch_shapes=[],
        operands=(o, p, gl, bias, x1, wpa, wpc, wout), sem=("parallel",), hosted=hosted)


def _merge_bwd(dx2, ya, yb, gl, bias, wpa, wpc, wout, *, tm, name, hosted=None):
    t = dx2.shape[0]

    def body(dx_ref, ya_ref, yb_ref, gl_ref, b_ref, wpa_ref, wpc_ref, wout_ref,
             dxb_ref, dya_ref, dyb_ref, dgl_ref, do_ref, dp_ref, db_ref):
        @pl.when(pl.program_id(0) == 0)
        def _():
            db_ref[...] = jnp.zeros_like(db_ref)

        dxb = dx_ref[...].astype(BF16)
        dxb_ref[...] = dxb
        dm = _dot_nt(dxb, wout_ref[...])
        gates = _sigmoid(gl_ref[...].astype(F32) + b_ref[...])
        ga, gb = gates[:, :D], gates[:, D:]
        dya = (dm * ga).astype(BF16)
        dyb = (dm * gb).astype(BF16)
        dya_ref[...] = dya
        dyb_ref[...] = dyb
        dgl = jnp.concatenate([dm * ya_ref[...].astype(F32) * ga * (1.0 - ga),
                               dm * yb_ref[...].astype(F32) * gb * (1.0 - gb)], axis=1)
        dgl_ref[...] = dgl.astype(BF16)
        db_ref[...] += jnp.sum(dgl, axis=0, keepdims=True)
        do_ref[...] = _dot_nt(dya, wpa_ref[...]).astype(BF16)
        dp_ref[...] = _dot_nt(dyb, wpc_ref[...]).astype(BF16)

    row = pl.BlockSpec((tm, D), lambda i: (i, 0))
    row2 = pl.BlockSpec((tm, GATE_COLS), lambda i: (i, 0))
    vec2 = pl.BlockSpec((1, GATE_COLS), lambda i: (0, 0))
    wsp = pl.BlockSpec((D, D), lambda i: (0, 0))
    wide = jax.ShapeDtypeStruct((t, D), BF16)
    return _call(
        body, name=name, grid=(t // tm,), in_specs=[row, row, row, row2, vec2, wsp, wsp, wsp],
        out_specs=[row, row, row, row2, row, row, vec2],
        out_shape=[wide, wide, wide, jax.ShapeDtypeStruct((t, GATE_COLS), BF16), wide, wide,
                   jax.ShapeDtypeStruct((1, GATE_COLS), F32)],
        scratch_shapes=[], operands=(dx2, ya, yb, gl, bias, wpa, wpc, wout), sem=("arbitrary",), hosted=hosted)


def _adamw(w, g, m, v, *, name):
    rows, cols = w.shape
    tr = max([c for c in range(8, 513, 8) if rows % c == 0], default=rows)
    c1 = 1.0 / (1.0 - ADAM_B1 ** ADAM_STEP)
    c2 = 1.0 / (1.0 - ADAM_B2 ** ADAM_STEP)

    def body(w_ref, g_ref, m_ref, v_ref, d_ref, nm_ref, nv_ref):
        gv = g_ref[...]
        nm = ADAM_B1 * m_ref[...] + (1.0 - ADAM_B1) * gv
        nv = ADAM_B2 * v_ref[...] + (1.0 - ADAM_B2) * (gv * gv)
        nm_ref[...] = nm
        nv_ref[...] = nv
        d_ref[...] = -ADAM_LR * ((nm * c1) / (jnp.sqrt(nv * c2) + ADAM_EPS) + ADAM_WD * w_ref[...])

    spec = pl.BlockSpec((tr, cols), lambda i: (i, 0))
    shp = jax.ShapeDtypeStruct((rows, cols), F32)
    return pl.pallas_call(
        body, name=name, grid=(rows // tr,), in_specs=[spec] * 4, out_specs=[spec] * 3, out_shape=[shp] * 3,
        compiler_params=_params("parallel"),
    )(w, g, m, v)


def _place():
    return lax.axis_index("x"), lax.axis_index("y"), lax.axis_index("c")


def _other_chips(x, y):
    return [(1 - x, y), (x, 1 - y), (1 - x, 1 - y)]


def _remote(src, dst, send, recv, dev):
    return pltpu.make_async_remote_copy(src_ref=src, dst_ref=dst, send_sem=send, recv_sem=recv, device_id=dev, device_id_type=MESH)


def _gather_chips_plan(n):
    def start(srcs, dsts, send, recv, local):
        x, y, cc = _place()
        me = 4 * x + 2 * y + cc
        for a in range(n):
            pltpu.make_async_copy(srcs[a], dsts[a].at[me], local.at[a]).start()
            for k, (px, py) in enumerate(_other_chips(x, y)):
                _remote(srcs[a], dsts[a].at[me], send.at[3 * a + k], recv.at[3 * a + k], (px, py, cc)).start()

    def wait(srcs, dsts, send, recv, local):
        x, y, cc = _place()
        me = 4 * x + 2 * y + cc
        for a in range(n):
            for k, (px, py) in enumerate(_other_chips(x, y)):
                _remote(srcs[a], dsts[a].at[4 * px + 2 * py + cc], send.at[3 * a + k], recv.at[3 * a + k], (px, py, cc)).wait_recv()
        for a in range(n):
            for k, (px, py) in enumerate(_other_chips(x, y)):
                _remote(srcs[a], dsts[a].at[me], send.at[3 * a + k], recv.at[3 * a + k], (px, py, cc)).wait_send()
            pltpu.make_async_copy(srcs[a], dsts[a].at[me], local.at[a]).wait()

    return _Plan(start, wait, 3 * n, n)


def _scatter_chips_plan(n):
    def start(srcs, dsts, send, recv, local):
        x, y, cc = _place()
        for a in range(n):
            for k, (px, py) in enumerate(_other_chips(x, y)):
                _remote(srcs[a].at[2 * px + py], dsts[a].at[k], send.at[3 * a + k], recv.at[3 * a + k], (px, py, cc)).start()

    def wait(srcs, dsts, send, recv, local):
        x, y, cc = _place()
        for a in range(n):
            for k, (px, py) in enumerate(_other_chips(x, y)):
                _remote(srcs[a].at[k], dsts[a].at[k], send.at[3 * a + k], recv.at[3 * a + k], (px, py, cc)).wait_recv()
        for a in range(n):
            for k, (px, py) in enumerate(_other_chips(x, y)):
                _remote(srcs[a].at[k], dsts[a].at[k], send.at[3 * a + k], recv.at[3 * a + k], (px, py, cc)).wait_send()

    return _Plan(start, wait, 3 * n, 0)


def _gather_shapes(blocks):
    return [jax.ShapeDtypeStruct((N_DEV,) + b.shape, b.dtype) for b in blocks]


def _scatter_shapes(parts):
    return [jax.ShapeDtypeStruct((3,) + p.shape[1:], p.dtype) for p in parts]


def _gather_sibling_plan(n):
    def start(srcs, dsts, send, recv, local):
        x, y, cc = _place()
        for a in range(n):
            for q in range(4):
                _remote(srcs[a].at[2 * q + cc], dsts[a].at[2 * q + cc], send.at[4 * a + q], recv.at[4 * a + q], (x, y, 1 - cc)).start()

    def wait(srcs, dsts, send, recv, local):
        x, y, cc = _place()
        for a in range(n):
            for q in range(4):
                _remote(srcs[a].at[2 * q + cc], dsts[a].at[2 * q + 1 - cc], send.at[4 * a + q], recv.at[4 * a + q],
                        (x, y, 1 - cc)).wait_recv()
        for a in range(n):
            for q in range(4):
                _remote(srcs[a].at[2 * q + cc], dsts[a].at[2 * q + cc], send.at[4 * a + q], recv.at[4 * a + q],
                        (x, y, 1 - cc)).wait_send()

    return _Plan(start, wait, 4 * n, 0, in_place=True)


def _scatter_sibling_plan(n):
    def start(srcs, dsts, send, recv, local):
        x, y, cc = _place()
        for a in range(n):
            for q in range(4):
                _remote(srcs[a].at[2 * q + 1 - cc], dsts[a].at[q], send.at[4 * a + q], recv.at[4 * a + q], (x, y, 1 - cc)).start()

    def wait(srcs, dsts, send, recv, local):
        x, y, cc = _place()
        for a in range(n):
            for q in range(4):
                _remote(srcs[a].at[q], dsts[a].at[q], send.at[4 * a + q], recv.at[4 * a + q], (x, y, 1 - cc)).wait_recv()
        for a in range(n):
            for q in range(4):
                _remote(srcs[a].at[q], dsts[a].at[q], send.at[4 * a + q], recv.at[4 * a + q], (x, y, 1 - cc)).wait_send()

    return _Plan(start, wait, 4 * n, 0)


def _same_shapes(arrs):
    return [jax.ShapeDtypeStruct(a.shape, a.dtype) for a in arrs]


def _halved_shapes(parts):
    return [jax.ShapeDtypeStruct((4,) + p.shape[1:], p.dtype) for p in parts]


def _run_plan(plan, srcs, out_shapes, *, name):
    n_in, n_out = len(srcs), len(out_shapes)

    def body(*refs):
        h_in, h_out, sems = refs[:n_in], refs[n_in:n_in + n_out], refs[n_in + n_out:]
        plan.start(h_in, h_out, *sems)
        plan.wait(h_in, h_out, *sems)

    return pl.pallas_call(body, name=name, in_specs=[ANY] * n_in, out_specs=[ANY] * n_out, out_shape=list(out_shapes),
                          input_output_aliases={a: a for a in range(n_in)} if plan.in_place else {},
                          scratch_shapes=plan.sems())(*srcs)


def _sum_sibling(p, q, core, *, name):
    _, r, c = p.shape

    def body(core_ref, p_ref, q_ref, o_ref):
        o_ref[...] = (p_ref[...].astype(F32) + q_ref[...].astype(F32)).astype(BF16)

    grid_spec = pltpu.PrefetchScalarGridSpec(
        num_scalar_prefetch=1, grid=(4,),
        in_specs=[pl.BlockSpec((1, r, c), lambda ch, core_ref: (2 * ch + core_ref[0], 0, 0)),
                  pl.BlockSpec((1, r, c), lambda ch, core_ref: (ch, 0, 0))],
        out_specs=pl.BlockSpec((1, r, c), lambda ch, core_ref: (ch, 0, 0)))
    return pl.pallas_call(
        body, name=name, grid_spec=grid_spec, out_shape=jax.ShapeDtypeStruct((4, r, c), BF16),
        compiler_params=_params("parallel"),
    )(core, p, q)


def _sum_chips(s1, r2, chip, *, name):
    _, r, c = s1.shape

    def body(chip_ref, s_ref, r_ref, o_ref):
        acc = s_ref[0].astype(F32)
        for k in range(3):
            acc = acc + r_ref[k].astype(F32)
        o_ref[...] = acc

    grid_spec = pltpu.PrefetchScalarGridSpec(
        num_scalar_prefetch=1, grid=(1,),
        in_specs=[pl.BlockSpec((1, r, c), lambda i, chip_ref: (chip_ref[0], 0, 0)),
                  pl.BlockSpec((3, r, c), lambda i, chip_ref: (0, 0, 0))],
        out_specs=pl.BlockSpec((r, c), lambda i, chip_ref: (0, 0)))
    return pl.pallas_call(
        body, name=name, grid_spec=grid_spec, out_shape=jax.ShapeDtypeStruct((r, c), F32),
        compiler_params=_params("arbitrary"),
    )(chip, s1, r2)


def _sum_adamw(s1, r2, chip, w, m, v, *, name):
    _, r, c = s1.shape
    c1 = 1.0 / (1.0 - ADAM_B1 ** ADAM_STEP)
    c2 = 1.0 / (1.0 - ADAM_B2 ** ADAM_STEP)

    def body(chip_ref, s_ref, r_ref, w_ref, m_ref, v_ref, g_ref, d_ref, nm_ref, nv_ref):
        gv = s_ref[0].astype(F32)
        for k in range(3):
            gv = gv + r_ref[k].astype(F32)
        g_ref[...] = gv
        nm = ADAM_B1 * m_ref[...] + (1.0 - ADAM_B1) * gv
        nv = ADAM_B2 * v_ref[...] + (1.0 - ADAM_B2) * (gv * gv)
        nm_ref[...] = nm
        nv_ref[...] = nv
        d_ref[...] = -ADAM_LR * ((nm * c1) / (jnp.sqrt(nv * c2) + ADAM_EPS) + ADAM_WD * w_ref[...])

    flat = pl.BlockSpec((r, c), lambda i, chip_ref: (0, 0))
    grid_spec = pltpu.PrefetchScalarGridSpec(
        num_scalar_prefetch=1, grid=(1,),
        in_specs=[pl.BlockSpec((1, r, c), lambda i, chip_ref: (chip_ref[0], 0, 0)),
                  pl.BlockSpec((3, r, c), lambda i, chip_ref: (0, 0, 0)), flat, flat, flat],
        out_specs=[flat] * 4)
    return pl.pallas_call(
        body, name=name, grid_spec=grid_spec, out_shape=[jax.ShapeDtypeStruct((r, c), F32)] * 4,
        compiler_params=_params("arbitrary"),
    )(chip, s1, r2, w, m, v)


def _small_exchange(v, *, reduce, name):
    r, c = v.shape

    def body(x_ref, o_ref, *rest):
        if reduce:
            buf_ref, send_sems, recv_sems = rest
        else:
            buf_ref = o_ref
            send_sems, recv_sems = rest
        x, y, cc = _place()
        me = 4 * x + 2 * y + cc

        def peer(k):
            return ((1 - x) if k & 4 else x, (1 - y) if k & 2 else y, (1 - cc) if k & 1 else cc)

        buf_ref[me] = x_ref[...]
        sends = []
        for k in range(1, N_DEV):
            cp = pltpu.make_async_remote_copy(src_ref=x_ref, dst_ref=buf_ref.at[me], send_sem=send_sems.at[k - 1],
                                              recv_sem=recv_sems.at[k - 1], device_id=peer(k), device_id_type=MESH)
            cp.start()
            sends.append(cp)
        for k in range(1, N_DEV):
            px, py, pc = peer(k)
            pltpu.make_async_remote_copy(src_ref=x_ref, dst_ref=buf_ref.at[4 * px + 2 * py + pc], send_sem=send_sems.at[k - 1],
                                         recv_sem=recv_sems.at[k - 1], device_id=peer(k), device_id_type=MESH).wait_recv()
        for cp in sends:
            cp.wait_send()
        if reduce:
            acc = buf_ref[0]
            for s in range(1, N_DEV):
                acc = acc + buf_ref[s]
            o_ref[...] = acc

    vm = pl.BlockSpec(memory_space=pltpu.VMEM)
    sems = [pltpu.SemaphoreType.DMA((N_DEV - 1,)), pltpu.SemaphoreType.DMA((N_DEV - 1,))]
    if reduce:
        out_shape, scratch = jax.ShapeDtypeStruct((r, c), F32), [pltpu.VMEM((N_DEV, r, c), F32)] + sems
    else:
        out_shape, scratch = jax.ShapeDtypeStruct((N_DEV, r, c), F32), sems
    return pl.pallas_call(body, name=name, in_specs=[vm], out_specs=vm, out_shape=out_shape, scratch_shapes=scratch)(v)


def _rows(a):
    return a.reshape(-1, D)


def _pad_cols(a, to):
    return jnp.pad(a, ((0, 0), (0, to - a.shape[1])))


def _pack_weights(w):
    parts = {
        "w_inT": jnp.pad(w["w_in"].T, ((0, IN_SHARD_PAD - IN_SHARD), (0, 0))),
        "w_uq": _rows(_pad_cols(w["w_uq"], HEAD_PAD)), "w_uk": _rows(_pad_cols(w["w_uk"], HEAD_PAD)),
        "w_uv": _rows(_pad_cols(w["w_uv"], HEAD_PAD)), "w_pa": _rows(w["w_proj_attn"]),
        "w_pc": w["w_proj_conv"], "w_out": w["w_out"],
    }
    return [jnp.concatenate([parts[n].astype(BF16) for n, _ in group], axis=0) for group in PACK]


def _cols_from_shards(gs, name, rows):
    idx, off, r = PACK_OFF[name]
    return gs[idx][:, off:off + r].reshape(N_DEV, rows, HEAD_PAD).transpose(1, 0, 2).reshape(rows, N_DEV * HEAD_PAD)


def _rows_from_shards(gs, name, keep=None):
    idx, off, r = PACK_OFF[name]
    keep = r if keep is None else keep
    return gs[idx][:, off:off + keep].reshape(N_DEV * keep, D)


def _rope_placement():
    i = lax.broadcasted_iota(jnp.int32, (HEAD_PAD, D), 0)
    j = lax.broadcasted_iota(jnp.int32, (HEAD_PAD, D), 1)
    return ((i < 2 * ROPE_HALF) & (j % HEAD_PAD == NOPE + i)).astype(BF16)


def _unpack_in(g_in):
    w_inT = _rows_from_shards([g_in, None], "w_inT", IN_SHARD)
    lat_rows = Q_LORA + KV_LORA + 2 * ROPE_HALF
    conv = w_inT[lat_rows:lat_rows + CONV_COLS].reshape(3, D // CONV_CB, CONV_CB, D).transpose(1, 0, 2, 3).reshape(CONV_COLS, D)
    return {"latT": jnp.pad(w_inT[:lat_rows], ((0, LAT_PAD - lat_rows), (0, 0))), "convT": conv,
            "gateT": w_inT[lat_rows + CONV_COLS:]}


def _unpack_misc(g_misc):
    g = [None, g_misc]
    wpa = _cols_from_shards(g, "w_pa", 512).reshape(N_HEADS, NOPE, D)
    return {
        "wq": _cols_from_shards(g, "w_uq", Q_LORA),
        "wk": jnp.concatenate([_cols_from_shards(g, "w_uk", KV_LORA), _rope_placement()], axis=0),
        "wv": _cols_from_shards(g, "w_uv", KV_LORA),
        "wpa": jnp.pad(wpa, ((0, 0), (0, HEAD_PAD - NOPE), (0, 0))).reshape(D, D),
        "wpc": _rows_from_shards(g, "w_pc"), "wout": _rows_from_shards(g, "w_out"),
    }


def _shards_from_cols(a):
    rows = a.shape[0]
    return a.reshape(rows, N_DEV, HEAD_PAD).transpose(1, 0, 2).reshape(N_DEV, rows * HEAD_PAD // D, D)


def _pack_grads(gw):
    lat_rows = Q_LORA + KV_LORA + 2 * ROPE_HALF
    conv = gw["convT"].reshape(D // CONV_CB, 3, CONV_CB, D).transpose(1, 0, 2, 3).reshape(CONV_COLS, D)
    w_inT = jnp.concatenate([gw["latT"][:lat_rows], conv, gw["gateT"]], axis=0).reshape(N_DEV, IN_SHARD, D)
    wpa = gw["wpa"].reshape(N_HEADS, HEAD_PAD, D)[:, :NOPE].reshape(N_HEADS * NOPE, D)
    parts = {}
    parts.update({
        "w_inT": jnp.pad(w_inT, ((0, 0), (0, IN_SHARD_PAD - IN_SHARD), (0, 0))),
        "w_uq": _shards_from_cols(gw["wq"]), "w_uk": _shards_from_cols(gw["wk"][:KV_LORA]),
        "w_uv": _shards_from_cols(gw["wv"][:KV_LORA]), "w_pa": _shards_from_cols(wpa),
        "w_pc": gw["wpc"].reshape(N_DEV, D // N_DEV, D), "w_out": gw["wout"].reshape(N_DEV, D // N_DEV, D),
    })
    return [jnp.concatenate([parts[n] for n, _ in group], axis=1) for group in PACK]


def _unpack_grads(mines):
    def seg(name, keep=None):
        idx, off, r = PACK_OFF[name]
        return mines[idx][off:off + (r if keep is None else keep)]

    return {
        "w_in": seg("w_inT", IN_SHARD).T,
        "w_uq": seg("w_uq").reshape(Q_LORA, HEAD_PAD)[:, :QK_DIM],
        "w_uk": seg("w_uk").reshape(KV_LORA, HEAD_PAD)[:, :NOPE],
        "w_uv": seg("w_uv").reshape(KV_LORA, HEAD_PAD)[:, :NOPE],
        "w_proj_attn": seg("w_pa").reshape(512, HEAD_PAD),
        "w_proj_conv": seg("w_pc"), "w_out": seg("w_out"),
    }


def _rope_tables(positions):
    inv_freq = 1.0 / (ROPE_THETA ** (jnp.arange(ROPE_HALF, dtype=F32) / ROPE_HALF))
    ang = positions.reshape(-1).astype(F32)[:, None] * inv_freq
    cos, sin = jnp.cos(ang), jnp.sin(ang)
    t = ang.shape[0]
    zero = jnp.zeros((t, ROPE_HALF), F32)
    head = jnp.ones((t, NOPE), F32)
    tail = jnp.zeros((t, HEAD_PAD - QK_DIM), F32)
    nohead = jnp.zeros((t, NOPE), F32)
    rc = jnp.concatenate([head, cos, cos, tail], axis=1)
    rs1 = jnp.concatenate([nohead, -sin, zero, tail], axis=1)
    rs2 = jnp.concatenate([nohead, zero, sin, tail], axis=1)
    return rc, rs1, rs2


def _local_step(x, positions, target, conv_w, small, ex):
    n_seq, seq, d = x.shape
    t = n_seq * seq
    x0 = x.reshape(t, d)
    tgt = target.reshape(t, d)
    rc, rs1, rs2 = _rope_tables(positions)
    ghq = _pad_cols(small["q_head_norm"], HEAD_PAD)
    ghk = _pad_cols(small["k_head_norm"], HEAD_PAD)
    TM, HC, TQ = 1024, 256, 512

    def mm(*args, hosted=None, **kw):
        res = _mm(*args, hosted=hosted, **kw)
        return res if hosted is not None else (res, None)

    def wgrad(a, b, name, tm=None, hosted=None):
        return mm(a, b, mode="tn", out_dtype=BF16, tm=tm or a.shape[1], tn=b.shape[1], tk=512, name=name, hosted=hosted)

    f1g, f1u, f1d = ex.gather_now("ffn1")
    (x1, h1, a1, b1), got = _ffn_fwd(x0, small["ffn1_norm"], f1g, f1u, f1d, tm=512, hc=DFF // 2, name="ffn1_fwd",
                                     hosted=ex.gather_chips("mix_in"))
    hm, got = _rms_fwd(x1, small["mix_norm"], tm=TM, name="mix_norm_fwd", hosted=ex.gather_sibling(got))
    W = ex.mix_in_weights(got)
    lat = _mm(hm, W["latT"], mode="nt", out_dtype=BF16, tm=TM, tn=LAT_PAD, tk=D, name="proj_lat")
    conv3, got = mm(hm, W["convT"], mode="nt", out_dtype=BF16, tm=TM, tn=CONV_COLS // 2, tk=D, name="proj_conv",
                    hosted=ex.gather_chips("mix_misc"))
    gl, got = mm(hm, W["gateT"], mode="nt", out_dtype=BF16, tm=TM, tn=GATE_COLS // 2, tk=D, name="proj_gate",
                 hosted=ex.gather_sibling(got))
    W.update(ex.mix_misc_weights(got))
    q, k, v, qn, ckv = _mla_prep_fwd(lat, small["q_a_norm"], small["kv_a_norm"], ghq, ghk, W["wq"], W["wk"], W["wv"], rc, rs1, rs2,
                                     tm=512, name="mla_prep_fwd")
    (o, lse), got = _flash_fwd(q, k, v, n_seq=n_seq, seq=seq, tq=TQ, name="attn_fwd", hosted=ex.gather_chips("ffn2"))
    p = _conv_fwd(conv3, conv_w, n_seq=n_seq, seq=seq, name="conv_fwd")
    (x2, merged, ya, yb), got = _merge_fwd(o, p, gl, small["gate_bias"], x1, W["wpa"], W["wpc"], W["wout"], tm=512, name="merge_fwd",
                                           hosted=ex.gather_sibling(got))
    f2g, f2u, f2d = ex.ffn_weights(got)
    (dy, h2, a2, b2, loss_row), _ = _ffn_fwd(x2, small["ffn2_norm"], f2g, f2u, f2d, tm=512, hc=DFF // 2, name="ffn2_fwd", target=tgt)

    gw, gs = {}, {}
    (da2, db2, *ffn2_grads), _ = _ffn_grads(dy, h2, a2, b2, f2d, tm=TM, hc=HC, name="ffn2_grads")
    (dx2, gs["ffn2_norm"]), _ = _ffn_up_bwd(da2, db2, f2g, f2u, x2, small["ffn2_norm"], dy, tm=512, name="ffn2_up_bwd")

    (dx2b, dya, dyb, dgl, do, dp, gs["gate_bias"]), got = _merge_bwd(
        dx2, ya, yb, gl, small["gate_bias"], W["wpa"], W["wpc"], W["wout"], tm=512, name="merge_bwd",
        hosted=ex.scatter_sibling("ffn2", ffn2_grads))
    ex.scatter_sibling_done("ffn2", got)
    gw["wout"] = wgrad(merged, dx2b, "dw_out")[0]
    gw["wpa"] = wgrad(o, dya, "dw_pa")[0]
    gw["wpc"] = wgrad(p, dyb, "dw_pc")[0]
    dconv3, dconv_w = _conv_bwd(dp, conv3, conv_w, n_seq=n_seq, seq=seq, name="conv_bwd")
    (dq, dk, dv), got = _flash_bwd(q, k, v, o, lse, do, n_seq=n_seq, seq=seq, tq=TQ, name="attn_bwd",
                                   hosted=ex.scatter_chips("ffn2"))
    ex.scatter_chips_done("ffn2", got)
    dlat, dqp, dkp, gs["q_a_norm"], gs["kv_a_norm"], dghq, dghk = _mla_prep_bwd(
        dq, dk, dv, lat, qn, ckv, small["q_a_norm"], small["kv_a_norm"], ghq, ghk, W["wq"], W["wk"], W["wv"], rc, rs1, rs2,
        tm=512, name="mla_prep_bwd")
    gs["q_head_norm"], gs["k_head_norm"] = dghq[:, :QK_DIM], dghk[:, :QK_DIM]
    gw["wq"] = wgrad(qn, dqp, "dw_uq")[0]
    gw["wk"] = wgrad(ckv, dkp, "dw_uk")[0]
    gw["wv"] = wgrad(ckv, dv, "dw_uv")[0]
    gw["convT"] = wgrad(dconv3, hm, "dw_conv", tm=CONV_COLS // 2)[0]
    gw["gateT"] = wgrad(dgl, hm, "dw_gate")[0]
    gw["latT"] = wgrad(dlat, hm, "dw_lat")[0]
    ex.scatter_sibling_now("mix", gw)
    (dx1, gs["mix_norm"]), got = _proj_bwd(dlat, dconv3, dgl, W["latT"], W["convT"], W["gateT"], x1, small["mix_norm"], dx2,
                                           tm=512, name="proj_bwd", hosted=ex.scatter_chips("mix_in"))
    ex.scatter_chips_done("mix_in", got)

    (da1, db1, *ffn1_grads), got = _ffn_grads(dx1, h1, a1, b1, f1d, tm=TM, hc=HC, name="ffn1_grads",
                                              hosted=ex.scatter_chips("mix_misc"))
    ex.scatter_chips_done("mix_misc", got)
    ex.scatter_sibling_now("ffn1", ffn1_grads)
    (dx0, gs["ffn1_norm"]), got = _ffn_up_bwd(da1, db1, f1g, f1u, x0, small["ffn1_norm"], dx1, tm=512, name="ffn1_up_bwd",
                                              hosted=ex.scatter_chips("ffn1"))
    ex.scatter_chips_done("ffn1", got)
    return loss_row, dx0.reshape(n_seq, seq, d), dconv_w, gs


class _MeshExchange:
    def __init__(self, w, core, chip):
        self.w, self.core, self.chip = w, core, chip
        self.partial, self.received, self._packed = {}, {}, None

    def _blocks(self, group):
        w = self.w
        if group.startswith("ffn"):
            return [w[group + "_w_gate"].T.astype(BF16), w[group + "_w_up"].T.astype(BF16), w[group + "_w_down"].astype(BF16)]
        if self._packed is None:
            self._packed = _pack_weights(w)
        return [self._packed[0 if group == "mix_in" else 1]]

    def gather_chips(self, *groups):
        blocks = [b for group in groups for b in self._blocks(group)]
        return _gather_chips_plan(len(blocks)), blocks, _gather_shapes(blocks)

    def gather_sibling(self, got):
        half = list(got)
        return _gather_sibling_plan(len(half)), half, _same_shapes(half)

    def gather_now(self, group):
        plan, blocks, shapes = self.gather_chips(group)
        half = list(_run_plan(plan, blocks, shapes, name="gather_%s_chips" % group))
        return self.ffn_weights(_run_plan(_gather_sibling_plan(len(half)), half, _same_shapes(half), name="gather_%s_sibling" % group))

    def ffn_weights(self, got):
        return [a.reshape(DFF, D) for a in got]

    def mix_in_weights(self, got):
        return _unpack_in(got[0])

    def mix_misc_weights(self, got):
        return _unpack_misc(got[0])

    def _parts(self, group, grads):
        if group == "mix":
            return _pack_grads(grads), ["mix_in", "mix_misc"]
        parts = [g.reshape(N_DEV, -1, D) for g in grads]
        return parts, ([group] if len(parts) == 1 else None)

    def scatter_sibling(self, group, grads):
        self._sent, self._names = self._parts(group, grads)
        return _scatter_sibling_plan(len(self._sent)), self._sent, _halved_shapes(self._sent)

    def scatter_sibling_done(self, group, got):
        sums = [_sum_sibling(p, q, self.core, name="sum_%s_sibling_%d" % (group, i)) for i, (p, q) in enumerate(zip(self._sent, got))]
        if self._names is None:
            self.partial[group] = sums
        else:
            for n, s in zip(self._names, sums):
                self.partial[n] = [s]

    def scatter_sibling_now(self, group, grads):
        plan, parts, shapes = self.scatter_sibling(group, grads)
        self.scatter_sibling_done(group, _run_plan(plan, parts, shapes, name="scatter_%s_sibling" % group))

    def scatter_chips(self, group):
        s1 = self.partial[group]
        return _scatter_chips_plan(len(s1)), s1, _scatter_shapes(s1)

    def scatter_chips_done(self, group, got):
        self.received[group] = list(got)


SMALL_NAMES = ("ffn1_norm", "mix_norm", "gate_bias", "q_a_norm", "kv_a_norm", "q_head_norm", "k_head_norm", "ffn2_norm")
SMALL_SLOTS = {"ffn1_norm": 1024, "mix_norm": 1024, "gate_bias": 2048, "q_a_norm": 384, "kv_a_norm": 256, "q_head_norm": 128,
               "k_head_norm": 128, "ffn2_norm": 1024, "conv_w": 3072, "loss": 128}
COLUMN_MAJOR = ("w_in", "w_uq", "w_uk", "w_uv")
WEIGHT_NAMES = ("ffn1_norm", "ffn1_w_gate", "ffn1_w_up", "ffn1_w_down", "mix_norm", "w_in", "gate_bias", "q_a_norm", "w_uq",
                "kv_a_norm", "w_uk", "w_uv", "q_head_norm", "k_head_norm", "w_proj_attn", "conv_w", "w_proj_conv", "w_out",
                "ffn2_norm", "ffn2_w_gate", "ffn2_w_up", "ffn2_w_down")


def _step(x, positions, loss_target, w, m, v):
    xi, yi, ci = _place()
    core = ci.astype(jnp.int32).reshape(1)
    chip = (2 * xi + yi).astype(jnp.int32).reshape(1)
    me = 4 * xi + 2 * yi + ci

    cw_all = _small_exchange(jnp.pad(w["conv_w"], ((0, 5), (0, 0))), reduce=False, name="gather_conv_w")
    conv_w = cw_all[:, :3].transpose(1, 0, 2).reshape(3, D)
    small = {n: w[n].reshape(1, -1) for n in SMALL_NAMES}
    ex = _MeshExchange(w, core, chip)

    loss_row, grad_x, dconv_w, gs = _local_step(x, positions, loss_target, conv_w, small, ex)

    grads, deltas, new_m, new_v = {}, {}, {}, {}
    where = {"ffn1_w_gate": ("ffn1", 0), "ffn1_w_up": ("ffn1", 1), "ffn1_w_down": ("ffn1", 2),
             "ffn2_w_gate": ("ffn2", 0), "ffn2_w_up": ("ffn2", 1), "ffn2_w_down": ("ffn2", 2)}
    for n, (group, i) in where.items():
        transposed = not n.endswith("down")
        wv, mv, vv = (a[n].T if transposed else a[n] for a in (w, m, v))
        res = _sum_adamw(ex.partial[group][i], ex.received[group][i], chip, wv, mv, vv, name="adamw_" + n)
        grads[n], deltas[n], new_m[n], new_v[n] = (r.T if transposed else r for r in res)
    grads.update(_unpack_grads([_sum_chips(ex.partial[g][0], ex.received[g][0], chip, name="sum_%s_chips" % g)
                                for g in ("mix_in", "mix_misc")]))

    pieces = [_pad_cols(gs[n], SMALL_SLOTS[n]) for n in SMALL_NAMES] + [dconv_w.reshape(1, 3 * D), loss_row]
    total = _small_exchange(jnp.concatenate(pieces, axis=1).reshape(-1, 128), reduce=True, name="reduce_small").reshape(-1)
    off = 0
    for n in SMALL_NAMES:
        grads[n] = total[off:off + w[n].shape[0]]
        off += SMALL_SLOTS[n]
    conv_full = total[off:off + 3 * D].reshape(3, D)
    grads["conv_w"] = lax.dynamic_slice(conv_full, (0, me * HEAD_PAD), (3, HEAD_PAD))
    loss = total[off + 3 * D]

    for n in WEIGHT_NAMES:
        if n in deltas:
            continue
        shape = w[n].shape
        if n in COLUMN_MAJOR:
            ops = [a.T for a in (w[n], grads[n], m[n], v[n])]
            deltas[n], new_m[n], new_v[n] = (r.T for r in _adamw(*ops, name="adamw_" + n))
            continue
        if len(shape) == 1:
            view = (-1, 128) if shape[0] % 128 == 0 else (1, shape[0])
        else:
            view = shape
        dlt, nm, nv = _adamw(w[n].reshape(view), grads[n].reshape(view), m[n].reshape(view), v[n].reshape(view), name="adamw_" + n)
        deltas[n], new_m[n], new_v[n] = dlt.reshape(shape), nm.reshape(shape), nv.reshape(shape)
    return (loss, grad_x, *[grads[n] for n in WEIGHT_NAMES], *[deltas[n] for n in WEIGHT_NAMES],
            *[new_m[n] for n in WEIGHT_NAMES], *[new_v[n] for n in WEIGHT_NAMES])


def kernel(x, positions, ffn1_norm, ffn1_w_gate, ffn1_w_up, ffn1_w_down, mix_norm, w_in, gate_bias, q_a_norm, w_uq, kv_a_norm, w_uk, w_uv, q_head_norm, k_head_norm, w_proj_attn, conv_w, w_proj_conv, w_out, ffn2_norm, ffn2_w_gate, ffn2_w_up, ffn2_w_down, loss_target, m_ffn1_norm, m_ffn1_w_gate, m_ffn1_w_up, m_ffn1_w_down, m_mix_norm, m_w_in, m_gate_bias, m_q_a_norm, m_w_uq, m_kv_a_norm, m_w_uk, m_w_uv, m_q_head_norm, m_k_head_norm, m_w_proj_attn, m_conv_w, m_w_proj_conv, m_w_out, m_ffn2_norm, m_ffn2_w_gate, m_ffn2_w_up, m_ffn2_w_down, v_ffn1_norm, v_ffn1_w_gate, v_ffn1_w_up, v_ffn1_w_down, v_mix_norm, v_w_in, v_gate_bias, v_q_a_norm, v_w_uq, v_kv_a_norm, v_w_uk, v_w_uv, v_q_head_norm, v_k_head_norm, v_w_proj_attn, v_conv_w, v_w_proj_conv, v_w_out, v_ffn2_norm, v_ffn2_w_gate, v_ffn2_w_up, v_ffn2_w_down):
    given = dict(locals())
    w = {n: given[n] for n in WEIGHT_NAMES}
    m = {n: given["m_" + n] for n in WEIGHT_NAMES}
    v = {n: given["v_" + n] for n in WEIGHT_NAMES}
    return _step(x, positions, loss_target, w, m, v)
```

```python
import functools

import jax
import jax.numpy as jnp
from jax import lax
from jax.experimental import pallas as pl
from jax.experimental.pallas import tpu as pltpu

F32 = jnp.float32
BF16 = jnp.bfloat16
MESH = pl.DeviceIdType.MESH
ANY = pl.BlockSpec(memory_space=pl.ANY)

N_DEV = 8
D = 1024
DFF = 2816
N_HEADS = 8
HEAD_PAD = 128
QK_DIM = 96
NOPE = 64
ROPE_HALF = 16
Q_LORA = 384
KV_LORA = 256
LAT_PAD = 768
CONV_COLS = 3072
GATE_COLS = 2048
IN_DIM = 5792
IN_SHARD = IN_DIM // N_DEV
IN_SHARD_PAD = 736
FF_SHARD = DFF // N_DEV
ROPE_THETA = 10000.0
NORM_EPS = 1e-6
ATTN_SCALE = QK_DIM ** -0.5
NEG = -1e30

ADAM_LR, ADAM_B1, ADAM_B2, ADAM_EPS, ADAM_WD, ADAM_STEP = 0.001, 0.9, 0.999, 1e-08, 0.01, 10

PACK = ((("w_inT", IN_SHARD_PAD),), (("w_uq", 48), ("w_uk", 32), ("w_uv", 32), ("w_pa", 64), ("w_pc", 128), ("w_out", 128)))
PACK_OFF = {}
for _i, _group in enumerate(PACK):
    _o = 0
    for _n, _r in _group:
        PACK_OFF[_n] = (_i, _o, _r)
        _o += _r

VMEM_LIMIT = 56 * 1024 * 1024


def _params(*sem):
    return pltpu.CompilerParams(dimension_semantics=sem if sem else None, vmem_limit_bytes=VMEM_LIMIT)


class _Plan:
    def __init__(self, start, wait, n_remote, n_local, in_place=False):
        self.start, self.wait, self.n_remote, self.n_local, self.in_place = start, wait, n_remote, n_local, in_place

    def sems(self):
        return [pltpu.SemaphoreType.DMA((self.n_remote,)), pltpu.SemaphoreType.DMA((self.n_remote,)),
                pltpu.SemaphoreType.DMA((max(self.n_local, 1),))]


def _call(body, *, name, grid, in_specs, out_specs, out_shape, scratch_shapes, operands, sem, hosted=None):
    if hosted is None:
        outs = pl.pallas_call(body, name=name, grid=grid, in_specs=in_specs, out_specs=out_specs, out_shape=out_shape,
                              scratch_shapes=scratch_shapes, compiler_params=_params(*sem))(*operands)
        return outs, None
    plan, srcs, h_shapes = hosted
    n_in, n_out, n_scr, nh_in, nh_out = len(in_specs), len(out_specs), len(scratch_shapes), len(srcs), len(h_shapes)
    aliases = {n_in + a: n_out + a for a in range(nh_in)} if plan.in_place else {}

    def full_body(*refs):
        ins, refs = refs[:n_in], refs[n_in:]
        h_in, refs = refs[:nh_in], refs[nh_in:]
        outs, refs = refs[:n_out], refs[n_out:]
        h_out, refs = refs[:nh_out], refs[nh_out:]
        scr, sems = refs[:n_scr], refs[n_scr:]
        ids = [pl.program_id(ax) for ax in range(len(grid))]
        first = functools.reduce(jnp.logical_and, [i == 0 for i in ids])
        last = functools.reduce(jnp.logical_and, [i == g - 1 for i, g in zip(ids, grid)])

        @pl.when(first)
        def _():
            plan.start(h_in, h_out, *sems)

        body(*ins, *outs, *scr)

        @pl.when(last)
        def _():
            plan.wait(h_in, h_out, *sems)

    res = pl.pallas_call(
        full_body, name=name, grid=grid, in_specs=list(in_specs) + [ANY] * nh_in, out_specs=list(out_specs) + [ANY] * nh_out,
        out_shape=list(out_shape) + list(h_shapes), scratch_shapes=list(scratch_shapes) + plan.sems(),
        input_output_aliases=aliases, compiler_params=_params(*(["arbitrary"] * len(grid))),
    )(*operands, *srcs)
    return res[:n_out], res[n_out:]


def _dot_nn(a, b):
    return lax.dot_general(a, b, (((1,), (0,)), ((), ())), preferred_element_type=F32)


def _dot_nt(a, b):
    return lax.dot_general(a, b, (((1,), (1,)), ((), ())), preferred_element_type=F32)


def _dot_tn(a, b):
    return lax.dot_general(a, b, (((0,), (0,)), ((), ())), preferred_element_type=F32)


def _sigmoid(x):
    return 0.5 * jnp.tanh(0.5 * x) + 0.5


def _rms_stats(x):
    r = lax.rsqrt(jnp.mean(x * x, axis=-1, keepdims=True) + NORM_EPS)
    return x * r, r


ROWS_WIDE = 16
MM_ROWS = 256


def _rms_bwd(dy, xhat, r, g):
    dg = jnp.sum(dy * xhat, axis=0, keepdims=True)
    dxh = dy * g
    dx = r * (dxh - xhat * jnp.mean(dxh * xhat, axis=-1, keepdims=True))
    return dx, dg


def _mm(a, b, *, mode, out_dtype, tm, tn, tk, name, add=None, scale=1.0, hosted=None):
    if mode == "nn":
        (m, k), (_, n) = a.shape, b.shape
    elif mode == "nt":
        (m, k), (n, _) = a.shape, b.shape
    else:
        (k, m), (_, n) = a.shape, b.shape
    assert m % tm == 0 and n % tn == 0 and k % tk == 0, (name, m, n, k, tm, tn, tk)
    nk = k // tk
    dot = {"nn": _dot_nn, "nt": _dot_nt, "tn": _dot_tn}[mode]
    a_spec = pl.BlockSpec((tk, tm), lambda i, j, kk: (kk, i)) if mode == "tn" else pl.BlockSpec((tm, tk), lambda i, j, kk: (i, kk))
    b_spec = pl.BlockSpec((tn, tk), lambda i, j, kk: (j, kk)) if mode == "nt" else pl.BlockSpec((tk, tn), lambda i, j, kk: (kk, j))
    o_spec = pl.BlockSpec((tm, tn), lambda i, j, kk: (i, j))
    has_add = add is not None

    def finish(prod, c_ref, o_ref):
        if scale != 1.0:
            prod = prod * scale
        o_ref[...] = ((c_ref[...] + prod) if has_add else prod).astype(out_dtype)

    def body(*refs):
        a_ref, b_ref = refs[:2]
        c_ref = refs[2] if has_add else None
        o_ref = refs[3] if has_add else refs[2]
        if nk == 1:
            finish(dot(a_ref[...], b_ref[...]), c_ref, o_ref)
            return
        acc_ref = refs[-1]
        kk = pl.program_id(2)

        @pl.when(kk == 0)
        def _():
            acc_ref[...] = jnp.zeros_like(acc_ref)

        acc_ref[...] += dot(a_ref[...], b_ref[...])

        @pl.when(kk == nk - 1)
        def _():
            finish(acc_ref[...], c_ref, o_ref)

    operands = (a, b, add) if has_add else (a, b)
    in_specs = [a_spec, b_spec] + ([o_spec] if has_add else [])
    (out,), got = _call(
        body, name=name, grid=(m // tm, n // tn, nk), in_specs=in_specs, out_specs=[o_spec],
        out_shape=[jax.ShapeDtypeStruct((m, n), out_dtype)], scratch_shapes=[pltpu.VMEM((tm, tn), F32)] if nk > 1 else [],
        operands=operands, sem=("parallel", "parallel", "arbitrary"), hosted=hosted)
    return out if hosted is None else (out, got)


def _rms_fwd(x, g, *, tm, name, hosted=None):
    t, d = x.shape

    def body(x_ref, g_ref, h_ref):
        xhat, _ = _rms_stats(x_ref[...])
        h_ref[...] = (xhat * g_ref[...]).astype(BF16)

    (h,), got = _call(
        body, name=name, grid=(t // tm,),
        in_specs=[pl.BlockSpec((tm, d), lambda i: (i, 0)), pl.BlockSpec((1, d), lambda i: (0, 0))],
        out_specs=[pl.BlockSpec((tm, d), lambda i: (i, 0))], out_shape=[jax.ShapeDtypeStruct((t, d), BF16)], scratch_shapes=[],
        operands=(x, g), sem=("parallel",), hosted=hosted)
    return h, got


def _ffn_fwd(x, g, wgT, wuT, wd, *, tm, hc, name, hosted=None, target=None):
    t, d = x.shape
    nj = DFF // hc
    with_loss = target is not None

    def body(*refs):
        x_ref, g_ref, wg_ref, wu_ref, wd_ref = refs[:5]
        t_ref = refs[5] if with_loss else None
        xo_ref, h_ref, a_ref, b_ref = refs[5 + with_loss:9 + with_loss]
        loss_ref = refs[9 + with_loss] if with_loss else None
        acc_ref = refs[-1]
        i, j = pl.program_id(0), pl.program_id(1)

        @pl.when(j == 0)
        def _():
            xhat, _ = _rms_stats(x_ref[...])
            h_ref[...] = (xhat * g_ref[...]).astype(BF16)
            acc_ref[...] = jnp.zeros_like(acc_ref)

        h = h_ref[...]
        a = _dot_nt(h, wg_ref[...])
        b = _dot_nt(h, wu_ref[...])
        a_ref[...] = a.astype(BF16)
        b_ref[...] = b.astype(BF16)
        s = (a * _sigmoid(a) * b).astype(BF16)
        acc_ref[...] += _dot_nn(s, wd_ref[...])

        if with_loss:
            @pl.when((i == 0) & (j == 0))
            def _():
                loss_ref[...] = jnp.zeros_like(loss_ref)

        @pl.when(j == nj - 1)
        def _():
            y = x_ref[...] + 0.5 * acc_ref[...]
            if with_loss:
                err = y - t_ref[...]
                xo_ref[...] = err * (1.0 / d)
                loss_ref[...] += jnp.sum(jnp.sum(err * err, axis=-1, keepdims=True), axis=0, keepdims=True) * (0.5 / d)
            else:
                xo_ref[...] = y

    row = pl.BlockSpec((tm, d), lambda i, j: (i, 0))
    vec = pl.BlockSpec((1, d), lambda i, j: (0, 0))
    wsp = pl.BlockSpec((hc, d), lambda i, j: (j, 0))
    hid = pl.BlockSpec((tm, hc), lambda i, j: (i, j))
    out_specs = [row, row, hid, hid] + ([pl.BlockSpec((1, 128), lambda i, j: (0, 0))] if with_loss else [])
    out_shape = [jax.ShapeDtypeStruct((t, d), F32), jax.ShapeDtypeStruct((t, d), BF16), jax.ShapeDtypeStruct((t, DFF), BF16),
                 jax.ShapeDtypeStruct((t, DFF), BF16)] + ([jax.ShapeDtypeStruct((1, 128), F32)] if with_loss else [])
    return _call(
        body, name=name, grid=(t // tm, nj), in_specs=[row, vec, wsp, wsp, wsp] + ([row] if with_loss else []),
        out_specs=out_specs, out_shape=out_shape, scratch_shapes=[pltpu.VMEM((tm, d), F32)],
        operands=(x, g, wgT, wuT, wd) + ((target,) if with_loss else ()),
        sem=("arbitrary" if with_loss else "parallel", "arbitrary"), hosted=hosted)


def _ffn_grads(dout, h, a, b, wd, *, tm, hc, name, hosted=None):
    t, d = dout.shape
    ni, nj = t // tm, DFF // hc

    def body(dout_ref, h_ref, a_ref, b_ref, wd_ref, da_ref, db_ref, dwg_ref, dwu_ref, dwd_ref,
             dy_all, h_all, ds_scr, s_scr, acc_g, acc_u, acc_d):
        j, i = pl.program_id(0), pl.program_id(1)
        rows_i = pl.ds(pl.multiple_of(i * tm, tm), tm)

        @pl.when(j == 0)
        def _():
            dy_all[rows_i, :] = (0.5 * dout_ref[...]).astype(BF16)
            h_all[rows_i, :] = h_ref[...]

        @pl.when(i == 0)
        def _():
            acc_g[...] = jnp.zeros_like(acc_g)
            acc_u[...] = jnp.zeros_like(acc_u)
            acc_d[...] = jnp.zeros_like(acc_d)

        def grad_rows(rows):
            ds = ds_scr[rows, :]
            av = a_ref[rows, :].astype(F32)
            bv = b_ref[rows, :].astype(F32)
            sg = _sigmoid(av)
            sl = av * sg
            s_scr[rows, :] = (sl * bv).astype(BF16)
            da_ref[rows, :] = (ds * bv * (sg + sl * (1.0 - sg))).astype(BF16)
            db_ref[rows, :] = (ds * sl).astype(BF16)

        for blk in range(tm // MM_ROWS):
            rs = slice(blk * MM_ROWS, (blk + 1) * MM_ROWS)
            ds_scr[rs, :] = _dot_nt(dy_all[pl.ds(pl.multiple_of(i * tm + blk * MM_ROWS, MM_ROWS), MM_ROWS), :], wd_ref[...])
            for c in range(MM_ROWS // ROWS_WIDE):
                grad_rows(slice(blk * MM_ROWS + c * ROWS_WIDE, blk * MM_ROWS + (c + 1) * ROWS_WIDE))

        dy_i = dy_all[rows_i, :]
        h_i = h_all[rows_i, :]
        acc_d[...] += _dot_tn(s_scr[...], dy_i)
        acc_g[...] += _dot_tn(da_ref[...], h_i)
        acc_u[...] += _dot_tn(db_ref[...], h_i)

        @pl.when(i == ni - 1)
        def _():
            dwg_ref[...] = acc_g[...].astype(BF16)
            dwu_ref[...] = acc_u[...].astype(BF16)
            dwd_ref[...] = acc_d[...].astype(BF16)

    first = pl.BlockSpec((tm, d), lambda j, i: (jnp.where(j == 0, i, 0), 0))
    hid = pl.BlockSpec((tm, hc), lambda j, i: (i, j))
    wsp = pl.BlockSpec((hc, d), lambda j, i: (j, 0))
    hid_shape = jax.ShapeDtypeStruct((t, DFF), BF16)
    w_shape = jax.ShapeDtypeStruct((DFF, d), BF16)
    return _call(
        body, name=name, grid=(nj, ni), in_specs=[first, first, hid, hid, wsp], out_specs=[hid, hid, wsp, wsp, wsp],
        out_shape=[hid_shape, hid_shape, w_shape, w_shape, w_shape],
        scratch_shapes=[pltpu.VMEM((t, d), BF16), pltpu.VMEM((t, d), BF16), pltpu.VMEM((tm, hc), F32), pltpu.VMEM((tm, hc), BF16),
                        pltpu.VMEM((hc, d), F32), pltpu.VMEM((hc, d), F32), pltpu.VMEM((hc, d), F32)],
        operands=(dout, h, a, b, wd), sem=("arbitrary", "arbitrary"), hosted=hosted)


def _proj_bwd(dlat, dconv3, dgl, latT, convT, gateT, x, g, dres, *, tm, name, hosted=None):
    t, d = x.shape

    def body(dl_ref, dc_ref, dg_ref, wl_ref, wc_ref, wg_ref, x_ref, g_ref, dres_ref, dx_ref, dgain_ref):
        @pl.when(pl.program_id(0) == 0)
        def _():
            dgain_ref[...] = jnp.zeros_like(dgain_ref)

        dh = _dot_nn(dl_ref[...], wl_ref[...]) + _dot_nn(dc_ref[...], wc_ref[...]) + _dot_nn(dg_ref[...], wg_ref[...])
        xhat, r = _rms_stats(x_ref[...])
        dx, dgain = _rms_bwd(dh, xhat, r, g_ref[...])
        dx_ref[...] = dres_ref[...] + dx
        dgain_ref[...] += dgain

    def rows(w):
        return pl.BlockSpec((tm, w), lambda i: (i, 0))

    def full(r):
        return pl.BlockSpec((r, d), lambda i: (0, 0))

    return _call(
        body, name=name, grid=(t // tm,),
        in_specs=[rows(LAT_PAD), rows(CONV_COLS), rows(GATE_COLS), full(LAT_PAD), full(CONV_COLS), full(GATE_COLS), rows(d), full(1), rows(d)],
        out_specs=[rows(d), full(1)], out_shape=[jax.ShapeDtypeStruct((t, d), F32), jax.ShapeDtypeStruct((1, d), F32)],
        scratch_shapes=[], operands=(dlat, dconv3, dgl, latT, convT, gateT, x, g, dres), sem=("arbitrary",), hosted=hosted)


def _ffn_up_bwd(da, db, wgT, wuT, x, g, dout, *, tm, name, hosted=None):
    t, d = x.shape

    def body(da_ref, db_ref, wg_ref, wu_ref, x_ref, g_ref, dout_ref, dx_ref, dg_ref):
        @pl.when(pl.program_id(0) == 0)
        def _():
            dg_ref[...] = jnp.zeros_like(dg_ref)

        dh = _dot_nn(da_ref[...], wg_ref[...]) + _dot_nn(db_ref[...], wu_ref[...])
        xhat, r = _rms_stats(x_ref[...])
        dx, dg = _rms_bwd(dh, xhat, r, g_ref[...])
        dx_ref[...] = dout_ref[...] + dx
        dg_ref[...] += dg

    row = pl.BlockSpec((tm, d), lambda i: (i, 0))
    vec = pl.BlockSpec((1, d), lambda i: (0, 0))
    hid = pl.BlockSpec((tm, DFF), lambda i: (i, 0))
    wsp = pl.BlockSpec((DFF, d), lambda i: (0, 0))
    return _call(
        body, name=name, grid=(t // tm,), in_specs=[hid, hid, wsp, wsp, row, vec, row], out_specs=[row, vec],
        out_shape=[jax.ShapeDtypeStruct((t, d), F32), jax.ShapeDtypeStruct((1, d), F32)], scratch_shapes=[],
        operands=(da, db, wgT, wuT, x, g, dout), sem=("arbitrary",), hosted=hosted)


def _rope_fwd(x, c, s1, s2):
    return x * c + pltpu.roll(x, HEAD_PAD - ROPE_HALF, 1) * s1 + pltpu.roll(x, ROPE_HALF, 1) * s2


def _rope_bwd(dy, c, s1, s2):
    return dy * c + pltpu.roll(dy * s1, ROPE_HALF, 1) + pltpu.roll(dy * s2, HEAD_PAD - ROPE_HALF, 1)


def _head_stats(x):
    r = lax.rsqrt(jnp.sum(x * x, axis=-1, keepdims=True) * (1.0 / QK_DIM) + NORM_EPS)
    return x * r, r


def _mla_prep_fwd(lat, gq, gkv, ghq, ghk, wq, wk, wv, rc, rs1, rs2, *, tm, name):
    t = lat.shape[0]

    def body(lat_ref, gq_ref, gkv_ref, ghq_ref, ghk_ref, wq_ref, wk_ref, wv_ref, c_ref, s1_ref, s2_ref,
             q_ref, k_ref, v_ref, qn_ref, ckv_ref):
        lat_v = lat_ref[...]
        qhat, _ = _rms_stats(lat_v[:, :Q_LORA].astype(F32))
        qn = (qhat * gq_ref[...]).astype(BF16)
        khat, _ = _rms_stats(lat_v[:, Q_LORA:Q_LORA + KV_LORA].astype(F32))
        ckv = (khat * gkv_ref[...]).astype(BF16)
        ckv_ext = jnp.concatenate([ckv, lat_v[:, Q_LORA + KV_LORA:]], axis=1)
        qn_ref[...] = qn
        ckv_ref[...] = ckv_ext
        q_pre = _dot_nn(qn, wq_ref[...])
        k_pre = _dot_nn(ckv_ext, wk_ref[...])
        v_ref[...] = _dot_nn(ckv, wv_ref[...]).astype(BF16)
        c, s1, s2 = c_ref[...], s1_ref[...], s2_ref[...]
        for h in range(N_HEADS):
            hs = slice(h * HEAD_PAD, (h + 1) * HEAD_PAD)
            xq, _ = _head_stats(q_pre[:, hs])
            q_ref[:, hs] = _rope_fwd(xq * ghq_ref[...], c, s1, s2).astype(BF16)
            xk, _ = _head_stats(k_pre[:, hs])
            k_ref[:, hs] = _rope_fwd(xk * ghk_ref[...], c, s1, s2).astype(BF16)

    def row(w):
        return pl.BlockSpec((tm, w), lambda i: (i, 0))

    def full(r, w):
        return pl.BlockSpec((r, w), lambda i: (0, 0))

    wide = jax.ShapeDtypeStruct((t, D), BF16)
    lat3 = jax.ShapeDtypeStruct((t, Q_LORA), BF16)
    return pl.pallas_call(
        body, name=name, grid=(t // tm,),
        in_specs=[row(LAT_PAD), full(1, Q_LORA), full(1, KV_LORA), full(1, HEAD_PAD), full(1, HEAD_PAD),
                  full(Q_LORA, D), full(Q_LORA, D), full(KV_LORA, D), row(HEAD_PAD), row(HEAD_PAD), row(HEAD_PAD)],
        out_specs=[row(D), row(D), row(D), row(Q_LORA), row(Q_LORA)],
        out_shape=[wide, wide, wide, lat3, lat3],
        compiler_params=_params("parallel"),
    )(lat, gq, gkv, ghq, ghk, wq, wk, wv, rc, rs1, rs2)


def _mla_prep_bwd(dq, dk, dv, lat, qn, ckv_ext, gq, gkv, ghq, ghk, wq, wk, wv, rc, rs1, rs2, *, tm, name):
    t = lat.shape[0]

    def body(dq_ref, dk_ref, dv_ref, lat_ref, qn_ref, ckv_ref, gq_ref, gkv_ref, ghq_ref, ghk_ref, wq_ref, wk_ref, wv_ref,
             c_ref, s1_ref, s2_ref, dlat_ref, dqp_ref, dkp_ref, dgq_ref, dgkv_ref, dghq_ref, dghk_ref):
        @pl.when(pl.program_id(0) == 0)
        def _():
            dgq_ref[...] = jnp.zeros_like(dgq_ref)
            dgkv_ref[...] = jnp.zeros_like(dgkv_ref)
            dghq_ref[...] = jnp.zeros_like(dghq_ref)
            dghk_ref[...] = jnp.zeros_like(dghk_ref)

        c, s1, s2 = c_ref[...], s1_ref[...], s2_ref[...]
        q_pre = _dot_nn(qn_ref[...], wq_ref[...])
        k_pre = _dot_nn(ckv_ref[...], wk_ref[...])

        def heads(pre, dy_ref, gh_ref, dgh_ref, out_ref):
            dgh = jnp.zeros((1, HEAD_PAD), F32)
            for h in range(N_HEADS):
                hs = slice(h * HEAD_PAD, (h + 1) * HEAD_PAD)
                d = _rope_bwd(dy_ref[:, hs].astype(F32), c, s1, s2)
                xhat, r = _head_stats(pre[:, hs])
                dgh = dgh + jnp.sum(d * xhat, axis=0, keepdims=True)
                dxh = d * gh_ref[...]
                dx = r * (dxh - xhat * (jnp.sum(dxh * xhat, axis=-1, keepdims=True) * (1.0 / QK_DIM)))
                out_ref[:, hs] = dx.astype(BF16)
            dgh_ref[...] += dgh

        heads(q_pre, dq_ref, ghq_ref, dghq_ref, dqp_ref)
        heads(k_pre, dk_ref, ghk_ref, dghk_ref, dkp_ref)
        dqn = _dot_nt(dqp_ref[...], wq_ref[...])
        dce = _dot_nt(dkp_ref[...], wk_ref[...])
        dckv = dce[:, :KV_LORA] + _dot_nt(dv_ref[...], wv_ref[...])
        lat_v = lat_ref[...]
        qhat, rq = _rms_stats(lat_v[:, :Q_LORA].astype(F32))
        dql, dgq = _rms_bwd(dqn, qhat, rq, gq_ref[...])
        khat, rk = _rms_stats(lat_v[:, Q_LORA:Q_LORA + KV_LORA].astype(F32))
        dkl, dgkv = _rms_bwd(dckv, khat, rk, gkv_ref[...])
        dgq_ref[...] += dgq
        dgkv_ref[...] += dgkv
        dlat_ref[...] = jnp.concatenate([dql, dkl, dce[:, KV_LORA:]], axis=1).astype(BF16)

    def row(w):
        return pl.BlockSpec((tm, w), lambda i: (i, 0))

    def full(r, w):
        return pl.BlockSpec((r, w), lambda i: (0, 0))

    return pl.pallas_call(
        body, name=name, grid=(t // tm,),
        in_specs=[row(D), row(D), row(D), row(LAT_PAD), row(Q_LORA), row(Q_LORA), full(1, Q_LORA), full(1, KV_LORA),
                  full(1, HEAD_PAD), full(1, HEAD_PAD), full(Q_LORA, D), full(Q_LORA, D), full(KV_LORA, D),
                  row(HEAD_PAD), row(HEAD_PAD), row(HEAD_PAD)],
        out_specs=[row(LAT_PAD), row(D), row(D), full(1, Q_LORA), full(1, KV_LORA), full(1, HEAD_PAD), full(1, HEAD_PAD)],
        out_shape=[jax.ShapeDtypeStruct((t, LAT_PAD), BF16), jax.ShapeDtypeStruct((t, D), BF16), jax.ShapeDtypeStruct((t, D), BF16),
                   jax.ShapeDtypeStruct((1, Q_LORA), F32), jax.ShapeDtypeStruct((1, KV_LORA), F32),
                   jax.ShapeDtypeStruct((1, HEAD_PAD), F32), jax.ShapeDtypeStruct((1, HEAD_PAD), F32)],
        compiler_params=_params("arbitrary"),
    )(dq, dk, dv, lat, qn, ckv_ext, gq, gkv, ghq, ghk, wq, wk, wv, rc, rs1, rs2)


def _causal_keep(tq):
    r = lax.broadcasted_iota(jnp.int32, (tq, tq), 0)
    c = lax.broadcasted_iota(jnp.int32, (tq, tq), 1)
    return c <= r


def _flash_fwd(q, k, v, *, n_seq, seq, tq, name, hosted=None):
    nq = seq // tq

    def body(q_ref, k_ref, v_ref, o_ref, lse_ref):
        qi = pl.program_id(2)
        qv = q_ref[...]

        def step(j, carry, masked):
            m, l, acc = carry
            kj = k_ref[pl.ds(pl.multiple_of(j * tq, tq), tq), :]
            vj = v_ref[pl.ds(pl.multiple_of(j * tq, tq), tq), :]
            s = _dot_nt(qv, kj) * ATTN_SCALE
            if masked:
                s = jnp.where(_causal_keep(tq), s, NEG)
            m_new = jnp.maximum(m, jnp.max(s, axis=-1, keepdims=True))
            alpha = jnp.exp(m - m_new)
            p = jnp.exp(s - m_new)
            l = alpha * l + jnp.sum(p, axis=-1, keepdims=True)
            acc = alpha * acc + _dot_nn(p.astype(BF16), vj)
            return m_new, l, acc

        init = (jnp.full((tq, 1), NEG, F32), jnp.zeros((tq, 1), F32), jnp.zeros((tq, HEAD_PAD), F32))
        carry = lax.fori_loop(0, qi, lambda j, cr: step(j, cr, False), init)
        m, l, acc = step(qi, carry, True)
        o_ref[...] = (acc / l).astype(BF16)
        lse_ref[...] = jnp.broadcast_to(m + jnp.log(l), (tq, HEAD_PAD))

    qspec = pl.BlockSpec((tq, HEAD_PAD), lambda b, h, i: (b * nq + i, h))
    kspec = pl.BlockSpec((seq, HEAD_PAD), lambda b, h, i: (b, h))
    t = n_seq * seq
    return _call(
        body, name=name, grid=(n_seq, N_HEADS, nq), in_specs=[qspec, kspec, kspec], out_specs=[qspec, qspec],
        out_shape=[jax.ShapeDtypeStruct((t, D), BF16), jax.ShapeDtypeStruct((t, D), F32)], scratch_shapes=[],
        operands=(q, k, v), sem=("parallel", "parallel", "arbitrary"), hosted=hosted)


def _flash_bwd(q, k, v, o, lse, do, *, n_seq, seq, tq, name, hosted=None):
    nq = seq // tq

    def body(q_ref, k_ref, v_ref, o_ref, lse_ref, do_ref, dq_ref, dk_ref, dv_ref, dk_acc, dv_acc):
        j = pl.program_id(2)

        @pl.when(j == 0)
        def _():
            dq_ref[...] = jnp.zeros_like(dq_ref)

        dk_acc[...] = jnp.zeros_like(dk_acc)
        dv_acc[...] = jnp.zeros_like(dv_acc)
        kv = k_ref[...]
        vv = v_ref[...]

        def step(i, masked):
            rows = pl.ds(pl.multiple_of(i * tq, tq), tq)
            qi = q_ref[rows, :]
            doi = do_ref[rows, :]
            delta = jnp.sum(doi.astype(F32) * o_ref[rows, :].astype(F32), axis=-1, keepdims=True)
            s = _dot_nt(qi, kv) * ATTN_SCALE
            p = jnp.exp(s - lse_ref[rows, :][:, :1])
            if masked:
                p = jnp.where(_causal_keep(tq), p, 0.0)
            dv_acc[...] += _dot_tn(p.astype(BF16), doi)
            dp = _dot_nt(doi, vv)
            ds = (p * (dp - delta) * ATTN_SCALE).astype(BF16)
            dk_acc[...] += _dot_tn(ds, qi)
            dq_ref[rows, :] += _dot_nn(ds, kv)

        step(j, True)

        def loop_body(i, carry):
            step(i, False)
            return carry

        lax.fori_loop(j + 1, nq, loop_body, 0)
        dk_ref[...] = dk_acc[...]
        dv_ref[...] = dv_acc[...].astype(BF16)

    full = pl.BlockSpec((seq, HEAD_PAD), lambda b, h, j: (b, h))
    tile = pl.BlockSpec((tq, HEAD_PAD), lambda b, h, j: (b * nq + j, h))
    t = n_seq * seq
    return _call(
        body, name=name, grid=(n_seq, N_HEADS, nq), in_specs=[full, tile, tile, full, full, full],
        out_specs=[full, tile, tile],
        out_shape=[jax.ShapeDtypeStruct((t, D), F32), jax.ShapeDtypeStruct((t, D), F32), jax.ShapeDtypeStruct((t, D), BF16)],
        scratch_shapes=[pltpu.VMEM((tq, HEAD_PAD), F32), pltpu.VMEM((tq, HEAD_PAD), F32)],
        operands=(q, k, v, o, lse, do), sem=("parallel", "parallel", "arbitrary"), hosted=hosted)


CONV_CB = 256


def _shift_down(u, k, row):
    return jnp.where(row >= k, pltpu.roll(u, k, 0), 0.0)


def _shift_up(u, k, row, n):
    return jnp.where(row < n - k, pltpu.roll(u, n - k, 0), 0.0)


def _conv_fwd(conv3, cw, *, n_seq, seq, name):
    def body(c_ref, w_ref, p_ref):
        blk = c_ref[...].astype(F32)
        xc, gb, gc = blk[:, :CONV_CB], blk[:, CONV_CB:2 * CONV_CB], blk[:, 2 * CONV_CB:]
        row = lax.broadcasted_iota(jnp.int32, (seq, CONV_CB), 0)
        u = gc * xc
        z = w_ref[0:1, :] * _shift_down(u, 2, row) + w_ref[1:2, :] * _shift_down(u, 1, row) + w_ref[2:3, :] * u
        p_ref[...] = (gb * z).astype(BF16)

    return pl.pallas_call(
        body, name=name, grid=(n_seq, D // CONV_CB),
        in_specs=[pl.BlockSpec((seq, 3 * CONV_CB), lambda b, j: (b, j)), pl.BlockSpec((3, CONV_CB), lambda b, j: (0, j))],
        out_specs=pl.BlockSpec((seq, CONV_CB), lambda b, j: (b, j)),
        out_shape=jax.ShapeDtypeStruct((n_seq * seq, D), BF16),
        compiler_params=_params("parallel", "parallel"),
    )(conv3, cw)


def _conv_bwd(dp, conv3, cw, *, n_seq, seq, name):
    def body(dp_ref, c_ref, w_ref, dc_ref, dw_ref):
        @pl.when(pl.program_id(1) == 0)
        def _():
            dw_ref[...] = jnp.zeros_like(dw_ref)

        blk = c_ref[...].astype(F32)
        xc, gb, gc = blk[:, :CONV_CB], blk[:, CONV_CB:2 * CONV_CB], blk[:, 2 * CONV_CB:]
        row = lax.broadcasted_iota(jnp.int32, (seq, CONV_CB), 0)
        w0, w1, w2 = w_ref[0:1, :], w_ref[1:2, :], w_ref[2:3, :]
        u = gc * xc
        u1 = _shift_down(u, 1, row)
        u2 = _shift_down(u, 2, row)
        z = w0 * u2 + w1 * u1 + w2 * u
        dpv = dp_ref[...].astype(F32)
        dz = dpv * gb
        du = w2 * dz + w1 * _shift_up(dz, 1, row, seq) + w0 * _shift_up(dz, 2, row, seq)
        dc_ref[...] = jnp.concatenate([du * gc, dpv * z, du * xc], axis=1).astype(BF16)
        dw_ref[0:1, :] += jnp.sum(dz * u2, axis=0, keepdims=True)
        dw_ref[1:2, :] += jnp.sum(dz * u1, axis=0, keepdims=True)
        dw_ref[2:3, :] += jnp.sum(dz * u, axis=0, keepdims=True)

    return pl.pallas_call(
        body, name=name, grid=(D // CONV_CB, n_seq),
        in_specs=[pl.BlockSpec((seq, CONV_CB), lambda j, b: (b, j)), pl.BlockSpec((seq, 3 * CONV_CB), lambda j, b: (b, j)),
                  pl.BlockSpec((3, CONV_CB), lambda j, b: (0, j))],
        out_specs=[pl.BlockSpec((seq, 3 * CONV_CB), lambda j, b: (b, j)), pl.BlockSpec((3, CONV_CB), lambda j, b: (0, j))],
        out_shape=[jax.ShapeDtypeStruct((n_seq * seq, CONV_COLS), BF16), jax.ShapeDtypeStruct((3, D), F32)],
        compiler_params=_params("parallel", "arbitrary"),
    )(dp, conv3, cw)


def _merge_fwd(o, p, gl, bias, x1, wpa, wpc, wout, *, tm, name, hosted=None):
    t = x1.shape[0]

    def body(o_ref, p_ref, gl_ref, b_ref, x_ref, wpa_ref, wpc_ref, wout_ref, x2_ref, mg_ref, ya_ref, yb_ref):
        ya = _dot_nn(o_ref[...], wpa_ref[...])
        yb = _dot_nn(p_ref[...], wpc_ref[...])
        gates = _sigmoid(gl_ref[...].astype(F32) + b_ref[...])
        merged = (gates[:, :D] * ya + gates[:, D:] * yb).astype(BF16)
        ya_ref[...] = ya.astype(BF16)
        yb_ref[...] = yb.astype(BF16)
        mg_ref[...] = merged
        x2_ref[...] = x_ref[...] + _dot_nn(merged, wout_ref[...])

    row = pl.BlockSpec((tm, D), lambda i: (i, 0))
    row2 = pl.BlockSpec((tm, GATE_COLS), lambda i: (i, 0))
    wsp = pl.BlockSpec((D, D), lambda i: (0, 0))
    wide = jax.ShapeDtypeStruct((t, D), BF16)
    return _call(
        body, name=name, grid=(t // tm,),
        in_specs=[row, row, row2, pl.BlockSpec((1, GATE_COLS), lambda i: (0, 0)), row, wsp, wsp, wsp],
        out_specs=[row, row, row, row], out_shape=[jax.ShapeDtypeStruct((t, D), F32), wide, wide, wide], scratch_shapes=[],
        operands=(o, p, gl, bias, x1, wpa, wpc, wout), sem=("parallel",), hosted=hosted)


def _merge_bwd(dx2, ya, yb, gl, bias, wpa, wpc, wout, *, tm, name, hosted=None):
    t = dx2.shape[0]

    def body(dx_ref, ya_ref, yb_ref, gl_ref, b_ref, wpa_ref, wpc_ref, wout_ref,
             dxb_ref, dya_ref, dyb_ref, dgl_ref, do_ref, dp_ref, db_ref):
        @pl.when(pl.program_id(0) == 0)
        def _():
            db_ref[...] = jnp.zeros_like(db_ref)

        dxb = dx_ref[...].astype(BF16)
        dxb_ref[...] = dxb
        dm = _dot_nt(dxb, wout_ref[...])
        gates = _sigmoid(gl_ref[...].astype(F32) + b_ref[...])
        ga, gb = gates[:, :D], gates[:, D:]
        dya = (dm * ga).astype(BF16)
        dyb = (dm * gb).astype(BF16)
        dya_ref[...] = dya
        dyb_ref[...] = dyb
        dgl = jnp.concatenate([dm * ya_ref[...].astype(F32) * ga * (1.0 - ga),
                               dm * yb_ref[...].astype(F32) * gb * (1.0 - gb)], axis=1)
        dgl_ref[...] = dgl.astype(BF16)
        db_ref[...] += jnp.sum(dgl, axis=0, keepdims=True)
        do_ref[...] = _dot_nt(dya, wpa_ref[...]).astype(BF16)
        dp_ref[...] = _dot_nt(dyb, wpc_ref[...]).astype(BF16)

    row = pl.BlockSpec((tm, D), lambda i: (i, 0))
    row2 = pl.BlockSpec((tm, GATE_COLS), lambda i: (i, 0))
    vec2 = pl.BlockSpec((1, GATE_COLS), lambda i: (0, 0))
    wsp = pl.BlockSpec((D, D), lambda i: (0, 0))
    wide = jax.ShapeDtypeStruct((t, D), BF16)
    return _call(
        body, name=name, grid=(t // tm,), in_specs=[row, row, row, row2, vec2, wsp, wsp, wsp],
        out_specs=[row, row, row, row2, row, row, vec2],
        out_shape=[wide, wide, wide, jax.ShapeDtypeStruct((t, GATE_COLS), BF16), wide, wide,
                   jax.ShapeDtypeStruct((1, GATE_COLS), F32)],
        scratch_shapes=[], operands=(dx2, ya, yb, gl, bias, wpa, wpc, wout), sem=("arbitrary",), hosted=hosted)


def _adamw(w, g, m, v, *, name):
    rows, cols = w.shape
    tr = max([c for c in range(8, 513, 8) if rows % c == 0], default=rows)
    c1 = 1.0 / (1.0 - ADAM_B1 ** ADAM_STEP)
    c2 = 1.0 / (1.0 - ADAM_B2 ** ADAM_STEP)

    def body(w_ref, g_ref, m_ref, v_ref, d_ref, nm_ref, nv_ref):
        gv = g_ref[...]
        nm = ADAM_B1 * m_ref[...] + (1.0 - ADAM_B1) * gv
        nv = ADAM_B2 * v_ref[...] + (1.0 - ADAM_B2) * (gv * gv)
        nm_ref[...] = nm
        nv_ref[...] = nv
        d_ref[...] = -ADAM_LR * ((nm * c1) / (jnp.sqrt(nv * c2) + ADAM_EPS) + ADAM_WD * w_ref[...])

    spec = pl.BlockSpec((tr, cols), lambda i: (i, 0))
    shp = jax.ShapeDtypeStruct((rows, cols), F32)
    return pl.pallas_call(
        body, name=name, grid=(rows // tr,), in_specs=[spec] * 4, out_specs=[spec] * 3, out_shape=[shp] * 3,
        compiler_params=_params("parallel"),
    )(w, g, m, v)


def _place():
    return lax.axis_index("x"), lax.axis_index("y"), lax.axis_index("c")


def _other_chips(x, y):
    return [(1 - x, y), (x, 1 - y), (1 - x, 1 - y)]


def _remote(src, dst, send, recv, dev):
    return pltpu.make_async_remote_copy(src_ref=src, dst_ref=dst, send_sem=send, recv_sem=recv, device_id=dev, device_id_type=MESH)


def _gather_chips_plan(n):
    def start(srcs, dsts, send, recv, local):
        x, y, cc = _place()
        me = 4 * x + 2 * y + cc
        for a in range(n):
            pltpu.make_async_copy(srcs[a], dsts[a].at[me], local.at[a]).start()
            for k, (px, py) in enumerate(_other_chips(x, y)):
                _remote(srcs[a], dsts[a].at[me], send.at[3 * a + k], recv.at[3 * a + k], (px, py, cc)).start()

    def wait(srcs, dsts, send, recv, local):
        x, y, cc = _place()
        me = 4 * x + 2 * y + cc
        for a in range(n):
            for k, (px, py) in enumerate(_other_chips(x, y)):
                _remote(srcs[a], dsts[a].at[4 * px + 2 * py + cc], send.at[3 * a + k], recv.at[3 * a + k], (px, py, cc)).wait_recv()
        for a in range(n):
            for k, (px, py) in enumerate(_other_chips(x, y)):
                _remote(srcs[a], dsts[a].at[me], send.at[3 * a + k], recv.at[3 * a + k], (px, py, cc)).wait_send()
            pltpu.make_async_copy(srcs[a], dsts[a].at[me], local.at[a]).wait()

    return _Plan(start, wait, 3 * n, n)


def _scatter_chips_plan(n):
    def start(srcs, dsts, send, recv, local):
        x, y, cc = _place()
        for a in range(n):
            for k, (px, py) in enumerate(_other_chips(x, y)):
                _remote(srcs[a].at[2 * px + py], dsts[a].at[k], send.at[3 * a + k], recv.at[3 * a + k], (px, py, cc)).start()

    def wait(srcs, dsts, send, recv, local):
        x, y, cc = _place()
        for a in range(n):
            for k, (px, py) in enumerate(_other_chips(x, y)):
                _remote(srcs[a].at[k], dsts[a].at[k], send.at[3 * a + k], recv.at[3 * a + k], (px, py, cc)).wait_recv()
        for a in range(n):
            for k, (px, py) in enumerate(_other_chips(x, y)):
                _remote(srcs[a].at[k], dsts[a].at[k], send.at[3 * a + k], recv.at[3 * a + k], (px, py, cc)).wait_send()

    return _Plan(start, wait, 3 * n, 0)


def _gather_shapes(blocks):
    return [jax.ShapeDtypeStruct((N_DEV,) + b.shape, b.dtype) for b in blocks]


def _scatter_shapes(parts):
    return [jax.ShapeDtypeStruct((3,) + p.shape[1:], p.dtype) for p in parts]


def _gather_sibling_plan(n):
    def start(srcs, dsts, send, recv, local):
        x, y, cc = _place()
        for a in range(n):
            for q in range(4):
                _remote(srcs[a].at[2 * q + cc], dsts[a].at[2 * q + cc], send.at[4 * a + q], recv.at[4 * a + q], (x, y, 1 - cc)).start()

    def wait(srcs, dsts, send, recv, local):
        x, y, cc = _place()
        for a in range(n):
            for q in range(4):
                _remote(srcs[a].at[2 * q + cc], dsts[a].at[2 * q + 1 - cc], send.at[4 * a + q], recv.at[4 * a + q],
                        (x, y, 1 - cc)).wait_recv()
        for a in range(n):
            for q in range(4):
                _remote(srcs[a].at[2 * q + cc], dsts[a].at[2 * q + cc], send.at[4 * a + q], recv.at[4 * a + q],
                        (x, y, 1 - cc)).wait_send()

    return _Plan(start, wait, 4 * n, 0, in_place=True)


def _scatter_sibling_plan(n):
    def start(srcs, dsts, send, recv, local):
        x, y, cc = _place()
        for a in range(n):
            for q in range(4):
                _remote(srcs[a].at[2 * q + 1 - cc], dsts[a].at[q], send.at[4 * a + q], recv.at[4 * a + q], (x, y, 1 - cc)).start()

    def wait(srcs, dsts, send, recv, local):
        x, y, cc = _place()
        for a in range(n):
            for q in range(4):
                _remote(srcs[a].at[q], dsts[a].at[q], send.at[4 * a + q], recv.at[4 * a + q], (x, y, 1 - cc)).wait_recv()
        for a in range(n):
            for q in range(4):
                _remote(srcs[a].at[q], dsts[a].at[q], send.at[4 * a + q], recv.at[4 * a + q], (x, y, 1 - cc)).wait_send()

    return _Plan(start, wait, 4 * n, 0)


def _same_shapes(arrs):
    return [jax.ShapeDtypeStruct(a.shape, a.dtype) for a in arrs]


def _halved_shapes(parts):
    return [jax.ShapeDtypeStruct((4,) + p.shape[1:], p.dtype) for p in parts]


def _run_plan(plan, srcs, out_shapes, *, name):
    n_in, n_out = len(srcs), len(out_shapes)

    def body(*refs):
        h_in, h_out, sems = refs[:n_in], refs[n_in:n_in + n_out], refs[n_in + n_out:]
        plan.start(h_in, h_out, *sems)
        plan.wait(h_in, h_out, *sems)

    return pl.pallas_call(body, name=name, in_specs=[ANY] * n_in, out_specs=[ANY] * n_out, out_shape=list(out_shapes),
                          input_output_aliases={a: a for a in range(n_in)} if plan.in_place else {},
                          scratch_shapes=plan.sems())(*srcs)


def _sum_sibling(p, q, core, *, name):
    _, r, c = p.shape

    def body(core_ref, p_ref, q_ref, o_ref):
        o_ref[...] = (p_ref[...].astype(F32) + q_ref[...].astype(F32)).astype(BF16)

    grid_spec = pltpu.PrefetchScalarGridSpec(
        num_scalar_prefetch=1, grid=(4,),
        in_specs=[pl.BlockSpec((1, r, c), lambda ch, core_ref: (2 * ch + core_ref[0], 0, 0)),
                  pl.BlockSpec((1, r, c), lambda ch, core_ref: (ch, 0, 0))],
        out_specs=pl.BlockSpec((1, r, c), lambda ch, core_ref: (ch, 0, 0)))
    return pl.pallas_call(
        body, name=name, grid_spec=grid_spec, out_shape=jax.ShapeDtypeStruct((4, r, c), BF16),
        compiler_params=_params("parallel"),
    )(core, p, q)


def _sum_chips(s1, r2, chip, *, name):
    _, r, c = s1.shape

    def body(chip_ref, s_ref, r_ref, o_ref):
        acc = s_ref[0].astype(F32)
        for k in range(3):
            acc = acc + r_ref[k].astype(F32)
        o_ref[...] = acc

    grid_spec = pltpu.PrefetchScalarGridSpec(
        num_scalar_prefetch=1, grid=(1,),
        in_specs=[pl.BlockSpec((1, r, c), lambda i, chip_ref: (chip_ref[0], 0, 0)),
                  pl.BlockSpec((3, r, c), lambda i, chip_ref: (0, 0, 0))],
        out_specs=pl.BlockSpec((r, c), lambda i, chip_ref: (0, 0)))
    return pl.pallas_call(
        body, name=name, grid_spec=grid_spec, out_shape=jax.ShapeDtypeStruct((r, c), F32),
        compiler_params=_params("arbitrary"),
    )(chip, s1, r2)


def _sum_adamw(s1, r2, chip, w, m, v, *, name):
    _, r, c = s1.shape
    c1 = 1.0 / (1.0 - ADAM_B1 ** ADAM_STEP)
    c2 = 1.0 / (1.0 - ADAM_B2 ** ADAM_STEP)

    def body(chip_ref, s_ref, r_ref, w_ref, m_ref, v_ref, g_ref, d_ref, nm_ref, nv_ref):
        gv = s_ref[0].astype(F32)
        for k in range(3):
            gv = gv + r_ref[k].astype(F32)
        g_ref[...] = gv
        nm = ADAM_B1 * m_ref[...] + (1.0 - ADAM_B1) * gv
        nv = ADAM_B2 * v_ref[...] + (1.0 - ADAM_B2) * (gv * gv)
        nm_ref[...] = nm
        nv_ref[...] = nv
        d_ref[...] = -ADAM_LR * ((nm * c1) / (jnp.sqrt(nv * c2) + ADAM_EPS) + ADAM_WD * w_ref[...])

    flat = pl.BlockSpec((r, c), lambda i, chip_ref: (0, 0))
    grid_spec = pltpu.PrefetchScalarGridSpec(
        num_scalar_prefetch=1, grid=(1,),
        in_specs=[pl.BlockSpec((1, r, c), lambda i, chip_ref: (chip_ref[0], 0, 0)),
                  pl.BlockSpec((3, r, c), lambda i, chip_ref: (0, 0, 0)), flat, flat, flat],
        out_specs=[flat] * 4)
    return pl.pallas_call(
        body, name=name, grid_spec=grid_spec, out_shape=[jax.ShapeDtypeStruct((r, c), F32)] * 4,
        compiler_params=_params("arbitrary"),
    )(chip, s1, r2, w, m, v)


def _small_exchange(v, *, reduce, name):
    r, c = v.shape

    def body(x_ref, o_ref, *rest):
        if reduce:
            buf_ref, send_sems, recv_sems = rest
        else:
            buf_ref = o_ref
            send_sems, recv_sems = rest
        x, y, cc = _place()
        me = 4 * x + 2 * y + cc

        def peer(k):
            return ((1 - x) if k & 4 else x, (1 - y) if k & 2 else y, (1 - cc) if k & 1 else cc)

        buf_ref[me] = x_ref[...]
        sends = []
        for k in range(1, N_DEV):
            cp = pltpu.make_async_remote_copy(src_ref=x_ref, dst_ref=buf_ref.at[me], send_sem=send_sems.at[k - 1],
                                              recv_sem=recv_sems.at[k - 1], device_id=peer(k), device_id_type=MESH)
            cp.start()
            sends.append(cp)
        for k in range(1, N_DEV):
            px, py, pc = peer(k)
            pltpu.make_async_remote_copy(src_ref=x_ref, dst_ref=buf_ref.at[4 * px + 2 * py + pc], send_sem=send_sems.at[k - 1],
                                         recv_sem=recv_sems.at[k - 1], device_id=peer(k), device_id_type=MESH).wait_recv()
        for cp in sends:
            cp.wait_send()
        if reduce:
            acc = buf_ref[0]
            for s in range(1, N_DEV):
                acc = acc + buf_ref[s]
            o_ref[...] = acc

    vm = pl.BlockSpec(memory_space=pltpu.VMEM)
    sems = [pltpu.SemaphoreType.DMA((N_DEV - 1,)), pltpu.SemaphoreType.DMA((N_DEV - 1,))]
    if reduce:
        out_shape, scratch = jax.ShapeDtypeStruct((r, c), F32), [pltpu.VMEM((N_DEV, r, c), F32)] + sems
    else:
        out_shape, scratch = jax.ShapeDtypeStruct((N_DEV, r, c), F32), sems
    return pl.pallas_call(body, name=name, in_specs=[vm], out_specs=vm, out_shape=out_shape, scratch_shapes=scratch)(v)


def _rows(a):
    return a.reshape(-1, D)


def _pad_cols(a, to):
    return jnp.pad(a, ((0, 0), (0, to - a.shape[1])))


def _pack_weights(w):
    parts = {
        "w_inT": jnp.pad(w["w_in"].T, ((0, IN_SHARD_PAD - IN_SHARD), (0, 0))),
        "w_uq": _rows(_pad_cols(w["w_uq"], HEAD_PAD)), "w_uk": _rows(_pad_cols(w["w_uk"], HEAD_PAD)),
        "w_uv": _rows(_pad_cols(w["w_uv"], HEAD_PAD)), "w_pa": _rows(w["w_proj_attn"]),
        "w_pc": w["w_proj_conv"], "w_out": w["w_out"],
    }
    return [jnp.concatenate([parts[n].astype(BF16) for n, _ in group], axis=0) for group in PACK]


def _cols_from_shards(gs, name, rows):
    idx, off, r = PACK_OFF[name]
    return gs[idx][:, off:off + r].reshape(N_DEV, rows, HEAD_PAD).transpose(1, 0, 2).reshape(rows, N_DEV * HEAD_PAD)


def _rows_from_shards(gs, name, keep=None):
    idx, off, r = PACK_OFF[name]
    keep = r if keep is None else keep
    return gs[idx][:, off:off + keep].reshape(N_DEV * keep, D)


def _rope_placement():
    i = lax.broadcasted_iota(jnp.int32, (HEAD_PAD, D), 0)
    j = lax.broadcasted_iota(jnp.int32, (HEAD_PAD, D), 1)
    return ((i < 2 * ROPE_HALF) & (j % HEAD_PAD == NOPE + i)).astype(BF16)


def _unpack_in(g_in):
    w_inT = _rows_from_shards([g_in, None], "w_inT", IN_SHARD)
    lat_rows = Q_LORA + KV_LORA + 2 * ROPE_HALF
    conv = w_inT[lat_rows:lat_rows + CONV_COLS].reshape(3, D // CONV_CB, CONV_CB, D).transpose(1, 0, 2, 3).reshape(CONV_COLS, D)
    return {"latT": jnp.pad(w_inT[:lat_rows], ((0, LAT_PAD - lat_rows), (0, 0))), "convT": conv,
            "gateT": w_inT[lat_rows + CONV_COLS:]}


def _unpack_misc(g_misc):
    g = [None, g_misc]
    wpa = _cols_from_shards(g, "w_pa", 512).reshape(N_HEADS, NOPE, D)
    return {
        "wq": _cols_from_shards(g, "w_uq", Q_LORA),
        "wk": jnp.concatenate([_cols_from_shards(g, "w_uk", KV_LORA), _rope_placement()], axis=0),
        "wv": _cols_from_shards(g, "w_uv", KV_LORA),
        "wpa": jnp.pad(wpa, ((0, 0), (0, HEAD_PAD - NOPE), (0, 0))).reshape(D, D),
        "wpc": _rows_from_shards(g, "w_pc"), "wout": _rows_from_shards(g, "w_out"),
    }


def _shards_from_cols(a):
    rows = a.shape[0]
    return a.reshape(rows, N_DEV, HEAD_PAD).transpose(1, 0, 2).reshape(N_DEV, rows * HEAD_PAD // D, D)


def _pack_grads(gw):
    lat_rows = Q_LORA + KV_LORA + 2 * ROPE_HALF
    conv = gw["convT"].reshape(D // CONV_CB, 3, CONV_CB, D).transpose(1, 0, 2, 3).reshape(CONV_COLS, D)
    w_inT = jnp.concatenate([gw["latT"][:lat_rows], conv, gw["gateT"]], axis=0).reshape(N_DEV, IN_SHARD, D)
    wpa = gw["wpa"].reshape(N_HEADS, HEAD_PAD, D)[:, :NOPE].reshape(N_HEADS * NOPE, D)
    parts = {}
    parts.update({
        "w_inT": jnp.pad(w_inT, ((0, 0), (0, IN_SHARD_PAD - IN_SHARD), (0, 0))),
        "w_uq": _shards_from_cols(gw["wq"]), "w_uk": _shards_from_cols(gw["wk"][:KV_LORA]),
        "w_uv": _shards_from_cols(gw["wv"][:KV_LORA]), "w_pa": _shards_from_cols(wpa),
        "w_pc": gw["wpc"].reshape(N_DEV, D // N_DEV, D), "w_out": gw["wout"].reshape(N_DEV, D // N_DEV, D),
    })
    return [jnp.concatenate([parts[n] for n, _ in group], axis=1) for group in PACK]


def _unpack_grads(mines):
    def seg(name, keep=None):
        idx, off, r = PACK_OFF[name]
        return mines[idx][off:off + (r if keep is None else keep)]

    return {
        "w_in": seg("w_inT", IN_SHARD).T,
        "w_uq": seg("w_uq").reshape(Q_LORA, HEAD_PAD)[:, :QK_DIM],
        "w_uk": seg("w_uk").reshape(KV_LORA, HEAD_PAD)[:, :NOPE],
        "w_uv": seg("w_uv").reshape(KV_LORA, HEAD_PAD)[:, :NOPE],
        "w_proj_attn": seg("w_pa").reshape(512, HEAD_PAD),
        "w_proj_conv": seg("w_pc"), "w_out": seg("w_out"),
    }


def _rope_tables(positions):
    inv_freq = 1.0 / (ROPE_THETA ** (jnp.arange(ROPE_HALF, dtype=F32) / ROPE_HALF))
    ang = positions.reshape(-1).astype(F32)[:, None] * inv_freq
    cos, sin = jnp.cos(ang), jnp.sin(ang)
    t = ang.shape[0]
    zero = jnp.zeros((t, ROPE_HALF), F32)
    head = jnp.ones((t, NOPE), F32)
    tail = jnp.zeros((t, HEAD_PAD - QK_DIM), F32)
    nohead = jnp.zeros((t, NOPE), F32)
    rc = jnp.concatenate([head, cos, cos, tail], axis=1)
    rs1 = jnp.concatenate([nohead, -sin, zero, tail], axis=1)
    rs2 = jnp.concatenate([nohead, zero, sin, tail], axis=1)
    return rc, rs1, rs2


def _local_step(x, positions, target, conv_w, small, ex):
    n_seq, seq, d = x.shape
    t = n_seq * seq
    x0 = x.reshape(t, d)
    tgt = target.reshape(t, d)
    rc, rs1, rs2 = _rope_tables(positions)
    ghq = _pad_cols(small["q_head_norm"], HEAD_PAD)
    ghk = _pad_cols(small["k_head_norm"], HEAD_PAD)
    TM, HC, TQ = 1024, 256, 1024

    def mm(*args, hosted=None, **kw):
        res = _mm(*args, hosted=hosted, **kw)
        return res if hosted is not None else (res, None)

    def wgrad(a, b, name, tm=None, hosted=None):
        return mm(a, b, mode="tn", out_dtype=BF16, tm=tm or a.shape[1], tn=b.shape[1], tk=512, name=name, hosted=hosted)

    f1g, f1u, f1d = ex.gather_now("ffn1")
    (x1, h1, a1, b1), got = _ffn_fwd(x0, small["ffn1_norm"], f1g, f1u, f1d, tm=512, hc=DFF // 2, name="ffn1_fwd",
                                     hosted=ex.gather_chips("mix_in"))
    hm, got = _rms_fwd(x1, small["mix_norm"], tm=TM, name="mix_norm_fwd", hosted=ex.gather_sibling(got))
    W = ex.mix_in_weights(got)
    lat = _mm(hm, W["latT"], mode="nt", out_dtype=BF16, tm=TM, tn=LAT_PAD, tk=D, name="proj_lat")
    conv3, got = mm(hm, W["convT"], mode="nt", out_dtype=BF16, tm=TM, tn=CONV_COLS // 2, tk=D, name="proj_conv",
                    hosted=ex.gather_chips("mix_misc"))
    gl, got = mm(hm, W["gateT"], mode="nt", out_dtype=BF16, tm=TM, tn=GATE_COLS // 2, tk=D, name="proj_gate",
                 hosted=ex.gather_sibling(got))
    W.update(ex.mix_misc_weights(got))
    q, k, v, qn, ckv = _mla_prep_fwd(lat, small["q_a_norm"], small["kv_a_norm"], ghq, ghk, W["wq"], W["wk"], W["wv"], rc, rs1, rs2,
                                     tm=512, name="mla_prep_fwd")
    (o, lse), got = _flash_fwd(q, k, v, n_seq=n_seq, seq=seq, tq=TQ, name="attn_fwd", hosted=ex.gather_chips("ffn2"))
    p = _conv_fwd(conv3, conv_w, n_seq=n_seq, seq=seq, name="conv_fwd")
    (x2, merged, ya, yb), got = _merge_fwd(o, p, gl, small["gate_bias"], x1, W["wpa"], W["wpc"], W["wout"], tm=512, name="merge_fwd",
                                           hosted=ex.gather_sibling(got))
    f2g, f2u, f2d = ex.ffn_weights(got)
    (dy, h2, a2, b2, loss_row), _ = _ffn_fwd(x2, small["ffn2_norm"], f2g, f2u, f2d, tm=512, hc=DFF // 2, name="ffn2_fwd", target=tgt)

    gw, gs = {}, {}
    (da2, db2, *ffn2_grads), _ = _ffn_grads(dy, h2, a2, b2, f2d, tm=TM, hc=HC, name="ffn2_grads")
    (dx2, gs["ffn2_norm"]), _ = _ffn_up_bwd(da2, db2, f2g, f2u, x2, small["ffn2_norm"], dy, tm=512, name="ffn2_up_bwd")

    (dx2b, dya, dyb, dgl, do, dp, gs["gate_bias"]), got = _merge_bwd(
        dx2, ya, yb, gl, small["gate_bias"], W["wpa"], W["wpc"], W["wout"], tm=512, name="merge_bwd",
        hosted=ex.scatter_sibling("ffn2", ffn2_grads))
    ex.scatter_sibling_done("ffn2", got)
    gw["wout"] = wgrad(merged, dx2b, "dw_out")[0]
    gw["wpa"] = wgrad(o, dya, "dw_pa")[0]
    gw["wpc"] = wgrad(p, dyb, "dw_pc")[0]
    dconv3, dconv_w = _conv_bwd(dp, conv3, conv_w, n_seq=n_seq, seq=seq, name="conv_bwd")
    (dq, dk, dv), got = _flash_bwd(q, k, v, o, lse, do, n_seq=n_seq, seq=seq, tq=TQ, name="attn_bwd",
                                   hosted=ex.scatter_chips("ffn2"))
    ex.scatter_chips_done("ffn2", got)
    dlat, dqp, dkp, gs["q_a_norm"], gs["kv_a_norm"], dghq, dghk = _mla_prep_bwd(
        dq, dk, dv, lat, qn, ckv, small["q_a_norm"], small["kv_a_norm"], ghq, ghk, W["wq"], W["wk"], W["wv"], rc, rs1, rs2,
        tm=512, name="mla_prep_bwd")
    gs["q_head_norm"], gs["k_head_norm"] = dghq[:, :QK_DIM], dghk[:, :QK_DIM]
    gw["wq"] = wgrad(qn, dqp, "dw_uq")[0]
    gw["wk"] = wgrad(ckv, dkp, "dw_uk")[0]
    gw["wv"] = wgrad(ckv, dv, "dw_uv")[0]
    gw["convT"] = wgrad(dconv3, hm, "dw_conv", tm=CONV_COLS // 2)[0]
    gw["gateT"] = wgrad(dgl, hm, "dw_gate")[0]
    gw["latT"] = wgrad(dlat, hm, "dw_lat")[0]
    ex.scatter_sibling_now("mix", gw)
    (dx1, gs["mix_norm"]), got = _proj_bwd(dlat, dconv3, dgl, W["latT"], W["convT"], W["gateT"], x1, small["mix_norm"], dx2,
                                           tm=512, name="proj_bwd", hosted=ex.scatter_chips("mix_in"))
    ex.scatter_chips_done("mix_in", got)

    (da1, db1, *ffn1_grads), got = _ffn_grads(dx1, h1, a1, b1, f1d, tm=TM, hc=HC, name="ffn1_grads",
                                              hosted=ex.scatter_chips("mix_misc"))
    ex.scatter_chips_done("mix_misc", got)
    ex.scatter_sibling_now("ffn1", ffn1_grads)
    (dx0, gs["ffn1_norm"]), got = _ffn_up_bwd(da1, db1, f1g, f1u, x0, small["ffn1_norm"], dx1, tm=512, name="ffn1_up_bwd",
                                              hosted=ex.scatter_chips("ffn1"))
    ex.scatter_chips_done("ffn1", got)
    return loss_row, dx0.reshape(n_seq, seq, d), dconv_w, gs


class _MeshExchange:
    def __init__(self, w, core, chip):
        self.w, self.core, self.chip = w, core, chip
        self.partial, self.received, self._packed = {}, {}, None

    def _blocks(self, group):
        w = self.w
        if group.startswith("ffn"):
            return [w[group + "_w_gate"].T.astype(BF16), w[group + "_w_up"].T.astype(BF16), w[group + "_w_down"].astype(BF16)]
        if self._packed is None:
            self._packed = _pack_weights(w)
        return [self._packed[0 if group == "mix_in" else 1]]

    def gather_chips(self, *groups):
        blocks = [b for group in groups for b in self._blocks(group)]
        return _gather_chips_plan(len(blocks)), blocks, _gather_shapes(blocks)

    def gather_sibling(self, got):
        half = list(got)
        return _gather_sibling_plan(len(half)), half, _same_shapes(half)

    def gather_now(self, group):
        plan, blocks, shapes = self.gather_chips(group)
        half = list(_run_plan(plan, blocks, shapes, name="gather_%s_chips" % group))
        return self.ffn_weights(_run_plan(_gather_sibling_plan(len(half)), half, _same_shapes(half), name="gather_%s_sibling" % group))

    def ffn_weights(self, got):
        return [a.reshape(DFF, D) for a in got]

    def mix_in_weights(self, got):
        return _unpack_in(got[0])

    def mix_misc_weights(self, got):
        return _unpack_misc(got[0])

    def _parts(self, group, grads):
        if group == "mix":
            return _pack_grads(grads), ["mix_in", "mix_misc"]
        parts = [g.reshape(N_DEV, -1, D) for g in grads]
        return parts, ([group] if len(parts) == 1 else None)

    def scatter_sibling(self, group, grads):
        self._sent, self._names = self._parts(group, grads)
        return _scatter_sibling_plan(len(self._sent)), self._sent, _halved_shapes(self._sent)

    def scatter_sibling_done(self, group, got):
        sums = [_sum_sibling(p, q, self.core, name="sum_%s_sibling_%d" % (group, i)) for i, (p, q) in enumerate(zip(self._sent, got))]
        if self._names is None:
            self.partial[group] = sums
        else:
            for n, s in zip(self._names, sums):
                self.partial[n] = [s]

    def scatter_sibling_now(self, group, grads):
        plan, parts, shapes = self.scatter_sibling(group, grads)
        self.scatter_sibling_done(group, _run_plan(plan, parts, shapes, name="scatter_%s_sibling" % group))

    def scatter_chips(self, group):
        s1 = self.partial[group]
        return _scatter_chips_plan(len(s1)), s1, _scatter_shapes(s1)

    def scatter_chips_done(self, group, got):
        self.received[group] = list(got)


SMALL_NAMES = ("ffn1_norm", "mix_norm", "gate_bias", "q_a_norm", "kv_a_norm", "q_head_norm", "k_head_norm", "ffn2_norm")
SMALL_SLOTS = {"ffn1_norm": 1024, "mix_norm": 1024, "gate_bias": 2048, "q_a_norm": 384, "kv_a_norm": 256, "q_head_norm": 128,
               "k_head_norm": 128, "ffn2_norm": 1024, "conv_w": 3072, "loss": 128}
COLUMN_MAJOR = ("w_in", "w_uq", "w_uk", "w_uv")
WEIGHT_NAMES = ("ffn1_norm", "ffn1_w_gate", "ffn1_w_up", "ffn1_w_down", "mix_norm", "w_in", "gate_bias", "q_a_norm", "w_uq",
                "kv_a_norm", "w_uk", "w_uv", "q_head_norm", "k_head_norm", "w_proj_attn", "conv_w", "w_proj_conv", "w_out",
                "ffn2_norm", "ffn2_w_gate", "ffn2_w_up", "ffn2_w_down")


def _step(x, positions, loss_target, w, m, v):
    xi, yi, ci = _place()
    core = ci.astype(jnp.int32).reshape(1)
    chip = (2 * xi + yi).astype(jnp.int32).reshape(1)
    me = 4 * xi + 2 * yi + ci

    cw_all = _small_exchange(jnp.pad(w["conv_w"], ((0, 5), (0, 0))), reduce=False, name="gather_conv_w")
    conv_w = cw_all[:, :3].transpose(1, 0, 2).reshape(3, D)
    small = {n: w[n].reshape(1, -1) for n in SMALL_NAMES}
    ex = _MeshExchange(w, core, chip)

    loss_row, grad_x, dconv_w, gs = _local_step(x, positions, loss_target, conv_w, small, ex)

    grads, deltas, new_m, new_v = {}, {}, {}, {}
    where = {"ffn1_w_gate": ("ffn1", 0), "ffn1_w_up": ("ffn1", 1), "ffn1_w_down": ("ffn1", 2),
             "ffn2_w_gate": ("ffn2", 0), "ffn2_w_up": ("ffn2", 1), "ffn2_w_down": ("ffn2", 2)}
    for n, (group, i) in where.items():
        transposed = not n.endswith("down")
        wv, mv, vv = (a[n].T if transposed else a[n] for a in (w, m, v))
        res = _sum_adamw(ex.partial[group][i], ex.received[group][i], chip, wv, mv, vv, name="adamw_" + n)
        grads[n], deltas[n], new_m[n], new_v[n] = (r.T if transposed else r for r in res)
    grads.update(_unpack_grads([_sum_chips(ex.partial[g][0], ex.received[g][0], chip, name="sum_%s_chips" % g)
                                for g in ("mix_in", "mix_misc")]))

    pieces = [_pad_cols(gs[n], SMALL_SLOTS[n]) for n in SMALL_NAMES] + [dconv_w.reshape(1, 3 * D), loss_row]
    total = _small_exchange(jnp.concatenate(pieces, axis=1).reshape(-1, 128), reduce=True, name="reduce_small").reshape(-1)
    off = 0
    for n in SMALL_NAMES:
        grads[n] = total[off:off + w[n].shape[0]]
        off += SMALL_SLOTS[n]
    conv_full = total[off:off + 3 * D].reshape(3, D)
    grads["conv_w"] = lax.dynamic_slice(conv_full, (0, me * HEAD_PAD), (3, HEAD_PAD))
    loss = total[off + 3 * D]

    for n in WEIGHT_NAMES:
        if n in deltas:
            continue
        shape = w[n].shape
        if n in COLUMN_MAJOR:
            ops = [a.T for a in (w[n], grads[n], m[n], v[n])]
            deltas[n], new_m[n], new_v[n] = (r.T for r in _adamw(*ops, name="adamw_" + n))
            continue
        if len(shape) == 1:
            view = (-1, 128) if shape[0] % 128 == 0 else (1, shape[0])
        else:
            view = shape
        dlt, nm, nv = _adamw(w[n].reshape(view), grads[n].reshape(view), m[n].reshape(view), v[n].reshape(view), name="adamw_" + n)
        deltas[n], new_m[n], new_v[n] = dlt.reshape(shape), nm.reshape(shape), nv.reshape(shape)
    return (loss, grad_x, *[grads[n] for n in WEIGHT_NAMES], *[deltas[n] for n in WEIGHT_NAMES],
            *[new_m[n] for n in WEIGHT_NAMES], *[new_v[n] for n in WEIGHT_NAMES])


def kernel(x, positions, ffn1_norm, ffn1_w_gate, ffn1_w_up, ffn1_w_down, mix_norm, w_in, gate_bias, q_a_norm, w_uq, kv_a_norm, w_uk, w_uv, q_head_norm, k_head_norm, w_proj_attn, conv_w, w_proj_conv, w_out, ffn2_norm, ffn2_w_gate, ffn2_w_up, ffn2_w_down, loss_target, m_ffn1_norm, m_ffn1_w_gate, m_ffn1_w_up, m_ffn1_w_down, m_mix_norm, m_w_in, m_gate_bias, m_q_a_norm, m_w_uq, m_kv_a_norm, m_w_uk, m_w_uv, m_q_head_norm, m_k_head_norm, m_w_proj_attn, m_conv_w, m_w_proj_conv, m_w_out, m_ffn2_norm, m_ffn2_w_gate, m_ffn2_w_up, m_ffn2_w_down, v_ffn1_norm, v_ffn1_w_gate, v_ffn1_w_up, v_ffn1_w_down, v_mix_norm, v_w_in, v_gate_bias, v_q_a_norm, v_w_uq, v_kv_a_norm, v_w_uk, v_w_uv, v_q_head_norm, v_k_head_norm, v_w_proj_attn, v_conv_w, v_w_proj_conv, v_w_out, v_ffn2_norm, v_ffn2_w_gate, v_ffn2_w_up, v_ffn2_w_down):
    given = dict(locals())
    w = {n: given[n] for n in WEIGHT_NAMES}
    m = {n: given["m_" + n] for n in WEIGHT_NAMES}
    v = {n: given["v_" + n] for n in WEIGHT_NAMES}
    return _step(x, positions, loss_target, w, m, v)
```

```python
import functools

import jax
import jax.numpy as jnp
from jax import lax
from jax.experimental import pallas as pl
from jax.experimental.pallas import tpu as pltpu

F32 = jnp.float32
BF16 = jnp.bfloat16
MESH = pl.DeviceIdType.MESH
ANY = pl.BlockSpec(memory_space=pl.ANY)

N_DEV = 8
D = 1024
DFF = 2816
N_HEADS = 8
HEAD_PAD = 128
QK_DIM = 96
NOPE = 64
ROPE_HALF = 16
Q_LORA = 384
KV_LORA = 256
LAT_PAD = 768
CONV_COLS = 3072
GATE_COLS = 2048
IN_DIM = 5792
IN_SHARD = IN_DIM // N_DEV
IN_SHARD_PAD = 736
FF_SHARD = DFF // N_DEV
ROPE_THETA = 10000.0
NORM_EPS = 1e-6
ATTN_SCALE = QK_DIM ** -0.5
NEG = -1e30

ADAM_LR, ADAM_B1, ADAM_B2, ADAM_EPS, ADAM_WD, ADAM_STEP = 0.001, 0.9, 0.999, 1e-08, 0.01, 10

PACK = ((("w_inT", IN_SHARD_PAD),), (("w_uq", 48), ("w_uk", 32), ("w_uv", 32), ("w_pa", 64), ("w_pc", 128), ("w_out", 128)))
PACK_OFF = {}
for _i, _group in enumerate(PACK):
    _o = 0
    for _n, _r in _group:
        PACK_OFF[_n] = (_i, _o, _r)
        _o += _r

VMEM_LIMIT = 56 * 1024 * 1024


def _params(*sem):
    return pltpu.CompilerParams(dimension_semantics=sem if sem else None, vmem_limit_bytes=VMEM_LIMIT)


class _Plan:
    def __init__(self, start, wait, n_remote, n_local, in_place=False):
        self.start, self.wait, self.n_remote, self.n_local, self.in_place = start, wait, n_remote, n_local, in_place

    def sems(self):
        return [pltpu.SemaphoreType.DMA((self.n_remote,)), pltpu.SemaphoreType.DMA((self.n_remote,)),
                pltpu.SemaphoreType.DMA((max(self.n_local, 1),))]


def _call(body, *, name, grid, in_specs, out_specs, out_shape, scratch_shapes, operands, sem, hosted=None):
    if hosted is None:
        outs = pl.pallas_call(body, name=name, grid=grid, in_specs=in_specs, out_specs=out_specs, out_shape=out_shape,
                              scratch_shapes=scratch_shapes, compiler_params=_params(*sem))(*operands)
        return outs, None
    plan, srcs, h_shapes = hosted
    n_in, n_out, n_scr, nh_in, nh_out = len(in_specs), len(out_specs), len(scratch_shapes), len(srcs), len(h_shapes)
    aliases = {n_in + a: n_out + a for a in range(nh_in)} if plan.in_place else {}

    def full_body(*refs):
        ins, refs = refs[:n_in], refs[n_in:]
        h_in, refs = refs[:nh_in], refs[nh_in:]
        outs, refs = refs[:n_out], refs[n_out:]
        h_out, refs = refs[:nh_out], refs[nh_out:]
        scr, sems = refs[:n_scr], refs[n_scr:]
        ids = [pl.program_id(ax) for ax in range(len(grid))]
        first = functools.reduce(jnp.logical_and, [i == 0 for i in ids])
        last = functools.reduce(jnp.logical_and, [i == g - 1 for i, g in zip(ids, grid)])

        @pl.when(first)
        def _():
            plan.start(h_in, h_out, *sems)

        body(*ins, *outs, *scr)

        @pl.when(last)
        def _():
            plan.wait(h_in, h_out, *sems)

    res = pl.pallas_call(
        full_body, name=name, grid=grid, in_specs=list(in_specs) + [ANY] * nh_in, out_specs=list(out_specs) + [ANY] * nh_out,
        out_shape=list(out_shape) + list(h_shapes), scratch_shapes=list(scratch_shapes) + plan.sems(),
        input_output_aliases=aliases, compiler_params=_params(*(["arbitrary"] * len(grid))),
    )(*operands, *srcs)
    return res[:n_out], res[n_out:]


def _dot_nn(a, b):
    return lax.dot_general(a, b, (((1,), (0,)), ((), ())), preferred_element_type=F32)


def _dot_nt(a, b):
    return lax.dot_general(a, b, (((1,), (1,)), ((), ())), preferred_element_type=F32)


def _dot_tn(a, b):
    return lax.dot_general(a, b, (((0,), (0,)), ((), ())), preferred_element_type=F32)


def _sigmoid(x):
    return 0.5 * jnp.tanh(0.5 * x) + 0.5


def _rms_stats(x):
    r = lax.rsqrt(jnp.mean(x * x, axis=-1, keepdims=True) + NORM_EPS)
    return x * r, r


ROWS_WIDE = 16
MM_ROWS = 256


def _rms_bwd(dy, xhat, r, g):
    dg = jnp.sum(dy * xhat, axis=0, keepdims=True)
    dxh = dy * g
    dx = r * (dxh - xhat * jnp.mean(dxh * xhat, axis=-1, keepdims=True))
    return dx, dg


def _mm(a, b, *, mode, out_dtype, tm, tn, tk, name, add=None, scale=1.0, hosted=None):
    if mode == "nn":
        (m, k), (_, n) = a.shape, b.shape
    elif mode == "nt":
        (m, k), (n, _) = a.shape, b.shape
    else:
        (k, m), (_, n) = a.shape, b.shape
    assert m % tm == 0 and n % tn == 0 and k % tk == 0, (name, m, n, k, tm, tn, tk)
    nk = k // tk
    dot = {"nn": _dot_nn, "nt": _dot_nt, "tn": _dot_tn}[mode]
    a_spec = pl.BlockSpec((tk, tm), lambda i, j, kk: (kk, i)) if mode == "tn" else pl.BlockSpec((tm, tk), lambda i, j, kk: (i, kk))
    b_spec = pl.BlockSpec((tn, tk), lambda i, j, kk: (j, kk)) if mode == "nt" else pl.BlockSpec((tk, tn), lambda i, j, kk: (kk, j))
    o_spec = pl.BlockSpec((tm, tn), lambda i, j, kk: (i, j))
    has_add = add is not None

    def finish(prod, c_ref, o_ref):
        if scale != 1.0:
            prod = prod * scale
        o_ref[...] = ((c_ref[...] + prod) if has_add else prod).astype(out_dtype)

    def body(*refs):
        a_ref, b_ref = refs[:2]
        c_ref = refs[2] if has_add else None
        o_ref = refs[3] if has_add else refs[2]
        if nk == 1:
            finish(dot(a_ref[...], b_ref[...]), c_ref, o_ref)
            return
        acc_ref = refs[-1]
        kk = pl.program_id(2)

        @pl.when(kk == 0)
        def _():
            acc_ref[...] = jnp.zeros_like(acc_ref)

        acc_ref[...] += dot(a_ref[...], b_ref[...])

        @pl.when(kk == nk - 1)
        def _():
            finish(acc_ref[...], c_ref, o_ref)

    operands = (a, b, add) if has_add else (a, b)
    in_specs = [a_spec, b_spec] + ([o_spec] if has_add else [])
    (out,), got = _call(
        body, name=name, grid=(m // tm, n // tn, nk), in_specs=in_specs, out_specs=[o_spec],
        out_shape=[jax.ShapeDtypeStruct((m, n), out_dtype)], scratch_shapes=[pltpu.VMEM((tm, tn), F32)] if nk > 1 else [],
        operands=operands, sem=("parallel", "parallel", "arbitrary"), hosted=hosted)
    return out if hosted is None else (out, got)


def _rms_fwd(x, g, *, tm, name, hosted=None):
    t, d = x.shape

    def body(x_ref, g_ref, h_ref):
        xhat, _ = _rms_stats(x_ref[...])
        h_ref[...] = (xhat * g_ref[...]).astype(BF16)

    (h,), got = _call(
        body, name=name, grid=(t // tm,),
        in_specs=[pl.BlockSpec((tm, d), lambda i: (i, 0)), pl.BlockSpec((1, d), lambda i: (0, 0))],
        out_specs=[pl.BlockSpec((tm, d), lambda i: (i, 0))], out_shape=[jax.ShapeDtypeStruct((t, d), BF16)], scratch_shapes=[],
        operands=(x, g), sem=("parallel",), hosted=hosted)
    return h, got


def _ffn_fwd(x, g, wgT, wuT, wd, *, tm, hc, name, hosted=None, target=None):
    t, d = x.shape
    nj = DFF // hc
    with_loss = target is not None

    def body(*refs):
        x_ref, g_ref, wg_ref, wu_ref, wd_ref = refs[:5]
        t_ref = refs[5] if with_loss else None
        xo_ref, h_ref, a_ref, b_ref = refs[5 + with_loss:9 + with_loss]
        loss_ref = refs[9 + with_loss] if with_loss else None
        acc_ref = refs[-1]
        i, j = pl.program_id(0), pl.program_id(1)

        @pl.when(j == 0)
        def _():
            xhat, _ = _rms_stats(x_ref[...])
            h_ref[...] = (xhat * g_ref[...]).astype(BF16)
            acc_ref[...] = jnp.zeros_like(acc_ref)

        h = h_ref[...]
        a = _dot_nt(h, wg_ref[...])
        b = _dot_nt(h, wu_ref[...])
        a_ref[...] = a.astype(BF16)
        b_ref[...] = b.astype(BF16)
        s = (a * _sigmoid(a) * b).astype(BF16)
        acc_ref[...] += _dot_nn(s, wd_ref[...])

        if with_loss:
            @pl.when((i == 0) & (j == 0))
            def _():
                loss_ref[...] = jnp.zeros_like(loss_ref)

        @pl.when(j == nj - 1)
        def _():
            y = x_ref[...] + 0.5 * acc_ref[...]
            if with_loss:
                err = y - t_ref[...]
                xo_ref[...] = err * (1.0 / d)
                loss_ref[...] += jnp.sum(jnp.sum(err * err, axis=-1, keepdims=True), axis=0, keepdims=True) * (0.5 / d)
            else:
                xo_ref[...] = y

    row = pl.BlockSpec((tm, d), lambda i, j: (i, 0))
    vec = pl.BlockSpec((1, d), lambda i, j: (0, 0))
    wsp = pl.BlockSpec((hc, d), lambda i, j: (j, 0))
    hid = pl.BlockSpec((tm, hc), lambda i, j: (i, j))
    out_specs = [row, row, hid, hid] + ([pl.BlockSpec((1, 128), lambda i, j: (0, 0))] if with_loss else [])
    out_shape = [jax.ShapeDtypeStruct((t, d), F32), jax.ShapeDtypeStruct((t, d), BF16), jax.ShapeDtypeStruct((t, DFF), BF16),
                 jax.ShapeDtypeStruct((t, DFF), BF16)] + ([jax.ShapeDtypeStruct((1, 128), F32)] if with_loss else [])
    return _call(
        body, name=name, grid=(t // tm, nj), in_specs=[row, vec, wsp, wsp, wsp] + ([row] if with_loss else []),
        out_specs=out_specs, out_shape=out_shape, scratch_shapes=[pltpu.VMEM((tm, d), F32)],
        operands=(x, g, wgT, wuT, wd) + ((target,) if with_loss else ()),
        sem=("arbitrary" if with_loss else "parallel", "arbitrary"), hosted=hosted)


def _ffn_grads(dout, h, a, b, wd, *, tm, hc, name, hosted=None):
    t, d = dout.shape
    ni, nj = t // tm, DFF // hc

    def body(dout_ref, h_ref, a_ref, b_ref, wd_ref, da_ref, db_ref, dwg_ref, dwu_ref, dwd_ref,
             dy_all, h_all, ds_scr, s_scr, acc_g, acc_u, acc_d):
        j, i = pl.program_id(0), pl.program_id(1)
        rows_i = pl.ds(pl.multiple_of(i * tm, tm), tm)

        @pl.when(j == 0)
        def _():
            dy_all[rows_i, :] = (0.5 * dout_ref[...]).astype(BF16)
            h_all[rows_i, :] = h_ref[...]

        @pl.when(i == 0)
        def _():
            acc_g[...] = jnp.zeros_like(acc_g)
            acc_u[...] = jnp.zeros_like(acc_u)
            acc_d[...] = jnp.zeros_like(acc_d)

        def grad_rows(rows):
            ds = ds_scr[rows, :]
            av = a_ref[rows, :].astype(F32)
            bv = b_ref[rows, :].astype(F32)
            sg = _sigmoid(av)
            sl = av * sg
            s_scr[rows, :] = (sl * bv).astype(BF16)
            da_ref[rows, :] = (ds * bv * (sg + sl * (1.0 - sg))).astype(BF16)
            db_ref[rows, :] = (ds * sl).astype(BF16)

        for blk in range(tm // MM_ROWS):
            rs = slice(blk * MM_ROWS, (blk + 1) * MM_ROWS)
            ds_scr[rs, :] = _dot_nt(dy_all[pl.ds(pl.multiple_of(i * tm + blk * MM_ROWS, MM_ROWS), MM_ROWS), :], wd_ref[...])
            for c in range(MM_ROWS // ROWS_WIDE):
                grad_rows(slice(blk * MM_ROWS + c * ROWS_WIDE, blk * MM_ROWS + (c + 1) * ROWS_WIDE))

        dy_i = dy_all[rows_i, :]
        h_i = h_all[rows_i, :]
        acc_d[...] += _dot_tn(s_scr[...], dy_i)
        acc_g[...] += _dot_tn(da_ref[...], h_i)
        acc_u[...] += _dot_tn(db_ref[...], h_i)

        @pl.when(i == ni - 1)
        def _():
            dwg_ref[...] = acc_g[...].astype(BF16)
            dwu_ref[...] = acc_u[...].astype(BF16)
            dwd_ref[...] = acc_d[...].astype(BF16)

    first = pl.BlockSpec((tm, d), lambda j, i: (jnp.where(j == 0, i, 0), 0))
    hid = pl.BlockSpec((tm, hc), lambda j, i: (i, j))
    wsp = pl.BlockSpec((hc, d), lambda j, i: (j, 0))
    hid_shape = jax.ShapeDtypeStruct((t, DFF), BF16)
    w_shape = jax.ShapeDtypeStruct((DFF, d), BF16)
    return _call(
        body, name=name, grid=(nj, ni), in_specs=[first, first, hid, hid, wsp], out_specs=[hid, hid, wsp, wsp, wsp],
        out_shape=[hid_shape, hid_shape, w_shape, w_shape, w_shape],
        scratch_shapes=[pltpu.VMEM((t, d), BF16), pltpu.VMEM((t, d), BF16), pltpu.VMEM((tm, hc), F32), pltpu.VMEM((tm, hc), BF16),
                        pltpu.VMEM((hc, d), F32), pltpu.VMEM((hc, d), F32), pltpu.VMEM((hc, d), F32)],
        operands=(dout, h, a, b, wd), sem=("arbitrary", "arbitrary"), hosted=hosted)


def _proj_fwd(h, latT, convT, gateT, *, tm, name, hosted=None):
    t, d = h.shape

    def body(h_ref, wl_ref, wc_ref, wg_ref, lat_ref, conv_ref, gl_ref):
        hv = h_ref[...]
        lat_ref[...] = _dot_nt(hv, wl_ref[...]).astype(BF16)
        conv_ref[...] = _dot_nt(hv, wc_ref[...]).astype(BF16)
        gl_ref[...] = _dot_nt(hv, wg_ref[...]).astype(BF16)

    def rows(w):
        return pl.BlockSpec((tm, w), lambda i: (i, 0))

    def full(r):
        return pl.BlockSpec((r, d), lambda i: (0, 0))

    return _call(
        body, name=name, grid=(t // tm,), in_specs=[rows(d), full(LAT_PAD), full(CONV_COLS), full(GATE_COLS)],
        out_specs=[rows(LAT_PAD), rows(CONV_COLS), rows(GATE_COLS)],
        out_shape=[jax.ShapeDtypeStruct((t, LAT_PAD), BF16), jax.ShapeDtypeStruct((t, CONV_COLS), BF16),
                   jax.ShapeDtypeStruct((t, GATE_COLS), BF16)],
        scratch_shapes=[], operands=(h, latT, convT, gateT), sem=("parallel",), hosted=hosted)


def _proj_bwd(dlat, dconv3, dgl, latT, convT, gateT, x, g, dres, *, tm, name, hosted=None):
    t, d = x.shape

    def body(dl_ref, dc_ref, dg_ref, wl_ref, wc_ref, wg_ref, x_ref, g_ref, dres_ref, dx_ref, dgain_ref):
        @pl.when(pl.program_id(0) == 0)
        def _():
            dgain_ref[...] = jnp.zeros_like(dgain_ref)

        dh = _dot_nn(dl_ref[...], wl_ref[...]) + _dot_nn(dc_ref[...], wc_ref[...]) + _dot_nn(dg_ref[...], wg_ref[...])
        xhat, r = _rms_stats(x_ref[...])
        dx, dgain = _rms_bwd(dh, xhat, r, g_ref[...])
        dx_ref[...] = dres_ref[...] + dx
        dgain_ref[...] += dgain

    def rows(w):
        return pl.BlockSpec((tm, w), lambda i: (i, 0))

    def full(r):
        return pl.BlockSpec((r, d), lambda i: (0, 0))

    return _call(
        body, name=name, grid=(t // tm,),
        in_specs=[rows(LAT_PAD), rows(CONV_COLS), rows(GATE_COLS), full(LAT_PAD), full(CONV_COLS), full(GATE_COLS), rows(d), full(1), rows(d)],
        out_specs=[rows(d), full(1)], out_shape=[jax.ShapeDtypeStruct((t, d), F32), jax.ShapeDtypeStruct((1, d), F32)],
        scratch_shapes=[], operands=(dlat, dconv3, dgl, latT, convT, gateT, x, g, dres), sem=("arbitrary",), hosted=hosted)


def _ffn_up_bwd(da, db, wgT, wuT, x, g, dout, *, tm, name, hosted=None):
    t, d = x.shape

    def body(da_ref, db_ref, wg_ref, wu_ref, x_ref, g_ref, dout_ref, dx_ref, dg_ref):
        @pl.when(pl.program_id(0) == 0)
        def _():
            dg_ref[...] = jnp.zeros_like(dg_ref)

        dh = _dot_nn(da_ref[...], wg_ref[...]) + _dot_nn(db_ref[...], wu_ref[...])
        xhat, r = _rms_stats(x_ref[...])
        dx, dg = _rms_bwd(dh, xhat, r, g_ref[...])
        dx_ref[...] = dout_ref[...] + dx
        dg_ref[...] += dg

    row = pl.BlockSpec((tm, d), lambda i: (i, 0))
    vec = pl.BlockSpec((1, d), lambda i: (0, 0))
    hid = pl.BlockSpec((tm, DFF), lambda i: (i, 0))
    wsp = pl.BlockSpec((DFF, d), lambda i: (0, 0))
    return _call(
        body, name=name, grid=(t // tm,), in_specs=[hid, hid, wsp, wsp, row, vec, row], out_specs=[row, vec],
        out_shape=[jax.ShapeDtypeStruct((t, d), F32), jax.ShapeDtypeStruct((1, d), F32)], scratch_shapes=[],
        operands=(da, db, wgT, wuT, x, g, dout), sem=("arbitrary",), hosted=hosted)


def _rope_fwd(x, c, s1, s2):
    return x * c + pltpu.roll(x, HEAD_PAD - ROPE_HALF, 1) * s1 + pltpu.roll(x, ROPE_HALF, 1) * s2


def _rope_bwd(dy, c, s1, s2):
    return dy * c + pltpu.roll(dy * s1, ROPE_HALF, 1) + pltpu.roll(dy * s2, HEAD_PAD - ROPE_HALF, 1)


def _head_stats(x):
    r = lax.rsqrt(jnp.sum(x * x, axis=-1, keepdims=True) * (1.0 / QK_DIM) + NORM_EPS)
    return x * r, r


def _mla_prep_fwd(lat, gq, gkv, ghq, ghk, wq, wk, wv, rc, rs1, rs2, *, tm, name):
    t = lat.shape[0]

    def body(lat_ref, gq_ref, gkv_ref, ghq_ref, ghk_ref, wq_ref, wk_ref, wv_ref, c_ref, s1_ref, s2_ref,
             q_ref, k_ref, v_ref, qn_ref, ckv_ref):
        lat_v = lat_ref[...]
        qhat, _ = _rms_stats(lat_v[:, :Q_LORA].astype(F32))
        qn = (qhat * gq_ref[...]).astype(BF16)
        khat, _ = _rms_stats(lat_v[:, Q_LORA:Q_LORA + KV_LORA].astype(F32))
        ckv = (khat * gkv_ref[...]).astype(BF16)
        ckv_ext = jnp.concatenate([ckv, lat_v[:, Q_LORA + KV_LORA:]], axis=1)
        qn_ref[...] = qn
        ckv_ref[...] = ckv_ext
        q_pre = _dot_nn(qn, wq_ref[...])
        k_pre = _dot_nn(ckv_ext, wk_ref[...])
        v_ref[...] = _dot_nn(ckv, wv_ref[...]).astype(BF16)
        c, s1, s2 = c_ref[...], s1_ref[...], s2_ref[...]
        for h in range(N_HEADS):
            hs = slice(h * HEAD_PAD, (h + 1) * HEAD_PAD)
            xq, _ = _head_stats(q_pre[:, hs])
            q_ref[:, hs] = _rope_fwd(xq * ghq_ref[...], c, s1, s2).astype(BF16)
            xk, _ = _head_stats(k_pre[:, hs])
            k_ref[:, hs] = _rope_fwd(xk * ghk_ref[...], c, s1, s2).astype(BF16)

    def row(w):
        return pl.BlockSpec((tm, w), lambda i: (i, 0))

    def full(r, w):
        return pl.BlockSpec((r, w), lambda i: (0, 0))

    wide = jax.ShapeDtypeStruct((t, D), BF16)
    lat3 = jax.ShapeDtypeStruct((t, Q_LORA), BF16)
    return pl.pallas_call(
        body, name=name, grid=(t // tm,),
        in_specs=[row(LAT_PAD), full(1, Q_LORA), full(1, KV_LORA), full(1, HEAD_PAD), full(1, HEAD_PAD),
                  full(Q_LORA, D), full(Q_LORA, D), full(KV_LORA, D), row(HEAD_PAD), row(HEAD_PAD), row(HEAD_PAD)],
        out_specs=[row(D), row(D), row(D), row(Q_LORA), row(Q_LORA)],
        out_shape=[wide, wide, wide, lat3, lat3],
        compiler_params=_params("parallel"),
    )(lat, gq, gkv, ghq, ghk, wq, wk, wv, rc, rs1, rs2)


def _mla_prep_bwd(dq, dk, dv, lat, qn, ckv_ext, gq, gkv, ghq, ghk, wq, wk, wv, rc, rs1, rs2, *, tm, name):
    t = lat.shape[0]

    def body(dq_ref, dk_ref, dv_ref, lat_ref, qn_ref, ckv_ref, gq_ref, gkv_ref, ghq_ref, ghk_ref, wq_ref, wk_ref, wv_ref,
             c_ref, s1_ref, s2_ref, dlat_ref, dqp_ref, dkp_ref, dgq_ref, dgkv_ref, dghq_ref, dghk_ref):
        @pl.when(pl.program_id(0) == 0)
        def _():
            dgq_ref[...] = jnp.zeros_like(dgq_ref)
            dgkv_ref[...] = jnp.zeros_like(dgkv_ref)
            dghq_ref[...] = jnp.zeros_like(dghq_ref)
            dghk_ref[...] = jnp.zeros_like(dghk_ref)

        c, s1, s2 = c_ref[...], s1_ref[...], s2_ref[...]
        q_pre = _dot_nn(qn_ref[...], wq_ref[...])
        k_pre = _dot_nn(ckv_ref[...], wk_ref[...])

        def heads(pre, dy_ref, gh_ref, dgh_ref, out_ref):
            dgh = jnp.zeros((1, HEAD_PAD), F32)
            for h in range(N_HEADS):
                hs = slice(h * HEAD_PAD, (h + 1) * HEAD_PAD)
                d = _rope_bwd(dy_ref[:, hs].astype(F32), c, s1, s2)
                xhat, r = _head_stats(pre[:, hs])
                dgh = dgh + jnp.sum(d * xhat, axis=0, keepdims=True)
                dxh = d * gh_ref[...]
                dx = r * (dxh - xhat * (jnp.sum(dxh * xhat, axis=-1, keepdims=True) * (1.0 / QK_DIM)))
                out_ref[:, hs] = dx.astype(BF16)
            dgh_ref[...] += dgh

        heads(q_pre, dq_ref, ghq_ref, dghq_ref, dqp_ref)
        heads(k_pre, dk_ref, ghk_ref, dghk_ref, dkp_ref)
        dqn = _dot_nt(dqp_ref[...], wq_ref[...])
        dce = _dot_nt(dkp_ref[...], wk_ref[...])
        dckv = dce[:, :KV_LORA] + _dot_nt(dv_ref[...], wv_ref[...])
        lat_v = lat_ref[...]
        qhat, rq = _rms_stats(lat_v[:, :Q_LORA].astype(F32))
        dql, dgq = _rms_bwd(dqn, qhat, rq, gq_ref[...])
        khat, rk = _rms_stats(lat_v[:, Q_LORA:Q_LORA + KV_LORA].astype(F32))
        dkl, dgkv = _rms_bwd(dckv, khat, rk, gkv_ref[...])
        dgq_ref[...] += dgq
        dgkv_ref[...] += dgkv
        dlat_ref[...] = jnp.concatenate([dql, dkl, dce[:, KV_LORA:]], axis=1).astype(BF16)

    def row(w):
        return pl.BlockSpec((tm, w), lambda i: (i, 0))

    def full(r, w):
        return pl.BlockSpec((r, w), lambda i: (0, 0))

    return pl.pallas_call(
        body, name=name, grid=(t // tm,),
        in_specs=[row(D), row(D), row(D), row(LAT_PAD), row(Q_LORA), row(Q_LORA), full(1, Q_LORA), full(1, KV_LORA),
                  full(1, HEAD_PAD), full(1, HEAD_PAD), full(Q_LORA, D), full(Q_LORA, D), full(KV_LORA, D),
                  row(HEAD_PAD), row(HEAD_PAD), row(HEAD_PAD)],
        out_specs=[row(LAT_PAD), row(D), row(D), full(1, Q_LORA), full(1, KV_LORA), full(1, HEAD_PAD), full(1, HEAD_PAD)],
        out_shape=[jax.ShapeDtypeStruct((t, LAT_PAD), BF16), jax.ShapeDtypeStruct((t, D), BF16), jax.ShapeDtypeStruct((t, D), BF16),
                   jax.ShapeDtypeStruct((1, Q_LORA), F32), jax.ShapeDtypeStruct((1, KV_LORA), F32),
                   jax.ShapeDtypeStruct((1, HEAD_PAD), F32), jax.ShapeDtypeStruct((1, HEAD_PAD), F32)],
        compiler_params=_params("arbitrary"),
    )(dq, dk, dv, lat, qn, ckv_ext, gq, gkv, ghq, ghk, wq, wk, wv, rc, rs1, rs2)


def _causal_keep(tq):
    r = lax.broadcasted_iota(jnp.int32, (tq, tq), 0)
    c = lax.broadcasted_iota(jnp.int32, (tq, tq), 1)
    return c <= r


def _flash_fwd(q, k, v, *, n_seq, seq, tq, name, hosted=None):
    nq = seq // tq

    def body(q_ref, k_ref, v_ref, o_ref, lse_ref):
        qi = pl.program_id(2)
        qv = q_ref[...]

        def step(j, carry, masked):
            m, l, acc = carry
            kj = k_ref[pl.ds(pl.multiple_of(j * tq, tq), tq), :]
            vj = v_ref[pl.ds(pl.multiple_of(j * tq, tq), tq), :]
            s = _dot_nt(qv, kj) * ATTN_SCALE
            if masked:
                s = jnp.where(_causal_keep(tq), s, NEG)
            m_new = jnp.maximum(m, jnp.max(s, axis=-1, keepdims=True))
            alpha = jnp.exp(m - m_new)
            p = jnp.exp(s - m_new)
            l = alpha * l + jnp.sum(p, axis=-1, keepdims=True)
            acc = alpha * acc + _dot_nn(p.astype(BF16), vj)
            return m_new, l, acc

        init = (jnp.full((tq, 1), NEG, F32), jnp.zeros((tq, 1), F32), jnp.zeros((tq, HEAD_PAD), F32))
        carry = lax.fori_loop(0, qi, lambda j, cr: step(j, cr, False), init)
        m, l, acc = step(qi, carry, True)
        o_ref[...] = (acc / l).astype(BF16)
        lse_ref[...] = jnp.broadcast_to(m + jnp.log(l), (tq, HEAD_PAD))

    qspec = pl.BlockSpec((tq, HEAD_PAD), lambda b, h, i: (b * nq + i, h))
    kspec = pl.BlockSpec((seq, HEAD_PAD), lambda b, h, i: (b, h))
    t = n_seq * seq
    return _call(
        body, name=name, grid=(n_seq, N_HEADS, nq), in_specs=[qspec, kspec, kspec], out_specs=[qspec, qspec],
        out_shape=[jax.ShapeDtypeStruct((t, D), BF16), jax.ShapeDtypeStruct((t, D), F32)], scratch_shapes=[],
        operands=(q, k, v), sem=("parallel", "parallel", "arbitrary"), hosted=hosted)


def _flash_bwd(q, k, v, o, lse, do, *, n_seq, seq, tq, name, hosted=None):
    nq = seq // tq

    def body(q_ref, k_ref, v_ref, o_ref, lse_ref, do_ref, dq_ref, dk_ref, dv_ref, dk_acc, dv_acc):
        j = pl.program_id(2)

        @pl.when(j == 0)
        def _():
            dq_ref[...] = jnp.zeros_like(dq_ref)

        dk_acc[...] = jnp.zeros_like(dk_acc)
        dv_acc[...] = jnp.zeros_like(dv_acc)
        kv = k_ref[...]
        vv = v_ref[...]

        def step(i, masked):
            rows = pl.ds(pl.multiple_of(i * tq, tq), tq)
            qi = q_ref[rows, :]
            doi = do_ref[rows, :]
            delta = jnp.sum(doi.astype(F32) * o_ref[rows, :].astype(F32), axis=-1, keepdims=True)
            s = _dot_nt(qi, kv) * ATTN_SCALE
            p = jnp.exp(s - lse_ref[rows, :][:, :1])
            if masked:
                p = jnp.where(_causal_keep(tq), p, 0.0)
            dv_acc[...] += _dot_tn(p.astype(BF16), doi)
            dp = _dot_nt(doi, vv)
            ds = (p * (dp - delta) * ATTN_SCALE).astype(BF16)
            dk_acc[...] += _dot_tn(ds, qi)
            dq_ref[rows, :] += _dot_nn(ds, kv)

        step(j, True)

        def loop_body(i, carry):
            step(i, False)
            return carry

        lax.fori_loop(j + 1, nq, loop_body, 0)
        dk_ref[...] = dk_acc[...]
        dv_ref[...] = dv_acc[...].astype(BF16)

    full = pl.BlockSpec((seq, HEAD_PAD), lambda b, h, j: (b, h))
    tile = pl.BlockSpec((tq, HEAD_PAD), lambda b, h, j: (b * nq + j, h))
    t = n_seq * seq
    return _call(
        body, name=name, grid=(n_seq, N_HEADS, nq), in_specs=[full, tile, tile, full, full, full],
        out_specs=[full, tile, tile],
        out_shape=[jax.ShapeDtypeStruct((t, D), F32), jax.ShapeDtypeStruct((t, D), F32), jax.ShapeDtypeStruct((t, D), BF16)],
        scratch_shapes=[pltpu.VMEM((tq, HEAD_PAD), F32), pltpu.VMEM((tq, HEAD_PAD), F32)],
        operands=(q, k, v, o, lse, do), sem=("parallel", "parallel", "arbitrary"), hosted=hosted)


CONV_CB = 256


def _shift_down(u, k, row):
    return jnp.where(row >= k, pltpu.roll(u, k, 0), 0.0)


def _shift_up(u, k, row, n):
    return jnp.where(row < n - k, pltpu.roll(u, n - k, 0), 0.0)


def _conv_fwd(conv3, cw, *, n_seq, seq, name, hosted=None):
    def body(c_ref, w_ref, p_ref):
        blk = c_ref[...].astype(F32)
        xc, gb, gc = blk[:, :CONV_CB], blk[:, CONV_CB:2 * CONV_CB], blk[:, 2 * CONV_CB:]
        row = lax.broadcasted_iota(jnp.int32, (seq, CONV_CB), 0)
        u = gc * xc
        z = w_ref[0:1, :] * _shift_down(u, 2, row) + w_ref[1:2, :] * _shift_down(u, 1, row) + w_ref[2:3, :] * u
        p_ref[...] = (gb * z).astype(BF16)

    (p,), got = _call(
        body, name=name, grid=(n_seq, D // CONV_CB),
        in_specs=[pl.BlockSpec((seq, 3 * CONV_CB), lambda b, j: (b, j)), pl.BlockSpec((3, CONV_CB), lambda b, j: (0, j))],
        out_specs=[pl.BlockSpec((seq, CONV_CB), lambda b, j: (b, j))],
        out_shape=[jax.ShapeDtypeStruct((n_seq * seq, D), BF16)], scratch_shapes=[],
        operands=(conv3, cw), sem=("parallel", "parallel"), hosted=hosted)
    return p, got


def _conv_bwd(dp, conv3, cw, *, n_seq, seq, name):
    def body(dp_ref, c_ref, w_ref, dc_ref, dw_ref):
        @pl.when(pl.program_id(1) == 0)
        def _():
            dw_ref[...] = jnp.zeros_like(dw_ref)

        blk = c_ref[...].astype(F32)
        xc, gb, gc = blk[:, :CONV_CB], blk[:, CONV_CB:2 * CONV_CB], blk[:, 2 * CONV_CB:]
        row = lax.broadcasted_iota(jnp.int32, (seq, CONV_CB), 0)
        w0, w1, w2 = w_ref[0:1, :], w_ref[1:2, :], w_ref[2:3, :]
        u = gc * xc
        u1 = _shift_down(u, 1, row)
        u2 = _shift_down(u, 2, row)
        z = w0 * u2 + w1 * u1 + w2 * u
        dpv = dp_ref[...].astype(F32)
        dz = dpv * gb
        du = w2 * dz + w1 * _shift_up(dz, 1, row, seq) + w0 * _shift_up(dz, 2, row, seq)
        dc_ref[...] = jnp.concatenate([du * gc, dpv * z, du * xc], axis=1).astype(BF16)
        dw_ref[0:1, :] += jnp.sum(dz * u2, axis=0, keepdims=True)
        dw_ref[1:2, :] += jnp.sum(dz * u1, axis=0, keepdims=True)
        dw_ref[2:3, :] += jnp.sum(dz * u, axis=0, keepdims=True)

    return pl.pallas_call(
        body, name=name, grid=(D // CONV_CB, n_seq),
        in_specs=[pl.BlockSpec((seq, CONV_CB), lambda j, b: (b, j)), pl.BlockSpec((seq, 3 * CONV_CB), lambda j, b: (b, j)),
                  pl.BlockSpec((3, CONV_CB), lambda j, b: (0, j))],
        out_specs=[pl.BlockSpec((seq, 3 * CONV_CB), lambda j, b: (b, j)), pl.BlockSpec((3, CONV_CB), lambda j, b: (0, j))],
        out_shape=[jax.ShapeDtypeStruct((n_seq * seq, CONV_COLS), BF16), jax.ShapeDtypeStruct((3, D), F32)],
        compiler_params=_params("parallel", "arbitrary"),
    )(dp, conv3, cw)


def _merge_fwd(o, p, gl, bias, x1, wpa, wpc, wout, *, tm, name, hosted=None):
    t = x1.shape[0]

    def body(o_ref, p_ref, gl_ref, b_ref, x_ref, wpa_ref, wpc_ref, wout_ref, x2_ref, mg_ref, ya_ref, yb_ref):
        ya = _dot_nn(o_ref[...], wpa_ref[...])
        yb = _dot_nn(p_ref[...], wpc_ref[...])
        gates = _sigmoid(gl_ref[...].astype(F32) + b_ref[...])
        merged = (gates[:, :D] * ya + gates[:, D:] * yb).astype(BF16)
        ya_ref[...] = ya.astype(BF16)
        yb_ref[...] = yb.astype(BF16)
        mg_ref[...] = merged
        x2_ref[...] = x_ref[...] + _dot_nn(merged, wout_ref[...])

    row = pl.BlockSpec((tm, D), lambda i: (i, 0))
    row2 = pl.BlockSpec((tm, GATE_COLS), lambda i: (i, 0))
    wsp = pl.BlockSpec((D, D), lambda i: (0, 0))
    wide = jax.ShapeDtypeStruct((t, D), BF16)
    return _call(
        body, name=name, grid=(t // tm,),
        in_specs=[row, row, row2, pl.BlockSpec((1, GATE_COLS), lambda i: (0, 0)), row, wsp, wsp, wsp],
        out_specs=[row, row, row, row], out_shape=[jax.ShapeDtypeStruct((t, D), F32), wide, wide, wide], scratch_shapes=[],
        operands=(o, p, gl, bias, x1, wpa, wpc, wout), sem=("parallel",), hosted=hosted)


def _merge_bwd(dx2, ya, yb, gl, bias, wpa, wpc, wout, *, tm, name, hosted=None):
    t = dx2.shape[0]

    def body(dx_ref, ya_ref, yb_ref, gl_ref, b_ref, wpa_ref, wpc_ref, wout_ref,
             dxb_ref, dya_ref, dyb_ref, dgl_ref, do_ref, dp_ref, db_ref):
        @pl.when(pl.program_id(0) == 0)
        def _():
            db_ref[...] = jnp.zeros_like(db_ref)

        dxb = dx_ref[...].astype(BF16)
        dxb_ref[...] = dxb
        dm = _dot_nt(dxb, wout_ref[...])
        gates = _sigmoid(gl_ref[...].astype(F32) + b_ref[...])
        ga, gb = gates[:, :D], gates[:, D:]
        dya = (dm * ga).astype(BF16)
        dyb = (dm * gb).astype(BF16)
        dya_ref[...] = dya
        dyb_ref[...] = dyb
        dgl = jnp.concatenate([dm * ya_ref[...].astype(F32) * ga * (1.0 - ga),
                               dm * yb_ref[...].astype(F32) * gb * (1.0 - gb)], axis=1)
        dgl_ref[...] = dgl.astype(BF16)
        db_ref[...] += jnp.sum(dgl, axis=0, keepdims=True)
        do_ref[...] = _dot_nt(dya, wpa_ref[...]).astype(BF16)
        dp_ref[...] = _dot_nt(dyb, wpc_ref[...]).astype(BF16)

    row = pl.BlockSpec((tm, D), lambda i: (i, 0))
    row2 = pl.BlockSpec((tm, GATE_COLS), lambda i: (i, 0))
    vec2 = pl.BlockSpec((1, GATE_COLS), lambda i: (0, 0))
    wsp = pl.BlockSpec((D, D), lambda i: (0, 0))
    wide = jax.ShapeDtypeStruct((t, D), BF16)
    return _call(
        body, name=name, grid=(t // tm,), in_specs=[row, row, row, row2, vec2, wsp, wsp, wsp],
        out_specs=[row, row, row, row2, row, row, vec2],
        out_shape=[wide, wide, wide, jax.ShapeDtypeStruct((t, GATE_COLS), BF16), wide, wide,
                   jax.ShapeDtypeStruct((1, GATE_COLS), F32)],
        scratch_shapes=[], operands=(dx2, ya, yb, gl, bias, wpa, wpc, wout), sem=("arbitrary",), hosted=hosted)


def _adamw(w, g, m, v, *, name):
    rows, cols = w.shape
    tr = max([c for c in range(8, 513, 8) if rows % c == 0], default=rows)
    c1 = 1.0 / (1.0 - ADAM_B1 ** ADAM_STEP)
    c2 = 1.0 / (1.0 - ADAM_B2 ** ADAM_STEP)

    def body(w_ref, g_ref, m_ref, v_ref, d_ref, nm_ref, nv_ref):
        gv = g_ref[...]
        nm = ADAM_B1 * m_ref[...] + (1.0 - ADAM_B1) * gv
        nv = ADAM_B2 * v_ref[...] + (1.0 - ADAM_B2) * (gv * gv)
        nm_ref[...] = nm
        nv_ref[...] = nv
        d_ref[...] = -ADAM_LR * ((nm * c1) / (jnp.sqrt(nv * c2) + ADAM_EPS) + ADAM_WD * w_ref[...])

    spec = pl.BlockSpec((tr, cols), lambda i: (i, 0))
    shp = jax.ShapeDtypeStruct((rows, cols), F32)
    return pl.pallas_call(
        body, name=name, grid=(rows // tr,), in_specs=[spec] * 4, out_specs=[spec] * 3, out_shape=[shp] * 3,
        compiler_params=_params("parallel"),
    )(w, g, m, v)


def _place():
    return lax.axis_index("x"), lax.axis_index("y"), lax.axis_index("c")


def _other_chips(x, y):
    return [(1 - x, y), (x, 1 - y), (1 - x, 1 - y)]


def _remote(src, dst, send, recv, dev):
    return pltpu.make_async_remote_copy(src_ref=src, dst_ref=dst, send_sem=send, recv_sem=recv, device_id=dev, device_id_type=MESH)


def _gather_chips_plan(n):
    def start(srcs, dsts, send, recv, local):
        x, y, cc = _place()
        me = 4 * x + 2 * y + cc
        for a in range(n):
            pltpu.make_async_copy(srcs[a], dsts[a].at[me], local.at[a]).start()
            for k, (px, py) in enumerate(_other_chips(x, y)):
                _remote(srcs[a], dsts[a].at[me], send.at[3 * a + k], recv.at[3 * a + k], (px, py, cc)).start()

    def wait(srcs, dsts, send, recv, local):
        x, y, cc = _place()
        me = 4 * x + 2 * y + cc
        for a in range(n):
            for k, (px, py) in enumerate(_other_chips(x, y)):
                _remote(srcs[a], dsts[a].at[4 * px + 2 * py + cc], send.at[3 * a + k], recv.at[3 * a + k], (px, py, cc)).wait_recv()
        for a in range(n):
            for k, (px, py) in enumerate(_other_chips(x, y)):
                _remote(srcs[a], dsts[a].at[me], send.at[3 * a + k], recv.at[3 * a + k], (px, py, cc)).wait_send()
            pltpu.make_async_copy(srcs[a], dsts[a].at[me], local.at[a]).wait()

    return _Plan(start, wait, 3 * n, n)


def _scatter_chips_plan(n):
    def start(srcs, dsts, send, recv, local):
        x, y, cc = _place()
        for a in range(n):
            for k, (px, py) in enumerate(_other_chips(x, y)):
                _remote(srcs[a].at[2 * px + py], dsts[a].at[k], send.at[3 * a + k], recv.at[3 * a + k], (px, py, cc)).start()

    def wait(srcs, dsts, send, recv, local):
        x, y, cc = _place()
        for a in range(n):
            for k, (px, py) in enumerate(_other_chips(x, y)):
                _remote(srcs[a].at[k], dsts[a].at[k], send.at[3 * a + k], recv.at[3 * a + k], (px, py, cc)).wait_recv()
        for a in range(n):
            for k, (px, py) in enumerate(_other_chips(x, y)):
                _remote(srcs[a].at[k], dsts[a].at[k], send.at[3 * a + k], recv.at[3 * a + k], (px, py, cc)).wait_send()

    return _Plan(start, wait, 3 * n, 0)


def _gather_shapes(blocks):
    return [jax.ShapeDtypeStruct((N_DEV,) + b.shape, b.dtype) for b in blocks]


def _scatter_shapes(parts):
    return [jax.ShapeDtypeStruct((3,) + p.shape[1:], p.dtype) for p in parts]


def _gather_sibling_plan(n):
    def start(srcs, dsts, send, recv, local):
        x, y, cc = _place()
        for a in range(n):
            for q in range(4):
                _remote(srcs[a].at[2 * q + cc], dsts[a].at[2 * q + cc], send.at[4 * a + q], recv.at[4 * a + q], (x, y, 1 - cc)).start()

    def wait(srcs, dsts, send, recv, local):
        x, y, cc = _place()
        for a in range(n):
            for q in range(4):
                _remote(srcs[a].at[2 * q + cc], dsts[a].at[2 * q + 1 - cc], send.at[4 * a + q], recv.at[4 * a + q],
                        (x, y, 1 - cc)).wait_recv()
        for a in range(n):
            for q in range(4):
                _remote(srcs[a].at[2 * q + cc], dsts[a].at[2 * q + cc], send.at[4 * a + q], recv.at[4 * a + q],
                        (x, y, 1 - cc)).wait_send()

    return _Plan(start, wait, 4 * n, 0, in_place=True)


def _scatter_sibling_plan(n):
    def start(srcs, dsts, send, recv, local):
        x, y, cc = _place()
        for a in range(n):
            for q in range(4):
                _remote(srcs[a].at[2 * q + 1 - cc], dsts[a].at[q], send.at[4 * a + q], recv.at[4 * a + q], (x, y, 1 - cc)).start()

    def wait(srcs, dsts, send, recv, local):
        x, y, cc = _place()
        for a in range(n):
            for q in range(4):
                _remote(srcs[a].at[q], dsts[a].at[q], send.at[4 * a + q], recv.at[4 * a + q], (x, y, 1 - cc)).wait_recv()
        for a in range(n):
            for q in range(4):
                _remote(srcs[a].at[q], dsts[a].at[q], send.at[4 * a + q], recv.at[4 * a + q], (x, y, 1 - cc)).wait_send()

    return _Plan(start, wait, 4 * n, 0)


def _same_shapes(arrs):
    return [jax.ShapeDtypeStruct(a.shape, a.dtype) for a in arrs]


def _halved_shapes(parts):
    return [jax.ShapeDtypeStruct((4,) + p.shape[1:], p.dtype) for p in parts]


def _run_plan(plan, srcs, out_shapes, *, name):
    n_in, n_out = len(srcs), len(out_shapes)

    def body(*refs):
        h_in, h_out, sems = refs[:n_in], refs[n_in:n_in + n_out], refs[n_in + n_out:]
        plan.start(h_in, h_out, *sems)
        plan.wait(h_in, h_out, *sems)

    return pl.pallas_call(body, name=name, in_specs=[ANY] * n_in, out_specs=[ANY] * n_out, out_shape=list(out_shapes),
                          input_output_aliases={a: a for a in range(n_in)} if plan.in_place else {},
                          scratch_shapes=plan.sems())(*srcs)


def _sum_sibling(p, q, core, *, name):
    _, r, c = p.shape

    def body(core_ref, p_ref, q_ref, o_ref):
        o_ref[...] = (p_ref[...].astype(F32) + q_ref[...].astype(F32)).astype(BF16)

    grid_spec = pltpu.PrefetchScalarGridSpec(
        num_scalar_prefetch=1, grid=(4,),
        in_specs=[pl.BlockSpec((1, r, c), lambda ch, core_ref: (2 * ch + core_ref[0], 0, 0)),
                  pl.BlockSpec((1, r, c), lambda ch, core_ref: (ch, 0, 0))],
        out_specs=pl.BlockSpec((1, r, c), lambda ch, core_ref: (ch, 0, 0)))
    return pl.pallas_call(
        body, name=name, grid_spec=grid_spec, out_shape=jax.ShapeDtypeStruct((4, r, c), BF16),
        compiler_params=_params("parallel"),
    )(core, p, q)


def _sum_chips(s1, r2, chip, *, name):
    _, r, c = s1.shape

    def body(chip_ref, s_ref, r_ref, o_ref):
        acc = s_ref[0].astype(F32)
        for k in range(3):
            acc = acc + r_ref[k].astype(F32)
        o_ref[...] = acc

    grid_spec = pltpu.PrefetchScalarGridSpec(
        num_scalar_prefetch=1, grid=(1,),
        in_specs=[pl.BlockSpec((1, r, c), lambda i, chip_ref: (chip_ref[0], 0, 0)),
                  pl.BlockSpec((3, r, c), lambda i, chip_ref: (0, 0, 0))],
        out_specs=pl.BlockSpec((r, c), lambda i, chip_ref: (0, 0)))
    return pl.pallas_call(
        body, name=name, grid_spec=grid_spec, out_shape=jax.ShapeDtypeStruct((r, c), F32),
        compiler_params=_params("arbitrary"),
    )(chip, s1, r2)


def _sum_adamw(s1, r2, chip, w, m, v, *, name):
    _, r, c = s1.shape
    c1 = 1.0 / (1.0 - ADAM_B1 ** ADAM_STEP)
    c2 = 1.0 / (1.0 - ADAM_B2 ** ADAM_STEP)

    def body(chip_ref, s_ref, r_ref, w_ref, m_ref, v_ref, g_ref, d_ref, nm_ref, nv_ref):
        gv = s_ref[0].astype(F32)
        for k in range(3):
            gv = gv + r_ref[k].astype(F32)
        g_ref[...] = gv
        nm = ADAM_B1 * m_ref[...] + (1.0 - ADAM_B1) * gv
        nv = ADAM_B2 * v_ref[...] + (1.0 - ADAM_B2) * (gv * gv)
        nm_ref[...] = nm
        nv_ref[...] = nv
        d_ref[...] = -ADAM_LR * ((nm * c1) / (jnp.sqrt(nv * c2) + ADAM_EPS) + ADAM_WD * w_ref[...])

    flat = pl.BlockSpec((r, c), lambda i, chip_ref: (0, 0))
    grid_spec = pltpu.PrefetchScalarGridSpec(
        num_scalar_prefetch=1, grid=(1,),
        in_specs=[pl.BlockSpec((1, r, c), lambda i, chip_ref: (chip_ref[0], 0, 0)),
                  pl.BlockSpec((3, r, c), lambda i, chip_ref: (0, 0, 0)), flat, flat, flat],
        out_specs=[flat] * 4)
    return pl.pallas_call(
        body, name=name, grid_spec=grid_spec, out_shape=[jax.ShapeDtypeStruct((r, c), F32)] * 4,
        compiler_params=_params("arbitrary"),
    )(chip, s1, r2, w, m, v)


def _small_exchange(v, *, reduce, name):
    r, c = v.shape

    def body(x_ref, o_ref, *rest):
        if reduce:
            buf_ref, send_sems, recv_sems = rest
        else:
            buf_ref = o_ref
            send_sems, recv_sems = rest
        x, y, cc = _place()
        me = 4 * x + 2 * y + cc

        def peer(k):
            return ((1 - x) if k & 4 else x, (1 - y) if k & 2 else y, (1 - cc) if k & 1 else cc)

        buf_ref[me] = x_ref[...]
        sends = []
        for k in range(1, N_DEV):
            cp = pltpu.make_async_remote_copy(src_ref=x_ref, dst_ref=buf_ref.at[me], send_sem=send_sems.at[k - 1],
                                              recv_sem=recv_sems.at[k - 1], device_id=peer(k), device_id_type=MESH)
            cp.start()
            sends.append(cp)
        for k in range(1, N_DEV):
            px, py, pc = peer(k)
            pltpu.make_async_remote_copy(src_ref=x_ref, dst_ref=buf_ref.at[4 * px + 2 * py + pc], send_sem=send_sems.at[k - 1],
                                         recv_sem=recv_sems.at[k - 1], device_id=peer(k), device_id_type=MESH).wait_recv()
        for cp in sends:
            cp.wait_send()
        if reduce:
            acc = buf_ref[0]
            for s in range(1, N_DEV):
                acc = acc + buf_ref[s]
            o_ref[...] = acc

    vm = pl.BlockSpec(memory_space=pltpu.VMEM)
    sems = [pltpu.SemaphoreType.DMA((N_DEV - 1,)), pltpu.SemaphoreType.DMA((N_DEV - 1,))]
    if reduce:
        out_shape, scratch = jax.ShapeDtypeStruct((r, c), F32), [pltpu.VMEM((N_DEV, r, c), F32)] + sems
    else:
        out_shape, scratch = jax.ShapeDtypeStruct((N_DEV, r, c), F32), sems
    return pl.pallas_call(body, name=name, in_specs=[vm], out_specs=vm, out_shape=out_shape, scratch_shapes=scratch)(v)


def _rows(a):
    return a.reshape(-1, D)


def _pad_cols(a, to):
    return jnp.pad(a, ((0, 0), (0, to - a.shape[1])))


def _pack_weights(w):
    parts = {
        "w_inT": jnp.pad(w["w_in"].T, ((0, IN_SHARD_PAD - IN_SHARD), (0, 0))),
        "w_uq": _rows(_pad_cols(w["w_uq"], HEAD_PAD)), "w_uk": _rows(_pad_cols(w["w_uk"], HEAD_PAD)),
        "w_uv": _rows(_pad_cols(w["w_uv"], HEAD_PAD)), "w_pa": _rows(w["w_proj_attn"]),
        "w_pc": w["w_proj_conv"], "w_out": w["w_out"],
    }
    return [jnp.concatenate([parts[n].astype(BF16) for n, _ in group], axis=0) for group in PACK]


def _cols_from_shards(gs, name, rows):
    idx, off, r = PACK_OFF[name]
    return gs[idx][:, off:off + r].reshape(N_DEV, rows, HEAD_PAD).transpose(1, 0, 2).reshape(rows, N_DEV * HEAD_PAD)


def _rows_from_shards(gs, name, keep=None):
    idx, off, r = PACK_OFF[name]
    keep = r if keep is None else keep
    return gs[idx][:, off:off + keep].reshape(N_DEV * keep, D)


def _rope_placement():
    i = lax.broadcasted_iota(jnp.int32, (HEAD_PAD, D), 0)
    j = lax.broadcasted_iota(jnp.int32, (HEAD_PAD, D), 1)
    return ((i < 2 * ROPE_HALF) & (j % HEAD_PAD == NOPE + i)).astype(BF16)


def _unpack_in(g_in):
    w_inT = _rows_from_shards([g_in, None], "w_inT", IN_SHARD)
    lat_rows = Q_LORA + KV_LORA + 2 * ROPE_HALF
    conv = w_inT[lat_rows:lat_rows + CONV_COLS].reshape(3, D // CONV_CB, CONV_CB, D).transpose(1, 0, 2, 3).reshape(CONV_COLS, D)
    return {"latT": jnp.pad(w_inT[:lat_rows], ((0, LAT_PAD - lat_rows), (0, 0))), "convT": conv,
            "gateT": w_inT[lat_rows + CONV_COLS:]}


def _unpack_misc(g_misc):
    g = [None, g_misc]
    wpa = _cols_from_shards(g, "w_pa", 512).reshape(N_HEADS, NOPE, D)
    return {
        "wq": _cols_from_shards(g, "w_uq", Q_LORA),
        "wk": jnp.concatenate([_cols_from_shards(g, "w_uk", KV_LORA), _rope_placement()], axis=0),
        "wv": _cols_from_shards(g, "w_uv", KV_LORA),
        "wpa": jnp.pad(wpa, ((0, 0), (0, HEAD_PAD - NOPE), (0, 0))).reshape(D, D),
        "wpc": _rows_from_shards(g, "w_pc"), "wout": _rows_from_shards(g, "w_out"),
    }


def _shards_from_cols(a):
    rows = a.shape[0]
    return a.reshape(rows, N_DEV, HEAD_PAD).transpose(1, 0, 2).reshape(N_DEV, rows * HEAD_PAD // D, D)


def _pack_grads(gw):
    lat_rows = Q_LORA + KV_LORA + 2 * ROPE_HALF
    conv = gw["convT"].reshape(D // CONV_CB, 3, CONV_CB, D).transpose(1, 0, 2, 3).reshape(CONV_COLS, D)
    w_inT = jnp.concatenate([gw["latT"][:lat_rows], conv, gw["gateT"]], axis=0).reshape(N_DEV, IN_SHARD, D)
    wpa = gw["wpa"].reshape(N_HEADS, HEAD_PAD, D)[:, :NOPE].reshape(N_HEADS * NOPE, D)
    parts = {}
    parts.update({
        "w_inT": jnp.pad(w_inT, ((0, 0), (0, IN_SHARD_PAD - IN_SHARD), (0, 0))),
        "w_uq": _shards_from_cols(gw["wq"]), "w_uk": _shards_from_cols(gw["wk"][:KV_LORA]),
        "w_uv": _shards_from_cols(gw["wv"][:KV_LORA]), "w_pa": _shards_from_cols(wpa),
        "w_pc": gw["wpc"].reshape(N_DEV, D // N_DEV, D), "w_out": gw["wout"].reshape(N_DEV, D // N_DEV, D),
    })
    return [jnp.concatenate([parts[n] for n, _ in group], axis=1) for group in PACK]


def _unpack_grads(mines):
    def seg(name, keep=None):
        idx, off, r = PACK_OFF[name]
        return mines[idx][off:off + (r if keep is None else keep)]

    return {
        "w_in": seg("w_inT", IN_SHARD).T,
        "w_uq": seg("w_uq").reshape(Q_LORA, HEAD_PAD)[:, :QK_DIM],
        "w_uk": seg("w_uk").reshape(KV_LORA, HEAD_PAD)[:, :NOPE],
        "w_uv": seg("w_uv").reshape(KV_LORA, HEAD_PAD)[:, :NOPE],
        "w_proj_attn": seg("w_pa").reshape(512, HEAD_PAD),
        "w_proj_conv": seg("w_pc"), "w_out": seg("w_out"),
    }


def _rope_tables(positions):
    inv_freq = 1.0 / (ROPE_THETA ** (jnp.arange(ROPE_HALF, dtype=F32) / ROPE_HALF))
    ang = positions.reshape(-1).astype(F32)[:, None] * inv_freq
    cos, sin = jnp.cos(ang), jnp.sin(ang)
    t = ang.shape[0]
    zero = jnp.zeros((t, ROPE_HALF), F32)
    head = jnp.ones((t, NOPE), F32)
    tail = jnp.zeros((t, HEAD_PAD - QK_DIM), F32)
    nohead = jnp.zeros((t, NOPE), F32)
    rc = jnp.concatenate([head, cos, cos, tail], axis=1)
    rs1 = jnp.concatenate([nohead, -sin, zero, tail], axis=1)
    rs2 = jnp.concatenate([nohead, zero, sin, tail], axis=1)
    return rc, rs1, rs2


def _local_step(x, positions, target, conv_w, small, ex):
    n_seq, seq, d = x.shape
    t = n_seq * seq
    x0 = x.reshape(t, d)
    tgt = target.reshape(t, d)
    rc, rs1, rs2 = _rope_tables(positions)
    ghq = _pad_cols(small["q_head_norm"], HEAD_PAD)
    ghk = _pad_cols(small["k_head_norm"], HEAD_PAD)
    TM, HC, TQ = 1024, 256, 1024

    def mm(*args, hosted=None, **kw):
        res = _mm(*args, hosted=hosted, **kw)
        return res if hosted is not None else (res, None)

    def wgrad(a, b, name, tm=None, hosted=None):
        tm = tm or a.shape[1]
        return mm(a, b, mode="tn", out_dtype=BF16, tm=tm, tn=b.shape[1], tk=2048 if tm <= D else 1024, name=name, hosted=hosted)

    f1g, f1u, f1d = ex.gather_now("ffn1")
    (x1, h1, a1, b1), got = _ffn_fwd(x0, small["ffn1_norm"], f1g, f1u, f1d, tm=512, hc=DFF // 2, name="ffn1_fwd",
                                     hosted=ex.gather_chips("mix_in"))
    hm, got = _rms_fwd(x1, small["mix_norm"], tm=TM, name="mix_norm_fwd", hosted=ex.gather_sibling(got))
    W = ex.mix_in_weights(got)
    (lat, conv3, gl), got = _proj_fwd(hm, W["latT"], W["convT"], W["gateT"], tm=512, name="proj_fwd",
                                      hosted=ex.gather_chips("mix_misc"))
    p, got = _conv_fwd(conv3, conv_w, n_seq=n_seq, seq=seq, name="conv_fwd", hosted=ex.gather_sibling(got))
    W.update(ex.mix_misc_weights(got))
    q, k, v, qn, ckv = _mla_prep_fwd(lat, small["q_a_norm"], small["kv_a_norm"], ghq, ghk, W["wq"], W["wk"], W["wv"], rc, rs1, rs2,
                                     tm=512, name="mla_prep_fwd")
    (o, lse), got = _flash_fwd(q, k, v, n_seq=n_seq, seq=seq, tq=TQ, name="attn_fwd", hosted=ex.gather_chips("ffn2"))
    (x2, merged, ya, yb), got = _merge_fwd(o, p, gl, small["gate_bias"], x1, W["wpa"], W["wpc"], W["wout"], tm=512, name="merge_fwd",
                                           hosted=ex.gather_sibling(got))
    f2g, f2u, f2d = ex.ffn_weights(got)
    (dy, h2, a2, b2, loss_row), _ = _ffn_fwd(x2, small["ffn2_norm"], f2g, f2u, f2d, tm=512, hc=DFF // 2, name="ffn2_fwd", target=tgt)

    gw, gs = {}, {}
    (da2, db2, *ffn2_grads), _ = _ffn_grads(dy, h2, a2, b2, f2d, tm=TM, hc=HC, name="ffn2_grads")
    (dx2, gs["ffn2_norm"]), _ = _ffn_up_bwd(da2, db2, f2g, f2u, x2, small["ffn2_norm"], dy, tm=512, name="ffn2_up_bwd")

    (dx2b, dya, dyb, dgl, do, dp, gs["gate_bias"]), got = _merge_bwd(
        dx2, ya, yb, gl, small["gate_bias"], W["wpa"], W["wpc"], W["wout"], tm=512, name="merge_bwd",
        hosted=ex.scatter_sibling("ffn2", ffn2_grads))
    ex.scatter_sibling_done("ffn2", got)
    gw["wout"] = wgrad(merged, dx2b, "dw_out")[0]
    gw["wpa"] = wgrad(o, dya, "dw_pa")[0]
    gw["wpc"] = wgrad(p, dyb, "dw_pc")[0]
    dconv3, dconv_w = _conv_bwd(dp, conv3, conv_w, n_seq=n_seq, seq=seq, name="conv_bwd")
    (dq, dk, dv), got = _flash_bwd(q, k, v, o, lse, do, n_seq=n_seq, seq=seq, tq=TQ, name="attn_bwd",
                                   hosted=ex.scatter_chips("ffn2"))
    ex.scatter_chips_done("ffn2", got)
    dlat, dqp, dkp, gs["q_a_norm"], gs["kv_a_norm"], dghq, dghk = _mla_prep_bwd(
        dq, dk, dv, lat, qn, ckv, small["q_a_norm"], small["kv_a_norm"], ghq, ghk, W["wq"], W["wk"], W["wv"], rc, rs1, rs2,
        tm=512, name="mla_prep_bwd")
    gs["q_head_norm"], gs["k_head_norm"] = dghq[:, :QK_DIM], dghk[:, :QK_DIM]
    gw["wq"] = wgrad(qn, dqp, "dw_uq")[0]
    gw["wk"] = wgrad(ckv, dkp, "dw_uk")[0]
    gw["wv"] = wgrad(ckv, dv, "dw_uv")[0]
    gw["convT"] = wgrad(dconv3, hm, "dw_conv", tm=CONV_COLS // 2)[0]
    gw["gateT"] = wgrad(dgl, hm, "dw_gate")[0]
    gw["latT"] = wgrad(dlat, hm, "dw_lat")[0]
    ex.scatter_sibling_now("mix", gw)
    (dx1, gs["mix_norm"]), got = _proj_bwd(dlat, dconv3, dgl, W["latT"], W["convT"], W["gateT"], x1, small["mix_norm"], dx2,
                                           tm=512, name="proj_bwd", hosted=ex.scatter_chips("mix_in"))
    ex.scatter_chips_done("mix_in", got)

    (da1, db1, *ffn1_grads), got = _ffn_grads(dx1, h1, a1, b1, f1d, tm=TM, hc=HC, name="ffn1_grads",
                                              hosted=ex.scatter_chips("mix_misc"))
    ex.scatter_chips_done("mix_misc", got)
    ex.scatter_sibling_now("ffn1", ffn1_grads)
    (dx0, gs["ffn1_norm"]), got = _ffn_up_bwd(da1, db1, f1g, f1u, x0, small["ffn1_norm"], dx1, tm=512, name="ffn1_up_bwd",
                                              hosted=ex.scatter_chips("ffn1"))
    ex.scatter_chips_done("ffn1", got)
    return loss_row, dx0.reshape(n_seq, seq, d), dconv_w, gs


class _MeshExchange:
    def __init__(self, w, core, chip):
        self.w, self.core, self.chip = w, core, chip
        self.partial, self.received, self._packed = {}, {}, None

    def _blocks(self, group):
        w = self.w
        if group.startswith("ffn"):
            return [w[group + "_w_gate"].T.astype(BF16), w[group + "_w_up"].T.astype(BF16), w[group + "_w_down"].astype(BF16)]
        if self._packed is None:
            self._packed = _pack_weights(w)
        return [self._packed[0 if group == "mix_in" else 1]]

    def gather_chips(self, *groups):
        blocks = [b for group in groups for b in self._blocks(group)]
        return _gather_chips_plan(len(blocks)), blocks, _gather_shapes(blocks)

    def gather_sibling(self, got):
        half = list(got)
        return _gather_sibling_plan(len(half)), half, _same_shapes(half)

    def gather_now(self, group):
        plan, blocks, shapes = self.gather_chips(group)
        half = list(_run_plan(plan, blocks, shapes, name="gather_%s_chips" % group))
        return self.ffn_weights(_run_plan(_gather_sibling_plan(len(half)), half, _same_shapes(half), name="gather_%s_sibling" % group))

    def ffn_weights(self, got):
        return [a.reshape(DFF, D) for a in got]

    def mix_in_weights(self, got):
        return _unpack_in(got[0])

    def mix_misc_weights(self, got):
        return _unpack_misc(got[0])

    def _parts(self, group, grads):
        if group == "mix":
            return _pack_grads(grads), ["mix_in", "mix_misc"]
        parts = [g.reshape(N_DEV, -1, D) for g in grads]
        return parts, ([group] if len(parts) == 1 else None)

    def scatter_sibling(self, group, grads):
        self._sent, self._names = self._parts(group, grads)
        return _scatter_sibling_plan(len(self._sent)), self._sent, _halved_shapes(self._sent)

    def scatter_sibling_done(self, group, got):
        sums = [_sum_sibling(p, q, self.core, name="sum_%s_sibling_%d" % (group, i)) for i, (p, q) in enumerate(zip(self._sent, got))]
        if self._names is None:
            self.partial[group] = sums
        else:
            for n, s in zip(self._names, sums):
                self.partial[n] = [s]

    def scatter_sibling_now(self, group, grads):
        plan, parts, shapes = self.scatter_sibling(group, grads)
        self.scatter_sibling_done(group, _run_plan(plan, parts, shapes, name="scatter_%s_sibling" % group))

    def scatter_chips(self, group):
        s1 = self.partial[group]
        return _scatter_chips_plan(len(s1)), s1, _scatter_shapes(s1)

    def scatter_chips_done(self, group, got):
        self.received[group] = list(got)


SMALL_NAMES = ("ffn1_norm", "mix_norm", "gate_bias", "q_a_norm", "kv_a_norm", "q_head_norm", "k_head_norm", "ffn2_norm")
SMALL_SLOTS = {"ffn1_norm": 1024, "mix_norm": 1024, "gate_bias": 2048, "q_a_norm": 384, "kv_a_norm": 256, "q_head_norm": 128,
               "k_head_norm": 128, "ffn2_norm": 1024, "conv_w": 3072, "loss": 128}
COLUMN_MAJOR = ("w_in", "w_uq", "w_uk", "w_uv")
WEIGHT_NAMES = ("ffn1_norm", "ffn1_w_gate", "ffn1_w_up", "ffn1_w_down", "mix_norm", "w_in", "gate_bias", "q_a_norm", "w_uq",
                "kv_a_norm", "w_uk", "w_uv", "q_head_norm", "k_head_norm", "w_proj_attn", "conv_w", "w_proj_conv", "w_out",
                "ffn2_norm", "ffn2_w_gate", "ffn2_w_up", "ffn2_w_down")


def _step(x, positions, loss_target, w, m, v):
    xi, yi, ci = _place()
    core = ci.astype(jnp.int32).reshape(1)
    chip = (2 * xi + yi).astype(jnp.int32).reshape(1)
    me = 4 * xi + 2 * yi + ci

    cw_all = _small_exchange(jnp.pad(w["conv_w"], ((0, 5), (0, 0))), reduce=False, name="gather_conv_w")
    conv_w = cw_all[:, :3].transpose(1, 0, 2).reshape(3, D)
    small = {n: w[n].reshape(1, -1) for n in SMALL_NAMES}
    ex = _MeshExchange(w, core, chip)

    loss_row, grad_x, dconv_w, gs = _local_step(x, positions, loss_target, conv_w, small, ex)

    grads, deltas, new_m, new_v = {}, {}, {}, {}
    where = {"ffn1_w_gate": ("ffn1", 0), "ffn1_w_up": ("ffn1", 1), "ffn1_w_down": ("ffn1", 2),
             "ffn2_w_gate": ("ffn2", 0), "ffn2_w_up": ("ffn2", 1), "ffn2_w_down": ("ffn2", 2)}
    for n, (group, i) in where.items():
        transposed = not n.endswith("down")
        wv, mv, vv = (a[n].T if transposed else a[n] for a in (w, m, v))
        res = _sum_adamw(ex.partial[group][i], ex.received[group][i], chip, wv, mv, vv, name="adamw_" + n)
        grads[n], deltas[n], new_m[n], new_v[n] = (r.T if transposed else r for r in res)
    grads.update(_unpack_grads([_sum_chips(ex.partial[g][0], ex.received[g][0], chip, name="sum_%s_chips" % g)
                                for g in ("mix_in", "mix_misc")]))

    pieces = [_pad_cols(gs[n], SMALL_SLOTS[n]) for n in SMALL_NAMES] + [dconv_w.reshape(1, 3 * D), loss_row]
    total = _small_exchange(jnp.concatenate(pieces, axis=1).reshape(-1, 128), reduce=True, name="reduce_small").reshape(-1)
    off = 0
    for n in SMALL_NAMES:
        grads[n] = total[off:off + w[n].shape[0]]
        off += SMALL_SLOTS[n]
    conv_full = total[off:off + 3 * D].reshape(3, D)
    grads["conv_w"] = lax.dynamic_slice(conv_full, (0, me * HEAD_PAD), (3, HEAD_PAD))
    loss = total[off + 3 * D]

    for n in WEIGHT_NAMES:
        if n in deltas:
            continue
        shape = w[n].shape
        if n in COLUMN_MAJOR:
            ops = [a.T for a in (w[n], grads[n], m[n], v[n])]
            deltas[n], new_m[n], new_v[n] = (r.T for r in _adamw(*ops, name="adamw_" + n))
            continue
        if len(shape) == 1:
            view = (-1, 128) if shape[0] % 128 == 0 else (1, shape[0])
        else:
            view = shape
        dlt, nm, nv = _adamw(w[n].reshape(view), grads[n].reshape(view), m[n].reshape(view), v[n].reshape(view), name="adamw_" + n)
        deltas[n], new_m[n], new_v[n] = dlt.reshape(shape), nm.reshape(shape), nv.reshape(shape)
    return (loss, grad_x, *[grads[n] for n in WEIGHT_NAMES], *[deltas[n] for n in WEIGHT_NAMES],
            *[new_m[n] for n in WEIGHT_NAMES], *[new_v[n] for n in WEIGHT_NAMES])


def kernel(x, positions, ffn1_norm, ffn1_w_gate, ffn1_w_up, ffn1_w_down, mix_norm, w_in, gate_bias, q_a_norm, w_uq, kv_a_norm, w_uk, w_uv, q_head_norm, k_head_norm, w_proj_attn, conv_w, w_proj_conv, w_out, ffn2_norm, ffn2_w_gate, ffn2_w_up, ffn2_w_down, loss_target, m_ffn1_norm, m_ffn1_w_gate, m_ffn1_w_up, m_ffn1_w_down, m_mix_norm, m_w_in, m_gate_bias, m_q_a_norm, m_w_uq, m_kv_a_norm, m_w_uk, m_w_uv, m_q_head_norm, m_k_head_norm, m_w_proj_attn, m_conv_w, m_w_proj_conv, m_w_out, m_ffn2_norm, m_ffn2_w_gate, m_ffn2_w_up, m_ffn2_w_down, v_ffn1_norm, v_ffn1_w_gate, v_ffn1_w_up, v_ffn1_w_down, v_mix_norm, v_w_in, v_gate_bias, v_q_a_norm, v_w_uq, v_kv_a_norm, v_w_uk, v_w_uv, v_q_head_norm, v_k_head_norm, v_w_proj_attn, v_conv_w, v_w_proj_conv, v_w_out, v_ffn2_norm, v_ffn2_w_gate, v_ffn2_w_up, v_ffn2_w_down):
    given = dict(locals())
    w = {n: given[n] for n in WEIGHT_NAMES}
    m = {n: given["m_" + n] for n in WEIGHT_NAMES}
    v = {n: given["v_" + n] for n in WEIGHT_NAMES}
    return _step(x, positions, loss_target, w, m, v)
```

```python
import functools

import jax
import jax.numpy as jnp
from jax import lax
from jax.experimental import pallas as pl
from jax.experimental.pallas import tpu as pltpu

F32 = jnp.float32
BF16 = jnp.bfloat16
MESH = pl.DeviceIdType.MESH
ANY = pl.BlockSpec(memory_space=pl.ANY)

N_DEV = 8
D = 1024
DFF = 2816
N_HEADS = 8
HEAD_PAD = 128
QK_DIM = 96
NOPE = 64
ROPE_HALF = 16
Q_LORA = 384
KV_LORA = 256
LAT_PAD = 768
CONV_COLS = 3072
GATE_COLS = 2048
IN_DIM = 5792
IN_SHARD = IN_DIM // N_DEV
IN_SHARD_PAD = 736
FF_SHARD = DFF // N_DEV
ROPE_THETA = 10000.0
NORM_EPS = 1e-6
ATTN_SCALE = QK_DIM ** -0.5
NEG = -1e30

ADAM_LR, ADAM_B1, ADAM_B2, ADAM_EPS, ADAM_WD, ADAM_STEP = 0.001, 0.9, 0.999, 1e-08, 0.01, 10

PACK = ((("w_inT", IN_SHARD_PAD),), (("w_uq", 48), ("w_uk", 32), ("w_uv", 32), ("w_pa", 64), ("w_pc", 128), ("w_out", 128)))
PACK_OFF = {}
for _i, _group in enumerate(PACK):
    _o = 0
    for _n, _r in _group:
        PACK_OFF[_n] = (_i, _o, _r)
        _o += _r

VMEM_LIMIT = 56 * 1024 * 1024


def _params(*sem):
    return pltpu.CompilerParams(dimension_semantics=sem if sem else None, vmem_limit_bytes=VMEM_LIMIT)


class _Plan:
    def __init__(self, start, wait, n_remote, n_local, in_place=False):
        self.start, self.wait, self.n_remote, self.n_local, self.in_place = start, wait, n_remote, n_local, in_place

    def sems(self):
        return [pltpu.SemaphoreType.DMA((self.n_remote,)), pltpu.SemaphoreType.DMA((self.n_remote,)),
                pltpu.SemaphoreType.DMA((max(self.n_local, 1),))]


def _call(body, *, name, grid, in_specs, out_specs, out_shape, scratch_shapes, operands, sem, hosted=None):
    if hosted is None:
        outs = pl.pallas_call(body, name=name, grid=grid, in_specs=in_specs, out_specs=out_specs, out_shape=out_shape,
                              scratch_shapes=scratch_shapes, compiler_params=_params(*sem))(*operands)
        return outs, None
    plan, srcs, h_shapes = hosted
    n_in, n_out, n_scr, nh_in, nh_out = len(in_specs), len(out_specs), len(scratch_shapes), len(srcs), len(h_shapes)
    aliases = {n_in + a: n_out + a for a in range(nh_in)} if plan.in_place else {}

    def full_body(*refs):
        ins, refs = refs[:n_in], refs[n_in:]
        h_in, refs = refs[:nh_in], refs[nh_in:]
        outs, refs = refs[:n_out], refs[n_out:]
        h_out, refs = refs[:nh_out], refs[nh_out:]
        scr, sems = refs[:n_scr], refs[n_scr:]
        ids = [pl.program_id(ax) for ax in range(len(grid))]
        first = functools.reduce(jnp.logical_and, [i == 0 for i in ids])
        last = functools.reduce(jnp.logical_and, [i == g - 1 for i, g in zip(ids, grid)])

        @pl.when(first)
        def _():
            plan.start(h_in, h_out, *sems)

        body(*ins, *outs, *scr)

        @pl.when(last)
        def _():
            plan.wait(h_in, h_out, *sems)

    res = pl.pallas_call(
        full_body, name=name, grid=grid, in_specs=list(in_specs) + [ANY] * nh_in, out_specs=list(out_specs) + [ANY] * nh_out,
        out_shape=list(out_shape) + list(h_shapes), scratch_shapes=list(scratch_shapes) + plan.sems(),
        input_output_aliases=aliases, compiler_params=_params(*(["arbitrary"] * len(grid))),
    )(*operands, *srcs)
    return res[:n_out], res[n_out:]


def _dot_nn(a, b):
    return lax.dot_general(a, b, (((1,), (0,)), ((), ())), preferred_element_type=F32)


def _dot_nt(a, b):
    return lax.dot_general(a, b, (((1,), (1,)), ((), ())), preferred_element_type=F32)


def _dot_tn(a, b):
    return lax.dot_general(a, b, (((0,), (0,)), ((), ())), preferred_element_type=F32)


def _sigmoid(x):
    return 0.5 * jnp.tanh(0.5 * x) + 0.5


def _rms_stats(x):
    r = lax.rsqrt(jnp.mean(x * x, axis=-1, keepdims=True) + NORM_EPS)
    return x * r, r


ROWS_WIDE = 16
MM_ROWS = 256


def _rms_bwd(dy, xhat, r, g):
    dg = jnp.sum(dy * xhat, axis=0, keepdims=True)
    dxh = dy * g
    dx = r * (dxh - xhat * jnp.mean(dxh * xhat, axis=-1, keepdims=True))
    return dx, dg


def _mm(a, b, *, mode, out_dtype, tm, tn, tk, name, add=None, scale=1.0, hosted=None):
    if mode == "nn":
        (m, k), (_, n) = a.shape, b.shape
    elif mode == "nt":
        (m, k), (n, _) = a.shape, b.shape
    else:
        (k, m), (_, n) = a.shape, b.shape
    assert m % tm == 0 and n % tn == 0 and k % tk == 0, (name, m, n, k, tm, tn, tk)
    nk = k // tk
    dot = {"nn": _dot_nn, "nt": _dot_nt, "tn": _dot_tn}[mode]
    a_spec = pl.BlockSpec((tk, tm), lambda i, j, kk: (kk, i)) if mode == "tn" else pl.BlockSpec((tm, tk), lambda i, j, kk: (i, kk))
    b_spec = pl.BlockSpec((tn, tk), lambda i, j, kk: (j, kk)) if mode == "nt" else pl.BlockSpec((tk, tn), lambda i, j, kk: (kk, j))
    o_spec = pl.BlockSpec((tm, tn), lambda i, j, kk: (i, j))
    has_add = add is not None

    def finish(prod, c_ref, o_ref):
        if scale != 1.0:
            prod = prod * scale
        o_ref[...] = ((c_ref[...] + prod) if has_add else prod).astype(out_dtype)

    def body(*refs):
        a_ref, b_ref = refs[:2]
        c_ref = refs[2] if has_add else None
        o_ref = refs[3] if has_add else refs[2]
        if nk == 1:
            finish(dot(a_ref[...], b_ref[...]), c_ref, o_ref)
            return
        acc_ref = refs[-1]
        kk = pl.program_id(2)

        @pl.when(kk == 0)
        def _():
            acc_ref[...] = jnp.zeros_like(acc_ref)

        acc_ref[...] += dot(a_ref[...], b_ref[...])

        @pl.when(kk == nk - 1)
        def _():
            finish(acc_ref[...], c_ref, o_ref)

    operands = (a, b, add) if has_add else (a, b)
    in_specs = [a_spec, b_spec] + ([o_spec] if has_add else [])
    (out,), got = _call(
        body, name=name, grid=(m // tm, n // tn, nk), in_specs=in_specs, out_specs=[o_spec],
        out_shape=[jax.ShapeDtypeStruct((m, n), out_dtype)], scratch_shapes=[pltpu.VMEM((tm, tn), F32)] if nk > 1 else [],
        operands=operands, sem=("parallel", "parallel", "arbitrary"), hosted=hosted)
    return out if hosted is None else (out, got)


def _rms_fwd(x, g, *, tm, name, hosted=None):
    t, d = x.shape

    def body(x_ref, g_ref, h_ref):
        xhat, _ = _rms_stats(x_ref[...])
        h_ref[...] = (xhat * g_ref[...]).astype(BF16)

    (h,), got = _call(
        body, name=name, grid=(t // tm,),
        in_specs=[pl.BlockSpec((tm, d), lambda i: (i, 0)), pl.BlockSpec((1, d), lambda i: (0, 0))],
        out_specs=[pl.BlockSpec((tm, d), lambda i: (i, 0))], out_shape=[jax.ShapeDtypeStruct((t, d), BF16)], scratch_shapes=[],
        operands=(x, g), sem=("parallel",), hosted=hosted)
    return h, got


def _ffn_fwd(x, g, wgT, wuT, wd, *, tm, hc, name, hosted=None, target=None):
    t, d = x.shape
    nj = DFF // hc
    with_loss = target is not None

    def body(*refs):
        x_ref, g_ref, wg_ref, wu_ref, wd_ref = refs[:5]
        t_ref = refs[5] if with_loss else None
        xo_ref, h_ref, a_ref, b_ref = refs[5 + with_loss:9 + with_loss]
        loss_ref = refs[9 + with_loss] if with_loss else None
        acc_ref = refs[-1]
        i, j = pl.program_id(0), pl.program_id(1)

        @pl.when(j == 0)
        def _():
            xhat, _ = _rms_stats(x_ref[...])
            h_ref[...] = (xhat * g_ref[...]).astype(BF16)
            acc_ref[...] = jnp.zeros_like(acc_ref)

        h = h_ref[...]
        a = _dot_nt(h, wg_ref[...])
        b = _dot_nt(h, wu_ref[...])
        a_ref[...] = a.astype(BF16)
        b_ref[...] = b.astype(BF16)
        s = (a * _sigmoid(a) * b).astype(BF16)
        acc_ref[...] += _dot_nn(s, wd_ref[...])

        if with_loss:
            @pl.when((i == 0) & (j == 0))
            def _():
                loss_ref[...] = jnp.zeros_like(loss_ref)

        @pl.when(j == nj - 1)
        def _():
            y = x_ref[...] + 0.5 * acc_ref[...]
            if with_loss:
                err = y - t_ref[...]
                xo_ref[...] = err * (1.0 / d)
                loss_ref[...] += jnp.sum(jnp.sum(err * err, axis=-1, keepdims=True), axis=0, keepdims=True) * (0.5 / d)
            else:
                xo_ref[...] = y

    row = pl.BlockSpec((tm, d), lambda i, j: (i, 0))
    vec = pl.BlockSpec((1, d), lambda i, j: (0, 0))
    wsp = pl.BlockSpec((hc, d), lambda i, j: (j, 0))
    hid = pl.BlockSpec((tm, hc), lambda i, j: (i, j))
    out_specs = [row, row, hid, hid] + ([pl.BlockSpec((1, 128), lambda i, j: (0, 0))] if with_loss else [])
    out_shape = [jax.ShapeDtypeStruct((t, d), F32), jax.ShapeDtypeStruct((t, d), BF16), jax.ShapeDtypeStruct((t, DFF), BF16),
                 jax.ShapeDtypeStruct((t, DFF), BF16)] + ([jax.ShapeDtypeStruct((1, 128), F32)] if with_loss else [])
    return _call(
        body, name=name, grid=(t // tm, nj), in_specs=[row, vec, wsp, wsp, wsp] + ([row] if with_loss else []),
        out_specs=out_specs, out_shape=out_shape, scratch_shapes=[pltpu.VMEM((tm, d), F32)],
        operands=(x, g, wgT, wuT, wd) + ((target,) if with_loss else ()),
        sem=("arbitrary" if with_loss else "parallel", "arbitrary"), hosted=hosted)


def _ffn_grads(dout, h, a, b, wd, *, tm, hc, name, hosted=None):
    t, d = dout.shape
    ni, nj = t // tm, DFF // hc

    def body(dout_ref, h_ref, a_ref, b_ref, wd_ref, da_ref, db_ref, dwg_ref, dwu_ref, dwd_ref,
             dy_all, h_all, ds_scr, s_scr, acc_g, acc_u, acc_d):
        j, i = pl.program_id(0), pl.program_id(1)
        rows_i = pl.ds(pl.multiple_of(i * tm, tm), tm)

        @pl.when(j == 0)
        def _():
            dy_all[rows_i, :] = (0.5 * dout_ref[...]).astype(BF16)
            h_all[rows_i, :] = h_ref[...]

        @pl.when(i == 0)
        def _():
            acc_g[...] = jnp.zeros_like(acc_g)
            acc_u[...] = jnp.zeros_like(acc_u)
            acc_d[...] = jnp.zeros_like(acc_d)

        def grad_rows(rows):
            ds = ds_scr[rows, :]
            av = a_ref[rows, :].astype(F32)
            bv = b_ref[rows, :].astype(F32)
            sg = _sigmoid(av)
            sl = av * sg
            s_scr[rows, :] = (sl * bv).astype(BF16)
            da_ref[rows, :] = (ds * bv * (sg + sl * (1.0 - sg))).astype(BF16)
            db_ref[rows, :] = (ds * sl).astype(BF16)

        for blk in range(tm // MM_ROWS):
            rs = slice(blk * MM_ROWS, (blk + 1) * MM_ROWS)
            ds_scr[rs, :] = _dot_nt(dy_all[pl.ds(pl.multiple_of(i * tm + blk * MM_ROWS, MM_ROWS), MM_ROWS), :], wd_ref[...])
            for c in range(MM_ROWS // ROWS_WIDE):
                grad_rows(slice(blk * MM_ROWS + c * ROWS_WIDE, blk * MM_ROWS + (c + 1) * ROWS_WIDE))

        dy_i = dy_all[rows_i, :]
        h_i = h_all[rows_i, :]
        acc_d[...] += _dot_tn(s_scr[...], dy_i)
        acc_g[...] += _dot_tn(da_ref[...], h_i)
        acc_u[...] += _dot_tn(db_ref[...], h_i)

        @pl.when(i == ni - 1)
        def _():
            dwg_ref[...] = acc_g[...].astype(BF16)
            dwu_ref[...] = acc_u[...].astype(BF16)
            dwd_ref[...] = acc_d[...].astype(BF16)

    first = pl.BlockSpec((tm, d), lambda j, i: (jnp.where(j == 0, i, 0), 0))
    hid = pl.BlockSpec((tm, hc), lambda j, i: (i, j))
    wsp = pl.BlockSpec((hc, d), lambda j, i: (j, 0))
    hid_shape = jax.ShapeDtypeStruct((t, DFF), BF16)
    w_shape = jax.ShapeDtypeStruct((DFF, d), BF16)
    return _call(
        body, name=name, grid=(nj, ni), in_specs=[first, first, hid, hid, wsp], out_specs=[hid, hid, wsp, wsp, wsp],
        out_shape=[hid_shape, hid_shape, w_shape, w_shape, w_shape],
        scratch_shapes=[pltpu.VMEM((t, d), BF16), pltpu.VMEM((t, d), BF16), pltpu.VMEM((tm, hc), F32), pltpu.VMEM((tm, hc), BF16),
                        pltpu.VMEM((hc, d), F32), pltpu.VMEM((hc, d), F32), pltpu.VMEM((hc, d), F32)],
        operands=(dout, h, a, b, wd), sem=("arbitrary", "arbitrary"), hosted=hosted)


def _proj_fwd(h, latT, convT, gateT, *, tm, name, hosted=None):
    t, d = h.shape

    def body(h_ref, wl_ref, wc_ref, wg_ref, lat_ref, conv_ref, gl_ref):
        hv = h_ref[...]
        lat_ref[...] = _dot_nt(hv, wl_ref[...]).astype(BF16)
        conv_ref[...] = _dot_nt(hv, wc_ref[...]).astype(BF16)
        gl_ref[...] = _dot_nt(hv, wg_ref[...]).astype(BF16)

    def rows(w):
        return pl.BlockSpec((tm, w), lambda i: (i, 0))

    def full(r):
        return pl.BlockSpec((r, d), lambda i: (0, 0))

    return _call(
        body, name=name, grid=(t // tm,), in_specs=[rows(d), full(LAT_PAD), full(CONV_COLS), full(GATE_COLS)],
        out_specs=[rows(LAT_PAD), rows(CONV_COLS), rows(GATE_COLS)],
        out_shape=[jax.ShapeDtypeStruct((t, LAT_PAD), BF16), jax.ShapeDtypeStruct((t, CONV_COLS), BF16),
                   jax.ShapeDtypeStruct((t, GATE_COLS), BF16)],
        scratch_shapes=[], operands=(h, latT, convT, gateT), sem=("parallel",), hosted=hosted)


def _proj_bwd(dlat, dconv3, dgl, latT, convT, gateT, x, g, dres, *, tm, name, hosted=None):
    t, d = x.shape

    def body(dl_ref, dc_ref, dg_ref, wl_ref, wc_ref, wg_ref, x_ref, g_ref, dres_ref, dx_ref, dgain_ref):
        @pl.when(pl.program_id(0) == 0)
        def _():
            dgain_ref[...] = jnp.zeros_like(dgain_ref)

        dh = _dot_nn(dl_ref[...], wl_ref[...]) + _dot_nn(dc_ref[...], wc_ref[...]) + _dot_nn(dg_ref[...], wg_ref[...])
        xhat, r = _rms_stats(x_ref[...])
        dx, dgain = _rms_bwd(dh, xhat, r, g_ref[...])
        dx_ref[...] = dres_ref[...] + dx
        dgain_ref[...] += dgain

    def rows(w):
        return pl.BlockSpec((tm, w), lambda i: (i, 0))

    def full(r):
        return pl.BlockSpec((r, d), lambda i: (0, 0))

    return _call(
        body, name=name, grid=(t // tm,),
        in_specs=[rows(LAT_PAD), rows(CONV_COLS), rows(GATE_COLS), full(LAT_PAD), full(CONV_COLS), full(GATE_COLS), rows(d), full(1), rows(d)],
        out_specs=[rows(d), full(1)], out_shape=[jax.ShapeDtypeStruct((t, d), F32), jax.ShapeDtypeStruct((1, d), F32)],
        scratch_shapes=[], operands=(dlat, dconv3, dgl, latT, convT, gateT, x, g, dres), sem=("arbitrary",), hosted=hosted)


def _ffn_up_bwd(da, db, wgT, wuT, x, g, dout, *, tm, name, hosted=None):
    t, d = x.shape

    def body(da_ref, db_ref, wg_ref, wu_ref, x_ref, g_ref, dout_ref, dx_ref, dg_ref):
        @pl.when(pl.program_id(0) == 0)
        def _():
            dg_ref[...] = jnp.zeros_like(dg_ref)

        dh = _dot_nn(da_ref[...], wg_ref[...]) + _dot_nn(db_ref[...], wu_ref[...])
        xhat, r = _rms_stats(x_ref[...])
        dx, dg = _rms_bwd(dh, xhat, r, g_ref[...])
        dx_ref[...] = dout_ref[...] + dx
        dg_ref[...] += dg

    row = pl.BlockSpec((tm, d), lambda i: (i, 0))
    vec = pl.BlockSpec((1, d), lambda i: (0, 0))
    hid = pl.BlockSpec((tm, DFF), lambda i: (i, 0))
    wsp = pl.BlockSpec((DFF, d), lambda i: (0, 0))
    return _call(
        body, name=name, grid=(t // tm,), in_specs=[hid, hid, wsp, wsp, row, vec, row], out_specs=[row, vec],
        out_shape=[jax.ShapeDtypeStruct((t, d), F32), jax.ShapeDtypeStruct((1, d), F32)], scratch_shapes=[],
        operands=(da, db, wgT, wuT, x, g, dout), sem=("arbitrary",), hosted=hosted)


HEAD_LANES = (slice(0, 32), slice(64, 80), None, slice(32, 64), slice(80, 96), None)


def _head_cols(a):
    def part(sl, width):
        if sl is None or sl.stop > a.shape[1]:
            return jnp.zeros((a.shape[0], width), a.dtype)
        return a[:, sl]

    return jnp.concatenate([part(sl, w) for sl, w in zip(HEAD_LANES, (32, 16, 16, 32, 16, 16))], axis=1)


def _head_cols_inv(a, dims):
    parts = [a[:, 0:32], a[:, 64:96]] + ([a[:, 32:48], a[:, 96:112]] if dims == QK_DIM else [])
    return jnp.concatenate(parts, axis=1)


def _rope_fwd(x, c, s):
    return x * c + pltpu.roll(x, HEAD_PAD // 2, 1) * s


def _rope_bwd(dy, c, s):
    return dy * c + pltpu.roll(dy * s, HEAD_PAD // 2, 1)


def _head_stats(x):
    r = lax.rsqrt(jnp.sum(x * x, axis=-1, keepdims=True) * (1.0 / QK_DIM) + NORM_EPS)
    return x * r, r


def _mla_prep_fwd(lat, gq, gkv, ghq, ghk, wq, wk, wv, rc, rs, *, tm, name):
    t = lat.shape[0]

    def body(lat_ref, gq_ref, gkv_ref, ghq_ref, ghk_ref, wq_ref, wk_ref, wv_ref, c_ref, s_ref,
             q_ref, k_ref, v_ref, qn_ref, ckv_ref):
        lat_v = lat_ref[...]
        qhat, _ = _rms_stats(lat_v[:, :Q_LORA].astype(F32))
        qn = (qhat * gq_ref[...]).astype(BF16)
        khat, _ = _rms_stats(lat_v[:, Q_LORA:Q_LORA + KV_LORA].astype(F32))
        ckv = (khat * gkv_ref[...]).astype(BF16)
        ckv_ext = jnp.concatenate([ckv, lat_v[:, Q_LORA + KV_LORA:]], axis=1)
        qn_ref[...] = qn
        ckv_ref[...] = ckv_ext
        q_pre = _dot_nn(qn, wq_ref[...])
        k_pre = _dot_nn(ckv_ext, wk_ref[...])
        v_ref[...] = _dot_nn(ckv, wv_ref[...]).astype(BF16)
        c, s = c_ref[...], s_ref[...]
        for h in range(N_HEADS):
            hs = slice(h * HEAD_PAD, (h + 1) * HEAD_PAD)
            xq, _ = _head_stats(q_pre[:, hs])
            q_ref[:, hs] = _rope_fwd(xq * ghq_ref[...], c, s).astype(BF16)
            xk, _ = _head_stats(k_pre[:, hs])
            k_ref[:, hs] = _rope_fwd(xk * ghk_ref[...], c, s).astype(BF16)

    def row(w):
        return pl.BlockSpec((tm, w), lambda i: (i, 0))

    def full(r, w):
        return pl.BlockSpec((r, w), lambda i: (0, 0))

    wide = jax.ShapeDtypeStruct((t, D), BF16)
    lat3 = jax.ShapeDtypeStruct((t, Q_LORA), BF16)
    return pl.pallas_call(
        body, name=name, grid=(t // tm,),
        in_specs=[row(LAT_PAD), full(1, Q_LORA), full(1, KV_LORA), full(1, HEAD_PAD), full(1, HEAD_PAD),
                  full(Q_LORA, D), full(Q_LORA, D), full(KV_LORA, D), row(HEAD_PAD), row(HEAD_PAD)],
        out_specs=[row(D), row(D), row(D), row(Q_LORA), row(Q_LORA)],
        out_shape=[wide, wide, wide, lat3, lat3],
        compiler_params=_params("parallel"),
    )(lat, gq, gkv, ghq, ghk, wq, wk, wv, rc, rs)


def _mla_prep_bwd(dq, dk, dv, lat, qn, ckv_ext, gq, gkv, ghq, ghk, wq, wk, wv, rc, rs, *, tm, name):
    t = lat.shape[0]

    def body(dq_ref, dk_ref, dv_ref, lat_ref, qn_ref, ckv_ref, gq_ref, gkv_ref, ghq_ref, ghk_ref, wq_ref, wk_ref, wv_ref,
             c_ref, s_ref, dlat_ref, dqp_ref, dkp_ref, dgq_ref, dgkv_ref, dghq_ref, dghk_ref):
        @pl.when(pl.program_id(0) == 0)
        def _():
            dgq_ref[...] = jnp.zeros_like(dgq_ref)
            dgkv_ref[...] = jnp.zeros_like(dgkv_ref)
            dghq_ref[...] = jnp.zeros_like(dghq_ref)
            dghk_ref[...] = jnp.zeros_like(dghk_ref)

        c, s = c_ref[...], s_ref[...]
        q_pre = _dot_nn(qn_ref[...], wq_ref[...])
        k_pre = _dot_nn(ckv_ref[...], wk_ref[...])

        def heads(pre, dy_ref, gh_ref, dgh_ref, out_ref):
            dgh = jnp.zeros((1, HEAD_PAD), F32)
            for h in range(N_HEADS):
                hs = slice(h * HEAD_PAD, (h + 1) * HEAD_PAD)
                d = _rope_bwd(dy_ref[:, hs].astype(F32), c, s)
                xhat, r = _head_stats(pre[:, hs])
                dgh = dgh + jnp.sum(d * xhat, axis=0, keepdims=True)
                dxh = d * gh_ref[...]
                dx = r * (dxh - xhat * (jnp.sum(dxh * xhat, axis=-1, keepdims=True) * (1.0 / QK_DIM)))
                out_ref[:, hs] = dx.astype(BF16)
            dgh_ref[...] += dgh

        heads(q_pre, dq_ref, ghq_ref, dghq_ref, dqp_ref)
        heads(k_pre, dk_ref, ghk_ref, dghk_ref, dkp_ref)
        dqn = _dot_nt(dqp_ref[...], wq_ref[...])
        dce = _dot_nt(dkp_ref[...], wk_ref[...])
        dckv = dce[:, :KV_LORA] + _dot_nt(dv_ref[...], wv_ref[...])
        lat_v = lat_ref[...]
        qhat, rq = _rms_stats(lat_v[:, :Q_LORA].astype(F32))
        dql, dgq = _rms_bwd(dqn, qhat, rq, gq_ref[...])
        khat, rk = _rms_stats(lat_v[:, Q_LORA:Q_LORA + KV_LORA].astype(F32))
        dkl, dgkv = _rms_bwd(dckv, khat, rk, gkv_ref[...])
        dgq_ref[...] += dgq
        dgkv_ref[...] += dgkv
        dlat_ref[...] = jnp.concatenate([dql, dkl, dce[:, KV_LORA:]], axis=1).astype(BF16)

    def row(w):
        return pl.BlockSpec((tm, w), lambda i: (i, 0))

    def full(r, w):
        return pl.BlockSpec((r, w), lambda i: (0, 0))

    return pl.pallas_call(
        body, name=name, grid=(t // tm,),
        in_specs=[row(D), row(D), row(D), row(LAT_PAD), row(Q_LORA), row(Q_LORA), full(1, Q_LORA), full(1, KV_LORA),
                  full(1, HEAD_PAD), full(1, HEAD_PAD), full(Q_LORA, D), full(Q_LORA, D), full(KV_LORA, D),
                  row(HEAD_PAD), row(HEAD_PAD)],
        out_specs=[row(LAT_PAD), row(D), row(D), full(1, Q_LORA), full(1, KV_LORA), full(1, HEAD_PAD), full(1, HEAD_PAD)],
        out_shape=[jax.ShapeDtypeStruct((t, LAT_PAD), BF16), jax.ShapeDtypeStruct((t, D), BF16), jax.ShapeDtypeStruct((t, D), BF16),
                   jax.ShapeDtypeStruct((1, Q_LORA), F32), jax.ShapeDtypeStruct((1, KV_LORA), F32),
                   jax.ShapeDtypeStruct((1, HEAD_PAD), F32), jax.ShapeDtypeStruct((1, HEAD_PAD), F32)],
        compiler_params=_params("arbitrary"),
    )(dq, dk, dv, lat, qn, ckv_ext, gq, gkv, ghq, ghk, wq, wk, wv, rc, rs)


def _causal_keep(tq):
    r = lax.broadcasted_iota(jnp.int32, (tq, tq), 0)
    c = lax.broadcasted_iota(jnp.int32, (tq, tq), 1)
    return c <= r


def _flash_fwd(q, k, v, *, n_seq, seq, tq, name, hosted=None):
    nq = seq // tq

    def body(q_ref, k_ref, v_ref, o_ref, lse_ref):
        qi = pl.program_id(2)
        qv = q_ref[...]

        def step(j, carry, masked):
            m, l, acc = carry
            kj = k_ref[pl.ds(pl.multiple_of(j * tq, tq), tq), :]
            vj = v_ref[pl.ds(pl.multiple_of(j * tq, tq), tq), :]
            s = _dot_nt(qv, kj) * ATTN_SCALE
            if masked:
                s = jnp.where(_causal_keep(tq), s, NEG)
            m_new = jnp.maximum(m, jnp.max(s, axis=-1, keepdims=True))
            alpha = jnp.exp(m - m_new)
            p = jnp.exp(s - m_new)
            l = alpha * l + jnp.sum(p, axis=-1, keepdims=True)
            acc = alpha * acc + _dot_nn(p.astype(BF16), vj)
            return m_new, l, acc

        init = (jnp.full((tq, 1), NEG, F32), jnp.zeros((tq, 1), F32), jnp.zeros((tq, HEAD_PAD), F32))
        carry = lax.fori_loop(0, qi, lambda j, cr: step(j, cr, False), init)
        m, l, acc = step(qi, carry, True)
        o_ref[...] = (acc / l).astype(BF16)
        lse_ref[...] = jnp.broadcast_to(m + jnp.log(l), (tq, HEAD_PAD))

    qspec = pl.BlockSpec((tq, HEAD_PAD), lambda b, h, i: (b * nq + i, h))
    kspec = pl.BlockSpec((seq, HEAD_PAD), lambda b, h, i: (b, h))
    t = n_seq * seq
    return _call(
        body, name=name, grid=(n_seq, N_HEADS, nq), in_specs=[qspec, kspec, kspec], out_specs=[qspec, qspec],
        out_shape=[jax.ShapeDtypeStruct((t, D), BF16), jax.ShapeDtypeStruct((t, D), F32)], scratch_shapes=[],
        operands=(q, k, v), sem=("parallel", "parallel", "arbitrary"), hosted=hosted)


def _flash_bwd(q, k, v, o, lse, do, *, n_seq, seq, tq, name, hosted=None):
    nq = seq // tq

    def body(q_ref, k_ref, v_ref, o_ref, lse_ref, do_ref, dq_ref, dk_ref, dv_ref, dk_acc, dv_acc):
        j = pl.program_id(2)

        @pl.when(j == 0)
        def _():
            dq_ref[...] = jnp.zeros_like(dq_ref)

        dk_acc[...] = jnp.zeros_like(dk_acc)
        dv_acc[...] = jnp.zeros_like(dv_acc)
        kv = k_ref[...]
        vv = v_ref[...]

        def step(i, masked):
            rows = pl.ds(pl.multiple_of(i * tq, tq), tq)
            qi = q_ref[rows, :]
            doi = do_ref[rows, :]
            delta = jnp.sum(doi.astype(F32) * o_ref[rows, :].astype(F32), axis=-1, keepdims=True)
            s = _dot_nt(qi, kv) * ATTN_SCALE
            p = jnp.exp(s - lse_ref[rows, :][:, :1])
            if masked:
                p = jnp.where(_causal_keep(tq), p, 0.0)
            dv_acc[...] += _dot_tn(p.astype(BF16), doi)
            dp = _dot_nt(doi, vv)
            ds = (p * (dp - delta) * ATTN_SCALE).astype(BF16)
            dk_acc[...] += _dot_tn(ds, qi)
            dq_ref[rows, :] += _dot_nn(ds, kv)

        step(j, True)

        def loop_body(i, carry):
            step(i, False)
            return carry

        lax.fori_loop(j + 1, nq, loop_body, 0)
        dk_ref[...] = dk_acc[...]
        dv_ref[...] = dv_acc[...].astype(BF16)

    full = pl.BlockSpec((seq, HEAD_PAD), lambda b, h, j: (b, h))
    tile = pl.BlockSpec((tq, HEAD_PAD), lambda b, h, j: (b * nq + j, h))
    t = n_seq * seq
    return _call(
        body, name=name, grid=(n_seq, N_HEADS, nq), in_specs=[full, tile, tile, full, full, full],
        out_specs=[full, tile, tile],
        out_shape=[jax.ShapeDtypeStruct((t, D), F32), jax.ShapeDtypeStruct((t, D), F32), jax.ShapeDtypeStruct((t, D), BF16)],
        scratch_shapes=[pltpu.VMEM((tq, HEAD_PAD), F32), pltpu.VMEM((tq, HEAD_PAD), F32)],
        operands=(q, k, v, o, lse, do), sem=("parallel", "parallel", "arbitrary"), hosted=hosted)


CONV_CB = 256


def _shift_down(u, k, row):
    return jnp.where(row >= k, pltpu.roll(u, k, 0), 0.0)


def _shift_up(u, k, row, n):
    return jnp.where(row < n - k, pltpu.roll(u, n - k, 0), 0.0)


def _conv_fwd(conv3, cw, *, n_seq, seq, name, hosted=None):
    def body(c_ref, w_ref, p_ref):
        blk = c_ref[...].astype(F32)
        xc, gb, gc = blk[:, :CONV_CB], blk[:, CONV_CB:2 * CONV_CB], blk[:, 2 * CONV_CB:]
        row = lax.broadcasted_iota(jnp.int32, (seq, CONV_CB), 0)
        u = gc * xc
        z = w_ref[0:1, :] * _shift_down(u, 2, row) + w_ref[1:2, :] * _shift_down(u, 1, row) + w_ref[2:3, :] * u
        p_ref[...] = (gb * z).astype(BF16)

    (p,), got = _call(
        body, name=name, grid=(n_seq, D // CONV_CB),
        in_specs=[pl.BlockSpec((seq, 3 * CONV_CB), lambda b, j: (b, j)), pl.BlockSpec((3, CONV_CB), lambda b, j: (0, j))],
        out_specs=[pl.BlockSpec((seq, CONV_CB), lambda b, j: (b, j))],
        out_shape=[jax.ShapeDtypeStruct((n_seq * seq, D), BF16)], scratch_shapes=[],
        operands=(conv3, cw), sem=("parallel", "parallel"), hosted=hosted)
    return p, got


def _conv_bwd(dp, conv3, cw, *, n_seq, seq, name):
    def body(dp_ref, c_ref, w_ref, dc_ref, dw_ref):
        @pl.when(pl.program_id(1) == 0)
        def _():
            dw_ref[...] = jnp.zeros_like(dw_ref)

        blk = c_ref[...].astype(F32)
        xc, gb, gc = blk[:, :CONV_CB], blk[:, CONV_CB:2 * CONV_CB], blk[:, 2 * CONV_CB:]
        row = lax.broadcasted_iota(jnp.int32, (seq, CONV_CB), 0)
        w0, w1, w2 = w_ref[0:1, :], w_ref[1:2, :], w_ref[2:3, :]
        u = gc * xc
        u1 = _shift_down(u, 1, row)
        u2 = _shift_down(u, 2, row)
        z = w0 * u2 + w1 * u1 + w2 * u
        dpv = dp_ref[...].astype(F32)
        dz = dpv * gb
        du = w2 * dz + w1 * _shift_up(dz, 1, row, seq) + w0 * _shift_up(dz, 2, row, seq)
        dc_ref[...] = jnp.concatenate([du * gc, dpv * z, du * xc], axis=1).astype(BF16)
        dw_ref[0:1, :] += jnp.sum(dz * u2, axis=0, keepdims=True)
        dw_ref[1:2, :] += jnp.sum(dz * u1, axis=0, keepdims=True)
        dw_ref[2:3, :] += jnp.sum(dz * u, axis=0, keepdims=True)

    return pl.pallas_call(
        body, name=name, grid=(D // CONV_CB, n_seq),
        in_specs=[pl.BlockSpec((seq, CONV_CB), lambda j, b: (b, j)), pl.BlockSpec((seq, 3 * CONV_CB), lambda j, b: (b, j)),
                  pl.BlockSpec((3, CONV_CB), lambda j, b: (0, j))],
        out_specs=[pl.BlockSpec((seq, 3 * CONV_CB), lambda j, b: (b, j)), pl.BlockSpec((3, CONV_CB), lambda j, b: (0, j))],
        out_shape=[jax.ShapeDtypeStruct((n_seq * seq, CONV_COLS), BF16), jax.ShapeDtypeStruct((3, D), F32)],
        compiler_params=_params("parallel", "arbitrary"),
    )(dp, conv3, cw)


def _merge_fwd(o, p, gl, bias, x1, wpa, wpc, wout, *, tm, name, hosted=None):
    t = x1.shape[0]

    def body(o_ref, p_ref, gl_ref, b_ref, x_ref, wpa_ref, wpc_ref, wout_ref, x2_ref, mg_ref, ya_ref, yb_ref):
        ya = _dot_nn(o_ref[...], wpa_ref[...])
        yb = _dot_nn(p_ref[...], wpc_ref[...])
        gates = _sigmoid(gl_ref[...].astype(F32) + b_ref[...])
        merged = (gates[:, :D] * ya + gates[:, D:] * yb).astype(BF16)
        ya_ref[...] = ya.astype(BF16)
        yb_ref[...] = yb.astype(BF16)
        mg_ref[...] = merged
        x2_ref[...] = x_ref[...] + _dot_nn(merged, wout_ref[...])

    row = pl.BlockSpec((tm, D), lambda i: (i, 0))
    row2 = pl.BlockSpec((tm, GATE_COLS), lambda i: (i, 0))
    wsp = pl.BlockSpec((D, D), lambda i: (0, 0))
    wide = jax.ShapeDtypeStruct((t, D), BF16)
    return _call(
        body, name=name, grid=(t // tm,),
        in_specs=[row, row, row2, pl.BlockSpec((1, GATE_COLS), lambda i: (0, 0)), row, wsp, wsp, wsp],
        out_specs=[row, row, row, row], out_shape=[jax.ShapeDtypeStruct((t, D), F32), wide, wide, wide], scratch_shapes=[],
        operands=(o, p, gl, bias, x1, wpa, wpc, wout), sem=("parallel",), hosted=hosted)


def _merge_bwd(dx2, ya, yb, gl, bias, wpa, wpc, wout, *, tm, name, hosted=None):
    t = dx2.shape[0]

    def body(dx_ref, ya_ref, yb_ref, gl_ref, b_ref, wpa_ref, wpc_ref, wout_ref,
             dxb_ref, dya_ref, dyb_ref, dgl_ref, do_ref, dp_ref, db_ref):
        @pl.when(pl.program_id(0) == 0)
        def _():
            db_ref[...] = jnp.zeros_like(db_ref)

        dxb = dx_ref[...].astype(BF16)
        dxb_ref[...] = dxb
        dm = _dot_nt(dxb, wout_ref[...])
        gates = _sigmoid(gl_ref[...].astype(F32) + b_ref[...])
        ga, gb = gates[:, :D], gates[:, D:]
        dya = (dm * ga).astype(BF16)
        dyb = (dm * gb).astype(BF16)
        dya_ref[...] = dya
        dyb_ref[...] = dyb
        dgl = jnp.concatenate([dm * ya_ref[...].astype(F32) * ga * (1.0 - ga),
                               dm * yb_ref[...].astype(F32) * gb * (1.0 - gb)], axis=1)
        dgl_ref[...] = dgl.astype(BF16)
        db_ref[...] += jnp.sum(dgl, axis=0, keepdims=True)
        do_ref[...] = _dot_nt(dya, wpa_ref[...]).astype(BF16)
        dp_ref[...] = _dot_nt(dyb, wpc_ref[...]).astype(BF16)

    row = pl.BlockSpec((tm, D), lambda i: (i, 0))
    row2 = pl.BlockSpec((tm, GATE_COLS), lambda i: (i, 0))
    vec2 = pl.BlockSpec((1, GATE_COLS), lambda i: (0, 0))
    wsp = pl.BlockSpec((D, D), lambda i: (0, 0))
    wide = jax.ShapeDtypeStruct((t, D), BF16)
    return _call(
        body, name=name, grid=(t // tm,), in_specs=[row, row, row, row2, vec2, wsp, wsp, wsp],
        out_specs=[row, row, row, row2, row, row, vec2],
        out_shape=[wide, wide, wide, jax.ShapeDtypeStruct((t, GATE_COLS), BF16), wide, wide,
                   jax.ShapeDtypeStruct((1, GATE_COLS), F32)],
        scratch_shapes=[], operands=(dx2, ya, yb, gl, bias, wpa, wpc, wout), sem=("arbitrary",), hosted=hosted)


def _adamw(w, g, m, v, *, name):
    rows, cols = w.shape
    tr = max([c for c in range(8, 513, 8) if rows % c == 0], default=rows)
    c1 = 1.0 / (1.0 - ADAM_B1 ** ADAM_STEP)
    c2 = 1.0 / (1.0 - ADAM_B2 ** ADAM_STEP)

    def body(w_ref, g_ref, m_ref, v_ref, d_ref, nm_ref, nv_ref):
        gv = g_ref[...]
        nm = ADAM_B1 * m_ref[...] + (1.0 - ADAM_B1) * gv
        nv = ADAM_B2 * v_ref[...] + (1.0 - ADAM_B2) * (gv * gv)
        nm_ref[...] = nm
        nv_ref[...] = nv
        d_ref[...] = -ADAM_LR * ((nm * c1) / (jnp.sqrt(nv * c2) + ADAM_EPS) + ADAM_WD * w_ref[...])

    spec = pl.BlockSpec((tr, cols), lambda i: (i, 0))
    shp = jax.ShapeDtypeStruct((rows, cols), F32)
    return pl.pallas_call(
        body, name=name, grid=(rows // tr,), in_specs=[spec] * 4, out_specs=[spec] * 3, out_shape=[shp] * 3,
        compiler_params=_params("parallel"),
    )(w, g, m, v)


def _place():
    return lax.axis_index("x"), lax.axis_index("y"), lax.axis_index("c")


def _other_chips(x, y):
    return [(1 - x, y), (x, 1 - y), (1 - x, 1 - y)]


def _remote(src, dst, send, recv, dev):
    return pltpu.make_async_remote_copy(src_ref=src, dst_ref=dst, send_sem=send, recv_sem=recv, device_id=dev, device_id_type=MESH)


def _gather_chips_plan(n):
    def start(srcs, dsts, send, recv, local):
        x, y, cc = _place()
        me = 4 * x + 2 * y + cc
        for a in range(n):
            pltpu.make_async_copy(srcs[a], dsts[a].at[me], local.at[a]).start()
            for k, (px, py) in enumerate(_other_chips(x, y)):
                _remote(srcs[a], dsts[a].at[me], send.at[3 * a + k], recv.at[3 * a + k], (px, py, cc)).start()

    def wait(srcs, dsts, send, recv, local):
        x, y, cc = _place()
        me = 4 * x + 2 * y + cc
        for a in range(n):
            for k, (px, py) in enumerate(_other_chips(x, y)):
                _remote(srcs[a], dsts[a].at[4 * px + 2 * py + cc], send.at[3 * a + k], recv.at[3 * a + k], (px, py, cc)).wait_recv()
        for a in range(n):
            for k, (px, py) in enumerate(_other_chips(x, y)):
                _remote(srcs[a], dsts[a].at[me], send.at[3 * a + k], recv.at[3 * a + k], (px, py, cc)).wait_send()
            pltpu.make_async_copy(srcs[a], dsts[a].at[me], local.at[a]).wait()

    return _Plan(start, wait, 3 * n, n)


def _scatter_chips_plan(n):
    def start(srcs, dsts, send, recv, local):
        x, y, cc = _place()
        for a in range(n):
            for k, (px, py) in enumerate(_other_chips(x, y)):
                _remote(srcs[a].at[2 * px + py], dsts[a].at[k], send.at[3 * a + k], recv.at[3 * a + k], (px, py, cc)).start()

    def wait(srcs, dsts, send, recv, local):
        x, y, cc = _place()
        for a in range(n):
            for k, (px, py) in enumerate(_other_chips(x, y)):
                _remote(srcs[a].at[k], dsts[a].at[k], send.at[3 * a + k], recv.at[3 * a + k], (px, py, cc)).wait_recv()
        for a in range(n):
            for k, (px, py) in enumerate(_other_chips(x, y)):
                _remote(srcs[a].at[k], dsts[a].at[k], send.at[3 * a + k], recv.at[3 * a + k], (px, py, cc)).wait_send()

    return _Plan(start, wait, 3 * n, 0)


def _gather_shapes(blocks):
    return [jax.ShapeDtypeStruct((N_DEV,) + b.shape, b.dtype) for b in blocks]


def _scatter_shapes(parts):
    return [jax.ShapeDtypeStruct((3,) + p.shape[1:], p.dtype) for p in parts]


def _gather_sibling_plan(n):
    def start(srcs, dsts, send, recv, local):
        x, y, cc = _place()
        for a in range(n):
            for q in range(4):
                _remote(srcs[a].at[2 * q + cc], dsts[a].at[2 * q + cc], send.at[4 * a + q], recv.at[4 * a + q], (x, y, 1 - cc)).start()

    def wait(srcs, dsts, send, recv, local):
        x, y, cc = _place()
        for a in range(n):
            for q in range(4):
                _remote(srcs[a].at[2 * q + cc], dsts[a].at[2 * q + 1 - cc], send.at[4 * a + q], recv.at[4 * a + q],
                        (x, y, 1 - cc)).wait_recv()
        for a in range(n):
            for q in range(4):
                _remote(srcs[a].at[2 * q + cc], dsts[a].at[2 * q + cc], send.at[4 * a + q], recv.at[4 * a + q],
                        (x, y, 1 - cc)).wait_send()

    return _Plan(start, wait, 4 * n, 0, in_place=True)


def _scatter_sibling_plan(n):
    def start(srcs, dsts, send, recv, local):
        x, y, cc = _place()
        for a in range(n):
            for q in range(4):
                _remote(srcs[a].at[2 * q + 1 - cc], dsts[a].at[q], send.at[4 * a + q], recv.at[4 * a + q], (x, y, 1 - cc)).start()

    def wait(srcs, dsts, send, recv, local):
        x, y, cc = _place()
        for a in range(n):
            for q in range(4):
                _remote(srcs[a].at[q], dsts[a].at[q], send.at[4 * a + q], recv.at[4 * a + q], (x, y, 1 - cc)).wait_recv()
        for a in range(n):
            for q in range(4):
                _remote(srcs[a].at[q], dsts[a].at[q], send.at[4 * a + q], recv.at[4 * a + q], (x, y, 1 - cc)).wait_send()

    return _Plan(start, wait, 4 * n, 0)


def _same_shapes(arrs):
    return [jax.ShapeDtypeStruct(a.shape, a.dtype) for a in arrs]


def _halved_shapes(parts):
    return [jax.ShapeDtypeStruct((4,) + p.shape[1:], p.dtype) for p in parts]


def _run_plan(plan, srcs, out_shapes, *, name):
    n_in, n_out = len(srcs), len(out_shapes)

    def body(*refs):
        h_in, h_out, sems = refs[:n_in], refs[n_in:n_in + n_out], refs[n_in + n_out:]
        plan.start(h_in, h_out, *sems)
        plan.wait(h_in, h_out, *sems)

    return pl.pallas_call(body, name=name, in_specs=[ANY] * n_in, out_specs=[ANY] * n_out, out_shape=list(out_shapes),
                          input_output_aliases={a: a for a in range(n_in)} if plan.in_place else {},
                          scratch_shapes=plan.sems())(*srcs)


def _sum_sibling(p, q, core, *, name):
    _, r, c = p.shape

    def body(core_ref, p_ref, q_ref, o_ref):
        o_ref[...] = (p_ref[...].astype(F32) + q_ref[...].astype(F32)).astype(BF16)

    grid_spec = pltpu.PrefetchScalarGridSpec(
        num_scalar_prefetch=1, grid=(4,),
        in_specs=[pl.BlockSpec((1, r, c), lambda ch, core_ref: (2 * ch + core_ref[0], 0, 0)),
                  pl.BlockSpec((1, r, c), lambda ch, core_ref: (ch, 0, 0))],
        out_specs=pl.BlockSpec((1, r, c), lambda ch, core_ref: (ch, 0, 0)))
    return pl.pallas_call(
        body, name=name, grid_spec=grid_spec, out_shape=jax.ShapeDtypeStruct((4, r, c), BF16),
        compiler_params=_params("parallel"),
    )(core, p, q)


def _sum_chips(s1, r2, chip, *, name):
    _, r, c = s1.shape

    def body(chip_ref, s_ref, r_ref, o_ref):
        acc = s_ref[0].astype(F32)
        for k in range(3):
            acc = acc + r_ref[k].astype(F32)
        o_ref[...] = acc

    grid_spec = pltpu.PrefetchScalarGridSpec(
        num_scalar_prefetch=1, grid=(1,),
        in_specs=[pl.BlockSpec((1, r, c), lambda i, chip_ref: (chip_ref[0], 0, 0)),
                  pl.BlockSpec((3, r, c), lambda i, chip_ref: (0, 0, 0))],
        out_specs=pl.BlockSpec((r, c), lambda i, chip_ref: (0, 0)))
    return pl.pallas_call(
        body, name=name, grid_spec=grid_spec, out_shape=jax.ShapeDtypeStruct((r, c), F32),
        compiler_params=_params("arbitrary"),
    )(chip, s1, r2)


def _sum_adamw(s1, r2, chip, w, m, v, *, name):
    _, r, c = s1.shape
    c1 = 1.0 / (1.0 - ADAM_B1 ** ADAM_STEP)
    c2 = 1.0 / (1.0 - ADAM_B2 ** ADAM_STEP)

    def body(chip_ref, s_ref, r_ref, w_ref, m_ref, v_ref, g_ref, d_ref, nm_ref, nv_ref):
        gv = s_ref[0].astype(F32)
        for k in range(3):
            gv = gv + r_ref[k].astype(F32)
        g_ref[...] = gv
        nm = ADAM_B1 * m_ref[...] + (1.0 - ADAM_B1) * gv
        nv = ADAM_B2 * v_ref[...] + (1.0 - ADAM_B2) * (gv * gv)
        nm_ref[...] = nm
        nv_ref[...] = nv
        d_ref[...] = -ADAM_LR * ((nm * c1) / (jnp.sqrt(nv * c2) + ADAM_EPS) + ADAM_WD * w_ref[...])

    flat = pl.BlockSpec((r, c), lambda i, chip_ref: (0, 0))
    grid_spec = pltpu.PrefetchScalarGridSpec(
        num_scalar_prefetch=1, grid=(1,),
        in_specs=[pl.BlockSpec((1, r, c), lambda i, chip_ref: (chip_ref[0], 0, 0)),
                  pl.BlockSpec((3, r, c), lambda i, chip_ref: (0, 0, 0)), flat, flat, flat],
        out_specs=[flat] * 4)
    return pl.pallas_call(
        body, name=name, grid_spec=grid_spec, out_shape=[jax.ShapeDtypeStruct((r, c), F32)] * 4,
        compiler_params=_params("arbitrary"),
    )(chip, s1, r2, w, m, v)


def _small_exchange(v, *, reduce, name):
    r, c = v.shape

    def body(x_ref, o_ref, *rest):
        if reduce:
            buf_ref, send_sems, recv_sems = rest
        else:
            buf_ref = o_ref
            send_sems, recv_sems = rest
        x, y, cc = _place()
        me = 4 * x + 2 * y + cc

        def peer(k):
            return ((1 - x) if k & 4 else x, (1 - y) if k & 2 else y, (1 - cc) if k & 1 else cc)

        buf_ref[me] = x_ref[...]
        sends = []
        for k in range(1, N_DEV):
            cp = pltpu.make_async_remote_copy(src_ref=x_ref, dst_ref=buf_ref.at[me], send_sem=send_sems.at[k - 1],
                                              recv_sem=recv_sems.at[k - 1], device_id=peer(k), device_id_type=MESH)
            cp.start()
            sends.append(cp)
        for k in range(1, N_DEV):
            px, py, pc = peer(k)
            pltpu.make_async_remote_copy(src_ref=x_ref, dst_ref=buf_ref.at[4 * px + 2 * py + pc], send_sem=send_sems.at[k - 1],
                                         recv_sem=recv_sems.at[k - 1], device_id=peer(k), device_id_type=MESH).wait_recv()
        for cp in sends:
            cp.wait_send()
        if reduce:
            acc = buf_ref[0]
            for s in range(1, N_DEV):
                acc = acc + buf_ref[s]
            o_ref[...] = acc

    vm = pl.BlockSpec(memory_space=pltpu.VMEM)
    sems = [pltpu.SemaphoreType.DMA((N_DEV - 1,)), pltpu.SemaphoreType.DMA((N_DEV - 1,))]
    if reduce:
        out_shape, scratch = jax.ShapeDtypeStruct((r, c), F32), [pltpu.VMEM((N_DEV, r, c), F32)] + sems
    else:
        out_shape, scratch = jax.ShapeDtypeStruct((N_DEV, r, c), F32), sems
    return pl.pallas_call(body, name=name, in_specs=[vm], out_specs=vm, out_shape=out_shape, scratch_shapes=scratch)(v)


def _rows(a):
    return a.reshape(-1, D)


def _pad_cols(a, to):
    return jnp.pad(a, ((0, 0), (0, to - a.shape[1])))


def _pack_weights(w):
    parts = {
        "w_inT": jnp.pad(w["w_in"].T, ((0, IN_SHARD_PAD - IN_SHARD), (0, 0))),
        "w_uq": _rows(_head_cols(w["w_uq"])), "w_uk": _rows(_head_cols(w["w_uk"])),
        "w_uv": _rows(_pad_cols(w["w_uv"], HEAD_PAD)), "w_pa": _rows(w["w_proj_attn"]),
        "w_pc": w["w_proj_conv"], "w_out": w["w_out"],
    }
    return [jnp.concatenate([parts[n].astype(BF16) for n, _ in group], axis=0) for group in PACK]


def _cols_from_shards(gs, name, rows):
    idx, off, r = PACK_OFF[name]
    return gs[idx][:, off:off + r].reshape(N_DEV, rows, HEAD_PAD).transpose(1, 0, 2).reshape(rows, N_DEV * HEAD_PAD)


def _rows_from_shards(gs, name, keep=None):
    idx, off, r = PACK_OFF[name]
    keep = r if keep is None else keep
    return gs[idx][:, off:off + keep].reshape(N_DEV * keep, D)


def _rope_placement():
    i = lax.broadcasted_iota(jnp.int32, (HEAD_PAD, D), 0)
    j = lax.broadcasted_iota(jnp.int32, (HEAD_PAD, D), 1)
    lane = jnp.where(i < ROPE_HALF, 32 + i, 96 + i - ROPE_HALF)
    return ((i < 2 * ROPE_HALF) & (j % HEAD_PAD == lane)).astype(BF16)


def _unpack_in(g_in):
    w_inT = _rows_from_shards([g_in, None], "w_inT", IN_SHARD)
    lat_rows = Q_LORA + KV_LORA + 2 * ROPE_HALF
    conv = w_inT[lat_rows:lat_rows + CONV_COLS].reshape(3, D // CONV_CB, CONV_CB, D).transpose(1, 0, 2, 3).reshape(CONV_COLS, D)
    return {"latT": jnp.pad(w_inT[:lat_rows], ((0, LAT_PAD - lat_rows), (0, 0))), "convT": conv,
            "gateT": w_inT[lat_rows + CONV_COLS:]}


def _unpack_misc(g_misc):
    g = [None, g_misc]
    wpa = _cols_from_shards(g, "w_pa", 512).reshape(N_HEADS, NOPE, D)
    return {
        "wq": _cols_from_shards(g, "w_uq", Q_LORA),
        "wk": jnp.concatenate([_cols_from_shards(g, "w_uk", KV_LORA), _rope_placement()], axis=0),
        "wv": _cols_from_shards(g, "w_uv", KV_LORA),
        "wpa": jnp.pad(wpa, ((0, 0), (0, HEAD_PAD - NOPE), (0, 0))).reshape(D, D),
        "wpc": _rows_from_shards(g, "w_pc"), "wout": _rows_from_shards(g, "w_out"),
    }


def _shards_from_cols(a):
    rows = a.shape[0]
    return a.reshape(rows, N_DEV, HEAD_PAD).transpose(1, 0, 2).reshape(N_DEV, rows * HEAD_PAD // D, D)


def _pack_grads(gw):
    lat_rows = Q_LORA + KV_LORA + 2 * ROPE_HALF
    conv = gw["convT"].reshape(D // CONV_CB, 3, CONV_CB, D).transpose(1, 0, 2, 3).reshape(CONV_COLS, D)
    w_inT = jnp.concatenate([gw["latT"][:lat_rows], conv, gw["gateT"]], axis=0).reshape(N_DEV, IN_SHARD, D)
    wpa = gw["wpa"].reshape(N_HEADS, HEAD_PAD, D)[:, :NOPE].reshape(N_HEADS * NOPE, D)
    parts = {}
    parts.update({
        "w_inT": jnp.pad(w_inT, ((0, 0), (0, IN_SHARD_PAD - IN_SHARD), (0, 0))),
        "w_uq": _shards_from_cols(gw["wq"]), "w_uk": _shards_from_cols(gw["wk"][:KV_LORA]),
        "w_uv": _shards_from_cols(gw["wv"][:KV_LORA]), "w_pa": _shards_from_cols(wpa),
        "w_pc": gw["wpc"].reshape(N_DEV, D // N_DEV, D), "w_out": gw["wout"].reshape(N_DEV, D // N_DEV, D),
    })
    return [jnp.concatenate([parts[n] for n, _ in group], axis=1) for group in PACK]


def _unpack_grads(mines):
    def seg(name, keep=None):
        idx, off, r = PACK_OFF[name]
        return mines[idx][off:off + (r if keep is None else keep)]

    return {
        "w_in": seg("w_inT", IN_SHARD).T,
        "w_uq": _head_cols_inv(seg("w_uq").reshape(Q_LORA, HEAD_PAD), QK_DIM),
        "w_uk": _head_cols_inv(seg("w_uk").reshape(KV_LORA, HEAD_PAD), NOPE),
        "w_uv": seg("w_uv").reshape(KV_LORA, HEAD_PAD)[:, :NOPE],
        "w_proj_attn": seg("w_pa").reshape(512, HEAD_PAD),
        "w_proj_conv": seg("w_pc"), "w_out": seg("w_out"),
    }


def _rope_tables(positions):
    inv_freq = 1.0 / (ROPE_THETA ** (jnp.arange(ROPE_HALF, dtype=F32) / ROPE_HALF))
    ang = positions.reshape(-1).astype(F32)[:, None] * inv_freq
    cos, sin = jnp.cos(ang), jnp.sin(ang)
    t = ang.shape[0]
    zero = jnp.zeros((t, ROPE_HALF), F32)
    ones = jnp.ones((t, NOPE // 2), F32)
    none = jnp.zeros((t, NOPE // 2), F32)
    rc = jnp.concatenate([ones, cos, zero, ones, cos, zero], axis=1)
    rs = jnp.concatenate([none, -sin, zero, none, sin, zero], axis=1)
    return rc, rs


def _local_step(x, positions, target, conv_w, small, ex):
    n_seq, seq, d = x.shape
    t = n_seq * seq
    x0 = x.reshape(t, d)
    tgt = target.reshape(t, d)
    rc, rs = _rope_tables(positions)
    ghq = _head_cols(small["q_head_norm"])
    ghk = _head_cols(small["k_head_norm"])
    TM, HC, TQ = 1024, 256, 1024

    def mm(*args, hosted=None, **kw):
        res = _mm(*args, hosted=hosted, **kw)
        return res if hosted is not None else (res, None)

    def wgrad(a, b, name, tm=None, hosted=None):
        tm = tm or a.shape[1]
        return mm(a, b, mode="tn", out_dtype=BF16, tm=tm, tn=b.shape[1], tk=2048 if tm <= D else 1024, name=name, hosted=hosted)

    f1g, f1u, f1d = ex.gather_now("ffn1")
    (x1, h1, a1, b1), got = _ffn_fwd(x0, small["ffn1_norm"], f1g, f1u, f1d, tm=512, hc=DFF // 2, name="ffn1_fwd",
                                     hosted=ex.gather_chips("mix_in"))
    hm, got = _rms_fwd(x1, small["mix_norm"], tm=TM, name="mix_norm_fwd", hosted=ex.gather_sibling(got))
    W = ex.mix_in_weights(got)
    (lat, conv3, gl), got = _proj_fwd(hm, W["latT"], W["convT"], W["gateT"], tm=512, name="proj_fwd",
                                      hosted=ex.gather_chips("mix_misc"))
    p, got = _conv_fwd(conv3, conv_w, n_seq=n_seq, seq=seq, name="conv_fwd", hosted=ex.gather_sibling(got))
    W.update(ex.mix_misc_weights(got))
    q, k, v, qn, ckv = _mla_prep_fwd(lat, small["q_a_norm"], small["kv_a_norm"], ghq, ghk, W["wq"], W["wk"], W["wv"], rc, rs,
                                     tm=512, name="mla_prep_fwd")
    (o, lse), got = _flash_fwd(q, k, v, n_seq=n_seq, seq=seq, tq=TQ, name="attn_fwd", hosted=ex.gather_chips("ffn2"))
    (x2, merged, ya, yb), got = _merge_fwd(o, p, gl, small["gate_bias"], x1, W["wpa"], W["wpc"], W["wout"], tm=512, name="merge_fwd",
                                           hosted=ex.gather_sibling(got))
    f2g, f2u, f2d = ex.ffn_weights(got)
    (dy, h2, a2, b2, loss_row), _ = _ffn_fwd(x2, small["ffn2_norm"], f2g, f2u, f2d, tm=512, hc=DFF // 2, name="ffn2_fwd", target=tgt)

    gw, gs = {}, {}
    (da2, db2, *ffn2_grads), _ = _ffn_grads(dy, h2, a2, b2, f2d, tm=TM, hc=HC, name="ffn2_grads")
    (dx2, gs["ffn2_norm"]), _ = _ffn_up_bwd(da2, db2, f2g, f2u, x2, small["ffn2_norm"], dy, tm=512, name="ffn2_up_bwd")

    (dx2b, dya, dyb, dgl, do, dp, gs["gate_bias"]), got = _merge_bwd(
        dx2, ya, yb, gl, small["gate_bias"], W["wpa"], W["wpc"], W["wout"], tm=512, name="merge_bwd",
        hosted=ex.scatter_sibling("ffn2", ffn2_grads))
    ex.scatter_sibling_done("ffn2", got)
    gw["wout"] = wgrad(merged, dx2b, "dw_out")[0]
    gw["wpa"] = wgrad(o, dya, "dw_pa")[0]
    gw["wpc"] = wgrad(p, dyb, "dw_pc")[0]
    dconv3, dconv_w = _conv_bwd(dp, conv3, conv_w, n_seq=n_seq, seq=seq, name="conv_bwd")
    (dq, dk, dv), got = _flash_bwd(q, k, v, o, lse, do, n_seq=n_seq, seq=seq, tq=TQ, name="attn_bwd",
                                   hosted=ex.scatter_chips("ffn2"))
    ex.scatter_chips_done("ffn2", got)
    dlat, dqp, dkp, gs["q_a_norm"], gs["kv_a_norm"], dghq, dghk = _mla_prep_bwd(
        dq, dk, dv, lat, qn, ckv, small["q_a_norm"], small["kv_a_norm"], ghq, ghk, W["wq"], W["wk"], W["wv"], rc, rs,
        tm=512, name="mla_prep_bwd")
    gs["q_head_norm"], gs["k_head_norm"] = _head_cols_inv(dghq, QK_DIM), _head_cols_inv(dghk, QK_DIM)
    gw["wq"] = wgrad(qn, dqp, "dw_uq")[0]
    gw["wk"] = wgrad(ckv, dkp, "dw_uk")[0]
    gw["wv"] = wgrad(ckv, dv, "dw_uv")[0]
    gw["convT"] = wgrad(dconv3, hm, "dw_conv", tm=CONV_COLS // 2)[0]
    gw["gateT"] = wgrad(dgl, hm, "dw_gate")[0]
    gw["latT"] = wgrad(dlat, hm, "dw_lat")[0]
    ex.scatter_sibling_now("mix", gw)
    (dx1, gs["mix_norm"]), got = _proj_bwd(dlat, dconv3, dgl, W["latT"], W["convT"], W["gateT"], x1, small["mix_norm"], dx2,
                                           tm=512, name="proj_bwd", hosted=ex.scatter_chips("mix_in"))
    ex.scatter_chips_done("mix_in", got)

    (da1, db1, *ffn1_grads), got = _ffn_grads(dx1, h1, a1, b1, f1d, tm=TM, hc=HC, name="ffn1_grads",
                                              hosted=ex.scatter_chips("mix_misc"))
    ex.scatter_chips_done("mix_misc", got)
    ex.scatter_sibling_now("ffn1", ffn1_grads)
    (dx0, gs["ffn1_norm"]), got = _ffn_up_bwd(da1, db1, f1g, f1u, x0, small["ffn1_norm"], dx1, tm=512, name="ffn1_up_bwd",
                                              hosted=ex.scatter_chips("ffn1"))
    ex.scatter_chips_done("ffn1", got)
    return loss_row, dx0.reshape(n_seq, seq, d), dconv_w, gs


class _MeshExchange:
    def __init__(self, w, core, chip):
        self.w, self.core, self.chip = w, core, chip
        self.partial, self.received, self._packed = {}, {}, None

    def _blocks(self, group):
        w = self.w
        if group.startswith("ffn"):
            return [w[group + "_w_gate"].T.astype(BF16), w[group + "_w_up"].T.astype(BF16), w[group + "_w_down"].astype(BF16)]
        if self._packed is None:
            self._packed = _pack_weights(w)
        return [self._packed[0 if group == "mix_in" else 1]]

    def gather_chips(self, *groups):
        blocks = [b for group in groups for b in self._blocks(group)]
        return _gather_chips_plan(len(blocks)), blocks, _gather_shapes(blocks)

    def gather_sibling(self, got):
        half = list(got)
        return _gather_sibling_plan(len(half)), half, _same_shapes(half)

    def gather_now(self, group):
        plan, blocks, shapes = self.gather_chips(group)
        half = list(_run_plan(plan, blocks, shapes, name="gather_%s_chips" % group))
        return self.ffn_weights(_run_plan(_gather_sibling_plan(len(half)), half, _same_shapes(half), name="gather_%s_sibling" % group))

    def ffn_weights(self, got):
        return [a.reshape(DFF, D) for a in got]

    def mix_in_weights(self, got):
        return _unpack_in(got[0])

    def mix_misc_weights(self, got):
        return _unpack_misc(got[0])

    def _parts(self, group, grads):
        if group == "mix":
            return _pack_grads(grads), ["mix_in", "mix_misc"]
        parts = [g.reshape(N_DEV, -1, D) for g in grads]
        return parts, ([group] if len(parts) == 1 else None)

    def scatter_sibling(self, group, grads):
        self._sent, self._names = self._parts(group, grads)
        return _scatter_sibling_plan(len(self._sent)), self._sent, _halved_shapes(self._sent)

    def scatter_sibling_done(self, group, got):
        sums = [_sum_sibling(p, q, self.core, name="sum_%s_sibling_%d" % (group, i)) for i, (p, q) in enumerate(zip(self._sent, got))]
        if self._names is None:
            self.partial[group] = sums
        else:
            for n, s in zip(self._names, sums):
                self.partial[n] = [s]

    def scatter_sibling_now(self, group, grads):
        plan, parts, shapes = self.scatter_sibling(group, grads)
        self.scatter_sibling_done(group, _run_plan(plan, parts, shapes, name="scatter_%s_sibling" % group))

    def scatter_chips(self, group):
        s1 = self.partial[group]
        return _scatter_chips_plan(len(s1)), s1, _scatter_shapes(s1)

    def scatter_chips_done(self, group, got):
        self.received[group] = list(got)


SMALL_NAMES = ("ffn1_norm", "mix_norm", "gate_bias", "q_a_norm", "kv_a_norm", "q_head_norm", "k_head_norm", "ffn2_norm")
SMALL_SLOTS = {"ffn1_norm": 1024, "mix_norm": 1024, "gate_bias": 2048, "q_a_norm": 384, "kv_a_norm": 256, "q_head_norm": 128,
               "k_head_norm": 128, "ffn2_norm": 1024, "conv_w": 3072, "loss": 128}
COLUMN_MAJOR = ("w_in", "w_uq", "w_uk", "w_uv")
WEIGHT_NAMES = ("ffn1_norm", "ffn1_w_gate", "ffn1_w_up", "ffn1_w_down", "mix_norm", "w_in", "gate_bias", "q_a_norm", "w_uq",
                "kv_a_norm", "w_uk", "w_uv", "q_head_norm", "k_head_norm", "w_proj_attn", "conv_w", "w_proj_conv", "w_out",
                "ffn2_norm", "ffn2_w_gate", "ffn2_w_up", "ffn2_w_down")


def _step(x, positions, loss_target, w, m, v):
    xi, yi, ci = _place()
    core = ci.astype(jnp.int32).reshape(1)
    chip = (2 * xi + yi).astype(jnp.int32).reshape(1)
    me = 4 * xi + 2 * yi + ci

    cw_all = _small_exchange(jnp.pad(w["conv_w"], ((0, 5), (0, 0))), reduce=False, name="gather_conv_w")
    conv_w = cw_all[:, :3].transpose(1, 0, 2).reshape(3, D)
    small = {n: w[n].reshape(1, -1) for n in SMALL_NAMES}
    ex = _MeshExchange(w, core, chip)

    loss_row, grad_x, dconv_w, gs = _local_step(x, positions, loss_target, conv_w, small, ex)

    grads, deltas, new_m, new_v = {}, {}, {}, {}
    where = {"ffn1_w_gate": ("ffn1", 0), "ffn1_w_up": ("ffn1", 1), "ffn1_w_down": ("ffn1", 2),
             "ffn2_w_gate": ("ffn2", 0), "ffn2_w_up": ("ffn2", 1), "ffn2_w_down": ("ffn2", 2)}
    for n, (group, i) in where.items():
        transposed = not n.endswith("down")
        wv, mv, vv = (a[n].T if transposed else a[n] for a in (w, m, v))
        res = _sum_adamw(ex.partial[group][i], ex.received[group][i], chip, wv, mv, vv, name="adamw_" + n)
        grads[n], deltas[n], new_m[n], new_v[n] = (r.T if transposed else r for r in res)
    grads.update(_unpack_grads([_sum_chips(ex.partial[g][0], ex.received[g][0], chip, name="sum_%s_chips" % g)
                                for g in ("mix_in", "mix_misc")]))

    pieces = [_pad_cols(gs[n], SMALL_SLOTS[n]) for n in SMALL_NAMES] + [dconv_w.reshape(1, 3 * D), loss_row]
    total = _small_exchange(jnp.concatenate(pieces, axis=1).reshape(-1, 128), reduce=True, name="reduce_small").reshape(-1)
    off = 0
    for n in SMALL_NAMES:
        grads[n] = total[off:off + w[n].shape[0]]
        off += SMALL_SLOTS[n]
    conv_full = total[off:off + 3 * D].reshape(3, D)
    grads["conv_w"] = lax.dynamic_slice(conv_full, (0, me * HEAD_PAD), (3, HEAD_PAD))
    loss = total[off + 3 * D]

    for n in WEIGHT_NAMES:
        if n in deltas:
            continue
        shape = w[n].shape
        if n in COLUMN_MAJOR:
            ops = [a.T for a in (w[n], grads[n], m[n], v[n])]
            deltas[n], new_m[n], new_v[n] = (r.T for r in _adamw(*ops, name="adamw_" + n))
            continue
        if len(shape) == 1:
            view = (-1, 128) if shape[0] % 128 == 0 else (1, shape[0])
        else:
            view = shape
        dlt, nm, nv = _adamw(w[n].reshape(view), grads[n].reshape(view), m[n].reshape(view), v[n].reshape(view), name="adamw_" + n)
        deltas[n], new_m[n], new_v[n] = dlt.reshape(shape), nm.reshape(shape), nv.reshape(shape)
    return (loss, grad_x, *[grads[n] for n in WEIGHT_NAMES], *[deltas[n] for n in WEIGHT_NAMES],
            *[new_m[n] for n in WEIGHT_NAMES], *[new_v[n] for n in WEIGHT_NAMES])


def kernel(x, positions, ffn1_norm, ffn1_w_gate, ffn1_w_up, ffn1_w_down, mix_norm, w_in, gate_bias, q_a_norm, w_uq, kv_a_norm, w_uk, w_uv, q_head_norm, k_head_norm, w_proj_attn, conv_w, w_proj_conv, w_out, ffn2_norm, ffn2_w_gate, ffn2_w_up, ffn2_w_down, loss_target, m_ffn1_norm, m_ffn1_w_gate, m_ffn1_w_up, m_ffn1_w_down, m_mix_norm, m_w_in, m_gate_bias, m_q_a_norm, m_w_uq, m_kv_a_norm, m_w_uk, m_w_uv, m_q_head_norm, m_k_head_norm, m_w_proj_attn, m_conv_w, m_w_proj_conv, m_w_out, m_ffn2_norm, m_ffn2_w_gate, m_ffn2_w_up, m_ffn2_w_down, v_ffn1_norm, v_ffn1_w_gate, v_ffn1_w_up, v_ffn1_w_down, v_mix_norm, v_w_in, v_gate_bias, v_q_a_norm, v_w_uq, v_kv_a_norm, v_w_uk, v_w_uv, v_q_head_norm, v_k_head_norm, v_w_proj_attn, v_conv_w, v_w_proj_conv, v_w_out, v_ffn2_norm, v_ffn2_w_gate, v_ffn2_w_up, v_ffn2_w_down):
    given = dict(locals())
    w = {n: given[n] for n in WEIGHT_NAMES}
    m = {n: given["m_" + n] for n in WEIGHT_NAMES}
    v = {n: given["v_" + n] for n in WEIGHT_NAMES}
    return _step(x, positions, loss_target, w, m, v)
```

```python
import functools

import jax
import jax.numpy as jnp
from jax import lax
from jax.experimental import pallas as pl
from jax.experimental.pallas import tpu as pltpu

F32 = jnp.float32
BF16 = jnp.bfloat16
MESH = pl.DeviceIdType.MESH
ANY = pl.BlockSpec(memory_space=pl.ANY)

N_DEV = 8
D = 1024
DFF = 2816
N_HEADS = 8
HEAD_PAD = 128
QK_DIM = 96
NOPE = 64
ROPE_HALF = 16
Q_LORA = 384
KV_LORA = 256
LAT_PAD = 768
CONV_COLS = 3072
GATE_COLS = 2048
IN_DIM = 5792
IN_SHARD = IN_DIM // N_DEV
IN_SHARD_PAD = 736
FF_SHARD = DFF // N_DEV
ROPE_THETA = 10000.0
NORM_EPS = 1e-6
ATTN_SCALE = QK_DIM ** -0.5
NEG = -1e30

ADAM_LR, ADAM_B1, ADAM_B2, ADAM_EPS, ADAM_WD, ADAM_STEP = 0.001, 0.9, 0.999, 1e-08, 0.01, 10

PACK = ((("w_inT", IN_SHARD_PAD),), (("w_uq", 48), ("w_uk", 32), ("w_uv", 32), ("w_pa", 64), ("w_pc", 128), ("w_out", 128)))
PACK_OFF = {}
for _i, _group in enumerate(PACK):
    _o = 0
    for _n, _r in _group:
        PACK_OFF[_n] = (_i, _o, _r)
        _o += _r

VMEM_LIMIT = 56 * 1024 * 1024


def _params(*sem):
    return pltpu.CompilerParams(dimension_semantics=sem if sem else None, vmem_limit_bytes=VMEM_LIMIT)


class _Plan:
    def __init__(self, start, wait, n_remote, n_local, in_place=False):
        self.start, self.wait, self.n_remote, self.n_local, self.in_place = start, wait, n_remote, n_local, in_place

    def sems(self):
        return [pltpu.SemaphoreType.DMA((self.n_remote,)), pltpu.SemaphoreType.DMA((self.n_remote,)),
                pltpu.SemaphoreType.DMA((max(self.n_local, 1),))]


def _call(body, *, name, grid, in_specs, out_specs, out_shape, scratch_shapes, operands, sem, hosted=None):
    if hosted is None:
        outs = pl.pallas_call(body, name=name, grid=grid, in_specs=in_specs, out_specs=out_specs, out_shape=out_shape,
                              scratch_shapes=scratch_shapes, compiler_params=_params(*sem))(*operands)
        return outs, None
    plan, srcs, h_shapes = hosted
    n_in, n_out, n_scr, nh_in, nh_out = len(in_specs), len(out_specs), len(scratch_shapes), len(srcs), len(h_shapes)
    aliases = {n_in + a: n_out + a for a in range(nh_in)} if plan.in_place else {}

    def full_body(*refs):
        ins, refs = refs[:n_in], refs[n_in:]
        h_in, refs = refs[:nh_in], refs[nh_in:]
        outs, refs = refs[:n_out], refs[n_out:]
        h_out, refs = refs[:nh_out], refs[nh_out:]
        scr, sems = refs[:n_scr], refs[n_scr:]
        ids = [pl.program_id(ax) for ax in range(len(grid))]
        first = functools.reduce(jnp.logical_and, [i == 0 for i in ids])
        last = functools.reduce(jnp.logical_and, [i == g - 1 for i, g in zip(ids, grid)])

        @pl.when(first)
        def _():
            plan.start(h_in, h_out, *sems)

        body(*ins, *outs, *scr)

        @pl.when(last)
        def _():
            plan.wait(h_in, h_out, *sems)

    res = pl.pallas_call(
        full_body, name=name, grid=grid, in_specs=list(in_specs) + [ANY] * nh_in, out_specs=list(out_specs) + [ANY] * nh_out,
        out_shape=list(out_shape) + list(h_shapes), scratch_shapes=list(scratch_shapes) + plan.sems(),
        input_output_aliases=aliases, compiler_params=_params(*(["arbitrary"] * len(grid))),
    )(*operands, *srcs)
    return res[:n_out], res[n_out:]


def _dot_nn(a, b):
    return lax.dot_general(a, b, (((1,), (0,)), ((), ())), preferred_element_type=F32)


def _dot_nt(a, b):
    return lax.dot_general(a, b, (((1,), (1,)), ((), ())), preferred_element_type=F32)


def _dot_tn(a, b):
    return lax.dot_general(a, b, (((0,), (0,)), ((), ())), preferred_element_type=F32)


def _sigmoid(x):
    return 0.5 * jnp.tanh(0.5 * x) + 0.5


def _rms_stats(x):
    r = lax.rsqrt(jnp.mean(x * x, axis=-1, keepdims=True) + NORM_EPS)
    return x * r, r


ROWS_WIDE = 16
MM_ROWS = 256


def _rms_bwd(dy, xhat, r, g):
    dg = jnp.sum(dy * xhat, axis=0, keepdims=True)
    dxh = dy * g
    dx = r * (dxh - xhat * jnp.mean(dxh * xhat, axis=-1, keepdims=True))
    return dx, dg


def _mm(a, b, *, mode, out_dtype, tm, tn, tk, name, add=None, scale=1.0, hosted=None):
    if mode == "nn":
        (m, k), (_, n) = a.shape, b.shape
    elif mode == "nt":
        (m, k), (n, _) = a.shape, b.shape
    else:
        (k, m), (_, n) = a.shape, b.shape
    assert m % tm == 0 and n % tn == 0 and k % tk == 0, (name, m, n, k, tm, tn, tk)
    nk = k // tk
    dot = {"nn": _dot_nn, "nt": _dot_nt, "tn": _dot_tn}[mode]
    a_spec = pl.BlockSpec((tk, tm), lambda i, j, kk: (kk, i)) if mode == "tn" else pl.BlockSpec((tm, tk), lambda i, j, kk: (i, kk))
    b_spec = pl.BlockSpec((tn, tk), lambda i, j, kk: (j, kk)) if mode == "nt" else pl.BlockSpec((tk, tn), lambda i, j, kk: (kk, j))
    o_spec = pl.BlockSpec((tm, tn), lambda i, j, kk: (i, j))
    has_add = add is not None

    def finish(prod, c_ref, o_ref):
        if scale != 1.0:
            prod = prod * scale
        o_ref[...] = ((c_ref[...] + prod) if has_add else prod).astype(out_dtype)

    def body(*refs):
        a_ref, b_ref = refs[:2]
        c_ref = refs[2] if has_add else None
        o_ref = refs[3] if has_add else refs[2]
        if nk == 1:
            finish(dot(a_ref[...], b_ref[...]), c_ref, o_ref)
            return
        acc_ref = refs[-1]
        kk = pl.program_id(2)

        @pl.when(kk == 0)
        def _():
            acc_ref[...] = jnp.zeros_like(acc_ref)

        acc_ref[...] += dot(a_ref[...], b_ref[...])

        @pl.when(kk == nk - 1)
        def _():
            finish(acc_ref[...], c_ref, o_ref)

    operands = (a, b, add) if has_add else (a, b)
    in_specs = [a_spec, b_spec] + ([o_spec] if has_add else [])
    (out,), got = _call(
        body, name=name, grid=(m // tm, n // tn, nk), in_specs=in_specs, out_specs=[o_spec],
        out_shape=[jax.ShapeDtypeStruct((m, n), out_dtype)], scratch_shapes=[pltpu.VMEM((tm, tn), F32)] if nk > 1 else [],
        operands=operands, sem=("parallel", "parallel", "arbitrary"), hosted=hosted)
    return out if hosted is None else (out, got)


def _rms_fwd(x, g, *, tm, name, hosted=None):
    t, d = x.shape

    def body(x_ref, g_ref, h_ref):
        xhat, _ = _rms_stats(x_ref[...])
        h_ref[...] = (xhat * g_ref[...]).astype(BF16)

    (h,), got = _call(
        body, name=name, grid=(t // tm,),
        in_specs=[pl.BlockSpec((tm, d), lambda i: (i, 0)), pl.BlockSpec((1, d), lambda i: (0, 0))],
        out_specs=[pl.BlockSpec((tm, d), lambda i: (i, 0))], out_shape=[jax.ShapeDtypeStruct((t, d), BF16)], scratch_shapes=[],
        operands=(x, g), sem=("parallel",), hosted=hosted)
    return h, got


def _ffn_fwd(x, g, wgT, wuT, wd, *, tm, hc, name, hosted=None, target=None):
    t, d = x.shape
    nj = DFF // hc
    with_loss = target is not None

    def body(*refs):
        x_ref, g_ref, wg_ref, wu_ref, wd_ref = refs[:5]
        t_ref = refs[5] if with_loss else None
        xo_ref, h_ref, a_ref, b_ref = refs[5 + with_loss:9 + with_loss]
        loss_ref = refs[9 + with_loss] if with_loss else None
        acc_ref = refs[-1]
        i, j = pl.program_id(0), pl.program_id(1)

        @pl.when(j == 0)
        def _():
            xhat, _ = _rms_stats(x_ref[...])
            h_ref[...] = (xhat * g_ref[...]).astype(BF16)
            acc_ref[...] = jnp.zeros_like(acc_ref)

        h = h_ref[...]
        a = _dot_nt(h, wg_ref[...])
        b = _dot_nt(h, wu_ref[...])
        a_ref[...] = a.astype(BF16)
        b_ref[...] = b.astype(BF16)
        s = (a * _sigmoid(a) * b).astype(BF16)
        acc_ref[...] += _dot_nn(s, wd_ref[...])

        if with_loss:
            @pl.when((i == 0) & (j == 0))
            def _():
                loss_ref[...] = jnp.zeros_like(loss_ref)

        @pl.when(j == nj - 1)
        def _():
            y = x_ref[...] + 0.5 * acc_ref[...]
            if with_loss:
                err = y - t_ref[...]
                xo_ref[...] = err * (1.0 / d)
                loss_ref[...] += jnp.sum(jnp.sum(err * err, axis=-1, keepdims=True), axis=0, keepdims=True) * (0.5 / d)
            else:
                xo_ref[...] = y

    row = pl.BlockSpec((tm, d), lambda i, j: (i, 0))
    vec = pl.BlockSpec((1, d), lambda i, j: (0, 0))
    wsp = pl.BlockSpec((hc, d), lambda i, j: (j, 0))
    hid = pl.BlockSpec((tm, hc), lambda i, j: (i, j))
    out_specs = [row, row, hid, hid] + ([pl.BlockSpec((1, 128), lambda i, j: (0, 0))] if with_loss else [])
    out_shape = [jax.ShapeDtypeStruct((t, d), F32), jax.ShapeDtypeStruct((t, d), BF16), jax.ShapeDtypeStruct((t, DFF), BF16),
                 jax.ShapeDtypeStruct((t, DFF), BF16)] + ([jax.ShapeDtypeStruct((1, 128), F32)] if with_loss else [])
    return _call(
        body, name=name, grid=(t // tm, nj), in_specs=[row, vec, wsp, wsp, wsp] + ([row] if with_loss else []),
        out_specs=out_specs, out_shape=out_shape, scratch_shapes=[pltpu.VMEM((tm, d), F32)],
        operands=(x, g, wgT, wuT, wd) + ((target,) if with_loss else ()),
        sem=("arbitrary" if with_loss else "parallel", "arbitrary"), hosted=hosted)


def _ffn_grads(dout, h, a, b, wd, *, tm, hc, name, hosted=None):
    t, d = dout.shape
    ni, nj = t // tm, DFF // hc

    def body(dout_ref, h_ref, a_ref, b_ref, wd_ref, da_ref, db_ref, dwg_ref, dwu_ref, dwd_ref,
             dy_all, h_all, ds_scr, s_scr, acc_g, acc_u, acc_d):
        j, i = pl.program_id(0), pl.program_id(1)
        rows_i = pl.ds(pl.multiple_of(i * tm, tm), tm)

        @pl.when(j == 0)
        def _():
            dy_all[rows_i, :] = (0.5 * dout_ref[...]).astype(BF16)
            h_all[rows_i, :] = h_ref[...]

        @pl.when(i == 0)
        def _():
            acc_g[...] = jnp.zeros_like(acc_g)
            acc_u[...] = jnp.zeros_like(acc_u)
            acc_d[...] = jnp.zeros_like(acc_d)

        def grad_rows(rows):
            ds = ds_scr[rows, :]
            av = a_ref[rows, :].astype(F32)
            bv = b_ref[rows, :].astype(F32)
            sg = _sigmoid(av)
            sl = av * sg
            s_scr[rows, :] = (sl * bv).astype(BF16)
            da_ref[rows, :] = (ds * bv * (sg + sl * (1.0 - sg))).astype(BF16)
            db_ref[rows, :] = (ds * sl).astype(BF16)

        for blk in range(tm // MM_ROWS):
            rs = slice(blk * MM_ROWS, (blk + 1) * MM_ROWS)
            ds_scr[rs, :] = _dot_nt(dy_all[pl.ds(pl.multiple_of(i * tm + blk * MM_ROWS, MM_ROWS), MM_ROWS), :], wd_ref[...])
            for c in range(MM_ROWS // ROWS_WIDE):
                grad_rows(slice(blk * MM_ROWS + c * ROWS_WIDE, blk * MM_ROWS + (c + 1) * ROWS_WIDE))

        dy_i = dy_all[rows_i, :]
        h_i = h_all[rows_i, :]
        acc_d[...] += _dot_tn(s_scr[...], dy_i)
        acc_g[...] += _dot_tn(da_ref[...], h_i)
        acc_u[...] += _dot_tn(db_ref[...], h_i)

        @pl.when(i == ni - 1)
        def _():
            dwg_ref[...] = acc_g[...].astype(BF16)
            dwu_ref[...] = acc_u[...].astype(BF16)
            dwd_ref[...] = acc_d[...].astype(BF16)

    first = pl.BlockSpec((tm, d), lambda j, i: (jnp.where(j == 0, i, 0), 0))
    hid = pl.BlockSpec((tm, hc), lambda j, i: (i, j))
    wsp = pl.BlockSpec((hc, d), lambda j, i: (j, 0))
    hid_shape = jax.ShapeDtypeStruct((t, DFF), BF16)
    w_shape = jax.ShapeDtypeStruct((DFF, d), BF16)
    return _call(
        body, name=name, grid=(nj, ni), in_specs=[first, first, hid, hid, wsp], out_specs=[hid, hid, wsp, wsp, wsp],
        out_shape=[hid_shape, hid_shape, w_shape, w_shape, w_shape],
        scratch_shapes=[pltpu.VMEM((t, d), BF16), pltpu.VMEM((t, d), BF16), pltpu.VMEM((tm, hc), F32), pltpu.VMEM((tm, hc), BF16),
                        pltpu.VMEM((hc, d), F32), pltpu.VMEM((hc, d), F32), pltpu.VMEM((hc, d), F32)],
        operands=(dout, h, a, b, wd), sem=("arbitrary", "arbitrary"), hosted=hosted)


def _proj_fwd(h, latT, convT, gateT, *, tm, name, hosted=None):
    t, d = h.shape

    def body(h_ref, wl_ref, wc_ref, wg_ref, lat_ref, conv_ref, gl_ref):
        hv = h_ref[...]
        lat_ref[...] = _dot_nt(hv, wl_ref[...]).astype(BF16)
        conv_ref[...] = _dot_nt(hv, wc_ref[...]).astype(BF16)
        gl_ref[...] = _dot_nt(hv, wg_ref[...]).astype(BF16)

    def rows(w):
        return pl.BlockSpec((tm, w), lambda i: (i, 0))

    def full(r):
        return pl.BlockSpec((r, d), lambda i: (0, 0))

    return _call(
        body, name=name, grid=(t // tm,), in_specs=[rows(d), full(LAT_PAD), full(CONV_COLS), full(GATE_COLS)],
        out_specs=[rows(LAT_PAD), rows(CONV_COLS), rows(GATE_COLS)],
        out_shape=[jax.ShapeDtypeStruct((t, LAT_PAD), BF16), jax.ShapeDtypeStruct((t, CONV_COLS), BF16),
                   jax.ShapeDtypeStruct((t, GATE_COLS), BF16)],
        scratch_shapes=[], operands=(h, latT, convT, gateT), sem=("parallel",), hosted=hosted)


def _proj_bwd(dlat, dconv3, dgl, latT, convT, gateT, x, g, dres, *, tm, name, hosted=None):
    t, d = x.shape

    def body(dl_ref, dc_ref, dg_ref, wl_ref, wc_ref, wg_ref, x_ref, g_ref, dres_ref, dx_ref, dgain_ref):
        @pl.when(pl.program_id(0) == 0)
        def _():
            dgain_ref[...] = jnp.zeros_like(dgain_ref)

        dh = _dot_nn(dl_ref[...], wl_ref[...]) + _dot_nn(dc_ref[...], wc_ref[...]) + _dot_nn(dg_ref[...], wg_ref[...])
        xhat, r = _rms_stats(x_ref[...])
        dx, dgain = _rms_bwd(dh, xhat, r, g_ref[...])
        dx_ref[...] = dres_ref[...] + dx
        dgain_ref[...] += dgain

    def rows(w):
        return pl.BlockSpec((tm, w), lambda i: (i, 0))

    def full(r):
        return pl.BlockSpec((r, d), lambda i: (0, 0))

    return _call(
        body, name=name, grid=(t // tm,),
        in_specs=[rows(LAT_PAD), rows(CONV_COLS), rows(GATE_COLS), full(LAT_PAD), full(CONV_COLS), full(GATE_COLS), rows(d), full(1), rows(d)],
        out_specs=[rows(d), full(1)], out_shape=[jax.ShapeDtypeStruct((t, d), F32), jax.ShapeDtypeStruct((1, d), F32)],
        scratch_shapes=[], operands=(dlat, dconv3, dgl, latT, convT, gateT, x, g, dres), sem=("arbitrary",), hosted=hosted)


def _ffn_up_bwd(da, db, wgT, wuT, x, g, dout, *, tm, name, hosted=None):
    t, d = x.shape

    def body(da_ref, db_ref, wg_ref, wu_ref, x_ref, g_ref, dout_ref, dx_ref, dg_ref):
        @pl.when(pl.program_id(0) == 0)
        def _():
            dg_ref[...] = jnp.zeros_like(dg_ref)

        dh = _dot_nn(da_ref[...], wg_ref[...]) + _dot_nn(db_ref[...], wu_ref[...])
        xhat, r = _rms_stats(x_ref[...])
        dx, dg = _rms_bwd(dh, xhat, r, g_ref[...])
        dx_ref[...] = dout_ref[...] + dx
        dg_ref[...] += dg

    row = pl.BlockSpec((tm, d), lambda i: (i, 0))
    vec = pl.BlockSpec((1, d), lambda i: (0, 0))
    hid = pl.BlockSpec((tm, DFF), lambda i: (i, 0))
    wsp = pl.BlockSpec((DFF, d), lambda i: (0, 0))
    return _call(
        body, name=name, grid=(t // tm,), in_specs=[hid, hid, wsp, wsp, row, vec, row], out_specs=[row, vec],
        out_shape=[jax.ShapeDtypeStruct((t, d), F32), jax.ShapeDtypeStruct((1, d), F32)], scratch_shapes=[],
        operands=(da, db, wgT, wuT, x, g, dout), sem=("arbitrary",), hosted=hosted)


HEAD_LANES = (slice(0, 32), slice(64, 80), None, slice(32, 64), slice(80, 96), None)


def _head_cols(a):
    def part(sl, width):
        if sl is None or sl.stop > a.shape[1]:
            return jnp.zeros((a.shape[0], width), a.dtype)
        return a[:, sl]

    return jnp.concatenate([part(sl, w) for sl, w in zip(HEAD_LANES, (32, 16, 16, 32, 16, 16))], axis=1)


def _head_cols_inv(a, dims):
    parts = [a[:, 0:32], a[:, 64:96]] + ([a[:, 32:48], a[:, 96:112]] if dims == QK_DIM else [])
    return jnp.concatenate(parts, axis=1)


def _rope_fwd(x, c, s):
    return x * c + pltpu.roll(x, HEAD_PAD // 2, 1) * s


def _rope_bwd(dy, c, s):
    return dy * c + pltpu.roll(dy * s, HEAD_PAD // 2, 1)


def _head_stats(x):
    r = lax.rsqrt(jnp.sum(x * x, axis=-1, keepdims=True) * (1.0 / QK_DIM) + NORM_EPS)
    return x * r, r


def _mla_prep_fwd(lat, gq, gkv, ghq, ghk, wq, wk, wv, rc, rs, *, tm, name):
    t = lat.shape[0]

    def body(lat_ref, gq_ref, gkv_ref, ghq_ref, ghk_ref, wq_ref, wk_ref, wv_ref, c_ref, s_ref,
             q_ref, k_ref, v_ref, qn_ref, ckv_ref):
        lat_v = lat_ref[...]
        qhat, _ = _rms_stats(lat_v[:, :Q_LORA].astype(F32))
        qn = (qhat * gq_ref[...]).astype(BF16)
        khat, _ = _rms_stats(lat_v[:, Q_LORA:Q_LORA + KV_LORA].astype(F32))
        ckv = (khat * gkv_ref[...]).astype(BF16)
        ckv_ext = jnp.concatenate([ckv, lat_v[:, Q_LORA + KV_LORA:]], axis=1)
        qn_ref[...] = qn
        ckv_ref[...] = ckv_ext
        q_pre = _dot_nn(qn, wq_ref[...])
        k_pre = _dot_nn(ckv_ext, wk_ref[...])
        v_ref[...] = _dot_nn(ckv, wv_ref[...]).astype(BF16)
        c, s = c_ref[...], s_ref[...]
        for h in range(N_HEADS):
            hs = slice(h * HEAD_PAD, (h + 1) * HEAD_PAD)
            xq, _ = _head_stats(q_pre[:, hs])
            q_ref[:, hs] = _rope_fwd(xq * ghq_ref[...], c, s).astype(BF16)
            xk, _ = _head_stats(k_pre[:, hs])
            k_ref[:, hs] = _rope_fwd(xk * ghk_ref[...], c, s).astype(BF16)

    def row(w):
        return pl.BlockSpec((tm, w), lambda i: (i, 0))

    def full(r, w):
        return pl.BlockSpec((r, w), lambda i: (0, 0))

    wide = jax.ShapeDtypeStruct((t, D), BF16)
    lat3 = jax.ShapeDtypeStruct((t, Q_LORA), BF16)
    return pl.pallas_call(
        body, name=name, grid=(t // tm,),
        in_specs=[row(LAT_PAD), full(1, Q_LORA), full(1, KV_LORA), full(1, HEAD_PAD), full(1, HEAD_PAD),
                  full(Q_LORA, D), full(Q_LORA, D), full(KV_LORA, D), row(HEAD_PAD), row(HEAD_PAD)],
        out_specs=[row(D), row(D), row(D), row(Q_LORA), row(Q_LORA)],
        out_shape=[wide, wide, wide, lat3, lat3],
        compiler_params=_params("parallel"),
    )(lat, gq, gkv, ghq, ghk, wq, wk, wv, rc, rs)


def _mla_prep_bwd(dq, dk, dv, lat, qn, ckv_ext, gq, gkv, ghq, ghk, wq, wk, wv, rc, rs, *, tm, name):
    t = lat.shape[0]

    def body(dq_ref, dk_ref, dv_ref, lat_ref, qn_ref, ckv_ref, gq_ref, gkv_ref, ghq_ref, ghk_ref, wq_ref, wk_ref, wv_ref,
             c_ref, s_ref, dlat_ref, dqp_ref, dkp_ref, dgq_ref, dgkv_ref, dghq_ref, dghk_ref):
        @pl.when(pl.program_id(0) == 0)
        def _():
            dgq_ref[...] = jnp.zeros_like(dgq_ref)
            dgkv_ref[...] = jnp.zeros_like(dgkv_ref)
            dghq_ref[...] = jnp.zeros_like(dghq_ref)
            dghk_ref[...] = jnp.zeros_like(dghk_ref)

        c, s = c_ref[...], s_ref[...]
        q_pre = _dot_nn(qn_ref[...], wq_ref[...])
        k_pre = _dot_nn(ckv_ref[...], wk_ref[...])

        def heads(pre, dy_ref, gh_ref, dgh_ref, out_ref):
            dgh = jnp.zeros((1, HEAD_PAD), F32)
            for h in range(N_HEADS):
                hs = slice(h * HEAD_PAD, (h + 1) * HEAD_PAD)
                d = _rope_bwd(dy_ref[:, hs].astype(F32), c, s)
                xhat, r = _head_stats(pre[:, hs])
                dgh = dgh + jnp.sum(d * xhat, axis=0, keepdims=True)
                dxh = d * gh_ref[...]
                dx = r * (dxh - xhat * (jnp.sum(dxh * xhat, axis=-1, keepdims=True) * (1.0 / QK_DIM)))
                out_ref[:, hs] = dx.astype(BF16)
            dgh_ref[...] += dgh

        heads(q_pre, dq_ref, ghq_ref, dghq_ref, dqp_ref)
        heads(k_pre, dk_ref, ghk_ref, dghk_ref, dkp_ref)
        dqn = _dot_nt(dqp_ref[...], wq_ref[...])
        dce = _dot_nt(dkp_ref[...], wk_ref[...])
        dckv = dce[:, :KV_LORA] + _dot_nt(dv_ref[...], wv_ref[...])
        lat_v = lat_ref[...]
        qhat, rq = _rms_stats(lat_v[:, :Q_LORA].astype(F32))
        dql, dgq = _rms_bwd(dqn, qhat, rq, gq_ref[...])
        khat, rk = _rms_stats(lat_v[:, Q_LORA:Q_LORA + KV_LORA].astype(F32))
        dkl, dgkv = _rms_bwd(dckv, khat, rk, gkv_ref[...])
        dgq_ref[...] += dgq
        dgkv_ref[...] += dgkv
        dlat_ref[...] = jnp.concatenate([dql, dkl, dce[:, KV_LORA:]], axis=1).astype(BF16)

    def row(w):
        return pl.BlockSpec((tm, w), lambda i: (i, 0))

    def full(r, w):
        return pl.BlockSpec((r, w), lambda i: (0, 0))

    return pl.pallas_call(
        body, name=name, grid=(t // tm,),
        in_specs=[row(D), row(D), row(D), row(LAT_PAD), row(Q_LORA), row(Q_LORA), full(1, Q_LORA), full(1, KV_LORA),
                  full(1, HEAD_PAD), full(1, HEAD_PAD), full(Q_LORA, D), full(Q_LORA, D), full(KV_LORA, D),
                  row(HEAD_PAD), row(HEAD_PAD)],
        out_specs=[row(LAT_PAD), row(D), row(D), full(1, Q_LORA), full(1, KV_LORA), full(1, HEAD_PAD), full(1, HEAD_PAD)],
        out_shape=[jax.ShapeDtypeStruct((t, LAT_PAD), BF16), jax.ShapeDtypeStruct((t, D), BF16), jax.ShapeDtypeStruct((t, D), BF16),
                   jax.ShapeDtypeStruct((1, Q_LORA), F32), jax.ShapeDtypeStruct((1, KV_LORA), F32),
                   jax.ShapeDtypeStruct((1, HEAD_PAD), F32), jax.ShapeDtypeStruct((1, HEAD_PAD), F32)],
        compiler_params=_params("arbitrary"),
    )(dq, dk, dv, lat, qn, ckv_ext, gq, gkv, ghq, ghk, wq, wk, wv, rc, rs)


def _causal_keep(tq):
    r = lax.broadcasted_iota(jnp.int32, (tq, tq), 0)
    c = lax.broadcasted_iota(jnp.int32, (tq, tq), 1)
    return c <= r


def _flash_fwd(q, k, v, *, n_seq, seq, tq, name, hosted=None):
    nq = seq // tq

    def body(q_ref, k_ref, v_ref, o_ref, lse_ref):
        qi = pl.program_id(2)
        qv = q_ref[...]

        def step(j, carry, masked):
            m, l, acc = carry
            kj = k_ref[pl.ds(pl.multiple_of(j * tq, tq), tq), :]
            vj = v_ref[pl.ds(pl.multiple_of(j * tq, tq), tq), :]
            s = _dot_nt(qv, kj) * ATTN_SCALE
            if masked:
                s = jnp.where(_causal_keep(tq), s, NEG)
            m_new = jnp.maximum(m, jnp.max(s, axis=-1, keepdims=True))
            alpha = jnp.exp(m - m_new)
            p = jnp.exp(s - m_new)
            l = alpha * l + jnp.sum(p, axis=-1, keepdims=True)
            acc = alpha * acc + _dot_nn(p.astype(BF16), vj)
            return m_new, l, acc

        init = (jnp.full((tq, 1), NEG, F32), jnp.zeros((tq, 1), F32), jnp.zeros((tq, HEAD_PAD), F32))
        carry = lax.fori_loop(0, qi, lambda j, cr: step(j, cr, False), init)
        m, l, acc = step(qi, carry, True)
        o_ref[...] = (acc / l).astype(BF16)
        lse_ref[...] = jnp.broadcast_to(m + jnp.log(l), (tq, HEAD_PAD))

    qspec = pl.BlockSpec((tq, HEAD_PAD), lambda b, h, i: (b * nq + i, h))
    kspec = pl.BlockSpec((seq, HEAD_PAD), lambda b, h, i: (b, h))
    t = n_seq * seq
    return _call(
        body, name=name, grid=(n_seq, N_HEADS, nq), in_specs=[qspec, kspec, kspec], out_specs=[qspec, qspec],
        out_shape=[jax.ShapeDtypeStruct((t, D), BF16), jax.ShapeDtypeStruct((t, D), F32)], scratch_shapes=[],
        operands=(q, k, v), sem=("parallel", "parallel", "arbitrary"), hosted=hosted)


def _flash_bwd(q, k, v, o, lse, do, *, n_seq, seq, tq, name, hosted=None):
    nq = seq // tq

    def body(q_ref, k_ref, v_ref, o_ref, lse_ref, do_ref, dq_ref, dk_ref, dv_ref, dk_acc, dv_acc):
        j = pl.program_id(2)

        @pl.when(j == 0)
        def _():
            dq_ref[...] = jnp.zeros_like(dq_ref)

        dk_acc[...] = jnp.zeros_like(dk_acc)
        dv_acc[...] = jnp.zeros_like(dv_acc)
        kv = k_ref[...]
        vv = v_ref[...]

        def step(i, masked):
            rows = pl.ds(pl.multiple_of(i * tq, tq), tq)
            qi = q_ref[rows, :]
            doi = do_ref[rows, :]
            delta = jnp.sum(doi.astype(F32) * o_ref[rows, :].astype(F32), axis=-1, keepdims=True)
            s = _dot_nt(qi, kv) * ATTN_SCALE
            p = jnp.exp(s - lse_ref[rows, :][:, :1])
            if masked:
                p = jnp.where(_causal_keep(tq), p, 0.0)
            dv_acc[...] += _dot_tn(p.astype(BF16), doi)
            dp = _dot_nt(doi, vv)
            ds = (p * (dp - delta) * ATTN_SCALE).astype(BF16)
            dk_acc[...] += _dot_tn(ds, qi)
            dq_ref[rows, :] += _dot_nn(ds, kv)

        step(j, True)

        def loop_body(i, carry):
            step(i, False)
            return carry

        lax.fori_loop(j + 1, nq, loop_body, 0)
        dk_ref[...] = dk_acc[...]
        dv_ref[...] = dv_acc[...].astype(BF16)

    full = pl.BlockSpec((seq, HEAD_PAD), lambda b, h, j: (b, h))
    tile = pl.BlockSpec((tq, HEAD_PAD), lambda b, h, j: (b * nq + j, h))
    t = n_seq * seq
    return _call(
        body, name=name, grid=(n_seq, N_HEADS, nq), in_specs=[full, tile, tile, full, full, full],
        out_specs=[full, tile, tile],
        out_shape=[jax.ShapeDtypeStruct((t, D), F32), jax.ShapeDtypeStruct((t, D), F32), jax.ShapeDtypeStruct((t, D), BF16)],
        scratch_shapes=[pltpu.VMEM((tq, HEAD_PAD), F32), pltpu.VMEM((tq, HEAD_PAD), F32)],
        operands=(q, k, v, o, lse, do), sem=("parallel", "parallel", "arbitrary"), hosted=hosted)


CONV_CB = 256


def _shift_down(u, k, row):
    return jnp.where(row >= k, pltpu.roll(u, k, 0), 0.0)


def _shift_up(u, k, row, n):
    return jnp.where(row < n - k, pltpu.roll(u, n - k, 0), 0.0)


def _conv_fwd(conv3, cw, *, n_seq, seq, name, hosted=None):
    def body(c_ref, w_ref, p_ref):
        blk = c_ref[...].astype(F32)
        xc, gb, gc = blk[:, :CONV_CB], blk[:, CONV_CB:2 * CONV_CB], blk[:, 2 * CONV_CB:]
        row = lax.broadcasted_iota(jnp.int32, (seq, CONV_CB), 0)
        u = gc * xc
        z = w_ref[0:1, :] * _shift_down(u, 2, row) + w_ref[1:2, :] * _shift_down(u, 1, row) + w_ref[2:3, :] * u
        p_ref[...] = (gb * z).astype(BF16)

    (p,), got = _call(
        body, name=name, grid=(n_seq, D // CONV_CB),
        in_specs=[pl.BlockSpec((seq, 3 * CONV_CB), lambda b, j: (b, j)), pl.BlockSpec((3, CONV_CB), lambda b, j: (0, j))],
        out_specs=[pl.BlockSpec((seq, CONV_CB), lambda b, j: (b, j))],
        out_shape=[jax.ShapeDtypeStruct((n_seq * seq, D), BF16)], scratch_shapes=[],
        operands=(conv3, cw), sem=("parallel", "parallel"), hosted=hosted)
    return p, got


def _conv_bwd(dp, conv3, cw, *, n_seq, seq, name):
    def body(dp_ref, c_ref, w_ref, dc_ref, dw_ref):
        @pl.when(pl.program_id(1) == 0)
        def _():
            dw_ref[...] = jnp.zeros_like(dw_ref)

        blk = c_ref[...].astype(F32)
        xc, gb, gc = blk[:, :CONV_CB], blk[:, CONV_CB:2 * CONV_CB], blk[:, 2 * CONV_CB:]
        row = lax.broadcasted_iota(jnp.int32, (seq, CONV_CB), 0)
        w0, w1, w2 = w_ref[0:1, :], w_ref[1:2, :], w_ref[2:3, :]
        u = gc * xc
        u1 = _shift_down(u, 1, row)
        u2 = _shift_down(u, 2, row)
        z = w0 * u2 + w1 * u1 + w2 * u
        dpv = dp_ref[...].astype(F32)
        dz = dpv * gb
        du = w2 * dz + w1 * _shift_up(dz, 1, row, seq) + w0 * _shift_up(dz, 2, row, seq)
        dc_ref[...] = jnp.concatenate([du * gc, dpv * z, du * xc], axis=1).astype(BF16)
        dw_ref[0:1, :] += jnp.sum(dz * u2, axis=0, keepdims=True)
        dw_ref[1:2, :] += jnp.sum(dz * u1, axis=0, keepdims=True)
        dw_ref[2:3, :] += jnp.sum(dz * u, axis=0, keepdims=True)

    return pl.pallas_call(
        body, name=name, grid=(D // CONV_CB, n_seq),
        in_specs=[pl.BlockSpec((seq, CONV_CB), lambda j, b: (b, j)), pl.BlockSpec((seq, 3 * CONV_CB), lambda j, b: (b, j)),
                  pl.BlockSpec((3, CONV_CB), lambda j, b: (0, j))],
        out_specs=[pl.BlockSpec((seq, 3 * CONV_CB), lambda j, b: (b, j)), pl.BlockSpec((3, CONV_CB), lambda j, b: (0, j))],
        out_shape=[jax.ShapeDtypeStruct((n_seq * seq, CONV_COLS), BF16), jax.ShapeDtypeStruct((3, D), F32)],
        compiler_params=_params("parallel", "arbitrary"),
    )(dp, conv3, cw)


def _merge_fwd(o, p, gl, bias, x1, wpa, wpc, wout, *, tm, name, hosted=None):
    t = x1.shape[0]

    def body(o_ref, p_ref, gl_ref, b_ref, x_ref, wpa_ref, wpc_ref, wout_ref, x2_ref, mg_ref, ya_ref, yb_ref):
        ya = _dot_nn(o_ref[...], wpa_ref[...])
        yb = _dot_nn(p_ref[...], wpc_ref[...])
        gates = _sigmoid(gl_ref[...].astype(F32) + b_ref[...])
        merged = (gates[:, :D] * ya + gates[:, D:] * yb).astype(BF16)
        ya_ref[...] = ya.astype(BF16)
        yb_ref[...] = yb.astype(BF16)
        mg_ref[...] = merged
        x2_ref[...] = x_ref[...] + _dot_nn(merged, wout_ref[...])

    row = pl.BlockSpec((tm, D), lambda i: (i, 0))
    row2 = pl.BlockSpec((tm, GATE_COLS), lambda i: (i, 0))
    wsp = pl.BlockSpec((D, D), lambda i: (0, 0))
    wide = jax.ShapeDtypeStruct((t, D), BF16)
    return _call(
        body, name=name, grid=(t // tm,),
        in_specs=[row, row, row2, pl.BlockSpec((1, GATE_COLS), lambda i: (0, 0)), row, wsp, wsp, wsp],
        out_specs=[row, row, row, row], out_shape=[jax.ShapeDtypeStruct((t, D), F32), wide, wide, wide], scratch_shapes=[],
        operands=(o, p, gl, bias, x1, wpa, wpc, wout), sem=("parallel",), hosted=hosted)


def _merge_bwd(dx2, ya, yb, gl, bias, wpa, wpc, wout, *, tm, name, hosted=None):
    t = dx2.shape[0]

    def body(dx_ref, ya_ref, yb_ref, gl_ref, b_ref, wpa_ref, wpc_ref, wout_ref,
             dxb_ref, dya_ref, dyb_ref, dgl_ref, do_ref, dp_ref, db_ref):
        @pl.when(pl.program_id(0) == 0)
        def _():
            db_ref[...] = jnp.zeros_like(db_ref)

        dxb = dx_ref[...].astype(BF16)
        dxb_ref[...] = dxb
        dm = _dot_nt(dxb, wout_ref[...])
        gates = _sigmoid(gl_ref[...].astype(F32) + b_ref[...])
        ga, gb = gates[:, :D], gates[:, D:]
        dya = (dm * ga).astype(BF16)
        dyb = (dm * gb).astype(BF16)
        dya_ref[...] = dya
        dyb_ref[...] = dyb
        dgl = jnp.concatenate([dm * ya_ref[...].astype(F32) * ga * (1.0 - ga),
                               dm * yb_ref[...].astype(F32) * gb * (1.0 - gb)], axis=1)
        dgl_ref[...] = dgl.astype(BF16)
        db_ref[...] += jnp.sum(dgl, axis=0, keepdims=True)
        do_ref[...] = _dot_nt(dya, wpa_ref[...]).astype(BF16)
        dp_ref[...] = _dot_nt(dyb, wpc_ref[...]).astype(BF16)

    row = pl.BlockSpec((tm, D), lambda i: (i, 0))
    row2 = pl.BlockSpec((tm, GATE_COLS), lambda i: (i, 0))
    vec2 = pl.BlockSpec((1, GATE_COLS), lambda i: (0, 0))
    wsp = pl.BlockSpec((D, D), lambda i: (0, 0))
    wide = jax.ShapeDtypeStruct((t, D), BF16)
    return _call(
        body, name=name, grid=(t // tm,), in_specs=[row, row, row, row2, vec2, wsp, wsp, wsp],
        out_specs=[row, row, row, row2, row, row, vec2],
        out_shape=[wide, wide, wide, jax.ShapeDtypeStruct((t, GATE_COLS), BF16), wide, wide,
                   jax.ShapeDtypeStruct((1, GATE_COLS), F32)],
        scratch_shapes=[], operands=(dx2, ya, yb, gl, bias, wpa, wpc, wout), sem=("arbitrary",), hosted=hosted)


def _adamw(w, g, m, v, *, name):
    rows, cols = w.shape
    tr = max([c for c in range(8, 513, 8) if rows % c == 0], default=rows)
    c1 = 1.0 / (1.0 - ADAM_B1 ** ADAM_STEP)
    c2 = 1.0 / (1.0 - ADAM_B2 ** ADAM_STEP)

    def body(w_ref, g_ref, m_ref, v_ref, d_ref, nm_ref, nv_ref):
        gv = g_ref[...]
        nm = ADAM_B1 * m_ref[...] + (1.0 - ADAM_B1) * gv
        nv = ADAM_B2 * v_ref[...] + (1.0 - ADAM_B2) * (gv * gv)
        nm_ref[...] = nm
        nv_ref[...] = nv
        d_ref[...] = -ADAM_LR * ((nm * c1) / (jnp.sqrt(nv * c2) + ADAM_EPS) + ADAM_WD * w_ref[...])

    spec = pl.BlockSpec((tr, cols), lambda i: (i, 0))
    shp = jax.ShapeDtypeStruct((rows, cols), F32)
    return pl.pallas_call(
        body, name=name, grid=(rows // tr,), in_specs=[spec] * 4, out_specs=[spec] * 3, out_shape=[shp] * 3,
        compiler_params=_params("parallel"),
    )(w, g, m, v)


def _place():
    return lax.axis_index("x"), lax.axis_index("y"), lax.axis_index("c")


def _other_chips(x, y):
    return [(1 - x, y), (x, 1 - y), (1 - x, 1 - y)]


def _remote(src, dst, send, recv, dev):
    return pltpu.make_async_remote_copy(src_ref=src, dst_ref=dst, send_sem=send, recv_sem=recv, device_id=dev, device_id_type=MESH)


def _gather_chips_plan(n):
    def start(srcs, dsts, send, recv, local):
        x, y, cc = _place()
        me = 4 * x + 2 * y + cc
        for a in range(n):
            pltpu.make_async_copy(srcs[a], dsts[a].at[me], local.at[a]).start()
            for k, (px, py) in enumerate(_other_chips(x, y)):
                _remote(srcs[a], dsts[a].at[me], send.at[3 * a + k], recv.at[3 * a + k], (px, py, cc)).start()

    def wait(srcs, dsts, send, recv, local):
        x, y, cc = _place()
        me = 4 * x + 2 * y + cc
        for a in range(n):
            for k, (px, py) in enumerate(_other_chips(x, y)):
                _remote(srcs[a], dsts[a].at[4 * px + 2 * py + cc], send.at[3 * a + k], recv.at[3 * a + k], (px, py, cc)).wait_recv()
        for a in range(n):
            for k, (px, py) in enumerate(_other_chips(x, y)):
                _remote(srcs[a], dsts[a].at[me], send.at[3 * a + k], recv.at[3 * a + k], (px, py, cc)).wait_send()
            pltpu.make_async_copy(srcs[a], dsts[a].at[me], local.at[a]).wait()

    return _Plan(start, wait, 3 * n, n)


def _scatter_chips_plan(n):
    def start(srcs, dsts, send, recv, local):
        x, y, cc = _place()
        for a in range(n):
            for k, (px, py) in enumerate(_other_chips(x, y)):
                _remote(srcs[a].at[2 * px + py], dsts[a].at[k], send.at[3 * a + k], recv.at[3 * a + k], (px, py, cc)).start()

    def wait(srcs, dsts, send, recv, local):
        x, y, cc = _place()
        for a in range(n):
            for k, (px, py) in enumerate(_other_chips(x, y)):
                _remote(srcs[a].at[k], dsts[a].at[k], send.at[3 * a + k], recv.at[3 * a + k], (px, py, cc)).wait_recv()
        for a in range(n):
            for k, (px, py) in enumerate(_other_chips(x, y)):
                _remote(srcs[a].at[k], dsts[a].at[k], send.at[3 * a + k], recv.at[3 * a + k], (px, py, cc)).wait_send()

    return _Plan(start, wait, 3 * n, 0)


def _gather_shapes(blocks):
    return [jax.ShapeDtypeStruct((N_DEV,) + b.shape, b.dtype) for b in blocks]


def _scatter_shapes(parts):
    return [jax.ShapeDtypeStruct((3,) + p.shape[1:], p.dtype) for p in parts]


def _gather_sibling_plan(n):
    def start(srcs, dsts, send, recv, local):
        x, y, cc = _place()
        for a in range(n):
            for q in range(4):
                _remote(srcs[a].at[2 * q + cc], dsts[a].at[2 * q + cc], send.at[4 * a + q], recv.at[4 * a + q], (x, y, 1 - cc)).start()

    def wait(srcs, dsts, send, recv, local):
        x, y, cc = _place()
        for a in range(n):
            for q in range(4):
                _remote(srcs[a].at[2 * q + cc], dsts[a].at[2 * q + 1 - cc], send.at[4 * a + q], recv.at[4 * a + q],
                        (x, y, 1 - cc)).wait_recv()
        for a in range(n):
            for q in range(4):
                _remote(srcs[a].at[2 * q + cc], dsts[a].at[2 * q + cc], send.at[4 * a + q], recv.at[4 * a + q],
                        (x, y, 1 - cc)).wait_send()

    return _Plan(start, wait, 4 * n, 0, in_place=True)


def _scatter_sibling_plan(n):
    def start(srcs, dsts, send, recv, local):
        x, y, cc = _place()
        for a in range(n):
            for q in range(4):
                _remote(srcs[a].at[2 * q + 1 - cc], dsts[a].at[q], send.at[4 * a + q], recv.at[4 * a + q], (x, y, 1 - cc)).start()

    def wait(srcs, dsts, send, recv, local):
        x, y, cc = _place()
        for a in range(n):
            for q in range(4):
                _remote(srcs[a].at[q], dsts[a].at[q], send.at[4 * a + q], recv.at[4 * a + q], (x, y, 1 - cc)).wait_recv()
        for a in range(n):
            for q in range(4):
                _remote(srcs[a].at[q], dsts[a].at[q], send.at[4 * a + q], recv.at[4 * a + q], (x, y, 1 - cc)).wait_send()

    return _Plan(start, wait, 4 * n, 0)


def _same_shapes(arrs):
    return [jax.ShapeDtypeStruct(a.shape, a.dtype) for a in arrs]


def _halved_shapes(parts):
    return [jax.ShapeDtypeStruct((4,) + p.shape[1:], p.dtype) for p in parts]


def _run_plan(plan, srcs, out_shapes, *, name):
    n_in, n_out = len(srcs), len(out_shapes)

    def body(*refs):
        h_in, h_out, sems = refs[:n_in], refs[n_in:n_in + n_out], refs[n_in + n_out:]
        plan.start(h_in, h_out, *sems)
        plan.wait(h_in, h_out, *sems)

    return pl.pallas_call(body, name=name, in_specs=[ANY] * n_in, out_specs=[ANY] * n_out, out_shape=list(out_shapes),
                          input_output_aliases={a: a for a in range(n_in)} if plan.in_place else {},
                          scratch_shapes=plan.sems())(*srcs)


def _sum_sibling(p, q, core, *, name):
    _, r, c = p.shape

    def body(core_ref, p_ref, q_ref, o_ref):
        o_ref[...] = (p_ref[...].astype(F32) + q_ref[...].astype(F32)).astype(BF16)

    grid_spec = pltpu.PrefetchScalarGridSpec(
        num_scalar_prefetch=1, grid=(4,),
        in_specs=[pl.BlockSpec((1, r, c), lambda ch, core_ref: (2 * ch + core_ref[0], 0, 0)),
                  pl.BlockSpec((1, r, c), lambda ch, core_ref: (ch, 0, 0))],
        out_specs=pl.BlockSpec((1, r, c), lambda ch, core_ref: (ch, 0, 0)))
    return pl.pallas_call(
        body, name=name, grid_spec=grid_spec, out_shape=jax.ShapeDtypeStruct((4, r, c), BF16),
        compiler_params=_params("parallel"),
    )(core, p, q)


def _sum_chips(s1, r2, chip, *, name):
    _, r, c = s1.shape

    def body(chip_ref, s_ref, r_ref, o_ref):
        acc = s_ref[0].astype(F32)
        for k in range(3):
            acc = acc + r_ref[k].astype(F32)
        o_ref[...] = acc

    grid_spec = pltpu.PrefetchScalarGridSpec(
        num_scalar_prefetch=1, grid=(1,),
        in_specs=[pl.BlockSpec((1, r, c), lambda i, chip_ref: (chip_ref[0], 0, 0)),
                  pl.BlockSpec((3, r, c), lambda i, chip_ref: (0, 0, 0))],
        out_specs=pl.BlockSpec((r, c), lambda i, chip_ref: (0, 0)))
    return pl.pallas_call(
        body, name=name, grid_spec=grid_spec, out_shape=jax.ShapeDtypeStruct((r, c), F32),
        compiler_params=_params("arbitrary"),
    )(chip, s1, r2)


def _sum_adamw(s1, r2, chip, w, m, v, *, name):
    _, r, c = s1.shape
    c1 = 1.0 / (1.0 - ADAM_B1 ** ADAM_STEP)
    c2 = 1.0 / (1.0 - ADAM_B2 ** ADAM_STEP)

    def body(chip_ref, s_ref, r_ref, w_ref, m_ref, v_ref, g_ref, d_ref, nm_ref, nv_ref):
        gv = s_ref[0].astype(F32)
        for k in range(3):
            gv = gv + r_ref[k].astype(F32)
        g_ref[...] = gv
        nm = ADAM_B1 * m_ref[...] + (1.0 - ADAM_B1) * gv
        nv = ADAM_B2 * v_ref[...] + (1.0 - ADAM_B2) * (gv * gv)
        nm_ref[...] = nm
        nv_ref[...] = nv
        d_ref[...] = -ADAM_LR * ((nm * c1) / (jnp.sqrt(nv * c2) + ADAM_EPS) + ADAM_WD * w_ref[...])

    flat = pl.BlockSpec((r, c), lambda i, chip_ref: (0, 0))
    grid_spec = pltpu.PrefetchScalarGridSpec(
        num_scalar_prefetch=1, grid=(1,),
        in_specs=[pl.BlockSpec((1, r, c), lambda i, chip_ref: (chip_ref[0], 0, 0)),
                  pl.BlockSpec((3, r, c), lambda i, chip_ref: (0, 0, 0)), flat, flat, flat],
        out_specs=[flat] * 4)
    return pl.pallas_call(
        body, name=name, grid_spec=grid_spec, out_shape=[jax.ShapeDtypeStruct((r, c), F32)] * 4,
        compiler_params=_params("arbitrary"),
    )(chip, s1, r2, w, m, v)


def _small_exchange(v, *, reduce, name):
    r, c = v.shape

    def body(x_ref, o_ref, *rest):
        if reduce:
            buf_ref, send_sems, recv_sems = rest
        else:
            buf_ref = o_ref
            send_sems, recv_sems = rest
        x, y, cc = _place()
        me = 4 * x + 2 * y + cc

        def peer(k):
            return ((1 - x) if k & 4 else x, (1 - y) if k & 2 else y, (1 - cc) if k & 1 else cc)

        buf_ref[me] = x_ref[...]
        sends = []
        for k in range(1, N_DEV):
            cp = pltpu.make_async_remote_copy(src_ref=x_ref, dst_ref=buf_ref.at[me], send_sem=send_sems.at[k - 1],
                                              recv_sem=recv_sems.at[k - 1], device_id=peer(k), device_id_type=MESH)
            cp.start()
            sends.append(cp)
        for k in range(1, N_DEV):
            px, py, pc = peer(k)
            pltpu.make_async_remote_copy(src_ref=x_ref, dst_ref=buf_ref.at[4 * px + 2 * py + pc], send_sem=send_sems.at[k - 1],
                                         recv_sem=recv_sems.at[k - 1], device_id=peer(k), device_id_type=MESH).wait_recv()
        for cp in sends:
            cp.wait_send()
        if reduce:
            acc = buf_ref[0]
            for s in range(1, N_DEV):
                acc = acc + buf_ref[s]
            o_ref[...] = acc

    vm = pl.BlockSpec(memory_space=pltpu.VMEM)
    sems = [pltpu.SemaphoreType.DMA((N_DEV - 1,)), pltpu.SemaphoreType.DMA((N_DEV - 1,))]
    if reduce:
        out_shape, scratch = jax.ShapeDtypeStruct((r, c), F32), [pltpu.VMEM((N_DEV, r, c), F32)] + sems
    else:
        out_shape, scratch = jax.ShapeDtypeStruct((N_DEV, r, c), F32), sems
    return pl.pallas_call(body, name=name, in_specs=[vm], out_specs=vm, out_shape=out_shape, scratch_shapes=scratch)(v)


def _rows(a):
    return a.reshape(-1, D)


def _pad_cols(a, to):
    return jnp.pad(a, ((0, 0), (0, to - a.shape[1])))


def _pack_weights(w):
    parts = {
        "w_inT": jnp.pad(w["w_in"].T, ((0, IN_SHARD_PAD - IN_SHARD), (0, 0))),
        "w_uq": _rows(_head_cols(w["w_uq"])), "w_uk": _rows(_head_cols(w["w_uk"])),
        "w_uv": _rows(_pad_cols(w["w_uv"], HEAD_PAD)), "w_pa": _rows(w["w_proj_attn"]),
        "w_pc": w["w_proj_conv"], "w_out": w["w_out"],
    }
    return [jnp.concatenate([parts[n].astype(BF16) for n, _ in group], axis=0) for group in PACK]


def _cols_from_shards(gs, name, rows):
    idx, off, r = PACK_OFF[name]
    return gs[idx][:, off:off + r].reshape(N_DEV, rows, HEAD_PAD).transpose(1, 0, 2).reshape(rows, N_DEV * HEAD_PAD)


def _rows_from_shards(gs, name, keep=None):
    idx, off, r = PACK_OFF[name]
    keep = r if keep is None else keep
    return gs[idx][:, off:off + keep].reshape(N_DEV * keep, D)


def _rope_placement():
    i = lax.broadcasted_iota(jnp.int32, (HEAD_PAD, D), 0)
    j = lax.broadcasted_iota(jnp.int32, (HEAD_PAD, D), 1)
    lane = jnp.where(i < ROPE_HALF, 32 + i, 96 + i - ROPE_HALF)
    return ((i < 2 * ROPE_HALF) & (j % HEAD_PAD == lane)).astype(BF16)


def _unpack_in(g_in):
    w_inT = _rows_from_shards([g_in, None], "w_inT", IN_SHARD)
    lat_rows = Q_LORA + KV_LORA + 2 * ROPE_HALF
    conv = w_inT[lat_rows:lat_rows + CONV_COLS].reshape(3, D // CONV_CB, CONV_CB, D).transpose(1, 0, 2, 3).reshape(CONV_COLS, D)
    return {"latT": jnp.pad(w_inT[:lat_rows], ((0, LAT_PAD - lat_rows), (0, 0))), "convT": conv,
            "gateT": w_inT[lat_rows + CONV_COLS:]}


def _unpack_misc(g_misc):
    g = [None, g_misc]
    wpa = _cols_from_shards(g, "w_pa", 512).reshape(N_HEADS, NOPE, D)
    return {
        "wq": _cols_from_shards(g, "w_uq", Q_LORA),
        "wk": jnp.concatenate([_cols_from_shards(g, "w_uk", KV_LORA), _rope_placement()], axis=0),
        "wv": _cols_from_shards(g, "w_uv", KV_LORA),
        "wpa": jnp.pad(wpa, ((0, 0), (0, HEAD_PAD - NOPE), (0, 0))).reshape(D, D),
        "wpc": _rows_from_shards(g, "w_pc"), "wout": _rows_from_shards(g, "w_out"),
    }


def _shards_from_cols(a):
    rows = a.shape[0]
    return a.reshape(rows, N_DEV, HEAD_PAD).transpose(1, 0, 2).reshape(N_DEV, rows * HEAD_PAD // D, D)


def _pack_grads(gw):
    lat_rows = Q_LORA + KV_LORA + 2 * ROPE_HALF
    conv = gw["convT"].reshape(D // CONV_CB, 3, CONV_CB, D).transpose(1, 0, 2, 3).reshape(CONV_COLS, D)
    w_inT = jnp.concatenate([gw["latT"][:lat_rows], conv, gw["gateT"]], axis=0).reshape(N_DEV, IN_SHARD, D)
    wpa = gw["wpa"].reshape(N_HEADS, HEAD_PAD, D)[:, :NOPE].reshape(N_HEADS * NOPE, D)
    parts = {}
    parts.update({
        "w_inT": jnp.pad(w_inT, ((0, 0), (0, IN_SHARD_PAD - IN_SHARD), (0, 0))),
        "w_uq": _shards_from_cols(gw["wq"]), "w_uk": _shards_from_cols(gw["wk"][:KV_LORA]),
        "w_uv": _shards_from_cols(gw["wv"][:KV_LORA]), "w_pa": _shards_from_cols(wpa),
        "w_pc": gw["wpc"].reshape(N_DEV, D // N_DEV, D), "w_out": gw["wout"].reshape(N_DEV, D // N_DEV, D),
    })
    return [jnp.concatenate([parts[n] for n, _ in group], axis=1) for group in PACK]


def _unpack_grads(mines):
    def seg(name, keep=None):
        idx, off, r = PACK_OFF[name]
        return mines[idx][off:off + (r if keep is None else keep)]

    return {
        "w_in": seg("w_inT", IN_SHARD).T,
        "w_uq": _head_cols_inv(seg("w_uq").reshape(Q_LORA, HEAD_PAD), QK_DIM),
        "w_uk": _head_cols_inv(seg("w_uk").reshape(KV_LORA, HEAD_PAD), NOPE),
        "w_uv": seg("w_uv").reshape(KV_LORA, HEAD_PAD)[:, :NOPE],
        "w_proj_attn": seg("w_pa").reshape(512, HEAD_PAD),
        "w_proj_conv": seg("w_pc"), "w_out": seg("w_out"),
    }


def _rope_tables(positions):
    lane = jnp.arange(HEAD_PAD)
    idx = jnp.where((lane >= 32) & (lane < 48), lane - 32, jnp.where((lane >= 96) & (lane < 112), lane - 96, -1))
    inv_freq = jnp.where(idx >= 0, 1.0 / (ROPE_THETA ** (idx.astype(F32) / ROPE_HALF)), 0.0)
    ang = positions.reshape(-1).astype(F32)[:, None] * inv_freq
    return jnp.cos(ang), jnp.sin(ang) * jnp.where(lane < HEAD_PAD // 2, -1.0, 1.0)


def _local_step(x, positions, target, conv_w, small, ex):
    n_seq, seq, d = x.shape
    t = n_seq * seq
    x0 = x.reshape(t, d)
    tgt = target.reshape(t, d)
    rc, rs = _rope_tables(positions)
    ghq = _head_cols(small["q_head_norm"])
    ghk = _head_cols(small["k_head_norm"])
    TM, HC, TQ = 1024, 256, 1024

    def mm(*args, hosted=None, **kw):
        res = _mm(*args, hosted=hosted, **kw)
        return res if hosted is not None else (res, None)

    def wgrad(a, b, name, tm=None, hosted=None):
        tm = tm or a.shape[1]
        return mm(a, b, mode="tn", out_dtype=BF16, tm=tm, tn=b.shape[1], tk=2048 if tm <= D else 1024, name=name, hosted=hosted)

    f1g, f1u, f1d = ex.gather_now("ffn1")
    (x1, h1, a1, b1), got = _ffn_fwd(x0, small["ffn1_norm"], f1g, f1u, f1d, tm=512, hc=DFF // 2, name="ffn1_fwd",
                                     hosted=ex.gather_chips("mix_in"))
    hm, got = _rms_fwd(x1, small["mix_norm"], tm=TM, name="mix_norm_fwd", hosted=ex.gather_sibling(got))
    W = ex.mix_in_weights(got)
    (lat, conv3, gl), got = _proj_fwd(hm, W["latT"], W["convT"], W["gateT"], tm=512, name="proj_fwd",
                                      hosted=ex.gather_chips("mix_misc"))
    p, got = _conv_fwd(conv3, conv_w, n_seq=n_seq, seq=seq, name="conv_fwd", hosted=ex.gather_sibling(got))
    W.update(ex.mix_misc_weights(got))
    q, k, v, qn, ckv = _mla_prep_fwd(lat, small["q_a_norm"], small["kv_a_norm"], ghq, ghk, W["wq"], W["wk"], W["wv"], rc, rs,
                                     tm=512, name="mla_prep_fwd")
    (o, lse), got = _flash_fwd(q, k, v, n_seq=n_seq, seq=seq, tq=TQ, name="attn_fwd", hosted=ex.gather_chips("ffn2"))
    (x2, merged, ya, yb), got = _merge_fwd(o, p, gl, small["gate_bias"], x1, W["wpa"], W["wpc"], W["wout"], tm=512, name="merge_fwd",
                                           hosted=ex.gather_sibling(got))
    f2g, f2u, f2d = ex.ffn_weights(got)
    (dy, h2, a2, b2, loss_row), _ = _ffn_fwd(x2, small["ffn2_norm"], f2g, f2u, f2d, tm=512, hc=DFF // 2, name="ffn2_fwd", target=tgt)

    gw, gs = {}, {}
    (da2, db2, *ffn2_grads), _ = _ffn_grads(dy, h2, a2, b2, f2d, tm=TM, hc=HC, name="ffn2_grads")
    (dx2, gs["ffn2_norm"]), _ = _ffn_up_bwd(da2, db2, f2g, f2u, x2, small["ffn2_norm"], dy, tm=512, name="ffn2_up_bwd")

    (dx2b, dya, dyb, dgl, do, dp, gs["gate_bias"]), got = _merge_bwd(
        dx2, ya, yb, gl, small["gate_bias"], W["wpa"], W["wpc"], W["wout"], tm=512, name="merge_bwd",
        hosted=ex.scatter_sibling("ffn2", ffn2_grads))
    ex.scatter_sibling_done("ffn2", got)
    gw["wout"] = wgrad(merged, dx2b, "dw_out")[0]
    gw["wpa"] = wgrad(o, dya, "dw_pa")[0]
    gw["wpc"] = wgrad(p, dyb, "dw_pc")[0]
    dconv3, dconv_w = _conv_bwd(dp, conv3, conv_w, n_seq=n_seq, seq=seq, name="conv_bwd")
    (dq, dk, dv), got = _flash_bwd(q, k, v, o, lse, do, n_seq=n_seq, seq=seq, tq=TQ, name="attn_bwd",
                                   hosted=ex.scatter_chips("ffn2"))
    ex.scatter_chips_done("ffn2", got)
    dlat, dqp, dkp, gs["q_a_norm"], gs["kv_a_norm"], dghq, dghk = _mla_prep_bwd(
        dq, dk, dv, lat, qn, ckv, small["q_a_norm"], small["kv_a_norm"], ghq, ghk, W["wq"], W["wk"], W["wv"], rc, rs,
        tm=512, name="mla_prep_bwd")
    gs["q_head_norm"], gs["k_head_norm"] = _head_cols_inv(dghq, QK_DIM), _head_cols_inv(dghk, QK_DIM)
    gw["wq"] = wgrad(qn, dqp, "dw_uq")[0]
    gw["wk"] = wgrad(ckv, dkp, "dw_uk")[0]
    gw["wv"] = wgrad(ckv, dv, "dw_uv")[0]
    gw["convT"] = wgrad(dconv3, hm, "dw_conv", tm=CONV_COLS // 2)[0]
    gw["gateT"] = wgrad(dgl, hm, "dw_gate")[0]
    gw["latT"] = wgrad(dlat, hm, "dw_lat")[0]
    ex.scatter_sibling_now("mix", gw)
    (dx1, gs["mix_norm"]), got = _proj_bwd(dlat, dconv3, dgl, W["latT"], W["convT"], W["gateT"], x1, small["mix_norm"], dx2,
                                           tm=512, name="proj_bwd", hosted=ex.scatter_chips("mix_in"))
    ex.scatter_chips_done("mix_in", got)

    (da1, db1, *ffn1_grads), got = _ffn_grads(dx1, h1, a1, b1, f1d, tm=TM, hc=HC, name="ffn1_grads",
                                              hosted=ex.scatter_chips("mix_misc"))
    ex.scatter_chips_done("mix_misc", got)
    ex.scatter_sibling_now("ffn1", ffn1_grads)
    (dx0, gs["ffn1_norm"]), got = _ffn_up_bwd(da1, db1, f1g, f1u, x0, small["ffn1_norm"], dx1, tm=512, name="ffn1_up_bwd",
                                              hosted=ex.scatter_chips("ffn1"))
    ex.scatter_chips_done("ffn1", got)
    return loss_row, dx0.reshape(n_seq, seq, d), dconv_w, gs


class _MeshExchange:
    def __init__(self, w, core, chip):
        self.w, self.core, self.chip = w, core, chip
        self.partial, self.received, self._packed = {}, {}, None

    def _blocks(self, group):
        w = self.w
        if group.startswith("ffn"):
            return [w[group + "_w_gate"].T.astype(BF16), w[group + "_w_up"].T.astype(BF16), w[group + "_w_down"].astype(BF16)]
        if self._packed is None:
            self._packed = _pack_weights(w)
        return [self._packed[0 if group == "mix_in" else 1]]

    def gather_chips(self, *groups):
        blocks = [b for group in groups for b in self._blocks(group)]
        return _gather_chips_plan(len(blocks)), blocks, _gather_shapes(blocks)

    def gather_sibling(self, got):
        half = list(got)
        return _gather_sibling_plan(len(half)), half, _same_shapes(half)

    def gather_now(self, group):
        plan, blocks, shapes = self.gather_chips(group)
        half = list(_run_plan(plan, blocks, shapes, name="gather_%s_chips" % group))
        return self.ffn_weights(_run_plan(_gather_sibling_plan(len(half)), half, _same_shapes(half), name="gather_%s_sibling" % group))

    def ffn_weights(self, got):
        return [a.reshape(DFF, D) for a in got]

    def mix_in_weights(self, got):
        return _unpack_in(got[0])

    def mix_misc_weights(self, got):
        return _unpack_misc(got[0])

    def _parts(self, group, grads):
        if group == "mix":
            return _pack_grads(grads), ["mix_in", "mix_misc"]
        parts = [g.reshape(N_DEV, -1, D) for g in grads]
        return parts, ([group] if len(parts) == 1 else None)

    def scatter_sibling(self, group, grads):
        self._sent, self._names = self._parts(group, grads)
        return _scatter_sibling_plan(len(self._sent)), self._sent, _halved_shapes(self._sent)

    def scatter_sibling_done(self, group, got):
        sums = [_sum_sibling(p, q, self.core, name="sum_%s_sibling_%d" % (group, i)) for i, (p, q) in enumerate(zip(self._sent, got))]
        if self._names is None:
            self.partial[group] = sums
        else:
            for n, s in zip(self._names, sums):
                self.partial[n] = [s]

    def scatter_sibling_now(self, group, grads):
        plan, parts, shapes = self.scatter_sibling(group, grads)
        self.scatter_sibling_done(group, _run_plan(plan, parts, shapes, name="scatter_%s_sibling" % group))

    def scatter_chips(self, group):
        s1 = self.partial[group]
        return _scatter_chips_plan(len(s1)), s1, _scatter_shapes(s1)

    def scatter_chips_done(self, group, got):
        self.received[group] = list(got)


SMALL_NAMES = ("ffn1_norm", "mix_norm", "gate_bias", "q_a_norm", "kv_a_norm", "q_head_norm", "k_head_norm", "ffn2_norm")
SMALL_SLOTS = {"ffn1_norm": 1024, "mix_norm": 1024, "gate_bias": 2048, "q_a_norm": 384, "kv_a_norm": 256, "q_head_norm": 128,
               "k_head_norm": 128, "ffn2_norm": 1024, "conv_w": 3072, "loss": 128}
COLUMN_MAJOR = ("w_in", "w_uq", "w_uk", "w_uv")
WEIGHT_NAMES = ("ffn1_norm", "ffn1_w_gate", "ffn1_w_up", "ffn1_w_down", "mix_norm", "w_in", "gate_bias", "q_a_norm", "w_uq",
                "kv_a_norm", "w_uk", "w_uv", "q_head_norm", "k_head_norm", "w_proj_attn", "conv_w", "w_proj_conv", "w_out",
                "ffn2_norm", "ffn2_w_gate", "ffn2_w_up", "ffn2_w_down")


def _step(x, positions, loss_target, w, m, v):
    xi, yi, ci = _place()
    core = ci.astype(jnp.int32).reshape(1)
    chip = (2 * xi + yi).astype(jnp.int32).reshape(1)
    me = 4 * xi + 2 * yi + ci

    cw_all = _small_exchange(jnp.pad(w["conv_w"], ((0, 5), (0, 0))), reduce=False, name="gather_conv_w")
    conv_w = cw_all[:, :3].transpose(1, 0, 2).reshape(3, D)
    small = {n: w[n].reshape(1, -1) for n in SMALL_NAMES}
    ex = _MeshExchange(w, core, chip)

    loss_row, grad_x, dconv_w, gs = _local_step(x, positions, loss_target, conv_w, small, ex)

    grads, deltas, new_m, new_v = {}, {}, {}, {}
    where = {"ffn1_w_gate": ("ffn1", 0), "ffn1_w_up": ("ffn1", 1), "ffn1_w_down": ("ffn1", 2),
             "ffn2_w_gate": ("ffn2", 0), "ffn2_w_up": ("ffn2", 1), "ffn2_w_down": ("ffn2", 2)}
    for n, (group, i) in where.items():
        transposed = not n.endswith("down")
        wv, mv, vv = (a[n].T if transposed else a[n] for a in (w, m, v))
        res = _sum_adamw(ex.partial[group][i], ex.received[group][i], chip, wv, mv, vv, name="adamw_" + n)
        grads[n], deltas[n], new_m[n], new_v[n] = (r.T if transposed else r for r in res)
    grads.update(_unpack_grads([_sum_chips(ex.partial[g][0], ex.received[g][0], chip, name="sum_%s_chips" % g)
                                for g in ("mix_in", "mix_misc")]))

    pieces = [_pad_cols(gs[n], SMALL_SLOTS[n]) for n in SMALL_NAMES] + [dconv_w.reshape(1, 3 * D), loss_row]
    total = _small_exchange(jnp.concatenate(pieces, axis=1).reshape(-1, 128), reduce=True, name="reduce_small").reshape(-1)
    off = 0
    for n in SMALL_NAMES:
        grads[n] = total[off:off + w[n].shape[0]]
        off += SMALL_SLOTS[n]
    conv_full = total[off:off + 3 * D].reshape(3, D)
    grads["conv_w"] = lax.dynamic_slice(conv_full, (0, me * HEAD_PAD), (3, HEAD_PAD))
    loss = total[off + 3 * D]

    for n in WEIGHT_NAMES:
        if n in deltas:
            continue
        shape = w[n].shape
        if n in COLUMN_MAJOR:
            ops = [a.T for a in (w[n], grads[n], m[n], v[n])]
            deltas[n], new_m[n], new_v[n] = (r.T for r in _adamw(*ops, name="adamw_" + n))
            continue
        if len(shape) == 1:
            view = (-1, 128) if shape[0] % 128 == 0 else (1, shape[0])
        else:
            view = shape
        dlt, nm, nv = _adamw(w[n].reshape(view), grads[n].reshape(view), m[n].reshape(view), v[n].reshape(view), name="adamw_" + n)
        deltas[n], new_m[n], new_v[n] = dlt.reshape(shape), nm.reshape(shape), nv.reshape(shape)
    return (loss, grad_x, *[grads[n] for n in WEIGHT_NAMES], *[deltas[n] for n in WEIGHT_NAMES],
            *[new_m[n] for n in WEIGHT_NAMES], *[new_v[n] for n in WEIGHT_NAMES])


def kernel(x, positions, ffn1_norm, ffn1_w_gate, ffn1_w_up, ffn1_w_down, mix_norm, w_in, gate_bias, q_a_norm, w_uq, kv_a_norm, w_uk, w_uv, q_head_norm, k_head_norm, w_proj_attn, conv_w, w_proj_conv, w_out, ffn2_norm, ffn2_w_gate, ffn2_w_up, ffn2_w_down, loss_target, m_ffn1_norm, m_ffn1_w_gate, m_ffn1_w_up, m_ffn1_w_down, m_mix_norm, m_w_in, m_gate_bias, m_q_a_norm, m_w_uq, m_kv_a_norm, m_w_uk, m_w_uv, m_q_head_norm, m_k_head_norm, m_w_proj_attn, m_conv_w, m_w_proj_conv, m_w_out, m_ffn2_norm, m_ffn2_w_gate, m_ffn2_w_up, m_ffn2_w_down, v_ffn1_norm, v_ffn1_w_gate, v_ffn1_w_up, v_ffn1_w_down, v_mix_norm, v_w_in, v_gate_bias, v_q_a_norm, v_w_uq, v_kv_a_norm, v_w_uk, v_w_uv, v_q_head_norm, v_k_head_norm, v_w_proj_attn, v_conv_w, v_w_proj_conv, v_w_out, v_ffn2_norm, v_ffn2_w_gate, v_ffn2_w_up, v_ffn2_w_down):
    given = dict(locals())
    w = {n: given[n] for n in WEIGHT_NAMES}
    m = {n: given["m_" + n] for n in WEIGHT_NAMES}
    v = {n: given["v_" + n] for n in WEIGHT_NAMES}
    return _step(x, positions, loss_target, w, m, v)
```

```python
import functools

import jax
import jax.numpy as jnp
from jax import lax
from jax.experimental import pallas as pl
from jax.experimental.pallas import tpu as pltpu

F32 = jnp.float32
BF16 = jnp.bfloat16
MESH = pl.DeviceIdType.MESH
ANY = pl.BlockSpec(memory_space=pl.ANY)

N_DEV = 8
D = 1024
DFF = 2816
N_HEADS = 8
HEAD_PAD = 128
QK_DIM = 96
NOPE = 64
ROPE_HALF = 16
Q_LORA = 384
KV_LORA = 256
LAT_PAD = 768
CONV_COLS = 3072
GATE_COLS = 2048
IN_DIM = 5792
IN_SHARD = IN_DIM // N_DEV
IN_SHARD_PAD = 736
FF_SHARD = DFF // N_DEV
ROPE_THETA = 10000.0
NORM_EPS = 1e-6
ATTN_SCALE = QK_DIM ** -0.5
NEG = -1e30

ADAM_LR, ADAM_B1, ADAM_B2, ADAM_EPS, ADAM_WD, ADAM_STEP = 0.001, 0.9, 0.999, 1e-08, 0.01, 10

PACK = ((("w_inT", IN_SHARD_PAD),), (("w_uq", 48), ("w_uk", 32), ("w_uv", 32), ("w_pa", 64), ("w_pc", 128), ("w_out", 128)))
PACK_OFF = {}
for _i, _group in enumerate(PACK):
    _o = 0
    for _n, _r in _group:
        PACK_OFF[_n] = (_i, _o, _r)
        _o += _r

VMEM_LIMIT = 56 * 1024 * 1024


def _params(*sem):
    return pltpu.CompilerParams(dimension_semantics=sem if sem else None, vmem_limit_bytes=VMEM_LIMIT)


class _Plan:
    def __init__(self, start, wait, n_remote, n_local, in_place=False):
        self.start, self.wait, self.n_remote, self.n_local, self.in_place = start, wait, n_remote, n_local, in_place

    def sems(self):
        return [pltpu.SemaphoreType.DMA((self.n_remote,)), pltpu.SemaphoreType.DMA((self.n_remote,)),
                pltpu.SemaphoreType.DMA((max(self.n_local, 1),))]


def _call(body, *, name, grid, in_specs, out_specs, out_shape, scratch_shapes, operands, sem, hosted=None):
    if hosted is None:
        outs = pl.pallas_call(body, name=name, grid=grid, in_specs=in_specs, out_specs=out_specs, out_shape=out_shape,
                              scratch_shapes=scratch_shapes, compiler_params=_params(*sem))(*operands)
        return outs, None
    plan, srcs, h_shapes = hosted
    n_in, n_out, n_scr, nh_in, nh_out = len(in_specs), len(out_specs), len(scratch_shapes), len(srcs), len(h_shapes)
    aliases = {n_in + a: n_out + a for a in range(nh_in)} if plan.in_place else {}

    def full_body(*refs):
        ins, refs = refs[:n_in], refs[n_in:]
        h_in, refs = refs[:nh_in], refs[nh_in:]
        outs, refs = refs[:n_out], refs[n_out:]
        h_out, refs = refs[:nh_out], refs[nh_out:]
        scr, sems = refs[:n_scr], refs[n_scr:]
        ids = [pl.program_id(ax) for ax in range(len(grid))]
        first = functools.reduce(jnp.logical_and, [i == 0 for i in ids])
        last = functools.reduce(jnp.logical_and, [i == g - 1 for i, g in zip(ids, grid)])

        @pl.when(first)
        def _():
            plan.start(h_in, h_out, *sems)

        body(*ins, *outs, *scr)

        @pl.when(last)
        def _():
            plan.wait(h_in, h_out, *sems)

    res = pl.pallas_call(
        full_body, name=name, grid=grid, in_specs=list(in_specs) + [ANY] * nh_in, out_specs=list(out_specs) + [ANY] * nh_out,
        out_shape=list(out_shape) + list(h_shapes), scratch_shapes=list(scratch_shapes) + plan.sems(),
        input_output_aliases=aliases, compiler_params=_params(*(["arbitrary"] * len(grid))),
    )(*operands, *srcs)
    return res[:n_out], res[n_out:]


def _dot_nn(a, b):
    return lax.dot_general(a, b, (((1,), (0,)), ((), ())), preferred_element_type=F32)


def _dot_nt(a, b):
    return lax.dot_general(a, b, (((1,), (1,)), ((), ())), preferred_element_type=F32)


def _dot_tn(a, b):
    return lax.dot_general(a, b, (((0,), (0,)), ((), ())), preferred_element_type=F32)


def _sigmoid(x):
    return 0.5 * jnp.tanh(0.5 * x) + 0.5


def _rms_stats(x):
    r = lax.rsqrt(jnp.mean(x * x, axis=-1, keepdims=True) + NORM_EPS)
    return x * r, r


ROWS_WIDE = 16
MM_ROWS = 256


def _rms_bwd(dy, xhat, r, g):
    dg = jnp.sum(dy * xhat, axis=0, keepdims=True)
    dxh = dy * g
    dx = r * (dxh - xhat * jnp.mean(dxh * xhat, axis=-1, keepdims=True))
    return dx, dg


def _mm(a, b, *, mode, out_dtype, tm, tn, tk, name, add=None, scale=1.0, hosted=None):
    if mode == "nn":
        (m, k), (_, n) = a.shape, b.shape
    elif mode == "nt":
        (m, k), (n, _) = a.shape, b.shape
    else:
        (k, m), (_, n) = a.shape, b.shape
    assert m % tm == 0 and n % tn == 0 and k % tk == 0, (name, m, n, k, tm, tn, tk)
    nk = k // tk
    dot = {"nn": _dot_nn, "nt": _dot_nt, "tn": _dot_tn}[mode]
    a_spec = pl.BlockSpec((tk, tm), lambda i, j, kk: (kk, i)) if mode == "tn" else pl.BlockSpec((tm, tk), lambda i, j, kk: (i, kk))
    b_spec = pl.BlockSpec((tn, tk), lambda i, j, kk: (j, kk)) if mode == "nt" else pl.BlockSpec((tk, tn), lambda i, j, kk: (kk, j))
    o_spec = pl.BlockSpec((tm, tn), lambda i, j, kk: (i, j))
    has_add = add is not None

    def finish(prod, c_ref, o_ref):
        if scale != 1.0:
            prod = prod * scale
        o_ref[...] = ((c_ref[...] + prod) if has_add else prod).astype(out_dtype)

    def body(*refs):
        a_ref, b_ref = refs[:2]
        c_ref = refs[2] if has_add else None
        o_ref = refs[3] if has_add else refs[2]
        if nk == 1:
            finish(dot(a_ref[...], b_ref[...]), c_ref, o_ref)
            return
        acc_ref = refs[-1]
        kk = pl.program_id(2)

        @pl.when(kk == 0)
        def _():
            acc_ref[...] = jnp.zeros_like(acc_ref)

        acc_ref[...] += dot(a_ref[...], b_ref[...])

        @pl.when(kk == nk - 1)
        def _():
            finish(acc_ref[...], c_ref, o_ref)

    operands = (a, b, add) if has_add else (a, b)
    in_specs = [a_spec, b_spec] + ([o_spec] if has_add else [])
    (out,), got = _call(
        body, name=name, grid=(m // tm, n // tn, nk), in_specs=in_specs, out_specs=[o_spec],
        out_shape=[jax.ShapeDtypeStruct((m, n), out_dtype)], scratch_shapes=[pltpu.VMEM((tm, tn), F32)] if nk > 1 else [],
        operands=operands, sem=("parallel", "parallel", "arbitrary"), hosted=hosted)
    return out if hosted is None else (out, got)


def _rms_fwd(x, g, *, tm, name, hosted=None):
    t, d = x.shape

    def body(x_ref, g_ref, h_ref):
        xhat, _ = _rms_stats(x_ref[...])
        h_ref[...] = (xhat * g_ref[...]).astype(BF16)

    (h,), got = _call(
        body, name=name, grid=(t // tm,),
        in_specs=[pl.BlockSpec((tm, d), lambda i: (i, 0)), pl.BlockSpec((1, d), lambda i: (0, 0))],
        out_specs=[pl.BlockSpec((tm, d), lambda i: (i, 0))], out_shape=[jax.ShapeDtypeStruct((t, d), BF16)], scratch_shapes=[],
        operands=(x, g), sem=("parallel",), hosted=hosted)
    return h, got


def _ffn_fwd(x, g, wgT, wuT, wd, *, tm, hc, name, hosted=None, target=None):
    t, d = x.shape
    nj = DFF // hc
    with_loss = target is not None

    def body(*refs):
        x_ref, g_ref, wg_ref, wu_ref, wd_ref = refs[:5]
        t_ref = refs[5] if with_loss else None
        xo_ref, h_ref, a_ref, b_ref = refs[5 + with_loss:9 + with_loss]
        loss_ref = refs[9 + with_loss] if with_loss else None
        acc_ref = refs[-1]
        i, j = pl.program_id(0), pl.program_id(1)

        @pl.when(j == 0)
        def _():
            xhat, _ = _rms_stats(x_ref[...])
            h_ref[...] = (xhat * g_ref[...]).astype(BF16)
            acc_ref[...] = jnp.zeros_like(acc_ref)

        h = h_ref[...]
        a = _dot_nt(h, wg_ref[...])
        b = _dot_nt(h, wu_ref[...])
        a_ref[...] = a.astype(BF16)
        b_ref[...] = b.astype(BF16)
        s = (a * _sigmoid(a) * b).astype(BF16)
        acc_ref[...] += _dot_nn(s, wd_ref[...])

        if with_loss:
            @pl.when((i == 0) & (j == 0))
            def _():
                loss_ref[...] = jnp.zeros_like(loss_ref)

        @pl.when(j == nj - 1)
        def _():
            y = x_ref[...] + 0.5 * acc_ref[...]
            if with_loss:
                err = y - t_ref[...]
                xo_ref[...] = err * (1.0 / d)
                loss_ref[...] += jnp.sum(jnp.sum(err * err, axis=-1, keepdims=True), axis=0, keepdims=True) * (0.5 / d)
            else:
                xo_ref[...] = y

    row = pl.BlockSpec((tm, d), lambda i, j: (i, 0))
    vec = pl.BlockSpec((1, d), lambda i, j: (0, 0))
    wsp = pl.BlockSpec((hc, d), lambda i, j: (j, 0))
    hid = pl.BlockSpec((tm, hc), lambda i, j: (i, j))
    out_specs = [row, row, hid, hid] + ([pl.BlockSpec((1, 128), lambda i, j: (0, 0))] if with_loss else [])
    out_shape = [jax.ShapeDtypeStruct((t, d), F32), jax.ShapeDtypeStruct((t, d), BF16), jax.ShapeDtypeStruct((t, DFF), BF16),
                 jax.ShapeDtypeStruct((t, DFF), BF16)] + ([jax.ShapeDtypeStruct((1, 128), F32)] if with_loss else [])
    return _call(
        body, name=name, grid=(t // tm, nj), in_specs=[row, vec, wsp, wsp, wsp] + ([row] if with_loss else []),
        out_specs=out_specs, out_shape=out_shape, scratch_shapes=[pltpu.VMEM((tm, d), F32)],
        operands=(x, g, wgT, wuT, wd) + ((target,) if with_loss else ()),
        sem=("arbitrary" if with_loss else "parallel", "arbitrary"), hosted=hosted)


def _ffn_grads(dout, h, a, b, wd, *, tm, hc, name, hosted=None):
    t, d = dout.shape
    ni, nj = t // tm, DFF // hc

    def body(dout_ref, h_ref, a_ref, b_ref, wd_ref, da_ref, db_ref, dwg_ref, dwu_ref, dwd_ref,
             dy_all, h_all, ds_scr, s_scr, acc_g, acc_u, acc_d):
        j, i = pl.program_id(0), pl.program_id(1)
        rows_i = pl.ds(pl.multiple_of(i * tm, tm), tm)

        @pl.when(j == 0)
        def _():
            dy_all[rows_i, :] = (0.5 * dout_ref[...]).astype(BF16)
            h_all[rows_i, :] = h_ref[...]

        @pl.when(i == 0)
        def _():
            acc_g[...] = jnp.zeros_like(acc_g)
            acc_u[...] = jnp.zeros_like(acc_u)
            acc_d[...] = jnp.zeros_like(acc_d)

        def grad_rows(rows):
            ds = ds_scr[rows, :]
            av = a_ref[rows, :].astype(F32)
            bv = b_ref[rows, :].astype(F32)
            sg = _sigmoid(av)
            sl = av * sg
            s_scr[rows, :] = (sl * bv).astype(BF16)
            da_ref[rows, :] = (ds * bv * (sg + sl * (1.0 - sg))).astype(BF16)
            db_ref[rows, :] = (ds * sl).astype(BF16)

        for blk in range(tm // MM_ROWS):
            rs = slice(blk * MM_ROWS, (blk + 1) * MM_ROWS)
            ds_scr[rs, :] = _dot_nt(dy_all[pl.ds(pl.multiple_of(i * tm + blk * MM_ROWS, MM_ROWS), MM_ROWS), :], wd_ref[...])
            for c in range(MM_ROWS // ROWS_WIDE):
                grad_rows(slice(blk * MM_ROWS + c * ROWS_WIDE, blk * MM_ROWS + (c + 1) * ROWS_WIDE))

        dy_i = dy_all[rows_i, :]
        h_i = h_all[rows_i, :]
        acc_d[...] += _dot_tn(s_scr[...], dy_i)
        acc_g[...] += _dot_tn(da_ref[...], h_i)
        acc_u[...] += _dot_tn(db_ref[...], h_i)

        @pl.when(i == ni - 1)
        def _():
            dwg_ref[...] = acc_g[...].astype(BF16)
            dwu_ref[...] = acc_u[...].astype(BF16)
            dwd_ref[...] = acc_d[...].astype(BF16)

    first = pl.BlockSpec((tm, d), lambda j, i: (jnp.where(j == 0, i, 0), 0))
    hid = pl.BlockSpec((tm, hc), lambda j, i: (i, j))
    wsp = pl.BlockSpec((hc, d), lambda j, i: (j, 0))
    hid_shape = jax.ShapeDtypeStruct((t, DFF), BF16)
    w_shape = jax.ShapeDtypeStruct((DFF, d), BF16)
    return _call(
        body, name=name, grid=(nj, ni), in_specs=[first, first, hid, hid, wsp], out_specs=[hid, hid, wsp, wsp, wsp],
        out_shape=[hid_shape, hid_shape, w_shape, w_shape, w_shape],
        scratch_shapes=[pltpu.VMEM((t, d), BF16), pltpu.VMEM((t, d), BF16), pltpu.VMEM((tm, hc), F32), pltpu.VMEM((tm, hc), BF16),
                        pltpu.VMEM((hc, d), F32), pltpu.VMEM((hc, d), F32), pltpu.VMEM((hc, d), F32)],
        operands=(dout, h, a, b, wd), sem=("arbitrary", "arbitrary"), hosted=hosted)


def _proj_fwd(h, latT, convT, gateT, *, tm, name, hosted=None):
    t, d = h.shape

    def body(h_ref, wl_ref, wc_ref, wg_ref, lat_ref, conv_ref, gl_ref):
        hv = h_ref[...]
        lat_ref[...] = _dot_nt(hv, wl_ref[...]).astype(BF16)
        conv_ref[...] = _dot_nt(hv, wc_ref[...]).astype(BF16)
        gl_ref[...] = _dot_nt(hv, wg_ref[...]).astype(BF16)

    def rows(w):
        return pl.BlockSpec((tm, w), lambda i: (i, 0))

    def full(r):
        return pl.BlockSpec((r, d), lambda i: (0, 0))

    return _call(
        body, name=name, grid=(t // tm,), in_specs=[rows(d), full(LAT_PAD), full(CONV_COLS), full(GATE_COLS)],
        out_specs=[rows(LAT_PAD), rows(CONV_COLS), rows(GATE_COLS)],
        out_shape=[jax.ShapeDtypeStruct((t, LAT_PAD), BF16), jax.ShapeDtypeStruct((t, CONV_COLS), BF16),
                   jax.ShapeDtypeStruct((t, GATE_COLS), BF16)],
        scratch_shapes=[], operands=(h, latT, convT, gateT), sem=("parallel",), hosted=hosted)


def _proj_bwd(dlat, dconv3, dgl, latT, convT, gateT, x, g, dres, *, tm, name, hosted=None):
    t, d = x.shape

    def body(dl_ref, dc_ref, dg_ref, wl_ref, wc_ref, wg_ref, x_ref, g_ref, dres_ref, dx_ref, dgain_ref):
        @pl.when(pl.program_id(0) == 0)
        def _():
            dgain_ref[...] = jnp.zeros_like(dgain_ref)

        dh = _dot_nn(dl_ref[...], wl_ref[...]) + _dot_nn(dc_ref[...], wc_ref[...]) + _dot_nn(dg_ref[...], wg_ref[...])
        xhat, r = _rms_stats(x_ref[...])
        dx, dgain = _rms_bwd(dh, xhat, r, g_ref[...])
        dx_ref[...] = dres_ref[...] + dx
        dgain_ref[...] += dgain

    def rows(w):
        return pl.BlockSpec((tm, w), lambda i: (i, 0))

    def full(r):
        return pl.BlockSpec((r, d), lambda i: (0, 0))

    return _call(
        body, name=name, grid=(t // tm,),
        in_specs=[rows(LAT_PAD), rows(CONV_COLS), rows(GATE_COLS), full(LAT_PAD), full(CONV_COLS), full(GATE_COLS), rows(d), full(1), rows(d)],
        out_specs=[rows(d), full(1)], out_shape=[jax.ShapeDtypeStruct((t, d), F32), jax.ShapeDtypeStruct((1, d), F32)],
        scratch_shapes=[], operands=(dlat, dconv3, dgl, latT, convT, gateT, x, g, dres), sem=("arbitrary",), hosted=hosted)


def _ffn_up_bwd(da, db, wgT, wuT, x, g, dout, *, tm, name, hosted=None):
    t, d = x.shape

    def body(da_ref, db_ref, wg_ref, wu_ref, x_ref, g_ref, dout_ref, dx_ref, dg_ref):
        @pl.when(pl.program_id(0) == 0)
        def _():
            dg_ref[...] = jnp.zeros_like(dg_ref)

        dh = _dot_nn(da_ref[...], wg_ref[...]) + _dot_nn(db_ref[...], wu_ref[...])
        xhat, r = _rms_stats(x_ref[...])
        dx, dg = _rms_bwd(dh, xhat, r, g_ref[...])
        dx_ref[...] = dout_ref[...] + dx
        dg_ref[...] += dg

    row = pl.BlockSpec((tm, d), lambda i: (i, 0))
    vec = pl.BlockSpec((1, d), lambda i: (0, 0))
    hid = pl.BlockSpec((tm, DFF), lambda i: (i, 0))
    wsp = pl.BlockSpec((DFF, d), lambda i: (0, 0))
    return _call(
        body, name=name, grid=(t // tm,), in_specs=[hid, hid, wsp, wsp, row, vec, row], out_specs=[row, vec],
        out_shape=[jax.ShapeDtypeStruct((t, d), F32), jax.ShapeDtypeStruct((1, d), F32)], scratch_shapes=[],
        operands=(da, db, wgT, wuT, x, g, dout), sem=("arbitrary",), hosted=hosted)


HEAD_LANES = (slice(0, 32), slice(64, 80), None, slice(32, 64), slice(80, 96), None)


def _head_cols(a):
    def part(sl, width):
        if sl is None or sl.stop > a.shape[1]:
            return jnp.zeros((a.shape[0], width), a.dtype)
        return a[:, sl]

    return jnp.concatenate([part(sl, w) for sl, w in zip(HEAD_LANES, (32, 16, 16, 32, 16, 16))], axis=1)


def _head_cols_inv(a, dims):
    parts = [a[:, 0:32], a[:, 64:96]] + ([a[:, 32:48], a[:, 96:112]] if dims == QK_DIM else [])
    return jnp.concatenate(parts, axis=1)


def _rope_fwd(x, c, s):
    return x * c + pltpu.roll(x, HEAD_PAD // 2, 1) * s


def _rope_bwd(dy, c, s):
    return dy * c + pltpu.roll(dy * s, HEAD_PAD // 2, 1)


def _head_stats(x):
    r = lax.rsqrt(jnp.sum(x * x, axis=-1, keepdims=True) * (1.0 / QK_DIM) + NORM_EPS)
    return x * r, r


def _mla_prep_fwd(lat, gq, gkv, ghq, ghk, wq, wk, wv, rc, rs, *, tm, name):
    t = lat.shape[0]

    def body(lat_ref, gq_ref, gkv_ref, ghq_ref, ghk_ref, wq_ref, wk_ref, wv_ref, c_ref, s_ref,
             q_ref, k_ref, v_ref, qn_ref, ckv_ref):
        lat_v = lat_ref[...]
        qhat, _ = _rms_stats(lat_v[:, :Q_LORA].astype(F32))
        qn = (qhat * gq_ref[...]).astype(BF16)
        khat, _ = _rms_stats(lat_v[:, Q_LORA:Q_LORA + KV_LORA].astype(F32))
        ckv = (khat * gkv_ref[...]).astype(BF16)
        ckv_ext = jnp.concatenate([ckv, lat_v[:, Q_LORA + KV_LORA:]], axis=1)
        qn_ref[...] = qn
        ckv_ref[...] = ckv_ext
        q_pre = _dot_nn(qn, wq_ref[...])
        k_pre = _dot_nn(ckv_ext, wk_ref[...])
        v_ref[...] = _dot_nn(ckv, wv_ref[...]).astype(BF16)
        c, s = c_ref[...], s_ref[...]
        for h in range(N_HEADS):
            hs = slice(h * HEAD_PAD, (h + 1) * HEAD_PAD)
            xq, _ = _head_stats(q_pre[:, hs])
            q_ref[:, hs] = _rope_fwd(xq * ghq_ref[...], c, s).astype(BF16)
            xk, _ = _head_stats(k_pre[:, hs])
            k_ref[:, hs] = _rope_fwd(xk * ghk_ref[...], c, s).astype(BF16)

    def row(w):
        return pl.BlockSpec((tm, w), lambda i: (i, 0))

    def full(r, w):
        return pl.BlockSpec((r, w), lambda i: (0, 0))

    wide = jax.ShapeDtypeStruct((t, D), BF16)
    lat3 = jax.ShapeDtypeStruct((t, Q_LORA), BF16)
    return pl.pallas_call(
        body, name=name, grid=(t // tm,),
        in_specs=[row(LAT_PAD), full(1, Q_LORA), full(1, KV_LORA), full(1, HEAD_PAD), full(1, HEAD_PAD),
                  full(Q_LORA, D), full(Q_LORA, D), full(KV_LORA, D), row(HEAD_PAD), row(HEAD_PAD)],
        out_specs=[row(D), row(D), row(D), row(Q_LORA), row(Q_LORA)],
        out_shape=[wide, wide, wide, lat3, lat3],
        compiler_params=_params("parallel"),
    )(lat, gq, gkv, ghq, ghk, wq, wk, wv, rc, rs)


def _mla_prep_bwd(dq, dk, dv, lat, qn, ckv_ext, gq, gkv, ghq, ghk, wq, wk, wv, rc, rs, *, tm, name):
    t = lat.shape[0]

    def body(dq_ref, dk_ref, dv_ref, lat_ref, qn_ref, ckv_ref, gq_ref, gkv_ref, ghq_ref, ghk_ref, wq_ref, wk_ref, wv_ref,
             c_ref, s_ref, dlat_ref, dqp_ref, dkp_ref, dgq_ref, dgkv_ref, dghq_ref, dghk_ref):
        @pl.when(pl.program_id(0) == 0)
        def _():
            dgq_ref[...] = jnp.zeros_like(dgq_ref)
            dgkv_ref[...] = jnp.zeros_like(dgkv_ref)
            dghq_ref[...] = jnp.zeros_like(dghq_ref)
            dghk_ref[...] = jnp.zeros_like(dghk_ref)

        c, s = c_ref[...], s_ref[...]
        q_pre = _dot_nn(qn_ref[...], wq_ref[...])
        k_pre = _dot_nn(ckv_ref[...], wk_ref[...])

        def heads(pre, dy_ref, gh_ref, dgh_ref, out_ref):
            dgh = jnp.zeros((1, HEAD_PAD), F32)
            for h in range(N_HEADS):
                hs = slice(h * HEAD_PAD, (h + 1) * HEAD_PAD)
                d = _rope_bwd(dy_ref[:, hs].astype(F32), c, s)
                xhat, r = _head_stats(pre[:, hs])
                dgh = dgh + jnp.sum(d * xhat, axis=0, keepdims=True)
                dxh = d * gh_ref[...]
                dx = r * (dxh - xhat * (jnp.sum(dxh * xhat, axis=-1, keepdims=True) * (1.0 / QK_DIM)))
                out_ref[:, hs] = dx.astype(BF16)
            dgh_ref[...] += dgh

        heads(q_pre, dq_ref, ghq_ref, dghq_ref, dqp_ref)
        heads(k_pre, dk_ref, ghk_ref, dghk_ref, dkp_ref)
        dqn = _dot_nt(dqp_ref[...], wq_ref[...])
        dce = _dot_nt(dkp_ref[...], wk_ref[...])
        dckv = dce[:, :KV_LORA] + _dot_nt(dv_ref[...], wv_ref[...])
        lat_v = lat_ref[...]
        qhat, rq = _rms_stats(lat_v[:, :Q_LORA].astype(F32))
        dql, dgq = _rms_bwd(dqn, qhat, rq, gq_ref[...])
        khat, rk = _rms_stats(lat_v[:, Q_LORA:Q_LORA + KV_LORA].astype(F32))
        dkl, dgkv = _rms_bwd(dckv, khat, rk, gkv_ref[...])
        dgq_ref[...] += dgq
        dgkv_ref[...] += dgkv
        dlat_ref[...] = jnp.concatenate([dql, dkl, dce[:, KV_LORA:]], axis=1).astype(BF16)

    def row(w):
        return pl.BlockSpec((tm, w), lambda i: (i, 0))

    def full(r, w):
        return pl.BlockSpec((r, w), lambda i: (0, 0))

    return pl.pallas_call(
        body, name=name, grid=(t // tm,),
        in_specs=[row(D), row(D), row(D), row(LAT_PAD), row(Q_LORA), row(Q_LORA), full(1, Q_LORA), full(1, KV_LORA),
                  full(1, HEAD_PAD), full(1, HEAD_PAD), full(Q_LORA, D), full(Q_LORA, D), full(KV_LORA, D),
                  row(HEAD_PAD), row(HEAD_PAD)],
        out_specs=[row(LAT_PAD), row(D), row(D), full(1, Q_LORA), full(1, KV_LORA), full(1, HEAD_PAD), full(1, HEAD_PAD)],
        out_shape=[jax.ShapeDtypeStruct((t, LAT_PAD), BF16), jax.ShapeDtypeStruct((t, D), BF16), jax.ShapeDtypeStruct((t, D), BF16),
                   jax.ShapeDtypeStruct((1, Q_LORA), F32), jax.ShapeDtypeStruct((1, KV_LORA), F32),
                   jax.ShapeDtypeStruct((1, HEAD_PAD), F32), jax.ShapeDtypeStruct((1, HEAD_PAD), F32)],
        compiler_params=_params("arbitrary"),
    )(dq, dk, dv, lat, qn, ckv_ext, gq, gkv, ghq, ghk, wq, wk, wv, rc, rs)


def _causal_keep(tq):
    r = lax.broadcasted_iota(jnp.int32, (tq, tq), 0)
    c = lax.broadcasted_iota(jnp.int32, (tq, tq), 1)
    return c <= r


def _flash_fwd(q, k, v, *, n_seq, seq, tq, name, hosted=None):
    nq = seq // tq

    def body(q_ref, k_ref, v_ref, o_ref, lse_ref):
        qi = pl.program_id(2)
        qv = q_ref[...]

        def step(j, carry, masked):
            m, l, acc = carry
            kj = k_ref[pl.ds(pl.multiple_of(j * tq, tq), tq), :]
            vj = v_ref[pl.ds(pl.multiple_of(j * tq, tq), tq), :]
            s = _dot_nt(qv, kj) * ATTN_SCALE
            if masked:
                s = jnp.where(_causal_keep(tq), s, NEG)
            m_new = jnp.maximum(m, jnp.max(s, axis=-1, keepdims=True))
            alpha = jnp.exp(m - m_new)
            p = jnp.exp(s - m_new)
            l = alpha * l + jnp.sum(p, axis=-1, keepdims=True)
            acc = alpha * acc + _dot_nn(p.astype(BF16), vj)
            return m_new, l, acc

        init = (jnp.full((tq, 1), NEG, F32), jnp.zeros((tq, 1), F32), jnp.zeros((tq, HEAD_PAD), F32))
        carry = lax.fori_loop(0, qi, lambda j, cr: step(j, cr, False), init)
        m, l, acc = step(qi, carry, True)
        o_ref[...] = (acc / l).astype(BF16)
        lse_ref[...] = jnp.broadcast_to(m + jnp.log(l), (tq, HEAD_PAD))

    qspec = pl.BlockSpec((tq, HEAD_PAD), lambda b, h, i: (b * nq + i, h))
    kspec = pl.BlockSpec((seq, HEAD_PAD), lambda b, h, i: (b, h))
    t = n_seq * seq
    return _call(
        body, name=name, grid=(n_seq, N_HEADS, nq), in_specs=[qspec, kspec, kspec], out_specs=[qspec, qspec],
        out_shape=[jax.ShapeDtypeStruct((t, D), BF16), jax.ShapeDtypeStruct((t, D), F32)], scratch_shapes=[],
        operands=(q, k, v), sem=("parallel", "parallel", "arbitrary"), hosted=hosted)


def _flash_bwd(q, k, v, o, lse, do, *, n_seq, seq, tq, name, hosted=None):
    nq = seq // tq

    def body(q_ref, k_ref, v_ref, o_ref, lse_ref, do_ref, dq_ref, dk_ref, dv_ref, dk_acc, dv_acc):
        j = pl.program_id(2)

        @pl.when(j == 0)
        def _():
            dq_ref[...] = jnp.zeros_like(dq_ref)

        dk_acc[...] = jnp.zeros_like(dk_acc)
        dv_acc[...] = jnp.zeros_like(dv_acc)
        kv = k_ref[...]
        vv = v_ref[...]

        def step(i, masked):
            rows = pl.ds(pl.multiple_of(i * tq, tq), tq)
            qi = q_ref[rows, :]
            doi = do_ref[rows, :]
            delta = jnp.sum(doi.astype(F32) * o_ref[rows, :].astype(F32), axis=-1, keepdims=True)
            s = _dot_nt(qi, kv) * ATTN_SCALE
            p = jnp.exp(s - lse_ref[rows, :][:, :1])
            if masked:
                p = jnp.where(_causal_keep(tq), p, 0.0)
            dv_acc[...] += _dot_tn(p.astype(BF16), doi)
            dp = _dot_nt(doi, vv)
            ds = (p * (dp - delta) * ATTN_SCALE).astype(BF16)
            dk_acc[...] += _dot_tn(ds, qi)
            dq_ref[rows, :] += _dot_nn(ds, kv)

        step(j, True)

        def loop_body(i, carry):
            step(i, False)
            return carry

        lax.fori_loop(j + 1, nq, loop_body, 0)
        dk_ref[...] = dk_acc[...]
        dv_ref[...] = dv_acc[...].astype(BF16)

    full = pl.BlockSpec((seq, HEAD_PAD), lambda b, h, j: (b, h))
    tile = pl.BlockSpec((tq, HEAD_PAD), lambda b, h, j: (b * nq + j, h))
    t = n_seq * seq
    return _call(
        body, name=name, grid=(n_seq, N_HEADS, nq), in_specs=[full, tile, tile, full, full, full],
        out_specs=[full, tile, tile],
        out_shape=[jax.ShapeDtypeStruct((t, D), F32), jax.ShapeDtypeStruct((t, D), F32), jax.ShapeDtypeStruct((t, D), BF16)],
        scratch_shapes=[pltpu.VMEM((tq, HEAD_PAD), F32), pltpu.VMEM((tq, HEAD_PAD), F32)],
        operands=(q, k, v, o, lse, do), sem=("parallel", "parallel", "arbitrary"), hosted=hosted)


CONV_CB = 256


def _shift_down(u, k, row):
    return jnp.where(row >= k, pltpu.roll(u, k, 0), 0.0)


def _shift_up(u, k, row, n):
    return jnp.where(row < n - k, pltpu.roll(u, n - k, 0), 0.0)


def _conv_fwd(conv3, cw, *, n_seq, seq, name, hosted=None):
    def body(c_ref, w_ref, p_ref):
        blk = c_ref[...].astype(F32)
        xc, gb, gc = blk[:, :CONV_CB], blk[:, CONV_CB:2 * CONV_CB], blk[:, 2 * CONV_CB:]
        row = lax.broadcasted_iota(jnp.int32, (seq, CONV_CB), 0)
        u = gc * xc
        z = w_ref[0:1, :] * _shift_down(u, 2, row) + w_ref[1:2, :] * _shift_down(u, 1, row) + w_ref[2:3, :] * u
        p_ref[...] = (gb * z).astype(BF16)

    (p,), got = _call(
        body, name=name, grid=(n_seq, D // CONV_CB),
        in_specs=[pl.BlockSpec((seq, 3 * CONV_CB), lambda b, j: (b, j)), pl.BlockSpec((3, CONV_CB), lambda b, j: (0, j))],
        out_specs=[pl.BlockSpec((seq, CONV_CB), lambda b, j: (b, j))],
        out_shape=[jax.ShapeDtypeStruct((n_seq * seq, D), BF16)], scratch_shapes=[],
        operands=(conv3, cw), sem=("parallel", "parallel"), hosted=hosted)
    return p, got


def _conv_bwd(dp, conv3, cw, *, n_seq, seq, name):
    def body(dp_ref, c_ref, w_ref, dc_ref, dw_ref):
        @pl.when(pl.program_id(1) == 0)
        def _():
            dw_ref[...] = jnp.zeros_like(dw_ref)

        blk = c_ref[...].astype(F32)
        xc, gb, gc = blk[:, :CONV_CB], blk[:, CONV_CB:2 * CONV_CB], blk[:, 2 * CONV_CB:]
        row = lax.broadcasted_iota(jnp.int32, (seq, CONV_CB), 0)
        w0, w1, w2 = w_ref[0:1, :], w_ref[1:2, :], w_ref[2:3, :]
        u = gc * xc
        u1 = _shift_down(u, 1, row)
        u2 = _shift_down(u, 2, row)
        z = w0 * u2 + w1 * u1 + w2 * u
        dpv = dp_ref[...].astype(F32)
        dz = dpv * gb
        du = w2 * dz + w1 * _shift_up(dz, 1, row, seq) + w0 * _shift_up(dz, 2, row, seq)
        dc_ref[...] = jnp.concatenate([du * gc, dpv * z, du * xc], axis=1).astype(BF16)
        dw_ref[0:1, :] += jnp.sum(dz * u2, axis=0, keepdims=True)
        dw_ref[1:2, :] += jnp.sum(dz * u1, axis=0, keepdims=True)
        dw_ref[2:3, :] += jnp.sum(dz * u, axis=0, keepdims=True)

    return pl.pallas_call(
        body, name=name, grid=(D // CONV_CB, n_seq),
        in_specs=[pl.BlockSpec((seq, CONV_CB), lambda j, b: (b, j)), pl.BlockSpec((seq, 3 * CONV_CB), lambda j, b: (b, j)),
                  pl.BlockSpec((3, CONV_CB), lambda j, b: (0, j))],
        out_specs=[pl.BlockSpec((seq, 3 * CONV_CB), lambda j, b: (b, j)), pl.BlockSpec((3, CONV_CB), lambda j, b: (0, j))],
        out_shape=[jax.ShapeDtypeStruct((n_seq * seq, CONV_COLS), BF16), jax.ShapeDtypeStruct((3, D), F32)],
        compiler_params=_params("parallel", "arbitrary"),
    )(dp, conv3, cw)


def _merge_fwd(o, p, gl, bias, x1, wpa, wpc, wout, *, tm, name, hosted=None):
    t = x1.shape[0]

    def body(o_ref, p_ref, gl_ref, b_ref, x_ref, wpa_ref, wpc_ref, wout_ref, x2_ref, mg_ref, ya_ref, yb_ref):
        ya = _dot_nn(o_ref[...], wpa_ref[...])
        yb = _dot_nn(p_ref[...], wpc_ref[...])
        gates = _sigmoid(gl_ref[...].astype(F32) + b_ref[...])
        merged = (gates[:, :D] * ya + gates[:, D:] * yb).astype(BF16)
        ya_ref[...] = ya.astype(BF16)
        yb_ref[...] = yb.astype(BF16)
        mg_ref[...] = merged
        x2_ref[...] = x_ref[...] + _dot_nn(merged, wout_ref[...])

    row = pl.BlockSpec((tm, D), lambda i: (i, 0))
    row2 = pl.BlockSpec((tm, GATE_COLS), lambda i: (i, 0))
    wsp = pl.BlockSpec((D, D), lambda i: (0, 0))
    wide = jax.ShapeDtypeStruct((t, D), BF16)
    return _call(
        body, name=name, grid=(t // tm,),
        in_specs=[row, row, row2, pl.BlockSpec((1, GATE_COLS), lambda i: (0, 0)), row, wsp, wsp, wsp],
        out_specs=[row, row, row, row], out_shape=[jax.ShapeDtypeStruct((t, D), F32), wide, wide, wide], scratch_shapes=[],
        operands=(o, p, gl, bias, x1, wpa, wpc, wout), sem=("parallel",), hosted=hosted)


def _merge_bwd(dx2, ya, yb, gl, bias, wpa, wpc, wout, *, tm, name, hosted=None):
    t = dx2.shape[0]

    def body(dx_ref, ya_ref, yb_ref, gl_ref, b_ref, wpa_ref, wpc_ref, wout_ref,
             dxb_ref, dya_ref, dyb_ref, dgl_ref, do_ref, dp_ref, db_ref):
        @pl.when(pl.program_id(0) == 0)
        def _():
            db_ref[...] = jnp.zeros_like(db_ref)

        dxb = dx_ref[...].astype(BF16)
        dxb_ref[...] = dxb
        dm = _dot_nt(dxb, wout_ref[...])
        gates = _sigmoid(gl_ref[...].astype(F32) + b_ref[...])
        ga, gb = gates[:, :D], gates[:, D:]
        dya = (dm * ga).astype(BF16)
        dyb = (dm * gb).astype(BF16)
        dya_ref[...] = dya
        dyb_ref[...] = dyb
        dgl = jnp.concatenate([dm * ya_ref[...].astype(F32) * ga * (1.0 - ga),
                               dm * yb_ref[...].astype(F32) * gb * (1.0 - gb)], axis=1)
        dgl_ref[...] = dgl.astype(BF16)
        db_ref[...] += jnp.sum(dgl, axis=0, keepdims=True)
        do_ref[...] = _dot_nt(dya, wpa_ref[...]).astype(BF16)
        dp_ref[...] = _dot_nt(dyb, wpc_ref[...]).astype(BF16)

    row = pl.BlockSpec((tm, D), lambda i: (i, 0))
    row2 = pl.BlockSpec((tm, GATE_COLS), lambda i: (i, 0))
    vec2 = pl.BlockSpec((1, GATE_COLS), lambda i: (0, 0))
    wsp = pl.BlockSpec((D, D), lambda i: (0, 0))
    wide = jax.ShapeDtypeStruct((t, D), BF16)
    return _call(
        body, name=name, grid=(t // tm,), in_specs=[row, row, row, row2, vec2, wsp, wsp, wsp],
        out_specs=[row, row, row, row2, row, row, vec2],
        out_shape=[wide, wide, wide, jax.ShapeDtypeStruct((t, GATE_COLS), BF16), wide, wide,
                   jax.ShapeDtypeStruct((1, GATE_COLS), F32)],
        scratch_shapes=[], operands=(dx2, ya, yb, gl, bias, wpa, wpc, wout), sem=("arbitrary",), hosted=hosted)


def _adamw(w, g, m, v, *, name):
    rows, cols = w.shape
    tr = max([c for c in range(8, 513, 8) if rows % c == 0], default=rows)
    c1 = 1.0 / (1.0 - ADAM_B1 ** ADAM_STEP)
    c2 = 1.0 / (1.0 - ADAM_B2 ** ADAM_STEP)

    def body(w_ref, g_ref, m_ref, v_ref, d_ref, nm_ref, nv_ref):
        gv = g_ref[...]
        nm = ADAM_B1 * m_ref[...] + (1.0 - ADAM_B1) * gv
        nv = ADAM_B2 * v_ref[...] + (1.0 - ADAM_B2) * (gv * gv)
        nm_ref[...] = nm
        nv_ref[...] = nv
        d_ref[...] = -ADAM_LR * ((nm * c1) / (jnp.sqrt(nv * c2) + ADAM_EPS) + ADAM_WD * w_ref[...])

    spec = pl.BlockSpec((tr, cols), lambda i: (i, 0))
    shp = jax.ShapeDtypeStruct((rows, cols), F32)
    return pl.pallas_call(
        body, name=name, grid=(rows // tr,), in_specs=[spec] * 4, out_specs=[spec] * 3, out_shape=[shp] * 3,
        compiler_params=_params("parallel"),
    )(w, g, m, v)


def _place():
    return lax.axis_index("x"), lax.axis_index("y"), lax.axis_index("c")


def _other_chips(x, y):
    return [(1 - x, y), (x, 1 - y), (1 - x, 1 - y)]


def _remote(src, dst, send, recv, dev):
    return pltpu.make_async_remote_copy(src_ref=src, dst_ref=dst, send_sem=send, recv_sem=recv, device_id=dev, device_id_type=MESH)


def _gather_chips_plan(n):
    def start(srcs, dsts, send, recv, local):
        x, y, cc = _place()
        me = 4 * x + 2 * y + cc
        for a in range(n):
            pltpu.make_async_copy(srcs[a], dsts[a].at[me], local.at[a]).start()
            for k, (px, py) in enumerate(_other_chips(x, y)):
                _remote(srcs[a], dsts[a].at[me], send.at[3 * a + k], recv.at[3 * a + k], (px, py, cc)).start()

    def wait(srcs, dsts, send, recv, local):
        x, y, cc = _place()
        me = 4 * x + 2 * y + cc
        for a in range(n):
            for k, (px, py) in enumerate(_other_chips(x, y)):
                _remote(srcs[a], dsts[a].at[4 * px + 2 * py + cc], send.at[3 * a + k], recv.at[3 * a + k], (px, py, cc)).wait_recv()
        for a in range(n):
            for k, (px, py) in enumerate(_other_chips(x, y)):
                _remote(srcs[a], dsts[a].at[me], send.at[3 * a + k], recv.at[3 * a + k], (px, py, cc)).wait_send()
            pltpu.make_async_copy(srcs[a], dsts[a].at[me], local.at[a]).wait()

    return _Plan(start, wait, 3 * n, n)


def _scatter_chips_plan(n):
    def start(srcs, dsts, send, recv, local):
        x, y, cc = _place()
        for a in range(n):
            for k, (px, py) in enumerate(_other_chips(x, y)):
                _remote(srcs[a].at[2 * px + py], dsts[a].at[k], send.at[3 * a + k], recv.at[3 * a + k], (px, py, cc)).start()

    def wait(srcs, dsts, send, recv, local):
        x, y, cc = _place()
        for a in range(n):
            for k, (px, py) in enumerate(_other_chips(x, y)):
                _remote(srcs[a].at[k], dsts[a].at[k], send.at[3 * a + k], recv.at[3 * a + k], (px, py, cc)).wait_recv()
        for a in range(n):
            for k, (px, py) in enumerate(_other_chips(x, y)):
                _remote(srcs[a].at[k], dsts[a].at[k], send.at[3 * a + k], recv.at[3 * a + k], (px, py, cc)).wait_send()

    return _Plan(start, wait, 3 * n, 0)


def _gather_shapes(blocks):
    return [jax.ShapeDtypeStruct((N_DEV,) + b.shape, b.dtype) for b in blocks]


def _scatter_shapes(parts):
    return [jax.ShapeDtypeStruct((3,) + p.shape[1:], p.dtype) for p in parts]


def _gather_sibling_plan(n):
    def start(srcs, dsts, send, recv, local):
        x, y, cc = _place()
        for a in range(n):
            for q in range(4):
                _remote(srcs[a].at[2 * q + cc], dsts[a].at[2 * q + cc], send.at[4 * a + q], recv.at[4 * a + q], (x, y, 1 - cc)).start()

    def wait(srcs, dsts, send, recv, local):
        x, y, cc = _place()
        for a in range(n):
            for q in range(4):
                _remote(srcs[a].at[2 * q + cc], dsts[a].at[2 * q + 1 - cc], send.at[4 * a + q], recv.at[4 * a + q],
                        (x, y, 1 - cc)).wait_recv()
        for a in range(n):
            for q in range(4):
                _remote(srcs[a].at[2 * q + cc], dsts[a].at[2 * q + cc], send.at[4 * a + q], recv.at[4 * a + q],
                        (x, y, 1 - cc)).wait_send()

    return _Plan(start, wait, 4 * n, 0, in_place=True)


def _scatter_sibling_plan(n):
    def start(srcs, dsts, send, recv, local):
        x, y, cc = _place()
        for a in range(n):
            for q in range(4):
                _remote(srcs[a].at[2 * q + 1 - cc], dsts[a].at[q], send.at[4 * a + q], recv.at[4 * a + q], (x, y, 1 - cc)).start()

    def wait(srcs, dsts, send, recv, local):
        x, y, cc = _place()
        for a in range(n):
            for q in range(4):
                _remote(srcs[a].at[q], dsts[a].at[q], send.at[4 * a + q], recv.at[4 * a + q], (x, y, 1 - cc)).wait_recv()
        for a in range(n):
            for q in range(4):
                _remote(srcs[a].at[q], dsts[a].at[q], send.at[4 * a + q], recv.at[4 * a + q], (x, y, 1 - cc)).wait_send()

    return _Plan(start, wait, 4 * n, 0)


def _same_shapes(arrs):
    return [jax.ShapeDtypeStruct(a.shape, a.dtype) for a in arrs]


def _halved_shapes(parts):
    return [jax.ShapeDtypeStruct((4,) + p.shape[1:], p.dtype) for p in parts]


def _run_plan(plan, srcs, out_shapes, *, name):
    n_in, n_out = len(srcs), len(out_shapes)

    def body(*refs):
        h_in, h_out, sems = refs[:n_in], refs[n_in:n_in + n_out], refs[n_in + n_out:]
        plan.start(h_in, h_out, *sems)
        plan.wait(h_in, h_out, *sems)

    return pl.pallas_call(body, name=name, in_specs=[ANY] * n_in, out_specs=[ANY] * n_out, out_shape=list(out_shapes),
                          input_output_aliases={a: a for a in range(n_in)} if plan.in_place else {},
                          scratch_shapes=plan.sems())(*srcs)


SEM = pl.BlockSpec(memory_space=pltpu.SEMAPHORE)
HBM = pl.BlockSpec(memory_space=pltpu.HBM)
SIDE_EFFECT = pltpu.CompilerParams(has_side_effects=pltpu.SideEffectType.DATAFLOW_SIDE_EFFECTING)


def _gather_chips_start(blocks, *, name):
    n = len(blocks)
    plan = _gather_chips_plan(n)
    lands = [lax.empty((N_DEV,) + b.shape, b.dtype) for b in blocks]

    def body(*refs):
        srcs, sems, lands_out, token = refs[:n], refs[2 * n:2 * n + 3], refs[3 * n + 3:4 * n + 3], refs[4 * n + 3]
        plan.start(srcs, lands_out, *sems)
        token[...] = jnp.zeros_like(token)

    out_shape = ([s for s in plan.sems()] + [pltpu.HBM(b.shape, b.dtype) for b in blocks]
                 + [pltpu.HBM(l.shape, l.dtype) for l in lands] + [jax.ShapeDtypeStruct((8, 128), F32)])
    res = pl.pallas_call(
        body, name=name, in_specs=[HBM] * (2 * n), out_specs=[SEM] * 3 + [HBM] * (2 * n) + [pl.BlockSpec(memory_space=pltpu.VMEM)],
        out_shape=out_shape, input_output_aliases={a: 3 + a for a in range(2 * n)}, compiler_params=SIDE_EFFECT,
    )(*[pltpu.with_memory_space_constraint(a, pltpu.HBM) for a in list(blocks) + lands])
    return res[:3], res[3:3 + n], res[3 + n:3 + 2 * n], res[3 + 2 * n]


def _gather_chips_wait(sems, blocks, lands, after, *, name):
    n = len(blocks)
    plan = _gather_chips_plan(n)

    def body(*refs):
        plan.wait(refs[:n], refs[n:2 * n], *refs[2 * n:2 * n + 3])

    res = pl.pallas_call(
        body, name=name, in_specs=[HBM] * (2 * n) + [SEM] * 3 + [ANY], out_specs=[HBM] * (2 * n),
        out_shape=[pltpu.HBM(a.shape, a.dtype) for a in list(blocks) + list(lands)],
        input_output_aliases={a: a for a in range(2 * n)}, compiler_params=SIDE_EFFECT,
    )(*blocks, *lands, *sems, after)
    return list(res[n:])


def _sum_sibling(p, q, core, *, name):
    _, r, c = p.shape

    def body(core_ref, p_ref, q_ref, o_ref):
        o_ref[...] = (p_ref[...].astype(F32) + q_ref[...].astype(F32)).astype(BF16)

    grid_spec = pltpu.PrefetchScalarGridSpec(
        num_scalar_prefetch=1, grid=(4,),
        in_specs=[pl.BlockSpec((1, r, c), lambda ch, core_ref: (2 * ch + core_ref[0], 0, 0)),
                  pl.BlockSpec((1, r, c), lambda ch, core_ref: (ch, 0, 0))],
        out_specs=pl.BlockSpec((1, r, c), lambda ch, core_ref: (ch, 0, 0)))
    return pl.pallas_call(
        body, name=name, grid_spec=grid_spec, out_shape=jax.ShapeDtypeStruct((4, r, c), BF16),
        compiler_params=_params("parallel"),
    )(core, p, q)


def _sum_chips(s1, r2, chip, *, name):
    _, r, c = s1.shape

    def body(chip_ref, s_ref, r_ref, o_ref):
        acc = s_ref[0].astype(F32)
        for k in range(3):
            acc = acc + r_ref[k].astype(F32)
        o_ref[...] = acc

    grid_spec = pltpu.PrefetchScalarGridSpec(
        num_scalar_prefetch=1, grid=(1,),
        in_specs=[pl.BlockSpec((1, r, c), lambda i, chip_ref: (chip_ref[0], 0, 0)),
                  pl.BlockSpec((3, r, c), lambda i, chip_ref: (0, 0, 0))],
        out_specs=pl.BlockSpec((r, c), lambda i, chip_ref: (0, 0)))
    return pl.pallas_call(
        body, name=name, grid_spec=grid_spec, out_shape=jax.ShapeDtypeStruct((r, c), F32),
        compiler_params=_params("arbitrary"),
    )(chip, s1, r2)


def _sum_adamw(s1, r2, chip, w, m, v, *, name):
    _, r, c = s1.shape
    c1 = 1.0 / (1.0 - ADAM_B1 ** ADAM_STEP)
    c2 = 1.0 / (1.0 - ADAM_B2 ** ADAM_STEP)

    def body(chip_ref, s_ref, r_ref, w_ref, m_ref, v_ref, g_ref, d_ref, nm_ref, nv_ref):
        gv = s_ref[0].astype(F32)
        for k in range(3):
            gv = gv + r_ref[k].astype(F32)
        g_ref[...] = gv
        nm = ADAM_B1 * m_ref[...] + (1.0 - ADAM_B1) * gv
        nv = ADAM_B2 * v_ref[...] + (1.0 - ADAM_B2) * (gv * gv)
        nm_ref[...] = nm
        nv_ref[...] = nv
        d_ref[...] = -ADAM_LR * ((nm * c1) / (jnp.sqrt(nv * c2) + ADAM_EPS) + ADAM_WD * w_ref[...])

    flat = pl.BlockSpec((r, c), lambda i, chip_ref: (0, 0))
    grid_spec = pltpu.PrefetchScalarGridSpec(
        num_scalar_prefetch=1, grid=(1,),
        in_specs=[pl.BlockSpec((1, r, c), lambda i, chip_ref: (chip_ref[0], 0, 0)),
                  pl.BlockSpec((3, r, c), lambda i, chip_ref: (0, 0, 0)), flat, flat, flat],
        out_specs=[flat] * 4)
    return pl.pallas_call(
        body, name=name, grid_spec=grid_spec, out_shape=[jax.ShapeDtypeStruct((r, c), F32)] * 4,
        compiler_params=_params("arbitrary"),
    )(chip, s1, r2, w, m, v)


def _small_exchange(v, *, reduce, name):
    r, c = v.shape

    def body(x_ref, o_ref, *rest):
        if reduce:
            buf_ref, send_sems, recv_sems = rest
        else:
            buf_ref = o_ref
            send_sems, recv_sems = rest
        x, y, cc = _place()
        me = 4 * x + 2 * y + cc

        def peer(k):
            return ((1 - x) if k & 4 else x, (1 - y) if k & 2 else y, (1 - cc) if k & 1 else cc)

        buf_ref[me] = x_ref[...]
        sends = []
        for k in range(1, N_DEV):
            cp = pltpu.make_async_remote_copy(src_ref=x_ref, dst_ref=buf_ref.at[me], send_sem=send_sems.at[k - 1],
                                              recv_sem=recv_sems.at[k - 1], device_id=peer(k), device_id_type=MESH)
            cp.start()
            sends.append(cp)
        for k in range(1, N_DEV):
            px, py, pc = peer(k)
            pltpu.make_async_remote_copy(src_ref=x_ref, dst_ref=buf_ref.at[4 * px + 2 * py + pc], send_sem=send_sems.at[k - 1],
                                         recv_sem=recv_sems.at[k - 1], device_id=peer(k), device_id_type=MESH).wait_recv()
        for cp in sends:
            cp.wait_send()
        if reduce:
            acc = buf_ref[0]
            for s in range(1, N_DEV):
                acc = acc + buf_ref[s]
            o_ref[...] = acc

    vm = pl.BlockSpec(memory_space=pltpu.VMEM)
    sems = [pltpu.SemaphoreType.DMA((N_DEV - 1,)), pltpu.SemaphoreType.DMA((N_DEV - 1,))]
    if reduce:
        out_shape, scratch = jax.ShapeDtypeStruct((r, c), F32), [pltpu.VMEM((N_DEV, r, c), F32)] + sems
    else:
        out_shape, scratch = jax.ShapeDtypeStruct((N_DEV, r, c), F32), sems
    return pl.pallas_call(body, name=name, in_specs=[vm], out_specs=vm, out_shape=out_shape, scratch_shapes=scratch)(v)


def _rows(a):
    return a.reshape(-1, D)


def _pad_cols(a, to):
    return jnp.pad(a, ((0, 0), (0, to - a.shape[1])))


def _pack_weights(w):
    parts = {
        "w_inT": jnp.pad(w["w_in"].T, ((0, IN_SHARD_PAD - IN_SHARD), (0, 0))),
        "w_uq": _rows(_head_cols(w["w_uq"])), "w_uk": _rows(_head_cols(w["w_uk"])),
        "w_uv": _rows(_pad_cols(w["w_uv"], HEAD_PAD)), "w_pa": _rows(w["w_proj_attn"]),
        "w_pc": w["w_proj_conv"], "w_out": w["w_out"],
    }
    return [jnp.concatenate([parts[n].astype(BF16) for n, _ in group], axis=0) for group in PACK]


def _cols_from_shards(gs, name, rows):
    idx, off, r = PACK_OFF[name]
    return gs[idx][:, off:off + r].reshape(N_DEV, rows, HEAD_PAD).transpose(1, 0, 2).reshape(rows, N_DEV * HEAD_PAD)


def _rows_from_shards(gs, name, keep=None):
    idx, off, r = PACK_OFF[name]
    keep = r if keep is None else keep
    return gs[idx][:, off:off + keep].reshape(N_DEV * keep, D)


def _rope_placement():
    i = lax.broadcasted_iota(jnp.int32, (HEAD_PAD, D), 0)
    j = lax.broadcasted_iota(jnp.int32, (HEAD_PAD, D), 1)
    lane = jnp.where(i < ROPE_HALF, 32 + i, 96 + i - ROPE_HALF)
    return ((i < 2 * ROPE_HALF) & (j % HEAD_PAD == lane)).astype(BF16)


def _unpack_in(g_in):
    w_inT = _rows_from_shards([g_in, None], "w_inT", IN_SHARD)
    lat_rows = Q_LORA + KV_LORA + 2 * ROPE_HALF
    conv = w_inT[lat_rows:lat_rows + CONV_COLS].reshape(3, D // CONV_CB, CONV_CB, D).transpose(1, 0, 2, 3).reshape(CONV_COLS, D)
    return {"latT": jnp.pad(w_inT[:lat_rows], ((0, LAT_PAD - lat_rows), (0, 0))), "convT": conv,
            "gateT": w_inT[lat_rows + CONV_COLS:]}


def _unpack_misc(g_misc):
    g = [None, g_misc]
    wpa = _cols_from_shards(g, "w_pa", 512).reshape(N_HEADS, NOPE, D)
    return {
        "wq": _cols_from_shards(g, "w_uq", Q_LORA),
        "wk": jnp.concatenate([_cols_from_shards(g, "w_uk", KV_LORA), _rope_placement()], axis=0),
        "wv": _cols_from_shards(g, "w_uv", KV_LORA),
        "wpa": jnp.pad(wpa, ((0, 0), (0, HEAD_PAD - NOPE), (0, 0))).reshape(D, D),
        "wpc": _rows_from_shards(g, "w_pc"), "wout": _rows_from_shards(g, "w_out"),
    }


def _shards_from_cols(a):
    rows = a.shape[0]
    return a.reshape(rows, N_DEV, HEAD_PAD).transpose(1, 0, 2).reshape(N_DEV, rows * HEAD_PAD // D, D)


def _pack_grads(gw):
    lat_rows = Q_LORA + KV_LORA + 2 * ROPE_HALF
    conv = gw["convT"].reshape(D // CONV_CB, 3, CONV_CB, D).transpose(1, 0, 2, 3).reshape(CONV_COLS, D)
    w_inT = jnp.concatenate([gw["latT"][:lat_rows], conv, gw["gateT"]], axis=0).reshape(N_DEV, IN_SHARD, D)
    wpa = gw["wpa"].reshape(N_HEADS, HEAD_PAD, D)[:, :NOPE].reshape(N_HEADS * NOPE, D)
    parts = {}
    parts.update({
        "w_inT": jnp.pad(w_inT, ((0, 0), (0, IN_SHARD_PAD - IN_SHARD), (0, 0))),
        "w_uq": _shards_from_cols(gw["wq"]), "w_uk": _shards_from_cols(gw["wk"][:KV_LORA]),
        "w_uv": _shards_from_cols(gw["wv"][:KV_LORA]), "w_pa": _shards_from_cols(wpa),
        "w_pc": gw["wpc"].reshape(N_DEV, D // N_DEV, D), "w_out": gw["wout"].reshape(N_DEV, D // N_DEV, D),
    })
    return [jnp.concatenate([parts[n] for n, _ in group], axis=1) for group in PACK]


def _unpack_grads(mines):
    def seg(name, keep=None):
        idx, off, r = PACK_OFF[name]
        return mines[idx][off:off + (r if keep is None else keep)]

    return {
        "w_in": seg("w_inT", IN_SHARD).T,
        "w_uq": _head_cols_inv(seg("w_uq").reshape(Q_LORA, HEAD_PAD), QK_DIM),
        "w_uk": _head_cols_inv(seg("w_uk").reshape(KV_LORA, HEAD_PAD), NOPE),
        "w_uv": seg("w_uv").reshape(KV_LORA, HEAD_PAD)[:, :NOPE],
        "w_proj_attn": seg("w_pa").reshape(512, HEAD_PAD),
        "w_proj_conv": seg("w_pc"), "w_out": seg("w_out"),
    }


def _rope_tables(positions):
    lane = jnp.arange(HEAD_PAD)
    idx = jnp.where((lane >= 32) & (lane < 48), lane - 32, jnp.where((lane >= 96) & (lane < 112), lane - 96, -1))
    inv_freq = jnp.where(idx >= 0, 1.0 / (ROPE_THETA ** (idx.astype(F32) / ROPE_HALF)), 0.0)
    ang = positions.reshape(-1).astype(F32)[:, None] * inv_freq
    return jnp.cos(ang), jnp.sin(ang) * jnp.where(lane < HEAD_PAD // 2, -1.0, 1.0)


def _local_step(x, positions, target, conv_w, small, ex):
    n_seq, seq, d = x.shape
    t = n_seq * seq
    x0 = x.reshape(t, d)
    tgt = target.reshape(t, d)
    rc, rs = _rope_tables(positions)
    ghq = _head_cols(small["q_head_norm"])
    ghk = _head_cols(small["k_head_norm"])
    TM, HC, TQ = 1024, 256, 1024

    def mm(*args, hosted=None, **kw):
        res = _mm(*args, hosted=hosted, **kw)
        return res if hosted is not None else (res, None)

    def wgrad(a, b, name, tm=None, hosted=None):
        tm = tm or a.shape[1]
        return mm(a, b, mode="tn", out_dtype=BF16, tm=tm, tn=b.shape[1], tk=2048 if tm <= D else 1024, name=name, hosted=hosted)

    f1g, f1u, f1d = ex.gather_finish(ex.witness() + rc[:8] + conv_w[:1, :HEAD_PAD])
    (x1, h1, a1, b1), got = _ffn_fwd(x0, small["ffn1_norm"], f1g, f1u, f1d, tm=512, hc=DFF // 2, name="ffn1_fwd",
                                     hosted=ex.gather_chips("mix_in"))
    hm, got = _rms_fwd(x1, small["mix_norm"], tm=TM, name="mix_norm_fwd", hosted=ex.gather_sibling(got))
    W = ex.mix_in_weights(got)
    (lat, conv3, gl), got = _proj_fwd(hm, W["latT"], W["convT"], W["gateT"], tm=512, name="proj_fwd",
                                      hosted=ex.gather_chips("mix_misc"))
    p, got = _conv_fwd(conv3, conv_w, n_seq=n_seq, seq=seq, name="conv_fwd", hosted=ex.gather_sibling(got))
    W.update(ex.mix_misc_weights(got))
    q, k, v, qn, ckv = _mla_prep_fwd(lat, small["q_a_norm"], small["kv_a_norm"], ghq, ghk, W["wq"], W["wk"], W["wv"], rc, rs,
                                     tm=512, name="mla_prep_fwd")
    (o, lse), got = _flash_fwd(q, k, v, n_seq=n_seq, seq=seq, tq=TQ, name="attn_fwd", hosted=ex.gather_chips("ffn2"))
    (x2, merged, ya, yb), got = _merge_fwd(o, p, gl, small["gate_bias"], x1, W["wpa"], W["wpc"], W["wout"], tm=512, name="merge_fwd",
                                           hosted=ex.gather_sibling(got))
    f2g, f2u, f2d = ex.ffn_weights(got)
    (dy, h2, a2, b2, loss_row), _ = _ffn_fwd(x2, small["ffn2_norm"], f2g, f2u, f2d, tm=512, hc=DFF // 2, name="ffn2_fwd", target=tgt)

    gw, gs = {}, {}
    (da2, db2, *ffn2_grads), _ = _ffn_grads(dy, h2, a2, b2, f2d, tm=TM, hc=HC, name="ffn2_grads")
    (dx2, gs["ffn2_norm"]), _ = _ffn_up_bwd(da2, db2, f2g, f2u, x2, small["ffn2_norm"], dy, tm=512, name="ffn2_up_bwd")

    (dx2b, dya, dyb, dgl, do, dp, gs["gate_bias"]), got = _merge_bwd(
        dx2, ya, yb, gl, small["gate_bias"], W["wpa"], W["wpc"], W["wout"], tm=512, name="merge_bwd",
        hosted=ex.scatter_sibling("ffn2", ffn2_grads))
    ex.scatter_sibling_done("ffn2", got)
    gw["wout"] = wgrad(merged, dx2b, "dw_out")[0]
    gw["wpa"] = wgrad(o, dya, "dw_pa")[0]
    gw["wpc"] = wgrad(p, dyb, "dw_pc")[0]
    dconv3, dconv_w = _conv_bwd(dp, conv3, conv_w, n_seq=n_seq, seq=seq, name="conv_bwd")
    (dq, dk, dv), got = _flash_bwd(q, k, v, o, lse, do, n_seq=n_seq, seq=seq, tq=TQ, name="attn_bwd",
                                   hosted=ex.scatter_chips("ffn2"))
    ex.scatter_chips_done("ffn2", got)
    dlat, dqp, dkp, gs["q_a_norm"], gs["kv_a_norm"], dghq, dghk = _mla_prep_bwd(
        dq, dk, dv, lat, qn, ckv, small["q_a_norm"], small["kv_a_norm"], ghq, ghk, W["wq"], W["wk"], W["wv"], rc, rs,
        tm=512, name="mla_prep_bwd")
    gs["q_head_norm"], gs["k_head_norm"] = _head_cols_inv(dghq, QK_DIM), _head_cols_inv(dghk, QK_DIM)
    gw["wq"] = wgrad(qn, dqp, "dw_uq")[0]
    gw["wk"] = wgrad(ckv, dkp, "dw_uk")[0]
    gw["wv"] = wgrad(ckv, dv, "dw_uv")[0]
    gw["convT"] = wgrad(dconv3, hm, "dw_conv", tm=CONV_COLS // 2)[0]
    gw["gateT"] = wgrad(dgl, hm, "dw_gate")[0]
    gw["latT"] = wgrad(dlat, hm, "dw_lat")[0]
    ex.scatter_sibling_now("mix", gw)
    (dx1, gs["mix_norm"]), got = _proj_bwd(dlat, dconv3, dgl, W["latT"], W["convT"], W["gateT"], x1, small["mix_norm"], dx2,
                                           tm=512, name="proj_bwd", hosted=ex.scatter_chips("mix_in"))
    ex.scatter_chips_done("mix_in", got)

    (da1, db1, *ffn1_grads), got = _ffn_grads(dx1, h1, a1, b1, f1d, tm=TM, hc=HC, name="ffn1_grads",
                                              hosted=ex.scatter_chips("mix_misc"))
    ex.scatter_chips_done("mix_misc", got)
    ex.scatter_sibling_now("ffn1", ffn1_grads)
    (dx0, gs["ffn1_norm"]), got = _ffn_up_bwd(da1, db1, f1g, f1u, x0, small["ffn1_norm"], dx1, tm=512, name="ffn1_up_bwd",
                                              hosted=ex.scatter_chips("ffn1"))
    ex.scatter_chips_done("ffn1", got)
    return loss_row, dx0.reshape(n_seq, seq, d), dconv_w, gs


class _MeshExchange:
    def __init__(self, w, core, chip):
        self.w, self.core, self.chip = w, core, chip
        self.partial, self.received, self._cache = {}, {}, {}

    def _blocks(self, group):
        w = self.w
        if group not in self._cache:
            if group.startswith("ffn"):
                self._cache[group] = [w[group + "_w_gate"].T.astype(BF16), w[group + "_w_up"].T.astype(BF16),
                                      w[group + "_w_down"].astype(BF16)]
            else:
                self._cache["mix_in"], self._cache["mix_misc"] = [[b] for b in _pack_weights(w)]
        return self._cache[group]

    def gather_chips(self, *groups):
        blocks = [b for group in groups for b in self._blocks(group)]
        return _gather_chips_plan(len(blocks)), blocks, _gather_shapes(blocks)

    def gather_sibling(self, got):
        half = list(got)
        return _gather_sibling_plan(len(half)), half, _same_shapes(half)

    def gather_start(self, group):
        sems, blocks, lands, token = _gather_chips_start(self._blocks(group), name="gather_%s_start" % group)
        self._started = (group, sems, blocks, lands)
        return token[0, 0]

    def gather_finish(self, after):
        group, sems, blocks, lands = self._started
        half = _gather_chips_wait(sems, blocks, lands, after, name="gather_%s_wait" % group)
        return self.ffn_weights(_run_plan(_gather_sibling_plan(len(half)), half, _same_shapes(half), name="gather_%s_sibling" % group))

    def witness(self):
        parts = [b[:8, :128].astype(F32) for g in ("mix_in", "mix_misc", "ffn2") for b in self._blocks(g)]
        return functools.reduce(jnp.add, parts)

    def ffn_weights(self, got):
        return [a.reshape(DFF, D) for a in got]

    def mix_in_weights(self, got):
        return _unpack_in(got[0])

    def mix_misc_weights(self, got):
        return _unpack_misc(got[0])

    def _parts(self, group, grads):
        if group == "mix":
            return _pack_grads(grads), ["mix_in", "mix_misc"]
        parts = [g.reshape(N_DEV, -1, D) for g in grads]
        return parts, ([group] if len(parts) == 1 else None)

    def scatter_sibling(self, group, grads):
        self._sent, self._names = self._parts(group, grads)
        return _scatter_sibling_plan(len(self._sent)), self._sent, _halved_shapes(self._sent)

    def scatter_sibling_done(self, group, got):
        sums = [_sum_sibling(p, q, self.core, name="sum_%s_sibling_%d" % (group, i)) for i, (p, q) in enumerate(zip(self._sent, got))]
        if self._names is None:
            self.partial[group] = sums
        else:
            for n, s in zip(self._names, sums):
                self.partial[n] = [s]

    def scatter_sibling_now(self, group, grads):
        plan, parts, shapes = self.scatter_sibling(group, grads)
        self.scatter_sibling_done(group, _run_plan(plan, parts, shapes, name="scatter_%s_sibling" % group))

    def scatter_chips(self, group):
        s1 = self.partial[group]
        return _scatter_chips_plan(len(s1)), s1, _scatter_shapes(s1)

    def scatter_chips_done(self, group, got):
        self.received[group] = list(got)


SMALL_NAMES = ("ffn1_norm", "mix_norm", "gate_bias", "q_a_norm", "kv_a_norm", "q_head_norm", "k_head_norm", "ffn2_norm")
SMALL_SLOTS = {"ffn1_norm": 1024, "mix_norm": 1024, "gate_bias": 2048, "q_a_norm": 384, "kv_a_norm": 256, "q_head_norm": 128,
               "k_head_norm": 128, "ffn2_norm": 1024, "conv_w": 3072, "loss": 128}
COLUMN_MAJOR = ("w_in", "w_uq", "w_uk", "w_uv")
WEIGHT_NAMES = ("ffn1_norm", "ffn1_w_gate", "ffn1_w_up", "ffn1_w_down", "mix_norm", "w_in", "gate_bias", "q_a_norm", "w_uq",
                "kv_a_norm", "w_uk", "w_uv", "q_head_norm", "k_head_norm", "w_proj_attn", "conv_w", "w_proj_conv", "w_out",
                "ffn2_norm", "ffn2_w_gate", "ffn2_w_up", "ffn2_w_down")


def _step(x, positions, loss_target, w, m, v):
    xi, yi, ci = _place()
    core = ci.astype(jnp.int32).reshape(1)
    chip = (2 * xi + yi).astype(jnp.int32).reshape(1)
    me = 4 * xi + 2 * yi + ci

    ex = _MeshExchange(w, core, chip)
    zero = ex.gather_start("ffn1")
    ex.w = {n: (a if n.startswith("ffn1") else a + zero) for n, a in w.items()}
    cw_all = _small_exchange(jnp.pad(ex.w["conv_w"], ((0, 5), (0, 0))), reduce=False, name="gather_conv_w")
    conv_w = cw_all[:, :3].transpose(1, 0, 2).reshape(3, D)
    small = {n: w[n].reshape(1, -1) for n in SMALL_NAMES}

    loss_row, grad_x, dconv_w, gs = _local_step(x, positions + zero.astype(jnp.int32), loss_target, conv_w, small, ex)

    grads, deltas, new_m, new_v = {}, {}, {}, {}
    where = {"ffn1_w_gate": ("ffn1", 0), "ffn1_w_up": ("ffn1", 1), "ffn1_w_down": ("ffn1", 2),
             "ffn2_w_gate": ("ffn2", 0), "ffn2_w_up": ("ffn2", 1), "ffn2_w_down": ("ffn2", 2)}
    for n, (group, i) in where.items():
        transposed = not n.endswith("down")
        wv, mv, vv = (a[n].T if transposed else a[n] for a in (w, m, v))
        res = _sum_adamw(ex.partial[group][i], ex.received[group][i], chip, wv, mv, vv, name="adamw_" + n)
        grads[n], deltas[n], new_m[n], new_v[n] = (r.T if transposed else r for r in res)
    grads.update(_unpack_grads([_sum_chips(ex.partial[g][0], ex.received[g][0], chip, name="sum_%s_chips" % g)
                                for g in ("mix_in", "mix_misc")]))

    pieces = [_pad_cols(gs[n], SMALL_SLOTS[n]) for n in SMALL_NAMES] + [dconv_w.reshape(1, 3 * D), loss_row]
    total = _small_exchange(jnp.concatenate(pieces, axis=1).reshape(-1, 128), reduce=True, name="reduce_small").reshape(-1)
    off = 0
    for n in SMALL_NAMES:
        grads[n] = total[off:off + w[n].shape[0]]
        off += SMALL_SLOTS[n]
    conv_full = total[off:off + 3 * D].reshape(3, D)
    grads["conv_w"] = lax.dynamic_slice(conv_full, (0, me * HEAD_PAD), (3, HEAD_PAD))
    loss = total[off + 3 * D]

    for n in WEIGHT_NAMES:
        if n in deltas:
            continue
        shape = w[n].shape
        if n in COLUMN_MAJOR:
            ops = [a.T for a in (w[n], grads[n], m[n], v[n])]
            deltas[n], new_m[n], new_v[n] = (r.T for r in _adamw(*ops, name="adamw_" + n))
            continue
        if len(shape) == 1:
            view = (-1, 128) if shape[0] % 128 == 0 else (1, shape[0])
        else:
            view = shape
        dlt, nm, nv = _adamw(w[n].reshape(view), grads[n].reshape(view), m[n].reshape(view), v[n].reshape(view), name="adamw_" + n)
        deltas[n], new_m[n], new_v[n] = dlt.reshape(shape), nm.reshape(shape), nv.reshape(shape)
    return (loss, grad_x, *[grads[n] for n in WEIGHT_NAMES], *[deltas[n] for n in WEIGHT_NAMES],
            *[new_m[n] for n in WEIGHT_NAMES], *[new_v[n] for n in WEIGHT_NAMES])


def kernel(x, positions, ffn1_norm, ffn1_w_gate, ffn1_w_up, ffn1_w_down, mix_norm, w_in, gate_bias, q_a_norm, w_uq, kv_a_norm, w_uk, w_uv, q_head_norm, k_head_norm, w_proj_attn, conv_w, w_proj_conv, w_out, ffn2_norm, ffn2_w_gate, ffn2_w_up, ffn2_w_down, loss_target, m_ffn1_norm, m_ffn1_w_gate, m_ffn1_w_up, m_ffn1_w_down, m_mix_norm, m_w_in, m_gate_bias, m_q_a_norm, m_w_uq, m_kv_a_norm, m_w_uk, m_w_uv, m_q_head_norm, m_k_head_norm, m_w_proj_attn, m_conv_w, m_w_proj_conv, m_w_out, m_ffn2_norm, m_ffn2_w_gate, m_ffn2_w_up, m_ffn2_w_down, v_ffn1_norm, v_ffn1_w_gate, v_ffn1_w_up, v_ffn1_w_down, v_mix_norm, v_w_in, v_gate_bias, v_q_a_norm, v_w_uq, v_kv_a_norm, v_w_uk, v_w_uv, v_q_head_norm, v_k_head_norm, v_w_proj_attn, v_conv_w, v_w_proj_conv, v_w_out, v_ffn2_norm, v_ffn2_w_gate, v_ffn2_w_up, v_ffn2_w_down):
    given = dict(locals())
    w = {n: given[n] for n in WEIGHT_NAMES}
    m = {n: given["m_" + n] for n in WEIGHT_NAMES}
    v = {n: given["v_" + n] for n in WEIGHT_NAMES}
    return _step(x, positions, loss_target, w, m, v)
```

```python
import functools

import jax
import jax.numpy as jnp
from jax import lax
from jax.experimental import pallas as pl
from jax.experimental.pallas import tpu as pltpu

F32 = jnp.float32
BF16 = jnp.bfloat16
MESH = pl.DeviceIdType.MESH
ANY = pl.BlockSpec(memory_space=pl.ANY)

N_DEV = 8
D = 1024
DFF = 2816
N_HEADS = 8
HEAD_PAD = 128
QK_DIM = 96
NOPE = 64
ROPE_HALF = 16
Q_LORA = 384
KV_LORA = 256
LAT_PAD = 768
CONV_COLS = 3072
GATE_COLS = 2048
IN_DIM = 5792
IN_SHARD = IN_DIM // N_DEV
IN_SHARD_PAD = 736
FF_SHARD = DFF // N_DEV
ROPE_THETA = 10000.0
NORM_EPS = 1e-6
ATTN_SCALE = QK_DIM ** -0.5
NEG = -1e30

ADAM_LR, ADAM_B1, ADAM_B2, ADAM_EPS, ADAM_WD, ADAM_STEP = 0.001, 0.9, 0.999, 1e-08, 0.01, 10

PACK = ((("w_inT", IN_SHARD_PAD),), (("w_uq", 48), ("w_uk", 32), ("w_uv", 32), ("w_pa", 64), ("w_pc", 128), ("w_out", 128)))
PACK_OFF = {}
for _i, _group in enumerate(PACK):
    _o = 0
    for _n, _r in _group:
        PACK_OFF[_n] = (_i, _o, _r)
        _o += _r

VMEM_LIMIT = 56 * 1024 * 1024


def _params(*sem):
    return pltpu.CompilerParams(dimension_semantics=sem if sem else None, vmem_limit_bytes=VMEM_LIMIT)


class _Plan:
    def __init__(self, start, wait, n_remote, n_local, in_place=False):
        self.start, self.wait, self.n_remote, self.n_local, self.in_place = start, wait, n_remote, n_local, in_place

    def sems(self):
        return [pltpu.SemaphoreType.DMA((self.n_remote,)), pltpu.SemaphoreType.DMA((self.n_remote,)),
                pltpu.SemaphoreType.DMA((max(self.n_local, 1),))]


def _call(body, *, name, grid, in_specs, out_specs, out_shape, scratch_shapes, operands, sem, hosted=None):
    if hosted is None:
        outs = pl.pallas_call(body, name=name, grid=grid, in_specs=in_specs, out_specs=out_specs, out_shape=out_shape,
                              scratch_shapes=scratch_shapes, compiler_params=_params(*sem))(*operands)
        return outs, None
    plan, srcs, h_shapes = hosted
    n_in, n_out, n_scr, nh_in, nh_out = len(in_specs), len(out_specs), len(scratch_shapes), len(srcs), len(h_shapes)
    aliases = {n_in + a: n_out + a for a in range(nh_in)} if plan.in_place else {}

    def full_body(*refs):
        ins, refs = refs[:n_in], refs[n_in:]
        h_in, refs = refs[:nh_in], refs[nh_in:]
        outs, refs = refs[:n_out], refs[n_out:]
        h_out, refs = refs[:nh_out], refs[nh_out:]
        scr, sems = refs[:n_scr], refs[n_scr:]
        ids = [pl.program_id(ax) for ax in range(len(grid))]
        first = functools.reduce(jnp.logical_and, [i == 0 for i in ids])
        last = functools.reduce(jnp.logical_and, [i == g - 1 for i, g in zip(ids, grid)])

        @pl.when(first)
        def _():
            plan.start(h_in, h_out, *sems)

        body(*ins, *outs, *scr)

        @pl.when(last)
        def _():
            plan.wait(h_in, h_out, *sems)

    res = pl.pallas_call(
        full_body, name=name, grid=grid, in_specs=list(in_specs) + [ANY] * nh_in, out_specs=list(out_specs) + [ANY] * nh_out,
        out_shape=list(out_shape) + list(h_shapes), scratch_shapes=list(scratch_shapes) + plan.sems(),
        input_output_aliases=aliases, compiler_params=_params(*(["arbitrary"] * len(grid))),
    )(*operands, *srcs)
    return res[:n_out], res[n_out:]


def _dot_nn(a, b):
    return lax.dot_general(a, b, (((1,), (0,)), ((), ())), preferred_element_type=F32)


def _dot_nt(a, b):
    return lax.dot_general(a, b, (((1,), (1,)), ((), ())), preferred_element_type=F32)


def _dot_tn(a, b):
    return lax.dot_general(a, b, (((0,), (0,)), ((), ())), preferred_element_type=F32)


def _sigmoid(x):
    return 0.5 * jnp.tanh(0.5 * x) + 0.5


def _rms_stats(x):
    r = lax.rsqrt(jnp.mean(x * x, axis=-1, keepdims=True) + NORM_EPS)
    return x * r, r


ROWS_WIDE = 16
MM_ROWS = 256


def _rms_bwd(dy, xhat, r, g):
    dg = jnp.sum(dy * xhat, axis=0, keepdims=True)
    dxh = dy * g
    dx = r * (dxh - xhat * jnp.mean(dxh * xhat, axis=-1, keepdims=True))
    return dx, dg


def _mm(a, b, *, mode, out_dtype, tm, tn, tk, name, add=None, scale=1.0, hosted=None):
    if mode == "nn":
        (m, k), (_, n) = a.shape, b.shape
    elif mode == "nt":
        (m, k), (n, _) = a.shape, b.shape
    else:
        (k, m), (_, n) = a.shape, b.shape
    assert m % tm == 0 and n % tn == 0 and k % tk == 0, (name, m, n, k, tm, tn, tk)
    nk = k // tk
    dot = {"nn": _dot_nn, "nt": _dot_nt, "tn": _dot_tn}[mode]
    a_spec = pl.BlockSpec((tk, tm), lambda i, j, kk: (kk, i)) if mode == "tn" else pl.BlockSpec((tm, tk), lambda i, j, kk: (i, kk))
    b_spec = pl.BlockSpec((tn, tk), lambda i, j, kk: (j, kk)) if mode == "nt" else pl.BlockSpec((tk, tn), lambda i, j, kk: (kk, j))
    o_spec = pl.BlockSpec((tm, tn), lambda i, j, kk: (i, j))
    has_add = add is not None

    def finish(prod, c_ref, o_ref):
        if scale != 1.0:
            prod = prod * scale
        o_ref[...] = ((c_ref[...] + prod) if has_add else prod).astype(out_dtype)

    def body(*refs):
        a_ref, b_ref = refs[:2]
        c_ref = refs[2] if has_add else None
        o_ref = refs[3] if has_add else refs[2]
        if nk == 1:
            finish(dot(a_ref[...], b_ref[...]), c_ref, o_ref)
            return
        acc_ref = refs[-1]
        kk = pl.program_id(2)

        @pl.when(kk == 0)
        def _():
            acc_ref[...] = jnp.zeros_like(acc_ref)

        acc_ref[...] += dot(a_ref[...], b_ref[...])

        @pl.when(kk == nk - 1)
        def _():
            finish(acc_ref[...], c_ref, o_ref)

    operands = (a, b, add) if has_add else (a, b)
    in_specs = [a_spec, b_spec] + ([o_spec] if has_add else [])
    (out,), got = _call(
        body, name=name, grid=(m // tm, n // tn, nk), in_specs=in_specs, out_specs=[o_spec],
        out_shape=[jax.ShapeDtypeStruct((m, n), out_dtype)], scratch_shapes=[pltpu.VMEM((tm, tn), F32)] if nk > 1 else [],
        operands=operands, sem=("parallel", "parallel", "arbitrary"), hosted=hosted)
    return out if hosted is None else (out, got)


def _rms_fwd(x, g, *, tm, name, hosted=None):
    t, d = x.shape

    def body(x_ref, g_ref, h_ref):
        xhat, _ = _rms_stats(x_ref[...])
        h_ref[...] = (xhat * g_ref[...]).astype(BF16)

    (h,), got = _call(
        body, name=name, grid=(t // tm,),
        in_specs=[pl.BlockSpec((tm, d), lambda i: (i, 0)), pl.BlockSpec((1, d), lambda i: (0, 0))],
        out_specs=[pl.BlockSpec((tm, d), lambda i: (i, 0))], out_shape=[jax.ShapeDtypeStruct((t, d), BF16)], scratch_shapes=[],
        operands=(x, g), sem=("parallel",), hosted=hosted)
    return h, got


def _ffn_fwd(x, g, wgT, wuT, wd, *, tm, hc, name, hosted=None, target=None):
    t, d = x.shape
    nj = DFF // hc
    with_loss = target is not None

    def body(*refs):
        x_ref, g_ref, wg_ref, wu_ref, wd_ref = refs[:5]
        t_ref = refs[5] if with_loss else None
        xo_ref, h_ref, a_ref, b_ref = refs[5 + with_loss:9 + with_loss]
        loss_ref = refs[9 + with_loss] if with_loss else None
        acc_ref = refs[-1]
        i, j = pl.program_id(0), pl.program_id(1)

        @pl.when(j == 0)
        def _():
            xhat, _ = _rms_stats(x_ref[...])
            h_ref[...] = (xhat * g_ref[...]).astype(BF16)
            acc_ref[...] = jnp.zeros_like(acc_ref)

        h = h_ref[...]
        a = _dot_nt(h, wg_ref[...])
        b = _dot_nt(h, wu_ref[...])
        a_ref[...] = a.astype(BF16)
        b_ref[...] = b.astype(BF16)
        s = (a * _sigmoid(a) * b).astype(BF16)
        acc_ref[...] += _dot_nn(s, wd_ref[...])

        if with_loss:
            @pl.when((i == 0) & (j == 0))
            def _():
                loss_ref[...] = jnp.zeros_like(loss_ref)

        @pl.when(j == nj - 1)
        def _():
            y = x_ref[...] + 0.5 * acc_ref[...]
            if with_loss:
                err = y - t_ref[...]
                xo_ref[...] = err * (1.0 / d)
                loss_ref[...] += jnp.sum(jnp.sum(err * err, axis=-1, keepdims=True), axis=0, keepdims=True) * (0.5 / d)
            else:
                xo_ref[...] = y

    row = pl.BlockSpec((tm, d), lambda i, j: (i, 0))
    vec = pl.BlockSpec((1, d), lambda i, j: (0, 0))
    wsp = pl.BlockSpec((hc, d), lambda i, j: (j, 0))
    hid = pl.BlockSpec((tm, hc), lambda i, j: (i, j))
    out_specs = [row, row, hid, hid] + ([pl.BlockSpec((1, 128), lambda i, j: (0, 0))] if with_loss else [])
    out_shape = [jax.ShapeDtypeStruct((t, d), F32), jax.ShapeDtypeStruct((t, d), BF16), jax.ShapeDtypeStruct((t, DFF), BF16),
                 jax.ShapeDtypeStruct((t, DFF), BF16)] + ([jax.ShapeDtypeStruct((1, 128), F32)] if with_loss else [])
    return _call(
        body, name=name, grid=(t // tm, nj), in_specs=[row, vec, wsp, wsp, wsp] + ([row] if with_loss else []),
        out_specs=out_specs, out_shape=out_shape, scratch_shapes=[pltpu.VMEM((tm, d), F32)],
        operands=(x, g, wgT, wuT, wd) + ((target,) if with_loss else ()),
        sem=("arbitrary" if with_loss else "parallel", "arbitrary"), hosted=hosted)


def _ffn_grads(dout, h, a, b, wd, *, tm, hc, name, hosted=None):
    t, d = dout.shape
    ni, nj = t // tm, DFF // hc

    def body(dout_ref, h_ref, a_ref, b_ref, wd_ref, da_ref, db_ref, dwg_ref, dwu_ref, dwd_ref,
             dy_all, h_all, ds_scr, s_scr, acc_g, acc_u, acc_d):
        j, i = pl.program_id(0), pl.program_id(1)
        rows_i = pl.ds(pl.multiple_of(i * tm, tm), tm)

        @pl.when(j == 0)
        def _():
            dy_all[rows_i, :] = (0.5 * dout_ref[...]).astype(BF16)
            h_all[rows_i, :] = h_ref[...]

        @pl.when(i == 0)
        def _():
            acc_g[...] = jnp.zeros_like(acc_g)
            acc_u[...] = jnp.zeros_like(acc_u)
            acc_d[...] = jnp.zeros_like(acc_d)

        def grad_rows(rows):
            ds = ds_scr[rows, :]
            av = a_ref[rows, :].astype(F32)
            bv = b_ref[rows, :].astype(F32)
            sg = _sigmoid(av)
            sl = av * sg
            s_scr[rows, :] = (sl * bv).astype(BF16)
            da_ref[rows, :] = (ds * bv * (sg + sl * (1.0 - sg))).astype(BF16)
            db_ref[rows, :] = (ds * sl).astype(BF16)

        for blk in range(tm // MM_ROWS):
            rs = slice(blk * MM_ROWS, (blk + 1) * MM_ROWS)
            ds_scr[rs, :] = _dot_nt(dy_all[pl.ds(pl.multiple_of(i * tm + blk * MM_ROWS, MM_ROWS), MM_ROWS), :], wd_ref[...])
            for c in range(MM_ROWS // ROWS_WIDE):
                grad_rows(slice(blk * MM_ROWS + c * ROWS_WIDE, blk * MM_ROWS + (c + 1) * ROWS_WIDE))

        dy_i = dy_all[rows_i, :]
        h_i = h_all[rows_i, :]
        acc_d[...] += _dot_tn(s_scr[...], dy_i)
        acc_g[...] += _dot_tn(da_ref[...], h_i)
        acc_u[...] += _dot_tn(db_ref[...], h_i)

        @pl.when(i == ni - 1)
        def _():
            dwg_ref[...] = acc_g[...].astype(BF16)
            dwu_ref[...] = acc_u[...].astype(BF16)
            dwd_ref[...] = acc_d[...].astype(BF16)

    first = pl.BlockSpec((tm, d), lambda j, i: (jnp.where(j == 0, i, 0), 0))
    hid = pl.BlockSpec((tm, hc), lambda j, i: (i, j))
    wsp = pl.BlockSpec((hc, d), lambda j, i: (j, 0))
    hid_shape = jax.ShapeDtypeStruct((t, DFF), BF16)
    w_shape = jax.ShapeDtypeStruct((DFF, d), BF16)
    return _call(
        body, name=name, grid=(nj, ni), in_specs=[first, first, hid, hid, wsp], out_specs=[hid, hid, wsp, wsp, wsp],
        out_shape=[hid_shape, hid_shape, w_shape, w_shape, w_shape],
        scratch_shapes=[pltpu.VMEM((t, d), BF16), pltpu.VMEM((t, d), BF16), pltpu.VMEM((tm, hc), F32), pltpu.VMEM((tm, hc), BF16),
                        pltpu.VMEM((hc, d), F32), pltpu.VMEM((hc, d), F32), pltpu.VMEM((hc, d), F32)],
        operands=(dout, h, a, b, wd), sem=("arbitrary", "arbitrary"), hosted=hosted)


def _proj_fwd(h, latT, convT, gateT, *, tm, name, hosted=None):
    t, d = h.shape

    def body(h_ref, wl_ref, wc_ref, wg_ref, lat_ref, conv_ref, gl_ref):
        hv = h_ref[...]
        lat_ref[...] = _dot_nt(hv, wl_ref[...]).astype(BF16)
        conv_ref[...] = _dot_nt(hv, wc_ref[...]).astype(BF16)
        gl_ref[...] = _dot_nt(hv, wg_ref[...]).astype(BF16)

    def rows(w):
        return pl.BlockSpec((tm, w), lambda i: (i, 0))

    def full(r):
        return pl.BlockSpec((r, d), lambda i: (0, 0))

    return _call(
        body, name=name, grid=(t // tm,), in_specs=[rows(d), full(LAT_PAD), full(CONV_COLS), full(GATE_COLS)],
        out_specs=[rows(LAT_PAD), rows(CONV_COLS), rows(GATE_COLS)],
        out_shape=[jax.ShapeDtypeStruct((t, LAT_PAD), BF16), jax.ShapeDtypeStruct((t, CONV_COLS), BF16),
                   jax.ShapeDtypeStruct((t, GATE_COLS), BF16)],
        scratch_shapes=[], operands=(h, latT, convT, gateT), sem=("parallel",), hosted=hosted)


def _proj_bwd(dlat, dconv3, dgl, latT, convT, gateT, x, g, dres, *, tm, name, hosted=None):
    t, d = x.shape

    def body(dl_ref, dc_ref, dg_ref, wl_ref, wc_ref, wg_ref, x_ref, g_ref, dres_ref, dx_ref, dgain_ref):
        @pl.when(pl.program_id(0) == 0)
        def _():
            dgain_ref[...] = jnp.zeros_like(dgain_ref)

        dh = _dot_nn(dl_ref[...], wl_ref[...]) + _dot_nn(dc_ref[...], wc_ref[...]) + _dot_nn(dg_ref[...], wg_ref[...])
        xhat, r = _rms_stats(x_ref[...])
        dx, dgain = _rms_bwd(dh, xhat, r, g_ref[...])
        dx_ref[...] = dres_ref[...] + dx
        dgain_ref[...] += dgain

    def rows(w):
        return pl.BlockSpec((tm, w), lambda i: (i, 0))

    def full(r):
        return pl.BlockSpec((r, d), lambda i: (0, 0))

    return _call(
        body, name=name, grid=(t // tm,),
        in_specs=[rows(LAT_PAD), rows(CONV_COLS), rows(GATE_COLS), full(LAT_PAD), full(CONV_COLS), full(GATE_COLS), rows(d), full(1), rows(d)],
        out_specs=[rows(d), full(1)], out_shape=[jax.ShapeDtypeStruct((t, d), F32), jax.ShapeDtypeStruct((1, d), F32)],
        scratch_shapes=[], operands=(dlat, dconv3, dgl, latT, convT, gateT, x, g, dres), sem=("arbitrary",), hosted=hosted)


def _ffn_up_bwd(da, db, wgT, wuT, x, g, dout, *, tm, name, hosted=None):
    t, d = x.shape

    def body(da_ref, db_ref, wg_ref, wu_ref, x_ref, g_ref, dout_ref, dx_ref, dg_ref):
        @pl.when(pl.program_id(0) == 0)
        def _():
            dg_ref[...] = jnp.zeros_like(dg_ref)

        dh = _dot_nn(da_ref[...], wg_ref[...]) + _dot_nn(db_ref[...], wu_ref[...])
        xhat, r = _rms_stats(x_ref[...])
        dx, dg = _rms_bwd(dh, xhat, r, g_ref[...])
        dx_ref[...] = dout_ref[...] + dx
        dg_ref[...] += dg

    row = pl.BlockSpec((tm, d), lambda i: (i, 0))
    vec = pl.BlockSpec((1, d), lambda i: (0, 0))
    hid = pl.BlockSpec((tm, DFF), lambda i: (i, 0))
    wsp = pl.BlockSpec((DFF, d), lambda i: (0, 0))
    return _call(
        body, name=name, grid=(t // tm,), in_specs=[hid, hid, wsp, wsp, row, vec, row], out_specs=[row, vec],
        out_shape=[jax.ShapeDtypeStruct((t, d), F32), jax.ShapeDtypeStruct((1, d), F32)], scratch_shapes=[],
        operands=(da, db, wgT, wuT, x, g, dout), sem=("arbitrary",), hosted=hosted)


HEAD_LANES = (slice(0, 32), slice(64, 80), None, slice(32, 64), slice(80, 96), None)


def _head_cols(a):
    def part(sl, width):
        if sl is None or sl.stop > a.shape[1]:
            return jnp.zeros((a.shape[0], width), a.dtype)
        return a[:, sl]

    return jnp.concatenate([part(sl, w) for sl, w in zip(HEAD_LANES, (32, 16, 16, 32, 16, 16))], axis=1)


def _head_cols_inv(a, dims):
    parts = [a[:, 0:32], a[:, 64:96]] + ([a[:, 32:48], a[:, 96:112]] if dims == QK_DIM else [])
    return jnp.concatenate(parts, axis=1)


def _rope_fwd(x, c, s):
    return x * c + pltpu.roll(x, HEAD_PAD // 2, 1) * s


def _rope_bwd(dy, c, s):
    return dy * c + pltpu.roll(dy * s, HEAD_PAD // 2, 1)


def _head_stats(x):
    r = lax.rsqrt(jnp.sum(x * x, axis=-1, keepdims=True) * (1.0 / QK_DIM) + NORM_EPS)
    return x * r, r


def _mla_prep_fwd(lat, gq, gkv, ghq, ghk, wq, wk, wv, rc, rs, *, tm, name):
    t = lat.shape[0]

    def body(lat_ref, gq_ref, gkv_ref, ghq_ref, ghk_ref, wq_ref, wk_ref, wv_ref, c_ref, s_ref,
             q_ref, k_ref, v_ref, qn_ref, ckv_ref):
        lat_v = lat_ref[...]
        qhat, _ = _rms_stats(lat_v[:, :Q_LORA].astype(F32))
        qn = (qhat * gq_ref[...]).astype(BF16)
        khat, _ = _rms_stats(lat_v[:, Q_LORA:Q_LORA + KV_LORA].astype(F32))
        ckv = (khat * gkv_ref[...]).astype(BF16)
        ckv_ext = jnp.concatenate([ckv, lat_v[:, Q_LORA + KV_LORA:]], axis=1)
        qn_ref[...] = qn
        ckv_ref[...] = ckv_ext
        q_pre = _dot_nn(qn, wq_ref[...])
        k_pre = _dot_nn(ckv_ext, wk_ref[...])
        v_ref[...] = _dot_nn(ckv, wv_ref[...]).astype(BF16)
        c, s = c_ref[...], s_ref[...]
        for h in range(N_HEADS):
            hs = slice(h * HEAD_PAD, (h + 1) * HEAD_PAD)
            xq, _ = _head_stats(q_pre[:, hs])
            q_ref[:, hs] = _rope_fwd(xq * ghq_ref[...], c, s).astype(BF16)
            xk, _ = _head_stats(k_pre[:, hs])
            k_ref[:, hs] = _rope_fwd(xk * ghk_ref[...], c, s).astype(BF16)

    def row(w):
        return pl.BlockSpec((tm, w), lambda i: (i, 0))

    def full(r, w):
        return pl.BlockSpec((r, w), lambda i: (0, 0))

    wide = jax.ShapeDtypeStruct((t, D), BF16)
    lat3 = jax.ShapeDtypeStruct((t, Q_LORA), BF16)
    return pl.pallas_call(
        body, name=name, grid=(t // tm,),
        in_specs=[row(LAT_PAD), full(1, Q_LORA), full(1, KV_LORA), full(1, HEAD_PAD), full(1, HEAD_PAD),
                  full(Q_LORA, D), full(Q_LORA, D), full(KV_LORA, D), row(HEAD_PAD), row(HEAD_PAD)],
        out_specs=[row(D), row(D), row(D), row(Q_LORA), row(Q_LORA)],
        out_shape=[wide, wide, wide, lat3, lat3],
        compiler_params=_params("parallel"),
    )(lat, gq, gkv, ghq, ghk, wq, wk, wv, rc, rs)


def _mla_prep_bwd(dq, dk, dv, lat, qn, ckv_ext, gq, gkv, ghq, ghk, wq, wk, wv, rc, rs, *, tm, name):
    t = lat.shape[0]

    def body(dq_ref, dk_ref, dv_ref, lat_ref, qn_ref, ckv_ref, gq_ref, gkv_ref, ghq_ref, ghk_ref, wq_ref, wk_ref, wv_ref,
             c_ref, s_ref, dlat_ref, dqp_ref, dkp_ref, dgq_ref, dgkv_ref, dghq_ref, dghk_ref):
        @pl.when(pl.program_id(0) == 0)
        def _():
            dgq_ref[...] = jnp.zeros_like(dgq_ref)
            dgkv_ref[...] = jnp.zeros_like(dgkv_ref)
            dghq_ref[...] = jnp.zeros_like(dghq_ref)
            dghk_ref[...] = jnp.zeros_like(dghk_ref)

        c, s = c_ref[...], s_ref[...]
        q_pre = _dot_nn(qn_ref[...], wq_ref[...])
        k_pre = _dot_nn(ckv_ref[...], wk_ref[...])

        def heads(pre, dy_ref, gh_ref, dgh_ref, out_ref):
            dgh = jnp.zeros((1, HEAD_PAD), F32)
            for h in range(N_HEADS):
                hs = slice(h * HEAD_PAD, (h + 1) * HEAD_PAD)
                d = _rope_bwd(dy_ref[:, hs].astype(F32), c, s)
                xhat, r = _head_stats(pre[:, hs])
                dgh = dgh + jnp.sum(d * xhat, axis=0, keepdims=True)
                dxh = d * gh_ref[...]
                dx = r * (dxh - xhat * (jnp.sum(dxh * xhat, axis=-1, keepdims=True) * (1.0 / QK_DIM)))
                out_ref[:, hs] = dx.astype(BF16)
            dgh_ref[...] += dgh

        heads(q_pre, dq_ref, ghq_ref, dghq_ref, dqp_ref)
        heads(k_pre, dk_ref, ghk_ref, dghk_ref, dkp_ref)
        dqn = _dot_nt(dqp_ref[...], wq_ref[...])
        dce = _dot_nt(dkp_ref[...], wk_ref[...])
        dckv = dce[:, :KV_LORA] + _dot_nt(dv_ref[...], wv_ref[...])
        lat_v = lat_ref[...]
        qhat, rq = _rms_stats(lat_v[:, :Q_LORA].astype(F32))
        dql, dgq = _rms_bwd(dqn, qhat, rq, gq_ref[...])
        khat, rk = _rms_stats(lat_v[:, Q_LORA:Q_LORA + KV_LORA].astype(F32))
        dkl, dgkv = _rms_bwd(dckv, khat, rk, gkv_ref[...])
        dgq_ref[...] += dgq
        dgkv_ref[...] += dgkv
        dlat_ref[...] = jnp.concatenate([dql, dkl, dce[:, KV_LORA:]], axis=1).astype(BF16)

    def row(w):
        return pl.BlockSpec((tm, w), lambda i: (i, 0))

    def full(r, w):
        return pl.BlockSpec((r, w), lambda i: (0, 0))

    return pl.pallas_call(
        body, name=name, grid=(t // tm,),
        in_specs=[row(D), row(D), row(D), row(LAT_PAD), row(Q_LORA), row(Q_LORA), full(1, Q_LORA), full(1, KV_LORA),
                  full(1, HEAD_PAD), full(1, HEAD_PAD), full(Q_LORA, D), full(Q_LORA, D), full(KV_LORA, D),
                  row(HEAD_PAD), row(HEAD_PAD)],
        out_specs=[row(LAT_PAD), row(D), row(D), full(1, Q_LORA), full(1, KV_LORA), full(1, HEAD_PAD), full(1, HEAD_PAD)],
        out_shape=[jax.ShapeDtypeStruct((t, LAT_PAD), BF16), jax.ShapeDtypeStruct((t, D), BF16), jax.ShapeDtypeStruct((t, D), BF16),
                   jax.ShapeDtypeStruct((1, Q_LORA), F32), jax.ShapeDtypeStruct((1, KV_LORA), F32),
                   jax.ShapeDtypeStruct((1, HEAD_PAD), F32), jax.ShapeDtypeStruct((1, HEAD_PAD), F32)],
        compiler_params=_params("arbitrary"),
    )(dq, dk, dv, lat, qn, ckv_ext, gq, gkv, ghq, ghk, wq, wk, wv, rc, rs)


def _causal_keep(tq):
    r = lax.broadcasted_iota(jnp.int32, (tq, tq), 0)
    c = lax.broadcasted_iota(jnp.int32, (tq, tq), 1)
    return c <= r


def _flash_fwd(q, k, v, *, n_seq, seq, tq, name, hosted=None):
    nq = seq // tq

    def body(q_ref, k_ref, v_ref, o_ref, lse_ref):
        qi = pl.program_id(2)
        qv = q_ref[...]

        def step(j, carry, masked):
            m, l, acc = carry
            kj = k_ref[pl.ds(pl.multiple_of(j * tq, tq), tq), :]
            vj = v_ref[pl.ds(pl.multiple_of(j * tq, tq), tq), :]
            s = _dot_nt(qv, kj) * ATTN_SCALE
            if masked:
                s = jnp.where(_causal_keep(tq), s, NEG)
            m_new = jnp.maximum(m, jnp.max(s, axis=-1, keepdims=True))
            alpha = jnp.exp(m - m_new)
            p = jnp.exp(s - m_new)
            l = alpha * l + jnp.sum(p, axis=-1, keepdims=True)
            acc = alpha * acc + _dot_nn(p.astype(BF16), vj)
            return m_new, l, acc

        init = (jnp.full((tq, 1), NEG, F32), jnp.zeros((tq, 1), F32), jnp.zeros((tq, HEAD_PAD), F32))
        carry = lax.fori_loop(0, qi, lambda j, cr: step(j, cr, False), init)
        m, l, acc = step(qi, carry, True)
        o_ref[...] = (acc / l).astype(BF16)
        lse_ref[...] = jnp.broadcast_to(m + jnp.log(l), (tq, HEAD_PAD))

    qspec = pl.BlockSpec((tq, HEAD_PAD), lambda b, h, i: (b * nq + i, h))
    kspec = pl.BlockSpec((seq, HEAD_PAD), lambda b, h, i: (b, h))
    t = n_seq * seq
    return _call(
        body, name=name, grid=(n_seq, N_HEADS, nq), in_specs=[qspec, kspec, kspec], out_specs=[qspec, qspec],
        out_shape=[jax.ShapeDtypeStruct((t, D), BF16), jax.ShapeDtypeStruct((t, D), F32)], scratch_shapes=[],
        operands=(q, k, v), sem=("parallel", "parallel", "arbitrary"), hosted=hosted)


def _flash_bwd(q, k, v, o, lse, do, *, n_seq, seq, tq, name, hosted=None):
    nq = seq // tq

    def body(q_ref, k_ref, v_ref, o_ref, lse_ref, do_ref, dq_ref, dk_ref, dv_ref, dk_acc, dv_acc):
        j = pl.program_id(2)

        @pl.when(j == 0)
        def _():
            dq_ref[...] = jnp.zeros_like(dq_ref)

        dk_acc[...] = jnp.zeros_like(dk_acc)
        dv_acc[...] = jnp.zeros_like(dv_acc)
        kv = k_ref[...]
        vv = v_ref[...]

        def step(i, masked):
            rows = pl.ds(pl.multiple_of(i * tq, tq), tq)
            qi = q_ref[rows, :]
            doi = do_ref[rows, :]
            delta = jnp.sum(doi.astype(F32) * o_ref[rows, :].astype(F32), axis=-1, keepdims=True)
            s = _dot_nt(qi, kv) * ATTN_SCALE
            p = jnp.exp(s - lse_ref[rows, :][:, :1])
            if masked:
                p = jnp.where(_causal_keep(tq), p, 0.0)
            dv_acc[...] += _dot_tn(p.astype(BF16), doi)
            dp = _dot_nt(doi, vv)
            ds = (p * (dp - delta) * ATTN_SCALE).astype(BF16)
            dk_acc[...] += _dot_tn(ds, qi)
            dq_ref[rows, :] += _dot_nn(ds, kv)

        step(j, True)

        def loop_body(i, carry):
            step(i, False)
            return carry

        lax.fori_loop(j + 1, nq, loop_body, 0)
        dk_ref[...] = dk_acc[...]
        dv_ref[...] = dv_acc[...].astype(BF16)

    full = pl.BlockSpec((seq, HEAD_PAD), lambda b, h, j: (b, h))
    tile = pl.BlockSpec((tq, HEAD_PAD), lambda b, h, j: (b * nq + j, h))
    t = n_seq * seq
    return _call(
        body, name=name, grid=(n_seq, N_HEADS, nq), in_specs=[full, tile, tile, full, full, full],
        out_specs=[full, tile, tile],
        out_shape=[jax.ShapeDtypeStruct((t, D), F32), jax.ShapeDtypeStruct((t, D), F32), jax.ShapeDtypeStruct((t, D), BF16)],
        scratch_shapes=[pltpu.VMEM((tq, HEAD_PAD), F32), pltpu.VMEM((tq, HEAD_PAD), F32)],
        operands=(q, k, v, o, lse, do), sem=("parallel", "parallel", "arbitrary"), hosted=hosted)


CONV_CB = 256


def _shift_down(u, k, row):
    return jnp.where(row >= k, pltpu.roll(u, k, 0), 0.0)


def _shift_up(u, k, row, n):
    return jnp.where(row < n - k, pltpu.roll(u, n - k, 0), 0.0)


def _conv_fwd(conv3, cw, *, n_seq, seq, name, hosted=None):
    def body(c_ref, w_ref, p_ref):
        blk = c_ref[...].astype(F32)
        xc, gb, gc = blk[:, :CONV_CB], blk[:, CONV_CB:2 * CONV_CB], blk[:, 2 * CONV_CB:]
        row = lax.broadcasted_iota(jnp.int32, (seq, CONV_CB), 0)
        u = gc * xc
        z = w_ref[0:1, :] * _shift_down(u, 2, row) + w_ref[1:2, :] * _shift_down(u, 1, row) + w_ref[2:3, :] * u
        p_ref[...] = (gb * z).astype(BF16)

    (p,), got = _call(
        body, name=name, grid=(n_seq, D // CONV_CB),
        in_specs=[pl.BlockSpec((seq, 3 * CONV_CB), lambda b, j: (b, j)), pl.BlockSpec((3, CONV_CB), lambda b, j: (0, j))],
        out_specs=[pl.BlockSpec((seq, CONV_CB), lambda b, j: (b, j))],
        out_shape=[jax.ShapeDtypeStruct((n_seq * seq, D), BF16)], scratch_shapes=[],
        operands=(conv3, cw), sem=("parallel", "parallel"), hosted=hosted)
    return p, got


def _conv_bwd(dp, conv3, cw, *, n_seq, seq, name):
    def body(dp_ref, c_ref, w_ref, dc_ref, dw_ref):
        @pl.when(pl.program_id(1) == 0)
        def _():
            dw_ref[...] = jnp.zeros_like(dw_ref)

        blk = c_ref[...].astype(F32)
        xc, gb, gc = blk[:, :CONV_CB], blk[:, CONV_CB:2 * CONV_CB], blk[:, 2 * CONV_CB:]
        row = lax.broadcasted_iota(jnp.int32, (seq, CONV_CB), 0)
        w0, w1, w2 = w_ref[0:1, :], w_ref[1:2, :], w_ref[2:3, :]
        u = gc * xc
        u1 = _shift_down(u, 1, row)
        u2 = _shift_down(u, 2, row)
        z = w0 * u2 + w1 * u1 + w2 * u
        dpv = dp_ref[...].astype(F32)
        dz = dpv * gb
        du = w2 * dz + w1 * _shift_up(dz, 1, row, seq) + w0 * _shift_up(dz, 2, row, seq)
        dc_ref[...] = jnp.concatenate([du * gc, dpv * z, du * xc], axis=1).astype(BF16)
        dw_ref[0:1, :] += jnp.sum(dz * u2, axis=0, keepdims=True)
        dw_ref[1:2, :] += jnp.sum(dz * u1, axis=0, keepdims=True)
        dw_ref[2:3, :] += jnp.sum(dz * u, axis=0, keepdims=True)

    return pl.pallas_call(
        body, name=name, grid=(D // CONV_CB, n_seq),
        in_specs=[pl.BlockSpec((seq, CONV_CB), lambda j, b: (b, j)), pl.BlockSpec((seq, 3 * CONV_CB), lambda j, b: (b, j)),
                  pl.BlockSpec((3, CONV_CB), lambda j, b: (0, j))],
        out_specs=[pl.BlockSpec((seq, 3 * CONV_CB), lambda j, b: (b, j)), pl.BlockSpec((3, CONV_CB), lambda j, b: (0, j))],
        out_shape=[jax.ShapeDtypeStruct((n_seq * seq, CONV_COLS), BF16), jax.ShapeDtypeStruct((3, D), F32)],
        compiler_params=_params("parallel", "arbitrary"),
    )(dp, conv3, cw)


def _merge_fwd(o, p, gl, bias, x1, wpa, wpc, wout, *, tm, name, hosted=None):
    t = x1.shape[0]

    def body(o_ref, p_ref, gl_ref, b_ref, x_ref, wpa_ref, wpc_ref, wout_ref, x2_ref, mg_ref, ya_ref, yb_ref):
        ya = _dot_nn(o_ref[...], wpa_ref[...])
        yb = _dot_nn(p_ref[...], wpc_ref[...])
        gates = _sigmoid(gl_ref[...].astype(F32) + b_ref[...])
        merged = (gates[:, :D] * ya + gates[:, D:] * yb).astype(BF16)
        ya_ref[...] = ya.astype(BF16)
        yb_ref[...] = yb.astype(BF16)
        mg_ref[...] = merged
        x2_ref[...] = x_ref[...] + _dot_nn(merged, wout_ref[...])

    row = pl.BlockSpec((tm, D), lambda i: (i, 0))
    row2 = pl.BlockSpec((tm, GATE_COLS), lambda i: (i, 0))
    wsp = pl.BlockSpec((D, D), lambda i: (0, 0))
    wide = jax.ShapeDtypeStruct((t, D), BF16)
    return _call(
        body, name=name, grid=(t // tm,),
        in_specs=[row, row, row2, pl.BlockSpec((1, GATE_COLS), lambda i: (0, 0)), row, wsp, wsp, wsp],
        out_specs=[row, row, row, row], out_shape=[jax.ShapeDtypeStruct((t, D), F32), wide, wide, wide], scratch_shapes=[],
        operands=(o, p, gl, bias, x1, wpa, wpc, wout), sem=("parallel",), hosted=hosted)


def _merge_bwd(dx2, ya, yb, gl, bias, wpa, wpc, wout, *, tm, name, hosted=None):
    t = dx2.shape[0]

    def body(dx_ref, ya_ref, yb_ref, gl_ref, b_ref, wpa_ref, wpc_ref, wout_ref,
             dxb_ref, dya_ref, dyb_ref, dgl_ref, do_ref, dp_ref, db_ref):
        @pl.when(pl.program_id(0) == 0)
        def _():
            db_ref[...] = jnp.zeros_like(db_ref)

        dxb = dx_ref[...].astype(BF16)
        dxb_ref[...] = dxb
        dm = _dot_nt(dxb, wout_ref[...])
        gates = _sigmoid(gl_ref[...].astype(F32) + b_ref[...])
        ga, gb = gates[:, :D], gates[:, D:]
        dya = (dm * ga).astype(BF16)
        dyb = (dm * gb).astype(BF16)
        dya_ref[...] = dya
        dyb_ref[...] = dyb
        dgl = jnp.concatenate([dm * ya_ref[...].astype(F32) * ga * (1.0 - ga),
                               dm * yb_ref[...].astype(F32) * gb * (1.0 - gb)], axis=1)
        dgl_ref[...] = dgl.astype(BF16)
        db_ref[...] += jnp.sum(dgl, axis=0, keepdims=True)
        do_ref[...] = _dot_nt(dya, wpa_ref[...]).astype(BF16)
        dp_ref[...] = _dot_nt(dyb, wpc_ref[...]).astype(BF16)

    row = pl.BlockSpec((tm, D), lambda i: (i, 0))
    row2 = pl.BlockSpec((tm, GATE_COLS), lambda i: (i, 0))
    vec2 = pl.BlockSpec((1, GATE_COLS), lambda i: (0, 0))
    wsp = pl.BlockSpec((D, D), lambda i: (0, 0))
    wide = jax.ShapeDtypeStruct((t, D), BF16)
    return _call(
        body, name=name, grid=(t // tm,), in_specs=[row, row, row, row2, vec2, wsp, wsp, wsp],
        out_specs=[row, row, row, row2, row, row, vec2],
        out_shape=[wide, wide, wide, jax.ShapeDtypeStruct((t, GATE_COLS), BF16), wide, wide,
                   jax.ShapeDtypeStruct((1, GATE_COLS), F32)],
        scratch_shapes=[], operands=(dx2, ya, yb, gl, bias, wpa, wpc, wout), sem=("arbitrary",), hosted=hosted)


def _adamw(w, g, m, v, *, name):
    rows, cols = w.shape
    tr = max([c for c in range(8, 513, 8) if rows % c == 0], default=rows)
    c1 = 1.0 / (1.0 - ADAM_B1 ** ADAM_STEP)
    c2 = 1.0 / (1.0 - ADAM_B2 ** ADAM_STEP)

    def body(w_ref, g_ref, m_ref, v_ref, d_ref, nm_ref, nv_ref):
        gv = g_ref[...]
        nm = ADAM_B1 * m_ref[...] + (1.0 - ADAM_B1) * gv
        nv = ADAM_B2 * v_ref[...] + (1.0 - ADAM_B2) * (gv * gv)
        nm_ref[...] = nm
        nv_ref[...] = nv
        d_ref[...] = -ADAM_LR * ((nm * c1) / (jnp.sqrt(nv * c2) + ADAM_EPS) + ADAM_WD * w_ref[...])

    spec = pl.BlockSpec((tr, cols), lambda i: (i, 0))
    shp = jax.ShapeDtypeStruct((rows, cols), F32)
    return pl.pallas_call(
        body, name=name, grid=(rows // tr,), in_specs=[spec] * 4, out_specs=[spec] * 3, out_shape=[shp] * 3,
        compiler_params=_params("parallel"),
    )(w, g, m, v)


def _place():
    return lax.axis_index("x"), lax.axis_index("y"), lax.axis_index("c")


def _other_chips(x, y):
    return [(1 - x, y), (x, 1 - y), (1 - x, 1 - y)]


def _remote(src, dst, send, recv, dev):
    return pltpu.make_async_remote_copy(src_ref=src, dst_ref=dst, send_sem=send, recv_sem=recv, device_id=dev, device_id_type=MESH)


def _gather_chips_plan(n):
    def start(srcs, dsts, send, recv, local):
        x, y, cc = _place()
        me = 4 * x + 2 * y + cc
        for a in range(n):
            pltpu.make_async_copy(srcs[a], dsts[a].at[me], local.at[a]).start()
            for k, (px, py) in enumerate(_other_chips(x, y)):
                _remote(srcs[a], dsts[a].at[me], send.at[3 * a + k], recv.at[3 * a + k], (px, py, cc)).start()

    def wait(srcs, dsts, send, recv, local):
        x, y, cc = _place()
        me = 4 * x + 2 * y + cc
        for a in range(n):
            for k, (px, py) in enumerate(_other_chips(x, y)):
                _remote(srcs[a], dsts[a].at[4 * px + 2 * py + cc], send.at[3 * a + k], recv.at[3 * a + k], (px, py, cc)).wait_recv()
        for a in range(n):
            for k, (px, py) in enumerate(_other_chips(x, y)):
                _remote(srcs[a], dsts[a].at[me], send.at[3 * a + k], recv.at[3 * a + k], (px, py, cc)).wait_send()
            pltpu.make_async_copy(srcs[a], dsts[a].at[me], local.at[a]).wait()

    return _Plan(start, wait, 3 * n, n)


def _scatter_chips_plan(n):
    def start(srcs, dsts, send, recv, local):
        x, y, cc = _place()
        for a in range(n):
            for k, (px, py) in enumerate(_other_chips(x, y)):
                _remote(srcs[a].at[2 * px + py], dsts[a].at[k], send.at[3 * a + k], recv.at[3 * a + k], (px, py, cc)).start()

    def wait(srcs, dsts, send, recv, local):
        x, y, cc = _place()
        for a in range(n):
            for k, (px, py) in enumerate(_other_chips(x, y)):
                _remote(srcs[a].at[k], dsts[a].at[k], send.at[3 * a + k], recv.at[3 * a + k], (px, py, cc)).wait_recv()
        for a in range(n):
            for k, (px, py) in enumerate(_other_chips(x, y)):
                _remote(srcs[a].at[k], dsts[a].at[k], send.at[3 * a + k], recv.at[3 * a + k], (px, py, cc)).wait_send()

    return _Plan(start, wait, 3 * n, 0)


def _gather_shapes(blocks):
    return [jax.ShapeDtypeStruct((N_DEV,) + b.shape, b.dtype) for b in blocks]


def _scatter_shapes(parts):
    return [jax.ShapeDtypeStruct((3,) + p.shape[1:], p.dtype) for p in parts]


def _gather_sibling_plan(n):
    def start(srcs, dsts, send, recv, local):
        x, y, cc = _place()
        for a in range(n):
            for q in range(4):
                _remote(srcs[a].at[2 * q + cc], dsts[a].at[2 * q + cc], send.at[4 * a + q], recv.at[4 * a + q], (x, y, 1 - cc)).start()

    def wait(srcs, dsts, send, recv, local):
        x, y, cc = _place()
        for a in range(n):
            for q in range(4):
                _remote(srcs[a].at[2 * q + cc], dsts[a].at[2 * q + 1 - cc], send.at[4 * a + q], recv.at[4 * a + q],
                        (x, y, 1 - cc)).wait_recv()
        for a in range(n):
            for q in range(4):
                _remote(srcs[a].at[2 * q + cc], dsts[a].at[2 * q + cc], send.at[4 * a + q], recv.at[4 * a + q],
                        (x, y, 1 - cc)).wait_send()

    return _Plan(start, wait, 4 * n, 0, in_place=True)


def _scatter_sibling_plan(n):
    def start(srcs, dsts, send, recv, local):
        x, y, cc = _place()
        for a in range(n):
            for q in range(4):
                _remote(srcs[a].at[2 * q + 1 - cc], dsts[a].at[q], send.at[4 * a + q], recv.at[4 * a + q], (x, y, 1 - cc)).start()

    def wait(srcs, dsts, send, recv, local):
        x, y, cc = _place()
        for a in range(n):
            for q in range(4):
                _remote(srcs[a].at[q], dsts[a].at[q], send.at[4 * a + q], recv.at[4 * a + q], (x, y, 1 - cc)).wait_recv()
        for a in range(n):
            for q in range(4):
                _remote(srcs[a].at[q], dsts[a].at[q], send.at[4 * a + q], recv.at[4 * a + q], (x, y, 1 - cc)).wait_send()

    return _Plan(start, wait, 4 * n, 0)


def _same_shapes(arrs):
    return [jax.ShapeDtypeStruct(a.shape, a.dtype) for a in arrs]


def _halved_shapes(parts):
    return [jax.ShapeDtypeStruct((4,) + p.shape[1:], p.dtype) for p in parts]


def _run_plan(plan, srcs, out_shapes, *, name):
    n_in, n_out = len(srcs), len(out_shapes)

    def body(*refs):
        h_in, h_out, sems = refs[:n_in], refs[n_in:n_in + n_out], refs[n_in + n_out:]
        plan.start(h_in, h_out, *sems)
        plan.wait(h_in, h_out, *sems)

    return pl.pallas_call(body, name=name, in_specs=[ANY] * n_in, out_specs=[ANY] * n_out, out_shape=list(out_shapes),
                          input_output_aliases={a: a for a in range(n_in)} if plan.in_place else {},
                          scratch_shapes=plan.sems())(*srcs)


SEM = pl.BlockSpec(memory_space=pltpu.SEMAPHORE)
HBM = pl.BlockSpec(memory_space=pltpu.HBM)
SIDE_EFFECT = pltpu.CompilerParams(has_side_effects=pltpu.SideEffectType.DATAFLOW_SIDE_EFFECTING)


def _gather_chips_start(blocks, *, name):
    n = len(blocks)
    plan = _gather_chips_plan(n)
    lands = [lax.empty((N_DEV,) + b.shape, b.dtype) for b in blocks]

    def body(*refs):
        srcs, sems, lands_out, token = refs[:n], refs[2 * n:2 * n + 3], refs[3 * n + 3:4 * n + 3], refs[4 * n + 3]
        plan.start(srcs, lands_out, *sems)
        token[...] = jnp.zeros_like(token)

    out_shape = ([s for s in plan.sems()] + [pltpu.HBM(b.shape, b.dtype) for b in blocks]
                 + [pltpu.HBM(l.shape, l.dtype) for l in lands] + [jax.ShapeDtypeStruct((8, 128), F32)])
    res = pl.pallas_call(
        body, name=name, in_specs=[HBM] * (2 * n), out_specs=[SEM] * 3 + [HBM] * (2 * n) + [pl.BlockSpec(memory_space=pltpu.VMEM)],
        out_shape=out_shape, input_output_aliases={a: 3 + a for a in range(2 * n)}, compiler_params=SIDE_EFFECT,
    )(*[pltpu.with_memory_space_constraint(a, pltpu.HBM) for a in list(blocks) + lands])
    return res[:3], res[3:3 + n], res[3 + n:3 + 2 * n], res[3 + 2 * n]


def _gather_chips_wait(sems, blocks, lands, after, *, name):
    n = len(blocks)
    plan = _gather_chips_plan(n)

    def body(*refs):
        plan.wait(refs[:n], refs[n:2 * n], *refs[2 * n:2 * n + 3])

    res = pl.pallas_call(
        body, name=name, in_specs=[HBM] * (2 * n) + [SEM] * 3 + [ANY], out_specs=[HBM] * (2 * n),
        out_shape=[pltpu.HBM(a.shape, a.dtype) for a in list(blocks) + list(lands)],
        input_output_aliases={a: a for a in range(2 * n)}, compiler_params=SIDE_EFFECT,
    )(*blocks, *lands, *sems, after)
    return list(res[n:])


def _sum_sibling(p, q, core, *, name):
    _, r, c = p.shape

    def body(core_ref, p_ref, q_ref, o_ref):
        o_ref[...] = (p_ref[...].astype(F32) + q_ref[...].astype(F32)).astype(BF16)

    grid_spec = pltpu.PrefetchScalarGridSpec(
        num_scalar_prefetch=1, grid=(4,),
        in_specs=[pl.BlockSpec((1, r, c), lambda ch, core_ref: (2 * ch + core_ref[0], 0, 0)),
                  pl.BlockSpec((1, r, c), lambda ch, core_ref: (ch, 0, 0))],
        out_specs=pl.BlockSpec((1, r, c), lambda ch, core_ref: (ch, 0, 0)))
    return pl.pallas_call(
        body, name=name, grid_spec=grid_spec, out_shape=jax.ShapeDtypeStruct((4, r, c), BF16),
        compiler_params=_params("parallel"),
    )(core, p, q)


def _sum_chips(s1, r2, chip, *, name):
    _, r, c = s1.shape

    def body(chip_ref, s_ref, r_ref, o_ref):
        acc = s_ref[0].astype(F32)
        for k in range(3):
            acc = acc + r_ref[k].astype(F32)
        o_ref[...] = acc

    grid_spec = pltpu.PrefetchScalarGridSpec(
        num_scalar_prefetch=1, grid=(1,),
        in_specs=[pl.BlockSpec((1, r, c), lambda i, chip_ref: (chip_ref[0], 0, 0)),
                  pl.BlockSpec((3, r, c), lambda i, chip_ref: (0, 0, 0))],
        out_specs=pl.BlockSpec((r, c), lambda i, chip_ref: (0, 0)))
    return pl.pallas_call(
        body, name=name, grid_spec=grid_spec, out_shape=jax.ShapeDtypeStruct((r, c), F32),
        compiler_params=_params("arbitrary"),
    )(chip, s1, r2)


def _sum_adamw(s1, r2, chip, w, m, v, *, name):
    _, r, c = s1.shape
    c1 = 1.0 / (1.0 - ADAM_B1 ** ADAM_STEP)
    c2 = 1.0 / (1.0 - ADAM_B2 ** ADAM_STEP)

    def body(chip_ref, s_ref, r_ref, w_ref, m_ref, v_ref, g_ref, d_ref, nm_ref, nv_ref):
        gv = s_ref[0].astype(F32)
        for k in range(3):
            gv = gv + r_ref[k].astype(F32)
        g_ref[...] = gv
        nm = ADAM_B1 * m_ref[...] + (1.0 - ADAM_B1) * gv
        nv = ADAM_B2 * v_ref[...] + (1.0 - ADAM_B2) * (gv * gv)
        nm_ref[...] = nm
        nv_ref[...] = nv
        d_ref[...] = -ADAM_LR * ((nm * c1) / (jnp.sqrt(nv * c2) + ADAM_EPS) + ADAM_WD * w_ref[...])

    flat = pl.BlockSpec((r, c), lambda i, chip_ref: (0, 0))
    grid_spec = pltpu.PrefetchScalarGridSpec(
        num_scalar_prefetch=1, grid=(1,),
        in_specs=[pl.BlockSpec((1, r, c), lambda i, chip_ref: (chip_ref[0], 0, 0)),
                  pl.BlockSpec((3, r, c), lambda i, chip_ref: (0, 0, 0)), flat, flat, flat],
        out_specs=[flat] * 4)
    return pl.pallas_call(
        body, name=name, grid_spec=grid_spec, out_shape=[jax.ShapeDtypeStruct((r, c), F32)] * 4,
        compiler_params=_params("arbitrary"),
    )(chip, s1, r2, w, m, v)


def _small_exchange(v, *, reduce, name):
    r, c = v.shape

    def body(x_ref, o_ref, *rest):
        if reduce:
            buf_ref, send_sems, recv_sems = rest
        else:
            buf_ref = o_ref
            send_sems, recv_sems = rest
        x, y, cc = _place()
        me = 4 * x + 2 * y + cc

        def peer(k):
            return ((1 - x) if k & 4 else x, (1 - y) if k & 2 else y, (1 - cc) if k & 1 else cc)

        buf_ref[me] = x_ref[...]
        sends = []
        for k in range(1, N_DEV):
            cp = pltpu.make_async_remote_copy(src_ref=x_ref, dst_ref=buf_ref.at[me], send_sem=send_sems.at[k - 1],
                                              recv_sem=recv_sems.at[k - 1], device_id=peer(k), device_id_type=MESH)
            cp.start()
            sends.append(cp)
        for k in range(1, N_DEV):
            px, py, pc = peer(k)
            pltpu.make_async_remote_copy(src_ref=x_ref, dst_ref=buf_ref.at[4 * px + 2 * py + pc], send_sem=send_sems.at[k - 1],
                                         recv_sem=recv_sems.at[k - 1], device_id=peer(k), device_id_type=MESH).wait_recv()
        for cp in sends:
            cp.wait_send()
        if reduce:
            acc = buf_ref[0]
            for s in range(1, N_DEV):
                acc = acc + buf_ref[s]
            o_ref[...] = acc

    vm = pl.BlockSpec(memory_space=pltpu.VMEM)
    sems = [pltpu.SemaphoreType.DMA((N_DEV - 1,)), pltpu.SemaphoreType.DMA((N_DEV - 1,))]
    if reduce:
        out_shape, scratch = jax.ShapeDtypeStruct((r, c), F32), [pltpu.VMEM((N_DEV, r, c), F32)] + sems
    else:
        out_shape, scratch = jax.ShapeDtypeStruct((N_DEV, r, c), F32), sems
    return pl.pallas_call(body, name=name, in_specs=[vm], out_specs=vm, out_shape=out_shape, scratch_shapes=scratch)(v)


def _rows(a):
    return a.reshape(-1, D)


def _pad_cols(a, to):
    return jnp.pad(a, ((0, 0), (0, to - a.shape[1])))


def _pack_weights(w):
    parts = {
        "w_inT": jnp.pad(w["w_in"].T, ((0, IN_SHARD_PAD - IN_SHARD), (0, 0))),
        "w_uq": _rows(_head_cols(w["w_uq"])), "w_uk": _rows(_head_cols(w["w_uk"])),
        "w_uv": _rows(_pad_cols(w["w_uv"], HEAD_PAD)), "w_pa": _rows(w["w_proj_attn"]),
        "w_pc": w["w_proj_conv"], "w_out": w["w_out"],
    }
    return [jnp.concatenate([parts[n].astype(BF16) for n, _ in group], axis=0) for group in PACK]


def _cols_from_shards(gs, name, rows):
    idx, off, r = PACK_OFF[name]
    return gs[idx][:, off:off + r].reshape(N_DEV, rows, HEAD_PAD).transpose(1, 0, 2).reshape(rows, N_DEV * HEAD_PAD)


def _rows_from_shards(gs, name, keep=None):
    idx, off, r = PACK_OFF[name]
    keep = r if keep is None else keep
    return gs[idx][:, off:off + keep].reshape(N_DEV * keep, D)


def _rope_placement():
    i = lax.broadcasted_iota(jnp.int32, (HEAD_PAD, D), 0)
    j = lax.broadcasted_iota(jnp.int32, (HEAD_PAD, D), 1)
    lane = jnp.where(i < ROPE_HALF, 32 + i, 96 + i - ROPE_HALF)
    return ((i < 2 * ROPE_HALF) & (j % HEAD_PAD == lane)).astype(BF16)


def _unpack_in(g_in):
    w_inT = _rows_from_shards([g_in, None], "w_inT", IN_SHARD)
    lat_rows = Q_LORA + KV_LORA + 2 * ROPE_HALF
    conv = w_inT[lat_rows:lat_rows + CONV_COLS].reshape(3, D // CONV_CB, CONV_CB, D).transpose(1, 0, 2, 3).reshape(CONV_COLS, D)
    return {"latT": jnp.pad(w_inT[:lat_rows], ((0, LAT_PAD - lat_rows), (0, 0))), "convT": conv,
            "gateT": w_inT[lat_rows + CONV_COLS:]}


def _unpack_misc(g_misc):
    g = [None, g_misc]
    wpa = _cols_from_shards(g, "w_pa", 512).reshape(N_HEADS, NOPE, D)
    return {
        "wq": _cols_from_shards(g, "w_uq", Q_LORA),
        "wk": jnp.concatenate([_cols_from_shards(g, "w_uk", KV_LORA), _rope_placement()], axis=0),
        "wv": _cols_from_shards(g, "w_uv", KV_LORA),
        "wpa": jnp.pad(wpa, ((0, 0), (0, HEAD_PAD - NOPE), (0, 0))).reshape(D, D),
        "wpc": _rows_from_shards(g, "w_pc"), "wout": _rows_from_shards(g, "w_out"),
    }


def _shards_from_cols(a):
    rows = a.shape[0]
    return a.reshape(rows, N_DEV, HEAD_PAD).transpose(1, 0, 2).reshape(N_DEV, rows * HEAD_PAD // D, D)


def _pack_grads(gw):
    lat_rows = Q_LORA + KV_LORA + 2 * ROPE_HALF
    conv = gw["convT"].reshape(D // CONV_CB, 3, CONV_CB, D).transpose(1, 0, 2, 3).reshape(CONV_COLS, D)
    w_inT = jnp.concatenate([gw["latT"][:lat_rows], conv, gw["gateT"]], axis=0).reshape(N_DEV, IN_SHARD, D)
    wpa = gw["wpa"].reshape(N_HEADS, HEAD_PAD, D)[:, :NOPE].reshape(N_HEADS * NOPE, D)
    parts = {}
    parts.update({
        "w_inT": jnp.pad(w_inT, ((0, 0), (0, IN_SHARD_PAD - IN_SHARD), (0, 0))),
        "w_uq": _shards_from_cols(gw["wq"]), "w_uk": _shards_from_cols(gw["wk"][:KV_LORA]),
        "w_uv": _shards_from_cols(gw["wv"][:KV_LORA]), "w_pa": _shards_from_cols(wpa),
        "w_pc": gw["wpc"].reshape(N_DEV, D // N_DEV, D), "w_out": gw["wout"].reshape(N_DEV, D // N_DEV, D),
    })
    return [jnp.concatenate([parts[n] for n, _ in group], axis=1) for group in PACK]


def _unpack_grads(mines):
    def seg(name, keep=None):
        idx, off, r = PACK_OFF[name]
        return mines[idx][off:off + (r if keep is None else keep)]

    return {
        "w_in": seg("w_inT", IN_SHARD).T,
        "w_uq": _head_cols_inv(seg("w_uq").reshape(Q_LORA, HEAD_PAD), QK_DIM),
        "w_uk": _head_cols_inv(seg("w_uk").reshape(KV_LORA, HEAD_PAD), NOPE),
        "w_uv": seg("w_uv").reshape(KV_LORA, HEAD_PAD)[:, :NOPE],
        "w_proj_attn": seg("w_pa").reshape(512, HEAD_PAD),
        "w_proj_conv": seg("w_pc"), "w_out": seg("w_out"),
    }


def _rope_tables(positions):
    lane = jnp.arange(HEAD_PAD)
    idx = jnp.where((lane >= 32) & (lane < 48), lane - 32, jnp.where((lane >= 96) & (lane < 112), lane - 96, -1))
    inv_freq = jnp.where(idx >= 0, 1.0 / (ROPE_THETA ** (idx.astype(F32) / ROPE_HALF)), 0.0)
    ang = positions.reshape(-1).astype(F32)[:, None] * inv_freq
    return jnp.cos(ang), jnp.sin(ang) * jnp.where(lane < HEAD_PAD // 2, -1.0, 1.0)


def _local_step(x, positions, target, conv_w, small, ex):
    n_seq, seq, d = x.shape
    t = n_seq * seq
    x0 = x.reshape(t, d)
    tgt = target.reshape(t, d)
    rc, rs = _rope_tables(positions)
    ghq = _head_cols(small["q_head_norm"])
    ghk = _head_cols(small["k_head_norm"])
    TM, HC, TQ = 1024, 256, 1024

    def mm(*args, hosted=None, **kw):
        res = _mm(*args, hosted=hosted, **kw)
        return res if hosted is not None else (res, None)

    def wgrad(a, b, name, tm=None, hosted=None):
        tm = tm or a.shape[1]
        return mm(a, b, mode="tn", out_dtype=BF16, tm=tm, tn=b.shape[1], tk=2048 if tm <= D else 1024, name=name, hosted=hosted)

    f1g, f1u, f1d = ex.gather_finish(ex.witness() + rc[:8] + conv_w[:1, :HEAD_PAD])
    (x1, h1, a1, b1), got = _ffn_fwd(x0, small["ffn1_norm"], f1g, f1u, f1d, tm=512, hc=DFF // 2, name="ffn1_fwd",
                                     hosted=ex.gather_chips("mix_in"))
    hm, got = _rms_fwd(x1, small["mix_norm"], tm=TM, name="mix_norm_fwd", hosted=ex.gather_sibling(got))
    W = ex.mix_in_weights(got)
    (lat, conv3, gl), got = _proj_fwd(hm, W["latT"], W["convT"], W["gateT"], tm=512, name="proj_fwd",
                                      hosted=ex.gather_chips("mix_misc"))
    p, got = _conv_fwd(conv3, conv_w, n_seq=n_seq, seq=seq, name="conv_fwd", hosted=ex.gather_sibling(got))
    W.update(ex.mix_misc_weights(got))
    q, k, v, qn, ckv = _mla_prep_fwd(lat, small["q_a_norm"], small["kv_a_norm"], ghq, ghk, W["wq"], W["wk"], W["wv"], rc, rs,
                                     tm=512, name="mla_prep_fwd")
    (o, lse), got = _flash_fwd(q, k, v, n_seq=n_seq, seq=seq, tq=TQ, name="attn_fwd", hosted=ex.gather_chips("ffn2"))
    (x2, merged, ya, yb), got = _merge_fwd(o, p, gl, small["gate_bias"], x1, W["wpa"], W["wpc"], W["wout"], tm=512, name="merge_fwd",
                                           hosted=ex.gather_sibling(got))
    f2g, f2u, f2d = ex.ffn_weights(got)
    (dy, h2, a2, b2, loss_row), _ = _ffn_fwd(x2, small["ffn2_norm"], f2g, f2u, f2d, tm=512, hc=DFF // 2, name="ffn2_fwd", target=tgt)

    gw, gs = {}, {}
    (da2, db2, *ffn2_grads), _ = _ffn_grads(dy, h2, a2, b2, f2d, tm=TM, hc=HC, name="ffn2_grads")
    (dx2, gs["ffn2_norm"]), _ = _ffn_up_bwd(da2, db2, f2g, f2u, x2, small["ffn2_norm"], dy, tm=512, name="ffn2_up_bwd")

    (dx2b, dya, dyb, dgl, do, dp, gs["gate_bias"]), got = _merge_bwd(
        dx2, ya, yb, gl, small["gate_bias"], W["wpa"], W["wpc"], W["wout"], tm=512, name="merge_bwd",
        hosted=ex.scatter_sibling("ffn2", ffn2_grads))
    ex.scatter_sibling_done("ffn2", got)
    gw["wout"] = wgrad(merged, dx2b, "dw_out")[0]
    gw["wpa"] = wgrad(o, dya, "dw_pa")[0]
    gw["wpc"] = wgrad(p, dyb, "dw_pc")[0]
    dconv3, dconv_w = _conv_bwd(dp, conv3, conv_w, n_seq=n_seq, seq=seq, name="conv_bwd")
    (dq, dk, dv), got = _flash_bwd(q, k, v, o, lse, do, n_seq=n_seq, seq=seq, tq=TQ, name="attn_bwd",
                                   hosted=ex.scatter_chips("ffn2"))
    ex.scatter_chips_done("ffn2", got)
    dlat, dqp, dkp, gs["q_a_norm"], gs["kv_a_norm"], dghq, dghk = _mla_prep_bwd(
        dq, dk, dv, lat, qn, ckv, small["q_a_norm"], small["kv_a_norm"], ghq, ghk, W["wq"], W["wk"], W["wv"], rc, rs,
        tm=512, name="mla_prep_bwd")
    gs["q_head_norm"], gs["k_head_norm"] = _head_cols_inv(dghq, QK_DIM), _head_cols_inv(dghk, QK_DIM)
    gw["wq"] = wgrad(qn, dqp, "dw_uq")[0]
    gw["wk"] = wgrad(ckv, dkp, "dw_uk")[0]
    gw["wv"] = wgrad(ckv, dv, "dw_uv")[0]
    gw["convT"] = wgrad(dconv3, hm, "dw_conv", tm=CONV_COLS // 2)[0]
    gw["gateT"] = wgrad(dgl, hm, "dw_gate")[0]
    gw["latT"] = wgrad(dlat, hm, "dw_lat")[0]
    ex.scatter_sibling_now("mix", gw)
    (dx1, gs["mix_norm"]), got = _proj_bwd(dlat, dconv3, dgl, W["latT"], W["convT"], W["gateT"], x1, small["mix_norm"], dx2,
                                           tm=512, name="proj_bwd", hosted=ex.scatter_chips("mix_in"))
    ex.scatter_chips_done("mix_in", got)

    (da1, db1, *ffn1_grads), got = _ffn_grads(dx1, h1, a1, b1, f1d, tm=TM, hc=HC, name="ffn1_grads",
                                              hosted=ex.scatter_chips("mix_misc"))
    ex.scatter_chips_done("mix_misc", got)
    ex.scatter_sibling_now("ffn1", ffn1_grads)
    (dx0, gs["ffn1_norm"]), got = _ffn_up_bwd(da1, db1, f1g, f1u, x0, small["ffn1_norm"], dx1, tm=512, name="ffn1_up_bwd",
                                              hosted=ex.scatter_chips("ffn1"))
    ex.scatter_chips_done("ffn1", got)
    return loss_row, dx0.reshape(n_seq, seq, d), dconv_w, gs


class _MeshExchange:
    def __init__(self, w, core, chip):
        self.w, self.core, self.chip = w, core, chip
        self.partial, self.received, self._cache = {}, {}, {}

    def _blocks(self, group):
        w = self.w
        if group not in self._cache:
            if group.startswith("ffn"):
                self._cache[group] = [w[group + "_w_gate"].T.astype(BF16), w[group + "_w_up"].T.astype(BF16),
                                      w[group + "_w_down"].astype(BF16)]
            else:
                self._cache["mix_in"], self._cache["mix_misc"] = [[b] for b in _pack_weights(w)]
        return self._cache[group]

    def gather_chips(self, *groups):
        blocks = [b for group in groups for b in self._blocks(group)]
        return _gather_chips_plan(len(blocks)), blocks, _gather_shapes(blocks)

    def gather_sibling(self, got):
        half = list(got)
        return _gather_sibling_plan(len(half)), half, _same_shapes(half)

    def gather_start(self, group):
        sems, blocks, lands, token = _gather_chips_start(self._blocks(group), name="gather_%s_start" % group)
        self._started = (group, sems, blocks, lands)
        return token[0, 0]

    def gather_finish(self, after):
        group, sems, blocks, lands = self._started
        half = _gather_chips_wait(sems, blocks, lands, after, name="gather_%s_wait" % group)
        return self.ffn_weights(_run_plan(_gather_sibling_plan(len(half)), half, _same_shapes(half), name="gather_%s_sibling" % group))

    def witness(self):
        parts = [b[:8, :128].astype(F32) for g in ("mix_in", "mix_misc", "ffn2") for b in self._blocks(g)]
        return functools.reduce(jnp.add, parts)

    def ffn_weights(self, got):
        return [a.reshape(DFF, D) for a in got]

    def mix_in_weights(self, got):
        return _unpack_in(got[0])

    def mix_misc_weights(self, got):
        return _unpack_misc(got[0])

    def _parts(self, group, grads):
        if group == "mix":
            return _pack_grads(grads), ["mix_in", "mix_misc"]
        parts = [g.reshape(N_DEV, -1, D) for g in grads]
        return parts, ([group] if len(parts) == 1 else None)

    def scatter_sibling(self, group, grads):
        self._sent, self._names = self._parts(group, grads)
        return _scatter_sibling_plan(len(self._sent)), self._sent, _halved_shapes(self._sent)

    def scatter_sibling_done(self, group, got):
        sums = [_sum_sibling(p, q, self.core, name="sum_%s_sibling_%d" % (group, i)) for i, (p, q) in enumerate(zip(self._sent, got))]
        if self._names is None:
            self.partial[group] = sums
        else:
            for n, s in zip(self._names, sums):
                self.partial[n] = [s]

    def scatter_sibling_now(self, group, grads):
        plan, parts, shapes = self.scatter_sibling(group, grads)
        self.scatter_sibling_done(group, _run_plan(plan, parts, shapes, name="scatter_%s_sibling" % group))

    def scatter_chips(self, group):
        s1 = self.partial[group]
        return _scatter_chips_plan(len(s1)), s1, _scatter_shapes(s1)

    def scatter_chips_done(self, group, got):
        self.received[group] = list(got)


SMALL_NAMES = ("ffn1_norm", "mix_norm", "gate_bias", "q_a_norm", "kv_a_norm", "q_head_norm", "k_head_norm", "ffn2_norm")
SMALL_SLOTS = {"ffn1_norm": 1024, "mix_norm": 1024, "gate_bias": 2048, "q_a_norm": 384, "kv_a_norm": 256, "q_head_norm": 128,
               "k_head_norm": 128, "ffn2_norm": 1024, "conv_w": 3072, "loss": 128}
COLUMN_MAJOR = ("w_in", "w_uq", "w_uk", "w_uv")
WEIGHT_NAMES = ("ffn1_norm", "ffn1_w_gate", "ffn1_w_up", "ffn1_w_down", "mix_norm", "w_in", "gate_bias", "q_a_norm", "w_uq",
                "kv_a_norm", "w_uk", "w_uv", "q_head_norm", "k_head_norm", "w_proj_attn", "conv_w", "w_proj_conv", "w_out",
                "ffn2_norm", "ffn2_w_gate", "ffn2_w_up", "ffn2_w_down")


def _step(x, positions, loss_target, w, m, v):
    xi, yi, ci = _place()
    core = ci.astype(jnp.int32).reshape(1)
    chip = (2 * xi + yi).astype(jnp.int32).reshape(1)
    me = 4 * xi + 2 * yi + ci

    ex = _MeshExchange(w, core, chip)
    cw_all = _small_exchange(jnp.pad(w["conv_w"], ((0, 5), (0, 0))), reduce=False, name="gather_conv_w")
    conv_w = cw_all[:, :3].transpose(1, 0, 2).reshape(3, D)
    ex.w = {n: (a + cw_all[0, 7, 0] if n.startswith("ffn1") else a) for n, a in w.items()}
    zero = ex.gather_start("ffn1")
    ex.w = {n: (a if n.startswith("ffn1") else a + zero) for n, a in w.items()}
    small = {n: w[n].reshape(1, -1) for n in SMALL_NAMES}

    loss_row, grad_x, dconv_w, gs = _local_step(x, positions + zero.astype(jnp.int32), loss_target, conv_w, small, ex)

    grads, deltas, new_m, new_v = {}, {}, {}, {}
    where = {"ffn1_w_gate": ("ffn1", 0), "ffn1_w_up": ("ffn1", 1), "ffn1_w_down": ("ffn1", 2),
             "ffn2_w_gate": ("ffn2", 0), "ffn2_w_up": ("ffn2", 1), "ffn2_w_down": ("ffn2", 2)}
    for n, (group, i) in where.items():
        transposed = not n.endswith("down")
        wv, mv, vv = (a[n].T if transposed else a[n] for a in (w, m, v))
        res = _sum_adamw(ex.partial[group][i], ex.received[group][i], chip, wv, mv, vv, name="adamw_" + n)
        grads[n], deltas[n], new_m[n], new_v[n] = (r.T if transposed else r for r in res)
    grads.update(_unpack_grads([_sum_chips(ex.partial[g][0], ex.received[g][0], chip, name="sum_%s_chips" % g)
                                for g in ("mix_in", "mix_misc")]))

    pieces = [_pad_cols(gs[n], SMALL_SLOTS[n]) for n in SMALL_NAMES] + [dconv_w.reshape(1, 3 * D), loss_row]
    total = _small_exchange(jnp.concatenate(pieces, axis=1).reshape(-1, 128), reduce=True, name="reduce_small").reshape(-1)
    off = 0
    for n in SMALL_NAMES:
        grads[n] = total[off:off + w[n].shape[0]]
        off += SMALL_SLOTS[n]
    conv_full = total[off:off + 3 * D].reshape(3, D)
    grads["conv_w"] = lax.dynamic_slice(conv_full, (0, me * HEAD_PAD), (3, HEAD_PAD))
    loss = total[off + 3 * D]

    for n in WEIGHT_NAMES:
        if n in deltas:
            continue
        shape = w[n].shape
        if n in COLUMN_MAJOR:
            ops = [a.T for a in (w[n], grads[n], m[n], v[n])]
            deltas[n], new_m[n], new_v[n] = (r.T for r in _adamw(*ops, name="adamw_" + n))
            continue
        if len(shape) == 1:
            view = (-1, 128) if shape[0] % 128 == 0 else (1, shape[0])
        else:
            view = shape
        dlt, nm, nv = _adamw(w[n].reshape(view), grads[n].reshape(view), m[n].reshape(view), v[n].reshape(view), name="adamw_" + n)
        deltas[n], new_m[n], new_v[n] = dlt.reshape(shape), nm.reshape(shape), nv.reshape(shape)
    return (loss, grad_x, *[grads[n] for n in WEIGHT_NAMES], *[deltas[n] for n in WEIGHT_NAMES],
            *[new_m[n] for n in WEIGHT_NAMES], *[new_v[n] for n in WEIGHT_NAMES])


def kernel(x, positions, ffn1_norm, ffn1_w_gate, ffn1_w_up, ffn1_w_down, mix_norm, w_in, gate_bias, q_a_norm, w_uq, kv_a_norm, w_uk, w_uv, q_head_norm, k_head_norm, w_proj_attn, conv_w, w_proj_conv, w_out, ffn2_norm, ffn2_w_gate, ffn2_w_up, ffn2_w_down, loss_target, m_ffn1_norm, m_ffn1_w_gate, m_ffn1_w_up, m_ffn1_w_down, m_mix_norm, m_w_in, m_gate_bias, m_q_a_norm, m_w_uq, m_kv_a_norm, m_w_uk, m_w_uv, m_q_head_norm, m_k_head_norm, m_w_proj_attn, m_conv_w, m_w_proj_conv, m_w_out, m_ffn2_norm, m_ffn2_w_gate, m_ffn2_w_up, m_ffn2_w_down, v_ffn1_norm, v_ffn1_w_gate, v_ffn1_w_up, v_ffn1_w_down, v_mix_norm, v_w_in, v_gate_bias, v_q_a_norm, v_w_uq, v_kv_a_norm, v_w_uk, v_w_uv, v_q_head_norm, v_k_head_norm, v_w_proj_attn, v_conv_w, v_w_proj_conv, v_w_out, v_ffn2_norm, v_ffn2_w_gate, v_ffn2_w_up, v_ffn2_w_down):
    given = dict(locals())
    w = {n: given[n] for n in WEIGHT_NAMES}
    m = {n: given["m_" + n] for n in WEIGHT_NAMES}
    v = {n: given["v_" + n] for n in WEIGHT_NAMES}
    return _step(x, positions, loss_target, w, m, v)
```

```python
import functools

import jax
import jax.numpy as jnp
from jax import lax
from jax.experimental import pallas as pl
from jax.experimental.pallas import tpu as pltpu

F32 = jnp.float32
BF16 = jnp.bfloat16
MESH = pl.DeviceIdType.MESH
ANY = pl.BlockSpec(memory_space=pl.ANY)

N_DEV = 8
D = 1024
DFF = 2816
N_HEADS = 8
HEAD_PAD = 128
QK_DIM = 96
NOPE = 64
ROPE_HALF = 16
Q_LORA = 384
KV_LORA = 256
LAT_PAD = 768
CONV_COLS = 3072
GATE_COLS = 2048
IN_DIM = 5792
IN_SHARD = IN_DIM // N_DEV
IN_SHARD_PAD = 736
FF_SHARD = DFF // N_DEV
ROPE_THETA = 10000.0
NORM_EPS = 1e-6
ATTN_SCALE = QK_DIM ** -0.5
NEG = -1e30

ADAM_LR, ADAM_B1, ADAM_B2, ADAM_EPS, ADAM_WD, ADAM_STEP = 0.001, 0.9, 0.999, 1e-08, 0.01, 10

PACK = ((("w_inT", IN_SHARD_PAD),), (("w_uq", 48), ("w_uk", 32), ("w_uv", 32), ("w_pa", 64), ("w_pc", 128), ("w_out", 128)))
PACK_OFF = {}
for _i, _group in enumerate(PACK):
    _o = 0
    for _n, _r in _group:
        PACK_OFF[_n] = (_i, _o, _r)
        _o += _r

VMEM_LIMIT = 56 * 1024 * 1024


def _params(*sem):
    return pltpu.CompilerParams(dimension_semantics=sem if sem else None, vmem_limit_bytes=VMEM_LIMIT)


class _Plan:
    def __init__(self, start, wait, n_remote, n_local, in_place=False):
        self.start, self.wait, self.n_remote, self.n_local, self.in_place = start, wait, n_remote, n_local, in_place

    def sems(self):
        return [pltpu.SemaphoreType.DMA((self.n_remote,)), pltpu.SemaphoreType.DMA((self.n_remote,)),
                pltpu.SemaphoreType.DMA((max(self.n_local, 1),))]


def _call(body, *, name, grid, in_specs, out_specs, out_shape, scratch_shapes, operands, sem, hosted=None):
    if hosted is None:
        outs = pl.pallas_call(body, name=name, grid=grid, in_specs=in_specs, out_specs=out_specs, out_shape=out_shape,
                              scratch_shapes=scratch_shapes, compiler_params=_params(*sem))(*operands)
        return outs, None
    plan, srcs, h_shapes = hosted
    n_in, n_out, n_scr, nh_in, nh_out = len(in_specs), len(out_specs), len(scratch_shapes), len(srcs), len(h_shapes)
    aliases = {n_in + a: n_out + a for a in range(nh_in)} if plan.in_place else {}

    def full_body(*refs):
        ins, refs = refs[:n_in], refs[n_in:]
        h_in, refs = refs[:nh_in], refs[nh_in:]
        outs, refs = refs[:n_out], refs[n_out:]
        h_out, refs = refs[:nh_out], refs[nh_out:]
        scr, sems = refs[:n_scr], refs[n_scr:]
        ids = [pl.program_id(ax) for ax in range(len(grid))]
        first = functools.reduce(jnp.logical_and, [i == 0 for i in ids])
        last = functools.reduce(jnp.logical_and, [i == g - 1 for i, g in zip(ids, grid)])

        @pl.when(first)
        def _():
            plan.start(h_in, h_out, *sems)

        body(*ins, *outs, *scr)

        @pl.when(last)
        def _():
            plan.wait(h_in, h_out, *sems)

    res = pl.pallas_call(
        full_body, name=name, grid=grid, in_specs=list(in_specs) + [ANY] * nh_in, out_specs=list(out_specs) + [ANY] * nh_out,
        out_shape=list(out_shape) + list(h_shapes), scratch_shapes=list(scratch_shapes) + plan.sems(),
        input_output_aliases=aliases, compiler_params=_params(*(["arbitrary"] * len(grid))),
    )(*operands, *srcs)
    return res[:n_out], res[n_out:]


def _dot_nn(a, b):
    return lax.dot_general(a, b, (((1,), (0,)), ((), ())), preferred_element_type=F32)


def _dot_nt(a, b):
    return lax.dot_general(a, b, (((1,), (1,)), ((), ())), preferred_element_type=F32)


def _dot_tn(a, b):
    return lax.dot_general(a, b, (((0,), (0,)), ((), ())), preferred_element_type=F32)


def _sigmoid(x):
    return 0.5 * jnp.tanh(0.5 * x) + 0.5


def _rms_stats(x):
    r = lax.rsqrt(jnp.mean(x * x, axis=-1, keepdims=True) + NORM_EPS)
    return x * r, r


ROWS_WIDE = 16
MM_ROWS = 256


def _rms_bwd(dy, xhat, r, g):
    dg = jnp.sum(dy * xhat, axis=0, keepdims=True)
    dxh = dy * g
    dx = r * (dxh - xhat * jnp.mean(dxh * xhat, axis=-1, keepdims=True))
    return dx, dg


def _mm(a, b, *, mode, out_dtype, tm, tn, tk, name, add=None, scale=1.0, hosted=None):
    if mode == "nn":
        (m, k), (_, n) = a.shape, b.shape
    elif mode == "nt":
        (m, k), (n, _) = a.shape, b.shape
    else:
        (k, m), (_, n) = a.shape, b.shape
    assert m % tm == 0 and n % tn == 0 and k % tk == 0, (name, m, n, k, tm, tn, tk)
    nk = k // tk
    dot = {"nn": _dot_nn, "nt": _dot_nt, "tn": _dot_tn}[mode]
    a_spec = pl.BlockSpec((tk, tm), lambda i, j, kk: (kk, i)) if mode == "tn" else pl.BlockSpec((tm, tk), lambda i, j, kk: (i, kk))
    b_spec = pl.BlockSpec((tn, tk), lambda i, j, kk: (j, kk)) if mode == "nt" else pl.BlockSpec((tk, tn), lambda i, j, kk: (kk, j))
    o_spec = pl.BlockSpec((tm, tn), lambda i, j, kk: (i, j))
    has_add = add is not None

    def finish(prod, c_ref, o_ref):
        if scale != 1.0:
            prod = prod * scale
        o_ref[...] = ((c_ref[...] + prod) if has_add else prod).astype(out_dtype)

    def body(*refs):
        a_ref, b_ref = refs[:2]
        c_ref = refs[2] if has_add else None
        o_ref = refs[3] if has_add else refs[2]
        if nk == 1:
            finish(dot(a_ref[...], b_ref[...]), c_ref, o_ref)
            return
        acc_ref = refs[-1]
        kk = pl.program_id(2)

        @pl.when(kk == 0)
        def _():
            acc_ref[...] = jnp.zeros_like(acc_ref)

        acc_ref[...] += dot(a_ref[...], b_ref[...])

        @pl.when(kk == nk - 1)
        def _():
            finish(acc_ref[...], c_ref, o_ref)

    operands = (a, b, add) if has_add else (a, b)
    in_specs = [a_spec, b_spec] + ([o_spec] if has_add else [])
    (out,), got = _call(
        body, name=name, grid=(m // tm, n // tn, nk), in_specs=in_specs, out_specs=[o_spec],
        out_shape=[jax.ShapeDtypeStruct((m, n), out_dtype)], scratch_shapes=[pltpu.VMEM((tm, tn), F32)] if nk > 1 else [],
        operands=operands, sem=("parallel", "parallel", "arbitrary"), hosted=hosted)
    return out if hosted is None else (out, got)


def _rms_fwd(x, g, *, tm, name, hosted=None):
    t, d = x.shape

    def body(x_ref, g_ref, h_ref):
        xhat, _ = _rms_stats(x_ref[...])
        h_ref[...] = (xhat * g_ref[...]).astype(BF16)

    (h,), got = _call(
        body, name=name, grid=(t // tm,),
        in_specs=[pl.BlockSpec((tm, d), lambda i: (i, 0)), pl.BlockSpec((1, d), lambda i: (0, 0))],
        out_specs=[pl.BlockSpec((tm, d), lambda i: (i, 0))], out_shape=[jax.ShapeDtypeStruct((t, d), BF16)], scratch_shapes=[],
        operands=(x, g), sem=("parallel",), hosted=hosted)
    return h, got


def _ffn_fwd(x, g, wgT, wuT, wd, *, tm, hc, name, hosted=None, target=None):
    t, d = x.shape
    nj = DFF // hc
    with_loss = target is not None

    def body(*refs):
        x_ref, g_ref, wg_ref, wu_ref, wd_ref = refs[:5]
        t_ref = refs[5] if with_loss else None
        xo_ref, h_ref, a_ref, b_ref = refs[5 + with_loss:9 + with_loss]
        loss_ref = refs[9 + with_loss] if with_loss else None
        acc_ref = refs[-1]
        i, j = pl.program_id(0), pl.program_id(1)

        @pl.when(j == 0)
        def _():
            xhat, _ = _rms_stats(x_ref[...])
            h_ref[...] = (xhat * g_ref[...]).astype(BF16)
            acc_ref[...] = jnp.zeros_like(acc_ref)

        h = h_ref[...]
        a = _dot_nt(h, wg_ref[...])
        b = _dot_nt(h, wu_ref[...])
        a_ref[...] = a.astype(BF16)
        b_ref[...] = b.astype(BF16)
        s = (a * _sigmoid(a) * b).astype(BF16)
        acc_ref[...] += _dot_nn(s, wd_ref[...])

        if with_loss:
            @pl.when((i == 0) & (j == 0))
            def _():
                loss_ref[...] = jnp.zeros_like(loss_ref)

        @pl.when(j == nj - 1)
        def _():
            y = x_ref[...] + 0.5 * acc_ref[...]
            if with_loss:
                err = y - t_ref[...]
                xo_ref[...] = err * (1.0 / d)
                loss_ref[...] += jnp.sum(jnp.sum(err * err, axis=-1, keepdims=True), axis=0, keepdims=True) * (0.5 / d)
            else:
                xo_ref[...] = y

    row = pl.BlockSpec((tm, d), lambda i, j: (i, 0))
    vec = pl.BlockSpec((1, d), lambda i, j: (0, 0))
    wsp = pl.BlockSpec((hc, d), lambda i, j: (j, 0))
    hid = pl.BlockSpec((tm, hc), lambda i, j: (i, j))
    out_specs = [row, row, hid, hid] + ([pl.BlockSpec((1, 128), lambda i, j: (0, 0))] if with_loss else [])
    out_shape = [jax.ShapeDtypeStruct((t, d), F32), jax.ShapeDtypeStruct((t, d), BF16), jax.ShapeDtypeStruct((t, DFF), BF16),
                 jax.ShapeDtypeStruct((t, DFF), BF16)] + ([jax.ShapeDtypeStruct((1, 128), F32)] if with_loss else [])
    return _call(
        body, name=name, grid=(t // tm, nj), in_specs=[row, vec, wsp, wsp, wsp] + ([row] if with_loss else []),
        out_specs=out_specs, out_shape=out_shape, scratch_shapes=[pltpu.VMEM((tm, d), F32)],
        operands=(x, g, wgT, wuT, wd) + ((target,) if with_loss else ()),
        sem=("arbitrary" if with_loss else "parallel", "arbitrary"), hosted=hosted)


def _ffn_grads(dout, h, a, b, wd, *, tm, hc, name, hosted=None):
    t, d = dout.shape
    ni, nj = t // tm, DFF // hc

    def body(dout_ref, h_ref, a_ref, b_ref, wd_ref, da_ref, db_ref, dwg_ref, dwu_ref, dwd_ref,
             dy_all, h_all, ds_scr, s_scr, acc_g, acc_u, acc_d):
        j, i = pl.program_id(0), pl.program_id(1)
        rows_i = pl.ds(pl.multiple_of(i * tm, tm), tm)

        @pl.when(j == 0)
        def _():
            dy_all[rows_i, :] = (0.5 * dout_ref[...]).astype(BF16)
            h_all[rows_i, :] = h_ref[...]

        @pl.when(i == 0)
        def _():
            acc_g[...] = jnp.zeros_like(acc_g)
            acc_u[...] = jnp.zeros_like(acc_u)
            acc_d[...] = jnp.zeros_like(acc_d)

        def grad_rows(rows):
            ds = ds_scr[rows, :]
            av = a_ref[rows, :].astype(F32)
            bv = b_ref[rows, :].astype(F32)
            sg = _sigmoid(av)
            sl = av * sg
            s_scr[rows, :] = (sl * bv).astype(BF16)
            da_ref[rows, :] = (ds * bv * (sg + sl * (1.0 - sg))).astype(BF16)
            db_ref[rows, :] = (ds * sl).astype(BF16)

        for blk in range(tm // MM_ROWS):
            rs = slice(blk * MM_ROWS, (blk + 1) * MM_ROWS)
            ds_scr[rs, :] = _dot_nt(dy_all[pl.ds(pl.multiple_of(i * tm + blk * MM_ROWS, MM_ROWS), MM_ROWS), :], wd_ref[...])
            for c in range(MM_ROWS // ROWS_WIDE):
                grad_rows(slice(blk * MM_ROWS + c * ROWS_WIDE, blk * MM_ROWS + (c + 1) * ROWS_WIDE))

        dy_i = dy_all[rows_i, :]
        h_i = h_all[rows_i, :]
        acc_d[...] += _dot_tn(s_scr[...], dy_i)
        acc_g[...] += _dot_tn(da_ref[...], h_i)
        acc_u[...] += _dot_tn(db_ref[...], h_i)

        @pl.when(i == ni - 1)
        def _():
            dwg_ref[...] = acc_g[...].astype(BF16)
            dwu_ref[...] = acc_u[...].astype(BF16)
            dwd_ref[...] = acc_d[...].astype(BF16)

    first = pl.BlockSpec((tm, d), lambda j, i: (jnp.where(j == 0, i, 0), 0))
    hid = pl.BlockSpec((tm, hc), lambda j, i: (i, j))
    wsp = pl.BlockSpec((hc, d), lambda j, i: (j, 0))
    hid_shape = jax.ShapeDtypeStruct((t, DFF), BF16)
    w_shape = jax.ShapeDtypeStruct((DFF, d), BF16)
    return _call(
        body, name=name, grid=(nj, ni), in_specs=[first, first, hid, hid, wsp], out_specs=[hid, hid, wsp, wsp, wsp],
        out_shape=[hid_shape, hid_shape, w_shape, w_shape, w_shape],
        scratch_shapes=[pltpu.VMEM((t, d), BF16), pltpu.VMEM((t, d), BF16), pltpu.VMEM((tm, hc), F32), pltpu.VMEM((tm, hc), BF16),
                        pltpu.VMEM((hc, d), F32), pltpu.VMEM((hc, d), F32), pltpu.VMEM((hc, d), F32)],
        operands=(dout, h, a, b, wd), sem=("arbitrary", "arbitrary"), hosted=hosted)


def _proj_fwd(h, latT, convT, gateT, *, tm, name, hosted=None):
    t, d = h.shape

    def body(h_ref, wl_ref, wc_ref, wg_ref, lat_ref, conv_ref, gl_ref):
        hv = h_ref[...]
        lat_ref[...] = _dot_nt(hv, wl_ref[...]).astype(BF16)
        conv_ref[...] = _dot_nt(hv, wc_ref[...]).astype(BF16)
        gl_ref[...] = _dot_nt(hv, wg_ref[...]).astype(BF16)

    def rows(w):
        return pl.BlockSpec((tm, w), lambda i: (i, 0))

    def full(r):
        return pl.BlockSpec((r, d), lambda i: (0, 0))

    return _call(
        body, name=name, grid=(t // tm,), in_specs=[rows(d), full(LAT_PAD), full(CONV_COLS), full(GATE_COLS)],
        out_specs=[rows(LAT_PAD), rows(CONV_COLS), rows(GATE_COLS)],
        out_shape=[jax.ShapeDtypeStruct((t, LAT_PAD), BF16), jax.ShapeDtypeStruct((t, CONV_COLS), BF16),
                   jax.ShapeDtypeStruct((t, GATE_COLS), BF16)],
        scratch_shapes=[], operands=(h, latT, convT, gateT), sem=("parallel",), hosted=hosted)


def _proj_bwd(dlat, dconv3, dgl, latT, convT, gateT, x, g, dres, *, tm, name, hosted=None):
    t, d = x.shape

    def body(dl_ref, dc_ref, dg_ref, wl_ref, wc_ref, wg_ref, x_ref, g_ref, dres_ref, dx_ref, dgain_ref):
        @pl.when(pl.program_id(0) == 0)
        def _():
            dgain_ref[...] = jnp.zeros_like(dgain_ref)

        dh = _dot_nn(dl_ref[...], wl_ref[...]) + _dot_nn(dc_ref[...], wc_ref[...]) + _dot_nn(dg_ref[...], wg_ref[...])
        xhat, r = _rms_stats(x_ref[...])
        dx, dgain = _rms_bwd(dh, xhat, r, g_ref[...])
        dx_ref[...] = dres_ref[...] + dx
        dgain_ref[...] += dgain

    def rows(w):
        return pl.BlockSpec((tm, w), lambda i: (i, 0))

    def full(r):
        return pl.BlockSpec((r, d), lambda i: (0, 0))

    return _call(
        body, name=name, grid=(t // tm,),
        in_specs=[rows(LAT_PAD), rows(CONV_COLS), rows(GATE_COLS), full(LAT_PAD), full(CONV_COLS), full(GATE_COLS), rows(d), full(1), rows(d)],
        out_specs=[rows(d), full(1)], out_shape=[jax.ShapeDtypeStruct((t, d), F32), jax.ShapeDtypeStruct((1, d), F32)],
        scratch_shapes=[], operands=(dlat, dconv3, dgl, latT, convT, gateT, x, g, dres), sem=("arbitrary",), hosted=hosted)


def _ffn_up_bwd(da, db, wgT, wuT, x, g, dout, *, tm, name, hosted=None):
    t, d = x.shape

    def body(da_ref, db_ref, wg_ref, wu_ref, x_ref, g_ref, dout_ref, dx_ref, dg_ref):
        @pl.when(pl.program_id(0) == 0)
        def _():
            dg_ref[...] = jnp.zeros_like(dg_ref)

        dh = _dot_nn(da_ref[...], wg_ref[...]) + _dot_nn(db_ref[...], wu_ref[...])
        xhat, r = _rms_stats(x_ref[...])
        dx, dg = _rms_bwd(dh, xhat, r, g_ref[...])
        dx_ref[...] = dout_ref[...] + dx
        dg_ref[...] += dg

    row = pl.BlockSpec((tm, d), lambda i: (i, 0))
    vec = pl.BlockSpec((1, d), lambda i: (0, 0))
    hid = pl.BlockSpec((tm, DFF), lambda i: (i, 0))
    wsp = pl.BlockSpec((DFF, d), lambda i: (0, 0))
    return _call(
        body, name=name, grid=(t // tm,), in_specs=[hid, hid, wsp, wsp, row, vec, row], out_specs=[row, vec],
        out_shape=[jax.ShapeDtypeStruct((t, d), F32), jax.ShapeDtypeStruct((1, d), F32)], scratch_shapes=[],
        operands=(da, db, wgT, wuT, x, g, dout), sem=("arbitrary",), hosted=hosted)


HEAD_LANES = (slice(0, 32), slice(64, 80), None, slice(32, 64), slice(80, 96), None)


def _head_cols(a):
    def part(sl, width):
        if sl is None or sl.stop > a.shape[1]:
            return jnp.zeros((a.shape[0], width), a.dtype)
        return a[:, sl]

    return jnp.concatenate([part(sl, w) for sl, w in zip(HEAD_LANES, (32, 16, 16, 32, 16, 16))], axis=1)


def _head_cols_inv(a, dims):
    parts = [a[:, 0:32], a[:, 64:96]] + ([a[:, 32:48], a[:, 96:112]] if dims == QK_DIM else [])
    return jnp.concatenate(parts, axis=1)


def _rope_fwd(x, c, s):
    return x * c + pltpu.roll(x, HEAD_PAD // 2, 1) * s


def _rope_bwd(dy, c, s):
    return dy * c + pltpu.roll(dy * s, HEAD_PAD // 2, 1)


def _head_stats(x):
    r = lax.rsqrt(jnp.sum(x * x, axis=-1, keepdims=True) * (1.0 / QK_DIM) + NORM_EPS)
    return x * r, r


def _mla_prep_fwd(lat, gq, gkv, ghq, ghk, wq, wk, wv, rc, rs, *, tm, name):
    t = lat.shape[0]

    def body(lat_ref, gq_ref, gkv_ref, ghq_ref, ghk_ref, wq_ref, wk_ref, wv_ref, c_ref, s_ref,
             q_ref, k_ref, v_ref, qn_ref, ckv_ref):
        lat_v = lat_ref[...]
        qhat, _ = _rms_stats(lat_v[:, :Q_LORA].astype(F32))
        qn = (qhat * gq_ref[...]).astype(BF16)
        khat, _ = _rms_stats(lat_v[:, Q_LORA:Q_LORA + KV_LORA].astype(F32))
        ckv = (khat * gkv_ref[...]).astype(BF16)
        ckv_ext = jnp.concatenate([ckv, lat_v[:, Q_LORA + KV_LORA:]], axis=1)
        qn_ref[...] = qn
        ckv_ref[...] = ckv_ext
        q_pre = _dot_nn(qn, wq_ref[...])
        k_pre = _dot_nn(ckv_ext, wk_ref[...])
        v_ref[...] = _dot_nn(ckv, wv_ref[...]).astype(BF16)
        c, s = c_ref[...], s_ref[...]
        for h in range(N_HEADS):
            hs = slice(h * HEAD_PAD, (h + 1) * HEAD_PAD)
            xq, _ = _head_stats(q_pre[:, hs])
            q_ref[:, hs] = _rope_fwd(xq * ghq_ref[...], c, s).astype(BF16)
            xk, _ = _head_stats(k_pre[:, hs])
            k_ref[:, hs] = _rope_fwd(xk * ghk_ref[...], c, s).astype(BF16)

    def row(w):
        return pl.BlockSpec((tm, w), lambda i: (i, 0))

    def full(r, w):
        return pl.BlockSpec((r, w), lambda i: (0, 0))

    wide = jax.ShapeDtypeStruct((t, D), BF16)
    lat3 = jax.ShapeDtypeStruct((t, Q_LORA), BF16)
    return pl.pallas_call(
        body, name=name, grid=(t // tm,),
        in_specs=[row(LAT_PAD), full(1, Q_LORA), full(1, KV_LORA), full(1, HEAD_PAD), full(1, HEAD_PAD),
                  full(Q_LORA, D), full(Q_LORA, D), full(KV_LORA, D), row(HEAD_PAD), row(HEAD_PAD)],
        out_specs=[row(D), row(D), row(D), row(Q_LORA), row(Q_LORA)],
        out_shape=[wide, wide, wide, lat3, lat3],
        compiler_params=_params("parallel"),
    )(lat, gq, gkv, ghq, ghk, wq, wk, wv, rc, rs)


def _mla_prep_bwd(dq, dk, dv, lat, qn, ckv_ext, gq, gkv, ghq, ghk, wq, wk, wv, rc, rs, *, tm, name):
    t = lat.shape[0]

    def body(dq_ref, dk_ref, dv_ref, lat_ref, qn_ref, ckv_ref, gq_ref, gkv_ref, ghq_ref, ghk_ref, wq_ref, wk_ref, wv_ref,
             c_ref, s_ref, dlat_ref, dqp_ref, dkp_ref, dgq_ref, dgkv_ref, dghq_ref, dghk_ref):
        @pl.when(pl.program_id(0) == 0)
        def _():
            dgq_ref[...] = jnp.zeros_like(dgq_ref)
            dgkv_ref[...] = jnp.zeros_like(dgkv_ref)
            dghq_ref[...] = jnp.zeros_like(dghq_ref)
            dghk_ref[...] = jnp.zeros_like(dghk_ref)

        c, s = c_ref[...], s_ref[...]
        q_pre = _dot_nn(qn_ref[...], wq_ref[...])
        k_pre = _dot_nn(ckv_ref[...], wk_ref[...])

        def heads(pre, dy_ref, gh_ref, dgh_ref, out_ref):
            dgh = jnp.zeros((1, HEAD_PAD), F32)
            for h in range(N_HEADS):
                hs = slice(h * HEAD_PAD, (h + 1) * HEAD_PAD)
                d = _rope_bwd(dy_ref[:, hs].astype(F32), c, s)
                xhat, r = _head_stats(pre[:, hs])
                dgh = dgh + jnp.sum(d * xhat, axis=0, keepdims=True)
                dxh = d * gh_ref[...]
                dx = r * (dxh - xhat * (jnp.sum(dxh * xhat, axis=-1, keepdims=True) * (1.0 / QK_DIM)))
                out_ref[:, hs] = dx.astype(BF16)
            dgh_ref[...] += dgh

        heads(q_pre, dq_ref, ghq_ref, dghq_ref, dqp_ref)
        heads(k_pre, dk_ref, ghk_ref, dghk_ref, dkp_ref)
        dqn = _dot_nt(dqp_ref[...], wq_ref[...])
        dce = _dot_nt(dkp_ref[...], wk_ref[...])
        dckv = dce[:, :KV_LORA] + _dot_nt(dv_ref[...], wv_ref[...])
        lat_v = lat_ref[...]
        qhat, rq = _rms_stats(lat_v[:, :Q_LORA].astype(F32))
        dql, dgq = _rms_bwd(dqn, qhat, rq, gq_ref[...])
        khat, rk = _rms_stats(lat_v[:, Q_LORA:Q_LORA + KV_LORA].astype(F32))
        dkl, dgkv = _rms_bwd(dckv, khat, rk, gkv_ref[...])
        dgq_ref[...] += dgq
        dgkv_ref[...] += dgkv
        dlat_ref[...] = jnp.concatenate([dql, dkl, dce[:, KV_LORA:]], axis=1).astype(BF16)

    def row(w):
        return pl.BlockSpec((tm, w), lambda i: (i, 0))

    def full(r, w):
        return pl.BlockSpec((r, w), lambda i: (0, 0))

    return pl.pallas_call(
        body, name=name, grid=(t // tm,),
        in_specs=[row(D), row(D), row(D), row(LAT_PAD), row(Q_LORA), row(Q_LORA), full(1, Q_LORA), full(1, KV_LORA),
                  full(1, HEAD_PAD), full(1, HEAD_PAD), full(Q_LORA, D), full(Q_LORA, D), full(KV_LORA, D),
                  row(HEAD_PAD), row(HEAD_PAD)],
        out_specs=[row(LAT_PAD), row(D), row(D), full(1, Q_LORA), full(1, KV_LORA), full(1, HEAD_PAD), full(1, HEAD_PAD)],
        out_shape=[jax.ShapeDtypeStruct((t, LAT_PAD), BF16), jax.ShapeDtypeStruct((t, D), BF16), jax.ShapeDtypeStruct((t, D), BF16),
                   jax.ShapeDtypeStruct((1, Q_LORA), F32), jax.ShapeDtypeStruct((1, KV_LORA), F32),
                   jax.ShapeDtypeStruct((1, HEAD_PAD), F32), jax.ShapeDtypeStruct((1, HEAD_PAD), F32)],
        compiler_params=_params("arbitrary"),
    )(dq, dk, dv, lat, qn, ckv_ext, gq, gkv, ghq, ghk, wq, wk, wv, rc, rs)


def _causal_keep(tq):
    r = lax.broadcasted_iota(jnp.int32, (tq, tq), 0)
    c = lax.broadcasted_iota(jnp.int32, (tq, tq), 1)
    return c <= r


def _flash_fwd(q, k, v, *, n_seq, seq, tq, name, hosted=None):
    nq = seq // tq

    def body(q_ref, k_ref, v_ref, o_ref, lse_ref):
        qi = pl.program_id(2)
        qv = q_ref[...]

        def step(j, carry, masked):
            m, l, acc = carry
            kj = k_ref[pl.ds(pl.multiple_of(j * tq, tq), tq), :]
            vj = v_ref[pl.ds(pl.multiple_of(j * tq, tq), tq), :]
            s = _dot_nt(qv, kj) * ATTN_SCALE
            if masked:
                s = jnp.where(_causal_keep(tq), s, NEG)
            m_new = jnp.maximum(m, jnp.max(s, axis=-1, keepdims=True))
            alpha = jnp.exp(m - m_new)
            p = jnp.exp(s - m_new)
            l = alpha * l + jnp.sum(p, axis=-1, keepdims=True)
            acc = alpha * acc + _dot_nn(p.astype(BF16), vj)
            return m_new, l, acc

        init = (jnp.full((tq, 1), NEG, F32), jnp.zeros((tq, 1), F32), jnp.zeros((tq, HEAD_PAD), F32))
        carry = lax.fori_loop(0, qi, lambda j, cr: step(j, cr, False), init)
        m, l, acc = step(qi, carry, True)
        o_ref[...] = (acc / l).astype(BF16)
        lse_ref[...] = jnp.broadcast_to(m + jnp.log(l), (tq, HEAD_PAD))

    qspec = pl.BlockSpec((tq, HEAD_PAD), lambda b, h, i: (b * nq + i, h))
    kspec = pl.BlockSpec((seq, HEAD_PAD), lambda b, h, i: (b, h))
    t = n_seq * seq
    return _call(
        body, name=name, grid=(n_seq, N_HEADS, nq), in_specs=[qspec, kspec, kspec], out_specs=[qspec, qspec],
        out_shape=[jax.ShapeDtypeStruct((t, D), BF16), jax.ShapeDtypeStruct((t, D), F32)], scratch_shapes=[],
        operands=(q, k, v), sem=("parallel", "parallel", "arbitrary"), hosted=hosted)


def _flash_bwd(q, k, v, o, lse, do, *, n_seq, seq, tq, name, hosted=None):
    nq = seq // tq

    def body(q_ref, k_ref, v_ref, o_ref, lse_ref, do_ref, dq_ref, dk_ref, dv_ref, dk_acc, dv_acc):
        j = pl.program_id(2)

        @pl.when(j == 0)
        def _():
            dq_ref[...] = jnp.zeros_like(dq_ref)

        dk_acc[...] = jnp.zeros_like(dk_acc)
        dv_acc[...] = jnp.zeros_like(dv_acc)
        kv = k_ref[...]
        vv = v_ref[...]

        def step(i, masked):
            rows = pl.ds(pl.multiple_of(i * tq, tq), tq)
            qi = q_ref[rows, :]
            doi = do_ref[rows, :]
            delta = jnp.sum(doi.astype(F32) * o_ref[rows, :].astype(F32), axis=-1, keepdims=True)
            s = _dot_nt(qi, kv) * ATTN_SCALE
            p = jnp.exp(s - lse_ref[rows, :][:, :1])
            if masked:
                p = jnp.where(_causal_keep(tq), p, 0.0)
            dv_acc[...] += _dot_tn(p.astype(BF16), doi)
            dp = _dot_nt(doi, vv)
            ds = (p * (dp - delta) * ATTN_SCALE).astype(BF16)
            dk_acc[...] += _dot_tn(ds, qi)
            dq_ref[rows, :] += _dot_nn(ds, kv)

        step(j, True)

        def loop_body(i, carry):
            step(i, False)
            return carry

        lax.fori_loop(j + 1, nq, loop_body, 0)
        dk_ref[...] = dk_acc[...]
        dv_ref[...] = dv_acc[...].astype(BF16)

    full = pl.BlockSpec((seq, HEAD_PAD), lambda b, h, j: (b, h))
    tile = pl.BlockSpec((tq, HEAD_PAD), lambda b, h, j: (b * nq + j, h))
    t = n_seq * seq
    return _call(
        body, name=name, grid=(n_seq, N_HEADS, nq), in_specs=[full, tile, tile, full, full, full],
        out_specs=[full, tile, tile],
        out_shape=[jax.ShapeDtypeStruct((t, D), F32), jax.ShapeDtypeStruct((t, D), F32), jax.ShapeDtypeStruct((t, D), BF16)],
        scratch_shapes=[pltpu.VMEM((tq, HEAD_PAD), F32), pltpu.VMEM((tq, HEAD_PAD), F32)],
        operands=(q, k, v, o, lse, do), sem=("parallel", "parallel", "arbitrary"), hosted=hosted)


CONV_CB = 256


def _shift_down(u, k, row):
    return jnp.where(row >= k, pltpu.roll(u, k, 0), 0.0)


def _shift_up(u, k, row, n):
    return jnp.where(row < n - k, pltpu.roll(u, n - k, 0), 0.0)


def _conv_fwd(conv3, cw, *, n_seq, seq, name, hosted=None):
    def body(c_ref, w_ref, p_ref):
        blk = c_ref[...].astype(F32)
        xc, gb, gc = blk[:, :CONV_CB], blk[:, CONV_CB:2 * CONV_CB], blk[:, 2 * CONV_CB:]
        row = lax.broadcasted_iota(jnp.int32, (seq, CONV_CB), 0)
        u = gc * xc
        z = w_ref[0:1, :] * _shift_down(u, 2, row) + w_ref[1:2, :] * _shift_down(u, 1, row) + w_ref[2:3, :] * u
        p_ref[...] = (gb * z).astype(BF16)

    (p,), got = _call(
        body, name=name, grid=(n_seq, D // CONV_CB),
        in_specs=[pl.BlockSpec((seq, 3 * CONV_CB), lambda b, j: (b, j)), pl.BlockSpec((3, CONV_CB), lambda b, j: (0, j))],
        out_specs=[pl.BlockSpec((seq, CONV_CB), lambda b, j: (b, j))],
        out_shape=[jax.ShapeDtypeStruct((n_seq * seq, D), BF16)], scratch_shapes=[],
        operands=(conv3, cw), sem=("parallel", "parallel"), hosted=hosted)
    return p, got


def _conv_bwd(dp, conv3, cw, *, n_seq, seq, name):
    def body(dp_ref, c_ref, w_ref, dc_ref, dw_ref):
        @pl.when(pl.program_id(1) == 0)
        def _():
            dw_ref[...] = jnp.zeros_like(dw_ref)

        blk = c_ref[...].astype(F32)
        xc, gb, gc = blk[:, :CONV_CB], blk[:, CONV_CB:2 * CONV_CB], blk[:, 2 * CONV_CB:]
        row = lax.broadcasted_iota(jnp.int32, (seq, CONV_CB), 0)
        w0, w1, w2 = w_ref[0:1, :], w_ref[1:2, :], w_ref[2:3, :]
        u = gc * xc
        u1 = _shift_down(u, 1, row)
        u2 = _shift_down(u, 2, row)
        z = w0 * u2 + w1 * u1 + w2 * u
        dpv = dp_ref[...].astype(F32)
        dz = dpv * gb
        du = w2 * dz + w1 * _shift_up(dz, 1, row, seq) + w0 * _shift_up(dz, 2, row, seq)
        dc_ref[...] = jnp.concatenate([du * gc, dpv * z, du * xc], axis=1).astype(BF16)
        dw_ref[0:1, :] += jnp.sum(dz * u2, axis=0, keepdims=True)
        dw_ref[1:2, :] += jnp.sum(dz * u1, axis=0, keepdims=True)
        dw_ref[2:3, :] += jnp.sum(dz * u, axis=0, keepdims=True)

    return pl.pallas_call(
        body, name=name, grid=(D // CONV_CB, n_seq),
        in_specs=[pl.BlockSpec((seq, CONV_CB), lambda j, b: (b, j)), pl.BlockSpec((seq, 3 * CONV_CB), lambda j, b: (b, j)),
                  pl.BlockSpec((3, CONV_CB), lambda j, b: (0, j))],
        out_specs=[pl.BlockSpec((seq, 3 * CONV_CB), lambda j, b: (b, j)), pl.BlockSpec((3, CONV_CB), lambda j, b: (0, j))],
        out_shape=[jax.ShapeDtypeStruct((n_seq * seq, CONV_COLS), BF16), jax.ShapeDtypeStruct((3, D), F32)],
        compiler_params=_params("parallel", "arbitrary"),
    )(dp, conv3, cw)


def _merge_fwd(o, p, gl, bias, x1, wpa, wpc, wout, *, tm, name, hosted=None):
    t = x1.shape[0]

    def body(o_ref, p_ref, gl_ref, b_ref, x_ref, wpa_ref, wpc_ref, wout_ref, x2_ref, mg_ref, ya_ref, yb_ref):
        ya = _dot_nn(o_ref[...], wpa_ref[...])
        yb = _dot_nn(p_ref[...], wpc_ref[...])
        gates = _sigmoid(gl_ref[...].astype(F32) + b_ref[...])
        merged = (gates[:, :D] * ya + gates[:, D:] * yb).astype(BF16)
        ya_ref[...] = ya.astype(BF16)
        yb_ref[...] = yb.astype(BF16)
        mg_ref[...] = merged
        x2_ref[...] = x_ref[...] + _dot_nn(merged, wout_ref[...])

    row = pl.BlockSpec((tm, D), lambda i: (i, 0))
    row2 = pl.BlockSpec((tm, GATE_COLS), lambda i: (i, 0))
    wsp = pl.BlockSpec((D, D), lambda i: (0, 0))
    wide = jax.ShapeDtypeStruct((t, D), BF16)
    return _call(
        body, name=name, grid=(t // tm,),
        in_specs=[row, row, row2, pl.BlockSpec((1, GATE_COLS), lambda i: (0, 0)), row, wsp, wsp, wsp],
        out_specs=[row, row, row, row], out_shape=[jax.ShapeDtypeStruct((t, D), F32), wide, wide, wide], scratch_shapes=[],
        operands=(o, p, gl, bias, x1, wpa, wpc, wout), sem=("parallel",), hosted=hosted)


def _merge_bwd(dx2, ya, yb, gl, bias, wpa, wpc, wout, *, tm, name, hosted=None):
    t = dx2.shape[0]

    def body(dx_ref, ya_ref, yb_ref, gl_ref, b_ref, wpa_ref, wpc_ref, wout_ref,
             dxb_ref, dya_ref, dyb_ref, dgl_ref, do_ref, dp_ref, db_ref):
        @pl.when(pl.program_id(0) == 0)
        def _():
            db_ref[...] = jnp.zeros_like(db_ref)

        dxb = dx_ref[...].astype(BF16)
        dxb_ref[...] = dxb
        dm = _dot_nt(dxb, wout_ref[...])
        gates = _sigmoid(gl_ref[...].astype(F32) + b_ref[...])
        ga, gb = gates[:, :D], gates[:, D:]
        dya = (dm * ga).astype(BF16)
        dyb = (dm * gb).astype(BF16)
        dya_ref[...] = dya
        dyb_ref[...] = dyb
        dgl = jnp.concatenate([dm * ya_ref[...].astype(F32) * ga * (1.0 - ga),
                               dm * yb_ref[...].astype(F32) * gb * (1.0 - gb)], axis=1)
        dgl_ref[...] = dgl.astype(BF16)
        db_ref[...] += jnp.sum(dgl, axis=0, keepdims=True)
        do_ref[...] = _dot_nt(dya, wpa_ref[...]).astype(BF16)
        dp_ref[...] = _dot_nt(dyb, wpc_ref[...]).astype(BF16)

    row = pl.BlockSpec((tm, D), lambda i: (i, 0))
    row2 = pl.BlockSpec((tm, GATE_COLS), lambda i: (i, 0))
    vec2 = pl.BlockSpec((1, GATE_COLS), lambda i: (0, 0))
    wsp = pl.BlockSpec((D, D), lambda i: (0, 0))
    wide = jax.ShapeDtypeStruct((t, D), BF16)
    return _call(
        body, name=name, grid=(t // tm,), in_specs=[row, row, row, row2, vec2, wsp, wsp, wsp],
        out_specs=[row, row, row, row2, row, row, vec2],
        out_shape=[wide, wide, wide, jax.ShapeDtypeStruct((t, GATE_COLS), BF16), wide, wide,
                   jax.ShapeDtypeStruct((1, GATE_COLS), F32)],
        scratch_shapes=[], operands=(dx2, ya, yb, gl, bias, wpa, wpc, wout), sem=("arbitrary",), hosted=hosted)


def _adamw(w, g, m, v, *, name):
    rows, cols = w.shape
    tr = max([c for c in range(8, 513, 8) if rows % c == 0], default=rows)
    c1 = 1.0 / (1.0 - ADAM_B1 ** ADAM_STEP)
    c2 = 1.0 / (1.0 - ADAM_B2 ** ADAM_STEP)

    def body(w_ref, g_ref, m_ref, v_ref, d_ref, nm_ref, nv_ref):
        gv = g_ref[...]
        nm = ADAM_B1 * m_ref[...] + (1.0 - ADAM_B1) * gv
        nv = ADAM_B2 * v_ref[...] + (1.0 - ADAM_B2) * (gv * gv)
        nm_ref[...] = nm
        nv_ref[...] = nv
        d_ref[...] = -ADAM_LR * ((nm * c1) / (jnp.sqrt(nv * c2) + ADAM_EPS) + ADAM_WD * w_ref[...])

    spec = pl.BlockSpec((tr, cols), lambda i: (i, 0))
    shp = jax.ShapeDtypeStruct((rows, cols), F32)
    return pl.pallas_call(
        body, name=name, grid=(rows // tr,), in_specs=[spec] * 4, out_specs=[spec] * 3, out_shape=[shp] * 3,
        compiler_params=_params("parallel"),
    )(w, g, m, v)


def _place():
    return lax.axis_index("x"), lax.axis_index("y"), lax.axis_index("c")


def _other_chips(x, y):
    return [(1 - x, y), (x, 1 - y), (1 - x, 1 - y)]


def _remote(src, dst, send, recv, dev):
    return pltpu.make_async_remote_copy(src_ref=src, dst_ref=dst, send_sem=send, recv_sem=recv, device_id=dev, device_id_type=MESH)


def _gather_chips_plan(n):
    def start(srcs, dsts, send, recv, local):
        x, y, cc = _place()
        me = 4 * x + 2 * y + cc
        for a in range(n):
            pltpu.make_async_copy(srcs[a], dsts[a].at[me], local.at[a]).start()
            for k, (px, py) in enumerate(_other_chips(x, y)):
                _remote(srcs[a], dsts[a].at[me], send.at[3 * a + k], recv.at[3 * a + k], (px, py, cc)).start()

    def wait(srcs, dsts, send, recv, local):
        x, y, cc = _place()
        me = 4 * x + 2 * y + cc
        for a in range(n):
            for k, (px, py) in enumerate(_other_chips(x, y)):
                _remote(srcs[a], dsts[a].at[4 * px + 2 * py + cc], send.at[3 * a + k], recv.at[3 * a + k], (px, py, cc)).wait_recv()
        for a in range(n):
            for k, (px, py) in enumerate(_other_chips(x, y)):
                _remote(srcs[a], dsts[a].at[me], send.at[3 * a + k], recv.at[3 * a + k], (px, py, cc)).wait_send()
            pltpu.make_async_copy(srcs[a], dsts[a].at[me], local.at[a]).wait()

    return _Plan(start, wait, 3 * n, n)


def _scatter_chips_plan(n):
    def start(srcs, dsts, send, recv, local):
        x, y, cc = _place()
        for a in range(n):
            for k, (px, py) in enumerate(_other_chips(x, y)):
                _remote(srcs[a].at[2 * px + py], dsts[a].at[k], send.at[3 * a + k], recv.at[3 * a + k], (px, py, cc)).start()

    def wait(srcs, dsts, send, recv, local):
        x, y, cc = _place()
        for a in range(n):
            for k, (px, py) in enumerate(_other_chips(x, y)):
                _remote(srcs[a].at[k], dsts[a].at[k], send.at[3 * a + k], recv.at[3 * a + k], (px, py, cc)).wait_recv()
        for a in range(n):
            for k, (px, py) in enumerate(_other_chips(x, y)):
                _remote(srcs[a].at[k], dsts[a].at[k], send.at[3 * a + k], recv.at[3 * a + k], (px, py, cc)).wait_send()

    return _Plan(start, wait, 3 * n, 0)


def _gather_shapes(blocks):
    return [jax.ShapeDtypeStruct((N_DEV,) + b.shape, b.dtype) for b in blocks]


def _scatter_shapes(parts):
    return [jax.ShapeDtypeStruct((3,) + p.shape[1:], p.dtype) for p in parts]


def _gather_sibling_plan(n):
    def start(srcs, dsts, send, recv, local):
        x, y, cc = _place()
        for a in range(n):
            for q in range(4):
                _remote(srcs[a].at[2 * q + cc], dsts[a].at[2 * q + cc], send.at[4 * a + q], recv.at[4 * a + q], (x, y, 1 - cc)).start()

    def wait(srcs, dsts, send, recv, local):
        x, y, cc = _place()
        for a in range(n):
            for q in range(4):
                _remote(srcs[a].at[2 * q + cc], dsts[a].at[2 * q + 1 - cc], send.at[4 * a + q], recv.at[4 * a + q],
                        (x, y, 1 - cc)).wait_recv()
        for a in range(n):
            for q in range(4):
                _remote(srcs[a].at[2 * q + cc], dsts[a].at[2 * q + cc], send.at[4 * a + q], recv.at[4 * a + q],
                        (x, y, 1 - cc)).wait_send()

    return _Plan(start, wait, 4 * n, 0, in_place=True)


def _scatter_sibling_plan(n):
    def start(srcs, dsts, send, recv, local):
        x, y, cc = _place()
        for a in range(n):
            for q in range(4):
                _remote(srcs[a].at[2 * q + 1 - cc], dsts[a].at[q], send.at[4 * a + q], recv.at[4 * a + q], (x, y, 1 - cc)).start()

    def wait(srcs, dsts, send, recv, local):
        x, y, cc = _place()
        for a in range(n):
            for q in range(4):
                _remote(srcs[a].at[q], dsts[a].at[q], send.at[4 * a + q], recv.at[4 * a + q], (x, y, 1 - cc)).wait_recv()
        for a in range(n):
            for q in range(4):
                _remote(srcs[a].at[q], dsts[a].at[q], send.at[4 * a + q], recv.at[4 * a + q], (x, y, 1 - cc)).wait_send()

    return _Plan(start, wait, 4 * n, 0)


def _same_shapes(arrs):
    return [jax.ShapeDtypeStruct(a.shape, a.dtype) for a in arrs]


def _halved_shapes(parts):
    return [jax.ShapeDtypeStruct((4,) + p.shape[1:], p.dtype) for p in parts]


def _run_plan(plan, srcs, out_shapes, *, name):
    n_in, n_out = len(srcs), len(out_shapes)

    def body(*refs):
        h_in, h_out, sems = refs[:n_in], refs[n_in:n_in + n_out], refs[n_in + n_out:]
        plan.start(h_in, h_out, *sems)
        plan.wait(h_in, h_out, *sems)

    return pl.pallas_call(body, name=name, in_specs=[ANY] * n_in, out_specs=[ANY] * n_out, out_shape=list(out_shapes),
                          input_output_aliases={a: a for a in range(n_in)} if plan.in_place else {},
                          scratch_shapes=plan.sems())(*srcs)


SEM = pl.BlockSpec(memory_space=pltpu.SEMAPHORE)
HBM = pl.BlockSpec(memory_space=pltpu.HBM)
SIDE_EFFECT = pltpu.CompilerParams(has_side_effects=pltpu.SideEffectType.DATAFLOW_SIDE_EFFECTING)


def _plan_start(plan, blocks, land_shapes, *, name):
    n = len(blocks)
    lands = [lax.empty(s.shape, s.dtype) for s in land_shapes]

    def body(*refs):
        srcs, sems, lands_out, token = refs[:n], refs[2 * n:2 * n + 3], refs[3 * n + 3:4 * n + 3], refs[4 * n + 3]
        plan.start(srcs, lands_out, *sems)
        token[...] = jnp.zeros_like(token)

    out_shape = ([s for s in plan.sems()] + [pltpu.HBM(b.shape, b.dtype) for b in blocks]
                 + [pltpu.HBM(l.shape, l.dtype) for l in lands] + [jax.ShapeDtypeStruct((8, 128), F32)])
    res = pl.pallas_call(
        body, name=name, in_specs=[HBM] * (2 * n), out_specs=[SEM] * 3 + [HBM] * (2 * n) + [pl.BlockSpec(memory_space=pltpu.VMEM)],
        out_shape=out_shape, input_output_aliases={a: 3 + a for a in range(2 * n)}, compiler_params=SIDE_EFFECT,
    )(*[pltpu.with_memory_space_constraint(a, pltpu.HBM) for a in list(blocks) + lands])
    return res[:3], res[3:3 + n], res[3 + n:3 + 2 * n], res[3 + 2 * n]


def _plan_wait(plan, sems, blocks, lands, after, *, name):
    n = len(blocks)

    def body(*refs):
        plan.wait(refs[:n], refs[n:2 * n], *refs[2 * n:2 * n + 3])

    res = pl.pallas_call(
        body, name=name, in_specs=[HBM] * (2 * n) + [SEM] * 3 + [ANY], out_specs=[HBM] * (2 * n),
        out_shape=[pltpu.HBM(a.shape, a.dtype) for a in list(blocks) + list(lands)],
        input_output_aliases={a: a for a in range(2 * n)}, compiler_params=SIDE_EFFECT,
    )(*blocks, *lands, *sems, after)
    return list(res[:n]), list(res[n:])


def _sum_sibling(p, q, core, *, name):
    _, r, c = p.shape

    def body(core_ref, p_ref, q_ref, o_ref):
        o_ref[...] = (p_ref[...].astype(F32) + q_ref[...].astype(F32)).astype(BF16)

    grid_spec = pltpu.PrefetchScalarGridSpec(
        num_scalar_prefetch=1, grid=(4,),
        in_specs=[pl.BlockSpec((1, r, c), lambda ch, core_ref: (2 * ch + core_ref[0], 0, 0)),
                  pl.BlockSpec((1, r, c), lambda ch, core_ref: (ch, 0, 0))],
        out_specs=pl.BlockSpec((1, r, c), lambda ch, core_ref: (ch, 0, 0)))
    return pl.pallas_call(
        body, name=name, grid_spec=grid_spec, out_shape=jax.ShapeDtypeStruct((4, r, c), BF16),
        compiler_params=_params("parallel"),
    )(core, p, q)


def _sum_chips(s1, r2, chip, *, name):
    _, r, c = s1.shape

    def body(chip_ref, s_ref, r_ref, o_ref):
        acc = s_ref[0].astype(F32)
        for k in range(3):
            acc = acc + r_ref[k].astype(F32)
        o_ref[...] = acc

    grid_spec = pltpu.PrefetchScalarGridSpec(
        num_scalar_prefetch=1, grid=(1,),
        in_specs=[pl.BlockSpec((1, r, c), lambda i, chip_ref: (chip_ref[0], 0, 0)),
                  pl.BlockSpec((3, r, c), lambda i, chip_ref: (0, 0, 0))],
        out_specs=pl.BlockSpec((r, c), lambda i, chip_ref: (0, 0)))
    return pl.pallas_call(
        body, name=name, grid_spec=grid_spec, out_shape=jax.ShapeDtypeStruct((r, c), F32),
        compiler_params=_params("arbitrary"),
    )(chip, s1, r2)


def _sum_adamw(s1, r2, chip, w, m, v, *, name):
    _, r, c = s1.shape
    c1 = 1.0 / (1.0 - ADAM_B1 ** ADAM_STEP)
    c2 = 1.0 / (1.0 - ADAM_B2 ** ADAM_STEP)

    def body(chip_ref, s_ref, r_ref, w_ref, m_ref, v_ref, g_ref, d_ref, nm_ref, nv_ref):
        gv = s_ref[0].astype(F32)
        for k in range(3):
            gv = gv + r_ref[k].astype(F32)
        g_ref[...] = gv
        nm = ADAM_B1 * m_ref[...] + (1.0 - ADAM_B1) * gv
        nv = ADAM_B2 * v_ref[...] + (1.0 - ADAM_B2) * (gv * gv)
        nm_ref[...] = nm
        nv_ref[...] = nv
        d_ref[...] = -ADAM_LR * ((nm * c1) / (jnp.sqrt(nv * c2) + ADAM_EPS) + ADAM_WD * w_ref[...])

    flat = pl.BlockSpec((r, c), lambda i, chip_ref: (0, 0))
    grid_spec = pltpu.PrefetchScalarGridSpec(
        num_scalar_prefetch=1, grid=(1,),
        in_specs=[pl.BlockSpec((1, r, c), lambda i, chip_ref: (chip_ref[0], 0, 0)),
                  pl.BlockSpec((3, r, c), lambda i, chip_ref: (0, 0, 0)), flat, flat, flat],
        out_specs=[flat] * 4)
    return pl.pallas_call(
        body, name=name, grid_spec=grid_spec, out_shape=[jax.ShapeDtypeStruct((r, c), F32)] * 4,
        compiler_params=_params("arbitrary"),
    )(chip, s1, r2, w, m, v)


def _small_exchange(v, *, reduce, name):
    r, c = v.shape

    def body(x_ref, o_ref, *rest):
        if reduce:
            buf_ref, send_sems, recv_sems = rest
        else:
            buf_ref = o_ref
            send_sems, recv_sems = rest
        x, y, cc = _place()
        me = 4 * x + 2 * y + cc

        def peer(k):
            return ((1 - x) if k & 4 else x, (1 - y) if k & 2 else y, (1 - cc) if k & 1 else cc)

        buf_ref[me] = x_ref[...]
        sends = []
        for k in range(1, N_DEV):
            cp = pltpu.make_async_remote_copy(src_ref=x_ref, dst_ref=buf_ref.at[me], send_sem=send_sems.at[k - 1],
                                              recv_sem=recv_sems.at[k - 1], device_id=peer(k), device_id_type=MESH)
            cp.start()
            sends.append(cp)
        for k in range(1, N_DEV):
            px, py, pc = peer(k)
            pltpu.make_async_remote_copy(src_ref=x_ref, dst_ref=buf_ref.at[4 * px + 2 * py + pc], send_sem=send_sems.at[k - 1],
                                         recv_sem=recv_sems.at[k - 1], device_id=peer(k), device_id_type=MESH).wait_recv()
        for cp in sends:
            cp.wait_send()
        if reduce:
            acc = buf_ref[0]
            for s in range(1, N_DEV):
                acc = acc + buf_ref[s]
            o_ref[...] = acc

    vm = pl.BlockSpec(memory_space=pltpu.VMEM)
    sems = [pltpu.SemaphoreType.DMA((N_DEV - 1,)), pltpu.SemaphoreType.DMA((N_DEV - 1,))]
    if reduce:
        out_shape, scratch = jax.ShapeDtypeStruct((r, c), F32), [pltpu.VMEM((N_DEV, r, c), F32)] + sems
    else:
        out_shape, scratch = jax.ShapeDtypeStruct((N_DEV, r, c), F32), sems
    return pl.pallas_call(body, name=name, in_specs=[vm], out_specs=vm, out_shape=out_shape, scratch_shapes=scratch)(v)


def _rows(a):
    return a.reshape(-1, D)


def _pad_cols(a, to):
    return jnp.pad(a, ((0, 0), (0, to - a.shape[1])))


def _pack_weights(w):
    parts = {
        "w_inT": jnp.pad(w["w_in"].T, ((0, IN_SHARD_PAD - IN_SHARD), (0, 0))),
        "w_uq": _rows(_head_cols(w["w_uq"])), "w_uk": _rows(_head_cols(w["w_uk"])),
        "w_uv": _rows(_pad_cols(w["w_uv"], HEAD_PAD)), "w_pa": _rows(w["w_proj_attn"]),
        "w_pc": w["w_proj_conv"], "w_out": w["w_out"],
    }
    return [jnp.concatenate([parts[n].astype(BF16) for n, _ in group], axis=0) for group in PACK]


def _cols_from_shards(gs, name, rows):
    idx, off, r = PACK_OFF[name]
    return gs[idx][:, off:off + r].reshape(N_DEV, rows, HEAD_PAD).transpose(1, 0, 2).reshape(rows, N_DEV * HEAD_PAD)


def _rows_from_shards(gs, name, keep=None):
    idx, off, r = PACK_OFF[name]
    keep = r if keep is None else keep
    return gs[idx][:, off:off + keep].reshape(N_DEV * keep, D)


def _rope_placement():
    i = lax.broadcasted_iota(jnp.int32, (HEAD_PAD, D), 0)
    j = lax.broadcasted_iota(jnp.int32, (HEAD_PAD, D), 1)
    lane = jnp.where(i < ROPE_HALF, 32 + i, 96 + i - ROPE_HALF)
    return ((i < 2 * ROPE_HALF) & (j % HEAD_PAD == lane)).astype(BF16)


def _unpack_in(g_in):
    w_inT = _rows_from_shards([g_in, None], "w_inT", IN_SHARD)
    lat_rows = Q_LORA + KV_LORA + 2 * ROPE_HALF
    conv = w_inT[lat_rows:lat_rows + CONV_COLS].reshape(3, D // CONV_CB, CONV_CB, D).transpose(1, 0, 2, 3).reshape(CONV_COLS, D)
    return {"latT": jnp.pad(w_inT[:lat_rows], ((0, LAT_PAD - lat_rows), (0, 0))), "convT": conv,
            "gateT": w_inT[lat_rows + CONV_COLS:]}


def _unpack_misc(g_misc):
    g = [None, g_misc]
    wpa = _cols_from_shards(g, "w_pa", 512).reshape(N_HEADS, NOPE, D)
    return {
        "wq": _cols_from_shards(g, "w_uq", Q_LORA),
        "wk": jnp.concatenate([_cols_from_shards(g, "w_uk", KV_LORA), _rope_placement()], axis=0),
        "wv": _cols_from_shards(g, "w_uv", KV_LORA),
        "wpa": jnp.pad(wpa, ((0, 0), (0, HEAD_PAD - NOPE), (0, 0))).reshape(D, D),
        "wpc": _rows_from_shards(g, "w_pc"), "wout": _rows_from_shards(g, "w_out"),
    }


def _shards_from_cols(a):
    rows = a.shape[0]
    return a.reshape(rows, N_DEV, HEAD_PAD).transpose(1, 0, 2).reshape(N_DEV, rows * HEAD_PAD // D, D)


def _pack_grads(gw):
    lat_rows = Q_LORA + KV_LORA + 2 * ROPE_HALF
    conv = gw["convT"].reshape(D // CONV_CB, 3, CONV_CB, D).transpose(1, 0, 2, 3).reshape(CONV_COLS, D)
    w_inT = jnp.concatenate([gw["latT"][:lat_rows], conv, gw["gateT"]], axis=0).reshape(N_DEV, IN_SHARD, D)
    wpa = gw["wpa"].reshape(N_HEADS, HEAD_PAD, D)[:, :NOPE].reshape(N_HEADS * NOPE, D)
    parts = {}
    parts.update({
        "w_inT": jnp.pad(w_inT, ((0, 0), (0, IN_SHARD_PAD - IN_SHARD), (0, 0))),
        "w_uq": _shards_from_cols(gw["wq"]), "w_uk": _shards_from_cols(gw["wk"][:KV_LORA]),
        "w_uv": _shards_from_cols(gw["wv"][:KV_LORA]), "w_pa": _shards_from_cols(wpa),
        "w_pc": gw["wpc"].reshape(N_DEV, D // N_DEV, D), "w_out": gw["wout"].reshape(N_DEV, D // N_DEV, D),
    })
    return [jnp.concatenate([parts[n] for n, _ in group], axis=1) for group in PACK]


def _unpack_grads(mines):
    def seg(name, keep=None):
        idx, off, r = PACK_OFF[name]
        return mines[idx][off:off + (r if keep is None else keep)]

    return {
        "w_in": seg("w_inT", IN_SHARD).T,
        "w_uq": _head_cols_inv(seg("w_uq").reshape(Q_LORA, HEAD_PAD), QK_DIM),
        "w_uk": _head_cols_inv(seg("w_uk").reshape(KV_LORA, HEAD_PAD), NOPE),
        "w_uv": seg("w_uv").reshape(KV_LORA, HEAD_PAD)[:, :NOPE],
        "w_proj_attn": seg("w_pa").reshape(512, HEAD_PAD),
        "w_proj_conv": seg("w_pc"), "w_out": seg("w_out"),
    }


def _rope_tables(positions):
    lane = jnp.arange(HEAD_PAD)
    idx = jnp.where((lane >= 32) & (lane < 48), lane - 32, jnp.where((lane >= 96) & (lane < 112), lane - 96, -1))
    inv_freq = jnp.where(idx >= 0, 1.0 / (ROPE_THETA ** (idx.astype(F32) / ROPE_HALF)), 0.0)
    ang = positions.reshape(-1).astype(F32)[:, None] * inv_freq
    return jnp.cos(ang), jnp.sin(ang) * jnp.where(lane < HEAD_PAD // 2, -1.0, 1.0)


def _local_step(x, positions, target, conv_w, small, ex):
    n_seq, seq, d = x.shape
    t = n_seq * seq
    x0 = x.reshape(t, d)
    tgt = target.reshape(t, d)
    rc, rs = _rope_tables(positions)
    ghq = _head_cols(small["q_head_norm"])
    ghk = _head_cols(small["k_head_norm"])
    TM, HC, TQ = 1024, 256, 1024

    def mm(*args, hosted=None, **kw):
        res = _mm(*args, hosted=hosted, **kw)
        return res if hosted is not None else (res, None)

    def wgrad(a, b, name, tm=None, hosted=None):
        tm = tm or a.shape[1]
        return mm(a, b, mode="tn", out_dtype=BF16, tm=tm, tn=b.shape[1], tk=2048 if tm <= D else 1024, name=name, hosted=hosted)

    f1g, f1u, f1d = ex.gather_finish(ex.witness() + rc[:8] + conv_w[:1, :HEAD_PAD])
    (x1, h1, a1, b1), got = _ffn_fwd(x0, small["ffn1_norm"], f1g, f1u, f1d, tm=512, hc=DFF // 2, name="ffn1_fwd",
                                     hosted=ex.gather_chips("mix_in"))
    hm, got = _rms_fwd(x1, small["mix_norm"], tm=TM, name="mix_norm_fwd", hosted=ex.gather_sibling(got))
    W = ex.mix_in_weights(got)
    (lat, conv3, gl), got = _proj_fwd(hm, W["latT"], W["convT"], W["gateT"], tm=512, name="proj_fwd",
                                      hosted=ex.gather_chips("mix_misc"))
    p, got = _conv_fwd(conv3, conv_w, n_seq=n_seq, seq=seq, name="conv_fwd", hosted=ex.gather_sibling(got))
    W.update(ex.mix_misc_weights(got))
    q, k, v, qn, ckv = _mla_prep_fwd(lat, small["q_a_norm"], small["kv_a_norm"], ghq, ghk, W["wq"], W["wk"], W["wv"], rc, rs,
                                     tm=512, name="mla_prep_fwd")
    (o, lse), got = _flash_fwd(q, k, v, n_seq=n_seq, seq=seq, tq=TQ, name="attn_fwd", hosted=ex.gather_chips("ffn2"))
    (x2, merged, ya, yb), got = _merge_fwd(o, p, gl, small["gate_bias"], x1, W["wpa"], W["wpc"], W["wout"], tm=512, name="merge_fwd",
                                           hosted=ex.gather_sibling(got))
    f2g, f2u, f2d = ex.ffn_weights(got)
    (dy, h2, a2, b2, loss_row), _ = _ffn_fwd(x2, small["ffn2_norm"], f2g, f2u, f2d, tm=512, hc=DFF // 2, name="ffn2_fwd", target=tgt)

    gw, gs = {}, {}
    (da2, db2, *ffn2_grads), _ = _ffn_grads(dy, h2, a2, b2, f2d, tm=TM, hc=HC, name="ffn2_grads")
    (dx2, gs["ffn2_norm"]), _ = _ffn_up_bwd(da2, db2, f2g, f2u, x2, small["ffn2_norm"], dy, tm=512, name="ffn2_up_bwd")

    (dx2b, dya, dyb, dgl, do, dp, gs["gate_bias"]), got = _merge_bwd(
        dx2, ya, yb, gl, small["gate_bias"], W["wpa"], W["wpc"], W["wout"], tm=512, name="merge_bwd",
        hosted=ex.scatter_sibling("ffn2", ffn2_grads))
    ex.scatter_sibling_done("ffn2", got)
    gw["wout"] = wgrad(merged, dx2b, "dw_out")[0]
    gw["wpa"] = wgrad(o, dya, "dw_pa")[0]
    gw["wpc"] = wgrad(p, dyb, "dw_pc")[0]
    dconv3, dconv_w = _conv_bwd(dp, conv3, conv_w, n_seq=n_seq, seq=seq, name="conv_bwd")
    (dq, dk, dv), got = _flash_bwd(q, k, v, o, lse, do, n_seq=n_seq, seq=seq, tq=TQ, name="attn_bwd",
                                   hosted=ex.scatter_chips("ffn2"))
    ex.scatter_chips_done("ffn2", got)
    dlat, dqp, dkp, gs["q_a_norm"], gs["kv_a_norm"], dghq, dghk = _mla_prep_bwd(
        dq, dk, dv, lat, qn, ckv, small["q_a_norm"], small["kv_a_norm"], ghq, ghk, W["wq"], W["wk"], W["wv"], rc, rs,
        tm=512, name="mla_prep_bwd")
    gs["q_head_norm"], gs["k_head_norm"] = _head_cols_inv(dghq, QK_DIM), _head_cols_inv(dghk, QK_DIM)
    gw["wq"] = wgrad(qn, dqp, "dw_uq")[0]
    gw["wk"] = wgrad(ckv, dkp, "dw_uk")[0]
    gw["wv"] = wgrad(ckv, dv, "dw_uv")[0]
    gw["convT"] = wgrad(dconv3, hm, "dw_conv", tm=CONV_COLS // 2)[0]
    gw["gateT"] = wgrad(dgl, hm, "dw_gate")[0]
    gw["latT"] = wgrad(dlat, hm, "dw_lat")[0]
    ex.scatter_sibling_now("mix", gw)
    (dx1, gs["mix_norm"]), got = _proj_bwd(dlat, dconv3, dgl, W["latT"], W["convT"], W["gateT"], x1, small["mix_norm"], dx2,
                                           tm=512, name="proj_bwd", hosted=ex.scatter_chips("mix_in"))
    ex.scatter_chips_done("mix_in", got)
    ex.reduce_small(gs, dconv_w, loss_row)

    (da1, db1, *ffn1_grads), got = _ffn_grads(dx1, h1, a1, b1, f1d, tm=TM, hc=HC, name="ffn1_grads",
                                              hosted=ex.scatter_chips("mix_misc"))
    ex.scatter_chips_done("mix_misc", got)
    ex.scatter_sibling_now("ffn1", ffn1_grads)
    zero = ex.scatter_chips_start("ffn1")
    (dx0, gs["ffn1_norm"]), _ = _ffn_up_bwd(da1, db1, f1g, f1u, x0, small["ffn1_norm"] + zero, dx1, tm=512, name="ffn1_up_bwd")
    return dx0.reshape(n_seq, seq, d), gs["ffn1_norm"]


class _MeshExchange:
    def __init__(self, w, core, chip):
        self.w, self.core, self.chip = w, core, chip
        self.partial, self.received, self._cache = {}, {}, {}

    def _blocks(self, group):
        w = self.w
        if group not in self._cache:
            if group.startswith("ffn"):
                self._cache[group] = [w[group + "_w_gate"].T.astype(BF16), w[group + "_w_up"].T.astype(BF16),
                                      w[group + "_w_down"].astype(BF16)]
            else:
                self._cache["mix_in"], self._cache["mix_misc"] = [[b] for b in _pack_weights(w)]
        return self._cache[group]

    def gather_chips(self, *groups):
        blocks = [b for group in groups for b in self._blocks(group)]
        return _gather_chips_plan(len(blocks)), blocks, _gather_shapes(blocks)

    def gather_sibling(self, got):
        half = list(got)
        return _gather_sibling_plan(len(half)), half, _same_shapes(half)

    def gather_start(self, group):
        blocks = self._blocks(group)
        plan = _gather_chips_plan(len(blocks))
        sems, blocks, lands, token = _plan_start(plan, blocks, _gather_shapes(blocks), name="gather_%s_start" % group)
        self._gathering = (group, plan, sems, blocks, lands)
        return token[0, 0]

    def gather_finish(self, after):
        group, plan, sems, blocks, lands = self._gathering
        _, half = _plan_wait(plan, sems, blocks, lands, after, name="gather_%s_wait" % group)
        return self.ffn_weights(_run_plan(_gather_sibling_plan(len(half)), half, _same_shapes(half), name="gather_%s_sibling" % group))

    def reduce_small(self, gs, dconv_w, loss_row):
        pieces = [_pad_cols(gs[n], SMALL_SLOTS[n]) for n in SMALL_NAMES[1:]] + [dconv_w.reshape(1, 3 * D), loss_row]
        self.small_total = _small_exchange(jnp.concatenate(pieces, axis=1).reshape(-1, 128), reduce=True,
                                           name="reduce_small").reshape(-1)

    def scatter_chips_start(self, group):
        s1 = self.partial[group]
        plan = _scatter_chips_plan(len(s1))
        sems, s1, lands, token = _plan_start(plan, s1, _scatter_shapes(s1), name="scatter_%s_start" % group)
        self._scattering = (group, plan, sems, s1, lands)
        return token[0, 0]

    def scatter_chips_finish(self, after):
        group, plan, sems, s1, lands = self._scattering
        self.partial[group], self.received[group] = _plan_wait(plan, sems, s1, lands, after, name="scatter_%s_wait" % group)

    def witness(self):
        parts = [b[:8, :128].astype(F32) for g in ("mix_in", "mix_misc", "ffn2") for b in self._blocks(g)]
        return functools.reduce(jnp.add, parts)

    def ffn_weights(self, got):
        return [a.reshape(DFF, D) for a in got]

    def mix_in_weights(self, got):
        return _unpack_in(got[0])

    def mix_misc_weights(self, got):
        return _unpack_misc(got[0])

    def _parts(self, group, grads):
        if group == "mix":
            return _pack_grads(grads), ["mix_in", "mix_misc"]
        parts = [g.reshape(N_DEV, -1, D) for g in grads]
        return parts, ([group] if len(parts) == 1 else None)

    def scatter_sibling(self, group, grads):
        self._sent, self._names = self._parts(group, grads)
        return _scatter_sibling_plan(len(self._sent)), self._sent, _halved_shapes(self._sent)

    def scatter_sibling_done(self, group, got):
        sums = [_sum_sibling(p, q, self.core, name="sum_%s_sibling_%d" % (group, i)) for i, (p, q) in enumerate(zip(self._sent, got))]
        if self._names is None:
            self.partial[group] = sums
        else:
            for n, s in zip(self._names, sums):
                self.partial[n] = [s]

    def scatter_sibling_now(self, group, grads):
        plan, parts, shapes = self.scatter_sibling(group, grads)
        self.scatter_sibling_done(group, _run_plan(plan, parts, shapes, name="scatter_%s_sibling" % group))

    def scatter_chips(self, group):
        s1 = self.partial[group]
        return _scatter_chips_plan(len(s1)), s1, _scatter_shapes(s1)

    def scatter_chips_done(self, group, got):
        self.received[group] = list(got)


SMALL_NAMES = ("ffn1_norm", "mix_norm", "gate_bias", "q_a_norm", "kv_a_norm", "q_head_norm", "k_head_norm", "ffn2_norm")
SMALL_SLOTS = {"ffn1_norm": 1024, "mix_norm": 1024, "gate_bias": 2048, "q_a_norm": 384, "kv_a_norm": 256, "q_head_norm": 128,
               "k_head_norm": 128, "ffn2_norm": 1024, "conv_w": 3072, "loss": 128}
COLUMN_MAJOR = ("w_in", "w_uq", "w_uk", "w_uv")
WEIGHT_NAMES = ("ffn1_norm", "ffn1_w_gate", "ffn1_w_up", "ffn1_w_down", "mix_norm", "w_in", "gate_bias", "q_a_norm", "w_uq",
                "kv_a_norm", "w_uk", "w_uv", "q_head_norm", "k_head_norm", "w_proj_attn", "conv_w", "w_proj_conv", "w_out",
                "ffn2_norm", "ffn2_w_gate", "ffn2_w_up", "ffn2_w_down")


def _step(x, positions, loss_target, w, m, v):
    xi, yi, ci = _place()
    core = ci.astype(jnp.int32).reshape(1)
    chip = (2 * xi + yi).astype(jnp.int32).reshape(1)
    me = 4 * xi + 2 * yi + ci

    ex = _MeshExchange(w, core, chip)
    cw_all = _small_exchange(jnp.pad(w["conv_w"], ((0, 5), (0, 0))), reduce=False, name="gather_conv_w")
    conv_w = cw_all[:, :3].transpose(1, 0, 2).reshape(3, D)
    ex.w = {n: (a + cw_all[0, 7, 0] if n.startswith("ffn1") else a) for n, a in w.items()}
    zero = ex.gather_start("ffn1")
    ex.w = {n: (a if n.startswith("ffn1") else a + zero) for n, a in w.items()}
    small = {n: w[n].reshape(1, -1) for n in SMALL_NAMES}

    grad_x, dffn1_norm = _local_step(x, positions + zero.astype(jnp.int32), loss_target, conv_w, small, ex)

    grads, deltas, new_m, new_v = {}, {}, {}, {}

    def ffn_update(group):
        for i, n in enumerate((group + "_w_gate", group + "_w_up", group + "_w_down")):
            transposed = not n.endswith("down")
            wv, mv, vv = (a[n].T if transposed else a[n] for a in (w, m, v))
            res = _sum_adamw(ex.partial[group][i], ex.received[group][i], chip, wv, mv, vv, name="adamw_" + n)
            grads[n], deltas[n], new_m[n], new_v[n] = (r.T if transposed else r for r in res)

    def update(n):
        shape = w[n].shape
        if n in COLUMN_MAJOR:
            ops = [a.T for a in (w[n], grads[n], m[n], v[n])]
            deltas[n], new_m[n], new_v[n] = (r.T for r in _adamw(*ops, name="adamw_" + n))
            return
        view = shape if len(shape) == 2 else ((-1, 128) if shape[0] % 128 == 0 else (1, shape[0]))
        dlt, nm, nv = _adamw(w[n].reshape(view), grads[n].reshape(view), m[n].reshape(view), v[n].reshape(view), name="adamw_" + n)
        deltas[n], new_m[n], new_v[n] = dlt.reshape(shape), nm.reshape(shape), nv.reshape(shape)

    ffn_update("ffn2")
    grads.update(_unpack_grads([_sum_chips(ex.partial[g][0], ex.received[g][0], chip, name="sum_%s_chips" % g)
                                for g in ("mix_in", "mix_misc")]))
    total, off = ex.small_total, 0
    for n in SMALL_NAMES[1:]:
        grads[n] = total[off:off + w[n].shape[0]]
        off += SMALL_SLOTS[n]
    conv_full = total[off:off + 3 * D].reshape(3, D)
    grads["conv_w"] = lax.dynamic_slice(conv_full, (0, me * HEAD_PAD), (3, HEAD_PAD))
    loss = total[off + 3 * D]
    later = ("ffn1_norm", "ffn1_w_gate", "ffn1_w_up", "ffn1_w_down")
    for n in WEIGHT_NAMES:
        if n not in deltas and n not in later:
            update(n)

    done = [deltas[n][:8, :128] for n in ("ffn2_w_down", "w_in", "w_out", "w_proj_attn")] + [deltas["mix_norm"].reshape(8, 128)]
    ex.scatter_chips_finish(functools.reduce(jnp.add, done) + grad_x.reshape(-1, D)[:8, :128])
    ffn_update("ffn1")
    grads["ffn1_norm"] = _small_exchange(dffn1_norm.reshape(-1, 128), reduce=True, name="reduce_ffn1_norm").reshape(-1)
    update("ffn1_norm")
    return (loss, grad_x, *[grads[n] for n in WEIGHT_NAMES], *[deltas[n] for n in WEIGHT_NAMES],
            *[new_m[n] for n in WEIGHT_NAMES], *[new_v[n] for n in WEIGHT_NAMES])


def kernel(x, positions, ffn1_norm, ffn1_w_gate, ffn1_w_up, ffn1_w_down, mix_norm, w_in, gate_bias, q_a_norm, w_uq, kv_a_norm, w_uk, w_uv, q_head_norm, k_head_norm, w_proj_attn, conv_w, w_proj_conv, w_out, ffn2_norm, ffn2_w_gate, ffn2_w_up, ffn2_w_down, loss_target, m_ffn1_norm, m_ffn1_w_gate, m_ffn1_w_up, m_ffn1_w_down, m_mix_norm, m_w_in, m_gate_bias, m_q_a_norm, m_w_uq, m_kv_a_norm, m_w_uk, m_w_uv, m_q_head_norm, m_k_head_norm, m_w_proj_attn, m_conv_w, m_w_proj_conv, m_w_out, m_ffn2_norm, m_ffn2_w_gate, m_ffn2_w_up, m_ffn2_w_down, v_ffn1_norm, v_ffn1_w_gate, v_ffn1_w_up, v_ffn1_w_down, v_mix_norm, v_w_in, v_gate_bias, v_q_a_norm, v_w_uq, v_kv_a_norm, v_w_uk, v_w_uv, v_q_head_norm, v_k_head_norm, v_w_proj_attn, v_conv_w, v_w_proj_conv, v_w_out, v_ffn2_norm, v_ffn2_w_gate, v_ffn2_w_up, v_ffn2_w_down):
    given = dict(locals())
    w = {n: given[n] for n in WEIGHT_NAMES}
    m = {n: given["m_" + n] for n in WEIGHT_NAMES}
    v = {n: given["v_" + n] for n in WEIGHT_NAMES}
    return _step(x, positions, loss_target, w, m, v)
```

```python
import functools

import jax
import jax.numpy as jnp
from jax import lax
from jax.experimental import pallas as pl
from jax.experimental.pallas import tpu as pltpu

F32 = jnp.float32
BF16 = jnp.bfloat16
MESH = pl.DeviceIdType.MESH
ANY = pl.BlockSpec(memory_space=pl.ANY)

N_DEV = 8
D = 1024
DFF = 2816
N_HEADS = 8
HEAD_PAD = 128
QK_DIM = 96
NOPE = 64
ROPE_HALF = 16
Q_LORA = 384
KV_LORA = 256
LAT_PAD = 768
CONV_COLS = 3072
GATE_COLS = 2048
IN_DIM = 5792
IN_SHARD = IN_DIM // N_DEV
IN_SHARD_PAD = 736
FF_SHARD = DFF // N_DEV
ROPE_THETA = 10000.0
NORM_EPS = 1e-6
ATTN_SCALE = QK_DIM ** -0.5
NEG = -1e30

ADAM_LR, ADAM_B1, ADAM_B2, ADAM_EPS, ADAM_WD, ADAM_STEP = 0.001, 0.9, 0.999, 1e-08, 0.01, 10

PACK = ((("w_inT", IN_SHARD_PAD),), (("w_uq", 48), ("w_uk", 32), ("w_uv", 32), ("w_pa", 64), ("w_pc", 128), ("w_out", 128)))
PACK_OFF = {}
for _i, _group in enumerate(PACK):
    _o = 0
    for _n, _r in _group:
        PACK_OFF[_n] = (_i, _o, _r)
        _o += _r

VMEM_LIMIT = 56 * 1024 * 1024


def _params(*sem):
    return pltpu.CompilerParams(dimension_semantics=sem if sem else None, vmem_limit_bytes=VMEM_LIMIT)


class _Plan:
    def __init__(self, start, wait, n_remote, n_local, in_place=False):
        self.start, self.wait, self.n_remote, self.n_local, self.in_place = start, wait, n_remote, n_local, in_place

    def sems(self):
        return [pltpu.SemaphoreType.DMA((self.n_remote,)), pltpu.SemaphoreType.DMA((self.n_remote,)),
                pltpu.SemaphoreType.DMA((max(self.n_local, 1),))]


def _call(body, *, name, grid, in_specs, out_specs, out_shape, scratch_shapes, operands, sem, hosted=None):
    if hosted is None:
        outs = pl.pallas_call(body, name=name, grid=grid, in_specs=in_specs, out_specs=out_specs, out_shape=out_shape,
                              scratch_shapes=scratch_shapes, compiler_params=_params(*sem))(*operands)
        return outs, None
    plan, srcs, h_shapes = hosted
    n_in, n_out, n_scr, nh_in, nh_out = len(in_specs), len(out_specs), len(scratch_shapes), len(srcs), len(h_shapes)
    aliases = {n_in + a: n_out + a for a in range(nh_in)} if plan.in_place else {}

    def full_body(*refs):
        ins, refs = refs[:n_in], refs[n_in:]
        h_in, refs = refs[:nh_in], refs[nh_in:]
        outs, refs = refs[:n_out], refs[n_out:]
        h_out, refs = refs[:nh_out], refs[nh_out:]
        scr, sems = refs[:n_scr], refs[n_scr:]
        ids = [pl.program_id(ax) for ax in range(len(grid))]
        first = functools.reduce(jnp.logical_and, [i == 0 for i in ids])
        last = functools.reduce(jnp.logical_and, [i == g - 1 for i, g in zip(ids, grid)])

        @pl.when(first)
        def _():
            plan.start(h_in, h_out, *sems)

        body(*ins, *outs, *scr)

        @pl.when(last)
        def _():
            plan.wait(h_in, h_out, *sems)

    res = pl.pallas_call(
        full_body, name=name, grid=grid, in_specs=list(in_specs) + [ANY] * nh_in, out_specs=list(out_specs) + [ANY] * nh_out,
        out_shape=list(out_shape) + list(h_shapes), scratch_shapes=list(scratch_shapes) + plan.sems(),
        input_output_aliases=aliases, compiler_params=_params(*(["arbitrary"] * len(grid))),
    )(*operands, *srcs)
    return res[:n_out], res[n_out:]


def _dot_nn(a, b):
    return lax.dot_general(a, b, (((1,), (0,)), ((), ())), preferred_element_type=F32)


def _dot_nt(a, b):
    return lax.dot_general(a, b, (((1,), (1,)), ((), ())), preferred_element_type=F32)


def _dot_tn(a, b):
    return lax.dot_general(a, b, (((0,), (0,)), ((), ())), preferred_element_type=F32)


def _sigmoid(x):
    return 0.5 * jnp.tanh(0.5 * x) + 0.5


def _rms_stats(x):
    r = lax.rsqrt(jnp.mean(x * x, axis=-1, keepdims=True) + NORM_EPS)
    return x * r, r


ROWS_WIDE = 16
MM_ROWS = 256


def _rms_bwd(dy, xhat, r, g):
    dg = jnp.sum(dy * xhat, axis=0, keepdims=True)
    dxh = dy * g
    dx = r * (dxh - xhat * jnp.mean(dxh * xhat, axis=-1, keepdims=True))
    return dx, dg


def _mm(a, b, *, mode, out_dtype, tm, tn, tk, name, add=None, scale=1.0, hosted=None):
    if mode == "nn":
        (m, k), (_, n) = a.shape, b.shape
    elif mode == "nt":
        (m, k), (n, _) = a.shape, b.shape
    else:
        (k, m), (_, n) = a.shape, b.shape
    assert m % tm == 0 and n % tn == 0 and k % tk == 0, (name, m, n, k, tm, tn, tk)
    nk = k // tk
    dot = {"nn": _dot_nn, "nt": _dot_nt, "tn": _dot_tn}[mode]
    a_spec = pl.BlockSpec((tk, tm), lambda i, j, kk: (kk, i)) if mode == "tn" else pl.BlockSpec((tm, tk), lambda i, j, kk: (i, kk))
    b_spec = pl.BlockSpec((tn, tk), lambda i, j, kk: (j, kk)) if mode == "nt" else pl.BlockSpec((tk, tn), lambda i, j, kk: (kk, j))
    o_spec = pl.BlockSpec((tm, tn), lambda i, j, kk: (i, j))
    has_add = add is not None

    def finish(prod, c_ref, o_ref):
        if scale != 1.0:
            prod = prod * scale
        o_ref[...] = ((c_ref[...] + prod) if has_add else prod).astype(out_dtype)

    def body(*refs):
        a_ref, b_ref = refs[:2]
        c_ref = refs[2] if has_add else None
        o_ref = refs[3] if has_add else refs[2]
        if nk == 1:
            finish(dot(a_ref[...], b_ref[...]), c_ref, o_ref)
            return
        acc_ref = refs[-1]
        kk = pl.program_id(2)

        @pl.when(kk == 0)
        def _():
            acc_ref[...] = jnp.zeros_like(acc_ref)

        acc_ref[...] += dot(a_ref[...], b_ref[...])

        @pl.when(kk == nk - 1)
        def _():
            finish(acc_ref[...], c_ref, o_ref)

    operands = (a, b, add) if has_add else (a, b)
    in_specs = [a_spec, b_spec] + ([o_spec] if has_add else [])
    (out,), got = _call(
        body, name=name, grid=(m // tm, n // tn, nk), in_specs=in_specs, out_specs=[o_spec],
        out_shape=[jax.ShapeDtypeStruct((m, n), out_dtype)], scratch_shapes=[pltpu.VMEM((tm, tn), F32)] if nk > 1 else [],
        operands=operands, sem=("parallel", "parallel", "arbitrary"), hosted=hosted)
    return out if hosted is None else (out, got)


def _rms_fwd(x, g, *, tm, name, hosted=None):
    t, d = x.shape

    def body(x_ref, g_ref, h_ref):
        xhat, _ = _rms_stats(x_ref[...])
        h_ref[...] = (xhat * g_ref[...]).astype(BF16)

    (h,), got = _call(
        body, name=name, grid=(t // tm,),
        in_specs=[pl.BlockSpec((tm, d), lambda i: (i, 0)), pl.BlockSpec((1, d), lambda i: (0, 0))],
        out_specs=[pl.BlockSpec((tm, d), lambda i: (i, 0))], out_shape=[jax.ShapeDtypeStruct((t, d), BF16)], scratch_shapes=[],
        operands=(x, g), sem=("parallel",), hosted=hosted)
    return h, got


def _ffn_fwd(x, g, wgT, wuT, wd, *, tm, hc, name, hosted=None, target=None):
    t, d = x.shape
    nj = DFF // hc
    with_loss = target is not None

    def body(*refs):
        x_ref, g_ref, wg_ref, wu_ref, wd_ref = refs[:5]
        t_ref = refs[5] if with_loss else None
        xo_ref, h_ref, a_ref, b_ref = refs[5 + with_loss:9 + with_loss]
        loss_ref = refs[9 + with_loss] if with_loss else None
        acc_ref = refs[-1]
        i, j = pl.program_id(0), pl.program_id(1)

        @pl.when(j == 0)
        def _():
            xhat, _ = _rms_stats(x_ref[...])
            h_ref[...] = (xhat * g_ref[...]).astype(BF16)
            acc_ref[...] = jnp.zeros_like(acc_ref)

        h = h_ref[...]
        a = _dot_nt(h, wg_ref[...])
        b = _dot_nt(h, wu_ref[...])
        a_ref[...] = a.astype(BF16)
        b_ref[...] = b.astype(BF16)
        s = (a * _sigmoid(a) * b).astype(BF16)
        acc_ref[...] += _dot_nn(s, wd_ref[...])

        if with_loss:
            @pl.when((i == 0) & (j == 0))
            def _():
                loss_ref[...] = jnp.zeros_like(loss_ref)

        @pl.when(j == nj - 1)
        def _():
            y = x_ref[...] + 0.5 * acc_ref[...]
            if with_loss:
                err = y - t_ref[...]
                xo_ref[...] = err * (1.0 / d)
                loss_ref[...] += jnp.sum(jnp.sum(err * err, axis=-1, keepdims=True), axis=0, keepdims=True) * (0.5 / d)
            else:
                xo_ref[...] = y

    row = pl.BlockSpec((tm, d), lambda i, j: (i, 0))
    vec = pl.BlockSpec((1, d), lambda i, j: (0, 0))
    wsp = pl.BlockSpec((hc, d), lambda i, j: (j, 0))
    hid = pl.BlockSpec((tm, hc), lambda i, j: (i, j))
    out_specs = [row, row, hid, hid] + ([pl.BlockSpec((1, 128), lambda i, j: (0, 0))] if with_loss else [])
    out_shape = [jax.ShapeDtypeStruct((t, d), F32), jax.ShapeDtypeStruct((t, d), BF16), jax.ShapeDtypeStruct((t, DFF), BF16),
                 jax.ShapeDtypeStruct((t, DFF), BF16)] + ([jax.ShapeDtypeStruct((1, 128), F32)] if with_loss else [])
    return _call(
        body, name=name, grid=(t // tm, nj), in_specs=[row, vec, wsp, wsp, wsp] + ([row] if with_loss else []),
        out_specs=out_specs, out_shape=out_shape, scratch_shapes=[pltpu.VMEM((tm, d), F32)],
        operands=(x, g, wgT, wuT, wd) + ((target,) if with_loss else ()),
        sem=("arbitrary" if with_loss else "parallel", "arbitrary"), hosted=hosted)


def _ffn_grads(dout, h, a, b, wd, *, tm, hc, name, hosted=None):
    t, d = dout.shape
    ni, nj = t // tm, DFF // hc

    def body(dout_ref, h_ref, a_ref, b_ref, wd_ref, da_ref, db_ref, dwg_ref, dwu_ref, dwd_ref,
             dy_all, h_all, ds_scr, s_scr, acc_g, acc_u, acc_d):
        j, i = pl.program_id(0), pl.program_id(1)
        rows_i = pl.ds(pl.multiple_of(i * tm, tm), tm)

        @pl.when(j == 0)
        def _():
            dy_all[rows_i, :] = (0.5 * dout_ref[...]).astype(BF16)
            h_all[rows_i, :] = h_ref[...]

        @pl.when(i == 0)
        def _():
            acc_g[...] = jnp.zeros_like(acc_g)
            acc_u[...] = jnp.zeros_like(acc_u)
            acc_d[...] = jnp.zeros_like(acc_d)

        def grad_rows(rows):
            ds = ds_scr[rows, :]
            av = a_ref[rows, :].astype(F32)
            bv = b_ref[rows, :].astype(F32)
            sg = _sigmoid(av)
            sl = av * sg
            s_scr[rows, :] = (sl * bv).astype(BF16)
            da_ref[rows, :] = (ds * bv * (sg + sl * (1.0 - sg))).astype(BF16)
            db_ref[rows, :] = (ds * sl).astype(BF16)

        for blk in range(tm // MM_ROWS):
            rs = slice(blk * MM_ROWS, (blk + 1) * MM_ROWS)
            ds_scr[rs, :] = _dot_nt(dy_all[pl.ds(pl.multiple_of(i * tm + blk * MM_ROWS, MM_ROWS), MM_ROWS), :], wd_ref[...])
            for c in range(MM_ROWS // ROWS_WIDE):
                grad_rows(slice(blk * MM_ROWS + c * ROWS_WIDE, blk * MM_ROWS + (c + 1) * ROWS_WIDE))

        dy_i = dy_all[rows_i, :]
        h_i = h_all[rows_i, :]
        acc_d[...] += _dot_tn(s_scr[...], dy_i)
        acc_g[...] += _dot_tn(da_ref[...], h_i)
        acc_u[...] += _dot_tn(db_ref[...], h_i)

        @pl.when(i == ni - 1)
        def _():
            dwg_ref[...] = acc_g[...].astype(BF16)
            dwu_ref[...] = acc_u[...].astype(BF16)
            dwd_ref[...] = acc_d[...].astype(BF16)

    first = pl.BlockSpec((tm, d), lambda j, i: (jnp.where(j == 0, i, 0), 0))
    hid = pl.BlockSpec((tm, hc), lambda j, i: (i, j))
    wsp = pl.BlockSpec((hc, d), lambda j, i: (j, 0))
    hid_shape = jax.ShapeDtypeStruct((t, DFF), BF16)
    w_shape = jax.ShapeDtypeStruct((DFF, d), BF16)
    return _call(
        body, name=name, grid=(nj, ni), in_specs=[first, first, hid, hid, wsp], out_specs=[hid, hid, wsp, wsp, wsp],
        out_shape=[hid_shape, hid_shape, w_shape, w_shape, w_shape],
        scratch_shapes=[pltpu.VMEM((t, d), BF16), pltpu.VMEM((t, d), BF16), pltpu.VMEM((tm, hc), F32), pltpu.VMEM((tm, hc), BF16),
                        pltpu.VMEM((hc, d), F32), pltpu.VMEM((hc, d), F32), pltpu.VMEM((hc, d), F32)],
        operands=(dout, h, a, b, wd), sem=("arbitrary", "arbitrary"), hosted=hosted)


def _proj_fwd(h, latT, convT, gateT, *, tm, name, hosted=None):
    t, d = h.shape

    def body(h_ref, wl_ref, wc_ref, wg_ref, lat_ref, conv_ref, gl_ref):
        hv = h_ref[...]
        lat_ref[...] = _dot_nt(hv, wl_ref[...]).astype(BF16)
        conv_ref[...] = _dot_nt(hv, wc_ref[...]).astype(BF16)
        gl_ref[...] = _dot_nt(hv, wg_ref[...]).astype(BF16)

    def rows(w):
        return pl.BlockSpec((tm, w), lambda i: (i, 0))

    def full(r):
        return pl.BlockSpec((r, d), lambda i: (0, 0))

    return _call(
        body, name=name, grid=(t // tm,), in_specs=[rows(d), full(LAT_PAD), full(CONV_COLS), full(GATE_COLS)],
        out_specs=[rows(LAT_PAD), rows(CONV_COLS), rows(GATE_COLS)],
        out_shape=[jax.ShapeDtypeStruct((t, LAT_PAD), BF16), jax.ShapeDtypeStruct((t, CONV_COLS), BF16),
                   jax.ShapeDtypeStruct((t, GATE_COLS), BF16)],
        scratch_shapes=[], operands=(h, latT, convT, gateT), sem=("parallel",), hosted=hosted)


def _proj_bwd(dlat, dconv3, dgl, latT, convT, gateT, x, g, dres, *, tm, name, hosted=None):
    t, d = x.shape

    def body(dl_ref, dc_ref, dg_ref, wl_ref, wc_ref, wg_ref, x_ref, g_ref, dres_ref, dx_ref, dgain_ref):
        @pl.when(pl.program_id(0) == 0)
        def _():
            dgain_ref[...] = jnp.zeros_like(dgain_ref)

        dh = _dot_nn(dl_ref[...], wl_ref[...]) + _dot_nn(dc_ref[...], wc_ref[...]) + _dot_nn(dg_ref[...], wg_ref[...])
        xhat, r = _rms_stats(x_ref[...])
        dx, dgain = _rms_bwd(dh, xhat, r, g_ref[...])
        dx_ref[...] = dres_ref[...] + dx
        dgain_ref[...] += dgain

    def rows(w):
        return pl.BlockSpec((tm, w), lambda i: (i, 0))

    def full(r):
        return pl.BlockSpec((r, d), lambda i: (0, 0))

    return _call(
        body, name=name, grid=(t // tm,),
        in_specs=[rows(LAT_PAD), rows(CONV_COLS), rows(GATE_COLS), full(LAT_PAD), full(CONV_COLS), full(GATE_COLS), rows(d), full(1), rows(d)],
        out_specs=[rows(d), full(1)], out_shape=[jax.ShapeDtypeStruct((t, d), F32), jax.ShapeDtypeStruct((1, d), F32)],
        scratch_shapes=[], operands=(dlat, dconv3, dgl, latT, convT, gateT, x, g, dres), sem=("arbitrary",), hosted=hosted)


def _ffn_up_bwd(da, db, wgT, wuT, x, g, dout, *, tm, name, hosted=None):
    t, d = x.shape

    def body(da_ref, db_ref, wg_ref, wu_ref, x_ref, g_ref, dout_ref, dx_ref, dg_ref):
        @pl.when(pl.program_id(0) == 0)
        def _():
            dg_ref[...] = jnp.zeros_like(dg_ref)

        dh = _dot_nn(da_ref[...], wg_ref[...]) + _dot_nn(db_ref[...], wu_ref[...])
        xhat, r = _rms_stats(x_ref[...])
        dx, dg = _rms_bwd(dh, xhat, r, g_ref[...])
        dx_ref[...] = dout_ref[...] + dx
        dg_ref[...] += dg

    row = pl.BlockSpec((tm, d), lambda i: (i, 0))
    vec = pl.BlockSpec((1, d), lambda i: (0, 0))
    hid = pl.BlockSpec((tm, DFF), lambda i: (i, 0))
    wsp = pl.BlockSpec((DFF, d), lambda i: (0, 0))
    return _call(
        body, name=name, grid=(t // tm,), in_specs=[hid, hid, wsp, wsp, row, vec, row], out_specs=[row, vec],
        out_shape=[jax.ShapeDtypeStruct((t, d), F32), jax.ShapeDtypeStruct((1, d), F32)], scratch_shapes=[],
        operands=(da, db, wgT, wuT, x, g, dout), sem=("arbitrary",), hosted=hosted)


HEAD_LANES = (slice(0, 32), slice(64, 80), None, slice(32, 64), slice(80, 96), None)


def _head_cols(a):
    def part(sl, width):
        if sl is None or sl.stop > a.shape[1]:
            return jnp.zeros((a.shape[0], width), a.dtype)
        return a[:, sl]

    return jnp.concatenate([part(sl, w) for sl, w in zip(HEAD_LANES, (32, 16, 16, 32, 16, 16))], axis=1)


def _head_cols_inv(a, dims):
    parts = [a[:, 0:32], a[:, 64:96]] + ([a[:, 32:48], a[:, 96:112]] if dims == QK_DIM else [])
    return jnp.concatenate(parts, axis=1)


def _rope_fwd(x, c, s):
    return x * c + pltpu.roll(x, HEAD_PAD // 2, 1) * s


def _rope_bwd(dy, c, s):
    return dy * c + pltpu.roll(dy * s, HEAD_PAD // 2, 1)


def _head_stats(x):
    r = lax.rsqrt(jnp.sum(x * x, axis=-1, keepdims=True) * (1.0 / QK_DIM) + NORM_EPS)
    return x * r, r


def _mla_prep_fwd(lat, gq, gkv, ghq, ghk, wq, wk, wv, rc, rs, *, tm, name):
    t = lat.shape[0]

    def body(lat_ref, gq_ref, gkv_ref, ghq_ref, ghk_ref, wq_ref, wk_ref, wv_ref, c_ref, s_ref,
             q_ref, k_ref, v_ref, qn_ref, ckv_ref):
        lat_v = lat_ref[...]
        qhat, _ = _rms_stats(lat_v[:, :Q_LORA].astype(F32))
        qn = (qhat * gq_ref[...]).astype(BF16)
        khat, _ = _rms_stats(lat_v[:, Q_LORA:Q_LORA + KV_LORA].astype(F32))
        ckv = (khat * gkv_ref[...]).astype(BF16)
        ckv_ext = jnp.concatenate([ckv, lat_v[:, Q_LORA + KV_LORA:]], axis=1)
        qn_ref[...] = qn
        ckv_ref[...] = ckv_ext
        q_pre = _dot_nn(qn, wq_ref[...])
        k_pre = _dot_nn(ckv_ext, wk_ref[...])
        v_ref[...] = _dot_nn(ckv, wv_ref[...]).astype(BF16)
        c, s = c_ref[...], s_ref[...]
        for h in range(N_HEADS):
            hs = slice(h * HEAD_PAD, (h + 1) * HEAD_PAD)
            xq, _ = _head_stats(q_pre[:, hs])
            q_ref[:, hs] = _rope_fwd(xq * ghq_ref[...], c, s).astype(BF16)
            xk, _ = _head_stats(k_pre[:, hs])
            k_ref[:, hs] = _rope_fwd(xk * ghk_ref[...], c, s).astype(BF16)

    def row(w):
        return pl.BlockSpec((tm, w), lambda i: (i, 0))

    def full(r, w):
        return pl.BlockSpec((r, w), lambda i: (0, 0))

    wide = jax.ShapeDtypeStruct((t, D), BF16)
    lat3 = jax.ShapeDtypeStruct((t, Q_LORA), BF16)
    return pl.pallas_call(
        body, name=name, grid=(t // tm,),
        in_specs=[row(LAT_PAD), full(1, Q_LORA), full(1, KV_LORA), full(1, HEAD_PAD), full(1, HEAD_PAD),
                  full(Q_LORA, D), full(Q_LORA, D), full(KV_LORA, D), row(HEAD_PAD), row(HEAD_PAD)],
        out_specs=[row(D), row(D), row(D), row(Q_LORA), row(Q_LORA)],
        out_shape=[wide, wide, wide, lat3, lat3],
        compiler_params=_params("parallel"),
    )(lat, gq, gkv, ghq, ghk, wq, wk, wv, rc, rs)


def _mla_prep_bwd(dq, dk, dv, lat, qn, ckv_ext, gq, gkv, ghq, ghk, wq, wk, wv, rc, rs, *, tm, name):
    t = lat.shape[0]

    def body(dq_ref, dk_ref, dv_ref, lat_ref, qn_ref, ckv_ref, gq_ref, gkv_ref, ghq_ref, ghk_ref, wq_ref, wk_ref, wv_ref,
             c_ref, s_ref, dlat_ref, dqp_ref, dkp_ref, dgq_ref, dgkv_ref, dghq_ref, dghk_ref):
        @pl.when(pl.program_id(0) == 0)
        def _():
            dgq_ref[...] = jnp.zeros_like(dgq_ref)
            dgkv_ref[...] = jnp.zeros_like(dgkv_ref)
            dghq_ref[...] = jnp.zeros_like(dghq_ref)
            dghk_ref[...] = jnp.zeros_like(dghk_ref)

        c, s = c_ref[...], s_ref[...]
        q_pre = _dot_nn(qn_ref[...], wq_ref[...])
        k_pre = _dot_nn(ckv_ref[...], wk_ref[...])

        def heads(pre, dy_ref, gh_ref, dgh_ref, out_ref):
            dgh = jnp.zeros((1, HEAD_PAD), F32)
            for h in range(N_HEADS):
                hs = slice(h * HEAD_PAD, (h + 1) * HEAD_PAD)
                d = _rope_bwd(dy_ref[:, hs].astype(F32), c, s)
                xhat, r = _head_stats(pre[:, hs])
                dgh = dgh + jnp.sum(d * xhat, axis=0, keepdims=True)
                dxh = d * gh_ref[...]
                dx = r * (dxh - xhat * (jnp.sum(dxh * xhat, axis=-1, keepdims=True) * (1.0 / QK_DIM)))
                out_ref[:, hs] = dx.astype(BF16)
            dgh_ref[...] += dgh

        heads(q_pre, dq_ref, ghq_ref, dghq_ref, dqp_ref)
        heads(k_pre, dk_ref, ghk_ref, dghk_ref, dkp_ref)
        dqn = _dot_nt(dqp_ref[...], wq_ref[...])
        dce = _dot_nt(dkp_ref[...], wk_ref[...])
        dckv = dce[:, :KV_LORA] + _dot_nt(dv_ref[...], wv_ref[...])
        lat_v = lat_ref[...]
        qhat, rq = _rms_stats(lat_v[:, :Q_LORA].astype(F32))
        dql, dgq = _rms_bwd(dqn, qhat, rq, gq_ref[...])
        khat, rk = _rms_stats(lat_v[:, Q_LORA:Q_LORA + KV_LORA].astype(F32))
        dkl, dgkv = _rms_bwd(dckv, khat, rk, gkv_ref[...])
        dgq_ref[...] += dgq
        dgkv_ref[...] += dgkv
        dlat_ref[...] = jnp.concatenate([dql, dkl, dce[:, KV_LORA:]], axis=1).astype(BF16)

    def row(w):
        return pl.BlockSpec((tm, w), lambda i: (i, 0))

    def full(r, w):
        return pl.BlockSpec((r, w), lambda i: (0, 0))

    return pl.pallas_call(
        body, name=name, grid=(t // tm,),
        in_specs=[row(D), row(D), row(D), row(LAT_PAD), row(Q_LORA), row(Q_LORA), full(1, Q_LORA), full(1, KV_LORA),
                  full(1, HEAD_PAD), full(1, HEAD_PAD), full(Q_LORA, D), full(Q_LORA, D), full(KV_LORA, D),
                  row(HEAD_PAD), row(HEAD_PAD)],
        out_specs=[row(LAT_PAD), row(D), row(D), full(1, Q_LORA), full(1, KV_LORA), full(1, HEAD_PAD), full(1, HEAD_PAD)],
        out_shape=[jax.ShapeDtypeStruct((t, LAT_PAD), BF16), jax.ShapeDtypeStruct((t, D), BF16), jax.ShapeDtypeStruct((t, D), BF16),
                   jax.ShapeDtypeStruct((1, Q_LORA), F32), jax.ShapeDtypeStruct((1, KV_LORA), F32),
                   jax.ShapeDtypeStruct((1, HEAD_PAD), F32), jax.ShapeDtypeStruct((1, HEAD_PAD), F32)],
        compiler_params=_params("arbitrary"),
    )(dq, dk, dv, lat, qn, ckv_ext, gq, gkv, ghq, ghk, wq, wk, wv, rc, rs)


def _causal_keep(tq):
    r = lax.broadcasted_iota(jnp.int32, (tq, tq), 0)
    c = lax.broadcasted_iota(jnp.int32, (tq, tq), 1)
    return c <= r


def _flash_fwd(q, k, v, *, n_seq, seq, tq, name, hosted=None):
    nq = seq // tq

    def body(q_ref, k_ref, v_ref, o_ref, lse_ref):
        qi = pl.program_id(2)
        qv = q_ref[...]

        def step(j, carry, masked):
            m, l, acc = carry
            kj = k_ref[pl.ds(pl.multiple_of(j * tq, tq), tq), :]
            vj = v_ref[pl.ds(pl.multiple_of(j * tq, tq), tq), :]
            s = _dot_nt(qv, kj) * ATTN_SCALE
            if masked:
                s = jnp.where(_causal_keep(tq), s, NEG)
            m_new = jnp.maximum(m, jnp.max(s, axis=-1, keepdims=True))
            alpha = jnp.exp(m - m_new)
            p = jnp.exp(s - m_new)
            l = alpha * l + jnp.sum(p, axis=-1, keepdims=True)
            acc = alpha * acc + _dot_nn(p.astype(BF16), vj)
            return m_new, l, acc

        init = (jnp.full((tq, 1), NEG, F32), jnp.zeros((tq, 1), F32), jnp.zeros((tq, HEAD_PAD), F32))
        carry = lax.fori_loop(0, qi, lambda j, cr: step(j, cr, False), init)
        m, l, acc = step(qi, carry, True)
        o_ref[...] = (acc / l).astype(BF16)
        lse_ref[...] = jnp.broadcast_to(m + jnp.log(l), (tq, HEAD_PAD))

    qspec = pl.BlockSpec((tq, HEAD_PAD), lambda b, h, i: (b * nq + i, h))
    kspec = pl.BlockSpec((seq, HEAD_PAD), lambda b, h, i: (b, h))
    t = n_seq * seq
    return _call(
        body, name=name, grid=(n_seq, N_HEADS, nq), in_specs=[qspec, kspec, kspec], out_specs=[qspec, qspec],
        out_shape=[jax.ShapeDtypeStruct((t, D), BF16), jax.ShapeDtypeStruct((t, D), F32)], scratch_shapes=[],
        operands=(q, k, v), sem=("parallel", "parallel", "arbitrary"), hosted=hosted)


def _flash_bwd(q, k, v, o, lse, do, *, n_seq, seq, tq, name, hosted=None):
    nq = seq // tq

    def body(q_ref, k_ref, v_ref, o_ref, lse_ref, do_ref, dq_ref, dk_ref, dv_ref, dk_acc, dv_acc):
        j = pl.program_id(2)

        @pl.when(j == 0)
        def _():
            dq_ref[...] = jnp.zeros_like(dq_ref)

        dk_acc[...] = jnp.zeros_like(dk_acc)
        dv_acc[...] = jnp.zeros_like(dv_acc)
        kv = k_ref[...]
        vv = v_ref[...]

        def step(i, masked):
            rows = pl.ds(pl.multiple_of(i * tq, tq), tq)
            qi = q_ref[rows, :]
            doi = do_ref[rows, :]
            delta = jnp.sum(doi.astype(F32) * o_ref[rows, :].astype(F32), axis=-1, keepdims=True)
            s = _dot_nt(qi, kv) * ATTN_SCALE
            p = jnp.exp(s - lse_ref[rows, :][:, :1])
            if masked:
                p = jnp.where(_causal_keep(tq), p, 0.0)
            dv_acc[...] += _dot_tn(p.astype(BF16), doi)
            dp = _dot_nt(doi, vv)
            ds = (p * (dp - delta) * ATTN_SCALE).astype(BF16)
            dk_acc[...] += _dot_tn(ds, qi)
            dq_ref[rows, :] += _dot_nn(ds, kv)

        step(j, True)

        def loop_body(i, carry):
            step(i, False)
            return carry

        lax.fori_loop(j + 1, nq, loop_body, 0)
        dk_ref[...] = dk_acc[...]
        dv_ref[...] = dv_acc[...].astype(BF16)

    full = pl.BlockSpec((seq, HEAD_PAD), lambda b, h, j: (b, h))
    tile = pl.BlockSpec((tq, HEAD_PAD), lambda b, h, j: (b * nq + j, h))
    t = n_seq * seq
    return _call(
        body, name=name, grid=(n_seq, N_HEADS, nq), in_specs=[full, tile, tile, full, full, full],
        out_specs=[full, tile, tile],
        out_shape=[jax.ShapeDtypeStruct((t, D), F32), jax.ShapeDtypeStruct((t, D), F32), jax.ShapeDtypeStruct((t, D), BF16)],
        scratch_shapes=[pltpu.VMEM((tq, HEAD_PAD), F32), pltpu.VMEM((tq, HEAD_PAD), F32)],
        operands=(q, k, v, o, lse, do), sem=("parallel", "parallel", "arbitrary"), hosted=hosted)


CONV_CB = 256


def _shift_down(u, k, row):
    return jnp.where(row >= k, pltpu.roll(u, k, 0), 0.0)


def _shift_up(u, k, row, n):
    return jnp.where(row < n - k, pltpu.roll(u, n - k, 0), 0.0)


def _conv_fwd(conv3, cw, *, n_seq, seq, name, hosted=None):
    def body(c_ref, w_ref, p_ref):
        blk = c_ref[...].astype(F32)
        xc, gb, gc = blk[:, :CONV_CB], blk[:, CONV_CB:2 * CONV_CB], blk[:, 2 * CONV_CB:]
        row = lax.broadcasted_iota(jnp.int32, (seq, CONV_CB), 0)
        u = gc * xc
        z = w_ref[0:1, :] * _shift_down(u, 2, row) + w_ref[1:2, :] * _shift_down(u, 1, row) + w_ref[2:3, :] * u
        p_ref[...] = (gb * z).astype(BF16)

    (p,), got = _call(
        body, name=name, grid=(n_seq, D // CONV_CB),
        in_specs=[pl.BlockSpec((seq, 3 * CONV_CB), lambda b, j: (b, j)), pl.BlockSpec((3, CONV_CB), lambda b, j: (0, j))],
        out_specs=[pl.BlockSpec((seq, CONV_CB), lambda b, j: (b, j))],
        out_shape=[jax.ShapeDtypeStruct((n_seq * seq, D), BF16)], scratch_shapes=[],
        operands=(conv3, cw), sem=("parallel", "parallel"), hosted=hosted)
    return p, got


def _conv_bwd(dp, conv3, cw, *, n_seq, seq, name):
    def body(dp_ref, c_ref, w_ref, dc_ref, dw_ref):
        @pl.when(pl.program_id(1) == 0)
        def _():
            dw_ref[...] = jnp.zeros_like(dw_ref)

        blk = c_ref[...].astype(F32)
        xc, gb, gc = blk[:, :CONV_CB], blk[:, CONV_CB:2 * CONV_CB], blk[:, 2 * CONV_CB:]
        row = lax.broadcasted_iota(jnp.int32, (seq, CONV_CB), 0)
        w0, w1, w2 = w_ref[0:1, :], w_ref[1:2, :], w_ref[2:3, :]
        u = gc * xc
        u1 = _shift_down(u, 1, row)
        u2 = _shift_down(u, 2, row)
        z = w0 * u2 + w1 * u1 + w2 * u
        dpv = dp_ref[...].astype(F32)
        dz = dpv * gb
        du = w2 * dz + w1 * _shift_up(dz, 1, row, seq) + w0 * _shift_up(dz, 2, row, seq)
        dc_ref[...] = jnp.concatenate([du * gc, dpv * z, du * xc], axis=1).astype(BF16)
        dw_ref[0:1, :] += jnp.sum(dz * u2, axis=0, keepdims=True)
        dw_ref[1:2, :] += jnp.sum(dz * u1, axis=0, keepdims=True)
        dw_ref[2:3, :] += jnp.sum(dz * u, axis=0, keepdims=True)

    return pl.pallas_call(
        body, name=name, grid=(D // CONV_CB, n_seq),
        in_specs=[pl.BlockSpec((seq, CONV_CB), lambda j, b: (b, j)), pl.BlockSpec((seq, 3 * CONV_CB), lambda j, b: (b, j)),
                  pl.BlockSpec((3, CONV_CB), lambda j, b: (0, j))],
        out_specs=[pl.BlockSpec((seq, 3 * CONV_CB), lambda j, b: (b, j)), pl.BlockSpec((3, CONV_CB), lambda j, b: (0, j))],
        out_shape=[jax.ShapeDtypeStruct((n_seq * seq, CONV_COLS), BF16), jax.ShapeDtypeStruct((3, D), F32)],
        compiler_params=_params("parallel", "arbitrary"),
    )(dp, conv3, cw)


def _merge_fwd(o, p, gl, bias, x1, wpa, wpc, wout, *, tm, name, hosted=None):
    t = x1.shape[0]

    def body(o_ref, p_ref, gl_ref, b_ref, x_ref, wpa_ref, wpc_ref, wout_ref, x2_ref, mg_ref, ya_ref, yb_ref):
        ya = _dot_nn(o_ref[...], wpa_ref[...])
        yb = _dot_nn(p_ref[...], wpc_ref[...])
        gates = _sigmoid(gl_ref[...].astype(F32) + b_ref[...])
        merged = (gates[:, :D] * ya + gates[:, D:] * yb).astype(BF16)
        ya_ref[...] = ya.astype(BF16)
        yb_ref[...] = yb.astype(BF16)
        mg_ref[...] = merged
        x2_ref[...] = x_ref[...] + _dot_nn(merged, wout_ref[...])

    row = pl.BlockSpec((tm, D), lambda i: (i, 0))
    row2 = pl.BlockSpec((tm, GATE_COLS), lambda i: (i, 0))
    wsp = pl.BlockSpec((D, D), lambda i: (0, 0))
    wide = jax.ShapeDtypeStruct((t, D), BF16)
    return _call(
        body, name=name, grid=(t // tm,),
        in_specs=[row, row, row2, pl.BlockSpec((1, GATE_COLS), lambda i: (0, 0)), row, wsp, wsp, wsp],
        out_specs=[row, row, row, row], out_shape=[jax.ShapeDtypeStruct((t, D), F32), wide, wide, wide], scratch_shapes=[],
        operands=(o, p, gl, bias, x1, wpa, wpc, wout), sem=("parallel",), hosted=hosted)


def _merge_bwd(dx2, ya, yb, gl, bias, wpa, wpc, wout, *, tm, name, hosted=None):
    t = dx2.shape[0]

    def body(dx_ref, ya_ref, yb_ref, gl_ref, b_ref, wpa_ref, wpc_ref, wout_ref,
             dxb_ref, dya_ref, dyb_ref, dgl_ref, do_ref, dp_ref, db_ref):
        @pl.when(pl.program_id(0) == 0)
        def _():
            db_ref[...] = jnp.zeros_like(db_ref)

        dxb = dx_ref[...].astype(BF16)
        dxb_ref[...] = dxb
        dm = _dot_nt(dxb, wout_ref[...])
        gates = _sigmoid(gl_ref[...].astype(F32) + b_ref[...])
        ga, gb = gates[:, :D], gates[:, D:]
        dya = (dm * ga).astype(BF16)
        dyb = (dm * gb).astype(BF16)
        dya_ref[...] = dya
        dyb_ref[...] = dyb
        dgl = jnp.concatenate([dm * ya_ref[...].astype(F32) * ga * (1.0 - ga),
                               dm * yb_ref[...].astype(F32) * gb * (1.0 - gb)], axis=1)
        dgl_ref[...] = dgl.astype(BF16)
        db_ref[...] += jnp.sum(dgl, axis=0, keepdims=True)
        do_ref[...] = _dot_nt(dya, wpa_ref[...]).astype(BF16)
        dp_ref[...] = _dot_nt(dyb, wpc_ref[...]).astype(BF16)

    row = pl.BlockSpec((tm, D), lambda i: (i, 0))
    row2 = pl.BlockSpec((tm, GATE_COLS), lambda i: (i, 0))
    vec2 = pl.BlockSpec((1, GATE_COLS), lambda i: (0, 0))
    wsp = pl.BlockSpec((D, D), lambda i: (0, 0))
    wide = jax.ShapeDtypeStruct((t, D), BF16)
    return _call(
        body, name=name, grid=(t // tm,), in_specs=[row, row, row, row2, vec2, wsp, wsp, wsp],
        out_specs=[row, row, row, row2, row, row, vec2],
        out_shape=[wide, wide, wide, jax.ShapeDtypeStruct((t, GATE_COLS), BF16), wide, wide,
                   jax.ShapeDtypeStruct((1, GATE_COLS), F32)],
        scratch_shapes=[], operands=(dx2, ya, yb, gl, bias, wpa, wpc, wout), sem=("arbitrary",), hosted=hosted)


def _adamw(w, g, m, v, *, name):
    rows, cols = w.shape
    tr = max([c for c in range(8, 513, 8) if rows % c == 0], default=rows)
    c1 = 1.0 / (1.0 - ADAM_B1 ** ADAM_STEP)
    c2 = 1.0 / (1.0 - ADAM_B2 ** ADAM_STEP)

    def body(w_ref, g_ref, m_ref, v_ref, d_ref, nm_ref, nv_ref):
        gv = g_ref[...]
        nm = ADAM_B1 * m_ref[...] + (1.0 - ADAM_B1) * gv
        nv = ADAM_B2 * v_ref[...] + (1.0 - ADAM_B2) * (gv * gv)
        nm_ref[...] = nm
        nv_ref[...] = nv
        d_ref[...] = -ADAM_LR * ((nm * c1) / (jnp.sqrt(nv * c2) + ADAM_EPS) + ADAM_WD * w_ref[...])

    spec = pl.BlockSpec((tr, cols), lambda i: (i, 0))
    shp = jax.ShapeDtypeStruct((rows, cols), F32)
    return pl.pallas_call(
        body, name=name, grid=(rows // tr,), in_specs=[spec] * 4, out_specs=[spec] * 3, out_shape=[shp] * 3,
        compiler_params=_params("parallel"),
    )(w, g, m, v)


def _place():
    return lax.axis_index("x"), lax.axis_index("y"), lax.axis_index("c")


def _other_chips(x, y):
    return [(1 - x, y), (x, 1 - y), (1 - x, 1 - y)]


def _remote(src, dst, send, recv, dev):
    return pltpu.make_async_remote_copy(src_ref=src, dst_ref=dst, send_sem=send, recv_sem=recv, device_id=dev, device_id_type=MESH)


def _gather_chips_plan(n):
    def start(srcs, dsts, send, recv, local):
        x, y, cc = _place()
        me = 4 * x + 2 * y + cc
        for a in range(n):
            pltpu.make_async_copy(srcs[a], dsts[a].at[me], local.at[a]).start()
            for k, (px, py) in enumerate(_other_chips(x, y)):
                _remote(srcs[a], dsts[a].at[me], send.at[3 * a + k], recv.at[3 * a + k], (px, py, cc)).start()

    def wait(srcs, dsts, send, recv, local):
        x, y, cc = _place()
        me = 4 * x + 2 * y + cc
        for a in range(n):
            for k, (px, py) in enumerate(_other_chips(x, y)):
                _remote(srcs[a], dsts[a].at[4 * px + 2 * py + cc], send.at[3 * a + k], recv.at[3 * a + k], (px, py, cc)).wait_recv()
        for a in range(n):
            for k, (px, py) in enumerate(_other_chips(x, y)):
                _remote(srcs[a], dsts[a].at[me], send.at[3 * a + k], recv.at[3 * a + k], (px, py, cc)).wait_send()
            pltpu.make_async_copy(srcs[a], dsts[a].at[me], local.at[a]).wait()

    return _Plan(start, wait, 3 * n, n)


def _scatter_chips_plan(n):
    def start(srcs, dsts, send, recv, local):
        x, y, cc = _place()
        for a in range(n):
            for k, (px, py) in enumerate(_other_chips(x, y)):
                _remote(srcs[a].at[2 * px + py], dsts[a].at[k], send.at[3 * a + k], recv.at[3 * a + k], (px, py, cc)).start()

    def wait(srcs, dsts, send, recv, local):
        x, y, cc = _place()
        for a in range(n):
            for k, (px, py) in enumerate(_other_chips(x, y)):
                _remote(srcs[a].at[k], dsts[a].at[k], send.at[3 * a + k], recv.at[3 * a + k], (px, py, cc)).wait_recv()
        for a in range(n):
            for k, (px, py) in enumerate(_other_chips(x, y)):
                _remote(srcs[a].at[k], dsts[a].at[k], send.at[3 * a + k], recv.at[3 * a + k], (px, py, cc)).wait_send()

    return _Plan(start, wait, 3 * n, 0)


def _gather_shapes(blocks):
    return [jax.ShapeDtypeStruct((N_DEV,) + b.shape, b.dtype) for b in blocks]


def _scatter_shapes(parts):
    return [jax.ShapeDtypeStruct((3,) + p.shape[1:], p.dtype) for p in parts]


def _gather_sibling_plan(n):
    def start(srcs, dsts, send, recv, local):
        x, y, cc = _place()
        for a in range(n):
            for q in range(4):
                _remote(srcs[a].at[2 * q + cc], dsts[a].at[2 * q + cc], send.at[4 * a + q], recv.at[4 * a + q], (x, y, 1 - cc)).start()

    def wait(srcs, dsts, send, recv, local):
        x, y, cc = _place()
        for a in range(n):
            for q in range(4):
                _remote(srcs[a].at[2 * q + cc], dsts[a].at[2 * q + 1 - cc], send.at[4 * a + q], recv.at[4 * a + q],
                        (x, y, 1 - cc)).wait_recv()
        for a in range(n):
            for q in range(4):
                _remote(srcs[a].at[2 * q + cc], dsts[a].at[2 * q + cc], send.at[4 * a + q], recv.at[4 * a + q],
                        (x, y, 1 - cc)).wait_send()

    return _Plan(start, wait, 4 * n, 0, in_place=True)


def _scatter_sibling_plan(n):
    def start(srcs, dsts, send, recv, local):
        x, y, cc = _place()
        for a in range(n):
            for q in range(4):
                _remote(srcs[a].at[2 * q + 1 - cc], dsts[a].at[q], send.at[4 * a + q], recv.at[4 * a + q], (x, y, 1 - cc)).start()

    def wait(srcs, dsts, send, recv, local):
        x, y, cc = _place()
        for a in range(n):
            for q in range(4):
                _remote(srcs[a].at[q], dsts[a].at[q], send.at[4 * a + q], recv.at[4 * a + q], (x, y, 1 - cc)).wait_recv()
        for a in range(n):
            for q in range(4):
                _remote(srcs[a].at[q], dsts[a].at[q], send.at[4 * a + q], recv.at[4 * a + q], (x, y, 1 - cc)).wait_send()

    return _Plan(start, wait, 4 * n, 0)


def _same_shapes(arrs):
    return [jax.ShapeDtypeStruct(a.shape, a.dtype) for a in arrs]


def _halved_shapes(parts):
    return [jax.ShapeDtypeStruct((4,) + p.shape[1:], p.dtype) for p in parts]


def _run_plan(plan, srcs, out_shapes, *, name):
    n_in, n_out = len(srcs), len(out_shapes)

    def body(*refs):
        h_in, h_out, sems = refs[:n_in], refs[n_in:n_in + n_out], refs[n_in + n_out:]
        plan.start(h_in, h_out, *sems)
        plan.wait(h_in, h_out, *sems)

    return pl.pallas_call(body, name=name, in_specs=[ANY] * n_in, out_specs=[ANY] * n_out, out_shape=list(out_shapes),
                          input_output_aliases={a: a for a in range(n_in)} if plan.in_place else {},
                          scratch_shapes=plan.sems())(*srcs)


SEM = pl.BlockSpec(memory_space=pltpu.SEMAPHORE)
HBM = pl.BlockSpec(memory_space=pltpu.HBM)
SIDE_EFFECT = pltpu.CompilerParams(has_side_effects=pltpu.SideEffectType.DATAFLOW_SIDE_EFFECTING)


def _plan_start(plan, blocks, land_shapes, *, name):
    n = len(blocks)
    lands = [lax.empty(s.shape, s.dtype) for s in land_shapes]

    def body(*refs):
        srcs, sems, lands_out, token = refs[:n], refs[2 * n:2 * n + 3], refs[3 * n + 3:4 * n + 3], refs[4 * n + 3]
        plan.start(srcs, lands_out, *sems)
        token[...] = jnp.zeros_like(token)

    out_shape = ([s for s in plan.sems()] + [pltpu.HBM(b.shape, b.dtype) for b in blocks]
                 + [pltpu.HBM(l.shape, l.dtype) for l in lands] + [jax.ShapeDtypeStruct((8, 128), F32)])
    res = pl.pallas_call(
        body, name=name, in_specs=[HBM] * (2 * n), out_specs=[SEM] * 3 + [HBM] * (2 * n) + [pl.BlockSpec(memory_space=pltpu.VMEM)],
        out_shape=out_shape, input_output_aliases={a: 3 + a for a in range(2 * n)}, compiler_params=SIDE_EFFECT,
    )(*[pltpu.with_memory_space_constraint(a, pltpu.HBM) for a in list(blocks) + lands])
    return res[:3], res[3:3 + n], res[3 + n:3 + 2 * n], res[3 + 2 * n]


def _plan_wait(plan, sems, blocks, lands, after, *, name):
    n = len(blocks)

    def body(*refs):
        plan.wait(refs[:n], refs[n:2 * n], *refs[2 * n:2 * n + 3])

    res = pl.pallas_call(
        body, name=name, in_specs=[HBM] * (2 * n) + [SEM] * 3 + [ANY], out_specs=[HBM] * (2 * n),
        out_shape=[pltpu.HBM(a.shape, a.dtype) for a in list(blocks) + list(lands)],
        input_output_aliases={a: a for a in range(2 * n)}, compiler_params=SIDE_EFFECT,
    )(*blocks, *lands, *sems, after)
    return list(res[:n]), list(res[n:])


def _sum_sibling(p, q, core, *, name):
    _, r, c = p.shape

    def body(core_ref, p_ref, q_ref, o_ref):
        o_ref[...] = (p_ref[...].astype(F32) + q_ref[...].astype(F32)).astype(BF16)

    grid_spec = pltpu.PrefetchScalarGridSpec(
        num_scalar_prefetch=1, grid=(4,),
        in_specs=[pl.BlockSpec((1, r, c), lambda ch, core_ref: (2 * ch + core_ref[0], 0, 0)),
                  pl.BlockSpec((1, r, c), lambda ch, core_ref: (ch, 0, 0))],
        out_specs=pl.BlockSpec((1, r, c), lambda ch, core_ref: (ch, 0, 0)))
    return pl.pallas_call(
        body, name=name, grid_spec=grid_spec, out_shape=jax.ShapeDtypeStruct((4, r, c), BF16),
        compiler_params=_params("parallel"),
    )(core, p, q)


def _sum_chips(s1, r2, chip, *, name):
    _, r, c = s1.shape

    def body(chip_ref, s_ref, r_ref, o_ref):
        acc = s_ref[0].astype(F32)
        for k in range(3):
            acc = acc + r_ref[k].astype(F32)
        o_ref[...] = acc

    grid_spec = pltpu.PrefetchScalarGridSpec(
        num_scalar_prefetch=1, grid=(1,),
        in_specs=[pl.BlockSpec((1, r, c), lambda i, chip_ref: (chip_ref[0], 0, 0)),
                  pl.BlockSpec((3, r, c), lambda i, chip_ref: (0, 0, 0))],
        out_specs=pl.BlockSpec((r, c), lambda i, chip_ref: (0, 0)))
    return pl.pallas_call(
        body, name=name, grid_spec=grid_spec, out_shape=jax.ShapeDtypeStruct((r, c), F32),
        compiler_params=_params("arbitrary"),
    )(chip, s1, r2)


def _sum_adamw(s1, r2, chip, w, m, v, *, name):
    _, r, c = s1.shape
    c1 = 1.0 / (1.0 - ADAM_B1 ** ADAM_STEP)
    c2 = 1.0 / (1.0 - ADAM_B2 ** ADAM_STEP)

    def body(chip_ref, s_ref, r_ref, w_ref, m_ref, v_ref, g_ref, d_ref, nm_ref, nv_ref):
        gv = s_ref[0].astype(F32)
        for k in range(3):
            gv = gv + r_ref[k].astype(F32)
        g_ref[...] = gv
        nm = ADAM_B1 * m_ref[...] + (1.0 - ADAM_B1) * gv
        nv = ADAM_B2 * v_ref[...] + (1.0 - ADAM_B2) * (gv * gv)
        nm_ref[...] = nm
        nv_ref[...] = nv
        d_ref[...] = -ADAM_LR * ((nm * c1) / (jnp.sqrt(nv * c2) + ADAM_EPS) + ADAM_WD * w_ref[...])

    flat = pl.BlockSpec((r, c), lambda i, chip_ref: (0, 0))
    grid_spec = pltpu.PrefetchScalarGridSpec(
        num_scalar_prefetch=1, grid=(1,),
        in_specs=[pl.BlockSpec((1, r, c), lambda i, chip_ref: (chip_ref[0], 0, 0)),
                  pl.BlockSpec((3, r, c), lambda i, chip_ref: (0, 0, 0)), flat, flat, flat],
        out_specs=[flat] * 4)
    return pl.pallas_call(
        body, name=name, grid_spec=grid_spec, out_shape=[jax.ShapeDtypeStruct((r, c), F32)] * 4,
        compiler_params=_params("arbitrary"),
    )(chip, s1, r2, w, m, v)


def _small_exchange(v, *, reduce, name):
    r, c = v.shape

    def body(x_ref, o_ref, *rest):
        if reduce:
            buf_ref, send_sems, recv_sems = rest
        else:
            buf_ref = o_ref
            send_sems, recv_sems = rest
        x, y, cc = _place()
        me = 4 * x + 2 * y + cc

        def peer(k):
            return ((1 - x) if k & 4 else x, (1 - y) if k & 2 else y, (1 - cc) if k & 1 else cc)

        buf_ref[me] = x_ref[...]
        sends = []
        for k in range(1, N_DEV):
            cp = pltpu.make_async_remote_copy(src_ref=x_ref, dst_ref=buf_ref.at[me], send_sem=send_sems.at[k - 1],
                                              recv_sem=recv_sems.at[k - 1], device_id=peer(k), device_id_type=MESH)
            cp.start()
            sends.append(cp)
        for k in range(1, N_DEV):
            px, py, pc = peer(k)
            pltpu.make_async_remote_copy(src_ref=x_ref, dst_ref=buf_ref.at[4 * px + 2 * py + pc], send_sem=send_sems.at[k - 1],
                                         recv_sem=recv_sems.at[k - 1], device_id=peer(k), device_id_type=MESH).wait_recv()
        for cp in sends:
            cp.wait_send()
        if reduce:
            acc = buf_ref[0]
            for s in range(1, N_DEV):
                acc = acc + buf_ref[s]
            o_ref[...] = acc

    vm = pl.BlockSpec(memory_space=pltpu.VMEM)
    sems = [pltpu.SemaphoreType.DMA((N_DEV - 1,)), pltpu.SemaphoreType.DMA((N_DEV - 1,))]
    if reduce:
        out_shape, scratch = jax.ShapeDtypeStruct((r, c), F32), [pltpu.VMEM((N_DEV, r, c), F32)] + sems
    else:
        out_shape, scratch = jax.ShapeDtypeStruct((N_DEV, r, c), F32), sems
    return pl.pallas_call(body, name=name, in_specs=[vm], out_specs=vm, out_shape=out_shape, scratch_shapes=scratch)(v)


def _rows(a):
    return a.reshape(-1, D)


def _pad_cols(a, to):
    return jnp.pad(a, ((0, 0), (0, to - a.shape[1])))


def _pack_weights(w):
    parts = {
        "w_inT": jnp.pad(w["w_in"].T, ((0, IN_SHARD_PAD - IN_SHARD), (0, 0))),
        "w_uq": _rows(_head_cols(w["w_uq"])), "w_uk": _rows(_head_cols(w["w_uk"])),
        "w_uv": _rows(_pad_cols(w["w_uv"], HEAD_PAD)), "w_pa": _rows(w["w_proj_attn"]),
        "w_pc": w["w_proj_conv"], "w_out": w["w_out"],
    }
    return [jnp.concatenate([parts[n].astype(BF16) for n, _ in group], axis=0) for group in PACK]


def _cols_from_shards(gs, name, rows):
    idx, off, r = PACK_OFF[name]
    return gs[idx][:, off:off + r].reshape(N_DEV, rows, HEAD_PAD).transpose(1, 0, 2).reshape(rows, N_DEV * HEAD_PAD)


def _rows_from_shards(gs, name, keep=None):
    idx, off, r = PACK_OFF[name]
    keep = r if keep is None else keep
    return gs[idx][:, off:off + keep].reshape(N_DEV * keep, D)


def _rope_placement():
    i = lax.broadcasted_iota(jnp.int32, (HEAD_PAD, D), 0)
    j = lax.broadcasted_iota(jnp.int32, (HEAD_PAD, D), 1)
    lane = jnp.where(i < ROPE_HALF, 32 + i, 96 + i - ROPE_HALF)
    return ((i < 2 * ROPE_HALF) & (j % HEAD_PAD == lane)).astype(BF16)


def _unpack_in(g_in):
    w_inT = _rows_from_shards([g_in, None], "w_inT", IN_SHARD)
    lat_rows = Q_LORA + KV_LORA + 2 * ROPE_HALF
    conv = w_inT[lat_rows:lat_rows + CONV_COLS].reshape(3, D // CONV_CB, CONV_CB, D).transpose(1, 0, 2, 3).reshape(CONV_COLS, D)
    return {"latT": jnp.pad(w_inT[:lat_rows], ((0, LAT_PAD - lat_rows), (0, 0))), "convT": conv,
            "gateT": w_inT[lat_rows + CONV_COLS:]}


def _unpack_misc(g_misc):
    g = [None, g_misc]
    wpa = _cols_from_shards(g, "w_pa", 512).reshape(N_HEADS, NOPE, D)
    return {
        "wq": _cols_from_shards(g, "w_uq", Q_LORA),
        "wk": jnp.concatenate([_cols_from_shards(g, "w_uk", KV_LORA), _rope_placement()], axis=0),
        "wv": _cols_from_shards(g, "w_uv", KV_LORA),
        "wpa": jnp.pad(wpa, ((0, 0), (0, HEAD_PAD - NOPE), (0, 0))).reshape(D, D),
        "wpc": _rows_from_shards(g, "w_pc"), "wout": _rows_from_shards(g, "w_out"),
    }


def _shards_from_cols(a):
    rows = a.shape[0]
    return a.reshape(rows, N_DEV, HEAD_PAD).transpose(1, 0, 2).reshape(N_DEV, rows * HEAD_PAD // D, D)


def _pack_grads(gw):
    lat_rows = Q_LORA + KV_LORA + 2 * ROPE_HALF
    conv = gw["convT"].reshape(D // CONV_CB, 3, CONV_CB, D).transpose(1, 0, 2, 3).reshape(CONV_COLS, D)
    w_inT = jnp.concatenate([gw["latT"][:lat_rows], conv, gw["gateT"]], axis=0).reshape(N_DEV, IN_SHARD, D)
    wpa = gw["wpa"].reshape(N_HEADS, HEAD_PAD, D)[:, :NOPE].reshape(N_HEADS * NOPE, D)
    parts = {}
    parts.update({
        "w_inT": jnp.pad(w_inT, ((0, 0), (0, IN_SHARD_PAD - IN_SHARD), (0, 0))),
        "w_uq": _shards_from_cols(gw["wq"]), "w_uk": _shards_from_cols(gw["wk"][:KV_LORA]),
        "w_uv": _shards_from_cols(gw["wv"][:KV_LORA]), "w_pa": _shards_from_cols(wpa),
        "w_pc": gw["wpc"].reshape(N_DEV, D // N_DEV, D), "w_out": gw["wout"].reshape(N_DEV, D // N_DEV, D),
    })
    return [jnp.concatenate([parts[n] for n, _ in group], axis=1) for group in PACK]


def _unpack_grads(mines):
    def seg(name, keep=None):
        idx, off, r = PACK_OFF[name]
        return mines[idx][off:off + (r if keep is None else keep)]

    return {
        "w_in": seg("w_inT", IN_SHARD).T,
        "w_uq": _head_cols_inv(seg("w_uq").reshape(Q_LORA, HEAD_PAD), QK_DIM),
        "w_uk": _head_cols_inv(seg("w_uk").reshape(KV_LORA, HEAD_PAD), NOPE),
        "w_uv": seg("w_uv").reshape(KV_LORA, HEAD_PAD)[:, :NOPE],
        "w_proj_attn": seg("w_pa").reshape(512, HEAD_PAD),
        "w_proj_conv": seg("w_pc"), "w_out": seg("w_out"),
    }


def _rope_tables(positions):
    lane = jnp.arange(HEAD_PAD)
    idx = jnp.where((lane >= 32) & (lane < 48), lane - 32, jnp.where((lane >= 96) & (lane < 112), lane - 96, -1))
    inv_freq = jnp.where(idx >= 0, 1.0 / (ROPE_THETA ** (idx.astype(F32) / ROPE_HALF)), 0.0)
    ang = positions.reshape(-1).astype(F32)[:, None] * inv_freq
    return jnp.cos(ang), jnp.sin(ang) * jnp.where(lane < HEAD_PAD // 2, -1.0, 1.0)


def _local_step(x, positions, target, conv_w, small, ex):
    n_seq, seq, d = x.shape
    t = n_seq * seq
    x0 = x.reshape(t, d)
    tgt = target.reshape(t, d)
    rc, rs = _rope_tables(positions)
    ghq = _head_cols(small["q_head_norm"])
    ghk = _head_cols(small["k_head_norm"])
    TM, HC, TQ = 1024, 256, 1024

    def mm(*args, hosted=None, **kw):
        res = _mm(*args, hosted=hosted, **kw)
        return res if hosted is not None else (res, None)

    def wgrad(a, b, name, tm=None, hosted=None):
        tm = tm or a.shape[1]
        return mm(a, b, mode="tn", out_dtype=BF16, tm=tm, tn=b.shape[1], tk=2048 if tm <= D else 1024, name=name, hosted=hosted)

    f1g, f1u, f1d = ex.gather_finish(ex.witness() + rc[:8] + conv_w[:1, :HEAD_PAD])
    (x1, h1, a1, b1), got = _ffn_fwd(x0, small["ffn1_norm"], f1g, f1u, f1d, tm=512, hc=DFF // 2, name="ffn1_fwd",
                                     hosted=ex.gather_chips("mix_in"))
    hm, got = _rms_fwd(x1, small["mix_norm"], tm=TM, name="mix_norm_fwd", hosted=ex.gather_sibling(got))
    W = ex.mix_in_weights(got)
    (lat, conv3, gl), got = _proj_fwd(hm, W["latT"], W["convT"], W["gateT"], tm=512, name="proj_fwd",
                                      hosted=ex.gather_chips("mix_misc"))
    p, got = _conv_fwd(conv3, conv_w, n_seq=n_seq, seq=seq, name="conv_fwd", hosted=ex.gather_sibling(got))
    W.update(ex.mix_misc_weights(got))
    q, k, v, qn, ckv = _mla_prep_fwd(lat, small["q_a_norm"], small["kv_a_norm"], ghq, ghk, W["wq"], W["wk"], W["wv"], rc, rs,
                                     tm=512, name="mla_prep_fwd")
    (o, lse), got = _flash_fwd(q, k, v, n_seq=n_seq, seq=seq, tq=TQ, name="attn_fwd", hosted=ex.gather_chips("ffn2"))
    (x2, merged, ya, yb), got = _merge_fwd(o, p, gl, small["gate_bias"], x1, W["wpa"], W["wpc"], W["wout"], tm=512, name="merge_fwd",
                                           hosted=ex.gather_sibling(got))
    f2g, f2u, f2d = ex.ffn_weights(got)
    (dy, h2, a2, b2, loss_row), _ = _ffn_fwd(x2, small["ffn2_norm"], f2g, f2u, f2d, tm=512, hc=DFF // 2, name="ffn2_fwd", target=tgt)

    gw, gs = {}, {}
    (da2, db2, *ffn2_grads), _ = _ffn_grads(dy, h2, a2, b2, f2d, tm=TM, hc=HC, name="ffn2_grads")
    (dx2, gs["ffn2_norm"]), _ = _ffn_up_bwd(da2, db2, f2g, f2u, x2, small["ffn2_norm"], dy, tm=512, name="ffn2_up_bwd")

    (dx2b, dya, dyb, dgl, do, dp, gs["gate_bias"]), got = _merge_bwd(
        dx2, ya, yb, gl, small["gate_bias"], W["wpa"], W["wpc"], W["wout"], tm=512, name="merge_bwd",
        hosted=ex.scatter_sibling("ffn2", ffn2_grads))
    ex.scatter_sibling_done("ffn2", got)
    gw["wout"] = wgrad(merged, dx2b, "dw_out")[0]
    gw["wpa"] = wgrad(o, dya, "dw_pa")[0]
    gw["wpc"] = wgrad(p, dyb, "dw_pc")[0]
    dconv3, dconv_w = _conv_bwd(dp, conv3, conv_w, n_seq=n_seq, seq=seq, name="conv_bwd")
    (dq, dk, dv), got = _flash_bwd(q, k, v, o, lse, do, n_seq=n_seq, seq=seq, tq=TQ, name="attn_bwd",
                                   hosted=ex.scatter_chips("ffn2"))
    ex.scatter_chips_done("ffn2", got)
    dlat, dqp, dkp, gs["q_a_norm"], gs["kv_a_norm"], dghq, dghk = _mla_prep_bwd(
        dq, dk, dv, lat, qn, ckv, small["q_a_norm"], small["kv_a_norm"], ghq, ghk, W["wq"], W["wk"], W["wv"], rc, rs,
        tm=512, name="mla_prep_bwd")
    gs["q_head_norm"], gs["k_head_norm"] = _head_cols_inv(dghq, QK_DIM), _head_cols_inv(dghk, QK_DIM)
    gw["wq"] = wgrad(qn, dqp, "dw_uq")[0]
    gw["wk"] = wgrad(ckv, dkp, "dw_uk")[0]
    gw["wv"] = wgrad(ckv, dv, "dw_uv")[0]
    gw["convT"] = wgrad(dconv3, hm, "dw_conv", tm=CONV_COLS // 2)[0]
    gw["gateT"] = wgrad(dgl, hm, "dw_gate")[0]
    gw["latT"] = wgrad(dlat, hm, "dw_lat")[0]
    ex.scatter_sibling_now("mix", gw)
    (dx1, gs["mix_norm"]), got = _proj_bwd(dlat, dconv3, dgl, W["latT"], W["convT"], W["gateT"], x1, small["mix_norm"], dx2,
                                           tm=512, name="proj_bwd", hosted=ex.scatter_chips("mix_in"))
    ex.scatter_chips_done("mix_in", got)
    ex.reduce_small(gs, dconv_w, loss_row)

    (da1, db1, *ffn1_grads), got = _ffn_grads(dx1, h1, a1, b1, f1d, tm=TM, hc=HC, name="ffn1_grads",
                                              hosted=ex.scatter_chips("mix_misc"))
    ex.scatter_chips_done("mix_misc", got)
    ex.scatter_sibling_now("ffn1", ffn1_grads)
    zero = ex.scatter_chips_start("ffn1")
    (dx0, gs["ffn1_norm"]), _ = _ffn_up_bwd(da1, db1, f1g, f1u, x0, small["ffn1_norm"] + zero, dx1, tm=512, name="ffn1_up_bwd")
    return dx0.reshape(n_seq, seq, d), gs["ffn1_norm"]


class _MeshExchange:
    def __init__(self, w, core, chip):
        self.w, self.core, self.chip = w, core, chip
        self.partial, self.received, self._cache = {}, {}, {}

    def _blocks(self, group):
        w = self.w
        if group not in self._cache:
            if group.startswith("ffn"):
                self._cache[group] = [w[group + "_w_gate"].T.astype(BF16), w[group + "_w_up"].T.astype(BF16),
                                      w[group + "_w_down"].astype(BF16)]
            else:
                self._cache["mix_in"], self._cache["mix_misc"] = [[b] for b in _pack_weights(w)]
        return self._cache[group]

    def gather_chips(self, *groups):
        blocks = [b for group in groups for b in self._blocks(group)]
        return _gather_chips_plan(len(blocks)), blocks, _gather_shapes(blocks)

    def gather_sibling(self, got):
        half = list(got)
        return _gather_sibling_plan(len(half)), half, _same_shapes(half)

    def gather_start(self, group):
        blocks = self._blocks(group)
        plan = _gather_chips_plan(len(blocks))
        sems, blocks, lands, token = _plan_start(plan, blocks, _gather_shapes(blocks), name="gather_%s_start" % group)
        self._gathering = (group, plan, sems, blocks, lands)
        return token[0, 0]

    def gather_finish(self, after):
        group, plan, sems, blocks, lands = self._gathering
        _, half = _plan_wait(plan, sems, blocks, lands, after, name="gather_%s_wait" % group)
        return self.ffn_weights(_run_plan(_gather_sibling_plan(len(half)), half, _same_shapes(half), name="gather_%s_sibling" % group))

    def reduce_small(self, gs, dconv_w, loss_row):
        pieces = [_pad_cols(gs[n], SMALL_SLOTS[n]) for n in SMALL_NAMES[1:]] + [dconv_w.reshape(1, 3 * D), loss_row]
        self.small_total = _small_exchange(jnp.concatenate(pieces, axis=1).reshape(-1, 128), reduce=True,
                                           name="reduce_small").reshape(-1)

    def scatter_chips_start(self, group):
        s1 = self.partial[group]
        plan = _scatter_chips_plan(len(s1))
        sems, s1, lands, token = _plan_start(plan, s1, _scatter_shapes(s1), name="scatter_%s_start" % group)
        self._scattering = (group, plan, sems, s1, lands)
        return token[0, 0]

    def scatter_chips_finish(self, after):
        group, plan, sems, s1, lands = self._scattering
        self.partial[group], self.received[group] = _plan_wait(plan, sems, s1, lands, after, name="scatter_%s_wait" % group)

    def witness(self):
        parts = [b[:8, :128].astype(F32) for g in ("mix_in", "mix_misc", "ffn2") for b in self._blocks(g)]
        return functools.reduce(jnp.add, parts)

    def ffn_weights(self, got):
        return [a.reshape(DFF, D) for a in got]

    def mix_in_weights(self, got):
        return _unpack_in(got[0])

    def mix_misc_weights(self, got):
        return _unpack_misc(got[0])

    def _parts(self, group, grads):
        if group == "mix":
            return _pack_grads(grads), ["mix_in", "mix_misc"]
        parts = [g.reshape(N_DEV, -1, D) for g in grads]
        return parts, ([group] if len(parts) == 1 else None)

    def scatter_sibling(self, group, grads):
        self._sent, self._names = self._parts(group, grads)
        return _scatter_sibling_plan(len(self._sent)), self._sent, _halved_shapes(self._sent)

    def scatter_sibling_done(self, group, got):
        sums = [_sum_sibling(p, q, self.core, name="sum_%s_sibling_%d" % (group, i)) for i, (p, q) in enumerate(zip(self._sent, got))]
        if self._names is None:
            self.partial[group] = sums
        else:
            for n, s in zip(self._names, sums):
                self.partial[n] = [s]

    def scatter_sibling_now(self, group, grads):
        plan, parts, shapes = self.scatter_sibling(group, grads)
        self.scatter_sibling_done(group, _run_plan(plan, parts, shapes, name="scatter_%s_sibling" % group))

    def scatter_chips(self, group):
        s1 = self.partial[group]
        return _scatter_chips_plan(len(s1)), s1, _scatter_shapes(s1)

    def scatter_chips_done(self, group, got):
        self.received[group] = list(got)


SMALL_NAMES = ("ffn1_norm", "mix_norm", "gate_bias", "q_a_norm", "kv_a_norm", "q_head_norm", "k_head_norm", "ffn2_norm")
SMALL_SLOTS = {"ffn1_norm": 1024, "mix_norm": 1024, "gate_bias": 2048, "q_a_norm": 384, "kv_a_norm": 256, "q_head_norm": 128,
               "k_head_norm": 128, "ffn2_norm": 1024, "conv_w": 3072, "loss": 128}
COLUMN_MAJOR = ("w_in", "w_uq", "w_uk", "w_uv")
WEIGHT_NAMES = ("ffn1_norm", "ffn1_w_gate", "ffn1_w_up", "ffn1_w_down", "mix_norm", "w_in", "gate_bias", "q_a_norm", "w_uq",
                "kv_a_norm", "w_uk", "w_uv", "q_head_norm", "k_head_norm", "w_proj_attn", "conv_w", "w_proj_conv", "w_out",
                "ffn2_norm", "ffn2_w_gate", "ffn2_w_up", "ffn2_w_down")


def _step(x, positions, loss_target, w, m, v):
    xi, yi, ci = _place()
    core = ci.astype(jnp.int32).reshape(1)
    chip = (2 * xi + yi).astype(jnp.int32).reshape(1)
    me = 4 * xi + 2 * yi + ci

    ex = _MeshExchange(w, core, chip)
    cw_all = _small_exchange(jnp.pad(w["conv_w"], ((0, 5), (0, 0))), reduce=False, name="gather_conv_w")
    conv_w = cw_all[:, :3].transpose(1, 0, 2).reshape(3, D)
    ex.w = {n: (a + cw_all[0, 7, 0] if n.startswith("ffn1") else a) for n, a in w.items()}
    zero = ex.gather_start("ffn1")
    ex.w = {n: (a if n.startswith("ffn1") else a + zero) for n, a in w.items()}
    small = {n: w[n].reshape(1, -1) for n in SMALL_NAMES}

    grad_x, dffn1_norm = _local_step(x, positions + zero.astype(jnp.int32), loss_target, conv_w, small, ex)

    grads, deltas, new_m, new_v = {}, {}, {}, {}

    def ffn_update(group):
        for i, n in enumerate((group + "_w_gate", group + "_w_up", group + "_w_down")):
            transposed = not n.endswith("down")
            wv, mv, vv = (a[n].T if transposed else a[n] for a in (w, m, v))
            res = _sum_adamw(ex.partial[group][i], ex.received[group][i], chip, wv, mv, vv, name="adamw_" + n)
            grads[n], deltas[n], new_m[n], new_v[n] = (r.T if transposed else r for r in res)

    def update(n):
        shape = w[n].shape
        if n in COLUMN_MAJOR:
            ops = [a.T for a in (w[n], grads[n], m[n], v[n])]
            deltas[n], new_m[n], new_v[n] = (r.T for r in _adamw(*ops, name="adamw_" + n))
            return
        view = shape if len(shape) == 2 else ((-1, 128) if shape[0] % 128 == 0 else (1, shape[0]))
        dlt, nm, nv = _adamw(w[n].reshape(view), grads[n].reshape(view), m[n].reshape(view), v[n].reshape(view), name="adamw_" + n)
        deltas[n], new_m[n], new_v[n] = dlt.reshape(shape), nm.reshape(shape), nv.reshape(shape)

    ffn_update("ffn2")
    grads.update(_unpack_grads([_sum_chips(ex.partial[g][0], ex.received[g][0], chip, name="sum_%s_chips" % g)
                                for g in ("mix_in", "mix_misc")]))
    total, off = ex.small_total, 0
    for n in SMALL_NAMES[1:]:
        grads[n] = total[off:off + w[n].shape[0]]
        off += SMALL_SLOTS[n]
    conv_full = total[off:off + 3 * D].reshape(3, D)
    grads["conv_w"] = lax.dynamic_slice(conv_full, (0, me * HEAD_PAD), (3, HEAD_PAD))
    loss = total[off + 3 * D]
    later = ("ffn1_norm", "ffn1_w_gate", "ffn1_w_up", "ffn1_w_down")
    for n in WEIGHT_NAMES:
        if n not in deltas and n not in later:
            update(n)

    done = [deltas[n][:8, :128] for n in ("ffn2_w_down", "w_in", "w_out", "w_proj_attn")] + [deltas["mix_norm"].reshape(8, 128)]
    ex.scatter_chips_finish(functools.reduce(jnp.add, done) + grad_x.reshape(-1, D)[:8, :128])
    ffn_update("ffn1")
    last = dffn1_norm + 0.0 * grads["ffn1_w_down"][:1, :1]
    grads["ffn1_norm"] = _small_exchange(last.reshape(-1, 128), reduce=True, name="reduce_ffn1_norm").reshape(-1)
    update("ffn1_norm")
    return (loss, grad_x, *[grads[n] for n in WEIGHT_NAMES], *[deltas[n] for n in WEIGHT_NAMES],
            *[new_m[n] for n in WEIGHT_NAMES], *[new_v[n] for n in WEIGHT_NAMES])


def kernel(x, positions, ffn1_norm, ffn1_w_gate, ffn1_w_up, ffn1_w_down, mix_norm, w_in, gate_bias, q_a_norm, w_uq, kv_a_norm, w_uk, w_uv, q_head_norm, k_head_norm, w_proj_attn, conv_w, w_proj_conv, w_out, ffn2_norm, ffn2_w_gate, ffn2_w_up, ffn2_w_down, loss_target, m_ffn1_norm, m_ffn1_w_gate, m_ffn1_w_up, m_ffn1_w_down, m_mix_norm, m_w_in, m_gate_bias, m_q_a_norm, m_w_uq, m_kv_a_norm, m_w_uk, m_w_uv, m_q_head_norm, m_k_head_norm, m_w_proj_attn, m_conv_w, m_w_proj_conv, m_w_out, m_ffn2_norm, m_ffn2_w_gate, m_ffn2_w_up, m_ffn2_w_down, v_ffn1_norm, v_ffn1_w_gate, v_ffn1_w_up, v_ffn1_w_down, v_mix_norm, v_w_in, v_gate_bias, v_q_a_norm, v_w_uq, v_kv_a_norm, v_w_uk, v_w_uv, v_q_head_norm, v_k_head_norm, v_w_proj_attn, v_conv_w, v_w_proj_conv, v_w_out, v_ffn2_norm, v_ffn2_w_gate, v_ffn2_w_up, v_ffn2_w_down):
    given = dict(locals())
    w = {n: given[n] for n in WEIGHT_NAMES}
    m = {n: given["m_" + n] for n in WEIGHT_NAMES}
    v = {n: given["v_" + n] for n in WEIGHT_NAMES}
    return _step(x, positions, loss_target, w, m, v)
```

```python
import functools

import jax
import jax.numpy as jnp
from jax import lax
from jax.experimental import pallas as pl
from jax.experimental.pallas import tpu as pltpu

F32 = jnp.float32
BF16 = jnp.bfloat16
MESH = pl.DeviceIdType.MESH
ANY = pl.BlockSpec(memory_space=pl.ANY)

N_DEV = 8
D = 1024
DFF = 2816
N_HEADS = 8
HEAD_PAD = 128
QK_DIM = 96
NOPE = 64
ROPE_HALF = 16
Q_LORA = 384
KV_LORA = 256
LAT_PAD = 768
CONV_COLS = 3072
GATE_COLS = 2048
IN_DIM = 5792
IN_SHARD = IN_DIM // N_DEV
IN_SHARD_PAD = 736
FF_SHARD = DFF // N_DEV
ROPE_THETA = 10000.0
NORM_EPS = 1e-6
ATTN_SCALE = QK_DIM ** -0.5
NEG = -1e30

ADAM_LR, ADAM_B1, ADAM_B2, ADAM_EPS, ADAM_WD, ADAM_STEP = 0.001, 0.9, 0.999, 1e-08, 0.01, 10

PACK = ((("w_inT", IN_SHARD_PAD),), (("w_uq", 48), ("w_uk", 32), ("w_uv", 32), ("w_pa", 64), ("w_pc", 128), ("w_out", 128)))
PACK_OFF = {}
for _i, _group in enumerate(PACK):
    _o = 0
    for _n, _r in _group:
        PACK_OFF[_n] = (_i, _o, _r)
        _o += _r

VMEM_LIMIT = 56 * 1024 * 1024


def _params(*sem):
    return pltpu.CompilerParams(dimension_semantics=sem if sem else None, vmem_limit_bytes=VMEM_LIMIT)


class _Plan:
    def __init__(self, start, wait, n_remote, n_local, in_place=False):
        self.start, self.wait, self.n_remote, self.n_local, self.in_place = start, wait, n_remote, n_local, in_place

    def sems(self):
        return [pltpu.SemaphoreType.DMA((self.n_remote,)), pltpu.SemaphoreType.DMA((self.n_remote,)),
                pltpu.SemaphoreType.DMA((max(self.n_local, 1),))]


def _call(body, *, name, grid, in_specs, out_specs, out_shape, scratch_shapes, operands, sem, hosted=None):
    if hosted is None:
        outs = pl.pallas_call(body, name=name, grid=grid, in_specs=in_specs, out_specs=out_specs, out_shape=out_shape,
                              scratch_shapes=scratch_shapes, compiler_params=_params(*sem))(*operands)
        return outs, None
    plan, srcs, h_shapes = hosted
    n_in, n_out, n_scr, nh_in, nh_out = len(in_specs), len(out_specs), len(scratch_shapes), len(srcs), len(h_shapes)
    aliases = {n_in + a: n_out + a for a in range(nh_in)} if plan.in_place else {}

    def full_body(*refs):
        ins, refs = refs[:n_in], refs[n_in:]
        h_in, refs = refs[:nh_in], refs[nh_in:]
        outs, refs = refs[:n_out], refs[n_out:]
        h_out, refs = refs[:nh_out], refs[nh_out:]
        scr, sems = refs[:n_scr], refs[n_scr:]
        ids = [pl.program_id(ax) for ax in range(len(grid))]
        first = functools.reduce(jnp.logical_and, [i == 0 for i in ids])
        last = functools.reduce(jnp.logical_and, [i == g - 1 for i, g in zip(ids, grid)])

        @pl.when(first)
        def _():
            plan.start(h_in, h_out, *sems)

        body(*ins, *outs, *scr)

        @pl.when(last)
        def _():
            plan.wait(h_in, h_out, *sems)

    res = pl.pallas_call(
        full_body, name=name, grid=grid, in_specs=list(in_specs) + [ANY] * nh_in, out_specs=list(out_specs) + [ANY] * nh_out,
        out_shape=list(out_shape) + list(h_shapes), scratch_shapes=list(scratch_shapes) + plan.sems(),
        input_output_aliases=aliases, compiler_params=_params(*(["arbitrary"] * len(grid))),
    )(*operands, *srcs)
    return res[:n_out], res[n_out:]


def _dot_nn(a, b):
    return lax.dot_general(a, b, (((1,), (0,)), ((), ())), preferred_element_type=F32)


def _dot_nt(a, b):
    return lax.dot_general(a, b, (((1,), (1,)), ((), ())), preferred_element_type=F32)


def _dot_tn(a, b):
    return lax.dot_general(a, b, (((0,), (0,)), ((), ())), preferred_element_type=F32)


def _sigmoid(x):
    return 0.5 * jnp.tanh(0.5 * x) + 0.5


def _rms_stats(x):
    r = lax.rsqrt(jnp.mean(x * x, axis=-1, keepdims=True) + NORM_EPS)
    return x * r, r


ROWS_WIDE = 16
MM_ROWS = 256


def _rms_bwd(dy, xhat, r, g):
    dg = jnp.sum(dy * xhat, axis=0, keepdims=True)
    dxh = dy * g
    dx = r * (dxh - xhat * jnp.mean(dxh * xhat, axis=-1, keepdims=True))
    return dx, dg


def _mm(a, b, *, mode, out_dtype, tm, tn, tk, name, add=None, scale=1.0, hosted=None):
    if mode == "nn":
        (m, k), (_, n) = a.shape, b.shape
    elif mode == "nt":
        (m, k), (n, _) = a.shape, b.shape
    else:
        (k, m), (_, n) = a.shape, b.shape
    assert m % tm == 0 and n % tn == 0 and k % tk == 0, (name, m, n, k, tm, tn, tk)
    nk = k // tk
    dot = {"nn": _dot_nn, "nt": _dot_nt, "tn": _dot_tn}[mode]
    a_spec = pl.BlockSpec((tk, tm), lambda i, j, kk: (kk, i)) if mode == "tn" else pl.BlockSpec((tm, tk), lambda i, j, kk: (i, kk))
    b_spec = pl.BlockSpec((tn, tk), lambda i, j, kk: (j, kk)) if mode == "nt" else pl.BlockSpec((tk, tn), lambda i, j, kk: (kk, j))
    o_spec = pl.BlockSpec((tm, tn), lambda i, j, kk: (i, j))
    has_add = add is not None

    def finish(prod, c_ref, o_ref):
        if scale != 1.0:
            prod = prod * scale
        o_ref[...] = ((c_ref[...] + prod) if has_add else prod).astype(out_dtype)

    def body(*refs):
        a_ref, b_ref = refs[:2]
        c_ref = refs[2] if has_add else None
        o_ref = refs[3] if has_add else refs[2]
        if nk == 1:
            finish(dot(a_ref[...], b_ref[...]), c_ref, o_ref)
            return
        acc_ref = refs[-1]
        kk = pl.program_id(2)

        @pl.when(kk == 0)
        def _():
            acc_ref[...] = jnp.zeros_like(acc_ref)

        acc_ref[...] += dot(a_ref[...], b_ref[...])

        @pl.when(kk == nk - 1)
        def _():
            finish(acc_ref[...], c_ref, o_ref)

    operands = (a, b, add) if has_add else (a, b)
    in_specs = [a_spec, b_spec] + ([o_spec] if has_add else [])
    (out,), got = _call(
        body, name=name, grid=(m // tm, n // tn, nk), in_specs=in_specs, out_specs=[o_spec],
        out_shape=[jax.ShapeDtypeStruct((m, n), out_dtype)], scratch_shapes=[pltpu.VMEM((tm, tn), F32)] if nk > 1 else [],
        operands=operands, sem=("parallel", "parallel", "arbitrary"), hosted=hosted)
    return out if hosted is None else (out, got)


def _rms_fwd(x, g, *, tm, name, hosted=None):
    t, d = x.shape

    def body(x_ref, g_ref, h_ref):
        xhat, _ = _rms_stats(x_ref[...])
        h_ref[...] = (xhat * g_ref[...]).astype(BF16)

    (h,), got = _call(
        body, name=name, grid=(t // tm,),
        in_specs=[pl.BlockSpec((tm, d), lambda i: (i, 0)), pl.BlockSpec((1, d), lambda i: (0, 0))],
        out_specs=[pl.BlockSpec((tm, d), lambda i: (i, 0))], out_shape=[jax.ShapeDtypeStruct((t, d), BF16)], scratch_shapes=[],
        operands=(x, g), sem=("parallel",), hosted=hosted)
    return h, got


def _ffn_fwd(x, g, wgT, wuT, wd, *, tm, hc, name, hosted=None, target=None):
    t, d = x.shape
    nj = DFF // hc
    with_loss = target is not None

    def body(*refs):
        x_ref, g_ref, wg_ref, wu_ref, wd_ref = refs[:5]
        t_ref = refs[5] if with_loss else None
        xo_ref, h_ref, a_ref, b_ref = refs[5 + with_loss:9 + with_loss]
        loss_ref = refs[9 + with_loss] if with_loss else None
        acc_ref = refs[-1]
        i, j = pl.program_id(0), pl.program_id(1)

        @pl.when(j == 0)
        def _():
            xhat, _ = _rms_stats(x_ref[...])
            h_ref[...] = (xhat * g_ref[...]).astype(BF16)
            acc_ref[...] = jnp.zeros_like(acc_ref)

        h = h_ref[...]
        a = _dot_nt(h, wg_ref[...])
        b = _dot_nt(h, wu_ref[...])
        a_ref[...] = a.astype(BF16)
        b_ref[...] = b.astype(BF16)
        s = (a * _sigmoid(a) * b).astype(BF16)
        acc_ref[...] += _dot_nn(s, wd_ref[...])

        if with_loss:
            @pl.when((i == 0) & (j == 0))
            def _():
                loss_ref[...] = jnp.zeros_like(loss_ref)

        @pl.when(j == nj - 1)
        def _():
            y = x_ref[...] + 0.5 * acc_ref[...]
            if with_loss:
                err = y - t_ref[...]
                xo_ref[...] = err * (1.0 / d)
                loss_ref[...] += jnp.sum(jnp.sum(err * err, axis=-1, keepdims=True), axis=0, keepdims=True) * (0.5 / d)
            else:
                xo_ref[...] = y

    row = pl.BlockSpec((tm, d), lambda i, j: (i, 0))
    vec = pl.BlockSpec((1, d), lambda i, j: (0, 0))
    wsp = pl.BlockSpec((hc, d), lambda i, j: (j, 0))
    hid = pl.BlockSpec((tm, hc), lambda i, j: (i, j))
    out_specs = [row, row, hid, hid] + ([pl.BlockSpec((1, 128), lambda i, j: (0, 0))] if with_loss else [])
    out_shape = [jax.ShapeDtypeStruct((t, d), F32), jax.ShapeDtypeStruct((t, d), BF16), jax.ShapeDtypeStruct((t, DFF), BF16),
                 jax.ShapeDtypeStruct((t, DFF), BF16)] + ([jax.ShapeDtypeStruct((1, 128), F32)] if with_loss else [])
    return _call(
        body, name=name, grid=(t // tm, nj), in_specs=[row, vec, wsp, wsp, wsp] + ([row] if with_loss else []),
        out_specs=out_specs, out_shape=out_shape, scratch_shapes=[pltpu.VMEM((tm, d), F32)],
        operands=(x, g, wgT, wuT, wd) + ((target,) if with_loss else ()),
        sem=("arbitrary" if with_loss else "parallel", "arbitrary"), hosted=hosted)


def _ffn_grads(dout, h, a, b, wd, *, tm, hc, name, hosted=None):
    t, d = dout.shape
    ni, nj = t // tm, DFF // hc

    def body(dout_ref, h_ref, a_ref, b_ref, wd_ref, da_ref, db_ref, dwg_ref, dwu_ref, dwd_ref,
             dy_all, h_all, ds_scr, s_scr, acc_g, acc_u, acc_d):
        j, i = pl.program_id(0), pl.program_id(1)
        rows_i = pl.ds(pl.multiple_of(i * tm, tm), tm)

        @pl.when(j == 0)
        def _():
            dy_all[rows_i, :] = (0.5 * dout_ref[...]).astype(BF16)
            h_all[rows_i, :] = h_ref[...]

        @pl.when(i == 0)
        def _():
            acc_g[...] = jnp.zeros_like(acc_g)
            acc_u[...] = jnp.zeros_like(acc_u)
            acc_d[...] = jnp.zeros_like(acc_d)

        def grad_rows(rows):
            ds = ds_scr[rows, :]
            av = a_ref[rows, :].astype(F32)
            bv = b_ref[rows, :].astype(F32)
            sg = _sigmoid(av)
            sl = av * sg
            s_scr[rows, :] = (sl * bv).astype(BF16)
            da_ref[rows, :] = (ds * bv * (sg + sl * (1.0 - sg))).astype(BF16)
            db_ref[rows, :] = (ds * sl).astype(BF16)

        for blk in range(tm // MM_ROWS):
            rs = slice(blk * MM_ROWS, (blk + 1) * MM_ROWS)
            ds_scr[rs, :] = _dot_nt(dy_all[pl.ds(pl.multiple_of(i * tm + blk * MM_ROWS, MM_ROWS), MM_ROWS), :], wd_ref[...])
            for c in range(MM_ROWS // ROWS_WIDE):
                grad_rows(slice(blk * MM_ROWS + c * ROWS_WIDE, blk * MM_ROWS + (c + 1) * ROWS_WIDE))

        dy_i = dy_all[rows_i, :]
        h_i = h_all[rows_i, :]
        acc_d[...] += _dot_tn(s_scr[...], dy_i)
        acc_g[...] += _dot_tn(da_ref[...], h_i)
        acc_u[...] += _dot_tn(db_ref[...], h_i)

        @pl.when(i == ni - 1)
        def _():
            dwg_ref[...] = acc_g[...].astype(BF16)
            dwu_ref[...] = acc_u[...].astype(BF16)
            dwd_ref[...] = acc_d[...].astype(BF16)

    first = pl.BlockSpec((tm, d), lambda j, i: (jnp.where(j == 0, i, 0), 0))
    hid = pl.BlockSpec((tm, hc), lambda j, i: (i, j))
    wsp = pl.BlockSpec((hc, d), lambda j, i: (j, 0))
    hid_shape = jax.ShapeDtypeStruct((t, DFF), BF16)
    w_shape = jax.ShapeDtypeStruct((DFF, d), BF16)
    return _call(
        body, name=name, grid=(nj, ni), in_specs=[first, first, hid, hid, wsp], out_specs=[hid, hid, wsp, wsp, wsp],
        out_shape=[hid_shape, hid_shape, w_shape, w_shape, w_shape],
        scratch_shapes=[pltpu.VMEM((t, d), BF16), pltpu.VMEM((t, d), BF16), pltpu.VMEM((tm, hc), F32), pltpu.VMEM((tm, hc), BF16),
                        pltpu.VMEM((hc, d), F32), pltpu.VMEM((hc, d), F32), pltpu.VMEM((hc, d), F32)],
        operands=(dout, h, a, b, wd), sem=("arbitrary", "arbitrary"), hosted=hosted)


def _proj_fwd(h, latT, convT, gateT, *, tm, name, hosted=None):
    t, d = h.shape

    def body(h_ref, wl_ref, wc_ref, wg_ref, lat_ref, conv_ref, gl_ref):
        hv = h_ref[...]
        lat_ref[...] = _dot_nt(hv, wl_ref[...]).astype(BF16)
        conv_ref[...] = _dot_nt(hv, wc_ref[...]).astype(BF16)
        gl_ref[...] = _dot_nt(hv, wg_ref[...]).astype(BF16)

    def rows(w):
        return pl.BlockSpec((tm, w), lambda i: (i, 0))

    def full(r):
        return pl.BlockSpec((r, d), lambda i: (0, 0))

    return _call(
        body, name=name, grid=(t // tm,), in_specs=[rows(d), full(LAT_PAD), full(CONV_COLS), full(GATE_COLS)],
        out_specs=[rows(LAT_PAD), rows(CONV_COLS), rows(GATE_COLS)],
        out_shape=[jax.ShapeDtypeStruct((t, LAT_PAD), BF16), jax.ShapeDtypeStruct((t, CONV_COLS), BF16),
                   jax.ShapeDtypeStruct((t, GATE_COLS), BF16)],
        scratch_shapes=[], operands=(h, latT, convT, gateT), sem=("parallel",), hosted=hosted)


def _proj_bwd(dlat, dconv3, dgl, latT, convT, gateT, x, g, dres, *, tm, name, hosted=None):
    t, d = x.shape

    def body(dl_ref, dc_ref, dg_ref, wl_ref, wc_ref, wg_ref, x_ref, g_ref, dres_ref, dx_ref, dgain_ref):
        @pl.when(pl.program_id(0) == 0)
        def _():
            dgain_ref[...] = jnp.zeros_like(dgain_ref)

        dh = _dot_nn(dl_ref[...], wl_ref[...]) + _dot_nn(dc_ref[...], wc_ref[...]) + _dot_nn(dg_ref[...], wg_ref[...])
        xhat, r = _rms_stats(x_ref[...])
        dx, dgain = _rms_bwd(dh, xhat, r, g_ref[...])
        dx_ref[...] = dres_ref[...] + dx
        dgain_ref[...] += dgain

    def rows(w):
        return pl.BlockSpec((tm, w), lambda i: (i, 0))

    def full(r):
        return pl.BlockSpec((r, d), lambda i: (0, 0))

    return _call(
        body, name=name, grid=(t // tm,),
        in_specs=[rows(LAT_PAD), rows(CONV_COLS), rows(GATE_COLS), full(LAT_PAD), full(CONV_COLS), full(GATE_COLS), rows(d), full(1), rows(d)],
        out_specs=[rows(d), full(1)], out_shape=[jax.ShapeDtypeStruct((t, d), F32), jax.ShapeDtypeStruct((1, d), F32)],
        scratch_shapes=[], operands=(dlat, dconv3, dgl, latT, convT, gateT, x, g, dres), sem=("arbitrary",), hosted=hosted)


def _ffn_up_bwd(da, db, wgT, wuT, x, g, dout, *, tm, name, hosted=None):
    t, d = x.shape

    def body(da_ref, db_ref, wg_ref, wu_ref, x_ref, g_ref, dout_ref, dx_ref, dg_ref):
        @pl.when(pl.program_id(0) == 0)
        def _():
            dg_ref[...] = jnp.zeros_like(dg_ref)

        dh = _dot_nn(da_ref[...], wg_ref[...]) + _dot_nn(db_ref[...], wu_ref[...])
        xhat, r = _rms_stats(x_ref[...])
        dx, dg = _rms_bwd(dh, xhat, r, g_ref[...])
        dx_ref[...] = dout_ref[...] + dx
        dg_ref[...] += dg

    row = pl.BlockSpec((tm, d), lambda i: (i, 0))
    vec = pl.BlockSpec((1, d), lambda i: (0, 0))
    hid = pl.BlockSpec((tm, DFF), lambda i: (i, 0))
    wsp = pl.BlockSpec((DFF, d), lambda i: (0, 0))
    return _call(
        body, name=name, grid=(t // tm,), in_specs=[hid, hid, wsp, wsp, row, vec, row], out_specs=[row, vec],
        out_shape=[jax.ShapeDtypeStruct((t, d), F32), jax.ShapeDtypeStruct((1, d), F32)], scratch_shapes=[],
        operands=(da, db, wgT, wuT, x, g, dout), sem=("arbitrary",), hosted=hosted)


HEAD_LANES = (slice(0, 32), slice(64, 80), None, slice(32, 64), slice(80, 96), None)


def _head_cols(a):
    def part(sl, width):
        if sl is None or sl.stop > a.shape[1]:
            return jnp.zeros((a.shape[0], width), a.dtype)
        return a[:, sl]

    return jnp.concatenate([part(sl, w) for sl, w in zip(HEAD_LANES, (32, 16, 16, 32, 16, 16))], axis=1)


def _head_cols_inv(a, dims):
    parts = [a[:, 0:32], a[:, 64:96]] + ([a[:, 32:48], a[:, 96:112]] if dims == QK_DIM else [])
    return jnp.concatenate(parts, axis=1)


def _rope_fwd(x, c, s):
    return x * c + pltpu.roll(x, HEAD_PAD // 2, 1) * s


def _rope_bwd(dy, c, s):
    return dy * c + pltpu.roll(dy * s, HEAD_PAD // 2, 1)


def _head_stats(x):
    r = lax.rsqrt(jnp.sum(x * x, axis=-1, keepdims=True) * (1.0 / QK_DIM) + NORM_EPS)
    return x * r, r


def _mla_prep_fwd(lat, gq, gkv, ghq, ghk, wq, wk, wv, rc, rs, *, tm, name):
    t = lat.shape[0]

    def body(lat_ref, gq_ref, gkv_ref, ghq_ref, ghk_ref, wq_ref, wk_ref, wv_ref, c_ref, s_ref,
             q_ref, k_ref, v_ref, qn_ref, ckv_ref):
        lat_v = lat_ref[...]
        qhat, _ = _rms_stats(lat_v[:, :Q_LORA].astype(F32))
        qn = (qhat * gq_ref[...]).astype(BF16)
        khat, _ = _rms_stats(lat_v[:, Q_LORA:Q_LORA + KV_LORA].astype(F32))
        ckv = (khat * gkv_ref[...]).astype(BF16)
        ckv_ext = jnp.concatenate([ckv, lat_v[:, Q_LORA + KV_LORA:]], axis=1)
        qn_ref[...] = qn
        ckv_ref[...] = ckv_ext
        q_pre = _dot_nn(qn, wq_ref[...])
        k_pre = _dot_nn(ckv_ext, wk_ref[...])
        v_ref[...] = _dot_nn(ckv, wv_ref[...]).astype(BF16)
        c, s = c_ref[...], s_ref[...]
        for h in range(N_HEADS):
            hs = slice(h * HEAD_PAD, (h + 1) * HEAD_PAD)
            xq, _ = _head_stats(q_pre[:, hs])
            q_ref[:, hs] = _rope_fwd(xq * ghq_ref[...], c, s).astype(BF16)
            xk, _ = _head_stats(k_pre[:, hs])
            k_ref[:, hs] = _rope_fwd(xk * ghk_ref[...], c, s).astype(BF16)

    def row(w):
        return pl.BlockSpec((tm, w), lambda i: (i, 0))

    def full(r, w):
        return pl.BlockSpec((r, w), lambda i: (0, 0))

    wide = jax.ShapeDtypeStruct((t, D), BF16)
    lat3 = jax.ShapeDtypeStruct((t, Q_LORA), BF16)
    return pl.pallas_call(
        body, name=name, grid=(t // tm,),
        in_specs=[row(LAT_PAD), full(1, Q_LORA), full(1, KV_LORA), full(1, HEAD_PAD), full(1, HEAD_PAD),
                  full(Q_LORA, D), full(Q_LORA, D), full(KV_LORA, D), row(HEAD_PAD), row(HEAD_PAD)],
        out_specs=[row(D), row(D), row(D), row(Q_LORA), row(Q_LORA)],
        out_shape=[wide, wide, wide, lat3, lat3],
        compiler_params=_params("parallel"),
    )(lat, gq, gkv, ghq, ghk, wq, wk, wv, rc, rs)


def _mla_prep_bwd(dq, dk, dv, lat, qn, ckv_ext, gq, gkv, ghq, ghk, wq, wk, wv, rc, rs, *, tm, name):
    t = lat.shape[0]

    def body(dq_ref, dk_ref, dv_ref, lat_ref, qn_ref, ckv_ref, gq_ref, gkv_ref, ghq_ref, ghk_ref, wq_ref, wk_ref, wv_ref,
             c_ref, s_ref, dlat_ref, dqp_ref, dkp_ref, dgq_ref, dgkv_ref, dghq_ref, dghk_ref):
        @pl.when(pl.program_id(0) == 0)
        def _():
            dgq_ref[...] = jnp.zeros_like(dgq_ref)
            dgkv_ref[...] = jnp.zeros_like(dgkv_ref)
            dghq_ref[...] = jnp.zeros_like(dghq_ref)
            dghk_ref[...] = jnp.zeros_like(dghk_ref)

        c, s = c_ref[...], s_ref[...]
        q_pre = _dot_nn(qn_ref[...], wq_ref[...])
        k_pre = _dot_nn(ckv_ref[...], wk_ref[...])

        def heads(pre, dy_ref, gh_ref, dgh_ref, out_ref):
            dgh = jnp.zeros((1, HEAD_PAD), F32)
            for h in range(N_HEADS):
                hs = slice(h * HEAD_PAD, (h + 1) * HEAD_PAD)
                d = _rope_bwd(dy_ref[:, hs].astype(F32), c, s)
                xhat, r = _head_stats(pre[:, hs])
                dgh = dgh + jnp.sum(d * xhat, axis=0, keepdims=True)
                dxh = d * gh_ref[...]
                dx = r * (dxh - xhat * (jnp.sum(dxh * xhat, axis=-1, keepdims=True) * (1.0 / QK_DIM)))
                out_ref[:, hs] = dx.astype(BF16)
            dgh_ref[...] += dgh

        heads(q_pre, dq_ref, ghq_ref, dghq_ref, dqp_ref)
        heads(k_pre, dk_ref, ghk_ref, dghk_ref, dkp_ref)
        dqn = _dot_nt(dqp_ref[...], wq_ref[...])
        dce = _dot_nt(dkp_ref[...], wk_ref[...])
        dckv = dce[:, :KV_LORA] + _dot_nt(dv_ref[...], wv_ref[...])
        lat_v = lat_ref[...]
        qhat, rq = _rms_stats(lat_v[:, :Q_LORA].astype(F32))
        dql, dgq = _rms_bwd(dqn, qhat, rq, gq_ref[...])
        khat, rk = _rms_stats(lat_v[:, Q_LORA:Q_LORA + KV_LORA].astype(F32))
        dkl, dgkv = _rms_bwd(dckv, khat, rk, gkv_ref[...])
        dgq_ref[...] += dgq
        dgkv_ref[...] += dgkv
        dlat_ref[...] = jnp.concatenate([dql, dkl, dce[:, KV_LORA:]], axis=1).astype(BF16)

    def row(w):
        return pl.BlockSpec((tm, w), lambda i: (i, 0))

    def full(r, w):
        return pl.BlockSpec((r, w), lambda i: (0, 0))

    return pl.pallas_call(
        body, name=name, grid=(t // tm,),
        in_specs=[row(D), row(D), row(D), row(LAT_PAD), row(Q_LORA), row(Q_LORA), full(1, Q_LORA), full(1, KV_LORA),
                  full(1, HEAD_PAD), full(1, HEAD_PAD), full(Q_LORA, D), full(Q_LORA, D), full(KV_LORA, D),
                  row(HEAD_PAD), row(HEAD_PAD)],
        out_specs=[row(LAT_PAD), row(D), row(D), full(1, Q_LORA), full(1, KV_LORA), full(1, HEAD_PAD), full(1, HEAD_PAD)],
        out_shape=[jax.ShapeDtypeStruct((t, LAT_PAD), BF16), jax.ShapeDtypeStruct((t, D), BF16), jax.ShapeDtypeStruct((t, D), BF16),
                   jax.ShapeDtypeStruct((1, Q_LORA), F32), jax.ShapeDtypeStruct((1, KV_LORA), F32),
                   jax.ShapeDtypeStruct((1, HEAD_PAD), F32), jax.ShapeDtypeStruct((1, HEAD_PAD), F32)],
        compiler_params=_params("arbitrary"),
    )(dq, dk, dv, lat, qn, ckv_ext, gq, gkv, ghq, ghk, wq, wk, wv, rc, rs)


def _causal_keep(tq):
    r = lax.broadcasted_iota(jnp.int32, (tq, tq), 0)
    c = lax.broadcasted_iota(jnp.int32, (tq, tq), 1)
    return c <= r


def _flash_fwd(q, k, v, *, n_seq, seq, tq, name, hosted=None):
    nq = seq // tq

    def body(q_ref, k_ref, v_ref, o_ref, lse_ref):
        qi = pl.program_id(2)
        qv = q_ref[...]

        def step(j, carry, masked):
            m, l, acc = carry
            kj = k_ref[pl.ds(pl.multiple_of(j * tq, tq), tq), :]
            vj = v_ref[pl.ds(pl.multiple_of(j * tq, tq), tq), :]
            s = _dot_nt(qv, kj) * ATTN_SCALE
            if masked:
                s = jnp.where(_causal_keep(tq), s, NEG)
            m_new = jnp.maximum(m, jnp.max(s, axis=-1, keepdims=True))
            alpha = jnp.exp(m - m_new)
            p = jnp.exp(s - m_new)
            l = alpha * l + jnp.sum(p, axis=-1, keepdims=True)
            acc = alpha * acc + _dot_nn(p.astype(BF16), vj)
            return m_new, l, acc

        init = (jnp.full((tq, 1), NEG, F32), jnp.zeros((tq, 1), F32), jnp.zeros((tq, HEAD_PAD), F32))
        carry = lax.fori_loop(0, qi, lambda j, cr: step(j, cr, False), init)
        m, l, acc = step(qi, carry, True)
        o_ref[...] = (acc / l).astype(BF16)
        lse_ref[...] = jnp.broadcast_to(m + jnp.log(l), (tq, HEAD_PAD))

    qspec = pl.BlockSpec((tq, HEAD_PAD), lambda b, h, i: (b * nq + i, h))
    kspec = pl.BlockSpec((seq, HEAD_PAD), lambda b, h, i: (b, h))
    t = n_seq * seq
    return _call(
        body, name=name, grid=(n_seq, N_HEADS, nq), in_specs=[qspec, kspec, kspec], out_specs=[qspec, qspec],
        out_shape=[jax.ShapeDtypeStruct((t, D), BF16), jax.ShapeDtypeStruct((t, D), F32)], scratch_shapes=[],
        operands=(q, k, v), sem=("parallel", "parallel", "arbitrary"), hosted=hosted)


def _flash_bwd(q, k, v, o, lse, do, *, n_seq, seq, tq, name, hosted=None):
    nq = seq // tq

    def body(q_ref, k_ref, v_ref, o_ref, lse_ref, do_ref, dq_ref, dk_ref, dv_ref, dk_acc, dv_acc):
        j = pl.program_id(2)

        @pl.when(j == 0)
        def _():
            dq_ref[...] = jnp.zeros_like(dq_ref)

        dk_acc[...] = jnp.zeros_like(dk_acc)
        dv_acc[...] = jnp.zeros_like(dv_acc)
        kv = k_ref[...]
        vv = v_ref[...]

        def step(i, masked):
            rows = pl.ds(pl.multiple_of(i * tq, tq), tq)
            qi = q_ref[rows, :]
            doi = do_ref[rows, :]
            delta = jnp.sum(doi.astype(F32) * o_ref[rows, :].astype(F32), axis=-1, keepdims=True)
            s = _dot_nt(qi, kv) * ATTN_SCALE
            p = jnp.exp(s - lse_ref[rows, :][:, :1])
            if masked:
                p = jnp.where(_causal_keep(tq), p, 0.0)
            dv_acc[...] += _dot_tn(p.astype(BF16), doi)
            dp = _dot_nt(doi, vv)
            ds = (p * (dp - delta) * ATTN_SCALE).astype(BF16)
            dk_acc[...] += _dot_tn(ds, qi)
            dq_ref[rows, :] += _dot_nn(ds, kv)

        step(j, True)

        def loop_body(i, carry):
            step(i, False)
            return carry

        lax.fori_loop(j + 1, nq, loop_body, 0)
        dk_ref[...] = dk_acc[...]
        dv_ref[...] = dv_acc[...].astype(BF16)

    full = pl.BlockSpec((seq, HEAD_PAD), lambda b, h, j: (b, h))
    tile = pl.BlockSpec((tq, HEAD_PAD), lambda b, h, j: (b * nq + j, h))
    t = n_seq * seq
    return _call(
        body, name=name, grid=(n_seq, N_HEADS, nq), in_specs=[full, tile, tile, full, full, full],
        out_specs=[full, tile, tile],
        out_shape=[jax.ShapeDtypeStruct((t, D), F32), jax.ShapeDtypeStruct((t, D), F32), jax.ShapeDtypeStruct((t, D), BF16)],
        scratch_shapes=[pltpu.VMEM((tq, HEAD_PAD), F32), pltpu.VMEM((tq, HEAD_PAD), F32)],
        operands=(q, k, v, o, lse, do), sem=("parallel", "parallel", "arbitrary"), hosted=hosted)


CONV_CB = 256


def _shift_down(u, k, row):
    return jnp.where(row >= k, pltpu.roll(u, k, 0), 0.0)


def _shift_up(u, k, row, n):
    return jnp.where(row < n - k, pltpu.roll(u, n - k, 0), 0.0)


def _conv_fwd(conv3, cw, *, n_seq, seq, name, hosted=None):
    def body(c_ref, w_ref, p_ref):
        blk = c_ref[...].astype(F32)
        xc, gb, gc = blk[:, :CONV_CB], blk[:, CONV_CB:2 * CONV_CB], blk[:, 2 * CONV_CB:]
        row = lax.broadcasted_iota(jnp.int32, (seq, CONV_CB), 0)
        u = gc * xc
        z = w_ref[0:1, :] * _shift_down(u, 2, row) + w_ref[1:2, :] * _shift_down(u, 1, row) + w_ref[2:3, :] * u
        p_ref[...] = (gb * z).astype(BF16)

    (p,), got = _call(
        body, name=name, grid=(n_seq, D // CONV_CB),
        in_specs=[pl.BlockSpec((seq, 3 * CONV_CB), lambda b, j: (b, j)), pl.BlockSpec((3, CONV_CB), lambda b, j: (0, j))],
        out_specs=[pl.BlockSpec((seq, CONV_CB), lambda b, j: (b, j))],
        out_shape=[jax.ShapeDtypeStruct((n_seq * seq, D), BF16)], scratch_shapes=[],
        operands=(conv3, cw), sem=("parallel", "parallel"), hosted=hosted)
    return p, got


def _conv_bwd(dp, conv3, cw, *, n_seq, seq, name):
    def body(dp_ref, c_ref, w_ref, dc_ref, dw_ref):
        @pl.when(pl.program_id(1) == 0)
        def _():
            dw_ref[...] = jnp.zeros_like(dw_ref)

        blk = c_ref[...].astype(F32)
        xc, gb, gc = blk[:, :CONV_CB], blk[:, CONV_CB:2 * CONV_CB], blk[:, 2 * CONV_CB:]
        row = lax.broadcasted_iota(jnp.int32, (seq, CONV_CB), 0)
        w0, w1, w2 = w_ref[0:1, :], w_ref[1:2, :], w_ref[2:3, :]
        u = gc * xc
        u1 = _shift_down(u, 1, row)
        u2 = _shift_down(u, 2, row)
        z = w0 * u2 + w1 * u1 + w2 * u
        dpv = dp_ref[...].astype(F32)
        dz = dpv * gb
        du = w2 * dz + w1 * _shift_up(dz, 1, row, seq) + w0 * _shift_up(dz, 2, row, seq)
        dc_ref[...] = jnp.concatenate([du * gc, dpv * z, du * xc], axis=1).astype(BF16)
        dw_ref[0:1, :] += jnp.sum(dz * u2, axis=0, keepdims=True)
        dw_ref[1:2, :] += jnp.sum(dz * u1, axis=0, keepdims=True)
        dw_ref[2:3, :] += jnp.sum(dz * u, axis=0, keepdims=True)

    return pl.pallas_call(
        body, name=name, grid=(D // CONV_CB, n_seq),
        in_specs=[pl.BlockSpec((seq, CONV_CB), lambda j, b: (b, j)), pl.BlockSpec((seq, 3 * CONV_CB), lambda j, b: (b, j)),
                  pl.BlockSpec((3, CONV_CB), lambda j, b: (0, j))],
        out_specs=[pl.BlockSpec((seq, 3 * CONV_CB), lambda j, b: (b, j)), pl.BlockSpec((3, CONV_CB), lambda j, b: (0, j))],
        out_shape=[jax.ShapeDtypeStruct((n_seq * seq, CONV_COLS), BF16), jax.ShapeDtypeStruct((3, D), F32)],
        compiler_params=_params("parallel", "arbitrary"),
    )(dp, conv3, cw)


def _merge_fwd(o, p, gl, bias, x1, wpa, wpc, wout, *, tm, name, hosted=None):
    t = x1.shape[0]

    def body(o_ref, p_ref, gl_ref, b_ref, x_ref, wpa_ref, wpc_ref, wout_ref, x2_ref, mg_ref, ya_ref, yb_ref):
        ya = _dot_nn(o_ref[...], wpa_ref[...])
        yb = _dot_nn(p_ref[...], wpc_ref[...])
        gates = _sigmoid(gl_ref[...].astype(F32) + b_ref[...])
        merged = (gates[:, :D] * ya + gates[:, D:] * yb).astype(BF16)
        ya_ref[...] = ya.astype(BF16)
        yb_ref[...] = yb.astype(BF16)
        mg_ref[...] = merged
        x2_ref[...] = x_ref[...] + _dot_nn(merged, wout_ref[...])

    row = pl.BlockSpec((tm, D), lambda i: (i, 0))
    row2 = pl.BlockSpec((tm, GATE_COLS), lambda i: (i, 0))
    wsp = pl.BlockSpec((D, D), lambda i: (0, 0))
    wide = jax.ShapeDtypeStruct((t, D), BF16)
    return _call(
        body, name=name, grid=(t // tm,),
        in_specs=[row, row, row2, pl.BlockSpec((1, GATE_COLS), lambda i: (0, 0)), row, wsp, wsp, wsp],
        out_specs=[row, row, row, row], out_shape=[jax.ShapeDtypeStruct((t, D), F32), wide, wide, wide], scratch_shapes=[],
        operands=(o, p, gl, bias, x1, wpa, wpc, wout), sem=("parallel",), hosted=hosted)


def _merge_bwd(dx2, ya, yb, gl, bias, wpa, wpc, wout, *, tm, name, hosted=None):
    t = dx2.shape[0]

    def body(dx_ref, ya_ref, yb_ref, gl_ref, b_ref, wpa_ref, wpc_ref, wout_ref,
             dxb_ref, dya_ref, dyb_ref, dgl_ref, do_ref, dp_ref, db_ref):
        @pl.when(pl.program_id(0) == 0)
        def _():
            db_ref[...] = jnp.zeros_like(db_ref)

        dxb = dx_ref[...].astype(BF16)
        dxb_ref[...] = dxb
        dm = _dot_nt(dxb, wout_ref[...])
        gates = _sigmoid(gl_ref[...].astype(F32) + b_ref[...])
        ga, gb = gates[:, :D], gates[:, D:]
        dya = (dm * ga).astype(BF16)
        dyb = (dm * gb).astype(BF16)
        dya_ref[...] = dya
        dyb_ref[...] = dyb
        dgl = jnp.concatenate([dm * ya_ref[...].astype(F32) * ga * (1.0 - ga),
                               dm * yb_ref[...].astype(F32) * gb * (1.0 - gb)], axis=1)
        dgl_ref[...] = dgl.astype(BF16)
        db_ref[...] += jnp.sum(dgl, axis=0, keepdims=True)
        do_ref[...] = _dot_nt(dya, wpa_ref[...]).astype(BF16)
        dp_ref[...] = _dot_nt(dyb, wpc_ref[...]).astype(BF16)

    row = pl.BlockSpec((tm, D), lambda i: (i, 0))
    row2 = pl.BlockSpec((tm, GATE_COLS), lambda i: (i, 0))
    vec2 = pl.BlockSpec((1, GATE_COLS), lambda i: (0, 0))
    wsp = pl.BlockSpec((D, D), lambda i: (0, 0))
    wide = jax.ShapeDtypeStruct((t, D), BF16)
    return _call(
        body, name=name, grid=(t // tm,), in_specs=[row, row, row, row2, vec2, wsp, wsp, wsp],
        out_specs=[row, row, row, row2, row, row, vec2],
        out_shape=[wide, wide, wide, jax.ShapeDtypeStruct((t, GATE_COLS), BF16), wide, wide,
                   jax.ShapeDtypeStruct((1, GATE_COLS), F32)],
        scratch_shapes=[], operands=(dx2, ya, yb, gl, bias, wpa, wpc, wout), sem=("arbitrary",), hosted=hosted)


def _adamw(w, g, m, v, *, name):
    rows, cols = w.shape
    tr = max([c for c in range(8, 513, 8) if rows % c == 0], default=rows)
    c1 = 1.0 / (1.0 - ADAM_B1 ** ADAM_STEP)
    c2 = 1.0 / (1.0 - ADAM_B2 ** ADAM_STEP)

    def body(w_ref, g_ref, m_ref, v_ref, d_ref, nm_ref, nv_ref):
        gv = g_ref[...]
        nm = ADAM_B1 * m_ref[...] + (1.0 - ADAM_B1) * gv
        nv = ADAM_B2 * v_ref[...] + (1.0 - ADAM_B2) * (gv * gv)
        nm_ref[...] = nm
        nv_ref[...] = nv
        d_ref[...] = -ADAM_LR * ((nm * c1) / (jnp.sqrt(nv * c2) + ADAM_EPS) + ADAM_WD * w_ref[...])

    spec = pl.BlockSpec((tr, cols), lambda i: (i, 0))
    shp = jax.ShapeDtypeStruct((rows, cols), F32)
    return pl.pallas_call(
        body, name=name, grid=(rows // tr,), in_specs=[spec] * 4, out_specs=[spec] * 3, out_shape=[shp] * 3,
        compiler_params=_params("parallel"),
    )(w, g, m, v)


def _place():
    return lax.axis_index("x"), lax.axis_index("y"), lax.axis_index("c")


def _other_chips(x, y):
    return [(1 - x, y), (x, 1 - y), (1 - x, 1 - y)]


def _remote(src, dst, send, recv, dev):
    return pltpu.make_async_remote_copy(src_ref=src, dst_ref=dst, send_sem=send, recv_sem=recv, device_id=dev, device_id_type=MESH)


def _gather_chips_plan(n):
    def start(srcs, dsts, send, recv, local):
        x, y, cc = _place()
        me = 4 * x + 2 * y + cc
        for a in range(n):
            pltpu.make_async_copy(srcs[a], dsts[a].at[me], local.at[a]).start()
            for k, (px, py) in enumerate(_other_chips(x, y)):
                _remote(srcs[a], dsts[a].at[me], send.at[3 * a + k], recv.at[3 * a + k], (px, py, cc)).start()

    def wait(srcs, dsts, send, recv, local):
        x, y, cc = _place()
        me = 4 * x + 2 * y + cc
        for a in range(n):
            for k, (px, py) in enumerate(_other_chips(x, y)):
                _remote(srcs[a], dsts[a].at[4 * px + 2 * py + cc], send.at[3 * a + k], recv.at[3 * a + k], (px, py, cc)).wait_recv()
        for a in range(n):
            for k, (px, py) in enumerate(_other_chips(x, y)):
                _remote(srcs[a], dsts[a].at[me], send.at[3 * a + k], recv.at[3 * a + k], (px, py, cc)).wait_send()
            pltpu.make_async_copy(srcs[a], dsts[a].at[me], local.at[a]).wait()

    return _Plan(start, wait, 3 * n, n)


def _scatter_chips_plan(n):
    def start(srcs, dsts, send, recv, local):
        x, y, cc = _place()
        for a in range(n):
            for k, (px, py) in enumerate(_other_chips(x, y)):
                _remote(srcs[a].at[2 * px + py], dsts[a].at[k], send.at[3 * a + k], recv.at[3 * a + k], (px, py, cc)).start()

    def wait(srcs, dsts, send, recv, local):
        x, y, cc = _place()
        for a in range(n):
            for k, (px, py) in enumerate(_other_chips(x, y)):
                _remote(srcs[a].at[k], dsts[a].at[k], send.at[3 * a + k], recv.at[3 * a + k], (px, py, cc)).wait_recv()
        for a in range(n):
            for k, (px, py) in enumerate(_other_chips(x, y)):
                _remote(srcs[a].at[k], dsts[a].at[k], send.at[3 * a + k], recv.at[3 * a + k], (px, py, cc)).wait_send()

    return _Plan(start, wait, 3 * n, 0)


def _gather_shapes(blocks):
    return [jax.ShapeDtypeStruct((N_DEV,) + b.shape, b.dtype) for b in blocks]


def _scatter_shapes(parts):
    return [jax.ShapeDtypeStruct((3,) + p.shape[1:], p.dtype) for p in parts]


def _gather_sibling_plan(n):
    def start(srcs, dsts, send, recv, local):
        x, y, cc = _place()
        for a in range(n):
            for q in range(4):
                _remote(srcs[a].at[2 * q + cc], dsts[a].at[2 * q + cc], send.at[4 * a + q], recv.at[4 * a + q], (x, y, 1 - cc)).start()

    def wait(srcs, dsts, send, recv, local):
        x, y, cc = _place()
        for a in range(n):
            for q in range(4):
                _remote(srcs[a].at[2 * q + cc], dsts[a].at[2 * q + 1 - cc], send.at[4 * a + q], recv.at[4 * a + q],
                        (x, y, 1 - cc)).wait_recv()
        for a in range(n):
            for q in range(4):
                _remote(srcs[a].at[2 * q + cc], dsts[a].at[2 * q + cc], send.at[4 * a + q], recv.at[4 * a + q],
                        (x, y, 1 - cc)).wait_send()

    return _Plan(start, wait, 4 * n, 0, in_place=True)


def _scatter_sibling_plan(n):
    def start(srcs, dsts, send, recv, local):
        x, y, cc = _place()
        for a in range(n):
            for q in range(4):
                _remote(srcs[a].at[2 * q + 1 - cc], dsts[a].at[q], send.at[4 * a + q], recv.at[4 * a + q], (x, y, 1 - cc)).start()

    def wait(srcs, dsts, send, recv, local):
        x, y, cc = _place()
        for a in range(n):
            for q in range(4):
                _remote(srcs[a].at[q], dsts[a].at[q], send.at[4 * a + q], recv.at[4 * a + q], (x, y, 1 - cc)).wait_recv()
        for a in range(n):
            for q in range(4):
                _remote(srcs[a].at[q], dsts[a].at[q], send.at[4 * a + q], recv.at[4 * a + q], (x, y, 1 - cc)).wait_send()

    return _Plan(start, wait, 4 * n, 0)


def _same_shapes(arrs):
    return [jax.ShapeDtypeStruct(a.shape, a.dtype) for a in arrs]


def _halved_shapes(parts):
    return [jax.ShapeDtypeStruct((4,) + p.shape[1:], p.dtype) for p in parts]


def _run_plan(plan, srcs, out_shapes, *, name):
    n_in, n_out = len(srcs), len(out_shapes)

    def body(*refs):
        h_in, h_out, sems = refs[:n_in], refs[n_in:n_in + n_out], refs[n_in + n_out:]
        plan.start(h_in, h_out, *sems)
        plan.wait(h_in, h_out, *sems)

    return pl.pallas_call(body, name=name, in_specs=[ANY] * n_in, out_specs=[ANY] * n_out, out_shape=list(out_shapes),
                          input_output_aliases={a: a for a in range(n_in)} if plan.in_place else {},
                          scratch_shapes=plan.sems())(*srcs)


SEM = pl.BlockSpec(memory_space=pltpu.SEMAPHORE)
HBM = pl.BlockSpec(memory_space=pltpu.HBM)
SIDE_EFFECT = pltpu.CompilerParams(has_side_effects=pltpu.SideEffectType.DATAFLOW_SIDE_EFFECTING)


def _plan_start(plan, blocks, land_shapes, *, name):
    n = len(blocks)
    lands = [lax.empty(s.shape, s.dtype) for s in land_shapes]

    def body(*refs):
        srcs, sems, lands_out, token = refs[:n], refs[2 * n:2 * n + 3], refs[3 * n + 3:4 * n + 3], refs[4 * n + 3]
        plan.start(srcs, lands_out, *sems)
        token[...] = jnp.zeros_like(token)

    out_shape = ([s for s in plan.sems()] + [pltpu.HBM(b.shape, b.dtype) for b in blocks]
                 + [pltpu.HBM(l.shape, l.dtype) for l in lands] + [jax.ShapeDtypeStruct((8, 128), F32)])
    res = pl.pallas_call(
        body, name=name, in_specs=[HBM] * (2 * n), out_specs=[SEM] * 3 + [HBM] * (2 * n) + [pl.BlockSpec(memory_space=pltpu.VMEM)],
        out_shape=out_shape, input_output_aliases={a: 3 + a for a in range(2 * n)}, compiler_params=SIDE_EFFECT,
    )(*[pltpu.with_memory_space_constraint(a, pltpu.HBM) for a in list(blocks) + lands])
    return res[:3], res[3:3 + n], res[3 + n:3 + 2 * n], res[3 + 2 * n]


def _plan_wait(plan, sems, blocks, lands, after, *, name):
    n = len(blocks)

    def body(*refs):
        plan.wait(refs[:n], refs[n:2 * n], *refs[2 * n:2 * n + 3])

    res = pl.pallas_call(
        body, name=name, in_specs=[HBM] * (2 * n) + [SEM] * 3 + [ANY], out_specs=[HBM] * (2 * n),
        out_shape=[pltpu.HBM(a.shape, a.dtype) for a in list(blocks) + list(lands)],
        input_output_aliases={a: a for a in range(2 * n)}, compiler_params=SIDE_EFFECT,
    )(*blocks, *lands, *sems, after)
    return list(res[:n]), list(res[n:])


def _sum_sibling(p, q, core, *, name):
    _, r, c = p.shape

    def body(core_ref, p_ref, q_ref, o_ref):
        o_ref[...] = (p_ref[...].astype(F32) + q_ref[...].astype(F32)).astype(BF16)

    grid_spec = pltpu.PrefetchScalarGridSpec(
        num_scalar_prefetch=1, grid=(4,),
        in_specs=[pl.BlockSpec((1, r, c), lambda ch, core_ref: (2 * ch + core_ref[0], 0, 0)),
                  pl.BlockSpec((1, r, c), lambda ch, core_ref: (ch, 0, 0))],
        out_specs=pl.BlockSpec((1, r, c), lambda ch, core_ref: (ch, 0, 0)))
    return pl.pallas_call(
        body, name=name, grid_spec=grid_spec, out_shape=jax.ShapeDtypeStruct((4, r, c), BF16),
        compiler_params=_params("parallel"),
    )(core, p, q)


def _sum_chips(s1, r2, chip, *, name):
    _, r, c = s1.shape

    def body(chip_ref, s_ref, r_ref, o_ref):
        acc = s_ref[0].astype(F32)
        for k in range(3):
            acc = acc + r_ref[k].astype(F32)
        o_ref[...] = acc

    grid_spec = pltpu.PrefetchScalarGridSpec(
        num_scalar_prefetch=1, grid=(1,),
        in_specs=[pl.BlockSpec((1, r, c), lambda i, chip_ref: (chip_ref[0], 0, 0)),
                  pl.BlockSpec((3, r, c), lambda i, chip_ref: (0, 0, 0))],
        out_specs=pl.BlockSpec((r, c), lambda i, chip_ref: (0, 0)))
    return pl.pallas_call(
        body, name=name, grid_spec=grid_spec, out_shape=jax.ShapeDtypeStruct((r, c), F32),
        compiler_params=_params("arbitrary"),
    )(chip, s1, r2)


def _sum_adamw(s1, r2, chip, w, m, v, *, name):
    _, r, c = s1.shape
    c1 = 1.0 / (1.0 - ADAM_B1 ** ADAM_STEP)
    c2 = 1.0 / (1.0 - ADAM_B2 ** ADAM_STEP)

    def body(chip_ref, s_ref, r_ref, w_ref, m_ref, v_ref, g_ref, d_ref, nm_ref, nv_ref):
        gv = s_ref[0].astype(F32)
        for k in range(3):
            gv = gv + r_ref[k].astype(F32)
        g_ref[...] = gv
        nm = ADAM_B1 * m_ref[...] + (1.0 - ADAM_B1) * gv
        nv = ADAM_B2 * v_ref[...] + (1.0 - ADAM_B2) * (gv * gv)
        nm_ref[...] = nm
        nv_ref[...] = nv
        d_ref[...] = -ADAM_LR * ((nm * c1) / (jnp.sqrt(nv * c2) + ADAM_EPS) + ADAM_WD * w_ref[...])

    flat = pl.BlockSpec((r, c), lambda i, chip_ref: (0, 0))
    grid_spec = pltpu.PrefetchScalarGridSpec(
        num_scalar_prefetch=1, grid=(1,),
        in_specs=[pl.BlockSpec((1, r, c), lambda i, chip_ref: (chip_ref[0], 0, 0)),
                  pl.BlockSpec((3, r, c), lambda i, chip_ref: (0, 0, 0)), flat, flat, flat],
        out_specs=[flat] * 4)
    return pl.pallas_call(
        body, name=name, grid_spec=grid_spec, out_shape=[jax.ShapeDtypeStruct((r, c), F32)] * 4,
        compiler_params=_params("arbitrary"),
    )(chip, s1, r2, w, m, v)


def _small_exchange(v, *, reduce, name):
    r, c = v.shape

    def body(x_ref, o_ref, *rest):
        if reduce:
            buf_ref, send_sems, recv_sems = rest
        else:
            buf_ref = o_ref
            send_sems, recv_sems = rest
        x, y, cc = _place()
        me = 4 * x + 2 * y + cc

        def peer(k):
            return ((1 - x) if k & 4 else x, (1 - y) if k & 2 else y, (1 - cc) if k & 1 else cc)

        buf_ref[me] = x_ref[...]
        sends = []
        for k in range(1, N_DEV):
            cp = pltpu.make_async_remote_copy(src_ref=x_ref, dst_ref=buf_ref.at[me], send_sem=send_sems.at[k - 1],
                                              recv_sem=recv_sems.at[k - 1], device_id=peer(k), device_id_type=MESH)
            cp.start()
            sends.append(cp)
        for k in range(1, N_DEV):
            px, py, pc = peer(k)
            pltpu.make_async_remote_copy(src_ref=x_ref, dst_ref=buf_ref.at[4 * px + 2 * py + pc], send_sem=send_sems.at[k - 1],
                                         recv_sem=recv_sems.at[k - 1], device_id=peer(k), device_id_type=MESH).wait_recv()
        for cp in sends:
            cp.wait_send()
        if reduce:
            acc = buf_ref[0]
            for s in range(1, N_DEV):
                acc = acc + buf_ref[s]
            o_ref[...] = acc

    vm = pl.BlockSpec(memory_space=pltpu.VMEM)
    sems = [pltpu.SemaphoreType.DMA((N_DEV - 1,)), pltpu.SemaphoreType.DMA((N_DEV - 1,))]
    if reduce:
        out_shape, scratch = jax.ShapeDtypeStruct((r, c), F32), [pltpu.VMEM((N_DEV, r, c), F32)] + sems
    else:
        out_shape, scratch = jax.ShapeDtypeStruct((N_DEV, r, c), F32), sems
    return pl.pallas_call(body, name=name, in_specs=[vm], out_specs=vm, out_shape=out_shape, scratch_shapes=scratch)(v)


def _rows(a):
    return a.reshape(-1, D)


def _pad_cols(a, to):
    return jnp.pad(a, ((0, 0), (0, to - a.shape[1])))


def _pack_weights(w):
    parts = {
        "w_inT": jnp.pad(w["w_in"].T, ((0, IN_SHARD_PAD - IN_SHARD), (0, 0))),
        "w_uq": _rows(_head_cols(w["w_uq"])), "w_uk": _rows(_head_cols(w["w_uk"])),
        "w_uv": _rows(_pad_cols(w["w_uv"], HEAD_PAD)), "w_pa": _rows(w["w_proj_attn"]),
        "w_pc": w["w_proj_conv"], "w_out": w["w_out"],
    }
    return [jnp.concatenate([parts[n].astype(BF16) for n, _ in group], axis=0) for group in PACK]


def _cols_from_shards(gs, name, rows):
    idx, off, r = PACK_OFF[name]
    return gs[idx][:, off:off + r].reshape(N_DEV, rows, HEAD_PAD).transpose(1, 0, 2).reshape(rows, N_DEV * HEAD_PAD)


def _rows_from_shards(gs, name, keep=None):
    idx, off, r = PACK_OFF[name]
    keep = r if keep is None else keep
    return gs[idx][:, off:off + keep].reshape(N_DEV * keep, D)


def _rope_placement():
    i = lax.broadcasted_iota(jnp.int32, (HEAD_PAD, D), 0)
    j = lax.broadcasted_iota(jnp.int32, (HEAD_PAD, D), 1)
    lane = jnp.where(i < ROPE_HALF, 32 + i, 96 + i - ROPE_HALF)
    return ((i < 2 * ROPE_HALF) & (j % HEAD_PAD == lane)).astype(BF16)


def _unpack_in(g_in):
    w_inT = _rows_from_shards([g_in, None], "w_inT", IN_SHARD)
    lat_rows = Q_LORA + KV_LORA + 2 * ROPE_HALF
    conv = w_inT[lat_rows:lat_rows + CONV_COLS].reshape(3, D // CONV_CB, CONV_CB, D).transpose(1, 0, 2, 3).reshape(CONV_COLS, D)
    return {"latT": jnp.pad(w_inT[:lat_rows], ((0, LAT_PAD - lat_rows), (0, 0))), "convT": conv,
            "gateT": w_inT[lat_rows + CONV_COLS:]}


def _unpack_misc(g_misc):
    g = [None, g_misc]
    wpa = _cols_from_shards(g, "w_pa", 512).reshape(N_HEADS, NOPE, D)
    return {
        "wq": _cols_from_shards(g, "w_uq", Q_LORA),
        "wk": jnp.concatenate([_cols_from_shards(g, "w_uk", KV_LORA), _rope_placement()], axis=0),
        "wv": _cols_from_shards(g, "w_uv", KV_LORA),
        "wpa": jnp.pad(wpa, ((0, 0), (0, HEAD_PAD - NOPE), (0, 0))).reshape(D, D),
        "wpc": _rows_from_shards(g, "w_pc"), "wout": _rows_from_shards(g, "w_out"),
    }


def _shards_from_cols(a):
    rows = a.shape[0]
    return a.reshape(rows, N_DEV, HEAD_PAD).transpose(1, 0, 2).reshape(N_DEV, rows * HEAD_PAD // D, D)


def _pack_grads(gw):
    lat_rows = Q_LORA + KV_LORA + 2 * ROPE_HALF
    conv = gw["convT"].reshape(D // CONV_CB, 3, CONV_CB, D).transpose(1, 0, 2, 3).reshape(CONV_COLS, D)
    w_inT = jnp.concatenate([gw["latT"][:lat_rows], conv, gw["gateT"]], axis=0).reshape(N_DEV, IN_SHARD, D)
    wpa = gw["wpa"].reshape(N_HEADS, HEAD_PAD, D)[:, :NOPE].reshape(N_HEADS * NOPE, D)
    parts = {}
    parts.update({
        "w_inT": jnp.pad(w_inT, ((0, 0), (0, IN_SHARD_PAD - IN_SHARD), (0, 0))),
        "w_uq": _shards_from_cols(gw["wq"]), "w_uk": _shards_from_cols(gw["wk"][:KV_LORA]),
        "w_uv": _shards_from_cols(gw["wv"][:KV_LORA]), "w_pa": _shards_from_cols(wpa),
        "w_pc": gw["wpc"].reshape(N_DEV, D // N_DEV, D), "w_out": gw["wout"].reshape(N_DEV, D // N_DEV, D),
    })
    return [jnp.concatenate([parts[n] for n, _ in group], axis=1) for group in PACK]


def _unpack_grads(mines):
    def seg(name, keep=None):
        idx, off, r = PACK_OFF[name]
        return mines[idx][off:off + (r if keep is None else keep)]

    return {
        "w_in": seg("w_inT", IN_SHARD).T,
        "w_uq": _head_cols_inv(seg("w_uq").reshape(Q_LORA, HEAD_PAD), QK_DIM),
        "w_uk": _head_cols_inv(seg("w_uk").reshape(KV_LORA, HEAD_PAD), NOPE),
        "w_uv": seg("w_uv").reshape(KV_LORA, HEAD_PAD)[:, :NOPE],
        "w_proj_attn": seg("w_pa").reshape(512, HEAD_PAD),
        "w_proj_conv": seg("w_pc"), "w_out": seg("w_out"),
    }


def _rope_tables(positions):
    lane = jnp.arange(HEAD_PAD)
    idx = jnp.where((lane >= 32) & (lane < 48), lane - 32, jnp.where((lane >= 96) & (lane < 112), lane - 96, -1))
    inv_freq = jnp.where(idx >= 0, 1.0 / (ROPE_THETA ** (idx.astype(F32) / ROPE_HALF)), 0.0)
    ang = positions.reshape(-1).astype(F32)[:, None] * inv_freq
    return jnp.cos(ang), jnp.sin(ang) * jnp.where(lane < HEAD_PAD // 2, -1.0, 1.0)


def _local_step(x, positions, target, conv_w, small, ex):
    n_seq, seq, d = x.shape
    t = n_seq * seq
    x0 = x.reshape(t, d)
    tgt = target.reshape(t, d)
    rc, rs = _rope_tables(positions)
    ghq = _head_cols(small["q_head_norm"])
    ghk = _head_cols(small["k_head_norm"])
    TM, HC, TQ = 1024, 256, 1024

    def mm(*args, hosted=None, **kw):
        res = _mm(*args, hosted=hosted, **kw)
        return res if hosted is not None else (res, None)

    def wgrad(a, b, name, tm=None, hosted=None):
        tm = tm or a.shape[1]
        return mm(a, b, mode="tn", out_dtype=BF16, tm=tm, tn=b.shape[1], tk=2048 if tm <= D else 1024, name=name, hosted=hosted)

    f1g, f1u, f1d = ex.gather_finish(ex.witness() + rc[:8] + conv_w[:1, :HEAD_PAD])
    (x1, h1, a1, b1), got = _ffn_fwd(x0, small["ffn1_norm"], f1g, f1u, f1d, tm=512, hc=DFF // 2, name="ffn1_fwd",
                                     hosted=ex.gather_chips("mix_in"))
    hm, got = _rms_fwd(x1, small["mix_norm"], tm=TM, name="mix_norm_fwd", hosted=ex.gather_sibling(got))
    W = ex.mix_in_weights(got)
    (lat, conv3, gl), got = _proj_fwd(hm, W["latT"], W["convT"], W["gateT"], tm=512, name="proj_fwd",
                                      hosted=ex.gather_chips("mix_misc"))
    p, got = _conv_fwd(conv3, conv_w, n_seq=n_seq, seq=seq, name="conv_fwd", hosted=ex.gather_sibling(got))
    W.update(ex.mix_misc_weights(got))
    q, k, v, qn, ckv = _mla_prep_fwd(lat, small["q_a_norm"], small["kv_a_norm"], ghq, ghk, W["wq"], W["wk"], W["wv"], rc, rs,
                                     tm=512, name="mla_prep_fwd")
    (o, lse), got = _flash_fwd(q, k, v, n_seq=n_seq, seq=seq, tq=TQ, name="attn_fwd", hosted=ex.gather_chips("ffn2"))
    (x2, merged, ya, yb), got = _merge_fwd(o, p, gl, small["gate_bias"], x1, W["wpa"], W["wpc"], W["wout"], tm=512, name="merge_fwd",
                                           hosted=ex.gather_sibling(got))
    f2g, f2u, f2d = ex.ffn_weights(got)
    (dy, h2, a2, b2, loss_row), _ = _ffn_fwd(x2, small["ffn2_norm"], f2g, f2u, f2d, tm=512, hc=DFF // 2, name="ffn2_fwd", target=tgt)

    gw, gs = {}, {}
    (da2, db2, *ffn2_grads), _ = _ffn_grads(dy, h2, a2, b2, f2d, tm=TM, hc=HC, name="ffn2_grads")
    (dx2, gs["ffn2_norm"]), _ = _ffn_up_bwd(da2, db2, f2g, f2u, x2, small["ffn2_norm"], dy, tm=512, name="ffn2_up_bwd")

    (dx2b, dya, dyb, dgl, do, dp, gs["gate_bias"]), got = _merge_bwd(
        dx2, ya, yb, gl, small["gate_bias"], W["wpa"], W["wpc"], W["wout"], tm=512, name="merge_bwd",
        hosted=ex.scatter_sibling("ffn2", ffn2_grads))
    ex.scatter_sibling_done("ffn2", got)
    gw["wout"] = wgrad(merged, dx2b, "dw_out")[0]
    gw["wpa"] = wgrad(o, dya, "dw_pa")[0]
    gw["wpc"] = wgrad(p, dyb, "dw_pc")[0]
    dconv3, dconv_w = _conv_bwd(dp, conv3, conv_w, n_seq=n_seq, seq=seq, name="conv_bwd")
    (dq, dk, dv), got = _flash_bwd(q, k, v, o, lse, do, n_seq=n_seq, seq=seq, tq=TQ, name="attn_bwd",
                                   hosted=ex.scatter_chips("ffn2"))
    ex.scatter_chips_done("ffn2", got)
    dlat, dqp, dkp, gs["q_a_norm"], gs["kv_a_norm"], dghq, dghk = _mla_prep_bwd(
        dq, dk, dv, lat, qn, ckv, small["q_a_norm"], small["kv_a_norm"], ghq, ghk, W["wq"], W["wk"], W["wv"], rc, rs,
        tm=512, name="mla_prep_bwd")
    gs["q_head_norm"], gs["k_head_norm"] = _head_cols_inv(dghq, QK_DIM), _head_cols_inv(dghk, QK_DIM)
    gw["wq"] = wgrad(qn, dqp, "dw_uq")[0]
    gw["wk"] = wgrad(ckv, dkp, "dw_uk")[0]
    gw["wv"] = wgrad(ckv, dv, "dw_uv")[0]
    gw["convT"] = wgrad(dconv3, hm, "dw_conv", tm=CONV_COLS // 2)[0]
    gw["gateT"] = wgrad(dgl, hm, "dw_gate")[0]
    gw["latT"] = wgrad(dlat, hm, "dw_lat")[0]
    ex.scatter_sibling_now("mix", gw)
    zero = ex.scatter_chips_start("mix_in")
    (dx1, gs["mix_norm"]), _ = _proj_bwd(dlat, dconv3, dgl, W["latT"], W["convT"], W["gateT"], x1, small["mix_norm"] + zero, dx2,
                                         tm=512, name="proj_bwd")

    (da1, db1, *ffn1_grads), got = _ffn_grads(dx1, h1, a1, b1, f1d, tm=TM, hc=HC, name="ffn1_grads",
                                              hosted=ex.scatter_chips("mix_misc"))
    ex.scatter_chips_done("mix_misc", got)
    ex.reduce_small(gs, dconv_w, loss_row)
    ex.scatter_sibling_now("ffn1", ffn1_grads)
    zero = ex.scatter_chips_start("ffn1")
    (dx0, gs["ffn1_norm"]), _ = _ffn_up_bwd(da1, db1, f1g, f1u, x0, small["ffn1_norm"] + zero, dx1, tm=512, name="ffn1_up_bwd")
    return dx0.reshape(n_seq, seq, d), gs["ffn1_norm"]


class _MeshExchange:
    def __init__(self, w, core, chip):
        self.w, self.core, self.chip = w, core, chip
        self.partial, self.received, self._cache, self._scattering = {}, {}, {}, {}

    def _blocks(self, group):
        w = self.w
        if group not in self._cache:
            if group.startswith("ffn"):
                self._cache[group] = [w[group + "_w_gate"].T.astype(BF16), w[group + "_w_up"].T.astype(BF16),
                                      w[group + "_w_down"].astype(BF16)]
            else:
                self._cache["mix_in"], self._cache["mix_misc"] = [[b] for b in _pack_weights(w)]
        return self._cache[group]

    def gather_chips(self, *groups):
        blocks = [b for group in groups for b in self._blocks(group)]
        return _gather_chips_plan(len(blocks)), blocks, _gather_shapes(blocks)

    def gather_sibling(self, got):
        half = list(got)
        return _gather_sibling_plan(len(half)), half, _same_shapes(half)

    def gather_start(self, group):
        self._gathering = []
        for tag, blocks in (("a", self._blocks(group)[:2]), ("b", self._blocks(group)[2:])):
            plan = _gather_chips_plan(len(blocks))
            name = "gather_%s%s" % (group, tag)
            sems, blocks, lands, token = _plan_start(plan, blocks, _gather_shapes(blocks), name=name + "_start")
            self._gathering.append((name, plan, sems, blocks, lands))
        return token[0, 0]

    def gather_finish(self, after):
        got = []
        for name, plan, sems, blocks, lands in self._gathering:
            _, half = _plan_wait(plan, sems, blocks, lands, after, name=name + "_wait")
            got += _run_plan(_gather_sibling_plan(len(half)), half, _same_shapes(half), name=name + "_sibling")
            after = got[-1]
        return self.ffn_weights(got)

    def reduce_small(self, gs, dconv_w, loss_row):
        pieces = [_pad_cols(gs[n], SMALL_SLOTS[n]) for n in SMALL_NAMES[1:]] + [dconv_w.reshape(1, 3 * D), loss_row]
        self.small_total = _small_exchange(jnp.concatenate(pieces, axis=1).reshape(-1, 128), reduce=True,
                                           name="reduce_small").reshape(-1)

    def scatter_chips_start(self, group):
        s1 = self.partial[group]
        plan = _scatter_chips_plan(len(s1))
        sems, s1, lands, token = _plan_start(plan, s1, _scatter_shapes(s1), name="scatter_%s_start" % group)
        self._scattering[group] = (plan, sems, s1, lands)
        return token[0, 0]

    def scatter_chips_finish(self, group, after):
        plan, sems, s1, lands = self._scattering[group]
        self.partial[group], self.received[group] = _plan_wait(plan, sems, s1, lands, after, name="scatter_%s_wait" % group)

    def witness(self):
        parts = [b[:8, :128].astype(F32) for g in ("mix_in", "mix_misc", "ffn2") for b in self._blocks(g)]
        return functools.reduce(jnp.add, parts)

    def ffn_weights(self, got):
        return [a.reshape(DFF, D) for a in got]

    def mix_in_weights(self, got):
        return _unpack_in(got[0])

    def mix_misc_weights(self, got):
        return _unpack_misc(got[0])

    def _parts(self, group, grads):
        if group == "mix":
            return _pack_grads(grads), ["mix_in", "mix_misc"]
        parts = [g.reshape(N_DEV, -1, D) for g in grads]
        return parts, ([group] if len(parts) == 1 else None)

    def scatter_sibling(self, group, grads):
        self._sent, self._names = self._parts(group, grads)
        return _scatter_sibling_plan(len(self._sent)), self._sent, _halved_shapes(self._sent)

    def scatter_sibling_done(self, group, got):
        sums = [_sum_sibling(p, q, self.core, name="sum_%s_sibling_%d" % (group, i)) for i, (p, q) in enumerate(zip(self._sent, got))]
        if self._names is None:
            self.partial[group] = sums
        else:
            for n, s in zip(self._names, sums):
                self.partial[n] = [s]

    def scatter_sibling_now(self, group, grads):
        plan, parts, shapes = self.scatter_sibling(group, grads)
        self.scatter_sibling_done(group, _run_plan(plan, parts, shapes, name="scatter_%s_sibling" % group))

    def scatter_chips(self, group):
        s1 = self.partial[group]
        return _scatter_chips_plan(len(s1)), s1, _scatter_shapes(s1)

    def scatter_chips_done(self, group, got):
        self.received[group] = list(got)


SMALL_NAMES = ("ffn1_norm", "mix_norm", "gate_bias", "q_a_norm", "kv_a_norm", "q_head_norm", "k_head_norm", "ffn2_norm")
SMALL_SLOTS = {"ffn1_norm": 1024, "mix_norm": 1024, "gate_bias": 2048, "q_a_norm": 384, "kv_a_norm": 256, "q_head_norm": 128,
               "k_head_norm": 128, "ffn2_norm": 1024, "conv_w": 3072, "loss": 128}
COLUMN_MAJOR = ("w_in", "w_uq", "w_uk", "w_uv")
WEIGHT_NAMES = ("ffn1_norm", "ffn1_w_gate", "ffn1_w_up", "ffn1_w_down", "mix_norm", "w_in", "gate_bias", "q_a_norm", "w_uq",
                "kv_a_norm", "w_uk", "w_uv", "q_head_norm", "k_head_norm", "w_proj_attn", "conv_w", "w_proj_conv", "w_out",
                "ffn2_norm", "ffn2_w_gate", "ffn2_w_up", "ffn2_w_down")


def _step(x, positions, loss_target, w, m, v):
    xi, yi, ci = _place()
    core = ci.astype(jnp.int32).reshape(1)
    chip = (2 * xi + yi).astype(jnp.int32).reshape(1)
    me = 4 * xi + 2 * yi + ci

    ex = _MeshExchange(w, core, chip)
    cw_all = _small_exchange(jnp.pad(w["conv_w"], ((0, 5), (0, 0))), reduce=False, name="gather_conv_w")
    conv_w = cw_all[:, :3].transpose(1, 0, 2).reshape(3, D)
    ex.w = {n: (a + cw_all[0, 7, 0] if n.startswith("ffn1") else a) for n, a in w.items()}
    zero = ex.gather_start("ffn1")
    ex.w = {n: (a if n.startswith("ffn1") else a + zero) for n, a in w.items()}
    small = {n: w[n].reshape(1, -1) for n in SMALL_NAMES}

    grad_x, dffn1_norm = _local_step(x, positions + zero.astype(jnp.int32), loss_target, conv_w, small, ex)

    grads, deltas, new_m, new_v = {}, {}, {}, {}

    def ffn_update(group):
        for i, n in enumerate((group + "_w_gate", group + "_w_up", group + "_w_down")):
            transposed = not n.endswith("down")
            wv, mv, vv = (a[n].T if transposed else a[n] for a in (w, m, v))
            res = _sum_adamw(ex.partial[group][i], ex.received[group][i], chip, wv, mv, vv, name="adamw_" + n)
            grads[n], deltas[n], new_m[n], new_v[n] = (r.T if transposed else r for r in res)

    def update(n):
        shape = w[n].shape
        if n in COLUMN_MAJOR:
            ops = [a.T for a in (w[n], grads[n], m[n], v[n])]
            deltas[n], new_m[n], new_v[n] = (r.T for r in _adamw(*ops, name="adamw_" + n))
            return
        view = shape if len(shape) == 2 else ((-1, 128) if shape[0] % 128 == 0 else (1, shape[0]))
        dlt, nm, nv = _adamw(w[n].reshape(view), grads[n].reshape(view), m[n].reshape(view), v[n].reshape(view), name="adamw_" + n)
        deltas[n], new_m[n], new_v[n] = dlt.reshape(shape), nm.reshape(shape), nv.reshape(shape)

    ffn_update("ffn2")
    ex.scatter_chips_finish("mix_in", dffn1_norm)
    grads.update(_unpack_grads([_sum_chips(ex.partial[g][0], ex.received[g][0], chip, name="sum_%s_chips" % g)
                                for g in ("mix_in", "mix_misc")]))
    total, off = ex.small_total, 0
    for n in SMALL_NAMES[1:]:
        grads[n] = total[off:off + w[n].shape[0]]
        off += SMALL_SLOTS[n]
    conv_full = total[off:off + 3 * D].reshape(3, D)
    grads["conv_w"] = lax.dynamic_slice(conv_full, (0, me * HEAD_PAD), (3, HEAD_PAD))
    loss = total[off + 3 * D]
    later = ("ffn1_norm", "ffn1_w_gate", "ffn1_w_up", "ffn1_w_down")
    for n in WEIGHT_NAMES:
        if n not in deltas and n not in later:
            update(n)

    done = [deltas[n][:8, :128] for n in ("ffn2_w_down", "w_in", "w_out", "w_proj_attn")] + [deltas["mix_norm"].reshape(8, 128)]
    ex.scatter_chips_finish("ffn1", functools.reduce(jnp.add, done) + grad_x.reshape(-1, D)[:8, :128])
    ffn_update("ffn1")
    last = dffn1_norm + 0.0 * grads["ffn1_w_down"][:1, :1]
    grads["ffn1_norm"] = _small_exchange(last.reshape(-1, 128), reduce=True, name="reduce_ffn1_norm").reshape(-1)
    update("ffn1_norm")
    return (loss, grad_x, *[grads[n] for n in WEIGHT_NAMES], *[deltas[n] for n in WEIGHT_NAMES],
            *[new_m[n] for n in WEIGHT_NAMES], *[new_v[n] for n in WEIGHT_NAMES])


def kernel(x, positions, ffn1_norm, ffn1_w_gate, ffn1_w_up, ffn1_w_down, mix_norm, w_in, gate_bias, q_a_norm, w_uq, kv_a_norm, w_uk, w_uv, q_head_norm, k_head_norm, w_proj_attn, conv_w, w_proj_conv, w_out, ffn2_norm, ffn2_w_gate, ffn2_w_up, ffn2_w_down, loss_target, m_ffn1_norm, m_ffn1_w_gate, m_ffn1_w_up, m_ffn1_w_down, m_mix_norm, m_w_in, m_gate_bias, m_q_a_norm, m_w_uq, m_kv_a_norm, m_w_uk, m_w_uv, m_q_head_norm, m_k_head_norm, m_w_proj_attn, m_conv_w, m_w_proj_conv, m_w_out, m_ffn2_norm, m_ffn2_w_gate, m_ffn2_w_up, m_ffn2_w_down, v_ffn1_norm, v_ffn1_w_gate, v_ffn1_w_up, v_ffn1_w_down, v_mix_norm, v_w_in, v_gate_bias, v_q_a_norm, v_w_uq, v_kv_a_norm, v_w_uk, v_w_uv, v_q_head_norm, v_k_head_norm, v_w_proj_attn, v_conv_w, v_w_proj_conv, v_w_out, v_ffn2_norm, v_ffn2_w_gate, v_ffn2_w_up, v_ffn2_w_down):
    given = dict(locals())
    w = {n: given[n] for n in WEIGHT_NAMES}
    m = {n: given["m_" + n] for n in WEIGHT_NAMES}
    v = {n: given["v_" + n] for n in WEIGHT_NAMES}
    return _step(x, positions, loss_target, w, m, v)
```

```python
import functools

import jax
import jax.numpy as jnp
from jax import lax
from jax.experimental import pallas as pl
from jax.experimental.pallas import tpu as pltpu

F32 = jnp.float32
BF16 = jnp.bfloat16
MESH = pl.DeviceIdType.MESH
ANY = pl.BlockSpec(memory_space=pl.ANY)

N_DEV = 8
D = 1024
DFF = 2816
N_HEADS = 8
HEAD_PAD = 128
QK_DIM = 96
NOPE = 64
ROPE_HALF = 16
Q_LORA = 384
KV_LORA = 256
LAT_PAD = 768
CONV_COLS = 3072
GATE_COLS = 2048
IN_DIM = 5792
IN_SHARD = IN_DIM // N_DEV
IN_SHARD_PAD = 736
FF_SHARD = DFF // N_DEV
ROPE_THETA = 10000.0
NORM_EPS = 1e-6
ATTN_SCALE = QK_DIM ** -0.5
NEG = -1e30

ADAM_LR, ADAM_B1, ADAM_B2, ADAM_EPS, ADAM_WD, ADAM_STEP = 0.001, 0.9, 0.999, 1e-08, 0.01, 10

PACK = ((("w_inT", IN_SHARD_PAD),), (("w_uq", 48), ("w_uk", 32), ("w_uv", 32), ("w_pa", 64), ("w_pc", 128), ("w_out", 128)))
PACK_OFF = {}
for _i, _group in enumerate(PACK):
    _o = 0
    for _n, _r in _group:
        PACK_OFF[_n] = (_i, _o, _r)
        _o += _r

VMEM_LIMIT = 56 * 1024 * 1024


def _params(*sem):
    return pltpu.CompilerParams(dimension_semantics=sem if sem else None, vmem_limit_bytes=VMEM_LIMIT)


class _Plan:
    def __init__(self, start, wait, n_remote, n_local, in_place=False):
        self.start, self.wait, self.n_remote, self.n_local, self.in_place = start, wait, n_remote, n_local, in_place

    def sems(self):
        return [pltpu.SemaphoreType.DMA((self.n_remote,)), pltpu.SemaphoreType.DMA((self.n_remote,)),
                pltpu.SemaphoreType.DMA((max(self.n_local, 1),))]


def _call(body, *, name, grid, in_specs, out_specs, out_shape, scratch_shapes, operands, sem, hosted=None):
    if hosted is None:
        outs = pl.pallas_call(body, name=name, grid=grid, in_specs=in_specs, out_specs=out_specs, out_shape=out_shape,
                              scratch_shapes=scratch_shapes, compiler_params=_params(*sem))(*operands)
        return outs, None
    plan, srcs, h_shapes = hosted
    n_in, n_out, n_scr, nh_in, nh_out = len(in_specs), len(out_specs), len(scratch_shapes), len(srcs), len(h_shapes)
    aliases = {n_in + a: n_out + a for a in range(nh_in)} if plan.in_place else {}

    def full_body(*refs):
        ins, refs = refs[:n_in], refs[n_in:]
        h_in, refs = refs[:nh_in], refs[nh_in:]
        outs, refs = refs[:n_out], refs[n_out:]
        h_out, refs = refs[:nh_out], refs[nh_out:]
        scr, sems = refs[:n_scr], refs[n_scr:]
        ids = [pl.program_id(ax) for ax in range(len(grid))]
        first = functools.reduce(jnp.logical_and, [i == 0 for i in ids])
        last = functools.reduce(jnp.logical_and, [i == g - 1 for i, g in zip(ids, grid)])

        @pl.when(first)
        def _():
            plan.start(h_in, h_out, *sems)

        body(*ins, *outs, *scr)

        @pl.when(last)
        def _():
            plan.wait(h_in, h_out, *sems)

    res = pl.pallas_call(
        full_body, name=name, grid=grid, in_specs=list(in_specs) + [ANY] * nh_in, out_specs=list(out_specs) + [ANY] * nh_out,
        out_shape=list(out_shape) + list(h_shapes), scratch_shapes=list(scratch_shapes) + plan.sems(),
        input_output_aliases=aliases, compiler_params=_params(*(["arbitrary"] * len(grid))),
    )(*operands, *srcs)
    return res[:n_out], res[n_out:]


def _dot_nn(a, b):
    return lax.dot_general(a, b, (((1,), (0,)), ((), ())), preferred_element_type=F32)


def _dot_nt(a, b):
    return lax.dot_general(a, b, (((1,), (1,)), ((), ())), preferred_element_type=F32)


def _dot_tn(a, b):
    return lax.dot_general(a, b, (((0,), (0,)), ((), ())), preferred_element_type=F32)


def _sigmoid(x):
    return 0.5 * jnp.tanh(0.5 * x) + 0.5


def _rms_stats(x):
    r = lax.rsqrt(jnp.mean(x * x, axis=-1, keepdims=True) + NORM_EPS)
    return x * r, r


ROWS_WIDE = 16
MM_ROWS = 256


def _rms_bwd(dy, xhat, r, g):
    dg = jnp.sum(dy * xhat, axis=0, keepdims=True)
    dxh = dy * g
    dx = r * (dxh - xhat * jnp.mean(dxh * xhat, axis=-1, keepdims=True))
    return dx, dg


def _mm(a, b, *, mode, out_dtype, tm, tn, tk, name, add=None, scale=1.0, hosted=None):
    if mode == "nn":
        (m, k), (_, n) = a.shape, b.shape
    elif mode == "nt":
        (m, k), (n, _) = a.shape, b.shape
    else:
        (k, m), (_, n) = a.shape, b.shape
    assert m % tm == 0 and n % tn == 0 and k % tk == 0, (name, m, n, k, tm, tn, tk)
    nk = k // tk
    dot = {"nn": _dot_nn, "nt": _dot_nt, "tn": _dot_tn}[mode]
    a_spec = pl.BlockSpec((tk, tm), lambda i, j, kk: (kk, i)) if mode == "tn" else pl.BlockSpec((tm, tk), lambda i, j, kk: (i, kk))
    b_spec = pl.BlockSpec((tn, tk), lambda i, j, kk: (j, kk)) if mode == "nt" else pl.BlockSpec((tk, tn), lambda i, j, kk: (kk, j))
    o_spec = pl.BlockSpec((tm, tn), lambda i, j, kk: (i, j))
    has_add = add is not None

    def finish(prod, c_ref, o_ref):
        if scale != 1.0:
            prod = prod * scale
        o_ref[...] = ((c_ref[...] + prod) if has_add else prod).astype(out_dtype)

    def body(*refs):
        a_ref, b_ref = refs[:2]
        c_ref = refs[2] if has_add else None
        o_ref = refs[3] if has_add else refs[2]
        if nk == 1:
            finish(dot(a_ref[...], b_ref[...]), c_ref, o_ref)
            return
        acc_ref = refs[-1]
        kk = pl.program_id(2)

        @pl.when(kk == 0)
        def _():
            acc_ref[...] = jnp.zeros_like(acc_ref)

        acc_ref[...] += dot(a_ref[...], b_ref[...])

        @pl.when(kk == nk - 1)
        def _():
            finish(acc_ref[...], c_ref, o_ref)

    operands = (a, b, add) if has_add else (a, b)
    in_specs = [a_spec, b_spec] + ([o_spec] if has_add else [])
    (out,), got = _call(
        body, name=name, grid=(m // tm, n // tn, nk), in_specs=in_specs, out_specs=[o_spec],
        out_shape=[jax.ShapeDtypeStruct((m, n), out_dtype)], scratch_shapes=[pltpu.VMEM((tm, tn), F32)] if nk > 1 else [],
        operands=operands, sem=("parallel", "parallel", "arbitrary"), hosted=hosted)
    return out if hosted is None else (out, got)


def _rms_fwd(x, g, *, tm, name, hosted=None):
    t, d = x.shape

    def body(x_ref, g_ref, h_ref):
        xhat, _ = _rms_stats(x_ref[...])
        h_ref[...] = (xhat * g_ref[...]).astype(BF16)

    (h,), got = _call(
        body, name=name, grid=(t // tm,),
        in_specs=[pl.BlockSpec((tm, d), lambda i: (i, 0)), pl.BlockSpec((1, d), lambda i: (0, 0))],
        out_specs=[pl.BlockSpec((tm, d), lambda i: (i, 0))], out_shape=[jax.ShapeDtypeStruct((t, d), BF16)], scratch_shapes=[],
        operands=(x, g), sem=("parallel",), hosted=hosted)
    return h, got


def _ffn_fwd(x, g, wgT, wuT, wd, *, tm, hc, name, hosted=None, target=None):
    t, d = x.shape
    nj = DFF // hc
    with_loss = target is not None

    def body(*refs):
        x_ref, g_ref, wg_ref, wu_ref, wd_ref = refs[:5]
        t_ref = refs[5] if with_loss else None
        xo_ref, h_ref, a_ref, b_ref = refs[5 + with_loss:9 + with_loss]
        loss_ref = refs[9 + with_loss] if with_loss else None
        acc_ref = refs[-1]
        i, j = pl.program_id(0), pl.program_id(1)

        @pl.when(j == 0)
        def _():
            xhat, _ = _rms_stats(x_ref[...])
            h_ref[...] = (xhat * g_ref[...]).astype(BF16)
            acc_ref[...] = jnp.zeros_like(acc_ref)

        h = h_ref[...]
        a = _dot_nt(h, wg_ref[...])
        b = _dot_nt(h, wu_ref[...])
        a_ref[...] = a.astype(BF16)
        b_ref[...] = b.astype(BF16)
        s = (a * _sigmoid(a) * b).astype(BF16)
        acc_ref[...] += _dot_nn(s, wd_ref[...])

        if with_loss:
            @pl.when((i == 0) & (j == 0))
            def _():
                loss_ref[...] = jnp.zeros_like(loss_ref)

        @pl.when(j == nj - 1)
        def _():
            y = x_ref[...] + 0.5 * acc_ref[...]
            if with_loss:
                err = y - t_ref[...]
                xo_ref[...] = err * (1.0 / d)
                loss_ref[...] += jnp.sum(jnp.sum(err * err, axis=-1, keepdims=True), axis=0, keepdims=True) * (0.5 / d)
            else:
                xo_ref[...] = y

    row = pl.BlockSpec((tm, d), lambda i, j: (i, 0))
    vec = pl.BlockSpec((1, d), lambda i, j: (0, 0))
    wsp = pl.BlockSpec((hc, d), lambda i, j: (j, 0))
    hid = pl.BlockSpec((tm, hc), lambda i, j: (i, j))
    out_specs = [row, row, hid, hid] + ([pl.BlockSpec((1, 128), lambda i, j: (0, 0))] if with_loss else [])
    out_shape = [jax.ShapeDtypeStruct((t, d), F32), jax.ShapeDtypeStruct((t, d), BF16), jax.ShapeDtypeStruct((t, DFF), BF16),
                 jax.ShapeDtypeStruct((t, DFF), BF16)] + ([jax.ShapeDtypeStruct((1, 128), F32)] if with_loss else [])
    return _call(
        body, name=name, grid=(t // tm, nj), in_specs=[row, vec, wsp, wsp, wsp] + ([row] if with_loss else []),
        out_specs=out_specs, out_shape=out_shape, scratch_shapes=[pltpu.VMEM((tm, d), F32)],
        operands=(x, g, wgT, wuT, wd) + ((target,) if with_loss else ()),
        sem=("arbitrary" if with_loss else "parallel", "arbitrary"), hosted=hosted)


def _ffn_grads(dout, h, a, b, wd, *, tm, hc, name, hosted=None):
    t, d = dout.shape
    ni, nj = t // tm, DFF // hc

    def body(dout_ref, h_ref, a_ref, b_ref, wd_ref, da_ref, db_ref, dwg_ref, dwu_ref, dwd_ref,
             dy_all, h_all, ds_scr, s_scr, acc_g, acc_u, acc_d):
        j, i = pl.program_id(0), pl.program_id(1)
        rows_i = pl.ds(pl.multiple_of(i * tm, tm), tm)

        @pl.when(j == 0)
        def _():
            dy_all[rows_i, :] = (0.5 * dout_ref[...]).astype(BF16)
            h_all[rows_i, :] = h_ref[...]

        @pl.when(i == 0)
        def _():
            acc_g[...] = jnp.zeros_like(acc_g)
            acc_u[...] = jnp.zeros_like(acc_u)
            acc_d[...] = jnp.zeros_like(acc_d)

        def grad_rows(rows):
            ds = ds_scr[rows, :]
            av = a_ref[rows, :].astype(F32)
            bv = b_ref[rows, :].astype(F32)
            sg = _sigmoid(av)
            sl = av * sg
            s_scr[rows, :] = (sl * bv).astype(BF16)
            da_ref[rows, :] = (ds * bv * (sg + sl * (1.0 - sg))).astype(BF16)
            db_ref[rows, :] = (ds * sl).astype(BF16)

        for blk in range(tm // MM_ROWS):
            rs = slice(blk * MM_ROWS, (blk + 1) * MM_ROWS)
            ds_scr[rs, :] = _dot_nt(dy_all[pl.ds(pl.multiple_of(i * tm + blk * MM_ROWS, MM_ROWS), MM_ROWS), :], wd_ref[...])
            for c in range(MM_ROWS // ROWS_WIDE):
                grad_rows(slice(blk * MM_ROWS + c * ROWS_WIDE, blk * MM_ROWS + (c + 1) * ROWS_WIDE))

        dy_i = dy_all[rows_i, :]
        h_i = h_all[rows_i, :]
        acc_d[...] += _dot_tn(s_scr[...], dy_i)
        acc_g[...] += _dot_tn(da_ref[...], h_i)
        acc_u[...] += _dot_tn(db_ref[...], h_i)

        @pl.when(i == ni - 1)
        def _():
            dwg_ref[...] = acc_g[...].astype(BF16)
            dwu_ref[...] = acc_u[...].astype(BF16)
            dwd_ref[...] = acc_d[...].astype(BF16)

    first = pl.BlockSpec((tm, d), lambda j, i: (jnp.where(j == 0, i, 0), 0))
    hid = pl.BlockSpec((tm, hc), lambda j, i: (i, j))
    wsp = pl.BlockSpec((hc, d), lambda j, i: (j, 0))
    hid_shape = jax.ShapeDtypeStruct((t, DFF), BF16)
    w_shape = jax.ShapeDtypeStruct((DFF, d), BF16)
    return _call(
        body, name=name, grid=(nj, ni), in_specs=[first, first, hid, hid, wsp], out_specs=[hid, hid, wsp, wsp, wsp],
        out_shape=[hid_shape, hid_shape, w_shape, w_shape, w_shape],
        scratch_shapes=[pltpu.VMEM((t, d), BF16), pltpu.VMEM((t, d), BF16), pltpu.VMEM((tm, hc), F32), pltpu.VMEM((tm, hc), BF16),
                        pltpu.VMEM((hc, d), F32), pltpu.VMEM((hc, d), F32), pltpu.VMEM((hc, d), F32)],
        operands=(dout, h, a, b, wd), sem=("arbitrary", "arbitrary"), hosted=hosted)


def _proj_fwd(h, latT, convT, gateT, *, tm, name, hosted=None):
    t, d = h.shape

    def body(h_ref, wl_ref, wc_ref, wg_ref, lat_ref, conv_ref, gl_ref):
        hv = h_ref[...]
        lat_ref[...] = _dot_nt(hv, wl_ref[...]).astype(BF16)
        conv_ref[...] = _dot_nt(hv, wc_ref[...]).astype(BF16)
        gl_ref[...] = _dot_nt(hv, wg_ref[...]).astype(BF16)

    def rows(w):
        return pl.BlockSpec((tm, w), lambda i: (i, 0))

    def full(r):
        return pl.BlockSpec((r, d), lambda i: (0, 0))

    return _call(
        body, name=name, grid=(t // tm,), in_specs=[rows(d), full(LAT_PAD), full(CONV_COLS), full(GATE_COLS)],
        out_specs=[rows(LAT_PAD), rows(CONV_COLS), rows(GATE_COLS)],
        out_shape=[jax.ShapeDtypeStruct((t, LAT_PAD), BF16), jax.ShapeDtypeStruct((t, CONV_COLS), BF16),
                   jax.ShapeDtypeStruct((t, GATE_COLS), BF16)],
        scratch_shapes=[], operands=(h, latT, convT, gateT), sem=("parallel",), hosted=hosted)


def _proj_bwd(dlat, dconv3, dgl, latT, convT, gateT, x, g, dres, *, tm, name, hosted=None):
    t, d = x.shape

    def body(dl_ref, dc_ref, dg_ref, wl_ref, wc_ref, wg_ref, x_ref, g_ref, dres_ref, dx_ref, dgain_ref):
        @pl.when(pl.program_id(0) == 0)
        def _():
            dgain_ref[...] = jnp.zeros_like(dgain_ref)

        dh = _dot_nn(dl_ref[...], wl_ref[...]) + _dot_nn(dc_ref[...], wc_ref[...]) + _dot_nn(dg_ref[...], wg_ref[...])
        xhat, r = _rms_stats(x_ref[...])
        dx, dgain = _rms_bwd(dh, xhat, r, g_ref[...])
        dx_ref[...] = dres_ref[...] + dx
        dgain_ref[...] += dgain

    def rows(w):
        return pl.BlockSpec((tm, w), lambda i: (i, 0))

    def full(r):
        return pl.BlockSpec((r, d), lambda i: (0, 0))

    return _call(
        body, name=name, grid=(t // tm,),
        in_specs=[rows(LAT_PAD), rows(CONV_COLS), rows(GATE_COLS), full(LAT_PAD), full(CONV_COLS), full(GATE_COLS), rows(d), full(1), rows(d)],
        out_specs=[rows(d), full(1)], out_shape=[jax.ShapeDtypeStruct((t, d), F32), jax.ShapeDtypeStruct((1, d), F32)],
        scratch_shapes=[], operands=(dlat, dconv3, dgl, latT, convT, gateT, x, g, dres), sem=("arbitrary",), hosted=hosted)


def _ffn_up_bwd(da, db, wgT, wuT, x, g, dout, *, tm, name, hosted=None):
    t, d = x.shape

    def body(da_ref, db_ref, wg_ref, wu_ref, x_ref, g_ref, dout_ref, dx_ref, dg_ref):
        @pl.when(pl.program_id(0) == 0)
        def _():
            dg_ref[...] = jnp.zeros_like(dg_ref)

        dh = _dot_nn(da_ref[...], wg_ref[...]) + _dot_nn(db_ref[...], wu_ref[...])
        xhat, r = _rms_stats(x_ref[...])
        dx, dg = _rms_bwd(dh, xhat, r, g_ref[...])
        dx_ref[...] = dout_ref[...] + dx
        dg_ref[...] += dg

    row = pl.BlockSpec((tm, d), lambda i: (i, 0))
    vec = pl.BlockSpec((1, d), lambda i: (0, 0))
    hid = pl.BlockSpec((tm, DFF), lambda i: (i, 0))
    wsp = pl.BlockSpec((DFF, d), lambda i: (0, 0))
    return _call(
        body, name=name, grid=(t // tm,), in_specs=[hid, hid, wsp, wsp, row, vec, row], out_specs=[row, vec],
        out_shape=[jax.ShapeDtypeStruct((t, d), F32), jax.ShapeDtypeStruct((1, d), F32)], scratch_shapes=[],
        operands=(da, db, wgT, wuT, x, g, dout), sem=("arbitrary",), hosted=hosted)


HEAD_LANES = (slice(0, 32), slice(64, 80), None, slice(32, 64), slice(80, 96), None)


def _head_cols(a):
    def part(sl, width):
        if sl is None or sl.stop > a.shape[1]:
            return jnp.zeros((a.shape[0], width), a.dtype)
        return a[:, sl]

    return jnp.concatenate([part(sl, w) for sl, w in zip(HEAD_LANES, (32, 16, 16, 32, 16, 16))], axis=1)


def _head_cols_inv(a, dims):
    parts = [a[:, 0:32], a[:, 64:96]] + ([a[:, 32:48], a[:, 96:112]] if dims == QK_DIM else [])
    return jnp.concatenate(parts, axis=1)


def _rope_fwd(x, c, s):
    return x * c + pltpu.roll(x, HEAD_PAD // 2, 1) * s


def _rope_bwd(dy, c, s):
    return dy * c + pltpu.roll(dy * s, HEAD_PAD // 2, 1)


def _head_stats(x):
    r = lax.rsqrt(jnp.sum(x * x, axis=-1, keepdims=True) * (1.0 / QK_DIM) + NORM_EPS)
    return x * r, r


def _mla_prep_fwd(lat, gq, gkv, ghq, ghk, wq, wk, wv, rc, rs, *, tm, name):
    t = lat.shape[0]

    def body(lat_ref, gq_ref, gkv_ref, ghq_ref, ghk_ref, wq_ref, wk_ref, wv_ref, c_ref, s_ref,
             q_ref, k_ref, v_ref, qn_ref, ckv_ref):
        lat_v = lat_ref[...]
        qhat, _ = _rms_stats(lat_v[:, :Q_LORA].astype(F32))
        qn = (qhat * gq_ref[...]).astype(BF16)
        khat, _ = _rms_stats(lat_v[:, Q_LORA:Q_LORA + KV_LORA].astype(F32))
        ckv = (khat * gkv_ref[...]).astype(BF16)
        ckv_ext = jnp.concatenate([ckv, lat_v[:, Q_LORA + KV_LORA:]], axis=1)
        qn_ref[...] = qn
        ckv_ref[...] = ckv_ext
        q_pre = _dot_nn(qn, wq_ref[...])
        k_pre = _dot_nn(ckv_ext, wk_ref[...])
        v_ref[...] = _dot_nn(ckv, wv_ref[...]).astype(BF16)
        c, s = c_ref[...], s_ref[...]
        for h in range(N_HEADS):
            hs = slice(h * HEAD_PAD, (h + 1) * HEAD_PAD)
            xq, _ = _head_stats(q_pre[:, hs])
            q_ref[:, hs] = _rope_fwd(xq * ghq_ref[...], c, s).astype(BF16)
            xk, _ = _head_stats(k_pre[:, hs])
            k_ref[:, hs] = _rope_fwd(xk * ghk_ref[...], c, s).astype(BF16)

    def row(w):
        return pl.BlockSpec((tm, w), lambda i: (i, 0))

    def full(r, w):
        return pl.BlockSpec((r, w), lambda i: (0, 0))

    wide = jax.ShapeDtypeStruct((t, D), BF16)
    lat3 = jax.ShapeDtypeStruct((t, Q_LORA), BF16)
    return pl.pallas_call(
        body, name=name, grid=(t // tm,),
        in_specs=[row(LAT_PAD), full(1, Q_LORA), full(1, KV_LORA), full(1, HEAD_PAD), full(1, HEAD_PAD),
                  full(Q_LORA, D), full(Q_LORA, D), full(KV_LORA, D), row(HEAD_PAD), row(HEAD_PAD)],
        out_specs=[row(D), row(D), row(D), row(Q_LORA), row(Q_LORA)],
        out_shape=[wide, wide, wide, lat3, lat3],
        compiler_params=_params("parallel"),
    )(lat, gq, gkv, ghq, ghk, wq, wk, wv, rc, rs)


def _mla_prep_bwd(dq, dk, dv, lat, qn, ckv_ext, gq, gkv, ghq, ghk, wq, wk, wv, rc, rs, *, tm, name):
    t = lat.shape[0]

    def body(dq_ref, dk_ref, dv_ref, lat_ref, qn_ref, ckv_ref, gq_ref, gkv_ref, ghq_ref, ghk_ref, wq_ref, wk_ref, wv_ref,
             c_ref, s_ref, dlat_ref, dqp_ref, dkp_ref, dgq_ref, dgkv_ref, dghq_ref, dghk_ref):
        @pl.when(pl.program_id(0) == 0)
        def _():
            dgq_ref[...] = jnp.zeros_like(dgq_ref)
            dgkv_ref[...] = jnp.zeros_like(dgkv_ref)
            dghq_ref[...] = jnp.zeros_like(dghq_ref)
            dghk_ref[...] = jnp.zeros_like(dghk_ref)

        c, s = c_ref[...], s_ref[...]
        q_pre = _dot_nn(qn_ref[...], wq_ref[...])
        k_pre = _dot_nn(ckv_ref[...], wk_ref[...])

        def heads(pre, dy_ref, gh_ref, dgh_ref, out_ref):
            dgh = jnp.zeros((1, HEAD_PAD), F32)
            for h in range(N_HEADS):
                hs = slice(h * HEAD_PAD, (h + 1) * HEAD_PAD)
                d = _rope_bwd(dy_ref[:, hs].astype(F32), c, s)
                xhat, r = _head_stats(pre[:, hs])
                dgh = dgh + jnp.sum(d * xhat, axis=0, keepdims=True)
                dxh = d * gh_ref[...]
                dx = r * (dxh - xhat * (jnp.sum(dxh * xhat, axis=-1, keepdims=True) * (1.0 / QK_DIM)))
                out_ref[:, hs] = dx.astype(BF16)
            dgh_ref[...] += dgh

        heads(q_pre, dq_ref, ghq_ref, dghq_ref, dqp_ref)
        heads(k_pre, dk_ref, ghk_ref, dghk_ref, dkp_ref)
        dqn = _dot_nt(dqp_ref[...], wq_ref[...])
        dce = _dot_nt(dkp_ref[...], wk_ref[...])
        dckv = dce[:, :KV_LORA] + _dot_nt(dv_ref[...], wv_ref[...])
        lat_v = lat_ref[...]
        qhat, rq = _rms_stats(lat_v[:, :Q_LORA].astype(F32))
        dql, dgq = _rms_bwd(dqn, qhat, rq, gq_ref[...])
        khat, rk = _rms_stats(lat_v[:, Q_LORA:Q_LORA + KV_LORA].astype(F32))
        dkl, dgkv = _rms_bwd(dckv, khat, rk, gkv_ref[...])
        dgq_ref[...] += dgq
        dgkv_ref[...] += dgkv
        dlat_ref[...] = jnp.concatenate([dql, dkl, dce[:, KV_LORA:]], axis=1).astype(BF16)

    def row(w):
        return pl.BlockSpec((tm, w), lambda i: (i, 0))

    def full(r, w):
        return pl.BlockSpec((r, w), lambda i: (0, 0))

    return pl.pallas_call(
        body, name=name, grid=(t // tm,),
        in_specs=[row(D), row(D), row(D), row(LAT_PAD), row(Q_LORA), row(Q_LORA), full(1, Q_LORA), full(1, KV_LORA),
                  full(1, HEAD_PAD), full(1, HEAD_PAD), full(Q_LORA, D), full(Q_LORA, D), full(KV_LORA, D),
                  row(HEAD_PAD), row(HEAD_PAD)],
        out_specs=[row(LAT_PAD), row(D), row(D), full(1, Q_LORA), full(1, KV_LORA), full(1, HEAD_PAD), full(1, HEAD_PAD)],
        out_shape=[jax.ShapeDtypeStruct((t, LAT_PAD), BF16), jax.ShapeDtypeStruct((t, D), BF16), jax.ShapeDtypeStruct((t, D), BF16),
                   jax.ShapeDtypeStruct((1, Q_LORA), F32), jax.ShapeDtypeStruct((1, KV_LORA), F32),
                   jax.ShapeDtypeStruct((1, HEAD_PAD), F32), jax.ShapeDtypeStruct((1, HEAD_PAD), F32)],
        compiler_params=_params("arbitrary"),
    )(dq, dk, dv, lat, qn, ckv_ext, gq, gkv, ghq, ghk, wq, wk, wv, rc, rs)


def _causal_keep(tq):
    r = lax.broadcasted_iota(jnp.int32, (tq, tq), 0)
    c = lax.broadcasted_iota(jnp.int32, (tq, tq), 1)
    return c <= r


def _flash_fwd(q, k, v, *, n_seq, seq, tq, name, hosted=None):
    nq = seq // tq

    def body(q_ref, k_ref, v_ref, o_ref, lse_ref):
        qi = pl.program_id(2)
        qv = q_ref[...]

        def step(j, carry, masked):
            m, l, acc = carry
            kj = k_ref[pl.ds(pl.multiple_of(j * tq, tq), tq), :]
            vj = v_ref[pl.ds(pl.multiple_of(j * tq, tq), tq), :]
            s = _dot_nt(qv, kj) * ATTN_SCALE
            if masked:
                s = jnp.where(_causal_keep(tq), s, NEG)
            m_new = jnp.maximum(m, jnp.max(s, axis=-1, keepdims=True))
            alpha = jnp.exp(m - m_new)
            p = jnp.exp(s - m_new)
            l = alpha * l + jnp.sum(p, axis=-1, keepdims=True)
            acc = alpha * acc + _dot_nn(p.astype(BF16), vj)
            return m_new, l, acc

        init = (jnp.full((tq, 1), NEG, F32), jnp.zeros((tq, 1), F32), jnp.zeros((tq, HEAD_PAD), F32))
        carry = lax.fori_loop(0, qi, lambda j, cr: step(j, cr, False), init)
        m, l, acc = step(qi, carry, True)
        o_ref[...] = (acc / l).astype(BF16)
        lse_ref[...] = jnp.broadcast_to(m + jnp.log(l), (tq, HEAD_PAD))

    qspec = pl.BlockSpec((tq, HEAD_PAD), lambda b, h, i: (b * nq + i, h))
    kspec = pl.BlockSpec((seq, HEAD_PAD), lambda b, h, i: (b, h))
    t = n_seq * seq
    return _call(
        body, name=name, grid=(n_seq, N_HEADS, nq), in_specs=[qspec, kspec, kspec], out_specs=[qspec, qspec],
        out_shape=[jax.ShapeDtypeStruct((t, D), BF16), jax.ShapeDtypeStruct((t, D), F32)], scratch_shapes=[],
        operands=(q, k, v), sem=("parallel", "parallel", "arbitrary"), hosted=hosted)


def _flash_bwd(q, k, v, o, lse, do, *, n_seq, seq, tq, name, hosted=None):
    nq = seq // tq

    def body(q_ref, k_ref, v_ref, o_ref, lse_ref, do_ref, dq_ref, dk_ref, dv_ref, dk_acc, dv_acc):
        j = pl.program_id(2)

        @pl.when(j == 0)
        def _():
            dq_ref[...] = jnp.zeros_like(dq_ref)

        dk_acc[...] = jnp.zeros_like(dk_acc)
        dv_acc[...] = jnp.zeros_like(dv_acc)
        kv = k_ref[...]
        vv = v_ref[...]

        def step(i, masked):
            rows = pl.ds(pl.multiple_of(i * tq, tq), tq)
            qi = q_ref[rows, :]
            doi = do_ref[rows, :]
            delta = jnp.sum(doi.astype(F32) * o_ref[rows, :].astype(F32), axis=-1, keepdims=True)
            s = _dot_nt(qi, kv) * ATTN_SCALE
            p = jnp.exp(s - lse_ref[rows, :][:, :1])
            if masked:
                p = jnp.where(_causal_keep(tq), p, 0.0)
            dv_acc[...] += _dot_tn(p.astype(BF16), doi)
            dp = _dot_nt(doi, vv)
            ds = (p * (dp - delta) * ATTN_SCALE).astype(BF16)
            dk_acc[...] += _dot_tn(ds, qi)
            dq_ref[rows, :] += _dot_nn(ds, kv)

        step(j, True)

        def loop_body(i, carry):
            step(i, False)
            return carry

        lax.fori_loop(j + 1, nq, loop_body, 0)
        dk_ref[...] = dk_acc[...]
        dv_ref[...] = dv_acc[...].astype(BF16)

    full = pl.BlockSpec((seq, HEAD_PAD), lambda b, h, j: (b, h))
    tile = pl.BlockSpec((tq, HEAD_PAD), lambda b, h, j: (b * nq + j, h))
    t = n_seq * seq
    return _call(
        body, name=name, grid=(n_seq, N_HEADS, nq), in_specs=[full, tile, tile, full, full, full],
        out_specs=[full, tile, tile],
        out_shape=[jax.ShapeDtypeStruct((t, D), F32), jax.ShapeDtypeStruct((t, D), F32), jax.ShapeDtypeStruct((t, D), BF16)],
        scratch_shapes=[pltpu.VMEM((tq, HEAD_PAD), F32), pltpu.VMEM((tq, HEAD_PAD), F32)],
        operands=(q, k, v, o, lse, do), sem=("parallel", "parallel", "arbitrary"), hosted=hosted)


CONV_CB = 256


def _shift_down(u, k, row):
    return jnp.where(row >= k, pltpu.roll(u, k, 0), 0.0)


def _shift_up(u, k, row, n):
    return jnp.where(row < n - k, pltpu.roll(u, n - k, 0), 0.0)


def _conv_fwd(conv3, cw, *, n_seq, seq, name, hosted=None):
    def body(c_ref, w_ref, p_ref):
        blk = c_ref[...].astype(F32)
        xc, gb, gc = blk[:, :CONV_CB], blk[:, CONV_CB:2 * CONV_CB], blk[:, 2 * CONV_CB:]
        row = lax.broadcasted_iota(jnp.int32, (seq, CONV_CB), 0)
        u = gc * xc
        z = w_ref[0:1, :] * _shift_down(u, 2, row) + w_ref[1:2, :] * _shift_down(u, 1, row) + w_ref[2:3, :] * u
        p_ref[...] = (gb * z).astype(BF16)

    (p,), got = _call(
        body, name=name, grid=(n_seq, D // CONV_CB),
        in_specs=[pl.BlockSpec((seq, 3 * CONV_CB), lambda b, j: (b, j)), pl.BlockSpec((3, CONV_CB), lambda b, j: (0, j))],
        out_specs=[pl.BlockSpec((seq, CONV_CB), lambda b, j: (b, j))],
        out_shape=[jax.ShapeDtypeStruct((n_seq * seq, D), BF16)], scratch_shapes=[],
        operands=(conv3, cw), sem=("parallel", "parallel"), hosted=hosted)
    return p, got


def _conv_bwd(dp, conv3, cw, *, n_seq, seq, name):
    def body(dp_ref, c_ref, w_ref, dc_ref, dw_ref):
        @pl.when(pl.program_id(1) == 0)
        def _():
            dw_ref[...] = jnp.zeros_like(dw_ref)

        blk = c_ref[...].astype(F32)
        xc, gb, gc = blk[:, :CONV_CB], blk[:, CONV_CB:2 * CONV_CB], blk[:, 2 * CONV_CB:]
        row = lax.broadcasted_iota(jnp.int32, (seq, CONV_CB), 0)
        w0, w1, w2 = w_ref[0:1, :], w_ref[1:2, :], w_ref[2:3, :]
        u = gc * xc
        u1 = _shift_down(u, 1, row)
        u2 = _shift_down(u, 2, row)
        z = w0 * u2 + w1 * u1 + w2 * u
        dpv = dp_ref[...].astype(F32)
        dz = dpv * gb
        du = w2 * dz + w1 * _shift_up(dz, 1, row, seq) + w0 * _shift_up(dz, 2, row, seq)
        dc_ref[...] = jnp.concatenate([du * gc, dpv * z, du * xc], axis=1).astype(BF16)
        dw_ref[0:1, :] += jnp.sum(dz * u2, axis=0, keepdims=True)
        dw_ref[1:2, :] += jnp.sum(dz * u1, axis=0, keepdims=True)
        dw_ref[2:3, :] += jnp.sum(dz * u, axis=0, keepdims=True)

    return pl.pallas_call(
        body, name=name, grid=(D // CONV_CB, n_seq),
        in_specs=[pl.BlockSpec((seq, CONV_CB), lambda j, b: (b, j)), pl.BlockSpec((seq, 3 * CONV_CB), lambda j, b: (b, j)),
                  pl.BlockSpec((3, CONV_CB), lambda j, b: (0, j))],
        out_specs=[pl.BlockSpec((seq, 3 * CONV_CB), lambda j, b: (b, j)), pl.BlockSpec((3, CONV_CB), lambda j, b: (0, j))],
        out_shape=[jax.ShapeDtypeStruct((n_seq * seq, CONV_COLS), BF16), jax.ShapeDtypeStruct((3, D), F32)],
        compiler_params=_params("parallel", "arbitrary"),
    )(dp, conv3, cw)


def _merge_fwd(o, p, gl, bias, x1, wpa, wpc, wout, *, tm, name, hosted=None):
    t = x1.shape[0]

    def body(o_ref, p_ref, gl_ref, b_ref, x_ref, wpa_ref, wpc_ref, wout_ref, x2_ref, mg_ref, ya_ref, yb_ref):
        ya = _dot_nn(o_ref[...], wpa_ref[...])
        yb = _dot_nn(p_ref[...], wpc_ref[...])
        gates = _sigmoid(gl_ref[...].astype(F32) + b_ref[...])
        merged = (gates[:, :D] * ya + gates[:, D:] * yb).astype(BF16)
        ya_ref[...] = ya.astype(BF16)
        yb_ref[...] = yb.astype(BF16)
        mg_ref[...] = merged
        x2_ref[...] = x_ref[...] + _dot_nn(merged, wout_ref[...])

    row = pl.BlockSpec((tm, D), lambda i: (i, 0))
    row2 = pl.BlockSpec((tm, GATE_COLS), lambda i: (i, 0))
    wsp = pl.BlockSpec((D, D), lambda i: (0, 0))
    wide = jax.ShapeDtypeStruct((t, D), BF16)
    return _call(
        body, name=name, grid=(t // tm,),
        in_specs=[row, row, row2, pl.BlockSpec((1, GATE_COLS), lambda i: (0, 0)), row, wsp, wsp, wsp],
        out_specs=[row, row, row, row], out_shape=[jax.ShapeDtypeStruct((t, D), F32), wide, wide, wide], scratch_shapes=[],
        operands=(o, p, gl, bias, x1, wpa, wpc, wout), sem=("parallel",), hosted=hosted)


def _merge_bwd(dx2, ya, yb, gl, bias, wpa, wpc, wout, *, tm, name, hosted=None):
    t = dx2.shape[0]

    def body(dx_ref, ya_ref, yb_ref, gl_ref, b_ref, wpa_ref, wpc_ref, wout_ref,
             dxb_ref, dya_ref, dyb_ref, dgl_ref, do_ref, dp_ref, db_ref):
        @pl.when(pl.program_id(0) == 0)
        def _():
            db_ref[...] = jnp.zeros_like(db_ref)

        dxb = dx_ref[...].astype(BF16)
        dxb_ref[...] = dxb
        dm = _dot_nt(dxb, wout_ref[...])
        gates = _sigmoid(gl_ref[...].astype(F32) + b_ref[...])
        ga, gb = gates[:, :D], gates[:, D:]
        dya = (dm * ga).astype(BF16)
        dyb = (dm * gb).astype(BF16)
        dya_ref[...] = dya
        dyb_ref[...] = dyb
        dgl = jnp.concatenate([dm * ya_ref[...].astype(F32) * ga * (1.0 - ga),
                               dm * yb_ref[...].astype(F32) * gb * (1.0 - gb)], axis=1)
        dgl_ref[...] = dgl.astype(BF16)
        db_ref[...] += jnp.sum(dgl, axis=0, keepdims=True)
        do_ref[...] = _dot_nt(dya, wpa_ref[...]).astype(BF16)
        dp_ref[...] = _dot_nt(dyb, wpc_ref[...]).astype(BF16)

    row = pl.BlockSpec((tm, D), lambda i: (i, 0))
    row2 = pl.BlockSpec((tm, GATE_COLS), lambda i: (i, 0))
    vec2 = pl.BlockSpec((1, GATE_COLS), lambda i: (0, 0))
    wsp = pl.BlockSpec((D, D), lambda i: (0, 0))
    wide = jax.ShapeDtypeStruct((t, D), BF16)
    return _call(
        body, name=name, grid=(t // tm,), in_specs=[row, row, row, row2, vec2, wsp, wsp, wsp],
        out_specs=[row, row, row, row2, row, row, vec2],
        out_shape=[wide, wide, wide, jax.ShapeDtypeStruct((t, GATE_COLS), BF16), wide, wide,
                   jax.ShapeDtypeStruct((1, GATE_COLS), F32)],
        scratch_shapes=[], operands=(dx2, ya, yb, gl, bias, wpa, wpc, wout), sem=("arbitrary",), hosted=hosted)


def _adamw(w, g, m, v, *, name):
    rows, cols = w.shape
    tr = max([c for c in range(8, 513, 8) if rows % c == 0], default=rows)
    c1 = 1.0 / (1.0 - ADAM_B1 ** ADAM_STEP)
    c2 = 1.0 / (1.0 - ADAM_B2 ** ADAM_STEP)

    def body(w_ref, g_ref, m_ref, v_ref, d_ref, nm_ref, nv_ref):
        gv = g_ref[...]
        nm = ADAM_B1 * m_ref[...] + (1.0 - ADAM_B1) * gv
        nv = ADAM_B2 * v_ref[...] + (1.0 - ADAM_B2) * (gv * gv)
        nm_ref[...] = nm
        nv_ref[...] = nv
        d_ref[...] = -ADAM_LR * ((nm * c1) / (jnp.sqrt(nv * c2) + ADAM_EPS) + ADAM_WD * w_ref[...])

    spec = pl.BlockSpec((tr, cols), lambda i: (i, 0))
    shp = jax.ShapeDtypeStruct((rows, cols), F32)
    return pl.pallas_call(
        body, name=name, grid=(rows // tr,), in_specs=[spec] * 4, out_specs=[spec] * 3, out_shape=[shp] * 3,
        compiler_params=_params("parallel"),
    )(w, g, m, v)


def _place():
    return lax.axis_index("x"), lax.axis_index("y"), lax.axis_index("c")


def _other_chips(x, y):
    return [(1 - x, y), (x, 1 - y), (1 - x, 1 - y)]


def _remote(src, dst, send, recv, dev):
    return pltpu.make_async_remote_copy(src_ref=src, dst_ref=dst, send_sem=send, recv_sem=recv, device_id=dev, device_id_type=MESH)


def _gather_chips_plan(n):
    def start(srcs, dsts, send, recv, local):
        x, y, cc = _place()
        me = 4 * x + 2 * y + cc
        for a in range(n):
            pltpu.make_async_copy(srcs[a], dsts[a].at[me], local.at[a]).start()
            for k, (px, py) in enumerate(_other_chips(x, y)):
                _remote(srcs[a], dsts[a].at[me], send.at[3 * a + k], recv.at[3 * a + k], (px, py, cc)).start()

    def wait(srcs, dsts, send, recv, local):
        x, y, cc = _place()
        me = 4 * x + 2 * y + cc
        for a in range(n):
            for k, (px, py) in enumerate(_other_chips(x, y)):
                _remote(srcs[a], dsts[a].at[4 * px + 2 * py + cc], send.at[3 * a + k], recv.at[3 * a + k], (px, py, cc)).wait_recv()
        for a in range(n):
            for k, (px, py) in enumerate(_other_chips(x, y)):
                _remote(srcs[a], dsts[a].at[me], send.at[3 * a + k], recv.at[3 * a + k], (px, py, cc)).wait_send()
            pltpu.make_async_copy(srcs[a], dsts[a].at[me], local.at[a]).wait()

    return _Plan(start, wait, 3 * n, n)


def _scatter_chips_plan(n):
    def start(srcs, dsts, send, recv, local):
        x, y, cc = _place()
        for a in range(n):
            for k, (px, py) in enumerate(_other_chips(x, y)):
                _remote(srcs[a].at[2 * px + py], dsts[a].at[k], send.at[3 * a + k], recv.at[3 * a + k], (px, py, cc)).start()

    def wait(srcs, dsts, send, recv, local):
        x, y, cc = _place()
        for a in range(n):
            for k, (px, py) in enumerate(_other_chips(x, y)):
                _remote(srcs[a].at[k], dsts[a].at[k], send.at[3 * a + k], recv.at[3 * a + k], (px, py, cc)).wait_recv()
        for a in range(n):
            for k, (px, py) in enumerate(_other_chips(x, y)):
                _remote(srcs[a].at[k], dsts[a].at[k], send.at[3 * a + k], recv.at[3 * a + k], (px, py, cc)).wait_send()

    return _Plan(start, wait, 3 * n, 0)


def _gather_shapes(blocks):
    return [jax.ShapeDtypeStruct((N_DEV,) + b.shape, b.dtype) for b in blocks]


def _scatter_shapes(parts):
    return [jax.ShapeDtypeStruct((3,) + p.shape[1:], p.dtype) for p in parts]


def _gather_sibling_plan(n):
    def start(srcs, dsts, send, recv, local):
        x, y, cc = _place()
        for a in range(n):
            for q in range(4):
                _remote(srcs[a].at[2 * q + cc], dsts[a].at[2 * q + cc], send.at[4 * a + q], recv.at[4 * a + q], (x, y, 1 - cc)).start()

    def wait(srcs, dsts, send, recv, local):
        x, y, cc = _place()
        for a in range(n):
            for q in range(4):
                _remote(srcs[a].at[2 * q + cc], dsts[a].at[2 * q + 1 - cc], send.at[4 * a + q], recv.at[4 * a + q],
                        (x, y, 1 - cc)).wait_recv()
        for a in range(n):
            for q in range(4):
                _remote(srcs[a].at[2 * q + cc], dsts[a].at[2 * q + cc], send.at[4 * a + q], recv.at[4 * a + q],
                        (x, y, 1 - cc)).wait_send()

    return _Plan(start, wait, 4 * n, 0, in_place=True)


def _scatter_sibling_plan(n):
    def start(srcs, dsts, send, recv, local):
        x, y, cc = _place()
        for a in range(n):
            for q in range(4):
                _remote(srcs[a].at[2 * q + 1 - cc], dsts[a].at[q], send.at[4 * a + q], recv.at[4 * a + q], (x, y, 1 - cc)).start()

    def wait(srcs, dsts, send, recv, local):
        x, y, cc = _place()
        for a in range(n):
            for q in range(4):
                _remote(srcs[a].at[q], dsts[a].at[q], send.at[4 * a + q], recv.at[4 * a + q], (x, y, 1 - cc)).wait_recv()
        for a in range(n):
            for q in range(4):
                _remote(srcs[a].at[q], dsts[a].at[q], send.at[4 * a + q], recv.at[4 * a + q], (x, y, 1 - cc)).wait_send()

    return _Plan(start, wait, 4 * n, 0)


def _same_shapes(arrs):
    return [jax.ShapeDtypeStruct(a.shape, a.dtype) for a in arrs]


def _halved_shapes(parts):
    return [jax.ShapeDtypeStruct((4,) + p.shape[1:], p.dtype) for p in parts]


def _run_plan(plan, srcs, out_shapes, *, name):
    n_in, n_out = len(srcs), len(out_shapes)

    def body(*refs):
        h_in, h_out, sems = refs[:n_in], refs[n_in:n_in + n_out], refs[n_in + n_out:]
        plan.start(h_in, h_out, *sems)
        plan.wait(h_in, h_out, *sems)

    return pl.pallas_call(body, name=name, in_specs=[ANY] * n_in, out_specs=[ANY] * n_out, out_shape=list(out_shapes),
                          input_output_aliases={a: a for a in range(n_in)} if plan.in_place else {},
                          scratch_shapes=plan.sems())(*srcs)


SEM = pl.BlockSpec(memory_space=pltpu.SEMAPHORE)
HBM = pl.BlockSpec(memory_space=pltpu.HBM)
SIDE_EFFECT = pltpu.CompilerParams(has_side_effects=pltpu.SideEffectType.DATAFLOW_SIDE_EFFECTING)


def _plan_start(plan, blocks, land_shapes, *, name):
    n = len(blocks)
    lands = [lax.empty(s.shape, s.dtype) for s in land_shapes]

    def body(*refs):
        srcs, sems, lands_out, token = refs[:n], refs[2 * n:2 * n + 3], refs[3 * n + 3:4 * n + 3], refs[4 * n + 3]
        plan.start(srcs, lands_out, *sems)
        token[...] = jnp.zeros_like(token)

    out_shape = ([s for s in plan.sems()] + [pltpu.HBM(b.shape, b.dtype) for b in blocks]
                 + [pltpu.HBM(l.shape, l.dtype) for l in lands] + [jax.ShapeDtypeStruct((8, 128), F32)])
    res = pl.pallas_call(
        body, name=name, in_specs=[HBM] * (2 * n), out_specs=[SEM] * 3 + [HBM] * (2 * n) + [pl.BlockSpec(memory_space=pltpu.VMEM)],
        out_shape=out_shape, input_output_aliases={a: 3 + a for a in range(2 * n)}, compiler_params=SIDE_EFFECT,
    )(*[pltpu.with_memory_space_constraint(a, pltpu.HBM) for a in list(blocks) + lands])
    return res[:3], res[3:3 + n], res[3 + n:3 + 2 * n], res[3 + 2 * n]


def _plan_wait(plan, sems, blocks, lands, after, *, name):
    n = len(blocks)

    def body(*refs):
        plan.wait(refs[:n], refs[n:2 * n], *refs[2 * n:2 * n + 3])

    res = pl.pallas_call(
        body, name=name, in_specs=[HBM] * (2 * n) + [SEM] * 3 + [ANY], out_specs=[HBM] * (2 * n),
        out_shape=[pltpu.HBM(a.shape, a.dtype) for a in list(blocks) + list(lands)],
        input_output_aliases={a: a for a in range(2 * n)}, compiler_params=SIDE_EFFECT,
    )(*blocks, *lands, *sems, after)
    return list(res[:n]), list(res[n:])


def _sum_sibling(ps, qs, core, *, name):
    n = len(ps)

    def body(core_ref, *refs):
        for p_ref, q_ref, o_ref in zip(refs[:n], refs[n:2 * n], refs[2 * n:]):
            o_ref[...] = (p_ref[...].astype(F32) + q_ref[...].astype(F32)).astype(BF16)

    def mine(p):
        return pl.BlockSpec((1,) + p.shape[1:], lambda ch, core_ref: (2 * ch + core_ref[0], 0, 0))

    def theirs(p):
        return pl.BlockSpec((1,) + p.shape[1:], lambda ch, core_ref: (ch, 0, 0))

    grid_spec = pltpu.PrefetchScalarGridSpec(
        num_scalar_prefetch=1, grid=(4,), in_specs=[mine(p) for p in ps] + [theirs(p) for p in ps], out_specs=[theirs(p) for p in ps])
    return pl.pallas_call(
        body, name=name, grid_spec=grid_spec, out_shape=[jax.ShapeDtypeStruct((4,) + p.shape[1:], BF16) for p in ps],
        compiler_params=_params("parallel"),
    )(core, *ps, *qs)


def _sum_chips(s1, r2, chip, *, name):
    _, r, c = s1.shape

    def body(chip_ref, s_ref, r_ref, o_ref):
        acc = s_ref[0].astype(F32)
        for k in range(3):
            acc = acc + r_ref[k].astype(F32)
        o_ref[...] = acc

    grid_spec = pltpu.PrefetchScalarGridSpec(
        num_scalar_prefetch=1, grid=(1,),
        in_specs=[pl.BlockSpec((1, r, c), lambda i, chip_ref: (chip_ref[0], 0, 0)),
                  pl.BlockSpec((3, r, c), lambda i, chip_ref: (0, 0, 0))],
        out_specs=pl.BlockSpec((r, c), lambda i, chip_ref: (0, 0)))
    return pl.pallas_call(
        body, name=name, grid_spec=grid_spec, out_shape=jax.ShapeDtypeStruct((r, c), F32),
        compiler_params=_params("arbitrary"),
    )(chip, s1, r2)


def _sum_adamw(s1, r2, chip, w, m, v, *, name):
    _, r, c = s1.shape
    c1 = 1.0 / (1.0 - ADAM_B1 ** ADAM_STEP)
    c2 = 1.0 / (1.0 - ADAM_B2 ** ADAM_STEP)

    def body(chip_ref, s_ref, r_ref, w_ref, m_ref, v_ref, g_ref, d_ref, nm_ref, nv_ref):
        gv = s_ref[0].astype(F32)
        for k in range(3):
            gv = gv + r_ref[k].astype(F32)
        g_ref[...] = gv
        nm = ADAM_B1 * m_ref[...] + (1.0 - ADAM_B1) * gv
        nv = ADAM_B2 * v_ref[...] + (1.0 - ADAM_B2) * (gv * gv)
        nm_ref[...] = nm
        nv_ref[...] = nv
        d_ref[...] = -ADAM_LR * ((nm * c1) / (jnp.sqrt(nv * c2) + ADAM_EPS) + ADAM_WD * w_ref[...])

    flat = pl.BlockSpec((r, c), lambda i, chip_ref: (0, 0))
    grid_spec = pltpu.PrefetchScalarGridSpec(
        num_scalar_prefetch=1, grid=(1,),
        in_specs=[pl.BlockSpec((1, r, c), lambda i, chip_ref: (chip_ref[0], 0, 0)),
                  pl.BlockSpec((3, r, c), lambda i, chip_ref: (0, 0, 0)), flat, flat, flat],
        out_specs=[flat] * 4)
    return pl.pallas_call(
        body, name=name, grid_spec=grid_spec, out_shape=[jax.ShapeDtypeStruct((r, c), F32)] * 4,
        compiler_params=_params("arbitrary"),
    )(chip, s1, r2, w, m, v)


def _small_exchange(v, *, reduce, name):
    r, c = v.shape

    def body(x_ref, o_ref, *rest):
        if reduce:
            buf_ref, send_sems, recv_sems = rest
        else:
            buf_ref = o_ref
            send_sems, recv_sems = rest
        x, y, cc = _place()
        me = 4 * x + 2 * y + cc

        def peer(k):
            return ((1 - x) if k & 4 else x, (1 - y) if k & 2 else y, (1 - cc) if k & 1 else cc)

        buf_ref[me] = x_ref[...]
        sends = []
        for k in range(1, N_DEV):
            cp = pltpu.make_async_remote_copy(src_ref=x_ref, dst_ref=buf_ref.at[me], send_sem=send_sems.at[k - 1],
                                              recv_sem=recv_sems.at[k - 1], device_id=peer(k), device_id_type=MESH)
            cp.start()
            sends.append(cp)
        for k in range(1, N_DEV):
            px, py, pc = peer(k)
            pltpu.make_async_remote_copy(src_ref=x_ref, dst_ref=buf_ref.at[4 * px + 2 * py + pc], send_sem=send_sems.at[k - 1],
                                         recv_sem=recv_sems.at[k - 1], device_id=peer(k), device_id_type=MESH).wait_recv()
        for cp in sends:
            cp.wait_send()
        if reduce:
            acc = buf_ref[0]
            for s in range(1, N_DEV):
                acc = acc + buf_ref[s]
            o_ref[...] = acc

    vm = pl.BlockSpec(memory_space=pltpu.VMEM)
    sems = [pltpu.SemaphoreType.DMA((N_DEV - 1,)), pltpu.SemaphoreType.DMA((N_DEV - 1,))]
    if reduce:
        out_shape, scratch = jax.ShapeDtypeStruct((r, c), F32), [pltpu.VMEM((N_DEV, r, c), F32)] + sems
    else:
        out_shape, scratch = jax.ShapeDtypeStruct((N_DEV, r, c), F32), sems
    return pl.pallas_call(body, name=name, in_specs=[vm], out_specs=vm, out_shape=out_shape, scratch_shapes=scratch)(v)


def _rows(a):
    return a.reshape(-1, D)


def _pad_cols(a, to):
    return jnp.pad(a, ((0, 0), (0, to - a.shape[1])))


def _pack_weights(w):
    parts = {
        "w_inT": jnp.pad(w["w_in"].T, ((0, IN_SHARD_PAD - IN_SHARD), (0, 0))),
        "w_uq": _rows(_head_cols(w["w_uq"])), "w_uk": _rows(_head_cols(w["w_uk"])),
        "w_uv": _rows(_pad_cols(w["w_uv"], HEAD_PAD)), "w_pa": _rows(w["w_proj_attn"]),
        "w_pc": w["w_proj_conv"], "w_out": w["w_out"],
    }
    return [jnp.concatenate([parts[n].astype(BF16) for n, _ in group], axis=0) for group in PACK]


def _cols_from_shards(gs, name, rows):
    idx, off, r = PACK_OFF[name]
    return gs[idx][:, off:off + r].reshape(N_DEV, rows, HEAD_PAD).transpose(1, 0, 2).reshape(rows, N_DEV * HEAD_PAD)


def _rows_from_shards(gs, name, keep=None):
    idx, off, r = PACK_OFF[name]
    keep = r if keep is None else keep
    return gs[idx][:, off:off + keep].reshape(N_DEV * keep, D)


def _rope_placement():
    i = lax.broadcasted_iota(jnp.int32, (HEAD_PAD, D), 0)
    j = lax.broadcasted_iota(jnp.int32, (HEAD_PAD, D), 1)
    lane = jnp.where(i < ROPE_HALF, 32 + i, 96 + i - ROPE_HALF)
    return ((i < 2 * ROPE_HALF) & (j % HEAD_PAD == lane)).astype(BF16)


def _unpack_in(g_in):
    w_inT = _rows_from_shards([g_in, None], "w_inT", IN_SHARD)
    lat_rows = Q_LORA + KV_LORA + 2 * ROPE_HALF
    conv = w_inT[lat_rows:lat_rows + CONV_COLS].reshape(3, D // CONV_CB, CONV_CB, D).transpose(1, 0, 2, 3).reshape(CONV_COLS, D)
    return {"latT": jnp.pad(w_inT[:lat_rows], ((0, LAT_PAD - lat_rows), (0, 0))), "convT": conv,
            "gateT": w_inT[lat_rows + CONV_COLS:]}


def _unpack_misc(g_misc):
    g = [None, g_misc]
    wpa = _cols_from_shards(g, "w_pa", 512).reshape(N_HEADS, NOPE, D)
    return {
        "wq": _cols_from_shards(g, "w_uq", Q_LORA),
        "wk": jnp.concatenate([_cols_from_shards(g, "w_uk", KV_LORA), _rope_placement()], axis=0),
        "wv": _cols_from_shards(g, "w_uv", KV_LORA),
        "wpa": jnp.pad(wpa, ((0, 0), (0, HEAD_PAD - NOPE), (0, 0))).reshape(D, D),
        "wpc": _rows_from_shards(g, "w_pc"), "wout": _rows_from_shards(g, "w_out"),
    }


def _shards_from_cols(a):
    rows = a.shape[0]
    return a.reshape(rows, N_DEV, HEAD_PAD).transpose(1, 0, 2).reshape(N_DEV, rows * HEAD_PAD // D, D)


def _pack_grads(gw):
    lat_rows = Q_LORA + KV_LORA + 2 * ROPE_HALF
    conv = gw["convT"].reshape(D // CONV_CB, 3, CONV_CB, D).transpose(1, 0, 2, 3).reshape(CONV_COLS, D)
    w_inT = jnp.concatenate([gw["latT"][:lat_rows], conv, gw["gateT"]], axis=0).reshape(N_DEV, IN_SHARD, D)
    wpa = gw["wpa"].reshape(N_HEADS, HEAD_PAD, D)[:, :NOPE].reshape(N_HEADS * NOPE, D)
    parts = {}
    parts.update({
        "w_inT": jnp.pad(w_inT, ((0, 0), (0, IN_SHARD_PAD - IN_SHARD), (0, 0))),
        "w_uq": _shards_from_cols(gw["wq"]), "w_uk": _shards_from_cols(gw["wk"][:KV_LORA]),
        "w_uv": _shards_from_cols(gw["wv"][:KV_LORA]), "w_pa": _shards_from_cols(wpa),
        "w_pc": gw["wpc"].reshape(N_DEV, D // N_DEV, D), "w_out": gw["wout"].reshape(N_DEV, D // N_DEV, D),
    })
    return [jnp.concatenate([parts[n] for n, _ in group], axis=1) for group in PACK]


def _unpack_grads(mines):
    def seg(name, keep=None):
        idx, off, r = PACK_OFF[name]
        return mines[idx][off:off + (r if keep is None else keep)]

    return {
        "w_in": seg("w_inT", IN_SHARD).T,
        "w_uq": _head_cols_inv(seg("w_uq").reshape(Q_LORA, HEAD_PAD), QK_DIM),
        "w_uk": _head_cols_inv(seg("w_uk").reshape(KV_LORA, HEAD_PAD), NOPE),
        "w_uv": seg("w_uv").reshape(KV_LORA, HEAD_PAD)[:, :NOPE],
        "w_proj_attn": seg("w_pa").reshape(512, HEAD_PAD),
        "w_proj_conv": seg("w_pc"), "w_out": seg("w_out"),
    }


def _rope_tables(positions):
    lane = jnp.arange(HEAD_PAD)
    idx = jnp.where((lane >= 32) & (lane < 48), lane - 32, jnp.where((lane >= 96) & (lane < 112), lane - 96, -1))
    inv_freq = jnp.where(idx >= 0, 1.0 / (ROPE_THETA ** (idx.astype(F32) / ROPE_HALF)), 0.0)
    ang = positions.reshape(-1).astype(F32)[:, None] * inv_freq
    return jnp.cos(ang), jnp.sin(ang) * jnp.where(lane < HEAD_PAD // 2, -1.0, 1.0)


def _local_step(x, positions, target, conv_w, small, ex):
    n_seq, seq, d = x.shape
    t = n_seq * seq
    x0 = x.reshape(t, d)
    tgt = target.reshape(t, d)
    rc, rs = _rope_tables(positions)
    ghq = _head_cols(small["q_head_norm"])
    ghk = _head_cols(small["k_head_norm"])
    TM, HC, TQ = 1024, 256, 1024

    def mm(*args, hosted=None, **kw):
        res = _mm(*args, hosted=hosted, **kw)
        return res if hosted is not None else (res, None)

    def wgrad(a, b, name, tm=None, hosted=None):
        tm = tm or a.shape[1]
        return mm(a, b, mode="tn", out_dtype=BF16, tm=tm, tn=b.shape[1], tk=2048 if tm <= D else 1024, name=name, hosted=hosted)

    f1g, f1u, f1d = ex.gather_finish(ex.witness() + rc[:8] + conv_w[:1, :HEAD_PAD])
    (x1, h1, a1, b1), got = _ffn_fwd(x0, small["ffn1_norm"], f1g, f1u, f1d, tm=512, hc=DFF // 2, name="ffn1_fwd",
                                     hosted=ex.gather_chips("mix_in"))
    hm, got = _rms_fwd(x1, small["mix_norm"], tm=TM, name="mix_norm_fwd", hosted=ex.gather_sibling(got))
    W = ex.mix_in_weights(got)
    (lat, conv3, gl), got = _proj_fwd(hm, W["latT"], W["convT"], W["gateT"], tm=512, name="proj_fwd",
                                      hosted=ex.gather_chips("mix_misc"))
    p, got = _conv_fwd(conv3, conv_w, n_seq=n_seq, seq=seq, name="conv_fwd", hosted=ex.gather_sibling(got))
    W.update(ex.mix_misc_weights(got))
    q, k, v, qn, ckv = _mla_prep_fwd(lat, small["q_a_norm"], small["kv_a_norm"], ghq, ghk, W["wq"], W["wk"], W["wv"], rc, rs,
                                     tm=512, name="mla_prep_fwd")
    (o, lse), got = _flash_fwd(q, k, v, n_seq=n_seq, seq=seq, tq=TQ, name="attn_fwd", hosted=ex.gather_chips("ffn2"))
    (x2, merged, ya, yb), got = _merge_fwd(o, p, gl, small["gate_bias"], x1, W["wpa"], W["wpc"], W["wout"], tm=512, name="merge_fwd",
                                           hosted=ex.gather_sibling(got))
    f2g, f2u, f2d = ex.ffn_weights(got)
    (dy, h2, a2, b2, loss_row), _ = _ffn_fwd(x2, small["ffn2_norm"], f2g, f2u, f2d, tm=512, hc=DFF // 2, name="ffn2_fwd", target=tgt)

    gw, gs = {}, {}
    (da2, db2, *ffn2_grads), _ = _ffn_grads(dy, h2, a2, b2, f2d, tm=TM, hc=HC, name="ffn2_grads")
    (dx2, gs["ffn2_norm"]), _ = _ffn_up_bwd(da2, db2, f2g, f2u, x2, small["ffn2_norm"], dy, tm=512, name="ffn2_up_bwd")

    (dx2b, dya, dyb, dgl, do, dp, gs["gate_bias"]), got = _merge_bwd(
        dx2, ya, yb, gl, small["gate_bias"], W["wpa"], W["wpc"], W["wout"], tm=512, name="merge_bwd",
        hosted=ex.scatter_sibling("ffn2", ffn2_grads))
    ex.scatter_sibling_done("ffn2", got)
    gw["wout"] = wgrad(merged, dx2b, "dw_out")[0]
    gw["wpa"] = wgrad(o, dya, "dw_pa")[0]
    gw["wpc"] = wgrad(p, dyb, "dw_pc")[0]
    dconv3, dconv_w = _conv_bwd(dp, conv3, conv_w, n_seq=n_seq, seq=seq, name="conv_bwd")
    (dq, dk, dv), got = _flash_bwd(q, k, v, o, lse, do, n_seq=n_seq, seq=seq, tq=TQ, name="attn_bwd",
                                   hosted=ex.scatter_chips("ffn2"))
    ex.scatter_chips_done("ffn2", got)
    dlat, dqp, dkp, gs["q_a_norm"], gs["kv_a_norm"], dghq, dghk = _mla_prep_bwd(
        dq, dk, dv, lat, qn, ckv, small["q_a_norm"], small["kv_a_norm"], ghq, ghk, W["wq"], W["wk"], W["wv"], rc, rs,
        tm=512, name="mla_prep_bwd")
    gs["q_head_norm"], gs["k_head_norm"] = _head_cols_inv(dghq, QK_DIM), _head_cols_inv(dghk, QK_DIM)
    gw["wq"] = wgrad(qn, dqp, "dw_uq")[0]
    gw["wk"] = wgrad(ckv, dkp, "dw_uk")[0]
    gw["wv"] = wgrad(ckv, dv, "dw_uv")[0]
    gw["convT"] = wgrad(dconv3, hm, "dw_conv", tm=CONV_COLS // 2)[0]
    gw["gateT"] = wgrad(dgl, hm, "dw_gate")[0]
    gw["latT"] = wgrad(dlat, hm, "dw_lat")[0]
    ex.scatter_sibling_now("mix", gw)
    zero = ex.scatter_chips_start("mix_in")
    (dx1, gs["mix_norm"]), _ = _proj_bwd(dlat, dconv3, dgl, W["latT"], W["convT"], W["gateT"], x1, small["mix_norm"] + zero, dx2,
                                         tm=512, name="proj_bwd")

    (da1, db1, *ffn1_grads), got = _ffn_grads(dx1, h1, a1, b1, f1d, tm=TM, hc=HC, name="ffn1_grads",
                                              hosted=ex.scatter_chips("mix_misc"))
    ex.scatter_chips_done("mix_misc", got)
    ex.reduce_small(gs, dconv_w, loss_row)
    ex.scatter_sibling_now("ffn1", ffn1_grads)
    zero = ex.scatter_chips_start("ffn1")
    (dx0, gs["ffn1_norm"]), _ = _ffn_up_bwd(da1, db1, f1g, f1u, x0, small["ffn1_norm"] + zero, dx1, tm=512, name="ffn1_up_bwd")
    return dx0.reshape(n_seq, seq, d), gs["ffn1_norm"]


class _MeshExchange:
    def __init__(self, w, core, chip):
        self.w, self.core, self.chip = w, core, chip
        self.partial, self.received, self._cache, self._scattering = {}, {}, {}, {}

    def _blocks(self, group):
        w = self.w
        if group not in self._cache:
            if group.startswith("ffn"):
                self._cache[group] = [w[group + "_w_gate"].T.astype(BF16), w[group + "_w_up"].T.astype(BF16),
                                      w[group + "_w_down"].astype(BF16)]
            else:
                self._cache["mix_in"], self._cache["mix_misc"] = [[b] for b in _pack_weights(w)]
        return self._cache[group]

    def gather_chips(self, *groups):
        blocks = [b for group in groups for b in self._blocks(group)]
        return _gather_chips_plan(len(blocks)), blocks, _gather_shapes(blocks)

    def gather_sibling(self, got):
        half = list(got)
        return _gather_sibling_plan(len(half)), half, _same_shapes(half)

    def gather_start(self, group):
        blocks = self._blocks(group)
        plan = _gather_chips_plan(len(blocks))
        sems, blocks, lands, token = _plan_start(plan, blocks, _gather_shapes(blocks), name="gather_%s_start" % group)
        self._gathering = (group, plan, sems, blocks, lands)
        return token[0, 0]

    def gather_finish(self, after):
        group, plan, sems, blocks, lands = self._gathering
        _, half = _plan_wait(plan, sems, blocks, lands, after, name="gather_%s_wait" % group)
        return self.ffn_weights(_run_plan(_gather_sibling_plan(len(half)), half, _same_shapes(half), name="gather_%s_sibling" % group))

    def reduce_small(self, gs, dconv_w, loss_row):
        pieces = [_pad_cols(gs[n], SMALL_SLOTS[n]) for n in SMALL_NAMES[1:]] + [dconv_w.reshape(1, 3 * D), loss_row]
        self.small_total = _small_exchange(jnp.concatenate(pieces, axis=1).reshape(-1, 128), reduce=True,
                                           name="reduce_small").reshape(-1)

    def scatter_chips_start(self, group):
        s1 = self.partial[group]
        plan = _scatter_chips_plan(len(s1))
        sems, s1, lands, token = _plan_start(plan, s1, _scatter_shapes(s1), name="scatter_%s_start" % group)
        self._scattering[group] = (plan, sems, s1, lands)
        return token[0, 0]

    def scatter_chips_finish(self, group, after):
        plan, sems, s1, lands = self._scattering[group]
        self.partial[group], self.received[group] = _plan_wait(plan, sems, s1, lands, after, name="scatter_%s_wait" % group)

    def witness(self):
        parts = [b[:8, :128].astype(F32) for g in ("mix_in", "mix_misc", "ffn2") for b in self._blocks(g)]
        return functools.reduce(jnp.add, parts)

    def ffn_weights(self, got):
        return [a.reshape(DFF, D) for a in got]

    def mix_in_weights(self, got):
        return _unpack_in(got[0])

    def mix_misc_weights(self, got):
        return _unpack_misc(got[0])

    def _parts(self, group, grads):
        if group == "mix":
            return _pack_grads(grads), ["mix_in", "mix_misc"]
        parts = [g.reshape(N_DEV, -1, D) for g in grads]
        return parts, ([group] if len(parts) == 1 else None)

    def scatter_sibling(self, group, grads):
        self._sent, self._names = self._parts(group, grads)
        return _scatter_sibling_plan(len(self._sent)), self._sent, _halved_shapes(self._sent)

    def scatter_sibling_done(self, group, got):
        sums = list(_sum_sibling(self._sent, list(got), self.core, name="sum_%s_sibling" % group))
        if self._names is None:
            self.partial[group] = sums
        else:
            for n, s in zip(self._names, sums):
                self.partial[n] = [s]

    def scatter_sibling_now(self, group, grads):
        plan, parts, shapes = self.scatter_sibling(group, grads)
        self.scatter_sibling_done(group, _run_plan(plan, parts, shapes, name="scatter_%s_sibling" % group))

    def scatter_chips(self, group):
        s1 = self.partial[group]
        return _scatter_chips_plan(len(s1)), s1, _scatter_shapes(s1)

    def scatter_chips_done(self, group, got):
        self.received[group] = list(got)


SMALL_NAMES = ("ffn1_norm", "mix_norm", "gate_bias", "q_a_norm", "kv_a_norm", "q_head_norm", "k_head_norm", "ffn2_norm")
SMALL_SLOTS = {"ffn1_norm": 1024, "mix_norm": 1024, "gate_bias": 2048, "q_a_norm": 384, "kv_a_norm": 256, "q_head_norm": 128,
               "k_head_norm": 128, "ffn2_norm": 1024, "conv_w": 3072, "loss": 128}
COLUMN_MAJOR = ("w_in", "w_uq", "w_uk", "w_uv")
WEIGHT_NAMES = ("ffn1_norm", "ffn1_w_gate", "ffn1_w_up", "ffn1_w_down", "mix_norm", "w_in", "gate_bias", "q_a_norm", "w_uq",
                "kv_a_norm", "w_uk", "w_uv", "q_head_norm", "k_head_norm", "w_proj_attn", "conv_w", "w_proj_conv", "w_out",
                "ffn2_norm", "ffn2_w_gate", "ffn2_w_up", "ffn2_w_down")


def _step(x, positions, loss_target, w, m, v):
    xi, yi, ci = _place()
    core = ci.astype(jnp.int32).reshape(1)
    chip = (2 * xi + yi).astype(jnp.int32).reshape(1)
    me = 4 * xi + 2 * yi + ci

    ex = _MeshExchange(w, core, chip)
    cw_all = _small_exchange(jnp.pad(w["conv_w"], ((0, 5), (0, 0))), reduce=False, name="gather_conv_w")
    conv_w = cw_all[:, :3].transpose(1, 0, 2).reshape(3, D)
    ex.w = {n: (a + cw_all[0, 7, 0] if n.startswith("ffn1") else a) for n, a in w.items()}
    zero = ex.gather_start("ffn1")
    ex.w = {n: (a if n.startswith("ffn1") else a + zero) for n, a in w.items()}
    small = {n: w[n].reshape(1, -1) for n in SMALL_NAMES}

    grad_x, dffn1_norm = _local_step(x, positions + zero.astype(jnp.int32), loss_target, conv_w, small, ex)

    grads, deltas, new_m, new_v = {}, {}, {}, {}

    def ffn_update(group):
        for i, n in enumerate((group + "_w_gate", group + "_w_up", group + "_w_down")):
            transposed = not n.endswith("down")
            wv, mv, vv = (a[n].T if transposed else a[n] for a in (w, m, v))
            res = _sum_adamw(ex.partial[group][i], ex.received[group][i], chip, wv, mv, vv, name="adamw_" + n)
            grads[n], deltas[n], new_m[n], new_v[n] = (r.T if transposed else r for r in res)

    def update(n):
        shape = w[n].shape
        if n in COLUMN_MAJOR:
            ops = [a.T for a in (w[n], grads[n], m[n], v[n])]
            deltas[n], new_m[n], new_v[n] = (r.T for r in _adamw(*ops, name="adamw_" + n))
            return
        view = shape if len(shape) == 2 else ((-1, 128) if shape[0] % 128 == 0 else (1, shape[0]))
        dlt, nm, nv = _adamw(w[n].reshape(view), grads[n].reshape(view), m[n].reshape(view), v[n].reshape(view), name="adamw_" + n)
        deltas[n], new_m[n], new_v[n] = dlt.reshape(shape), nm.reshape(shape), nv.reshape(shape)

    ffn_update("ffn2")
    ex.scatter_chips_finish("mix_in", dffn1_norm)
    grads.update(_unpack_grads([_sum_chips(ex.partial[g][0], ex.received[g][0], chip, name="sum_%s_chips" % g)
                                for g in ("mix_in", "mix_misc")]))
    total, off = ex.small_total, 0
    for n in SMALL_NAMES[1:]:
        grads[n] = total[off:off + w[n].shape[0]]
        off += SMALL_SLOTS[n]
    conv_full = total[off:off + 3 * D].reshape(3, D)
    grads["conv_w"] = lax.dynamic_slice(conv_full, (0, me * HEAD_PAD), (3, HEAD_PAD))
    loss = total[off + 3 * D]
    later = ("ffn1_norm", "ffn1_w_gate", "ffn1_w_up", "ffn1_w_down")
    for n in WEIGHT_NAMES:
        if n not in deltas and n not in later:
            update(n)

    done = [deltas[n][:8, :128] for n in ("ffn2_w_down", "w_in", "w_out", "w_proj_attn")] + [deltas["mix_norm"].reshape(8, 128)]
    ex.scatter_chips_finish("ffn1", functools.reduce(jnp.add, done) + grad_x.reshape(-1, D)[:8, :128])
    ffn_update("ffn1")
    last = dffn1_norm + 0.0 * grads["ffn1_w_down"][:1, :1]
    grads["ffn1_norm"] = _small_exchange(last.reshape(-1, 128), reduce=True, name="reduce_ffn1_norm").reshape(-1)
    update("ffn1_norm")
    return (loss, grad_x, *[grads[n] for n in WEIGHT_NAMES], *[deltas[n] for n in WEIGHT_NAMES],
            *[new_m[n] for n in WEIGHT_NAMES], *[new_v[n] for n in WEIGHT_NAMES])


def kernel(x, positions, ffn1_norm, ffn1_w_gate, ffn1_w_up, ffn1_w_down, mix_norm, w_in, gate_bias, q_a_norm, w_uq, kv_a_norm, w_uk, w_uv, q_head_norm, k_head_norm, w_proj_attn, conv_w, w_proj_conv, w_out, ffn2_norm, ffn2_w_gate, ffn2_w_up, ffn2_w_down, loss_target, m_ffn1_norm, m_ffn1_w_gate, m_ffn1_w_up, m_ffn1_w_down, m_mix_norm, m_w_in, m_gate_bias, m_q_a_norm, m_w_uq, m_kv_a_norm, m_w_uk, m_w_uv, m_q_head_norm, m_k_head_norm, m_w_proj_attn, m_conv_w, m_w_proj_conv, m_w_out, m_ffn2_norm, m_ffn2_w_gate, m_ffn2_w_up, m_ffn2_w_down, v_ffn1_norm, v_ffn1_w_gate, v_ffn1_w_up, v_ffn1_w_down, v_mix_norm, v_w_in, v_gate_bias, v_q_a_norm, v_w_uq, v_kv_a_norm, v_w_uk, v_w_uv, v_q_head_norm, v_k_head_norm, v_w_proj_attn, v_conv_w, v_w_proj_conv, v_w_out, v_ffn2_norm, v_ffn2_w_gate, v_ffn2_w_up, v_ffn2_w_down):
    given = dict(locals())
    w = {n: given[n] for n in WEIGHT_NAMES}
    m = {n: given["m_" + n] for n in WEIGHT_NAMES}
    v = {n: given["v_" + n] for n in WEIGHT_NAMES}
    return _step(x, positions, loss_target, w, m, v)
```

```python
import functools

import jax
import jax.numpy as jnp
from jax import lax
from jax.experimental import pallas as pl
from jax.experimental.pallas import tpu as pltpu

F32 = jnp.float32
BF16 = jnp.bfloat16
MESH = pl.DeviceIdType.MESH
ANY = pl.BlockSpec(memory_space=pl.ANY)

N_DEV = 8
D = 1024
DFF = 2816
N_HEADS = 8
HEAD_PAD = 128
QK_DIM = 96
NOPE = 64
ROPE_HALF = 16
Q_LORA = 384
KV_LORA = 256
LAT_PAD = 768
CONV_COLS = 3072
GATE_COLS = 2048
IN_DIM = 5792
IN_SHARD = IN_DIM // N_DEV
IN_SHARD_PAD = 736
FF_SHARD = DFF // N_DEV
ROPE_THETA = 10000.0
NORM_EPS = 1e-6
ATTN_SCALE = QK_DIM ** -0.5
NEG = -1e30

ADAM_LR, ADAM_B1, ADAM_B2, ADAM_EPS, ADAM_WD, ADAM_STEP = 0.001, 0.9, 0.999, 1e-08, 0.01, 10

PACK = ((("w_inT", IN_SHARD_PAD),), (("w_uq", 48), ("w_uk", 32), ("w_uv", 32), ("w_pa", 64), ("w_pc", 128), ("w_out", 128)))
PACK_OFF = {}
for _i, _group in enumerate(PACK):
    _o = 0
    for _n, _r in _group:
        PACK_OFF[_n] = (_i, _o, _r)
        _o += _r

VMEM_LIMIT = 56 * 1024 * 1024


def _params(*sem):
    return pltpu.CompilerParams(dimension_semantics=sem if sem else None, vmem_limit_bytes=VMEM_LIMIT)


class _Plan:
    def __init__(self, start, wait, n_remote, n_local, in_place=False):
        self.start, self.wait, self.n_remote, self.n_local, self.in_place = start, wait, n_remote, n_local, in_place

    def sems(self):
        return [pltpu.SemaphoreType.DMA((self.n_remote,)), pltpu.SemaphoreType.DMA((self.n_remote,)),
                pltpu.SemaphoreType.DMA((max(self.n_local, 1),))]


def _call(body, *, name, grid, in_specs, out_specs, out_shape, scratch_shapes, operands, sem, hosted=None):
    if hosted is None:
        outs = pl.pallas_call(body, name=name, grid=grid, in_specs=in_specs, out_specs=out_specs, out_shape=out_shape,
                              scratch_shapes=scratch_shapes, compiler_params=_params(*sem))(*operands)
        return outs, None
    plan, srcs, h_shapes = hosted
    n_in, n_out, n_scr, nh_in, nh_out = len(in_specs), len(out_specs), len(scratch_shapes), len(srcs), len(h_shapes)
    aliases = {n_in + a: n_out + a for a in range(nh_in)} if plan.in_place else {}

    def full_body(*refs):
        ins, refs = refs[:n_in], refs[n_in:]
        h_in, refs = refs[:nh_in], refs[nh_in:]
        outs, refs = refs[:n_out], refs[n_out:]
        h_out, refs = refs[:nh_out], refs[nh_out:]
        scr, sems = refs[:n_scr], refs[n_scr:]
        ids = [pl.program_id(ax) for ax in range(len(grid))]
        first = functools.reduce(jnp.logical_and, [i == 0 for i in ids])
        last = functools.reduce(jnp.logical_and, [i == g - 1 for i, g in zip(ids, grid)])

        @pl.when(first)
        def _():
            plan.start(h_in, h_out, *sems)

        body(*ins, *outs, *scr)

        @pl.when(last)
        def _():
            plan.wait(h_in, h_out, *sems)

    res = pl.pallas_call(
        full_body, name=name, grid=grid, in_specs=list(in_specs) + [ANY] * nh_in, out_specs=list(out_specs) + [ANY] * nh_out,
        out_shape=list(out_shape) + list(h_shapes), scratch_shapes=list(scratch_shapes) + plan.sems(),
        input_output_aliases=aliases, compiler_params=_params(*(["arbitrary"] * len(grid))),
    )(*operands, *srcs)
    return res[:n_out], res[n_out:]


def _dot_nn(a, b):
    return lax.dot_general(a, b, (((1,), (0,)), ((), ())), preferred_element_type=F32)


def _dot_nt(a, b):
    return lax.dot_general(a, b, (((1,), (1,)), ((), ())), preferred_element_type=F32)


def _dot_tn(a, b):
    return lax.dot_general(a, b, (((0,), (0,)), ((), ())), preferred_element_type=F32)


def _sigmoid(x):
    return 0.5 * jnp.tanh(0.5 * x) + 0.5


def _rms_stats(x):
    r = lax.rsqrt(jnp.mean(x * x, axis=-1, keepdims=True) + NORM_EPS)
    return x * r, r


ROWS_WIDE = 16
MM_ROWS = 256


def _rms_bwd(dy, xhat, r, g):
    dg = jnp.sum(dy * xhat, axis=0, keepdims=True)
    dxh = dy * g
    dx = r * (dxh - xhat * jnp.mean(dxh * xhat, axis=-1, keepdims=True))
    return dx, dg


def _mm(a, b, *, mode, out_dtype, tm, tn, tk, name, add=None, scale=1.0, hosted=None):
    if mode == "nn":
        (m, k), (_, n) = a.shape, b.shape
    elif mode == "nt":
        (m, k), (n, _) = a.shape, b.shape
    else:
        (k, m), (_, n) = a.shape, b.shape
    assert m % tm == 0 and n % tn == 0 and k % tk == 0, (name, m, n, k, tm, tn, tk)
    nk = k // tk
    dot = {"nn": _dot_nn, "nt": _dot_nt, "tn": _dot_tn}[mode]
    a_spec = pl.BlockSpec((tk, tm), lambda i, j, kk: (kk, i)) if mode == "tn" else pl.BlockSpec((tm, tk), lambda i, j, kk: (i, kk))
    b_spec = pl.BlockSpec((tn, tk), lambda i, j, kk: (j, kk)) if mode == "nt" else pl.BlockSpec((tk, tn), lambda i, j, kk: (kk, j))
    o_spec = pl.BlockSpec((tm, tn), lambda i, j, kk: (i, j))
    has_add = add is not None

    def finish(prod, c_ref, o_ref):
        if scale != 1.0:
            prod = prod * scale
        o_ref[...] = ((c_ref[...] + prod) if has_add else prod).astype(out_dtype)

    def body(*refs):
        a_ref, b_ref = refs[:2]
        c_ref = refs[2] if has_add else None
        o_ref = refs[3] if has_add else refs[2]
        if nk == 1:
            finish(dot(a_ref[...], b_ref[...]), c_ref, o_ref)
            return
        acc_ref = refs[-1]
        kk = pl.program_id(2)

        @pl.when(kk == 0)
        def _():
            acc_ref[...] = jnp.zeros_like(acc_ref)

        acc_ref[...] += dot(a_ref[...], b_ref[...])

        @pl.when(kk == nk - 1)
        def _():
            finish(acc_ref[...], c_ref, o_ref)

    operands = (a, b, add) if has_add else (a, b)
    in_specs = [a_spec, b_spec] + ([o_spec] if has_add else [])
    (out,), got = _call(
        body, name=name, grid=(m // tm, n // tn, nk), in_specs=in_specs, out_specs=[o_spec],
        out_shape=[jax.ShapeDtypeStruct((m, n), out_dtype)], scratch_shapes=[pltpu.VMEM((tm, tn), F32)] if nk > 1 else [],
        operands=operands, sem=("parallel", "parallel", "arbitrary"), hosted=hosted)
    return out if hosted is None else (out, got)


def _rms_fwd(x, g, *, tm, name, hosted=None):
    t, d = x.shape

    def body(x_ref, g_ref, h_ref):
        xhat, _ = _rms_stats(x_ref[...])
        h_ref[...] = (xhat * g_ref[...]).astype(BF16)

    (h,), got = _call(
        body, name=name, grid=(t // tm,),
        in_specs=[pl.BlockSpec((tm, d), lambda i: (i, 0)), pl.BlockSpec((1, d), lambda i: (0, 0))],
        out_specs=[pl.BlockSpec((tm, d), lambda i: (i, 0))], out_shape=[jax.ShapeDtypeStruct((t, d), BF16)], scratch_shapes=[],
        operands=(x, g), sem=("parallel",), hosted=hosted)
    return h, got


def _ffn_fwd(x, g, wgT, wuT, wd, *, tm, hc, name, hosted=None, target=None):
    t, d = x.shape
    nj = DFF // hc
    with_loss = target is not None

    def body(*refs):
        x_ref, g_ref, wg_ref, wu_ref, wd_ref = refs[:5]
        t_ref = refs[5] if with_loss else None
        xo_ref, h_ref, a_ref, b_ref = refs[5 + with_loss:9 + with_loss]
        loss_ref = refs[9 + with_loss] if with_loss else None
        acc_ref = refs[-1]
        i, j = pl.program_id(0), pl.program_id(1)

        @pl.when(j == 0)
        def _():
            xhat, _ = _rms_stats(x_ref[...])
            h_ref[...] = (xhat * g_ref[...]).astype(BF16)
            acc_ref[...] = jnp.zeros_like(acc_ref)

        h = h_ref[...]
        a = _dot_nt(h, wg_ref[...])
        b = _dot_nt(h, wu_ref[...])
        a_ref[...] = a.astype(BF16)
        b_ref[...] = b.astype(BF16)
        s = (a * _sigmoid(a) * b).astype(BF16)
        acc_ref[...] += _dot_nn(s, wd_ref[...])

        if with_loss:
            @pl.when((i == 0) & (j == 0))
            def _():
                loss_ref[...] = jnp.zeros_like(loss_ref)

        @pl.when(j == nj - 1)
        def _():
            y = x_ref[...] + 0.5 * acc_ref[...]
            if with_loss:
                err = y - t_ref[...]
                xo_ref[...] = err * (1.0 / d)
                loss_ref[...] += jnp.sum(jnp.sum(err * err, axis=-1, keepdims=True), axis=0, keepdims=True) * (0.5 / d)
            else:
                xo_ref[...] = y

    row = pl.BlockSpec((tm, d), lambda i, j: (i, 0))
    vec = pl.BlockSpec((1, d), lambda i, j: (0, 0))
    wsp = pl.BlockSpec((hc, d), lambda i, j: (j, 0))
    hid = pl.BlockSpec((tm, hc), lambda i, j: (i, j))
    out_specs = [row, row, hid, hid] + ([pl.BlockSpec((1, 128), lambda i, j: (0, 0))] if with_loss else [])
    out_shape = [jax.ShapeDtypeStruct((t, d), F32), jax.ShapeDtypeStruct((t, d), BF16), jax.ShapeDtypeStruct((t, DFF), BF16),
                 jax.ShapeDtypeStruct((t, DFF), BF16)] + ([jax.ShapeDtypeStruct((1, 128), F32)] if with_loss else [])
    return _call(
        body, name=name, grid=(t // tm, nj), in_specs=[row, vec, wsp, wsp, wsp] + ([row] if with_loss else []),
        out_specs=out_specs, out_shape=out_shape, scratch_shapes=[pltpu.VMEM((tm, d), F32)],
        operands=(x, g, wgT, wuT, wd) + ((target,) if with_loss else ()),
        sem=("arbitrary" if with_loss else "parallel", "arbitrary"), hosted=hosted)


def _ffn_grads(dout, h, a, b, wd, *, tm, hc, name, hosted=None):
    t, d = dout.shape
    ni, nj = t // tm, DFF // hc

    def body(dout_ref, h_ref, a_ref, b_ref, wd_ref, da_ref, db_ref, dwg_ref, dwu_ref, dwd_ref,
             dy_all, h_all, ds_scr, s_scr, acc_g, acc_u, acc_d):
        j, i = pl.program_id(0), pl.program_id(1)
        rows_i = pl.ds(pl.multiple_of(i * tm, tm), tm)

        @pl.when(j == 0)
        def _():
            dy_all[rows_i, :] = (0.5 * dout_ref[...]).astype(BF16)
            h_all[rows_i, :] = h_ref[...]

        @pl.when(i == 0)
        def _():
            acc_g[...] = jnp.zeros_like(acc_g)
            acc_u[...] = jnp.zeros_like(acc_u)
            acc_d[...] = jnp.zeros_like(acc_d)

        def grad_rows(rows):
            ds = ds_scr[rows, :]
            av = a_ref[rows, :].astype(F32)
            bv = b_ref[rows, :].astype(F32)
            sg = _sigmoid(av)
            sl = av * sg
            s_scr[rows, :] = (sl * bv).astype(BF16)
            da_ref[rows, :] = (ds * bv * (sg + sl * (1.0 - sg))).astype(BF16)
            db_ref[rows, :] = (ds * sl).astype(BF16)

        for blk in range(tm // MM_ROWS):
            rs = slice(blk * MM_ROWS, (blk + 1) * MM_ROWS)
            ds_scr[rs, :] = _dot_nt(dy_all[pl.ds(pl.multiple_of(i * tm + blk * MM_ROWS, MM_ROWS), MM_ROWS), :], wd_ref[...])
            for c in range(MM_ROWS // ROWS_WIDE):
                grad_rows(slice(blk * MM_ROWS + c * ROWS_WIDE, blk * MM_ROWS + (c + 1) * ROWS_WIDE))

        dy_i = dy_all[rows_i, :]
        h_i = h_all[rows_i, :]
        acc_d[...] += _dot_tn(s_scr[...], dy_i)
        acc_g[...] += _dot_tn(da_ref[...], h_i)
        acc_u[...] += _dot_tn(db_ref[...], h_i)

        @pl.when(i == ni - 1)
        def _():
            dwg_ref[...] = acc_g[...].astype(BF16)
            dwu_ref[...] = acc_u[...].astype(BF16)
            dwd_ref[...] = acc_d[...].astype(BF16)

    first = pl.BlockSpec((tm, d), lambda j, i: (jnp.where(j == 0, i, 0), 0))
    hid = pl.BlockSpec((tm, hc), lambda j, i: (i, j))
    wsp = pl.BlockSpec((hc, d), lambda j, i: (j, 0))
    hid_shape = jax.ShapeDtypeStruct((t, DFF), BF16)
    w_shape = jax.ShapeDtypeStruct((DFF, d), BF16)
    return _call(
        body, name=name, grid=(nj, ni), in_specs=[first, first, hid, hid, wsp], out_specs=[hid, hid, wsp, wsp, wsp],
        out_shape=[hid_shape, hid_shape, w_shape, w_shape, w_shape],
        scratch_shapes=[pltpu.VMEM((t, d), BF16), pltpu.VMEM((t, d), BF16), pltpu.VMEM((tm, hc), F32), pltpu.VMEM((tm, hc), BF16),
                        pltpu.VMEM((hc, d), F32), pltpu.VMEM((hc, d), F32), pltpu.VMEM((hc, d), F32)],
        operands=(dout, h, a, b, wd), sem=("arbitrary", "arbitrary"), hosted=hosted)


def _proj_fwd(h, latT, convT, gateT, *, tm, name, hosted=None):
    t, d = h.shape

    def body(h_ref, wl_ref, wc_ref, wg_ref, lat_ref, conv_ref, gl_ref):
        hv = h_ref[...]
        lat_ref[...] = _dot_nt(hv, wl_ref[...]).astype(BF16)
        conv_ref[...] = _dot_nt(hv, wc_ref[...]).astype(BF16)
        gl_ref[...] = _dot_nt(hv, wg_ref[...]).astype(BF16)

    def rows(w):
        return pl.BlockSpec((tm, w), lambda i: (i, 0))

    def full(r):
        return pl.BlockSpec((r, d), lambda i: (0, 0))

    return _call(
        body, name=name, grid=(t // tm,), in_specs=[rows(d), full(LAT_PAD), full(CONV_COLS), full(GATE_COLS)],
        out_specs=[rows(LAT_PAD), rows(CONV_COLS), rows(GATE_COLS)],
        out_shape=[jax.ShapeDtypeStruct((t, LAT_PAD), BF16), jax.ShapeDtypeStruct((t, CONV_COLS), BF16),
                   jax.ShapeDtypeStruct((t, GATE_COLS), BF16)],
        scratch_shapes=[], operands=(h, latT, convT, gateT), sem=("parallel",), hosted=hosted)


def _proj_bwd(dlat, dconv3, dgl, latT, convT, gateT, x, g, dres, *, tm, name, hosted=None):
    t, d = x.shape

    def body(dl_ref, dc_ref, dg_ref, wl_ref, wc_ref, wg_ref, x_ref, g_ref, dres_ref, dx_ref, dgain_ref):
        @pl.when(pl.program_id(0) == 0)
        def _():
            dgain_ref[...] = jnp.zeros_like(dgain_ref)

        dh = _dot_nn(dl_ref[...], wl_ref[...]) + _dot_nn(dc_ref[...], wc_ref[...]) + _dot_nn(dg_ref[...], wg_ref[...])
        xhat, r = _rms_stats(x_ref[...])
        dx, dgain = _rms_bwd(dh, xhat, r, g_ref[...])
        dx_ref[...] = dres_ref[...] + dx
        dgain_ref[...] += dgain

    def rows(w):
        return pl.BlockSpec((tm, w), lambda i: (i, 0))

    def full(r):
        return pl.BlockSpec((r, d), lambda i: (0, 0))

    return _call(
        body, name=name, grid=(t // tm,),
        in_specs=[rows(LAT_PAD), rows(CONV_COLS), rows(GATE_COLS), full(LAT_PAD), full(CONV_COLS), full(GATE_COLS), rows(d), full(1), rows(d)],
        out_specs=[rows(d), full(1)], out_shape=[jax.ShapeDtypeStruct((t, d), F32), jax.ShapeDtypeStruct((1, d), F32)],
        scratch_shapes=[], operands=(dlat, dconv3, dgl, latT, convT, gateT, x, g, dres), sem=("arbitrary",), hosted=hosted)


def _ffn_up_bwd(da, db, wgT, wuT, x, g, dout, *, tm, name, hosted=None):
    t, d = x.shape

    def body(da_ref, db_ref, wg_ref, wu_ref, x_ref, g_ref, dout_ref, dx_ref, dg_ref):
        @pl.when(pl.program_id(0) == 0)
        def _():
            dg_ref[...] = jnp.zeros_like(dg_ref)

        dh = _dot_nn(da_ref[...], wg_ref[...]) + _dot_nn(db_ref[...], wu_ref[...])
        xhat, r = _rms_stats(x_ref[...])
        dx, dg = _rms_bwd(dh, xhat, r, g_ref[...])
        dx_ref[...] = dout_ref[...] + dx
        dg_ref[...] += dg

    row = pl.BlockSpec((tm, d), lambda i: (i, 0))
    vec = pl.BlockSpec((1, d), lambda i: (0, 0))
    hid = pl.BlockSpec((tm, DFF), lambda i: (i, 0))
    wsp = pl.BlockSpec((DFF, d), lambda i: (0, 0))
    return _call(
        body, name=name, grid=(t // tm,), in_specs=[hid, hid, wsp, wsp, row, vec, row], out_specs=[row, vec],
        out_shape=[jax.ShapeDtypeStruct((t, d), F32), jax.ShapeDtypeStruct((1, d), F32)], scratch_shapes=[],
        operands=(da, db, wgT, wuT, x, g, dout), sem=("arbitrary",), hosted=hosted)


HEAD_LANES = (slice(0, 32), slice(64, 80), None, slice(32, 64), slice(80, 96), None)


def _head_cols(a):
    def part(sl, width):
        if sl is None or sl.stop > a.shape[1]:
            return jnp.zeros((a.shape[0], width), a.dtype)
        return a[:, sl]

    return jnp.concatenate([part(sl, w) for sl, w in zip(HEAD_LANES, (32, 16, 16, 32, 16, 16))], axis=1)


def _head_cols_inv(a, dims):
    parts = [a[:, 0:32], a[:, 64:96]] + ([a[:, 32:48], a[:, 96:112]] if dims == QK_DIM else [])
    return jnp.concatenate(parts, axis=1)


def _rope_fwd(x, c, s):
    return x * c + pltpu.roll(x, HEAD_PAD // 2, 1) * s


def _rope_bwd(dy, c, s):
    return dy * c + pltpu.roll(dy * s, HEAD_PAD // 2, 1)


def _head_stats(x):
    r = lax.rsqrt(jnp.sum(x * x, axis=-1, keepdims=True) * (1.0 / QK_DIM) + NORM_EPS)
    return x * r, r


def _mla_prep_fwd(lat, gq, gkv, ghq, ghk, wq, wk, wv, rc, rs, *, tm, name):
    t = lat.shape[0]

    def body(lat_ref, gq_ref, gkv_ref, ghq_ref, ghk_ref, wq_ref, wk_ref, wv_ref, c_ref, s_ref,
             q_ref, k_ref, v_ref, qn_ref, ckv_ref):
        lat_v = lat_ref[...]
        qhat, _ = _rms_stats(lat_v[:, :Q_LORA].astype(F32))
        qn = (qhat * gq_ref[...]).astype(BF16)
        khat, _ = _rms_stats(lat_v[:, Q_LORA:Q_LORA + KV_LORA].astype(F32))
        ckv = (khat * gkv_ref[...]).astype(BF16)
        ckv_ext = jnp.concatenate([ckv, lat_v[:, Q_LORA + KV_LORA:]], axis=1)
        qn_ref[...] = qn
        ckv_ref[...] = ckv_ext
        q_pre = _dot_nn(qn, wq_ref[...])
        k_pre = _dot_nn(ckv_ext, wk_ref[...])
        v_ref[...] = _dot_nn(ckv, wv_ref[...]).astype(BF16)
        c, s = c_ref[...], s_ref[...]
        for h in range(N_HEADS):
            hs = slice(h * HEAD_PAD, (h + 1) * HEAD_PAD)
            xq, _ = _head_stats(q_pre[:, hs])
            q_ref[:, hs] = _rope_fwd(xq * ghq_ref[...], c, s).astype(BF16)
            xk, _ = _head_stats(k_pre[:, hs])
            k_ref[:, hs] = _rope_fwd(xk * ghk_ref[...], c, s).astype(BF16)

    def row(w):
        return pl.BlockSpec((tm, w), lambda i: (i, 0))

    def full(r, w):
        return pl.BlockSpec((r, w), lambda i: (0, 0))

    wide = jax.ShapeDtypeStruct((t, D), BF16)
    lat3 = jax.ShapeDtypeStruct((t, Q_LORA), BF16)
    return pl.pallas_call(
        body, name=name, grid=(t // tm,),
        in_specs=[row(LAT_PAD), full(1, Q_LORA), full(1, KV_LORA), full(1, HEAD_PAD), full(1, HEAD_PAD),
                  full(Q_LORA, D), full(Q_LORA, D), full(KV_LORA, D), row(HEAD_PAD), row(HEAD_PAD)],
        out_specs=[row(D), row(D), row(D), row(Q_LORA), row(Q_LORA)],
        out_shape=[wide, wide, wide, lat3, lat3],
        compiler_params=_params("parallel"),
    )(lat, gq, gkv, ghq, ghk, wq, wk, wv, rc, rs)


def _mla_prep_bwd(dq, dk, dv, lat, qn, ckv_ext, gq, gkv, ghq, ghk, wq, wk, wv, rc, rs, *, tm, name):
    t = lat.shape[0]

    def body(dq_ref, dk_ref, dv_ref, lat_ref, qn_ref, ckv_ref, gq_ref, gkv_ref, ghq_ref, ghk_ref, wq_ref, wk_ref, wv_ref,
             c_ref, s_ref, dlat_ref, dqp_ref, dkp_ref, dgq_ref, dgkv_ref, dghq_ref, dghk_ref):
        @pl.when(pl.program_id(0) == 0)
        def _():
            dgq_ref[...] = jnp.zeros_like(dgq_ref)
            dgkv_ref[...] = jnp.zeros_like(dgkv_ref)
            dghq_ref[...] = jnp.zeros_like(dghq_ref)
            dghk_ref[...] = jnp.zeros_like(dghk_ref)

        c, s = c_ref[...], s_ref[...]
        q_pre = _dot_nn(qn_ref[...], wq_ref[...])
        k_pre = _dot_nn(ckv_ref[...], wk_ref[...])

        def heads(pre, dy_ref, gh_ref, dgh_ref, out_ref):
            dgh = jnp.zeros((1, HEAD_PAD), F32)
            for h in range(N_HEADS):
                hs = slice(h * HEAD_PAD, (h + 1) * HEAD_PAD)
                d = _rope_bwd(dy_ref[:, hs].astype(F32), c, s)
                xhat, r = _head_stats(pre[:, hs])
                dgh = dgh + jnp.sum(d * xhat, axis=0, keepdims=True)
                dxh = d * gh_ref[...]
                dx = r * (dxh - xhat * (jnp.sum(dxh * xhat, axis=-1, keepdims=True) * (1.0 / QK_DIM)))
                out_ref[:, hs] = dx.astype(BF16)
            dgh_ref[...] += dgh

        heads(q_pre, dq_ref, ghq_ref, dghq_ref, dqp_ref)
        heads(k_pre, dk_ref, ghk_ref, dghk_ref, dkp_ref)
        dqn = _dot_nt(dqp_ref[...], wq_ref[...])
        dce = _dot_nt(dkp_ref[...], wk_ref[...])
        dckv = dce[:, :KV_LORA] + _dot_nt(dv_ref[...], wv_ref[...])
        lat_v = lat_ref[...]
        qhat, rq = _rms_stats(lat_v[:, :Q_LORA].astype(F32))
        dql, dgq = _rms_bwd(dqn, qhat, rq, gq_ref[...])
        khat, rk = _rms_stats(lat_v[:, Q_LORA:Q_LORA + KV_LORA].astype(F32))
        dkl, dgkv = _rms_bwd(dckv, khat, rk, gkv_ref[...])
        dgq_ref[...] += dgq
        dgkv_ref[...] += dgkv
        dlat_ref[...] = jnp.concatenate([dql, dkl, dce[:, KV_LORA:]], axis=1).astype(BF16)

    def row(w):
        return pl.BlockSpec((tm, w), lambda i: (i, 0))

    def full(r, w):
        return pl.BlockSpec((r, w), lambda i: (0, 0))

    return pl.pallas_call(
        body, name=name, grid=(t // tm,),
        in_specs=[row(D), row(D), row(D), row(LAT_PAD), row(Q_LORA), row(Q_LORA), full(1, Q_LORA), full(1, KV_LORA),
                  full(1, HEAD_PAD), full(1, HEAD_PAD), full(Q_LORA, D), full(Q_LORA, D), full(KV_LORA, D),
                  row(HEAD_PAD), row(HEAD_PAD)],
        out_specs=[row(LAT_PAD), row(D), row(D), full(1, Q_LORA), full(1, KV_LORA), full(1, HEAD_PAD), full(1, HEAD_PAD)],
        out_shape=[jax.ShapeDtypeStruct((t, LAT_PAD), BF16), jax.ShapeDtypeStruct((t, D), BF16), jax.ShapeDtypeStruct((t, D), BF16),
                   jax.ShapeDtypeStruct((1, Q_LORA), F32), jax.ShapeDtypeStruct((1, KV_LORA), F32),
                   jax.ShapeDtypeStruct((1, HEAD_PAD), F32), jax.ShapeDtypeStruct((1, HEAD_PAD), F32)],
        compiler_params=_params("arbitrary"),
    )(dq, dk, dv, lat, qn, ckv_ext, gq, gkv, ghq, ghk, wq, wk, wv, rc, rs)


def _causal_keep(tq):
    r = lax.broadcasted_iota(jnp.int32, (tq, tq), 0)
    c = lax.broadcasted_iota(jnp.int32, (tq, tq), 1)
    return c <= r


def _flash_fwd(q, k, v, *, n_seq, seq, tq, name, hosted=None):
    nq = seq // tq

    def body(q_ref, k_ref, v_ref, o_ref, lse_ref):
        for qi in range(nq):
            rows = slice(qi * tq, (qi + 1) * tq)
            qv = q_ref[rows, :]
            m = jnp.full((tq, 1), NEG, F32)
            l = jnp.zeros((tq, 1), F32)
            acc = jnp.zeros((tq, HEAD_PAD), F32)
            for j in range(qi + 1):
                cols = slice(j * tq, (j + 1) * tq)
                s = _dot_nt(qv, k_ref[cols, :]) * ATTN_SCALE
                if j == qi:
                    s = jnp.where(_causal_keep(tq), s, NEG)
                m_new = jnp.maximum(m, jnp.max(s, axis=-1, keepdims=True))
                alpha = jnp.exp(m - m_new)
                p = jnp.exp(s - m_new)
                l = alpha * l + jnp.sum(p, axis=-1, keepdims=True)
                acc = alpha * acc + _dot_nn(p.astype(BF16), v_ref[cols, :])
                m = m_new
            o_ref[rows, :] = (acc / l).astype(BF16)
            lse_ref[rows, :] = jnp.broadcast_to(m + jnp.log(l), (tq, HEAD_PAD))

    spec = pl.BlockSpec((seq, HEAD_PAD), lambda b, h: (b, h))
    t = n_seq * seq
    return _call(
        body, name=name, grid=(n_seq, N_HEADS), in_specs=[spec, spec, spec], out_specs=[spec, spec],
        out_shape=[jax.ShapeDtypeStruct((t, D), BF16), jax.ShapeDtypeStruct((t, D), F32)], scratch_shapes=[],
        operands=(q, k, v), sem=("parallel", "parallel"), hosted=hosted)


def _flash_bwd(q, k, v, o, lse, do, *, n_seq, seq, tq, name, hosted=None):
    nq = seq // tq

    def body(q_ref, k_ref, v_ref, o_ref, lse_ref, do_ref, dq_ref, dk_ref, dv_ref, dk_acc, dv_acc):
        j = pl.program_id(2)

        @pl.when(j == 0)
        def _():
            dq_ref[...] = jnp.zeros_like(dq_ref)

        dk_acc[...] = jnp.zeros_like(dk_acc)
        dv_acc[...] = jnp.zeros_like(dv_acc)
        kv = k_ref[...]
        vv = v_ref[...]

        def step(i, masked):
            rows = pl.ds(pl.multiple_of(i * tq, tq), tq)
            qi = q_ref[rows, :]
            doi = do_ref[rows, :]
            delta = jnp.sum(doi.astype(F32) * o_ref[rows, :].astype(F32), axis=-1, keepdims=True)
            s = _dot_nt(qi, kv) * ATTN_SCALE
            p = jnp.exp(s - lse_ref[rows, :][:, :1])
            if masked:
                p = jnp.where(_causal_keep(tq), p, 0.0)
            dv_acc[...] += _dot_tn(p.astype(BF16), doi)
            dp = _dot_nt(doi, vv)
            ds = (p * (dp - delta) * ATTN_SCALE).astype(BF16)
            dk_acc[...] += _dot_tn(ds, qi)
            dq_ref[rows, :] += _dot_nn(ds, kv)

        step(j, True)

        def loop_body(i, carry):
            step(i, False)
            return carry

        lax.fori_loop(j + 1, nq, loop_body, 0)
        dk_ref[...] = dk_acc[...]
        dv_ref[...] = dv_acc[...].astype(BF16)

    full = pl.BlockSpec((seq, HEAD_PAD), lambda b, h, j: (b, h))
    tile = pl.BlockSpec((tq, HEAD_PAD), lambda b, h, j: (b * nq + j, h))
    t = n_seq * seq
    return _call(
        body, name=name, grid=(n_seq, N_HEADS, nq), in_specs=[full, tile, tile, full, full, full],
        out_specs=[full, tile, tile],
        out_shape=[jax.ShapeDtypeStruct((t, D), F32), jax.ShapeDtypeStruct((t, D), F32), jax.ShapeDtypeStruct((t, D), BF16)],
        scratch_shapes=[pltpu.VMEM((tq, HEAD_PAD), F32), pltpu.VMEM((tq, HEAD_PAD), F32)],
        operands=(q, k, v, o, lse, do), sem=("parallel", "parallel", "arbitrary"), hosted=hosted)


CONV_CB = 256


def _shift_down(u, k, row):
    return jnp.where(row >= k, pltpu.roll(u, k, 0), 0.0)


def _shift_up(u, k, row, n):
    return jnp.where(row < n - k, pltpu.roll(u, n - k, 0), 0.0)


def _conv_fwd(conv3, cw, *, n_seq, seq, name, hosted=None):
    def body(c_ref, w_ref, p_ref):
        blk = c_ref[...].astype(F32)
        xc, gb, gc = blk[:, :CONV_CB], blk[:, CONV_CB:2 * CONV_CB], blk[:, 2 * CONV_CB:]
        row = lax.broadcasted_iota(jnp.int32, (seq, CONV_CB), 0)
        u = gc * xc
        z = w_ref[0:1, :] * _shift_down(u, 2, row) + w_ref[1:2, :] * _shift_down(u, 1, row) + w_ref[2:3, :] * u
        p_ref[...] = (gb * z).astype(BF16)

    (p,), got = _call(
        body, name=name, grid=(n_seq, D // CONV_CB),
        in_specs=[pl.BlockSpec((seq, 3 * CONV_CB), lambda b, j: (b, j)), pl.BlockSpec((3, CONV_CB), lambda b, j: (0, j))],
        out_specs=[pl.BlockSpec((seq, CONV_CB), lambda b, j: (b, j))],
        out_shape=[jax.ShapeDtypeStruct((n_seq * seq, D), BF16)], scratch_shapes=[],
        operands=(conv3, cw), sem=("parallel", "parallel"), hosted=hosted)
    return p, got


def _conv_bwd(dp, conv3, cw, *, n_seq, seq, name):
    def body(dp_ref, c_ref, w_ref, dc_ref, dw_ref):
        @pl.when(pl.program_id(1) == 0)
        def _():
            dw_ref[...] = jnp.zeros_like(dw_ref)

        blk = c_ref[...].astype(F32)
        xc, gb, gc = blk[:, :CONV_CB], blk[:, CONV_CB:2 * CONV_CB], blk[:, 2 * CONV_CB:]
        row = lax.broadcasted_iota(jnp.int32, (seq, CONV_CB), 0)
        w0, w1, w2 = w_ref[0:1, :], w_ref[1:2, :], w_ref[2:3, :]
        u = gc * xc
        u1 = _shift_down(u, 1, row)
        u2 = _shift_down(u, 2, row)
        z = w0 * u2 + w1 * u1 + w2 * u
        dpv = dp_ref[...].astype(F32)
        dz = dpv * gb
        du = w2 * dz + w1 * _shift_up(dz, 1, row, seq) + w0 * _shift_up(dz, 2, row, seq)
        dc_ref[...] = jnp.concatenate([du * gc, dpv * z, du * xc], axis=1).astype(BF16)
        dw_ref[0:1, :] += jnp.sum(dz * u2, axis=0, keepdims=True)
        dw_ref[1:2, :] += jnp.sum(dz * u1, axis=0, keepdims=True)
        dw_ref[2:3, :] += jnp.sum(dz * u, axis=0, keepdims=True)

    return pl.pallas_call(
        body, name=name, grid=(D // CONV_CB, n_seq),
        in_specs=[pl.BlockSpec((seq, CONV_CB), lambda j, b: (b, j)), pl.BlockSpec((seq, 3 * CONV_CB), lambda j, b: (b, j)),
                  pl.BlockSpec((3, CONV_CB), lambda j, b: (0, j))],
        out_specs=[pl.BlockSpec((seq, 3 * CONV_CB), lambda j, b: (b, j)), pl.BlockSpec((3, CONV_CB), lambda j, b: (0, j))],
        out_shape=[jax.ShapeDtypeStruct((n_seq * seq, CONV_COLS), BF16), jax.ShapeDtypeStruct((3, D), F32)],
        compiler_params=_params("parallel", "arbitrary"),
    )(dp, conv3, cw)


def _merge_fwd(o, p, gl, bias, x1, wpa, wpc, wout, *, tm, name, hosted=None):
    t = x1.shape[0]

    def body(o_ref, p_ref, gl_ref, b_ref, x_ref, wpa_ref, wpc_ref, wout_ref, x2_ref, mg_ref, ya_ref, yb_ref):
        ya = _dot_nn(o_ref[...], wpa_ref[...])
        yb = _dot_nn(p_ref[...], wpc_ref[...])
        gates = _sigmoid(gl_ref[...].astype(F32) + b_ref[...])
        merged = (gates[:, :D] * ya + gates[:, D:] * yb).astype(BF16)
        ya_ref[...] = ya.astype(BF16)
        yb_ref[...] = yb.astype(BF16)
        mg_ref[...] = merged
        x2_ref[...] = x_ref[...] + _dot_nn(merged, wout_ref[...])

    row = pl.BlockSpec((tm, D), lambda i: (i, 0))
    row2 = pl.BlockSpec((tm, GATE_COLS), lambda i: (i, 0))
    wsp = pl.BlockSpec((D, D), lambda i: (0, 0))
    wide = jax.ShapeDtypeStruct((t, D), BF16)
    return _call(
        body, name=name, grid=(t // tm,),
        in_specs=[row, row, row2, pl.BlockSpec((1, GATE_COLS), lambda i: (0, 0)), row, wsp, wsp, wsp],
        out_specs=[row, row, row, row], out_shape=[jax.ShapeDtypeStruct((t, D), F32), wide, wide, wide], scratch_shapes=[],
        operands=(o, p, gl, bias, x1, wpa, wpc, wout), sem=("parallel",), hosted=hosted)


def _merge_bwd(dx2, ya, yb, gl, bias, wpa, wpc, wout, *, tm, name, hosted=None):
    t = dx2.shape[0]

    def body(dx_ref, ya_ref, yb_ref, gl_ref, b_ref, wpa_ref, wpc_ref, wout_ref,
             dxb_ref, dya_ref, dyb_ref, dgl_ref, do_ref, dp_ref, db_ref):
        @pl.when(pl.program_id(0) == 0)
        def _():
            db_ref[...] = jnp.zeros_like(db_ref)

        dxb = dx_ref[...].astype(BF16)
        dxb_ref[...] = dxb
        dm = _dot_nt(dxb, wout_ref[...])
        gates = _sigmoid(gl_ref[...].astype(F32) + b_ref[...])
        ga, gb = gates[:, :D], gates[:, D:]
        dya = (dm * ga).astype(BF16)
        dyb = (dm * gb).astype(BF16)
        dya_ref[...] = dya
        dyb_ref[...] = dyb
        dgl = jnp.concatenate([dm * ya_ref[...].astype(F32) * ga * (1.0 - ga),
                               dm * yb_ref[...].astype(F32) * gb * (1.0 - gb)], axis=1)
        dgl_ref[...] = dgl.astype(BF16)
        db_ref[...] += jnp.sum(dgl, axis=0, keepdims=True)
        do_ref[...] = _dot_nt(dya, wpa_ref[...]).astype(BF16)
        dp_ref[...] = _dot_nt(dyb, wpc_ref[...]).astype(BF16)

    row = pl.BlockSpec((tm, D), lambda i: (i, 0))
    row2 = pl.BlockSpec((tm, GATE_COLS), lambda i: (i, 0))
    vec2 = pl.BlockSpec((1, GATE_COLS), lambda i: (0, 0))
    wsp = pl.BlockSpec((D, D), lambda i: (0, 0))
    wide = jax.ShapeDtypeStruct((t, D), BF16)
    return _call(
        body, name=name, grid=(t // tm,), in_specs=[row, row, row, row2, vec2, wsp, wsp, wsp],
        out_specs=[row, row, row, row2, row, row, vec2],
        out_shape=[wide, wide, wide, jax.ShapeDtypeStruct((t, GATE_COLS), BF16), wide, wide,
                   jax.ShapeDtypeStruct((1, GATE_COLS), F32)],
        scratch_shapes=[], operands=(dx2, ya, yb, gl, bias, wpa, wpc, wout), sem=("arbitrary",), hosted=hosted)


def _adamw_small(ws, gs, ms, vs, *, name):
    n = len(ws)
    c1 = 1.0 / (1.0 - ADAM_B1 ** ADAM_STEP)
    c2 = 1.0 / (1.0 - ADAM_B2 ** ADAM_STEP)

    def body(*refs):
        for i in range(n):
            w_ref, g_ref, m_ref, v_ref, d_ref, nm_ref, nv_ref = (refs[k * n + i] for k in range(7))
            gv = g_ref[...]
            nm = ADAM_B1 * m_ref[...] + (1.0 - ADAM_B1) * gv
            nv = ADAM_B2 * v_ref[...] + (1.0 - ADAM_B2) * (gv * gv)
            nm_ref[...] = nm
            nv_ref[...] = nv
            d_ref[...] = -ADAM_LR * ((nm * c1) / (jnp.sqrt(nv * c2) + ADAM_EPS) + ADAM_WD * w_ref[...])

    vm = pl.BlockSpec(memory_space=pltpu.VMEM)
    shapes = [jax.ShapeDtypeStruct(a.shape, F32) for a in ws]
    outs = pl.pallas_call(body, name=name, in_specs=[vm] * (4 * n), out_specs=[vm] * (3 * n), out_shape=shapes * 3)(*ws, *gs, *ms, *vs)
    return outs[:n], outs[n:2 * n], outs[2 * n:]


def _adamw(w, g, m, v, *, name):
    rows, cols = w.shape
    tr = max([c for c in range(8, 513, 8) if rows % c == 0], default=rows)
    c1 = 1.0 / (1.0 - ADAM_B1 ** ADAM_STEP)
    c2 = 1.0 / (1.0 - ADAM_B2 ** ADAM_STEP)

    def body(w_ref, g_ref, m_ref, v_ref, d_ref, nm_ref, nv_ref):
        gv = g_ref[...]
        nm = ADAM_B1 * m_ref[...] + (1.0 - ADAM_B1) * gv
        nv = ADAM_B2 * v_ref[...] + (1.0 - ADAM_B2) * (gv * gv)
        nm_ref[...] = nm
        nv_ref[...] = nv
        d_ref[...] = -ADAM_LR * ((nm * c1) / (jnp.sqrt(nv * c2) + ADAM_EPS) + ADAM_WD * w_ref[...])

    spec = pl.BlockSpec((tr, cols), lambda i: (i, 0))
    shp = jax.ShapeDtypeStruct((rows, cols), F32)
    return pl.pallas_call(
        body, name=name, grid=(rows // tr,), in_specs=[spec] * 4, out_specs=[spec] * 3, out_shape=[shp] * 3,
        compiler_params=_params("parallel"),
    )(w, g, m, v)


def _place():
    return lax.axis_index("x"), lax.axis_index("y"), lax.axis_index("c")


def _other_chips(x, y):
    return [(1 - x, y), (x, 1 - y), (1 - x, 1 - y)]


def _remote(src, dst, send, recv, dev):
    return pltpu.make_async_remote_copy(src_ref=src, dst_ref=dst, send_sem=send, recv_sem=recv, device_id=dev, device_id_type=MESH)


def _gather_chips_plan(n):
    def start(srcs, dsts, send, recv, local):
        x, y, cc = _place()
        me = 4 * x + 2 * y + cc
        for a in range(n):
            pltpu.make_async_copy(srcs[a], dsts[a].at[me], local.at[a]).start()
            for k, (px, py) in enumerate(_other_chips(x, y)):
                _remote(srcs[a], dsts[a].at[me], send.at[3 * a + k], recv.at[3 * a + k], (px, py, cc)).start()

    def wait(srcs, dsts, send, recv, local):
        x, y, cc = _place()
        me = 4 * x + 2 * y + cc
        for a in range(n):
            for k, (px, py) in enumerate(_other_chips(x, y)):
                _remote(srcs[a], dsts[a].at[4 * px + 2 * py + cc], send.at[3 * a + k], recv.at[3 * a + k], (px, py, cc)).wait_recv()
        for a in range(n):
            for k, (px, py) in enumerate(_other_chips(x, y)):
                _remote(srcs[a], dsts[a].at[me], send.at[3 * a + k], recv.at[3 * a + k], (px, py, cc)).wait_send()
            pltpu.make_async_copy(srcs[a], dsts[a].at[me], local.at[a]).wait()

    return _Plan(start, wait, 3 * n, n)


def _scatter_chips_plan(n):
    def start(srcs, dsts, send, recv, local):
        x, y, cc = _place()
        for a in range(n):
            for k, (px, py) in enumerate(_other_chips(x, y)):
                _remote(srcs[a].at[2 * px + py], dsts[a].at[k], send.at[3 * a + k], recv.at[3 * a + k], (px, py, cc)).start()

    def wait(srcs, dsts, send, recv, local):
        x, y, cc = _place()
        for a in range(n):
            for k, (px, py) in enumerate(_other_chips(x, y)):
                _remote(srcs[a].at[k], dsts[a].at[k], send.at[3 * a + k], recv.at[3 * a + k], (px, py, cc)).wait_recv()
        for a in range(n):
            for k, (px, py) in enumerate(_other_chips(x, y)):
                _remote(srcs[a].at[k], dsts[a].at[k], send.at[3 * a + k], recv.at[3 * a + k], (px, py, cc)).wait_send()

    return _Plan(start, wait, 3 * n, 0)


def _gather_shapes(blocks):
    return [jax.ShapeDtypeStruct((N_DEV,) + b.shape, b.dtype) for b in blocks]


def _scatter_shapes(parts):
    return [jax.ShapeDtypeStruct((3,) + p.shape[1:], p.dtype) for p in parts]


def _gather_sibling_plan(n):
    def start(srcs, dsts, send, recv, local):
        x, y, cc = _place()
        for a in range(n):
            for q in range(4):
                _remote(srcs[a].at[2 * q + cc], dsts[a].at[2 * q + cc], send.at[4 * a + q], recv.at[4 * a + q], (x, y, 1 - cc)).start()

    def wait(srcs, dsts, send, recv, local):
        x, y, cc = _place()
        for a in range(n):
            for q in range(4):
                _remote(srcs[a].at[2 * q + cc], dsts[a].at[2 * q + 1 - cc], send.at[4 * a + q], recv.at[4 * a + q],
                        (x, y, 1 - cc)).wait_recv()
        for a in range(n):
            for q in range(4):
                _remote(srcs[a].at[2 * q + cc], dsts[a].at[2 * q + cc], send.at[4 * a + q], recv.at[4 * a + q],
                        (x, y, 1 - cc)).wait_send()

    return _Plan(start, wait, 4 * n, 0, in_place=True)


def _scatter_sibling_plan(n):
    def start(srcs, dsts, send, recv, local):
        x, y, cc = _place()
        for a in range(n):
            for q in range(4):
                _remote(srcs[a].at[2 * q + 1 - cc], dsts[a].at[q], send.at[4 * a + q], recv.at[4 * a + q], (x, y, 1 - cc)).start()

    def wait(srcs, dsts, send, recv, local):
        x, y, cc = _place()
        for a in range(n):
            for q in range(4):
                _remote(srcs[a].at[q], dsts[a].at[q], send.at[4 * a + q], recv.at[4 * a + q], (x, y, 1 - cc)).wait_recv()
        for a in range(n):
            for q in range(4):
                _remote(srcs[a].at[q], dsts[a].at[q], send.at[4 * a + q], recv.at[4 * a + q], (x, y, 1 - cc)).wait_send()

    return _Plan(start, wait, 4 * n, 0)


def _same_shapes(arrs):
    return [jax.ShapeDtypeStruct(a.shape, a.dtype) for a in arrs]


def _halved_shapes(parts):
    return [jax.ShapeDtypeStruct((4,) + p.shape[1:], p.dtype) for p in parts]


def _run_plan(plan, srcs, out_shapes, *, name):
    n_in, n_out = len(srcs), len(out_shapes)

    def body(*refs):
        h_in, h_out, sems = refs[:n_in], refs[n_in:n_in + n_out], refs[n_in + n_out:]
        plan.start(h_in, h_out, *sems)
        plan.wait(h_in, h_out, *sems)

    return pl.pallas_call(body, name=name, in_specs=[ANY] * n_in, out_specs=[ANY] * n_out, out_shape=list(out_shapes),
                          input_output_aliases={a: a for a in range(n_in)} if plan.in_place else {},
                          scratch_shapes=plan.sems())(*srcs)


SEM = pl.BlockSpec(memory_space=pltpu.SEMAPHORE)
HBM = pl.BlockSpec(memory_space=pltpu.HBM)
SIDE_EFFECT = pltpu.CompilerParams(has_side_effects=pltpu.SideEffectType.DATAFLOW_SIDE_EFFECTING)


def _plan_start(plan, blocks, land_shapes, *, name):
    n = len(blocks)
    lands = [lax.empty(s.shape, s.dtype) for s in land_shapes]

    def body(*refs):
        srcs, sems, lands_out, token = refs[:n], refs[2 * n:2 * n + 3], refs[3 * n + 3:4 * n + 3], refs[4 * n + 3]
        plan.start(srcs, lands_out, *sems)
        token[...] = jnp.zeros_like(token)

    out_shape = ([s for s in plan.sems()] + [pltpu.HBM(b.shape, b.dtype) for b in blocks]
                 + [pltpu.HBM(l.shape, l.dtype) for l in lands] + [jax.ShapeDtypeStruct((8, 128), F32)])
    res = pl.pallas_call(
        body, name=name, in_specs=[HBM] * (2 * n), out_specs=[SEM] * 3 + [HBM] * (2 * n) + [pl.BlockSpec(memory_space=pltpu.VMEM)],
        out_shape=out_shape, input_output_aliases={a: 3 + a for a in range(2 * n)}, compiler_params=SIDE_EFFECT,
    )(*[pltpu.with_memory_space_constraint(a, pltpu.HBM) for a in list(blocks) + lands])
    return res[:3], res[3:3 + n], res[3 + n:3 + 2 * n], res[3 + 2 * n]


def _plan_wait(plan, sems, blocks, lands, after, *, name):
    n = len(blocks)

    def body(*refs):
        plan.wait(refs[:n], refs[n:2 * n], *refs[2 * n:2 * n + 3])

    res = pl.pallas_call(
        body, name=name, in_specs=[HBM] * (2 * n) + [SEM] * 3 + [ANY], out_specs=[HBM] * (2 * n),
        out_shape=[pltpu.HBM(a.shape, a.dtype) for a in list(blocks) + list(lands)],
        input_output_aliases={a: a for a in range(2 * n)}, compiler_params=SIDE_EFFECT,
    )(*blocks, *lands, *sems, after)
    return list(res[:n]), list(res[n:])


def _sum_sibling(ps, qs, core, *, name):
    n = len(ps)

    def body(core_ref, *refs):
        for p_ref, q_ref, o_ref in zip(refs[:n], refs[n:2 * n], refs[2 * n:]):
            o_ref[...] = (p_ref[...].astype(F32) + q_ref[...].astype(F32)).astype(BF16)

    def mine(p):
        return pl.BlockSpec((1,) + p.shape[1:], lambda ch, core_ref: (2 * ch + core_ref[0], 0, 0))

    def theirs(p):
        return pl.BlockSpec((1,) + p.shape[1:], lambda ch, core_ref: (ch, 0, 0))

    grid_spec = pltpu.PrefetchScalarGridSpec(
        num_scalar_prefetch=1, grid=(4,), in_specs=[mine(p) for p in ps] + [theirs(p) for p in ps], out_specs=[theirs(p) for p in ps])
    return pl.pallas_call(
        body, name=name, grid_spec=grid_spec, out_shape=[jax.ShapeDtypeStruct((4,) + p.shape[1:], BF16) for p in ps],
        compiler_params=_params("parallel"),
    )(core, *ps, *qs)


def _sum_chips(s1, r2, chip, *, name):
    _, r, c = s1.shape

    def body(chip_ref, s_ref, r_ref, o_ref):
        acc = s_ref[0].astype(F32)
        for k in range(3):
            acc = acc + r_ref[k].astype(F32)
        o_ref[...] = acc

    grid_spec = pltpu.PrefetchScalarGridSpec(
        num_scalar_prefetch=1, grid=(1,),
        in_specs=[pl.BlockSpec((1, r, c), lambda i, chip_ref: (chip_ref[0], 0, 0)),
                  pl.BlockSpec((3, r, c), lambda i, chip_ref: (0, 0, 0))],
        out_specs=pl.BlockSpec((r, c), lambda i, chip_ref: (0, 0)))
    return pl.pallas_call(
        body, name=name, grid_spec=grid_spec, out_shape=jax.ShapeDtypeStruct((r, c), F32),
        compiler_params=_params("arbitrary"),
    )(chip, s1, r2)


def _sum_adamw(s1, r2, chip, w, m, v, *, name):
    _, r, c = s1.shape
    c1 = 1.0 / (1.0 - ADAM_B1 ** ADAM_STEP)
    c2 = 1.0 / (1.0 - ADAM_B2 ** ADAM_STEP)

    def body(chip_ref, s_ref, r_ref, w_ref, m_ref, v_ref, g_ref, d_ref, nm_ref, nv_ref):
        gv = s_ref[0].astype(F32)
        for k in range(3):
            gv = gv + r_ref[k].astype(F32)
        g_ref[...] = gv
        nm = ADAM_B1 * m_ref[...] + (1.0 - ADAM_B1) * gv
        nv = ADAM_B2 * v_ref[...] + (1.0 - ADAM_B2) * (gv * gv)
        nm_ref[...] = nm
        nv_ref[...] = nv
        d_ref[...] = -ADAM_LR * ((nm * c1) / (jnp.sqrt(nv * c2) + ADAM_EPS) + ADAM_WD * w_ref[...])

    flat = pl.BlockSpec((r, c), lambda i, chip_ref: (0, 0))
    grid_spec = pltpu.PrefetchScalarGridSpec(
        num_scalar_prefetch=1, grid=(1,),
        in_specs=[pl.BlockSpec((1, r, c), lambda i, chip_ref: (chip_ref[0], 0, 0)),
                  pl.BlockSpec((3, r, c), lambda i, chip_ref: (0, 0, 0)), flat, flat, flat],
        out_specs=[flat] * 4)
    return pl.pallas_call(
        body, name=name, grid_spec=grid_spec, out_shape=[jax.ShapeDtypeStruct((r, c), F32)] * 4,
        compiler_params=_params("arbitrary"),
    )(chip, s1, r2, w, m, v)


def _small_exchange(v, *, reduce, name):
    r, c = v.shape

    def body(x_ref, o_ref, *rest):
        if reduce:
            buf_ref, send_sems, recv_sems = rest
        else:
            buf_ref = o_ref
            send_sems, recv_sems = rest
        x, y, cc = _place()
        me = 4 * x + 2 * y + cc

        def peer(k):
            return ((1 - x) if k & 4 else x, (1 - y) if k & 2 else y, (1 - cc) if k & 1 else cc)

        buf_ref[me] = x_ref[...]
        sends = []
        for k in range(1, N_DEV):
            cp = pltpu.make_async_remote_copy(src_ref=x_ref, dst_ref=buf_ref.at[me], send_sem=send_sems.at[k - 1],
                                              recv_sem=recv_sems.at[k - 1], device_id=peer(k), device_id_type=MESH)
            cp.start()
            sends.append(cp)
        for k in range(1, N_DEV):
            px, py, pc = peer(k)
            pltpu.make_async_remote_copy(src_ref=x_ref, dst_ref=buf_ref.at[4 * px + 2 * py + pc], send_sem=send_sems.at[k - 1],
                                         recv_sem=recv_sems.at[k - 1], device_id=peer(k), device_id_type=MESH).wait_recv()
        for cp in sends:
            cp.wait_send()
        if reduce:
            acc = buf_ref[0]
            for s in range(1, N_DEV):
                acc = acc + buf_ref[s]
            o_ref[...] = acc

    vm = pl.BlockSpec(memory_space=pltpu.VMEM)
    sems = [pltpu.SemaphoreType.DMA((N_DEV - 1,)), pltpu.SemaphoreType.DMA((N_DEV - 1,))]
    if reduce:
        out_shape, scratch = jax.ShapeDtypeStruct((r, c), F32), [pltpu.VMEM((N_DEV, r, c), F32)] + sems
    else:
        out_shape, scratch = jax.ShapeDtypeStruct((N_DEV, r, c), F32), sems
    return pl.pallas_call(body, name=name, in_specs=[vm], out_specs=vm, out_shape=out_shape, scratch_shapes=scratch)(v)


def _rows(a):
    return a.reshape(-1, D)


def _pad_cols(a, to):
    return jnp.pad(a, ((0, 0), (0, to - a.shape[1])))


def _pack_weights(w):
    parts = {
        "w_inT": jnp.pad(w["w_in"].T, ((0, IN_SHARD_PAD - IN_SHARD), (0, 0))),
        "w_uq": _rows(_head_cols(w["w_uq"])), "w_uk": _rows(_head_cols(w["w_uk"])),
        "w_uv": _rows(_pad_cols(w["w_uv"], HEAD_PAD)), "w_pa": _rows(w["w_proj_attn"]),
        "w_pc": w["w_proj_conv"], "w_out": w["w_out"],
    }
    return [jnp.concatenate([parts[n].astype(BF16) for n, _ in group], axis=0) for group in PACK]


def _cols_from_shards(gs, name, rows):
    idx, off, r = PACK_OFF[name]
    return gs[idx][:, off:off + r].reshape(N_DEV, rows, HEAD_PAD).transpose(1, 0, 2).reshape(rows, N_DEV * HEAD_PAD)


def _rows_from_shards(gs, name, keep=None):
    idx, off, r = PACK_OFF[name]
    keep = r if keep is None else keep
    return gs[idx][:, off:off + keep].reshape(N_DEV * keep, D)


def _rope_placement():
    i = lax.broadcasted_iota(jnp.int32, (HEAD_PAD, D), 0)
    j = lax.broadcasted_iota(jnp.int32, (HEAD_PAD, D), 1)
    lane = jnp.where(i < ROPE_HALF, 32 + i, 96 + i - ROPE_HALF)
    return ((i < 2 * ROPE_HALF) & (j % HEAD_PAD == lane)).astype(BF16)


def _unpack_in(g_in):
    w_inT = _rows_from_shards([g_in, None], "w_inT", IN_SHARD)
    lat_rows = Q_LORA + KV_LORA + 2 * ROPE_HALF
    conv = w_inT[lat_rows:lat_rows + CONV_COLS].reshape(3, D // CONV_CB, CONV_CB, D).transpose(1, 0, 2, 3).reshape(CONV_COLS, D)
    return {"latT": jnp.pad(w_inT[:lat_rows], ((0, LAT_PAD - lat_rows), (0, 0))), "convT": conv,
            "gateT": w_inT[lat_rows + CONV_COLS:]}


def _unpack_misc(g_misc):
    g = [None, g_misc]
    wpa = _cols_from_shards(g, "w_pa", 512).reshape(N_HEADS, NOPE, D)
    return {
        "wq": _cols_from_shards(g, "w_uq", Q_LORA),
        "wk": jnp.concatenate([_cols_from_shards(g, "w_uk", KV_LORA), _rope_placement()], axis=0),
        "wv": _cols_from_shards(g, "w_uv", KV_LORA),
        "wpa": jnp.pad(wpa, ((0, 0), (0, HEAD_PAD - NOPE), (0, 0))).reshape(D, D),
        "wpc": _rows_from_shards(g, "w_pc"), "wout": _rows_from_shards(g, "w_out"),
    }


def _shards_from_cols(a):
    rows = a.shape[0]
    return a.reshape(rows, N_DEV, HEAD_PAD).transpose(1, 0, 2).reshape(N_DEV, rows * HEAD_PAD // D, D)


def _pack_grads(gw):
    lat_rows = Q_LORA + KV_LORA + 2 * ROPE_HALF
    conv = gw["convT"].reshape(D // CONV_CB, 3, CONV_CB, D).transpose(1, 0, 2, 3).reshape(CONV_COLS, D)
    w_inT = jnp.concatenate([gw["latT"][:lat_rows], conv, gw["gateT"]], axis=0).reshape(N_DEV, IN_SHARD, D)
    wpa = gw["wpa"].reshape(N_HEADS, HEAD_PAD, D)[:, :NOPE].reshape(N_HEADS * NOPE, D)
    parts = {}
    parts.update({
        "w_inT": jnp.pad(w_inT, ((0, 0), (0, IN_SHARD_PAD - IN_SHARD), (0, 0))),
        "w_uq": _shards_from_cols(gw["wq"]), "w_uk": _shards_from_cols(gw["wk"][:KV_LORA]),
        "w_uv": _shards_from_cols(gw["wv"][:KV_LORA]), "w_pa": _shards_from_cols(wpa),
        "w_pc": gw["wpc"].reshape(N_DEV, D // N_DEV, D), "w_out": gw["wout"].reshape(N_DEV, D // N_DEV, D),
    })
    return [jnp.concatenate([parts[n] for n, _ in group], axis=1) for group in PACK]


def _unpack_grads(mines):
    def seg(name, keep=None):
        idx, off, r = PACK_OFF[name]
        return mines[idx][off:off + (r if keep is None else keep)]

    return {
        "w_in": seg("w_inT", IN_SHARD).T,
        "w_uq": _head_cols_inv(seg("w_uq").reshape(Q_LORA, HEAD_PAD), QK_DIM),
        "w_uk": _head_cols_inv(seg("w_uk").reshape(KV_LORA, HEAD_PAD), NOPE),
        "w_uv": seg("w_uv").reshape(KV_LORA, HEAD_PAD)[:, :NOPE],
        "w_proj_attn": seg("w_pa").reshape(512, HEAD_PAD),
        "w_proj_conv": seg("w_pc"), "w_out": seg("w_out"),
    }


def _rope_tables(positions):
    lane = jnp.arange(HEAD_PAD)
    idx = jnp.where((lane >= 32) & (lane < 48), lane - 32, jnp.where((lane >= 96) & (lane < 112), lane - 96, -1))
    inv_freq = jnp.where(idx >= 0, 1.0 / (ROPE_THETA ** (idx.astype(F32) / ROPE_HALF)), 0.0)
    ang = positions.reshape(-1).astype(F32)[:, None] * inv_freq
    return jnp.cos(ang), jnp.sin(ang) * jnp.where(lane < HEAD_PAD // 2, -1.0, 1.0)


def _local_step(x, positions, target, conv_w, small, ex):
    n_seq, seq, d = x.shape
    t = n_seq * seq
    x0 = x.reshape(t, d)
    tgt = target.reshape(t, d)
    rc, rs = _rope_tables(positions)
    ghq = _head_cols(small["q_head_norm"])
    ghk = _head_cols(small["k_head_norm"])
    TM, HC, TQ = 1024, 256, 1024

    def mm(*args, hosted=None, **kw):
        res = _mm(*args, hosted=hosted, **kw)
        return res if hosted is not None else (res, None)

    def wgrad(a, b, name, tm=None, hosted=None):
        tm = tm or a.shape[1]
        return mm(a, b, mode="tn", out_dtype=BF16, tm=tm, tn=b.shape[1], tk=2048 if tm <= D else 1024, name=name, hosted=hosted)

    f1g, f1u, f1d = ex.gather_finish(ex.witness() + rc[:8] + conv_w[:1, :HEAD_PAD])
    (x1, h1, a1, b1), got = _ffn_fwd(x0, small["ffn1_norm"], f1g, f1u, f1d, tm=512, hc=DFF // 2, name="ffn1_fwd",
                                     hosted=ex.gather_chips("mix_in"))
    hm, got = _rms_fwd(x1, small["mix_norm"], tm=TM, name="mix_norm_fwd", hosted=ex.gather_sibling(got))
    W = ex.mix_in_weights(got)
    (lat, conv3, gl), got = _proj_fwd(hm, W["latT"], W["convT"], W["gateT"], tm=512, name="proj_fwd",
                                      hosted=ex.gather_chips("mix_misc"))
    p, got = _conv_fwd(conv3, conv_w, n_seq=n_seq, seq=seq, name="conv_fwd", hosted=ex.gather_sibling(got))
    W.update(ex.mix_misc_weights(got))
    q, k, v, qn, ckv = _mla_prep_fwd(lat, small["q_a_norm"], small["kv_a_norm"], ghq, ghk, W["wq"], W["wk"], W["wv"], rc, rs,
                                     tm=512, name="mla_prep_fwd")
    (o, lse), got = _flash_fwd(q, k, v, n_seq=n_seq, seq=seq, tq=TQ, name="attn_fwd", hosted=ex.gather_chips("ffn2"))
    (x2, merged, ya, yb), got = _merge_fwd(o, p, gl, small["gate_bias"], x1, W["wpa"], W["wpc"], W["wout"], tm=512, name="merge_fwd",
                                           hosted=ex.gather_sibling(got))
    f2g, f2u, f2d = ex.ffn_weights(got)
    (dy, h2, a2, b2, loss_row), _ = _ffn_fwd(x2, small["ffn2_norm"], f2g, f2u, f2d, tm=512, hc=DFF // 2, name="ffn2_fwd", target=tgt)

    gw, gs = {}, {}
    (da2, db2, *ffn2_grads), _ = _ffn_grads(dy, h2, a2, b2, f2d, tm=TM, hc=HC, name="ffn2_grads")
    (dx2, gs["ffn2_norm"]), _ = _ffn_up_bwd(da2, db2, f2g, f2u, x2, small["ffn2_norm"], dy, tm=512, name="ffn2_up_bwd")

    (dx2b, dya, dyb, dgl, do, dp, gs["gate_bias"]), got = _merge_bwd(
        dx2, ya, yb, gl, small["gate_bias"], W["wpa"], W["wpc"], W["wout"], tm=512, name="merge_bwd",
        hosted=ex.scatter_sibling("ffn2", ffn2_grads))
    ex.scatter_sibling_done("ffn2", got)
    gw["wout"] = wgrad(merged, dx2b, "dw_out")[0]
    gw["wpa"] = wgrad(o, dya, "dw_pa")[0]
    gw["wpc"] = wgrad(p, dyb, "dw_pc")[0]
    dconv3, dconv_w = _conv_bwd(dp, conv3, conv_w, n_seq=n_seq, seq=seq, name="conv_bwd")
    (dq, dk, dv), got = _flash_bwd(q, k, v, o, lse, do, n_seq=n_seq, seq=seq, tq=TQ, name="attn_bwd",
                                   hosted=ex.scatter_chips("ffn2"))
    ex.scatter_chips_done("ffn2", got)
    dlat, dqp, dkp, gs["q_a_norm"], gs["kv_a_norm"], dghq, dghk = _mla_prep_bwd(
        dq, dk, dv, lat, qn, ckv, small["q_a_norm"], small["kv_a_norm"], ghq, ghk, W["wq"], W["wk"], W["wv"], rc, rs,
        tm=512, name="mla_prep_bwd")
    gs["q_head_norm"], gs["k_head_norm"] = _head_cols_inv(dghq, QK_DIM), _head_cols_inv(dghk, QK_DIM)
    gw["wq"] = wgrad(qn, dqp, "dw_uq")[0]
    gw["wk"] = wgrad(ckv, dkp, "dw_uk")[0]
    gw["wv"] = wgrad(ckv, dv, "dw_uv")[0]
    gw["convT"] = wgrad(dconv3, hm, "dw_conv", tm=CONV_COLS // 2)[0]
    gw["gateT"] = wgrad(dgl, hm, "dw_gate")[0]
    gw["latT"] = wgrad(dlat, hm, "dw_lat")[0]
    ex.scatter_sibling_now("mix", gw)
    zero = ex.scatter_chips_start("mix_in")
    (dx1, gs["mix_norm"]), _ = _proj_bwd(dlat, dconv3, dgl, W["latT"], W["convT"], W["gateT"], x1, small["mix_norm"] + zero, dx2,
                                         tm=512, name="proj_bwd")

    (da1, db1, *ffn1_grads), got = _ffn_grads(dx1, h1, a1, b1, f1d, tm=TM, hc=HC, name="ffn1_grads",
                                              hosted=ex.scatter_chips("mix_misc"))
    ex.scatter_chips_done("mix_misc", got)
    ex.reduce_small(gs, dconv_w, loss_row)
    ex.scatter_sibling_now("ffn1", ffn1_grads)
    zero = ex.scatter_chips_start("ffn1")
    (dx0, gs["ffn1_norm"]), _ = _ffn_up_bwd(da1, db1, f1g, f1u, x0, small["ffn1_norm"] + zero, dx1, tm=512, name="ffn1_up_bwd")
    return dx0.reshape(n_seq, seq, d), gs["ffn1_norm"]


class _MeshExchange:
    def __init__(self, w, core, chip):
        self.w, self.core, self.chip = w, core, chip
        self.partial, self.received, self._cache, self._scattering = {}, {}, {}, {}

    def _blocks(self, group):
        w = self.w
        if group not in self._cache:
            if group.startswith("ffn"):
                self._cache[group] = [w[group + "_w_gate"].T.astype(BF16), w[group + "_w_up"].T.astype(BF16),
                                      w[group + "_w_down"].astype(BF16)]
            else:
                self._cache["mix_in"], self._cache["mix_misc"] = [[b] for b in _pack_weights(w)]
        return self._cache[group]

    def gather_chips(self, *groups):
        blocks = [b for group in groups for b in self._blocks(group)]
        return _gather_chips_plan(len(blocks)), blocks, _gather_shapes(blocks)

    def gather_sibling(self, got):
        half = list(got)
        return _gather_sibling_plan(len(half)), half, _same_shapes(half)

    def gather_start(self, group):
        blocks = self._blocks(group)
        plan = _gather_chips_plan(len(blocks))
        sems, blocks, lands, token = _plan_start(plan, blocks, _gather_shapes(blocks), name="gather_%s_start" % group)
        self._gathering = (group, plan, sems, blocks, lands)
        return token[0, 0]

    def gather_finish(self, after):
        group, plan, sems, blocks, lands = self._gathering
        _, half = _plan_wait(plan, sems, blocks, lands, after, name="gather_%s_wait" % group)
        return self.ffn_weights(_run_plan(_gather_sibling_plan(len(half)), half, _same_shapes(half), name="gather_%s_sibling" % group))

    def reduce_small(self, gs, dconv_w, loss_row):
        pieces = [_pad_cols(gs[n], SMALL_SLOTS[n]) for n in SMALL_NAMES[1:]] + [dconv_w.reshape(1, 3 * D), loss_row]
        self.small_total = _small_exchange(jnp.concatenate(pieces, axis=1).reshape(-1, 128), reduce=True,
                                           name="reduce_small").reshape(-1)

    def scatter_chips_start(self, group):
        s1 = self.partial[group]
        plan = _scatter_chips_plan(len(s1))
        sems, s1, lands, token = _plan_start(plan, s1, _scatter_shapes(s1), name="scatter_%s_start" % group)
        self._scattering[group] = (plan, sems, s1, lands)
        return token[0, 0]

    def scatter_chips_finish(self, group, after):
        plan, sems, s1, lands = self._scattering[group]
        self.partial[group], self.received[group] = _plan_wait(plan, sems, s1, lands, after, name="scatter_%s_wait" % group)

    def witness(self):
        parts = [b[:8, :128].astype(F32) for g in ("mix_in", "mix_misc", "ffn2") for b in self._blocks(g)]
        return functools.reduce(jnp.add, parts)

    def ffn_weights(self, got):
        return [a.reshape(DFF, D) for a in got]

    def mix_in_weights(self, got):
        return _unpack_in(got[0])

    def mix_misc_weights(self, got):
        return _unpack_misc(got[0])

    def _parts(self, group, grads):
        if group == "mix":
            return _pack_grads(grads), ["mix_in", "mix_misc"]
        parts = [g.reshape(N_DEV, -1, D) for g in grads]
        return parts, ([group] if len(parts) == 1 else None)

    def scatter_sibling(self, group, grads):
        self._sent, self._names = self._parts(group, grads)
        return _scatter_sibling_plan(len(self._sent)), self._sent, _halved_shapes(self._sent)

    def scatter_sibling_done(self, group, got):
        sums = list(_sum_sibling(self._sent, list(got), self.core, name="sum_%s_sibling" % group))
        if self._names is None:
            self.partial[group] = sums
        else:
            for n, s in zip(self._names, sums):
                self.partial[n] = [s]

    def scatter_sibling_now(self, group, grads):
        plan, parts, shapes = self.scatter_sibling(group, grads)
        self.scatter_sibling_done(group, _run_plan(plan, parts, shapes, name="scatter_%s_sibling" % group))

    def scatter_chips(self, group):
        s1 = self.partial[group]
        return _scatter_chips_plan(len(s1)), s1, _scatter_shapes(s1)

    def scatter_chips_done(self, group, got):
        self.received[group] = list(got)


SMALL_NAMES = ("ffn1_norm", "mix_norm", "gate_bias", "q_a_norm", "kv_a_norm", "q_head_norm", "k_head_norm", "ffn2_norm")
SMALL_SLOTS = {"ffn1_norm": 1024, "mix_norm": 1024, "gate_bias": 2048, "q_a_norm": 384, "kv_a_norm": 256, "q_head_norm": 128,
               "k_head_norm": 128, "ffn2_norm": 1024, "conv_w": 3072, "loss": 128}
COLUMN_MAJOR = ("w_in", "w_uq", "w_uk", "w_uv")
WEIGHT_NAMES = ("ffn1_norm", "ffn1_w_gate", "ffn1_w_up", "ffn1_w_down", "mix_norm", "w_in", "gate_bias", "q_a_norm", "w_uq",
                "kv_a_norm", "w_uk", "w_uv", "q_head_norm", "k_head_norm", "w_proj_attn", "conv_w", "w_proj_conv", "w_out",
                "ffn2_norm", "ffn2_w_gate", "ffn2_w_up", "ffn2_w_down")


def _step(x, positions, loss_target, w, m, v):
    xi, yi, ci = _place()
    core = ci.astype(jnp.int32).reshape(1)
    chip = (2 * xi + yi).astype(jnp.int32).reshape(1)
    me = 4 * xi + 2 * yi + ci

    ex = _MeshExchange(w, core, chip)
    cw_all = _small_exchange(jnp.pad(w["conv_w"], ((0, 5), (0, 0))), reduce=False, name="gather_conv_w")
    conv_w = cw_all[:, :3].transpose(1, 0, 2).reshape(3, D)
    ex.w = {n: (a + cw_all[0, 7, 0] if n.startswith("ffn1") else a) for n, a in w.items()}
    zero = ex.gather_start("ffn1")
    ex.w = {n: (a if n.startswith("ffn1") else a + zero) for n, a in w.items()}
    small = {n: w[n].reshape(1, -1) for n in SMALL_NAMES}

    grad_x, dffn1_norm = _local_step(x, positions + zero.astype(jnp.int32), loss_target, conv_w, small, ex)

    grads, deltas, new_m, new_v = {}, {}, {}, {}

    def ffn_update(group):
        for i, n in enumerate((group + "_w_gate", group + "_w_up", group + "_w_down")):
            transposed = not n.endswith("down")
            wv, mv, vv = (a[n].T if transposed else a[n] for a in (w, m, v))
            res = _sum_adamw(ex.partial[group][i], ex.received[group][i], chip, wv, mv, vv, name="adamw_" + n)
            grads[n], deltas[n], new_m[n], new_v[n] = (r.T if transposed else r for r in res)

    def update(n):
        shape = w[n].shape
        if n in COLUMN_MAJOR:
            ops = [a.T for a in (w[n], grads[n], m[n], v[n])]
            deltas[n], new_m[n], new_v[n] = (r.T for r in _adamw(*ops, name="adamw_" + n))
            return
        view = shape if len(shape) == 2 else ((-1, 128) if shape[0] % 128 == 0 else (1, shape[0]))
        dlt, nm, nv = _adamw(w[n].reshape(view), grads[n].reshape(view), m[n].reshape(view), v[n].reshape(view), name="adamw_" + n)
        deltas[n], new_m[n], new_v[n] = dlt.reshape(shape), nm.reshape(shape), nv.reshape(shape)

    ffn_update("ffn2")
    ex.scatter_chips_finish("mix_in", dffn1_norm)
    grads.update(_unpack_grads([_sum_chips(ex.partial[g][0], ex.received[g][0], chip, name="sum_%s_chips" % g)
                                for g in ("mix_in", "mix_misc")]))
    total, off = ex.small_total, 0
    for n in SMALL_NAMES[1:]:
        grads[n] = total[off:off + w[n].shape[0]]
        off += SMALL_SLOTS[n]
    conv_full = total[off:off + 3 * D].reshape(3, D)
    grads["conv_w"] = lax.dynamic_slice(conv_full, (0, me * HEAD_PAD), (3, HEAD_PAD))
    loss = total[off + 3 * D]
    tiny = SMALL_NAMES[1:] + ("conv_w",)
    views = {n: ((-1, 128) if w[n].size % 128 == 0 else (1, w[n].size)) for n in tiny}
    res = _adamw_small(*[[a[n].reshape(views[n]) for n in tiny] for a in (w, grads, m, v)], name="adamw_small")
    for out, arrs in zip((deltas, new_m, new_v), res):
        out.update({n: a.reshape(w[n].shape) for n, a in zip(tiny, arrs)})
    later = ("ffn1_norm", "ffn1_w_gate", "ffn1_w_up", "ffn1_w_down")
    for n in WEIGHT_NAMES:
        if n not in deltas and n not in later:
            update(n)

    done = [deltas[n][:8, :128] for n in ("ffn2_w_down", "w_in", "w_out", "w_proj_attn")] + [deltas["mix_norm"].reshape(8, 128)]
    ex.scatter_chips_finish("ffn1", functools.reduce(jnp.add, done) + grad_x.reshape(-1, D)[:8, :128])
    ffn_update("ffn1")
    last = dffn1_norm + 0.0 * grads["ffn1_w_down"][:1, :1]
    grads["ffn1_norm"] = _small_exchange(last.reshape(-1, 128), reduce=True, name="reduce_ffn1_norm").reshape(-1)
    update("ffn1_norm")
    return (loss, grad_x, *[grads[n] for n in WEIGHT_NAMES], *[deltas[n] for n in WEIGHT_NAMES],
            *[new_m[n] for n in WEIGHT_NAMES], *[new_v[n] for n in WEIGHT_NAMES])


def kernel(x, positions, ffn1_norm, ffn1_w_gate, ffn1_w_up, ffn1_w_down, mix_norm, w_in, gate_bias, q_a_norm, w_uq, kv_a_norm, w_uk, w_uv, q_head_norm, k_head_norm, w_proj_attn, conv_w, w_proj_conv, w_out, ffn2_norm, ffn2_w_gate, ffn2_w_up, ffn2_w_down, loss_target, m_ffn1_norm, m_ffn1_w_gate, m_ffn1_w_up, m_ffn1_w_down, m_mix_norm, m_w_in, m_gate_bias, m_q_a_norm, m_w_uq, m_kv_a_norm, m_w_uk, m_w_uv, m_q_head_norm, m_k_head_norm, m_w_proj_attn, m_conv_w, m_w_proj_conv, m_w_out, m_ffn2_norm, m_ffn2_w_gate, m_ffn2_w_up, m_ffn2_w_down, v_ffn1_norm, v_ffn1_w_gate, v_ffn1_w_up, v_ffn1_w_down, v_mix_norm, v_w_in, v_gate_bias, v_q_a_norm, v_w_uq, v_kv_a_norm, v_w_uk, v_w_uv, v_q_head_norm, v_k_head_norm, v_w_proj_attn, v_conv_w, v_w_proj_conv, v_w_out, v_ffn2_norm, v_ffn2_w_gate, v_ffn2_w_up, v_ffn2_w_down):
    given = dict(locals())
    w = {n: given[n] for n in WEIGHT_NAMES}
    m = {n: given["m_" + n] for n in WEIGHT_NAMES}
    v = {n: given["v_" + n] for n in WEIGHT_NAMES}
    return _step(x, positions, loss_target, w, m, v)
```

```python
import functools

import jax
import jax.numpy as jnp
from jax import lax
from jax.experimental import pallas as pl
from jax.experimental.pallas import tpu as pltpu

F32 = jnp.float32
BF16 = jnp.bfloat16
MESH = pl.DeviceIdType.MESH
ANY = pl.BlockSpec(memory_space=pl.ANY)

N_DEV = 8
D = 1024
DFF = 2816
N_HEADS = 8
HEAD_PAD = 128
QK_DIM = 96
NOPE = 64
ROPE_HALF = 16
Q_LORA = 384
KV_LORA = 256
LAT_PAD = 768
CONV_COLS = 3072
GATE_COLS = 2048
IN_DIM = 5792
IN_SHARD = IN_DIM // N_DEV
IN_SHARD_PAD = 736
FF_SHARD = DFF // N_DEV
ROPE_THETA = 10000.0
NORM_EPS = 1e-6
ATTN_SCALE = QK_DIM ** -0.5
NEG = -1e30

ADAM_LR, ADAM_B1, ADAM_B2, ADAM_EPS, ADAM_WD, ADAM_STEP = 0.001, 0.9, 0.999, 1e-08, 0.01, 10

PACK = ((("w_inT", IN_SHARD_PAD),), (("w_uq", 48), ("w_uk", 32), ("w_uv", 32), ("w_pa", 64), ("w_pc", 128), ("w_out", 128)))
PACK_OFF = {}
for _i, _group in enumerate(PACK):
    _o = 0
    for _n, _r in _group:
        PACK_OFF[_n] = (_i, _o, _r)
        _o += _r

VMEM_LIMIT = 56 * 1024 * 1024


def _params(*sem):
    return pltpu.CompilerParams(dimension_semantics=sem if sem else None, vmem_limit_bytes=VMEM_LIMIT)


class _Plan:
    def __init__(self, start, wait, n_remote, n_local, in_place=False):
        self.start, self.wait, self.n_remote, self.n_local, self.in_place = start, wait, n_remote, n_local, in_place

    def sems(self):
        return [pltpu.SemaphoreType.DMA((self.n_remote,)), pltpu.SemaphoreType.DMA((self.n_remote,)),
                pltpu.SemaphoreType.DMA((max(self.n_local, 1),))]


def _call(body, *, name, grid, in_specs, out_specs, out_shape, scratch_shapes, operands, sem, hosted=None):
    if hosted is None:
        outs = pl.pallas_call(body, name=name, grid=grid, in_specs=in_specs, out_specs=out_specs, out_shape=out_shape,
                              scratch_shapes=scratch_shapes, compiler_params=_params(*sem))(*operands)
        return outs, None
    plan, srcs, h_shapes = hosted
    n_in, n_out, n_scr, nh_in, nh_out = len(in_specs), len(out_specs), len(scratch_shapes), len(srcs), len(h_shapes)
    aliases = {n_in + a: n_out + a for a in range(nh_in)} if plan.in_place else {}

    def full_body(*refs):
        ins, refs = refs[:n_in], refs[n_in:]
        h_in, refs = refs[:nh_in], refs[nh_in:]
        outs, refs = refs[:n_out], refs[n_out:]
        h_out, refs = refs[:nh_out], refs[nh_out:]
        scr, sems = refs[:n_scr], refs[n_scr:]
        ids = [pl.program_id(ax) for ax in range(len(grid))]
        first = functools.reduce(jnp.logical_and, [i == 0 for i in ids])
        last = functools.reduce(jnp.logical_and, [i == g - 1 for i, g in zip(ids, grid)])

        @pl.when(first)
        def _():
            plan.start(h_in, h_out, *sems)

        body(*ins, *outs, *scr)

        @pl.when(last)
        def _():
            plan.wait(h_in, h_out, *sems)

    res = pl.pallas_call(
        full_body, name=name, grid=grid, in_specs=list(in_specs) + [ANY] * nh_in, out_specs=list(out_specs) + [ANY] * nh_out,
        out_shape=list(out_shape) + list(h_shapes), scratch_shapes=list(scratch_shapes) + plan.sems(),
        input_output_aliases=aliases, compiler_params=_params(*(["arbitrary"] * len(grid))),
    )(*operands, *srcs)
    return res[:n_out], res[n_out:]


def _dot_nn(a, b):
    return lax.dot_general(a, b, (((1,), (0,)), ((), ())), preferred_element_type=F32)


def _dot_nt(a, b):
    return lax.dot_general(a, b, (((1,), (1,)), ((), ())), preferred_element_type=F32)


def _dot_tn(a, b):
    return lax.dot_general(a, b, (((0,), (0,)), ((), ())), preferred_element_type=F32)


def _sigmoid(x):
    return 0.5 * jnp.tanh(0.5 * x) + 0.5


def _rms_stats(x):
    r = lax.rsqrt(jnp.mean(x * x, axis=-1, keepdims=True) + NORM_EPS)
    return x * r, r


ROWS_WIDE = 16
MM_ROWS = 256


def _rms_bwd(dy, xhat, r, g):
    dg = jnp.sum(dy * xhat, axis=0, keepdims=True)
    dxh = dy * g
    dx = r * (dxh - xhat * jnp.mean(dxh * xhat, axis=-1, keepdims=True))
    return dx, dg


def _mm(a, b, *, mode, out_dtype, tm, tn, tk, name, add=None, scale=1.0, hosted=None):
    if mode == "nn":
        (m, k), (_, n) = a.shape, b.shape
    elif mode == "nt":
        (m, k), (n, _) = a.shape, b.shape
    else:
        (k, m), (_, n) = a.shape, b.shape
    assert m % tm == 0 and n % tn == 0 and k % tk == 0, (name, m, n, k, tm, tn, tk)
    nk = k // tk
    dot = {"nn": _dot_nn, "nt": _dot_nt, "tn": _dot_tn}[mode]
    a_spec = pl.BlockSpec((tk, tm), lambda i, j, kk: (kk, i)) if mode == "tn" else pl.BlockSpec((tm, tk), lambda i, j, kk: (i, kk))
    b_spec = pl.BlockSpec((tn, tk), lambda i, j, kk: (j, kk)) if mode == "nt" else pl.BlockSpec((tk, tn), lambda i, j, kk: (kk, j))
    o_spec = pl.BlockSpec((tm, tn), lambda i, j, kk: (i, j))
    has_add = add is not None

    def finish(prod, c_ref, o_ref):
        if scale != 1.0:
            prod = prod * scale
        o_ref[...] = ((c_ref[...] + prod) if has_add else prod).astype(out_dtype)

    def body(*refs):
        a_ref, b_ref = refs[:2]
        c_ref = refs[2] if has_add else None
        o_ref = refs[3] if has_add else refs[2]
        if nk == 1:
            finish(dot(a_ref[...], b_ref[...]), c_ref, o_ref)
            return
        acc_ref = refs[-1]
        kk = pl.program_id(2)

        @pl.when(kk == 0)
        def _():
            acc_ref[...] = jnp.zeros_like(acc_ref)

        acc_ref[...] += dot(a_ref[...], b_ref[...])

        @pl.when(kk == nk - 1)
        def _():
            finish(acc_ref[...], c_ref, o_ref)

    operands = (a, b, add) if has_add else (a, b)
    in_specs = [a_spec, b_spec] + ([o_spec] if has_add else [])
    (out,), got = _call(
        body, name=name, grid=(m // tm, n // tn, nk), in_specs=in_specs, out_specs=[o_spec],
        out_shape=[jax.ShapeDtypeStruct((m, n), out_dtype)], scratch_shapes=[pltpu.VMEM((tm, tn), F32)] if nk > 1 else [],
        operands=operands, sem=("parallel", "parallel", "arbitrary"), hosted=hosted)
    return out if hosted is None else (out, got)


def _rms_fwd(x, g, *, tm, name, hosted=None):
    t, d = x.shape

    def body(x_ref, g_ref, h_ref):
        xhat, _ = _rms_stats(x_ref[...])
        h_ref[...] = (xhat * g_ref[...]).astype(BF16)

    (h,), got = _call(
        body, name=name, grid=(t // tm,),
        in_specs=[pl.BlockSpec((tm, d), lambda i: (i, 0)), pl.BlockSpec((1, d), lambda i: (0, 0))],
        out_specs=[pl.BlockSpec((tm, d), lambda i: (i, 0))], out_shape=[jax.ShapeDtypeStruct((t, d), BF16)], scratch_shapes=[],
        operands=(x, g), sem=("parallel",), hosted=hosted)
    return h, got


def _ffn_fwd(x, g, wgT, wuT, wd, *, tm, hc, name, hosted=None, target=None):
    t, d = x.shape
    nj = DFF // hc
    with_loss = target is not None

    def body(*refs):
        x_ref, g_ref, wg_ref, wu_ref, wd_ref = refs[:5]
        t_ref = refs[5] if with_loss else None
        xo_ref, h_ref, a_ref, b_ref = refs[5 + with_loss:9 + with_loss]
        loss_ref = refs[9 + with_loss] if with_loss else None
        acc_ref = refs[-1]
        i, j = pl.program_id(0), pl.program_id(1)

        @pl.when(j == 0)
        def _():
            xhat, _ = _rms_stats(x_ref[...])
            h_ref[...] = (xhat * g_ref[...]).astype(BF16)
            acc_ref[...] = jnp.zeros_like(acc_ref)

        h = h_ref[...]
        a = _dot_nt(h, wg_ref[...])
        b = _dot_nt(h, wu_ref[...])
        a_ref[...] = a.astype(BF16)
        b_ref[...] = b.astype(BF16)
        s = (a * _sigmoid(a) * b).astype(BF16)
        acc_ref[...] += _dot_nn(s, wd_ref[...])

        if with_loss:
            @pl.when((i == 0) & (j == 0))
            def _():
                loss_ref[...] = jnp.zeros_like(loss_ref)

        @pl.when(j == nj - 1)
        def _():
            y = x_ref[...] + 0.5 * acc_ref[...]
            if with_loss:
                err = y - t_ref[...]
                xo_ref[...] = err * (1.0 / d)
                loss_ref[...] += jnp.sum(jnp.sum(err * err, axis=-1, keepdims=True), axis=0, keepdims=True) * (0.5 / d)
            else:
                xo_ref[...] = y

    row = pl.BlockSpec((tm, d), lambda i, j: (i, 0))
    vec = pl.BlockSpec((1, d), lambda i, j: (0, 0))
    wsp = pl.BlockSpec((hc, d), lambda i, j: (j, 0))
    hid = pl.BlockSpec((tm, hc), lambda i, j: (i, j))
    out_specs = [row, row, hid, hid] + ([pl.BlockSpec((1, 128), lambda i, j: (0, 0))] if with_loss else [])
    out_shape = [jax.ShapeDtypeStruct((t, d), F32), jax.ShapeDtypeStruct((t, d), BF16), jax.ShapeDtypeStruct((t, DFF), BF16),
                 jax.ShapeDtypeStruct((t, DFF), BF16)] + ([jax.ShapeDtypeStruct((1, 128), F32)] if with_loss else [])
    return _call(
        body, name=name, grid=(t // tm, nj), in_specs=[row, vec, wsp, wsp, wsp] + ([row] if with_loss else []),
        out_specs=out_specs, out_shape=out_shape, scratch_shapes=[pltpu.VMEM((tm, d), F32)],
        operands=(x, g, wgT, wuT, wd) + ((target,) if with_loss else ()),
        sem=("arbitrary" if with_loss else "parallel", "arbitrary"), hosted=hosted)


def _ffn_grads(dout, h, a, b, wd, *, tm, hc, name, hosted=None):
    t, d = dout.shape
    ni, nj = t // tm, DFF // hc

    def body(dout_ref, h_ref, a_ref, b_ref, wd_ref, da_ref, db_ref, dwg_ref, dwu_ref, dwd_ref,
             dy_all, h_all, ds_scr, s_scr, acc_g, acc_u, acc_d):
        j, i = pl.program_id(0), pl.program_id(1)
        rows_i = pl.ds(pl.multiple_of(i * tm, tm), tm)

        @pl.when(j == 0)
        def _():
            dy_all[rows_i, :] = (0.5 * dout_ref[...]).astype(BF16)
            h_all[rows_i, :] = h_ref[...]

        @pl.when(i == 0)
        def _():
            acc_g[...] = jnp.zeros_like(acc_g)
            acc_u[...] = jnp.zeros_like(acc_u)
            acc_d[...] = jnp.zeros_like(acc_d)

        def grad_rows(rows):
            ds = ds_scr[rows, :]
            av = a_ref[rows, :].astype(F32)
            bv = b_ref[rows, :].astype(F32)
            sg = _sigmoid(av)
            sl = av * sg
            s_scr[rows, :] = (sl * bv).astype(BF16)
            da_ref[rows, :] = (ds * bv * (sg + sl * (1.0 - sg))).astype(BF16)
            db_ref[rows, :] = (ds * sl).astype(BF16)

        for blk in range(tm // MM_ROWS):
            rs = slice(blk * MM_ROWS, (blk + 1) * MM_ROWS)
            ds_scr[rs, :] = _dot_nt(dy_all[pl.ds(pl.multiple_of(i * tm + blk * MM_ROWS, MM_ROWS), MM_ROWS), :], wd_ref[...])
            for c in range(MM_ROWS // ROWS_WIDE):
                grad_rows(slice(blk * MM_ROWS + c * ROWS_WIDE, blk * MM_ROWS + (c + 1) * ROWS_WIDE))

        dy_i = dy_all[rows_i, :]
        h_i = h_all[rows_i, :]
        acc_d[...] += _dot_tn(s_scr[...], dy_i)
        acc_g[...] += _dot_tn(da_ref[...], h_i)
        acc_u[...] += _dot_tn(db_ref[...], h_i)

        @pl.when(i == ni - 1)
        def _():
            dwg_ref[...] = acc_g[...].astype(BF16)
            dwu_ref[...] = acc_u[...].astype(BF16)
            dwd_ref[...] = acc_d[...].astype(BF16)

    first = pl.BlockSpec((tm, d), lambda j, i: (jnp.where(j == 0, i, 0), 0))
    hid = pl.BlockSpec((tm, hc), lambda j, i: (i, j))
    wsp = pl.BlockSpec((hc, d), lambda j, i: (j, 0))
    hid_shape = jax.ShapeDtypeStruct((t, DFF), BF16)
    w_shape = jax.ShapeDtypeStruct((DFF, d), BF16)
    return _call(
        body, name=name, grid=(nj, ni), in_specs=[first, first, hid, hid, wsp], out_specs=[hid, hid, wsp, wsp, wsp],
        out_shape=[hid_shape, hid_shape, w_shape, w_shape, w_shape],
        scratch_shapes=[pltpu.VMEM((t, d), BF16), pltpu.VMEM((t, d), BF16), pltpu.VMEM((tm, hc), F32), pltpu.VMEM((tm, hc), BF16),
                        pltpu.VMEM((hc, d), F32), pltpu.VMEM((hc, d), F32), pltpu.VMEM((hc, d), F32)],
        operands=(dout, h, a, b, wd), sem=("arbitrary", "arbitrary"), hosted=hosted)


def _proj_fwd(h, latT, convT, gateT, *, tm, name, hosted=None):
    t, d = h.shape

    def body(h_ref, wl_ref, wc_ref, wg_ref, lat_ref, conv_ref, gl_ref):
        hv = h_ref[...]
        lat_ref[...] = _dot_nt(hv, wl_ref[...]).astype(BF16)
        conv_ref[...] = _dot_nt(hv, wc_ref[...]).astype(BF16)
        gl_ref[...] = _dot_nt(hv, wg_ref[...]).astype(BF16)

    def rows(w):
        return pl.BlockSpec((tm, w), lambda i: (i, 0))

    def full(r):
        return pl.BlockSpec((r, d), lambda i: (0, 0))

    return _call(
        body, name=name, grid=(t // tm,), in_specs=[rows(d), full(LAT_PAD), full(CONV_COLS), full(GATE_COLS)],
        out_specs=[rows(LAT_PAD), rows(CONV_COLS), rows(GATE_COLS)],
        out_shape=[jax.ShapeDtypeStruct((t, LAT_PAD), BF16), jax.ShapeDtypeStruct((t, CONV_COLS), BF16),
                   jax.ShapeDtypeStruct((t, GATE_COLS), BF16)],
        scratch_shapes=[], operands=(h, latT, convT, gateT), sem=("parallel",), hosted=hosted)


def _proj_bwd(dlat, dconv3, dgl, latT, convT, gateT, x, g, dres, *, tm, name, hosted=None):
    t, d = x.shape

    def body(dl_ref, dc_ref, dg_ref, wl_ref, wc_ref, wg_ref, x_ref, g_ref, dres_ref, dx_ref, dgain_ref):
        @pl.when(pl.program_id(0) == 0)
        def _():
            dgain_ref[...] = jnp.zeros_like(dgain_ref)

        dh = _dot_nn(dl_ref[...], wl_ref[...]) + _dot_nn(dc_ref[...], wc_ref[...]) + _dot_nn(dg_ref[...], wg_ref[...])
        xhat, r = _rms_stats(x_ref[...])
        dx, dgain = _rms_bwd(dh, xhat, r, g_ref[...])
        dx_ref[...] = dres_ref[...] + dx
        dgain_ref[...] += dgain

    def rows(w):
        return pl.BlockSpec((tm, w), lambda i: (i, 0))

    def full(r):
        return pl.BlockSpec((r, d), lambda i: (0, 0))

    return _call(
        body, name=name, grid=(t // tm,),
        in_specs=[rows(LAT_PAD), rows(CONV_COLS), rows(GATE_COLS), full(LAT_PAD), full(CONV_COLS), full(GATE_COLS), rows(d), full(1), rows(d)],
        out_specs=[rows(d), full(1)], out_shape=[jax.ShapeDtypeStruct((t, d), F32), jax.ShapeDtypeStruct((1, d), F32)],
        scratch_shapes=[], operands=(dlat, dconv3, dgl, latT, convT, gateT, x, g, dres), sem=("arbitrary",), hosted=hosted)


def _ffn_up_bwd(da, db, wgT, wuT, x, g, dout, *, tm, name, hosted=None):
    t, d = x.shape

    def body(da_ref, db_ref, wg_ref, wu_ref, x_ref, g_ref, dout_ref, dx_ref, dg_ref):
        @pl.when(pl.program_id(0) == 0)
        def _():
            dg_ref[...] = jnp.zeros_like(dg_ref)

        dh = _dot_nn(da_ref[...], wg_ref[...]) + _dot_nn(db_ref[...], wu_ref[...])
        xhat, r = _rms_stats(x_ref[...])
        dx, dg = _rms_bwd(dh, xhat, r, g_ref[...])
        dx_ref[...] = dout_ref[...] + dx
        dg_ref[...] += dg

    row = pl.BlockSpec((tm, d), lambda i: (i, 0))
    vec = pl.BlockSpec((1, d), lambda i: (0, 0))
    hid = pl.BlockSpec((tm, DFF), lambda i: (i, 0))
    wsp = pl.BlockSpec((DFF, d), lambda i: (0, 0))
    return _call(
        body, name=name, grid=(t // tm,), in_specs=[hid, hid, wsp, wsp, row, vec, row], out_specs=[row, vec],
        out_shape=[jax.ShapeDtypeStruct((t, d), F32), jax.ShapeDtypeStruct((1, d), F32)], scratch_shapes=[],
        operands=(da, db, wgT, wuT, x, g, dout), sem=("arbitrary",), hosted=hosted)


HEAD_LANES = (slice(0, 32), slice(64, 80), None, slice(32, 64), slice(80, 96), None)


def _head_cols(a):
    def part(sl, width):
        if sl is None or sl.stop > a.shape[1]:
            return jnp.zeros((a.shape[0], width), a.dtype)
        return a[:, sl]

    return jnp.concatenate([part(sl, w) for sl, w in zip(HEAD_LANES, (32, 16, 16, 32, 16, 16))], axis=1)


def _head_cols_inv(a, dims):
    parts = [a[:, 0:32], a[:, 64:96]] + ([a[:, 32:48], a[:, 96:112]] if dims == QK_DIM else [])
    return jnp.concatenate(parts, axis=1)


def _rope_fwd(x, c, s):
    return x * c + pltpu.roll(x, HEAD_PAD // 2, 1) * s


def _rope_bwd(dy, c, s):
    return dy * c + pltpu.roll(dy * s, HEAD_PAD // 2, 1)


def _head_stats(x):
    r = lax.rsqrt(jnp.sum(x * x, axis=-1, keepdims=True) * (1.0 / QK_DIM) + NORM_EPS)
    return x * r, r


def _mla_prep_fwd(lat, gq, gkv, ghq, ghk, wq, wk, wv, rc, rs, *, tm, name):
    t = lat.shape[0]

    def body(lat_ref, gq_ref, gkv_ref, ghq_ref, ghk_ref, wq_ref, wk_ref, wv_ref, c_ref, s_ref,
             q_ref, k_ref, v_ref, qn_ref, ckv_ref):
        lat_v = lat_ref[...]
        qhat, _ = _rms_stats(lat_v[:, :Q_LORA].astype(F32))
        qn = (qhat * gq_ref[...]).astype(BF16)
        khat, _ = _rms_stats(lat_v[:, Q_LORA:Q_LORA + KV_LORA].astype(F32))
        ckv = (khat * gkv_ref[...]).astype(BF16)
        ckv_ext = jnp.concatenate([ckv, lat_v[:, Q_LORA + KV_LORA:]], axis=1)
        qn_ref[...] = qn
        ckv_ref[...] = ckv_ext
        q_pre = _dot_nn(qn, wq_ref[...])
        k_pre = _dot_nn(ckv_ext, wk_ref[...])
        v_ref[...] = _dot_nn(ckv, wv_ref[...]).astype(BF16)
        c, s = c_ref[...], s_ref[...]
        for h in range(N_HEADS):
            hs = slice(h * HEAD_PAD, (h + 1) * HEAD_PAD)
            xq, _ = _head_stats(q_pre[:, hs])
            q_ref[:, hs] = _rope_fwd(xq * ghq_ref[...], c, s).astype(BF16)
            xk, _ = _head_stats(k_pre[:, hs])
            k_ref[:, hs] = _rope_fwd(xk * ghk_ref[...], c, s).astype(BF16)

    def row(w):
        return pl.BlockSpec((tm, w), lambda i: (i, 0))

    def full(r, w):
        return pl.BlockSpec((r, w), lambda i: (0, 0))

    wide = jax.ShapeDtypeStruct((t, D), BF16)
    lat3 = jax.ShapeDtypeStruct((t, Q_LORA), BF16)
    return pl.pallas_call(
        body, name=name, grid=(t // tm,),
        in_specs=[row(LAT_PAD), full(1, Q_LORA), full(1, KV_LORA), full(1, HEAD_PAD), full(1, HEAD_PAD),
                  full(Q_LORA, D), full(Q_LORA, D), full(KV_LORA, D), row(HEAD_PAD), row(HEAD_PAD)],
        out_specs=[row(D), row(D), row(D), row(Q_LORA), row(Q_LORA)],
        out_shape=[wide, wide, wide, lat3, lat3],
        compiler_params=_params("parallel"),
    )(lat, gq, gkv, ghq, ghk, wq, wk, wv, rc, rs)


def _mla_prep_bwd(dq, dk, dv, lat, qn, ckv_ext, gq, gkv, ghq, ghk, wq, wk, wv, rc, rs, *, tm, name):
    t = lat.shape[0]

    def body(dq_ref, dk_ref, dv_ref, lat_ref, qn_ref, ckv_ref, gq_ref, gkv_ref, ghq_ref, ghk_ref, wq_ref, wk_ref, wv_ref,
             c_ref, s_ref, dlat_ref, dqp_ref, dkp_ref, dgq_ref, dgkv_ref, dghq_ref, dghk_ref):
        @pl.when(pl.program_id(0) == 0)
        def _():
            dgq_ref[...] = jnp.zeros_like(dgq_ref)
            dgkv_ref[...] = jnp.zeros_like(dgkv_ref)
            dghq_ref[...] = jnp.zeros_like(dghq_ref)
            dghk_ref[...] = jnp.zeros_like(dghk_ref)

        c, s = c_ref[...], s_ref[...]
        q_pre = _dot_nn(qn_ref[...], wq_ref[...])
        k_pre = _dot_nn(ckv_ref[...], wk_ref[...])

        def heads(pre, dy_ref, gh_ref, dgh_ref, out_ref):
            dgh = jnp.zeros((1, HEAD_PAD), F32)
            for h in range(N_HEADS):
                hs = slice(h * HEAD_PAD, (h + 1) * HEAD_PAD)
                d = _rope_bwd(dy_ref[:, hs].astype(F32), c, s)
                xhat, r = _head_stats(pre[:, hs])
                dgh = dgh + jnp.sum(d * xhat, axis=0, keepdims=True)
                dxh = d * gh_ref[...]
                dx = r * (dxh - xhat * (jnp.sum(dxh * xhat, axis=-1, keepdims=True) * (1.0 / QK_DIM)))
                out_ref[:, hs] = dx.astype(BF16)
            dgh_ref[...] += dgh

        heads(q_pre, dq_ref, ghq_ref, dghq_ref, dqp_ref)
        heads(k_pre, dk_ref, ghk_ref, dghk_ref, dkp_ref)
        dqn = _dot_nt(dqp_ref[...], wq_ref[...])
        dce = _dot_nt(dkp_ref[...], wk_ref[...])
        dckv = dce[:, :KV_LORA] + _dot_nt(dv_ref[...], wv_ref[...])
        lat_v = lat_ref[...]
        qhat, rq = _rms_stats(lat_v[:, :Q_LORA].astype(F32))
        dql, dgq = _rms_bwd(dqn, qhat, rq, gq_ref[...])
        khat, rk = _rms_stats(lat_v[:, Q_LORA:Q_LORA + KV_LORA].astype(F32))
        dkl, dgkv = _rms_bwd(dckv, khat, rk, gkv_ref[...])
        dgq_ref[...] += dgq
        dgkv_ref[...] += dgkv
        dlat_ref[...] = jnp.concatenate([dql, dkl, dce[:, KV_LORA:]], axis=1).astype(BF16)

    def row(w):
        return pl.BlockSpec((tm, w), lambda i: (i, 0))

    def full(r, w):
        return pl.BlockSpec((r, w), lambda i: (0, 0))

    return pl.pallas_call(
        body, name=name, grid=(t // tm,),
        in_specs=[row(D), row(D), row(D), row(LAT_PAD), row(Q_LORA), row(Q_LORA), full(1, Q_LORA), full(1, KV_LORA),
                  full(1, HEAD_PAD), full(1, HEAD_PAD), full(Q_LORA, D), full(Q_LORA, D), full(KV_LORA, D),
                  row(HEAD_PAD), row(HEAD_PAD)],
        out_specs=[row(LAT_PAD), row(D), row(D), full(1, Q_LORA), full(1, KV_LORA), full(1, HEAD_PAD), full(1, HEAD_PAD)],
        out_shape=[jax.ShapeDtypeStruct((t, LAT_PAD), BF16), jax.ShapeDtypeStruct((t, D), BF16), jax.ShapeDtypeStruct((t, D), BF16),
                   jax.ShapeDtypeStruct((1, Q_LORA), F32), jax.ShapeDtypeStruct((1, KV_LORA), F32),
                   jax.ShapeDtypeStruct((1, HEAD_PAD), F32), jax.ShapeDtypeStruct((1, HEAD_PAD), F32)],
        compiler_params=_params("arbitrary"),
    )(dq, dk, dv, lat, qn, ckv_ext, gq, gkv, ghq, ghk, wq, wk, wv, rc, rs)


def _causal_keep(tq):
    r = lax.broadcasted_iota(jnp.int32, (tq, tq), 0)
    c = lax.broadcasted_iota(jnp.int32, (tq, tq), 1)
    return c <= r


def _flash_fwd(q, k, v, *, n_seq, seq, tq, name, hosted=None):
    nq = seq // tq

    def body(q_ref, k_ref, v_ref, o_ref, lse_ref):
        for qi in range(nq):
            rows = slice(qi * tq, (qi + 1) * tq)
            qv = q_ref[rows, :]
            m = jnp.full((tq, 1), NEG, F32)
            l = jnp.zeros((tq, 1), F32)
            acc = jnp.zeros((tq, HEAD_PAD), F32)
            for j in range(qi + 1):
                cols = slice(j * tq, (j + 1) * tq)
                s = _dot_nt(qv, k_ref[cols, :]) * ATTN_SCALE
                if j == qi:
                    s = jnp.where(_causal_keep(tq), s, NEG)
                m_new = jnp.maximum(m, jnp.max(s, axis=-1, keepdims=True))
                alpha = jnp.exp(m - m_new)
                p = jnp.exp(s - m_new)
                l = alpha * l + jnp.sum(p, axis=-1, keepdims=True)
                acc = alpha * acc + _dot_nn(p.astype(BF16), v_ref[cols, :])
                m = m_new
            o_ref[rows, :] = (acc / l).astype(BF16)
            lse_ref[rows, :] = jnp.broadcast_to(m + jnp.log(l), (tq, HEAD_PAD))

    spec = pl.BlockSpec((seq, HEAD_PAD), lambda b, h: (b, h))
    t = n_seq * seq
    return _call(
        body, name=name, grid=(n_seq, N_HEADS), in_specs=[spec, spec, spec], out_specs=[spec, spec],
        out_shape=[jax.ShapeDtypeStruct((t, D), BF16), jax.ShapeDtypeStruct((t, D), F32)], scratch_shapes=[],
        operands=(q, k, v), sem=("parallel", "parallel"), hosted=hosted)


def _flash_bwd(q, k, v, o, lse, do, *, n_seq, seq, tq, name, hosted=None):
    nq = seq // tq

    def body(q_ref, k_ref, v_ref, o_ref, lse_ref, do_ref, dq_ref, dk_ref, dv_ref, dk_acc, dv_acc):
        j = pl.program_id(2)

        @pl.when(j == 0)
        def _():
            dq_ref[...] = jnp.zeros_like(dq_ref)

        dk_acc[...] = jnp.zeros_like(dk_acc)
        dv_acc[...] = jnp.zeros_like(dv_acc)
        kv = k_ref[...]
        vv = v_ref[...]

        def step(i, masked):
            rows = pl.ds(pl.multiple_of(i * tq, tq), tq)
            qi = q_ref[rows, :]
            doi = do_ref[rows, :]
            delta = jnp.sum(doi.astype(F32) * o_ref[rows, :].astype(F32), axis=-1, keepdims=True)
            s = _dot_nt(qi, kv) * ATTN_SCALE
            p = jnp.exp(s - lse_ref[rows, :][:, :1])
            if masked:
                p = jnp.where(_causal_keep(tq), p, 0.0)
            dv_acc[...] += _dot_tn(p.astype(BF16), doi)
            dp = _dot_nt(doi, vv)
            ds = (p * (dp - delta) * ATTN_SCALE).astype(BF16)
            dk_acc[...] += _dot_tn(ds, qi)
            dq_ref[rows, :] += _dot_nn(ds, kv)

        step(j, True)

        def loop_body(i, carry):
            step(i, False)
            return carry

        lax.fori_loop(j + 1, nq, loop_body, 0)
        dk_ref[...] = dk_acc[...]
        dv_ref[...] = dv_acc[...].astype(BF16)

    full = pl.BlockSpec((seq, HEAD_PAD), lambda b, h, j: (b, h))
    tile = pl.BlockSpec((tq, HEAD_PAD), lambda b, h, j: (b * nq + j, h))
    t = n_seq * seq
    return _call(
        body, name=name, grid=(n_seq, N_HEADS, nq), in_specs=[full, tile, tile, full, full, full],
        out_specs=[full, tile, tile],
        out_shape=[jax.ShapeDtypeStruct((t, D), F32), jax.ShapeDtypeStruct((t, D), F32), jax.ShapeDtypeStruct((t, D), BF16)],
        scratch_shapes=[pltpu.VMEM((tq, HEAD_PAD), F32), pltpu.VMEM((tq, HEAD_PAD), F32)],
        operands=(q, k, v, o, lse, do), sem=("parallel", "parallel", "arbitrary"), hosted=hosted)


CONV_CB = 256


def _shift_down(u, k, row):
    return jnp.where(row >= k, pltpu.roll(u, k, 0), 0.0)


def _shift_up(u, k, row, n):
    return jnp.where(row < n - k, pltpu.roll(u, n - k, 0), 0.0)


def _conv_fwd(conv3, cw, *, n_seq, seq, name, hosted=None):
    def body(c_ref, w_ref, p_ref):
        blk = c_ref[...].astype(F32)
        xc, gb, gc = blk[:, :CONV_CB], blk[:, CONV_CB:2 * CONV_CB], blk[:, 2 * CONV_CB:]
        row = lax.broadcasted_iota(jnp.int32, (seq, CONV_CB), 0)
        u = gc * xc
        z = w_ref[0:1, :] * _shift_down(u, 2, row) + w_ref[1:2, :] * _shift_down(u, 1, row) + w_ref[2:3, :] * u
        p_ref[...] = (gb * z).astype(BF16)

    (p,), got = _call(
        body, name=name, grid=(n_seq, D // CONV_CB),
        in_specs=[pl.BlockSpec((seq, 3 * CONV_CB), lambda b, j: (b, j)), pl.BlockSpec((3, CONV_CB), lambda b, j: (0, j))],
        out_specs=[pl.BlockSpec((seq, CONV_CB), lambda b, j: (b, j))],
        out_shape=[jax.ShapeDtypeStruct((n_seq * seq, D), BF16)], scratch_shapes=[],
        operands=(conv3, cw), sem=("parallel", "parallel"), hosted=hosted)
    return p, got


def _conv_bwd(dp, conv3, cw, *, n_seq, seq, name):
    def body(dp_ref, c_ref, w_ref, dc_ref, dw_ref):
        @pl.when(pl.program_id(1) == 0)
        def _():
            dw_ref[...] = jnp.zeros_like(dw_ref)

        blk = c_ref[...].astype(F32)
        xc, gb, gc = blk[:, :CONV_CB], blk[:, CONV_CB:2 * CONV_CB], blk[:, 2 * CONV_CB:]
        row = lax.broadcasted_iota(jnp.int32, (seq, CONV_CB), 0)
        w0, w1, w2 = w_ref[0:1, :], w_ref[1:2, :], w_ref[2:3, :]
        u = gc * xc
        u1 = _shift_down(u, 1, row)
        u2 = _shift_down(u, 2, row)
        z = w0 * u2 + w1 * u1 + w2 * u
        dpv = dp_ref[...].astype(F32)
        dz = dpv * gb
        du = w2 * dz + w1 * _shift_up(dz, 1, row, seq) + w0 * _shift_up(dz, 2, row, seq)
        dc_ref[...] = jnp.concatenate([du * gc, dpv * z, du * xc], axis=1).astype(BF16)
        dw_ref[0:1, :] += jnp.sum(dz * u2, axis=0, keepdims=True)
        dw_ref[1:2, :] += jnp.sum(dz * u1, axis=0, keepdims=True)
        dw_ref[2:3, :] += jnp.sum(dz * u, axis=0, keepdims=True)

    return pl.pallas_call(
        body, name=name, grid=(D // CONV_CB, n_seq),
        in_specs=[pl.BlockSpec((seq, CONV_CB), lambda j, b: (b, j)), pl.BlockSpec((seq, 3 * CONV_CB), lambda j, b: (b, j)),
                  pl.BlockSpec((3, CONV_CB), lambda j, b: (0, j))],
        out_specs=[pl.BlockSpec((seq, 3 * CONV_CB), lambda j, b: (b, j)), pl.BlockSpec((3, CONV_CB), lambda j, b: (0, j))],
        out_shape=[jax.ShapeDtypeStruct((n_seq * seq, CONV_COLS), BF16), jax.ShapeDtypeStruct((3, D), F32)],
        compiler_params=_params("parallel", "arbitrary"),
    )(dp, conv3, cw)


def _merge_fwd(o, p, gl, bias, x1, wpa, wpc, wout, *, tm, name, hosted=None):
    t = x1.shape[0]

    def body(o_ref, p_ref, gl_ref, b_ref, x_ref, wpa_ref, wpc_ref, wout_ref, x2_ref, mg_ref, ya_ref, yb_ref):
        ya = _dot_nn(o_ref[...], wpa_ref[...])
        yb = _dot_nn(p_ref[...], wpc_ref[...])
        gates = _sigmoid(gl_ref[...].astype(F32) + b_ref[...])
        merged = (gates[:, :D] * ya + gates[:, D:] * yb).astype(BF16)
        ya_ref[...] = ya.astype(BF16)
        yb_ref[...] = yb.astype(BF16)
        mg_ref[...] = merged
        x2_ref[...] = x_ref[...] + _dot_nn(merged, wout_ref[...])

    row = pl.BlockSpec((tm, D), lambda i: (i, 0))
    row2 = pl.BlockSpec((tm, GATE_COLS), lambda i: (i, 0))
    wsp = pl.BlockSpec((D, D), lambda i: (0, 0))
    wide = jax.ShapeDtypeStruct((t, D), BF16)
    return _call(
        body, name=name, grid=(t // tm,),
        in_specs=[row, row, row2, pl.BlockSpec((1, GATE_COLS), lambda i: (0, 0)), row, wsp, wsp, wsp],
        out_specs=[row, row, row, row], out_shape=[jax.ShapeDtypeStruct((t, D), F32), wide, wide, wide], scratch_shapes=[],
        operands=(o, p, gl, bias, x1, wpa, wpc, wout), sem=("parallel",), hosted=hosted)


def _merge_bwd(dx2, ya, yb, gl, bias, wpa, wpc, wout, *, tm, name, hosted=None):
    t = dx2.shape[0]

    def body(dx_ref, ya_ref, yb_ref, gl_ref, b_ref, wpa_ref, wpc_ref, wout_ref,
             dxb_ref, dya_ref, dyb_ref, dgl_ref, do_ref, dp_ref, db_ref):
        @pl.when(pl.program_id(0) == 0)
        def _():
            db_ref[...] = jnp.zeros_like(db_ref)

        dxb = dx_ref[...].astype(BF16)
        dxb_ref[...] = dxb
        dm = _dot_nt(dxb, wout_ref[...])
        gates = _sigmoid(gl_ref[...].astype(F32) + b_ref[...])
        ga, gb = gates[:, :D], gates[:, D:]
        dya = (dm * ga).astype(BF16)
        dyb = (dm * gb).astype(BF16)
        dya_ref[...] = dya
        dyb_ref[...] = dyb
        dgl = jnp.concatenate([dm * ya_ref[...].astype(F32) * ga * (1.0 - ga),
                               dm * yb_ref[...].astype(F32) * gb * (1.0 - gb)], axis=1)
        dgl_ref[...] = dgl.astype(BF16)
        db_ref[...] += jnp.sum(dgl, axis=0, keepdims=True)
        do_ref[...] = _dot_nt(dya, wpa_ref[...]).astype(BF16)
        dp_ref[...] = _dot_nt(dyb, wpc_ref[...]).astype(BF16)

    row = pl.BlockSpec((tm, D), lambda i: (i, 0))
    row2 = pl.BlockSpec((tm, GATE_COLS), lambda i: (i, 0))
    vec2 = pl.BlockSpec((1, GATE_COLS), lambda i: (0, 0))
    wsp = pl.BlockSpec((D, D), lambda i: (0, 0))
    wide = jax.ShapeDtypeStruct((t, D), BF16)
    return _call(
        body, name=name, grid=(t // tm,), in_specs=[row, row, row, row2, vec2, wsp, wsp, wsp],
        out_specs=[row, row, row, row2, row, row, vec2],
        out_shape=[wide, wide, wide, jax.ShapeDtypeStruct((t, GATE_COLS), BF16), wide, wide,
                   jax.ShapeDtypeStruct((1, GATE_COLS), F32)],
        scratch_shapes=[], operands=(dx2, ya, yb, gl, bias, wpa, wpc, wout), sem=("arbitrary",), hosted=hosted)


def _adamw_small(ws, gs, ms, vs, *, name):
    n = len(ws)
    c1 = 1.0 / (1.0 - ADAM_B1 ** ADAM_STEP)
    c2 = 1.0 / (1.0 - ADAM_B2 ** ADAM_STEP)

    def body(*refs):
        for i in range(n):
            w_ref, g_ref, m_ref, v_ref, d_ref, nm_ref, nv_ref = (refs[k * n + i] for k in range(7))
            gv = g_ref[...]
            nm = ADAM_B1 * m_ref[...] + (1.0 - ADAM_B1) * gv
            nv = ADAM_B2 * v_ref[...] + (1.0 - ADAM_B2) * (gv * gv)
            nm_ref[...] = nm
            nv_ref[...] = nv
            d_ref[...] = -ADAM_LR * ((nm * c1) / (jnp.sqrt(nv * c2) + ADAM_EPS) + ADAM_WD * w_ref[...])

    vm = pl.BlockSpec(memory_space=pltpu.VMEM)
    shapes = [jax.ShapeDtypeStruct(a.shape, F32) for a in ws]
    outs = pl.pallas_call(body, name=name, in_specs=[vm] * (4 * n), out_specs=[vm] * (3 * n), out_shape=shapes * 3)(*ws, *gs, *ms, *vs)
    return outs[:n], outs[n:2 * n], outs[2 * n:]


def _adamw(w, g, m, v, *, name):
    rows, cols = w.shape
    tr = max([c for c in range(8, 513, 8) if rows % c == 0], default=rows)
    c1 = 1.0 / (1.0 - ADAM_B1 ** ADAM_STEP)
    c2 = 1.0 / (1.0 - ADAM_B2 ** ADAM_STEP)

    def body(w_ref, g_ref, m_ref, v_ref, d_ref, nm_ref, nv_ref):
        gv = g_ref[...]
        nm = ADAM_B1 * m_ref[...] + (1.0 - ADAM_B1) * gv
        nv = ADAM_B2 * v_ref[...] + (1.0 - ADAM_B2) * (gv * gv)
        nm_ref[...] = nm
        nv_ref[...] = nv
        d_ref[...] = -ADAM_LR * ((nm * c1) / (jnp.sqrt(nv * c2) + ADAM_EPS) + ADAM_WD * w_ref[...])

    spec = pl.BlockSpec((tr, cols), lambda i: (i, 0))
    shp = jax.ShapeDtypeStruct((rows, cols), F32)
    return pl.pallas_call(
        body, name=name, grid=(rows // tr,), in_specs=[spec] * 4, out_specs=[spec] * 3, out_shape=[shp] * 3,
        compiler_params=_params("parallel"),
    )(w, g, m, v)


def _place():
    return lax.axis_index("x"), lax.axis_index("y"), lax.axis_index("c")


def _other_chips(x, y):
    return [(1 - x, y), (x, 1 - y), (1 - x, 1 - y)]


def _remote(src, dst, send, recv, dev):
    return pltpu.make_async_remote_copy(src_ref=src, dst_ref=dst, send_sem=send, recv_sem=recv, device_id=dev, device_id_type=MESH)


def _gather_chips_plan(n):
    def start(srcs, dsts, send, recv, local):
        x, y, cc = _place()
        me = 4 * x + 2 * y + cc
        for a in range(n):
            pltpu.make_async_copy(srcs[a], dsts[a].at[me], local.at[a]).start()
            for k, (px, py) in enumerate(_other_chips(x, y)):
                _remote(srcs[a], dsts[a].at[me], send.at[3 * a + k], recv.at[3 * a + k], (px, py, cc)).start()

    def wait(srcs, dsts, send, recv, local):
        x, y, cc = _place()
        me = 4 * x + 2 * y + cc
        for a in range(n):
            for k, (px, py) in enumerate(_other_chips(x, y)):
                _remote(srcs[a], dsts[a].at[4 * px + 2 * py + cc], send.at[3 * a + k], recv.at[3 * a + k], (px, py, cc)).wait_recv()
        for a in range(n):
            for k, (px, py) in enumerate(_other_chips(x, y)):
                _remote(srcs[a], dsts[a].at[me], send.at[3 * a + k], recv.at[3 * a + k], (px, py, cc)).wait_send()
            pltpu.make_async_copy(srcs[a], dsts[a].at[me], local.at[a]).wait()

    return _Plan(start, wait, 3 * n, n)


def _scatter_chips_plan(n):
    def start(srcs, dsts, send, recv, local):
        x, y, cc = _place()
        for a in range(n):
            for k, (px, py) in enumerate(_other_chips(x, y)):
                _remote(srcs[a].at[2 * px + py], dsts[a].at[k], send.at[3 * a + k], recv.at[3 * a + k], (px, py, cc)).start()

    def wait(srcs, dsts, send, recv, local):
        x, y, cc = _place()
        for a in range(n):
            for k, (px, py) in enumerate(_other_chips(x, y)):
                _remote(srcs[a].at[k], dsts[a].at[k], send.at[3 * a + k], recv.at[3 * a + k], (px, py, cc)).wait_recv()
        for a in range(n):
            for k, (px, py) in enumerate(_other_chips(x, y)):
                _remote(srcs[a].at[k], dsts[a].at[k], send.at[3 * a + k], recv.at[3 * a + k], (px, py, cc)).wait_send()

    return _Plan(start, wait, 3 * n, 0)


def _gather_shapes(blocks):
    return [jax.ShapeDtypeStruct((N_DEV,) + b.shape, b.dtype) for b in blocks]


def _scatter_shapes(parts):
    return [jax.ShapeDtypeStruct((3,) + p.shape[1:], p.dtype) for p in parts]


def _gather_sibling_plan(n):
    def start(srcs, dsts, send, recv, local):
        x, y, cc = _place()
        for a in range(n):
            for q in range(4):
                _remote(srcs[a].at[2 * q + cc], dsts[a].at[2 * q + cc], send.at[4 * a + q], recv.at[4 * a + q], (x, y, 1 - cc)).start()

    def wait(srcs, dsts, send, recv, local):
        x, y, cc = _place()
        for a in range(n):
            for q in range(4):
                _remote(srcs[a].at[2 * q + cc], dsts[a].at[2 * q + 1 - cc], send.at[4 * a + q], recv.at[4 * a + q],
                        (x, y, 1 - cc)).wait_recv()
        for a in range(n):
            for q in range(4):
                _remote(srcs[a].at[2 * q + cc], dsts[a].at[2 * q + cc], send.at[4 * a + q], recv.at[4 * a + q],
                        (x, y, 1 - cc)).wait_send()

    return _Plan(start, wait, 4 * n, 0, in_place=True)


def _scatter_sibling_plan(n):
    def start(srcs, dsts, send, recv, local):
        x, y, cc = _place()
        for a in range(n):
            for q in range(4):
                _remote(srcs[a].at[2 * q + 1 - cc], dsts[a].at[q], send.at[4 * a + q], recv.at[4 * a + q], (x, y, 1 - cc)).start()

    def wait(srcs, dsts, send, recv, local):
        x, y, cc = _place()
        for a in range(n):
            for q in range(4):
                _remote(srcs[a].at[q], dsts[a].at[q], send.at[4 * a + q], recv.at[4 * a + q], (x, y, 1 - cc)).wait_recv()
        for a in range(n):
            for q in range(4):
                _remote(srcs[a].at[q], dsts[a].at[q], send.at[4 * a + q], recv.at[4 * a + q], (x, y, 1 - cc)).wait_send()

    return _Plan(start, wait, 4 * n, 0)


def _same_shapes(arrs):
    return [jax.ShapeDtypeStruct(a.shape, a.dtype) for a in arrs]


def _halved_shapes(parts):
    return [jax.ShapeDtypeStruct((4,) + p.shape[1:], p.dtype) for p in parts]


def _run_plan(plan, srcs, out_shapes, *, name):
    n_in, n_out = len(srcs), len(out_shapes)

    def body(*refs):
        h_in, h_out, sems = refs[:n_in], refs[n_in:n_in + n_out], refs[n_in + n_out:]
        plan.start(h_in, h_out, *sems)
        plan.wait(h_in, h_out, *sems)

    return pl.pallas_call(body, name=name, in_specs=[ANY] * n_in, out_specs=[ANY] * n_out, out_shape=list(out_shapes),
                          input_output_aliases={a: a for a in range(n_in)} if plan.in_place else {},
                          scratch_shapes=plan.sems())(*srcs)


SEM = pl.BlockSpec(memory_space=pltpu.SEMAPHORE)
HBM = pl.BlockSpec(memory_space=pltpu.HBM)
SIDE_EFFECT = pltpu.CompilerParams(has_side_effects=pltpu.SideEffectType.DATAFLOW_SIDE_EFFECTING)


def _plan_start(plan, blocks, land_shapes, *, name):
    n = len(blocks)
    lands = [lax.empty(s.shape, s.dtype) for s in land_shapes]

    def body(*refs):
        srcs, sems, lands_out, token = refs[:n], refs[2 * n:2 * n + 3], refs[3 * n + 3:4 * n + 3], refs[4 * n + 3]
        plan.start(srcs, lands_out, *sems)
        token[...] = jnp.zeros_like(token)

    out_shape = ([s for s in plan.sems()] + [pltpu.HBM(b.shape, b.dtype) for b in blocks]
                 + [pltpu.HBM(l.shape, l.dtype) for l in lands] + [jax.ShapeDtypeStruct((8, 128), F32)])
    res = pl.pallas_call(
        body, name=name, in_specs=[HBM] * (2 * n), out_specs=[SEM] * 3 + [HBM] * (2 * n) + [pl.BlockSpec(memory_space=pltpu.VMEM)],
        out_shape=out_shape, input_output_aliases={a: 3 + a for a in range(2 * n)}, compiler_params=SIDE_EFFECT,
    )(*[pltpu.with_memory_space_constraint(a, pltpu.HBM) for a in list(blocks) + lands])
    return res[:3], res[3:3 + n], res[3 + n:3 + 2 * n], res[3 + 2 * n]


def _plan_wait(plan, sems, blocks, lands, after, *, name):
    n = len(blocks)

    def body(*refs):
        plan.wait(refs[:n], refs[n:2 * n], *refs[2 * n:2 * n + 3])

    res = pl.pallas_call(
        body, name=name, in_specs=[HBM] * (2 * n) + [SEM] * 3 + [ANY], out_specs=[HBM] * (2 * n),
        out_shape=[pltpu.HBM(a.shape, a.dtype) for a in list(blocks) + list(lands)],
        input_output_aliases={a: a for a in range(2 * n)}, compiler_params=SIDE_EFFECT,
    )(*blocks, *lands, *sems, after)
    return list(res[:n]), list(res[n:])


def _sum_sibling(ps, qs, core, *, name):
    n = len(ps)

    def body(core_ref, *refs):
        for p_ref, q_ref, o_ref in zip(refs[:n], refs[n:2 * n], refs[2 * n:]):
            o_ref[...] = (p_ref[...].astype(F32) + q_ref[...].astype(F32)).astype(BF16)

    def mine(p):
        return pl.BlockSpec((1,) + p.shape[1:], lambda ch, core_ref: (2 * ch + core_ref[0], 0, 0))

    def theirs(p):
        return pl.BlockSpec((1,) + p.shape[1:], lambda ch, core_ref: (ch, 0, 0))

    grid_spec = pltpu.PrefetchScalarGridSpec(
        num_scalar_prefetch=1, grid=(4,), in_specs=[mine(p) for p in ps] + [theirs(p) for p in ps], out_specs=[theirs(p) for p in ps])
    return pl.pallas_call(
        body, name=name, grid_spec=grid_spec, out_shape=[jax.ShapeDtypeStruct((4,) + p.shape[1:], BF16) for p in ps],
        compiler_params=_params("parallel"),
    )(core, *ps, *qs)


def _sum_chips(s1, r2, chip, *, name):
    _, r, c = s1.shape

    def body(chip_ref, s_ref, r_ref, o_ref):
        acc = s_ref[0].astype(F32)
        for k in range(3):
            acc = acc + r_ref[k].astype(F32)
        o_ref[...] = acc

    grid_spec = pltpu.PrefetchScalarGridSpec(
        num_scalar_prefetch=1, grid=(1,),
        in_specs=[pl.BlockSpec((1, r, c), lambda i, chip_ref: (chip_ref[0], 0, 0)),
                  pl.BlockSpec((3, r, c), lambda i, chip_ref: (0, 0, 0))],
        out_specs=pl.BlockSpec((r, c), lambda i, chip_ref: (0, 0)))
    return pl.pallas_call(
        body, name=name, grid_spec=grid_spec, out_shape=jax.ShapeDtypeStruct((r, c), F32),
        compiler_params=_params("arbitrary"),
    )(chip, s1, r2)


def _sum_adamw(s1, r2, chip, w, m, v, *, name):
    _, r, c = s1.shape
    c1 = 1.0 / (1.0 - ADAM_B1 ** ADAM_STEP)
    c2 = 1.0 / (1.0 - ADAM_B2 ** ADAM_STEP)

    def body(chip_ref, s_ref, r_ref, w_ref, m_ref, v_ref, g_ref, d_ref, nm_ref, nv_ref):
        gv = s_ref[0].astype(F32)
        for k in range(3):
            gv = gv + r_ref[k].astype(F32)
        g_ref[...] = gv
        nm = ADAM_B1 * m_ref[...] + (1.0 - ADAM_B1) * gv
        nv = ADAM_B2 * v_ref[...] + (1.0 - ADAM_B2) * (gv * gv)
        nm_ref[...] = nm
        nv_ref[...] = nv
        d_ref[...] = -ADAM_LR * ((nm * c1) / (jnp.sqrt(nv * c2) + ADAM_EPS) + ADAM_WD * w_ref[...])

    flat = pl.BlockSpec((r, c), lambda i, chip_ref: (0, 0))
    grid_spec = pltpu.PrefetchScalarGridSpec(
        num_scalar_prefetch=1, grid=(1,),
        in_specs=[pl.BlockSpec((1, r, c), lambda i, chip_ref: (chip_ref[0], 0, 0)),
                  pl.BlockSpec((3, r, c), lambda i, chip_ref: (0, 0, 0)), flat, flat, flat],
        out_specs=[flat] * 4)
    return pl.pallas_call(
        body, name=name, grid_spec=grid_spec, out_shape=[jax.ShapeDtypeStruct((r, c), F32)] * 4,
        compiler_params=_params("arbitrary"),
    )(chip, s1, r2, w, m, v)


def _small_exchange(v, *, reduce, name):
    r, c = v.shape

    def body(x_ref, o_ref, *rest):
        if reduce:
            buf_ref, send_sems, recv_sems = rest
        else:
            buf_ref = o_ref
            send_sems, recv_sems = rest
        x, y, cc = _place()
        me = 4 * x + 2 * y + cc

        def peer(k):
            return ((1 - x) if k & 4 else x, (1 - y) if k & 2 else y, (1 - cc) if k & 1 else cc)

        buf_ref[me] = x_ref[...]
        sends = []
        for k in range(1, N_DEV):
            cp = pltpu.make_async_remote_copy(src_ref=x_ref, dst_ref=buf_ref.at[me], send_sem=send_sems.at[k - 1],
                                              recv_sem=recv_sems.at[k - 1], device_id=peer(k), device_id_type=MESH)
            cp.start()
            sends.append(cp)
        for k in range(1, N_DEV):
            px, py, pc = peer(k)
            pltpu.make_async_remote_copy(src_ref=x_ref, dst_ref=buf_ref.at[4 * px + 2 * py + pc], send_sem=send_sems.at[k - 1],
                                         recv_sem=recv_sems.at[k - 1], device_id=peer(k), device_id_type=MESH).wait_recv()
        for cp in sends:
            cp.wait_send()
        if reduce:
            acc = buf_ref[0]
            for s in range(1, N_DEV):
                acc = acc + buf_ref[s]
            o_ref[...] = acc

    vm = pl.BlockSpec(memory_space=pltpu.VMEM)
    sems = [pltpu.SemaphoreType.DMA((N_DEV - 1,)), pltpu.SemaphoreType.DMA((N_DEV - 1,))]
    if reduce:
        out_shape, scratch = jax.ShapeDtypeStruct((r, c), F32), [pltpu.VMEM((N_DEV, r, c), F32)] + sems
    else:
        out_shape, scratch = jax.ShapeDtypeStruct((N_DEV, r, c), F32), sems
    return pl.pallas_call(body, name=name, in_specs=[vm], out_specs=vm, out_shape=out_shape, scratch_shapes=scratch)(v)


def _rows(a):
    return a.reshape(-1, D)


def _pad_cols(a, to):
    return jnp.pad(a, ((0, 0), (0, to - a.shape[1])))


def _pack_weights(w):
    parts = {
        "w_inT": jnp.pad(w["w_in"].T, ((0, IN_SHARD_PAD - IN_SHARD), (0, 0))),
        "w_uq": _rows(_head_cols(w["w_uq"])), "w_uk": _rows(_head_cols(w["w_uk"])),
        "w_uv": _rows(_pad_cols(w["w_uv"], HEAD_PAD)), "w_pa": _rows(w["w_proj_attn"]),
        "w_pc": w["w_proj_conv"], "w_out": w["w_out"],
    }
    return [jnp.concatenate([parts[n].astype(BF16) for n, _ in group], axis=0) for group in PACK]


def _cols_from_shards(gs, name, rows):
    idx, off, r = PACK_OFF[name]
    return gs[idx][:, off:off + r].reshape(N_DEV, rows, HEAD_PAD).transpose(1, 0, 2).reshape(rows, N_DEV * HEAD_PAD)


def _rows_from_shards(gs, name, keep=None):
    idx, off, r = PACK_OFF[name]
    keep = r if keep is None else keep
    return gs[idx][:, off:off + keep].reshape(N_DEV * keep, D)


def _rope_placement():
    i = lax.broadcasted_iota(jnp.int32, (HEAD_PAD, D), 0)
    j = lax.broadcasted_iota(jnp.int32, (HEAD_PAD, D), 1)
    lane = jnp.where(i < ROPE_HALF, 32 + i, 96 + i - ROPE_HALF)
    return ((i < 2 * ROPE_HALF) & (j % HEAD_PAD == lane)).astype(BF16)


def _unpack_in(g_in):
    w_inT = _rows_from_shards([g_in, None], "w_inT", IN_SHARD)
    lat_rows = Q_LORA + KV_LORA + 2 * ROPE_HALF
    conv = w_inT[lat_rows:lat_rows + CONV_COLS].reshape(3, D // CONV_CB, CONV_CB, D).transpose(1, 0, 2, 3).reshape(CONV_COLS, D)
    return {"latT": jnp.pad(w_inT[:lat_rows], ((0, LAT_PAD - lat_rows), (0, 0))), "convT": conv,
            "gateT": w_inT[lat_rows + CONV_COLS:]}


def _unpack_misc(g_misc):
    g = [None, g_misc]
    wpa = _cols_from_shards(g, "w_pa", 512).reshape(N_HEADS, NOPE, D)
    return {
        "wq": _cols_from_shards(g, "w_uq", Q_LORA),
        "wk": jnp.concatenate([_cols_from_shards(g, "w_uk", KV_LORA), _rope_placement()], axis=0),
        "wv": _cols_from_shards(g, "w_uv", KV_LORA),
        "wpa": jnp.pad(wpa, ((0, 0), (0, HEAD_PAD - NOPE), (0, 0))).reshape(D, D),
        "wpc": _rows_from_shards(g, "w_pc"), "wout": _rows_from_shards(g, "w_out"),
    }


def _shards_from_cols(a):
    rows = a.shape[0]
    return a.reshape(rows, N_DEV, HEAD_PAD).transpose(1, 0, 2).reshape(N_DEV, rows * HEAD_PAD // D, D)


def _pack_grads(gw):
    lat_rows = Q_LORA + KV_LORA + 2 * ROPE_HALF
    conv = gw["convT"].reshape(D // CONV_CB, 3, CONV_CB, D).transpose(1, 0, 2, 3).reshape(CONV_COLS, D)
    w_inT = jnp.concatenate([gw["latT"][:lat_rows], conv, gw["gateT"]], axis=0).reshape(N_DEV, IN_SHARD, D)
    wpa = gw["wpa"].reshape(N_HEADS, HEAD_PAD, D)[:, :NOPE].reshape(N_HEADS * NOPE, D)
    parts = {}
    parts.update({
        "w_inT": jnp.pad(w_inT, ((0, 0), (0, IN_SHARD_PAD - IN_SHARD), (0, 0))),
        "w_uq": _shards_from_cols(gw["wq"]), "w_uk": _shards_from_cols(gw["wk"][:KV_LORA]),
        "w_uv": _shards_from_cols(gw["wv"][:KV_LORA]), "w_pa": _shards_from_cols(wpa),
        "w_pc": gw["wpc"].reshape(N_DEV, D // N_DEV, D), "w_out": gw["wout"].reshape(N_DEV, D // N_DEV, D),
    })
    return [jnp.concatenate([parts[n] for n, _ in group], axis=1) for group in PACK]


def _unpack_grads(mines):
    def seg(name, keep=None):
        idx, off, r = PACK_OFF[name]
        return mines[idx][off:off + (r if keep is None else keep)]

    return {
        "w_in": seg("w_inT", IN_SHARD).T,
        "w_uq": _head_cols_inv(seg("w_uq").reshape(Q_LORA, HEAD_PAD), QK_DIM),
        "w_uk": _head_cols_inv(seg("w_uk").reshape(KV_LORA, HEAD_PAD), NOPE),
        "w_uv": seg("w_uv").reshape(KV_LORA, HEAD_PAD)[:, :NOPE],
        "w_proj_attn": seg("w_pa").reshape(512, HEAD_PAD),
        "w_proj_conv": seg("w_pc"), "w_out": seg("w_out"),
    }


def _rope_tables(positions):
    lane = jnp.arange(HEAD_PAD)
    idx = jnp.where((lane >= 32) & (lane < 48), lane - 32, jnp.where((lane >= 96) & (lane < 112), lane - 96, -1))
    inv_freq = jnp.where(idx >= 0, 1.0 / (ROPE_THETA ** (idx.astype(F32) / ROPE_HALF)), 0.0)
    ang = positions.reshape(-1).astype(F32)[:, None] * inv_freq
    return jnp.cos(ang), jnp.sin(ang) * jnp.where(lane < HEAD_PAD // 2, -1.0, 1.0)


def _local_step(x, positions, target, conv_w, small, ex):
    n_seq, seq, d = x.shape
    t = n_seq * seq
    x0 = x.reshape(t, d)
    tgt = target.reshape(t, d)
    rc, rs = _rope_tables(positions)
    ghq = _head_cols(small["q_head_norm"])
    ghk = _head_cols(small["k_head_norm"])
    TM, HC, TQ = 1024, 256, 1024

    def mm(*args, hosted=None, **kw):
        res = _mm(*args, hosted=hosted, **kw)
        return res if hosted is not None else (res, None)

    def wgrad(a, b, name, tm=None, hosted=None):
        tm = tm or a.shape[1]
        return mm(a, b, mode="tn", out_dtype=BF16, tm=tm, tn=b.shape[1], tk=2048 if tm <= D else 1024, name=name, hosted=hosted)

    f1g, f1u, f1d = ex.gather_finish(ex.witness() + rc[:8] + conv_w[:1, :HEAD_PAD])
    (x1, h1, a1, b1), got = _ffn_fwd(x0, small["ffn1_norm"], f1g, f1u, f1d, tm=512, hc=DFF // 2, name="ffn1_fwd",
                                     hosted=ex.gather_chips("mix_in"))
    hm, got = _rms_fwd(x1, small["mix_norm"], tm=TM, name="mix_norm_fwd", hosted=ex.gather_sibling(got))
    W = ex.mix_in_weights(got)
    (lat, conv3, gl), got = _proj_fwd(hm, W["latT"], W["convT"], W["gateT"], tm=512, name="proj_fwd",
                                      hosted=ex.gather_chips("mix_misc"))
    p, got = _conv_fwd(conv3, conv_w, n_seq=n_seq, seq=seq, name="conv_fwd", hosted=ex.gather_sibling(got))
    W.update(ex.mix_misc_weights(got))
    q, k, v, qn, ckv = _mla_prep_fwd(lat, small["q_a_norm"], small["kv_a_norm"], ghq, ghk, W["wq"], W["wk"], W["wv"], rc, rs,
                                     tm=512, name="mla_prep_fwd")
    (o, lse), got = _flash_fwd(q, k, v, n_seq=n_seq, seq=seq, tq=TQ, name="attn_fwd", hosted=ex.gather_chips("ffn2"))
    (x2, merged, ya, yb), got = _merge_fwd(o, p, gl, small["gate_bias"], x1, W["wpa"], W["wpc"], W["wout"], tm=512, name="merge_fwd",
                                           hosted=ex.gather_sibling(got))
    f2g, f2u, f2d = ex.ffn_weights(got)
    (dy, h2, a2, b2, loss_row), _ = _ffn_fwd(x2, small["ffn2_norm"], f2g, f2u, f2d, tm=512, hc=DFF // 2, name="ffn2_fwd", target=tgt)

    gw, gs = {}, {}
    (da2, db2, *ffn2_grads), _ = _ffn_grads(dy, h2, a2, b2, f2d, tm=TM, hc=HC, name="ffn2_grads")
    (dx2, gs["ffn2_norm"]), _ = _ffn_up_bwd(da2, db2, f2g, f2u, x2, small["ffn2_norm"], dy, tm=512, name="ffn2_up_bwd")

    (dx2b, dya, dyb, dgl, do, dp, gs["gate_bias"]), got = _merge_bwd(
        dx2, ya, yb, gl, small["gate_bias"], W["wpa"], W["wpc"], W["wout"], tm=512, name="merge_bwd",
        hosted=ex.scatter_sibling("ffn2", ffn2_grads))
    ex.scatter_sibling_done("ffn2", got)
    gw["wout"] = wgrad(merged, dx2b, "dw_out")[0]
    gw["wpa"] = wgrad(o, dya, "dw_pa")[0]
    gw["wpc"] = wgrad(p, dyb, "dw_pc")[0]
    dconv3, dconv_w = _conv_bwd(dp, conv3, conv_w, n_seq=n_seq, seq=seq, name="conv_bwd")
    (dq, dk, dv), got = _flash_bwd(q, k, v, o, lse, do, n_seq=n_seq, seq=seq, tq=TQ, name="attn_bwd",
                                   hosted=ex.scatter_chips("ffn2"))
    ex.scatter_chips_done("ffn2", got)
    dlat, dqp, dkp, gs["q_a_norm"], gs["kv_a_norm"], dghq, dghk = _mla_prep_bwd(
        dq, dk, dv, lat, qn, ckv, small["q_a_norm"], small["kv_a_norm"], ghq, ghk, W["wq"], W["wk"], W["wv"], rc, rs,
        tm=512, name="mla_prep_bwd")
    gs["q_head_norm"], gs["k_head_norm"] = _head_cols_inv(dghq, QK_DIM), _head_cols_inv(dghk, QK_DIM)
    gw["wq"] = wgrad(qn, dqp, "dw_uq")[0]
    gw["wk"] = wgrad(ckv, dkp, "dw_uk")[0]
    gw["wv"] = wgrad(ckv, dv, "dw_uv")[0]
    gw["convT"] = wgrad(dconv3, hm, "dw_conv", tm=CONV_COLS // 2)[0]
    gw["gateT"] = wgrad(dgl, hm, "dw_gate")[0]
    gw["latT"] = wgrad(dlat, hm, "dw_lat")[0]
    ex.scatter_sibling_now("mix", gw)
    zero = ex.scatter_chips_start("mix_in")
    (dx1, gs["mix_norm"]), _ = _proj_bwd(dlat, dconv3, dgl, W["latT"], W["convT"], W["gateT"], x1, small["mix_norm"] + zero, dx2,
                                         tm=512, name="proj_bwd")

    (da1, db1, *ffn1_grads), got = _ffn_grads(dx1, h1, a1, b1, f1d, tm=TM, hc=HC, name="ffn1_grads",
                                              hosted=ex.scatter_chips("mix_misc"))
    ex.scatter_chips_done("mix_misc", got)
    ex.reduce_small(gs, dconv_w, loss_row)
    ex.scatter_sibling_now("ffn1", ffn1_grads)
    zero = ex.scatter_chips_start("ffn1")
    (dx0, gs["ffn1_norm"]), _ = _ffn_up_bwd(da1, db1, f1g, f1u, x0, small["ffn1_norm"] + zero, dx1, tm=512, name="ffn1_up_bwd")
    return dx0.reshape(n_seq, seq, d), gs["ffn1_norm"]


class _MeshExchange:
    def __init__(self, w, core, chip):
        self.w, self.core, self.chip = w, core, chip
        self.partial, self.received, self._cache, self._scattering = {}, {}, {}, {}

    def _blocks(self, group):
        w = self.w
        if group not in self._cache:
            if group.startswith("ffn"):
                self._cache[group] = [w[group + "_w_gate"].T.astype(BF16), w[group + "_w_up"].T.astype(BF16),
                                      w[group + "_w_down"].astype(BF16)]
            else:
                self._cache["mix_in"], self._cache["mix_misc"] = [[b] for b in _pack_weights(w)]
        return self._cache[group]

    def gather_chips(self, *groups):
        blocks = [b for group in groups for b in self._blocks(group)]
        return _gather_chips_plan(len(blocks)), blocks, _gather_shapes(blocks)

    def gather_sibling(self, got):
        half = list(got)
        return _gather_sibling_plan(len(half)), half, _same_shapes(half)

    def gather_start(self, group):
        blocks = self._blocks(group)
        plan = _gather_chips_plan(len(blocks))
        sems, blocks, lands, token = _plan_start(plan, blocks, _gather_shapes(blocks), name="gather_%s_start" % group)
        self._gathering = (group, plan, sems, blocks, lands)
        return token[0, 0]

    def gather_finish(self, after):
        group, plan, sems, blocks, lands = self._gathering
        _, half = _plan_wait(plan, sems, blocks, lands, after, name="gather_%s_wait" % group)
        return self.ffn_weights(_run_plan(_gather_sibling_plan(len(half)), half, _same_shapes(half), name="gather_%s_sibling" % group))

    def reduce_small(self, gs, dconv_w, loss_row):
        pieces = [_pad_cols(gs[n], SMALL_SLOTS[n]) for n in SMALL_NAMES[1:]] + [dconv_w.reshape(1, 3 * D), loss_row]
        self.small_total = _small_exchange(jnp.concatenate(pieces, axis=1).reshape(-1, 128), reduce=True,
                                           name="reduce_small").reshape(-1)

    def scatter_chips_start(self, group):
        s1 = self.partial[group]
        plan = _scatter_chips_plan(len(s1))
        sems, s1, lands, token = _plan_start(plan, s1, _scatter_shapes(s1), name="scatter_%s_start" % group)
        self._scattering[group] = (plan, sems, s1, lands)
        return token[0, 0]

    def scatter_chips_finish(self, group, after):
        plan, sems, s1, lands = self._scattering[group]
        self.partial[group], self.received[group] = _plan_wait(plan, sems, s1, lands, after, name="scatter_%s_wait" % group)

    def witness(self):
        parts = [b[:8, :128].astype(F32) for g in ("mix_in", "mix_misc", "ffn2") for b in self._blocks(g)]
        return functools.reduce(jnp.add, parts)

    def ffn_weights(self, got):
        return [a.reshape(DFF, D) for a in got]

    def mix_in_weights(self, got):
        return _unpack_in(got[0])

    def mix_misc_weights(self, got):
        return _unpack_misc(got[0])

    def _parts(self, group, grads):
        if group == "mix":
            return _pack_grads(grads), ["mix_in", "mix_misc"]
        parts = [g.reshape(N_DEV, -1, D) for g in grads]
        return parts, ([group] if len(parts) == 1 else None)

    def scatter_sibling(self, group, grads):
        self._sent, self._names = self._parts(group, grads)
        return _scatter_sibling_plan(len(self._sent)), self._sent, _halved_shapes(self._sent)

    def scatter_sibling_done(self, group, got):
        sums = list(_sum_sibling(self._sent, list(got), self.core, name="sum_%s_sibling" % group))
        if self._names is None:
            self.partial[group] = sums
        else:
            for n, s in zip(self._names, sums):
                self.partial[n] = [s]

    def scatter_sibling_now(self, group, grads):
        plan, parts, shapes = self.scatter_sibling(group, grads)
        self.scatter_sibling_done(group, _run_plan(plan, parts, shapes, name="scatter_%s_sibling" % group))

    def scatter_chips(self, group):
        s1 = self.partial[group]
        return _scatter_chips_plan(len(s1)), s1, _scatter_shapes(s1)

    def scatter_chips_done(self, group, got):
        self.received[group] = list(got)


SMALL_NAMES = ("ffn1_norm", "mix_norm", "gate_bias", "q_a_norm", "kv_a_norm", "q_head_norm", "k_head_norm", "ffn2_norm")
SMALL_SLOTS = {"ffn1_norm": 1024, "mix_norm": 1024, "gate_bias": 2048, "q_a_norm": 384, "kv_a_norm": 256, "q_head_norm": 128,
               "k_head_norm": 128, "ffn2_norm": 1024, "conv_w": 3072, "loss": 128}
COLUMN_MAJOR = ("w_in", "w_uq", "w_uk", "w_uv")
WEIGHT_NAMES = ("ffn1_norm", "ffn1_w_gate", "ffn1_w_up", "ffn1_w_down", "mix_norm", "w_in", "gate_bias", "q_a_norm", "w_uq",
                "kv_a_norm", "w_uk", "w_uv", "q_head_norm", "k_head_norm", "w_proj_attn", "conv_w", "w_proj_conv", "w_out",
                "ffn2_norm", "ffn2_w_gate", "ffn2_w_up", "ffn2_w_down")


def _step(x, positions, loss_target, w, m, v):
    xi, yi, ci = _place()
    core = ci.astype(jnp.int32).reshape(1)
    chip = (2 * xi + yi).astype(jnp.int32).reshape(1)
    me = 4 * xi + 2 * yi + ci

    ex = _MeshExchange(w, core, chip)
    cw_all = _small_exchange(jnp.pad(w["conv_w"], ((0, 5), (0, 0))), reduce=False, name="gather_conv_w")
    conv_w = cw_all[:, :3].transpose(1, 0, 2).reshape(3, D)
    ex.w = {n: (a + cw_all[0, 7, 0] if n.startswith("ffn1") else a) for n, a in w.items()}
    zero = ex.gather_start("ffn1")
    ex.w = {n: (a if n.startswith("ffn1") else a + zero) for n, a in w.items()}
    small = {n: w[n].reshape(1, -1) for n in SMALL_NAMES}

    grad_x, dffn1_norm = _local_step(x, positions + zero.astype(jnp.int32), loss_target, conv_w, small, ex)

    grads, deltas, new_m, new_v = {}, {}, {}, {}

    def ffn_update(group):
        for i, n in enumerate((group + "_w_gate", group + "_w_up", group + "_w_down")):
            transposed = not n.endswith("down")
            wv, mv, vv = (a[n].T if transposed else a[n] for a in (w, m, v))
            res = _sum_adamw(ex.partial[group][i], ex.received[group][i], chip, wv, mv, vv, name="adamw_" + n)
            grads[n], deltas[n], new_m[n], new_v[n] = (r.T if transposed else r for r in res)

    def update(n):
        shape = w[n].shape
        if n in COLUMN_MAJOR:
            ops = [a.T for a in (w[n], grads[n], m[n], v[n])]
            deltas[n], new_m[n], new_v[n] = (r.T for r in _adamw(*ops, name="adamw_" + n))
            return
        view = shape if len(shape) == 2 else ((-1, 128) if shape[0] % 128 == 0 else (1, shape[0]))
        dlt, nm, nv = _adamw(w[n].reshape(view), grads[n].reshape(view), m[n].reshape(view), v[n].reshape(view), name="adamw_" + n)
        deltas[n], new_m[n], new_v[n] = dlt.reshape(shape), nm.reshape(shape), nv.reshape(shape)

    ffn_update("ffn2")
    ex.scatter_chips_finish("mix_in", dffn1_norm)
    grads.update(_unpack_grads([_sum_chips(ex.partial[g][0], ex.received[g][0], chip, name="sum_%s_chips" % g)
                                for g in ("mix_in", "mix_misc")]))
    total, off = ex.small_total, 0
    for n in SMALL_NAMES[1:]:
        grads[n] = total[off:off + w[n].shape[0]]
        off += SMALL_SLOTS[n]
    conv_full = total[off:off + 3 * D].reshape(3, D)
    grads["conv_w"] = lax.dynamic_slice(conv_full, (0, me * HEAD_PAD), (3, HEAD_PAD))
    loss = total[off + 3 * D]
    tiny = SMALL_NAMES[1:] + ("conv_w", "w_uq", "w_uk", "w_uv", "w_proj_attn", "w_proj_conv", "w_out")

    def view(n, a):
        if a.ndim == 2:
            return a.T if n in COLUMN_MAJOR else a
        return a.reshape((-1, 128) if a.size % 128 == 0 else (1, a.size))

    def unview(n, a):
        return (a.T if n in COLUMN_MAJOR else a) if w[n].ndim == 2 else a.reshape(w[n].shape)

    res = _adamw_small(*[[view(n, a[n]) for n in tiny] for a in (w, grads, m, v)], name="adamw_small")
    for out, arrs in zip((deltas, new_m, new_v), res):
        out.update({n: unview(n, a) for n, a in zip(tiny, arrs)})
    later = ("ffn1_norm", "ffn1_w_gate", "ffn1_w_up", "ffn1_w_down")
    for n in WEIGHT_NAMES:
        if n not in deltas and n not in later:
            update(n)

    done = [deltas[n][:8, :128] for n in ("ffn2_w_down", "w_in", "w_out", "w_proj_attn")] + [deltas["mix_norm"].reshape(8, 128)]
    ex.scatter_chips_finish("ffn1", functools.reduce(jnp.add, done) + grad_x.reshape(-1, D)[:8, :128])
    ffn_update("ffn1")
    last = dffn1_norm + 0.0 * grads["ffn1_w_down"][:1, :1]
    grads["ffn1_norm"] = _small_exchange(last.reshape(-1, 128), reduce=True, name="reduce_ffn1_norm").reshape(-1)
    update("ffn1_norm")
    return (loss, grad_x, *[grads[n] for n in WEIGHT_NAMES], *[deltas[n] for n in WEIGHT_NAMES],
            *[new_m[n] for n in WEIGHT_NAMES], *[new_v[n] for n in WEIGHT_NAMES])


def kernel(x, positions, ffn1_norm, ffn1_w_gate, ffn1_w_up, ffn1_w_down, mix_norm, w_in, gate_bias, q_a_norm, w_uq, kv_a_norm, w_uk, w_uv, q_head_norm, k_head_norm, w_proj_attn, conv_w, w_proj_conv, w_out, ffn2_norm, ffn2_w_gate, ffn2_w_up, ffn2_w_down, loss_target, m_ffn1_norm, m_ffn1_w_gate, m_ffn1_w_up, m_ffn1_w_down, m_mix_norm, m_w_in, m_gate_bias, m_q_a_norm, m_w_uq, m_kv_a_norm, m_w_uk, m_w_uv, m_q_head_norm, m_k_head_norm, m_w_proj_attn, m_conv_w, m_w_proj_conv, m_w_out, m_ffn2_norm, m_ffn2_w_gate, m_ffn2_w_up, m_ffn2_w_down, v_ffn1_norm, v_ffn1_w_gate, v_ffn1_w_up, v_ffn1_w_down, v_mix_norm, v_w_in, v_gate_bias, v_q_a_norm, v_w_uq, v_kv_a_norm, v_w_uk, v_w_uv, v_q_head_norm, v_k_head_norm, v_w_proj_attn, v_conv_w, v_w_proj_conv, v_w_out, v_ffn2_norm, v_ffn2_w_gate, v_ffn2_w_up, v_ffn2_w_down):
    given = dict(locals())
    w = {n: given[n] for n in WEIGHT_NAMES}
    m = {n: given["m_" + n] for n in WEIGHT_NAMES}
    v = {n: given["v_" + n] for n in WEIGHT_NAMES}
    return _step(x, positions, loss_target, w, m, v)
```

```python
import functools

import jax
import jax.numpy as jnp
from jax import lax
from jax.experimental import pallas as pl
from jax.experimental.pallas import tpu as pltpu

F32 = jnp.float32
BF16 = jnp.bfloat16
MESH = pl.DeviceIdType.MESH
ANY = pl.BlockSpec(memory_space=pl.ANY)

N_DEV = 8
D = 1024
DFF = 2816
N_HEADS = 8
HEAD_PAD = 128
QK_DIM = 96
NOPE = 64
ROPE_HALF = 16
Q_LORA = 384
KV_LORA = 256
LAT_PAD = 768
CONV_COLS = 3072
GATE_COLS = 2048
IN_DIM = 5792
IN_SHARD = IN_DIM // N_DEV
IN_SHARD_PAD = 736
FF_SHARD = DFF // N_DEV
ROPE_THETA = 10000.0
NORM_EPS = 1e-6
ATTN_SCALE = QK_DIM ** -0.5
NEG = -1e30

ADAM_LR, ADAM_B1, ADAM_B2, ADAM_EPS, ADAM_WD, ADAM_STEP = 0.001, 0.9, 0.999, 1e-08, 0.01, 10

PACK = ((("w_inT", IN_SHARD_PAD),), (("w_uq", 48), ("w_uk", 32), ("w_uv", 32), ("w_pa", 64), ("w_pc", 128), ("w_out", 128)))
PACK_OFF = {}
for _i, _group in enumerate(PACK):
    _o = 0
    for _n, _r in _group:
        PACK_OFF[_n] = (_i, _o, _r)
        _o += _r

VMEM_LIMIT = 56 * 1024 * 1024


def _params(*sem):
    return pltpu.CompilerParams(dimension_semantics=sem if sem else None, vmem_limit_bytes=VMEM_LIMIT)


class _Plan:
    def __init__(self, start, wait, n_remote, n_local, in_place=False):
        self.start, self.wait, self.n_remote, self.n_local, self.in_place = start, wait, n_remote, n_local, in_place

    def sems(self):
        return [pltpu.SemaphoreType.DMA((self.n_remote,)), pltpu.SemaphoreType.DMA((self.n_remote,)),
                pltpu.SemaphoreType.DMA((max(self.n_local, 1),))]


def _call(body, *, name, grid, in_specs, out_specs, out_shape, scratch_shapes, operands, sem, hosted=None):
    if hosted is None:
        outs = pl.pallas_call(body, name=name, grid=grid, in_specs=in_specs, out_specs=out_specs, out_shape=out_shape,
                              scratch_shapes=scratch_shapes, compiler_params=_params(*sem))(*operands)
        return outs, None
    plan, srcs, h_shapes = hosted
    n_in, n_out, n_scr, nh_in, nh_out = len(in_specs), len(out_specs), len(scratch_shapes), len(srcs), len(h_shapes)
    aliases = {n_in + a: n_out + a for a in range(nh_in)} if plan.in_place else {}

    def full_body(*refs):
        ins, refs = refs[:n_in], refs[n_in:]
        h_in, refs = refs[:nh_in], refs[nh_in:]
        outs, refs = refs[:n_out], refs[n_out:]
        h_out, refs = refs[:nh_out], refs[nh_out:]
        scr, sems = refs[:n_scr], refs[n_scr:]
        ids = [pl.program_id(ax) for ax in range(len(grid))]
        first = functools.reduce(jnp.logical_and, [i == 0 for i in ids])
        last = functools.reduce(jnp.logical_and, [i == g - 1 for i, g in zip(ids, grid)])

        @pl.when(first)
        def _():
            plan.start(h_in, h_out, *sems)

        body(*ins, *outs, *scr)

        @pl.when(last)
        def _():
            plan.wait(h_in, h_out, *sems)

    res = pl.pallas_call(
        full_body, name=name, grid=grid, in_specs=list(in_specs) + [ANY] * nh_in, out_specs=list(out_specs) + [ANY] * nh_out,
        out_shape=list(out_shape) + list(h_shapes), scratch_shapes=list(scratch_shapes) + plan.sems(),
        input_output_aliases=aliases, compiler_params=_params(*(["arbitrary"] * len(grid))),
    )(*operands, *srcs)
    return res[:n_out], res[n_out:]


def _dot_nn(a, b):
    return lax.dot_general(a, b, (((1,), (0,)), ((), ())), preferred_element_type=F32)


def _dot_nt(a, b):
    return lax.dot_general(a, b, (((1,), (1,)), ((), ())), preferred_element_type=F32)


def _dot_tn(a, b):
    return lax.dot_general(a, b, (((0,), (0,)), ((), ())), preferred_element_type=F32)


def _sigmoid(x):
    return 0.5 * jnp.tanh(0.5 * x) + 0.5


def _rms_stats(x):
    r = lax.rsqrt(jnp.mean(x * x, axis=-1, keepdims=True) + NORM_EPS)
    return x * r, r


ROWS_WIDE = 16
MM_ROWS = 256


def _rms_bwd(dy, xhat, r, g):
    dg = jnp.sum(dy * xhat, axis=0, keepdims=True)
    dxh = dy * g
    dx = r * (dxh - xhat * jnp.mean(dxh * xhat, axis=-1, keepdims=True))
    return dx, dg


def _mm(a, b, *, mode, out_dtype, tm, tn, tk, name, add=None, scale=1.0, hosted=None):
    if mode == "nn":
        (m, k), (_, n) = a.shape, b.shape
    elif mode == "nt":
        (m, k), (n, _) = a.shape, b.shape
    else:
        (k, m), (_, n) = a.shape, b.shape
    assert m % tm == 0 and n % tn == 0 and k % tk == 0, (name, m, n, k, tm, tn, tk)
    nk = k // tk
    dot = {"nn": _dot_nn, "nt": _dot_nt, "tn": _dot_tn}[mode]
    a_spec = pl.BlockSpec((tk, tm), lambda i, j, kk: (kk, i)) if mode == "tn" else pl.BlockSpec((tm, tk), lambda i, j, kk: (i, kk))
    b_spec = pl.BlockSpec((tn, tk), lambda i, j, kk: (j, kk)) if mode == "nt" else pl.BlockSpec((tk, tn), lambda i, j, kk: (kk, j))
    o_spec = pl.BlockSpec((tm, tn), lambda i, j, kk: (i, j))
    has_add = add is not None

    def finish(prod, c_ref, o_ref):
        if scale != 1.0:
            prod = prod * scale
        o_ref[...] = ((c_ref[...] + prod) if has_add else prod).astype(out_dtype)

    def body(*refs):
        a_ref, b_ref = refs[:2]
        c_ref = refs[2] if has_add else None
        o_ref = refs[3] if has_add else refs[2]
        if nk == 1:
            finish(dot(a_ref[...], b_ref[...]), c_ref, o_ref)
            return
        acc_ref = refs[-1]
        kk = pl.program_id(2)

        @pl.when(kk == 0)
        def _():
            acc_ref[...] = jnp.zeros_like(acc_ref)

        acc_ref[...] += dot(a_ref[...], b_ref[...])

        @pl.when(kk == nk - 1)
        def _():
            finish(acc_ref[...], c_ref, o_ref)

    operands = (a, b, add) if has_add else (a, b)
    in_specs = [a_spec, b_spec] + ([o_spec] if has_add else [])
    (out,), got = _call(
        body, name=name, grid=(m // tm, n // tn, nk), in_specs=in_specs, out_specs=[o_spec],
        out_shape=[jax.ShapeDtypeStruct((m, n), out_dtype)], scratch_shapes=[pltpu.VMEM((tm, tn), F32)] if nk > 1 else [],
        operands=operands, sem=("parallel", "parallel", "arbitrary"), hosted=hosted)
    return out if hosted is None else (out, got)


def _rms_fwd(x, g, *, tm, name, hosted=None):
    t, d = x.shape

    def body(x_ref, g_ref, h_ref):
        xhat, _ = _rms_stats(x_ref[...])
        h_ref[...] = (xhat * g_ref[...]).astype(BF16)

    (h,), got = _call(
        body, name=name, grid=(t // tm,),
        in_specs=[pl.BlockSpec((tm, d), lambda i: (i, 0)), pl.BlockSpec((1, d), lambda i: (0, 0))],
        out_specs=[pl.BlockSpec((tm, d), lambda i: (i, 0))], out_shape=[jax.ShapeDtypeStruct((t, d), BF16)], scratch_shapes=[],
        operands=(x, g), sem=("parallel",), hosted=hosted)
    return h, got


def _ffn_fwd(x, g, wgT, wuT, wd, *, tm, hc, name, hosted=None, target=None):
    t, d = x.shape
    nj = DFF // hc
    with_loss = target is not None

    def body(*refs):
        x_ref, g_ref, wg_ref, wu_ref, wd_ref = refs[:5]
        t_ref = refs[5] if with_loss else None
        xo_ref, h_ref, a_ref, b_ref = refs[5 + with_loss:9 + with_loss]
        loss_ref = refs[9 + with_loss] if with_loss else None
        acc_ref = refs[-1]
        i, j = pl.program_id(0), pl.program_id(1)

        @pl.when(j == 0)
        def _():
            xhat, _ = _rms_stats(x_ref[...])
            h_ref[...] = (xhat * g_ref[...]).astype(BF16)
            acc_ref[...] = jnp.zeros_like(acc_ref)

        h = h_ref[...]
        a = _dot_nt(h, wg_ref[...])
        b = _dot_nt(h, wu_ref[...])
        a_ref[...] = a.astype(BF16)
        b_ref[...] = b.astype(BF16)
        s = (a * _sigmoid(a) * b).astype(BF16)
        acc_ref[...] += _dot_nn(s, wd_ref[...])

        if with_loss:
            @pl.when((i == 0) & (j == 0))
            def _():
                loss_ref[...] = jnp.zeros_like(loss_ref)

        @pl.when(j == nj - 1)
        def _():
            y = x_ref[...] + 0.5 * acc_ref[...]
            if with_loss:
                err = y - t_ref[...]
                xo_ref[...] = err * (1.0 / d)
                loss_ref[...] += jnp.sum(jnp.sum(err * err, axis=-1, keepdims=True), axis=0, keepdims=True) * (0.5 / d)
            else:
                xo_ref[...] = y

    row = pl.BlockSpec((tm, d), lambda i, j: (i, 0))
    vec = pl.BlockSpec((1, d), lambda i, j: (0, 0))
    wsp = pl.BlockSpec((hc, d), lambda i, j: (j, 0))
    hid = pl.BlockSpec((tm, hc), lambda i, j: (i, j))
    out_specs = [row, row, hid, hid] + ([pl.BlockSpec((1, 128), lambda i, j: (0, 0))] if with_loss else [])
    out_shape = [jax.ShapeDtypeStruct((t, d), F32), jax.ShapeDtypeStruct((t, d), BF16), jax.ShapeDtypeStruct((t, DFF), BF16),
                 jax.ShapeDtypeStruct((t, DFF), BF16)] + ([jax.ShapeDtypeStruct((1, 128), F32)] if with_loss else [])
    return _call(
        body, name=name, grid=(t // tm, nj), in_specs=[row, vec, wsp, wsp, wsp] + ([row] if with_loss else []),
        out_specs=out_specs, out_shape=out_shape, scratch_shapes=[pltpu.VMEM((tm, d), F32)],
        operands=(x, g, wgT, wuT, wd) + ((target,) if with_loss else ()),
        sem=("arbitrary" if with_loss else "parallel", "arbitrary"), hosted=hosted)


def _ffn_grads(dout, h, a, b, wd, *, tm, hc, name, hosted=None):
    t, d = dout.shape
    ni, nj = t // tm, DFF // hc

    def body(dout_ref, h_ref, a_ref, b_ref, wd_ref, da_ref, db_ref, dwg_ref, dwu_ref, dwd_ref,
             dy_all, h_all, ds_scr, s_scr, acc_g, acc_u, acc_d):
        j, i = pl.program_id(0), pl.program_id(1)
        rows_i = pl.ds(pl.multiple_of(i * tm, tm), tm)

        @pl.when(j == 0)
        def _():
            dy_all[rows_i, :] = (0.5 * dout_ref[...]).astype(BF16)
            h_all[rows_i, :] = h_ref[...]

        @pl.when(i == 0)
        def _():
            acc_g[...] = jnp.zeros_like(acc_g)
            acc_u[...] = jnp.zeros_like(acc_u)
            acc_d[...] = jnp.zeros_like(acc_d)

        def grad_rows(rows):
            ds = ds_scr[rows, :]
            av = a_ref[rows, :].astype(F32)
            bv = b_ref[rows, :].astype(F32)
            sg = _sigmoid(av)
            sl = av * sg
            s_scr[rows, :] = (sl * bv).astype(BF16)
            da_ref[rows, :] = (ds * bv * (sg + sl * (1.0 - sg))).astype(BF16)
            db_ref[rows, :] = (ds * sl).astype(BF16)

        for blk in range(tm // MM_ROWS):
            rs = slice(blk * MM_ROWS, (blk + 1) * MM_ROWS)
            ds_scr[rs, :] = _dot_nt(dy_all[pl.ds(pl.multiple_of(i * tm + blk * MM_ROWS, MM_ROWS), MM_ROWS), :], wd_ref[...])
            for c in range(MM_ROWS // ROWS_WIDE):
                grad_rows(slice(blk * MM_ROWS + c * ROWS_WIDE, blk * MM_ROWS + (c + 1) * ROWS_WIDE))

        dy_i = dy_all[rows_i, :]
        h_i = h_all[rows_i, :]
        acc_d[...] += _dot_tn(s_scr[...], dy_i)
        acc_g[...] += _dot_tn(da_ref[...], h_i)
        acc_u[...] += _dot_tn(db_ref[...], h_i)

        @pl.when(i == ni - 1)
        def _():
            dwg_ref[...] = acc_g[...].astype(BF16)
            dwu_ref[...] = acc_u[...].astype(BF16)
            dwd_ref[...] = acc_d[...].astype(BF16)

    first = pl.BlockSpec((tm, d), lambda j, i: (jnp.where(j == 0, i, 0), 0))
    hid = pl.BlockSpec((tm, hc), lambda j, i: (i, j))
    wsp = pl.BlockSpec((hc, d), lambda j, i: (j, 0))
    hid_shape = jax.ShapeDtypeStruct((t, DFF), BF16)
    w_shape = jax.ShapeDtypeStruct((DFF, d), BF16)
    return _call(
        body, name=name, grid=(nj, ni), in_specs=[first, first, hid, hid, wsp], out_specs=[hid, hid, wsp, wsp, wsp],
        out_shape=[hid_shape, hid_shape, w_shape, w_shape, w_shape],
        scratch_shapes=[pltpu.VMEM((t, d), BF16), pltpu.VMEM((t, d), BF16), pltpu.VMEM((tm, hc), F32), pltpu.VMEM((tm, hc), BF16),
                        pltpu.VMEM((hc, d), F32), pltpu.VMEM((hc, d), F32), pltpu.VMEM((hc, d), F32)],
        operands=(dout, h, a, b, wd), sem=("arbitrary", "arbitrary"), hosted=hosted)


def _proj_fwd(h, latT, convT, gateT, *, tm, name, hosted=None):
    t, d = h.shape

    def body(h_ref, wl_ref, wc_ref, wg_ref, lat_ref, conv_ref, gl_ref):
        hv = h_ref[...]
        lat_ref[...] = _dot_nt(hv, wl_ref[...]).astype(BF16)
        conv_ref[...] = _dot_nt(hv, wc_ref[...]).astype(BF16)
        gl_ref[...] = _dot_nt(hv, wg_ref[...]).astype(BF16)

    def rows(w):
        return pl.BlockSpec((tm, w), lambda i: (i, 0))

    def full(r):
        return pl.BlockSpec((r, d), lambda i: (0, 0))

    return _call(
        body, name=name, grid=(t // tm,), in_specs=[rows(d), full(LAT_PAD), full(CONV_COLS), full(GATE_COLS)],
        out_specs=[rows(LAT_PAD), rows(CONV_COLS), rows(GATE_COLS)],
        out_shape=[jax.ShapeDtypeStruct((t, LAT_PAD), BF16), jax.ShapeDtypeStruct((t, CONV_COLS), BF16),
                   jax.ShapeDtypeStruct((t, GATE_COLS), BF16)],
        scratch_shapes=[], operands=(h, latT, convT, gateT), sem=("parallel",), hosted=hosted)


def _proj_bwd(dlat, dconv3, dgl, latT, convT, gateT, x, g, dres, *, tm, name, hosted=None):
    t, d = x.shape

    def body(dl_ref, dc_ref, dg_ref, wl_ref, wc_ref, wg_ref, x_ref, g_ref, dres_ref, dx_ref, dgain_ref):
        @pl.when(pl.program_id(0) == 0)
        def _():
            dgain_ref[...] = jnp.zeros_like(dgain_ref)

        dh = _dot_nn(dl_ref[...], wl_ref[...]) + _dot_nn(dc_ref[...], wc_ref[...]) + _dot_nn(dg_ref[...], wg_ref[...])
        xhat, r = _rms_stats(x_ref[...])
        dx, dgain = _rms_bwd(dh, xhat, r, g_ref[...])
        dx_ref[...] = dres_ref[...] + dx
        dgain_ref[...] += dgain

    def rows(w):
        return pl.BlockSpec((tm, w), lambda i: (i, 0))

    def full(r):
        return pl.BlockSpec((r, d), lambda i: (0, 0))

    return _call(
        body, name=name, grid=(t // tm,),
        in_specs=[rows(LAT_PAD), rows(CONV_COLS), rows(GATE_COLS), full(LAT_PAD), full(CONV_COLS), full(GATE_COLS), rows(d), full(1), rows(d)],
        out_specs=[rows(d), full(1)], out_shape=[jax.ShapeDtypeStruct((t, d), F32), jax.ShapeDtypeStruct((1, d), F32)],
        scratch_shapes=[], operands=(dlat, dconv3, dgl, latT, convT, gateT, x, g, dres), sem=("arbitrary",), hosted=hosted)


def _ffn_up_bwd(da, db, wgT, wuT, x, g, dout, *, tm, name, hosted=None):
    t, d = x.shape

    def body(da_ref, db_ref, wg_ref, wu_ref, x_ref, g_ref, dout_ref, dx_ref, dg_ref):
        @pl.when(pl.program_id(0) == 0)
        def _():
            dg_ref[...] = jnp.zeros_like(dg_ref)

        dh = _dot_nn(da_ref[...], wg_ref[...]) + _dot_nn(db_ref[...], wu_ref[...])
        xhat, r = _rms_stats(x_ref[...])
        dx, dg = _rms_bwd(dh, xhat, r, g_ref[...])
        dx_ref[...] = dout_ref[...] + dx
        dg_ref[...] += dg

    row = pl.BlockSpec((tm, d), lambda i: (i, 0))
    vec = pl.BlockSpec((1, d), lambda i: (0, 0))
    hid = pl.BlockSpec((tm, DFF), lambda i: (i, 0))
    wsp = pl.BlockSpec((DFF, d), lambda i: (0, 0))
    return _call(
        body, name=name, grid=(t // tm,), in_specs=[hid, hid, wsp, wsp, row, vec, row], out_specs=[row, vec],
        out_shape=[jax.ShapeDtypeStruct((t, d), F32), jax.ShapeDtypeStruct((1, d), F32)], scratch_shapes=[],
        operands=(da, db, wgT, wuT, x, g, dout), sem=("arbitrary",), hosted=hosted)


HEAD_LANES = (slice(0, 32), slice(64, 80), None, slice(32, 64), slice(80, 96), None)


def _head_cols(a):
    def part(sl, width):
        if sl is None or sl.stop > a.shape[1]:
            return jnp.zeros((a.shape[0], width), a.dtype)
        return a[:, sl]

    return jnp.concatenate([part(sl, w) for sl, w in zip(HEAD_LANES, (32, 16, 16, 32, 16, 16))], axis=1)


def _head_cols_inv(a, dims):
    parts = [a[:, 0:32], a[:, 64:96]] + ([a[:, 32:48], a[:, 96:112]] if dims == QK_DIM else [])
    return jnp.concatenate(parts, axis=1)


def _rope_fwd(x, c, s):
    return x * c + pltpu.roll(x, HEAD_PAD // 2, 1) * s


def _rope_bwd(dy, c, s):
    return dy * c + pltpu.roll(dy * s, HEAD_PAD // 2, 1)


def _head_stats(x):
    r = lax.rsqrt(jnp.sum(x * x, axis=-1, keepdims=True) * (1.0 / QK_DIM) + NORM_EPS)
    return x * r, r


def _mla_prep_fwd(lat, gq, gkv, ghq, ghk, wq, wk, wv, rc, rs, *, tm, name):
    t = lat.shape[0]

    def body(lat_ref, gq_ref, gkv_ref, ghq_ref, ghk_ref, wq_ref, wk_ref, wv_ref, c_ref, s_ref,
             q_ref, k_ref, v_ref, qn_ref, ckv_ref):
        lat_v = lat_ref[...]
        qhat, _ = _rms_stats(lat_v[:, :Q_LORA].astype(F32))
        qn = (qhat * gq_ref[...]).astype(BF16)
        khat, _ = _rms_stats(lat_v[:, Q_LORA:Q_LORA + KV_LORA].astype(F32))
        ckv = (khat * gkv_ref[...]).astype(BF16)
        ckv_ext = jnp.concatenate([ckv, lat_v[:, Q_LORA + KV_LORA:]], axis=1)
        qn_ref[...] = qn
        ckv_ref[...] = ckv_ext
        q_pre = _dot_nn(qn, wq_ref[...])
        k_pre = _dot_nn(ckv_ext, wk_ref[...])
        v_ref[...] = _dot_nn(ckv, wv_ref[...]).astype(BF16)
        c, s = c_ref[...], s_ref[...]
        for h in range(N_HEADS):
            hs = slice(h * HEAD_PAD, (h + 1) * HEAD_PAD)
            xq, _ = _head_stats(q_pre[:, hs])
            q_ref[:, hs] = _rope_fwd(xq * ghq_ref[...], c, s).astype(BF16)
            xk, _ = _head_stats(k_pre[:, hs])
            k_ref[:, hs] = _rope_fwd(xk * ghk_ref[...], c, s).astype(BF16)

    def row(w):
        return pl.BlockSpec((tm, w), lambda i: (i, 0))

    def full(r, w):
        return pl.BlockSpec((r, w), lambda i: (0, 0))

    wide = jax.ShapeDtypeStruct((t, D), BF16)
    lat3 = jax.ShapeDtypeStruct((t, Q_LORA), BF16)
    return pl.pallas_call(
        body, name=name, grid=(t // tm,),
        in_specs=[row(LAT_PAD), full(1, Q_LORA), full(1, KV_LORA), full(1, HEAD_PAD), full(1, HEAD_PAD),
                  full(Q_LORA, D), full(Q_LORA, D), full(KV_LORA, D), row(HEAD_PAD), row(HEAD_PAD)],
        out_specs=[row(D), row(D), row(D), row(Q_LORA), row(Q_LORA)],
        out_shape=[wide, wide, wide, lat3, lat3],
        compiler_params=_params("parallel"),
    )(lat, gq, gkv, ghq, ghk, wq, wk, wv, rc, rs)


def _mla_prep_bwd(dq, dk, dv, lat, qn, ckv_ext, gq, gkv, ghq, ghk, wq, wk, wv, rc, rs, *, tm, name):
    t = lat.shape[0]

    def body(dq_ref, dk_ref, dv_ref, lat_ref, qn_ref, ckv_ref, gq_ref, gkv_ref, ghq_ref, ghk_ref, wq_ref, wk_ref, wv_ref,
             c_ref, s_ref, dlat_ref, dqp_ref, dkp_ref, dgq_ref, dgkv_ref, dghq_ref, dghk_ref):
        @pl.when(pl.program_id(0) == 0)
        def _():
            dgq_ref[...] = jnp.zeros_like(dgq_ref)
            dgkv_ref[...] = jnp.zeros_like(dgkv_ref)
            dghq_ref[...] = jnp.zeros_like(dghq_ref)
            dghk_ref[...] = jnp.zeros_like(dghk_ref)

        c, s = c_ref[...], s_ref[...]
        q_pre = _dot_nn(qn_ref[...], wq_ref[...])
        k_pre = _dot_nn(ckv_ref[...], wk_ref[...])

        def heads(pre, dy_ref, gh_ref, dgh_ref, out_ref):
            dgh = jnp.zeros((1, HEAD_PAD), F32)
            for h in range(N_HEADS):
                hs = slice(h * HEAD_PAD, (h + 1) * HEAD_PAD)
                d = _rope_bwd(dy_ref[:, hs].astype(F32), c, s)
                xhat, r = _head_stats(pre[:, hs])
                dgh = dgh + jnp.sum(d * xhat, axis=0, keepdims=True)
                dxh = d * gh_ref[...]
                dx = r * (dxh - xhat * (jnp.sum(dxh * xhat, axis=-1, keepdims=True) * (1.0 / QK_DIM)))
                out_ref[:, hs] = dx.astype(BF16)
            dgh_ref[...] += dgh

        heads(q_pre, dq_ref, ghq_ref, dghq_ref, dqp_ref)
        heads(k_pre, dk_ref, ghk_ref, dghk_ref, dkp_ref)
        dqn = _dot_nt(dqp_ref[...], wq_ref[...])
        dce = _dot_nt(dkp_ref[...], wk_ref[...])
        dckv = dce[:, :KV_LORA] + _dot_nt(dv_ref[...], wv_ref[...])
        lat_v = lat_ref[...]
        qhat, rq = _rms_stats(lat_v[:, :Q_LORA].astype(F32))
        dql, dgq = _rms_bwd(dqn, qhat, rq, gq_ref[...])
        khat, rk = _rms_stats(lat_v[:, Q_LORA:Q_LORA + KV_LORA].astype(F32))
        dkl, dgkv = _rms_bwd(dckv, khat, rk, gkv_ref[...])
        dgq_ref[...] += dgq
        dgkv_ref[...] += dgkv
        dlat_ref[...] = jnp.concatenate([dql, dkl, dce[:, KV_LORA:]], axis=1).astype(BF16)

    def row(w):
        return pl.BlockSpec((tm, w), lambda i: (i, 0))

    def full(r, w):
        return pl.BlockSpec((r, w), lambda i: (0, 0))

    return pl.pallas_call(
        body, name=name, grid=(t // tm,),
        in_specs=[row(D), row(D), row(D), row(LAT_PAD), row(Q_LORA), row(Q_LORA), full(1, Q_LORA), full(1, KV_LORA),
                  full(1, HEAD_PAD), full(1, HEAD_PAD), full(Q_LORA, D), full(Q_LORA, D), full(KV_LORA, D),
                  row(HEAD_PAD), row(HEAD_PAD)],
        out_specs=[row(LAT_PAD), row(D), row(D), full(1, Q_LORA), full(1, KV_LORA), full(1, HEAD_PAD), full(1, HEAD_PAD)],
        out_shape=[jax.ShapeDtypeStruct((t, LAT_PAD), BF16), jax.ShapeDtypeStruct((t, D), BF16), jax.ShapeDtypeStruct((t, D), BF16),
                   jax.ShapeDtypeStruct((1, Q_LORA), F32), jax.ShapeDtypeStruct((1, KV_LORA), F32),
                   jax.ShapeDtypeStruct((1, HEAD_PAD), F32), jax.ShapeDtypeStruct((1, HEAD_PAD), F32)],
        compiler_params=_params("arbitrary"),
    )(dq, dk, dv, lat, qn, ckv_ext, gq, gkv, ghq, ghk, wq, wk, wv, rc, rs)


def _causal_keep(tq):
    r = lax.broadcasted_iota(jnp.int32, (tq, tq), 0)
    c = lax.broadcasted_iota(jnp.int32, (tq, tq), 1)
    return c <= r


def _flash_fwd(q, k, v, *, n_seq, seq, tq, name, hosted=None):
    nq = seq // tq

    def body(q_ref, k_ref, v_ref, o_ref, lse_ref):
        for qi in range(nq):
            rows = slice(qi * tq, (qi + 1) * tq)
            qv = q_ref[rows, :]
            m = jnp.full((tq, 1), NEG, F32)
            l = jnp.zeros((tq, 1), F32)
            acc = jnp.zeros((tq, HEAD_PAD), F32)
            for j in range(qi + 1):
                cols = slice(j * tq, (j + 1) * tq)
                s = _dot_nt(qv, k_ref[cols, :]) * ATTN_SCALE
                if j == qi:
                    s = jnp.where(_causal_keep(tq), s, NEG)
                m_new = jnp.maximum(m, jnp.max(s, axis=-1, keepdims=True))
                alpha = jnp.exp(m - m_new)
                p = jnp.exp(s - m_new)
                l = alpha * l + jnp.sum(p, axis=-1, keepdims=True)
                acc = alpha * acc + _dot_nn(p.astype(BF16), v_ref[cols, :])
                m = m_new
            o_ref[rows, :] = (acc / l).astype(BF16)
            lse_ref[rows, :] = jnp.broadcast_to(m + jnp.log(l), (tq, HEAD_PAD))

    spec = pl.BlockSpec((seq, HEAD_PAD), lambda b, h: (b, h))
    t = n_seq * seq
    return _call(
        body, name=name, grid=(n_seq, N_HEADS), in_specs=[spec, spec, spec], out_specs=[spec, spec],
        out_shape=[jax.ShapeDtypeStruct((t, D), BF16), jax.ShapeDtypeStruct((t, D), F32)], scratch_shapes=[],
        operands=(q, k, v), sem=("parallel", "parallel"), hosted=hosted)


def _flash_bwd(q, k, v, o, lse, do, *, n_seq, seq, tq, name, hosted=None):
    nq = seq // tq

    def body(q_ref, k_ref, v_ref, o_ref, lse_ref, do_ref, dq_ref, dk_ref, dv_ref, dk_acc, dv_acc):
        j = pl.program_id(2)

        @pl.when(j == 0)
        def _():
            dq_ref[...] = jnp.zeros_like(dq_ref)

        dk_acc[...] = jnp.zeros_like(dk_acc)
        dv_acc[...] = jnp.zeros_like(dv_acc)
        kv = k_ref[...]
        vv = v_ref[...]

        def step(i, masked):
            rows = pl.ds(pl.multiple_of(i * tq, tq), tq)
            qi = q_ref[rows, :]
            doi = do_ref[rows, :]
            delta = jnp.sum(doi.astype(F32) * o_ref[rows, :].astype(F32), axis=-1, keepdims=True)
            s = _dot_nt(qi, kv) * ATTN_SCALE
            p = jnp.exp(s - lse_ref[rows, :][:, :1])
            if masked:
                p = jnp.where(_causal_keep(tq), p, 0.0)
            dv_acc[...] += _dot_tn(p.astype(BF16), doi)
            dp = _dot_nt(doi, vv)
            ds = (p * (dp - delta) * ATTN_SCALE).astype(BF16)
            dk_acc[...] += _dot_tn(ds, qi)
            dq_ref[rows, :] += _dot_nn(ds, kv)

        step(j, True)

        def loop_body(i, carry):
            step(i, False)
            return carry

        lax.fori_loop(j + 1, nq, loop_body, 0)
        dk_ref[...] = dk_acc[...]
        dv_ref[...] = dv_acc[...].astype(BF16)

    full = pl.BlockSpec((seq, HEAD_PAD), lambda b, h, j: (b, h))
    tile = pl.BlockSpec((tq, HEAD_PAD), lambda b, h, j: (b * nq + j, h))
    t = n_seq * seq
    return _call(
        body, name=name, grid=(n_seq, N_HEADS, nq), in_specs=[full, tile, tile, full, full, full],
        out_specs=[full, tile, tile],
        out_shape=[jax.ShapeDtypeStruct((t, D), F32), jax.ShapeDtypeStruct((t, D), F32), jax.ShapeDtypeStruct((t, D), BF16)],
        scratch_shapes=[pltpu.VMEM((tq, HEAD_PAD), F32), pltpu.VMEM((tq, HEAD_PAD), F32)],
        operands=(q, k, v, o, lse, do), sem=("parallel", "parallel", "arbitrary"), hosted=hosted)


CONV_CB = 256


def _shift_down(u, k, row):
    return jnp.where(row >= k, pltpu.roll(u, k, 0), 0.0)


def _shift_up(u, k, row, n):
    return jnp.where(row < n - k, pltpu.roll(u, n - k, 0), 0.0)


def _conv_fwd(conv3, cw, *, n_seq, seq, name, hosted=None):
    def body(c_ref, w_ref, p_ref):
        blk = c_ref[...].astype(F32)
        xc, gb, gc = blk[:, :CONV_CB], blk[:, CONV_CB:2 * CONV_CB], blk[:, 2 * CONV_CB:]
        row = lax.broadcasted_iota(jnp.int32, (seq, CONV_CB), 0)
        u = gc * xc
        z = w_ref[0:1, :] * _shift_down(u, 2, row) + w_ref[1:2, :] * _shift_down(u, 1, row) + w_ref[2:3, :] * u
        p_ref[...] = (gb * z).astype(BF16)

    (p,), got = _call(
        body, name=name, grid=(n_seq, D // CONV_CB),
        in_specs=[pl.BlockSpec((seq, 3 * CONV_CB), lambda b, j: (b, j)), pl.BlockSpec((3, CONV_CB), lambda b, j: (0, j))],
        out_specs=[pl.BlockSpec((seq, CONV_CB), lambda b, j: (b, j))],
        out_shape=[jax.ShapeDtypeStruct((n_seq * seq, D), BF16)], scratch_shapes=[],
        operands=(conv3, cw), sem=("parallel", "parallel"), hosted=hosted)
    return p, got


def _conv_bwd(dp, conv3, cw, *, n_seq, seq, name):
    def body(dp_ref, c_ref, w_ref, dc_ref, dw_ref):
        @pl.when(pl.program_id(1) == 0)
        def _():
            dw_ref[...] = jnp.zeros_like(dw_ref)

        blk = c_ref[...].astype(F32)
        xc, gb, gc = blk[:, :CONV_CB], blk[:, CONV_CB:2 * CONV_CB], blk[:, 2 * CONV_CB:]
        row = lax.broadcasted_iota(jnp.int32, (seq, CONV_CB), 0)
        w0, w1, w2 = w_ref[0:1, :], w_ref[1:2, :], w_ref[2:3, :]
        u = gc * xc
        u1 = _shift_down(u, 1, row)
        u2 = _shift_down(u, 2, row)
        z = w0 * u2 + w1 * u1 + w2 * u
        dpv = dp_ref[...].astype(F32)
        dz = dpv * gb
        du = w2 * dz + w1 * _shift_up(dz, 1, row, seq) + w0 * _shift_up(dz, 2, row, seq)
        dc_ref[...] = jnp.concatenate([du * gc, dpv * z, du * xc], axis=1).astype(BF16)
        dw_ref[0:1, :] += jnp.sum(dz * u2, axis=0, keepdims=True)
        dw_ref[1:2, :] += jnp.sum(dz * u1, axis=0, keepdims=True)
        dw_ref[2:3, :] += jnp.sum(dz * u, axis=0, keepdims=True)

    return pl.pallas_call(
        body, name=name, grid=(D // CONV_CB, n_seq),
        in_specs=[pl.BlockSpec((seq, CONV_CB), lambda j, b: (b, j)), pl.BlockSpec((seq, 3 * CONV_CB), lambda j, b: (b, j)),
                  pl.BlockSpec((3, CONV_CB), lambda j, b: (0, j))],
        out_specs=[pl.BlockSpec((seq, 3 * CONV_CB), lambda j, b: (b, j)), pl.BlockSpec((3, CONV_CB), lambda j, b: (0, j))],
        out_shape=[jax.ShapeDtypeStruct((n_seq * seq, CONV_COLS), BF16), jax.ShapeDtypeStruct((3, D), F32)],
        compiler_params=_params("parallel", "arbitrary"),
    )(dp, conv3, cw)


def _merge_fwd(o, p, gl, bias, x1, wpa, wpc, wout, *, tm, name, hosted=None):
    t = x1.shape[0]

    def body(o_ref, p_ref, gl_ref, b_ref, x_ref, wpa_ref, wpc_ref, wout_ref, x2_ref, mg_ref, ya_ref, yb_ref):
        ya = _dot_nn(o_ref[...], wpa_ref[...])
        yb = _dot_nn(p_ref[...], wpc_ref[...])
        gates = _sigmoid(gl_ref[...].astype(F32) + b_ref[...])
        merged = (gates[:, :D] * ya + gates[:, D:] * yb).astype(BF16)
        ya_ref[...] = ya.astype(BF16)
        yb_ref[...] = yb.astype(BF16)
        mg_ref[...] = merged
        x2_ref[...] = x_ref[...] + _dot_nn(merged, wout_ref[...])

    row = pl.BlockSpec((tm, D), lambda i: (i, 0))
    row2 = pl.BlockSpec((tm, GATE_COLS), lambda i: (i, 0))
    wsp = pl.BlockSpec((D, D), lambda i: (0, 0))
    wide = jax.ShapeDtypeStruct((t, D), BF16)
    return _call(
        body, name=name, grid=(t // tm,),
        in_specs=[row, row, row2, pl.BlockSpec((1, GATE_COLS), lambda i: (0, 0)), row, wsp, wsp, wsp],
        out_specs=[row, row, row, row], out_shape=[jax.ShapeDtypeStruct((t, D), F32), wide, wide, wide], scratch_shapes=[],
        operands=(o, p, gl, bias, x1, wpa, wpc, wout), sem=("parallel",), hosted=hosted)


def _merge_bwd(dx2, ya, yb, gl, bias, wpa, wpc, wout, *, tm, name, hosted=None):
    t = dx2.shape[0]

    def body(dx_ref, ya_ref, yb_ref, gl_ref, b_ref, wpa_ref, wpc_ref, wout_ref,
             dxb_ref, dya_ref, dyb_ref, dgl_ref, do_ref, dp_ref, db_ref):
        @pl.when(pl.program_id(0) == 0)
        def _():
            db_ref[...] = jnp.zeros_like(db_ref)

        dxb = dx_ref[...].astype(BF16)
        dxb_ref[...] = dxb
        dm = _dot_nt(dxb, wout_ref[...])
        gates = _sigmoid(gl_ref[...].astype(F32) + b_ref[...])
        ga, gb = gates[:, :D], gates[:, D:]
        dya = (dm * ga).astype(BF16)
        dyb = (dm * gb).astype(BF16)
        dya_ref[...] = dya
        dyb_ref[...] = dyb
        dgl = jnp.concatenate([dm * ya_ref[...].astype(F32) * ga * (1.0 - ga),
                               dm * yb_ref[...].astype(F32) * gb * (1.0 - gb)], axis=1)
        dgl_ref[...] = dgl.astype(BF16)
        db_ref[...] += jnp.sum(dgl, axis=0, keepdims=True)
        do_ref[...] = _dot_nt(dya, wpa_ref[...]).astype(BF16)
        dp_ref[...] = _dot_nt(dyb, wpc_ref[...]).astype(BF16)

    row = pl.BlockSpec((tm, D), lambda i: (i, 0))
    row2 = pl.BlockSpec((tm, GATE_COLS), lambda i: (i, 0))
    vec2 = pl.BlockSpec((1, GATE_COLS), lambda i: (0, 0))
    wsp = pl.BlockSpec((D, D), lambda i: (0, 0))
    wide = jax.ShapeDtypeStruct((t, D), BF16)
    return _call(
        body, name=name, grid=(t // tm,), in_specs=[row, row, row, row2, vec2, wsp, wsp, wsp],
        out_specs=[row, row, row, row2, row, row, vec2],
        out_shape=[wide, wide, wide, jax.ShapeDtypeStruct((t, GATE_COLS), BF16), wide, wide,
                   jax.ShapeDtypeStruct((1, GATE_COLS), F32)],
        scratch_shapes=[], operands=(dx2, ya, yb, gl, bias, wpa, wpc, wout), sem=("arbitrary",), hosted=hosted)


def _adamw_small(ws, gs, ms, vs, *, name):
    n = len(ws)
    c1 = 1.0 / (1.0 - ADAM_B1 ** ADAM_STEP)
    c2 = 1.0 / (1.0 - ADAM_B2 ** ADAM_STEP)

    def body(*refs):
        for i in range(n):
            w_ref, g_ref, m_ref, v_ref, d_ref, nm_ref, nv_ref = (refs[k * n + i] for k in range(7))
            gv = g_ref[...]
            nm = ADAM_B1 * m_ref[...] + (1.0 - ADAM_B1) * gv
            nv = ADAM_B2 * v_ref[...] + (1.0 - ADAM_B2) * (gv * gv)
            nm_ref[...] = nm
            nv_ref[...] = nv
            d_ref[...] = -ADAM_LR * ((nm * c1) / (jnp.sqrt(nv * c2) + ADAM_EPS) + ADAM_WD * w_ref[...])

    vm = pl.BlockSpec(memory_space=pltpu.VMEM)
    shapes = [jax.ShapeDtypeStruct(a.shape, F32) for a in ws]
    outs = pl.pallas_call(body, name=name, in_specs=[vm] * (4 * n), out_specs=[vm] * (3 * n), out_shape=shapes * 3)(*ws, *gs, *ms, *vs)
    return outs[:n], outs[n:2 * n], outs[2 * n:]


def _adamw(w, g, m, v, *, name):
    rows, cols = w.shape
    tr = max([c for c in range(8, 513, 8) if rows % c == 0], default=rows)
    c1 = 1.0 / (1.0 - ADAM_B1 ** ADAM_STEP)
    c2 = 1.0 / (1.0 - ADAM_B2 ** ADAM_STEP)

    def body(w_ref, g_ref, m_ref, v_ref, d_ref, nm_ref, nv_ref):
        gv = g_ref[...]
        nm = ADAM_B1 * m_ref[...] + (1.0 - ADAM_B1) * gv
        nv = ADAM_B2 * v_ref[...] + (1.0 - ADAM_B2) * (gv * gv)
        nm_ref[...] = nm
        nv_ref[...] = nv
        d_ref[...] = -ADAM_LR * ((nm * c1) / (jnp.sqrt(nv * c2) + ADAM_EPS) + ADAM_WD * w_ref[...])

    spec = pl.BlockSpec((tr, cols), lambda i: (i, 0))
    shp = jax.ShapeDtypeStruct((rows, cols), F32)
    return pl.pallas_call(
        body, name=name, grid=(rows // tr,), in_specs=[spec] * 4, out_specs=[spec] * 3, out_shape=[shp] * 3,
        compiler_params=_params("parallel"),
    )(w, g, m, v)


def _place():
    return lax.axis_index("x"), lax.axis_index("y"), lax.axis_index("c")


def _other_chips(x, y):
    return [(1 - x, y), (x, 1 - y), (1 - x, 1 - y)]


def _remote(src, dst, send, recv, dev):
    return pltpu.make_async_remote_copy(src_ref=src, dst_ref=dst, send_sem=send, recv_sem=recv, device_id=dev, device_id_type=MESH)


def _gather_chips_plan(n):
    def start(srcs, dsts, send, recv, local):
        x, y, cc = _place()
        me = 4 * x + 2 * y + cc
        for a in range(n):
            pltpu.make_async_copy(srcs[a], dsts[a].at[me], local.at[a]).start()
            for k, (px, py) in enumerate(_other_chips(x, y)):
                _remote(srcs[a], dsts[a].at[me], send.at[3 * a + k], recv.at[3 * a + k], (px, py, cc)).start()

    def wait(srcs, dsts, send, recv, local):
        x, y, cc = _place()
        me = 4 * x + 2 * y + cc
        for a in range(n):
            for k, (px, py) in enumerate(_other_chips(x, y)):
                _remote(srcs[a], dsts[a].at[4 * px + 2 * py + cc], send.at[3 * a + k], recv.at[3 * a + k], (px, py, cc)).wait_recv()
        for a in range(n):
            for k, (px, py) in enumerate(_other_chips(x, y)):
                _remote(srcs[a], dsts[a].at[me], send.at[3 * a + k], recv.at[3 * a + k], (px, py, cc)).wait_send()
            pltpu.make_async_copy(srcs[a], dsts[a].at[me], local.at[a]).wait()

    return _Plan(start, wait, 3 * n, n)


def _scatter_chips_plan(n):
    def start(srcs, dsts, send, recv, local):
        x, y, cc = _place()
        for a in range(n):
            for k, (px, py) in enumerate(_other_chips(x, y)):
                _remote(srcs[a].at[2 * px + py], dsts[a].at[k], send.at[3 * a + k], recv.at[3 * a + k], (px, py, cc)).start()

    def wait(srcs, dsts, send, recv, local):
        x, y, cc = _place()
        for a in range(n):
            for k, (px, py) in enumerate(_other_chips(x, y)):
                _remote(srcs[a].at[k], dsts[a].at[k], send.at[3 * a + k], recv.at[3 * a + k], (px, py, cc)).wait_recv()
        for a in range(n):
            for k, (px, py) in enumerate(_other_chips(x, y)):
                _remote(srcs[a].at[k], dsts[a].at[k], send.at[3 * a + k], recv.at[3 * a + k], (px, py, cc)).wait_send()

    return _Plan(start, wait, 3 * n, 0)


def _gather_shapes(blocks):
    return [jax.ShapeDtypeStruct((N_DEV,) + b.shape, b.dtype) for b in blocks]


def _scatter_shapes(parts):
    return [jax.ShapeDtypeStruct((3,) + p.shape[1:], p.dtype) for p in parts]


def _gather_sibling_plan(n):
    def start(srcs, dsts, send, recv, local):
        x, y, cc = _place()
        for a in range(n):
            for q in range(4):
                _remote(srcs[a].at[2 * q + cc], dsts[a].at[2 * q + cc], send.at[4 * a + q], recv.at[4 * a + q], (x, y, 1 - cc)).start()

    def wait(srcs, dsts, send, recv, local):
        x, y, cc = _place()
        for a in range(n):
            for q in range(4):
                _remote(srcs[a].at[2 * q + cc], dsts[a].at[2 * q + 1 - cc], send.at[4 * a + q], recv.at[4 * a + q],
                        (x, y, 1 - cc)).wait_recv()
        for a in range(n):
            for q in range(4):
                _remote(srcs[a].at[2 * q + cc], dsts[a].at[2 * q + cc], send.at[4 * a + q], recv.at[4 * a + q],
                        (x, y, 1 - cc)).wait_send()

    return _Plan(start, wait, 4 * n, 0, in_place=True)


def _scatter_sibling_plan(n):
    def start(srcs, dsts, send, recv, local):
        x, y, cc = _place()
        for a in range(n):
            for q in range(4):
                _remote(srcs[a].at[2 * q + 1 - cc], dsts[a].at[q], send.at[4 * a + q], recv.at[4 * a + q], (x, y, 1 - cc)).start()

    def wait(srcs, dsts, send, recv, local):
        x, y, cc = _place()
        for a in range(n):
            for q in range(4):
                _remote(srcs[a].at[q], dsts[a].at[q], send.at[4 * a + q], recv.at[4 * a + q], (x, y, 1 - cc)).wait_recv()
        for a in range(n):
            for q in range(4):
                _remote(srcs[a].at[q], dsts[a].at[q], send.at[4 * a + q], recv.at[4 * a + q], (x, y, 1 - cc)).wait_send()

    return _Plan(start, wait, 4 * n, 0)


def _same_shapes(arrs):
    return [jax.ShapeDtypeStruct(a.shape, a.dtype) for a in arrs]


def _halved_shapes(parts):
    return [jax.ShapeDtypeStruct((4,) + p.shape[1:], p.dtype) for p in parts]


def _run_plan(plan, srcs, out_shapes, *, name):
    n_in, n_out = len(srcs), len(out_shapes)

    def body(*refs):
        h_in, h_out, sems = refs[:n_in], refs[n_in:n_in + n_out], refs[n_in + n_out:]
        plan.start(h_in, h_out, *sems)
        plan.wait(h_in, h_out, *sems)

    return pl.pallas_call(body, name=name, in_specs=[ANY] * n_in, out_specs=[ANY] * n_out, out_shape=list(out_shapes),
                          input_output_aliases={a: a for a in range(n_in)} if plan.in_place else {},
                          scratch_shapes=plan.sems())(*srcs)


SEM = pl.BlockSpec(memory_space=pltpu.SEMAPHORE)
HBM = pl.BlockSpec(memory_space=pltpu.HBM)
SIDE_EFFECT = pltpu.CompilerParams(has_side_effects=pltpu.SideEffectType.DATAFLOW_SIDE_EFFECTING)


def _plan_start(plan, blocks, land_shapes, *, name):
    n = len(blocks)
    lands = [lax.empty(s.shape, s.dtype) for s in land_shapes]

    def body(*refs):
        srcs, sems, lands_out, token = refs[:n], refs[2 * n:2 * n + 3], refs[3 * n + 3:4 * n + 3], refs[4 * n + 3]
        plan.start(srcs, lands_out, *sems)
        token[...] = jnp.zeros_like(token)

    out_shape = ([s for s in plan.sems()] + [pltpu.HBM(b.shape, b.dtype) for b in blocks]
                 + [pltpu.HBM(l.shape, l.dtype) for l in lands] + [jax.ShapeDtypeStruct((8, 128), F32)])
    res = pl.pallas_call(
        body, name=name, in_specs=[HBM] * (2 * n), out_specs=[SEM] * 3 + [HBM] * (2 * n) + [pl.BlockSpec(memory_space=pltpu.VMEM)],
        out_shape=out_shape, input_output_aliases={a: 3 + a for a in range(2 * n)}, compiler_params=SIDE_EFFECT,
    )(*[pltpu.with_memory_space_constraint(a, pltpu.HBM) for a in list(blocks) + lands])
    return res[:3], res[3:3 + n], res[3 + n:3 + 2 * n], res[3 + 2 * n]


def _plan_wait(plan, sems, blocks, lands, after, *, name):
    n = len(blocks)

    def body(*refs):
        plan.wait(refs[:n], refs[n:2 * n], *refs[2 * n:2 * n + 3])

    res = pl.pallas_call(
        body, name=name, in_specs=[HBM] * (2 * n) + [SEM] * 3 + [ANY], out_specs=[HBM] * (2 * n),
        out_shape=[pltpu.HBM(a.shape, a.dtype) for a in list(blocks) + list(lands)],
        input_output_aliases={a: a for a in range(2 * n)}, compiler_params=SIDE_EFFECT,
    )(*blocks, *lands, *sems, after)
    return list(res[:n]), list(res[n:])


def _sum_sibling(ps, qs, core, *, name):
    n = len(ps)

    def body(core_ref, *refs):
        for p_ref, q_ref, o_ref in zip(refs[:n], refs[n:2 * n], refs[2 * n:]):
            o_ref[...] = (p_ref[...].astype(F32) + q_ref[...].astype(F32)).astype(BF16)

    def mine(p):
        return pl.BlockSpec((1,) + p.shape[1:], lambda ch, core_ref: (2 * ch + core_ref[0], 0, 0))

    def theirs(p):
        return pl.BlockSpec((1,) + p.shape[1:], lambda ch, core_ref: (ch, 0, 0))

    grid_spec = pltpu.PrefetchScalarGridSpec(
        num_scalar_prefetch=1, grid=(4,), in_specs=[mine(p) for p in ps] + [theirs(p) for p in ps], out_specs=[theirs(p) for p in ps])
    return pl.pallas_call(
        body, name=name, grid_spec=grid_spec, out_shape=[jax.ShapeDtypeStruct((4,) + p.shape[1:], BF16) for p in ps],
        compiler_params=_params("parallel"),
    )(core, *ps, *qs)


def _sum_chips(s1, r2, chip, *, name):
    _, r, c = s1.shape

    def body(chip_ref, s_ref, r_ref, o_ref):
        acc = s_ref[0].astype(F32)
        for k in range(3):
            acc = acc + r_ref[k].astype(F32)
        o_ref[...] = acc

    grid_spec = pltpu.PrefetchScalarGridSpec(
        num_scalar_prefetch=1, grid=(1,),
        in_specs=[pl.BlockSpec((1, r, c), lambda i, chip_ref: (chip_ref[0], 0, 0)),
                  pl.BlockSpec((3, r, c), lambda i, chip_ref: (0, 0, 0))],
        out_specs=pl.BlockSpec((r, c), lambda i, chip_ref: (0, 0)))
    return pl.pallas_call(
        body, name=name, grid_spec=grid_spec, out_shape=jax.ShapeDtypeStruct((r, c), F32),
        compiler_params=_params("arbitrary"),
    )(chip, s1, r2)


def _sum_adamw(s1s, r2s, chip, ws, ms, vs, *, name):
    n = len(s1s)
    _, r, c = s1s[0].shape
    tr = r // 2
    c1 = 1.0 / (1.0 - ADAM_B1 ** ADAM_STEP)
    c2 = 1.0 / (1.0 - ADAM_B2 ** ADAM_STEP)

    def body(chip_ref, *refs):
        for a in range(n):
            s_ref, r_ref, w_ref, m_ref, v_ref = (refs[k * n + a] for k in range(5))
            g_ref, d_ref, nm_ref, nv_ref = refs[5 * n + 4 * a:5 * n + 4 * a + 4]
            gv = s_ref[0].astype(F32)
            for k in range(3):
                gv = gv + r_ref[k].astype(F32)
            g_ref[...] = gv
            nm = ADAM_B1 * m_ref[...] + (1.0 - ADAM_B1) * gv
            nv = ADAM_B2 * v_ref[...] + (1.0 - ADAM_B2) * (gv * gv)
            nm_ref[...] = nm
            nv_ref[...] = nv
            d_ref[...] = -ADAM_LR * ((nm * c1) / (jnp.sqrt(nv * c2) + ADAM_EPS) + ADAM_WD * w_ref[...])

    flat = pl.BlockSpec((tr, c), lambda i, chip_ref: (i, 0))
    own = pl.BlockSpec((1, tr, c), lambda i, chip_ref: (chip_ref[0], i, 0))
    got = pl.BlockSpec((3, tr, c), lambda i, chip_ref: (0, i, 0))
    grid_spec = pltpu.PrefetchScalarGridSpec(
        num_scalar_prefetch=1, grid=(2,), in_specs=[own] * n + [got] * n + [flat] * (3 * n), out_specs=[flat] * (4 * n))
    res = pl.pallas_call(
        body, name=name, grid_spec=grid_spec, out_shape=[jax.ShapeDtypeStruct((r, c), F32)] * (4 * n),
        compiler_params=_params("parallel"),
    )(chip, *s1s, *r2s, *ws, *ms, *vs)
    return [res[4 * a:4 * a + 4] for a in range(n)]


def _small_exchange(v, *, reduce, name):
    r, c = v.shape

    def body(x_ref, o_ref, *rest):
        if reduce:
            buf_ref, send_sems, recv_sems = rest
        else:
            buf_ref = o_ref
            send_sems, recv_sems = rest
        x, y, cc = _place()
        me = 4 * x + 2 * y + cc

        def peer(k):
            return ((1 - x) if k & 4 else x, (1 - y) if k & 2 else y, (1 - cc) if k & 1 else cc)

        buf_ref[me] = x_ref[...]
        sends = []
        for k in range(1, N_DEV):
            cp = pltpu.make_async_remote_copy(src_ref=x_ref, dst_ref=buf_ref.at[me], send_sem=send_sems.at[k - 1],
                                              recv_sem=recv_sems.at[k - 1], device_id=peer(k), device_id_type=MESH)
            cp.start()
            sends.append(cp)
        for k in range(1, N_DEV):
            px, py, pc = peer(k)
            pltpu.make_async_remote_copy(src_ref=x_ref, dst_ref=buf_ref.at[4 * px + 2 * py + pc], send_sem=send_sems.at[k - 1],
                                         recv_sem=recv_sems.at[k - 1], device_id=peer(k), device_id_type=MESH).wait_recv()
        for cp in sends:
            cp.wait_send()
        if reduce:
            acc = buf_ref[0]
            for s in range(1, N_DEV):
                acc = acc + buf_ref[s]
            o_ref[...] = acc

    vm = pl.BlockSpec(memory_space=pltpu.VMEM)
    sems = [pltpu.SemaphoreType.DMA((N_DEV - 1,)), pltpu.SemaphoreType.DMA((N_DEV - 1,))]
    if reduce:
        out_shape, scratch = jax.ShapeDtypeStruct((r, c), F32), [pltpu.VMEM((N_DEV, r, c), F32)] + sems
    else:
        out_shape, scratch = jax.ShapeDtypeStruct((N_DEV, r, c), F32), sems
    return pl.pallas_call(body, name=name, in_specs=[vm], out_specs=vm, out_shape=out_shape, scratch_shapes=scratch)(v)


def _rows(a):
    return a.reshape(-1, D)


def _pad_cols(a, to):
    return jnp.pad(a, ((0, 0), (0, to - a.shape[1])))


def _pack_weights(w):
    parts = {
        "w_inT": jnp.pad(w["w_in"].T, ((0, IN_SHARD_PAD - IN_SHARD), (0, 0))),
        "w_uq": _rows(_head_cols(w["w_uq"])), "w_uk": _rows(_head_cols(w["w_uk"])),
        "w_uv": _rows(_pad_cols(w["w_uv"], HEAD_PAD)), "w_pa": _rows(w["w_proj_attn"]),
        "w_pc": w["w_proj_conv"], "w_out": w["w_out"],
    }
    return [jnp.concatenate([parts[n].astype(BF16) for n, _ in group], axis=0) for group in PACK]


def _cols_from_shards(gs, name, rows):
    idx, off, r = PACK_OFF[name]
    return gs[idx][:, off:off + r].reshape(N_DEV, rows, HEAD_PAD).transpose(1, 0, 2).reshape(rows, N_DEV * HEAD_PAD)


def _rows_from_shards(gs, name, keep=None):
    idx, off, r = PACK_OFF[name]
    keep = r if keep is None else keep
    return gs[idx][:, off:off + keep].reshape(N_DEV * keep, D)


def _rope_placement():
    i = lax.broadcasted_iota(jnp.int32, (HEAD_PAD, D), 0)
    j = lax.broadcasted_iota(jnp.int32, (HEAD_PAD, D), 1)
    lane = jnp.where(i < ROPE_HALF, 32 + i, 96 + i - ROPE_HALF)
    return ((i < 2 * ROPE_HALF) & (j % HEAD_PAD == lane)).astype(BF16)


def _unpack_in(g_in):
    w_inT = _rows_from_shards([g_in, None], "w_inT", IN_SHARD)
    lat_rows = Q_LORA + KV_LORA + 2 * ROPE_HALF
    conv = w_inT[lat_rows:lat_rows + CONV_COLS].reshape(3, D // CONV_CB, CONV_CB, D).transpose(1, 0, 2, 3).reshape(CONV_COLS, D)
    return {"latT": jnp.pad(w_inT[:lat_rows], ((0, LAT_PAD - lat_rows), (0, 0))), "convT": conv,
            "gateT": w_inT[lat_rows + CONV_COLS:]}


def _unpack_misc(g_misc):
    g = [None, g_misc]
    wpa = _cols_from_shards(g, "w_pa", 512).reshape(N_HEADS, NOPE, D)
    return {
        "wq": _cols_from_shards(g, "w_uq", Q_LORA),
        "wk": jnp.concatenate([_cols_from_shards(g, "w_uk", KV_LORA), _rope_placement()], axis=0),
        "wv": _cols_from_shards(g, "w_uv", KV_LORA),
        "wpa": jnp.pad(wpa, ((0, 0), (0, HEAD_PAD - NOPE), (0, 0))).reshape(D, D),
        "wpc": _rows_from_shards(g, "w_pc"), "wout": _rows_from_shards(g, "w_out"),
    }


def _shards_from_cols(a):
    rows = a.shape[0]
    return a.reshape(rows, N_DEV, HEAD_PAD).transpose(1, 0, 2).reshape(N_DEV, rows * HEAD_PAD // D, D)


def _pack_grads(gw):
    lat_rows = Q_LORA + KV_LORA + 2 * ROPE_HALF
    conv = gw["convT"].reshape(D // CONV_CB, 3, CONV_CB, D).transpose(1, 0, 2, 3).reshape(CONV_COLS, D)
    w_inT = jnp.concatenate([gw["latT"][:lat_rows], conv, gw["gateT"]], axis=0).reshape(N_DEV, IN_SHARD, D)
    wpa = gw["wpa"].reshape(N_HEADS, HEAD_PAD, D)[:, :NOPE].reshape(N_HEADS * NOPE, D)
    parts = {}
    parts.update({
        "w_inT": jnp.pad(w_inT, ((0, 0), (0, IN_SHARD_PAD - IN_SHARD), (0, 0))),
        "w_uq": _shards_from_cols(gw["wq"]), "w_uk": _shards_from_cols(gw["wk"][:KV_LORA]),
        "w_uv": _shards_from_cols(gw["wv"][:KV_LORA]), "w_pa": _shards_from_cols(wpa),
        "w_pc": gw["wpc"].reshape(N_DEV, D // N_DEV, D), "w_out": gw["wout"].reshape(N_DEV, D // N_DEV, D),
    })
    return [jnp.concatenate([parts[n] for n, _ in group], axis=1) for group in PACK]


def _unpack_grads(mines):
    def seg(name, keep=None):
        idx, off, r = PACK_OFF[name]
        return mines[idx][off:off + (r if keep is None else keep)]

    return {
        "w_in": seg("w_inT", IN_SHARD).T,
        "w_uq": _head_cols_inv(seg("w_uq").reshape(Q_LORA, HEAD_PAD), QK_DIM),
        "w_uk": _head_cols_inv(seg("w_uk").reshape(KV_LORA, HEAD_PAD), NOPE),
        "w_uv": seg("w_uv").reshape(KV_LORA, HEAD_PAD)[:, :NOPE],
        "w_proj_attn": seg("w_pa").reshape(512, HEAD_PAD),
        "w_proj_conv": seg("w_pc"), "w_out": seg("w_out"),
    }


def _rope_tables(positions):
    lane = jnp.arange(HEAD_PAD)
    idx = jnp.where((lane >= 32) & (lane < 48), lane - 32, jnp.where((lane >= 96) & (lane < 112), lane - 96, -1))
    inv_freq = jnp.where(idx >= 0, 1.0 / (ROPE_THETA ** (idx.astype(F32) / ROPE_HALF)), 0.0)
    ang = positions.reshape(-1).astype(F32)[:, None] * inv_freq
    return jnp.cos(ang), jnp.sin(ang) * jnp.where(lane < HEAD_PAD // 2, -1.0, 1.0)


def _local_step(x, positions, target, conv_w, small, ex):
    n_seq, seq, d = x.shape
    t = n_seq * seq
    x0 = x.reshape(t, d)
    tgt = target.reshape(t, d)
    rc, rs = _rope_tables(positions)
    ghq = _head_cols(small["q_head_norm"])
    ghk = _head_cols(small["k_head_norm"])
    TM, HC, TQ = 1024, 256, 1024

    def mm(*args, hosted=None, **kw):
        res = _mm(*args, hosted=hosted, **kw)
        return res if hosted is not None else (res, None)

    def wgrad(a, b, name, tm=None, hosted=None):
        tm = tm or a.shape[1]
        return mm(a, b, mode="tn", out_dtype=BF16, tm=tm, tn=b.shape[1], tk=2048 if tm <= D else 1024, name=name, hosted=hosted)

    f1g, f1u, f1d = ex.gather_finish(ex.witness() + rc[:8] + conv_w[:1, :HEAD_PAD])
    (x1, h1, a1, b1), got = _ffn_fwd(x0, small["ffn1_norm"], f1g, f1u, f1d, tm=512, hc=DFF // 2, name="ffn1_fwd",
                                     hosted=ex.gather_chips("mix_in"))
    hm, got = _rms_fwd(x1, small["mix_norm"], tm=TM, name="mix_norm_fwd", hosted=ex.gather_sibling(got))
    W = ex.mix_in_weights(got)
    (lat, conv3, gl), got = _proj_fwd(hm, W["latT"], W["convT"], W["gateT"], tm=512, name="proj_fwd",
                                      hosted=ex.gather_chips("mix_misc"))
    p, got = _conv_fwd(conv3, conv_w, n_seq=n_seq, seq=seq, name="conv_fwd", hosted=ex.gather_sibling(got))
    W.update(ex.mix_misc_weights(got))
    q, k, v, qn, ckv = _mla_prep_fwd(lat, small["q_a_norm"], small["kv_a_norm"], ghq, ghk, W["wq"], W["wk"], W["wv"], rc, rs,
                                     tm=512, name="mla_prep_fwd")
    (o, lse), got = _flash_fwd(q, k, v, n_seq=n_seq, seq=seq, tq=TQ, name="attn_fwd", hosted=ex.gather_chips("ffn2"))
    (x2, merged, ya, yb), got = _merge_fwd(o, p, gl, small["gate_bias"], x1, W["wpa"], W["wpc"], W["wout"], tm=512, name="merge_fwd",
                                           hosted=ex.gather_sibling(got))
    f2g, f2u, f2d = ex.ffn_weights(got)
    (dy, h2, a2, b2, loss_row), _ = _ffn_fwd(x2, small["ffn2_norm"], f2g, f2u, f2d, tm=512, hc=DFF // 2, name="ffn2_fwd", target=tgt)

    gw, gs = {}, {}
    (da2, db2, *ffn2_grads), _ = _ffn_grads(dy, h2, a2, b2, f2d, tm=TM, hc=HC, name="ffn2_grads")
    (dx2, gs["ffn2_norm"]), _ = _ffn_up_bwd(da2, db2, f2g, f2u, x2, small["ffn2_norm"], dy, tm=512, name="ffn2_up_bwd")

    (dx2b, dya, dyb, dgl, do, dp, gs["gate_bias"]), got = _merge_bwd(
        dx2, ya, yb, gl, small["gate_bias"], W["wpa"], W["wpc"], W["wout"], tm=512, name="merge_bwd",
        hosted=ex.scatter_sibling("ffn2", ffn2_grads))
    ex.scatter_sibling_done("ffn2", got)
    gw["wout"] = wgrad(merged, dx2b, "dw_out")[0]
    gw["wpa"] = wgrad(o, dya, "dw_pa")[0]
    gw["wpc"] = wgrad(p, dyb, "dw_pc")[0]
    dconv3, dconv_w = _conv_bwd(dp, conv3, conv_w, n_seq=n_seq, seq=seq, name="conv_bwd")
    (dq, dk, dv), got = _flash_bwd(q, k, v, o, lse, do, n_seq=n_seq, seq=seq, tq=TQ, name="attn_bwd",
                                   hosted=ex.scatter_chips("ffn2"))
    ex.scatter_chips_done("ffn2", got)
    dlat, dqp, dkp, gs["q_a_norm"], gs["kv_a_norm"], dghq, dghk = _mla_prep_bwd(
        dq, dk, dv, lat, qn, ckv, small["q_a_norm"], small["kv_a_norm"], ghq, ghk, W["wq"], W["wk"], W["wv"], rc, rs,
        tm=512, name="mla_prep_bwd")
    gs["q_head_norm"], gs["k_head_norm"] = _head_cols_inv(dghq, QK_DIM), _head_cols_inv(dghk, QK_DIM)
    gw["wq"] = wgrad(qn, dqp, "dw_uq")[0]
    gw["wk"] = wgrad(ckv, dkp, "dw_uk")[0]
    gw["wv"] = wgrad(ckv, dv, "dw_uv")[0]
    gw["convT"] = wgrad(dconv3, hm, "dw_conv", tm=CONV_COLS // 2)[0]
    gw["gateT"] = wgrad(dgl, hm, "dw_gate")[0]
    gw["latT"] = wgrad(dlat, hm, "dw_lat")[0]
    ex.scatter_sibling_now("mix", gw)
    zero = ex.scatter_chips_start("mix_in")
    (dx1, gs["mix_norm"]), _ = _proj_bwd(dlat, dconv3, dgl, W["latT"], W["convT"], W["gateT"], x1, small["mix_norm"] + zero, dx2,
                                         tm=512, name="proj_bwd")

    (da1, db1, *ffn1_grads), got = _ffn_grads(dx1, h1, a1, b1, f1d, tm=TM, hc=HC, name="ffn1_grads",
                                              hosted=ex.scatter_chips("mix_misc"))
    ex.scatter_chips_done("mix_misc", got)
    ex.reduce_small(gs, dconv_w, loss_row)
    ex.scatter_sibling_now("ffn1", ffn1_grads)
    zero = ex.scatter_chips_start("ffn1")
    (dx0, gs["ffn1_norm"]), _ = _ffn_up_bwd(da1, db1, f1g, f1u, x0, small["ffn1_norm"] + zero, dx1, tm=512, name="ffn1_up_bwd")
    return dx0.reshape(n_seq, seq, d), gs["ffn1_norm"]


class _MeshExchange:
    def __init__(self, w, core, chip):
        self.w, self.core, self.chip = w, core, chip
        self.partial, self.received, self._cache, self._scattering = {}, {}, {}, {}

    def _blocks(self, group):
        w = self.w
        if group not in self._cache:
            if group.startswith("ffn"):
                self._cache[group] = [w[group + "_w_gate"].T.astype(BF16), w[group + "_w_up"].T.astype(BF16),
                                      w[group + "_w_down"].astype(BF16)]
            else:
                self._cache["mix_in"], self._cache["mix_misc"] = [[b] for b in _pack_weights(w)]
        return self._cache[group]

    def gather_chips(self, *groups):
        blocks = [b for group in groups for b in self._blocks(group)]
        return _gather_chips_plan(len(blocks)), blocks, _gather_shapes(blocks)

    def gather_sibling(self, got):
        half = list(got)
        return _gather_sibling_plan(len(half)), half, _same_shapes(half)

    def gather_start(self, group):
        blocks = self._blocks(group)
        plan = _gather_chips_plan(len(blocks))
        sems, blocks, lands, token = _plan_start(plan, blocks, _gather_shapes(blocks), name="gather_%s_start" % group)
        self._gathering = (group, plan, sems, blocks, lands)
        return token[0, 0]

    def gather_finish(self, after):
        group, plan, sems, blocks, lands = self._gathering
        _, half = _plan_wait(plan, sems, blocks, lands, after, name="gather_%s_wait" % group)
        return self.ffn_weights(_run_plan(_gather_sibling_plan(len(half)), half, _same_shapes(half), name="gather_%s_sibling" % group))

    def reduce_small(self, gs, dconv_w, loss_row):
        pieces = [_pad_cols(gs[n], SMALL_SLOTS[n]) for n in SMALL_NAMES[1:]] + [dconv_w.reshape(1, 3 * D), loss_row]
        self.small_total = _small_exchange(jnp.concatenate(pieces, axis=1).reshape(-1, 128), reduce=True,
                                           name="reduce_small").reshape(-1)

    def scatter_chips_start(self, group):
        s1 = self.partial[group]
        plan = _scatter_chips_plan(len(s1))
        sems, s1, lands, token = _plan_start(plan, s1, _scatter_shapes(s1), name="scatter_%s_start" % group)
        self._scattering[group] = (plan, sems, s1, lands)
        return token[0, 0]

    def scatter_chips_finish(self, group, after):
        plan, sems, s1, lands = self._scattering[group]
        self.partial[group], self.received[group] = _plan_wait(plan, sems, s1, lands, after, name="scatter_%s_wait" % group)

    def witness(self):
        parts = [b[:8, :128].astype(F32) for g in ("mix_in", "mix_misc", "ffn2") for b in self._blocks(g)]
        return functools.reduce(jnp.add, parts)

    def ffn_weights(self, got):
        return [a.reshape(DFF, D) for a in got]

    def mix_in_weights(self, got):
        return _unpack_in(got[0])

    def mix_misc_weights(self, got):
        return _unpack_misc(got[0])

    def _parts(self, group, grads):
        if group == "mix":
            return _pack_grads(grads), ["mix_in", "mix_misc"]
        parts = [g.reshape(N_DEV, -1, D) for g in grads]
        return parts, ([group] if len(parts) == 1 else None)

    def scatter_sibling(self, group, grads):
        self._sent, self._names = self._parts(group, grads)
        return _scatter_sibling_plan(len(self._sent)), self._sent, _halved_shapes(self._sent)

    def scatter_sibling_done(self, group, got):
        sums = list(_sum_sibling(self._sent, list(got), self.core, name="sum_%s_sibling" % group))
        if self._names is None:
            self.partial[group] = sums
        else:
            for n, s in zip(self._names, sums):
                self.partial[n] = [s]

    def scatter_sibling_now(self, group, grads):
        plan, parts, shapes = self.scatter_sibling(group, grads)
        self.scatter_sibling_done(group, _run_plan(plan, parts, shapes, name="scatter_%s_sibling" % group))

    def scatter_chips(self, group):
        s1 = self.partial[group]
        return _scatter_chips_plan(len(s1)), s1, _scatter_shapes(s1)

    def scatter_chips_done(self, group, got):
        self.received[group] = list(got)


SMALL_NAMES = ("ffn1_norm", "mix_norm", "gate_bias", "q_a_norm", "kv_a_norm", "q_head_norm", "k_head_norm", "ffn2_norm")
SMALL_SLOTS = {"ffn1_norm": 1024, "mix_norm": 1024, "gate_bias": 2048, "q_a_norm": 384, "kv_a_norm": 256, "q_head_norm": 128,
               "k_head_norm": 128, "ffn2_norm": 1024, "conv_w": 3072, "loss": 128}
COLUMN_MAJOR = ("w_in", "w_uq", "w_uk", "w_uv")
WEIGHT_NAMES = ("ffn1_norm", "ffn1_w_gate", "ffn1_w_up", "ffn1_w_down", "mix_norm", "w_in", "gate_bias", "q_a_norm", "w_uq",
                "kv_a_norm", "w_uk", "w_uv", "q_head_norm", "k_head_norm", "w_proj_attn", "conv_w", "w_proj_conv", "w_out",
                "ffn2_norm", "ffn2_w_gate", "ffn2_w_up", "ffn2_w_down")


def _step(x, positions, loss_target, w, m, v):
    xi, yi, ci = _place()
    core = ci.astype(jnp.int32).reshape(1)
    chip = (2 * xi + yi).astype(jnp.int32).reshape(1)
    me = 4 * xi + 2 * yi + ci

    ex = _MeshExchange(w, core, chip)
    cw_all = _small_exchange(jnp.pad(w["conv_w"], ((0, 5), (0, 0))), reduce=False, name="gather_conv_w")
    conv_w = cw_all[:, :3].transpose(1, 0, 2).reshape(3, D)
    ex.w = {n: (a + cw_all[0, 7, 0] if n.startswith("ffn1") else a) for n, a in w.items()}
    zero = ex.gather_start("ffn1")
    ex.w = {n: (a if n.startswith("ffn1") else a + zero) for n, a in w.items()}
    small = {n: w[n].reshape(1, -1) for n in SMALL_NAMES}

    grad_x, dffn1_norm = _local_step(x, positions + zero.astype(jnp.int32), loss_target, conv_w, small, ex)

    grads, deltas, new_m, new_v = {}, {}, {}, {}

    def ffn_update(group):
        names = (group + "_w_gate", group + "_w_up", group + "_w_down")
        views = [[a[n] if n.endswith("down") else a[n].T for n in names] for a in (w, m, v)]
        res = _sum_adamw(ex.partial[group], ex.received[group], chip, *views, name="adamw_" + group)
        for n, four in zip(names, res):
            grads[n], deltas[n], new_m[n], new_v[n] = (r if n.endswith("down") else r.T for r in four)

    def update(n):
        shape = w[n].shape
        if n in COLUMN_MAJOR:
            ops = [a.T for a in (w[n], grads[n], m[n], v[n])]
            deltas[n], new_m[n], new_v[n] = (r.T for r in _adamw(*ops, name="adamw_" + n))
            return
        view = shape if len(shape) == 2 else ((-1, 128) if shape[0] % 128 == 0 else (1, shape[0]))
        dlt, nm, nv = _adamw(w[n].reshape(view), grads[n].reshape(view), m[n].reshape(view), v[n].reshape(view), name="adamw_" + n)
        deltas[n], new_m[n], new_v[n] = dlt.reshape(shape), nm.reshape(shape), nv.reshape(shape)

    ffn_update("ffn2")
    ex.scatter_chips_finish("mix_in", dffn1_norm)
    grads.update(_unpack_grads([_sum_chips(ex.partial[g][0], ex.received[g][0], chip, name="sum_%s_chips" % g)
                                for g in ("mix_in", "mix_misc")]))
    total, off = ex.small_total, 0
    for n in SMALL_NAMES[1:]:
        grads[n] = total[off:off + w[n].shape[0]]
        off += SMALL_SLOTS[n]
    conv_full = total[off:off + 3 * D].reshape(3, D)
    grads["conv_w"] = lax.dynamic_slice(conv_full, (0, me * HEAD_PAD), (3, HEAD_PAD))
    loss = total[off + 3 * D]
    tiny = SMALL_NAMES[1:] + ("conv_w", "w_uq", "w_uk", "w_uv", "w_proj_attn", "w_proj_conv", "w_out")

    def view(n, a):
        if a.ndim == 2:
            return a.T if n in COLUMN_MAJOR else a
        return a.reshape((-1, 128) if a.size % 128 == 0 else (1, a.size))

    def unview(n, a):
        return (a.T if n in COLUMN_MAJOR else a) if w[n].ndim == 2 else a.reshape(w[n].shape)

    res = _adamw_small(*[[view(n, a[n]) for n in tiny] for a in (w, grads, m, v)], name="adamw_small")
    for out, arrs in zip((deltas, new_m, new_v), res):
        out.update({n: unview(n, a) for n, a in zip(tiny, arrs)})
    later = ("ffn1_norm", "ffn1_w_gate", "ffn1_w_up", "ffn1_w_down")
    for n in WEIGHT_NAMES:
        if n not in deltas and n not in later:
            update(n)

    done = [deltas[n][:8, :128] for n in ("ffn2_w_down", "w_in", "w_out", "w_proj_attn")] + [deltas["mix_norm"].reshape(8, 128)]
    ex.scatter_chips_finish("ffn1", functools.reduce(jnp.add, done) + grad_x.reshape(-1, D)[:8, :128])
    ffn_update("ffn1")
    last = dffn1_norm + 0.0 * grads["ffn1_w_down"][:1, :1]
    grads["ffn1_norm"] = _small_exchange(last.reshape(-1, 128), reduce=True, name="reduce_ffn1_norm").reshape(-1)
    update("ffn1_norm")
    return (loss, grad_x, *[grads[n] for n in WEIGHT_NAMES], *[deltas[n] for n in WEIGHT_NAMES],
            *[new_m[n] for n in WEIGHT_NAMES], *[new_v[n] for n in WEIGHT_NAMES])


def kernel(x, positions, ffn1_norm, ffn1_w_gate, ffn1_w_up, ffn1_w_down, mix_norm, w_in, gate_bias, q_a_norm, w_uq, kv_a_norm, w_uk, w_uv, q_head_norm, k_head_norm, w_proj_attn, conv_w, w_proj_conv, w_out, ffn2_norm, ffn2_w_gate, ffn2_w_up, ffn2_w_down, loss_target, m_ffn1_norm, m_ffn1_w_gate, m_ffn1_w_up, m_ffn1_w_down, m_mix_norm, m_w_in, m_gate_bias, m_q_a_norm, m_w_uq, m_kv_a_norm, m_w_uk, m_w_uv, m_q_head_norm, m_k_head_norm, m_w_proj_attn, m_conv_w, m_w_proj_conv, m_w_out, m_ffn2_norm, m_ffn2_w_gate, m_ffn2_w_up, m_ffn2_w_down, v_ffn1_norm, v_ffn1_w_gate, v_ffn1_w_up, v_ffn1_w_down, v_mix_norm, v_w_in, v_gate_bias, v_q_a_norm, v_w_uq, v_kv_a_norm, v_w_uk, v_w_uv, v_q_head_norm, v_k_head_norm, v_w_proj_attn, v_conv_w, v_w_proj_conv, v_w_out, v_ffn2_norm, v_ffn2_w_gate, v_ffn2_w_up, v_ffn2_w_down):
    given = dict(locals())
    w = {n: given[n] for n in WEIGHT_NAMES}
    m = {n: given["m_" + n] for n in WEIGHT_NAMES}
    v = {n: given["v_" + n] for n in WEIGHT_NAMES}
    return _step(x, positions, loss_target, w, m, v)
```

```python
import functools

import jax
import jax.numpy as jnp
from jax import lax
from jax.experimental import pallas as pl
from jax.experimental.pallas import tpu as pltpu

F32 = jnp.float32
BF16 = jnp.bfloat16
MESH = pl.DeviceIdType.MESH
ANY = pl.BlockSpec(memory_space=pl.ANY)

N_DEV = 8
D = 1024
DFF = 2816
N_HEADS = 8
HEAD_PAD = 128
QK_DIM = 96
NOPE = 64
ROPE_HALF = 16
Q_LORA = 384
KV_LORA = 256
LAT_PAD = 768
CONV_COLS = 3072
GATE_COLS = 2048
IN_DIM = 5792
IN_SHARD = IN_DIM // N_DEV
IN_SHARD_PAD = 736
FF_SHARD = DFF // N_DEV
ROPE_THETA = 10000.0
NORM_EPS = 1e-6
ATTN_SCALE = QK_DIM ** -0.5
NEG = -1e30

ADAM_LR, ADAM_B1, ADAM_B2, ADAM_EPS, ADAM_WD, ADAM_STEP = 0.001, 0.9, 0.999, 1e-08, 0.01, 10

PACK = ((("w_inT", IN_SHARD_PAD),), (("w_uq", 48), ("w_uk", 32), ("w_uv", 32), ("w_pa", 64), ("w_pc", 128), ("w_out", 128)))
PACK_OFF = {}
for _i, _group in enumerate(PACK):
    _o = 0
    for _n, _r in _group:
        PACK_OFF[_n] = (_i, _o, _r)
        _o += _r

VMEM_LIMIT = 56 * 1024 * 1024


def _params(*sem):
    return pltpu.CompilerParams(dimension_semantics=sem if sem else None, vmem_limit_bytes=VMEM_LIMIT)


class _Plan:
    def __init__(self, start, wait, n_remote, n_local, in_place=False):
        self.start, self.wait, self.n_remote, self.n_local, self.in_place = start, wait, n_remote, n_local, in_place

    def sems(self):
        return [pltpu.SemaphoreType.DMA((self.n_remote,)), pltpu.SemaphoreType.DMA((self.n_remote,)),
                pltpu.SemaphoreType.DMA((max(self.n_local, 1),))]


def _call(body, *, name, grid, in_specs, out_specs, out_shape, scratch_shapes, operands, sem, hosted=None):
    if hosted is None:
        outs = pl.pallas_call(body, name=name, grid=grid, in_specs=in_specs, out_specs=out_specs, out_shape=out_shape,
                              scratch_shapes=scratch_shapes, compiler_params=_params(*sem))(*operands)
        return outs, None
    plan, srcs, h_shapes = hosted
    n_in, n_out, n_scr, nh_in, nh_out = len(in_specs), len(out_specs), len(scratch_shapes), len(srcs), len(h_shapes)
    aliases = {n_in + a: n_out + a for a in range(nh_in)} if plan.in_place else {}

    def full_body(*refs):
        ins, refs = refs[:n_in], refs[n_in:]
        h_in, refs = refs[:nh_in], refs[nh_in:]
        outs, refs = refs[:n_out], refs[n_out:]
        h_out, refs = refs[:nh_out], refs[nh_out:]
        scr, sems = refs[:n_scr], refs[n_scr:]
        ids = [pl.program_id(ax) for ax in range(len(grid))]
        first = functools.reduce(jnp.logical_and, [i == 0 for i in ids])
        last = functools.reduce(jnp.logical_and, [i == g - 1 for i, g in zip(ids, grid)])

        @pl.when(first)
        def _():
            plan.start(h_in, h_out, *sems)

        body(*ins, *outs, *scr)

        @pl.when(last)
        def _():
            plan.wait(h_in, h_out, *sems)

    res = pl.pallas_call(
        full_body, name=name, grid=grid, in_specs=list(in_specs) + [ANY] * nh_in, out_specs=list(out_specs) + [ANY] * nh_out,
        out_shape=list(out_shape) + list(h_shapes), scratch_shapes=list(scratch_shapes) + plan.sems(),
        input_output_aliases=aliases, compiler_params=_params(*(["arbitrary"] * len(grid))),
    )(*operands, *srcs)
    return res[:n_out], res[n_out:]


def _dot_nn(a, b):
    return lax.dot_general(a, b, (((1,), (0,)), ((), ())), preferred_element_type=F32)


def _dot_nt(a, b):
    return lax.dot_general(a, b, (((1,), (1,)), ((), ())), preferred_element_type=F32)


def _dot_tn(a, b):
    return lax.dot_general(a, b, (((0,), (0,)), ((), ())), preferred_element_type=F32)


def _sigmoid(x):
    return 0.5 * jnp.tanh(0.5 * x) + 0.5


def _rms_stats(x):
    r = lax.rsqrt(jnp.mean(x * x, axis=-1, keepdims=True) + NORM_EPS)
    return x * r, r


ROWS_WIDE = 16
MM_ROWS = 256


def _rms_bwd(dy, xhat, r, g):
    dg = jnp.sum(dy * xhat, axis=0, keepdims=True)
    dxh = dy * g
    dx = r * (dxh - xhat * jnp.mean(dxh * xhat, axis=-1, keepdims=True))
    return dx, dg


def _mm(a, b, *, mode, out_dtype, tm, tn, tk, name, add=None, scale=1.0, hosted=None):
    if mode == "nn":
        (m, k), (_, n) = a.shape, b.shape
    elif mode == "nt":
        (m, k), (n, _) = a.shape, b.shape
    else:
        (k, m), (_, n) = a.shape, b.shape
    assert m % tm == 0 and n % tn == 0 and k % tk == 0, (name, m, n, k, tm, tn, tk)
    nk = k // tk
    dot = {"nn": _dot_nn, "nt": _dot_nt, "tn": _dot_tn}[mode]
    a_spec = pl.BlockSpec((tk, tm), lambda i, j, kk: (kk, i)) if mode == "tn" else pl.BlockSpec((tm, tk), lambda i, j, kk: (i, kk))
    b_spec = pl.BlockSpec((tn, tk), lambda i, j, kk: (j, kk)) if mode == "nt" else pl.BlockSpec((tk, tn), lambda i, j, kk: (kk, j))
    o_spec = pl.BlockSpec((tm, tn), lambda i, j, kk: (i, j))
    has_add = add is not None

    def finish(prod, c_ref, o_ref):
        if scale != 1.0:
            prod = prod * scale
        o_ref[...] = ((c_ref[...] + prod) if has_add else prod).astype(out_dtype)

    def body(*refs):
        a_ref, b_ref = refs[:2]
        c_ref = refs[2] if has_add else None
        o_ref = refs[3] if has_add else refs[2]
        if nk == 1:
            finish(dot(a_ref[...], b_ref[...]), c_ref, o_ref)
            return
        acc_ref = refs[-1]
        kk = pl.program_id(2)

        @pl.when(kk == 0)
        def _():
            acc_ref[...] = jnp.zeros_like(acc_ref)

        acc_ref[...] += dot(a_ref[...], b_ref[...])

        @pl.when(kk == nk - 1)
        def _():
            finish(acc_ref[...], c_ref, o_ref)

    operands = (a, b, add) if has_add else (a, b)
    in_specs = [a_spec, b_spec] + ([o_spec] if has_add else [])
    (out,), got = _call(
        body, name=name, grid=(m // tm, n // tn, nk), in_specs=in_specs, out_specs=[o_spec],
        out_shape=[jax.ShapeDtypeStruct((m, n), out_dtype)], scratch_shapes=[pltpu.VMEM((tm, tn), F32)] if nk > 1 else [],
        operands=operands, sem=("parallel", "parallel", "arbitrary"), hosted=hosted)
    return out if hosted is None else (out, got)


def _rms_fwd(x, g, *, tm, name, hosted=None):
    t, d = x.shape

    def body(x_ref, g_ref, h_ref):
        xhat, _ = _rms_stats(x_ref[...])
        h_ref[...] = (xhat * g_ref[...]).astype(BF16)

    (h,), got = _call(
        body, name=name, grid=(t // tm,),
        in_specs=[pl.BlockSpec((tm, d), lambda i: (i, 0)), pl.BlockSpec((1, d), lambda i: (0, 0))],
        out_specs=[pl.BlockSpec((tm, d), lambda i: (i, 0))], out_shape=[jax.ShapeDtypeStruct((t, d), BF16)], scratch_shapes=[],
        operands=(x, g), sem=("parallel",), hosted=hosted)
    return h, got


def _ffn_fwd(x, g, wgT, wuT, wd, *, tm, hc, name, hosted=None, target=None):
    t, d = x.shape
    nj = DFF // hc
    with_loss = target is not None

    def body(*refs):
        x_ref, g_ref, wg_ref, wu_ref, wd_ref = refs[:5]
        t_ref = refs[5] if with_loss else None
        xo_ref, h_ref, a_ref, b_ref = refs[5 + with_loss:9 + with_loss]
        loss_ref = refs[9 + with_loss] if with_loss else None
        acc_ref = refs[-1]
        i, j = pl.program_id(0), pl.program_id(1)

        @pl.when(j == 0)
        def _():
            xhat, _ = _rms_stats(x_ref[...])
            h_ref[...] = (xhat * g_ref[...]).astype(BF16)
            acc_ref[...] = jnp.zeros_like(acc_ref)

        h = h_ref[...]
        a = _dot_nt(h, wg_ref[...])
        b = _dot_nt(h, wu_ref[...])
        a_ref[...] = a.astype(BF16)
        b_ref[...] = b.astype(BF16)
        s = (a * _sigmoid(a) * b).astype(BF16)
        acc_ref[...] += _dot_nn(s, wd_ref[...])

        if with_loss:
            @pl.when((i == 0) & (j == 0))
            def _():
                loss_ref[...] = jnp.zeros_like(loss_ref)

        @pl.when(j == nj - 1)
        def _():
            y = x_ref[...] + 0.5 * acc_ref[...]
            if with_loss:
                err = y - t_ref[...]
                xo_ref[...] = err * (1.0 / d)
                loss_ref[...] += jnp.sum(jnp.sum(err * err, axis=-1, keepdims=True), axis=0, keepdims=True) * (0.5 / d)
            else:
                xo_ref[...] = y

    row = pl.BlockSpec((tm, d), lambda i, j: (i, 0))
    vec = pl.BlockSpec((1, d), lambda i, j: (0, 0))
    wsp = pl.BlockSpec((hc, d), lambda i, j: (j, 0))
    hid = pl.BlockSpec((tm, hc), lambda i, j: (i, j))
    out_specs = [row, row, hid, hid] + ([pl.BlockSpec((1, 128), lambda i, j: (0, 0))] if with_loss else [])
    out_shape = [jax.ShapeDtypeStruct((t, d), F32), jax.ShapeDtypeStruct((t, d), BF16), jax.ShapeDtypeStruct((t, DFF), BF16),
                 jax.ShapeDtypeStruct((t, DFF), BF16)] + ([jax.ShapeDtypeStruct((1, 128), F32)] if with_loss else [])
    return _call(
        body, name=name, grid=(t // tm, nj), in_specs=[row, vec, wsp, wsp, wsp] + ([row] if with_loss else []),
        out_specs=out_specs, out_shape=out_shape, scratch_shapes=[pltpu.VMEM((tm, d), F32)],
        operands=(x, g, wgT, wuT, wd) + ((target,) if with_loss else ()),
        sem=("arbitrary" if with_loss else "parallel", "arbitrary"), hosted=hosted)


def _ffn_grads(dout, h, a, b, wd, *, tm, hc, name, hosted=None):
    t, d = dout.shape
    ni, nj = t // tm, DFF // hc

    def body(dout_ref, h_ref, a_ref, b_ref, wd_ref, da_ref, db_ref, dwg_ref, dwu_ref, dwd_ref,
             dy_all, h_all, ds_scr, s_scr, acc_g, acc_u, acc_d):
        j, i = pl.program_id(0), pl.program_id(1)
        rows_i = pl.ds(pl.multiple_of(i * tm, tm), tm)

        @pl.when(j == 0)
        def _():
            dy_all[rows_i, :] = (0.5 * dout_ref[...]).astype(BF16)
            h_all[rows_i, :] = h_ref[...]

        @pl.when(i == 0)
        def _():
            acc_g[...] = jnp.zeros_like(acc_g)
            acc_u[...] = jnp.zeros_like(acc_u)
            acc_d[...] = jnp.zeros_like(acc_d)

        def grad_rows(rows):
            ds = ds_scr[rows, :]
            av = a_ref[rows, :].astype(F32)
            bv = b_ref[rows, :].astype(F32)
            sg = _sigmoid(av)
            sl = av * sg
            s_scr[rows, :] = (sl * bv).astype(BF16)
            da_ref[rows, :] = (ds * bv * (sg + sl * (1.0 - sg))).astype(BF16)
            db_ref[rows, :] = (ds * sl).astype(BF16)

        for blk in range(tm // MM_ROWS):
            rs = slice(blk * MM_ROWS, (blk + 1) * MM_ROWS)
            ds_scr[rs, :] = _dot_nt(dy_all[pl.ds(pl.multiple_of(i * tm + blk * MM_ROWS, MM_ROWS), MM_ROWS), :], wd_ref[...])
            for c in range(MM_ROWS // ROWS_WIDE):
                grad_rows(slice(blk * MM_ROWS + c * ROWS_WIDE, blk * MM_ROWS + (c + 1) * ROWS_WIDE))

        dy_i = dy_all[rows_i, :]
        h_i = h_all[rows_i, :]
        acc_d[...] += _dot_tn(s_scr[...], dy_i)
        acc_g[...] += _dot_tn(da_ref[...], h_i)
        acc_u[...] += _dot_tn(db_ref[...], h_i)

        @pl.when(i == ni - 1)
        def _():
            dwg_ref[...] = acc_g[...].astype(BF16)
            dwu_ref[...] = acc_u[...].astype(BF16)
            dwd_ref[...] = acc_d[...].astype(BF16)

    first = pl.BlockSpec((tm, d), lambda j, i: (jnp.where(j == 0, i, 0), 0))
    hid = pl.BlockSpec((tm, hc), lambda j, i: (i, j))
    wsp = pl.BlockSpec((hc, d), lambda j, i: (j, 0))
    hid_shape = jax.ShapeDtypeStruct((t, DFF), BF16)
    w_shape = jax.ShapeDtypeStruct((DFF, d), BF16)
    return _call(
        body, name=name, grid=(nj, ni), in_specs=[first, first, hid, hid, wsp], out_specs=[hid, hid, wsp, wsp, wsp],
        out_shape=[hid_shape, hid_shape, w_shape, w_shape, w_shape],
        scratch_shapes=[pltpu.VMEM((t, d), BF16), pltpu.VMEM((t, d), BF16), pltpu.VMEM((tm, hc), F32), pltpu.VMEM((tm, hc), BF16),
                        pltpu.VMEM((hc, d), F32), pltpu.VMEM((hc, d), F32), pltpu.VMEM((hc, d), F32)],
        operands=(dout, h, a, b, wd), sem=("arbitrary", "arbitrary"), hosted=hosted)


def _proj_fwd(h, latT, convT, gateT, *, tm, name, hosted=None):
    t, d = h.shape

    def body(h_ref, wl_ref, wc_ref, wg_ref, lat_ref, conv_ref, gl_ref):
        hv = h_ref[...]
        lat_ref[...] = _dot_nt(hv, wl_ref[...]).astype(BF16)
        conv_ref[...] = _dot_nt(hv, wc_ref[...]).astype(BF16)
        gl_ref[...] = _dot_nt(hv, wg_ref[...]).astype(BF16)

    def rows(w):
        return pl.BlockSpec((tm, w), lambda i: (i, 0))

    def full(r):
        return pl.BlockSpec((r, d), lambda i: (0, 0))

    return _call(
        body, name=name, grid=(t // tm,), in_specs=[rows(d), full(LAT_PAD), full(CONV_COLS), full(GATE_COLS)],
        out_specs=[rows(LAT_PAD), rows(CONV_COLS), rows(GATE_COLS)],
        out_shape=[jax.ShapeDtypeStruct((t, LAT_PAD), BF16), jax.ShapeDtypeStruct((t, CONV_COLS), BF16),
                   jax.ShapeDtypeStruct((t, GATE_COLS), BF16)],
        scratch_shapes=[], operands=(h, latT, convT, gateT), sem=("parallel",), hosted=hosted)


def _proj_bwd(dlat, dconv3, dgl, latT, convT, gateT, x, g, dres, *, tm, name, hosted=None):
    t, d = x.shape

    def body(dl_ref, dc_ref, dg_ref, wl_ref, wc_ref, wg_ref, x_ref, g_ref, dres_ref, dx_ref, dgain_ref):
        @pl.when(pl.program_id(0) == 0)
        def _():
            dgain_ref[...] = jnp.zeros_like(dgain_ref)

        dh = _dot_nn(dl_ref[...], wl_ref[...]) + _dot_nn(dc_ref[...], wc_ref[...]) + _dot_nn(dg_ref[...], wg_ref[...])
        xhat, r = _rms_stats(x_ref[...])
        dx, dgain = _rms_bwd(dh, xhat, r, g_ref[...])
        dx_ref[...] = dres_ref[...] + dx
        dgain_ref[...] += dgain

    def rows(w):
        return pl.BlockSpec((tm, w), lambda i: (i, 0))

    def full(r):
        return pl.BlockSpec((r, d), lambda i: (0, 0))

    return _call(
        body, name=name, grid=(t // tm,),
        in_specs=[rows(LAT_PAD), rows(CONV_COLS), rows(GATE_COLS), full(LAT_PAD), full(CONV_COLS), full(GATE_COLS), rows(d), full(1), rows(d)],
        out_specs=[rows(d), full(1)], out_shape=[jax.ShapeDtypeStruct((t, d), F32), jax.ShapeDtypeStruct((1, d), F32)],
        scratch_shapes=[], operands=(dlat, dconv3, dgl, latT, convT, gateT, x, g, dres), sem=("arbitrary",), hosted=hosted)


def _ffn_up_bwd(da, db, wgT, wuT, x, g, dout, *, tm, name, hosted=None):
    t, d = x.shape

    def body(da_ref, db_ref, wg_ref, wu_ref, x_ref, g_ref, dout_ref, dx_ref, dg_ref):
        @pl.when(pl.program_id(0) == 0)
        def _():
            dg_ref[...] = jnp.zeros_like(dg_ref)

        dh = _dot_nn(da_ref[...], wg_ref[...]) + _dot_nn(db_ref[...], wu_ref[...])
        xhat, r = _rms_stats(x_ref[...])
        dx, dg = _rms_bwd(dh, xhat, r, g_ref[...])
        dx_ref[...] = dout_ref[...] + dx
        dg_ref[...] += dg

    row = pl.BlockSpec((tm, d), lambda i: (i, 0))
    vec = pl.BlockSpec((1, d), lambda i: (0, 0))
    hid = pl.BlockSpec((tm, DFF), lambda i: (i, 0))
    wsp = pl.BlockSpec((DFF, d), lambda i: (0, 0))
    return _call(
        body, name=name, grid=(t // tm,), in_specs=[hid, hid, wsp, wsp, row, vec, row], out_specs=[row, vec],
        out_shape=[jax.ShapeDtypeStruct((t, d), F32), jax.ShapeDtypeStruct((1, d), F32)], scratch_shapes=[],
        operands=(da, db, wgT, wuT, x, g, dout), sem=("arbitrary",), hosted=hosted)


HEAD_LANES = (slice(0, 32), slice(64, 80), None, slice(32, 64), slice(80, 96), None)


def _head_cols(a):
    def part(sl, width):
        if sl is None or sl.stop > a.shape[1]:
            return jnp.zeros((a.shape[0], width), a.dtype)
        return a[:, sl]

    return jnp.concatenate([part(sl, w) for sl, w in zip(HEAD_LANES, (32, 16, 16, 32, 16, 16))], axis=1)


def _head_cols_inv(a, dims):
    parts = [a[:, 0:32], a[:, 64:96]] + ([a[:, 32:48], a[:, 96:112]] if dims == QK_DIM else [])
    return jnp.concatenate(parts, axis=1)


def _rope_fwd(x, c, s):
    return x * c + pltpu.roll(x, HEAD_PAD // 2, 1) * s


def _rope_bwd(dy, c, s):
    return dy * c + pltpu.roll(dy * s, HEAD_PAD // 2, 1)


def _head_stats(x):
    r = lax.rsqrt(jnp.sum(x * x, axis=-1, keepdims=True) * (1.0 / QK_DIM) + NORM_EPS)
    return x * r, r


def _mla_prep_fwd(lat, gq, gkv, ghq, ghk, wq, wk, wv, rc, rs, *, tm, name):
    t = lat.shape[0]

    def body(lat_ref, gq_ref, gkv_ref, ghq_ref, ghk_ref, wq_ref, wk_ref, wv_ref, c_ref, s_ref,
             q_ref, k_ref, v_ref, qn_ref, ckv_ref):
        lat_v = lat_ref[...]
        qhat, _ = _rms_stats(lat_v[:, :Q_LORA].astype(F32))
        qn = (qhat * gq_ref[...]).astype(BF16)
        khat, _ = _rms_stats(lat_v[:, Q_LORA:Q_LORA + KV_LORA].astype(F32))
        ckv = (khat * gkv_ref[...]).astype(BF16)
        ckv_ext = jnp.concatenate([ckv, lat_v[:, Q_LORA + KV_LORA:]], axis=1)
        qn_ref[...] = qn
        ckv_ref[...] = ckv_ext
        q_pre = _dot_nn(qn, wq_ref[...])
        k_pre = _dot_nn(ckv_ext, wk_ref[...])
        v_ref[...] = _dot_nn(ckv, wv_ref[...]).astype(BF16)
        c, s = c_ref[...], s_ref[...]
        for h in range(N_HEADS):
            hs = slice(h * HEAD_PAD, (h + 1) * HEAD_PAD)
            xq, _ = _head_stats(q_pre[:, hs])
            q_ref[:, hs] = _rope_fwd(xq * ghq_ref[...], c, s).astype(BF16)
            xk, _ = _head_stats(k_pre[:, hs])
            k_ref[:, hs] = _rope_fwd(xk * ghk_ref[...], c, s).astype(BF16)

    def row(w):
        return pl.BlockSpec((tm, w), lambda i: (i, 0))

    def full(r, w):
        return pl.BlockSpec((r, w), lambda i: (0, 0))

    wide = jax.ShapeDtypeStruct((t, D), BF16)
    lat3 = jax.ShapeDtypeStruct((t, Q_LORA), BF16)
    return pl.pallas_call(
        body, name=name, grid=(t // tm,),
        in_specs=[row(LAT_PAD), full(1, Q_LORA), full(1, KV_LORA), full(1, HEAD_PAD), full(1, HEAD_PAD),
                  full(Q_LORA, D), full(Q_LORA, D), full(KV_LORA, D), row(HEAD_PAD), row(HEAD_PAD)],
        out_specs=[row(D), row(D), row(D), row(Q_LORA), row(Q_LORA)],
        out_shape=[wide, wide, wide, lat3, lat3],
        compiler_params=_params("parallel"),
    )(lat, gq, gkv, ghq, ghk, wq, wk, wv, rc, rs)


def _mla_prep_bwd(dq, dk, dv, lat, qn, ckv_ext, gq, gkv, ghq, ghk, wq, wk, wv, rc, rs, *, tm, name):
    t = lat.shape[0]

    def body(dq_ref, dk_ref, dv_ref, lat_ref, qn_ref, ckv_ref, gq_ref, gkv_ref, ghq_ref, ghk_ref, wq_ref, wk_ref, wv_ref,
             c_ref, s_ref, dlat_ref, dqp_ref, dkp_ref, dgq_ref, dgkv_ref, dghq_ref, dghk_ref):
        @pl.when(pl.program_id(0) == 0)
        def _():
            dgq_ref[...] = jnp.zeros_like(dgq_ref)
            dgkv_ref[...] = jnp.zeros_like(dgkv_ref)
            dghq_ref[...] = jnp.zeros_like(dghq_ref)
            dghk_ref[...] = jnp.zeros_like(dghk_ref)

        c, s = c_ref[...], s_ref[...]
        q_pre = _dot_nn(qn_ref[...], wq_ref[...])
        k_pre = _dot_nn(ckv_ref[...], wk_ref[...])

        def heads(pre, dy_ref, gh_ref, dgh_ref, out_ref):
            dgh = jnp.zeros((1, HEAD_PAD), F32)
            for h in range(N_HEADS):
                hs = slice(h * HEAD_PAD, (h + 1) * HEAD_PAD)
                d = _rope_bwd(dy_ref[:, hs].astype(F32), c, s)
                xhat, r = _head_stats(pre[:, hs])
                dgh = dgh + jnp.sum(d * xhat, axis=0, keepdims=True)
                dxh = d * gh_ref[...]
                dx = r * (dxh - xhat * (jnp.sum(dxh * xhat, axis=-1, keepdims=True) * (1.0 / QK_DIM)))
                out_ref[:, hs] = dx.astype(BF16)
            dgh_ref[...] += dgh

        heads(q_pre, dq_ref, ghq_ref, dghq_ref, dqp_ref)
        heads(k_pre, dk_ref, ghk_ref, dghk_ref, dkp_ref)
        dqn = _dot_nt(dqp_ref[...], wq_ref[...])
        dce = _dot_nt(dkp_ref[...], wk_ref[...])
        dckv = dce[:, :KV_LORA] + _dot_nt(dv_ref[...], wv_ref[...])
        lat_v = lat_ref[...]
        qhat, rq = _rms_stats(lat_v[:, :Q_LORA].astype(F32))
        dql, dgq = _rms_bwd(dqn, qhat, rq, gq_ref[...])
        khat, rk = _rms_stats(lat_v[:, Q_LORA:Q_LORA + KV_LORA].astype(F32))
        dkl, dgkv = _rms_bwd(dckv, khat, rk, gkv_ref[...])
        dgq_ref[...] += dgq
        dgkv_ref[...] += dgkv
        dlat_ref[...] = jnp.concatenate([dql, dkl, dce[:, KV_LORA:]], axis=1).astype(BF16)

    def row(w):
        return pl.BlockSpec((tm, w), lambda i: (i, 0))

    def full(r, w):
        return pl.BlockSpec((r, w), lambda i: (0, 0))

    return pl.pallas_call(
        body, name=name, grid=(t // tm,),
        in_specs=[row(D), row(D), row(D), row(LAT_PAD), row(Q_LORA), row(Q_LORA), full(1, Q_LORA), full(1, KV_LORA),
                  full(1, HEAD_PAD), full(1, HEAD_PAD), full(Q_LORA, D), full(Q_LORA, D), full(KV_LORA, D),
                  row(HEAD_PAD), row(HEAD_PAD)],
        out_specs=[row(LAT_PAD), row(D), row(D), full(1, Q_LORA), full(1, KV_LORA), full(1, HEAD_PAD), full(1, HEAD_PAD)],
        out_shape=[jax.ShapeDtypeStruct((t, LAT_PAD), BF16), jax.ShapeDtypeStruct((t, D), BF16), jax.ShapeDtypeStruct((t, D), BF16),
                   jax.ShapeDtypeStruct((1, Q_LORA), F32), jax.ShapeDtypeStruct((1, KV_LORA), F32),
                   jax.ShapeDtypeStruct((1, HEAD_PAD), F32), jax.ShapeDtypeStruct((1, HEAD_PAD), F32)],
        compiler_params=_params("arbitrary"),
    )(dq, dk, dv, lat, qn, ckv_ext, gq, gkv, ghq, ghk, wq, wk, wv, rc, rs)


def _causal_keep(tq):
    r = lax.broadcasted_iota(jnp.int32, (tq, tq), 0)
    c = lax.broadcasted_iota(jnp.int32, (tq, tq), 1)
    return c <= r


def _flash_fwd(q, k, v, *, n_seq, seq, tq, name, hosted=None):
    nq = seq // tq

    def body(q_ref, k_ref, v_ref, o_ref, lse_ref):
        for qi in range(nq):
            rows = slice(qi * tq, (qi + 1) * tq)
            qv = q_ref[rows, :]
            m = jnp.full((tq, 1), NEG, F32)
            l = jnp.zeros((tq, 1), F32)
            acc = jnp.zeros((tq, HEAD_PAD), F32)
            for j in range(qi + 1):
                cols = slice(j * tq, (j + 1) * tq)
                s = _dot_nt(qv, k_ref[cols, :]) * ATTN_SCALE
                if j == qi:
                    s = jnp.where(_causal_keep(tq), s, NEG)
                m_new = jnp.maximum(m, jnp.max(s, axis=-1, keepdims=True))
                alpha = jnp.exp(m - m_new)
                p = jnp.exp(s - m_new)
                l = alpha * l + jnp.sum(p, axis=-1, keepdims=True)
                acc = alpha * acc + _dot_nn(p.astype(BF16), v_ref[cols, :])
                m = m_new
            o_ref[rows, :] = (acc / l).astype(BF16)
            lse_ref[rows, :] = jnp.broadcast_to(m + jnp.log(l), (tq, HEAD_PAD))

    spec = pl.BlockSpec((seq, HEAD_PAD), lambda b, h: (b, h))
    t = n_seq * seq
    return _call(
        body, name=name, grid=(n_seq, N_HEADS), in_specs=[spec, spec, spec], out_specs=[spec, spec],
        out_shape=[jax.ShapeDtypeStruct((t, D), BF16), jax.ShapeDtypeStruct((t, D), F32)], scratch_shapes=[],
        operands=(q, k, v), sem=("parallel", "parallel"), hosted=hosted)


def _flash_bwd(q, k, v, o, lse, do, *, n_seq, seq, tq, name, hosted=None):
    nq = seq // tq

    def body(q_ref, k_ref, v_ref, o_ref, lse_ref, do_ref, dq_ref, dk_ref, dv_ref, dk_acc, dv_acc):
        j = pl.program_id(2)

        @pl.when(j == 0)
        def _():
            dq_ref[...] = jnp.zeros_like(dq_ref)

        dk_acc[...] = jnp.zeros_like(dk_acc)
        dv_acc[...] = jnp.zeros_like(dv_acc)
        kv = k_ref[...]
        vv = v_ref[...]

        def step(i, masked):
            rows = pl.ds(pl.multiple_of(i * tq, tq), tq)
            qi = q_ref[rows, :]
            doi = do_ref[rows, :]
            delta = jnp.sum(doi.astype(F32) * o_ref[rows, :].astype(F32), axis=-1, keepdims=True)
            s = _dot_nt(qi, kv) * ATTN_SCALE
            p = jnp.exp(s - lse_ref[rows, :][:, :1])
            if masked:
                p = jnp.where(_causal_keep(tq), p, 0.0)
            dv_acc[...] += _dot_tn(p.astype(BF16), doi)
            dp = _dot_nt(doi, vv)
            ds = (p * (dp - delta) * ATTN_SCALE).astype(BF16)
            dk_acc[...] += _dot_tn(ds, qi)
            dq_ref[rows, :] += _dot_nn(ds, kv)

        step(j, True)

        def loop_body(i, carry):
            step(i, False)
            return carry

        lax.fori_loop(j + 1, nq, loop_body, 0)
        dk_ref[...] = dk_acc[...]
        dv_ref[...] = dv_acc[...].astype(BF16)

    full = pl.BlockSpec((seq, HEAD_PAD), lambda b, h, j: (b, h))
    tile = pl.BlockSpec((tq, HEAD_PAD), lambda b, h, j: (b * nq + j, h))
    t = n_seq * seq
    return _call(
        body, name=name, grid=(n_seq, N_HEADS, nq), in_specs=[full, tile, tile, full, full, full],
        out_specs=[full, tile, tile],
        out_shape=[jax.ShapeDtypeStruct((t, D), F32), jax.ShapeDtypeStruct((t, D), F32), jax.ShapeDtypeStruct((t, D), BF16)],
        scratch_shapes=[pltpu.VMEM((tq, HEAD_PAD), F32), pltpu.VMEM((tq, HEAD_PAD), F32)],
        operands=(q, k, v, o, lse, do), sem=("parallel", "parallel", "arbitrary"), hosted=hosted)


CONV_CB = 256


def _shift_down(u, k, row):
    return jnp.where(row >= k, pltpu.roll(u, k, 0), 0.0)


def _shift_up(u, k, row, n):
    return jnp.where(row < n - k, pltpu.roll(u, n - k, 0), 0.0)


def _conv_fwd(conv3, cw, *, n_seq, seq, name, hosted=None):
    def body(c_ref, w_ref, p_ref):
        blk = c_ref[...].astype(F32)
        xc, gb, gc = blk[:, :CONV_CB], blk[:, CONV_CB:2 * CONV_CB], blk[:, 2 * CONV_CB:]
        row = lax.broadcasted_iota(jnp.int32, (seq, CONV_CB), 0)
        u = gc * xc
        z = w_ref[0:1, :] * _shift_down(u, 2, row) + w_ref[1:2, :] * _shift_down(u, 1, row) + w_ref[2:3, :] * u
        p_ref[...] = (gb * z).astype(BF16)

    (p,), got = _call(
        body, name=name, grid=(n_seq, D // CONV_CB),
        in_specs=[pl.BlockSpec((seq, 3 * CONV_CB), lambda b, j: (b, j)), pl.BlockSpec((3, CONV_CB), lambda b, j: (0, j))],
        out_specs=[pl.BlockSpec((seq, CONV_CB), lambda b, j: (b, j))],
        out_shape=[jax.ShapeDtypeStruct((n_seq * seq, D), BF16)], scratch_shapes=[],
        operands=(conv3, cw), sem=("parallel", "parallel"), hosted=hosted)
    return p, got


def _conv_bwd(dp, conv3, cw, *, n_seq, seq, name):
    def body(dp_ref, c_ref, w_ref, dc_ref, dw_ref):
        @pl.when(pl.program_id(1) == 0)
        def _():
            dw_ref[...] = jnp.zeros_like(dw_ref)

        blk = c_ref[...].astype(F32)
        xc, gb, gc = blk[:, :CONV_CB], blk[:, CONV_CB:2 * CONV_CB], blk[:, 2 * CONV_CB:]
        row = lax.broadcasted_iota(jnp.int32, (seq, CONV_CB), 0)
        w0, w1, w2 = w_ref[0:1, :], w_ref[1:2, :], w_ref[2:3, :]
        u = gc * xc
        u1 = _shift_down(u, 1, row)
        u2 = _shift_down(u, 2, row)
        z = w0 * u2 + w1 * u1 + w2 * u
        dpv = dp_ref[...].astype(F32)
        dz = dpv * gb
        du = w2 * dz + w1 * _shift_up(dz, 1, row, seq) + w0 * _shift_up(dz, 2, row, seq)
        dc_ref[...] = jnp.concatenate([du * gc, dpv * z, du * xc], axis=1).astype(BF16)
        dw_ref[0:1, :] += jnp.sum(dz * u2, axis=0, keepdims=True)
        dw_ref[1:2, :] += jnp.sum(dz * u1, axis=0, keepdims=True)
        dw_ref[2:3, :] += jnp.sum(dz * u, axis=0, keepdims=True)

    return pl.pallas_call(
        body, name=name, grid=(D // CONV_CB, n_seq),
        in_specs=[pl.BlockSpec((seq, CONV_CB), lambda j, b: (b, j)), pl.BlockSpec((seq, 3 * CONV_CB), lambda j, b: (b, j)),
                  pl.BlockSpec((3, CONV_CB), lambda j, b: (0, j))],
        out_specs=[pl.BlockSpec((seq, 3 * CONV_CB), lambda j, b: (b, j)), pl.BlockSpec((3, CONV_CB), lambda j, b: (0, j))],
        out_shape=[jax.ShapeDtypeStruct((n_seq * seq, CONV_COLS), BF16), jax.ShapeDtypeStruct((3, D), F32)],
        compiler_params=_params("parallel", "arbitrary"),
    )(dp, conv3, cw)


def _merge_fwd(o, p, gl, bias, x1, wpa, wpc, wout, *, tm, name, hosted=None):
    t = x1.shape[0]

    def body(o_ref, p_ref, gl_ref, b_ref, x_ref, wpa_ref, wpc_ref, wout_ref, x2_ref, mg_ref, ya_ref, yb_ref):
        ya = _dot_nn(o_ref[...], wpa_ref[...])
        yb = _dot_nn(p_ref[...], wpc_ref[...])
        gates = _sigmoid(gl_ref[...].astype(F32) + b_ref[...])
        merged = (gates[:, :D] * ya + gates[:, D:] * yb).astype(BF16)
        ya_ref[...] = ya.astype(BF16)
        yb_ref[...] = yb.astype(BF16)
        mg_ref[...] = merged
        x2_ref[...] = x_ref[...] + _dot_nn(merged, wout_ref[...])

    row = pl.BlockSpec((tm, D), lambda i: (i, 0))
    row2 = pl.BlockSpec((tm, GATE_COLS), lambda i: (i, 0))
    wsp = pl.BlockSpec((D, D), lambda i: (0, 0))
    wide = jax.ShapeDtypeStruct((t, D), BF16)
    return _call(
        body, name=name, grid=(t // tm,),
        in_specs=[row, row, row2, pl.BlockSpec((1, GATE_COLS), lambda i: (0, 0)), row, wsp, wsp, wsp],
        out_specs=[row, row, row, row], out_shape=[jax.ShapeDtypeStruct((t, D), F32), wide, wide, wide], scratch_shapes=[],
        operands=(o, p, gl, bias, x1, wpa, wpc, wout), sem=("parallel",), hosted=hosted)


def _merge_bwd(dx2, ya, yb, gl, bias, wpa, wpc, wout, *, tm, name, hosted=None):
    t = dx2.shape[0]

    def body(dx_ref, ya_ref, yb_ref, gl_ref, b_ref, wpa_ref, wpc_ref, wout_ref,
             dxb_ref, dya_ref, dyb_ref, dgl_ref, do_ref, dp_ref, db_ref):
        @pl.when(pl.program_id(0) == 0)
        def _():
            db_ref[...] = jnp.zeros_like(db_ref)

        dxb = dx_ref[...].astype(BF16)
        dxb_ref[...] = dxb
        dm = _dot_nt(dxb, wout_ref[...])
        gates = _sigmoid(gl_ref[...].astype(F32) + b_ref[...])
        ga, gb = gates[:, :D], gates[:, D:]
        dya = (dm * ga).astype(BF16)
        dyb = (dm * gb).astype(BF16)
        dya_ref[...] = dya
        dyb_ref[...] = dyb
        dgl = jnp.concatenate([dm * ya_ref[...].astype(F32) * ga * (1.0 - ga),
                               dm * yb_ref[...].astype(F32) * gb * (1.0 - gb)], axis=1)
        dgl_ref[...] = dgl.astype(BF16)
        db_ref[...] += jnp.sum(dgl, axis=0, keepdims=True)
        do_ref[...] = _dot_nt(dya, wpa_ref[...]).astype(BF16)
        dp_ref[...] = _dot_nt(dyb, wpc_ref[...]).astype(BF16)

    row = pl.BlockSpec((tm, D), lambda i: (i, 0))
    row2 = pl.BlockSpec((tm, GATE_COLS), lambda i: (i, 0))
    vec2 = pl.BlockSpec((1, GATE_COLS), lambda i: (0, 0))
    wsp = pl.BlockSpec((D, D), lambda i: (0, 0))
    wide = jax.ShapeDtypeStruct((t, D), BF16)
    return _call(
        body, name=name, grid=(t // tm,), in_specs=[row, row, row, row2, vec2, wsp, wsp, wsp],
        out_specs=[row, row, row, row2, row, row, vec2],
        out_shape=[wide, wide, wide, jax.ShapeDtypeStruct((t, GATE_COLS), BF16), wide, wide,
                   jax.ShapeDtypeStruct((1, GATE_COLS), F32)],
        scratch_shapes=[], operands=(dx2, ya, yb, gl, bias, wpa, wpc, wout), sem=("arbitrary",), hosted=hosted)


def _adamw_small(ws, gs, ms, vs, *, name):
    n = len(ws)
    c1 = 1.0 / (1.0 - ADAM_B1 ** ADAM_STEP)
    c2 = 1.0 / (1.0 - ADAM_B2 ** ADAM_STEP)

    def body(*refs):
        for i in range(n):
            w_ref, g_ref, m_ref, v_ref, d_ref, nm_ref, nv_ref = (refs[k * n + i] for k in range(7))
            gv = g_ref[...]
            nm = ADAM_B1 * m_ref[...] + (1.0 - ADAM_B1) * gv
            nv = ADAM_B2 * v_ref[...] + (1.0 - ADAM_B2) * (gv * gv)
            nm_ref[...] = nm
            nv_ref[...] = nv
            d_ref[...] = -ADAM_LR * ((nm * c1) / (jnp.sqrt(nv * c2) + ADAM_EPS) + ADAM_WD * w_ref[...])

    vm = pl.BlockSpec(memory_space=pltpu.VMEM)
    shapes = [jax.ShapeDtypeStruct(a.shape, F32) for a in ws]
    outs = pl.pallas_call(body, name=name, in_specs=[vm] * (4 * n), out_specs=[vm] * (3 * n), out_shape=shapes * 3)(*ws, *gs, *ms, *vs)
    return outs[:n], outs[n:2 * n], outs[2 * n:]


def _adamw(w, g, m, v, *, name):
    rows, cols = w.shape
    tr = max([c for c in range(8, 513, 8) if rows % c == 0], default=rows)
    c1 = 1.0 / (1.0 - ADAM_B1 ** ADAM_STEP)
    c2 = 1.0 / (1.0 - ADAM_B2 ** ADAM_STEP)

    def body(w_ref, g_ref, m_ref, v_ref, d_ref, nm_ref, nv_ref):
        gv = g_ref[...]
        nm = ADAM_B1 * m_ref[...] + (1.0 - ADAM_B1) * gv
        nv = ADAM_B2 * v_ref[...] + (1.0 - ADAM_B2) * (gv * gv)
        nm_ref[...] = nm
        nv_ref[...] = nv
        d_ref[...] = -ADAM_LR * ((nm * c1) / (jnp.sqrt(nv * c2) + ADAM_EPS) + ADAM_WD * w_ref[...])

    spec = pl.BlockSpec((tr, cols), lambda i: (i, 0))
    shp = jax.ShapeDtypeStruct((rows, cols), F32)
    return pl.pallas_call(
        body, name=name, grid=(rows // tr,), in_specs=[spec] * 4, out_specs=[spec] * 3, out_shape=[shp] * 3,
        compiler_params=_params("parallel"),
    )(w, g, m, v)


def _place():
    return lax.axis_index("x"), lax.axis_index("y"), lax.axis_index("c")


def _other_chips(x, y):
    return [(1 - x, y), (x, 1 - y), (1 - x, 1 - y)]


def _remote(src, dst, send, recv, dev):
    return pltpu.make_async_remote_copy(src_ref=src, dst_ref=dst, send_sem=send, recv_sem=recv, device_id=dev, device_id_type=MESH)


def _gather_chips_plan(n):
    def start(srcs, dsts, send, recv, local):
        x, y, cc = _place()
        me = 4 * x + 2 * y + cc
        for a in range(n):
            pltpu.make_async_copy(srcs[a], dsts[a].at[me], local.at[a]).start()
            for k, (px, py) in enumerate(_other_chips(x, y)):
                _remote(srcs[a], dsts[a].at[me], send.at[3 * a + k], recv.at[3 * a + k], (px, py, cc)).start()

    def wait(srcs, dsts, send, recv, local):
        x, y, cc = _place()
        me = 4 * x + 2 * y + cc
        for a in range(n):
            for k, (px, py) in enumerate(_other_chips(x, y)):
                _remote(srcs[a], dsts[a].at[4 * px + 2 * py + cc], send.at[3 * a + k], recv.at[3 * a + k], (px, py, cc)).wait_recv()
        for a in range(n):
            for k, (px, py) in enumerate(_other_chips(x, y)):
                _remote(srcs[a], dsts[a].at[me], send.at[3 * a + k], recv.at[3 * a + k], (px, py, cc)).wait_send()
            pltpu.make_async_copy(srcs[a], dsts[a].at[me], local.at[a]).wait()

    return _Plan(start, wait, 3 * n, n)


def _scatter_chips_plan(n):
    def start(srcs, dsts, send, recv, local):
        x, y, cc = _place()
        for a in range(n):
            for k, (px, py) in enumerate(_other_chips(x, y)):
                _remote(srcs[a].at[2 * px + py], dsts[a].at[k], send.at[3 * a + k], recv.at[3 * a + k], (px, py, cc)).start()

    def wait(srcs, dsts, send, recv, local):
        x, y, cc = _place()
        for a in range(n):
            for k, (px, py) in enumerate(_other_chips(x, y)):
                _remote(srcs[a].at[k], dsts[a].at[k], send.at[3 * a + k], recv.at[3 * a + k], (px, py, cc)).wait_recv()
        for a in range(n):
            for k, (px, py) in enumerate(_other_chips(x, y)):
                _remote(srcs[a].at[k], dsts[a].at[k], send.at[3 * a + k], recv.at[3 * a + k], (px, py, cc)).wait_send()

    return _Plan(start, wait, 3 * n, 0)


def _gather_shapes(blocks):
    return [jax.ShapeDtypeStruct((N_DEV,) + b.shape, b.dtype) for b in blocks]


def _scatter_shapes(parts):
    return [jax.ShapeDtypeStruct((3,) + p.shape[1:], p.dtype) for p in parts]


def _gather_sibling_plan(n):
    def start(srcs, dsts, send, recv, local):
        x, y, cc = _place()
        for a in range(n):
            for q in range(4):
                _remote(srcs[a].at[2 * q + cc], dsts[a].at[2 * q + cc], send.at[4 * a + q], recv.at[4 * a + q], (x, y, 1 - cc)).start()

    def wait(srcs, dsts, send, recv, local):
        x, y, cc = _place()
        for a in range(n):
            for q in range(4):
                _remote(srcs[a].at[2 * q + cc], dsts[a].at[2 * q + 1 - cc], send.at[4 * a + q], recv.at[4 * a + q],
                        (x, y, 1 - cc)).wait_recv()
        for a in range(n):
            for q in range(4):
                _remote(srcs[a].at[2 * q + cc], dsts[a].at[2 * q + cc], send.at[4 * a + q], recv.at[4 * a + q],
                        (x, y, 1 - cc)).wait_send()

    return _Plan(start, wait, 4 * n, 0, in_place=True)


def _scatter_sibling_plan(n):
    def start(srcs, dsts, send, recv, local):
        x, y, cc = _place()
        for a in range(n):
            for q in range(4):
                _remote(srcs[a].at[2 * q + 1 - cc], dsts[a].at[q], send.at[4 * a + q], recv.at[4 * a + q], (x, y, 1 - cc)).start()

    def wait(srcs, dsts, send, recv, local):
        x, y, cc = _place()
        for a in range(n):
            for q in range(4):
                _remote(srcs[a].at[q], dsts[a].at[q], send.at[4 * a + q], recv.at[4 * a + q], (x, y, 1 - cc)).wait_recv()
        for a in range(n):
            for q in range(4):
                _remote(srcs[a].at[q], dsts[a].at[q], send.at[4 * a + q], recv.at[4 * a + q], (x, y, 1 - cc)).wait_send()

    return _Plan(start, wait, 4 * n, 0)


def _same_shapes(arrs):
    return [jax.ShapeDtypeStruct(a.shape, a.dtype) for a in arrs]


def _halved_shapes(parts):
    return [jax.ShapeDtypeStruct((4,) + p.shape[1:], p.dtype) for p in parts]


def _run_plan(plan, srcs, out_shapes, *, name):
    n_in, n_out = len(srcs), len(out_shapes)

    def body(*refs):
        h_in, h_out, sems = refs[:n_in], refs[n_in:n_in + n_out], refs[n_in + n_out:]
        plan.start(h_in, h_out, *sems)
        plan.wait(h_in, h_out, *sems)

    return pl.pallas_call(body, name=name, in_specs=[ANY] * n_in, out_specs=[ANY] * n_out, out_shape=list(out_shapes),
                          input_output_aliases={a: a for a in range(n_in)} if plan.in_place else {},
                          scratch_shapes=plan.sems())(*srcs)


SEM = pl.BlockSpec(memory_space=pltpu.SEMAPHORE)
HBM = pl.BlockSpec(memory_space=pltpu.HBM)
SIDE_EFFECT = pltpu.CompilerParams(has_side_effects=pltpu.SideEffectType.DATAFLOW_SIDE_EFFECTING)


def _plan_start(plan, blocks, land_shapes, *, name):
    n = len(blocks)
    lands = [lax.empty(s.shape, s.dtype) for s in land_shapes]

    def body(*refs):
        srcs, sems, lands_out, token = refs[:n], refs[2 * n:2 * n + 3], refs[3 * n + 3:4 * n + 3], refs[4 * n + 3]
        plan.start(srcs, lands_out, *sems)
        token[...] = jnp.zeros_like(token)

    out_shape = ([s for s in plan.sems()] + [pltpu.HBM(b.shape, b.dtype) for b in blocks]
                 + [pltpu.HBM(l.shape, l.dtype) for l in lands] + [jax.ShapeDtypeStruct((8, 128), F32)])
    res = pl.pallas_call(
        body, name=name, in_specs=[HBM] * (2 * n), out_specs=[SEM] * 3 + [HBM] * (2 * n) + [pl.BlockSpec(memory_space=pltpu.VMEM)],
        out_shape=out_shape, input_output_aliases={a: 3 + a for a in range(2 * n)}, compiler_params=SIDE_EFFECT,
    )(*[pltpu.with_memory_space_constraint(a, pltpu.HBM) for a in list(blocks) + lands])
    return res[:3], res[3:3 + n], res[3 + n:3 + 2 * n], res[3 + 2 * n]


def _plan_wait(plan, sems, blocks, lands, after, *, name):
    n = len(blocks)

    def body(*refs):
        plan.wait(refs[:n], refs[n:2 * n], *refs[2 * n:2 * n + 3])

    res = pl.pallas_call(
        body, name=name, in_specs=[HBM] * (2 * n) + [SEM] * 3 + [ANY], out_specs=[HBM] * (2 * n),
        out_shape=[pltpu.HBM(a.shape, a.dtype) for a in list(blocks) + list(lands)],
        input_output_aliases={a: a for a in range(2 * n)}, compiler_params=SIDE_EFFECT,
    )(*blocks, *lands, *sems, after)
    return list(res[:n]), list(res[n:])


def _sum_sibling(ps, qs, core, *, name):
    n = len(ps)

    def body(core_ref, *refs):
        for p_ref, q_ref, o_ref in zip(refs[:n], refs[n:2 * n], refs[2 * n:]):
            o_ref[...] = (p_ref[...].astype(F32) + q_ref[...].astype(F32)).astype(BF16)

    def mine(p):
        return pl.BlockSpec((1,) + p.shape[1:], lambda ch, core_ref: (2 * ch + core_ref[0], 0, 0))

    def theirs(p):
        return pl.BlockSpec((1,) + p.shape[1:], lambda ch, core_ref: (ch, 0, 0))

    grid_spec = pltpu.PrefetchScalarGridSpec(
        num_scalar_prefetch=1, grid=(4,), in_specs=[mine(p) for p in ps] + [theirs(p) for p in ps], out_specs=[theirs(p) for p in ps])
    return pl.pallas_call(
        body, name=name, grid_spec=grid_spec, out_shape=[jax.ShapeDtypeStruct((4,) + p.shape[1:], BF16) for p in ps],
        compiler_params=_params("parallel"),
    )(core, *ps, *qs)


def _sum_chips(s1, r2, chip, *, name):
    _, r, c = s1.shape

    def body(chip_ref, s_ref, r_ref, o_ref):
        acc = s_ref[0].astype(F32)
        for k in range(3):
            acc = acc + r_ref[k].astype(F32)
        o_ref[...] = acc

    grid_spec = pltpu.PrefetchScalarGridSpec(
        num_scalar_prefetch=1, grid=(1,),
        in_specs=[pl.BlockSpec((1, r, c), lambda i, chip_ref: (chip_ref[0], 0, 0)),
                  pl.BlockSpec((3, r, c), lambda i, chip_ref: (0, 0, 0))],
        out_specs=pl.BlockSpec((r, c), lambda i, chip_ref: (0, 0)))
    return pl.pallas_call(
        body, name=name, grid_spec=grid_spec, out_shape=jax.ShapeDtypeStruct((r, c), F32),
        compiler_params=_params("arbitrary"),
    )(chip, s1, r2)


def _sum_adamw(s1s, r2s, chip, ws, ms, vs, *, name):
    n = len(s1s)
    _, r, c = s1s[0].shape
    tr = r // 2
    c1 = 1.0 / (1.0 - ADAM_B1 ** ADAM_STEP)
    c2 = 1.0 / (1.0 - ADAM_B2 ** ADAM_STEP)

    def body(chip_ref, *refs):
        for a in range(n):
            s_ref, r_ref, w_ref, m_ref, v_ref = (refs[k * n + a] for k in range(5))
            g_ref, d_ref, nm_ref, nv_ref = refs[5 * n + 4 * a:5 * n + 4 * a + 4]
            gv = s_ref[0].astype(F32)
            for k in range(3):
                gv = gv + r_ref[k].astype(F32)
            g_ref[...] = gv
            nm = ADAM_B1 * m_ref[...] + (1.0 - ADAM_B1) * gv
            nv = ADAM_B2 * v_ref[...] + (1.0 - ADAM_B2) * (gv * gv)
            nm_ref[...] = nm
            nv_ref[...] = nv
            d_ref[...] = -ADAM_LR * ((nm * c1) / (jnp.sqrt(nv * c2) + ADAM_EPS) + ADAM_WD * w_ref[...])

    flat = pl.BlockSpec((tr, c), lambda i, chip_ref: (i, 0))
    own = pl.BlockSpec((1, tr, c), lambda i, chip_ref: (chip_ref[0], i, 0))
    got = pl.BlockSpec((3, tr, c), lambda i, chip_ref: (0, i, 0))
    grid_spec = pltpu.PrefetchScalarGridSpec(
        num_scalar_prefetch=1, grid=(2,), in_specs=[own] * n + [got] * n + [flat] * (3 * n), out_specs=[flat] * (4 * n))
    res = pl.pallas_call(
        body, name=name, grid_spec=grid_spec, out_shape=[jax.ShapeDtypeStruct((r, c), F32)] * (4 * n),
        compiler_params=_params("parallel"),
    )(chip, *s1s, *r2s, *ws, *ms, *vs)
    return [res[4 * a:4 * a + 4] for a in range(n)]


def _small_exchange(v, *, reduce, name):
    r, c = v.shape

    def body(x_ref, o_ref, *rest):
        if reduce:
            buf_ref, send_sems, recv_sems = rest
        else:
            buf_ref = o_ref
            send_sems, recv_sems = rest
        x, y, cc = _place()
        me = 4 * x + 2 * y + cc

        def peer(k):
            return ((1 - x) if k & 4 else x, (1 - y) if k & 2 else y, (1 - cc) if k & 1 else cc)

        buf_ref[me] = x_ref[...]
        sends = []
        for k in range(1, N_DEV):
            cp = pltpu.make_async_remote_copy(src_ref=x_ref, dst_ref=buf_ref.at[me], send_sem=send_sems.at[k - 1],
                                              recv_sem=recv_sems.at[k - 1], device_id=peer(k), device_id_type=MESH)
            cp.start()
            sends.append(cp)
        for k in range(1, N_DEV):
            px, py, pc = peer(k)
            pltpu.make_async_remote_copy(src_ref=x_ref, dst_ref=buf_ref.at[4 * px + 2 * py + pc], send_sem=send_sems.at[k - 1],
                                         recv_sem=recv_sems.at[k - 1], device_id=peer(k), device_id_type=MESH).wait_recv()
        for cp in sends:
            cp.wait_send()
        if reduce:
            acc = buf_ref[0]
            for s in range(1, N_DEV):
                acc = acc + buf_ref[s]
            o_ref[...] = acc

    vm = pl.BlockSpec(memory_space=pltpu.VMEM)
    sems = [pltpu.SemaphoreType.DMA((N_DEV - 1,)), pltpu.SemaphoreType.DMA((N_DEV - 1,))]
    if reduce:
        out_shape, scratch = jax.ShapeDtypeStruct((r, c), F32), [pltpu.VMEM((N_DEV, r, c), F32)] + sems
    else:
        out_shape, scratch = jax.ShapeDtypeStruct((N_DEV, r, c), F32), sems
    return pl.pallas_call(body, name=name, in_specs=[vm], out_specs=vm, out_shape=out_shape, scratch_shapes=scratch)(v)


def _rows(a):
    return a.reshape(-1, D)


def _pad_cols(a, to):
    return jnp.pad(a, ((0, 0), (0, to - a.shape[1])))


def _pack_weights(w):
    parts = {
        "w_inT": jnp.pad(w["w_in"].T, ((0, IN_SHARD_PAD - IN_SHARD), (0, 0))),
        "w_uq": _rows(_head_cols(w["w_uq"])), "w_uk": _rows(_head_cols(w["w_uk"])),
        "w_uv": _rows(_pad_cols(w["w_uv"], HEAD_PAD)), "w_pa": _rows(w["w_proj_attn"]),
        "w_pc": w["w_proj_conv"], "w_out": w["w_out"],
    }
    return [jnp.concatenate([parts[n].astype(BF16) for n, _ in group], axis=0) for group in PACK]


def _cols_from_shards(gs, name, rows):
    idx, off, r = PACK_OFF[name]
    return gs[idx][:, off:off + r].reshape(N_DEV, rows, HEAD_PAD).transpose(1, 0, 2).reshape(rows, N_DEV * HEAD_PAD)


def _rows_from_shards(gs, name, keep=None):
    idx, off, r = PACK_OFF[name]
    keep = r if keep is None else keep
    return gs[idx][:, off:off + keep].reshape(N_DEV * keep, D)


def _rope_placement():
    i = lax.broadcasted_iota(jnp.int32, (HEAD_PAD, D), 0)
    j = lax.broadcasted_iota(jnp.int32, (HEAD_PAD, D), 1)
    lane = jnp.where(i < ROPE_HALF, 32 + i, 96 + i - ROPE_HALF)
    return ((i < 2 * ROPE_HALF) & (j % HEAD_PAD == lane)).astype(BF16)


def _unpack_in(g_in):
    w_inT = _rows_from_shards([g_in, None], "w_inT", IN_SHARD)
    lat_rows = Q_LORA + KV_LORA + 2 * ROPE_HALF
    conv = w_inT[lat_rows:lat_rows + CONV_COLS].reshape(3, D // CONV_CB, CONV_CB, D).transpose(1, 0, 2, 3).reshape(CONV_COLS, D)
    return {"latT": jnp.pad(w_inT[:lat_rows], ((0, LAT_PAD - lat_rows), (0, 0))), "convT": conv,
            "gateT": w_inT[lat_rows + CONV_COLS:]}


def _unpack_misc(g_misc):
    g = [None, g_misc]
    wpa = _cols_from_shards(g, "w_pa", 512).reshape(N_HEADS, NOPE, D)
    return {
        "wq": _cols_from_shards(g, "w_uq", Q_LORA),
        "wk": jnp.concatenate([_cols_from_shards(g, "w_uk", KV_LORA), _rope_placement()], axis=0),
        "wv": _cols_from_shards(g, "w_uv", KV_LORA),
        "wpa": jnp.pad(wpa, ((0, 0), (0, HEAD_PAD - NOPE), (0, 0))).reshape(D, D),
        "wpc": _rows_from_shards(g, "w_pc"), "wout": _rows_from_shards(g, "w_out"),
    }


def _shards_from_cols(a):
    rows = a.shape[0]
    return a.reshape(rows, N_DEV, HEAD_PAD).transpose(1, 0, 2).reshape(N_DEV, rows * HEAD_PAD // D, D)


def _pack_grads(gw):
    lat_rows = Q_LORA + KV_LORA + 2 * ROPE_HALF
    conv = gw["convT"].reshape(D // CONV_CB, 3, CONV_CB, D).transpose(1, 0, 2, 3).reshape(CONV_COLS, D)
    w_inT = jnp.concatenate([gw["latT"][:lat_rows], conv, gw["gateT"]], axis=0).reshape(N_DEV, IN_SHARD, D)
    wpa = gw["wpa"].reshape(N_HEADS, HEAD_PAD, D)[:, :NOPE].reshape(N_HEADS * NOPE, D)
    parts = {}
    parts.update({
        "w_inT": jnp.pad(w_inT, ((0, 0), (0, IN_SHARD_PAD - IN_SHARD), (0, 0))),
        "w_uq": _shards_from_cols(gw["wq"]), "w_uk": _shards_from_cols(gw["wk"][:KV_LORA]),
        "w_uv": _shards_from_cols(gw["wv"][:KV_LORA]), "w_pa": _shards_from_cols(wpa),
        "w_pc": gw["wpc"].reshape(N_DEV, D // N_DEV, D), "w_out": gw["wout"].reshape(N_DEV, D // N_DEV, D),
    })
    return [jnp.concatenate([parts[n] for n, _ in group], axis=1) for group in PACK]


def _unpack_grads(mines):
    def seg(name, keep=None):
        idx, off, r = PACK_OFF[name]
        return mines[idx][off:off + (r if keep is None else keep)]

    return {
        "w_in": seg("w_inT", IN_SHARD).T,
        "w_uq": _head_cols_inv(seg("w_uq").reshape(Q_LORA, HEAD_PAD), QK_DIM),
        "w_uk": _head_cols_inv(seg("w_uk").reshape(KV_LORA, HEAD_PAD), NOPE),
        "w_uv": seg("w_uv").reshape(KV_LORA, HEAD_PAD)[:, :NOPE],
        "w_proj_attn": seg("w_pa").reshape(512, HEAD_PAD),
        "w_proj_conv": seg("w_pc"), "w_out": seg("w_out"),
    }


def _rope_tables(positions):
    lane = jnp.arange(HEAD_PAD)
    idx = jnp.where((lane >= 32) & (lane < 48), lane - 32, jnp.where((lane >= 96) & (lane < 112), lane - 96, -1))
    inv_freq = jnp.where(idx >= 0, 1.0 / (ROPE_THETA ** (idx.astype(F32) / ROPE_HALF)), 0.0)
    ang = positions.reshape(-1).astype(F32)[:, None] * inv_freq
    return jnp.cos(ang), jnp.sin(ang) * jnp.where(lane < HEAD_PAD // 2, -1.0, 1.0)


def _local_step(x, positions, target, small, ex):
    n_seq, seq, d = x.shape
    t = n_seq * seq
    x0 = x.reshape(t, d)
    tgt = target.reshape(t, d)
    rc, rs = _rope_tables(positions)
    ghq = _head_cols(small["q_head_norm"])
    ghk = _head_cols(small["k_head_norm"])
    TM, HC, TQ = 1024, 256, 1024

    def mm(*args, hosted=None, **kw):
        res = _mm(*args, hosted=hosted, **kw)
        return res if hosted is not None else (res, None)

    def wgrad(a, b, name, tm=None, hosted=None):
        tm = tm or a.shape[1]
        return mm(a, b, mode="tn", out_dtype=BF16, tm=tm, tn=b.shape[1], tk=2048 if tm <= D else 1024, name=name, hosted=hosted)

    f1g, f1u, f1d = ex.gather_finish(ex.witness() + rc[:8])
    conv_w, landed = ex.gather_conv_w(f1d)
    (x1, h1, a1, b1), got = _ffn_fwd(x0, small["ffn1_norm"] + landed, f1g, f1u, f1d, tm=512, hc=DFF // 2, name="ffn1_fwd",
                                     hosted=ex.gather_chips("mix_in"))
    hm, got = _rms_fwd(x1, small["mix_norm"], tm=TM, name="mix_norm_fwd", hosted=ex.gather_sibling(got))
    W = ex.mix_in_weights(got)
    (lat, conv3, gl), got = _proj_fwd(hm, W["latT"], W["convT"], W["gateT"], tm=512, name="proj_fwd",
                                      hosted=ex.gather_chips("mix_misc"))
    p, got = _conv_fwd(conv3, conv_w, n_seq=n_seq, seq=seq, name="conv_fwd", hosted=ex.gather_sibling(got))
    W.update(ex.mix_misc_weights(got))
    q, k, v, qn, ckv = _mla_prep_fwd(lat, small["q_a_norm"], small["kv_a_norm"], ghq, ghk, W["wq"], W["wk"], W["wv"], rc, rs,
                                     tm=512, name="mla_prep_fwd")
    (o, lse), got = _flash_fwd(q, k, v, n_seq=n_seq, seq=seq, tq=TQ, name="attn_fwd", hosted=ex.gather_chips("ffn2"))
    (x2, merged, ya, yb), got = _merge_fwd(o, p, gl, small["gate_bias"], x1, W["wpa"], W["wpc"], W["wout"], tm=512, name="merge_fwd",
                                           hosted=ex.gather_sibling(got))
    f2g, f2u, f2d = ex.ffn_weights(got)
    (dy, h2, a2, b2, loss_row), _ = _ffn_fwd(x2, small["ffn2_norm"], f2g, f2u, f2d, tm=512, hc=DFF // 2, name="ffn2_fwd", target=tgt)

    gw, gs = {}, {}
    (da2, db2, *ffn2_grads), _ = _ffn_grads(dy, h2, a2, b2, f2d, tm=TM, hc=HC, name="ffn2_grads")
    (dx2, gs["ffn2_norm"]), _ = _ffn_up_bwd(da2, db2, f2g, f2u, x2, small["ffn2_norm"], dy, tm=512, name="ffn2_up_bwd")

    (dx2b, dya, dyb, dgl, do, dp, gs["gate_bias"]), got = _merge_bwd(
        dx2, ya, yb, gl, small["gate_bias"], W["wpa"], W["wpc"], W["wout"], tm=512, name="merge_bwd",
        hosted=ex.scatter_sibling("ffn2", ffn2_grads))
    ex.scatter_sibling_done("ffn2", got)
    gw["wout"] = wgrad(merged, dx2b, "dw_out")[0]
    gw["wpa"] = wgrad(o, dya, "dw_pa")[0]
    gw["wpc"] = wgrad(p, dyb, "dw_pc")[0]
    dconv3, dconv_w = _conv_bwd(dp, conv3, conv_w, n_seq=n_seq, seq=seq, name="conv_bwd")
    (dq, dk, dv), got = _flash_bwd(q, k, v, o, lse, do, n_seq=n_seq, seq=seq, tq=TQ, name="attn_bwd",
                                   hosted=ex.scatter_chips("ffn2"))
    ex.scatter_chips_done("ffn2", got)
    dlat, dqp, dkp, gs["q_a_norm"], gs["kv_a_norm"], dghq, dghk = _mla_prep_bwd(
        dq, dk, dv, lat, qn, ckv, small["q_a_norm"], small["kv_a_norm"], ghq, ghk, W["wq"], W["wk"], W["wv"], rc, rs,
        tm=512, name="mla_prep_bwd")
    gs["q_head_norm"], gs["k_head_norm"] = _head_cols_inv(dghq, QK_DIM), _head_cols_inv(dghk, QK_DIM)
    gw["wq"] = wgrad(qn, dqp, "dw_uq")[0]
    gw["wk"] = wgrad(ckv, dkp, "dw_uk")[0]
    gw["wv"] = wgrad(ckv, dv, "dw_uv")[0]
    gw["convT"] = wgrad(dconv3, hm, "dw_conv", tm=CONV_COLS // 2)[0]
    gw["gateT"] = wgrad(dgl, hm, "dw_gate")[0]
    gw["latT"] = wgrad(dlat, hm, "dw_lat")[0]
    ex.scatter_sibling_now("mix", gw)
    zero = ex.scatter_chips_start("mix_in")
    (dx1, gs["mix_norm"]), _ = _proj_bwd(dlat, dconv3, dgl, W["latT"], W["convT"], W["gateT"], x1, small["mix_norm"] + zero, dx2,
                                         tm=512, name="proj_bwd")

    (da1, db1, *ffn1_grads), got = _ffn_grads(dx1, h1, a1, b1, f1d, tm=TM, hc=HC, name="ffn1_grads",
                                              hosted=ex.scatter_chips("mix_misc"))
    ex.scatter_chips_done("mix_misc", got)
    ex.reduce_small(gs, dconv_w, loss_row)
    ex.scatter_sibling_now("ffn1", ffn1_grads)
    zero = ex.scatter_chips_start("ffn1")
    (dx0, gs["ffn1_norm"]), _ = _ffn_up_bwd(da1, db1, f1g, f1u, x0, small["ffn1_norm"] + zero, dx1, tm=512, name="ffn1_up_bwd")
    return dx0.reshape(n_seq, seq, d), gs["ffn1_norm"]


class _MeshExchange:
    def __init__(self, w, core, chip):
        self.w, self.core, self.chip = w, core, chip
        self.partial, self.received, self._cache, self._scattering = {}, {}, {}, {}

    def _blocks(self, group):
        w = self.w
        if group not in self._cache:
            if group.startswith("ffn"):
                self._cache[group] = [w[group + "_w_gate"].T.astype(BF16), w[group + "_w_up"].T.astype(BF16),
                                      w[group + "_w_down"].astype(BF16)]
            else:
                self._cache["mix_in"], self._cache["mix_misc"] = [[b] for b in _pack_weights(w)]
        return self._cache[group]

    def gather_chips(self, *groups):
        blocks = [b for group in groups for b in self._blocks(group)]
        return _gather_chips_plan(len(blocks)), blocks, _gather_shapes(blocks)

    def gather_sibling(self, got):
        half = list(got)
        return _gather_sibling_plan(len(half)), half, _same_shapes(half)

    def gather_start(self, group):
        blocks = self._blocks(group)
        plan = _gather_chips_plan(len(blocks))
        sems, blocks, lands, token = _plan_start(plan, blocks, _gather_shapes(blocks), name="gather_%s_start" % group)
        self._gathering = (group, plan, sems, blocks, lands)
        return token[0, 0]

    def gather_finish(self, after):
        group, plan, sems, blocks, lands = self._gathering
        _, half = _plan_wait(plan, sems, blocks, lands, after, name="gather_%s_wait" % group)
        return self.ffn_weights(_run_plan(_gather_sibling_plan(len(half)), half, _same_shapes(half), name="gather_%s_sibling" % group))

    def gather_conv_w(self, after):
        shard = self.w["conv_w"] + 0.0 * after[:1, :1].astype(F32)
        cw_all = _small_exchange(jnp.pad(shard, ((0, 5), (0, 0))), reduce=False, name="gather_conv_w")
        return cw_all[:, :3].transpose(1, 0, 2).reshape(3, D), cw_all[0, 7, 0]

    def reduce_small(self, gs, dconv_w, loss_row):
        pieces = [_pad_cols(gs[n], SMALL_SLOTS[n]) for n in SMALL_NAMES[1:]] + [dconv_w.reshape(1, 3 * D), loss_row]
        self.small_total = _small_exchange(jnp.concatenate(pieces, axis=1).reshape(-1, 128), reduce=True,
                                           name="reduce_small").reshape(-1)

    def scatter_chips_start(self, group):
        s1 = self.partial[group]
        plan = _scatter_chips_plan(len(s1))
        sems, s1, lands, token = _plan_start(plan, s1, _scatter_shapes(s1), name="scatter_%s_start" % group)
        self._scattering[group] = (plan, sems, s1, lands)
        return token[0, 0]

    def scatter_chips_finish(self, group, after):
        plan, sems, s1, lands = self._scattering[group]
        self.partial[group], self.received[group] = _plan_wait(plan, sems, s1, lands, after, name="scatter_%s_wait" % group)

    def witness(self):
        parts = [b[:8, :128].astype(F32) for g in ("mix_in", "mix_misc", "ffn2") for b in self._blocks(g)]
        return functools.reduce(jnp.add, parts)

    def ffn_weights(self, got):
        return [a.reshape(DFF, D) for a in got]

    def mix_in_weights(self, got):
        return _unpack_in(got[0])

    def mix_misc_weights(self, got):
        return _unpack_misc(got[0])

    def _parts(self, group, grads):
        if group == "mix":
            return _pack_grads(grads), ["mix_in", "mix_misc"]
        parts = [g.reshape(N_DEV, -1, D) for g in grads]
        return parts, ([group] if len(parts) == 1 else None)

    def scatter_sibling(self, group, grads):
        self._sent, self._names = self._parts(group, grads)
        return _scatter_sibling_plan(len(self._sent)), self._sent, _halved_shapes(self._sent)

    def scatter_sibling_done(self, group, got):
        sums = list(_sum_sibling(self._sent, list(got), self.core, name="sum_%s_sibling" % group))
        if self._names is None:
            self.partial[group] = sums
        else:
            for n, s in zip(self._names, sums):
                self.partial[n] = [s]

    def scatter_sibling_now(self, group, grads):
        plan, parts, shapes = self.scatter_sibling(group, grads)
        self.scatter_sibling_done(group, _run_plan(plan, parts, shapes, name="scatter_%s_sibling" % group))

    def scatter_chips(self, group):
        s1 = self.partial[group]
        return _scatter_chips_plan(len(s1)), s1, _scatter_shapes(s1)

    def scatter_chips_done(self, group, got):
        self.received[group] = list(got)


SMALL_NAMES = ("ffn1_norm", "mix_norm", "gate_bias", "q_a_norm", "kv_a_norm", "q_head_norm", "k_head_norm", "ffn2_norm")
SMALL_SLOTS = {"ffn1_norm": 1024, "mix_norm": 1024, "gate_bias": 2048, "q_a_norm": 384, "kv_a_norm": 256, "q_head_norm": 128,
               "k_head_norm": 128, "ffn2_norm": 1024, "conv_w": 3072, "loss": 128}
COLUMN_MAJOR = ("w_in", "w_uq", "w_uk", "w_uv")
WEIGHT_NAMES = ("ffn1_norm", "ffn1_w_gate", "ffn1_w_up", "ffn1_w_down", "mix_norm", "w_in", "gate_bias", "q_a_norm", "w_uq",
                "kv_a_norm", "w_uk", "w_uv", "q_head_norm", "k_head_norm", "w_proj_attn", "conv_w", "w_proj_conv", "w_out",
                "ffn2_norm", "ffn2_w_gate", "ffn2_w_up", "ffn2_w_down")


def _step(x, positions, loss_target, w, m, v):
    xi, yi, ci = _place()
    core = ci.astype(jnp.int32).reshape(1)
    chip = (2 * xi + yi).astype(jnp.int32).reshape(1)
    me = 4 * xi + 2 * yi + ci

    ex = _MeshExchange(w, core, chip)
    zero = ex.gather_start("ffn1")
    ex.w = {n: (a if n.startswith("ffn1") else a + zero) for n, a in w.items()}
    small = {n: w[n].reshape(1, -1) for n in SMALL_NAMES}

    grad_x, dffn1_norm = _local_step(x, positions + zero.astype(jnp.int32), loss_target, small, ex)

    grads, deltas, new_m, new_v = {}, {}, {}, {}

    def ffn_update(group):
        names = (group + "_w_gate", group + "_w_up", group + "_w_down")
        views = [[a[n] if n.endswith("down") else a[n].T for n in names] for a in (w, m, v)]
        res = _sum_adamw(ex.partial[group], ex.received[group], chip, *views, name="adamw_" + group)
        for n, four in zip(names, res):
            grads[n], deltas[n], new_m[n], new_v[n] = (r if n.endswith("down") else r.T for r in four)

    def update(n):
        shape = w[n].shape
        if n in COLUMN_MAJOR:
            ops = [a.T for a in (w[n], grads[n], m[n], v[n])]
            deltas[n], new_m[n], new_v[n] = (r.T for r in _adamw(*ops, name="adamw_" + n))
            return
        view = shape if len(shape) == 2 else ((-1, 128) if shape[0] % 128 == 0 else (1, shape[0]))
        dlt, nm, nv = _adamw(w[n].reshape(view), grads[n].reshape(view), m[n].reshape(view), v[n].reshape(view), name="adamw_" + n)
        deltas[n], new_m[n], new_v[n] = dlt.reshape(shape), nm.reshape(shape), nv.reshape(shape)

    ffn_update("ffn2")
    ex.scatter_chips_finish("mix_in", dffn1_norm)
    grads.update(_unpack_grads([_sum_chips(ex.partial[g][0], ex.received[g][0], chip, name="sum_%s_chips" % g)
                                for g in ("mix_in", "mix_misc")]))
    total, off = ex.small_total, 0
    for n in SMALL_NAMES[1:]:
        grads[n] = total[off:off + w[n].shape[0]]
        off += SMALL_SLOTS[n]
    conv_full = total[off:off + 3 * D].reshape(3, D)
    grads["conv_w"] = lax.dynamic_slice(conv_full, (0, me * HEAD_PAD), (3, HEAD_PAD))
    loss = total[off + 3 * D]
    tiny = SMALL_NAMES[1:] + ("conv_w", "w_uq", "w_uk", "w_uv", "w_proj_attn", "w_proj_conv", "w_out")

    def view(n, a):
        if a.ndim == 2:
            return a.T if n in COLUMN_MAJOR else a
        return a.reshape((-1, 128) if a.size % 128 == 0 else (1, a.size))

    def unview(n, a):
        return (a.T if n in COLUMN_MAJOR else a) if w[n].ndim == 2 else a.reshape(w[n].shape)

    res = _adamw_small(*[[view(n, a[n]) for n in tiny] for a in (w, grads, m, v)], name="adamw_small")
    for out, arrs in zip((deltas, new_m, new_v), res):
        out.update({n: unview(n, a) for n, a in zip(tiny, arrs)})
    later = ("ffn1_norm", "ffn1_w_gate", "ffn1_w_up", "ffn1_w_down")
    for n in WEIGHT_NAMES:
        if n not in deltas and n not in later:
            update(n)

    done = [deltas[n][:8, :128] for n in ("ffn2_w_down", "w_in", "w_out", "w_proj_attn")] + [deltas["mix_norm"].reshape(8, 128)]
    ex.scatter_chips_finish("ffn1", functools.reduce(jnp.add, done) + grad_x.reshape(-1, D)[:8, :128])
    ffn_update("ffn1")
    last = dffn1_norm + 0.0 * grads["ffn1_w_down"][:1, :1]
    grads["ffn1_norm"] = _small_exchange(last.reshape(-1, 128), reduce=True, name="reduce_ffn1_norm").reshape(-1)
    update("ffn1_norm")
    return (loss, grad_x, *[grads[n] for n in WEIGHT_NAMES], *[deltas[n] for n in WEIGHT_NAMES],
            *[new_m[n] for n in WEIGHT_NAMES], *[new_v[n] for n in WEIGHT_NAMES])


def kernel(x, positions, ffn1_norm, ffn1_w_gate, ffn1_w_up, ffn1_w_down, mix_norm, w_in, gate_bias, q_a_norm, w_uq, kv_a_norm, w_uk, w_uv, q_head_norm, k_head_norm, w_proj_attn, conv_w, w_proj_conv, w_out, ffn2_norm, ffn2_w_gate, ffn2_w_up, ffn2_w_down, loss_target, m_ffn1_norm, m_ffn1_w_gate, m_ffn1_w_up, m_ffn1_w_down, m_mix_norm, m_w_in, m_gate_bias, m_q_a_norm, m_w_uq, m_kv_a_norm, m_w_uk, m_w_uv, m_q_head_norm, m_k_head_norm, m_w_proj_attn, m_conv_w, m_w_proj_conv, m_w_out, m_ffn2_norm, m_ffn2_w_gate, m_ffn2_w_up, m_ffn2_w_down, v_ffn1_norm, v_ffn1_w_gate, v_ffn1_w_up, v_ffn1_w_down, v_mix_norm, v_w_in, v_gate_bias, v_q_a_norm, v_w_uq, v_kv_a_norm, v_w_uk, v_w_uv, v_q_head_norm, v_k_head_norm, v_w_proj_attn, v_conv_w, v_w_proj_conv, v_w_out, v_ffn2_norm, v_ffn2_w_gate, v_ffn2_w_up, v_ffn2_w_down):
    given = dict(locals())
    w = {n: given[n] for n in WEIGHT_NAMES}
    m = {n: given["m_" + n] for n in WEIGHT_NAMES}
    v = {n: given["v_" + n] for n in WEIGHT_NAMES}
    return _step(x, positions, loss_target, w, m, v)
```

```python
import functools

import jax
import jax.numpy as jnp
from jax import lax
from jax.experimental import pallas as pl
from jax.experimental.pallas import tpu as pltpu

F32 = jnp.float32
BF16 = jnp.bfloat16
MESH = pl.DeviceIdType.MESH
ANY = pl.BlockSpec(memory_space=pl.ANY)

N_DEV = 8
D = 1024
DFF = 2816
N_HEADS = 8
HEAD_PAD = 128
QK_DIM = 96
NOPE = 64
ROPE_HALF = 16
Q_LORA = 384
KV_LORA = 256
LAT_PAD = 768
CONV_COLS = 3072
GATE_COLS = 2048
IN_DIM = 5792
IN_SHARD = IN_DIM // N_DEV
IN_SHARD_PAD = 736
FF_SHARD = DFF // N_DEV
ROPE_THETA = 10000.0
NORM_EPS = 1e-6
ATTN_SCALE = QK_DIM ** -0.5
NEG = -1e30

ADAM_LR, ADAM_B1, ADAM_B2, ADAM_EPS, ADAM_WD, ADAM_STEP = 0.001, 0.9, 0.999, 1e-08, 0.01, 10

PACK = ((("w_inT", IN_SHARD_PAD),), (("w_uq", 48), ("w_uk", 32), ("w_uv", 32), ("w_pa", 64), ("w_pc", 128), ("w_out", 128)))
PACK_OFF = {}
for _i, _group in enumerate(PACK):
    _o = 0
    for _n, _r in _group:
        PACK_OFF[_n] = (_i, _o, _r)
        _o += _r

VMEM_LIMIT = 56 * 1024 * 1024


def _params(*sem):
    return pltpu.CompilerParams(dimension_semantics=sem if sem else None, vmem_limit_bytes=VMEM_LIMIT)


class _Plan:
    def __init__(self, start, wait, n_remote, n_local, in_place=False):
        self.start, self.wait, self.n_remote, self.n_local, self.in_place = start, wait, n_remote, n_local, in_place

    def sems(self):
        return [pltpu.SemaphoreType.DMA((self.n_remote,)), pltpu.SemaphoreType.DMA((self.n_remote,)),
                pltpu.SemaphoreType.DMA((max(self.n_local, 1),))]


def _call(body, *, name, grid, in_specs, out_specs, out_shape, scratch_shapes, operands, sem, hosted=None):
    if hosted is None:
        outs = pl.pallas_call(body, name=name, grid=grid, in_specs=in_specs, out_specs=out_specs, out_shape=out_shape,
                              scratch_shapes=scratch_shapes, compiler_params=_params(*sem))(*operands)
        return outs, None
    plan, srcs, h_shapes = hosted
    n_in, n_out, n_scr, nh_in, nh_out = len(in_specs), len(out_specs), len(scratch_shapes), len(srcs), len(h_shapes)
    aliases = {n_in + a: n_out + a for a in range(nh_in)} if plan.in_place else {}

    def full_body(*refs):
        ins, refs = refs[:n_in], refs[n_in:]
        h_in, refs = refs[:nh_in], refs[nh_in:]
        outs, refs = refs[:n_out], refs[n_out:]
        h_out, refs = refs[:nh_out], refs[nh_out:]
        scr, sems = refs[:n_scr], refs[n_scr:]
        ids = [pl.program_id(ax) for ax in range(len(grid))]
        first = functools.reduce(jnp.logical_and, [i == 0 for i in ids])
        last = functools.reduce(jnp.logical_and, [i == g - 1 for i, g in zip(ids, grid)])

        @pl.when(first)
        def _():
            plan.start(h_in, h_out, *sems)

        body(*ins, *outs, *scr)

        @pl.when(last)
        def _():
            plan.wait(h_in, h_out, *sems)

    res = pl.pallas_call(
        full_body, name=name, grid=grid, in_specs=list(in_specs) + [ANY] * nh_in, out_specs=list(out_specs) + [ANY] * nh_out,
        out_shape=list(out_shape) + list(h_shapes), scratch_shapes=list(scratch_shapes) + plan.sems(),
        input_output_aliases=aliases, compiler_params=_params(*(["arbitrary"] * len(grid))),
    )(*operands, *srcs)
    return res[:n_out], res[n_out:]


def _dot_nn(a, b):
    return lax.dot_general(a, b, (((1,), (0,)), ((), ())), preferred_element_type=F32)


def _dot_nt(a, b):
    return lax.dot_general(a, b, (((1,), (1,)), ((), ())), preferred_element_type=F32)


def _dot_tn(a, b):
    return lax.dot_general(a, b, (((0,), (0,)), ((), ())), preferred_element_type=F32)


def _sigmoid(x):
    return 0.5 * jnp.tanh(0.5 * x) + 0.5


def _rms_stats(x):
    r = lax.rsqrt(jnp.mean(x * x, axis=-1, keepdims=True) + NORM_EPS)
    return x * r, r


ROWS_WIDE = 16
MM_ROWS = 256


def _rms_bwd(dy, xhat, r, g):
    dg = jnp.sum(dy * xhat, axis=0, keepdims=True)
    dxh = dy * g
    dx = r * (dxh - xhat * jnp.mean(dxh * xhat, axis=-1, keepdims=True))
    return dx, dg


def _mm(a, b, *, mode, out_dtype, tm, tn, tk, name, add=None, scale=1.0, hosted=None):
    if mode == "nn":
        (m, k), (_, n) = a.shape, b.shape
    elif mode == "nt":
        (m, k), (n, _) = a.shape, b.shape
    else:
        (k, m), (_, n) = a.shape, b.shape
    assert m % tm == 0 and n % tn == 0 and k % tk == 0, (name, m, n, k, tm, tn, tk)
    nk = k // tk
    dot = {"nn": _dot_nn, "nt": _dot_nt, "tn": _dot_tn}[mode]
    a_spec = pl.BlockSpec((tk, tm), lambda i, j, kk: (kk, i)) if mode == "tn" else pl.BlockSpec((tm, tk), lambda i, j, kk: (i, kk))
    b_spec = pl.BlockSpec((tn, tk), lambda i, j, kk: (j, kk)) if mode == "nt" else pl.BlockSpec((tk, tn), lambda i, j, kk: (kk, j))
    o_spec = pl.BlockSpec((tm, tn), lambda i, j, kk: (i, j))
    has_add = add is not None

    def finish(prod, c_ref, o_ref):
        if scale != 1.0:
            prod = prod * scale
        o_ref[...] = ((c_ref[...] + prod) if has_add else prod).astype(out_dtype)

    def body(*refs):
        a_ref, b_ref = refs[:2]
        c_ref = refs[2] if has_add else None
        o_ref = refs[3] if has_add else refs[2]
        if nk == 1:
            finish(dot(a_ref[...], b_ref[...]), c_ref, o_ref)
            return
        acc_ref = refs[-1]
        kk = pl.program_id(2)

        @pl.when(kk == 0)
        def _():
            acc_ref[...] = jnp.zeros_like(acc_ref)

        acc_ref[...] += dot(a_ref[...], b_ref[...])

        @pl.when(kk == nk - 1)
        def _():
            finish(acc_ref[...], c_ref, o_ref)

    operands = (a, b, add) if has_add else (a, b)
    in_specs = [a_spec, b_spec] + ([o_spec] if has_add else [])
    (out,), got = _call(
        body, name=name, grid=(m // tm, n // tn, nk), in_specs=in_specs, out_specs=[o_spec],
        out_shape=[jax.ShapeDtypeStruct((m, n), out_dtype)], scratch_shapes=[pltpu.VMEM((tm, tn), F32)] if nk > 1 else [],
        operands=operands, sem=("parallel", "parallel", "arbitrary"), hosted=hosted)
    return out if hosted is None else (out, got)


def _rms_fwd(x, g, *, tm, name, hosted=None):
    t, d = x.shape

    def body(x_ref, g_ref, h_ref):
        xhat, _ = _rms_stats(x_ref[...])
        h_ref[...] = (xhat * g_ref[...]).astype(BF16)

    (h,), got = _call(
        body, name=name, grid=(t // tm,),
        in_specs=[pl.BlockSpec((tm, d), lambda i: (i, 0)), pl.BlockSpec((1, d), lambda i: (0, 0))],
        out_specs=[pl.BlockSpec((tm, d), lambda i: (i, 0))], out_shape=[jax.ShapeDtypeStruct((t, d), BF16)], scratch_shapes=[],
        operands=(x, g), sem=("parallel",), hosted=hosted)
    return h, got


def _ffn_fwd(x, g, wgT, wuT, wd, *, tm, hc, name, hosted=None, target=None):
    t, d = x.shape
    nj = DFF // hc
    with_loss = target is not None

    def body(*refs):
        x_ref, g_ref, wg_ref, wu_ref, wd_ref = refs[:5]
        t_ref = refs[5] if with_loss else None
        xo_ref, h_ref, a_ref, b_ref = refs[5 + with_loss:9 + with_loss]
        loss_ref = refs[9 + with_loss] if with_loss else None
        acc_ref = refs[-1]
        i, j = pl.program_id(0), pl.program_id(1)

        @pl.when(j == 0)
        def _():
            xhat, _ = _rms_stats(x_ref[...])
            h_ref[...] = (xhat * g_ref[...]).astype(BF16)
            acc_ref[...] = jnp.zeros_like(acc_ref)

        h = h_ref[...]
        a = _dot_nt(h, wg_ref[...])
        b = _dot_nt(h, wu_ref[...])
        a_ref[...] = a.astype(BF16)
        b_ref[...] = b.astype(BF16)
        s = (a * _sigmoid(a) * b).astype(BF16)
        acc_ref[...] += _dot_nn(s, wd_ref[...])

        if with_loss:
            @pl.when((i == 0) & (j == 0))
            def _():
                loss_ref[...] = jnp.zeros_like(loss_ref)

        @pl.when(j == nj - 1)
        def _():
            y = x_ref[...] + 0.5 * acc_ref[...]
            if with_loss:
                err = y - t_ref[...]
                xo_ref[...] = err * (1.0 / d)
                loss_ref[...] += jnp.sum(jnp.sum(err * err, axis=-1, keepdims=True), axis=0, keepdims=True) * (0.5 / d)
            else:
                xo_ref[...] = y

    row = pl.BlockSpec((tm, d), lambda i, j: (i, 0))
    vec = pl.BlockSpec((1, d), lambda i, j: (0, 0))
    wsp = pl.BlockSpec((hc, d), lambda i, j: (j, 0))
    hid = pl.BlockSpec((tm, hc), lambda i, j: (i, j))
    out_specs = [row, row, hid, hid] + ([pl.BlockSpec((1, 128), lambda i, j: (0, 0))] if with_loss else [])
    out_shape = [jax.ShapeDtypeStruct((t, d), F32), jax.ShapeDtypeStruct((t, d), BF16), jax.ShapeDtypeStruct((t, DFF), BF16),
                 jax.ShapeDtypeStruct((t, DFF), BF16)] + ([jax.ShapeDtypeStruct((1, 128), F32)] if with_loss else [])
    return _call(
        body, name=name, grid=(t // tm, nj), in_specs=[row, vec, wsp, wsp, wsp] + ([row] if with_loss else []),
        out_specs=out_specs, out_shape=out_shape, scratch_shapes=[pltpu.VMEM((tm, d), F32)],
        operands=(x, g, wgT, wuT, wd) + ((target,) if with_loss else ()),
        sem=("arbitrary" if with_loss else "parallel", "arbitrary"), hosted=hosted)


def _ffn_grads(dout, h, a, b, wd, *, tm, hc, name, hosted=None):
    t, d = dout.shape
    ni, nj = t // tm, DFF // hc

    def body(dout_ref, h_ref, a_ref, b_ref, wd_ref, da_ref, db_ref, dwg_ref, dwu_ref, dwd_ref,
             dy_all, h_all, ds_scr, s_scr, acc_g, acc_u, acc_d):
        j, i = pl.program_id(0), pl.program_id(1)
        rows_i = pl.ds(pl.multiple_of(i * tm, tm), tm)

        @pl.when(j == 0)
        def _():
            dy_all[rows_i, :] = (0.5 * dout_ref[...]).astype(BF16)
            h_all[rows_i, :] = h_ref[...]

        @pl.when(i == 0)
        def _():
            acc_g[...] = jnp.zeros_like(acc_g)
            acc_u[...] = jnp.zeros_like(acc_u)
            acc_d[...] = jnp.zeros_like(acc_d)

        def grad_rows(rows):
            ds = ds_scr[rows, :]
            av = a_ref[rows, :].astype(F32)
            bv = b_ref[rows, :].astype(F32)
            sg = _sigmoid(av)
            sl = av * sg
            s_scr[rows, :] = (sl * bv).astype(BF16)
            da_ref[rows, :] = (ds * bv * (sg + sl * (1.0 - sg))).astype(BF16)
            db_ref[rows, :] = (ds * sl).astype(BF16)

        for blk in range(tm // MM_ROWS):
            rs = slice(blk * MM_ROWS, (blk + 1) * MM_ROWS)
            ds_scr[rs, :] = _dot_nt(dy_all[pl.ds(pl.multiple_of(i * tm + blk * MM_ROWS, MM_ROWS), MM_ROWS), :], wd_ref[...])
            for c in range(MM_ROWS // ROWS_WIDE):
                grad_rows(slice(blk * MM_ROWS + c * ROWS_WIDE, blk * MM_ROWS + (c + 1) * ROWS_WIDE))

        dy_i = dy_all[rows_i, :]
        h_i = h_all[rows_i, :]
        acc_d[...] += _dot_tn(s_scr[...], dy_i)
        acc_g[...] += _dot_tn(da_ref[...], h_i)
        acc_u[...] += _dot_tn(db_ref[...], h_i)

        @pl.when(i == ni - 1)
        def _():
            dwg_ref[...] = acc_g[...].astype(BF16)
            dwu_ref[...] = acc_u[...].astype(BF16)
            dwd_ref[...] = acc_d[...].astype(BF16)

    first = pl.BlockSpec((tm, d), lambda j, i: (jnp.where(j == 0, i, 0), 0))
    hid = pl.BlockSpec((tm, hc), lambda j, i: (i, j))
    wsp = pl.BlockSpec((hc, d), lambda j, i: (j, 0))
    hid_shape = jax.ShapeDtypeStruct((t, DFF), BF16)
    w_shape = jax.ShapeDtypeStruct((DFF, d), BF16)
    return _call(
        body, name=name, grid=(nj, ni), in_specs=[first, first, hid, hid, wsp], out_specs=[hid, hid, wsp, wsp, wsp],
        out_shape=[hid_shape, hid_shape, w_shape, w_shape, w_shape],
        scratch_shapes=[pltpu.VMEM((t, d), BF16), pltpu.VMEM((t, d), BF16), pltpu.VMEM((tm, hc), F32), pltpu.VMEM((tm, hc), BF16),
                        pltpu.VMEM((hc, d), F32), pltpu.VMEM((hc, d), F32), pltpu.VMEM((hc, d), F32)],
        operands=(dout, h, a, b, wd), sem=("arbitrary", "arbitrary"), hosted=hosted)


def _proj_fwd(h, latT, convT, gateT, *, tm, name, hosted=None):
    t, d = h.shape

    def body(h_ref, wl_ref, wc_ref, wg_ref, lat_ref, conv_ref, gl_ref):
        hv = h_ref[...]
        lat_ref[...] = _dot_nt(hv, wl_ref[...]).astype(BF16)
        conv_ref[...] = _dot_nt(hv, wc_ref[...]).astype(BF16)
        gl_ref[...] = _dot_nt(hv, wg_ref[...]).astype(BF16)

    def rows(w):
        return pl.BlockSpec((tm, w), lambda i: (i, 0))

    def full(r):
        return pl.BlockSpec((r, d), lambda i: (0, 0))

    return _call(
        body, name=name, grid=(t // tm,), in_specs=[rows(d), full(LAT_PAD), full(CONV_COLS), full(GATE_COLS)],
        out_specs=[rows(LAT_PAD), rows(CONV_COLS), rows(GATE_COLS)],
        out_shape=[jax.ShapeDtypeStruct((t, LAT_PAD), BF16), jax.ShapeDtypeStruct((t, CONV_COLS), BF16),
                   jax.ShapeDtypeStruct((t, GATE_COLS), BF16)],
        scratch_shapes=[], operands=(h, latT, convT, gateT), sem=("parallel",), hosted=hosted)


def _proj_bwd(dlat, dconv3, dgl, latT, convT, gateT, x, g, dres, *, tm, name, hosted=None):
    t, d = x.shape

    def body(dl_ref, dc_ref, dg_ref, wl_ref, wc_ref, wg_ref, x_ref, g_ref, dres_ref, dx_ref, dgain_ref):
        @pl.when(pl.program_id(0) == 0)
        def _():
            dgain_ref[...] = jnp.zeros_like(dgain_ref)

        dh = _dot_nn(dl_ref[...], wl_ref[...]) + _dot_nn(dc_ref[...], wc_ref[...]) + _dot_nn(dg_ref[...], wg_ref[...])
        xhat, r = _rms_stats(x_ref[...])
        dx, dgain = _rms_bwd(dh, xhat, r, g_ref[...])
        dx_ref[...] = dres_ref[...] + dx
        dgain_ref[...] += dgain

    def rows(w):
        return pl.BlockSpec((tm, w), lambda i: (i, 0))

    def full(r):
        return pl.BlockSpec((r, d), lambda i: (0, 0))

    return _call(
        body, name=name, grid=(t // tm,),
        in_specs=[rows(LAT_PAD), rows(CONV_COLS), rows(GATE_COLS), full(LAT_PAD), full(CONV_COLS), full(GATE_COLS), rows(d), full(1), rows(d)],
        out_specs=[rows(d), full(1)], out_shape=[jax.ShapeDtypeStruct((t, d), F32), jax.ShapeDtypeStruct((1, d), F32)],
        scratch_shapes=[], operands=(dlat, dconv3, dgl, latT, convT, gateT, x, g, dres), sem=("arbitrary",), hosted=hosted)


def _ffn_up_bwd(da, db, wgT, wuT, x, g, dout, *, tm, name, hosted=None):
    t, d = x.shape

    def body(da_ref, db_ref, wg_ref, wu_ref, x_ref, g_ref, dout_ref, dx_ref, dg_ref):
        @pl.when(pl.program_id(0) == 0)
        def _():
            dg_ref[...] = jnp.zeros_like(dg_ref)

        dh = _dot_nn(da_ref[...], wg_ref[...]) + _dot_nn(db_ref[...], wu_ref[...])
        xhat, r = _rms_stats(x_ref[...])
        dx, dg = _rms_bwd(dh, xhat, r, g_ref[...])
        dx_ref[...] = dout_ref[...] + dx
        dg_ref[...] += dg

    row = pl.BlockSpec((tm, d), lambda i: (i, 0))
    vec = pl.BlockSpec((1, d), lambda i: (0, 0))
    hid = pl.BlockSpec((tm, DFF), lambda i: (i, 0))
    wsp = pl.BlockSpec((DFF, d), lambda i: (0, 0))
    return _call(
        body, name=name, grid=(t // tm,), in_specs=[hid, hid, wsp, wsp, row, vec, row], out_specs=[row, vec],
        out_shape=[jax.ShapeDtypeStruct((t, d), F32), jax.ShapeDtypeStruct((1, d), F32)], scratch_shapes=[],
        operands=(da, db, wgT, wuT, x, g, dout), sem=("arbitrary",), hosted=hosted)


HEAD_LANES = (slice(0, 32), slice(64, 80), None, slice(32, 64), slice(80, 96), None)


def _head_cols(a):
    def part(sl, width):
        if sl is None or sl.stop > a.shape[1]:
            return jnp.zeros((a.shape[0], width), a.dtype)
        return a[:, sl]

    return jnp.concatenate([part(sl, w) for sl, w in zip(HEAD_LANES, (32, 16, 16, 32, 16, 16))], axis=1)


def _head_cols_inv(a, dims):
    parts = [a[:, 0:32], a[:, 64:96]] + ([a[:, 32:48], a[:, 96:112]] if dims == QK_DIM else [])
    return jnp.concatenate(parts, axis=1)


def _rope_fwd(x, c, s):
    return x * c + pltpu.roll(x, HEAD_PAD // 2, 1) * s


def _rope_bwd(dy, c, s):
    return dy * c + pltpu.roll(dy * s, HEAD_PAD // 2, 1)


def _head_stats(x):
    r = lax.rsqrt(jnp.sum(x * x, axis=-1, keepdims=True) * (1.0 / QK_DIM) + NORM_EPS)
    return x * r, r


def _mla_prep_fwd(lat, gq, gkv, ghq, ghk, wq, wk, wv, rc, rs, *, tm, name):
    t = lat.shape[0]

    def body(lat_ref, gq_ref, gkv_ref, ghq_ref, ghk_ref, wq_ref, wk_ref, wv_ref, c_ref, s_ref,
             q_ref, k_ref, v_ref, qn_ref, ckv_ref):
        lat_v = lat_ref[...]
        qhat, _ = _rms_stats(lat_v[:, :Q_LORA].astype(F32))
        qn = (qhat * gq_ref[...]).astype(BF16)
        khat, _ = _rms_stats(lat_v[:, Q_LORA:Q_LORA + KV_LORA].astype(F32))
        ckv = (khat * gkv_ref[...]).astype(BF16)
        ckv_ext = jnp.concatenate([ckv, lat_v[:, Q_LORA + KV_LORA:]], axis=1)
        qn_ref[...] = qn
        ckv_ref[...] = ckv_ext
        q_pre = _dot_nn(qn, wq_ref[...])
        k_pre = _dot_nn(ckv_ext, wk_ref[...])
        v_ref[...] = _dot_nn(ckv, wv_ref[...]).astype(BF16)
        c, s = c_ref[...], s_ref[...]
        for h in range(N_HEADS):
            hs = slice(h * HEAD_PAD, (h + 1) * HEAD_PAD)
            xq, _ = _head_stats(q_pre[:, hs])
            q_ref[:, hs] = _rope_fwd(xq * ghq_ref[...], c, s).astype(BF16)
            xk, _ = _head_stats(k_pre[:, hs])
            k_ref[:, hs] = _rope_fwd(xk * ghk_ref[...], c, s).astype(BF16)

    def row(w):
        return pl.BlockSpec((tm, w), lambda i: (i, 0))

    def full(r, w):
        return pl.BlockSpec((r, w), lambda i: (0, 0))

    wide = jax.ShapeDtypeStruct((t, D), BF16)
    lat3 = jax.ShapeDtypeStruct((t, Q_LORA), BF16)
    return pl.pallas_call(
        body, name=name, grid=(t // tm,),
        in_specs=[row(LAT_PAD), full(1, Q_LORA), full(1, KV_LORA), full(1, HEAD_PAD), full(1, HEAD_PAD),
                  full(Q_LORA, D), full(Q_LORA, D), full(KV_LORA, D), row(HEAD_PAD), row(HEAD_PAD)],
        out_specs=[row(D), row(D), row(D), row(Q_LORA), row(Q_LORA)],
        out_shape=[wide, wide, wide, lat3, lat3],
        compiler_params=_params("parallel"),
    )(lat, gq, gkv, ghq, ghk, wq, wk, wv, rc, rs)


def _mla_prep_bwd(dq, dk, dv, lat, qn, ckv_ext, gq, gkv, ghq, ghk, wq, wk, wv, rc, rs, *, tm, name):
    t = lat.shape[0]

    def body(dq_ref, dk_ref, dv_ref, lat_ref, qn_ref, ckv_ref, gq_ref, gkv_ref, ghq_ref, ghk_ref, wq_ref, wk_ref, wv_ref,
             c_ref, s_ref, dlat_ref, dqp_ref, dkp_ref, dgq_ref, dgkv_ref, dghq_ref, dghk_ref):
        @pl.when(pl.program_id(0) == 0)
        def _():
            dgq_ref[...] = jnp.zeros_like(dgq_ref)
            dgkv_ref[...] = jnp.zeros_like(dgkv_ref)
            dghq_ref[...] = jnp.zeros_like(dghq_ref)
            dghk_ref[...] = jnp.zeros_like(dghk_ref)

        c, s = c_ref[...], s_ref[...]
        q_pre = _dot_nn(qn_ref[...], wq_ref[...])
        k_pre = _dot_nn(ckv_ref[...], wk_ref[...])

        def heads(pre, dy_ref, gh_ref, dgh_ref, out_ref):
            dgh = jnp.zeros((1, HEAD_PAD), F32)
            for h in range(N_HEADS):
                hs = slice(h * HEAD_PAD, (h + 1) * HEAD_PAD)
                d = _rope_bwd(dy_ref[:, hs].astype(F32), c, s)
                xhat, r = _head_stats(pre[:, hs])
                dgh = dgh + jnp.sum(d * xhat, axis=0, keepdims=True)
                dxh = d * gh_ref[...]
                dx = r * (dxh - xhat * (jnp.sum(dxh * xhat, axis=-1, keepdims=True) * (1.0 / QK_DIM)))
                out_ref[:, hs] = dx.astype(BF16)
            dgh_ref[...] += dgh

        heads(q_pre, dq_ref, ghq_ref, dghq_ref, dqp_ref)
        heads(k_pre, dk_ref, ghk_ref, dghk_ref, dkp_ref)
        dqn = _dot_nt(dqp_ref[...], wq_ref[...])
        dce = _dot_nt(dkp_ref[...], wk_ref[...])
        dckv = dce[:, :KV_LORA] + _dot_nt(dv_ref[...], wv_ref[...])
        lat_v = lat_ref[...]
        qhat, rq = _rms_stats(lat_v[:, :Q_LORA].astype(F32))
        dql, dgq = _rms_bwd(dqn, qhat, rq, gq_ref[...])
        khat, rk = _rms_stats(lat_v[:, Q_LORA:Q_LORA + KV_LORA].astype(F32))
        dkl, dgkv = _rms_bwd(dckv, khat, rk, gkv_ref[...])
        dgq_ref[...] += dgq
        dgkv_ref[...] += dgkv
        dlat_ref[...] = jnp.concatenate([dql, dkl, dce[:, KV_LORA:]], axis=1).astype(BF16)

    def row(w):
        return pl.BlockSpec((tm, w), lambda i: (i, 0))

    def full(r, w):
        return pl.BlockSpec((r, w), lambda i: (0, 0))

    return pl.pallas_call(
        body, name=name, grid=(t // tm,),
        in_specs=[row(D), row(D), row(D), row(LAT_PAD), row(Q_LORA), row(Q_LORA), full(1, Q_LORA), full(1, KV_LORA),
                  full(1, HEAD_PAD), full(1, HEAD_PAD), full(Q_LORA, D), full(Q_LORA, D), full(KV_LORA, D),
                  row(HEAD_PAD), row(HEAD_PAD)],
        out_specs=[row(LAT_PAD), row(D), row(D), full(1, Q_LORA), full(1, KV_LORA), full(1, HEAD_PAD), full(1, HEAD_PAD)],
        out_shape=[jax.ShapeDtypeStruct((t, LAT_PAD), BF16), jax.ShapeDtypeStruct((t, D), BF16), jax.ShapeDtypeStruct((t, D), BF16),
                   jax.ShapeDtypeStruct((1, Q_LORA), F32), jax.ShapeDtypeStruct((1, KV_LORA), F32),
                   jax.ShapeDtypeStruct((1, HEAD_PAD), F32), jax.ShapeDtypeStruct((1, HEAD_PAD), F32)],
        compiler_params=_params("arbitrary"),
    )(dq, dk, dv, lat, qn, ckv_ext, gq, gkv, ghq, ghk, wq, wk, wv, rc, rs)


def _causal_keep(tq):
    r = lax.broadcasted_iota(jnp.int32, (tq, tq), 0)
    c = lax.broadcasted_iota(jnp.int32, (tq, tq), 1)
    return c <= r


def _flash_fwd(q, k, v, *, n_seq, seq, tq, name, hosted=None):
    nq = seq // tq

    def body(q_ref, k_ref, v_ref, o_ref, lse_ref):
        for qi in range(nq):
            rows = slice(qi * tq, (qi + 1) * tq)
            qv = q_ref[rows, :]
            m = jnp.full((tq, 1), NEG, F32)
            l = jnp.zeros((tq, 1), F32)
            acc = jnp.zeros((tq, HEAD_PAD), F32)
            for j in range(qi + 1):
                cols = slice(j * tq, (j + 1) * tq)
                s = _dot_nt(qv, k_ref[cols, :]) * ATTN_SCALE
                if j == qi:
                    s = jnp.where(_causal_keep(tq), s, NEG)
                m_new = jnp.maximum(m, jnp.max(s, axis=-1, keepdims=True))
                alpha = jnp.exp(m - m_new)
                p = jnp.exp(s - m_new)
                l = alpha * l + jnp.sum(p, axis=-1, keepdims=True)
                acc = alpha * acc + _dot_nn(p.astype(BF16), v_ref[cols, :])
                m = m_new
            o_ref[rows, :] = (acc / l).astype(BF16)
            lse_ref[rows, :] = jnp.broadcast_to(m + jnp.log(l), (tq, HEAD_PAD))

    spec = pl.BlockSpec((seq, HEAD_PAD), lambda b, h: (b, h))
    t = n_seq * seq
    return _call(
        body, name=name, grid=(n_seq, N_HEADS), in_specs=[spec, spec, spec], out_specs=[spec, spec],
        out_shape=[jax.ShapeDtypeStruct((t, D), BF16), jax.ShapeDtypeStruct((t, D), F32)], scratch_shapes=[],
        operands=(q, k, v), sem=("parallel", "parallel"), hosted=hosted)


def _flash_bwd(q, k, v, o, lse, do, *, n_seq, seq, tq, name, hosted=None):
    nq = seq // tq

    def body(q_ref, k_ref, v_ref, o_ref, lse_ref, do_ref, dq_ref, dk_ref, dv_ref, dk_acc, dv_acc):
        j = pl.program_id(2)

        @pl.when(j == 0)
        def _():
            dq_ref[...] = jnp.zeros_like(dq_ref)

        dk_acc[...] = jnp.zeros_like(dk_acc)
        dv_acc[...] = jnp.zeros_like(dv_acc)
        kv = k_ref[...]
        vv = v_ref[...]

        def step(i, masked):
            rows = pl.ds(pl.multiple_of(i * tq, tq), tq)
            qi = q_ref[rows, :]
            doi = do_ref[rows, :]
            delta = jnp.sum(doi.astype(F32) * o_ref[rows, :].astype(F32), axis=-1, keepdims=True)
            s = _dot_nt(qi, kv) * ATTN_SCALE
            p = jnp.exp(s - lse_ref[rows, :][:, :1])
            if masked:
                p = jnp.where(_causal_keep(tq), p, 0.0)
            dv_acc[...] += _dot_tn(p.astype(BF16), doi)
            dp = _dot_nt(doi, vv)
            ds = (p * (dp - delta) * ATTN_SCALE).astype(BF16)
            dk_acc[...] += _dot_tn(ds, qi)
            dq_ref[rows, :] += _dot_nn(ds, kv)

        step(j, True)

        def loop_body(i, carry):
            step(i, False)
            return carry

        lax.fori_loop(j + 1, nq, loop_body, 0)
        dk_ref[...] = dk_acc[...]
        dv_ref[...] = dv_acc[...].astype(BF16)

    full = pl.BlockSpec((seq, HEAD_PAD), lambda b, h, j: (b, h))
    tile = pl.BlockSpec((tq, HEAD_PAD), lambda b, h, j: (b * nq + j, h))
    t = n_seq * seq
    return _call(
        body, name=name, grid=(n_seq, N_HEADS, nq), in_specs=[full, tile, tile, full, full, full],
        out_specs=[full, tile, tile],
        out_shape=[jax.ShapeDtypeStruct((t, D), F32), jax.ShapeDtypeStruct((t, D), F32), jax.ShapeDtypeStruct((t, D), BF16)],
        scratch_shapes=[pltpu.VMEM((tq, HEAD_PAD), F32), pltpu.VMEM((tq, HEAD_PAD), F32)],
        operands=(q, k, v, o, lse, do), sem=("parallel", "parallel", "arbitrary"), hosted=hosted)


CONV_CB = 256


def _shift_down(u, k, row):
    return jnp.where(row >= k, pltpu.roll(u, k, 0), 0.0)


def _shift_up(u, k, row, n):
    return jnp.where(row < n - k, pltpu.roll(u, n - k, 0), 0.0)


def _conv_fwd(conv3, cw, *, n_seq, seq, name, hosted=None):
    def body(c_ref, w_ref, p_ref):
        blk = c_ref[...].astype(F32)
        xc, gb, gc = blk[:, :CONV_CB], blk[:, CONV_CB:2 * CONV_CB], blk[:, 2 * CONV_CB:]
        row = lax.broadcasted_iota(jnp.int32, (seq, CONV_CB), 0)
        u = gc * xc
        z = w_ref[0:1, :] * _shift_down(u, 2, row) + w_ref[1:2, :] * _shift_down(u, 1, row) + w_ref[2:3, :] * u
        p_ref[...] = (gb * z).astype(BF16)

    (p,), got = _call(
        body, name=name, grid=(n_seq, D // CONV_CB),
        in_specs=[pl.BlockSpec((seq, 3 * CONV_CB), lambda b, j: (b, j)), pl.BlockSpec((3, CONV_CB), lambda b, j: (0, j))],
        out_specs=[pl.BlockSpec((seq, CONV_CB), lambda b, j: (b, j))],
        out_shape=[jax.ShapeDtypeStruct((n_seq * seq, D), BF16)], scratch_shapes=[],
        operands=(conv3, cw), sem=("parallel", "parallel"), hosted=hosted)
    return p, got


def _conv_bwd(dp, conv3, cw, *, n_seq, seq, name):
    def body(dp_ref, c_ref, w_ref, dc_ref, dw_ref):
        @pl.when(pl.program_id(1) == 0)
        def _():
            dw_ref[...] = jnp.zeros_like(dw_ref)

        blk = c_ref[...].astype(F32)
        xc, gb, gc = blk[:, :CONV_CB], blk[:, CONV_CB:2 * CONV_CB], blk[:, 2 * CONV_CB:]
        row = lax.broadcasted_iota(jnp.int32, (seq, CONV_CB), 0)
        w0, w1, w2 = w_ref[0:1, :], w_ref[1:2, :], w_ref[2:3, :]
        u = gc * xc
        u1 = _shift_down(u, 1, row)
        u2 = _shift_down(u, 2, row)
        z = w0 * u2 + w1 * u1 + w2 * u
        dpv = dp_ref[...].astype(F32)
        dz = dpv * gb
        du = w2 * dz + w1 * _shift_up(dz, 1, row, seq) + w0 * _shift_up(dz, 2, row, seq)
        dc_ref[...] = jnp.concatenate([du * gc, dpv * z, du * xc], axis=1).astype(BF16)
        dw_ref[0:1, :] += jnp.sum(dz * u2, axis=0, keepdims=True)
        dw_ref[1:2, :] += jnp.sum(dz * u1, axis=0, keepdims=True)
        dw_ref[2:3, :] += jnp.sum(dz * u, axis=0, keepdims=True)

    return pl.pallas_call(
        body, name=name, grid=(D // CONV_CB, n_seq),
        in_specs=[pl.BlockSpec((seq, CONV_CB), lambda j, b: (b, j)), pl.BlockSpec((seq, 3 * CONV_CB), lambda j, b: (b, j)),
                  pl.BlockSpec((3, CONV_CB), lambda j, b: (0, j))],
        out_specs=[pl.BlockSpec((seq, 3 * CONV_CB), lambda j, b: (b, j)), pl.BlockSpec((3, CONV_CB), lambda j, b: (0, j))],
        out_shape=[jax.ShapeDtypeStruct((n_seq * seq, CONV_COLS), BF16), jax.ShapeDtypeStruct((3, D), F32)],
        compiler_params=_params("parallel", "arbitrary"),
    )(dp, conv3, cw)


def _merge_fwd(o, p, gl, bias, x1, wpa, wpc, wout, *, tm, name, hosted=None):
    t = x1.shape[0]

    def body(o_ref, p_ref, gl_ref, b_ref, x_ref, wpa_ref, wpc_ref, wout_ref, x2_ref, mg_ref, ya_ref, yb_ref):
        ya = _dot_nn(o_ref[...], wpa_ref[...])
        yb = _dot_nn(p_ref[...], wpc_ref[...])
        gates = _sigmoid(gl_ref[...].astype(F32) + b_ref[...])
        merged = (gates[:, :D] * ya + gates[:, D:] * yb).astype(BF16)
        ya_ref[...] = ya.astype(BF16)
        yb_ref[...] = yb.astype(BF16)
        mg_ref[...] = merged
        x2_ref[...] = x_ref[...] + _dot_nn(merged, wout_ref[...])

    row = pl.BlockSpec((tm, D), lambda i: (i, 0))
    row2 = pl.BlockSpec((tm, GATE_COLS), lambda i: (i, 0))
    wsp = pl.BlockSpec((D, D), lambda i: (0, 0))
    wide = jax.ShapeDtypeStruct((t, D), BF16)
    return _call(
        body, name=name, grid=(t // tm,),
        in_specs=[row, row, row2, pl.BlockSpec((1, GATE_COLS), lambda i: (0, 0)), row, wsp, wsp, wsp],
        out_specs=[row, row, row, row], out_shape=[jax.ShapeDtypeStruct((t, D), F32), wide, wide, wide], scratch_shapes=[],
        operands=(o, p, gl, bias, x1, wpa, wpc, wout), sem=("parallel",), hosted=hosted)


def _merge_bwd(dx2, ya, yb, gl, bias, wpa, wpc, wout, *, tm, name, hosted=None):
    t = dx2.shape[0]

    def body(dx_ref, ya_ref, yb_ref, gl_ref, b_ref, wpa_ref, wpc_ref, wout_ref,
             dxb_ref, dya_ref, dyb_ref, dgl_ref, do_ref, dp_ref, db_ref):
        @pl.when(pl.program_id(0) == 0)
        def _():
            db_ref[...] = jnp.zeros_like(db_ref)

        dxb = dx_ref[...].astype(BF16)
        dxb_ref[...] = dxb
        dm = _dot_nt(dxb, wout_ref[...])
        gates = _sigmoid(gl_ref[...].astype(F32) + b_ref[...])
        ga, gb = gates[:, :D], gates[:, D:]
        dya = (dm * ga).astype(BF16)
        dyb = (dm * gb).astype(BF16)
        dya_ref[...] = dya
        dyb_ref[...] = dyb
        dgl = jnp.concatenate([dm * ya_ref[...].astype(F32) * ga * (1.0 - ga),
                               dm * yb_ref[...].astype(F32) * gb * (1.0 - gb)], axis=1)
        dgl_ref[...] = dgl.astype(BF16)
        db_ref[...] += jnp.sum(dgl, axis=0, keepdims=True)
        do_ref[...] = _dot_nt(dya, wpa_ref[...]).astype(BF16)
        dp_ref[...] = _dot_nt(dyb, wpc_ref[...]).astype(BF16)

    row = pl.BlockSpec((tm, D), lambda i: (i, 0))
    row2 = pl.BlockSpec((tm, GATE_COLS), lambda i: (i, 0))
    vec2 = pl.BlockSpec((1, GATE_COLS), lambda i: (0, 0))
    wsp = pl.BlockSpec((D, D), lambda i: (0, 0))
    wide = jax.ShapeDtypeStruct((t, D), BF16)
    return _call(
        body, name=name, grid=(t // tm,), in_specs=[row, row, row, row2, vec2, wsp, wsp, wsp],
        out_specs=[row, row, row, row2, row, row, vec2],
        out_shape=[wide, wide, wide, jax.ShapeDtypeStruct((t, GATE_COLS), BF16), wide, wide,
                   jax.ShapeDtypeStruct((1, GATE_COLS), F32)],
        scratch_shapes=[], operands=(dx2, ya, yb, gl, bias, wpa, wpc, wout), sem=("arbitrary",), hosted=hosted)


def _adamw_small(ws, gs, ms, vs, *, name):
    n = len(ws)
    c1 = 1.0 / (1.0 - ADAM_B1 ** ADAM_STEP)
    c2 = 1.0 / (1.0 - ADAM_B2 ** ADAM_STEP)

    def body(*refs):
        for i in range(n):
            w_ref, g_ref, m_ref, v_ref, d_ref, nm_ref, nv_ref = (refs[k * n + i] for k in range(7))
            gv = g_ref[...]
            nm = ADAM_B1 * m_ref[...] + (1.0 - ADAM_B1) * gv
            nv = ADAM_B2 * v_ref[...] + (1.0 - ADAM_B2) * (gv * gv)
            nm_ref[...] = nm
            nv_ref[...] = nv
            d_ref[...] = -ADAM_LR * ((nm * c1) / (jnp.sqrt(nv * c2) + ADAM_EPS) + ADAM_WD * w_ref[...])

    vm = pl.BlockSpec(memory_space=pltpu.VMEM)
    shapes = [jax.ShapeDtypeStruct(a.shape, F32) for a in ws]
    outs = pl.pallas_call(body, name=name, in_specs=[vm] * (4 * n), out_specs=[vm] * (3 * n), out_shape=shapes * 3)(*ws, *gs, *ms, *vs)
    return outs[:n], outs[n:2 * n], outs[2 * n:]


def _adamw(w, g, m, v, *, name):
    rows, cols = w.shape
    tr = max([c for c in range(8, 513, 8) if rows % c == 0], default=rows)
    c1 = 1.0 / (1.0 - ADAM_B1 ** ADAM_STEP)
    c2 = 1.0 / (1.0 - ADAM_B2 ** ADAM_STEP)

    def body(w_ref, g_ref, m_ref, v_ref, d_ref, nm_ref, nv_ref):
        gv = g_ref[...]
        nm = ADAM_B1 * m_ref[...] + (1.0 - ADAM_B1) * gv
        nv = ADAM_B2 * v_ref[...] + (1.0 - ADAM_B2) * (gv * gv)
        nm_ref[...] = nm
        nv_ref[...] = nv
        d_ref[...] = -ADAM_LR * ((nm * c1) / (jnp.sqrt(nv * c2) + ADAM_EPS) + ADAM_WD * w_ref[...])

    spec = pl.BlockSpec((tr, cols), lambda i: (i, 0))
    shp = jax.ShapeDtypeStruct((rows, cols), F32)
    return pl.pallas_call(
        body, name=name, grid=(rows // tr,), in_specs=[spec] * 4, out_specs=[spec] * 3, out_shape=[shp] * 3,
        compiler_params=_params("parallel"),
    )(w, g, m, v)


def _place():
    return lax.axis_index("x"), lax.axis_index("y"), lax.axis_index("c")


def _other_chips(x, y):
    return [(1 - x, y), (x, 1 - y), (1 - x, 1 - y)]


def _remote(src, dst, send, recv, dev):
    return pltpu.make_async_remote_copy(src_ref=src, dst_ref=dst, send_sem=send, recv_sem=recv, device_id=dev, device_id_type=MESH)


def _gather_chips_plan(n):
    def start(srcs, dsts, send, recv, local):
        x, y, cc = _place()
        me = 4 * x + 2 * y + cc
        for a in range(n):
            pltpu.make_async_copy(srcs[a], dsts[a].at[me], local.at[a]).start()
            for k, (px, py) in enumerate(_other_chips(x, y)):
                _remote(srcs[a], dsts[a].at[me], send.at[3 * a + k], recv.at[3 * a + k], (px, py, cc)).start()

    def wait(srcs, dsts, send, recv, local):
        x, y, cc = _place()
        me = 4 * x + 2 * y + cc
        for a in range(n):
            for k, (px, py) in enumerate(_other_chips(x, y)):
                _remote(srcs[a], dsts[a].at[4 * px + 2 * py + cc], send.at[3 * a + k], recv.at[3 * a + k], (px, py, cc)).wait_recv()
        for a in range(n):
            for k, (px, py) in enumerate(_other_chips(x, y)):
                _remote(srcs[a], dsts[a].at[me], send.at[3 * a + k], recv.at[3 * a + k], (px, py, cc)).wait_send()
            pltpu.make_async_copy(srcs[a], dsts[a].at[me], local.at[a]).wait()

    return _Plan(start, wait, 3 * n, n)


def _scatter_chips_plan(n):
    def start(srcs, dsts, send, recv, local):
        x, y, cc = _place()
        for a in range(n):
            for k, (px, py) in enumerate(_other_chips(x, y)):
                _remote(srcs[a].at[2 * px + py], dsts[a].at[k], send.at[3 * a + k], recv.at[3 * a + k], (px, py, cc)).start()

    def wait(srcs, dsts, send, recv, local):
        x, y, cc = _place()
        for a in range(n):
            for k, (px, py) in enumerate(_other_chips(x, y)):
                _remote(srcs[a].at[k], dsts[a].at[k], send.at[3 * a + k], recv.at[3 * a + k], (px, py, cc)).wait_recv()
        for a in range(n):
            for k, (px, py) in enumerate(_other_chips(x, y)):
                _remote(srcs[a].at[k], dsts[a].at[k], send.at[3 * a + k], recv.at[3 * a + k], (px, py, cc)).wait_send()

    return _Plan(start, wait, 3 * n, 0)


def _gather_shapes(blocks):
    return [jax.ShapeDtypeStruct((N_DEV,) + b.shape, b.dtype) for b in blocks]


def _scatter_shapes(parts):
    return [jax.ShapeDtypeStruct((3,) + p.shape[1:], p.dtype) for p in parts]


def _gather_sibling_plan(n):
    def start(srcs, dsts, send, recv, local):
        x, y, cc = _place()
        for a in range(n):
            for q in range(4):
                _remote(srcs[a].at[2 * q + cc], dsts[a].at[2 * q + cc], send.at[4 * a + q], recv.at[4 * a + q], (x, y, 1 - cc)).start()

    def wait(srcs, dsts, send, recv, local):
        x, y, cc = _place()
        for a in range(n):
            for q in range(4):
                _remote(srcs[a].at[2 * q + cc], dsts[a].at[2 * q + 1 - cc], send.at[4 * a + q], recv.at[4 * a + q],
                        (x, y, 1 - cc)).wait_recv()
        for a in range(n):
            for q in range(4):
                _remote(srcs[a].at[2 * q + cc], dsts[a].at[2 * q + cc], send.at[4 * a + q], recv.at[4 * a + q],
                        (x, y, 1 - cc)).wait_send()

    return _Plan(start, wait, 4 * n, 0, in_place=True)


def _scatter_sibling_plan(n):
    def start(srcs, dsts, send, recv, local):
        x, y, cc = _place()
        for a in range(n):
            for q in range(4):
                _remote(srcs[a].at[2 * q + 1 - cc], dsts[a].at[q], send.at[4 * a + q], recv.at[4 * a + q], (x, y, 1 - cc)).start()

    def wait(srcs, dsts, send, recv, local):
        x, y, cc = _place()
        for a in range(n):
            for q in range(4):
                _remote(srcs[a].at[q], dsts[a].at[q], send.at[4 * a + q], recv.at[4 * a + q], (x, y, 1 - cc)).wait_recv()
        for a in range(n):
            for q in range(4):
                _remote(srcs[a].at[q], dsts[a].at[q], send.at[4 * a + q], recv.at[4 * a + q], (x, y, 1 - cc)).wait_send()

    return _Plan(start, wait, 4 * n, 0)


def _same_shapes(arrs):
    return [jax.ShapeDtypeStruct(a.shape, a.dtype) for a in arrs]


def _halved_shapes(parts):
    return [jax.ShapeDtypeStruct((4,) + p.shape[1:], p.dtype) for p in parts]


def _run_plan(plan, srcs, out_shapes, *, name):
    n_in, n_out = len(srcs), len(out_shapes)

    def body(*refs):
        h_in, h_out, sems = refs[:n_in], refs[n_in:n_in + n_out], refs[n_in + n_out:]
        plan.start(h_in, h_out, *sems)
        plan.wait(h_in, h_out, *sems)

    return pl.pallas_call(body, name=name, in_specs=[ANY] * n_in, out_specs=[ANY] * n_out, out_shape=list(out_shapes),
                          input_output_aliases={a: a for a in range(n_in)} if plan.in_place else {},
                          scratch_shapes=plan.sems())(*srcs)


SEM = pl.BlockSpec(memory_space=pltpu.SEMAPHORE)
HBM = pl.BlockSpec(memory_space=pltpu.HBM)
SIDE_EFFECT = pltpu.CompilerParams(has_side_effects=pltpu.SideEffectType.DATAFLOW_SIDE_EFFECTING)


def _plan_start(plan, blocks, land_shapes, *, name):
    n = len(blocks)
    lands = [lax.empty(s.shape, s.dtype) for s in land_shapes]

    def body(*refs):
        srcs, sems, lands_out, token = refs[:n], refs[2 * n:2 * n + 3], refs[3 * n + 3:4 * n + 3], refs[4 * n + 3]
        plan.start(srcs, lands_out, *sems)
        token[...] = jnp.zeros_like(token)

    out_shape = ([s for s in plan.sems()] + [pltpu.HBM(b.shape, b.dtype) for b in blocks]
                 + [pltpu.HBM(l.shape, l.dtype) for l in lands] + [jax.ShapeDtypeStruct((8, 128), F32)])
    res = pl.pallas_call(
        body, name=name, in_specs=[HBM] * (2 * n), out_specs=[SEM] * 3 + [HBM] * (2 * n) + [pl.BlockSpec(memory_space=pltpu.VMEM)],
        out_shape=out_shape, input_output_aliases={a: 3 + a for a in range(2 * n)}, compiler_params=SIDE_EFFECT,
    )(*[pltpu.with_memory_space_constraint(a, pltpu.HBM) for a in list(blocks) + lands])
    return res[:3], res[3:3 + n], res[3 + n:3 + 2 * n], res[3 + 2 * n]


def _plan_wait(plan, sems, blocks, lands, after, *, name):
    n = len(blocks)

    def body(*refs):
        plan.wait(refs[:n], refs[n:2 * n], *refs[2 * n:2 * n + 3])

    res = pl.pallas_call(
        body, name=name, in_specs=[HBM] * (2 * n) + [SEM] * 3 + [ANY], out_specs=[HBM] * (2 * n),
        out_shape=[pltpu.HBM(a.shape, a.dtype) for a in list(blocks) + list(lands)],
        input_output_aliases={a: a for a in range(2 * n)}, compiler_params=SIDE_EFFECT,
    )(*blocks, *lands, *sems, after)
    return list(res[:n]), list(res[n:])


def _sum_sibling(ps, qs, core, *, name):
    n = len(ps)

    def body(core_ref, *refs):
        for p_ref, q_ref, o_ref in zip(refs[:n], refs[n:2 * n], refs[2 * n:]):
            o_ref[...] = (p_ref[...].astype(F32) + q_ref[...].astype(F32)).astype(BF16)

    def mine(p):
        return pl.BlockSpec((1,) + p.shape[1:], lambda ch, core_ref: (2 * ch + core_ref[0], 0, 0))

    def theirs(p):
        return pl.BlockSpec((1,) + p.shape[1:], lambda ch, core_ref: (ch, 0, 0))

    grid_spec = pltpu.PrefetchScalarGridSpec(
        num_scalar_prefetch=1, grid=(4,), in_specs=[mine(p) for p in ps] + [theirs(p) for p in ps], out_specs=[theirs(p) for p in ps])
    return pl.pallas_call(
        body, name=name, grid_spec=grid_spec, out_shape=[jax.ShapeDtypeStruct((4,) + p.shape[1:], BF16) for p in ps],
        compiler_params=_params("parallel"),
    )(core, *ps, *qs)


def _sum_chips(s1, r2, chip, *, name):
    _, r, c = s1.shape

    def body(chip_ref, s_ref, r_ref, o_ref):
        acc = s_ref[0].astype(F32)
        for k in range(3):
            acc = acc + r_ref[k].astype(F32)
        o_ref[...] = acc

    grid_spec = pltpu.PrefetchScalarGridSpec(
        num_scalar_prefetch=1, grid=(1,),
        in_specs=[pl.BlockSpec((1, r, c), lambda i, chip_ref: (chip_ref[0], 0, 0)),
                  pl.BlockSpec((3, r, c), lambda i, chip_ref: (0, 0, 0))],
        out_specs=pl.BlockSpec((r, c), lambda i, chip_ref: (0, 0)))
    return pl.pallas_call(
        body, name=name, grid_spec=grid_spec, out_shape=jax.ShapeDtypeStruct((r, c), F32),
        compiler_params=_params("arbitrary"),
    )(chip, s1, r2)


def _sum_adamw(s1s, r2s, chip, ws, ms, vs, *, name):
    n = len(s1s)
    _, r, c = s1s[0].shape
    tr = r // 2
    c1 = 1.0 / (1.0 - ADAM_B1 ** ADAM_STEP)
    c2 = 1.0 / (1.0 - ADAM_B2 ** ADAM_STEP)

    def body(chip_ref, *refs):
        for a in range(n):
            s_ref, r_ref, w_ref, m_ref, v_ref = (refs[k * n + a] for k in range(5))
            g_ref, d_ref, nm_ref, nv_ref = refs[5 * n + 4 * a:5 * n + 4 * a + 4]
            gv = s_ref[0].astype(F32)
            for k in range(3):
                gv = gv + r_ref[k].astype(F32)
            g_ref[...] = gv
            nm = ADAM_B1 * m_ref[...] + (1.0 - ADAM_B1) * gv
            nv = ADAM_B2 * v_ref[...] + (1.0 - ADAM_B2) * (gv * gv)
            nm_ref[...] = nm
            nv_ref[...] = nv
            d_ref[...] = -ADAM_LR * ((nm * c1) / (jnp.sqrt(nv * c2) + ADAM_EPS) + ADAM_WD * w_ref[...])

    flat = pl.BlockSpec((tr, c), lambda i, chip_ref: (i, 0))
    own = pl.BlockSpec((1, tr, c), lambda i, chip_ref: (chip_ref[0], i, 0))
    got = pl.BlockSpec((3, tr, c), lambda i, chip_ref: (0, i, 0))
    grid_spec = pltpu.PrefetchScalarGridSpec(
        num_scalar_prefetch=1, grid=(2,), in_specs=[own] * n + [got] * n + [flat] * (3 * n), out_specs=[flat] * (4 * n))
    res = pl.pallas_call(
        body, name=name, grid_spec=grid_spec, out_shape=[jax.ShapeDtypeStruct((r, c), F32)] * (4 * n),
        compiler_params=_params("parallel"),
    )(chip, *s1s, *r2s, *ws, *ms, *vs)
    return [res[4 * a:4 * a + 4] for a in range(n)]


def _small_exchange(v, *, reduce, name):
    r, c = v.shape

    def body(x_ref, o_ref, *rest):
        if reduce:
            buf_ref, send_sems, recv_sems = rest
        else:
            buf_ref = o_ref
            send_sems, recv_sems = rest
        x, y, cc = _place()
        me = 4 * x + 2 * y + cc

        def peer(k):
            return ((1 - x) if k & 4 else x, (1 - y) if k & 2 else y, (1 - cc) if k & 1 else cc)

        buf_ref[me] = x_ref[...]
        sends = []
        for k in range(1, N_DEV):
            cp = pltpu.make_async_remote_copy(src_ref=x_ref, dst_ref=buf_ref.at[me], send_sem=send_sems.at[k - 1],
                                              recv_sem=recv_sems.at[k - 1], device_id=peer(k), device_id_type=MESH)
            cp.start()
            sends.append(cp)
        for k in range(1, N_DEV):
            px, py, pc = peer(k)
            pltpu.make_async_remote_copy(src_ref=x_ref, dst_ref=buf_ref.at[4 * px + 2 * py + pc], send_sem=send_sems.at[k - 1],
                                         recv_sem=recv_sems.at[k - 1], device_id=peer(k), device_id_type=MESH).wait_recv()
        for cp in sends:
            cp.wait_send()
        if reduce:
            acc = buf_ref[0]
            for s in range(1, N_DEV):
                acc = acc + buf_ref[s]
            o_ref[...] = acc

    vm = pl.BlockSpec(memory_space=pltpu.VMEM)
    sems = [pltpu.SemaphoreType.DMA((N_DEV - 1,)), pltpu.SemaphoreType.DMA((N_DEV - 1,))]
    if reduce:
        out_shape, scratch = jax.ShapeDtypeStruct((r, c), F32), [pltpu.VMEM((N_DEV, r, c), F32)] + sems
    else:
        out_shape, scratch = jax.ShapeDtypeStruct((N_DEV, r, c), F32), sems
    return pl.pallas_call(body, name=name, in_specs=[vm], out_specs=vm, out_shape=out_shape, scratch_shapes=scratch)(v)


def _rows(a):
    return a.reshape(-1, D)


def _pad_cols(a, to):
    return jnp.pad(a, ((0, 0), (0, to - a.shape[1])))


def _pack_weights(w):
    parts = {
        "w_inT": jnp.pad(w["w_in"].T, ((0, IN_SHARD_PAD - IN_SHARD), (0, 0))),
        "w_uq": _rows(_head_cols(w["w_uq"])), "w_uk": _rows(_head_cols(w["w_uk"])),
        "w_uv": _rows(_pad_cols(w["w_uv"], HEAD_PAD)), "w_pa": _rows(w["w_proj_attn"]),
        "w_pc": w["w_proj_conv"], "w_out": w["w_out"],
    }
    return [jnp.concatenate([parts[n].astype(BF16) for n, _ in group], axis=0) for group in PACK]


def _cols_from_shards(gs, name, rows):
    idx, off, r = PACK_OFF[name]
    return gs[idx][:, off:off + r].reshape(N_DEV, rows, HEAD_PAD).transpose(1, 0, 2).reshape(rows, N_DEV * HEAD_PAD)


def _rows_from_shards(gs, name, keep=None):
    idx, off, r = PACK_OFF[name]
    keep = r if keep is None else keep
    return gs[idx][:, off:off + keep].reshape(N_DEV * keep, D)


def _rope_placement():
    i = lax.broadcasted_iota(jnp.int32, (HEAD_PAD, D), 0)
    j = lax.broadcasted_iota(jnp.int32, (HEAD_PAD, D), 1)
    lane = jnp.where(i < ROPE_HALF, 32 + i, 96 + i - ROPE_HALF)
    return ((i < 2 * ROPE_HALF) & (j % HEAD_PAD == lane)).astype(BF16)


def _unpack_in(g_in):
    w_inT = _rows_from_shards([g_in, None], "w_inT", IN_SHARD)
    lat_rows = Q_LORA + KV_LORA + 2 * ROPE_HALF
    conv = w_inT[lat_rows:lat_rows + CONV_COLS].reshape(3, D // CONV_CB, CONV_CB, D).transpose(1, 0, 2, 3).reshape(CONV_COLS, D)
    return {"latT": jnp.pad(w_inT[:lat_rows], ((0, LAT_PAD - lat_rows), (0, 0))), "convT": conv,
            "gateT": w_inT[lat_rows + CONV_COLS:]}


def _unpack_misc(g_misc):
    g = [None, g_misc]
    wpa = _cols_from_shards(g, "w_pa", 512).reshape(N_HEADS, NOPE, D)
    return {
        "wq": _cols_from_shards(g, "w_uq", Q_LORA),
        "wk": jnp.concatenate([_cols_from_shards(g, "w_uk", KV_LORA), _rope_placement()], axis=0),
        "wv": _cols_from_shards(g, "w_uv", KV_LORA),
        "wpa": jnp.pad(wpa, ((0, 0), (0, HEAD_PAD - NOPE), (0, 0))).reshape(D, D),
        "wpc": _rows_from_shards(g, "w_pc"), "wout": _rows_from_shards(g, "w_out"),
    }


def _shards_from_cols(a):
    rows = a.shape[0]
    return a.reshape(rows, N_DEV, HEAD_PAD).transpose(1, 0, 2).reshape(N_DEV, rows * HEAD_PAD // D, D)


def _pack_grads(gw):
    lat_rows = Q_LORA + KV_LORA + 2 * ROPE_HALF
    conv = gw["convT"].reshape(D // CONV_CB, 3, CONV_CB, D).transpose(1, 0, 2, 3).reshape(CONV_COLS, D)
    w_inT = jnp.concatenate([gw["latT"][:lat_rows], conv, gw["gateT"]], axis=0).reshape(N_DEV, IN_SHARD, D)
    wpa = gw["wpa"].reshape(N_HEADS, HEAD_PAD, D)[:, :NOPE].reshape(N_HEADS * NOPE, D)
    parts = {}
    parts.update({
        "w_inT": jnp.pad(w_inT, ((0, 0), (0, IN_SHARD_PAD - IN_SHARD), (0, 0))),
        "w_uq": _shards_from_cols(gw["wq"]), "w_uk": _shards_from_cols(gw["wk"][:KV_LORA]),
        "w_uv": _shards_from_cols(gw["wv"][:KV_LORA]), "w_pa": _shards_from_cols(wpa),
        "w_pc": gw["wpc"].reshape(N_DEV, D // N_DEV, D), "w_out": gw["wout"].reshape(N_DEV, D // N_DEV, D),
    })
    return [jnp.concatenate([parts[n] for n, _ in group], axis=1) for group in PACK]


def _unpack_grads(mines):
    def seg(name, keep=None):
        idx, off, r = PACK_OFF[name]
        return mines[idx][off:off + (r if keep is None else keep)]

    return {
        "w_in": seg("w_inT", IN_SHARD).T,
        "w_uq": _head_cols_inv(seg("w_uq").reshape(Q_LORA, HEAD_PAD), QK_DIM),
        "w_uk": _head_cols_inv(seg("w_uk").reshape(KV_LORA, HEAD_PAD), NOPE),
        "w_uv": seg("w_uv").reshape(KV_LORA, HEAD_PAD)[:, :NOPE],
        "w_proj_attn": seg("w_pa").reshape(512, HEAD_PAD),
        "w_proj_conv": seg("w_pc"), "w_out": seg("w_out"),
    }


def _rope_tables(positions):
    lane = jnp.arange(HEAD_PAD)
    idx = jnp.where((lane >= 32) & (lane < 48), lane - 32, jnp.where((lane >= 96) & (lane < 112), lane - 96, -1))
    inv_freq = jnp.where(idx >= 0, 1.0 / (ROPE_THETA ** (idx.astype(F32) / ROPE_HALF)), 0.0)
    ang = positions.reshape(-1).astype(F32)[:, None] * inv_freq
    return jnp.cos(ang), jnp.sin(ang) * jnp.where(lane < HEAD_PAD // 2, -1.0, 1.0)


def _local_step(x, positions, target, small, ex):
    n_seq, seq, d = x.shape
    t = n_seq * seq
    x0 = x.reshape(t, d)
    tgt = target.reshape(t, d)
    rc, rs = _rope_tables(positions)
    ghq = _head_cols(small["q_head_norm"])
    ghk = _head_cols(small["k_head_norm"])
    TM, HC, TQ = 1024, 256, 1024

    def mm(*args, hosted=None, **kw):
        res = _mm(*args, hosted=hosted, **kw)
        return res if hosted is not None else (res, None)

    def wgrad(a, b, name, tm=None, hosted=None):
        tm = tm or a.shape[1]
        return mm(a, b, mode="tn", out_dtype=BF16, tm=tm, tn=b.shape[1], tk=2048 if tm <= D else 1024, name=name, hosted=hosted)

    f1g, f1u, f1d = ex.gather_finish(ex.witness() + rc[:8])
    conv_w, landed = ex.gather_conv_w(f1d)
    (x1, h1, a1, b1), got = _ffn_fwd(x0, small["ffn1_norm"] + landed, f1g, f1u, f1d, tm=512, hc=DFF // 2, name="ffn1_fwd",
                                     hosted=ex.gather_chips("mix_in"))
    hm, got = _rms_fwd(x1, small["mix_norm"], tm=TM, name="mix_norm_fwd", hosted=ex.gather_sibling(got))
    W = ex.mix_in_weights(got)
    (lat, conv3, gl), got = _proj_fwd(hm, W["latT"], W["convT"], W["gateT"], tm=512, name="proj_fwd",
                                      hosted=ex.gather_chips("mix_misc"))
    p, got = _conv_fwd(conv3, conv_w, n_seq=n_seq, seq=seq, name="conv_fwd", hosted=ex.gather_sibling(got))
    W.update(ex.mix_misc_weights(got))
    q, k, v, qn, ckv = _mla_prep_fwd(lat, small["q_a_norm"], small["kv_a_norm"], ghq, ghk, W["wq"], W["wk"], W["wv"], rc, rs,
                                     tm=TM, name="mla_prep_fwd")
    (o, lse), got = _flash_fwd(q, k, v, n_seq=n_seq, seq=seq, tq=TQ, name="attn_fwd", hosted=ex.gather_chips("ffn2"))
    (x2, merged, ya, yb), got = _merge_fwd(o, p, gl, small["gate_bias"], x1, W["wpa"], W["wpc"], W["wout"], tm=512, name="merge_fwd",
                                           hosted=ex.gather_sibling(got))
    f2g, f2u, f2d = ex.ffn_weights(got)
    (dy, h2, a2, b2, loss_row), _ = _ffn_fwd(x2, small["ffn2_norm"], f2g, f2u, f2d, tm=512, hc=DFF // 2, name="ffn2_fwd", target=tgt)

    gw, gs = {}, {}
    (da2, db2, *ffn2_grads), _ = _ffn_grads(dy, h2, a2, b2, f2d, tm=TM, hc=HC, name="ffn2_grads")
    (dx2, gs["ffn2_norm"]), _ = _ffn_up_bwd(da2, db2, f2g, f2u, x2, small["ffn2_norm"], dy, tm=512, name="ffn2_up_bwd")

    (dx2b, dya, dyb, dgl, do, dp, gs["gate_bias"]), got = _merge_bwd(
        dx2, ya, yb, gl, small["gate_bias"], W["wpa"], W["wpc"], W["wout"], tm=512, name="merge_bwd",
        hosted=ex.scatter_sibling("ffn2", ffn2_grads))
    ex.scatter_sibling_done("ffn2", got)
    gw["wout"] = wgrad(merged, dx2b, "dw_out")[0]
    gw["wpa"] = wgrad(o, dya, "dw_pa")[0]
    gw["wpc"] = wgrad(p, dyb, "dw_pc")[0]
    dconv3, dconv_w = _conv_bwd(dp, conv3, conv_w, n_seq=n_seq, seq=seq, name="conv_bwd")
    (dq, dk, dv), got = _flash_bwd(q, k, v, o, lse, do, n_seq=n_seq, seq=seq, tq=TQ, name="attn_bwd",
                                   hosted=ex.scatter_chips("ffn2"))
    ex.scatter_chips_done("ffn2", got)
    dlat, dqp, dkp, gs["q_a_norm"], gs["kv_a_norm"], dghq, dghk = _mla_prep_bwd(
        dq, dk, dv, lat, qn, ckv, small["q_a_norm"], small["kv_a_norm"], ghq, ghk, W["wq"], W["wk"], W["wv"], rc, rs,
        tm=512, name="mla_prep_bwd")
    gs["q_head_norm"], gs["k_head_norm"] = _head_cols_inv(dghq, QK_DIM), _head_cols_inv(dghk, QK_DIM)
    gw["wq"] = wgrad(qn, dqp, "dw_uq")[0]
    gw["wk"] = wgrad(ckv, dkp, "dw_uk")[0]
    gw["wv"] = wgrad(ckv, dv, "dw_uv")[0]
    gw["convT"] = wgrad(dconv3, hm, "dw_conv", tm=CONV_COLS // 2)[0]
    gw["gateT"] = wgrad(dgl, hm, "dw_gate")[0]
    gw["latT"] = wgrad(dlat, hm, "dw_lat")[0]
    ex.scatter_sibling_now("mix", gw)
    zero = ex.scatter_chips_start("mix_in")
    (dx1, gs["mix_norm"]), _ = _proj_bwd(dlat, dconv3, dgl, W["latT"], W["convT"], W["gateT"], x1, small["mix_norm"] + zero, dx2,
                                         tm=512, name="proj_bwd")

    (da1, db1, *ffn1_grads), got = _ffn_grads(dx1, h1, a1, b1, f1d, tm=TM, hc=HC, name="ffn1_grads",
                                              hosted=ex.scatter_chips("mix_misc"))
    ex.scatter_chips_done("mix_misc", got)
    ex.reduce_small(gs, dconv_w, loss_row)
    ex.scatter_sibling_now("ffn1", ffn1_grads)
    zero = ex.scatter_chips_start("ffn1")
    (dx0, gs["ffn1_norm"]), _ = _ffn_up_bwd(da1, db1, f1g, f1u, x0, small["ffn1_norm"] + zero, dx1, tm=512, name="ffn1_up_bwd")
    return dx0.reshape(n_seq, seq, d), gs["ffn1_norm"]


class _MeshExchange:
    def __init__(self, w, core, chip):
        self.w, self.core, self.chip = w, core, chip
        self.partial, self.received, self._cache, self._scattering = {}, {}, {}, {}

    def _blocks(self, group):
        w = self.w
        if group not in self._cache:
            if group.startswith("ffn"):
                self._cache[group] = [w[group + "_w_gate"].T.astype(BF16), w[group + "_w_up"].T.astype(BF16),
                                      w[group + "_w_down"].astype(BF16)]
            else:
                self._cache["mix_in"], self._cache["mix_misc"] = [[b] for b in _pack_weights(w)]
        return self._cache[group]

    def gather_chips(self, *groups):
        blocks = [b for group in groups for b in self._blocks(group)]
        return _gather_chips_plan(len(blocks)), blocks, _gather_shapes(blocks)

    def gather_sibling(self, got):
        half = list(got)
        return _gather_sibling_plan(len(half)), half, _same_shapes(half)

    def gather_start(self, group):
        blocks = self._blocks(group)
        plan = _gather_chips_plan(len(blocks))
        sems, blocks, lands, token = _plan_start(plan, blocks, _gather_shapes(blocks), name="gather_%s_start" % group)
        self._gathering = (group, plan, sems, blocks, lands)
        return token[0, 0]

    def gather_finish(self, after):
        group, plan, sems, blocks, lands = self._gathering
        _, half = _plan_wait(plan, sems, blocks, lands, after, name="gather_%s_wait" % group)
        return self.ffn_weights(_run_plan(_gather_sibling_plan(len(half)), half, _same_shapes(half), name="gather_%s_sibling" % group))

    def gather_conv_w(self, after):
        shard = self.w["conv_w"] + 0.0 * after[:1, :1].astype(F32)
        cw_all = _small_exchange(jnp.pad(shard, ((0, 5), (0, 0))), reduce=False, name="gather_conv_w")
        return cw_all[:, :3].transpose(1, 0, 2).reshape(3, D), cw_all[0, 7, 0]

    def reduce_small(self, gs, dconv_w, loss_row):
        pieces = [_pad_cols(gs[n], SMALL_SLOTS[n]) for n in SMALL_NAMES[1:]] + [dconv_w.reshape(1, 3 * D), loss_row]
        self.small_total = _small_exchange(jnp.concatenate(pieces, axis=1).reshape(-1, 128), reduce=True,
                                           name="reduce_small").reshape(-1)

    def scatter_chips_start(self, group):
        s1 = self.partial[group]
        plan = _scatter_chips_plan(len(s1))
        sems, s1, lands, token = _plan_start(plan, s1, _scatter_shapes(s1), name="scatter_%s_start" % group)
        self._scattering[group] = (plan, sems, s1, lands)
        return token[0, 0]

    def scatter_chips_finish(self, group, after):
        plan, sems, s1, lands = self._scattering[group]
        self.partial[group], self.received[group] = _plan_wait(plan, sems, s1, lands, after, name="scatter_%s_wait" % group)

    def witness(self):
        parts = [b[:8, :128].astype(F32) for g in ("mix_in", "mix_misc", "ffn2") for b in self._blocks(g)]
        return functools.reduce(jnp.add, parts)

    def ffn_weights(self, got):
        return [a.reshape(DFF, D) for a in got]

    def mix_in_weights(self, got):
        return _unpack_in(got[0])

    def mix_misc_weights(self, got):
        return _unpack_misc(got[0])

    def _parts(self, group, grads):
        if group == "mix":
            return _pack_grads(grads), ["mix_in", "mix_misc"]
        parts = [g.reshape(N_DEV, -1, D) for g in grads]
        return parts, ([group] if len(parts) == 1 else None)

    def scatter_sibling(self, group, grads):
        self._sent, self._names = self._parts(group, grads)
        return _scatter_sibling_plan(len(self._sent)), self._sent, _halved_shapes(self._sent)

    def scatter_sibling_done(self, group, got):
        sums = list(_sum_sibling(self._sent, list(got), self.core, name="sum_%s_sibling" % group))
        if self._names is None:
            self.partial[group] = sums
        else:
            for n, s in zip(self._names, sums):
                self.partial[n] = [s]

    def scatter_sibling_now(self, group, grads):
        plan, parts, shapes = self.scatter_sibling(group, grads)
        self.scatter_sibling_done(group, _run_plan(plan, parts, shapes, name="scatter_%s_sibling" % group))

    def scatter_chips(self, group):
        s1 = self.partial[group]
        return _scatter_chips_plan(len(s1)), s1, _scatter_shapes(s1)

    def scatter_chips_done(self, group, got):
        self.received[group] = list(got)


SMALL_NAMES = ("ffn1_norm", "mix_norm", "gate_bias", "q_a_norm", "kv_a_norm", "q_head_norm", "k_head_norm", "ffn2_norm")
SMALL_SLOTS = {"ffn1_norm": 1024, "mix_norm": 1024, "gate_bias": 2048, "q_a_norm": 384, "kv_a_norm": 256, "q_head_norm": 128,
               "k_head_norm": 128, "ffn2_norm": 1024, "conv_w": 3072, "loss": 128}
COLUMN_MAJOR = ("w_in", "w_uq", "w_uk", "w_uv")
WEIGHT_NAMES = ("ffn1_norm", "ffn1_w_gate", "ffn1_w_up", "ffn1_w_down", "mix_norm", "w_in", "gate_bias", "q_a_norm", "w_uq",
                "kv_a_norm", "w_uk", "w_uv", "q_head_norm", "k_head_norm", "w_proj_attn", "conv_w", "w_proj_conv", "w_out",
                "ffn2_norm", "ffn2_w_gate", "ffn2_w_up", "ffn2_w_down")


def _step(x, positions, loss_target, w, m, v):
    xi, yi, ci = _place()
    core = ci.astype(jnp.int32).reshape(1)
    chip = (2 * xi + yi).astype(jnp.int32).reshape(1)
    me = 4 * xi + 2 * yi + ci

    ex = _MeshExchange(w, core, chip)
    zero = ex.gather_start("ffn1")
    ex.w = {n: (a if n.startswith("ffn1") else a + zero) for n, a in w.items()}
    small = {n: w[n].reshape(1, -1) for n in SMALL_NAMES}

    grad_x, dffn1_norm = _local_step(x, positions + zero.astype(jnp.int32), loss_target, small, ex)

    grads, deltas, new_m, new_v = {}, {}, {}, {}

    def ffn_update(group):
        names = (group + "_w_gate", group + "_w_up", group + "_w_down")
        views = [[a[n] if n.endswith("down") else a[n].T for n in names] for a in (w, m, v)]
        res = _sum_adamw(ex.partial[group], ex.received[group], chip, *views, name="adamw_" + group)
        for n, four in zip(names, res):
            grads[n], deltas[n], new_m[n], new_v[n] = (r if n.endswith("down") else r.T for r in four)

    def update(n):
        shape = w[n].shape
        if n in COLUMN_MAJOR:
            ops = [a.T for a in (w[n], grads[n], m[n], v[n])]
            deltas[n], new_m[n], new_v[n] = (r.T for r in _adamw(*ops, name="adamw_" + n))
            return
        view = shape if len(shape) == 2 else ((-1, 128) if shape[0] % 128 == 0 else (1, shape[0]))
        dlt, nm, nv = _adamw(w[n].reshape(view), grads[n].reshape(view), m[n].reshape(view), v[n].reshape(view), name="adamw_" + n)
        deltas[n], new_m[n], new_v[n] = dlt.reshape(shape), nm.reshape(shape), nv.reshape(shape)

    ffn_update("ffn2")
    ex.scatter_chips_finish("mix_in", dffn1_norm)
    grads.update(_unpack_grads([_sum_chips(ex.partial[g][0], ex.received[g][0], chip, name="sum_%s_chips" % g)
                                for g in ("mix_in", "mix_misc")]))
    total, off = ex.small_total, 0
    for n in SMALL_NAMES[1:]:
        grads[n] = total[off:off + w[n].shape[0]]
        off += SMALL_SLOTS[n]
    conv_full = total[off:off + 3 * D].reshape(3, D)
    grads["conv_w"] = lax.dynamic_slice(conv_full, (0, me * HEAD_PAD), (3, HEAD_PAD))
    loss = total[off + 3 * D]
    tiny = SMALL_NAMES[1:] + ("conv_w", "w_uq", "w_uk", "w_uv", "w_proj_attn", "w_proj_conv", "w_out")

    def view(n, a):
        if a.ndim == 2:
            return a.T if n in COLUMN_MAJOR else a
        return a.reshape((-1, 128) if a.size % 128 == 0 else (1, a.size))

    def unview(n, a):
        return (a.T if n in COLUMN_MAJOR else a) if w[n].ndim == 2 else a.reshape(w[n].shape)

    res = _adamw_small(*[[view(n, a[n]) for n in tiny] for a in (w, grads, m, v)], name="adamw_small")
    for out, arrs in zip((deltas, new_m, new_v), res):
        out.update({n: unview(n, a) for n, a in zip(tiny, arrs)})
    later = ("ffn1_norm", "ffn1_w_gate", "ffn1_w_up", "ffn1_w_down")
    for n in WEIGHT_NAMES:
        if n not in deltas and n not in later:
            update(n)

    done = [deltas[n][:8, :128] for n in ("ffn2_w_down", "w_in", "w_out", "w_proj_attn")] + [deltas["mix_norm"].reshape(8, 128)]
    ex.scatter_chips_finish("ffn1", functools.reduce(jnp.add, done) + grad_x.reshape(-1, D)[:8, :128])
    ffn_update("ffn1")
    last = dffn1_norm + 0.0 * grads["ffn1_w_down"][:1, :1]
    grads["ffn1_norm"] = _small_exchange(last.reshape(-1, 128), reduce=True, name="reduce_ffn1_norm").reshape(-1)
    update("ffn1_norm")
    return (loss, grad_x, *[grads[n] for n in WEIGHT_NAMES], *[deltas[n] for n in WEIGHT_NAMES],
            *[new_m[n] for n in WEIGHT_NAMES], *[new_v[n] for n in WEIGHT_NAMES])


def kernel(x, positions, ffn1_norm, ffn1_w_gate, ffn1_w_up, ffn1_w_down, mix_norm, w_in, gate_bias, q_a_norm, w_uq, kv_a_norm, w_uk, w_uv, q_head_norm, k_head_norm, w_proj_attn, conv_w, w_proj_conv, w_out, ffn2_norm, ffn2_w_gate, ffn2_w_up, ffn2_w_down, loss_target, m_ffn1_norm, m_ffn1_w_gate, m_ffn1_w_up, m_ffn1_w_down, m_mix_norm, m_w_in, m_gate_bias, m_q_a_norm, m_w_uq, m_kv_a_norm, m_w_uk, m_w_uv, m_q_head_norm, m_k_head_norm, m_w_proj_attn, m_conv_w, m_w_proj_conv, m_w_out, m_ffn2_norm, m_ffn2_w_gate, m_ffn2_w_up, m_ffn2_w_down, v_ffn1_norm, v_ffn1_w_gate, v_ffn1_w_up, v_ffn1_w_down, v_mix_norm, v_w_in, v_gate_bias, v_q_a_norm, v_w_uq, v_kv_a_norm, v_w_uk, v_w_uv, v_q_head_norm, v_k_head_norm, v_w_proj_attn, v_conv_w, v_w_proj_conv, v_w_out, v_ffn2_norm, v_ffn2_w_gate, v_ffn2_w_up, v_ffn2_w_down):
    given = dict(locals())
    w = {n: given[n] for n in WEIGHT_NAMES}
    m = {n: given["m_" + n] for n in WEIGHT_NAMES}
    v = {n: given["v_" + n] for n in WEIGHT_NAMES}
    return _step(x, positions, loss_target, w, m, v)
```
